```python
import jax, jax.numpy as jnp
from jax import lax
import numpy as np

D_MODEL = 1024
BATCH = 8
SEQ = 4096
DEPTH = 1

ATTN_WIDTH = D_MODEL // 2
HGRN_WIDTH = D_MODEL - ATTN_WIDTH
MIX_WIDTH = ATTN_WIDTH + HGRN_WIDTH
ATTN_HEAD_DIM = 64
ATTN_HEADS = ATTN_WIDTH // ATTN_HEAD_DIM
HGRN_EXPAND = 128
HGRN_HEADS = HGRN_WIDTH // HGRN_EXPAND
DILATED_PATTERNS = ((128, 1), (512, 4), (2048, 16))
ATTN_BLOCK = 128
ROPE_THETA = 500000.0
ROPE_DIMS = ATTN_HEAD_DIM // 4
HGRN_CHUNK = 64
NORM_EPS = 1e-6
IN_COLS = 4 * ATTN_WIDTH + 4 * HGRN_WIDTH

kernel_name = "hymba_dilated_attn_hgrn2_hybrid"


def _rmsnorm(x, w):
    xf = x.astype(jnp.float32)
    y = xf * lax.rsqrt(jnp.mean(xf * xf, axis=-1, keepdims=True) + NORM_EPS)
    return (y * w.astype(jnp.float32)).astype(x.dtype)


def _head_rmsnorm(o, w, n_heads):
    b, s, width = o.shape
    oh = o.reshape(b, s, n_heads, width // n_heads)
    oh = oh * lax.rsqrt(jnp.mean(oh * oh, axis=-1, keepdims=True) + NORM_EPS)
    return oh.reshape(b, s, width) * w.astype(jnp.float32)


def _partial_rotary(t, positions):
    half = ROPE_DIMS // 2
    inv_freq = ROPE_THETA ** (-jnp.arange(half, dtype=jnp.float32) * (2.0 / ROPE_DIMS))
    ang = positions.astype(jnp.float32)[..., None] * inv_freq
    cos = jnp.cos(ang)[:, :, None, :]
    sin = jnp.sin(ang)[:, :, None, :]
    t = t.astype(jnp.float32)
    t1, t2, rest = t[..., :half], t[..., half:ROPE_DIMS], t[..., ROPE_DIMS:]
    return jnp.concatenate([t1 * cos - t2 * sin, t2 * cos + t1 * sin, rest], axis=-1)


def _dilated_window_attn(q, k, v, window, dilation):
    b, h, s, e = q.shape
    span = window // dilation
    sub_len = s // dilation
    n_blk = -(-sub_len // ATTN_BLOCK)
    pad = n_blk * ATTN_BLOCK - sub_len

    def to_blocks(t):
        t = t.reshape(b, h, sub_len, dilation, e).transpose(0, 1, 3, 2, 4)
        t = jnp.pad(t, ((0, 0), (0, 0), (0, 0), (0, pad), (0, 0)))
        return t.reshape(b, h, dilation, n_blk, ATTN_BLOCK, e)

    def with_prev(t):
        prev = jnp.concatenate([jnp.zeros_like(t[:, :, :, :1]), t[:, :, :, :-1]], axis=3)
        return jnp.concatenate([prev, t], axis=4)

    qb, kb, vb = to_blocks(q), to_blocks(k), to_blocks(v)
    kw, vw = with_prev(kb), with_prev(vb)
    scores = jnp.einsum('bhrnqe,bhrnke->bhrnqk', qb, kw) * (e ** -0.5)
    qi = jnp.arange(ATTN_BLOCK)[:, None]
    kj = jnp.arange(2 * ATTN_BLOCK)[None, :]
    dist = ATTN_BLOCK + qi - kj
    band = (dist >= 0) & (dist <= span)
    first = (jnp.arange(n_blk) == 0)[:, None, None]
    mask = band[None] & ~(first & (kj < ATTN_BLOCK)[None])
    scores = jnp.where(mask, scores, -jnp.inf)
    m = jnp.max(scores, axis=-1)
    p = jnp.exp(scores - m[..., None])
    l = jnp.sum(p, axis=-1)
    o = jnp.einsum('bhrnqk,bhrnke->bhrnqe', p, vw)

    def from_blocks(t):
        tail = t.shape[5:]
        t = t.reshape(b, h, dilation, n_blk * ATTN_BLOCK, *tail)[:, :, :, :sub_len]
        t = jnp.moveaxis(t, 2, 3)
        return t.reshape(b, h, s, *tail)

    return from_blocks(o), from_blocks(m), from_blocks(l)


def _longnet_mixture(q, k, v):
    outs = [_dilated_window_attn(q, k, v, w, d) for (w, d) in DILATED_PATTERNS]
    m_all = jnp.stack([o[1] for o in outs], axis=0)
    m_top = jnp.max(m_all, axis=0)
    wts = jnp.exp(m_all - m_top)
    num = sum(wts[i][..., None] * outs[i][0] for i in range(len(outs)))
    den = sum(wts[i] * outs[i][2] for i in range(len(outs)))
    return num / den[..., None]


def _hgrn2_chunked(q, k, v, log_f):
    b, h, s, e = q.shape
    ev = v.shape[-1]
    nc = s // HGRN_CHUNK
    rs = lambda t: t.reshape(b, h, nc, HGRN_CHUNK, t.shape[-1])
    q, k, v, log_f = rs(q), rs(k), rs(v), rs(log_f)
    cum = jnp.cumsum(log_f, axis=3)
    last = cum[:, :, :, -1:]
    q_dec = q * jnp.exp(cum)
    k_inv = k * jnp.exp(-cum)
    k_end = k * jnp.exp(last - cum)
    causal = jnp.tril(jnp.ones((HGRN_CHUNK, HGRN_CHUNK), dtype=bool))
    att = jnp.where(causal, jnp.einsum('bhnte,bhnse->bhnts', q_dec, k_inv), 0.0)
    o_intra = jnp.einsum('bhnts,bhnsv->bhntv', att, v)
    chunk_decay = jnp.exp(last[:, :, :, 0])

    def step(state, xs):
        qd, ke, vc, dec = xs
        o = jnp.einsum('bhte,bhev->bhtv', qd, state)
        state = dec[..., None] * state + jnp.einsum('bhte,bhtv->bhev', ke, vc)
        return state, o

    xs = (jnp.moveaxis(q_dec, 2, 0), jnp.moveaxis(k_end, 2, 0),
          jnp.moveaxis(v, 2, 0), jnp.moveaxis(chunk_decay, 2, 0))
    state0 = jnp.zeros((b, h, e, ev), jnp.float32)
    _, o_inter = lax.scan(step, state0, xs)
    o = o_intra + jnp.moveaxis(o_inter, 0, 2)
    return o.reshape(b, h, s, ev)


def _fwd_setup_inputs(seed: int = 0) -> dict:
    key = jax.random.key(seed)
    ks = jax.random.split(key, 10)
    x = jax.random.normal(ks[0], (BATCH, SEQ, D_MODEL), jnp.float32)
    offset = jax.random.randint(ks[1], (BATCH, 1), 0, 4096, dtype=jnp.int32)
    positions = (offset + jnp.arange(SEQ, dtype=jnp.int32)[None, :]).astype(jnp.int32)
    w_in = jax.random.normal(ks[2], (DEPTH, D_MODEL, IN_COLS), jnp.float32) * D_MODEL ** -0.5
    w_out = jax.random.normal(ks[3], (DEPTH, MIX_WIDTH, D_MODEL), jnp.float32) * MIX_WIDTH ** -0.5
    mix_norm_w = 1.0 + 0.02 * jax.random.normal(ks[4], (DEPTH, D_MODEL), jnp.float32)
    attn_out_norm_w = 1.0 + 0.02 * jax.random.normal(ks[5], (DEPTH, ATTN_WIDTH), jnp.float32)
    hgrn_out_norm_w = 1.0 + 0.02 * jax.random.normal(ks[6], (DEPTH, HGRN_WIDTH), jnp.float32)
    hgrn_lb_raw = 0.1 * jax.random.normal(ks[7], (DEPTH + 1, HGRN_WIDTH), jnp.float32)
    final_norm_w = 1.0 + 0.02 * jax.random.normal(ks[8], (D_MODEL,), jnp.float32)
    return {"x": x, "positions": positions, "w_in": w_in, "w_out": w_out,
            "mix_norm_w": mix_norm_w, "attn_out_norm_w": attn_out_norm_w,
            "hgrn_out_norm_w": hgrn_out_norm_w, "hgrn_lb_raw": hgrn_lb_raw,
            "final_norm_w": final_norm_w}


def _fwd_reference(x, positions, w_in, w_out, mix_norm_w, attn_out_norm_w,
              hgrn_out_norm_w, hgrn_lb_raw, final_norm_w):
    b, s, _ = x.shape
    f32 = jnp.float32
    lower_bounds = jnp.cumsum(jax.nn.softmax(hgrn_lb_raw.astype(f32), axis=0), axis=0)
    split_at = [ATTN_WIDTH * i for i in range(1, 5)] + \
               [4 * ATTN_WIDTH + HGRN_WIDTH * i for i in range(1, 4)]
    for layer in range(DEPTH):
        hn = _rmsnorm(x, mix_norm_w[layer])
        proj = hn @ w_in[layer]
        aq, ak, av, ag, hq, hf, hi, hg = jnp.split(proj, split_at, axis=-1)

        aq = _partial_rotary(aq.reshape(b, s, ATTN_HEADS, ATTN_HEAD_DIM), positions)
        ak = _partial_rotary(ak.reshape(b, s, ATTN_HEADS, ATTN_HEAD_DIM), positions)
        av = av.reshape(b, s, ATTN_HEADS, ATTN_HEAD_DIM).astype(f32)
        bhse = lambda t: t.transpose(0, 2, 1, 3)
        attn = _longnet_mixture(bhse(aq), bhse(ak), bhse(av))
        attn = attn.transpose(0, 2, 1, 3).reshape(b, s, ATTN_WIDTH)

        lb = lower_bounds[layer]
        f = lb + (1.0 - lb) * jax.nn.sigmoid(hf.astype(f32))
        hkey = 1.0 - f
        hquery = jax.nn.silu(hq.astype(f32))
        hh = lambda t: t.reshape(b, s, HGRN_HEADS, HGRN_EXPAND).transpose(0, 2, 1, 3)
        rec = _hgrn2_chunked(hh(hquery), hh(hkey), hh(hi.astype(f32)), hh(jnp.log(f)))
        rec = rec.transpose(0, 2, 1, 3).reshape(b, s, HGRN_WIDTH)

        y_attn = _head_rmsnorm(attn, attn_out_norm_w[layer], ATTN_HEADS) * jax.nn.silu(ag.astype(f32))
        y_hgrn = _head_rmsnorm(rec, hgrn_out_norm_w[layer], HGRN_HEADS) * jax.nn.silu(hg.astype(f32))
        mixed = jnp.concatenate([y_attn, y_hgrn], axis=-1).astype(x.dtype)
        x = x + mixed @ w_out[layer]
    return _rmsnorm(x, final_norm_w)


import jax as _jax
import jax.numpy as _jnp

TWIN_FORMAT = 'train_step'
FWD_PARAMS = ['x', 'positions', 'w_in', 'w_out', 'mix_norm_w', 'attn_out_norm_w', 'hgrn_out_norm_w', 'hgrn_lb_raw', 'final_norm_w']
TWIN_WEIGHTS = ['w_in', 'w_out', 'mix_norm_w', 'attn_out_norm_w', 'hgrn_out_norm_w', 'hgrn_lb_raw', 'final_norm_w']
TWIN_DIFF_INPUT = 'x'
TWIN_INPUTS = ['x', 'positions', 'w_in', 'w_out', 'mix_norm_w', 'attn_out_norm_w', 'hgrn_out_norm_w', 'hgrn_lb_raw', 'final_norm_w', 'loss_target', 'm_w_in', 'm_w_out', 'm_mix_norm_w', 'm_attn_out_norm_w', 'm_hgrn_out_norm_w', 'm_hgrn_lb_raw', 'm_final_norm_w', 'v_w_in', 'v_w_out', 'v_mix_norm_w', 'v_attn_out_norm_w', 'v_hgrn_out_norm_w', 'v_hgrn_lb_raw', 'v_final_norm_w']
TWIN_OUTPUTS = ['loss', 'grad_x', 'grad_w_in', 'grad_w_out', 'grad_mix_norm_w', 'grad_attn_out_norm_w', 'grad_hgrn_out_norm_w', 'grad_hgrn_lb_raw', 'grad_final_norm_w', 'delta_w_in', 'delta_w_out', 'delta_mix_norm_w', 'delta_attn_out_norm_w', 'delta_hgrn_out_norm_w', 'delta_hgrn_lb_raw', 'delta_final_norm_w', 'new_m_w_in', 'new_m_w_out', 'new_m_mix_norm_w', 'new_m_attn_out_norm_w', 'new_m_hgrn_out_norm_w', 'new_m_hgrn_lb_raw', 'new_m_final_norm_w', 'new_v_w_in', 'new_v_w_out', 'new_v_mix_norm_w', 'new_v_attn_out_norm_w', 'new_v_hgrn_out_norm_w', 'new_v_hgrn_lb_raw', 'new_v_final_norm_w']
TWIN_LEAF_KINDS = {'loss': 'loss', 'grad_x': 'grad_x', 'grad_w_in': 'grad_w', 'grad_w_out': 'grad_w', 'grad_mix_norm_w': 'grad_w', 'grad_attn_out_norm_w': 'grad_w', 'grad_hgrn_out_norm_w': 'grad_w', 'grad_hgrn_lb_raw': 'grad_w', 'grad_final_norm_w': 'grad_w', 'delta_w_in': 'delta_w', 'delta_w_out': 'delta_w', 'delta_mix_norm_w': 'delta_w', 'delta_attn_out_norm_w': 'delta_w', 'delta_hgrn_out_norm_w': 'delta_w', 'delta_hgrn_lb_raw': 'delta_w', 'delta_final_norm_w': 'delta_w', 'new_m_w_in': 'new_m', 'new_m_w_out': 'new_m', 'new_m_mix_norm_w': 'new_m', 'new_m_attn_out_norm_w': 'new_m', 'new_m_hgrn_out_norm_w': 'new_m', 'new_m_hgrn_lb_raw': 'new_m', 'new_m_final_norm_w': 'new_m', 'new_v_w_in': 'new_v', 'new_v_w_out': 'new_v', 'new_v_mix_norm_w': 'new_v', 'new_v_attn_out_norm_w': 'new_v', 'new_v_hgrn_out_norm_w': 'new_v', 'new_v_hgrn_lb_raw': 'new_v', 'new_v_final_norm_w': 'new_v'}


def _forward(args):
    return _fwd_reference(*[args[k] for k in FWD_PARAMS])


def _output_shape():
    out = _jax.eval_shape(lambda: _forward(_fwd_setup_inputs(0)))
    return out.shape, out.dtype

N_MICROBATCH = 1
ADAM_LR = 0.001
ADAM_B1 = 0.9
ADAM_B2 = 0.999
ADAM_EPS = 1e-08
ADAM_WD = 0.01
ADAM_STEP = 10
PER_EXAMPLE_BATCH_AXIS = {'x': 0, 'positions': 0, 'loss_target': 0}
SHARED_INPUTS = []
_WEIGHT_DTYPES = {'w_in': _jnp.float32, 'w_out': _jnp.float32, 'mix_norm_w': _jnp.float32, 'attn_out_norm_w': _jnp.float32, 'hgrn_out_norm_w': _jnp.float32, 'hgrn_lb_raw': _jnp.float32, 'final_norm_w': _jnp.float32}
MOMENT_SCALE = {'w_in': 7.981602e-02, 'w_out': 9.041171e-02, 'mix_norm_w': 1.745442e-01, 'attn_out_norm_w': 9.327626e-02, 'hgrn_out_norm_w': 8.868516e-02, 'hgrn_lb_raw': 8.722494e-03, 'final_norm_w': 3.201403e+01}


def _to_microbatches(a, axis):
    t = _jnp.moveaxis(a, axis, 0)
    t = t.reshape((N_MICROBATCH, t.shape[0] // N_MICROBATCH) + t.shape[1:])
    return _jnp.moveaxis(t, 1, axis + 1)


def setup_inputs(seed: int = 0) -> dict:
    inp = _fwd_setup_inputs(seed)
    key = _jax.random.fold_in(_jax.random.key(seed), 7919)
    shape, _ = _output_shape()
    out = dict(inp)
    out["loss_target"] = _jax.random.normal(_jax.random.fold_in(key, 0), shape, _jnp.float32)
    for i, name in enumerate(TWIN_WEIGHTS):
        w = inp[name].astype(_jnp.float32)
        if MOMENT_SCALE is None:
            s = _jnp.sqrt(_jnp.mean(_jnp.square(w)) + 1e-30)
        else:
            s = MOMENT_SCALE[name]
        km, kv = _jax.random.split(_jax.random.fold_in(key, i + 1))
        out[name] = w
        out["m_" + name] = s * _jax.random.normal(km, w.shape, _jnp.float32)
        out["v_" + name] = (s * s) * _jax.random.uniform(kv, w.shape, _jnp.float32, 0.5, 1.5)
    if N_MICROBATCH > 1:
        for name, axis in PER_EXAMPLE_BATCH_AXIS.items():
            out[name] = _to_microbatches(out[name], axis)
    return {'x': out['x'], 'positions': out['positions'], 'w_in': out['w_in'], 'w_out': out['w_out'], 'mix_norm_w': out['mix_norm_w'], 'attn_out_norm_w': out['attn_out_norm_w'], 'hgrn_out_norm_w': out['hgrn_out_norm_w'], 'hgrn_lb_raw': out['hgrn_lb_raw'], 'final_norm_w': out['final_norm_w'], 'loss_target': out['loss_target'], 'm_w_in': out['m_w_in'], 'm_w_out': out['m_w_out'], 'm_mix_norm_w': out['m_mix_norm_w'], 'm_attn_out_norm_w': out['m_attn_out_norm_w'], 'm_hgrn_out_norm_w': out['m_hgrn_out_norm_w'], 'm_hgrn_lb_raw': out['m_hgrn_lb_raw'], 'm_final_norm_w': out['m_final_norm_w'], 'v_w_in': out['v_w_in'], 'v_w_out': out['v_w_out'], 'v_mix_norm_w': out['v_mix_norm_w'], 'v_attn_out_norm_w': out['v_attn_out_norm_w'], 'v_hgrn_out_norm_w': out['v_hgrn_out_norm_w'], 'v_hgrn_lb_raw': out['v_hgrn_lb_raw'], 'v_final_norm_w': out['v_final_norm_w']}


def _loss(weights, diff, rest, loss_target):
    with _jax.named_scope("forward"):
        args = {**rest, TWIN_DIFF_INPUT: diff, **{k: w.astype(_WEIGHT_DTYPES[k]) for k, w in weights.items()}}
        y = _forward(args)
    with _jax.named_scope("loss_head"):
        err = _jnp.square(y.astype(_jnp.float32) - loss_target)
        return 0.5 * _jnp.sum(_jnp.mean(err, axis=-1)) if err.ndim else 0.5 * err


def _adamw(w, g, m, v):
    m = ADAM_B1 * m + (1.0 - ADAM_B1) * g
    v = ADAM_B2 * v + (1.0 - ADAM_B2) * _jnp.square(g)
    m_hat = m / (1.0 - ADAM_B1 ** ADAM_STEP)
    v_hat = v / (1.0 - ADAM_B2 ** ADAM_STEP)
    delta = -ADAM_LR * (m_hat / (_jnp.sqrt(v_hat) + ADAM_EPS) + ADAM_WD * w)
    return delta, m, v


def reference(x, positions, w_in, w_out, mix_norm_w, attn_out_norm_w, hgrn_out_norm_w, hgrn_lb_raw, final_norm_w, loss_target, m_w_in, m_w_out, m_mix_norm_w, m_attn_out_norm_w, m_hgrn_out_norm_w, m_hgrn_lb_raw, m_final_norm_w, v_w_in, v_w_out, v_mix_norm_w, v_attn_out_norm_w, v_hgrn_out_norm_w, v_hgrn_lb_raw, v_final_norm_w):
    given = dict(x=x, positions=positions, w_in=w_in, w_out=w_out, mix_norm_w=mix_norm_w, attn_out_norm_w=attn_out_norm_w, hgrn_out_norm_w=hgrn_out_norm_w, hgrn_lb_raw=hgrn_lb_raw, final_norm_w=final_norm_w, loss_target=loss_target, m_w_in=m_w_in, m_w_out=m_w_out, m_mix_norm_w=m_mix_norm_w, m_attn_out_norm_w=m_attn_out_norm_w, m_hgrn_out_norm_w=m_hgrn_out_norm_w, m_hgrn_lb_raw=m_hgrn_lb_raw, m_final_norm_w=m_final_norm_w, v_w_in=v_w_in, v_w_out=v_w_out, v_mix_norm_w=v_mix_norm_w, v_attn_out_norm_w=v_attn_out_norm_w, v_hgrn_out_norm_w=v_hgrn_out_norm_w, v_hgrn_lb_raw=v_hgrn_lb_raw, v_final_norm_w=v_final_norm_w)
    weights = {n: given[n] for n in TWIN_WEIGHTS}
    shared = {n: given[n] for n in SHARED_INPUTS}
    per_example = {n: given[n] for n in ['x', 'positions']}
    grad_fn = _jax.value_and_grad(_loss, argnums=(0, 1))

    def one_microbatch(ex, loss_target):
        ex = dict(ex)
        diff = ex.pop(TWIN_DIFF_INPUT)
        return grad_fn(weights, diff, {**shared, **ex}, loss_target)

    if N_MICROBATCH == 1:
        loss, (grad_w, grad_x) = one_microbatch(per_example, given["loss_target"])
    else:
        def body(carry, xs):
            loss_sum, grad_sum = carry
            l_k, (gw_k, gx_k) = one_microbatch(xs[0], xs[1])
            with _jax.named_scope("update"):
                return (loss_sum + l_k, _jax.tree.map(_jnp.add, grad_sum, gw_k)), gx_k

        init = (_jnp.zeros((), _jnp.float32), _jax.tree.map(_jnp.zeros_like, weights))
        (loss, grad_w), grad_x = _jax.lax.scan(body, init, (per_example, given["loss_target"]))
    with _jax.named_scope("update"):
        delta_w, new_m, new_v = {}, {}, {}
        for n in TWIN_WEIGHTS:
            delta_w[n], new_m[n], new_v[n] = _adamw(weights[n], grad_w[n], given["m_" + n], given["v_" + n])
    return (loss, grad_x, *[grad_w[n] for n in TWIN_WEIGHTS], *[delta_w[n] for n in TWIN_WEIGHTS],
            *[new_m[n] for n in TWIN_WEIGHTS], *[new_v[n] for n in TWIN_WEIGHTS])
```

```python
import functools

import numpy as np
import jax
import jax.numpy as jnp
from jax import lax
from jax.experimental import pallas as pl
from jax.experimental.pallas import tpu as pltpu

F32 = jnp.float32
BF16 = jnp.bfloat16

T = 4096
D = 1024
AW = 512
HW = 512
NCOL = 4096
HEAD = 64
BLK = 128
CHUNK = 64
EPS = 1e-6
SCALE = HEAD ** -0.5
NEG = -1e30
ROPE_THETA = 500000.0
INV_FREQ = [float(v) for v in
            (np.float32(ROPE_THETA) ** (-(np.arange(8, dtype=np.float32)) * np.float32(0.125)))]
LR, B1, B2, AEPS, WD, STEP = 0.001, 0.9, 0.999, 1e-08, 0.01, 10
VMEM_LIMIT = 56 * 1024 * 1024
MESH = pl.DeviceIdType.MESH


def _cp(sem=None, **kw):
    return pltpu.CompilerParams(dimension_semantics=sem, vmem_limit_bytes=VMEM_LIMIT, **kw)


def _mm(a, b):
    return jnp.dot(a, b, preferred_element_type=F32)


def _mm_nt(a, b):
    return lax.dot_general(a, b, (((1,), (1,)), ((), ())), preferred_element_type=F32)


def _mm_tn(a, b):
    return lax.dot_general(a, b, (((0,), (0,)), ((), ())), preferred_element_type=F32)


def _split3(x):
    h = x.astype(BF16)
    r = x - h.astype(F32)
    m = r.astype(BF16)
    l = (r - m.astype(F32)).astype(BF16)
    return h, m, l


def _mm_exact_l(mat_bf, x):
    h, m, l = _split3(x)
    return _mm(mat_bf, h) + _mm(mat_bf, m) + _mm(mat_bf, l)


def _mm_exact_r(x, mat_bf):
    h, m, l = _split3(x)
    return _mm(h, mat_bf) + _mm(m, mat_bf) + _mm(l, mat_bf)


def _sigmoid(x):
    return 1.0 / (1.0 + jnp.exp(-x))


def _rope_tables(pos):
    lane = lax.broadcasted_iota(jnp.int32, (1, 128), 1)
    jl = lane & 63
    fi = jl & 7
    inv = jnp.zeros((1, 128), F32)
    for kk in range(8):
        inv = jnp.where(fi == kk, INV_FREQ[kk], inv)
    ang = pos.astype(F32) * inv
    c = jnp.cos(ang)
    s = jnp.sin(ang)
    cosf = jnp.where(jl < 16, c, 1.0)
    s1 = jnp.where(jl < 8, -s, 0.0)
    s2 = jnp.where((jl >= 8) & (jl < 16), s, 0.0)
    return cosf, s1, s2


def _rope(t, cosf, s1, s2):
    parts = []
    for ci in range(t.shape[1] // 128):
        tc = t[:, ci * 128:(ci + 1) * 128]
        parts.append(tc * cosf + pltpu.roll(tc, 120, 1) * s1 + pltpu.roll(tc, 8, 1) * s2)
    return jnp.concatenate(parts, axis=1)


def _rope_bwd(g, cosf, s1, s2):
    parts = []
    for ci in range(g.shape[1] // 128):
        gc = g[:, ci * 128:(ci + 1) * 128]
        parts.append(gc * cosf + pltpu.roll(gc * s1, 8, 1) + pltpu.roll(gc * s2, 120, 1))
    return jnp.concatenate(parts, axis=1)


def _perm_store(val, scr, o1, o4, o16, dt):
    n = val.shape[0]
    o1[...] = val.astype(dt)
    for ci in range(4):
        cs = slice(ci * 128, (ci + 1) * 128)
        scr[ci] = val[:, cs]
        for rr in range(4):
            o4[rr, :, cs] = scr[ci, pl.ds(rr, n // 4, stride=4), :].astype(dt)
        for rr in range(16):
            o16[rr, :, cs] = scr[ci, pl.ds(rr, n // 16, stride=16), :].astype(dt)


def _unperm_load(r4, r16, scr_a, scr_b):
    n = scr_a.shape[1]
    for ci in range(4):
        cs = slice(ci * 128, (ci + 1) * 128)
        for rr in range(4):
            scr_a[ci, pl.ds(rr, n // 4, stride=4), :] = r4[rr, :, cs]
        for rr in range(16):
            scr_b[ci, pl.ds(rr, n // 16, stride=16), :] = r16[rr, :, cs]
    return (jnp.concatenate([scr_a[ci] for ci in range(4)], axis=1),
            jnp.concatenate([scr_b[ci] for ci in range(4)], axis=1))


def _weight_gather(w_in, w_out):
    def body(win_ref, wout_ref, fin_ref, fout_ref, bin_ref, bout_ref, send_sems, recv_sems, loc_sems):
        x, y, c = lax.axis_index("x"), lax.axis_index("y"), lax.axis_index("c")
        sibling = (x, y, 1 - c)
        chips = [(1 - x, y), (x, 1 - y), (1 - x, 1 - y)]
        jm = 2 * x + y
        bin_ref[...] = win_ref[...].astype(BF16)
        bout_ref[...] = wout_ref[...].astype(BF16)

        def in_rows(px, py, half):
            return fin_ref.at[pl.ds(half * 512, 512), pl.ds((2 * px + py) * 1024, 1024)]

        def out_rows(px, py, half):
            return fout_ref.at[pl.ds((2 * px + py) * 256 + half * 128, 128), :]

        def rcopy(k, src, dst, to):
            return pltpu.make_async_remote_copy(src_ref=src, dst_ref=dst, send_sem=send_sems.at[k],
                                                recv_sem=recv_sems.at[k], device_id=to, device_id_type=MESH)

        loc_in = pltpu.make_async_copy(bin_ref, fin_ref.at[:, pl.ds(jm * 1024, 1024)], loc_sems.at[0])
        loc_out = pltpu.make_async_copy(bout_ref, fout_ref.at[pl.ds(jm * 256, 256), :], loc_sems.at[1])
        loc_in.start()
        loc_out.start()
        first = []
        for k, chip in enumerate(chips):
            first.append(rcopy(k, bin_ref.at[pl.ds(c * 512, 512), :], in_rows(x, y, c), (*chip, c)))
            first.append(rcopy(3 + k, bout_ref.at[pl.ds(c * 128, 128), :], out_rows(x, y, c), (*chip, c)))
        for cp in first:
            cp.start()
        passed = []
        for k, chip in enumerate(chips):
            rcopy(k, in_rows(*chip, c), in_rows(*chip, c), (x, y, c)).wait_recv()
            p1 = rcopy(6 + k, in_rows(*chip, c), in_rows(*chip, c), sibling)
            p1.start()
            rcopy(3 + k, out_rows(*chip, c), out_rows(*chip, c), (x, y, c)).wait_recv()
            p2 = rcopy(9 + k, out_rows(*chip, c), out_rows(*chip, c), sibling)
            p2.start()
            passed += [p1, p2]
        for k, chip in enumerate(chips):
            rcopy(6 + k, in_rows(*chip, 1 - c), in_rows(*chip, 1 - c), (x, y, c)).wait_recv()
            rcopy(9 + k, out_rows(*chip, 1 - c), out_rows(*chip, 1 - c), (x, y, c)).wait_recv()
        for cp in first + passed:
            cp.wait_send()
        loc_in.wait()
        loc_out.wait()

    return pl.pallas_call(
        body, name="weight_gather",
        out_shape=(jax.ShapeDtypeStruct((D, NCOL), BF16), jax.ShapeDtypeStruct((D, D), BF16)),
        in_specs=[pl.BlockSpec(memory_space=pltpu.VMEM), pl.BlockSpec(memory_space=pltpu.VMEM)],
        out_specs=(pl.BlockSpec(memory_space=pltpu.HBM), pl.BlockSpec(memory_space=pltpu.HBM)),
        scratch_shapes=[pltpu.VMEM((D, 1024), BF16), pltpu.VMEM((256, D), BF16),
                        pltpu.SemaphoreType.DMA((12,)), pltpu.SemaphoreType.DMA((12,)),
                        pltpu.SemaphoreType.DMA((2,))],
        compiler_params=_cp(),
    )(w_in, w_out)


def _fwd_in(x, pos, mixw, w_full):
    TT = 512

    def body(x_ref, pos_ref, mw_ref, w_ref, hn_ref, q1, k1, v1, q4, k4, v4, q16, k16, v16,
             ag, hq, hf, hi, hg, scr):
        xv = x_ref[...]
        r = lax.rsqrt(jnp.mean(xv * xv, axis=-1, keepdims=True) + EPS)
        hn = ((xv * r) * mw_ref[...]).astype(BF16)
        hn_ref[...] = hn
        cosf, s1, s2 = _rope_tables(pos_ref[...])

        def proj(g):
            return _mm(hn, w_ref[:, g * 512:(g + 1) * 512])

        def emit(val, o1, o4, o16):
            _perm_store(val, scr, o1, o4, o16, BF16)

        emit(_rope(proj(0), cosf, s1, s2), q1, q4, q16)
        emit(_rope(proj(1), cosf, s1, s2), k1, k4, k16)
        emit(proj(2), v1, v4, v16)
        ag[...] = proj(3)
        hq[...] = proj(4)
        hf[...] = proj(5)
        hi[...] = proj(6).astype(BF16)
        hg[...] = proj(7)

    tok = lambda w: pl.BlockSpec((TT, w), lambda i: (i, 0))
    d4 = pl.BlockSpec((4, TT // 4, AW), lambda i: (0, i, 0))
    d16 = pl.BlockSpec((16, TT // 16, AW), lambda i: (0, i, 0))
    sd = lambda shape, dt: jax.ShapeDtypeStruct(shape, dt)
    return pl.pallas_call(
        body, name="fwd_in", grid=(T // TT,),
        in_specs=[tok(D), tok(1), pl.BlockSpec((1, D), lambda i: (0, 0)),
                  pl.BlockSpec((D, NCOL), lambda i: (0, 0))],
        out_specs=[tok(D)] + [tok(AW)] * 3 + [d4] * 3 + [d16] * 3 + [tok(AW)] * 5,
        out_shape=[sd((T, D), BF16)] + [sd((T, AW), BF16)] * 3 + [sd((4, T // 4, AW), BF16)] * 3
        + [sd((16, T // 16, AW), BF16)] * 3
        + [sd((T, AW), F32), sd((T, AW), F32), sd((T, AW), F32), sd((T, AW), BF16), sd((T, AW), F32)],
        scratch_shapes=[pltpu.VMEM((4, TT, 128), F32)],
        compiler_params=_cp(("parallel",)),
    )(x, pos, mixw, w_full)


def _band_masks():
    qi = lax.broadcasted_iota(jnp.int32, (BLK, 2 * BLK), 0)
    kj = lax.broadcasted_iota(jnp.int32, (BLK, 2 * BLK), 1)
    band = (kj >= qi) & (kj <= qi + BLK)
    return band, kj


def _attn_fwd(q, k, v, nb, name):
    CH = 2 * BLK

    def body(q_ref, k_ref, v_ref, kp_ref, vp_ref, o_ref, lse_ref):
        i = pl.program_id(0)
        thr0 = jnp.where((2 * i) % nb == 0, BLK, 0)
        lane = lax.broadcasted_iota(jnp.int32, (1, 128), 1)
        in_a = [lane < HEAD, lane >= HEAD]
        band, kj = _band_masks()
        mask0 = band & (kj >= thr0)
        for hp in range(4):
            cs = slice(hp * 128, (hp + 1) * 128)
            for b in range(2):
                rs = slice(b * BLK, (b + 1) * BLK)
                q2 = q_ref[rs, cs]
                if b == 0:
                    kk = jnp.concatenate([kp_ref[:, cs], k_ref[rs, cs]], axis=0)
                    vv = jnp.concatenate([vp_ref[:, cs], v_ref[rs, cs]], axis=0)
                    mask = mask0
                else:
                    kk = k_ref[:, cs]
                    vv = v_ref[:, cs]
                    mask = band
                res = []
                for a in range(2):
                    qa = jnp.where(in_a[a], q2, jnp.zeros_like(q2))
                    s = jnp.where(mask, _mm_nt(qa, kk) * SCALE, NEG)
                    m = jnp.max(s, axis=-1, keepdims=True)
                    p = jnp.exp(s - m)
                    l = jnp.sum(p, axis=-1, keepdims=True)
                    o = _mm(p.astype(BF16), vv)
                    res.append((o / l, m + jnp.log(l)))
                o_ref[rs, cs] = jnp.where(in_a[0], res[0][0], res[1][0])
                lse_ref[rs, cs] = jnp.where(in_a[0], res[0][1], res[1][1])

    cur = pl.BlockSpec((CH, AW), lambda i: (i, 0))
    prev = pl.BlockSpec((BLK, AW), lambda i: (jnp.maximum(2 * i - 1, 0), 0))
    return pl.pallas_call(
        body, name=name, grid=(T // CH,),
        in_specs=[cur, cur, cur, prev, prev],
        out_specs=[cur, cur],
        out_shape=[jax.ShapeDtypeStruct((T, AW), F32)] * 2,
        compiler_params=_cp(("parallel",)),
    )(q, k, v, k, v)


def _attn_bwd(q, k, v, do, lse, dl, nb, name):
    CH = 2 * BLK
    NBLK = T // BLK

    def body(q_ref, k_ref, v_ref, do_ref, lse_ref, dl_ref, kp_ref, vp_ref,
             qn_ref, don_ref, lsen_ref, dln_ref, dq_ref, dk_ref, dv_ref):
        i = pl.program_id(0)
        thr0 = jnp.where((2 * i) % nb == 0, BLK, 0)
        nxt_thr = jnp.where((2 * i + 2) % nb == 0, 2 * BLK, 0)
        lane = lax.broadcasted_iota(jnp.int32, (1, 128), 1)
        in_a = [lane < HEAD, lane >= HEAD]
        band, kj = _band_masks()
        mask0 = band & (kj >= thr0)
        qi1 = lax.broadcasted_iota(jnp.int32, (BLK, BLK), 0)
        kj1 = lax.broadcasted_iota(jnp.int32, (BLK, BLK), 1)
        mask_next = kj1 >= qi1 + nxt_thr

        def stat(x2, a):
            xr = pltpu.roll(x2, HEAD, 1)
            return jnp.where(in_a[0], x2, xr) if a == 0 else jnp.where(in_a[0], xr, x2)

        for hp in range(4):
            cs = slice(hp * 128, (hp + 1) * 128)
            kb = [kp_ref[:, cs], k_ref[0:BLK, cs], k_ref[BLK:CH, cs]]
            vb = [vp_ref[:, cs], v_ref[0:BLK, cs], v_ref[BLK:CH, cs]]
            dk_acc = [jnp.zeros((BLK, 128), F32), jnp.zeros((BLK, 128), F32)]
            dv_acc = [jnp.zeros((BLK, 128), F32), jnp.zeros((BLK, 128), F32)]
            for b in range(2):
                rs = slice(b * BLK, (b + 1) * BLK)
                q2, do2, lse2, dl2 = q_ref[rs, cs], do_ref[rs, cs], lse_ref[rs, cs], dl_ref[rs, cs]
                kk = jnp.concatenate([kb[b], kb[b + 1]], axis=0)
                vv = jnp.concatenate([vb[b], vb[b + 1]], axis=0)
                mask = mask0 if b == 0 else band
                dq_parts = []
                for a in range(2):
                    qa = jnp.where(in_a[a], q2, jnp.zeros_like(q2))
                    doa = jnp.where(in_a[a], do2, jnp.zeros_like(do2))
                    lse_a = stat(lse2, a)
                    dl_a = stat(dl2, a)
                    lse_w = jnp.concatenate([lse_a, lse_a], axis=1)
                    dl_w = jnp.concatenate([dl_a, dl_a], axis=1)
                    s = jnp.where(mask, _mm_nt(qa, kk) * SCALE, NEG)
                    p = jnp.exp(s - lse_w)
                    dp = _mm_nt(doa, vv)
                    ds = (p * (dp - dl_w)).astype(BF16)
                    pb = p.astype(BF16)
                    dq_parts.append(_mm(ds, kk) * SCALE)
                    if b == 0:
                        dk_acc[0] += _mm_tn(ds[:, BLK:], qa) * SCALE
                        dv_acc[0] += _mm_tn(pb[:, BLK:], doa)
                    else:
                        dkk = _mm_tn(ds, qa) * SCALE
                        dvv = _mm_tn(pb, doa)
                        dk_acc[0] += dkk[:BLK]
                        dk_acc[1] += dkk[BLK:]
                        dv_acc[0] += dvv[:BLK]
                        dv_acc[1] += dvv[BLK:]
                dq_ref[rs, cs] = jnp.where(in_a[0], dq_parts[0], dq_parts[1])
            q2, do2, lse2, dl2 = qn_ref[:, cs], don_ref[:, cs], lsen_ref[:, cs], dln_ref[:, cs]
            for a in range(2):
                qa = jnp.where(in_a[a], q2, jnp.zeros_like(q2))
                doa = jnp.where(in_a[a], do2, jnp.zeros_like(do2))
                s = jnp.where(mask_next, _mm_nt(qa, kb[2]) * SCALE, NEG)
                p = jnp.exp(s - stat(lse2, a))
                dp = _mm_nt(doa, vb[2])
                ds = (p * (dp - stat(dl2, a))).astype(BF16)
                dk_acc[1] += _mm_tn(ds, qa) * SCALE
                dv_acc[1] += _mm_tn(p.astype(BF16), doa)
            dk_ref[0:BLK, cs] = dk_acc[0]
            dk_ref[BLK:CH, cs] = dk_acc[1]
            dv_ref[0:BLK, cs] = dv_acc[0]
            dv_ref[BLK:CH, cs] = dv_acc[1]

    cur = pl.BlockSpec((CH, AW), lambda i: (i, 0))
    prev = pl.BlockSpec((BLK, AW), lambda i: (jnp.maximum(2 * i - 1, 0), 0))
    nxt = pl.BlockSpec((BLK, AW), lambda i: (jnp.minimum(2 * i + 2, NBLK - 1), 0))
    return pl.pallas_call(
        body, name=name, grid=(T // CH,),
        in_specs=[cur] * 6 + [prev] * 2 + [nxt] * 4,
        out_specs=[cur] * 3,
        out_shape=[jax.ShapeDtypeStruct((T, AW), F32)] * 3,
        compiler_params=_cp(("parallel",)),
    )(q, k, v, do, lse, dl, k, v, q, do, lse, dl)


TH = 256
NCH = TH // CHUNK


def _hgrn_common(hq_ref, hf_ref, lbr_ref, tri_ref):
    r0 = lbr_ref[0:1, :]
    r1 = lbr_ref[1:2, :]
    mx = jnp.maximum(r0, r1)
    e0 = jnp.exp(r0 - mx)
    e1 = jnp.exp(r1 - mx)
    lb = e0 / (e0 + e1)
    hqv = hq_ref[...]
    sq = _sigmoid(hqv)
    qv = hqv * sq
    sf = _sigmoid(hf_ref[...])
    f = lb + (1.0 - lb) * sf
    kv = 1.0 - f
    g = jnp.log(f)
    cum = _mm_exact_l(tri_ref[...], g)
    lastb = jnp.concatenate(
        [jnp.broadcast_to(cum[c * CHUNK + CHUNK - 1:(c + 1) * CHUNK, :], (CHUNK, HW)) for c in range(NCH)], axis=0)
    ea = jnp.exp(cum)
    ena = jnp.exp(-cum)
    eend = jnp.exp(lastb - cum)
    return dict(lb=lb, hq=hqv, sq=sq, q=qv, sf=sf, f=f, k=kv, cum=cum, lastb=lastb, ea=ea, ena=ena, eend=eend,
                qd=qv * ea, ki=kv * ena, ke=kv * eend, dec=jnp.exp(lastb))


def _tri_mask():
    ti = lax.broadcasted_iota(jnp.int32, (CHUNK, CHUNK), 0)
    si = lax.broadcasted_iota(jnp.int32, (CHUNK, CHUNK), 1)
    return si <= ti


def _hgrn_fwd(hq, hf, hi, lbr, tri):
    def body(hq_ref, hf_ref, hi_ref, lbr_ref, tri_ref, rec_ref, sall_ref, st_scr):
        @pl.when(pl.program_id(0) == 0)
        def _():
            st_scr[...] = jnp.zeros_like(st_scr)

        w = _hgrn_common(hq_ref, hf_ref, lbr_ref, tri_ref)
        qd, ki, ke = w["qd"].astype(BF16), w["ki"].astype(BF16), w["ke"].astype(BF16)
        dec = w["dec"]
        vb = hi_ref[...]
        causal = _tri_mask()
        for c in range(NCH):
            rs = slice(c * CHUNK, (c + 1) * CHUNK)
            sall_ref[c] = st_scr[...]
            for h in range(4):
                cs = slice(h * 128, (h + 1) * 128)
                st = st_scr[:, cs]
                att = jnp.where(causal, _mm_nt(qd[rs, cs], ki[rs, cs]), 0.0)
                o = _mm(att.astype(BF16), vb[rs, cs]) + _mm_nt(qd[rs, cs], st.astype(BF16))
                rec_ref[rs, cs] = o
                st_scr[:, cs] = dec[c * CHUNK:c * CHUNK + 1, cs] * st + _mm_tn(vb[rs, cs], ke[rs, cs])

    tok = pl.BlockSpec((TH, HW), lambda i: (i, 0))
    return pl.pallas_call(
        body, name="hgrn_fwd", grid=(T // TH,),
        in_specs=[tok, tok, tok, pl.BlockSpec((2, HW), lambda i: (0, 0)), pl.BlockSpec((TH, TH), lambda i: (0, 0))],
        out_specs=[tok, pl.BlockSpec((NCH, 128, HW), lambda i: (i, 0, 0))],
        out_shape=[jax.ShapeDtypeStruct((T, HW), F32), jax.ShapeDtypeStruct((T // CHUNK, 128, HW), F32)],
        scratch_shapes=[pltpu.VMEM((128, HW), F32)],
        compiler_params=_cp(("arbitrary",)),
    )(hq, hf, hi, lbr, tri)


def _hgrn_bwd(hq, hf, hi, lbr, tri, trit, drec, sall):
    NT = T // TH

    def body(hq_ref, hf_ref, hi_ref, lbr_ref, tri_ref, trit_ref, do_ref, sall_ref,
             dhq_ref, dhf_ref, dhi_ref, small_ref, dst_scr, dlb_scr, dqd_scr, dki_scr, dke_scr, dlast_scr):
        step = pl.program_id(0)

        @pl.when(step == 0)
        def _():
            dst_scr[...] = jnp.zeros_like(dst_scr)
            dlb_scr[...] = jnp.zeros_like(dlb_scr)

        w = _hgrn_common(hq_ref, hf_ref, lbr_ref, tri_ref)
        qd, ki, ke = w["qd"].astype(BF16), w["ki"].astype(BF16), w["ke"].astype(BF16)
        dec = w["dec"]
        vb = hi_ref[...]
        dob = do_ref[...].astype(BF16)
        causal = _tri_mask()
        for c in reversed(range(NCH)):
            rs = slice(c * CHUNK, (c + 1) * CHUNK)
            dec_c = dec[c * CHUNK:c * CHUNK + 1, :]
            for h in range(4):
                cs = slice(h * 128, (h + 1) * 128)
                st = sall_ref[c, :, cs]
                dst = dst_scr[:, cs]
                dstb = dst.astype(BF16)
                att = jnp.where(causal, _mm_nt(qd[rs, cs], ki[rs, cs]), 0.0).astype(BF16)
                datt = jnp.where(causal, _mm_nt(dob[rs, cs], vb[rs, cs]), 0.0).astype(BF16)
                dhi_ref[rs, cs] = (_mm_tn(att, dob[rs, cs]) + _mm_nt(ke[rs, cs], dstb)).astype(BF16)
                dqd_scr[rs, cs] = _mm(datt, ki[rs, cs]) + _mm(dob[rs, cs], st.astype(BF16))
                dki_scr[rs, cs] = _mm_tn(datt, qd[rs, cs])
                dke_scr[rs, cs] = _mm(vb[rs, cs], dstb)
                ddec = jnp.sum(dst * st, axis=0, keepdims=True)
                dlast_scr[c:c + 1, cs] = ddec * dec_c[:, cs]
                dst_scr[:, cs] = dec_c[:, cs] * dst + _mm_tn(dob[rs, cs], qd[rs, cs])
        dqd, dki, dke = dqd_scr[...], dki_scr[...], dke_scr[...]
        dq = dqd * w["ea"]
        dk = dki * w["ena"] + dke * w["eend"]
        dcum = dqd * w["qd"] - dki * w["ki"] - dke * w["ke"]
        dkeke = dke * w["ke"]
        dlastb = jnp.concatenate(
            [jnp.broadcast_to(dlast_scr[c:c + 1, :] + jnp.sum(dkeke[c * CHUNK:(c + 1) * CHUNK], axis=0, keepdims=True),
                              (CHUNK, HW)) for c in range(NCH)], axis=0)
        dg = _mm_exact_l(trit_ref[...], dcum) + dlastb
        df = dg / w["f"] - dk
        lb, sf, sq = w["lb"], w["sf"], w["sq"]
        dhf_ref[...] = (df * (1.0 - lb) * sf * (1.0 - sf)).astype(BF16)
        dhq_ref[...] = (dq * (sq * (1.0 + w["hq"] * (1.0 - sq)))).astype(BF16)
        dlb_scr[...] += jnp.sum(df * (1.0 - sf), axis=0, keepdims=True)

        @pl.when(step == NT - 1)
        def _():
            gr = dlb_scr[...] * lb * (1.0 - lb)
            small_ref[...] = jnp.zeros_like(small_ref)
            small_ref[0:1, 0:HW] = gr
            small_ref[1:2, 0:HW] = -gr

    tok = pl.BlockSpec((TH, HW), lambda i: (NT - 1 - i, 0))
    const = lambda shape: pl.BlockSpec(shape, lambda i: (0,) * len(shape))
    return pl.pallas_call(
        body, name="hgrn_bwd", grid=(NT,),
        in_specs=[tok, tok, tok, const((2, HW)), const((TH, TH)), const((TH, TH)), tok,
                  pl.BlockSpec((NCH, 128, HW), lambda i: (NT - 1 - i, 0, 0))],
        out_specs=[tok, tok, tok, const((8, D))],
        out_shape=[jax.ShapeDtypeStruct((T, HW), BF16)] * 3 + [jax.ShapeDtypeStruct((8, D), F32)],
        scratch_shapes=[pltpu.VMEM((128, HW), F32), pltpu.VMEM((1, HW), F32), pltpu.VMEM((TH, HW), F32),
                        pltpu.VMEM((TH, HW), F32), pltpu.VMEM((TH, HW), F32), pltpu.VMEM((8, HW), F32)],
        compiler_params=_cp(("arbitrary",)),
    )(hq, hf, hi, lbr, tri, trit, drec, sall)


def _fwd_out(o1, o4, o16, l1, l4, l16, rec, ag, hg, x, tgt, anw, hnw, fnw, wout_full, gmat):
    TT = 256

    def body(o1_r, o4_r, o16_r, l1_r, l4_r, l16_r, rec_r, ag_r, hg_r, x_r, tgt_r, anw_r, hnw_r, fnw_r, wo_r, g_r,
             dx2_o, do1_o, do4_o, do16_o, ls1_o, ls4_o, ls16_o, dl1_o, dl4_o, dl16_o, drec_o, dag_o, dhg_o,
             gwout_o, small_o, scr_a, scr_b):
        @pl.when(pl.program_id(0) == 0)
        def _():
            gwout_o[...] = jnp.zeros_like(gwout_o)
            small_o[...] = jnp.zeros_like(small_o)

        def unperm(r4, r16):
            return _unperm_load(r4, r16, scr_a, scr_b)

        def perm_out(val, p1, p4, p16, dt):
            _perm_store(val, scr_a, p1, p4, p16, dt)

        o4u, o16u = unperm(o4_r, o16_r)
        l4u, l16u = unperm(l4_r, l16_r)
        o1v, l1v = o1_r[...], l1_r[...]
        mx = jnp.maximum(jnp.maximum(l1v, l4u), l16u)
        w1, w4, w16 = jnp.exp(l1v - mx), jnp.exp(l4u - mx), jnp.exp(l16u - mx)
        den = w1 + w4 + w16
        attn = (w1 * o1v + w4 * o4u + w16 * o16u) / den
        lse = mx + jnp.log(den)
        gm = g_r[...]

        def head_mean_a(t):
            return _mm_exact_r(t, gm)

        def head_mean_h(t):
            return jnp.concatenate(
                [jnp.broadcast_to(jnp.mean(t[:, h * 128:(h + 1) * 128], axis=-1, keepdims=True), (TT, 128))
                 for h in range(4)], axis=1)

        rs_a = lax.rsqrt(head_mean_a(attn * attn) + EPS)
        n_a = attn * rs_a
        agv = ag_r[...]
        sg_a = _sigmoid(agv)
        si_a = agv * sg_a
        anw_v = anw_r[...]
        y_a = (n_a * anw_v) * si_a
        recv = rec_r[...]
        rs_h = lax.rsqrt(head_mean_h(recv * recv) + EPS)
        n_h = recv * rs_h
        hgv = hg_r[...]
        sg_h = _sigmoid(hgv)
        si_h = hgv * sg_h
        hnw_v = hnw_r[...]
        y_h = (n_h * hnw_v) * si_h
        mixed = jnp.concatenate([y_a, y_h], axis=1).astype(BF16)
        xv = x_r[...]
        x2 = xv + _mm(mixed, wo_r[...])
        r2 = lax.rsqrt(jnp.mean(x2 * x2, axis=-1, keepdims=True) + EPS)
        fnw_v = fnw_r[...]
        xn = x2 * r2
        err = xn * fnw_v - tgt_r[...]
        small_o[2:3, :] += 0.5 * jnp.sum(jnp.mean(err * err, axis=-1, keepdims=True), axis=0, keepdims=True)
        dy = err * (1.0 / D)
        small_o[0:1, :] += jnp.sum(dy * xn, axis=0, keepdims=True)
        dyw = dy * fnw_v
        dx2 = r2 * dyw - x2 * ((r2 * r2 * r2) * jnp.mean(dyw * x2, axis=-1, keepdims=True))
        dx2_o[...] = dx2
        dx2b = dx2.astype(BF16)
        gwout_o[...] += _mm_tn(mixed, dx2b)
        dmix = _mm_nt(dx2b, wo_r[...])
        dm_a, dm_h = dmix[:, :AW], dmix[:, AW:]
        dag_o[...] = (dm_a * (n_a * anw_v) * (sg_a * (1.0 + agv * (1.0 - sg_a)))).astype(BF16)
        dn_a = dm_a * anw_v * si_a
        small_o[1:2, 0:AW] += jnp.sum(dm_a * n_a * si_a, axis=0, keepdims=True)
        dattn = rs_a * (dn_a - n_a * head_mean_a(dn_a * n_a))
        delta = head_mean_a(dattn * attn) * float(HEAD)
        perm_out(dattn, do1_o, do4_o, do16_o, BF16)
        perm_out(lse, ls1_o, ls4_o, ls16_o, F32)
        perm_out(delta, dl1_o, dl4_o, dl16_o, F32)
        dhg_o[...] = (dm_h * (n_h * hnw_v) * (sg_h * (1.0 + hgv * (1.0 - sg_h)))).astype(BF16)
        dn_h = dm_h * hnw_v * si_h
        small_o[1:2, AW:] += jnp.sum(dm_h * n_h * si_h, axis=0, keepdims=True)
        drec_o[...] = rs_h * (dn_h - n_h * head_mean_h(dn_h * n_h))

    tok = lambda w: pl.BlockSpec((TT, w), lambda i: (i, 0))
    d4 = pl.BlockSpec((4, TT // 4, AW), lambda i: (0, i, 0))
    d16 = pl.BlockSpec((16, TT // 16, AW), lambda i: (0, i, 0))
    const = lambda shape: pl.BlockSpec(shape, lambda i: (0,) * len(shape))
    sd = lambda shape, dt: jax.ShapeDtypeStruct(shape, dt)
    p3 = lambda dt: [sd((T, AW), dt), sd((4, T // 4, AW), dt), sd((16, T // 16, AW), dt)]
    return pl.pallas_call(
        body, name="fwd_out", grid=(T // TT,),
        in_specs=[tok(AW), d4, d16, tok(AW), d4, d16, tok(AW), tok(AW), tok(AW), tok(D), tok(D),
                  const((1, AW)), const((1, HW)), const((1, D)), const((D, D)), const((AW, AW))],
        out_specs=[tok(D)] + [tok(AW), d4, d16] * 3 + [tok(AW)] * 3 + [const((D, D)), const((8, D))],
        out_shape=[sd((T, D), F32)] + p3(BF16) + p3(F32) + p3(F32)
        + [sd((T, AW), F32), sd((T, AW), BF16), sd((T, AW), BF16), sd((D, D), F32), sd((8, D), F32)],
        scratch_shapes=[pltpu.VMEM((4, TT, 128), F32), pltpu.VMEM((4, TT, 128), F32)],
        compiler_params=_cp(("arbitrary",)),
    )(o1, o4, o16, l1, l4, l16, rec, ag, hg, x, tgt, anw, hnw, fnw, wout_full, gmat)


def _bwd_in(dq, dk, dv, dag, dhq, dhf, dhi, dhg, pos, x, dx2, mixw, w_full):
    TT = 256

    def body(dq1, dq4, dq16, dk1, dk4, dk16, dv1, dv4, dv16, dag_r, dhq_r, dhf_r, dhi_r, dhg_r,
             pos_r, x_r, dx2_r, mw_r, w_r, gx_o, dproj_o, small_o, scr_a, scr_b):
        @pl.when(pl.program_id(0) == 0)
        def _():
            small_o[...] = jnp.zeros_like(small_o)

        def unperm_sum(r1, r4, r16):
            u4, u16 = _unperm_load(r4, r16, scr_a, scr_b)
            return r1[...] + u4 + u16

        cosf, s1, s2 = _rope_tables(pos_r[...])
        dproj_o[:, 0:512] = _rope_bwd(unperm_sum(dq1, dq4, dq16), cosf, s1, s2).astype(BF16)
        dproj_o[:, 512:1024] = _rope_bwd(unperm_sum(dk1, dk4, dk16), cosf, s1, s2).astype(BF16)
        dproj_o[:, 1024:1536] = unperm_sum(dv1, dv4, dv16).astype(BF16)
        dproj_o[:, 1536:2048] = dag_r[...]
        dproj_o[:, 2048:2560] = dhq_r[...]
        dproj_o[:, 2560:3072] = dhf_r[...]
        dproj_o[:, 3072:3584] = dhi_r[...]
        dproj_o[:, 3584:4096] = dhg_r[...]
        dhn = _mm_nt(dproj_o[...], w_r[...])
        xv = x_r[...]
        r = lax.rsqrt(jnp.mean(xv * xv, axis=-1, keepdims=True) + EPS)
        dxw = dhn * mw_r[...]
        gx_o[...] = dx2_r[...] + r * dxw - xv * ((r * r * r) * jnp.mean(dxw * xv, axis=-1, keepdims=True))
        small_o[0:1, :] += jnp.sum(dhn * (xv * r), axis=0, keepdims=True)

    tok = lambda w: pl.BlockSpec((TT, w), lambda i: (i, 0))
    d4 = pl.BlockSpec((4, TT // 4, AW), lambda i: (0, i, 0))
    d16 = pl.BlockSpec((16, TT // 16, AW), lambda i: (0, i, 0))
    const = lambda shape: pl.BlockSpec(shape, lambda i: (0,) * len(shape))
    return pl.pallas_call(
        body, name="bwd_in", grid=(T // TT,),
        in_specs=[tok(AW), d4, d16] * 3 + [tok(AW)] * 5 + [tok(1), tok(D), tok(D), const((1, D)), const((D, NCOL))],
        out_specs=[tok(D), tok(NCOL), const((8, D))],
        out_shape=[jax.ShapeDtypeStruct((T, D), F32), jax.ShapeDtypeStruct((T, NCOL), BF16),
                   jax.ShapeDtypeStruct((8, D), F32)],
        scratch_shapes=[pltpu.VMEM((4, TT, 128), F32), pltpu.VMEM((4, TT, 128), F32)],
        compiler_params=_cp(("arbitrary",)),
    )(*dq, *dk, *dv, dag, dhq, dhf, dhi, dhg, pos, x, dx2, mixw, w_full)


def _grad_w_in(hn, dproj):
    TK = 512

    def body(hn_r, dp_r, out_r):
        @pl.when(pl.program_id(1) == 0)
        def _():
            out_r[...] = jnp.zeros_like(out_r)

        out_r[...] += _mm_tn(hn_r[...], dp_r[...])

    return pl.pallas_call(
        body, name="grad_w_in", grid=(4, T // TK),
        in_specs=[pl.BlockSpec((TK, D), lambda j, kk: (kk, 0)), pl.BlockSpec((TK, 1024), lambda j, kk: (kk, j))],
        out_specs=pl.BlockSpec((D, 1024), lambda j, kk: (0, j)),
        out_shape=jax.ShapeDtypeStruct((D, NCOL), F32),
        compiler_params=_cp(("parallel", "arbitrary")),
    )(hn, dproj)


def _pair_exchange(gin, gout4):
    def body(gin_r, gout_r, bin_o, bout_o, send_sems, recv_sems, loc_sems):
        x, y, c = lax.axis_index("x"), lax.axis_index("y"), lax.axis_index("c")
        sibling = (x, y, 1 - c)
        loc = [pltpu.make_async_copy(gin_r.at[pl.ds(c * 512, 512), :], bin_o.at[0], loc_sems.at[0]),
               pltpu.make_async_copy(gout_r.at[:, pl.ds(c, 1)], bout_o.at[0], loc_sems.at[1])]
        rem = [pltpu.make_async_remote_copy(src_ref=gin_r.at[pl.ds((1 - c) * 512, 512), :], dst_ref=bin_o.at[1],
                                            send_sem=send_sems.at[0], recv_sem=recv_sems.at[0],
                                            device_id=sibling, device_id_type=MESH),
               pltpu.make_async_remote_copy(src_ref=gout_r.at[:, pl.ds(1 - c, 1)], dst_ref=bout_o.at[1],
                                            send_sem=send_sems.at[1], recv_sem=recv_sems.at[1],
                                            device_id=sibling, device_id_type=MESH)]
        for cp in loc + rem:
            cp.start()
        for cp in rem:
            cp.wait_recv()
        for cp in rem:
            cp.wait_send()
        for cp in loc:
            cp.wait()

    hbm = pl.BlockSpec(memory_space=pltpu.HBM)
    return pl.pallas_call(
        body, name="pair_exchange",
        out_shape=(jax.ShapeDtypeStruct((2, 512, NCOL), F32), jax.ShapeDtypeStruct((2, 4, 1, 128, D), F32)),
        in_specs=[hbm, hbm], out_specs=(hbm, hbm),
        scratch_shapes=[pltpu.SemaphoreType.DMA((2,)), pltpu.SemaphoreType.DMA((2,)), pltpu.SemaphoreType.DMA((2,))],
        compiler_params=_cp(),
    )(gin, gout4)


def _add_slots(a, name):
    n, rows, cols = a.shape
    tr = min(rows, 256)
    tc = min(cols, 1024)

    def body(a_r, o_r):
        acc = a_r[0]
        for s in range(1, n):
            acc = acc + a_r[s]
        o_r[...] = acc

    return pl.pallas_call(
        body, name=name, grid=(rows // tr, cols // tc),
        in_specs=[pl.BlockSpec((n, tr, tc), lambda i, j: (0, i, j))],
        out_specs=pl.BlockSpec((tr, tc), lambda i, j: (i, j)),
        out_shape=jax.ShapeDtypeStruct((rows, cols), F32),
        compiler_params=_cp(("parallel", "parallel")),
    )(a)


def _chip_exchange(rin, rout, small):
    def body(rin_r, rout_r, small_r, pin_o, pout_o, sall_o, send_sems, recv_sems, loc_sems):
        x, y, c = lax.axis_index("x"), lax.axis_index("y"), lax.axis_index("c")
        chips = [(1 - x, y), (x, 1 - y), (1 - x, 1 - y)]
        jm = 2 * x + y
        me = 4 * x + 2 * y + c
        loc = [pltpu.make_async_copy(rin_r.at[:, pl.ds(jm * 1024, 1024)], pin_o.at[0], loc_sems.at[0]),
               pltpu.make_async_copy(rout_r.at[pl.ds(jm * 128, 128), :], pout_o.at[0], loc_sems.at[1]),
               pltpu.make_async_copy(small_r, sall_o.at[me], loc_sems.at[2])]
        rem = []
        for k, (px, py) in enumerate(chips):
            j = 2 * px + py
            rem.append(pltpu.make_async_remote_copy(
                src_ref=rin_r.at[:, pl.ds(j * 1024, 1024)], dst_ref=pin_o.at[k + 1],
                send_sem=send_sems.at[k], recv_sem=recv_sems.at[k], device_id=(px, py, c), device_id_type=MESH))
            rem.append(pltpu.make_async_remote_copy(
                src_ref=rout_r.at[pl.ds(j * 128, 128), :], dst_ref=pout_o.at[k + 1],
                send_sem=send_sems.at[3 + k], recv_sem=recv_sems.at[3 + k], device_id=(px, py, c),
                device_id_type=MESH))
        k = 6
        for fx in range(2):
            for fy in range(2):
                for fc in range(2):
                    if fx or fy or fc:
                        peer = (1 - x if fx else x, 1 - y if fy else y, 1 - c if fc else c)
                        rem.append(pltpu.make_async_remote_copy(
                            src_ref=small_r, dst_ref=sall_o.at[me], send_sem=send_sems.at[k],
                            recv_sem=recv_sems.at[k], device_id=peer, device_id_type=MESH))
                        k += 1
        for cp in loc + rem:
            cp.start()
        for cp in rem:
            cp.wait_recv()
        for cp in rem:
            cp.wait_send()
        for cp in loc:
            cp.wait()

    hbm = pl.BlockSpec(memory_space=pltpu.HBM)
    return pl.pallas_call(
        body, name="chip_exchange",
        out_shape=(jax.ShapeDtypeStruct((4, 512, 1024), F32), jax.ShapeDtypeStruct((4, 128, D), F32),
                   jax.ShapeDtypeStruct((8, 24, D), F32)),
        in_specs=[hbm, hbm, hbm], out_specs=(hbm, hbm, hbm),
        scratch_shapes=[pltpu.SemaphoreType.DMA((13,)), pltpu.SemaphoreType.DMA((13,)),
                        pltpu.SemaphoreType.DMA((3,))],
        compiler_params=_cp(),
    )(rin, rout, small)


def _pair_share(pin, pout):
    def body(pin_r, pout_r, fin_o, fout_o, send_sems, recv_sems, loc_sems):
        x, y, c = lax.axis_index("x"), lax.axis_index("y"), lax.axis_index("c")
        sibling = (x, y, 1 - c)
        loc = [pltpu.make_async_copy(pin_r, fin_o.at[c], loc_sems.at[0]),
               pltpu.make_async_copy(pout_r, fout_o.at[c], loc_sems.at[1])]
        rem = [pltpu.make_async_remote_copy(src_ref=pin_r, dst_ref=fin_o.at[c], send_sem=send_sems.at[0],
                                            recv_sem=recv_sems.at[0], device_id=sibling, device_id_type=MESH),
               pltpu.make_async_remote_copy(src_ref=pout_r, dst_ref=fout_o.at[c], send_sem=send_sems.at[1],
                                            recv_sem=recv_sems.at[1], device_id=sibling, device_id_type=MESH)]
        for cp in loc + rem:
            cp.start()
        for cp in rem:
            cp.wait_recv()
        for cp in rem:
            cp.wait_send()
        for cp in loc:
            cp.wait()

    hbm = pl.BlockSpec(memory_space=pltpu.HBM)
    return pl.pallas_call(
        body, name="pair_share",
        out_shape=(jax.ShapeDtypeStruct((2, 512, 1024), F32), jax.ShapeDtypeStruct((2, 128, D), F32)),
        in_specs=[hbm, hbm], out_specs=(hbm, hbm),
        scratch_shapes=[pltpu.SemaphoreType.DMA((2,)), pltpu.SemaphoreType.DMA((2,)), pltpu.SemaphoreType.DMA((2,))],
        compiler_params=_cp(),
    )(pin, pout)


def _adamw_math(w, g, m, v):
    m = B1 * m + (1.0 - B1) * g
    v = B2 * v + (1.0 - B2) * (g * g)
    m_hat = m / (1.0 - B1 ** STEP)
    v_hat = v / (1.0 - B2 ** STEP)
    delta = -LR * (m_hat / (jnp.sqrt(v_hat) + AEPS) + WD * w)
    return delta, m, v


def _adamw(w, g, m, v, name):
    rows, cols = w.shape
    tr = min(rows, 256)

    def body(w_r, g_r, m_r, v_r, d_o, m_o, v_o):
        d, mm, vv = _adamw_math(w_r[...], g_r[...], m_r[...], v_r[...])
        d_o[...] = d
        m_o[...] = mm
        v_o[...] = vv

    blk = pl.BlockSpec((tr, cols), lambda i: (i, 0))
    return pl.pallas_call(
        body, name=name, grid=(rows // tr,),
        in_specs=[blk] * 4, out_specs=[blk] * 3,
        out_shape=[jax.ShapeDtypeStruct((rows, cols), F32)] * 3,
        compiler_params=_cp(("parallel",)),
    )(w, g, m, v)


def _adamw_small(sall, params):
    def body(sall_r, *refs):
        ins, outs = refs[:15], refs[15:]
        tot = sall_r[0]
        for dv in range(1, 8):
            tot = tot + sall_r[dv]
        grads = [tot[16:17, :], tot[1:2, 0:AW], tot[1:2, AW:], tot[8:10, 0:HW], tot[0:1, :]]
        outs[0][...] = tot[2:3, 0:1]
        for p in range(5):
            w_r, m_r, v_r = ins[3 * p:3 * p + 3]
            g = grads[p]
            d, mm, vv = _adamw_math(w_r[...], g, m_r[...], v_r[...])
            outs[1 + 4 * p][...] = g
            outs[2 + 4 * p][...] = d
            outs[3 + 4 * p][...] = mm
            outs[4 + 4 * p][...] = vv

    flat = [a for p in params for a in p]
    shapes = [jax.ShapeDtypeStruct((1, 1), F32)]
    for p in params:
        shapes += [jax.ShapeDtypeStruct(p[0].shape, F32)] * 4
    vm = pl.BlockSpec(memory_space=pltpu.VMEM)
    return pl.pallas_call(
        body, name="adamw_small",
        in_specs=[vm] * 16, out_specs=[vm] * 21, out_shape=shapes,
        compiler_params=_cp(),
    )(sall, *flat)


def kernel(x, positions, w_in, w_out, mix_norm_w, attn_out_norm_w, hgrn_out_norm_w, hgrn_lb_raw, final_norm_w, loss_target, m_w_in, m_w_out, m_mix_norm_w, m_attn_out_norm_w, m_hgrn_out_norm_w, m_hgrn_lb_raw, m_final_norm_w, v_w_in, v_w_out, v_mix_norm_w, v_attn_out_norm_w, v_hgrn_out_norm_w, v_hgrn_lb_raw, v_final_norm_w):
    xs = x.reshape(T, D)
    tgt = loss_target.reshape(T, D)
    pos = positions.reshape(T, 1)
    fnw = final_norm_w.reshape(1, D)

    ti = np.arange(TH)
    tri_np = ((ti[:, None] // CHUNK == ti[None, :] // CHUNK) & (ti[None, :] <= ti[:, None])).astype(np.float32)
    tri = jnp.asarray(tri_np, BF16)
    trit = jnp.asarray(tri_np.T, BF16)
    hi_ = np.arange(AW) // HEAD
    gmat = jnp.asarray((hi_[:, None] == hi_[None, :]).astype(np.float32) / HEAD, BF16)

    w_full, wout_full = _weight_gather(w_in.reshape(D, 1024), w_out.reshape(256, D))

    (hn, q1, k1, v1, q4, k4, v4, q16, k16, v16, ag, hq, hf, hi, hg) = _fwd_in(xs, pos, mix_norm_w, w_full)
    flat = lambda a: a.reshape(T, AW)
    o1, l1 = _attn_fwd(q1, k1, v1, T // BLK, "attn_fwd_d1")
    o4, l4 = _attn_fwd(flat(q4), flat(k4), flat(v4), T // 4 // BLK, "attn_fwd_d4")
    o16, l16 = _attn_fwd(flat(q16), flat(k16), flat(v16), T // 16 // BLK, "attn_fwd_d16")
    rec, sall = _hgrn_fwd(hq, hf, hi, hgrn_lb_raw, tri)

    (dx2, do1, do4, do16, ls1, ls4, ls16, dl1, dl4, dl16, drec, dag, dhg, gw_out, small4) = _fwd_out(
        o1, o4.reshape(4, T // 4, AW), o16.reshape(16, T // 16, AW),
        l1, l4.reshape(4, T // 4, AW), l16.reshape(16, T // 16, AW),
        rec, ag, hg, xs, tgt, attn_out_norm_w, hgrn_out_norm_w, fnw, wout_full, gmat)

    dq1, dk1, dv1 = _attn_bwd(q1, k1, v1, do1, ls1, dl1, T // BLK, "attn_bwd_d1")
    dq4, dk4, dv4 = _attn_bwd(flat(q4), flat(k4), flat(v4), flat(do4), flat(ls4), flat(dl4), T // 4 // BLK,
                              "attn_bwd_d4")
    dq16, dk16, dv16 = _attn_bwd(flat(q16), flat(k16), flat(v16), flat(do16), flat(ls16), flat(dl16),
                                 T // 16 // BLK, "attn_bwd_d16")
    dhq, dhf, dhi, small6 = _hgrn_bwd(hq, hf, hi, hgrn_lb_raw, tri, trit, drec, sall)

    r4 = lambda a: a.reshape(4, T // 4, AW)
    r16 = lambda a: a.reshape(16, T // 16, AW)
    gx, dproj, small8 = _bwd_in((dq1, r4(dq4), r16(dq16)), (dk1, r4(dk4), r16(dk16)), (dv1, r4(dv4), r16(dv16)),
                                dag, dhq, dhf, dhi, dhg, pos, xs, dx2, mix_norm_w, w_full)
    gw_in = _grad_w_in(hn, dproj)

    small = jnp.concatenate([small4, small6, small8], axis=0)
    bin_, bout_ = _pair_exchange(gw_in, gw_out.reshape(4, 2, 128, D))
    rin = _add_slots(bin_, "pair_sum_in")
    rout = _add_slots(bout_.reshape(2, 4 * 128, D), "pair_sum_out")
    pin_s, pout_s, small_all = _chip_exchange(rin, rout, small)
    pin = _add_slots(pin_s, "chip_sum_in")
    pout = _add_slots(pout_s, "chip_sum_out")
    fin, fout = _pair_share(pin, pout)
    g_w_in = fin.reshape(D, 1024)
    g_w_out = fout.reshape(256, D)

    d_in, nm_in, nv_in = _adamw(w_in.reshape(D, 1024), g_w_in, m_w_in.reshape(D, 1024), v_w_in.reshape(D, 1024),
                                "adamw_w_in")
    d_out, nm_out, nv_out = _adamw(w_out.reshape(256, D), g_w_out, m_w_out.reshape(256, D), v_w_out.reshape(256, D),
                                   "adamw_w_out")
    params = [(mix_norm_w, m_mix_norm_w, v_mix_norm_w),
              (attn_out_norm_w, m_attn_out_norm_w, v_attn_out_norm_w),
              (hgrn_out_norm_w, m_hgrn_out_norm_w, v_hgrn_out_norm_w),
              (hgrn_lb_raw, m_hgrn_lb_raw, v_hgrn_lb_raw),
              (fnw, m_final_norm_w.reshape(1, D), v_final_norm_w.reshape(1, D))]
    so = _adamw_small(small_all, params)
    loss = so[0].reshape(())
    g_s = [so[1 + 4 * p] for p in range(5)]
    d_s = [so[2 + 4 * p] for p in range(5)]
    m_s = [so[3 + 4 * p] for p in range(5)]
    v_s = [so[4 + 4 * p] for p in range(5)]
    for lst in (g_s, d_s, m_s, v_s):
        lst[4] = lst[4].reshape(D)

    return (loss, gx.reshape(1, T, D),
            g_w_in.reshape(1, D, 1024), g_w_out.reshape(1, 256, D), *g_s,
            d_in.reshape(1, D, 1024), d_out.reshape(1, 256, D), *d_s,
            nm_in.reshape(1, D, 1024), nm_out.reshape(1, 256, D), *m_s,
            nv_in.reshape(1, D, 1024), nv_out.reshape(1, 256, D), *v_s)
```

```python
import functools

import numpy as np
import jax
import jax.numpy as jnp
from jax import lax
from jax.experimental import pallas as pl
from jax.experimental.pallas import tpu as pltpu

F32 = jnp.float32
BF16 = jnp.bfloat16

T = 4096
D = 1024
AW = 512
HW = 512
NCOL = 4096
HEAD = 64
BLK = 128
CHUNK = 64
EPS = 1e-6
SCALE = HEAD ** -0.5
NEG = -1e30
ROPE_THETA = 500000.0
INV_FREQ = [float(v) for v in
            (np.float32(ROPE_THETA) ** (-(np.arange(8, dtype=np.float32)) * np.float32(0.125)))]
LR, B1, B2, AEPS, WD, STEP = 0.001, 0.9, 0.999, 1e-08, 0.01, 10
VMEM_LIMIT = 56 * 1024 * 1024
MESH = pl.DeviceIdType.MESH


def _cp(sem=None, **kw):
    return pltpu.CompilerParams(dimension_semantics=sem, vmem_limit_bytes=VMEM_LIMIT, **kw)


def _mm(a, b):
    return jnp.dot(a, b, preferred_element_type=F32)


def _mm_nt(a, b):
    return lax.dot_general(a, b, (((1,), (1,)), ((), ())), preferred_element_type=F32)


def _mm_tn(a, b):
    return lax.dot_general(a, b, (((0,), (0,)), ((), ())), preferred_element_type=F32)


def _split3(x):
    h = x.astype(BF16)
    r = x - h.astype(F32)
    m = r.astype(BF16)
    l = (r - m.astype(F32)).astype(BF16)
    return h, m, l


def _mm_exact_l(mat_bf, x):
    h, m, l = _split3(x)
    return _mm(mat_bf, h) + _mm(mat_bf, m) + _mm(mat_bf, l)


def _mm_exact_r(x, mat_bf):
    h, m, l = _split3(x)
    return _mm(h, mat_bf) + _mm(m, mat_bf) + _mm(l, mat_bf)


def _sigmoid(x):
    return 1.0 / (1.0 + jnp.exp(-x))


def _rope_tables(pos):
    lane = lax.broadcasted_iota(jnp.int32, (1, 128), 1)
    jl = lane & 63
    fi = jl & 7
    inv = jnp.zeros((1, 128), F32)
    for kk in range(8):
        inv = jnp.where(fi == kk, INV_FREQ[kk], inv)
    ang = pos.astype(F32) * inv
    c = jnp.cos(ang)
    s = jnp.sin(ang)
    cosf = jnp.where(jl < 16, c, 1.0)
    s1 = jnp.where(jl < 8, -s, 0.0)
    s2 = jnp.where((jl >= 8) & (jl < 16), s, 0.0)
    return cosf, s1, s2


def _rope(t, cosf, s1, s2):
    parts = []
    for ci in range(t.shape[1] // 128):
        tc = t[:, ci * 128:(ci + 1) * 128]
        parts.append(tc * cosf + pltpu.roll(tc, 120, 1) * s1 + pltpu.roll(tc, 8, 1) * s2)
    return jnp.concatenate(parts, axis=1)


def _rope_bwd(g, cosf, s1, s2):
    parts = []
    for ci in range(g.shape[1] // 128):
        gc = g[:, ci * 128:(ci + 1) * 128]
        parts.append(gc * cosf + pltpu.roll(gc * s1, 8, 1) + pltpu.roll(gc * s2, 120, 1))
    return jnp.concatenate(parts, axis=1)


def _perm_store(val, scr, o1, o4, o16, dt):
    n = val.shape[0]
    o1[...] = val.astype(dt)
    for ci in range(4):
        cs = slice(ci * 128, (ci + 1) * 128)
        scr[ci] = val[:, cs]
        for rr in range(4):
            o4[rr, :, cs] = scr[ci, pl.ds(rr, n // 4, stride=4), :].astype(dt)
        for rr in range(16):
            o16[rr, :, cs] = scr[ci, pl.ds(rr, n // 16, stride=16), :].astype(dt)


def _unperm_load(r4, r16, scr_a, scr_b):
    n = scr_a.shape[1]
    for ci in range(4):
        cs = slice(ci * 128, (ci + 1) * 128)
        for rr in range(4):
            scr_a[ci, pl.ds(rr, n // 4, stride=4), :] = r4[rr, :, cs]
        for rr in range(16):
            scr_b[ci, pl.ds(rr, n // 16, stride=16), :] = r16[rr, :, cs]
    return (jnp.concatenate([scr_a[ci] for ci in range(4)], axis=1),
            jnp.concatenate([scr_b[ci] for ci in range(4)], axis=1))


def _weight_gather(w_in, w_out):
    def body(win_ref, wout_ref, fin_ref, fout_ref, bin_ref, bout_ref, send_sems, recv_sems, loc_sems):
        x, y, c = lax.axis_index("x"), lax.axis_index("y"), lax.axis_index("c")
        sibling = (x, y, 1 - c)
        chips = [(1 - x, y), (x, 1 - y), (1 - x, 1 - y)]
        jm = 2 * x + y
        bin_ref[...] = win_ref[...].astype(BF16)
        bout_ref[...] = wout_ref[...].astype(BF16)

        def in_rows(px, py, half):
            return fin_ref.at[pl.ds(half * 512, 512), pl.ds((2 * px + py) * 1024, 1024)]

        def out_rows(px, py, half):
            return fout_ref.at[pl.ds((2 * px + py) * 256 + half * 128, 128), :]

        def rcopy(k, src, dst, to):
            return pltpu.make_async_remote_copy(src_ref=src, dst_ref=dst, send_sem=send_sems.at[k],
                                                recv_sem=recv_sems.at[k], device_id=to, device_id_type=MESH)

        loc_in = pltpu.make_async_copy(bin_ref, fin_ref.at[:, pl.ds(jm * 1024, 1024)], loc_sems.at[0])
        loc_out = pltpu.make_async_copy(bout_ref, fout_ref.at[pl.ds(jm * 256, 256), :], loc_sems.at[1])
        loc_in.start()
        loc_out.start()
        first = []
        for k, chip in enumerate(chips):
            first.append(rcopy(k, bin_ref.at[pl.ds(c * 512, 512), :], in_rows(x, y, c), (*chip, c)))
            first.append(rcopy(3 + k, bout_ref.at[pl.ds(c * 128, 128), :], out_rows(x, y, c), (*chip, c)))
        for cp in first:
            cp.start()
        passed = []
        for k, chip in enumerate(chips):
            rcopy(k, in_rows(*chip, c), in_rows(*chip, c), (x, y, c)).wait_recv()
            p1 = rcopy(6 + k, in_rows(*chip, c), in_rows(*chip, c), sibling)
            p1.start()
            rcopy(3 + k, out_rows(*chip, c), out_rows(*chip, c), (x, y, c)).wait_recv()
            p2 = rcopy(9 + k, out_rows(*chip, c), out_rows(*chip, c), sibling)
            p2.start()
            passed += [p1, p2]
        for k, chip in enumerate(chips):
            rcopy(6 + k, in_rows(*chip, 1 - c), in_rows(*chip, 1 - c), (x, y, c)).wait_recv()
            rcopy(9 + k, out_rows(*chip, 1 - c), out_rows(*chip, 1 - c), (x, y, c)).wait_recv()
        for cp in first + passed:
            cp.wait_send()
        loc_in.wait()
        loc_out.wait()

    return pl.pallas_call(
        body, name="weight_gather",
        out_shape=(jax.ShapeDtypeStruct((D, NCOL), BF16), jax.ShapeDtypeStruct((D, D), BF16)),
        in_specs=[pl.BlockSpec(memory_space=pltpu.VMEM), pl.BlockSpec(memory_space=pltpu.VMEM)],
        out_specs=(pl.BlockSpec(memory_space=pltpu.HBM), pl.BlockSpec(memory_space=pltpu.HBM)),
        scratch_shapes=[pltpu.VMEM((D, 1024), BF16), pltpu.VMEM((256, D), BF16),
                        pltpu.SemaphoreType.DMA((12,)), pltpu.SemaphoreType.DMA((12,)),
                        pltpu.SemaphoreType.DMA((2,))],
        compiler_params=_cp(),
    )(w_in, w_out)


def _fwd_in(x, pos, mixw, w_full):
    TT = 512

    def body(x_ref, pos_ref, mw_ref, w_ref, hn_ref, q1, k1, v1, q4, k4, v4, q16, k16, v16,
             ag, hq, hf, hi, hg, scr):
        xv = x_ref[...]
        r = lax.rsqrt(jnp.mean(xv * xv, axis=-1, keepdims=True) + EPS)
        hn = ((xv * r) * mw_ref[...]).astype(BF16)
        hn_ref[...] = hn
        cosf, s1, s2 = _rope_tables(pos_ref[...])

        def proj(g):
            return _mm(hn, w_ref[:, g * 512:(g + 1) * 512])

        def emit(val, o1, o4, o16):
            _perm_store(val, scr, o1, o4, o16, BF16)

        emit(_rope(proj(0), cosf, s1, s2), q1, q4, q16)
        emit(_rope(proj(1), cosf, s1, s2), k1, k4, k16)
        emit(proj(2), v1, v4, v16)
        ag[...] = proj(3)
        hq[...] = proj(4)
        hf[...] = proj(5)
        hi[...] = proj(6).astype(BF16)
        hg[...] = proj(7)

    tok = lambda w: pl.BlockSpec((TT, w), lambda i: (i, 0))
    d4 = pl.BlockSpec((4, TT // 4, AW), lambda i: (0, i, 0))
    d16 = pl.BlockSpec((16, TT // 16, AW), lambda i: (0, i, 0))
    sd = lambda shape, dt: jax.ShapeDtypeStruct(shape, dt)
    return pl.pallas_call(
        body, name="fwd_in", grid=(T // TT,),
        in_specs=[tok(D), tok(1), pl.BlockSpec((1, D), lambda i: (0, 0)),
                  pl.BlockSpec((D, NCOL), lambda i: (0, 0))],
        out_specs=[tok(D)] + [tok(AW)] * 3 + [d4] * 3 + [d16] * 3 + [tok(AW)] * 5,
        out_shape=[sd((T, D), BF16)] + [sd((T, AW), BF16)] * 3 + [sd((4, T // 4, AW), BF16)] * 3
        + [sd((16, T // 16, AW), BF16)] * 3
        + [sd((T, AW), F32), sd((T, AW), F32), sd((T, AW), F32), sd((T, AW), BF16), sd((T, AW), F32)],
        scratch_shapes=[pltpu.VMEM((4, TT, 128), F32)],
        compiler_params=_cp(("parallel",)),
    )(x, pos, mixw, w_full)


def _band_masks():
    qi = lax.broadcasted_iota(jnp.int32, (BLK, 2 * BLK), 0)
    kj = lax.broadcasted_iota(jnp.int32, (BLK, 2 * BLK), 1)
    band = (kj >= qi) & (kj <= qi + BLK)
    return band, kj


def _attn_fwd(q, k, v, nb, name):
    CH = 2 * BLK

    def body(q_ref, k_ref, v_ref, kp_ref, vp_ref, o_ref, lse_ref):
        i = pl.program_id(0)
        thr0 = jnp.where((2 * i) % nb == 0, BLK, 0)
        lane = lax.broadcasted_iota(jnp.int32, (1, 128), 1)
        in_a = [lane < HEAD, lane >= HEAD]
        band, kj = _band_masks()
        mask0 = band & (kj >= thr0)
        for hp in range(4):
            cs = slice(hp * 128, (hp + 1) * 128)
            for b in range(2):
                rs = slice(b * BLK, (b + 1) * BLK)
                q2 = q_ref[rs, cs]
                if b == 0:
                    kk = jnp.concatenate([kp_ref[:, cs], k_ref[rs, cs]], axis=0)
                    vv = jnp.concatenate([vp_ref[:, cs], v_ref[rs, cs]], axis=0)
                    mask = mask0
                else:
                    kk = k_ref[:, cs]
                    vv = v_ref[:, cs]
                    mask = band
                res = []
                for a in range(2):
                    qa = jnp.where(in_a[a], q2, jnp.zeros_like(q2))
                    s = jnp.where(mask, _mm_nt(qa, kk) * SCALE, NEG)
                    m = jnp.max(s, axis=-1, keepdims=True)
                    p = jnp.exp(s - m)
                    l = jnp.sum(p, axis=-1, keepdims=True)
                    o = _mm(p.astype(BF16), vv)
                    res.append((o / l, m + jnp.log(l)))
                o_ref[rs, cs] = jnp.where(in_a[0], res[0][0], res[1][0])
                lse_ref[rs, cs] = jnp.where(in_a[0], res[0][1], res[1][1])

    cur = pl.BlockSpec((CH, AW), lambda i: (i, 0))
    prev = pl.BlockSpec((BLK, AW), lambda i: (jnp.maximum(2 * i - 1, 0), 0))
    return pl.pallas_call(
        body, name=name, grid=(T // CH,),
        in_specs=[cur, cur, cur, prev, prev],
        out_specs=[cur, cur],
        out_shape=[jax.ShapeDtypeStruct((T, AW), F32)] * 2,
        compiler_params=_cp(("parallel",)),
    )(q, k, v, k, v)


def _attn_bwd(q, k, v, do, lse, dl, nb, name):
    CH = 2 * BLK
    NBLK = T // BLK

    def body(q_ref, k_ref, v_ref, do_ref, lse_ref, dl_ref, kp_ref, vp_ref,
             qn_ref, don_ref, lsen_ref, dln_ref, dq_ref, dk_ref, dv_ref):
        i = pl.program_id(0)
        thr0 = jnp.where((2 * i) % nb == 0, BLK, 0)
        nxt_thr = jnp.where((2 * i + 2) % nb == 0, 2 * BLK, 0)
        lane = lax.broadcasted_iota(jnp.int32, (1, 128), 1)
        in_a = [lane < HEAD, lane >= HEAD]
        band, kj = _band_masks()
        mask0 = band & (kj >= thr0)
        qi1 = lax.broadcasted_iota(jnp.int32, (BLK, BLK), 0)
        kj1 = lax.broadcasted_iota(jnp.int32, (BLK, BLK), 1)
        mask_next = kj1 >= qi1 + nxt_thr

        def stat(x2, a):
            xr = pltpu.roll(x2, HEAD, 1)
            return jnp.where(in_a[0], x2, xr) if a == 0 else jnp.where(in_a[0], xr, x2)

        for hp in range(4):
            cs = slice(hp * 128, (hp + 1) * 128)
            kb = [kp_ref[:, cs], k_ref[0:BLK, cs], k_ref[BLK:CH, cs]]
            vb = [vp_ref[:, cs], v_ref[0:BLK, cs], v_ref[BLK:CH, cs]]
            dk_acc = [jnp.zeros((BLK, 128), F32), jnp.zeros((BLK, 128), F32)]
            dv_acc = [jnp.zeros((BLK, 128), F32), jnp.zeros((BLK, 128), F32)]
            for b in range(2):
                rs = slice(b * BLK, (b + 1) * BLK)
                q2, do2, lse2, dl2 = q_ref[rs, cs], do_ref[rs, cs], lse_ref[rs, cs], dl_ref[rs, cs]
                kk = jnp.concatenate([kb[b], kb[b + 1]], axis=0)
                vv = jnp.concatenate([vb[b], vb[b + 1]], axis=0)
                mask = mask0 if b == 0 else band
                dq_parts = []
                for a in range(2):
                    qa = jnp.where(in_a[a], q2, jnp.zeros_like(q2))
                    doa = jnp.where(in_a[a], do2, jnp.zeros_like(do2))
                    lse_a = stat(lse2, a)
                    dl_a = stat(dl2, a)
                    lse_w = jnp.concatenate([lse_a, lse_a], axis=1)
                    dl_w = jnp.concatenate([dl_a, dl_a], axis=1)
                    s = jnp.where(mask, _mm_nt(qa, kk) * SCALE, NEG)
                    p = jnp.exp(s - lse_w)
                    dp = _mm_nt(doa, vv)
                    ds = (p * (dp - dl_w)).astype(BF16)
                    pb = p.astype(BF16)
                    dq_parts.append(_mm(ds, kk) * SCALE)
                    if b == 0:
                        dk_acc[0] += _mm_tn(ds[:, BLK:], qa) * SCALE
                        dv_acc[0] += _mm_tn(pb[:, BLK:], doa)
                    else:
                        dkk = _mm_tn(ds, qa) * SCALE
                        dvv = _mm_tn(pb, doa)
                        dk_acc[0] += dkk[:BLK]
                        dk_acc[1] += dkk[BLK:]
                        dv_acc[0] += dvv[:BLK]
                        dv_acc[1] += dvv[BLK:]
                dq_ref[rs, cs] = jnp.where(in_a[0], dq_parts[0], dq_parts[1])
            q2, do2, lse2, dl2 = qn_ref[:, cs], don_ref[:, cs], lsen_ref[:, cs], dln_ref[:, cs]
            for a in range(2):
                qa = jnp.where(in_a[a], q2, jnp.zeros_like(q2))
                doa = jnp.where(in_a[a], do2, jnp.zeros_like(do2))
                s = jnp.where(mask_next, _mm_nt(qa, kb[2]) * SCALE, NEG)
                p = jnp.exp(s - stat(lse2, a))
                dp = _mm_nt(doa, vb[2])
                ds = (p * (dp - stat(dl2, a))).astype(BF16)
                dk_acc[1] += _mm_tn(ds, qa) * SCALE
                dv_acc[1] += _mm_tn(p.astype(BF16), doa)
            dk_ref[0:BLK, cs] = dk_acc[0]
            dk_ref[BLK:CH, cs] = dk_acc[1]
            dv_ref[0:BLK, cs] = dv_acc[0]
            dv_ref[BLK:CH, cs] = dv_acc[1]

    cur = pl.BlockSpec((CH, AW), lambda i: (i, 0))
    prev = pl.BlockSpec((BLK, AW), lambda i: (jnp.maximum(2 * i - 1, 0), 0))
    nxt = pl.BlockSpec((BLK, AW), lambda i: (jnp.minimum(2 * i + 2, NBLK - 1), 0))
    return pl.pallas_call(
        body, name=name, grid=(T // CH,),
        in_specs=[cur] * 6 + [prev] * 2 + [nxt] * 4,
        out_specs=[cur] * 3,
        out_shape=[jax.ShapeDtypeStruct((T, AW), F32)] * 3,
        compiler_params=_cp(("parallel",)),
    )(q, k, v, do, lse, dl, k, v, q, do, lse, dl)


TH = 256
NCH = TH // CHUNK


def _hgrn_common(hq_ref, hf_ref, lbr_ref, tri_ref):
    r0 = lbr_ref[0:1, :]
    r1 = lbr_ref[1:2, :]
    mx = jnp.maximum(r0, r1)
    e0 = jnp.exp(r0 - mx)
    e1 = jnp.exp(r1 - mx)
    lb = e0 / (e0 + e1)
    hqv = hq_ref[...]
    sq = _sigmoid(hqv)
    qv = hqv * sq
    sf = _sigmoid(hf_ref[...])
    f = lb + (1.0 - lb) * sf
    kv = 1.0 - f
    g = jnp.log(f)
    cum = _mm_exact_l(tri_ref[...], g)
    lastb = jnp.concatenate(
        [jnp.broadcast_to(cum[c * CHUNK + CHUNK - 1:(c + 1) * CHUNK, :], (CHUNK, HW)) for c in range(NCH)], axis=0)
    ea = jnp.exp(cum)
    ena = jnp.exp(-cum)
    eend = jnp.exp(lastb - cum)
    return dict(lb=lb, hq=hqv, sq=sq, q=qv, sf=sf, f=f, k=kv, cum=cum, lastb=lastb, ea=ea, ena=ena, eend=eend,
                qd=qv * ea, ki=kv * ena, ke=kv * eend, dec=jnp.exp(lastb))


def _tri_mask():
    ti = lax.broadcasted_iota(jnp.int32, (CHUNK, CHUNK), 0)
    si = lax.broadcasted_iota(jnp.int32, (CHUNK, CHUNK), 1)
    return si <= ti


def _hgrn_fwd(hq, hf, hi, lbr, tri):
    def body(hq_ref, hf_ref, hi_ref, lbr_ref, tri_ref, rec_ref, sall_ref, st_scr):
        @pl.when(pl.program_id(0) == 0)
        def _():
            st_scr[...] = jnp.zeros_like(st_scr)

        w = _hgrn_common(hq_ref, hf_ref, lbr_ref, tri_ref)
        qd, ki, ke = w["qd"].astype(BF16), w["ki"].astype(BF16), w["ke"].astype(BF16)
        dec = w["dec"]
        vb = hi_ref[...]
        causal = _tri_mask()
        for c in range(NCH):
            rs = slice(c * CHUNK, (c + 1) * CHUNK)
            sall_ref[c] = st_scr[...]
            for h in range(4):
                cs = slice(h * 128, (h + 1) * 128)
                st = st_scr[:, cs]
                att = jnp.where(causal, _mm_nt(qd[rs, cs], ki[rs, cs]), 0.0)
                o = _mm(att.astype(BF16), vb[rs, cs]) + _mm_nt(qd[rs, cs], st.astype(BF16))
                rec_ref[rs, cs] = o
                st_scr[:, cs] = dec[c * CHUNK:c * CHUNK + 1, cs] * st + _mm_tn(vb[rs, cs], ke[rs, cs])

    tok = pl.BlockSpec((TH, HW), lambda i: (i, 0))
    return pl.pallas_call(
        body, name="hgrn_fwd", grid=(T // TH,),
        in_specs=[tok, tok, tok, pl.BlockSpec((2, HW), lambda i: (0, 0)), pl.BlockSpec((TH, TH), lambda i: (0, 0))],
        out_specs=[tok, pl.BlockSpec((NCH, 128, HW), lambda i: (i, 0, 0))],
        out_shape=[jax.ShapeDtypeStruct((T, HW), F32), jax.ShapeDtypeStruct((T // CHUNK, 128, HW), F32)],
        scratch_shapes=[pltpu.VMEM((128, HW), F32)],
        compiler_params=_cp(("arbitrary",)),
    )(hq, hf, hi, lbr, tri)


def _hgrn_bwd(hq, hf, hi, lbr, tri, trit, drec, sall):
    NT = T // TH

    def body(hq_ref, hf_ref, hi_ref, lbr_ref, tri_ref, trit_ref, do_ref, sall_ref,
             dhq_ref, dhf_ref, dhi_ref, small_ref, dst_scr, dlb_scr, dqd_scr, dki_scr, dke_scr, dlast_scr):
        step = pl.program_id(0)

        @pl.when(step == 0)
        def _():
            dst_scr[...] = jnp.zeros_like(dst_scr)
            dlb_scr[...] = jnp.zeros_like(dlb_scr)

        w = _hgrn_common(hq_ref, hf_ref, lbr_ref, tri_ref)
        qd, ki, ke = w["qd"].astype(BF16), w["ki"].astype(BF16), w["ke"].astype(BF16)
        dec = w["dec"]
        vb = hi_ref[...]
        dob = do_ref[...].astype(BF16)
        causal = _tri_mask()
        for c in reversed(range(NCH)):
            rs = slice(c * CHUNK, (c + 1) * CHUNK)
            dec_c = dec[c * CHUNK:c * CHUNK + 1, :]
            for h in range(4):
                cs = slice(h * 128, (h + 1) * 128)
                st = sall_ref[c, :, cs]
                dst = dst_scr[:, cs]
                dstb = dst.astype(BF16)
                att = jnp.where(causal, _mm_nt(qd[rs, cs], ki[rs, cs]), 0.0).astype(BF16)
                datt = jnp.where(causal, _mm_nt(dob[rs, cs], vb[rs, cs]), 0.0).astype(BF16)
                dhi_ref[rs, cs] = (_mm_tn(att, dob[rs, cs]) + _mm_nt(ke[rs, cs], dstb)).astype(BF16)
                dqd_scr[rs, cs] = _mm(datt, ki[rs, cs]) + _mm(dob[rs, cs], st.astype(BF16))
                dki_scr[rs, cs] = _mm_tn(datt, qd[rs, cs])
                dke_scr[rs, cs] = _mm(vb[rs, cs], dstb)
                ddec = jnp.sum(dst * st, axis=0, keepdims=True)
                dlast_scr[c:c + 1, cs] = ddec * dec_c[:, cs]
                dst_scr[:, cs] = dec_c[:, cs] * dst + _mm_tn(dob[rs, cs], qd[rs, cs])
        dqd, dki, dke = dqd_scr[...], dki_scr[...], dke_scr[...]
        dq = dqd * w["ea"]
        dk = dki * w["ena"] + dke * w["eend"]
        dcum = dqd * w["qd"] - dki * w["ki"] - dke * w["ke"]
        dkeke = dke * w["ke"]
        dlastb = jnp.concatenate(
            [jnp.broadcast_to(dlast_scr[c:c + 1, :] + jnp.sum(dkeke[c * CHUNK:(c + 1) * CHUNK], axis=0, keepdims=True),
                              (CHUNK, HW)) for c in range(NCH)], axis=0)
        dg = _mm_exact_l(trit_ref[...], dcum) + dlastb
        df = dg / w["f"] - dk
        lb, sf, sq = w["lb"], w["sf"], w["sq"]
        dhf_ref[...] = (df * (1.0 - lb) * sf * (1.0 - sf)).astype(BF16)
        dhq_ref[...] = (dq * (sq * (1.0 + w["hq"] * (1.0 - sq)))).astype(BF16)
        dlb_scr[...] += jnp.sum(df * (1.0 - sf), axis=0, keepdims=True)

        @pl.when(step == NT - 1)
        def _():
            gr = dlb_scr[...] * lb * (1.0 - lb)
            small_ref[...] = jnp.zeros_like(small_ref)
            small_ref[0:1, 0:HW] = gr
            small_ref[1:2, 0:HW] = -gr

    tok = pl.BlockSpec((TH, HW), lambda i: (NT - 1 - i, 0))
    const = lambda shape: pl.BlockSpec(shape, lambda i: (0,) * len(shape))
    return pl.pallas_call(
        body, name="hgrn_bwd", grid=(NT,),
        in_specs=[tok, tok, tok, const((2, HW)), const((TH, TH)), const((TH, TH)), tok,
                  pl.BlockSpec((NCH, 128, HW), lambda i: (NT - 1 - i, 0, 0))],
        out_specs=[tok, tok, tok, const((8, D))],
        out_shape=[jax.ShapeDtypeStruct((T, HW), BF16)] * 3 + [jax.ShapeDtypeStruct((8, D), F32)],
        scratch_shapes=[pltpu.VMEM((128, HW), F32), pltpu.VMEM((1, HW), F32), pltpu.VMEM((TH, HW), F32),
                        pltpu.VMEM((TH, HW), F32), pltpu.VMEM((TH, HW), F32), pltpu.VMEM((8, HW), F32)],
        compiler_params=_cp(("arbitrary",)),
    )(hq, hf, hi, lbr, tri, trit, drec, sall)


def _fwd_out(o1, o4, o16, l1, l4, l16, rec, ag, hg, x, tgt, anw, hnw, fnw, wout_full, gmat):
    TT = 256

    def body(o1_r, o4_r, o16_r, l1_r, l4_r, l16_r, rec_r, ag_r, hg_r, x_r, tgt_r, anw_r, hnw_r, fnw_r, wo_r, g_r,
             dx2_o, do1_o, do4_o, do16_o, ls1_o, ls4_o, ls16_o, dl1_o, dl4_o, dl16_o, drec_o, dag_o, dhg_o,
             rout_o, small_o, scr_a, scr_b, gwout_o, rbuf, send_sems, recv_sems):
        @pl.when(pl.program_id(0) == 0)
        def _():
            gwout_o[...] = jnp.zeros_like(gwout_o)
            small_o[...] = jnp.zeros_like(small_o)

        def unperm(r4, r16):
            return _unperm_load(r4, r16, scr_a, scr_b)

        def perm_out(val, p1, p4, p16, dt):
            _perm_store(val, scr_a, p1, p4, p16, dt)

        o4u, o16u = unperm(o4_r, o16_r)
        l4u, l16u = unperm(l4_r, l16_r)
        o1v, l1v = o1_r[...], l1_r[...]
        mx = jnp.maximum(jnp.maximum(l1v, l4u), l16u)
        w1, w4, w16 = jnp.exp(l1v - mx), jnp.exp(l4u - mx), jnp.exp(l16u - mx)
        den = w1 + w4 + w16
        attn = (w1 * o1v + w4 * o4u + w16 * o16u) / den
        lse = mx + jnp.log(den)
        gm = g_r[...]

        def head_mean_a(t):
            return _mm_exact_r(t, gm)

        def head_mean_h(t):
            return jnp.concatenate(
                [jnp.broadcast_to(jnp.mean(t[:, h * 128:(h + 1) * 128], axis=-1, keepdims=True), (TT, 128))
                 for h in range(4)], axis=1)

        rs_a = lax.rsqrt(head_mean_a(attn * attn) + EPS)
        n_a = attn * rs_a
        agv = ag_r[...]
        sg_a = _sigmoid(agv)
        si_a = agv * sg_a
        anw_v = anw_r[...]
        y_a = (n_a * anw_v) * si_a
        recv = rec_r[...]
        rs_h = lax.rsqrt(head_mean_h(recv * recv) + EPS)
        n_h = recv * rs_h
        hgv = hg_r[...]
        sg_h = _sigmoid(hgv)
        si_h = hgv * sg_h
        hnw_v = hnw_r[...]
        y_h = (n_h * hnw_v) * si_h
        mixed = jnp.concatenate([y_a, y_h], axis=1).astype(BF16)
        xv = x_r[...]
        x2 = xv + _mm(mixed, wo_r[...])
        r2 = lax.rsqrt(jnp.mean(x2 * x2, axis=-1, keepdims=True) + EPS)
        fnw_v = fnw_r[...]
        xn = x2 * r2
        err = xn * fnw_v - tgt_r[...]
        small_o[2:3, :] += 0.5 * jnp.sum(jnp.mean(err * err, axis=-1, keepdims=True), axis=0, keepdims=True)
        dy = err * (1.0 / D)
        small_o[0:1, :] += jnp.sum(dy * xn, axis=0, keepdims=True)
        dyw = dy * fnw_v
        dx2 = r2 * dyw - x2 * ((r2 * r2 * r2) * jnp.mean(dyw * x2, axis=-1, keepdims=True))
        dx2_o[...] = dx2
        dx2b = dx2.astype(BF16)
        gwout_o[...] += _mm_tn(mixed, dx2b)
        dmix = _mm_nt(dx2b, wo_r[...])
        dm_a, dm_h = dmix[:, :AW], dmix[:, AW:]
        dag_o[...] = (dm_a * (n_a * anw_v) * (sg_a * (1.0 + agv * (1.0 - sg_a)))).astype(BF16)
        dn_a = dm_a * anw_v * si_a
        small_o[1:2, 0:AW] += jnp.sum(dm_a * n_a * si_a, axis=0, keepdims=True)
        dattn = rs_a * (dn_a - n_a * head_mean_a(dn_a * n_a))
        delta = head_mean_a(dattn * attn) * float(HEAD)
        perm_out(dattn, do1_o, do4_o, do16_o, BF16)
        perm_out(lse, ls1_o, ls4_o, ls16_o, F32)
        perm_out(delta, dl1_o, dl4_o, dl16_o, F32)
        dhg_o[...] = (dm_h * (n_h * hnw_v) * (sg_h * (1.0 + hgv * (1.0 - sg_h)))).astype(BF16)
        dn_h = dm_h * hnw_v * si_h
        small_o[1:2, AW:] += jnp.sum(dm_h * n_h * si_h, axis=0, keepdims=True)
        drec_o[...] = rs_h * (dn_h - n_h * head_mean_h(dn_h * n_h))

        @pl.when(pl.program_id(0) == T // TT - 1)
        def _():
            x, y, c = lax.axis_index("x"), lax.axis_index("y"), lax.axis_index("c")
            cps = [pltpu.make_async_remote_copy(
                src_ref=gwout_o.at[pl.ds(pl.multiple_of(j * 256 + (1 - c) * 128, 128), 128), :], dst_ref=rbuf.at[j],
                send_sem=send_sems.at[j], recv_sem=recv_sems.at[j], device_id=(x, y, 1 - c), device_id_type=MESH)
                for j in range(4)]
            for cp in cps:
                cp.start()
            for j, cp in enumerate(cps):
                cp.wait_recv()
                rout_o[j * 128:(j + 1) * 128, :] = gwout_o[pl.ds(pl.multiple_of(j * 256 + c * 128, 128), 128), :] + rbuf[j]
            for cp in cps:
                cp.wait_send()

    tok = lambda w: pl.BlockSpec((TT, w), lambda i: (i, 0))
    d4 = pl.BlockSpec((4, TT // 4, AW), lambda i: (0, i, 0))
    d16 = pl.BlockSpec((16, TT // 16, AW), lambda i: (0, i, 0))
    const = lambda shape: pl.BlockSpec(shape, lambda i: (0,) * len(shape))
    sd = lambda shape, dt: jax.ShapeDtypeStruct(shape, dt)
    p3 = lambda dt: [sd((T, AW), dt), sd((4, T // 4, AW), dt), sd((16, T // 16, AW), dt)]
    return pl.pallas_call(
        body, name="fwd_out", grid=(T // TT,),
        in_specs=[tok(AW), d4, d16, tok(AW), d4, d16, tok(AW), tok(AW), tok(AW), tok(D), tok(D),
                  const((1, AW)), const((1, HW)), const((1, D)), const((D, D)), const((AW, AW))],
        out_specs=[tok(D)] + [tok(AW), d4, d16] * 3 + [tok(AW)] * 3 + [const((512, D)), const((8, D))],
        out_shape=[sd((T, D), F32)] + p3(BF16) + p3(F32) + p3(F32)
        + [sd((T, AW), F32), sd((T, AW), BF16), sd((T, AW), BF16), sd((512, D), F32), sd((8, D), F32)],
        scratch_shapes=[pltpu.VMEM((4, TT, 128), F32), pltpu.VMEM((4, TT, 128), F32), pltpu.VMEM((D, D), F32),
                        pltpu.VMEM((4, 128, D), F32), pltpu.SemaphoreType.DMA((4,)), pltpu.SemaphoreType.DMA((4,))],
        compiler_params=_cp(("arbitrary",)),
    )(o1, o4, o16, l1, l4, l16, rec, ag, hg, x, tgt, anw, hnw, fnw, wout_full, gmat)


def _dproj_build(dq, dk, dv, dag, dhq, dhf, dhi, dhg, pos):
    TT = 256

    def body(dq1, dq4, dq16, dk1, dk4, dk16, dv1, dv4, dv16, dag_r, dhq_r, dhf_r, dhi_r, dhg_r,
             pos_r, dproj_o, scr_a, scr_b):
        def unperm_sum(r1, r4, r16):
            u4, u16 = _unperm_load(r4, r16, scr_a, scr_b)
            return r1[...] + u4 + u16

        cosf, s1, s2 = _rope_tables(pos_r[...])
        dproj_o[:, 0:512] = _rope_bwd(unperm_sum(dq1, dq4, dq16), cosf, s1, s2).astype(BF16)
        dproj_o[:, 512:1024] = _rope_bwd(unperm_sum(dk1, dk4, dk16), cosf, s1, s2).astype(BF16)
        dproj_o[:, 1024:1536] = unperm_sum(dv1, dv4, dv16).astype(BF16)
        dproj_o[:, 1536:2048] = dag_r[...]
        dproj_o[:, 2048:2560] = dhq_r[...]
        dproj_o[:, 2560:3072] = dhf_r[...]
        dproj_o[:, 3072:3584] = dhi_r[...]
        dproj_o[:, 3584:4096] = dhg_r[...]

    tok = lambda w: pl.BlockSpec((TT, w), lambda i: (i, 0))
    d4 = pl.BlockSpec((4, TT // 4, AW), lambda i: (0, i, 0))
    d16 = pl.BlockSpec((16, TT // 16, AW), lambda i: (0, i, 0))
    return pl.pallas_call(
        body, name="dproj_build", grid=(T // TT,),
        in_specs=[tok(AW), d4, d16] * 3 + [tok(AW)] * 5 + [tok(1)],
        out_specs=tok(NCOL),
        out_shape=jax.ShapeDtypeStruct((T, NCOL), BF16),
        scratch_shapes=[pltpu.VMEM((4, TT, 128), F32), pltpu.VMEM((4, TT, 128), F32)],
        compiler_params=_cp(("parallel",)),
    )(*dq, *dk, *dv, dag, dhq, dhf, dhi, dhg, pos)


def _bwd_x(dproj, x, dx2, mixw, w_full, rin, rout, small4, small6):
    TT = 256
    NT = T // TT

    def body(dp_r, x_r, dx2_r, mw_r, w_r, rin_r, rout_r, s4_r, s6_r, gx_o, pin_o, pout_o, sall_o,
             sbuf, send_sems, recv_sems, loc_sems):
        i = pl.program_id(0)
        loc, rem = _chip_copies(rin_r, rout_r, pin_o, pout_o, send_sems, recv_sems, loc_sems)

        @pl.when(i == 0)
        def _():
            sbuf[...] = jnp.zeros_like(sbuf)
            for cp in loc + rem:
                cp.start()

        dhn = _mm_nt(dp_r[...], w_r[...])
        xv = x_r[...]
        r = lax.rsqrt(jnp.mean(xv * xv, axis=-1, keepdims=True) + EPS)
        dxw = dhn * mw_r[...]
        gx_o[...] = dx2_r[...] + r * dxw - xv * ((r * r * r) * jnp.mean(dxw * xv, axis=-1, keepdims=True))
        sbuf[16:17, :] += jnp.sum(dhn * (xv * r), axis=0, keepdims=True)

        @pl.when(i == NT - 1)
        def _():
            sbuf[0:8, :] = s4_r[...]
            sbuf[8:16, :] = s6_r[...]
            sloc, srem = _small_copies(sbuf, sall_o, send_sems, recv_sems, loc_sems)
            for cp in sloc + srem:
                cp.start()
            for cp in rem + srem:
                cp.wait_recv()
            for cp in rem + srem:
                cp.wait_send()
            for cp in loc + sloc:
                cp.wait()

    tok = lambda w: pl.BlockSpec((TT, w), lambda i: (i, 0))
    const = lambda shape: pl.BlockSpec(shape, lambda i: (0,) * len(shape))
    hbm = pl.BlockSpec(memory_space=pltpu.HBM)
    return pl.pallas_call(
        body, name="bwd_x", grid=(NT,),
        in_specs=[tok(NCOL), tok(D), tok(D), const((1, D)), const((D, NCOL)), hbm, hbm, const((8, D)), const((8, D))],
        out_specs=[tok(D), hbm, hbm, hbm],
        out_shape=[jax.ShapeDtypeStruct((T, D), F32), jax.ShapeDtypeStruct((4, 512, 1024), F32),
                   jax.ShapeDtypeStruct((4, 128, D), F32), jax.ShapeDtypeStruct((8, 24, D), F32)],
        scratch_shapes=[pltpu.VMEM((24, D), F32), pltpu.SemaphoreType.DMA((13,)), pltpu.SemaphoreType.DMA((13,)),
                        pltpu.SemaphoreType.DMA((3,))],
        compiler_params=_cp(("arbitrary",)),
    )(dproj, x, dx2, mixw, w_full, rin, rout, small4, small6)


def _grad_w_in(hn, dproj):
    TK = 512
    NK = T // TK

    def body(hn_r, dp_r, rin_o, acc, rbuf, obuf, send_sems, recv_sems, wb_sems):
        j = pl.program_id(0)
        kk = pl.program_id(1)
        x, y, c = lax.axis_index("x"), lax.axis_index("y"), lax.axis_index("c")
        mine = pl.ds(pl.multiple_of(c * 512, 512), 512)
        theirs = pl.ds(pl.multiple_of((1 - c) * 512, 512), 512)

        def send(jj):
            return pltpu.make_async_remote_copy(
                src_ref=acc.at[jj % 2, theirs, :], dst_ref=rbuf.at[jj], send_sem=send_sems.at[jj],
                recv_sem=recv_sems.at[jj], device_id=(x, y, 1 - c), device_id_type=MESH)

        def writeback(jj):
            return pltpu.make_async_copy(obuf.at[jj % 2], rin_o.at[:, pl.ds(jj * 1024, 1024)], wb_sems.at[jj % 2])

        def finalize(jj):
            send(jj).wait_recv()
            obuf[jj % 2] = acc[jj % 2, mine, :] + rbuf[jj]
            writeback(jj).start()

        prod = _mm_tn(hn_r[...], dp_r[...])

        @pl.when(kk == 0)
        def _():
            for jj in (2, 3):
                @pl.when(j == jj)
                def _():
                    send(jj - 2).wait_send()
            acc[j % 2] = prod

        @pl.when(kk > 0)
        def _():
            acc[j % 2] += prod

        @pl.when(kk == NK - 1)
        def _():
            for jj in range(4):
                @pl.when(j == jj)
                def _():
                    send(jj).start()
                    if jj in (1, 2):
                        finalize(jj - 1)
                    if jj == 3:
                        writeback(0).wait()
                        finalize(2)
                        writeback(1).wait()
                        finalize(3)
                        writeback(2).wait()
                        writeback(3).wait()
                        send(2).wait_send()
                        send(3).wait_send()

    return pl.pallas_call(
        body, name="grad_w_in", grid=(4, NK),
        in_specs=[pl.BlockSpec((TK, D), lambda j, kk: (kk, 0)), pl.BlockSpec((TK, 1024), lambda j, kk: (kk, j))],
        out_specs=pl.BlockSpec(memory_space=pltpu.HBM),
        out_shape=jax.ShapeDtypeStruct((512, NCOL), F32),
        scratch_shapes=[pltpu.VMEM((2, D, 1024), F32), pltpu.VMEM((4, 512, 1024), F32), pltpu.VMEM((2, 512, 1024), F32),
                        pltpu.SemaphoreType.DMA((4,)), pltpu.SemaphoreType.DMA((4,)), pltpu.SemaphoreType.DMA((2,))],
        compiler_params=_cp(("arbitrary", "arbitrary")),
    )(hn, dproj)


def _add_slots(a, name):
    n, rows, cols = a.shape
    tr = min(rows, 256)
    tc = min(cols, 1024)

    def body(a_r, o_r):
        acc = a_r[0]
        for s in range(1, n):
            acc = acc + a_r[s]
        o_r[...] = acc

    return pl.pallas_call(
        body, name=name, grid=(rows // tr, cols // tc),
        in_specs=[pl.BlockSpec((n, tr, tc), lambda i, j: (0, i, j))],
        out_specs=pl.BlockSpec((tr, tc), lambda i, j: (i, j)),
        out_shape=jax.ShapeDtypeStruct((rows, cols), F32),
        compiler_params=_cp(("parallel", "parallel")),
    )(a)


def _chip_copies(rin_r, rout_r, pin_o, pout_o, send_sems, recv_sems, loc_sems):
    x, y, c = lax.axis_index("x"), lax.axis_index("y"), lax.axis_index("c")
    chips = [(1 - x, y), (x, 1 - y), (1 - x, 1 - y)]
    jm = 2 * x + y
    loc = [pltpu.make_async_copy(rin_r.at[:, pl.ds(jm * 1024, 1024)], pin_o.at[0], loc_sems.at[0]),
           pltpu.make_async_copy(rout_r.at[pl.ds(jm * 128, 128), :], pout_o.at[0], loc_sems.at[1])]
    rem = []
    for k, (px, py) in enumerate(chips):
        j = 2 * px + py
        rem.append(pltpu.make_async_remote_copy(
            src_ref=rin_r.at[:, pl.ds(j * 1024, 1024)], dst_ref=pin_o.at[k + 1],
            send_sem=send_sems.at[k], recv_sem=recv_sems.at[k], device_id=(px, py, c), device_id_type=MESH))
        rem.append(pltpu.make_async_remote_copy(
            src_ref=rout_r.at[pl.ds(j * 128, 128), :], dst_ref=pout_o.at[k + 1],
            send_sem=send_sems.at[3 + k], recv_sem=recv_sems.at[3 + k], device_id=(px, py, c),
            device_id_type=MESH))
    return loc, rem


def _small_copies(small_r, sall_o, send_sems, recv_sems, loc_sems):
    x, y, c = lax.axis_index("x"), lax.axis_index("y"), lax.axis_index("c")
    me = 4 * x + 2 * y + c
    loc = [pltpu.make_async_copy(small_r, sall_o.at[me], loc_sems.at[2])]
    rem = []
    k = 6
    for fx in range(2):
        for fy in range(2):
            for fc in range(2):
                if fx or fy or fc:
                    peer = (1 - x if fx else x, 1 - y if fy else y, 1 - c if fc else c)
                    rem.append(pltpu.make_async_remote_copy(
                        src_ref=small_r, dst_ref=sall_o.at[me], send_sem=send_sems.at[k],
                        recv_sem=recv_sems.at[k], device_id=peer, device_id_type=MESH))
                    k += 1
    return loc, rem


def _pair_share(pin, pout):
    def body(pin_r, pout_r, fin_o, fout_o, send_sems, recv_sems):
        x, y, c = lax.axis_index("x"), lax.axis_index("y"), lax.axis_index("c")
        sibling = (x, y, 1 - c)
        rem = [pltpu.make_async_remote_copy(src_ref=pin_r, dst_ref=fin_o.at[c], send_sem=send_sems.at[0],
                                            recv_sem=recv_sems.at[0], device_id=sibling, device_id_type=MESH),
               pltpu.make_async_remote_copy(src_ref=pout_r, dst_ref=fout_o.at[c], send_sem=send_sems.at[1],
                                            recv_sem=recv_sems.at[1], device_id=sibling, device_id_type=MESH)]
        for cp in rem:
            cp.start()
        fin_o[c] = pin_r[...]
        fout_o[c] = pout_r[...]
        for cp in rem:
            cp.wait_recv()
        for cp in rem:
            cp.wait_send()

    vm = pl.BlockSpec(memory_space=pltpu.VMEM)
    return pl.pallas_call(
        body, name="pair_share",
        out_shape=(jax.ShapeDtypeStruct((2, 512, 1024), F32), jax.ShapeDtypeStruct((2, 128, D), F32)),
        in_specs=[vm, vm], out_specs=(vm, vm),
        scratch_shapes=[pltpu.SemaphoreType.DMA((2,)), pltpu.SemaphoreType.DMA((2,))],
        compiler_params=_cp(),
    )(pin, pout)


def _adamw_math(w, g, m, v):
    m = B1 * m + (1.0 - B1) * g
    v = B2 * v + (1.0 - B2) * (g * g)
    m_hat = m / (1.0 - B1 ** STEP)
    v_hat = v / (1.0 - B2 ** STEP)
    delta = -LR * (m_hat / (jnp.sqrt(v_hat) + AEPS) + WD * w)
    return delta, m, v


def _adamw(w, g, m, v, name):
    rows, cols = w.shape
    tr = min(rows, 256)

    def body(w_r, g_r, m_r, v_r, d_o, m_o, v_o):
        d, mm, vv = _adamw_math(w_r[...], g_r[...], m_r[...], v_r[...])
        d_o[...] = d
        m_o[...] = mm
        v_o[...] = vv

    blk = pl.BlockSpec((tr, cols), lambda i: (i, 0))
    return pl.pallas_call(
        body, name=name, grid=(rows // tr,),
        in_specs=[blk] * 4, out_specs=[blk] * 3,
        out_shape=[jax.ShapeDtypeStruct((rows, cols), F32)] * 3,
        compiler_params=_cp(("parallel",)),
    )(w, g, m, v)


def _adamw_small(sall, params):
    def body(sall_r, *refs):
        ins, outs = refs[:15], refs[15:]
        tot = sall_r[0]
        for dv in range(1, 8):
            tot = tot + sall_r[dv]
        grads = [tot[16:17, :], tot[1:2, 0:AW], tot[1:2, AW:], tot[8:10, 0:HW], tot[0:1, :]]
        outs[0][...] = tot[2:3, 0:1]
        for p in range(5):
            w_r, m_r, v_r = ins[3 * p:3 * p + 3]
            g = grads[p]
            d, mm, vv = _adamw_math(w_r[...], g, m_r[...], v_r[...])
            outs[1 + 4 * p][...] = g
            outs[2 + 4 * p][...] = d
            outs[3 + 4 * p][...] = mm
            outs[4 + 4 * p][...] = vv

    flat = [a for p in params for a in p]
    shapes = [jax.ShapeDtypeStruct((1, 1), F32)]
    for p in params:
        shapes += [jax.ShapeDtypeStruct(p[0].shape, F32)] * 4
    vm = pl.BlockSpec(memory_space=pltpu.VMEM)
    return pl.pallas_call(
        body, name="adamw_small",
        in_specs=[vm] * 16, out_specs=[vm] * 21, out_shape=shapes,
        compiler_params=_cp(),
    )(sall, *flat)


def kernel(x, positions, w_in, w_out, mix_norm_w, attn_out_norm_w, hgrn_out_norm_w, hgrn_lb_raw, final_norm_w, loss_target, m_w_in, m_w_out, m_mix_norm_w, m_attn_out_norm_w, m_hgrn_out_norm_w, m_hgrn_lb_raw, m_final_norm_w, v_w_in, v_w_out, v_mix_norm_w, v_attn_out_norm_w, v_hgrn_out_norm_w, v_hgrn_lb_raw, v_final_norm_w):
    xs = x.reshape(T, D)
    tgt = loss_target.reshape(T, D)
    pos = positions.reshape(T, 1)
    fnw = final_norm_w.reshape(1, D)

    ti = np.arange(TH)
    tri_np = ((ti[:, None] // CHUNK == ti[None, :] // CHUNK) & (ti[None, :] <= ti[:, None])).astype(np.float32)
    tri = jnp.asarray(tri_np, BF16)
    trit = jnp.asarray(tri_np.T, BF16)
    hi_ = np.arange(AW) // HEAD
    gmat = jnp.asarray((hi_[:, None] == hi_[None, :]).astype(np.float32) / HEAD, BF16)

    w_full, wout_full = _weight_gather(w_in.reshape(D, 1024), w_out.reshape(256, D))

    (hn, q1, k1, v1, q4, k4, v4, q16, k16, v16, ag, hq, hf, hi, hg) = _fwd_in(xs, pos, mix_norm_w, w_full)
    flat = lambda a: a.reshape(T, AW)
    o1, l1 = _attn_fwd(q1, k1, v1, T // BLK, "attn_fwd_d1")
    o4, l4 = _attn_fwd(flat(q4), flat(k4), flat(v4), T // 4 // BLK, "attn_fwd_d4")
    o16, l16 = _attn_fwd(flat(q16), flat(k16), flat(v16), T // 16 // BLK, "attn_fwd_d16")
    rec, sall = _hgrn_fwd(hq, hf, hi, hgrn_lb_raw, tri)

    (dx2, do1, do4, do16, ls1, ls4, ls16, dl1, dl4, dl16, drec, dag, dhg, rout, small4) = _fwd_out(
        o1, o4.reshape(4, T // 4, AW), o16.reshape(16, T // 16, AW),
        l1, l4.reshape(4, T // 4, AW), l16.reshape(16, T // 16, AW),
        rec, ag, hg, xs, tgt, attn_out_norm_w, hgrn_out_norm_w, fnw, wout_full, gmat)

    dq1, dk1, dv1 = _attn_bwd(q1, k1, v1, do1, ls1, dl1, T // BLK, "attn_bwd_d1")
    dq4, dk4, dv4 = _attn_bwd(flat(q4), flat(k4), flat(v4), flat(do4), flat(ls4), flat(dl4), T // 4 // BLK,
                              "attn_bwd_d4")
    dq16, dk16, dv16 = _attn_bwd(flat(q16), flat(k16), flat(v16), flat(do16), flat(ls16), flat(dl16),
                                 T // 16 // BLK, "attn_bwd_d16")
    dhq, dhf, dhi, small6 = _hgrn_bwd(hq, hf, hi, hgrn_lb_raw, tri, trit, drec, sall)

    r4 = lambda a: a.reshape(4, T // 4, AW)
    r16 = lambda a: a.reshape(16, T // 16, AW)
    dproj = _dproj_build((dq1, r4(dq4), r16(dq16)), (dk1, r4(dk4), r16(dk16)), (dv1, r4(dv4), r16(dv16)),
                         dag, dhq, dhf, dhi, dhg, pos)
    rin = _grad_w_in(hn, dproj)
    gx, pin_s, pout_s, small_all = _bwd_x(dproj, xs, dx2, mix_norm_w, w_full, rin, rout, small4, small6)
    pin = _add_slots(pin_s, "chip_sum_in")
    pout = _add_slots(pout_s, "chip_sum_out")
    fin, fout = _pair_share(pin, pout)
    g_w_in = fin.reshape(D, 1024)
    g_w_out = fout.reshape(256, D)

    d_in, nm_in, nv_in = _adamw(w_in.reshape(D, 1024), g_w_in, m_w_in.reshape(D, 1024), v_w_in.reshape(D, 1024),
                                "adamw_w_in")
    d_out, nm_out, nv_out = _adamw(w_out.reshape(256, D), g_w_out, m_w_out.reshape(256, D), v_w_out.reshape(256, D),
                                   "adamw_w_out")
    params = [(mix_norm_w, m_mix_norm_w, v_mix_norm_w),
              (attn_out_norm_w, m_attn_out_norm_w, v_attn_out_norm_w),
              (hgrn_out_norm_w, m_hgrn_out_norm_w, v_hgrn_out_norm_w),
              (hgrn_lb_raw, m_hgrn_lb_raw, v_hgrn_lb_raw),
              (fnw, m_final_norm_w.reshape(1, D), v_final_norm_w.reshape(1, D))]
    so = _adamw_small(small_all, params)
    loss = so[0].reshape(())
    g_s = [so[1 + 4 * p] for p in range(5)]
    d_s = [so[2 + 4 * p] for p in range(5)]
    m_s = [so[3 + 4 * p] for p in range(5)]
    v_s = [so[4 + 4 * p] for p in range(5)]
    for lst in (g_s, d_s, m_s, v_s):
        lst[4] = lst[4].reshape(D)

    return (loss, gx.reshape(1, T, D),
            g_w_in.reshape(1, D, 1024), g_w_out.reshape(1, 256, D), *g_s,
            d_in.reshape(1, D, 1024), d_out.reshape(1, 256, D), *d_s,
            nm_in.reshape(1, D, 1024), nm_out.reshape(1, 256, D), *m_s,
            nv_in.reshape(1, D, 1024), nv_out.reshape(1, 256, D), *v_s)
```

```python
import functools

import numpy as np
import jax
import jax.numpy as jnp
from jax import lax
from jax.experimental import pallas as pl
from jax.experimental.pallas import tpu as pltpu

F32 = jnp.float32
BF16 = jnp.bfloat16

T = 4096
D = 1024
AW = 512
HW = 512
NCOL = 4096
HEAD = 64
BLK = 128
CHUNK = 64
EPS = 1e-6
SCALE = HEAD ** -0.5
NEG = -1e30
ROPE_THETA = 500000.0
INV_FREQ = [float(v) for v in
            (np.float32(ROPE_THETA) ** (-(np.arange(8, dtype=np.float32)) * np.float32(0.125)))]
LR, B1, B2, AEPS, WD, STEP = 0.001, 0.9, 0.999, 1e-08, 0.01, 10
VMEM_LIMIT = 56 * 1024 * 1024
MESH = pl.DeviceIdType.MESH


def _cp(sem=None, **kw):
    return pltpu.CompilerParams(dimension_semantics=sem, vmem_limit_bytes=VMEM_LIMIT, **kw)


def _mm(a, b):
    return jnp.dot(a, b, preferred_element_type=F32)


def _mm_nt(a, b):
    return lax.dot_general(a, b, (((1,), (1,)), ((), ())), preferred_element_type=F32)


def _mm_tn(a, b):
    return lax.dot_general(a, b, (((0,), (0,)), ((), ())), preferred_element_type=F32)


def _split3(x):
    h = x.astype(BF16)
    r = x - h.astype(F32)
    m = r.astype(BF16)
    l = (r - m.astype(F32)).astype(BF16)
    return h, m, l


def _mm_exact_l(mat_bf, x):
    h, m, l = _split3(x)
    return _mm(mat_bf, h) + _mm(mat_bf, m) + _mm(mat_bf, l)


def _mm_exact_r(x, mat_bf):
    h, m, l = _split3(x)
    return _mm(h, mat_bf) + _mm(m, mat_bf) + _mm(l, mat_bf)


def _sigmoid(x):
    return 1.0 / (1.0 + jnp.exp(-x))


def _rope_tables(pos):
    lane = lax.broadcasted_iota(jnp.int32, (1, 128), 1)
    jl = lane & 63
    fi = jl & 7
    inv = jnp.zeros((1, 128), F32)
    for kk in range(8):
        inv = jnp.where(fi == kk, INV_FREQ[kk], inv)
    ang = pos.astype(F32) * inv
    c = jnp.cos(ang)
    s = jnp.sin(ang)
    cosf = jnp.where(jl < 16, c, 1.0)
    s1 = jnp.where(jl < 8, -s, 0.0)
    s2 = jnp.where((jl >= 8) & (jl < 16), s, 0.0)
    return cosf, s1, s2


def _rope(t, cosf, s1, s2):
    parts = []
    for ci in range(t.shape[1] // 128):
        tc = t[:, ci * 128:(ci + 1) * 128]
        parts.append(tc * cosf + pltpu.roll(tc, 120, 1) * s1 + pltpu.roll(tc, 8, 1) * s2)
    return jnp.concatenate(parts, axis=1)


def _rope_bwd(g, cosf, s1, s2):
    parts = []
    for ci in range(g.shape[1] // 128):
        gc = g[:, ci * 128:(ci + 1) * 128]
        parts.append(gc * cosf + pltpu.roll(gc * s1, 8, 1) + pltpu.roll(gc * s2, 120, 1))
    return jnp.concatenate(parts, axis=1)


def _perm_store(val, scr, o1, o4, o16, dt):
    n = val.shape[0]
    o1[...] = val.astype(dt)
    for ci in range(4):
        cs = slice(ci * 128, (ci + 1) * 128)
        scr[ci] = val[:, cs]
        for rr in range(4):
            o4[rr, :, cs] = scr[ci, pl.ds(rr, n // 4, stride=4), :].astype(dt)
        for rr in range(16):
            o16[rr, :, cs] = scr[ci, pl.ds(rr, n // 16, stride=16), :].astype(dt)


def _unperm_load(r4, r16, scr_a, scr_b):
    n = scr_a.shape[1]
    for ci in range(4):
        cs = slice(ci * 128, (ci + 1) * 128)
        for rr in range(4):
            scr_a[ci, pl.ds(rr, n // 4, stride=4), :] = r4[rr, :, cs]
        for rr in range(16):
            scr_b[ci, pl.ds(rr, n // 16, stride=16), :] = r16[rr, :, cs]
    return (jnp.concatenate([scr_a[ci] for ci in range(4)], axis=1),
            jnp.concatenate([scr_b[ci] for ci in range(4)], axis=1))


def _weight_gather(w_in, w_out):
    def body(win_ref, wout_ref, fin_ref, fout_ref, bin_ref, bout_ref, send_sems, recv_sems, loc_sems):
        x, y, c = lax.axis_index("x"), lax.axis_index("y"), lax.axis_index("c")
        sibling = (x, y, 1 - c)
        chips = [(1 - x, y), (x, 1 - y), (1 - x, 1 - y)]
        jm = 2 * x + y
        bin_ref[...] = win_ref[...].astype(BF16)
        bout_ref[...] = wout_ref[...].astype(BF16)

        def in_rows(px, py, half):
            return fin_ref.at[pl.ds(half * 512, 512), pl.ds((2 * px + py) * 1024, 1024)]

        def out_rows(px, py, half):
            return fout_ref.at[pl.ds((2 * px + py) * 256 + half * 128, 128), :]

        def rcopy(k, src, dst, to):
            return pltpu.make_async_remote_copy(src_ref=src, dst_ref=dst, send_sem=send_sems.at[k],
                                                recv_sem=recv_sems.at[k], device_id=to, device_id_type=MESH)

        loc_in = pltpu.make_async_copy(bin_ref, fin_ref.at[:, pl.ds(jm * 1024, 1024)], loc_sems.at[0])
        loc_out = pltpu.make_async_copy(bout_ref, fout_ref.at[pl.ds(jm * 256, 256), :], loc_sems.at[1])
        loc_in.start()
        loc_out.start()
        first = []
        for k, chip in enumerate(chips):
            first.append(rcopy(k, bin_ref.at[pl.ds(c * 512, 512), :], in_rows(x, y, c), (*chip, c)))
            first.append(rcopy(3 + k, bout_ref.at[pl.ds(c * 128, 128), :], out_rows(x, y, c), (*chip, c)))
        for cp in first:
            cp.start()
        passed = []
        for k, chip in enumerate(chips):
            rcopy(k, in_rows(*chip, c), in_rows(*chip, c), (x, y, c)).wait_recv()
            p1 = rcopy(6 + k, in_rows(*chip, c), in_rows(*chip, c), sibling)
            p1.start()
            rcopy(3 + k, out_rows(*chip, c), out_rows(*chip, c), (x, y, c)).wait_recv()
            p2 = rcopy(9 + k, out_rows(*chip, c), out_rows(*chip, c), sibling)
            p2.start()
            passed += [p1, p2]
        for k, chip in enumerate(chips):
            rcopy(6 + k, in_rows(*chip, 1 - c), in_rows(*chip, 1 - c), (x, y, c)).wait_recv()
            rcopy(9 + k, out_rows(*chip, 1 - c), out_rows(*chip, 1 - c), (x, y, c)).wait_recv()
        for cp in first + passed:
            cp.wait_send()
        loc_in.wait()
        loc_out.wait()

    return pl.pallas_call(
        body, name="weight_gather",
        out_shape=(jax.ShapeDtypeStruct((D, NCOL), BF16), jax.ShapeDtypeStruct((D, D), BF16)),
        in_specs=[pl.BlockSpec(memory_space=pltpu.VMEM), pl.BlockSpec(memory_space=pltpu.VMEM)],
        out_specs=(pl.BlockSpec(memory_space=pltpu.HBM), pl.BlockSpec(memory_space=pltpu.HBM)),
        scratch_shapes=[pltpu.VMEM((D, 1024), BF16), pltpu.VMEM((256, D), BF16),
                        pltpu.SemaphoreType.DMA((12,)), pltpu.SemaphoreType.DMA((12,)),
                        pltpu.SemaphoreType.DMA((2,))],
        compiler_params=_cp(),
    )(w_in, w_out)


def _fwd_in(x, pos, mixw, w_full):
    TT = 512

    def body(x_ref, pos_ref, mw_ref, w_ref, hn_ref, q1, k1, v1, q4, k4, v4, q16, k16, v16,
             ag, hq, hf, hi, hg, scr):
        xv = x_ref[...]
        r = lax.rsqrt(jnp.mean(xv * xv, axis=-1, keepdims=True) + EPS)
        hn = ((xv * r) * mw_ref[...]).astype(BF16)
        hn_ref[...] = hn
        cosf, s1, s2 = _rope_tables(pos_ref[...])

        def proj(g):
            return _mm(hn, w_ref[:, g * 512:(g + 1) * 512])

        def emit(val, o1, o4, o16):
            _perm_store(val, scr, o1, o4, o16, BF16)

        emit(_rope(proj(0), cosf, s1, s2), q1, q4, q16)
        emit(_rope(proj(1), cosf, s1, s2), k1, k4, k16)
        emit(proj(2), v1, v4, v16)
        ag[...] = proj(3)
        hq[...] = proj(4)
        hf[...] = proj(5)
        hi[...] = proj(6).astype(BF16)
        hg[...] = proj(7)

    tok = lambda w: pl.BlockSpec((TT, w), lambda i: (i, 0))
    d4 = pl.BlockSpec((4, TT // 4, AW), lambda i: (0, i, 0))
    d16 = pl.BlockSpec((16, TT // 16, AW), lambda i: (0, i, 0))
    sd = lambda shape, dt: jax.ShapeDtypeStruct(shape, dt)
    return pl.pallas_call(
        body, name="fwd_in", grid=(T // TT,),
        in_specs=[tok(D), tok(1), pl.BlockSpec((1, D), lambda i: (0, 0)),
                  pl.BlockSpec((D, NCOL), lambda i: (0, 0))],
        out_specs=[tok(D)] + [tok(AW)] * 3 + [d4] * 3 + [d16] * 3 + [tok(AW)] * 5,
        out_shape=[sd((T, D), BF16)] + [sd((T, AW), BF16)] * 3 + [sd((4, T // 4, AW), BF16)] * 3
        + [sd((16, T // 16, AW), BF16)] * 3
        + [sd((T, AW), F32), sd((T, AW), F32), sd((T, AW), F32), sd((T, AW), BF16), sd((T, AW), F32)],
        scratch_shapes=[pltpu.VMEM((4, TT, 128), F32)],
        compiler_params=_cp(("parallel",)),
    )(x, pos, mixw, w_full)


def _band_masks():
    qi = lax.broadcasted_iota(jnp.int32, (BLK, 2 * BLK), 0)
    kj = lax.broadcasted_iota(jnp.int32, (BLK, 2 * BLK), 1)
    band = (kj >= qi) & (kj <= qi + BLK)
    return band, kj


def _attn_fwd(q, k, v, nb, name):
    CH = 2 * BLK

    def body(q_ref, k_ref, v_ref, kp_ref, vp_ref, o_ref, lse_ref):
        i = pl.program_id(0)
        thr0 = jnp.where((2 * i) % nb == 0, BLK, 0)
        lane = lax.broadcasted_iota(jnp.int32, (1, 128), 1)
        in_a = [lane < HEAD, lane >= HEAD]
        band, kj = _band_masks()
        mask0 = band & (kj >= thr0)
        for hp in range(4):
            cs = slice(hp * 128, (hp + 1) * 128)
            for b in range(2):
                rs = slice(b * BLK, (b + 1) * BLK)
                q2 = q_ref[rs, cs]
                if b == 0:
                    kk = jnp.concatenate([kp_ref[:, cs], k_ref[rs, cs]], axis=0)
                    vv = jnp.concatenate([vp_ref[:, cs], v_ref[rs, cs]], axis=0)
                    mask = mask0
                else:
                    kk = k_ref[:, cs]
                    vv = v_ref[:, cs]
                    mask = band
                res = []
                for a in range(2):
                    qa = jnp.where(in_a[a], q2, jnp.zeros_like(q2))
                    s = jnp.where(mask, _mm_nt(qa, kk) * SCALE, NEG)
                    m = jnp.max(s, axis=-1, keepdims=True)
                    p = jnp.exp(s - m)
                    l = jnp.sum(p, axis=-1, keepdims=True)
                    o = _mm(p.astype(BF16), vv)
                    res.append((o / l, m + jnp.log(l)))
                o_ref[rs, cs] = jnp.where(in_a[0], res[0][0], res[1][0])
                lse_ref[rs, cs] = jnp.where(in_a[0], res[0][1], res[1][1])

    cur = pl.BlockSpec((CH, AW), lambda i: (i, 0))
    prev = pl.BlockSpec((BLK, AW), lambda i: (jnp.maximum(2 * i - 1, 0), 0))
    return pl.pallas_call(
        body, name=name, grid=(T // CH,),
        in_specs=[cur, cur, cur, prev, prev],
        out_specs=[cur, cur],
        out_shape=[jax.ShapeDtypeStruct((T, AW), F32)] * 2,
        compiler_params=_cp(("parallel",)),
    )(q, k, v, k, v)


def _attn_bwd(q, k, v, do, lse, dl, nb, name):
    CH = 2 * BLK
    NBLK = T // BLK

    def body(q_ref, k_ref, v_ref, do_ref, lse_ref, dl_ref, kp_ref, vp_ref,
             qn_ref, don_ref, lsen_ref, dln_ref, dq_ref, dk_ref, dv_ref):
        i = pl.program_id(0)
        thr0 = jnp.where((2 * i) % nb == 0, BLK, 0)
        nxt_thr = jnp.where((2 * i + 2) % nb == 0, 2 * BLK, 0)
        lane = lax.broadcasted_iota(jnp.int32, (1, 128), 1)
        in_a = [lane < HEAD, lane >= HEAD]
        band, kj = _band_masks()
        mask0 = band & (kj >= thr0)
        qi1 = lax.broadcasted_iota(jnp.int32, (BLK, BLK), 0)
        kj1 = lax.broadcasted_iota(jnp.int32, (BLK, BLK), 1)
        mask_next = kj1 >= qi1 + nxt_thr

        def stat(x2, a):
            xr = pltpu.roll(x2, HEAD, 1)
            return jnp.where(in_a[0], x2, xr) if a == 0 else jnp.where(in_a[0], xr, x2)

        for hp in range(4):
            cs = slice(hp * 128, (hp + 1) * 128)
            kb = [kp_ref[:, cs], k_ref[0:BLK, cs], k_ref[BLK:CH, cs]]
            vb = [vp_ref[:, cs], v_ref[0:BLK, cs], v_ref[BLK:CH, cs]]
            dk_acc = [jnp.zeros((BLK, 128), F32), jnp.zeros((BLK, 128), F32)]
            dv_acc = [jnp.zeros((BLK, 128), F32), jnp.zeros((BLK, 128), F32)]
            for b in range(2):
                rs = slice(b * BLK, (b + 1) * BLK)
                q2, do2, lse2, dl2 = q_ref[rs, cs], do_ref[rs, cs], lse_ref[rs, cs], dl_ref[rs, cs]
                kk = jnp.concatenate([kb[b], kb[b + 1]], axis=0)
                vv = jnp.concatenate([vb[b], vb[b + 1]], axis=0)
                mask = mask0 if b == 0 else band
                dq_parts = []
                for a in range(2):
                    qa = jnp.where(in_a[a], q2, jnp.zeros_like(q2))
                    doa = jnp.where(in_a[a], do2, jnp.zeros_like(do2))
                    lse_a = stat(lse2, a)
                    dl_a = stat(dl2, a)
                    lse_w = jnp.concatenate([lse_a, lse_a], axis=1)
                    dl_w = jnp.concatenate([dl_a, dl_a], axis=1)
                    s = jnp.where(mask, _mm_nt(qa, kk) * SCALE, NEG)
                    p = jnp.exp(s - lse_w)
                    dp = _mm_nt(doa, vv)
                    ds = (p * (dp - dl_w)).astype(BF16)
                    pb = p.astype(BF16)
                    dq_parts.append(_mm(ds, kk) * SCALE)
                    if b == 0:
                        dk_acc[0] += _mm_tn(ds[:, BLK:], qa) * SCALE
                        dv_acc[0] += _mm_tn(pb[:, BLK:], doa)
                    else:
                        dkk = _mm_tn(ds, qa) * SCALE
                        dvv = _mm_tn(pb, doa)
                        dk_acc[0] += dkk[:BLK]
                        dk_acc[1] += dkk[BLK:]
                        dv_acc[0] += dvv[:BLK]
                        dv_acc[1] += dvv[BLK:]
                dq_ref[rs, cs] = jnp.where(in_a[0], dq_parts[0], dq_parts[1])
            q2, do2, lse2, dl2 = qn_ref[:, cs], don_ref[:, cs], lsen_ref[:, cs], dln_ref[:, cs]
            for a in range(2):
                qa = jnp.where(in_a[a], q2, jnp.zeros_like(q2))
                doa = jnp.where(in_a[a], do2, jnp.zeros_like(do2))
                s = jnp.where(mask_next, _mm_nt(qa, kb[2]) * SCALE, NEG)
                p = jnp.exp(s - stat(lse2, a))
                dp = _mm_nt(doa, vb[2])
                ds = (p * (dp - stat(dl2, a))).astype(BF16)
                dk_acc[1] += _mm_tn(ds, qa) * SCALE
                dv_acc[1] += _mm_tn(p.astype(BF16), doa)
            dk_ref[0:BLK, cs] = dk_acc[0]
            dk_ref[BLK:CH, cs] = dk_acc[1]
            dv_ref[0:BLK, cs] = dv_acc[0]
            dv_ref[BLK:CH, cs] = dv_acc[1]

    cur = pl.BlockSpec((CH, AW), lambda i: (i, 0))
    prev = pl.BlockSpec((BLK, AW), lambda i: (jnp.maximum(2 * i - 1, 0), 0))
    nxt = pl.BlockSpec((BLK, AW), lambda i: (jnp.minimum(2 * i + 2, NBLK - 1), 0))
    return pl.pallas_call(
        body, name=name, grid=(T // CH,),
        in_specs=[cur] * 6 + [prev] * 2 + [nxt] * 4,
        out_specs=[cur] * 3,
        out_shape=[jax.ShapeDtypeStruct((T, AW), F32)] * 3,
        compiler_params=_cp(("parallel",)),
    )(q, k, v, do, lse, dl, k, v, q, do, lse, dl)


TH = 256
NCH = TH // CHUNK


def _hgrn_common(hq_ref, hf_ref, lbr_ref, tri_ref):
    r0 = lbr_ref[0:1, :]
    r1 = lbr_ref[1:2, :]
    mx = jnp.maximum(r0, r1)
    e0 = jnp.exp(r0 - mx)
    e1 = jnp.exp(r1 - mx)
    lb = e0 / (e0 + e1)
    hqv = hq_ref[...]
    sq = _sigmoid(hqv)
    qv = hqv * sq
    sf = _sigmoid(hf_ref[...])
    f = lb + (1.0 - lb) * sf
    kv = 1.0 - f
    g = jnp.log(f)
    cum = _mm_exact_l(tri_ref[...], g)
    lastb = jnp.concatenate(
        [jnp.broadcast_to(cum[c * CHUNK + CHUNK - 1:(c + 1) * CHUNK, :], (CHUNK, HW)) for c in range(NCH)], axis=0)
    ea = jnp.exp(cum)
    ena = jnp.exp(-cum)
    eend = jnp.exp(lastb - cum)
    return dict(lb=lb, hq=hqv, sq=sq, q=qv, sf=sf, f=f, k=kv, cum=cum, lastb=lastb, ea=ea, ena=ena, eend=eend,
                qd=qv * ea, ki=kv * ena, ke=kv * eend, dec=jnp.exp(lastb))


def _tri_mask():
    ti = lax.broadcasted_iota(jnp.int32, (CHUNK, CHUNK), 0)
    si = lax.broadcasted_iota(jnp.int32, (CHUNK, CHUNK), 1)
    return si <= ti


def _hgrn_fwd(hq, hf, hi, lbr, tri):
    def body(hq_ref, hf_ref, hi_ref, lbr_ref, tri_ref, rec_ref, sall_ref, st_scr):
        @pl.when(pl.program_id(0) == 0)
        def _():
            st_scr[...] = jnp.zeros_like(st_scr)

        w = _hgrn_common(hq_ref, hf_ref, lbr_ref, tri_ref)
        qd, ki, ke = w["qd"].astype(BF16), w["ki"].astype(BF16), w["ke"].astype(BF16)
        dec = w["dec"]
        vb = hi_ref[...]
        causal = _tri_mask()
        for c in range(NCH):
            rs = slice(c * CHUNK, (c + 1) * CHUNK)
            sall_ref[c] = st_scr[...]
            for h in range(4):
                cs = slice(h * 128, (h + 1) * 128)
                st = st_scr[:, cs]
                att = jnp.where(causal, _mm_nt(qd[rs, cs], ki[rs, cs]), 0.0)
                o = _mm(att.astype(BF16), vb[rs, cs]) + _mm_nt(qd[rs, cs], st.astype(BF16))
                rec_ref[rs, cs] = o
                st_scr[:, cs] = dec[c * CHUNK:c * CHUNK + 1, cs] * st + _mm_tn(vb[rs, cs], ke[rs, cs])

    tok = pl.BlockSpec((TH, HW), lambda i: (i, 0))
    return pl.pallas_call(
        body, name="hgrn_fwd", grid=(T // TH,),
        in_specs=[tok, tok, tok, pl.BlockSpec((2, HW), lambda i: (0, 0)), pl.BlockSpec((TH, TH), lambda i: (0, 0))],
        out_specs=[tok, pl.BlockSpec((NCH, 128, HW), lambda i: (i, 0, 0))],
        out_shape=[jax.ShapeDtypeStruct((T, HW), F32), jax.ShapeDtypeStruct((T // CHUNK, 128, HW), F32)],
        scratch_shapes=[pltpu.VMEM((128, HW), F32)],
        compiler_params=_cp(("arbitrary",)),
    )(hq, hf, hi, lbr, tri)


def _hgrn_bwd(hq, hf, hi, lbr, tri, trit, drec, sall):
    NT = T // TH

    def body(hq_ref, hf_ref, hi_ref, lbr_ref, tri_ref, trit_ref, do_ref, sall_ref,
             dhq_ref, dhf_ref, dhi_ref, small_ref, dst_scr, dlb_scr, dqd_scr, dki_scr, dke_scr, dlast_scr):
        step = pl.program_id(0)

        @pl.when(step == 0)
        def _():
            dst_scr[...] = jnp.zeros_like(dst_scr)
            dlb_scr[...] = jnp.zeros_like(dlb_scr)

        w = _hgrn_common(hq_ref, hf_ref, lbr_ref, tri_ref)
        qd, ki, ke = w["qd"].astype(BF16), w["ki"].astype(BF16), w["ke"].astype(BF16)
        dec = w["dec"]
        vb = hi_ref[...]
        dob = do_ref[...].astype(BF16)
        causal = _tri_mask()
        for c in reversed(range(NCH)):
            rs = slice(c * CHUNK, (c + 1) * CHUNK)
            dec_c = dec[c * CHUNK:c * CHUNK + 1, :]
            for h in range(4):
                cs = slice(h * 128, (h + 1) * 128)
                st = sall_ref[c, :, cs]
                dst = dst_scr[:, cs]
                dstb = dst.astype(BF16)
                att = jnp.where(causal, _mm_nt(qd[rs, cs], ki[rs, cs]), 0.0).astype(BF16)
                datt = jnp.where(causal, _mm_nt(dob[rs, cs], vb[rs, cs]), 0.0).astype(BF16)
                dhi_ref[rs, cs] = (_mm_tn(att, dob[rs, cs]) + _mm_nt(ke[rs, cs], dstb)).astype(BF16)
                dqd_scr[rs, cs] = _mm(datt, ki[rs, cs]) + _mm(dob[rs, cs], st.astype(BF16))
                dki_scr[rs, cs] = _mm_tn(datt, qd[rs, cs])
                dke_scr[rs, cs] = _mm(vb[rs, cs], dstb)
                ddec = jnp.sum(dst * st, axis=0, keepdims=True)
                dlast_scr[c:c + 1, cs] = ddec * dec_c[:, cs]
                dst_scr[:, cs] = dec_c[:, cs] * dst + _mm_tn(dob[rs, cs], qd[rs, cs])
        dqd, dki, dke = dqd_scr[...], dki_scr[...], dke_scr[...]
        dq = dqd * w["ea"]
        dk = dki * w["ena"] + dke * w["eend"]
        dcum = dqd * w["qd"] - dki * w["ki"] - dke * w["ke"]
        dkeke = dke * w["ke"]
        dlastb = jnp.concatenate(
            [jnp.broadcast_to(dlast_scr[c:c + 1, :] + jnp.sum(dkeke[c * CHUNK:(c + 1) * CHUNK], axis=0, keepdims=True),
                              (CHUNK, HW)) for c in range(NCH)], axis=0)
        dg = _mm_exact_l(trit_ref[...], dcum) + dlastb
        df = dg / w["f"] - dk
        lb, sf, sq = w["lb"], w["sf"], w["sq"]
        dhf_ref[...] = (df * (1.0 - lb) * sf * (1.0 - sf)).astype(BF16)
        dhq_ref[...] = (dq * (sq * (1.0 + w["hq"] * (1.0 - sq)))).astype(BF16)
        dlb_scr[...] += jnp.sum(df * (1.0 - sf), axis=0, keepdims=True)

        @pl.when(step == NT - 1)
        def _():
            gr = dlb_scr[...] * lb * (1.0 - lb)
            small_ref[...] = jnp.zeros_like(small_ref)
            small_ref[0:1, 0:HW] = gr
            small_ref[1:2, 0:HW] = -gr

    tok = pl.BlockSpec((TH, HW), lambda i: (NT - 1 - i, 0))
    const = lambda shape: pl.BlockSpec(shape, lambda i: (0,) * len(shape))
    return pl.pallas_call(
        body, name="hgrn_bwd", grid=(NT,),
        in_specs=[tok, tok, tok, const((2, HW)), const((TH, TH)), const((TH, TH)), tok,
                  pl.BlockSpec((NCH, 128, HW), lambda i: (NT - 1 - i, 0, 0))],
        out_specs=[tok, tok, tok, const((8, D))],
        out_shape=[jax.ShapeDtypeStruct((T, HW), BF16)] * 3 + [jax.ShapeDtypeStruct((8, D), F32)],
        scratch_shapes=[pltpu.VMEM((128, HW), F32), pltpu.VMEM((1, HW), F32), pltpu.VMEM((TH, HW), F32),
                        pltpu.VMEM((TH, HW), F32), pltpu.VMEM((TH, HW), F32), pltpu.VMEM((8, HW), F32)],
        compiler_params=_cp(("arbitrary",)),
    )(hq, hf, hi, lbr, tri, trit, drec, sall)


def _fwd_out(o1, o4, o16, l1, l4, l16, rec, ag, hg, x, tgt, anw, hnw, fnw, wout_full, gmat):
    TT = 256

    def body(o1_r, o4_r, o16_r, l1_r, l4_r, l16_r, rec_r, ag_r, hg_r, x_r, tgt_r, anw_r, hnw_r, fnw_r, wo_r, g_r,
             dx2_o, do1_o, do4_o, do16_o, ls1_o, ls4_o, ls16_o, dl1_o, dl4_o, dl16_o, drec_o, dag_o, dhg_o,
             rout_o, routb_o, small_o, scr_a, scr_b, gwout_o, rbuf, send_sems, recv_sems):
        @pl.when(pl.program_id(0) == 0)
        def _():
            gwout_o[...] = jnp.zeros_like(gwout_o)
            small_o[...] = jnp.zeros_like(small_o)

        def unperm(r4, r16):
            return _unperm_load(r4, r16, scr_a, scr_b)

        def perm_out(val, p1, p4, p16, dt):
            _perm_store(val, scr_a, p1, p4, p16, dt)

        o4u, o16u = unperm(o4_r, o16_r)
        l4u, l16u = unperm(l4_r, l16_r)
        o1v, l1v = o1_r[...], l1_r[...]
        mx = jnp.maximum(jnp.maximum(l1v, l4u), l16u)
        w1, w4, w16 = jnp.exp(l1v - mx), jnp.exp(l4u - mx), jnp.exp(l16u - mx)
        den = w1 + w4 + w16
        attn = (w1 * o1v + w4 * o4u + w16 * o16u) / den
        lse = mx + jnp.log(den)
        gm = g_r[...]

        def head_mean_a(t):
            return _mm_exact_r(t, gm)

        def head_mean_h(t):
            return jnp.concatenate(
                [jnp.broadcast_to(jnp.mean(t[:, h * 128:(h + 1) * 128], axis=-1, keepdims=True), (TT, 128))
                 for h in range(4)], axis=1)

        rs_a = lax.rsqrt(head_mean_a(attn * attn) + EPS)
        n_a = attn * rs_a
        agv = ag_r[...]
        sg_a = _sigmoid(agv)
        si_a = agv * sg_a
        anw_v = anw_r[...]
        y_a = (n_a * anw_v) * si_a
        recv = rec_r[...]
        rs_h = lax.rsqrt(head_mean_h(recv * recv) + EPS)
        n_h = recv * rs_h
        hgv = hg_r[...]
        sg_h = _sigmoid(hgv)
        si_h = hgv * sg_h
        hnw_v = hnw_r[...]
        y_h = (n_h * hnw_v) * si_h
        mixed = jnp.concatenate([y_a, y_h], axis=1).astype(BF16)
        xv = x_r[...]
        x2 = xv + _mm(mixed, wo_r[...])
        r2 = lax.rsqrt(jnp.mean(x2 * x2, axis=-1, keepdims=True) + EPS)
        fnw_v = fnw_r[...]
        xn = x2 * r2
        err = xn * fnw_v - tgt_r[...]
        small_o[2:3, :] += 0.5 * jnp.sum(jnp.mean(err * err, axis=-1, keepdims=True), axis=0, keepdims=True)
        dy = err * (1.0 / D)
        small_o[0:1, :] += jnp.sum(dy * xn, axis=0, keepdims=True)
        dyw = dy * fnw_v
        dx2 = r2 * dyw - x2 * ((r2 * r2 * r2) * jnp.mean(dyw * x2, axis=-1, keepdims=True))
        dx2_o[...] = dx2
        dx2b = dx2.astype(BF16)
        gwout_o[...] += _mm_tn(mixed, dx2b)
        dmix = _mm_nt(dx2b, wo_r[...])
        dm_a, dm_h = dmix[:, :AW], dmix[:, AW:]
        dag_o[...] = (dm_a * (n_a * anw_v) * (sg_a * (1.0 + agv * (1.0 - sg_a)))).astype(BF16)
        dn_a = dm_a * anw_v * si_a
        small_o[1:2, 0:AW] += jnp.sum(dm_a * n_a * si_a, axis=0, keepdims=True)
        dattn = rs_a * (dn_a - n_a * head_mean_a(dn_a * n_a))
        delta = head_mean_a(dattn * attn) * float(HEAD)
        perm_out(dattn, do1_o, do4_o, do16_o, BF16)
        perm_out(lse, ls1_o, ls4_o, ls16_o, F32)
        perm_out(delta, dl1_o, dl4_o, dl16_o, F32)
        dhg_o[...] = (dm_h * (n_h * hnw_v) * (sg_h * (1.0 + hgv * (1.0 - sg_h)))).astype(BF16)
        dn_h = dm_h * hnw_v * si_h
        small_o[1:2, AW:] += jnp.sum(dm_h * n_h * si_h, axis=0, keepdims=True)
        drec_o[...] = rs_h * (dn_h - n_h * head_mean_h(dn_h * n_h))

        @pl.when(pl.program_id(0) == T // TT - 1)
        def _():
            x, y, c = lax.axis_index("x"), lax.axis_index("y"), lax.axis_index("c")
            cps = [pltpu.make_async_remote_copy(
                src_ref=gwout_o.at[pl.ds(pl.multiple_of(j * 256 + (1 - c) * 128, 128), 128), :], dst_ref=rbuf.at[j],
                send_sem=send_sems.at[j], recv_sem=recv_sems.at[j], device_id=(x, y, 1 - c), device_id_type=MESH)
                for j in range(4)]
            for cp in cps:
                cp.start()
            for j, cp in enumerate(cps):
                cp.wait_recv()
                red = gwout_o[pl.ds(pl.multiple_of(j * 256 + c * 128, 128), 128), :] + rbuf[j]
                rout_o[j * 128:(j + 1) * 128, :] = red
                routb_o[j * 128:(j + 1) * 128, :] = red.astype(BF16)
            for cp in cps:
                cp.wait_send()

    tok = lambda w: pl.BlockSpec((TT, w), lambda i: (i, 0))
    d4 = pl.BlockSpec((4, TT // 4, AW), lambda i: (0, i, 0))
    d16 = pl.BlockSpec((16, TT // 16, AW), lambda i: (0, i, 0))
    const = lambda shape: pl.BlockSpec(shape, lambda i: (0,) * len(shape))
    sd = lambda shape, dt: jax.ShapeDtypeStruct(shape, dt)
    p3 = lambda dt: [sd((T, AW), dt), sd((4, T // 4, AW), dt), sd((16, T // 16, AW), dt)]
    return pl.pallas_call(
        body, name="fwd_out", grid=(T // TT,),
        in_specs=[tok(AW), d4, d16, tok(AW), d4, d16, tok(AW), tok(AW), tok(AW), tok(D), tok(D),
                  const((1, AW)), const((1, HW)), const((1, D)), const((D, D)), const((AW, AW))],
        out_specs=[tok(D)] + [tok(AW), d4, d16] * 3 + [tok(AW)] * 3 + [const((512, D)), const((512, D)), const((8, D))],
        out_shape=[sd((T, D), F32)] + p3(BF16) + p3(F32) + p3(F32)
        + [sd((T, AW), F32), sd((T, AW), BF16), sd((T, AW), BF16), sd((512, D), F32), sd((512, D), BF16),
           sd((8, D), F32)],
        scratch_shapes=[pltpu.VMEM((4, TT, 128), F32), pltpu.VMEM((4, TT, 128), F32), pltpu.VMEM((D, D), F32),
                        pltpu.VMEM((4, 128, D), F32), pltpu.SemaphoreType.DMA((4,)), pltpu.SemaphoreType.DMA((4,))],
        compiler_params=_cp(("arbitrary",)),
    )(o1, o4, o16, l1, l4, l16, rec, ag, hg, x, tgt, anw, hnw, fnw, wout_full, gmat)


def _dproj_build(dq, dk, dv, dag, dhq, dhf, dhi, dhg, pos):
    TT = 256

    def body(dq1, dq4, dq16, dk1, dk4, dk16, dv1, dv4, dv16, dag_r, dhq_r, dhf_r, dhi_r, dhg_r,
             pos_r, dproj_o, scr_a, scr_b):
        def unperm_sum(r1, r4, r16):
            u4, u16 = _unperm_load(r4, r16, scr_a, scr_b)
            return r1[...] + u4 + u16

        cosf, s1, s2 = _rope_tables(pos_r[...])
        dproj_o[:, 0:512] = _rope_bwd(unperm_sum(dq1, dq4, dq16), cosf, s1, s2).astype(BF16)
        dproj_o[:, 512:1024] = _rope_bwd(unperm_sum(dk1, dk4, dk16), cosf, s1, s2).astype(BF16)
        dproj_o[:, 1024:1536] = unperm_sum(dv1, dv4, dv16).astype(BF16)
        dproj_o[:, 1536:2048] = dag_r[...]
        dproj_o[:, 2048:2560] = dhq_r[...]
        dproj_o[:, 2560:3072] = dhf_r[...]
        dproj_o[:, 3072:3584] = dhi_r[...]
        dproj_o[:, 3584:4096] = dhg_r[...]

    tok = lambda w: pl.BlockSpec((TT, w), lambda i: (i, 0))
    d4 = pl.BlockSpec((4, TT // 4, AW), lambda i: (0, i, 0))
    d16 = pl.BlockSpec((16, TT // 16, AW), lambda i: (0, i, 0))
    return pl.pallas_call(
        body, name="dproj_build", grid=(T // TT,),
        in_specs=[tok(AW), d4, d16] * 3 + [tok(AW)] * 5 + [tok(1)],
        out_specs=tok(NCOL),
        out_shape=jax.ShapeDtypeStruct((T, NCOL), BF16),
        scratch_shapes=[pltpu.VMEM((4, TT, 128), F32), pltpu.VMEM((4, TT, 128), F32)],
        compiler_params=_cp(("parallel",)),
    )(*dq, *dk, *dv, dag, dhq, dhf, dhi, dhg, pos)


def _bwd_x(dproj, x, dx2, mixw, w_full, rin, rinb, rout, routb, small4, small6):
    TT = 256
    NT = T // TT

    def body(dp_r, x_r, dx2_r, mw_r, w_r, rin_r, rinb_r, rout_r, routb_r, s4_r, s6_r,
             gx_o, pin_o, pinr_o, pout_o, poutr_o, sall_o, sbuf, send_sems, recv_sems, loc_sems):
        i = pl.program_id(0)
        loc, rem = _chip_copies(rin_r, rinb_r, rout_r, routb_r, pin_o, pinr_o, pout_o, poutr_o,
                                send_sems, recv_sems, loc_sems)

        @pl.when(i == 0)
        def _():
            sbuf[...] = jnp.zeros_like(sbuf)
            for cp in loc + rem:
                cp.start()

        dhn = _mm_nt(dp_r[...], w_r[...])
        xv = x_r[...]
        r = lax.rsqrt(jnp.mean(xv * xv, axis=-1, keepdims=True) + EPS)
        dxw = dhn * mw_r[...]
        gx_o[...] = dx2_r[...] + r * dxw - xv * ((r * r * r) * jnp.mean(dxw * xv, axis=-1, keepdims=True))
        sbuf[16:17, :] += jnp.sum(dhn * (xv * r), axis=0, keepdims=True)

        @pl.when(i == NT - 1)
        def _():
            sbuf[0:8, :] = s4_r[...]
            sbuf[8:16, :] = s6_r[...]
            sloc, srem = _small_copies(sbuf, sall_o, send_sems, recv_sems, loc_sems)
            for cp in sloc + srem:
                cp.start()
            for cp in rem + srem:
                cp.wait_recv()
            for cp in rem + srem:
                cp.wait_send()
            for cp in loc + sloc:
                cp.wait()

    tok = lambda w: pl.BlockSpec((TT, w), lambda i: (i, 0))
    const = lambda shape: pl.BlockSpec(shape, lambda i: (0,) * len(shape))
    hbm = pl.BlockSpec(memory_space=pltpu.HBM)
    return pl.pallas_call(
        body, name="bwd_x", grid=(NT,),
        in_specs=[tok(NCOL), tok(D), tok(D), const((1, D)), const((D, NCOL)), hbm, hbm, hbm, hbm,
                  const((8, D)), const((8, D))],
        out_specs=[tok(D), hbm, hbm, hbm, hbm, hbm],
        out_shape=[jax.ShapeDtypeStruct((T, D), F32),
                   jax.ShapeDtypeStruct((512, 1024), F32), jax.ShapeDtypeStruct((3, 512, 1024), BF16),
                   jax.ShapeDtypeStruct((128, D), F32), jax.ShapeDtypeStruct((3, 128, D), BF16),
                   jax.ShapeDtypeStruct((8, 24, D), F32)],
        scratch_shapes=[pltpu.VMEM((24, D), F32), pltpu.SemaphoreType.DMA((13,)), pltpu.SemaphoreType.DMA((13,)),
                        pltpu.SemaphoreType.DMA((3,))],
        compiler_params=_cp(("arbitrary",)),
    )(dproj, x, dx2, mixw, w_full, rin, rinb, rout, routb, small4, small6)


def _grad_w_in(hn, dproj):
    TK = 512
    NK = T // TK

    def body(hn_r, dp_r, rin_o, rinb_o, acc, rbuf, obuf, obufb, send_sems, recv_sems, wb_sems):
        j = pl.program_id(0)
        kk = pl.program_id(1)
        x, y, c = lax.axis_index("x"), lax.axis_index("y"), lax.axis_index("c")
        mine = pl.ds(pl.multiple_of(c * 512, 512), 512)
        theirs = pl.ds(pl.multiple_of((1 - c) * 512, 512), 512)

        def send(jj):
            return pltpu.make_async_remote_copy(
                src_ref=acc.at[jj % 2, theirs, :], dst_ref=rbuf.at[jj], send_sem=send_sems.at[jj],
                recv_sem=recv_sems.at[jj], device_id=(x, y, 1 - c), device_id_type=MESH)

        def writeback(jj):
            cols = pl.ds(jj * 1024, 1024)
            return [pltpu.make_async_copy(obuf.at[jj % 2], rin_o.at[:, cols], wb_sems.at[jj % 2]),
                    pltpu.make_async_copy(obufb.at[jj % 2], rinb_o.at[:, cols], wb_sems.at[2 + jj % 2])]

        def wait_writeback(jj):
            for cp in writeback(jj):
                cp.wait()

        def finalize(jj):
            send(jj).wait_recv()
            red = acc[jj % 2, mine, :] + rbuf[jj]
            obuf[jj % 2] = red
            obufb[jj % 2] = red.astype(BF16)
            for cp in writeback(jj):
                cp.start()

        prod = _mm_tn(hn_r[...], dp_r[...])

        @pl.when(kk == 0)
        def _():
            for jj in (2, 3):
                @pl.when(j == jj)
                def _():
                    send(jj - 2).wait_send()
            acc[j % 2] = prod

        @pl.when(kk > 0)
        def _():
            acc[j % 2] += prod

        @pl.when(kk == NK - 1)
        def _():
            for jj in range(4):
                @pl.when(j == jj)
                def _():
                    send(jj).start()
                    if jj in (1, 2):
                        finalize(jj - 1)
                    if jj == 3:
                        wait_writeback(0)
                        finalize(2)
                        wait_writeback(1)
                        finalize(3)
                        wait_writeback(2)
                        wait_writeback(3)
                        send(2).wait_send()
                        send(3).wait_send()

    hbm = pl.BlockSpec(memory_space=pltpu.HBM)
    return pl.pallas_call(
        body, name="grad_w_in", grid=(4, NK),
        in_specs=[pl.BlockSpec((TK, D), lambda j, kk: (kk, 0)), pl.BlockSpec((TK, 1024), lambda j, kk: (kk, j))],
        out_specs=[hbm, hbm],
        out_shape=[jax.ShapeDtypeStruct((512, NCOL), F32), jax.ShapeDtypeStruct((512, NCOL), BF16)],
        scratch_shapes=[pltpu.VMEM((2, D, 1024), F32), pltpu.VMEM((4, 512, 1024), F32), pltpu.VMEM((2, 512, 1024), F32),
                        pltpu.VMEM((2, 512, 1024), BF16),
                        pltpu.SemaphoreType.DMA((4,)), pltpu.SemaphoreType.DMA((4,)), pltpu.SemaphoreType.DMA((4,))],
        compiler_params=_cp(("arbitrary", "arbitrary")),
    )(hn, dproj)


def _chip_sum(own, rem, name):
    rows, cols = own.shape
    tr = min(rows, 256)

    def body(own_r, rem_r, o_r):
        acc = own_r[...]
        for s in range(3):
            acc = acc + rem_r[s].astype(F32)
        o_r[...] = acc

    return pl.pallas_call(
        body, name=name, grid=(rows // tr,),
        in_specs=[pl.BlockSpec((tr, cols), lambda i: (i, 0)), pl.BlockSpec((3, tr, cols), lambda i: (0, i, 0))],
        out_specs=pl.BlockSpec((tr, cols), lambda i: (i, 0)),
        out_shape=jax.ShapeDtypeStruct((rows, cols), F32),
        compiler_params=_cp(("parallel",)),
    )(own, rem)


def _chip_copies(rin_r, rinb_r, rout_r, routb_r, pin_o, pinr_o, pout_o, poutr_o, send_sems, recv_sems, loc_sems):
    x, y, c = lax.axis_index("x"), lax.axis_index("y"), lax.axis_index("c")
    chips = [(1 - x, y), (x, 1 - y), (1 - x, 1 - y)]
    jm = 2 * x + y
    loc = [pltpu.make_async_copy(rin_r.at[:, pl.ds(jm * 1024, 1024)], pin_o, loc_sems.at[0]),
           pltpu.make_async_copy(rout_r.at[pl.ds(jm * 128, 128), :], pout_o, loc_sems.at[1])]
    rem = []
    for k, (px, py) in enumerate(chips):
        j = 2 * px + py
        rem.append(pltpu.make_async_remote_copy(
            src_ref=rinb_r.at[:, pl.ds(j * 1024, 1024)], dst_ref=pinr_o.at[k],
            send_sem=send_sems.at[k], recv_sem=recv_sems.at[k], device_id=(px, py, c), device_id_type=MESH))
        rem.append(pltpu.make_async_remote_copy(
            src_ref=routb_r.at[pl.ds(j * 128, 128), :], dst_ref=poutr_o.at[k],
            send_sem=send_sems.at[3 + k], recv_sem=recv_sems.at[3 + k], device_id=(px, py, c),
            device_id_type=MESH))
    return loc, rem


def _small_copies(small_r, sall_o, send_sems, recv_sems, loc_sems):
    x, y, c = lax.axis_index("x"), lax.axis_index("y"), lax.axis_index("c")
    me = 4 * x + 2 * y + c
    loc = [pltpu.make_async_copy(small_r, sall_o.at[me], loc_sems.at[2])]
    rem = []
    k = 6
    for fx in range(2):
        for fy in range(2):
            for fc in range(2):
                if fx or fy or fc:
                    peer = (1 - x if fx else x, 1 - y if fy else y, 1 - c if fc else c)
                    rem.append(pltpu.make_async_remote_copy(
                        src_ref=small_r, dst_ref=sall_o.at[me], send_sem=send_sems.at[k],
                        recv_sem=recv_sems.at[k], device_id=peer, device_id_type=MESH))
                    k += 1
    return loc, rem


def _pair_share(pin, pout):
    def body(pin_r, pout_r, fin_o, fout_o, send_sems, recv_sems):
        x, y, c = lax.axis_index("x"), lax.axis_index("y"), lax.axis_index("c")
        sibling = (x, y, 1 - c)
        rem = [pltpu.make_async_remote_copy(src_ref=pin_r, dst_ref=fin_o.at[c], send_sem=send_sems.at[0],
                                            recv_sem=recv_sems.at[0], device_id=sibling, device_id_type=MESH),
               pltpu.make_async_remote_copy(src_ref=pout_r, dst_ref=fout_o.at[c], send_sem=send_sems.at[1],
                                            recv_sem=recv_sems.at[1], device_id=sibling, device_id_type=MESH)]
        for cp in rem:
            cp.start()
        fin_o[c] = pin_r[...]
        fout_o[c] = pout_r[...]
        for cp in rem:
            cp.wait_recv()
        for cp in rem:
            cp.wait_send()

    vm = pl.BlockSpec(memory_space=pltpu.VMEM)
    return pl.pallas_call(
        body, name="pair_share",
        out_shape=(jax.ShapeDtypeStruct((2, 512, 1024), F32), jax.ShapeDtypeStruct((2, 128, D), F32)),
        in_specs=[vm, vm], out_specs=(vm, vm),
        scratch_shapes=[pltpu.SemaphoreType.DMA((2,)), pltpu.SemaphoreType.DMA((2,))],
        compiler_params=_cp(),
    )(pin, pout)


def _adamw_math(w, g, m, v):
    m = B1 * m + (1.0 - B1) * g
    v = B2 * v + (1.0 - B2) * (g * g)
    m_hat = m / (1.0 - B1 ** STEP)
    v_hat = v / (1.0 - B2 ** STEP)
    delta = -LR * (m_hat / (jnp.sqrt(v_hat) + AEPS) + WD * w)
    return delta, m, v


def _adamw(w, g, m, v, name):
    rows, cols = w.shape
    tr = min(rows, 256)

    def body(w_r, g_r, m_r, v_r, d_o, m_o, v_o):
        d, mm, vv = _adamw_math(w_r[...], g_r[...], m_r[...], v_r[...])
        d_o[...] = d
        m_o[...] = mm
        v_o[...] = vv

    blk = pl.BlockSpec((tr, cols), lambda i: (i, 0))
    return pl.pallas_call(
        body, name=name, grid=(rows // tr,),
        in_specs=[blk] * 4, out_specs=[blk] * 3,
        out_shape=[jax.ShapeDtypeStruct((rows, cols), F32)] * 3,
        compiler_params=_cp(("parallel",)),
    )(w, g, m, v)


def _adamw_small(sall, params):
    def body(sall_r, *refs):
        ins, outs = refs[:15], refs[15:]
        tot = sall_r[0]
        for dv in range(1, 8):
            tot = tot + sall_r[dv]
        grads = [tot[16:17, :], tot[1:2, 0:AW], tot[1:2, AW:], tot[8:10, 0:HW], tot[0:1, :]]
        outs[0][...] = tot[2:3, 0:1]
        for p in range(5):
            w_r, m_r, v_r = ins[3 * p:3 * p + 3]
            g = grads[p]
            d, mm, vv = _adamw_math(w_r[...], g, m_r[...], v_r[...])
            outs[1 + 4 * p][...] = g
            outs[2 + 4 * p][...] = d
            outs[3 + 4 * p][...] = mm
            outs[4 + 4 * p][...] = vv

    flat = [a for p in params for a in p]
    shapes = [jax.ShapeDtypeStruct((1, 1), F32)]
    for p in params:
        shapes += [jax.ShapeDtypeStruct(p[0].shape, F32)] * 4
    vm = pl.BlockSpec(memory_space=pltpu.VMEM)
    return pl.pallas_call(
        body, name="adamw_small",
        in_specs=[vm] * 16, out_specs=[vm] * 21, out_shape=shapes,
        compiler_params=_cp(),
    )(sall, *flat)


def kernel(x, positions, w_in, w_out, mix_norm_w, attn_out_norm_w, hgrn_out_norm_w, hgrn_lb_raw, final_norm_w, loss_target, m_w_in, m_w_out, m_mix_norm_w, m_attn_out_norm_w, m_hgrn_out_norm_w, m_hgrn_lb_raw, m_final_norm_w, v_w_in, v_w_out, v_mix_norm_w, v_attn_out_norm_w, v_hgrn_out_norm_w, v_hgrn_lb_raw, v_final_norm_w):
    xs = x.reshape(T, D)
    tgt = loss_target.reshape(T, D)
    pos = positions.reshape(T, 1)
    fnw = final_norm_w.reshape(1, D)

    ti = np.arange(TH)
    tri_np = ((ti[:, None] // CHUNK == ti[None, :] // CHUNK) & (ti[None, :] <= ti[:, None])).astype(np.float32)
    tri = jnp.asarray(tri_np, BF16)
    trit = jnp.asarray(tri_np.T, BF16)
    hi_ = np.arange(AW) // HEAD
    gmat = jnp.asarray((hi_[:, None] == hi_[None, :]).astype(np.float32) / HEAD, BF16)

    w_full, wout_full = _weight_gather(w_in.reshape(D, 1024), w_out.reshape(256, D))

    (hn, q1, k1, v1, q4, k4, v4, q16, k16, v16, ag, hq, hf, hi, hg) = _fwd_in(xs, pos, mix_norm_w, w_full)
    flat = lambda a: a.reshape(T, AW)
    o1, l1 = _attn_fwd(q1, k1, v1, T // BLK, "attn_fwd_d1")
    o4, l4 = _attn_fwd(flat(q4), flat(k4), flat(v4), T // 4 // BLK, "attn_fwd_d4")
    o16, l16 = _attn_fwd(flat(q16), flat(k16), flat(v16), T // 16 // BLK, "attn_fwd_d16")
    rec, sall = _hgrn_fwd(hq, hf, hi, hgrn_lb_raw, tri)

    (dx2, do1, do4, do16, ls1, ls4, ls16, dl1, dl4, dl16, drec, dag, dhg, rout, routb, small4) = _fwd_out(
        o1, o4.reshape(4, T // 4, AW), o16.reshape(16, T // 16, AW),
        l1, l4.reshape(4, T // 4, AW), l16.reshape(16, T // 16, AW),
        rec, ag, hg, xs, tgt, attn_out_norm_w, hgrn_out_norm_w, fnw, wout_full, gmat)

    dq1, dk1, dv1 = _attn_bwd(q1, k1, v1, do1, ls1, dl1, T // BLK, "attn_bwd_d1")
    dq4, dk4, dv4 = _attn_bwd(flat(q4), flat(k4), flat(v4), flat(do4), flat(ls4), flat(dl4), T // 4 // BLK,
                              "attn_bwd_d4")
    dq16, dk16, dv16 = _attn_bwd(flat(q16), flat(k16), flat(v16), flat(do16), flat(ls16), flat(dl16),
                                 T // 16 // BLK, "attn_bwd_d16")
    dhq, dhf, dhi, small6 = _hgrn_bwd(hq, hf, hi, hgrn_lb_raw, tri, trit, drec, sall)

    r4 = lambda a: a.reshape(4, T // 4, AW)
    r16 = lambda a: a.reshape(16, T // 16, AW)
    dproj = _dproj_build((dq1, r4(dq4), r16(dq16)), (dk1, r4(dk4), r16(dk16)), (dv1, r4(dv4), r16(dv16)),
                         dag, dhq, dhf, dhi, dhg, pos)
    rin, rinb = _grad_w_in(hn, dproj)
    gx, pin_own, pin_rem, pout_own, pout_rem, small_all = _bwd_x(
        dproj, xs, dx2, mix_norm_w, w_full, rin, rinb, rout, routb, small4, small6)
    pin = _chip_sum(pin_own, pin_rem, "chip_sum_in")
    pout = _chip_sum(pout_own, pout_rem, "chip_sum_out")
    fin, fout = _pair_share(pin, pout)
    g_w_in = fin.reshape(D, 1024)
    g_w_out = fout.reshape(256, D)

    d_in, nm_in, nv_in = _adamw(w_in.reshape(D, 1024), g_w_in, m_w_in.reshape(D, 1024), v_w_in.reshape(D, 1024),
                                "adamw_w_in")
    d_out, nm_out, nv_out = _adamw(w_out.reshape(256, D), g_w_out, m_w_out.reshape(256, D), v_w_out.reshape(256, D),
                                   "adamw_w_out")
    params = [(mix_norm_w, m_mix_norm_w, v_mix_norm_w),
              (attn_out_norm_w, m_attn_out_norm_w, v_attn_out_norm_w),
              (hgrn_out_norm_w, m_hgrn_out_norm_w, v_hgrn_out_norm_w),
              (hgrn_lb_raw, m_hgrn_lb_raw, v_hgrn_lb_raw),
              (fnw, m_final_norm_w.reshape(1, D), v_final_norm_w.reshape(1, D))]
    so = _adamw_small(small_all, params)
    loss = so[0].reshape(())
    g_s = [so[1 + 4 * p] for p in range(5)]
    d_s = [so[2 + 4 * p] for p in range(5)]
    m_s = [so[3 + 4 * p] for p in range(5)]
    v_s = [so[4 + 4 * p] for p in range(5)]
    for lst in (g_s, d_s, m_s, v_s):
        lst[4] = lst[4].reshape(D)

    return (loss, gx.reshape(1, T, D),
            g_w_in.reshape(1, D, 1024), g_w_out.reshape(1, 256, D), *g_s,
            d_in.reshape(1, D, 1024), d_out.reshape(1, 256, D), *d_s,
            nm_in.reshape(1, D, 1024), nm_out.reshape(1, 256, D), *m_s,
            nv_in.reshape(1, D, 1024), nv_out.reshape(1, 256, D), *v_s)
```

```python
import functools

import numpy as np
import jax
import jax.numpy as jnp
from jax import lax
from jax.experimental import pallas as pl
from jax.experimental.pallas import tpu as pltpu

F32 = jnp.float32
BF16 = jnp.bfloat16

T = 4096
D = 1024
AW = 512
HW = 512
NCOL = 4096
HEAD = 64
BLK = 128
CHUNK = 64
EPS = 1e-6
SCALE = HEAD ** -0.5
NEG = -1e30
ROPE_THETA = 500000.0
INV_FREQ = [float(v) for v in
            (np.float32(ROPE_THETA) ** (-(np.arange(8, dtype=np.float32)) * np.float32(0.125)))]
LR, B1, B2, AEPS, WD, STEP = 0.001, 0.9, 0.999, 1e-08, 0.01, 10
VMEM_LIMIT = 56 * 1024 * 1024
MESH = pl.DeviceIdType.MESH


def _cp(sem=None, **kw):
    return pltpu.CompilerParams(dimension_semantics=sem, vmem_limit_bytes=VMEM_LIMIT, **kw)


def _mm(a, b):
    return jnp.dot(a, b, preferred_element_type=F32)


def _mm_nt(a, b):
    return lax.dot_general(a, b, (((1,), (1,)), ((), ())), preferred_element_type=F32)


def _mm_tn(a, b):
    return lax.dot_general(a, b, (((0,), (0,)), ((), ())), preferred_element_type=F32)


def _split3(x):
    h = x.astype(BF16)
    r = x - h.astype(F32)
    m = r.astype(BF16)
    l = (r - m.astype(F32)).astype(BF16)
    return h, m, l


def _mm_exact_l(mat_bf, x):
    h, m, l = _split3(x)
    return _mm(mat_bf, h) + _mm(mat_bf, m) + _mm(mat_bf, l)


def _mm_exact_r(x, mat_bf):
    h, m, l = _split3(x)
    return _mm(h, mat_bf) + _mm(m, mat_bf) + _mm(l, mat_bf)


def _sigmoid(x):
    return 1.0 / (1.0 + jnp.exp(-x))


def _rope_tables(pos):
    lane = lax.broadcasted_iota(jnp.int32, (1, 128), 1)
    jl = lane & 63
    fi = jl & 7
    inv = jnp.zeros((1, 128), F32)
    for kk in range(8):
        inv = jnp.where(fi == kk, INV_FREQ[kk], inv)
    ang = pos.astype(F32) * inv
    c = jnp.cos(ang)
    s = jnp.sin(ang)
    cosf = jnp.where(jl < 16, c, 1.0)
    s1 = jnp.where(jl < 8, -s, 0.0)
    s2 = jnp.where((jl >= 8) & (jl < 16), s, 0.0)
    return cosf, s1, s2


def _rope(t, cosf, s1, s2):
    parts = []
    for ci in range(t.shape[1] // 128):
        tc = t[:, ci * 128:(ci + 1) * 128]
        parts.append(tc * cosf + pltpu.roll(tc, 120, 1) * s1 + pltpu.roll(tc, 8, 1) * s2)
    return jnp.concatenate(parts, axis=1)


def _rope_bwd(g, cosf, s1, s2):
    parts = []
    for ci in range(g.shape[1] // 128):
        gc = g[:, ci * 128:(ci + 1) * 128]
        parts.append(gc * cosf + pltpu.roll(gc * s1, 8, 1) + pltpu.roll(gc * s2, 120, 1))
    return jnp.concatenate(parts, axis=1)


def _perm_store(val, scr, o1, o4, o16, dt):
    n = val.shape[0]
    o1[...] = val.astype(dt)
    for ci in range(val.shape[1] // 128):
        cs = slice(ci * 128, (ci + 1) * 128)
        scr[ci] = val[:, cs]
        for rr in range(4):
            o4[rr, :, cs] = scr[ci, pl.ds(rr, n // 4, stride=4), :].astype(dt)
        for rr in range(16):
            o16[rr, :, cs] = scr[ci, pl.ds(rr, n // 16, stride=16), :].astype(dt)


def _unperm_load(r4, r16, scr_a, scr_b):
    n = scr_a.shape[1]
    nc = r4.shape[-1] // 128
    for ci in range(nc):
        cs = slice(ci * 128, (ci + 1) * 128)
        for rr in range(4):
            scr_a[ci, pl.ds(rr, n // 4, stride=4), :] = r4[rr, :, cs].astype(F32)
        for rr in range(16):
            scr_b[ci, pl.ds(rr, n // 16, stride=16), :] = r16[rr, :, cs].astype(F32)
    return (jnp.concatenate([scr_a[ci] for ci in range(nc)], axis=1),
            jnp.concatenate([scr_b[ci] for ci in range(nc)], axis=1))


def _weight_gather(w_in, w_out):
    def body(win_ref, wout_ref, fin_ref, fout_ref, bin_ref, bout_ref, send_sems, recv_sems, loc_sems):
        x, y, c = lax.axis_index("x"), lax.axis_index("y"), lax.axis_index("c")
        sibling = (x, y, 1 - c)
        chips = [(1 - x, y), (x, 1 - y), (1 - x, 1 - y)]
        jm = 2 * x + y
        bin_ref[...] = win_ref[...].astype(BF16)
        bout_ref[...] = wout_ref[...].astype(BF16)

        def in_rows(px, py, half):
            return fin_ref.at[pl.ds(half * 512, 512), pl.ds((2 * px + py) * 1024, 1024)]

        def out_rows(px, py, half):
            return fout_ref.at[pl.ds((2 * px + py) * 256 + half * 128, 128), :]

        def rcopy(k, src, dst, to):
            return pltpu.make_async_remote_copy(src_ref=src, dst_ref=dst, send_sem=send_sems.at[k],
                                                recv_sem=recv_sems.at[k], device_id=to, device_id_type=MESH)

        loc_in = pltpu.make_async_copy(bin_ref, fin_ref.at[:, pl.ds(jm * 1024, 1024)], loc_sems.at[0])
        loc_out = pltpu.make_async_copy(bout_ref, fout_ref.at[pl.ds(jm * 256, 256), :], loc_sems.at[1])
        loc_in.start()
        loc_out.start()
        first = []
        for k, chip in enumerate(chips):
            first.append(rcopy(k, bin_ref.at[pl.ds(c * 512, 512), :], in_rows(x, y, c), (*chip, c)))
            first.append(rcopy(3 + k, bout_ref.at[pl.ds(c * 128, 128), :], out_rows(x, y, c), (*chip, c)))
        for cp in first:
            cp.start()
        passed = []
        for k, chip in enumerate(chips):
            rcopy(k, in_rows(*chip, c), in_rows(*chip, c), (x, y, c)).wait_recv()
            p1 = rcopy(6 + k, in_rows(*chip, c), in_rows(*chip, c), sibling)
            p1.start()
            rcopy(3 + k, out_rows(*chip, c), out_rows(*chip, c), (x, y, c)).wait_recv()
            p2 = rcopy(9 + k, out_rows(*chip, c), out_rows(*chip, c), sibling)
            p2.start()
            passed += [p1, p2]
        for k, chip in enumerate(chips):
            rcopy(6 + k, in_rows(*chip, 1 - c), in_rows(*chip, 1 - c), (x, y, c)).wait_recv()
            rcopy(9 + k, out_rows(*chip, 1 - c), out_rows(*chip, 1 - c), (x, y, c)).wait_recv()
        for cp in first + passed:
            cp.wait_send()
        loc_in.wait()
        loc_out.wait()

    return pl.pallas_call(
        body, name="weight_gather",
        out_shape=(jax.ShapeDtypeStruct((D, NCOL), BF16), jax.ShapeDtypeStruct((D, D), BF16)),
        in_specs=[pl.BlockSpec(memory_space=pltpu.VMEM), pl.BlockSpec(memory_space=pltpu.VMEM)],
        out_specs=(pl.BlockSpec(memory_space=pltpu.HBM), pl.BlockSpec(memory_space=pltpu.HBM)),
        scratch_shapes=[pltpu.VMEM((D, 1024), BF16), pltpu.VMEM((256, D), BF16),
                        pltpu.SemaphoreType.DMA((12,)), pltpu.SemaphoreType.DMA((12,)),
                        pltpu.SemaphoreType.DMA((2,))],
        compiler_params=_cp(),
    )(w_in, w_out)


def _fwd_in(x, pos, mixw, w_full):
    TT = 512

    def body(x_ref, pos_ref, mw_ref, w_ref, hn_ref, q1, k1, v1, q4, k4, v4, q16, k16, v16,
             ag, hq, hf, hi, hg, scr):
        xv = x_ref[...]
        r = lax.rsqrt(jnp.mean(xv * xv, axis=-1, keepdims=True) + EPS)
        hn = ((xv * r) * mw_ref[...]).astype(BF16)
        hn_ref[...] = hn
        cosf, s1, s2 = _rope_tables(pos_ref[...])

        def proj(g):
            return _mm(hn, w_ref[:, g * 512:(g + 1) * 512])

        def emit(val, o1, o4, o16):
            _perm_store(val, scr, o1, o4, o16, BF16)

        emit(_rope(proj(0), cosf, s1, s2), q1, q4, q16)
        emit(_rope(proj(1), cosf, s1, s2), k1, k4, k16)
        emit(proj(2), v1, v4, v16)
        ag[...] = proj(3)
        hq[...] = proj(4)
        hf[...] = proj(5)
        hi[...] = proj(6).astype(BF16)
        hg[...] = proj(7)

    tok = lambda w: pl.BlockSpec((TT, w), lambda i: (i, 0))
    d4 = pl.BlockSpec((4, TT // 4, AW), lambda i: (0, i, 0))
    d16 = pl.BlockSpec((16, TT // 16, AW), lambda i: (0, i, 0))
    sd = lambda shape, dt: jax.ShapeDtypeStruct(shape, dt)
    return pl.pallas_call(
        body, name="fwd_in", grid=(T // TT,),
        in_specs=[tok(D), tok(1), pl.BlockSpec((1, D), lambda i: (0, 0)),
                  pl.BlockSpec((D, NCOL), lambda i: (0, 0))],
        out_specs=[tok(D)] + [tok(AW)] * 3 + [d4] * 3 + [d16] * 3 + [tok(AW)] * 5,
        out_shape=[sd((T, D), BF16)] + [sd((T, AW), BF16)] * 3 + [sd((4, T // 4, AW), BF16)] * 3
        + [sd((16, T // 16, AW), BF16)] * 3
        + [sd((T, AW), F32), sd((T, AW), F32), sd((T, AW), F32), sd((T, AW), BF16), sd((T, AW), F32)],
        scratch_shapes=[pltpu.VMEM((4, TT, 128), F32)],
        compiler_params=_cp(("parallel",)),
    )(x, pos, mixw, w_full)


def _band_mask(key_axis, nkeys=2 * BLK):
    shape = (nkeys, 2 * BLK) if key_axis == 0 else (2 * BLK, nkeys)
    kj = lax.broadcasted_iota(jnp.int32, shape, key_axis)
    qi = lax.broadcasted_iota(jnp.int32, shape, 1 - key_axis) & (BLK - 1)
    return (kj >= qi) & (kj <= qi + BLK), kj, qi


def _stack_heads(t2, in_a):
    z = jnp.zeros_like(t2)
    return jnp.concatenate([jnp.where(in_a[0], t2, z), jnp.where(in_a[1], t2, z)], axis=0)


def _attn_fwd(q, k, v, nb, name):
    n = min(4, nb)
    CH = n * BLK
    halo = nb > n

    def body(*refs):
        if halo:
            q_ref, k_ref, v_ref, kp_ref, vp_ref, o_ref, lse_ref = refs
        else:
            q_ref, k_ref, v_ref, o_ref, lse_ref = refs
        lane = lax.broadcasted_iota(jnp.int32, (1, 128), 1)
        in_a = [lane < HEAD, lane >= HEAD]
        band, kj, _ = _band_mask(1)
        thr0 = jnp.where((n * pl.program_id(0)) % nb == 0, BLK, 0) if halo else BLK
        mask0 = band & (kj >= thr0)
        for b in range(n):
            rs = slice(b * BLK, (b + 1) * BLK)
            stat = jnp.zeros((BLK, 128), F32)
            for hp in range(4):
                cs = slice(hp * 128, (hp + 1) * 128)
                q2s = _stack_heads(q_ref[rs, cs], in_a)
                if b == 0:
                    kprev = kp_ref[:, cs] if halo else k_ref[rs, cs]
                    vprev = vp_ref[:, cs] if halo else v_ref[rs, cs]
                    kk = jnp.concatenate([kprev, k_ref[rs, cs]], axis=0)
                    vv = jnp.concatenate([vprev, v_ref[rs, cs]], axis=0)
                    mask = mask0
                else:
                    kk = k_ref[(b - 1) * BLK:(b + 1) * BLK, cs]
                    vv = v_ref[(b - 1) * BLK:(b + 1) * BLK, cs]
                    mask = band
                s = jnp.where(mask, _mm_nt(q2s, kk) * SCALE, NEG)
                m = jnp.max(s, axis=-1, keepdims=True)
                p = jnp.exp(s - m)
                l = jnp.sum(p, axis=-1, keepdims=True)
                o = _mm(p.astype(BF16), vv) / l
                lse = m + jnp.log(l)
                o_ref[rs, cs] = jnp.where(in_a[0], o[:BLK], o[BLK:])
                stat = jnp.where(lane == 2 * hp, lse[:BLK], stat)
                stat = jnp.where(lane == 2 * hp + 1, lse[BLK:], stat)
            lse_ref[rs, :] = stat

    cur = pl.BlockSpec((CH, AW), lambda i: (i, 0))
    prev = pl.BlockSpec((BLK, AW), lambda i: (jnp.maximum(n * i - 1, 0), 0))
    return pl.pallas_call(
        body, name=name, grid=(T // CH,),
        in_specs=[cur, cur, cur] + ([prev, prev] if halo else []),
        out_specs=[cur, pl.BlockSpec((CH, 128), lambda i: (i, 0))],
        out_shape=[jax.ShapeDtypeStruct((T, AW), F32), jax.ShapeDtypeStruct((T, 128), F32)],
        compiler_params=_cp(("parallel",)),
    )(*((q, k, v) + ((k, v) if halo else ())))


def _attn_bwd(q, k, v, do, st, nb, name):
    n = min(4, nb)
    CH = n * BLK
    NBLK = T // BLK
    halo = nb > n

    def body(*refs):
        if halo:
            (q_ref, k_ref, v_ref, do_ref, st_ref, kp_ref, vp_ref, qn_ref, don_ref, stn_ref,
             dq_ref, dk_ref, dv_ref) = refs
        else:
            q_ref, k_ref, v_ref, do_ref, st_ref, dq_ref, dk_ref, dv_ref = refs
        i = pl.program_id(0)
        lane = lax.broadcasted_iota(jnp.int32, (1, 128), 1)
        in_a = [lane < HEAD, lane >= HEAD]
        band, kj, _ = _band_mask(0)
        thr0 = jnp.where((n * i) % nb == 0, BLK, 0) if halo else BLK
        mask0 = band & (kj >= thr0)

        def stat_rows(st_t, hp):
            lse_r = jnp.concatenate([st_t[2 * hp:2 * hp + 1, :], st_t[2 * hp + 1:2 * hp + 2, :]], axis=1)
            dl_r = jnp.concatenate([st_t[8 + 2 * hp:9 + 2 * hp, :], st_t[9 + 2 * hp:10 + 2 * hp, :]], axis=1)
            return lse_r, dl_r

        st_t = [st_ref[b * BLK:(b + 1) * BLK, :].T for b in range(n)]
        if halo:
            nxt_thr = jnp.where((n * i + n) % nb == 0, 2 * BLK, 0)
            _, kj1, qi1 = _band_mask(0, BLK)
            mask_next = kj1 >= qi1 + nxt_thr
            stn_t = stn_ref[...].T

        for hp in range(4):
            cs = slice(hp * 128, (hp + 1) * 128)
            kb = [k_ref[b * BLK:(b + 1) * BLK, cs] for b in range(n)]
            vb = [v_ref[b * BLK:(b + 1) * BLK, cs] for b in range(n)]
            dk_acc = [jnp.zeros((BLK, 128), F32) for _ in range(n)]
            dv_acc = [jnp.zeros((BLK, 128), F32) for _ in range(n)]
            for b in range(n):
                rs = slice(b * BLK, (b + 1) * BLK)
                q2s = _stack_heads(q_ref[rs, cs], in_a)
                do2s = _stack_heads(do_ref[rs, cs], in_a)
                if b == 0:
                    kprev = kp_ref[:, cs] if halo else kb[0]
                    vprev = vp_ref[:, cs] if halo else vb[0]
                    mask = mask0
                else:
                    kprev, vprev, mask = kb[b - 1], vb[b - 1], band
                kk = jnp.concatenate([kprev, kb[b]], axis=0)
                vv = jnp.concatenate([vprev, vb[b]], axis=0)
                lse_r, dl_r = stat_rows(st_t[b], hp)
                s_t = jnp.where(mask, _mm_nt(kk, q2s) * SCALE, NEG)
                p_t = jnp.exp(s_t - lse_r)
                ds_t = (p_t * (_mm_nt(vv, do2s) - dl_r)).astype(BF16)
                dkk = _mm(ds_t, q2s) * SCALE
                dvv = _mm(p_t.astype(BF16), do2s)
                dqs = _mm_tn(ds_t, kk) * SCALE
                dq_ref[rs, cs] = jnp.where(in_a[0], dqs[:BLK], dqs[BLK:]).astype(BF16)
                dk_acc[b] += dkk[BLK:]
                dv_acc[b] += dvv[BLK:]
                if b > 0:
                    dk_acc[b - 1] += dkk[:BLK]
                    dv_acc[b - 1] += dvv[:BLK]
            if halo:
                q2s = _stack_heads(qn_ref[:, cs], in_a)
                do2s = _stack_heads(don_ref[:, cs], in_a)
                lse_r, dl_r = stat_rows(stn_t, hp)
                s_t = jnp.where(mask_next, _mm_nt(kb[n - 1], q2s) * SCALE, NEG)
                p_t = jnp.exp(s_t - lse_r)
                ds_t = (p_t * (_mm_nt(vb[n - 1], do2s) - dl_r)).astype(BF16)
                dk_acc[n - 1] += _mm(ds_t, q2s) * SCALE
                dv_acc[n - 1] += _mm(p_t.astype(BF16), do2s)
            for b in range(n):
                dk_ref[b * BLK:(b + 1) * BLK, cs] = dk_acc[b].astype(BF16)
                dv_ref[b * BLK:(b + 1) * BLK, cs] = dv_acc[b].astype(BF16)

    cur = pl.BlockSpec((CH, AW), lambda i: (i, 0))
    cur_st = pl.BlockSpec((CH, 128), lambda i: (i, 0))
    prev = pl.BlockSpec((BLK, AW), lambda i: (jnp.maximum(n * i - 1, 0), 0))
    nxt = pl.BlockSpec((BLK, AW), lambda i: (jnp.minimum(n * i + n, NBLK - 1), 0))
    nxt_st = pl.BlockSpec((BLK, 128), lambda i: (jnp.minimum(n * i + n, NBLK - 1), 0))
    ins = [cur] * 4 + [cur_st] + ([prev, prev, nxt, nxt, nxt_st] if halo else [])
    args = (q, k, v, do, st) + ((k, v, q, do, st) if halo else ())
    return pl.pallas_call(
        body, name=name, grid=(T // CH,),
        in_specs=ins,
        out_specs=[cur] * 3,
        out_shape=[jax.ShapeDtypeStruct((T, AW), BF16)] * 3,
        compiler_params=_cp(("parallel",)),
    )(*args)


TH = 256
NCH = TH // CHUNK


def _hgrn_common(hq_ref, hf_ref, lbr_ref, tri_ref):
    r0 = lbr_ref[0:1, :]
    r1 = lbr_ref[1:2, :]
    mx = jnp.maximum(r0, r1)
    e0 = jnp.exp(r0 - mx)
    e1 = jnp.exp(r1 - mx)
    lb = e0 / (e0 + e1)
    hqv = hq_ref[...]
    sq = _sigmoid(hqv)
    qv = hqv * sq
    sf = _sigmoid(hf_ref[...])
    f = lb + (1.0 - lb) * sf
    kv = 1.0 - f
    g = jnp.log(f)
    cum = _mm_exact_l(tri_ref[...], g)
    lastb = jnp.concatenate(
        [jnp.broadcast_to(cum[c * CHUNK + CHUNK - 1:(c + 1) * CHUNK, :], (CHUNK, HW)) for c in range(NCH)], axis=0)
    ea = jnp.exp(cum)
    ena = jnp.exp(-cum)
    eend = jnp.exp(lastb - cum)
    return dict(lb=lb, hq=hqv, sq=sq, q=qv, sf=sf, f=f, k=kv, cum=cum, lastb=lastb, ea=ea, ena=ena, eend=eend,
                qd=qv * ea, ki=kv * ena, ke=kv * eend, dec=jnp.exp(lastb))


def _tri_mask():
    ti = lax.broadcasted_iota(jnp.int32, (CHUNK, CHUNK), 0)
    si = lax.broadcasted_iota(jnp.int32, (CHUNK, CHUNK), 1)
    return si <= ti


def _hgrn_fwd(hq, hf, hi, lbr, tri):
    def body(hq_ref, hf_ref, hi_ref, lbr_ref, tri_ref, rec_ref, sall_ref, st_scr):
        @pl.when(pl.program_id(0) == 0)
        def _():
            st_scr[...] = jnp.zeros_like(st_scr)

        w = _hgrn_common(hq_ref, hf_ref, lbr_ref, tri_ref)
        qd, ki, ke = w["qd"].astype(BF16), w["ki"].astype(BF16), w["ke"].astype(BF16)
        dec = w["dec"]
        vb = hi_ref[...]
        causal = _tri_mask()
        for c in range(NCH):
            rs = slice(c * CHUNK, (c + 1) * CHUNK)
            sall_ref[c] = st_scr[...]
            for h in range(4):
                cs = slice(h * 128, (h + 1) * 128)
                st = st_scr[:, cs]
                att = jnp.where(causal, _mm_nt(qd[rs, cs], ki[rs, cs]), 0.0)
                o = _mm(att.astype(BF16), vb[rs, cs]) + _mm_nt(qd[rs, cs], st.astype(BF16))
                rec_ref[rs, cs] = o
                st_scr[:, cs] = dec[c * CHUNK:c * CHUNK + 1, cs] * st + _mm_tn(vb[rs, cs], ke[rs, cs])

    tok = pl.BlockSpec((TH, HW), lambda i: (i, 0))
    return pl.pallas_call(
        body, name="hgrn_fwd", grid=(T // TH,),
        in_specs=[tok, tok, tok, pl.BlockSpec((2, HW), lambda i: (0, 0)), pl.BlockSpec((TH, TH), lambda i: (0, 0))],
        out_specs=[tok, pl.BlockSpec((NCH, 128, HW), lambda i: (i, 0, 0))],
        out_shape=[jax.ShapeDtypeStruct((T, HW), F32), jax.ShapeDtypeStruct((T // CHUNK, 128, HW), F32)],
        scratch_shapes=[pltpu.VMEM((128, HW), F32)],
        compiler_params=_cp(("arbitrary",)),
    )(hq, hf, hi, lbr, tri)


def _hgrn_bwd(hq, hf, hi, lbr, tri, trit, drec, sall):
    NT = T // TH

    def body(hq_ref, hf_ref, hi_ref, lbr_ref, tri_ref, trit_ref, do_ref, sall_ref,
             dhq_ref, dhf_ref, dhi_ref, small_ref, dst_scr, dlb_scr, dqd_scr, dki_scr, dke_scr, dlast_scr):
        step = pl.program_id(0)

        @pl.when(step == 0)
        def _():
            dst_scr[...] = jnp.zeros_like(dst_scr)
            dlb_scr[...] = jnp.zeros_like(dlb_scr)

        w = _hgrn_common(hq_ref, hf_ref, lbr_ref, tri_ref)
        qd, ki, ke = w["qd"].astype(BF16), w["ki"].astype(BF16), w["ke"].astype(BF16)
        dec = w["dec"]
        vb = hi_ref[...]
        dob = do_ref[...].astype(BF16)
        causal = _tri_mask()
        for c in reversed(range(NCH)):
            rs = slice(c * CHUNK, (c + 1) * CHUNK)
            dec_c = dec[c * CHUNK:c * CHUNK + 1, :]
            for h in range(4):
                cs = slice(h * 128, (h + 1) * 128)
                st = sall_ref[c, :, cs]
                dst = dst_scr[:, cs]
                dstb = dst.astype(BF16)
                att = jnp.where(causal, _mm_nt(qd[rs, cs], ki[rs, cs]), 0.0).astype(BF16)
                datt = jnp.where(causal, _mm_nt(dob[rs, cs], vb[rs, cs]), 0.0).astype(BF16)
                dhi_ref[rs, cs] = (_mm_tn(att, dob[rs, cs]) + _mm_nt(ke[rs, cs], dstb)).astype(BF16)
                dqd_scr[rs, cs] = _mm(datt, ki[rs, cs]) + _mm(dob[rs, cs], st.astype(BF16))
                dki_scr[rs, cs] = _mm_tn(datt, qd[rs, cs])
                dke_scr[rs, cs] = _mm(vb[rs, cs], dstb)
                ddec = jnp.sum(dst * st, axis=0, keepdims=True)
                dlast_scr[c:c + 1, cs] = ddec * dec_c[:, cs]
                dst_scr[:, cs] = dec_c[:, cs] * dst + _mm_tn(dob[rs, cs], qd[rs, cs])
        dqd, dki, dke = dqd_scr[...], dki_scr[...], dke_scr[...]
        dq = dqd * w["ea"]
        dk = dki * w["ena"] + dke * w["eend"]
        dcum = dqd * w["qd"] - dki * w["ki"] - dke * w["ke"]
        dkeke = dke * w["ke"]
        dlastb = jnp.concatenate(
            [jnp.broadcast_to(dlast_scr[c:c + 1, :] + jnp.sum(dkeke[c * CHUNK:(c + 1) * CHUNK], axis=0, keepdims=True),
                              (CHUNK, HW)) for c in range(NCH)], axis=0)
        dg = _mm_exact_l(trit_ref[...], dcum) + dlastb
        df = dg / w["f"] - dk
        lb, sf, sq = w["lb"], w["sf"], w["sq"]
        dhf_ref[...] = (df * (1.0 - lb) * sf * (1.0 - sf)).astype(BF16)
        dhq_ref[...] = (dq * (sq * (1.0 + w["hq"] * (1.0 - sq)))).astype(BF16)
        dlb_scr[...] += jnp.sum(df * (1.0 - sf), axis=0, keepdims=True)

        @pl.when(step == NT - 1)
        def _():
            gr = dlb_scr[...] * lb * (1.0 - lb)
            small_ref[...] = jnp.zeros_like(small_ref)
            small_ref[0:1, 0:HW] = gr
            small_ref[1:2, 0:HW] = -gr

    tok = pl.BlockSpec((TH, HW), lambda i: (NT - 1 - i, 0))
    const = lambda shape: pl.BlockSpec(shape, lambda i: (0,) * len(shape))
    return pl.pallas_call(
        body, name="hgrn_bwd", grid=(NT,),
        in_specs=[tok, tok, tok, const((2, HW)), const((TH, TH)), const((TH, TH)), tok,
                  pl.BlockSpec((NCH, 128, HW), lambda i: (NT - 1 - i, 0, 0))],
        out_specs=[tok, tok, tok, const((8, D))],
        out_shape=[jax.ShapeDtypeStruct((T, HW), BF16)] * 3 + [jax.ShapeDtypeStruct((8, D), F32)],
        scratch_shapes=[pltpu.VMEM((128, HW), F32), pltpu.VMEM((1, HW), F32), pltpu.VMEM((TH, HW), F32),
                        pltpu.VMEM((TH, HW), F32), pltpu.VMEM((TH, HW), F32), pltpu.VMEM((8, HW), F32)],
        compiler_params=_cp(("arbitrary",)),
    )(hq, hf, hi, lbr, tri, trit, drec, sall)


def _fwd_out(o1, o4, o16, l1, l4, l16, rec, ag, hg, x, tgt, anw, hnw, fnw, wout_full, gmat, emat, selmat):
    TT = 256

    def body(o1_r, o4_r, o16_r, l1_r, l4_r, l16_r, rec_r, ag_r, hg_r, x_r, tgt_r, anw_r, hnw_r, fnw_r, wo_r, g_r,
             e_r, sel_r, dx2_o, do1_o, do4_o, do16_o, st1_o, st4_o, st16_o, drec_o, dag_o, dhg_o,
             rout_o, routb_o, small_o, scr_a, scr_b, gwout_o, rbuf, send_sems, recv_sems):
        @pl.when(pl.program_id(0) == 0)
        def _():
            gwout_o[...] = jnp.zeros_like(gwout_o)
            small_o[...] = jnp.zeros_like(small_o)

        def unperm(r4, r16):
            return _unperm_load(r4, r16, scr_a, scr_b)

        def perm_out(val, p1, p4, p16, dt):
            _perm_store(val, scr_a, p1, p4, p16, dt)

        o4u, o16u = unperm(o4_r, o16_r)
        l4c, l16c = unperm(l4_r, l16_r)
        em = e_r[...]
        l1v, l4u, l16u = _mm_exact_r(l1_r[...], em), _mm_exact_r(l4c, em), _mm_exact_r(l16c, em)
        o1v = o1_r[...]
        mx = jnp.maximum(jnp.maximum(l1v, l4u), l16u)
        w1, w4, w16 = jnp.exp(l1v - mx), jnp.exp(l4u - mx), jnp.exp(l16u - mx)
        den = w1 + w4 + w16
        attn = (w1 * o1v + w4 * o4u + w16 * o16u) / den
        lse = mx + jnp.log(den)
        gm = g_r[...]

        def head_mean_a(t):
            return _mm_exact_r(t, gm)

        def head_mean_h(t):
            return jnp.concatenate(
                [jnp.broadcast_to(jnp.mean(t[:, h * 128:(h + 1) * 128], axis=-1, keepdims=True), (TT, 128))
                 for h in range(4)], axis=1)

        rs_a = lax.rsqrt(head_mean_a(attn * attn) + EPS)
        n_a = attn * rs_a
        agv = ag_r[...]
        sg_a = _sigmoid(agv)
        si_a = agv * sg_a
        anw_v = anw_r[...]
        y_a = (n_a * anw_v) * si_a
        recv = rec_r[...]
        rs_h = lax.rsqrt(head_mean_h(recv * recv) + EPS)
        n_h = recv * rs_h
        hgv = hg_r[...]
        sg_h = _sigmoid(hgv)
        si_h = hgv * sg_h
        hnw_v = hnw_r[...]
        y_h = (n_h * hnw_v) * si_h
        mixed = jnp.concatenate([y_a, y_h], axis=1).astype(BF16)
        xv = x_r[...]
        x2 = xv + _mm(mixed, wo_r[...])
        r2 = lax.rsqrt(jnp.mean(x2 * x2, axis=-1, keepdims=True) + EPS)
        fnw_v = fnw_r[...]
        xn = x2 * r2
        err = xn * fnw_v - tgt_r[...]
        small_o[2:3, :] += 0.5 * jnp.sum(jnp.mean(err * err, axis=-1, keepdims=True), axis=0, keepdims=True)
        dy = err * (1.0 / D)
        small_o[0:1, :] += jnp.sum(dy * xn, axis=0, keepdims=True)
        dyw = dy * fnw_v
        dx2 = r2 * dyw - x2 * ((r2 * r2 * r2) * jnp.mean(dyw * x2, axis=-1, keepdims=True))
        dx2_o[...] = dx2
        dx2b = dx2.astype(BF16)
        gwout_o[...] += _mm_tn(mixed, dx2b)
        dmix = _mm_nt(dx2b, wo_r[...])
        dm_a, dm_h = dmix[:, :AW], dmix[:, AW:]
        dag_o[...] = (dm_a * (n_a * anw_v) * (sg_a * (1.0 + agv * (1.0 - sg_a)))).astype(BF16)
        dn_a = dm_a * anw_v * si_a
        small_o[1:2, 0:AW] += jnp.sum(dm_a * n_a * si_a, axis=0, keepdims=True)
        dattn = rs_a * (dn_a - n_a * head_mean_a(dn_a * n_a))
        delta = head_mean_a(dattn * attn) * float(HEAD)
        perm_out(dattn, do1_o, do4_o, do16_o, BF16)
        stats = _mm_exact_r(lse, sel_r[0]) + _mm_exact_r(delta, sel_r[1])
        perm_out(stats, st1_o, st4_o, st16_o, F32)
        dhg_o[...] = (dm_h * (n_h * hnw_v) * (sg_h * (1.0 + hgv * (1.0 - sg_h)))).astype(BF16)
        dn_h = dm_h * hnw_v * si_h
        small_o[1:2, AW:] += jnp.sum(dm_h * n_h * si_h, axis=0, keepdims=True)
        drec_o[...] = rs_h * (dn_h - n_h * head_mean_h(dn_h * n_h))

        @pl.when(pl.program_id(0) == T // TT - 1)
        def _():
            x, y, c = lax.axis_index("x"), lax.axis_index("y"), lax.axis_index("c")
            cps = [pltpu.make_async_remote_copy(
                src_ref=gwout_o.at[pl.ds(pl.multiple_of(j * 256 + (1 - c) * 128, 128), 128), :], dst_ref=rbuf.at[j],
                send_sem=send_sems.at[j], recv_sem=recv_sems.at[j], device_id=(x, y, 1 - c), device_id_type=MESH)
                for j in range(4)]
            for cp in cps:
                cp.start()
            for j, cp in enumerate(cps):
                cp.wait_recv()
                red = gwout_o[pl.ds(pl.multiple_of(j * 256 + c * 128, 128), 128), :] + rbuf[j]
                rout_o[j * 128:(j + 1) * 128, :] = red
                routb_o[j * 128:(j + 1) * 128, :] = red.astype(BF16)
            for cp in cps:
                cp.wait_send()

    tok = lambda w: pl.BlockSpec((TT, w), lambda i: (i, 0))
    d4 = pl.BlockSpec((4, TT // 4, AW), lambda i: (0, i, 0))
    d16 = pl.BlockSpec((16, TT // 16, AW), lambda i: (0, i, 0))
    const = lambda shape: pl.BlockSpec(shape, lambda i: (0,) * len(shape))
    sd = lambda shape, dt: jax.ShapeDtypeStruct(shape, dt)
    c4 = pl.BlockSpec((4, TT // 4, 128), lambda i: (0, i, 0))
    c16 = pl.BlockSpec((16, TT // 16, 128), lambda i: (0, i, 0))
    p3 = lambda w, dt: [sd((T, w), dt), sd((4, T // 4, w), dt), sd((16, T // 16, w), dt)]
    return pl.pallas_call(
        body, name="fwd_out", grid=(T // TT,),
        in_specs=[tok(AW), d4, d16, tok(128), c4, c16, tok(AW), tok(AW), tok(AW), tok(D), tok(D),
                  const((1, AW)), const((1, HW)), const((1, D)), const((D, D)), const((AW, AW)),
                  const((128, AW)), const((2, AW, 128))],
        out_specs=[tok(D)] + [tok(AW), d4, d16] + [tok(128), c4, c16] + [tok(AW)] * 3
        + [const((512, D)), const((512, D)), const((8, D))],
        out_shape=[sd((T, D), F32)] + p3(AW, BF16) + p3(128, F32)
        + [sd((T, AW), F32), sd((T, AW), BF16), sd((T, AW), BF16), sd((512, D), F32), sd((512, D), BF16),
           sd((8, D), F32)],
        scratch_shapes=[pltpu.VMEM((4, TT, 128), F32), pltpu.VMEM((4, TT, 128), F32), pltpu.VMEM((D, D), F32),
                        pltpu.VMEM((4, 128, D), F32), pltpu.SemaphoreType.DMA((4,)), pltpu.SemaphoreType.DMA((4,))],
        compiler_params=_cp(("arbitrary",)),
    )(o1, o4, o16, l1, l4, l16, rec, ag, hg, x, tgt, anw, hnw, fnw, wout_full, gmat, emat, selmat)


def _dproj_build(dq, dk, dv, dag, dhq, dhf, dhi, dhg, pos):
    TT = 256

    def body(dq1, dq4, dq16, dk1, dk4, dk16, dv1, dv4, dv16, dag_r, dhq_r, dhf_r, dhi_r, dhg_r,
             pos_r, dproj_o, scr_a, scr_b):
        def unperm_sum(r1, r4, r16):
            u4, u16 = _unperm_load(r4, r16, scr_a, scr_b)
            return r1[...] + u4 + u16

        cosf, s1, s2 = _rope_tables(pos_r[...])
        dproj_o[:, 0:512] = _rope_bwd(unperm_sum(dq1, dq4, dq16), cosf, s1, s2).astype(BF16)
        dproj_o[:, 512:1024] = _rope_bwd(unperm_sum(dk1, dk4, dk16), cosf, s1, s2).astype(BF16)
        dproj_o[:, 1024:1536] = unperm_sum(dv1, dv4, dv16).astype(BF16)
        dproj_o[:, 1536:2048] = dag_r[...]
        dproj_o[:, 2048:2560] = dhq_r[...]
        dproj_o[:, 2560:3072] = dhf_r[...]
        dproj_o[:, 3072:3584] = dhi_r[...]
        dproj_o[:, 3584:4096] = dhg_r[...]

    tok = lambda w: pl.BlockSpec((TT, w), lambda i: (i, 0))
    d4 = pl.BlockSpec((4, TT // 4, AW), lambda i: (0, i, 0))
    d16 = pl.BlockSpec((16, TT // 16, AW), lambda i: (0, i, 0))
    return pl.pallas_call(
        body, name="dproj_build", grid=(T // TT,),
        in_specs=[tok(AW), d4, d16] * 3 + [tok(AW)] * 5 + [tok(1)],
        out_specs=tok(NCOL),
        out_shape=jax.ShapeDtypeStruct((T, NCOL), BF16),
        scratch_shapes=[pltpu.VMEM((4, TT, 128), F32), pltpu.VMEM((4, TT, 128), F32)],
        compiler_params=_cp(("parallel",)),
    )(*dq, *dk, *dv, dag, dhq, dhf, dhi, dhg, pos)


def _bwd_x(dproj, x, dx2, mixw, w_full, rin, rinb, rout, routb, small4, small6):
    TT = 256
    NT = T // TT

    def body(dp_r, x_r, dx2_r, mw_r, w_r, rin_r, rinb_r, rout_r, routb_r, s4_r, s6_r,
             gx_o, pin_o, pinr_o, pout_o, poutr_o, sall_o, sbuf, send_sems, recv_sems, loc_sems):
        i = pl.program_id(0)
        loc, rem = _chip_copies(rin_r, rinb_r, rout_r, routb_r, pin_o, pinr_o, pout_o, poutr_o,
                                send_sems, recv_sems, loc_sems)

        @pl.when(i == 0)
        def _():
            sbuf[...] = jnp.zeros_like(sbuf)
            for cp in loc + rem:
                cp.start()

        dhn = _mm_nt(dp_r[...], w_r[...])
        xv = x_r[...]
        r = lax.rsqrt(jnp.mean(xv * xv, axis=-1, keepdims=True) + EPS)
        dxw = dhn * mw_r[...]
        gx_o[...] = dx2_r[...] + r * dxw - xv * ((r * r * r) * jnp.mean(dxw * xv, axis=-1, keepdims=True))
        sbuf[16:17, :] += jnp.sum(dhn * (xv * r), axis=0, keepdims=True)

        @pl.when(i == NT - 1)
        def _():
            sbuf[0:8, :] = s4_r[...]
            sbuf[8:16, :] = s6_r[...]
            sloc, srem = _small_copies(sbuf, sall_o, send_sems, recv_sems, loc_sems)
            for cp in sloc + srem:
                cp.start()
            for cp in rem + srem:
                cp.wait_recv()
            for cp in rem + srem:
                cp.wait_send()
            for cp in loc + sloc:
                cp.wait()

    tok = lambda w: pl.BlockSpec((TT, w), lambda i: (i, 0))
    const = lambda shape: pl.BlockSpec(shape, lambda i: (0,) * len(shape))
    hbm = pl.BlockSpec(memory_space=pltpu.HBM)
    return pl.pallas_call(
        body, name="bwd_x", grid=(NT,),
        in_specs=[tok(NCOL), tok(D), tok(D), const((1, D)), const((D, NCOL)), hbm, hbm, hbm, hbm,
                  const((8, D)), const((8, D))],
        out_specs=[tok(D), hbm, hbm, hbm, hbm, hbm],
        out_shape=[jax.ShapeDtypeStruct((T, D), F32),
                   jax.ShapeDtypeStruct((512, 1024), F32), jax.ShapeDtypeStruct((3, 512, 1024), BF16),
                   jax.ShapeDtypeStruct((128, D), F32), jax.ShapeDtypeStruct((3, 128, D), BF16),
                   jax.ShapeDtypeStruct((8, 24, D), F32)],
        scratch_shapes=[pltpu.VMEM((24, D), F32), pltpu.SemaphoreType.DMA((13,)), pltpu.SemaphoreType.DMA((13,)),
                        pltpu.SemaphoreType.DMA((3,))],
        compiler_params=_cp(("arbitrary",)),
    )(dproj, x, dx2, mixw, w_full, rin, rinb, rout, routb, small4, small6)


def _grad_w_in(hn, dproj):
    TK = 512
    NK = T // TK

    def body(hn_r, dp_r, rin_o, rinb_o, acc, rbuf, obuf, obufb, send_sems, recv_sems, wb_sems):
        j = pl.program_id(0)
        kk = pl.program_id(1)
        x, y, c = lax.axis_index("x"), lax.axis_index("y"), lax.axis_index("c")
        mine = pl.ds(pl.multiple_of(c * 512, 512), 512)
        theirs = pl.ds(pl.multiple_of((1 - c) * 512, 512), 512)

        def send(jj):
            return pltpu.make_async_remote_copy(
                src_ref=acc.at[jj % 2, theirs, :], dst_ref=rbuf.at[jj], send_sem=send_sems.at[jj],
                recv_sem=recv_sems.at[jj], device_id=(x, y, 1 - c), device_id_type=MESH)

        def writeback(jj):
            cols = pl.ds(jj * 1024, 1024)
            return [pltpu.make_async_copy(obuf.at[jj % 2], rin_o.at[:, cols], wb_sems.at[jj % 2]),
                    pltpu.make_async_copy(obufb.at[jj % 2], rinb_o.at[:, cols], wb_sems.at[2 + jj % 2])]

        def wait_writeback(jj):
            for cp in writeback(jj):
                cp.wait()

        def finalize(jj):
            send(jj).wait_recv()
            red = acc[jj % 2, mine, :] + rbuf[jj]
            obuf[jj % 2] = red
            obufb[jj % 2] = red.astype(BF16)
            for cp in writeback(jj):
                cp.start()

        prod = _mm_tn(hn_r[...], dp_r[...])

        @pl.when(kk == 0)
        def _():
            for jj in (2, 3):
                @pl.when(j == jj)
                def _():
                    send(jj - 2).wait_send()
            acc[j % 2] = prod

        @pl.when(kk > 0)
        def _():
            acc[j % 2] += prod

        @pl.when(kk == NK - 1)
        def _():
            for jj in range(4):
                @pl.when(j == jj)
                def _():
                    send(jj).start()
                    if jj in (1, 2):
                        finalize(jj - 1)
                    if jj == 3:
                        wait_writeback(0)
                        finalize(2)
                        wait_writeback(1)
                        finalize(3)
                        wait_writeback(2)
                        wait_writeback(3)
                        send(2).wait_send()
                        send(3).wait_send()

    hbm = pl.BlockSpec(memory_space=pltpu.HBM)
    return pl.pallas_call(
        body, name="grad_w_in", grid=(4, NK),
        in_specs=[pl.BlockSpec((TK, D), lambda j, kk: (kk, 0)), pl.BlockSpec((TK, 1024), lambda j, kk: (kk, j))],
        out_specs=[hbm, hbm],
        out_shape=[jax.ShapeDtypeStruct((512, NCOL), F32), jax.ShapeDtypeStruct((512, NCOL), BF16)],
        scratch_shapes=[pltpu.VMEM((2, D, 1024), F32), pltpu.VMEM((4, 512, 1024), F32), pltpu.VMEM((2, 512, 1024), F32),
                        pltpu.VMEM((2, 512, 1024), BF16),
                        pltpu.SemaphoreType.DMA((4,)), pltpu.SemaphoreType.DMA((4,)), pltpu.SemaphoreType.DMA((4,))],
        compiler_params=_cp(("arbitrary", "arbitrary")),
    )(hn, dproj)


def _chip_sum(own, rem, name):
    rows, cols = own.shape
    tr = min(rows, 256)

    def body(own_r, rem_r, o_r):
        acc = own_r[...]
        for s in range(3):
            acc = acc + rem_r[s].astype(F32)
        o_r[...] = acc

    return pl.pallas_call(
        body, name=name, grid=(rows // tr,),
        in_specs=[pl.BlockSpec((tr, cols), lambda i: (i, 0)), pl.BlockSpec((3, tr, cols), lambda i: (0, i, 0))],
        out_specs=pl.BlockSpec((tr, cols), lambda i: (i, 0)),
        out_shape=jax.ShapeDtypeStruct((rows, cols), F32),
        compiler_params=_cp(("parallel",)),
    )(own, rem)


def _chip_copies(rin_r, rinb_r, rout_r, routb_r, pin_o, pinr_o, pout_o, poutr_o, send_sems, recv_sems, loc_sems):
    x, y, c = lax.axis_index("x"), lax.axis_index("y"), lax.axis_index("c")
    chips = [(1 - x, y), (x, 1 - y), (1 - x, 1 - y)]
    jm = 2 * x + y
    loc = [pltpu.make_async_copy(rin_r.at[:, pl.ds(jm * 1024, 1024)], pin_o, loc_sems.at[0]),
           pltpu.make_async_copy(rout_r.at[pl.ds(jm * 128, 128), :], pout_o, loc_sems.at[1])]
    rem = []
    for k, (px, py) in enumerate(chips):
        j = 2 * px + py
        rem.append(pltpu.make_async_remote_copy(
            src_ref=rinb_r.at[:, pl.ds(j * 1024, 1024)], dst_ref=pinr_o.at[k],
            send_sem=send_sems.at[k], recv_sem=recv_sems.at[k], device_id=(px, py, c), device_id_type=MESH))
        rem.append(pltpu.make_async_remote_copy(
            src_ref=routb_r.at[pl.ds(j * 128, 128), :], dst_ref=poutr_o.at[k],
            send_sem=send_sems.at[3 + k], recv_sem=recv_sems.at[3 + k], device_id=(px, py, c),
            device_id_type=MESH))
    return loc, rem


def _small_copies(small_r, sall_o, send_sems, recv_sems, loc_sems):
    x, y, c = lax.axis_index("x"), lax.axis_index("y"), lax.axis_index("c")
    me = 4 * x + 2 * y + c
    loc = [pltpu.make_async_copy(small_r, sall_o.at[me], loc_sems.at[2])]
    rem = []
    k = 6
    for fx in range(2):
        for fy in range(2):
            for fc in range(2):
                if fx or fy or fc:
                    peer = (1 - x if fx else x, 1 - y if fy else y, 1 - c if fc else c)
                    rem.append(pltpu.make_async_remote_copy(
                        src_ref=small_r, dst_ref=sall_o.at[me], send_sem=send_sems.at[k],
                        recv_sem=recv_sems.at[k], device_id=peer, device_id_type=MESH))
                    k += 1
    return loc, rem


def _pair_share(pin, pout):
    def body(pin_r, pout_r, fin_o, fout_o, send_sems, recv_sems):
        x, y, c = lax.axis_index("x"), lax.axis_index("y"), lax.axis_index("c")
        sibling = (x, y, 1 - c)
        rem = [pltpu.make_async_remote_copy(src_ref=pin_r, dst_ref=fin_o.at[c], send_sem=send_sems.at[0],
                                            recv_sem=recv_sems.at[0], device_id=sibling, device_id_type=MESH),
               pltpu.make_async_remote_copy(src_ref=pout_r, dst_ref=fout_o.at[c], send_sem=send_sems.at[1],
                                            recv_sem=recv_sems.at[1], device_id=sibling, device_id_type=MESH)]
        for cp in rem:
            cp.start()
        fin_o[c] = pin_r[...]
        fout_o[c] = pout_r[...]
        for cp in rem:
            cp.wait_recv()
        for cp in rem:
            cp.wait_send()

    vm = pl.BlockSpec(memory_space=pltpu.VMEM)
    return pl.pallas_call(
        body, name="pair_share",
        out_shape=(jax.ShapeDtypeStruct((2, 512, 1024), F32), jax.ShapeDtypeStruct((2, 128, D), F32)),
        in_specs=[vm, vm], out_specs=(vm, vm),
        scratch_shapes=[pltpu.SemaphoreType.DMA((2,)), pltpu.SemaphoreType.DMA((2,))],
        compiler_params=_cp(),
    )(pin, pout)


def _adamw_math(w, g, m, v):
    m = B1 * m + (1.0 - B1) * g
    v = B2 * v + (1.0 - B2) * (g * g)
    m_hat = m / (1.0 - B1 ** STEP)
    v_hat = v / (1.0 - B2 ** STEP)
    delta = -LR * (m_hat / (jnp.sqrt(v_hat) + AEPS) + WD * w)
    return delta, m, v


def _adamw(w, g, m, v, name):
    rows, cols = w.shape
    tr = min(rows, 256)

    def body(w_r, g_r, m_r, v_r, d_o, m_o, v_o):
        d, mm, vv = _adamw_math(w_r[...], g_r[...], m_r[...], v_r[...])
        d_o[...] = d
        m_o[...] = mm
        v_o[...] = vv

    blk = pl.BlockSpec((tr, cols), lambda i: (i, 0))
    return pl.pallas_call(
        body, name=name, grid=(rows // tr,),
        in_specs=[blk] * 4, out_specs=[blk] * 3,
        out_shape=[jax.ShapeDtypeStruct((rows, cols), F32)] * 3,
        compiler_params=_cp(("parallel",)),
    )(w, g, m, v)


def _adamw_small(sall, params):
    def body(sall_r, *refs):
        ins, outs = refs[:15], refs[15:]
        tot = sall_r[0]
        for dv in range(1, 8):
            tot = tot + sall_r[dv]
        grads = [tot[16:17, :], tot[1:2, 0:AW], tot[1:2, AW:], tot[8:10, 0:HW], tot[0:1, :]]
        outs[0][...] = tot[2:3, 0:1]
        for p in range(5):
            w_r, m_r, v_r = ins[3 * p:3 * p + 3]
            g = grads[p]
            d, mm, vv = _adamw_math(w_r[...], g, m_r[...], v_r[...])
            outs[1 + 4 * p][...] = g
            outs[2 + 4 * p][...] = d
            outs[3 + 4 * p][...] = mm
            outs[4 + 4 * p][...] = vv

    flat = [a for p in params for a in p]
    shapes = [jax.ShapeDtypeStruct((1, 1), F32)]
    for p in params:
        shapes += [jax.ShapeDtypeStruct(p[0].shape, F32)] * 4
    vm = pl.BlockSpec(memory_space=pltpu.VMEM)
    return pl.pallas_call(
        body, name="adamw_small",
        in_specs=[vm] * 16, out_specs=[vm] * 21, out_shape=shapes,
        compiler_params=_cp(),
    )(sall, *flat)


def kernel(x, positions, w_in, w_out, mix_norm_w, attn_out_norm_w, hgrn_out_norm_w, hgrn_lb_raw, final_norm_w, loss_target, m_w_in, m_w_out, m_mix_norm_w, m_attn_out_norm_w, m_hgrn_out_norm_w, m_hgrn_lb_raw, m_final_norm_w, v_w_in, v_w_out, v_mix_norm_w, v_attn_out_norm_w, v_hgrn_out_norm_w, v_hgrn_lb_raw, v_final_norm_w):
    xs = x.reshape(T, D)
    tgt = loss_target.reshape(T, D)
    pos = positions.reshape(T, 1)
    fnw = final_norm_w.reshape(1, D)

    ti = np.arange(TH)
    tri_np = ((ti[:, None] // CHUNK == ti[None, :] // CHUNK) & (ti[None, :] <= ti[:, None])).astype(np.float32)
    tri = jnp.asarray(tri_np, BF16)
    trit = jnp.asarray(tri_np.T, BF16)
    hi_ = np.arange(AW) // HEAD
    gmat = jnp.asarray((hi_[:, None] == hi_[None, :]).astype(np.float32) / HEAD, BF16)
    emat_np = (np.arange(128)[:, None] == hi_[None, :]).astype(np.float32)
    sel_np = np.zeros((2, AW, 128), np.float32)
    sel_np[0, np.arange(8) * HEAD, np.arange(8)] = 1.0
    sel_np[1, np.arange(8) * HEAD, 8 + np.arange(8)] = 1.0
    emat = jnp.asarray(emat_np, BF16)
    selmat = jnp.asarray(sel_np, BF16)

    w_full, wout_full = _weight_gather(w_in.reshape(D, 1024), w_out.reshape(256, D))

    (hn, q1, k1, v1, q4, k4, v4, q16, k16, v16, ag, hq, hf, hi, hg) = _fwd_in(xs, pos, mix_norm_w, w_full)
    flat = lambda a: a.reshape(T, AW)
    o1, l1 = _attn_fwd(q1, k1, v1, T // BLK, "attn_fwd_d1")
    o4, l4 = _attn_fwd(flat(q4), flat(k4), flat(v4), T // 4 // BLK, "attn_fwd_d4")
    o16, l16 = _attn_fwd(flat(q16), flat(k16), flat(v16), T // 16 // BLK, "attn_fwd_d16")
    rec, sall = _hgrn_fwd(hq, hf, hi, hgrn_lb_raw, tri)

    (dx2, do1, do4, do16, st1, st4, st16, drec, dag, dhg, rout, routb, small4) = _fwd_out(
        o1, o4.reshape(4, T // 4, AW), o16.reshape(16, T // 16, AW),
        l1, l4.reshape(4, T // 4, 128), l16.reshape(16, T // 16, 128),
        rec, ag, hg, xs, tgt, attn_out_norm_w, hgrn_out_norm_w, fnw, wout_full, gmat, emat, selmat)

    fst = lambda a: a.reshape(T, 128)
    dq1, dk1, dv1 = _attn_bwd(q1, k1, v1, do1, st1, T // BLK, "attn_bwd_d1")
    dq4, dk4, dv4 = _attn_bwd(flat(q4), flat(k4), flat(v4), flat(do4), fst(st4), T // 4 // BLK, "attn_bwd_d4")
    dq16, dk16, dv16 = _attn_bwd(flat(q16), flat(k16), flat(v16), flat(do16), fst(st16), T // 16 // BLK,
                                 "attn_bwd_d16")
    dhq, dhf, dhi, small6 = _hgrn_bwd(hq, hf, hi, hgrn_lb_raw, tri, trit, drec, sall)

    r4 = lambda a: a.reshape(4, T // 4, AW)
    r16 = lambda a: a.reshape(16, T // 16, AW)
    dproj = _dproj_build((dq1, r4(dq4), r16(dq16)), (dk1, r4(dk4), r16(dk16)), (dv1, r4(dv4), r16(dv16)),
                         dag, dhq, dhf, dhi, dhg, pos)
    rin, rinb = _grad_w_in(hn, dproj)
    gx, pin_own, pin_rem, pout_own, pout_rem, small_all = _bwd_x(
        dproj, xs, dx2, mix_norm_w, w_full, rin, rinb, rout, routb, small4, small6)
    pin = _chip_sum(pin_own, pin_rem, "chip_sum_in")
    pout = _chip_sum(pout_own, pout_rem, "chip_sum_out")
    fin, fout = _pair_share(pin, pout)
    g_w_in = fin.reshape(D, 1024)
    g_w_out = fout.reshape(256, D)

    d_in, nm_in, nv_in = _adamw(w_in.reshape(D, 1024), g_w_in, m_w_in.reshape(D, 1024), v_w_in.reshape(D, 1024),
                                "adamw_w_in")
    d_out, nm_out, nv_out = _adamw(w_out.reshape(256, D), g_w_out, m_w_out.reshape(256, D), v_w_out.reshape(256, D),
                                   "adamw_w_out")
    params = [(mix_norm_w, m_mix_norm_w, v_mix_norm_w),
              (attn_out_norm_w, m_attn_out_norm_w, v_attn_out_norm_w),
              (hgrn_out_norm_w, m_hgrn_out_norm_w, v_hgrn_out_norm_w),
              (hgrn_lb_raw, m_hgrn_lb_raw, v_hgrn_lb_raw),
              (fnw, m_final_norm_w.reshape(1, D), v_final_norm_w.reshape(1, D))]
    so = _adamw_small(small_all, params)
    loss = so[0].reshape(())
    g_s = [so[1 + 4 * p] for p in range(5)]
    d_s = [so[2 + 4 * p] for p in range(5)]
    m_s = [so[3 + 4 * p] for p in range(5)]
    v_s = [so[4 + 4 * p] for p in range(5)]
    for lst in (g_s, d_s, m_s, v_s):
        lst[4] = lst[4].reshape(D)

    return (loss, gx.reshape(1, T, D),
            g_w_in.reshape(1, D, 1024), g_w_out.reshape(1, 256, D), *g_s,
            d_in.reshape(1, D, 1024), d_out.reshape(1, 256, D), *d_s,
            nm_in.reshape(1, D, 1024), nm_out.reshape(1, 256, D), *m_s,
            nv_in.reshape(1, D, 1024), nv_out.reshape(1, 256, D), *v_s)
```

```python
import functools

import numpy as np
import jax
import jax.numpy as jnp
from jax import lax
from jax.experimental import pallas as pl
from jax.experimental.pallas import tpu as pltpu

F32 = jnp.float32
BF16 = jnp.bfloat16

T = 4096
D = 1024
AW = 512
HW = 512
NCOL = 4096
HEAD = 64
BLK = 128
CHUNK = 64
EPS = 1e-6
SCALE = HEAD ** -0.5
NEG = -1e30
ROPE_THETA = 500000.0
INV_FREQ = [float(v) for v in
            (np.float32(ROPE_THETA) ** (-(np.arange(8, dtype=np.float32)) * np.float32(0.125)))]
LR, B1, B2, AEPS, WD, STEP = 0.001, 0.9, 0.999, 1e-08, 0.01, 10
VMEM_LIMIT = 56 * 1024 * 1024
MESH = pl.DeviceIdType.MESH


def _cp(sem=None, **kw):
    return pltpu.CompilerParams(dimension_semantics=sem, vmem_limit_bytes=VMEM_LIMIT, **kw)


def _mm(a, b):
    return jnp.dot(a, b, preferred_element_type=F32)


def _mm_nt(a, b):
    return lax.dot_general(a, b, (((1,), (1,)), ((), ())), preferred_element_type=F32)


def _mm_tn(a, b):
    return lax.dot_general(a, b, (((0,), (0,)), ((), ())), preferred_element_type=F32)


def _split3(x):
    h = x.astype(BF16)
    r = x - h.astype(F32)
    m = r.astype(BF16)
    l = (r - m.astype(F32)).astype(BF16)
    return h, m, l


def _mm_exact_l(mat_bf, x):
    h, m, l = _split3(x)
    return _mm(mat_bf, h) + _mm(mat_bf, m) + _mm(mat_bf, l)


def _mm_exact_r(x, mat_bf):
    h = x.astype(BF16)
    l = (x - h.astype(F32)).astype(BF16)
    return _mm(h, mat_bf) + _mm(l, mat_bf)


def _sigmoid(x):
    return 1.0 / (1.0 + jnp.exp(-x))


def _rope_tables(pos):
    lane = lax.broadcasted_iota(jnp.int32, (1, 128), 1)
    jl = lane & 63
    fi = jl & 7
    inv = jnp.zeros((1, 128), F32)
    for kk in range(8):
        inv = jnp.where(fi == kk, INV_FREQ[kk], inv)
    ang = pos.astype(F32) * inv
    c = jnp.cos(ang)
    s = jnp.sin(ang)
    cosf = jnp.where(jl < 16, c, 1.0)
    s1 = jnp.where(jl < 8, -s, 0.0)
    s2 = jnp.where((jl >= 8) & (jl < 16), s, 0.0)
    return cosf, s1, s2


def _rope(t, cosf, s1, s2):
    parts = []
    for ci in range(t.shape[1] // 128):
        tc = t[:, ci * 128:(ci + 1) * 128]
        parts.append(tc * cosf + pltpu.roll(tc, 120, 1) * s1 + pltpu.roll(tc, 8, 1) * s2)
    return jnp.concatenate(parts, axis=1)


def _rope_bwd(g, cosf, s1, s2):
    parts = []
    for ci in range(g.shape[1] // 128):
        gc = g[:, ci * 128:(ci + 1) * 128]
        parts.append(gc * cosf + pltpu.roll(gc * s1, 8, 1) + pltpu.roll(gc * s2, 120, 1))
    return jnp.concatenate(parts, axis=1)


def _perm_store(val, scr, o1, o4, o16, dt):
    n = val.shape[0]
    o1[...] = val.astype(dt)
    for ci in range(val.shape[1] // 128):
        cs = slice(ci * 128, (ci + 1) * 128)
        scr[ci] = val[:, cs]
        for rr in range(4):
            o4[rr, :, cs] = scr[ci, pl.ds(rr, n // 4, stride=4), :].astype(dt)
        for rr in range(16):
            o16[rr, :, cs] = scr[ci, pl.ds(rr, n // 16, stride=16), :].astype(dt)


def _unperm_load(r4, r16, scr_a, scr_b):
    n = scr_a.shape[1]
    nc = r4.shape[-1] // 128
    for ci in range(nc):
        cs = slice(ci * 128, (ci + 1) * 128)
        for rr in range(4):
            scr_a[ci, pl.ds(rr, n // 4, stride=4), :] = r4[rr, :, cs].astype(F32)
        for rr in range(16):
            scr_b[ci, pl.ds(rr, n // 16, stride=16), :] = r16[rr, :, cs].astype(F32)
    return (jnp.concatenate([scr_a[ci] for ci in range(nc)], axis=1),
            jnp.concatenate([scr_b[ci] for ci in range(nc)], axis=1))


def _weight_gather(w_in, w_out):
    def body(win_ref, wout_ref, fin_ref, fout_ref, bin_ref, bout_ref, send_sems, recv_sems, loc_sems):
        x, y, c = lax.axis_index("x"), lax.axis_index("y"), lax.axis_index("c")
        sibling = (x, y, 1 - c)
        chips = [(1 - x, y), (x, 1 - y), (1 - x, 1 - y)]
        jm = 2 * x + y
        bin_ref[...] = win_ref[...].astype(BF16)
        bout_ref[...] = wout_ref[...].astype(BF16)

        def in_rows(px, py, half):
            return fin_ref.at[pl.ds(half * 512, 512), pl.ds((2 * px + py) * 1024, 1024)]

        def out_rows(px, py, half):
            return fout_ref.at[pl.ds((2 * px + py) * 256 + half * 128, 128), :]

        def rcopy(k, src, dst, to):
            return pltpu.make_async_remote_copy(src_ref=src, dst_ref=dst, send_sem=send_sems.at[k],
                                                recv_sem=recv_sems.at[k], device_id=to, device_id_type=MESH)

        loc_in = pltpu.make_async_copy(bin_ref, fin_ref.at[:, pl.ds(jm * 1024, 1024)], loc_sems.at[0])
        loc_out = pltpu.make_async_copy(bout_ref, fout_ref.at[pl.ds(jm * 256, 256), :], loc_sems.at[1])
        loc_in.start()
        loc_out.start()
        first = []
        for k, chip in enumerate(chips):
            first.append(rcopy(k, bin_ref.at[pl.ds(c * 512, 512), :], in_rows(x, y, c), (*chip, c)))
            first.append(rcopy(3 + k, bout_ref.at[pl.ds(c * 128, 128), :], out_rows(x, y, c), (*chip, c)))
        for cp in first:
            cp.start()
        passed = []
        for k, chip in enumerate(chips):
            rcopy(k, in_rows(*chip, c), in_rows(*chip, c), (x, y, c)).wait_recv()
            p1 = rcopy(6 + k, in_rows(*chip, c), in_rows(*chip, c), sibling)
            p1.start()
            rcopy(3 + k, out_rows(*chip, c), out_rows(*chip, c), (x, y, c)).wait_recv()
            p2 = rcopy(9 + k, out_rows(*chip, c), out_rows(*chip, c), sibling)
            p2.start()
            passed += [p1, p2]
        for k, chip in enumerate(chips):
            rcopy(6 + k, in_rows(*chip, 1 - c), in_rows(*chip, 1 - c), (x, y, c)).wait_recv()
            rcopy(9 + k, out_rows(*chip, 1 - c), out_rows(*chip, 1 - c), (x, y, c)).wait_recv()
        for cp in first + passed:
            cp.wait_send()
        loc_in.wait()
        loc_out.wait()

    return pl.pallas_call(
        body, name="weight_gather",
        out_shape=(jax.ShapeDtypeStruct((D, NCOL), BF16), jax.ShapeDtypeStruct((D, D), BF16)),
        in_specs=[pl.BlockSpec(memory_space=pltpu.VMEM), pl.BlockSpec(memory_space=pltpu.VMEM)],
        out_specs=(pl.BlockSpec(memory_space=pltpu.HBM), pl.BlockSpec(memory_space=pltpu.HBM)),
        scratch_shapes=[pltpu.VMEM((D, 1024), BF16), pltpu.VMEM((256, D), BF16),
                        pltpu.SemaphoreType.DMA((12,)), pltpu.SemaphoreType.DMA((12,)),
                        pltpu.SemaphoreType.DMA((2,))],
        compiler_params=_cp(),
    )(w_in, w_out)


def _fwd_in(x, pos, mixw, w_full):
    TT = 512

    def body(x_ref, pos_ref, mw_ref, w_ref, hnt_ref, q1, k1, v1, q4, k4, v4, q16, k16, v16,
             ag, hq, hf, hi, hg, scr):
        xv = x_ref[...]
        r = lax.rsqrt(jnp.mean(xv * xv, axis=-1, keepdims=True) + EPS)
        hnf = (xv * r) * mw_ref[...]
        hn = hnf.astype(BF16)
        hnt_ref[...] = hnf.T.astype(BF16)
        cosf, s1, s2 = _rope_tables(pos_ref[...])

        def proj(g):
            return _mm(hn, w_ref[:, g * 512:(g + 1) * 512])

        def emit(val, o1, o4, o16):
            _perm_store(val, scr, o1, o4, o16, BF16)

        emit(_rope(proj(0), cosf, s1, s2), q1, q4, q16)
        emit(_rope(proj(1), cosf, s1, s2), k1, k4, k16)
        emit(proj(2), v1, v4, v16)
        ag[...] = proj(3)
        hq[...] = proj(4)
        hf[...] = proj(5)
        hi[...] = proj(6).astype(BF16)
        hg[...] = proj(7)

    tok = lambda w: pl.BlockSpec((TT, w), lambda i: (i, 0))
    d4 = pl.BlockSpec((4, TT // 4, AW), lambda i: (0, i, 0))
    d16 = pl.BlockSpec((16, TT // 16, AW), lambda i: (0, i, 0))
    sd = lambda shape, dt: jax.ShapeDtypeStruct(shape, dt)
    return pl.pallas_call(
        body, name="fwd_in", grid=(T // TT,),
        in_specs=[tok(D), tok(1), pl.BlockSpec((1, D), lambda i: (0, 0)),
                  pl.BlockSpec((D, NCOL), lambda i: (0, 0))],
        out_specs=[pl.BlockSpec((D, TT), lambda i: (0, i))] + [tok(AW)] * 3 + [d4] * 3 + [d16] * 3 + [tok(AW)] * 5,
        out_shape=[sd((D, T), BF16)] + [sd((T, AW), BF16)] * 3 + [sd((4, T // 4, AW), BF16)] * 3
        + [sd((16, T // 16, AW), BF16)] * 3
        + [sd((T, AW), F32), sd((T, AW), F32), sd((T, AW), F32), sd((T, AW), BF16), sd((T, AW), F32)],
        scratch_shapes=[pltpu.VMEM((4, TT, 128), F32)],
        compiler_params=_cp(("parallel",)),
    )(x, pos, mixw, w_full)


def _band_mask(key_axis, nkeys=2 * BLK):
    shape = (nkeys, 2 * BLK) if key_axis == 0 else (2 * BLK, nkeys)
    kj = lax.broadcasted_iota(jnp.int32, shape, key_axis)
    qi = lax.broadcasted_iota(jnp.int32, shape, 1 - key_axis) & (BLK - 1)
    return (kj >= qi) & (kj <= qi + BLK), kj, qi


def _stack_heads(t2, in_a):
    z = jnp.zeros_like(t2)
    return jnp.concatenate([jnp.where(in_a[0], t2, z), jnp.where(in_a[1], t2, z)], axis=0)


def _attn_fwd(q, k, v, nb, name):
    n = min(4, nb)
    CH = n * BLK
    halo = nb > n

    def body(*refs):
        if halo:
            q_ref, k_ref, v_ref, kp_ref, vp_ref, o_ref, lse_ref = refs
        else:
            q_ref, k_ref, v_ref, o_ref, lse_ref = refs
        lane = lax.broadcasted_iota(jnp.int32, (1, 128), 1)
        in_a = [lane < HEAD, lane >= HEAD]
        band, kj, _ = _band_mask(1)
        thr0 = jnp.where((n * pl.program_id(0)) % nb == 0, BLK, 0) if halo else BLK
        mask0 = band & (kj >= thr0)
        for b in range(n):
            rs = slice(b * BLK, (b + 1) * BLK)
            stat = jnp.zeros((BLK, 128), F32)
            for hp in range(4):
                cs = slice(hp * 128, (hp + 1) * 128)
                q2s = _stack_heads(q_ref[rs, cs], in_a)
                if b == 0:
                    kprev = kp_ref[:, cs] if halo else k_ref[rs, cs]
                    vprev = vp_ref[:, cs] if halo else v_ref[rs, cs]
                    kk = jnp.concatenate([kprev, k_ref[rs, cs]], axis=0)
                    vv = jnp.concatenate([vprev, v_ref[rs, cs]], axis=0)
                    mask = mask0
                else:
                    kk = k_ref[(b - 1) * BLK:(b + 1) * BLK, cs]
                    vv = v_ref[(b - 1) * BLK:(b + 1) * BLK, cs]
                    mask = band
                s = jnp.where(mask, _mm_nt(q2s, kk) * SCALE, NEG)
                m = jnp.max(s, axis=-1, keepdims=True)
                p = jnp.exp(s - m)
                l = jnp.sum(p, axis=-1, keepdims=True)
                o = _mm(p.astype(BF16), vv) / l
                lse = m + jnp.log(l)
                o_ref[rs, cs] = jnp.where(in_a[0], o[:BLK], o[BLK:])
                stat = jnp.where(lane == 2 * hp, lse[:BLK], stat)
                stat = jnp.where(lane == 2 * hp + 1, lse[BLK:], stat)
            lse_ref[rs, :] = stat

    cur = pl.BlockSpec((CH, AW), lambda i: (i, 0))
    prev = pl.BlockSpec((BLK, AW), lambda i: (jnp.maximum(n * i - 1, 0), 0))
    return pl.pallas_call(
        body, name=name, grid=(T // CH,),
        in_specs=[cur, cur, cur] + ([prev, prev] if halo else []),
        out_specs=[cur, pl.BlockSpec((CH, 128), lambda i: (i, 0))],
        out_shape=[jax.ShapeDtypeStruct((T, AW), F32), jax.ShapeDtypeStruct((T, 128), F32)],
        compiler_params=_cp(("parallel",)),
    )(*((q, k, v) + ((k, v) if halo else ())))


def _attn_bwd(q, k, v, do, st, nb, name):
    n = min(4, nb)
    CH = n * BLK
    NBLK = T // BLK
    halo = nb > n

    def body(*refs):
        if halo:
            (q_ref, k_ref, v_ref, do_ref, st_ref, kp_ref, vp_ref, qn_ref, don_ref, stn_ref,
             dq_ref, dk_ref, dv_ref) = refs
        else:
            q_ref, k_ref, v_ref, do_ref, st_ref, dq_ref, dk_ref, dv_ref = refs
        i = pl.program_id(0)
        lane = lax.broadcasted_iota(jnp.int32, (1, 128), 1)
        in_a = [lane < HEAD, lane >= HEAD]
        band, kj, _ = _band_mask(0)
        thr0 = jnp.where((n * i) % nb == 0, BLK, 0) if halo else BLK
        mask0 = band & (kj >= thr0)

        def stat_rows(st_t, hp):
            lse_r = jnp.concatenate([st_t[2 * hp:2 * hp + 1, :], st_t[2 * hp + 1:2 * hp + 2, :]], axis=1)
            dl_r = jnp.concatenate([st_t[8 + 2 * hp:9 + 2 * hp, :], st_t[9 + 2 * hp:10 + 2 * hp, :]], axis=1)
            return lse_r, dl_r

        st_t = [st_ref[b * BLK:(b + 1) * BLK, :].T for b in range(n)]
        if halo:
            nxt_thr = jnp.where((n * i + n) % nb == 0, 2 * BLK, 0)
            _, kj1, qi1 = _band_mask(0, BLK)
            mask_next = kj1 >= qi1 + nxt_thr
            stn_t = stn_ref[...].T

        for hp in range(4):
            cs = slice(hp * 128, (hp + 1) * 128)
            kb = [k_ref[b * BLK:(b + 1) * BLK, cs] for b in range(n)]
            vb = [v_ref[b * BLK:(b + 1) * BLK, cs] for b in range(n)]
            dk_acc = [jnp.zeros((BLK, 128), F32) for _ in range(n)]
            dv_acc = [jnp.zeros((BLK, 128), F32) for _ in range(n)]
            for b in range(n):
                rs = slice(b * BLK, (b + 1) * BLK)
                q2s = _stack_heads(q_ref[rs, cs], in_a)
                do2s = _stack_heads(do_ref[rs, cs], in_a)
                if b == 0:
                    kprev = kp_ref[:, cs] if halo else kb[0]
                    vprev = vp_ref[:, cs] if halo else vb[0]
                    mask = mask0
                else:
                    kprev, vprev, mask = kb[b - 1], vb[b - 1], band
                kk = jnp.concatenate([kprev, kb[b]], axis=0)
                vv = jnp.concatenate([vprev, vb[b]], axis=0)
                lse_r, dl_r = stat_rows(st_t[b], hp)
                s_t = jnp.where(mask, _mm_nt(kk, q2s) * SCALE, NEG)
                p_t = jnp.exp(s_t - lse_r)
                ds_t = (p_t * (_mm_nt(vv, do2s) - dl_r)).astype(BF16)
                dkk = _mm(ds_t, q2s) * SCALE
                dvv = _mm(p_t.astype(BF16), do2s)
                dqs = _mm_tn(ds_t, kk) * SCALE
                dq_ref[rs, cs] = jnp.where(in_a[0], dqs[:BLK], dqs[BLK:]).astype(BF16)
                dk_acc[b] += dkk[BLK:]
                dv_acc[b] += dvv[BLK:]
                if b > 0:
                    dk_acc[b - 1] += dkk[:BLK]
                    dv_acc[b - 1] += dvv[:BLK]
            if halo:
                q2s = _stack_heads(qn_ref[:, cs], in_a)
                do2s = _stack_heads(don_ref[:, cs], in_a)
                lse_r, dl_r = stat_rows(stn_t, hp)
                s_t = jnp.where(mask_next, _mm_nt(kb[n - 1], q2s) * SCALE, NEG)
                p_t = jnp.exp(s_t - lse_r)
                ds_t = (p_t * (_mm_nt(vb[n - 1], do2s) - dl_r)).astype(BF16)
                dk_acc[n - 1] += _mm(ds_t, q2s) * SCALE
                dv_acc[n - 1] += _mm(p_t.astype(BF16), do2s)
            for b in range(n):
                dk_ref[b * BLK:(b + 1) * BLK, cs] = dk_acc[b].astype(BF16)
                dv_ref[b * BLK:(b + 1) * BLK, cs] = dv_acc[b].astype(BF16)

    cur = pl.BlockSpec((CH, AW), lambda i: (i, 0))
    cur_st = pl.BlockSpec((CH, 128), lambda i: (i, 0))
    prev = pl.BlockSpec((BLK, AW), lambda i: (jnp.maximum(n * i - 1, 0), 0))
    nxt = pl.BlockSpec((BLK, AW), lambda i: (jnp.minimum(n * i + n, NBLK - 1), 0))
    nxt_st = pl.BlockSpec((BLK, 128), lambda i: (jnp.minimum(n * i + n, NBLK - 1), 0))
    ins = [cur] * 4 + [cur_st] + ([prev, prev, nxt, nxt, nxt_st] if halo else [])
    args = (q, k, v, do, st) + ((k, v, q, do, st) if halo else ())
    return pl.pallas_call(
        body, name=name, grid=(T // CH,),
        in_specs=ins,
        out_specs=[cur] * 3,
        out_shape=[jax.ShapeDtypeStruct((T, AW), BF16)] * 3,
        compiler_params=_cp(("parallel",)),
    )(*args)


TH = 256
NCH = TH // CHUNK


def _hgrn_common(hq_ref, hf_ref, lbr_ref, tri_ref):
    r0 = lbr_ref[0:1, :]
    r1 = lbr_ref[1:2, :]
    mx = jnp.maximum(r0, r1)
    e0 = jnp.exp(r0 - mx)
    e1 = jnp.exp(r1 - mx)
    lb = e0 / (e0 + e1)
    hqv = hq_ref[...]
    sq = _sigmoid(hqv)
    qv = hqv * sq
    sf = _sigmoid(hf_ref[...])
    f = lb + (1.0 - lb) * sf
    kv = 1.0 - f
    g = jnp.log(f)
    cum = _mm_exact_l(tri_ref[...], g)
    lastb = jnp.concatenate(
        [jnp.broadcast_to(cum[c * CHUNK + CHUNK - 1:(c + 1) * CHUNK, :], (CHUNK, HW)) for c in range(NCH)], axis=0)
    ea = jnp.exp(cum)
    ena = jnp.exp(-cum)
    eend = jnp.exp(lastb - cum)
    return dict(lb=lb, hq=hqv, sq=sq, q=qv, sf=sf, f=f, k=kv, cum=cum, lastb=lastb, ea=ea, ena=ena, eend=eend,
                qd=qv * ea, ki=kv * ena, ke=kv * eend, dec=jnp.exp(lastb))


def _tri_mask(transposed=False):
    ti = lax.broadcasted_iota(jnp.int32, (TH, TH), 1 if transposed else 0)
    si = lax.broadcasted_iota(jnp.int32, (TH, TH), 0 if transposed else 1)
    return (si <= ti) & ((si // CHUNK) == (ti // CHUNK))


def _hgrn_fwd(hq, hf, hi, lbr, tri):
    def body(hq_ref, hf_ref, hi_ref, lbr_ref, tri_ref, rec_ref, sall_ref, st_scr):
        @pl.when(pl.program_id(0) == 0)
        def _():
            st_scr[...] = jnp.zeros_like(st_scr)

        w = _hgrn_common(hq_ref, hf_ref, lbr_ref, tri_ref)
        qd, ki, ke = w["qd"].astype(BF16), w["ki"].astype(BF16), w["ke"].astype(BF16)
        dec = w["dec"]
        vb = hi_ref[...]
        causal = _tri_mask()
        for h in range(4):
            cs = slice(h * 128, (h + 1) * 128)
            att = jnp.where(causal, _mm_nt(qd[:, cs], ki[:, cs]), 0.0)
            o_intra = _mm(att.astype(BF16), vb[:, cs])
            for c in range(NCH):
                rs = slice(c * CHUNK, (c + 1) * CHUNK)
                st = st_scr[:, cs]
                sall_ref[c, :, cs] = st
                rec_ref[rs, cs] = o_intra[rs] + _mm_nt(qd[rs, cs], st.astype(BF16))
                st_scr[:, cs] = dec[c * CHUNK:c * CHUNK + 1, cs] * st + _mm_tn(vb[rs, cs], ke[rs, cs])

    tok = pl.BlockSpec((TH, HW), lambda i: (i, 0))
    return pl.pallas_call(
        body, name="hgrn_fwd", grid=(T // TH,),
        in_specs=[tok, tok, tok, pl.BlockSpec((2, HW), lambda i: (0, 0)), pl.BlockSpec((TH, TH), lambda i: (0, 0))],
        out_specs=[tok, pl.BlockSpec((NCH, 128, HW), lambda i: (i, 0, 0))],
        out_shape=[jax.ShapeDtypeStruct((T, HW), F32), jax.ShapeDtypeStruct((T // CHUNK, 128, HW), F32)],
        scratch_shapes=[pltpu.VMEM((128, HW), F32)],
        compiler_params=_cp(("arbitrary",)),
    )(hq, hf, hi, lbr, tri)


def _hgrn_bwd(hq, hf, hi, lbr, tri, trit, drec, sall):
    NT = T // TH

    def body(hq_ref, hf_ref, hi_ref, lbr_ref, tri_ref, trit_ref, do_ref, sall_ref,
             dhq_ref, dhf_ref, dhi_ref, small_ref, dst_scr, dlb_scr, dqd_scr, dki_scr, dke_scr, dlast_scr):
        step = pl.program_id(0)

        @pl.when(step == 0)
        def _():
            dst_scr[...] = jnp.zeros_like(dst_scr)
            dlb_scr[...] = jnp.zeros_like(dlb_scr)

        w = _hgrn_common(hq_ref, hf_ref, lbr_ref, tri_ref)
        qd, ki, ke = w["qd"].astype(BF16), w["ki"].astype(BF16), w["ke"].astype(BF16)
        dec = w["dec"]
        vb = hi_ref[...]
        dob = do_ref[...].astype(BF16)
        causal = _tri_mask()
        causal_t = _tri_mask(transposed=True)
        for h in range(4):
            cs = slice(h * 128, (h + 1) * 128)
            att_t = jnp.where(causal_t, _mm_nt(ki[:, cs], qd[:, cs]), 0.0).astype(BF16)
            datt_t = jnp.where(causal_t, _mm_nt(vb[:, cs], dob[:, cs]), 0.0).astype(BF16)
            datt = jnp.where(causal, _mm_nt(dob[:, cs], vb[:, cs]), 0.0).astype(BF16)
            dv_intra = _mm(att_t, dob[:, cs])
            dqd_intra = _mm(datt, ki[:, cs])
            dki_scr[:, cs] = _mm(datt_t, qd[:, cs])
            for c in reversed(range(NCH)):
                rs = slice(c * CHUNK, (c + 1) * CHUNK)
                dec_c = dec[c * CHUNK:c * CHUNK + 1, :]
                st = sall_ref[c, :, cs]
                dst = dst_scr[:, cs]
                dstb = dst.astype(BF16)
                dhi_ref[rs, cs] = (dv_intra[rs] + _mm_nt(ke[rs, cs], dstb)).astype(BF16)
                dqd_scr[rs, cs] = dqd_intra[rs] + _mm(dob[rs, cs], st.astype(BF16))
                dke_scr[rs, cs] = _mm(vb[rs, cs], dstb)
                ddec = jnp.sum(dst * st, axis=0, keepdims=True)
                dlast_scr[c:c + 1, cs] = ddec * dec_c[:, cs]
                dst_scr[:, cs] = dec_c[:, cs] * dst + _mm_tn(dob[rs, cs], qd[rs, cs])
        dqd, dki, dke = dqd_scr[...], dki_scr[...], dke_scr[...]
        dq = dqd * w["ea"]
        dk = dki * w["ena"] + dke * w["eend"]
        dcum = dqd * w["qd"] - dki * w["ki"] - dke * w["ke"]
        dkeke = dke * w["ke"]
        dlastb = jnp.concatenate(
            [jnp.broadcast_to(dlast_scr[c:c + 1, :] + jnp.sum(dkeke[c * CHUNK:(c + 1) * CHUNK], axis=0, keepdims=True),
                              (CHUNK, HW)) for c in range(NCH)], axis=0)
        dg = _mm_exact_l(trit_ref[...], dcum) + dlastb
        df = dg / w["f"] - dk
        lb, sf, sq = w["lb"], w["sf"], w["sq"]
        dhf_ref[...] = (df * (1.0 - lb) * sf * (1.0 - sf)).astype(BF16)
        dhq_ref[...] = (dq * (sq * (1.0 + w["hq"] * (1.0 - sq)))).astype(BF16)
        dlb_scr[...] += jnp.sum(df * (1.0 - sf), axis=0, keepdims=True)

        @pl.when(step == NT - 1)
        def _():
            gr = dlb_scr[...] * lb * (1.0 - lb)
            small_ref[...] = jnp.zeros_like(small_ref)
            small_ref[0:1, 0:HW] = gr
            small_ref[1:2, 0:HW] = -gr

    tok = pl.BlockSpec((TH, HW), lambda i: (NT - 1 - i, 0))
    const = lambda shape: pl.BlockSpec(shape, lambda i: (0,) * len(shape))
    return pl.pallas_call(
        body, name="hgrn_bwd", grid=(NT,),
        in_specs=[tok, tok, tok, const((2, HW)), const((TH, TH)), const((TH, TH)), tok,
                  pl.BlockSpec((NCH, 128, HW), lambda i: (NT - 1 - i, 0, 0))],
        out_specs=[tok, tok, tok, const((8, D))],
        out_shape=[jax.ShapeDtypeStruct((T, HW), BF16)] * 3 + [jax.ShapeDtypeStruct((8, D), F32)],
        scratch_shapes=[pltpu.VMEM((128, HW), F32), pltpu.VMEM((1, HW), F32), pltpu.VMEM((TH, HW), F32),
                        pltpu.VMEM((TH, HW), F32), pltpu.VMEM((TH, HW), F32), pltpu.VMEM((8, HW), F32)],
        compiler_params=_cp(("arbitrary",)),
    )(hq, hf, hi, lbr, tri, trit, drec, sall)


def _fwd_out(o1, o4, o16, l1, l4, l16, rec, ag, hg, x, tgt, anw, hnw, fnw, wout_full, gmat, emat, selmat):
    TT = 256

    def body(o1_r, o4_r, o16_r, l1_r, l4_r, l16_r, rec_r, ag_r, hg_r, x_r, tgt_r, anw_r, hnw_r, fnw_r, wo_r, g_r,
             e_r, sel_r, dx2_o, do1_o, do4_o, do16_o, st1_o, st4_o, st16_o, drec_o, dag_o, dhg_o,
             rout_o, routb_o, small_o, scr_a, scr_b, gwout_o, rbuf, send_sems, recv_sems):
        @pl.when(pl.program_id(0) == 0)
        def _():
            gwout_o[...] = jnp.zeros_like(gwout_o)
            small_o[...] = jnp.zeros_like(small_o)

        def unperm(r4, r16):
            return _unperm_load(r4, r16, scr_a, scr_b)

        def perm_out(val, p1, p4, p16, dt):
            _perm_store(val, scr_a, p1, p4, p16, dt)

        o4u, o16u = unperm(o4_r, o16_r)
        l4c, l16c = unperm(l4_r, l16_r)
        em = e_r[...]
        l1v, l4u, l16u = _mm_exact_r(l1_r[...], em), _mm_exact_r(l4c, em), _mm_exact_r(l16c, em)
        o1v = o1_r[...]
        mx = jnp.maximum(jnp.maximum(l1v, l4u), l16u)
        w1, w4, w16 = jnp.exp(l1v - mx), jnp.exp(l4u - mx), jnp.exp(l16u - mx)
        den = w1 + w4 + w16
        attn = (w1 * o1v + w4 * o4u + w16 * o16u) / den
        lse = mx + jnp.log(den)
        gm = g_r[...]

        def head_mean_a(t):
            return _mm_exact_r(t, gm)

        def head_mean_h(t):
            return jnp.concatenate(
                [jnp.broadcast_to(jnp.mean(t[:, h * 128:(h + 1) * 128], axis=-1, keepdims=True), (TT, 128))
                 for h in range(4)], axis=1)

        rs_a = lax.rsqrt(head_mean_a(attn * attn) + EPS)
        n_a = attn * rs_a
        agv = ag_r[...]
        sg_a = _sigmoid(agv)
        si_a = agv * sg_a
        anw_v = anw_r[...]
        y_a = (n_a * anw_v) * si_a
        recv = rec_r[...]
        rs_h = lax.rsqrt(head_mean_h(recv * recv) + EPS)
        n_h = recv * rs_h
        hgv = hg_r[...]
        sg_h = _sigmoid(hgv)
        si_h = hgv * sg_h
        hnw_v = hnw_r[...]
        y_h = (n_h * hnw_v) * si_h
        mixed = jnp.concatenate([y_a, y_h], axis=1).astype(BF16)
        xv = x_r[...]
        x2 = xv + _mm(mixed, wo_r[...])
        r2 = lax.rsqrt(jnp.mean(x2 * x2, axis=-1, keepdims=True) + EPS)
        fnw_v = fnw_r[...]
        xn = x2 * r2
        err = xn * fnw_v - tgt_r[...]
        small_o[2:3, :] += 0.5 * jnp.sum(jnp.mean(err * err, axis=-1, keepdims=True), axis=0, keepdims=True)
        dy = err * (1.0 / D)
        small_o[0:1, :] += jnp.sum(dy * xn, axis=0, keepdims=True)
        dyw = dy * fnw_v
        dx2 = r2 * dyw - x2 * ((r2 * r2 * r2) * jnp.mean(dyw * x2, axis=-1, keepdims=True))
        dx2_o[...] = dx2
        dx2b = dx2.astype(BF16)
        gwout_o[...] += _mm_tn(mixed, dx2b)
        dmix = _mm_nt(dx2b, wo_r[...])
        dm_a, dm_h = dmix[:, :AW], dmix[:, AW:]
        dag_o[...] = (dm_a * (n_a * anw_v) * (sg_a * (1.0 + agv * (1.0 - sg_a)))).astype(BF16)
        dn_a = dm_a * anw_v * si_a
        small_o[1:2, 0:AW] += jnp.sum(dm_a * n_a * si_a, axis=0, keepdims=True)
        dattn = rs_a * (dn_a - n_a * head_mean_a(dn_a * n_a))
        delta = head_mean_a(dattn * attn) * float(HEAD)
        perm_out(dattn, do1_o, do4_o, do16_o, BF16)
        stats = _mm_exact_r(lse, sel_r[0]) + _mm_exact_r(delta, sel_r[1])
        perm_out(stats, st1_o, st4_o, st16_o, F32)
        dhg_o[...] = (dm_h * (n_h * hnw_v) * (sg_h * (1.0 + hgv * (1.0 - sg_h)))).astype(BF16)
        dn_h = dm_h * hnw_v * si_h
        small_o[1:2, AW:] += jnp.sum(dm_h * n_h * si_h, axis=0, keepdims=True)
        drec_o[...] = rs_h * (dn_h - n_h * head_mean_h(dn_h * n_h))

        @pl.when(pl.program_id(0) == T // TT - 1)
        def _():
            x, y, c = lax.axis_index("x"), lax.axis_index("y"), lax.axis_index("c")
            cps = [pltpu.make_async_remote_copy(
                src_ref=gwout_o.at[pl.ds(pl.multiple_of(j * 256 + (1 - c) * 128, 128), 128), :], dst_ref=rbuf.at[j],
                send_sem=send_sems.at[j], recv_sem=recv_sems.at[j], device_id=(x, y, 1 - c), device_id_type=MESH)
                for j in range(4)]
            for cp in cps:
                cp.start()
            for j, cp in enumerate(cps):
                cp.wait_recv()
                red = gwout_o[pl.ds(pl.multiple_of(j * 256 + c * 128, 128), 128), :] + rbuf[j]
                rout_o[j * 128:(j + 1) * 128, :] = red
                routb_o[j * 128:(j + 1) * 128, :] = red.astype(BF16)
            for cp in cps:
                cp.wait_send()

    tok = lambda w: pl.BlockSpec((TT, w), lambda i: (i, 0))
    d4 = pl.BlockSpec((4, TT // 4, AW), lambda i: (0, i, 0))
    d16 = pl.BlockSpec((16, TT // 16, AW), lambda i: (0, i, 0))
    const = lambda shape: pl.BlockSpec(shape, lambda i: (0,) * len(shape))
    sd = lambda shape, dt: jax.ShapeDtypeStruct(shape, dt)
    c4 = pl.BlockSpec((4, TT // 4, 128), lambda i: (0, i, 0))
    c16 = pl.BlockSpec((16, TT // 16, 128), lambda i: (0, i, 0))
    p3 = lambda w, dt: [sd((T, w), dt), sd((4, T // 4, w), dt), sd((16, T // 16, w), dt)]
    return pl.pallas_call(
        body, name="fwd_out", grid=(T // TT,),
        in_specs=[tok(AW), d4, d16, tok(128), c4, c16, tok(AW), tok(AW), tok(AW), tok(D), tok(D),
                  const((1, AW)), const((1, HW)), const((1, D)), const((D, D)), const((AW, AW)),
                  const((128, AW)), const((2, AW, 128))],
        out_specs=[tok(D)] + [tok(AW), d4, d16] + [tok(128), c4, c16] + [tok(AW)] * 3
        + [const((512, D)), const((512, D)), const((8, D))],
        out_shape=[sd((T, D), F32)] + p3(AW, BF16) + p3(128, F32)
        + [sd((T, AW), F32), sd((T, AW), BF16), sd((T, AW), BF16), sd((512, D), F32), sd((512, D), BF16),
           sd((8, D), F32)],
        scratch_shapes=[pltpu.VMEM((4, TT, 128), F32), pltpu.VMEM((4, TT, 128), F32), pltpu.VMEM((D, D), F32),
                        pltpu.VMEM((4, 128, D), F32), pltpu.SemaphoreType.DMA((4,)), pltpu.SemaphoreType.DMA((4,))],
        compiler_params=_cp(("arbitrary",)),
    )(o1, o4, o16, l1, l4, l16, rec, ag, hg, x, tgt, anw, hnw, fnw, wout_full, gmat, emat, selmat)


def _dproj_build(dq, dk, dv, dag, dhq, dhf, dhi, dhg, pos):
    TT = 256

    def body(dq1, dq4, dq16, dk1, dk4, dk16, dv1, dv4, dv16, dag_r, dhq_r, dhf_r, dhi_r, dhg_r,
             pos_r, dproj_o, scr_a, scr_b):
        def unperm_sum(r1, r4, r16):
            u4, u16 = _unperm_load(r4, r16, scr_a, scr_b)
            return r1[...] + u4 + u16

        cosf, s1, s2 = _rope_tables(pos_r[...])
        dproj_o[:, 0:512] = _rope_bwd(unperm_sum(dq1, dq4, dq16), cosf, s1, s2).astype(BF16)
        dproj_o[:, 512:1024] = _rope_bwd(unperm_sum(dk1, dk4, dk16), cosf, s1, s2).astype(BF16)
        dproj_o[:, 1024:1536] = unperm_sum(dv1, dv4, dv16).astype(BF16)
        dproj_o[:, 1536:2048] = dag_r[...]
        dproj_o[:, 2048:2560] = dhq_r[...]
        dproj_o[:, 2560:3072] = dhf_r[...]
        dproj_o[:, 3072:3584] = dhi_r[...]
        dproj_o[:, 3584:4096] = dhg_r[...]

    tok = lambda w: pl.BlockSpec((TT, w), lambda i: (i, 0))
    d4 = pl.BlockSpec((4, TT // 4, AW), lambda i: (0, i, 0))
    d16 = pl.BlockSpec((16, TT // 16, AW), lambda i: (0, i, 0))
    return pl.pallas_call(
        body, name="dproj_build", grid=(T // TT,),
        in_specs=[tok(AW), d4, d16] * 3 + [tok(AW)] * 5 + [tok(1)],
        out_specs=tok(NCOL),
        out_shape=jax.ShapeDtypeStruct((T, NCOL), BF16),
        scratch_shapes=[pltpu.VMEM((4, TT, 128), F32), pltpu.VMEM((4, TT, 128), F32)],
        compiler_params=_cp(("parallel",)),
    )(*dq, *dk, *dv, dag, dhq, dhf, dhi, dhg, pos)


def _bwd_x(dproj, x, dx2, mixw, w_full, rin, rinb, rout, routb, small4, small6):
    TT = 256
    NT = T // TT

    def body(dp_r, x_r, dx2_r, mw_r, w_r, rin_r, rinb_r, rout_r, routb_r, s4_r, s6_r,
             gx_o, pin_o, pinr_o, pout_o, poutr_o, sall_o, sbuf, send_sems, recv_sems, loc_sems):
        i = pl.program_id(0)
        loc, rem = _chip_copies(rin_r, rinb_r, rout_r, routb_r, pin_o, pinr_o, pout_o, poutr_o,
                                send_sems, recv_sems, loc_sems)

        @pl.when(i == 0)
        def _():
            sbuf[...] = jnp.zeros_like(sbuf)
            for cp in loc + rem:
                cp.start()

        dhn = _mm_nt(dp_r[...], w_r[...])
        xv = x_r[...]
        r = lax.rsqrt(jnp.mean(xv * xv, axis=-1, keepdims=True) + EPS)
        dxw = dhn * mw_r[...]
        gx_o[...] = dx2_r[...] + r * dxw - xv * ((r * r * r) * jnp.mean(dxw * xv, axis=-1, keepdims=True))
        sbuf[16:17, :] += jnp.sum(dhn * (xv * r), axis=0, keepdims=True)

        @pl.when(i == NT - 1)
        def _():
            sbuf[0:8, :] = s4_r[...]
            sbuf[8:16, :] = s6_r[...]
            sloc, srem = _small_copies(sbuf, sall_o, send_sems, recv_sems, loc_sems)
            for cp in sloc + srem:
                cp.start()
            for cp in rem + srem:
                cp.wait_recv()
            for cp in rem + srem:
                cp.wait_send()
            for cp in loc + sloc:
                cp.wait()

    tok = lambda w: pl.BlockSpec((TT, w), lambda i: (i, 0))
    const = lambda shape: pl.BlockSpec(shape, lambda i: (0,) * len(shape))
    hbm = pl.BlockSpec(memory_space=pltpu.HBM)
    return pl.pallas_call(
        body, name="bwd_x", grid=(NT,),
        in_specs=[tok(NCOL), tok(D), tok(D), const((1, D)), const((D, NCOL)), hbm, hbm, hbm, hbm,
                  const((8, D)), const((8, D))],
        out_specs=[tok(D), hbm, hbm, hbm, hbm, hbm],
        out_shape=[jax.ShapeDtypeStruct((T, D), F32),
                   jax.ShapeDtypeStruct((512, 1024), F32), jax.ShapeDtypeStruct((3, 512, 1024), BF16),
                   jax.ShapeDtypeStruct((128, D), F32), jax.ShapeDtypeStruct((3, 128, D), BF16),
                   jax.ShapeDtypeStruct((8, 24, D), F32)],
        scratch_shapes=[pltpu.VMEM((24, D), F32), pltpu.SemaphoreType.DMA((13,)), pltpu.SemaphoreType.DMA((13,)),
                        pltpu.SemaphoreType.DMA((3,))],
        compiler_params=_cp(("arbitrary",)),
    )(dproj, x, dx2, mixw, w_full, rin, rinb, rout, routb, small4, small6)


def _grad_w_in(hn, dproj):
    TK = 1024
    NK = T // TK

    def body(hnt_r, dp_r, rin_o, rinb_o, acc, rbuf, obuf, obufb, send_sems, recv_sems, wb_sems):
        j = pl.program_id(0)
        kk = pl.program_id(1)
        x, y, c = lax.axis_index("x"), lax.axis_index("y"), lax.axis_index("c")
        mine = pl.ds(pl.multiple_of(c * 512, 512), 512)
        theirs = pl.ds(pl.multiple_of((1 - c) * 512, 512), 512)

        def send(jj):
            return pltpu.make_async_remote_copy(
                src_ref=acc.at[jj % 2, theirs, :], dst_ref=rbuf.at[jj], send_sem=send_sems.at[jj],
                recv_sem=recv_sems.at[jj], device_id=(x, y, 1 - c), device_id_type=MESH)

        def writeback(jj):
            cols = pl.ds(jj * 1024, 1024)
            return [pltpu.make_async_copy(obuf.at[jj % 2], rin_o.at[:, cols], wb_sems.at[jj % 2]),
                    pltpu.make_async_copy(obufb.at[jj % 2], rinb_o.at[:, cols], wb_sems.at[2 + jj % 2])]

        def wait_writeback(jj):
            for cp in writeback(jj):
                cp.wait()

        def finalize(jj):
            send(jj).wait_recv()
            red = acc[jj % 2, mine, :] + rbuf[jj]
            obuf[jj % 2] = red
            obufb[jj % 2] = red.astype(BF16)
            for cp in writeback(jj):
                cp.start()

        prod = _mm(hnt_r[...], dp_r[...])

        @pl.when(kk == 0)
        def _():
            for jj in (2, 3):
                @pl.when(j == jj)
                def _():
                    send(jj - 2).wait_send()
            acc[j % 2] = prod

        @pl.when(kk > 0)
        def _():
            acc[j % 2] += prod

        @pl.when(kk == NK - 1)
        def _():
            for jj in range(4):
                @pl.when(j == jj)
                def _():
                    send(jj).start()
                    if jj in (1, 2):
                        finalize(jj - 1)
                    if jj == 3:
                        wait_writeback(0)
                        finalize(2)
                        wait_writeback(1)
                        finalize(3)
                        wait_writeback(2)
                        wait_writeback(3)
                        send(2).wait_send()
                        send(3).wait_send()

    hbm = pl.BlockSpec(memory_space=pltpu.HBM)
    return pl.pallas_call(
        body, name="grad_w_in", grid=(4, NK),
        in_specs=[pl.BlockSpec((D, TK), lambda j, kk: (0, kk)), pl.BlockSpec((TK, 1024), lambda j, kk: (kk, j))],
        out_specs=[hbm, hbm],
        out_shape=[jax.ShapeDtypeStruct((512, NCOL), F32), jax.ShapeDtypeStruct((512, NCOL), BF16)],
        scratch_shapes=[pltpu.VMEM((2, D, 1024), F32), pltpu.VMEM((4, 512, 1024), F32), pltpu.VMEM((2, 512, 1024), F32),
                        pltpu.VMEM((2, 512, 1024), BF16),
                        pltpu.SemaphoreType.DMA((4,)), pltpu.SemaphoreType.DMA((4,)), pltpu.SemaphoreType.DMA((4,))],
        compiler_params=_cp(("arbitrary", "arbitrary")),
    )(hn, dproj)


def _chip_sum(own, rem, name):
    rows, cols = own.shape
    tr = min(rows, 256)

    def body(own_r, rem_r, o_r):
        acc = own_r[...]
        for s in range(3):
            acc = acc + rem_r[s].astype(F32)
        o_r[...] = acc

    return pl.pallas_call(
        body, name=name, grid=(rows // tr,),
        in_specs=[pl.BlockSpec((tr, cols), lambda i: (i, 0)), pl.BlockSpec((3, tr, cols), lambda i: (0, i, 0))],
        out_specs=pl.BlockSpec((tr, cols), lambda i: (i, 0)),
        out_shape=jax.ShapeDtypeStruct((rows, cols), F32),
        compiler_params=_cp(("parallel",)),
    )(own, rem)


def _chip_copies(rin_r, rinb_r, rout_r, routb_r, pin_o, pinr_o, pout_o, poutr_o, send_sems, recv_sems, loc_sems):
    x, y, c = lax.axis_index("x"), lax.axis_index("y"), lax.axis_index("c")
    chips = [(1 - x, y), (x, 1 - y), (1 - x, 1 - y)]
    jm = 2 * x + y
    loc = [pltpu.make_async_copy(rin_r.at[:, pl.ds(jm * 1024, 1024)], pin_o, loc_sems.at[0]),
           pltpu.make_async_copy(rout_r.at[pl.ds(jm * 128, 128), :], pout_o, loc_sems.at[1])]
    rem = []
    for k, (px, py) in enumerate(chips):
        j = 2 * px + py
        rem.append(pltpu.make_async_remote_copy(
            src_ref=rinb_r.at[:, pl.ds(j * 1024, 1024)], dst_ref=pinr_o.at[k],
            send_sem=send_sems.at[k], recv_sem=recv_sems.at[k], device_id=(px, py, c), device_id_type=MESH))
        rem.append(pltpu.make_async_remote_copy(
            src_ref=routb_r.at[pl.ds(j * 128, 128), :], dst_ref=poutr_o.at[k],
            send_sem=send_sems.at[3 + k], recv_sem=recv_sems.at[3 + k], device_id=(px, py, c),
            device_id_type=MESH))
    return loc, rem


def _small_copies(small_r, sall_o, send_sems, recv_sems, loc_sems):
    x, y, c = lax.axis_index("x"), lax.axis_index("y"), lax.axis_index("c")
    me = 4 * x + 2 * y + c
    loc = [pltpu.make_async_copy(small_r, sall_o.at[me], loc_sems.at[2])]
    rem = []
    k = 6
    for fx in range(2):
        for fy in range(2):
            for fc in range(2):
                if fx or fy or fc:
                    peer = (1 - x if fx else x, 1 - y if fy else y, 1 - c if fc else c)
                    rem.append(pltpu.make_async_remote_copy(
                        src_ref=small_r, dst_ref=sall_o.at[me], send_sem=send_sems.at[k],
                        recv_sem=recv_sems.at[k], device_id=peer, device_id_type=MESH))
                    k += 1
    return loc, rem


def _pair_share(pin, pout):
    def body(pin_r, pout_r, fin_o, fout_o, send_sems, recv_sems):
        x, y, c = lax.axis_index("x"), lax.axis_index("y"), lax.axis_index("c")
        sibling = (x, y, 1 - c)
        rem = [pltpu.make_async_remote_copy(src_ref=pin_r, dst_ref=fin_o.at[c], send_sem=send_sems.at[0],
                                            recv_sem=recv_sems.at[0], device_id=sibling, device_id_type=MESH),
               pltpu.make_async_remote_copy(src_ref=pout_r, dst_ref=fout_o.at[c], send_sem=send_sems.at[1],
                                            recv_sem=recv_sems.at[1], device_id=sibling, device_id_type=MESH)]
        for cp in rem:
            cp.start()
        fin_o[c] = pin_r[...]
        fout_o[c] = pout_r[...]
        for cp in rem:
            cp.wait_recv()
        for cp in rem:
            cp.wait_send()

    vm = pl.BlockSpec(memory_space=pltpu.VMEM)
    return pl.pallas_call(
        body, name="pair_share",
        out_shape=(jax.ShapeDtypeStruct((2, 512, 1024), F32), jax.ShapeDtypeStruct((2, 128, D), F32)),
        in_specs=[vm, vm], out_specs=(vm, vm),
        scratch_shapes=[pltpu.SemaphoreType.DMA((2,)), pltpu.SemaphoreType.DMA((2,))],
        compiler_params=_cp(),
    )(pin, pout)


def _adamw_math(w, g, m, v):
    m = B1 * m + (1.0 - B1) * g
    v = B2 * v + (1.0 - B2) * (g * g)
    m_hat = m / (1.0 - B1 ** STEP)
    v_hat = v / (1.0 - B2 ** STEP)
    delta = -LR * (m_hat / (jnp.sqrt(v_hat) + AEPS) + WD * w)
    return delta, m, v


def _adamw(w, g, m, v, name):
    rows, cols = w.shape
    tr = min(rows, 256)

    def body(w_r, g_r, m_r, v_r, d_o, m_o, v_o):
        d, mm, vv = _adamw_math(w_r[...], g_r[...], m_r[...], v_r[...])
        d_o[...] = d
        m_o[...] = mm
        v_o[...] = vv

    blk = pl.BlockSpec((tr, cols), lambda i: (i, 0))
    return pl.pallas_call(
        body, name=name, grid=(rows // tr,),
        in_specs=[blk] * 4, out_specs=[blk] * 3,
        out_shape=[jax.ShapeDtypeStruct((rows, cols), F32)] * 3,
        compiler_params=_cp(("parallel",)),
    )(w, g, m, v)


def _adamw_small(sall, params):
    def body(sall_r, *refs):
        ins, outs = refs[:15], refs[15:]
        tot = sall_r[0]
        for dv in range(1, 8):
            tot = tot + sall_r[dv]
        grads = [tot[16:17, :], tot[1:2, 0:AW], tot[1:2, AW:], tot[8:10, 0:HW], tot[0:1, :]]
        outs[0][...] = tot[2:3, 0:1]
        for p in range(5):
            w_r, m_r, v_r = ins[3 * p:3 * p + 3]
            g = grads[p]
            d, mm, vv = _adamw_math(w_r[...], g, m_r[...], v_r[...])
            outs[1 + 4 * p][...] = g
            outs[2 + 4 * p][...] = d
            outs[3 + 4 * p][...] = mm
            outs[4 + 4 * p][...] = vv

    flat = [a for p in params for a in p]
    shapes = [jax.ShapeDtypeStruct((1, 1), F32)]
    for p in params:
        shapes += [jax.ShapeDtypeStruct(p[0].shape, F32)] * 4
    vm = pl.BlockSpec(memory_space=pltpu.VMEM)
    return pl.pallas_call(
        body, name="adamw_small",
        in_specs=[vm] * 16, out_specs=[vm] * 21, out_shape=shapes,
        compiler_params=_cp(),
    )(sall, *flat)


def kernel(x, positions, w_in, w_out, mix_norm_w, attn_out_norm_w, hgrn_out_norm_w, hgrn_lb_raw, final_norm_w, loss_target, m_w_in, m_w_out, m_mix_norm_w, m_attn_out_norm_w, m_hgrn_out_norm_w, m_hgrn_lb_raw, m_final_norm_w, v_w_in, v_w_out, v_mix_norm_w, v_attn_out_norm_w, v_hgrn_out_norm_w, v_hgrn_lb_raw, v_final_norm_w):
    xs = x.reshape(T, D)
    tgt = loss_target.reshape(T, D)
    pos = positions.reshape(T, 1)
    fnw = final_norm_w.reshape(1, D)

    ti = np.arange(TH)
    tri_np = ((ti[:, None] // CHUNK == ti[None, :] // CHUNK) & (ti[None, :] <= ti[:, None])).astype(np.float32)
    tri = jnp.asarray(tri_np, BF16)
    trit = jnp.asarray(tri_np.T, BF16)
    hi_ = np.arange(AW) // HEAD
    gmat = jnp.asarray((hi_[:, None] == hi_[None, :]).astype(np.float32) / HEAD, BF16)
    emat_np = (np.arange(128)[:, None] == hi_[None, :]).astype(np.float32)
    sel_np = np.zeros((2, AW, 128), np.float32)
    sel_np[0, np.arange(8) * HEAD, np.arange(8)] = 1.0
    sel_np[1, np.arange(8) * HEAD, 8 + np.arange(8)] = 1.0
    emat = jnp.asarray(emat_np, BF16)
    selmat = jnp.asarray(sel_np, BF16)

    w_full, wout_full = _weight_gather(w_in.reshape(D, 1024), w_out.reshape(256, D))

    (hn, q1, k1, v1, q4, k4, v4, q16, k16, v16, ag, hq, hf, hi, hg) = _fwd_in(xs, pos, mix_norm_w, w_full)
    flat = lambda a: a.reshape(T, AW)
    o1, l1 = _attn_fwd(q1, k1, v1, T // BLK, "attn_fwd_d1")
    o4, l4 = _attn_fwd(flat(q4), flat(k4), flat(v4), T // 4 // BLK, "attn_fwd_d4")
    o16, l16 = _attn_fwd(flat(q16), flat(k16), flat(v16), T // 16 // BLK, "attn_fwd_d16")
    rec, sall = _hgrn_fwd(hq, hf, hi, hgrn_lb_raw, tri)

    (dx2, do1, do4, do16, st1, st4, st16, drec, dag, dhg, rout, routb, small4) = _fwd_out(
        o1, o4.reshape(4, T // 4, AW), o16.reshape(16, T // 16, AW),
        l1, l4.reshape(4, T // 4, 128), l16.reshape(16, T // 16, 128),
        rec, ag, hg, xs, tgt, attn_out_norm_w, hgrn_out_norm_w, fnw, wout_full, gmat, emat, selmat)

    fst = lambda a: a.reshape(T, 128)
    dq1, dk1, dv1 = _attn_bwd(q1, k1, v1, do1, st1, T // BLK, "attn_bwd_d1")
    dq4, dk4, dv4 = _attn_bwd(flat(q4), flat(k4), flat(v4), flat(do4), fst(st4), T // 4 // BLK, "attn_bwd_d4")
    dq16, dk16, dv16 = _attn_bwd(flat(q16), flat(k16), flat(v16), flat(do16), fst(st16), T // 16 // BLK,
                                 "attn_bwd_d16")
    dhq, dhf, dhi, small6 = _hgrn_bwd(hq, hf, hi, hgrn_lb_raw, tri, trit, drec, sall)

    r4 = lambda a: a.reshape(4, T // 4, AW)
    r16 = lambda a: a.reshape(16, T // 16, AW)
    dproj = _dproj_build((dq1, r4(dq4), r16(dq16)), (dk1, r4(dk4), r16(dk16)), (dv1, r4(dv4), r16(dv16)),
                         dag, dhq, dhf, dhi, dhg, pos)
    rin, rinb = _grad_w_in(hn, dproj)
    gx, pin_own, pin_rem, pout_own, pout_rem, small_all = _bwd_x(
        dproj, xs, dx2, mix_norm_w, w_full, rin, rinb, rout, routb, small4, small6)
    pin = _chip_sum(pin_own, pin_rem, "chip_sum_in")
    pout = _chip_sum(pout_own, pout_rem, "chip_sum_out")
    fin, fout = _pair_share(pin, pout)
    g_w_in = fin.reshape(D, 1024)
    g_w_out = fout.reshape(256, D)

    d_in, nm_in, nv_in = _adamw(w_in.reshape(D, 1024), g_w_in, m_w_in.reshape(D, 1024), v_w_in.reshape(D, 1024),
                                "adamw_w_in")
    d_out, nm_out, nv_out = _adamw(w_out.reshape(256, D), g_w_out, m_w_out.reshape(256, D), v_w_out.reshape(256, D),
                                   "adamw_w_out")
    params = [(mix_norm_w, m_mix_norm_w, v_mix_norm_w),
              (attn_out_norm_w, m_attn_out_norm_w, v_attn_out_norm_w),
              (hgrn_out_norm_w, m_hgrn_out_norm_w, v_hgrn_out_norm_w),
              (hgrn_lb_raw, m_hgrn_lb_raw, v_hgrn_lb_raw),
              (fnw, m_final_norm_w.reshape(1, D), v_final_norm_w.reshape(1, D))]
    so = _adamw_small(small_all, params)
    loss = so[0].reshape(())
    g_s = [so[1 + 4 * p] for p in range(5)]
    d_s = [so[2 + 4 * p] for p in range(5)]
    m_s = [so[3 + 4 * p] for p in range(5)]
    v_s = [so[4 + 4 * p] for p in range(5)]
    for lst in (g_s, d_s, m_s, v_s):
        lst[4] = lst[4].reshape(D)

    return (loss, gx.reshape(1, T, D),
            g_w_in.reshape(1, D, 1024), g_w_out.reshape(1, 256, D), *g_s,
            d_in.reshape(1, D, 1024), d_out.reshape(1, 256, D), *d_s,
            nm_in.reshape(1, D, 1024), nm_out.reshape(1, 256, D), *m_s,
            nv_in.reshape(1, D, 1024), nv_out.reshape(1, 256, D), *v_s)
```

```python
import functools

import numpy as np
import jax
import jax.numpy as jnp
from jax import lax
from jax.experimental import pallas as pl
from jax.experimental.pallas import tpu as pltpu

F32 = jnp.float32
BF16 = jnp.bfloat16

T = 4096
D = 1024
AW = 512
HW = 512
NCOL = 4096
HEAD = 64
BLK = 128
CHUNK = 64
EPS = 1e-6
SCALE = HEAD ** -0.5
NEG = -1e30
ROPE_THETA = 500000.0
INV_FREQ = [float(v) for v in
            (np.float32(ROPE_THETA) ** (-(np.arange(8, dtype=np.float32)) * np.float32(0.125)))]
LR, B1, B2, AEPS, WD, STEP = 0.001, 0.9, 0.999, 1e-08, 0.01, 10
VMEM_LIMIT = 56 * 1024 * 1024
MESH = pl.DeviceIdType.MESH


def _cp(sem=None, **kw):
    return pltpu.CompilerParams(dimension_semantics=sem, vmem_limit_bytes=VMEM_LIMIT, **kw)


def _mm(a, b):
    return jnp.dot(a, b, preferred_element_type=F32)


def _mm_nt(a, b):
    return lax.dot_general(a, b, (((1,), (1,)), ((), ())), preferred_element_type=F32)


def _mm_tn(a, b):
    return lax.dot_general(a, b, (((0,), (0,)), ((), ())), preferred_element_type=F32)


def _split3(x):
    h = x.astype(BF16)
    r = x - h.astype(F32)
    m = r.astype(BF16)
    l = (r - m.astype(F32)).astype(BF16)
    return h, m, l


def _mm_exact_l(mat_bf, x):
    h, m, l = _split3(x)
    return _mm(mat_bf, h) + _mm(mat_bf, m) + _mm(mat_bf, l)


def _mm_exact_r(x, mat_bf):
    h = x.astype(BF16)
    l = (x - h.astype(F32)).astype(BF16)
    return _mm(h, mat_bf) + _mm(l, mat_bf)


def _sigmoid(x):
    return 1.0 / (1.0 + jnp.exp(-x))


def _rope_tables(pos):
    lane = lax.broadcasted_iota(jnp.int32, (1, 128), 1)
    jl = lane & 63
    fi = jl & 7
    inv = jnp.zeros((1, 128), F32)
    for kk in range(8):
        inv = jnp.where(fi == kk, INV_FREQ[kk], inv)
    ang = pos.astype(F32) * inv
    c = jnp.cos(ang)
    s = jnp.sin(ang)
    cosf = jnp.where(jl < 16, c, 1.0)
    s1 = jnp.where(jl < 8, -s, 0.0)
    s2 = jnp.where((jl >= 8) & (jl < 16), s, 0.0)
    return cosf, s1, s2


def _rope(t, cosf, s1, s2):
    parts = []
    for ci in range(t.shape[1] // 128):
        tc = t[:, ci * 128:(ci + 1) * 128]
        parts.append(tc * cosf + pltpu.roll(tc, 120, 1) * s1 + pltpu.roll(tc, 8, 1) * s2)
    return jnp.concatenate(parts, axis=1)


def _rope_bwd(g, cosf, s1, s2):
    parts = []
    for ci in range(g.shape[1] // 128):
        gc = g[:, ci * 128:(ci + 1) * 128]
        parts.append(gc * cosf + pltpu.roll(gc * s1, 8, 1) + pltpu.roll(gc * s2, 120, 1))
    return jnp.concatenate(parts, axis=1)


def _perm_store(val, scr, o1, o4, o16, dt):
    n = val.shape[0]
    o1[...] = val.astype(dt)
    for ci in range(val.shape[1] // 128):
        cs = slice(ci * 128, (ci + 1) * 128)
        scr[ci] = val[:, cs]
        for rr in range(4):
            o4[rr, :, cs] = scr[ci, pl.ds(rr, n // 4, stride=4), :].astype(dt)
        for rr in range(16):
            o16[rr, :, cs] = scr[ci, pl.ds(rr, n // 16, stride=16), :].astype(dt)


def _unperm_load(r4, r16, scr_a, scr_b):
    n = scr_a.shape[1]
    nc = r4.shape[-1] // 128
    for ci in range(nc):
        cs = slice(ci * 128, (ci + 1) * 128)
        for rr in range(4):
            scr_a[ci, pl.ds(rr, n // 4, stride=4), :] = r4[rr, :, cs].astype(F32)
        for rr in range(16):
            scr_b[ci, pl.ds(rr, n // 16, stride=16), :] = r16[rr, :, cs].astype(F32)
    return (jnp.concatenate([scr_a[ci] for ci in range(nc)], axis=1),
            jnp.concatenate([scr_b[ci] for ci in range(nc)], axis=1))


def _cast_weights(w_in, w_out):
    def body(win_ref, wout_ref, bin_ref, bout_ref):
        bin_ref[...] = win_ref[...].astype(BF16)

        @pl.when(pl.program_id(0) == 0)
        def _():
            bout_ref[...] = wout_ref[...].astype(BF16)

    return pl.pallas_call(
        body, name="cast_weights", grid=(4,),
        in_specs=[pl.BlockSpec((256, 1024), lambda i: (i, 0)), pl.BlockSpec((256, D), lambda i: (0, 0))],
        out_specs=[pl.BlockSpec((256, 1024), lambda i: (i, 0)), pl.BlockSpec((256, D), lambda i: (0, 0))],
        out_shape=(jax.ShapeDtypeStruct((D, 1024), BF16), jax.ShapeDtypeStruct((256, D), BF16)),
        compiler_params=_cp(("arbitrary",)),
    )(w_in, w_out)


def _fwd_in(x, pos, mixw, wb_in, wb_out, jm_arr):
    TT = 512
    NT = T // TT

    def body(jm_ref, x_ref, pos_ref, mw_ref, wbin_ref, wbout_ref,
             hnt_ref, q1, k1, v1, q4, k4, v4, q16, k16, v16, ag, hq, hf, hi, hg, wfull_o, woutfull_o,
             wbuf, wobuf, hn_all, scr, send_sems, recv_sems, loc_sems):
        s = pl.program_id(0)
        i = pl.program_id(1)
        mx, my, c = lax.axis_index("x"), lax.axis_index("y"), lax.axis_index("c")
        me, sibling = (mx, my, c), (mx, my, 1 - c)
        chips = [(mx, 1 - my), (1 - mx, my), (1 - mx, 1 - my)]
        jm = 2 * mx + my
        rows_in = [pl.ds(pl.multiple_of(h * 512, 512), 512) for h in (c, 1 - c)]
        rows_out = [pl.ds(pl.multiple_of(h * 128, 128), 128) for h in (c, 1 - c)]

        def blk(k):
            return lax.bitwise_xor(jm, k + 1)

        def rc(n, ref, to):
            return pltpu.make_async_remote_copy(src_ref=ref, dst_ref=ref, send_sem=send_sems.at[n],
                                                recv_sem=recv_sems.at[n], device_id=to, device_id_type=MESH)

        send_in = lambda k: rc(k, wbuf.at[jm, rows_in[0], :], (*chips[k], c))
        send_out = lambda k: rc(3 + k, wobuf.at[jm, rows_out[0], :], (*chips[k], c))
        got_in = lambda k: rc(k, wbuf.at[blk(k), rows_in[0], :], me)
        got_out = lambda k: rc(3 + k, wobuf.at[blk(k), rows_out[0], :], me)
        pass_in = lambda k: rc(6 + k, wbuf.at[blk(k), rows_in[0], :], sibling)
        pass_out = lambda k: rc(9 + k, wobuf.at[blk(k), rows_out[0], :], sibling)
        passed_in = lambda k: rc(6 + k, wbuf.at[blk(k), rows_in[1], :], me)
        passed_out = lambda k: rc(9 + k, wobuf.at[blk(k), rows_out[1], :], me)

        def keep(j, n):
            return pltpu.make_async_copy(wbuf.at[j], wfull_o.at[:, pl.ds(j * 1024, 1024)], loc_sems.at[n])

        @pl.when((s == 0) & (i == 0))
        def _():
            own = [pltpu.make_async_copy(wbin_ref, wbuf.at[jm], loc_sems.at[4]),
                   pltpu.make_async_copy(wbout_ref, wobuf.at[jm], loc_sems.at[5])]
            for cp in own:
                cp.start()
            for cp in own:
                cp.wait()
            send_in(0).start()
            send_in(1).start()
            keep(jm, 0).start()

        def arrive(k):
            if k == 0:
                send_in(0).wait_send()
                send_in(1).wait_send()
                send_in(2).start()
                for kk in range(3):
                    send_out(kk).start()
            got_in(k).wait_recv()
            pass_in(k).start()
            passed_in(k).wait_recv()
            keep(blk(k), k + 1).start()

        for k in range(3):
            pl.when((s == k + 1) & (i == 0))(functools.partial(arrive, k))

        tile = pl.ds(pl.multiple_of(i * TT, TT), TT)

        @pl.when(s == 0)
        def _():
            xv = x_ref[...]
            r = lax.rsqrt(jnp.mean(xv * xv, axis=-1, keepdims=True) + EPS)
            hnf = (xv * r) * mw_ref[...]
            hn_all[tile, :] = hnf.astype(BF16)
            hnt_ref[...] = hnf.T.astype(BF16)

        def project(jj):
            hn = hn_all[tile, :]
            lo = _mm(hn, wbuf[jj, :, 0:512])
            hi_cols = _mm(hn, wbuf[jj, :, 512:1024])
            if jj == 0:
                cosf, s1, s2 = _rope_tables(pos_ref[...])
                _perm_store(_rope(lo, cosf, s1, s2), scr, q1, q4, q16, BF16)
                _perm_store(_rope(hi_cols, cosf, s1, s2), scr, k1, k4, k16, BF16)
            elif jj == 1:
                _perm_store(lo, scr, v1, v4, v16, BF16)
                ag[...] = hi_cols
            elif jj == 2:
                hq[...] = lo
                hf[...] = hi_cols
            else:
                hi[...] = lo.astype(BF16)
                hg[...] = hi_cols

        j = lax.bitwise_xor(jm, s)
        for jj in range(4):
            pl.when(j == jj)(functools.partial(project, jj))

        @pl.when((s == 3) & (i == NT - 1))
        def _():
            for k in range(3):
                got_out(k).wait_recv()
                pass_out(k).start()
            for k in range(3):
                passed_out(k).wait_recv()
            out = pltpu.make_async_copy(wobuf, woutfull_o, loc_sems.at[4])
            out.start()
            send_in(2).wait_send()
            for k in range(3):
                send_out(k).wait_send()
                pass_in(k).wait_send()
                pass_out(k).wait_send()
            keep(jm, 0).wait()
            for k in range(3):
                keep(blk(k), k + 1).wait()
            out.wait()

    def at_stage_of(jb):
        def index(s, i, jm_ref):
            sa = lax.bitwise_xor(jm_ref[0], jb)
            return jnp.where(s < sa, 0, jnp.where(s == sa, i, NT - 1))
        return index

    tok = lambda w, jb: pl.BlockSpec((TT, w), lambda s, i, jm_ref: (at_stage_of(jb)(s, i, jm_ref), 0))
    d4 = lambda jb: pl.BlockSpec((4, TT // 4, AW), lambda s, i, jm_ref: (0, at_stage_of(jb)(s, i, jm_ref), 0))
    d16 = lambda jb: pl.BlockSpec((16, TT // 16, AW), lambda s, i, jm_ref: (0, at_stage_of(jb)(s, i, jm_ref), 0))
    hbm = pl.BlockSpec(memory_space=pltpu.HBM)
    sd = lambda shape, dt: jax.ShapeDtypeStruct(shape, dt)
    in_own_stage = lambda s, i: jnp.where(s == 0, i, NT - 1)
    grid_spec = pltpu.PrefetchScalarGridSpec(
        num_scalar_prefetch=1, grid=(4, NT),
        in_specs=[pl.BlockSpec((TT, D), lambda s, i, jm_ref: (in_own_stage(s, i), 0)),
                  pl.BlockSpec((TT, 1), lambda s, i, jm_ref: (i, 0)),
                  pl.BlockSpec((1, D), lambda s, i, jm_ref: (0, 0)), hbm, hbm],
        out_specs=[pl.BlockSpec((D, TT), lambda s, i, jm_ref: (0, in_own_stage(s, i))),
                   tok(AW, 0), tok(AW, 0), tok(AW, 1), d4(0), d4(0), d4(1), d16(0), d16(0), d16(1),
                   tok(AW, 1), tok(AW, 2), tok(AW, 2), tok(AW, 3), tok(AW, 3), hbm, hbm],
        scratch_shapes=[pltpu.VMEM((4, D, 1024), BF16), pltpu.VMEM((4, 256, D), BF16), pltpu.VMEM((T, D), BF16),
                        pltpu.VMEM((4, TT, 128), F32), pltpu.SemaphoreType.DMA((12,)),
                        pltpu.SemaphoreType.DMA((12,)), pltpu.SemaphoreType.DMA((6,))])
    return pl.pallas_call(
        body, name="fwd_in", grid_spec=grid_spec,
        out_shape=[sd((D, T), BF16)] + [sd((T, AW), BF16)] * 3 + [sd((4, T // 4, AW), BF16)] * 3
        + [sd((16, T // 16, AW), BF16)] * 3
        + [sd((T, AW), F32), sd((T, AW), F32), sd((T, AW), F32), sd((T, AW), BF16), sd((T, AW), F32),
           sd((D, NCOL), BF16), sd((4, 256, D), BF16)],
        compiler_params=_cp(("arbitrary", "arbitrary")),
    )(jm_arr, x, pos, mixw, wb_in, wb_out)


def _band_mask(key_axis, nkeys=2 * BLK):
    shape = (nkeys, 2 * BLK) if key_axis == 0 else (2 * BLK, nkeys)
    kj = lax.broadcasted_iota(jnp.int32, shape, key_axis)
    qi = lax.broadcasted_iota(jnp.int32, shape, 1 - key_axis) & (BLK - 1)
    return (kj >= qi) & (kj <= qi + BLK), kj, qi


def _stack_heads(t2, in_a):
    z = jnp.zeros_like(t2)
    return jnp.concatenate([jnp.where(in_a[0], t2, z), jnp.where(in_a[1], t2, z)], axis=0)


def _attn_fwd(q, k, v, nb, name):
    n = min(4, nb)
    CH = n * BLK
    halo = nb > n

    def body(*refs):
        if halo:
            q_ref, k_ref, v_ref, kp_ref, vp_ref, o_ref, lse_ref = refs
        else:
            q_ref, k_ref, v_ref, o_ref, lse_ref = refs
        lane = lax.broadcasted_iota(jnp.int32, (1, 128), 1)
        in_a = [lane < HEAD, lane >= HEAD]
        band, kj, _ = _band_mask(1)
        thr0 = jnp.where((n * pl.program_id(0)) % nb == 0, BLK, 0) if halo else BLK
        mask0 = band & (kj >= thr0)
        for b in range(n):
            rs = slice(b * BLK, (b + 1) * BLK)
            stat = jnp.zeros((BLK, 128), F32)
            for hp in range(4):
                cs = slice(hp * 128, (hp + 1) * 128)
                q2s = _stack_heads(q_ref[rs, cs], in_a)
                if b == 0:
                    kprev = kp_ref[:, cs] if halo else k_ref[rs, cs]
                    vprev = vp_ref[:, cs] if halo else v_ref[rs, cs]
                    kk = jnp.concatenate([kprev, k_ref[rs, cs]], axis=0)
                    vv = jnp.concatenate([vprev, v_ref[rs, cs]], axis=0)
                    mask = mask0
                else:
                    kk = k_ref[(b - 1) * BLK:(b + 1) * BLK, cs]
                    vv = v_ref[(b - 1) * BLK:(b + 1) * BLK, cs]
                    mask = band
                s = jnp.where(mask, _mm_nt(q2s, kk) * SCALE, NEG)
                m = jnp.max(s, axis=-1, keepdims=True)
                p = jnp.exp(s - m)
                l = jnp.sum(p, axis=-1, keepdims=True)
                o = _mm(p.astype(BF16), vv) / l
                lse = m + jnp.log(l)
                o_ref[rs, cs] = jnp.where(in_a[0], o[:BLK], o[BLK:])
                stat = jnp.where(lane == 2 * hp, lse[:BLK], stat)
                stat = jnp.where(lane == 2 * hp + 1, lse[BLK:], stat)
            lse_ref[rs, :] = stat

    cur = pl.BlockSpec((CH, AW), lambda i: (i, 0))
    prev = pl.BlockSpec((BLK, AW), lambda i: (jnp.maximum(n * i - 1, 0), 0))
    return pl.pallas_call(
        body, name=name, grid=(T // CH,),
        in_specs=[cur, cur, cur] + ([prev, prev] if halo else []),
        out_specs=[cur, pl.BlockSpec((CH, 128), lambda i: (i, 0))],
        out_shape=[jax.ShapeDtypeStruct((T, AW), F32), jax.ShapeDtypeStruct((T, 128), F32)],
        compiler_params=_cp(("parallel",)),
    )(*((q, k, v) + ((k, v) if halo else ())))


def _attn_bwd(q, k, v, do, st, nb, name):
    n = min(4, nb)
    CH = n * BLK
    NBLK = T // BLK
    halo = nb > n

    def body(*refs):
        if halo:
            (q_ref, k_ref, v_ref, do_ref, st_ref, kp_ref, vp_ref, qn_ref, don_ref, stn_ref,
             dq_ref, dk_ref, dv_ref) = refs
        else:
            q_ref, k_ref, v_ref, do_ref, st_ref, dq_ref, dk_ref, dv_ref = refs
        i = pl.program_id(0)
        lane = lax.broadcasted_iota(jnp.int32, (1, 128), 1)
        in_a = [lane < HEAD, lane >= HEAD]
        band, kj, _ = _band_mask(0)
        thr0 = jnp.where((n * i) % nb == 0, BLK, 0) if halo else BLK
        mask0 = band & (kj >= thr0)

        def stat_rows(st_t, hp):
            lse_r = jnp.concatenate([st_t[2 * hp:2 * hp + 1, :], st_t[2 * hp + 1:2 * hp + 2, :]], axis=1)
            dl_r = jnp.concatenate([st_t[8 + 2 * hp:9 + 2 * hp, :], st_t[9 + 2 * hp:10 + 2 * hp, :]], axis=1)
            return lse_r, dl_r

        st_t = [st_ref[b * BLK:(b + 1) * BLK, :].T for b in range(n)]
        if halo:
            nxt_thr = jnp.where((n * i + n) % nb == 0, 2 * BLK, 0)
            _, kj1, qi1 = _band_mask(0, BLK)
            mask_next = kj1 >= qi1 + nxt_thr
            stn_t = stn_ref[...].T

        for hp in range(4):
            cs = slice(hp * 128, (hp + 1) * 128)
            kb = [k_ref[b * BLK:(b + 1) * BLK, cs] for b in range(n)]
            vb = [v_ref[b * BLK:(b + 1) * BLK, cs] for b in range(n)]
            dk_acc = [jnp.zeros((BLK, 128), F32) for _ in range(n)]
            dv_acc = [jnp.zeros((BLK, 128), F32) for _ in range(n)]
            for b in range(n):
                rs = slice(b * BLK, (b + 1) * BLK)
                q2s = _stack_heads(q_ref[rs, cs], in_a)
                do2s = _stack_heads(do_ref[rs, cs], in_a)
                if b == 0:
                    kprev = kp_ref[:, cs] if halo else kb[0]
                    vprev = vp_ref[:, cs] if halo else vb[0]
                    mask = mask0
                else:
                    kprev, vprev, mask = kb[b - 1], vb[b - 1], band
                kk = jnp.concatenate([kprev, kb[b]], axis=0)
                vv = jnp.concatenate([vprev, vb[b]], axis=0)
                lse_r, dl_r = stat_rows(st_t[b], hp)
                s_t = jnp.where(mask, _mm_nt(kk, q2s) * SCALE, NEG)
                p_t = jnp.exp(s_t - lse_r)
                ds_t = (p_t * (_mm_nt(vv, do2s) - dl_r)).astype(BF16)
                dkk = _mm(ds_t, q2s) * SCALE
                dvv = _mm(p_t.astype(BF16), do2s)
                dqs = _mm_tn(ds_t, kk) * SCALE
                dq_ref[rs, cs] = jnp.where(in_a[0], dqs[:BLK], dqs[BLK:]).astype(BF16)
                dk_acc[b] += dkk[BLK:]
                dv_acc[b] += dvv[BLK:]
                if b > 0:
                    dk_acc[b - 1] += dkk[:BLK]
                    dv_acc[b - 1] += dvv[:BLK]
            if halo:
                q2s = _stack_heads(qn_ref[:, cs], in_a)
                do2s = _stack_heads(don_ref[:, cs], in_a)
                lse_r, dl_r = stat_rows(stn_t, hp)
                s_t = jnp.where(mask_next, _mm_nt(kb[n - 1], q2s) * SCALE, NEG)
                p_t = jnp.exp(s_t - lse_r)
                ds_t = (p_t * (_mm_nt(vb[n - 1], do2s) - dl_r)).astype(BF16)
                dk_acc[n - 1] += _mm(ds_t, q2s) * SCALE
                dv_acc[n - 1] += _mm(p_t.astype(BF16), do2s)
            for b in range(n):
                dk_ref[b * BLK:(b + 1) * BLK, cs] = dk_acc[b].astype(BF16)
                dv_ref[b * BLK:(b + 1) * BLK, cs] = dv_acc[b].astype(BF16)

    cur = pl.BlockSpec((CH, AW), lambda i: (i, 0))
    cur_st = pl.BlockSpec((CH, 128), lambda i: (i, 0))
    prev = pl.BlockSpec((BLK, AW), lambda i: (jnp.maximum(n * i - 1, 0), 0))
    nxt = pl.BlockSpec((BLK, AW), lambda i: (jnp.minimum(n * i + n, NBLK - 1), 0))
    nxt_st = pl.BlockSpec((BLK, 128), lambda i: (jnp.minimum(n * i + n, NBLK - 1), 0))
    ins = [cur] * 4 + [cur_st] + ([prev, prev, nxt, nxt, nxt_st] if halo else [])
    args = (q, k, v, do, st) + ((k, v, q, do, st) if halo else ())
    return pl.pallas_call(
        body, name=name, grid=(T // CH,),
        in_specs=ins,
        out_specs=[cur] * 3,
        out_shape=[jax.ShapeDtypeStruct((T, AW), BF16)] * 3,
        compiler_params=_cp(("parallel",)),
    )(*args)


TH = 256
NCH = TH // CHUNK


def _hgrn_common(hq_ref, hf_ref, lbr_ref, tri_ref):
    r0 = lbr_ref[0:1, :]
    r1 = lbr_ref[1:2, :]
    mx = jnp.maximum(r0, r1)
    e0 = jnp.exp(r0 - mx)
    e1 = jnp.exp(r1 - mx)
    lb = e0 / (e0 + e1)
    hqv = hq_ref[...]
    sq = _sigmoid(hqv)
    qv = hqv * sq
    sf = _sigmoid(hf_ref[...])
    f = lb + (1.0 - lb) * sf
    kv = 1.0 - f
    g = jnp.log(f)
    cum = _mm_exact_l(tri_ref[...], g)
    lastb = jnp.concatenate(
        [jnp.broadcast_to(cum[c * CHUNK + CHUNK - 1:(c + 1) * CHUNK, :], (CHUNK, HW)) for c in range(NCH)], axis=0)
    ea = jnp.exp(cum)
    ena = jnp.exp(-cum)
    eend = jnp.exp(lastb - cum)
    return dict(lb=lb, hq=hqv, sq=sq, q=qv, sf=sf, f=f, k=kv, cum=cum, lastb=lastb, ea=ea, ena=ena, eend=eend,
                qd=qv * ea, ki=kv * ena, ke=kv * eend, dec=jnp.exp(lastb))


def _tri_mask(transposed=False):
    ti = lax.broadcasted_iota(jnp.int32, (TH, TH), 1 if transposed else 0)
    si = lax.broadcasted_iota(jnp.int32, (TH, TH), 0 if transposed else 1)
    return (si <= ti) & ((si // CHUNK) == (ti // CHUNK))


def _hgrn_fwd(hq, hf, hi, lbr, tri):
    def body(hq_ref, hf_ref, hi_ref, lbr_ref, tri_ref, rec_ref, sall_ref, st_scr):
        @pl.when(pl.program_id(0) == 0)
        def _():
            st_scr[...] = jnp.zeros_like(st_scr)

        w = _hgrn_common(hq_ref, hf_ref, lbr_ref, tri_ref)
        qd, ki, ke = w["qd"].astype(BF16), w["ki"].astype(BF16), w["ke"].astype(BF16)
        dec = w["dec"]
        vb = hi_ref[...]
        causal = _tri_mask()
        for h in range(4):
            cs = slice(h * 128, (h + 1) * 128)
            att = jnp.where(causal, _mm_nt(qd[:, cs], ki[:, cs]), 0.0)
            o_intra = _mm(att.astype(BF16), vb[:, cs])
            for c in range(NCH):
                rs = slice(c * CHUNK, (c + 1) * CHUNK)
                st = st_scr[:, cs]
                sall_ref[c, :, cs] = st
                rec_ref[rs, cs] = o_intra[rs] + _mm_nt(qd[rs, cs], st.astype(BF16))
                st_scr[:, cs] = dec[c * CHUNK:c * CHUNK + 1, cs] * st + _mm_tn(vb[rs, cs], ke[rs, cs])

    tok = pl.BlockSpec((TH, HW), lambda i: (i, 0))
    return pl.pallas_call(
        body, name="hgrn_fwd", grid=(T // TH,),
        in_specs=[tok, tok, tok, pl.BlockSpec((2, HW), lambda i: (0, 0)), pl.BlockSpec((TH, TH), lambda i: (0, 0))],
        out_specs=[tok, pl.BlockSpec((NCH, 128, HW), lambda i: (i, 0, 0))],
        out_shape=[jax.ShapeDtypeStruct((T, HW), F32), jax.ShapeDtypeStruct((T // CHUNK, 128, HW), F32)],
        scratch_shapes=[pltpu.VMEM((128, HW), F32)],
        compiler_params=_cp(("arbitrary",)),
    )(hq, hf, hi, lbr, tri)


def _hgrn_bwd(hq, hf, hi, lbr, tri, trit, drec, sall):
    NT = T // TH

    def body(hq_ref, hf_ref, hi_ref, lbr_ref, tri_ref, trit_ref, do_ref, sall_ref,
             dhq_ref, dhf_ref, dhi_ref, small_ref, dst_scr, dlb_scr, dqd_scr, dki_scr, dke_scr, dlast_scr):
        step = pl.program_id(0)

        @pl.when(step == 0)
        def _():
            dst_scr[...] = jnp.zeros_like(dst_scr)
            dlb_scr[...] = jnp.zeros_like(dlb_scr)

        w = _hgrn_common(hq_ref, hf_ref, lbr_ref, tri_ref)
        qd, ki, ke = w["qd"].astype(BF16), w["ki"].astype(BF16), w["ke"].astype(BF16)
        dec = w["dec"]
        vb = hi_ref[...]
        dob = do_ref[...].astype(BF16)
        causal = _tri_mask()
        causal_t = _tri_mask(transposed=True)
        for h in range(4):
            cs = slice(h * 128, (h + 1) * 128)
            att_t = jnp.where(causal_t, _mm_nt(ki[:, cs], qd[:, cs]), 0.0).astype(BF16)
            datt_t = jnp.where(causal_t, _mm_nt(vb[:, cs], dob[:, cs]), 0.0).astype(BF16)
            datt = jnp.where(causal, _mm_nt(dob[:, cs], vb[:, cs]), 0.0).astype(BF16)
            dv_intra = _mm(att_t, dob[:, cs])
            dqd_intra = _mm(datt, ki[:, cs])
            dki_scr[:, cs] = _mm(datt_t, qd[:, cs])
            for c in reversed(range(NCH)):
                rs = slice(c * CHUNK, (c + 1) * CHUNK)
                dec_c = dec[c * CHUNK:c * CHUNK + 1, :]
                st = sall_ref[c, :, cs]
                dst = dst_scr[:, cs]
                dstb = dst.astype(BF16)
                dhi_ref[rs, cs] = (dv_intra[rs] + _mm_nt(ke[rs, cs], dstb)).astype(BF16)
                dqd_scr[rs, cs] = dqd_intra[rs] + _mm(dob[rs, cs], st.astype(BF16))
                dke_scr[rs, cs] = _mm(vb[rs, cs], dstb)
                ddec = jnp.sum(dst * st, axis=0, keepdims=True)
                dlast_scr[c:c + 1, cs] = ddec * dec_c[:, cs]
                dst_scr[:, cs] = dec_c[:, cs] * dst + _mm_tn(dob[rs, cs], qd[rs, cs])
        dqd, dki, dke = dqd_scr[...], dki_scr[...], dke_scr[...]
        dq = dqd * w["ea"]
        dk = dki * w["ena"] + dke * w["eend"]
        dcum = dqd * w["qd"] - dki * w["ki"] - dke * w["ke"]
        dkeke = dke * w["ke"]
        dlastb = jnp.concatenate(
            [jnp.broadcast_to(dlast_scr[c:c + 1, :] + jnp.sum(dkeke[c * CHUNK:(c + 1) * CHUNK], axis=0, keepdims=True),
                              (CHUNK, HW)) for c in range(NCH)], axis=0)
        dg = _mm_exact_l(trit_ref[...], dcum) + dlastb
        df = dg / w["f"] - dk
        lb, sf, sq = w["lb"], w["sf"], w["sq"]
        dhf_ref[...] = (df * (1.0 - lb) * sf * (1.0 - sf)).astype(BF16)
        dhq_ref[...] = (dq * (sq * (1.0 + w["hq"] * (1.0 - sq)))).astype(BF16)
        dlb_scr[...] += jnp.sum(df * (1.0 - sf), axis=0, keepdims=True)

        @pl.when(step == NT - 1)
        def _():
            gr = dlb_scr[...] * lb * (1.0 - lb)
            small_ref[...] = jnp.zeros_like(small_ref)
            small_ref[0:1, 0:HW] = gr
            small_ref[1:2, 0:HW] = -gr

    tok = pl.BlockSpec((TH, HW), lambda i: (NT - 1 - i, 0))
    const = lambda shape: pl.BlockSpec(shape, lambda i: (0,) * len(shape))
    return pl.pallas_call(
        body, name="hgrn_bwd", grid=(NT,),
        in_specs=[tok, tok, tok, const((2, HW)), const((TH, TH)), const((TH, TH)), tok,
                  pl.BlockSpec((NCH, 128, HW), lambda i: (NT - 1 - i, 0, 0))],
        out_specs=[tok, tok, tok, const((8, D))],
        out_shape=[jax.ShapeDtypeStruct((T, HW), BF16)] * 3 + [jax.ShapeDtypeStruct((8, D), F32)],
        scratch_shapes=[pltpu.VMEM((128, HW), F32), pltpu.VMEM((1, HW), F32), pltpu.VMEM((TH, HW), F32),
                        pltpu.VMEM((TH, HW), F32), pltpu.VMEM((TH, HW), F32), pltpu.VMEM((8, HW), F32)],
        compiler_params=_cp(("arbitrary",)),
    )(hq, hf, hi, lbr, tri, trit, drec, sall)


def _fwd_out(o1, o4, o16, l1, l4, l16, rec, ag, hg, x, tgt, anw, hnw, fnw, wout_full, gmat, emat, selmat):
    TT = 256

    def body(o1_r, o4_r, o16_r, l1_r, l4_r, l16_r, rec_r, ag_r, hg_r, x_r, tgt_r, anw_r, hnw_r, fnw_r, wo_r, g_r,
             e_r, sel_r, dx2_o, do1_o, do4_o, do16_o, st1_o, st4_o, st16_o, drec_o, dag_o, dhg_o,
             rout_o, routb_o, small_o, scr_a, scr_b, gwout_o, rbuf, send_sems, recv_sems):
        @pl.when(pl.program_id(0) == 0)
        def _():
            gwout_o[...] = jnp.zeros_like(gwout_o)
            small_o[...] = jnp.zeros_like(small_o)

        def unperm(r4, r16):
            return _unperm_load(r4, r16, scr_a, scr_b)

        def perm_out(val, p1, p4, p16, dt):
            _perm_store(val, scr_a, p1, p4, p16, dt)

        o4u, o16u = unperm(o4_r, o16_r)
        l4c, l16c = unperm(l4_r, l16_r)
        em = e_r[...]
        l1v, l4u, l16u = _mm_exact_r(l1_r[...], em), _mm_exact_r(l4c, em), _mm_exact_r(l16c, em)
        o1v = o1_r[...]
        mx = jnp.maximum(jnp.maximum(l1v, l4u), l16u)
        w1, w4, w16 = jnp.exp(l1v - mx), jnp.exp(l4u - mx), jnp.exp(l16u - mx)
        den = w1 + w4 + w16
        attn = (w1 * o1v + w4 * o4u + w16 * o16u) / den
        lse = mx + jnp.log(den)
        gm = g_r[...]

        def head_mean_a(t):
            return _mm_exact_r(t, gm)

        def head_mean_h(t):
            return jnp.concatenate(
                [jnp.broadcast_to(jnp.mean(t[:, h * 128:(h + 1) * 128], axis=-1, keepdims=True), (TT, 128))
                 for h in range(4)], axis=1)

        rs_a = lax.rsqrt(head_mean_a(attn * attn) + EPS)
        n_a = attn * rs_a
        agv = ag_r[...]
        sg_a = _sigmoid(agv)
        si_a = agv * sg_a
        anw_v = anw_r[...]
        y_a = (n_a * anw_v) * si_a
        recv = rec_r[...]
        rs_h = lax.rsqrt(head_mean_h(recv * recv) + EPS)
        n_h = recv * rs_h
        hgv = hg_r[...]
        sg_h = _sigmoid(hgv)
        si_h = hgv * sg_h
        hnw_v = hnw_r[...]
        y_h = (n_h * hnw_v) * si_h
        mixed = jnp.concatenate([y_a, y_h], axis=1).astype(BF16)
        xv = x_r[...]
        x2 = xv + _mm(mixed, wo_r[...])
        r2 = lax.rsqrt(jnp.mean(x2 * x2, axis=-1, keepdims=True) + EPS)
        fnw_v = fnw_r[...]
        xn = x2 * r2
        err = xn * fnw_v - tgt_r[...]
        small_o[2:3, :] += 0.5 * jnp.sum(jnp.mean(err * err, axis=-1, keepdims=True), axis=0, keepdims=True)
        dy = err * (1.0 / D)
        small_o[0:1, :] += jnp.sum(dy * xn, axis=0, keepdims=True)
        dyw = dy * fnw_v
        dx2 = r2 * dyw - x2 * ((r2 * r2 * r2) * jnp.mean(dyw * x2, axis=-1, keepdims=True))
        dx2_o[...] = dx2
        dx2b = dx2.astype(BF16)
        gwout_o[...] += _mm_tn(mixed, dx2b)
        dmix = _mm_nt(dx2b, wo_r[...])
        dm_a, dm_h = dmix[:, :AW], dmix[:, AW:]
        dag_o[...] = (dm_a * (n_a * anw_v) * (sg_a * (1.0 + agv * (1.0 - sg_a)))).astype(BF16)
        dn_a = dm_a * anw_v * si_a
        small_o[1:2, 0:AW] += jnp.sum(dm_a * n_a * si_a, axis=0, keepdims=True)
        dattn = rs_a * (dn_a - n_a * head_mean_a(dn_a * n_a))
        delta = head_mean_a(dattn * attn) * float(HEAD)
        perm_out(dattn, do1_o, do4_o, do16_o, BF16)
        stats = _mm_exact_r(lse, sel_r[0]) + _mm_exact_r(delta, sel_r[1])
        perm_out(stats, st1_o, st4_o, st16_o, F32)
        dhg_o[...] = (dm_h * (n_h * hnw_v) * (sg_h * (1.0 + hgv * (1.0 - sg_h)))).astype(BF16)
        dn_h = dm_h * hnw_v * si_h
        small_o[1:2, AW:] += jnp.sum(dm_h * n_h * si_h, axis=0, keepdims=True)
        drec_o[...] = rs_h * (dn_h - n_h * head_mean_h(dn_h * n_h))

        @pl.when(pl.program_id(0) == T // TT - 1)
        def _():
            x, y, c = lax.axis_index("x"), lax.axis_index("y"), lax.axis_index("c")
            cps = [pltpu.make_async_remote_copy(
                src_ref=gwout_o.at[pl.ds(pl.multiple_of(j * 256 + (1 - c) * 128, 128), 128), :], dst_ref=rbuf.at[j],
                send_sem=send_sems.at[j], recv_sem=recv_sems.at[j], device_id=(x, y, 1 - c), device_id_type=MESH)
                for j in range(4)]
            for cp in cps:
                cp.start()
            for j, cp in enumerate(cps):
                cp.wait_recv()
                red = gwout_o[pl.ds(pl.multiple_of(j * 256 + c * 128, 128), 128), :] + rbuf[j]
                rout_o[j * 128:(j + 1) * 128, :] = red
                routb_o[j * 128:(j + 1) * 128, :] = red.astype(BF16)
            for cp in cps:
                cp.wait_send()

    tok = lambda w: pl.BlockSpec((TT, w), lambda i: (i, 0))
    d4 = pl.BlockSpec((4, TT // 4, AW), lambda i: (0, i, 0))
    d16 = pl.BlockSpec((16, TT // 16, AW), lambda i: (0, i, 0))
    const = lambda shape: pl.BlockSpec(shape, lambda i: (0,) * len(shape))
    sd = lambda shape, dt: jax.ShapeDtypeStruct(shape, dt)
    c4 = pl.BlockSpec((4, TT // 4, 128), lambda i: (0, i, 0))
    c16 = pl.BlockSpec((16, TT // 16, 128), lambda i: (0, i, 0))
    p3 = lambda w, dt: [sd((T, w), dt), sd((4, T // 4, w), dt), sd((16, T // 16, w), dt)]
    return pl.pallas_call(
        body, name="fwd_out", grid=(T // TT,),
        in_specs=[tok(AW), d4, d16, tok(128), c4, c16, tok(AW), tok(AW), tok(AW), tok(D), tok(D),
                  const((1, AW)), const((1, HW)), const((1, D)), const((D, D)), const((AW, AW)),
                  const((128, AW)), const((2, AW, 128))],
        out_specs=[tok(D)] + [tok(AW), d4, d16] + [tok(128), c4, c16] + [tok(AW)] * 3
        + [const((512, D)), const((512, D)), const((8, D))],
        out_shape=[sd((T, D), F32)] + p3(AW, BF16) + p3(128, F32)
        + [sd((T, AW), F32), sd((T, AW), BF16), sd((T, AW), BF16), sd((512, D), F32), sd((512, D), BF16),
           sd((8, D), F32)],
        scratch_shapes=[pltpu.VMEM((4, TT, 128), F32), pltpu.VMEM((4, TT, 128), F32), pltpu.VMEM((D, D), F32),
                        pltpu.VMEM((4, 128, D), F32), pltpu.SemaphoreType.DMA((4,)), pltpu.SemaphoreType.DMA((4,))],
        compiler_params=_cp(("arbitrary",)),
    )(o1, o4, o16, l1, l4, l16, rec, ag, hg, x, tgt, anw, hnw, fnw, wout_full, gmat, emat, selmat)


def _dproj_build(dq, dk, dv, dag, dhq, dhf, dhi, dhg, pos):
    TT = 256

    def body(dq1, dq4, dq16, dk1, dk4, dk16, dv1, dv4, dv16, dag_r, dhq_r, dhf_r, dhi_r, dhg_r,
             pos_r, dproj_o, scr_a, scr_b):
        def unperm_sum(r1, r4, r16):
            u4, u16 = _unperm_load(r4, r16, scr_a, scr_b)
            return r1[...] + u4 + u16

        cosf, s1, s2 = _rope_tables(pos_r[...])
        dproj_o[:, 0:512] = _rope_bwd(unperm_sum(dq1, dq4, dq16), cosf, s1, s2).astype(BF16)
        dproj_o[:, 512:1024] = _rope_bwd(unperm_sum(dk1, dk4, dk16), cosf, s1, s2).astype(BF16)
        dproj_o[:, 1024:1536] = unperm_sum(dv1, dv4, dv16).astype(BF16)
        dproj_o[:, 1536:2048] = dag_r[...]
        dproj_o[:, 2048:2560] = dhq_r[...]
        dproj_o[:, 2560:3072] = dhf_r[...]
        dproj_o[:, 3072:3584] = dhi_r[...]
        dproj_o[:, 3584:4096] = dhg_r[...]

    tok = lambda w: pl.BlockSpec((TT, w), lambda i: (i, 0))
    d4 = pl.BlockSpec((4, TT // 4, AW), lambda i: (0, i, 0))
    d16 = pl.BlockSpec((16, TT // 16, AW), lambda i: (0, i, 0))
    return pl.pallas_call(
        body, name="dproj_build", grid=(T // TT,),
        in_specs=[tok(AW), d4, d16] * 3 + [tok(AW)] * 5 + [tok(1)],
        out_specs=tok(NCOL),
        out_shape=jax.ShapeDtypeStruct((T, NCOL), BF16),
        scratch_shapes=[pltpu.VMEM((4, TT, 128), F32), pltpu.VMEM((4, TT, 128), F32)],
        compiler_params=_cp(("parallel",)),
    )(*dq, *dk, *dv, dag, dhq, dhf, dhi, dhg, pos)


def _bwd_x(dproj, x, dx2, mixw, w_full, rin, rinb, rout, routb, small4, small6):
    TT = 256
    NT = T // TT

    def body(dp_r, x_r, dx2_r, mw_r, w_r, rin_r, rinb_r, rout_r, routb_r, s4_r, s6_r,
             gx_o, pin_o, pinr_o, pout_o, poutr_o, sall_o, sbuf, send_sems, recv_sems, loc_sems):
        i = pl.program_id(0)
        loc, rem = _chip_copies(rin_r, rinb_r, rout_r, routb_r, pin_o, pinr_o, pout_o, poutr_o,
                                send_sems, recv_sems, loc_sems)

        @pl.when(i == 0)
        def _():
            sbuf[...] = jnp.zeros_like(sbuf)
            for cp in loc + rem:
                cp.start()

        dhn = _mm_nt(dp_r[...], w_r[...])
        xv = x_r[...]
        r = lax.rsqrt(jnp.mean(xv * xv, axis=-1, keepdims=True) + EPS)
        dxw = dhn * mw_r[...]
        gx_o[...] = dx2_r[...] + r * dxw - xv * ((r * r * r) * jnp.mean(dxw * xv, axis=-1, keepdims=True))
        sbuf[16:17, :] += jnp.sum(dhn * (xv * r), axis=0, keepdims=True)

        @pl.when(i == NT - 1)
        def _():
            sbuf[0:8, :] = s4_r[...]
            sbuf[8:16, :] = s6_r[...]
            sloc, srem = _small_copies(sbuf, sall_o, send_sems, recv_sems, loc_sems)
            for cp in sloc + srem:
                cp.start()
            for cp in rem + srem:
                cp.wait_recv()
            for cp in rem + srem:
                cp.wait_send()
            for cp in loc + sloc:
                cp.wait()

    tok = lambda w: pl.BlockSpec((TT, w), lambda i: (i, 0))
    const = lambda shape: pl.BlockSpec(shape, lambda i: (0,) * len(shape))
    hbm = pl.BlockSpec(memory_space=pltpu.HBM)
    return pl.pallas_call(
        body, name="bwd_x", grid=(NT,),
        in_specs=[tok(NCOL), tok(D), tok(D), const((1, D)), const((D, NCOL)), hbm, hbm, hbm, hbm,
                  const((8, D)), const((8, D))],
        out_specs=[tok(D), hbm, hbm, hbm, hbm, hbm],
        out_shape=[jax.ShapeDtypeStruct((T, D), F32),
                   jax.ShapeDtypeStruct((512, 1024), F32), jax.ShapeDtypeStruct((3, 512, 1024), BF16),
                   jax.ShapeDtypeStruct((128, D), F32), jax.ShapeDtypeStruct((3, 128, D), BF16),
                   jax.ShapeDtypeStruct((8, 24, D), F32)],
        scratch_shapes=[pltpu.VMEM((24, D), F32), pltpu.SemaphoreType.DMA((13,)), pltpu.SemaphoreType.DMA((13,)),
                        pltpu.SemaphoreType.DMA((3,))],
        compiler_params=_cp(("arbitrary",)),
    )(dproj, x, dx2, mixw, w_full, rin, rinb, rout, routb, small4, small6)


def _grad_w_in(hn, dproj):
    TK = 1024
    NK = T // TK

    def body(hnt_r, dp_r, rin_o, rinb_o, acc, rbuf, obuf, obufb, send_sems, recv_sems, wb_sems):
        j = pl.program_id(0)
        kk = pl.program_id(1)
        x, y, c = lax.axis_index("x"), lax.axis_index("y"), lax.axis_index("c")
        mine = pl.ds(pl.multiple_of(c * 512, 512), 512)
        theirs = pl.ds(pl.multiple_of((1 - c) * 512, 512), 512)

        def send(jj):
            return pltpu.make_async_remote_copy(
                src_ref=acc.at[jj % 2, theirs, :], dst_ref=rbuf.at[jj], send_sem=send_sems.at[jj],
                recv_sem=recv_sems.at[jj], device_id=(x, y, 1 - c), device_id_type=MESH)

        def writeback(jj):
            cols = pl.ds(jj * 1024, 1024)
            return [pltpu.make_async_copy(obuf.at[jj % 2], rin_o.at[:, cols], wb_sems.at[jj % 2]),
                    pltpu.make_async_copy(obufb.at[jj % 2], rinb_o.at[:, cols], wb_sems.at[2 + jj % 2])]

        def wait_writeback(jj):
            for cp in writeback(jj):
                cp.wait()

        def finalize(jj):
            send(jj).wait_recv()
            red = acc[jj % 2, mine, :] + rbuf[jj]
            obuf[jj % 2] = red
            obufb[jj % 2] = red.astype(BF16)
            for cp in writeback(jj):
                cp.start()

        prod = _mm(hnt_r[...], dp_r[...])

        @pl.when(kk == 0)
        def _():
            for jj in (2, 3):
                @pl.when(j == jj)
                def _():
                    send(jj - 2).wait_send()
            acc[j % 2] = prod

        @pl.when(kk > 0)
        def _():
            acc[j % 2] += prod

        @pl.when(kk == NK - 1)
        def _():
            for jj in range(4):
                @pl.when(j == jj)
                def _():
                    send(jj).start()
                    if jj in (1, 2):
                        finalize(jj - 1)
                    if jj == 3:
                        wait_writeback(0)
                        finalize(2)
                        wait_writeback(1)
                        finalize(3)
                        wait_writeback(2)
                        wait_writeback(3)
                        send(2).wait_send()
                        send(3).wait_send()

    hbm = pl.BlockSpec(memory_space=pltpu.HBM)
    return pl.pallas_call(
        body, name="grad_w_in", grid=(4, NK),
        in_specs=[pl.BlockSpec((D, TK), lambda j, kk: (0, kk)), pl.BlockSpec((TK, 1024), lambda j, kk: (kk, j))],
        out_specs=[hbm, hbm],
        out_shape=[jax.ShapeDtypeStruct((512, NCOL), F32), jax.ShapeDtypeStruct((512, NCOL), BF16)],
        scratch_shapes=[pltpu.VMEM((2, D, 1024), F32), pltpu.VMEM((4, 512, 1024), F32), pltpu.VMEM((2, 512, 1024), F32),
                        pltpu.VMEM((2, 512, 1024), BF16),
                        pltpu.SemaphoreType.DMA((4,)), pltpu.SemaphoreType.DMA((4,)), pltpu.SemaphoreType.DMA((4,))],
        compiler_params=_cp(("arbitrary", "arbitrary")),
    )(hn, dproj)


def _chip_sum(own, rem, name):
    rows, cols = own.shape
    tr = min(rows, 256)

    def body(own_r, rem_r, o_r):
        acc = own_r[...]
        for s in range(3):
            acc = acc + rem_r[s].astype(F32)
        o_r[...] = acc

    return pl.pallas_call(
        body, name=name, grid=(rows // tr,),
        in_specs=[pl.BlockSpec((tr, cols), lambda i: (i, 0)), pl.BlockSpec((3, tr, cols), lambda i: (0, i, 0))],
        out_specs=pl.BlockSpec((tr, cols), lambda i: (i, 0)),
        out_shape=jax.ShapeDtypeStruct((rows, cols), F32),
        compiler_params=_cp(("parallel",)),
    )(own, rem)


def _chip_copies(rin_r, rinb_r, rout_r, routb_r, pin_o, pinr_o, pout_o, poutr_o, send_sems, recv_sems, loc_sems):
    x, y, c = lax.axis_index("x"), lax.axis_index("y"), lax.axis_index("c")
    chips = [(1 - x, y), (x, 1 - y), (1 - x, 1 - y)]
    jm = 2 * x + y
    loc = [pltpu.make_async_copy(rin_r.at[:, pl.ds(jm * 1024, 1024)], pin_o, loc_sems.at[0]),
           pltpu.make_async_copy(rout_r.at[pl.ds(jm * 128, 128), :], pout_o, loc_sems.at[1])]
    rem = []
    for k, (px, py) in enumerate(chips):
        j = 2 * px + py
        rem.append(pltpu.make_async_remote_copy(
            src_ref=rinb_r.at[:, pl.ds(j * 1024, 1024)], dst_ref=pinr_o.at[k],
            send_sem=send_sems.at[k], recv_sem=recv_sems.at[k], device_id=(px, py, c), device_id_type=MESH))
        rem.append(pltpu.make_async_remote_copy(
            src_ref=routb_r.at[pl.ds(j * 128, 128), :], dst_ref=poutr_o.at[k],
            send_sem=send_sems.at[3 + k], recv_sem=recv_sems.at[3 + k], device_id=(px, py, c),
            device_id_type=MESH))
    return loc, rem


def _small_copies(small_r, sall_o, send_sems, recv_sems, loc_sems):
    x, y, c = lax.axis_index("x"), lax.axis_index("y"), lax.axis_index("c")
    me = 4 * x + 2 * y + c
    loc = [pltpu.make_async_copy(small_r, sall_o.at[me], loc_sems.at[2])]
    rem = []
    k = 6
    for fx in range(2):
        for fy in range(2):
            for fc in range(2):
                if fx or fy or fc:
                    peer = (1 - x if fx else x, 1 - y if fy else y, 1 - c if fc else c)
                    rem.append(pltpu.make_async_remote_copy(
                        src_ref=small_r, dst_ref=sall_o.at[me], send_sem=send_sems.at[k],
                        recv_sem=recv_sems.at[k], device_id=peer, device_id_type=MESH))
                    k += 1
    return loc, rem


def _pair_share(pin, pout):
    def body(pin_r, pout_r, fin_o, fout_o, send_sems, recv_sems):
        x, y, c = lax.axis_index("x"), lax.axis_index("y"), lax.axis_index("c")
        sibling = (x, y, 1 - c)
        rem = [pltpu.make_async_remote_copy(src_ref=pin_r, dst_ref=fin_o.at[c], send_sem=send_sems.at[0],
                                            recv_sem=recv_sems.at[0], device_id=sibling, device_id_type=MESH),
               pltpu.make_async_remote_copy(src_ref=pout_r, dst_ref=fout_o.at[c], send_sem=send_sems.at[1],
                                            recv_sem=recv_sems.at[1], device_id=sibling, device_id_type=MESH)]
        for cp in rem:
            cp.start()
        fin_o[c] = pin_r[...]
        fout_o[c] = pout_r[...]
        for cp in rem:
            cp.wait_recv()
        for cp in rem:
            cp.wait_send()

    vm = pl.BlockSpec(memory_space=pltpu.VMEM)
    return pl.pallas_call(
        body, name="pair_share",
        out_shape=(jax.ShapeDtypeStruct((2, 512, 1024), F32), jax.ShapeDtypeStruct((2, 128, D), F32)),
        in_specs=[vm, vm], out_specs=(vm, vm),
        scratch_shapes=[pltpu.SemaphoreType.DMA((2,)), pltpu.SemaphoreType.DMA((2,))],
        compiler_params=_cp(),
    )(pin, pout)


def _adamw_math(w, g, m, v):
    m = B1 * m + (1.0 - B1) * g
    v = B2 * v + (1.0 - B2) * (g * g)
    m_hat = m / (1.0 - B1 ** STEP)
    v_hat = v / (1.0 - B2 ** STEP)
    delta = -LR * (m_hat / (jnp.sqrt(v_hat) + AEPS) + WD * w)
    return delta, m, v


def _adamw(w, g, m, v, name):
    rows, cols = w.shape
    tr = min(rows, 256)

    def body(w_r, g_r, m_r, v_r, d_o, m_o, v_o):
        d, mm, vv = _adamw_math(w_r[...], g_r[...], m_r[...], v_r[...])
        d_o[...] = d
        m_o[...] = mm
        v_o[...] = vv

    blk = pl.BlockSpec((tr, cols), lambda i: (i, 0))
    return pl.pallas_call(
        body, name=name, grid=(rows // tr,),
        in_specs=[blk] * 4, out_specs=[blk] * 3,
        out_shape=[jax.ShapeDtypeStruct((rows, cols), F32)] * 3,
        compiler_params=_cp(("parallel",)),
    )(w, g, m, v)


def _adamw_small(sall, params):
    def body(sall_r, *refs):
        ins, outs = refs[:15], refs[15:]
        tot = sall_r[0]
        for dv in range(1, 8):
            tot = tot + sall_r[dv]
        grads = [tot[16:17, :], tot[1:2, 0:AW], tot[1:2, AW:], tot[8:10, 0:HW], tot[0:1, :]]
        outs[0][...] = tot[2:3, 0:1]
        for p in range(5):
            w_r, m_r, v_r = ins[3 * p:3 * p + 3]
            g = grads[p]
            d, mm, vv = _adamw_math(w_r[...], g, m_r[...], v_r[...])
            outs[1 + 4 * p][...] = g
            outs[2 + 4 * p][...] = d
            outs[3 + 4 * p][...] = mm
            outs[4 + 4 * p][...] = vv

    flat = [a for p in params for a in p]
    shapes = [jax.ShapeDtypeStruct((1, 1), F32)]
    for p in params:
        shapes += [jax.ShapeDtypeStruct(p[0].shape, F32)] * 4
    vm = pl.BlockSpec(memory_space=pltpu.VMEM)
    return pl.pallas_call(
        body, name="adamw_small",
        in_specs=[vm] * 16, out_specs=[vm] * 21, out_shape=shapes,
        compiler_params=_cp(),
    )(sall, *flat)


def kernel(x, positions, w_in, w_out, mix_norm_w, attn_out_norm_w, hgrn_out_norm_w, hgrn_lb_raw, final_norm_w, loss_target, m_w_in, m_w_out, m_mix_norm_w, m_attn_out_norm_w, m_hgrn_out_norm_w, m_hgrn_lb_raw, m_final_norm_w, v_w_in, v_w_out, v_mix_norm_w, v_attn_out_norm_w, v_hgrn_out_norm_w, v_hgrn_lb_raw, v_final_norm_w):
    xs = x.reshape(T, D)
    tgt = loss_target.reshape(T, D)
    pos = positions.reshape(T, 1)
    fnw = final_norm_w.reshape(1, D)

    ti = np.arange(TH)
    tri_np = ((ti[:, None] // CHUNK == ti[None, :] // CHUNK) & (ti[None, :] <= ti[:, None])).astype(np.float32)
    tri = jnp.asarray(tri_np, BF16)
    trit = jnp.asarray(tri_np.T, BF16)
    hi_ = np.arange(AW) // HEAD
    gmat = jnp.asarray((hi_[:, None] == hi_[None, :]).astype(np.float32) / HEAD, BF16)
    emat_np = (np.arange(128)[:, None] == hi_[None, :]).astype(np.float32)
    sel_np = np.zeros((2, AW, 128), np.float32)
    sel_np[0, np.arange(8) * HEAD, np.arange(8)] = 1.0
    sel_np[1, np.arange(8) * HEAD, 8 + np.arange(8)] = 1.0
    emat = jnp.asarray(emat_np, BF16)
    selmat = jnp.asarray(sel_np, BF16)

    wb_in, wb_out = _cast_weights(w_in.reshape(D, 1024), w_out.reshape(256, D))
    jm_arr = (2 * lax.axis_index("x") + lax.axis_index("y")).astype(jnp.int32).reshape(1)
    (hn, q1, k1, v1, q4, k4, v4, q16, k16, v16, ag, hq, hf, hi, hg, w_full, wout4) = _fwd_in(
        xs, pos, mix_norm_w, wb_in, wb_out, jm_arr)
    wout_full = wout4.reshape(D, D)
    flat = lambda a: a.reshape(T, AW)
    o1, l1 = _attn_fwd(q1, k1, v1, T // BLK, "attn_fwd_d1")
    o4, l4 = _attn_fwd(flat(q4), flat(k4), flat(v4), T // 4 // BLK, "attn_fwd_d4")
    o16, l16 = _attn_fwd(flat(q16), flat(k16), flat(v16), T // 16 // BLK, "attn_fwd_d16")
    rec, sall = _hgrn_fwd(hq, hf, hi, hgrn_lb_raw, tri)

    (dx2, do1, do4, do16, st1, st4, st16, drec, dag, dhg, rout, routb, small4) = _fwd_out(
        o1, o4.reshape(4, T // 4, AW), o16.reshape(16, T // 16, AW),
        l1, l4.reshape(4, T // 4, 128), l16.reshape(16, T // 16, 128),
        rec, ag, hg, xs, tgt, attn_out_norm_w, hgrn_out_norm_w, fnw, wout_full, gmat, emat, selmat)

    fst = lambda a: a.reshape(T, 128)
    dq1, dk1, dv1 = _attn_bwd(q1, k1, v1, do1, st1, T // BLK, "attn_bwd_d1")
    dq4, dk4, dv4 = _attn_bwd(flat(q4), flat(k4), flat(v4), flat(do4), fst(st4), T // 4 // BLK, "attn_bwd_d4")
    dq16, dk16, dv16 = _attn_bwd(flat(q16), flat(k16), flat(v16), flat(do16), fst(st16), T // 16 // BLK,
                                 "attn_bwd_d16")
    dhq, dhf, dhi, small6 = _hgrn_bwd(hq, hf, hi, hgrn_lb_raw, tri, trit, drec, sall)

    r4 = lambda a: a.reshape(4, T // 4, AW)
    r16 = lambda a: a.reshape(16, T // 16, AW)
    dproj = _dproj_build((dq1, r4(dq4), r16(dq16)), (dk1, r4(dk4), r16(dk16)), (dv1, r4(dv4), r16(dv16)),
                         dag, dhq, dhf, dhi, dhg, pos)
    rin, rinb = _grad_w_in(hn, dproj)
    gx, pin_own, pin_rem, pout_own, pout_rem, small_all = _bwd_x(
        dproj, xs, dx2, mix_norm_w, w_full, rin, rinb, rout, routb, small4, small6)
    pin = _chip_sum(pin_own, pin_rem, "chip_sum_in")
    pout = _chip_sum(pout_own, pout_rem, "chip_sum_out")
    fin, fout = _pair_share(pin, pout)
    g_w_in = fin.reshape(D, 1024)
    g_w_out = fout.reshape(256, D)

    d_in, nm_in, nv_in = _adamw(w_in.reshape(D, 1024), g_w_in, m_w_in.reshape(D, 1024), v_w_in.reshape(D, 1024),
                                "adamw_w_in")
    d_out, nm_out, nv_out = _adamw(w_out.reshape(256, D), g_w_out, m_w_out.reshape(256, D), v_w_out.reshape(256, D),
                                   "adamw_w_out")
    params = [(mix_norm_w, m_mix_norm_w, v_mix_norm_w),
              (attn_out_norm_w, m_attn_out_norm_w, v_attn_out_norm_w),
              (hgrn_out_norm_w, m_hgrn_out_norm_w, v_hgrn_out_norm_w),
              (hgrn_lb_raw, m_hgrn_lb_raw, v_hgrn_lb_raw),
              (fnw, m_final_norm_w.reshape(1, D), v_final_norm_w.reshape(1, D))]
    so = _adamw_small(small_all, params)
    loss = so[0].reshape(())
    g_s = [so[1 + 4 * p] for p in range(5)]
    d_s = [so[2 + 4 * p] for p in range(5)]
    m_s = [so[3 + 4 * p] for p in range(5)]
    v_s = [so[4 + 4 * p] for p in range(5)]
    for lst in (g_s, d_s, m_s, v_s):
        lst[4] = lst[4].reshape(D)

    return (loss, gx.reshape(1, T, D),
            g_w_in.reshape(1, D, 1024), g_w_out.reshape(1, 256, D), *g_s,
            d_in.reshape(1, D, 1024), d_out.reshape(1, 256, D), *d_s,
            nm_in.reshape(1, D, 1024), nm_out.reshape(1, 256, D), *m_s,
            nv_in.reshape(1, D, 1024), nv_out.reshape(1, 256, D), *v_s)
```

```python
import functools

import numpy as np
import jax
import jax.numpy as jnp
from jax import lax
from jax.experimental import pallas as pl
from jax.experimental.pallas import tpu as pltpu

F32 = jnp.float32
BF16 = jnp.bfloat16

T = 4096
D = 1024
AW = 512
HW = 512
NCOL = 4096
HEAD = 64
BLK = 128
CHUNK = 64
EPS = 1e-6
SCALE = HEAD ** -0.5
NEG = -1e30
ROPE_THETA = 500000.0
INV_FREQ = [float(v) for v in
            (np.float32(ROPE_THETA) ** (-(np.arange(8, dtype=np.float32)) * np.float32(0.125)))]
LR, B1, B2, AEPS, WD, STEP = 0.001, 0.9, 0.999, 1e-08, 0.01, 10
VMEM_LIMIT = 56 * 1024 * 1024
MESH = pl.DeviceIdType.MESH


def _cp(sem=None, **kw):
    return pltpu.CompilerParams(dimension_semantics=sem, vmem_limit_bytes=VMEM_LIMIT, **kw)


def _mm(a, b):
    return jnp.dot(a, b, preferred_element_type=F32)


def _mm_nt(a, b):
    return lax.dot_general(a, b, (((1,), (1,)), ((), ())), preferred_element_type=F32)


def _mm_tn(a, b):
    return lax.dot_general(a, b, (((0,), (0,)), ((), ())), preferred_element_type=F32)


def _split3(x):
    h = x.astype(BF16)
    r = x - h.astype(F32)
    m = r.astype(BF16)
    l = (r - m.astype(F32)).astype(BF16)
    return h, m, l


def _mm_exact_l(mat_bf, x):
    h, m, l = _split3(x)
    return _mm(mat_bf, h) + _mm(mat_bf, m) + _mm(mat_bf, l)


def _mm_exact_r(x, mat_bf):
    h = x.astype(BF16)
    l = (x - h.astype(F32)).astype(BF16)
    return _mm(h, mat_bf) + _mm(l, mat_bf)


def _sigmoid(x):
    return 0.5 * jnp.tanh(0.5 * x) + 0.5


def _rope_tables(pos):
    lane = lax.broadcasted_iota(jnp.int32, (1, 128), 1)
    jl = lane & 63
    fi = jl & 7
    inv = jnp.zeros((1, 128), F32)
    for kk in range(8):
        inv = jnp.where(fi == kk, INV_FREQ[kk], inv)
    ang = pos.astype(F32) * inv
    c = jnp.cos(ang)
    s = jnp.sin(ang)
    cosf = jnp.where(jl < 16, c, 1.0)
    s1 = jnp.where(jl < 8, -s, 0.0)
    s2 = jnp.where((jl >= 8) & (jl < 16), s, 0.0)
    return cosf, s1, s2


def _rope(t, cosf, s1, s2):
    parts = []
    for ci in range(t.shape[1] // 128):
        tc = t[:, ci * 128:(ci + 1) * 128]
        parts.append(tc * cosf + pltpu.roll(tc, 120, 1) * s1 + pltpu.roll(tc, 8, 1) * s2)
    return jnp.concatenate(parts, axis=1)


def _rope_bwd(g, cosf, s1, s2):
    parts = []
    for ci in range(g.shape[1] // 128):
        gc = g[:, ci * 128:(ci + 1) * 128]
        parts.append(gc * cosf + pltpu.roll(gc * s1, 8, 1) + pltpu.roll(gc * s2, 120, 1))
    return jnp.concatenate(parts, axis=1)


def _perm_store(val, scr, o1, o4, o16, dt):
    n = val.shape[0]
    o1[...] = val.astype(dt)
    for ci in range(val.shape[1] // 128):
        cs = slice(ci * 128, (ci + 1) * 128)
        scr[ci] = val[:, cs]
        for rr in range(4):
            o4[rr, :, cs] = scr[ci, pl.ds(rr, n // 4, stride=4), :].astype(dt)
        for rr in range(16):
            o16[rr, :, cs] = scr[ci, pl.ds(rr, n // 16, stride=16), :].astype(dt)


def _unperm_load(r4, r16, scr_a, scr_b):
    n = scr_a.shape[1]
    nc = r4.shape[-1] // 128
    for ci in range(nc):
        cs = slice(ci * 128, (ci + 1) * 128)
        for rr in range(4):
            scr_a[ci, pl.ds(rr, n // 4, stride=4), :] = r4[rr, :, cs].astype(F32)
        for rr in range(16):
            scr_b[ci, pl.ds(rr, n // 16, stride=16), :] = r16[rr, :, cs].astype(F32)
    return (jnp.concatenate([scr_a[ci] for ci in range(nc)], axis=1),
            jnp.concatenate([scr_b[ci] for ci in range(nc)], axis=1))


def _cast_weights(w_in, w_out):
    def body(win_ref, wout_ref, bin_ref, bout_ref):
        bin_ref[...] = win_ref[...].astype(BF16)

        @pl.when(pl.program_id(0) == 0)
        def _():
            bout_ref[...] = wout_ref[...].astype(BF16)

    return pl.pallas_call(
        body, name="cast_weights", grid=(4,),
        in_specs=[pl.BlockSpec((256, 1024), lambda i: (i, 0)), pl.BlockSpec((256, D), lambda i: (0, 0))],
        out_specs=[pl.BlockSpec((256, 1024), lambda i: (i, 0)), pl.BlockSpec((256, D), lambda i: (0, 0))],
        out_shape=(jax.ShapeDtypeStruct((D, 1024), BF16), jax.ShapeDtypeStruct((256, D), BF16)),
        compiler_params=_cp(("arbitrary",)),
    )(w_in, w_out)


def _fwd_in(x, pos, mixw, wb_in, wb_out, jm_arr):
    TT = 512
    NT = T // TT

    def body(jm_ref, x_ref, pos_ref, mw_ref, wbin_ref, wbout_ref,
             hnt_ref, q1, k1, v1, q4, k4, v4, q16, k16, v16, ag, hq, hf, hi, hg, wfull_o, woutfull_o,
             wbuf, wobuf, hn_all, scr, send_sems, recv_sems, loc_sems):
        s = pl.program_id(0)
        i = pl.program_id(1)
        mx, my, c = lax.axis_index("x"), lax.axis_index("y"), lax.axis_index("c")
        me, sibling = (mx, my, c), (mx, my, 1 - c)
        chips = [(mx, 1 - my), (1 - mx, my), (1 - mx, 1 - my)]
        jm = 2 * mx + my
        rows_in = [pl.ds(pl.multiple_of(h * 512, 512), 512) for h in (c, 1 - c)]
        rows_out = [pl.ds(pl.multiple_of(h * 128, 128), 128) for h in (c, 1 - c)]

        def blk(k):
            return lax.bitwise_xor(jm, k + 1)

        def rc(n, ref, to):
            return pltpu.make_async_remote_copy(src_ref=ref, dst_ref=ref, send_sem=send_sems.at[n],
                                                recv_sem=recv_sems.at[n], device_id=to, device_id_type=MESH)

        send_in = lambda k: rc(k, wbuf.at[jm, rows_in[0], :], (*chips[k], c))
        send_out = lambda k: rc(3 + k, wobuf.at[jm, rows_out[0], :], (*chips[k], c))
        got_in = lambda k: rc(k, wbuf.at[blk(k), rows_in[0], :], me)
        got_out = lambda k: rc(3 + k, wobuf.at[blk(k), rows_out[0], :], me)
        pass_in = lambda k: rc(6 + k, wbuf.at[blk(k), rows_in[0], :], sibling)
        pass_out = lambda k: rc(9 + k, wobuf.at[blk(k), rows_out[0], :], sibling)
        passed_in = lambda k: rc(6 + k, wbuf.at[blk(k), rows_in[1], :], me)
        passed_out = lambda k: rc(9 + k, wobuf.at[blk(k), rows_out[1], :], me)

        def keep(j, n):
            return pltpu.make_async_copy(wbuf.at[j], wfull_o.at[:, pl.ds(j * 1024, 1024)], loc_sems.at[n])

        @pl.when((s == 0) & (i == 0))
        def _():
            own = [pltpu.make_async_copy(wbin_ref, wbuf.at[jm], loc_sems.at[4]),
                   pltpu.make_async_copy(wbout_ref, wobuf.at[jm], loc_sems.at[5])]
            for cp in own:
                cp.start()
            for cp in own:
                cp.wait()
            send_in(0).start()
            send_in(1).start()
            keep(jm, 0).start()

        def arrive(k):
            if k == 0:
                send_in(0).wait_send()
                send_in(1).wait_send()
                send_in(2).start()
            got_in(k).wait_recv()
            pass_in(k).start()
            passed_in(k).wait_recv()
            keep(blk(k), k + 1).start()
            if k == 2:
                for kk in range(3):
                    send_out(kk).start()

        for k in range(3):
            pl.when((s == k + 1) & (i == 0))(functools.partial(arrive, k))

        tile = pl.ds(pl.multiple_of(i * TT, TT), TT)

        @pl.when(s == 0)
        def _():
            xv = x_ref[...]
            r = lax.rsqrt(jnp.mean(xv * xv, axis=-1, keepdims=True) + EPS)
            hnf = (xv * r) * mw_ref[...]
            hn_all[tile, :] = hnf.astype(BF16)
            hnt_ref[...] = hnf.T.astype(BF16)

        def project(jj):
            hn = hn_all[tile, :]
            lo = _mm(hn, wbuf[jj, :, 0:512])
            hi_cols = _mm(hn, wbuf[jj, :, 512:1024])
            if jj == 0:
                cosf, s1, s2 = _rope_tables(pos_ref[...])
                _perm_store(_rope(lo, cosf, s1, s2), scr, q1, q4, q16, BF16)
                _perm_store(_rope(hi_cols, cosf, s1, s2), scr, k1, k4, k16, BF16)
            elif jj == 1:
                _perm_store(lo, scr, v1, v4, v16, BF16)
                ag[...] = hi_cols
            elif jj == 2:
                hq[...] = lo
                hf[...] = hi_cols
            else:
                hi[...] = lo.astype(BF16)
                hg[...] = hi_cols

        j = lax.bitwise_xor(jm, s)
        for jj in range(4):
            pl.when(j == jj)(functools.partial(project, jj))

        @pl.when((s == 3) & (i == NT - 1))
        def _():
            for k in range(3):
                got_out(k).wait_recv()
                pass_out(k).start()
            for k in range(3):
                passed_out(k).wait_recv()
            out = pltpu.make_async_copy(wobuf, woutfull_o, loc_sems.at[4])
            out.start()
            send_in(2).wait_send()
            for k in range(3):
                send_out(k).wait_send()
                pass_in(k).wait_send()
                pass_out(k).wait_send()
            keep(jm, 0).wait()
            for k in range(3):
                keep(blk(k), k + 1).wait()
            out.wait()

    def at_stage_of(jb):
        def index(s, i, jm_ref):
            sa = lax.bitwise_xor(jm_ref[0], jb)
            return jnp.where(s < sa, 0, jnp.where(s == sa, i, NT - 1))
        return index

    tok = lambda w, jb: pl.BlockSpec((TT, w), lambda s, i, jm_ref: (at_stage_of(jb)(s, i, jm_ref), 0))
    d4 = lambda jb: pl.BlockSpec((4, TT // 4, AW), lambda s, i, jm_ref: (0, at_stage_of(jb)(s, i, jm_ref), 0))
    d16 = lambda jb: pl.BlockSpec((16, TT // 16, AW), lambda s, i, jm_ref: (0, at_stage_of(jb)(s, i, jm_ref), 0))
    hbm = pl.BlockSpec(memory_space=pltpu.HBM)
    sd = lambda shape, dt: jax.ShapeDtypeStruct(shape, dt)
    in_own_stage = lambda s, i: jnp.where(s == 0, i, NT - 1)
    grid_spec = pltpu.PrefetchScalarGridSpec(
        num_scalar_prefetch=1, grid=(4, NT),
        in_specs=[pl.BlockSpec((TT, D), lambda s, i, jm_ref: (in_own_stage(s, i), 0)),
                  pl.BlockSpec((TT, 1), lambda s, i, jm_ref: (i, 0)),
                  pl.BlockSpec((1, D), lambda s, i, jm_ref: (0, 0)), hbm, hbm],
        out_specs=[pl.BlockSpec((D, TT), lambda s, i, jm_ref: (0, in_own_stage(s, i))),
                   tok(AW, 0), tok(AW, 0), tok(AW, 1), d4(0), d4(0), d4(1), d16(0), d16(0), d16(1),
                   tok(AW, 1), tok(AW, 2), tok(AW, 2), tok(AW, 3), tok(AW, 3), hbm, hbm],
        scratch_shapes=[pltpu.VMEM((4, D, 1024), BF16), pltpu.VMEM((4, 256, D), BF16), pltpu.VMEM((T, D), BF16),
                        pltpu.VMEM((4, TT, 128), F32), pltpu.SemaphoreType.DMA((12,)),
                        pltpu.SemaphoreType.DMA((12,)), pltpu.SemaphoreType.DMA((6,))])
    return pl.pallas_call(
        body, name="fwd_in", grid_spec=grid_spec,
        out_shape=[sd((D, T), BF16)] + [sd((T, AW), BF16)] * 3 + [sd((4, T // 4, AW), BF16)] * 3
        + [sd((16, T // 16, AW), BF16)] * 3
        + [sd((T, AW), F32), sd((T, AW), F32), sd((T, AW), F32), sd((T, AW), BF16), sd((T, AW), F32),
           sd((D, NCOL), BF16), sd((4, 256, D), BF16)],
        compiler_params=_cp(("arbitrary", "arbitrary")),
    )(jm_arr, x, pos, mixw, wb_in, wb_out)


def _band_mask(key_axis, nkeys=2 * BLK):
    shape = (nkeys, 2 * BLK) if key_axis == 0 else (2 * BLK, nkeys)
    kj = lax.broadcasted_iota(jnp.int32, shape, key_axis)
    qi = lax.broadcasted_iota(jnp.int32, shape, 1 - key_axis) & (BLK - 1)
    return (kj >= qi) & (kj <= qi + BLK), kj, qi


def _stack_heads(t2, in_a):
    z = jnp.zeros_like(t2)
    return jnp.concatenate([jnp.where(in_a[0], t2, z), jnp.where(in_a[1], t2, z)], axis=0)


def _attn_fwd(q, k, v, nb, name):
    n = min(4, nb)
    CH = n * BLK
    halo = nb > n

    def body(*refs):
        if halo:
            q_ref, k_ref, v_ref, kp_ref, vp_ref, o_ref, lse_ref = refs
        else:
            q_ref, k_ref, v_ref, o_ref, lse_ref = refs
        lane = lax.broadcasted_iota(jnp.int32, (1, 128), 1)
        in_a = [lane < HEAD, lane >= HEAD]
        band, kj, _ = _band_mask(1)
        thr0 = jnp.where((n * pl.program_id(0)) % nb == 0, BLK, 0) if halo else BLK
        mask0 = band & (kj >= thr0)
        for b in range(n):
            rs = slice(b * BLK, (b + 1) * BLK)
            stat = jnp.zeros((BLK, 128), F32)
            for hp in range(4):
                cs = slice(hp * 128, (hp + 1) * 128)
                q2s = _stack_heads(q_ref[rs, cs], in_a)
                if b == 0:
                    kprev = kp_ref[:, cs] if halo else k_ref[rs, cs]
                    vprev = vp_ref[:, cs] if halo else v_ref[rs, cs]
                    kk = jnp.concatenate([kprev, k_ref[rs, cs]], axis=0)
                    vv = jnp.concatenate([vprev, v_ref[rs, cs]], axis=0)
                    mask = mask0
                else:
                    kk = k_ref[(b - 1) * BLK:(b + 1) * BLK, cs]
                    vv = v_ref[(b - 1) * BLK:(b + 1) * BLK, cs]
                    mask = band
                s = jnp.where(mask, _mm_nt(q2s, kk) * SCALE, NEG)
                m = jnp.max(s, axis=-1, keepdims=True)
                p = jnp.exp(s - m)
                l = jnp.sum(p, axis=-1, keepdims=True)
                o = _mm(p.astype(BF16), vv) / l
                lse = m + jnp.log(l)
                o_ref[rs, cs] = jnp.where(in_a[0], o[:BLK], o[BLK:])
                stat = jnp.where(lane == 2 * hp, lse[:BLK], stat)
                stat = jnp.where(lane == 2 * hp + 1, lse[BLK:], stat)
            lse_ref[rs, :] = stat

    cur = pl.BlockSpec((CH, AW), lambda i: (i, 0))
    prev = pl.BlockSpec((BLK, AW), lambda i: (jnp.maximum(n * i - 1, 0), 0))
    return pl.pallas_call(
        body, name=name, grid=(T // CH,),
        in_specs=[cur, cur, cur] + ([prev, prev] if halo else []),
        out_specs=[cur, pl.BlockSpec((CH, 128), lambda i: (i, 0))],
        out_shape=[jax.ShapeDtypeStruct((T, AW), F32), jax.ShapeDtypeStruct((T, 128), F32)],
        compiler_params=_cp(("parallel",)),
    )(*((q, k, v) + ((k, v) if halo else ())))


def _attn_bwd(q, k, v, do, st, nb, name):
    n = min(4, nb)
    CH = n * BLK
    NBLK = T // BLK
    halo = nb > n

    def body(*refs):
        if halo:
            (q_ref, k_ref, v_ref, do_ref, st_ref, kp_ref, vp_ref, qn_ref, don_ref, stn_ref,
             dq_ref, dk_ref, dv_ref) = refs
        else:
            q_ref, k_ref, v_ref, do_ref, st_ref, dq_ref, dk_ref, dv_ref = refs
        i = pl.program_id(0)
        lane = lax.broadcasted_iota(jnp.int32, (1, 128), 1)
        in_a = [lane < HEAD, lane >= HEAD]
        band, kj, _ = _band_mask(0)
        thr0 = jnp.where((n * i) % nb == 0, BLK, 0) if halo else BLK
        mask0 = band & (kj >= thr0)

        def stat_rows(st_t, hp):
            lse_r = jnp.concatenate([st_t[2 * hp:2 * hp + 1, :], st_t[2 * hp + 1:2 * hp + 2, :]], axis=1)
            dl_r = jnp.concatenate([st_t[8 + 2 * hp:9 + 2 * hp, :], st_t[9 + 2 * hp:10 + 2 * hp, :]], axis=1)
            return lse_r, dl_r

        st_t = [st_ref[b * BLK:(b + 1) * BLK, :].T for b in range(n)]
        if halo:
            nxt_thr = jnp.where((n * i + n) % nb == 0, 2 * BLK, 0)
            _, kj1, qi1 = _band_mask(0, BLK)
            mask_next = kj1 >= qi1 + nxt_thr
            stn_t = stn_ref[...].T

        for hp in range(4):
            cs = slice(hp * 128, (hp + 1) * 128)
            kb = [k_ref[b * BLK:(b + 1) * BLK, cs] for b in range(n)]
            vb = [v_ref[b * BLK:(b + 1) * BLK, cs] for b in range(n)]
            dk_acc = [jnp.zeros((BLK, 128), F32) for _ in range(n)]
            dv_acc = [jnp.zeros((BLK, 128), F32) for _ in range(n)]
            for b in range(n):
                rs = slice(b * BLK, (b + 1) * BLK)
                q2s = _stack_heads(q_ref[rs, cs], in_a)
                do2s = _stack_heads(do_ref[rs, cs], in_a)
                if b == 0:
                    kprev = kp_ref[:, cs] if halo else kb[0]
                    vprev = vp_ref[:, cs] if halo else vb[0]
                    mask = mask0
                else:
                    kprev, vprev, mask = kb[b - 1], vb[b - 1], band
                kk = jnp.concatenate([kprev, kb[b]], axis=0)
                vv = jnp.concatenate([vprev, vb[b]], axis=0)
                lse_r, dl_r = stat_rows(st_t[b], hp)
                s_t = jnp.where(mask, _mm_nt(kk, q2s) * SCALE, NEG)
                p_t = jnp.exp(s_t - lse_r)
                ds_t = (p_t * (_mm_nt(vv, do2s) - dl_r)).astype(BF16)
                dkk = _mm(ds_t, q2s) * SCALE
                dvv = _mm(p_t.astype(BF16), do2s)
                dqs = _mm_tn(ds_t, kk) * SCALE
                dq_ref[rs, cs] = jnp.where(in_a[0], dqs[:BLK], dqs[BLK:]).astype(BF16)
                dk_acc[b] += dkk[BLK:]
                dv_acc[b] += dvv[BLK:]
                if b > 0:
                    dk_acc[b - 1] += dkk[:BLK]
                    dv_acc[b - 1] += dvv[:BLK]
            if halo:
                q2s = _stack_heads(qn_ref[:, cs], in_a)
                do2s = _stack_heads(don_ref[:, cs], in_a)
                lse_r, dl_r = stat_rows(stn_t, hp)
                s_t = jnp.where(mask_next, _mm_nt(kb[n - 1], q2s) * SCALE, NEG)
                p_t = jnp.exp(s_t - lse_r)
                ds_t = (p_t * (_mm_nt(vb[n - 1], do2s) - dl_r)).astype(BF16)
                dk_acc[n - 1] += _mm(ds_t, q2s) * SCALE
                dv_acc[n - 1] += _mm(p_t.astype(BF16), do2s)
            for b in range(n):
                dk_ref[b * BLK:(b + 1) * BLK, cs] = dk_acc[b].astype(BF16)
                dv_ref[b * BLK:(b + 1) * BLK, cs] = dv_acc[b].astype(BF16)

    cur = pl.BlockSpec((CH, AW), lambda i: (i, 0))
    cur_st = pl.BlockSpec((CH, 128), lambda i: (i, 0))
    prev = pl.BlockSpec((BLK, AW), lambda i: (jnp.maximum(n * i - 1, 0), 0))
    nxt = pl.BlockSpec((BLK, AW), lambda i: (jnp.minimum(n * i + n, NBLK - 1), 0))
    nxt_st = pl.BlockSpec((BLK, 128), lambda i: (jnp.minimum(n * i + n, NBLK - 1), 0))
    ins = [cur] * 4 + [cur_st] + ([prev, prev, nxt, nxt, nxt_st] if halo else [])
    args = (q, k, v, do, st) + ((k, v, q, do, st) if halo else ())
    return pl.pallas_call(
        body, name=name, grid=(T // CH,),
        in_specs=ins,
        out_specs=[cur] * 3,
        out_shape=[jax.ShapeDtypeStruct((T, AW), BF16)] * 3,
        compiler_params=_cp(("parallel",)),
    )(*args)


TH = 256
NCH = TH // CHUNK


def _hgrn_common(hq_ref, hf_ref, lbr_ref, tri_ref):
    r0 = lbr_ref[0:1, :]
    r1 = lbr_ref[1:2, :]
    mx = jnp.maximum(r0, r1)
    e0 = jnp.exp(r0 - mx)
    e1 = jnp.exp(r1 - mx)
    lb = e0 / (e0 + e1)
    hqv = hq_ref[...]
    sq = _sigmoid(hqv)
    qv = hqv * sq
    sf = _sigmoid(hf_ref[...])
    f = lb + (1.0 - lb) * sf
    kv = 1.0 - f
    g = jnp.log(f)
    cum = _mm_exact_l(tri_ref[...], g)
    lastb = jnp.concatenate(
        [jnp.broadcast_to(cum[c * CHUNK + CHUNK - 1:(c + 1) * CHUNK, :], (CHUNK, HW)) for c in range(NCH)], axis=0)
    ea = jnp.exp(cum)
    ena = jnp.exp(-cum)
    eend = jnp.exp(lastb - cum)
    return dict(lb=lb, hq=hqv, sq=sq, q=qv, sf=sf, f=f, k=kv, cum=cum, lastb=lastb, ea=ea, ena=ena, eend=eend,
                qd=qv * ea, ki=kv * ena, ke=kv * eend, dec=jnp.exp(lastb))


def _tri_mask(transposed=False):
    ti = lax.broadcasted_iota(jnp.int32, (TH, TH), 1 if transposed else 0)
    si = lax.broadcasted_iota(jnp.int32, (TH, TH), 0 if transposed else 1)
    return (si <= ti) & ((si // CHUNK) == (ti // CHUNK))


def _hgrn_fwd(hq, hf, hi, lbr, tri):
    def body(hq_ref, hf_ref, hi_ref, lbr_ref, tri_ref, rec_ref, sall_ref, st_scr):
        @pl.when(pl.program_id(0) == 0)
        def _():
            st_scr[...] = jnp.zeros_like(st_scr)

        w = _hgrn_common(hq_ref, hf_ref, lbr_ref, tri_ref)
        qd, ki, ke = w["qd"].astype(BF16), w["ki"].astype(BF16), w["ke"].astype(BF16)
        dec = w["dec"]
        vb = hi_ref[...]
        causal = _tri_mask()
        for h in range(4):
            cs = slice(h * 128, (h + 1) * 128)
            att = jnp.where(causal, _mm_nt(qd[:, cs], ki[:, cs]), 0.0)
            o_intra = _mm(att.astype(BF16), vb[:, cs])
            for c in range(NCH):
                rs = slice(c * CHUNK, (c + 1) * CHUNK)
                st = st_scr[:, cs]
                sall_ref[c, :, cs] = st
                rec_ref[rs, cs] = o_intra[rs] + _mm_nt(qd[rs, cs], st.astype(BF16))
                st_scr[:, cs] = dec[c * CHUNK:c * CHUNK + 1, cs] * st + _mm_tn(vb[rs, cs], ke[rs, cs])

    tok = pl.BlockSpec((TH, HW), lambda i: (i, 0))
    return pl.pallas_call(
        body, name="hgrn_fwd", grid=(T // TH,),
        in_specs=[tok, tok, tok, pl.BlockSpec((2, HW), lambda i: (0, 0)), pl.BlockSpec((TH, TH), lambda i: (0, 0))],
        out_specs=[tok, pl.BlockSpec((NCH, 128, HW), lambda i: (i, 0, 0))],
        out_shape=[jax.ShapeDtypeStruct((T, HW), F32), jax.ShapeDtypeStruct((T // CHUNK, 128, HW), F32)],
        scratch_shapes=[pltpu.VMEM((128, HW), F32)],
        compiler_params=_cp(("arbitrary",)),
    )(hq, hf, hi, lbr, tri)


def _hgrn_bwd(hq, hf, hi, lbr, tri, trit, drec, sall, rout, routb):
    NT = T // TH

    def body(hq_ref, hf_ref, hi_ref, lbr_ref, tri_ref, trit_ref, do_ref, sall_ref, rout_r, routb_r,
             dhq_ref, dhf_ref, dhi_ref, small_ref, pout_o, poutr_o,
             dst_scr, dlb_scr, dqd_scr, dki_scr, dke_scr, dlast_scr, send_sems, recv_sems, loc_sems):
        step = pl.program_id(0)
        loc, rem = _chip_copies(_w_out_piece, rout_r, routb_r, pout_o, poutr_o, send_sems, recv_sems,
                                loc_sems.at[0])

        @pl.when(step == 0)
        def _():
            dst_scr[...] = jnp.zeros_like(dst_scr)
            dlb_scr[...] = jnp.zeros_like(dlb_scr)
            for cp in loc + rem:
                cp.start()

        w = _hgrn_common(hq_ref, hf_ref, lbr_ref, tri_ref)
        qd, ki, ke = w["qd"].astype(BF16), w["ki"].astype(BF16), w["ke"].astype(BF16)
        dec = w["dec"]
        vb = hi_ref[...]
        dob = do_ref[...].astype(BF16)
        causal = _tri_mask()
        causal_t = _tri_mask(transposed=True)
        for h in range(4):
            cs = slice(h * 128, (h + 1) * 128)
            att_t = jnp.where(causal_t, _mm_nt(ki[:, cs], qd[:, cs]), 0.0).astype(BF16)
            datt_t = jnp.where(causal_t, _mm_nt(vb[:, cs], dob[:, cs]), 0.0).astype(BF16)
            datt = jnp.where(causal, _mm_nt(dob[:, cs], vb[:, cs]), 0.0).astype(BF16)
            dv_intra = _mm(att_t, dob[:, cs])
            dqd_intra = _mm(datt, ki[:, cs])
            dki_scr[:, cs] = _mm(datt_t, qd[:, cs])
            for c in reversed(range(NCH)):
                rs = slice(c * CHUNK, (c + 1) * CHUNK)
                dec_c = dec[c * CHUNK:c * CHUNK + 1, :]
                st = sall_ref[c, :, cs]
                dst = dst_scr[:, cs]
                dstb = dst.astype(BF16)
                dhi_ref[rs, cs] = (dv_intra[rs] + _mm_nt(ke[rs, cs], dstb)).astype(BF16)
                dqd_scr[rs, cs] = dqd_intra[rs] + _mm(dob[rs, cs], st.astype(BF16))
                dke_scr[rs, cs] = _mm(vb[rs, cs], dstb)
                ddec = jnp.sum(dst * st, axis=0, keepdims=True)
                dlast_scr[c:c + 1, cs] = ddec * dec_c[:, cs]
                dst_scr[:, cs] = dec_c[:, cs] * dst + _mm_tn(dob[rs, cs], qd[rs, cs])
        dqd, dki, dke = dqd_scr[...], dki_scr[...], dke_scr[...]
        dq = dqd * w["ea"]
        dk = dki * w["ena"] + dke * w["eend"]
        dcum = dqd * w["qd"] - dki * w["ki"] - dke * w["ke"]
        dkeke = dke * w["ke"]
        dlastb = jnp.concatenate(
            [jnp.broadcast_to(dlast_scr[c:c + 1, :] + jnp.sum(dkeke[c * CHUNK:(c + 1) * CHUNK], axis=0, keepdims=True),
                              (CHUNK, HW)) for c in range(NCH)], axis=0)
        dg = _mm_exact_l(trit_ref[...], dcum) + dlastb
        df = dg / w["f"] - dk
        lb, sf, sq = w["lb"], w["sf"], w["sq"]
        dhf_ref[...] = (df * (1.0 - lb) * sf * (1.0 - sf)).astype(BF16)
        dhq_ref[...] = (dq * (sq * (1.0 + w["hq"] * (1.0 - sq)))).astype(BF16)
        dlb_scr[...] += jnp.sum(df * (1.0 - sf), axis=0, keepdims=True)

        @pl.when(step == NT - 1)
        def _():
            gr = dlb_scr[...] * lb * (1.0 - lb)
            small_ref[...] = jnp.zeros_like(small_ref)
            small_ref[0:1, 0:HW] = gr
            small_ref[1:2, 0:HW] = -gr
            for cp in rem:
                cp.wait_recv()
            for cp in rem:
                cp.wait_send()
            for cp in loc:
                cp.wait()

    tok = pl.BlockSpec((TH, HW), lambda i: (NT - 1 - i, 0))
    const = lambda shape: pl.BlockSpec(shape, lambda i: (0,) * len(shape))
    hbm = pl.BlockSpec(memory_space=pltpu.HBM)
    return pl.pallas_call(
        body, name="hgrn_bwd", grid=(NT,),
        in_specs=[tok, tok, tok, const((2, HW)), const((TH, TH)), const((TH, TH)), tok,
                  pl.BlockSpec((NCH, 128, HW), lambda i: (NT - 1 - i, 0, 0)), hbm, hbm],
        out_specs=[tok, tok, tok, const((8, D)), hbm, hbm],
        out_shape=[jax.ShapeDtypeStruct((T, HW), BF16)] * 3 + [jax.ShapeDtypeStruct((8, D), F32),
                                                              jax.ShapeDtypeStruct((128, D), F32),
                                                              jax.ShapeDtypeStruct((3, 128, D), BF16)],
        scratch_shapes=[pltpu.VMEM((128, HW), F32), pltpu.VMEM((1, HW), F32), pltpu.VMEM((TH, HW), F32),
                        pltpu.VMEM((TH, HW), F32), pltpu.VMEM((TH, HW), F32), pltpu.VMEM((8, HW), F32),
                        pltpu.SemaphoreType.DMA((3,)), pltpu.SemaphoreType.DMA((3,)), pltpu.SemaphoreType.DMA((1,))],
        compiler_params=_cp(("arbitrary",)),
    )(hq, hf, hi, lbr, tri, trit, drec, sall, rout, routb)


def _fwd_out(o1, o4, o16, l1, l4, l16, rec, ag, hg, x, tgt, anw, hnw, fnw, wout_full, gmat, emat, selmat):
    TT = 256

    def body(o1_r, o4_r, o16_r, l1_r, l4_r, l16_r, rec_r, ag_r, hg_r, x_r, tgt_r, anw_r, hnw_r, fnw_r, wo_r, g_r,
             e_r, sel_r, dx2_o, do1_o, do4_o, do16_o, st1_o, st4_o, st16_o, drec_o, dag_o, dhg_o,
             rout_o, routb_o, small_o, scr_a, scr_b, gwout_o, rbuf, send_sems, recv_sems):
        @pl.when(pl.program_id(0) == 0)
        def _():
            gwout_o[...] = jnp.zeros_like(gwout_o)
            small_o[...] = jnp.zeros_like(small_o)

        def unperm(r4, r16):
            return _unperm_load(r4, r16, scr_a, scr_b)

        def perm_out(val, p1, p4, p16, dt):
            _perm_store(val, scr_a, p1, p4, p16, dt)

        o4u, o16u = unperm(o4_r, o16_r)
        l4c, l16c = unperm(l4_r, l16_r)
        em = e_r[...]
        l1v, l4u, l16u = _mm_exact_r(l1_r[...], em), _mm_exact_r(l4c, em), _mm_exact_r(l16c, em)
        o1v = o1_r[...]
        mx = jnp.maximum(jnp.maximum(l1v, l4u), l16u)
        w1, w4, w16 = jnp.exp(l1v - mx), jnp.exp(l4u - mx), jnp.exp(l16u - mx)
        den = w1 + w4 + w16
        attn = (w1 * o1v + w4 * o4u + w16 * o16u) / den
        lse = mx + jnp.log(den)
        gm = g_r[...]

        def head_mean_a(t):
            return _mm_exact_r(t, gm)

        def head_mean_h(t):
            return jnp.concatenate(
                [jnp.broadcast_to(jnp.mean(t[:, h * 128:(h + 1) * 128], axis=-1, keepdims=True), (TT, 128))
                 for h in range(4)], axis=1)

        rs_a = lax.rsqrt(head_mean_a(attn * attn) + EPS)
        n_a = attn * rs_a
        agv = ag_r[...]
        sg_a = _sigmoid(agv)
        si_a = agv * sg_a
        anw_v = anw_r[...]
        y_a = (n_a * anw_v) * si_a
        recv = rec_r[...]
        rs_h = lax.rsqrt(head_mean_h(recv * recv) + EPS)
        n_h = recv * rs_h
        hgv = hg_r[...]
        sg_h = _sigmoid(hgv)
        si_h = hgv * sg_h
        hnw_v = hnw_r[...]
        y_h = (n_h * hnw_v) * si_h
        mixed = jnp.concatenate([y_a, y_h], axis=1).astype(BF16)
        xv = x_r[...]
        x2 = xv + _mm(mixed, wo_r[...])
        r2 = lax.rsqrt(jnp.mean(x2 * x2, axis=-1, keepdims=True) + EPS)
        fnw_v = fnw_r[...]
        xn = x2 * r2
        err = xn * fnw_v - tgt_r[...]
        small_o[2:3, :] += 0.5 * jnp.sum(jnp.mean(err * err, axis=-1, keepdims=True), axis=0, keepdims=True)
        dy = err * (1.0 / D)
        small_o[0:1, :] += jnp.sum(dy * xn, axis=0, keepdims=True)
        dyw = dy * fnw_v
        dx2 = r2 * dyw - x2 * ((r2 * r2 * r2) * jnp.mean(dyw * x2, axis=-1, keepdims=True))
        dx2_o[...] = dx2
        dx2b = dx2.astype(BF16)
        gwout_o[...] += _mm_tn(mixed, dx2b)
        dmix = _mm_nt(dx2b, wo_r[...])
        dm_a, dm_h = dmix[:, :AW], dmix[:, AW:]
        dag_o[...] = (dm_a * (n_a * anw_v) * (sg_a * (1.0 + agv * (1.0 - sg_a)))).astype(BF16)
        dn_a = dm_a * anw_v * si_a
        small_o[1:2, 0:AW] += jnp.sum(dm_a * n_a * si_a, axis=0, keepdims=True)
        dattn = rs_a * (dn_a - n_a * head_mean_a(dn_a * n_a))
        delta = head_mean_a(dattn * attn) * float(HEAD)
        perm_out(dattn, do1_o, do4_o, do16_o, BF16)
        stats = _mm_exact_r(lse, sel_r[0]) + _mm_exact_r(delta, sel_r[1])
        perm_out(stats, st1_o, st4_o, st16_o, F32)
        dhg_o[...] = (dm_h * (n_h * hnw_v) * (sg_h * (1.0 + hgv * (1.0 - sg_h)))).astype(BF16)
        dn_h = dm_h * hnw_v * si_h
        small_o[1:2, AW:] += jnp.sum(dm_h * n_h * si_h, axis=0, keepdims=True)
        drec_o[...] = rs_h * (dn_h - n_h * head_mean_h(dn_h * n_h))

        @pl.when(pl.program_id(0) == T // TT - 1)
        def _():
            x, y, c = lax.axis_index("x"), lax.axis_index("y"), lax.axis_index("c")
            cps = [pltpu.make_async_remote_copy(
                src_ref=gwout_o.at[pl.ds(pl.multiple_of(j * 256 + (1 - c) * 128, 128), 128), :], dst_ref=rbuf.at[j],
                send_sem=send_sems.at[j], recv_sem=recv_sems.at[j], device_id=(x, y, 1 - c), device_id_type=MESH)
                for j in range(4)]
            for cp in cps:
                cp.start()
            for j, cp in enumerate(cps):
                cp.wait_recv()
                red = gwout_o[pl.ds(pl.multiple_of(j * 256 + c * 128, 128), 128), :] + rbuf[j]
                rout_o[j * 128:(j + 1) * 128, :] = red
                routb_o[j * 128:(j + 1) * 128, :] = red.astype(BF16)
            for cp in cps:
                cp.wait_send()

    tok = lambda w: pl.BlockSpec((TT, w), lambda i: (i, 0))
    d4 = pl.BlockSpec((4, TT // 4, AW), lambda i: (0, i, 0))
    d16 = pl.BlockSpec((16, TT // 16, AW), lambda i: (0, i, 0))
    const = lambda shape: pl.BlockSpec(shape, lambda i: (0,) * len(shape))
    sd = lambda shape, dt: jax.ShapeDtypeStruct(shape, dt)
    c4 = pl.BlockSpec((4, TT // 4, 128), lambda i: (0, i, 0))
    c16 = pl.BlockSpec((16, TT // 16, 128), lambda i: (0, i, 0))
    p3 = lambda w, dt: [sd((T, w), dt), sd((4, T // 4, w), dt), sd((16, T // 16, w), dt)]
    return pl.pallas_call(
        body, name="fwd_out", grid=(T // TT,),
        in_specs=[tok(AW), d4, d16, tok(128), c4, c16, tok(AW), tok(AW), tok(AW), tok(D), tok(D),
                  const((1, AW)), const((1, HW)), const((1, D)), const((D, D)), const((AW, AW)),
                  const((128, AW)), const((2, AW, 128))],
        out_specs=[tok(D)] + [tok(AW), d4, d16] + [tok(128), c4, c16] + [tok(AW)] * 3
        + [const((512, D)), const((512, D)), const((8, D))],
        out_shape=[sd((T, D), F32)] + p3(AW, BF16) + p3(128, F32)
        + [sd((T, AW), F32), sd((T, AW), BF16), sd((T, AW), BF16), sd((512, D), F32), sd((512, D), BF16),
           sd((8, D), F32)],
        scratch_shapes=[pltpu.VMEM((4, TT, 128), F32), pltpu.VMEM((4, TT, 128), F32), pltpu.VMEM((D, D), F32),
                        pltpu.VMEM((4, 128, D), F32), pltpu.SemaphoreType.DMA((4,)), pltpu.SemaphoreType.DMA((4,))],
        compiler_params=_cp(("arbitrary",)),
    )(o1, o4, o16, l1, l4, l16, rec, ag, hg, x, tgt, anw, hnw, fnw, wout_full, gmat, emat, selmat)


def _dproj_build(dq, dk, dv, dag, dhq, dhf, dhi, dhg, pos):
    TT = 256

    def body(dq1, dq4, dq16, dk1, dk4, dk16, dv1, dv4, dv16, dag_r, dhq_r, dhf_r, dhi_r, dhg_r,
             pos_r, dproj_o, scr_a, scr_b):
        def unperm_sum(r1, r4, r16):
            u4, u16 = _unperm_load(r4, r16, scr_a, scr_b)
            return r1[...] + u4 + u16

        cosf, s1, s2 = _rope_tables(pos_r[...])
        dproj_o[:, 0:512] = _rope_bwd(unperm_sum(dq1, dq4, dq16), cosf, s1, s2).astype(BF16)
        dproj_o[:, 512:1024] = _rope_bwd(unperm_sum(dk1, dk4, dk16), cosf, s1, s2).astype(BF16)
        dproj_o[:, 1024:1536] = unperm_sum(dv1, dv4, dv16).astype(BF16)
        dproj_o[:, 1536:2048] = dag_r[...]
        dproj_o[:, 2048:2560] = dhq_r[...]
        dproj_o[:, 2560:3072] = dhf_r[...]
        dproj_o[:, 3072:3584] = dhi_r[...]
        dproj_o[:, 3584:4096] = dhg_r[...]

    tok = lambda w: pl.BlockSpec((TT, w), lambda i: (i, 0))
    d4 = pl.BlockSpec((4, TT // 4, AW), lambda i: (0, i, 0))
    d16 = pl.BlockSpec((16, TT // 16, AW), lambda i: (0, i, 0))
    return pl.pallas_call(
        body, name="dproj_build", grid=(T // TT,),
        in_specs=[tok(AW), d4, d16] * 3 + [tok(AW)] * 5 + [tok(1)],
        out_specs=tok(NCOL),
        out_shape=jax.ShapeDtypeStruct((T, NCOL), BF16),
        scratch_shapes=[pltpu.VMEM((4, TT, 128), F32), pltpu.VMEM((4, TT, 128), F32)],
        compiler_params=_cp(("parallel",)),
    )(*dq, *dk, *dv, dag, dhq, dhf, dhi, dhg, pos)


def _bwd_x(dproj, x, dx2, mixw, w_full, rin, rinb, small4, small6):
    TT = 256
    NT = T // TT

    def body(dp_r, x_r, dx2_r, mw_r, w_r, rin_r, rinb_r, s4_r, s6_r,
             gx_o, pin_o, pinr_o, sall_o, sbuf, send_sems, recv_sems, loc_sems):
        i = pl.program_id(0)
        loc, rem = _chip_copies(_w_in_piece, rin_r, rinb_r, pin_o, pinr_o, send_sems, recv_sems, loc_sems.at[0])

        @pl.when(i == 0)
        def _():
            sbuf[...] = jnp.zeros_like(sbuf)
            for cp in loc + rem:
                cp.start()

        dhn = _mm_nt(dp_r[...], w_r[...])
        xv = x_r[...]
        r = lax.rsqrt(jnp.mean(xv * xv, axis=-1, keepdims=True) + EPS)
        dxw = dhn * mw_r[...]
        gx_o[...] = dx2_r[...] + r * dxw - xv * ((r * r * r) * jnp.mean(dxw * xv, axis=-1, keepdims=True))
        sbuf[16:17, :] += jnp.sum(dhn * (xv * r), axis=0, keepdims=True)

        @pl.when(i == NT - 1)
        def _():
            sbuf[0:8, :] = s4_r[...]
            sbuf[8:16, :] = s6_r[...]
            sloc, srem = _small_copies(sbuf, sall_o, send_sems, recv_sems, loc_sems.at[1])
            for cp in sloc + srem:
                cp.start()
            for cp in rem + srem:
                cp.wait_recv()
            for cp in rem + srem:
                cp.wait_send()
            for cp in loc + sloc:
                cp.wait()

    tok = lambda w: pl.BlockSpec((TT, w), lambda i: (i, 0))
    const = lambda shape: pl.BlockSpec(shape, lambda i: (0,) * len(shape))
    hbm = pl.BlockSpec(memory_space=pltpu.HBM)
    return pl.pallas_call(
        body, name="bwd_x", grid=(NT,),
        in_specs=[tok(NCOL), tok(D), tok(D), const((1, D)), const((D, NCOL)), hbm, hbm, const((8, D)), const((8, D))],
        out_specs=[tok(D), hbm, hbm, hbm],
        out_shape=[jax.ShapeDtypeStruct((T, D), F32),
                   jax.ShapeDtypeStruct((512, 1024), F32), jax.ShapeDtypeStruct((3, 512, 1024), BF16),
                   jax.ShapeDtypeStruct((8, 24, D), F32)],
        scratch_shapes=[pltpu.VMEM((24, D), F32), pltpu.SemaphoreType.DMA((10,)), pltpu.SemaphoreType.DMA((10,)),
                        pltpu.SemaphoreType.DMA((2,))],
        compiler_params=_cp(("arbitrary",)),
    )(dproj, x, dx2, mixw, w_full, rin, rinb, small4, small6)


def _grad_w_in(hn, dproj):
    TK = 1024
    NK = T // TK

    def body(hnt_r, dp_r, rin_o, rinb_o, acc, rbuf, obuf, obufb, send_sems, recv_sems, wb_sems):
        j = pl.program_id(0)
        kk = pl.program_id(1)
        x, y, c = lax.axis_index("x"), lax.axis_index("y"), lax.axis_index("c")
        mine = pl.ds(pl.multiple_of(c * 512, 512), 512)
        theirs = pl.ds(pl.multiple_of((1 - c) * 512, 512), 512)

        def send(jj):
            return pltpu.make_async_remote_copy(
                src_ref=acc.at[jj % 2, theirs, :], dst_ref=rbuf.at[jj], send_sem=send_sems.at[jj],
                recv_sem=recv_sems.at[jj], device_id=(x, y, 1 - c), device_id_type=MESH)

        def writeback(jj):
            cols = pl.ds(jj * 1024, 1024)
            return [pltpu.make_async_copy(obuf.at[jj % 2], rin_o.at[:, cols], wb_sems.at[jj % 2]),
                    pltpu.make_async_copy(obufb.at[jj % 2], rinb_o.at[:, cols], wb_sems.at[2 + jj % 2])]

        def wait_writeback(jj):
            for cp in writeback(jj):
                cp.wait()

        def finalize(jj):
            send(jj).wait_recv()
            red = acc[jj % 2, mine, :] + rbuf[jj]
            obuf[jj % 2] = red
            obufb[jj % 2] = red.astype(BF16)
            for cp in writeback(jj):
                cp.start()

        prod = _mm(hnt_r[...], dp_r[...])

        @pl.when(kk == 0)
        def _():
            for jj in (2, 3):
                @pl.when(j == jj)
                def _():
                    send(jj - 2).wait_send()
            acc[j % 2] = prod

        @pl.when(kk > 0)
        def _():
            acc[j % 2] += prod

        @pl.when(kk == NK - 1)
        def _():
            for jj in range(4):
                @pl.when(j == jj)
                def _():
                    send(jj).start()
                    if jj in (1, 2):
                        finalize(jj - 1)
                    if jj == 3:
                        wait_writeback(0)
                        finalize(2)
                        wait_writeback(1)
                        finalize(3)
                        wait_writeback(2)
                        wait_writeback(3)
                        send(2).wait_send()
                        send(3).wait_send()

    hbm = pl.BlockSpec(memory_space=pltpu.HBM)
    return pl.pallas_call(
        body, name="grad_w_in", grid=(4, NK),
        in_specs=[pl.BlockSpec((D, TK), lambda j, kk: (0, kk)), pl.BlockSpec((TK, 1024), lambda j, kk: (kk, j))],
        out_specs=[hbm, hbm],
        out_shape=[jax.ShapeDtypeStruct((512, NCOL), F32), jax.ShapeDtypeStruct((512, NCOL), BF16)],
        scratch_shapes=[pltpu.VMEM((2, D, 1024), F32), pltpu.VMEM((4, 512, 1024), F32), pltpu.VMEM((2, 512, 1024), F32),
                        pltpu.VMEM((2, 512, 1024), BF16),
                        pltpu.SemaphoreType.DMA((4,)), pltpu.SemaphoreType.DMA((4,)), pltpu.SemaphoreType.DMA((4,))],
        compiler_params=_cp(("arbitrary", "arbitrary")),
    )(hn, dproj)


def _chip_sum(own, rem, name):
    rows, cols = own.shape
    tr = min(rows, 256)

    def body(own_r, rem_r, o_r):
        acc = own_r[...]
        for s in range(3):
            acc = acc + rem_r[s].astype(F32)
        o_r[...] = acc

    return pl.pallas_call(
        body, name=name, grid=(rows // tr,),
        in_specs=[pl.BlockSpec((tr, cols), lambda i: (i, 0)), pl.BlockSpec((3, tr, cols), lambda i: (0, i, 0))],
        out_specs=pl.BlockSpec((tr, cols), lambda i: (i, 0)),
        out_shape=jax.ShapeDtypeStruct((rows, cols), F32),
        compiler_params=_cp(("parallel",)),
    )(own, rem)


def _w_in_piece(ref, j):
    return ref.at[:, pl.ds(j * 1024, 1024)]


def _w_out_piece(ref, j):
    return ref.at[pl.ds(j * 128, 128), :]


def _chip_copies(piece, src_r, srcb_r, own_o, rem_o, send_sems, recv_sems, loc_sem):
    x, y, c = lax.axis_index("x"), lax.axis_index("y"), lax.axis_index("c")
    chips = [(1 - x, y), (x, 1 - y), (1 - x, 1 - y)]
    loc = [pltpu.make_async_copy(piece(src_r, 2 * x + y), own_o, loc_sem)]
    rem = [pltpu.make_async_remote_copy(
        src_ref=piece(srcb_r, 2 * px + py), dst_ref=rem_o.at[k], send_sem=send_sems.at[k],
        recv_sem=recv_sems.at[k], device_id=(px, py, c), device_id_type=MESH) for k, (px, py) in enumerate(chips)]
    return loc, rem


def _small_copies(small_r, sall_o, send_sems, recv_sems, loc_sem):
    x, y, c = lax.axis_index("x"), lax.axis_index("y"), lax.axis_index("c")
    me = 4 * x + 2 * y + c
    loc = [pltpu.make_async_copy(small_r, sall_o.at[me], loc_sem)]
    rem = []
    k = 3
    for fx in range(2):
        for fy in range(2):
            for fc in range(2):
                if fx or fy or fc:
                    peer = (1 - x if fx else x, 1 - y if fy else y, 1 - c if fc else c)
                    rem.append(pltpu.make_async_remote_copy(
                        src_ref=small_r, dst_ref=sall_o.at[me], send_sem=send_sems.at[k],
                        recv_sem=recv_sems.at[k], device_id=peer, device_id_type=MESH))
                    k += 1
    return loc, rem


def _pair_share(pin, pout):
    def body(pin_r, pout_r, fin_o, fout_o, send_sems, recv_sems):
        x, y, c = lax.axis_index("x"), lax.axis_index("y"), lax.axis_index("c")
        sibling = (x, y, 1 - c)
        rem = [pltpu.make_async_remote_copy(src_ref=pin_r, dst_ref=fin_o.at[c], send_sem=send_sems.at[0],
                                            recv_sem=recv_sems.at[0], device_id=sibling, device_id_type=MESH),
               pltpu.make_async_remote_copy(src_ref=pout_r, dst_ref=fout_o.at[c], send_sem=send_sems.at[1],
                                            recv_sem=recv_sems.at[1], device_id=sibling, device_id_type=MESH)]
        for cp in rem:
            cp.start()
        fin_o[c] = pin_r[...]
        fout_o[c] = pout_r[...]
        for cp in rem:
            cp.wait_recv()
        for cp in rem:
            cp.wait_send()

    vm = pl.BlockSpec(memory_space=pltpu.VMEM)
    return pl.pallas_call(
        body, name="pair_share",
        out_shape=(jax.ShapeDtypeStruct((2, 512, 1024), F32), jax.ShapeDtypeStruct((2, 128, D), F32)),
        in_specs=[vm, vm], out_specs=(vm, vm),
        scratch_shapes=[pltpu.SemaphoreType.DMA((2,)), pltpu.SemaphoreType.DMA((2,))],
        compiler_params=_cp(),
    )(pin, pout)


def _adamw_math(w, g, m, v):
    m = B1 * m + (1.0 - B1) * g
    v = B2 * v + (1.0 - B2) * (g * g)
    m_hat = m / (1.0 - B1 ** STEP)
    v_hat = v / (1.0 - B2 ** STEP)
    delta = -LR * (m_hat / (jnp.sqrt(v_hat) + AEPS) + WD * w)
    return delta, m, v


def _adamw(w, g, m, v, name):
    rows, cols = w.shape
    tr = min(rows, 256)

    def body(w_r, g_r, m_r, v_r, d_o, m_o, v_o):
        d, mm, vv = _adamw_math(w_r[...], g_r[...], m_r[...], v_r[...])
        d_o[...] = d
        m_o[...] = mm
        v_o[...] = vv

    blk = pl.BlockSpec((tr, cols), lambda i: (i, 0))
    return pl.pallas_call(
        body, name=name, grid=(rows // tr,),
        in_specs=[blk] * 4, out_specs=[blk] * 3,
        out_shape=[jax.ShapeDtypeStruct((rows, cols), F32)] * 3,
        compiler_params=_cp(("parallel",)),
    )(w, g, m, v)


def _adamw_small(sall, params):
    def body(sall_r, *refs):
        ins, outs = refs[:15], refs[15:]
        tot = sall_r[0]
        for dv in range(1, 8):
            tot = tot + sall_r[dv]
        grads = [tot[16:17, :], tot[1:2, 0:AW], tot[1:2, AW:], tot[8:10, 0:HW], tot[0:1, :]]
        outs[0][...] = tot[2:3, 0:1]
        for p in range(5):
            w_r, m_r, v_r = ins[3 * p:3 * p + 3]
            g = grads[p]
            d, mm, vv = _adamw_math(w_r[...], g, m_r[...], v_r[...])
            outs[1 + 4 * p][...] = g
            outs[2 + 4 * p][...] = d
            outs[3 + 4 * p][...] = mm
            outs[4 + 4 * p][...] = vv

    flat = [a for p in params for a in p]
    shapes = [jax.ShapeDtypeStruct((1, 1), F32)]
    for p in params:
        shapes += [jax.ShapeDtypeStruct(p[0].shape, F32)] * 4
    vm = pl.BlockSpec(memory_space=pltpu.VMEM)
    return pl.pallas_call(
        body, name="adamw_small",
        in_specs=[vm] * 16, out_specs=[vm] * 21, out_shape=shapes,
        compiler_params=_cp(),
    )(sall, *flat)


def kernel(x, positions, w_in, w_out, mix_norm_w, attn_out_norm_w, hgrn_out_norm_w, hgrn_lb_raw, final_norm_w, loss_target, m_w_in, m_w_out, m_mix_norm_w, m_attn_out_norm_w, m_hgrn_out_norm_w, m_hgrn_lb_raw, m_final_norm_w, v_w_in, v_w_out, v_mix_norm_w, v_attn_out_norm_w, v_hgrn_out_norm_w, v_hgrn_lb_raw, v_final_norm_w):
    xs = x.reshape(T, D)
    tgt = loss_target.reshape(T, D)
    pos = positions.reshape(T, 1)
    fnw = final_norm_w.reshape(1, D)

    ti = np.arange(TH)
    tri_np = ((ti[:, None] // CHUNK == ti[None, :] // CHUNK) & (ti[None, :] <= ti[:, None])).astype(np.float32)
    tri = jnp.asarray(tri_np, BF16)
    trit = jnp.asarray(tri_np.T, BF16)
    hi_ = np.arange(AW) // HEAD
    gmat = jnp.asarray((hi_[:, None] == hi_[None, :]).astype(np.float32) / HEAD, BF16)
    emat_np = (np.arange(128)[:, None] == hi_[None, :]).astype(np.float32)
    sel_np = np.zeros((2, AW, 128), np.float32)
    sel_np[0, np.arange(8) * HEAD, np.arange(8)] = 1.0
    sel_np[1, np.arange(8) * HEAD, 8 + np.arange(8)] = 1.0
    emat = jnp.asarray(emat_np, BF16)
    selmat = jnp.asarray(sel_np, BF16)

    wb_in, wb_out = _cast_weights(w_in.reshape(D, 1024), w_out.reshape(256, D))
    jm_arr = (2 * lax.axis_index("x") + lax.axis_index("y")).astype(jnp.int32).reshape(1)
    (hn, q1, k1, v1, q4, k4, v4, q16, k16, v16, ag, hq, hf, hi, hg, w_full, wout4) = _fwd_in(
        xs, pos, mix_norm_w, wb_in, wb_out, jm_arr)
    wout_full = wout4.reshape(D, D)
    flat = lambda a: a.reshape(T, AW)
    o1, l1 = _attn_fwd(q1, k1, v1, T // BLK, "attn_fwd_d1")
    o4, l4 = _attn_fwd(flat(q4), flat(k4), flat(v4), T // 4 // BLK, "attn_fwd_d4")
    o16, l16 = _attn_fwd(flat(q16), flat(k16), flat(v16), T // 16 // BLK, "attn_fwd_d16")
    rec, sall = _hgrn_fwd(hq, hf, hi, hgrn_lb_raw, tri)

    (dx2, do1, do4, do16, st1, st4, st16, drec, dag, dhg, rout, routb, small4) = _fwd_out(
        o1, o4.reshape(4, T // 4, AW), o16.reshape(16, T // 16, AW),
        l1, l4.reshape(4, T // 4, 128), l16.reshape(16, T // 16, 128),
        rec, ag, hg, xs, tgt, attn_out_norm_w, hgrn_out_norm_w, fnw, wout_full, gmat, emat, selmat)

    fst = lambda a: a.reshape(T, 128)
    dq1, dk1, dv1 = _attn_bwd(q1, k1, v1, do1, st1, T // BLK, "attn_bwd_d1")
    dq4, dk4, dv4 = _attn_bwd(flat(q4), flat(k4), flat(v4), flat(do4), fst(st4), T // 4 // BLK, "attn_bwd_d4")
    dq16, dk16, dv16 = _attn_bwd(flat(q16), flat(k16), flat(v16), flat(do16), fst(st16), T // 16 // BLK,
                                 "attn_bwd_d16")
    dhq, dhf, dhi, small6, pout_own, pout_rem = _hgrn_bwd(hq, hf, hi, hgrn_lb_raw, tri, trit, drec, sall,
                                                          rout, routb)

    r4 = lambda a: a.reshape(4, T // 4, AW)
    r16 = lambda a: a.reshape(16, T // 16, AW)
    dproj = _dproj_build((dq1, r4(dq4), r16(dq16)), (dk1, r4(dk4), r16(dk16)), (dv1, r4(dv4), r16(dv16)),
                         dag, dhq, dhf, dhi, dhg, pos)
    rin, rinb = _grad_w_in(hn, dproj)
    gx, pin_own, pin_rem, small_all = _bwd_x(dproj, xs, dx2, mix_norm_w, w_full, rin, rinb, small4, small6)
    pin = _chip_sum(pin_own, pin_rem, "chip_sum_in")
    pout = _chip_sum(pout_own, pout_rem, "chip_sum_out")
    fin, fout = _pair_share(pin, pout)
    g_w_in = fin.reshape(D, 1024)
    g_w_out = fout.reshape(256, D)

    d_in, nm_in, nv_in = _adamw(w_in.reshape(D, 1024), g_w_in, m_w_in.reshape(D, 1024), v_w_in.reshape(D, 1024),
                                "adamw_w_in")
    d_out, nm_out, nv_out = _adamw(w_out.reshape(256, D), g_w_out, m_w_out.reshape(256, D), v_w_out.reshape(256, D),
                                   "adamw_w_out")
    params = [(mix_norm_w, m_mix_norm_w, v_mix_norm_w),
              (attn_out_norm_w, m_attn_out_norm_w, v_attn_out_norm_w),
              (hgrn_out_norm_w, m_hgrn_out_norm_w, v_hgrn_out_norm_w),
              (hgrn_lb_raw, m_hgrn_lb_raw, v_hgrn_lb_raw),
              (fnw, m_final_norm_w.reshape(1, D), v_final_norm_w.reshape(1, D))]
    so = _adamw_small(small_all, params)
    loss = so[0].reshape(())
    g_s = [so[1 + 4 * p] for p in range(5)]
    d_s = [so[2 + 4 * p] for p in range(5)]
    m_s = [so[3 + 4 * p] for p in range(5)]
    v_s = [so[4 + 4 * p] for p in range(5)]
    for lst in (g_s, d_s, m_s, v_s):
        lst[4] = lst[4].reshape(D)

    return (loss, gx.reshape(1, T, D),
            g_w_in.reshape(1, D, 1024), g_w_out.reshape(1, 256, D), *g_s,
            d_in.reshape(1, D, 1024), d_out.reshape(1, 256, D), *d_s,
            nm_in.reshape(1, D, 1024), nm_out.reshape(1, 256, D), *m_s,
            nv_in.reshape(1, D, 1024), nv_out.reshape(1, 256, D), *v_s)
```

```python
import functools

import numpy as np
import jax
import jax.numpy as jnp
from jax import lax
from jax.experimental import pallas as pl
from jax.experimental.pallas import tpu as pltpu

F32 = jnp.float32
BF16 = jnp.bfloat16

T = 4096
D = 1024
AW = 512
HW = 512
NCOL = 4096
HEAD = 64
BLK = 128
CHUNK = 64
EPS = 1e-6
SCALE = HEAD ** -0.5
NEG = -1e30
ROPE_THETA = 500000.0
INV_FREQ = [float(v) for v in
            (np.float32(ROPE_THETA) ** (-(np.arange(8, dtype=np.float32)) * np.float32(0.125)))]
LR, B1, B2, AEPS, WD, STEP = 0.001, 0.9, 0.999, 1e-08, 0.01, 10
VMEM_LIMIT = 56 * 1024 * 1024
MESH = pl.DeviceIdType.MESH


def _cp(sem=None, **kw):
    return pltpu.CompilerParams(dimension_semantics=sem, vmem_limit_bytes=VMEM_LIMIT, **kw)


def _mm(a, b):
    return jnp.dot(a, b, preferred_element_type=F32)


def _mm_nt(a, b):
    return lax.dot_general(a, b, (((1,), (1,)), ((), ())), preferred_element_type=F32)


def _mm_tn(a, b):
    return lax.dot_general(a, b, (((0,), (0,)), ((), ())), preferred_element_type=F32)


def _split3(x):
    h = x.astype(BF16)
    r = x - h.astype(F32)
    m = r.astype(BF16)
    l = (r - m.astype(F32)).astype(BF16)
    return h, m, l


def _mm_exact_l(mat_bf, x):
    h, m, l = _split3(x)
    return _mm(mat_bf, h) + _mm(mat_bf, m) + _mm(mat_bf, l)


def _mm_exact_r(x, mat_bf):
    h = x.astype(BF16)
    l = (x - h.astype(F32)).astype(BF16)
    return _mm(h, mat_bf) + _mm(l, mat_bf)


def _sigmoid(x):
    return 0.5 * jnp.tanh(0.5 * x) + 0.5


def _rope_tables(pos):
    lane = lax.broadcasted_iota(jnp.int32, (1, 128), 1)
    jl = lane & 63
    fi = jl & 7
    inv = jnp.zeros((1, 128), F32)
    for kk in range(8):
        inv = jnp.where(fi == kk, INV_FREQ[kk], inv)
    ang = pos.astype(F32) * inv
    c = jnp.cos(ang)
    s = jnp.sin(ang)
    cosf = jnp.where(jl < 16, c, 1.0)
    s1 = jnp.where(jl < 8, -s, 0.0)
    s2 = jnp.where((jl >= 8) & (jl < 16), s, 0.0)
    return cosf, s1, s2


def _rope(t, cosf, s1, s2):
    parts = []
    for ci in range(t.shape[1] // 128):
        tc = t[:, ci * 128:(ci + 1) * 128]
        parts.append(tc * cosf + pltpu.roll(tc, 120, 1) * s1 + pltpu.roll(tc, 8, 1) * s2)
    return jnp.concatenate(parts, axis=1)


def _rope_bwd(g, cosf, s1, s2):
    parts = []
    for ci in range(g.shape[1] // 128):
        gc = g[:, ci * 128:(ci + 1) * 128]
        parts.append(gc * cosf + pltpu.roll(gc * s1, 8, 1) + pltpu.roll(gc * s2, 120, 1))
    return jnp.concatenate(parts, axis=1)


def _perm_store(val, scr, o1, o4, o16, dt):
    n = val.shape[0]
    o1[...] = val.astype(dt)
    for ci in range(val.shape[1] // 128):
        cs = slice(ci * 128, (ci + 1) * 128)
        scr[ci] = val[:, cs]
        for rr in range(4):
            o4[rr, :, cs] = scr[ci, pl.ds(rr, n // 4, stride=4), :].astype(dt)
        for rr in range(16):
            o16[rr, :, cs] = scr[ci, pl.ds(rr, n // 16, stride=16), :].astype(dt)


def _unperm_load(r4, r16, scr_a, scr_b):
    n = scr_a.shape[1]
    nc = r4.shape[-1] // 128
    for ci in range(nc):
        cs = slice(ci * 128, (ci + 1) * 128)
        for rr in range(4):
            scr_a[ci, pl.ds(rr, n // 4, stride=4), :] = r4[rr, :, cs].astype(F32)
        for rr in range(16):
            scr_b[ci, pl.ds(rr, n // 16, stride=16), :] = r16[rr, :, cs].astype(F32)
    return (jnp.concatenate([scr_a[ci] for ci in range(nc)], axis=1),
            jnp.concatenate([scr_b[ci] for ci in range(nc)], axis=1))


def _cast_weights(w_in, w_out):
    def body(win_ref, wout_ref, bin_ref, bout_ref):
        bin_ref[...] = win_ref[...].astype(BF16)

        @pl.when(pl.program_id(0) == 0)
        def _():
            bout_ref[...] = wout_ref[...].astype(BF16)

    return pl.pallas_call(
        body, name="cast_weights", grid=(4,),
        in_specs=[pl.BlockSpec((256, 1024), lambda i: (i, 0)), pl.BlockSpec((256, D), lambda i: (0, 0))],
        out_specs=[pl.BlockSpec((256, 1024), lambda i: (i, 0)), pl.BlockSpec((256, D), lambda i: (0, 0))],
        out_shape=(jax.ShapeDtypeStruct((D, 1024), BF16), jax.ShapeDtypeStruct((256, D), BF16)),
        compiler_params=_cp(("arbitrary",)),
    )(w_in, w_out)


def _fwd_in(x, pos, mixw, wb_in, wb_out, jm_arr):
    TT = 512
    NT = T // TT

    def body(jm_ref, x_ref, pos_ref, mw_ref, wbin_ref, wbout_ref,
             hnt_ref, q1, k1, v1, q4, k4, v4, q16, k16, v16, ag, hq, hf, hi, hg, wfull_o, woutfull_o,
             wbuf, wobuf, hn_all, scr, send_sems, recv_sems, loc_sems):
        s = pl.program_id(0)
        i = pl.program_id(1)
        mx, my, c = lax.axis_index("x"), lax.axis_index("y"), lax.axis_index("c")
        me, sibling = (mx, my, c), (mx, my, 1 - c)
        chips = [(mx, 1 - my), (1 - mx, my), (1 - mx, 1 - my)]
        jm = 2 * mx + my
        rows_in = [pl.ds(pl.multiple_of(h * 512, 512), 512) for h in (c, 1 - c)]
        rows_out = [pl.ds(pl.multiple_of(h * 128, 128), 128) for h in (c, 1 - c)]

        def blk(k):
            return lax.bitwise_xor(jm, k + 1)

        def rc(n, ref, to):
            return pltpu.make_async_remote_copy(src_ref=ref, dst_ref=ref, send_sem=send_sems.at[n],
                                                recv_sem=recv_sems.at[n], device_id=to, device_id_type=MESH)

        send_in = lambda k: rc(k, wbuf.at[jm, rows_in[0], :], (*chips[k], c))
        send_out = lambda k: rc(3 + k, wobuf.at[jm, rows_out[0], :], (*chips[k], c))
        got_in = lambda k: rc(k, wbuf.at[blk(k), rows_in[0], :], me)
        got_out = lambda k: rc(3 + k, wobuf.at[blk(k), rows_out[0], :], me)
        pass_in = lambda k: rc(6 + k, wbuf.at[blk(k), rows_in[0], :], sibling)
        pass_out = lambda k: rc(9 + k, wobuf.at[blk(k), rows_out[0], :], sibling)
        passed_in = lambda k: rc(6 + k, wbuf.at[blk(k), rows_in[1], :], me)
        passed_out = lambda k: rc(9 + k, wobuf.at[blk(k), rows_out[1], :], me)

        def keep(j, n):
            return pltpu.make_async_copy(wbuf.at[j], wfull_o.at[:, pl.ds(j * 1024, 1024)], loc_sems.at[n])

        @pl.when((s == 0) & (i == 0))
        def _():
            own = [pltpu.make_async_copy(wbin_ref, wbuf.at[jm], loc_sems.at[4]),
                   pltpu.make_async_copy(wbout_ref, wobuf.at[jm], loc_sems.at[5])]
            for cp in own:
                cp.start()
            for cp in own:
                cp.wait()
            send_in(0).start()
            send_in(1).start()
            keep(jm, 0).start()

        def arrive(k):
            if k == 0:
                send_in(0).wait_send()
                send_in(1).wait_send()
                send_in(2).start()
            got_in(k).wait_recv()
            pass_in(k).start()
            passed_in(k).wait_recv()
            keep(blk(k), k + 1).start()
            if k == 2:
                for kk in range(3):
                    send_out(kk).start()

        for k in range(3):
            pl.when((s == k + 1) & (i == 0))(functools.partial(arrive, k))

        tile = pl.ds(pl.multiple_of(i * TT, TT), TT)

        @pl.when(s == 0)
        def _():
            xv = x_ref[...]
            r = lax.rsqrt(jnp.mean(xv * xv, axis=-1, keepdims=True) + EPS)
            hnf = (xv * r) * mw_ref[...]
            hn_all[tile, :] = hnf.astype(BF16)
            hnt_ref[...] = hnf.T.astype(BF16)

        def project(jj):
            hn = hn_all[tile, :]
            lo = _mm(hn, wbuf[jj, :, 0:512])
            hi_cols = _mm(hn, wbuf[jj, :, 512:1024])
            if jj == 0:
                cosf, s1, s2 = _rope_tables(pos_ref[...])
                _perm_store(_rope(lo, cosf, s1, s2), scr, q1, q4, q16, BF16)
                _perm_store(_rope(hi_cols, cosf, s1, s2), scr, k1, k4, k16, BF16)
            elif jj == 1:
                _perm_store(lo, scr, v1, v4, v16, BF16)
                ag[...] = hi_cols
            elif jj == 2:
                hq[...] = lo
                hf[...] = hi_cols
            else:
                hi[...] = lo.astype(BF16)
                hg[...] = hi_cols

        j = lax.bitwise_xor(jm, s)
        for jj in range(4):
            pl.when(j == jj)(functools.partial(project, jj))

        @pl.when((s == 3) & (i == NT - 1))
        def _():
            for k in range(3):
                got_out(k).wait_recv()
                pass_out(k).start()
            for k in range(3):
                passed_out(k).wait_recv()
            out = pltpu.make_async_copy(wobuf, woutfull_o, loc_sems.at[4])
            out.start()
            send_in(2).wait_send()
            for k in range(3):
                send_out(k).wait_send()
                pass_in(k).wait_send()
                pass_out(k).wait_send()
            keep(jm, 0).wait()
            for k in range(3):
                keep(blk(k), k + 1).wait()
            out.wait()

    def at_stage_of(jb):
        def index(s, i, jm_ref):
            sa = lax.bitwise_xor(jm_ref[0], jb)
            return jnp.where(s < sa, 0, jnp.where(s == sa, i, NT - 1))
        return index

    tok = lambda w, jb: pl.BlockSpec((TT, w), lambda s, i, jm_ref: (at_stage_of(jb)(s, i, jm_ref), 0))
    d4 = lambda jb: pl.BlockSpec((4, TT // 4, AW), lambda s, i, jm_ref: (0, at_stage_of(jb)(s, i, jm_ref), 0))
    d16 = lambda jb: pl.BlockSpec((16, TT // 16, AW), lambda s, i, jm_ref: (0, at_stage_of(jb)(s, i, jm_ref), 0))
    hbm = pl.BlockSpec(memory_space=pltpu.HBM)
    sd = lambda shape, dt: jax.ShapeDtypeStruct(shape, dt)
    in_own_stage = lambda s, i: jnp.where(s == 0, i, NT - 1)
    grid_spec = pltpu.PrefetchScalarGridSpec(
        num_scalar_prefetch=1, grid=(4, NT),
        in_specs=[pl.BlockSpec((TT, D), lambda s, i, jm_ref: (in_own_stage(s, i), 0)),
                  pl.BlockSpec((TT, 1), lambda s, i, jm_ref: (i, 0)),
                  pl.BlockSpec((1, D), lambda s, i, jm_ref: (0, 0)), hbm, hbm],
        out_specs=[pl.BlockSpec((D, TT), lambda s, i, jm_ref: (0, in_own_stage(s, i))),
                   tok(AW, 0), tok(AW, 0), tok(AW, 1), d4(0), d4(0), d4(1), d16(0), d16(0), d16(1),
                   tok(AW, 1), tok(AW, 2), tok(AW, 2), tok(AW, 3), tok(AW, 3), hbm, hbm],
        scratch_shapes=[pltpu.VMEM((4, D, 1024), BF16), pltpu.VMEM((4, 256, D), BF16), pltpu.VMEM((T, D), BF16),
                        pltpu.VMEM((4, TT, 128), F32), pltpu.SemaphoreType.DMA((12,)),
                        pltpu.SemaphoreType.DMA((12,)), pltpu.SemaphoreType.DMA((6,))])
    return pl.pallas_call(
        body, name="fwd_in", grid_spec=grid_spec,
        out_shape=[sd((D, T), BF16)] + [sd((T, AW), BF16)] * 3 + [sd((4, T // 4, AW), BF16)] * 3
        + [sd((16, T // 16, AW), BF16)] * 3
        + [sd((T, AW), F32), sd((T, AW), F32), sd((T, AW), F32), sd((T, AW), BF16), sd((T, AW), F32),
           sd((D, NCOL), BF16), sd((4, 256, D), BF16)],
        compiler_params=_cp(("arbitrary", "arbitrary")),
    )(jm_arr, x, pos, mixw, wb_in, wb_out)


def _band_mask(key_axis, nkeys=2 * BLK):
    shape = (nkeys, 2 * BLK) if key_axis == 0 else (2 * BLK, nkeys)
    kj = lax.broadcasted_iota(jnp.int32, shape, key_axis)
    qi = lax.broadcasted_iota(jnp.int32, shape, 1 - key_axis) & (BLK - 1)
    return (kj >= qi) & (kj <= qi + BLK), kj, qi


def _stack_heads(t2, in_a):
    z = jnp.zeros_like(t2)
    return jnp.concatenate([jnp.where(in_a[0], t2, z), jnp.where(in_a[1], t2, z)], axis=0)


def _attn_fwd(q, k, v, nb, name):
    n = min(4, nb)
    CH = n * BLK
    halo = nb > n

    def body(*refs):
        if halo:
            q_ref, k_ref, v_ref, kp_ref, vp_ref, o_ref, lse_ref = refs
        else:
            q_ref, k_ref, v_ref, o_ref, lse_ref = refs
        lane = lax.broadcasted_iota(jnp.int32, (1, 128), 1)
        in_a = [lane < HEAD, lane >= HEAD]
        band, kj, _ = _band_mask(1)
        thr0 = jnp.where((n * pl.program_id(0)) % nb == 0, BLK, 0) if halo else BLK
        mask0 = band & (kj >= thr0)
        for b in range(n):
            rs = slice(b * BLK, (b + 1) * BLK)
            stat = jnp.zeros((BLK, 128), F32)
            for hp in range(4):
                cs = slice(hp * 128, (hp + 1) * 128)
                q2s = _stack_heads(q_ref[rs, cs], in_a)
                if b == 0:
                    kprev = kp_ref[:, cs] if halo else k_ref[rs, cs]
                    vprev = vp_ref[:, cs] if halo else v_ref[rs, cs]
                    kk = jnp.concatenate([kprev, k_ref[rs, cs]], axis=0)
                    vv = jnp.concatenate([vprev, v_ref[rs, cs]], axis=0)
                    mask = mask0
                else:
                    kk = k_ref[(b - 1) * BLK:(b + 1) * BLK, cs]
                    vv = v_ref[(b - 1) * BLK:(b + 1) * BLK, cs]
                    mask = band
                s = jnp.where(mask, _mm_nt(q2s, kk) * SCALE, NEG)
                m = jnp.max(s, axis=-1, keepdims=True)
                p = jnp.exp(s - m)
                l = jnp.sum(p, axis=-1, keepdims=True)
                o = _mm(p.astype(BF16), vv) / l
                lse = m + jnp.log(l)
                o_ref[rs, cs] = jnp.where(in_a[0], o[:BLK], o[BLK:])
                stat = jnp.where(lane == 2 * hp, lse[:BLK], stat)
                stat = jnp.where(lane == 2 * hp + 1, lse[BLK:], stat)
            lse_ref[rs, :] = stat

    cur = pl.BlockSpec((CH, AW), lambda i: (i, 0))
    prev = pl.BlockSpec((BLK, AW), lambda i: (jnp.maximum(n * i - 1, 0), 0))
    return pl.pallas_call(
        body, name=name, grid=(T // CH,),
        in_specs=[cur, cur, cur] + ([prev, prev] if halo else []),
        out_specs=[cur, pl.BlockSpec((CH, 128), lambda i: (i, 0))],
        out_shape=[jax.ShapeDtypeStruct((T, AW), F32), jax.ShapeDtypeStruct((T, 128), F32)],
        compiler_params=_cp(("parallel",)),
    )(*((q, k, v) + ((k, v) if halo else ())))


def _attn_bwd(q, k, v, do, st, nb, name):
    n = min(4, nb)
    CH = n * BLK
    NBLK = T // BLK
    halo = nb > n

    def body(*refs):
        if halo:
            (q_ref, k_ref, v_ref, do_ref, st_ref, kp_ref, vp_ref, qn_ref, don_ref, stn_ref,
             dq_ref, dk_ref, dv_ref) = refs
        else:
            q_ref, k_ref, v_ref, do_ref, st_ref, dq_ref, dk_ref, dv_ref = refs
        i = pl.program_id(0)
        lane = lax.broadcasted_iota(jnp.int32, (1, 128), 1)
        in_a = [lane < HEAD, lane >= HEAD]
        band, kj, _ = _band_mask(0)
        thr0 = jnp.where((n * i) % nb == 0, BLK, 0) if halo else BLK
        mask0 = band & (kj >= thr0)

        def stat_rows(st_t, hp):
            lse_r = jnp.concatenate([st_t[2 * hp:2 * hp + 1, :], st_t[2 * hp + 1:2 * hp + 2, :]], axis=1)
            dl_r = jnp.concatenate([st_t[8 + 2 * hp:9 + 2 * hp, :], st_t[9 + 2 * hp:10 + 2 * hp, :]], axis=1)
            return lse_r, dl_r

        st_t = [st_ref[b * BLK:(b + 1) * BLK, :].T for b in range(n)]
        if halo:
            nxt_thr = jnp.where((n * i + n) % nb == 0, 2 * BLK, 0)
            _, kj1, qi1 = _band_mask(0, BLK)
            mask_next = kj1 >= qi1 + nxt_thr
            stn_t = stn_ref[...].T

        for hp in range(4):
            cs = slice(hp * 128, (hp + 1) * 128)
            kb = [k_ref[b * BLK:(b + 1) * BLK, cs] for b in range(n)]
            vb = [v_ref[b * BLK:(b + 1) * BLK, cs] for b in range(n)]
            dk_acc = [jnp.zeros((BLK, 128), F32) for _ in range(n)]
            dv_acc = [jnp.zeros((BLK, 128), F32) for _ in range(n)]
            for b in range(n):
                rs = slice(b * BLK, (b + 1) * BLK)
                q2s = _stack_heads(q_ref[rs, cs], in_a)
                do2s = _stack_heads(do_ref[rs, cs], in_a)
                if b == 0:
                    kprev = kp_ref[:, cs] if halo else kb[0]
                    vprev = vp_ref[:, cs] if halo else vb[0]
                    mask = mask0
                else:
                    kprev, vprev, mask = kb[b - 1], vb[b - 1], band
                kk = jnp.concatenate([kprev, kb[b]], axis=0)
                vv = jnp.concatenate([vprev, vb[b]], axis=0)
                lse_r, dl_r = stat_rows(st_t[b], hp)
                s_t = jnp.where(mask, _mm_nt(kk, q2s) * SCALE, NEG)
                p_t = jnp.exp(s_t - lse_r)
                ds_t = (p_t * (_mm_nt(vv, do2s) - dl_r)).astype(BF16)
                dkk = _mm(ds_t, q2s) * SCALE
                dvv = _mm(p_t.astype(BF16), do2s)
                dqs = _mm_tn(ds_t, kk) * SCALE
                dq_ref[rs, cs] = jnp.where(in_a[0], dqs[:BLK], dqs[BLK:]).astype(BF16)
                dk_acc[b] += dkk[BLK:]
                dv_acc[b] += dvv[BLK:]
                if b > 0:
                    dk_acc[b - 1] += dkk[:BLK]
                    dv_acc[b - 1] += dvv[:BLK]
            if halo:
                q2s = _stack_heads(qn_ref[:, cs], in_a)
                do2s = _stack_heads(don_ref[:, cs], in_a)
                lse_r, dl_r = stat_rows(stn_t, hp)
                s_t = jnp.where(mask_next, _mm_nt(kb[n - 1], q2s) * SCALE, NEG)
                p_t = jnp.exp(s_t - lse_r)
                ds_t = (p_t * (_mm_nt(vb[n - 1], do2s) - dl_r)).astype(BF16)
                dk_acc[n - 1] += _mm(ds_t, q2s) * SCALE
                dv_acc[n - 1] += _mm(p_t.astype(BF16), do2s)
            for b in range(n):
                dk_ref[b * BLK:(b + 1) * BLK, cs] = dk_acc[b].astype(BF16)
                dv_ref[b * BLK:(b + 1) * BLK, cs] = dv_acc[b].astype(BF16)

    cur = pl.BlockSpec((CH, AW), lambda i: (i, 0))
    cur_st = pl.BlockSpec((CH, 128), lambda i: (i, 0))
    prev = pl.BlockSpec((BLK, AW), lambda i: (jnp.maximum(n * i - 1, 0), 0))
    nxt = pl.BlockSpec((BLK, AW), lambda i: (jnp.minimum(n * i + n, NBLK - 1), 0))
    nxt_st = pl.BlockSpec((BLK, 128), lambda i: (jnp.minimum(n * i + n, NBLK - 1), 0))
    ins = [cur] * 4 + [cur_st] + ([prev, prev, nxt, nxt, nxt_st] if halo else [])
    args = (q, k, v, do, st) + ((k, v, q, do, st) if halo else ())
    return pl.pallas_call(
        body, name=name, grid=(T // CH,),
        in_specs=ins,
        out_specs=[cur] * 3,
        out_shape=[jax.ShapeDtypeStruct((T, AW), BF16)] * 3,
        compiler_params=_cp(("parallel",)),
    )(*args)


TH = 256
NCH = TH // CHUNK


def _hgrn_common(hq_ref, hf_ref, lbr_ref, tri_ref):
    r0 = lbr_ref[0:1, :]
    r1 = lbr_ref[1:2, :]
    mx = jnp.maximum(r0, r1)
    e0 = jnp.exp(r0 - mx)
    e1 = jnp.exp(r1 - mx)
    lb = e0 / (e0 + e1)
    hqv = hq_ref[...]
    sq = _sigmoid(hqv)
    qv = hqv * sq
    sf = _sigmoid(hf_ref[...])
    f = lb + (1.0 - lb) * sf
    kv = 1.0 - f
    g = jnp.log(f)
    cum = _mm_exact_l(tri_ref[...], g)
    lastb = jnp.concatenate(
        [jnp.broadcast_to(cum[c * CHUNK + CHUNK - 1:(c + 1) * CHUNK, :], (CHUNK, HW)) for c in range(NCH)], axis=0)
    ea = jnp.exp(cum)
    ena = jnp.exp(-cum)
    eend = jnp.exp(lastb - cum)
    return dict(lb=lb, hq=hqv, sq=sq, q=qv, sf=sf, f=f, k=kv, cum=cum, lastb=lastb, ea=ea, ena=ena, eend=eend,
                qd=qv * ea, ki=kv * ena, ke=kv * eend, dec=jnp.exp(lastb))


def _tri_mask(transposed=False):
    ti = lax.broadcasted_iota(jnp.int32, (TH, TH), 1 if transposed else 0)
    si = lax.broadcasted_iota(jnp.int32, (TH, TH), 0 if transposed else 1)
    return (si <= ti) & ((si // CHUNK) == (ti // CHUNK))


def _hgrn_fwd(hq, hf, hi, lbr, tri):
    def body(hq_ref, hf_ref, hi_ref, lbr_ref, tri_ref, rec_ref, sall_ref, st_scr):
        @pl.when(pl.program_id(0) == 0)
        def _():
            st_scr[...] = jnp.zeros_like(st_scr)

        w = _hgrn_common(hq_ref, hf_ref, lbr_ref, tri_ref)
        qd, ki, ke = w["qd"].astype(BF16), w["ki"].astype(BF16), w["ke"].astype(BF16)
        dec = w["dec"]
        vb = hi_ref[...]
        causal = _tri_mask()
        for h in range(4):
            cs = slice(h * 128, (h + 1) * 128)
            att = jnp.where(causal, _mm_nt(qd[:, cs], ki[:, cs]), 0.0)
            o_intra = _mm(att.astype(BF16), vb[:, cs])
            for c in range(NCH):
                rs = slice(c * CHUNK, (c + 1) * CHUNK)
                st = st_scr[:, cs]
                sall_ref[c, :, cs] = st
                rec_ref[rs, cs] = o_intra[rs] + _mm_nt(qd[rs, cs], st.astype(BF16))
                st_scr[:, cs] = dec[c * CHUNK:c * CHUNK + 1, cs] * st + _mm_tn(vb[rs, cs], ke[rs, cs])

    tok = pl.BlockSpec((TH, HW), lambda i: (i, 0))
    return pl.pallas_call(
        body, name="hgrn_fwd", grid=(T // TH,),
        in_specs=[tok, tok, tok, pl.BlockSpec((2, HW), lambda i: (0, 0)), pl.BlockSpec((TH, TH), lambda i: (0, 0))],
        out_specs=[tok, pl.BlockSpec((NCH, 128, HW), lambda i: (i, 0, 0))],
        out_shape=[jax.ShapeDtypeStruct((T, HW), F32), jax.ShapeDtypeStruct((T // CHUNK, 128, HW), F32)],
        scratch_shapes=[pltpu.VMEM((128, HW), F32)],
        compiler_params=_cp(("arbitrary",)),
    )(hq, hf, hi, lbr, tri)


def _hgrn_bwd(hq, hf, hi, lbr, tri, trit, drec, sall, dhg, rout, routb):
    NT = T // TH

    def body(hq_ref, hf_ref, hi_ref, lbr_ref, tri_ref, trit_ref, do_ref, sall_ref, dhg_ref, rout_r, routb_r,
             dph_ref, small_ref, pout_o, poutr_o,
             dst_scr, dlb_scr, dqd_scr, dki_scr, dke_scr, dlast_scr, send_sems, recv_sems, loc_sems):
        step = pl.program_id(0)
        loc, rem = _chip_copies(_w_out_piece, rout_r, routb_r, pout_o, poutr_o, send_sems, recv_sems,
                                loc_sems.at[0])

        @pl.when(step == 0)
        def _():
            dst_scr[...] = jnp.zeros_like(dst_scr)
            dlb_scr[...] = jnp.zeros_like(dlb_scr)
            for cp in loc + rem:
                cp.start()

        w = _hgrn_common(hq_ref, hf_ref, lbr_ref, tri_ref)
        qd, ki, ke = w["qd"].astype(BF16), w["ki"].astype(BF16), w["ke"].astype(BF16)
        dec = w["dec"]
        vb = hi_ref[...]
        dob = do_ref[...].astype(BF16)
        causal = _tri_mask()
        causal_t = _tri_mask(transposed=True)
        for h in range(4):
            cs = slice(h * 128, (h + 1) * 128)
            att_t = jnp.where(causal_t, _mm_nt(ki[:, cs], qd[:, cs]), 0.0).astype(BF16)
            datt_t = jnp.where(causal_t, _mm_nt(vb[:, cs], dob[:, cs]), 0.0).astype(BF16)
            datt = jnp.where(causal, _mm_nt(dob[:, cs], vb[:, cs]), 0.0).astype(BF16)
            dv_intra = _mm(att_t, dob[:, cs])
            dqd_intra = _mm(datt, ki[:, cs])
            dki_scr[:, cs] = _mm(datt_t, qd[:, cs])
            for c in reversed(range(NCH)):
                rs = slice(c * CHUNK, (c + 1) * CHUNK)
                dec_c = dec[c * CHUNK:c * CHUNK + 1, :]
                st = sall_ref[c, :, cs]
                dst = dst_scr[:, cs]
                dstb = dst.astype(BF16)
                dph_ref[rs, 2 * HW + h * 128:2 * HW + (h + 1) * 128] = (
                    dv_intra[rs] + _mm_nt(ke[rs, cs], dstb)).astype(BF16)
                dqd_scr[rs, cs] = dqd_intra[rs] + _mm(dob[rs, cs], st.astype(BF16))
                dke_scr[rs, cs] = _mm(vb[rs, cs], dstb)
                ddec = jnp.sum(dst * st, axis=0, keepdims=True)
                dlast_scr[c:c + 1, cs] = ddec * dec_c[:, cs]
                dst_scr[:, cs] = dec_c[:, cs] * dst + _mm_tn(dob[rs, cs], qd[rs, cs])
        dqd, dki, dke = dqd_scr[...], dki_scr[...], dke_scr[...]
        dq = dqd * w["ea"]
        dk = dki * w["ena"] + dke * w["eend"]
        dcum = dqd * w["qd"] - dki * w["ki"] - dke * w["ke"]
        dkeke = dke * w["ke"]
        dlastb = jnp.concatenate(
            [jnp.broadcast_to(dlast_scr[c:c + 1, :] + jnp.sum(dkeke[c * CHUNK:(c + 1) * CHUNK], axis=0, keepdims=True),
                              (CHUNK, HW)) for c in range(NCH)], axis=0)
        dg = _mm_exact_l(trit_ref[...], dcum) + dlastb
        df = dg / w["f"] - dk
        lb, sf, sq = w["lb"], w["sf"], w["sq"]
        dph_ref[:, HW:2 * HW] = (df * (1.0 - lb) * sf * (1.0 - sf)).astype(BF16)
        dph_ref[:, 0:HW] = (dq * (sq * (1.0 + w["hq"] * (1.0 - sq)))).astype(BF16)
        dph_ref[:, 3 * HW:4 * HW] = dhg_ref[...]
        dlb_scr[...] += jnp.sum(df * (1.0 - sf), axis=0, keepdims=True)

        @pl.when(step == NT - 1)
        def _():
            gr = dlb_scr[...] * lb * (1.0 - lb)
            small_ref[...] = jnp.zeros_like(small_ref)
            small_ref[0:1, 0:HW] = gr
            small_ref[1:2, 0:HW] = -gr
            for cp in rem:
                cp.wait_recv()
            for cp in rem:
                cp.wait_send()
            for cp in loc:
                cp.wait()

    tok = pl.BlockSpec((TH, HW), lambda i: (NT - 1 - i, 0))
    const = lambda shape: pl.BlockSpec(shape, lambda i: (0,) * len(shape))
    hbm = pl.BlockSpec(memory_space=pltpu.HBM)
    return pl.pallas_call(
        body, name="hgrn_bwd", grid=(NT,),
        in_specs=[tok, tok, tok, const((2, HW)), const((TH, TH)), const((TH, TH)), tok,
                  pl.BlockSpec((NCH, 128, HW), lambda i: (NT - 1 - i, 0, 0)), tok, hbm, hbm],
        out_specs=[pl.BlockSpec((TH, NCOL // 2), lambda i: (NT - 1 - i, 0)), const((8, D)), hbm, hbm],
        out_shape=[jax.ShapeDtypeStruct((T, NCOL // 2), BF16), jax.ShapeDtypeStruct((8, D), F32),
                   jax.ShapeDtypeStruct((128, D), F32), jax.ShapeDtypeStruct((3, 128, D), BF16)],
        scratch_shapes=[pltpu.VMEM((128, HW), F32), pltpu.VMEM((1, HW), F32), pltpu.VMEM((TH, HW), F32),
                        pltpu.VMEM((TH, HW), F32), pltpu.VMEM((TH, HW), F32), pltpu.VMEM((8, HW), F32),
                        pltpu.SemaphoreType.DMA((3,)), pltpu.SemaphoreType.DMA((3,)), pltpu.SemaphoreType.DMA((1,))],
        compiler_params=_cp(("arbitrary",)),
    )(hq, hf, hi, lbr, tri, trit, drec, sall, dhg, rout, routb)


def _fwd_out(o1, o4, o16, l1, l4, l16, rec, ag, hg, x, tgt, anw, hnw, fnw, wout_full, gmat, emat, selmat):
    TT = 256

    def body(o1_r, o4_r, o16_r, l1_r, l4_r, l16_r, rec_r, ag_r, hg_r, x_r, tgt_r, anw_r, hnw_r, fnw_r, wo_r, g_r,
             e_r, sel_r, dx2_o, do1_o, do4_o, do16_o, st1_o, st4_o, st16_o, drec_o, dag_o, dhg_o,
             rout_o, routb_o, small_o, scr_a, scr_b, gwout_o, rbuf, send_sems, recv_sems):
        @pl.when(pl.program_id(0) == 0)
        def _():
            gwout_o[...] = jnp.zeros_like(gwout_o)
            small_o[...] = jnp.zeros_like(small_o)

        def unperm(r4, r16):
            return _unperm_load(r4, r16, scr_a, scr_b)

        def perm_out(val, p1, p4, p16, dt):
            _perm_store(val, scr_a, p1, p4, p16, dt)

        o4u, o16u = unperm(o4_r, o16_r)
        l4c, l16c = unperm(l4_r, l16_r)
        em = e_r[...]
        l1v, l4u, l16u = _mm_exact_r(l1_r[...], em), _mm_exact_r(l4c, em), _mm_exact_r(l16c, em)
        o1v = o1_r[...]
        mx = jnp.maximum(jnp.maximum(l1v, l4u), l16u)
        w1, w4, w16 = jnp.exp(l1v - mx), jnp.exp(l4u - mx), jnp.exp(l16u - mx)
        den = w1 + w4 + w16
        attn = (w1 * o1v + w4 * o4u + w16 * o16u) / den
        lse = mx + jnp.log(den)
        gm = g_r[...]

        def head_mean_a(t):
            return jnp.concatenate([_mm_exact_r(t[:, :256], gm), _mm_exact_r(t[:, 256:], gm)], axis=1)

        def head_mean_h(t):
            return jnp.concatenate(
                [jnp.broadcast_to(jnp.mean(t[:, h * 128:(h + 1) * 128], axis=-1, keepdims=True), (TT, 128))
                 for h in range(4)], axis=1)

        rs_a = lax.rsqrt(head_mean_a(attn * attn) + EPS)
        n_a = attn * rs_a
        agv = ag_r[...]
        sg_a = _sigmoid(agv)
        si_a = agv * sg_a
        anw_v = anw_r[...]
        y_a = (n_a * anw_v) * si_a
        recv = rec_r[...]
        rs_h = lax.rsqrt(head_mean_h(recv * recv) + EPS)
        n_h = recv * rs_h
        hgv = hg_r[...]
        sg_h = _sigmoid(hgv)
        si_h = hgv * sg_h
        hnw_v = hnw_r[...]
        y_h = (n_h * hnw_v) * si_h
        mixed = jnp.concatenate([y_a, y_h], axis=1).astype(BF16)
        xv = x_r[...]
        x2 = xv + _mm(mixed, wo_r[...])
        r2 = lax.rsqrt(jnp.mean(x2 * x2, axis=-1, keepdims=True) + EPS)
        fnw_v = fnw_r[...]
        xn = x2 * r2
        err = xn * fnw_v - tgt_r[...]
        small_o[2:3, :] += 0.5 * jnp.sum(jnp.mean(err * err, axis=-1, keepdims=True), axis=0, keepdims=True)
        dy = err * (1.0 / D)
        small_o[0:1, :] += jnp.sum(dy * xn, axis=0, keepdims=True)
        dyw = dy * fnw_v
        dx2 = r2 * dyw - x2 * ((r2 * r2 * r2) * jnp.mean(dyw * x2, axis=-1, keepdims=True))
        dx2_o[...] = dx2
        dx2b = dx2.astype(BF16)
        gwout_o[...] += _mm_tn(mixed, dx2b)
        dmix = _mm_nt(dx2b, wo_r[...])
        dm_a, dm_h = dmix[:, :AW], dmix[:, AW:]
        dag_o[...] = (dm_a * (n_a * anw_v) * (sg_a * (1.0 + agv * (1.0 - sg_a)))).astype(BF16)
        dn_a = dm_a * anw_v * si_a
        small_o[1:2, 0:AW] += jnp.sum(dm_a * n_a * si_a, axis=0, keepdims=True)
        dattn = rs_a * (dn_a - n_a * head_mean_a(dn_a * n_a))
        delta = head_mean_a(dattn * attn) * float(HEAD)
        perm_out(dattn, do1_o, do4_o, do16_o, BF16)
        stats = _mm_exact_r(lse, sel_r[0]) + _mm_exact_r(delta, sel_r[1])
        perm_out(stats, st1_o, st4_o, st16_o, F32)
        dhg_o[...] = (dm_h * (n_h * hnw_v) * (sg_h * (1.0 + hgv * (1.0 - sg_h)))).astype(BF16)
        dn_h = dm_h * hnw_v * si_h
        small_o[1:2, AW:] += jnp.sum(dm_h * n_h * si_h, axis=0, keepdims=True)
        drec_o[...] = (rs_h * (dn_h - n_h * head_mean_h(dn_h * n_h))).astype(BF16)

        @pl.when(pl.program_id(0) == T // TT - 1)
        def _():
            x, y, c = lax.axis_index("x"), lax.axis_index("y"), lax.axis_index("c")
            cps = [pltpu.make_async_remote_copy(
                src_ref=gwout_o.at[pl.ds(pl.multiple_of(j * 256 + (1 - c) * 128, 128), 128), :], dst_ref=rbuf.at[j],
                send_sem=send_sems.at[j], recv_sem=recv_sems.at[j], device_id=(x, y, 1 - c), device_id_type=MESH)
                for j in range(4)]
            for cp in cps:
                cp.start()
            for j, cp in enumerate(cps):
                cp.wait_recv()
                red = gwout_o[pl.ds(pl.multiple_of(j * 256 + c * 128, 128), 128), :] + rbuf[j]
                rout_o[j * 128:(j + 1) * 128, :] = red
                routb_o[j * 128:(j + 1) * 128, :] = red.astype(BF16)
            for cp in cps:
                cp.wait_send()

    tok = lambda w: pl.BlockSpec((TT, w), lambda i: (i, 0))
    d4 = pl.BlockSpec((4, TT // 4, AW), lambda i: (0, i, 0))
    d16 = pl.BlockSpec((16, TT // 16, AW), lambda i: (0, i, 0))
    const = lambda shape: pl.BlockSpec(shape, lambda i: (0,) * len(shape))
    sd = lambda shape, dt: jax.ShapeDtypeStruct(shape, dt)
    c4 = pl.BlockSpec((4, TT // 4, 128), lambda i: (0, i, 0))
    c16 = pl.BlockSpec((16, TT // 16, 128), lambda i: (0, i, 0))
    p3 = lambda w, dt: [sd((T, w), dt), sd((4, T // 4, w), dt), sd((16, T // 16, w), dt)]
    return pl.pallas_call(
        body, name="fwd_out", grid=(T // TT,),
        in_specs=[tok(AW), d4, d16, tok(128), c4, c16, tok(AW), tok(AW), tok(AW), tok(D), tok(D),
                  const((1, AW)), const((1, HW)), const((1, D)), const((D, D)), const((256, 256)),
                  const((128, AW)), const((2, AW, 128))],
        out_specs=[tok(D)] + [tok(AW), d4, d16] + [tok(128), c4, c16] + [tok(AW)] * 3
        + [const((512, D)), const((512, D)), const((8, D))],
        out_shape=[sd((T, D), F32)] + p3(AW, BF16) + p3(128, F32)
        + [sd((T, AW), BF16), sd((T, AW), BF16), sd((T, AW), BF16), sd((512, D), F32), sd((512, D), BF16),
           sd((8, D), F32)],
        scratch_shapes=[pltpu.VMEM((4, TT, 128), F32), pltpu.VMEM((4, TT, 128), F32), pltpu.VMEM((D, D), F32),
                        pltpu.VMEM((4, 128, D), F32), pltpu.SemaphoreType.DMA((4,)), pltpu.SemaphoreType.DMA((4,))],
        compiler_params=_cp(("arbitrary",)),
    )(o1, o4, o16, l1, l4, l16, rec, ag, hg, x, tgt, anw, hnw, fnw, wout_full, gmat, emat, selmat)


def _dproj_build(dq, dk, dv, dag, pos):
    TT = 256

    def body(dq1, dq4, dq16, dk1, dk4, dk16, dv1, dv4, dv16, dag_r, pos_r, dproj_o, scr_a, scr_b):
        def unperm_sum(r1, r4, r16):
            u4, u16 = _unperm_load(r4, r16, scr_a, scr_b)
            return r1[...] + u4 + u16

        cosf, s1, s2 = _rope_tables(pos_r[...])
        dproj_o[:, 0:512] = _rope_bwd(unperm_sum(dq1, dq4, dq16), cosf, s1, s2).astype(BF16)
        dproj_o[:, 512:1024] = _rope_bwd(unperm_sum(dk1, dk4, dk16), cosf, s1, s2).astype(BF16)
        dproj_o[:, 1024:1536] = unperm_sum(dv1, dv4, dv16).astype(BF16)
        dproj_o[:, 1536:2048] = dag_r[...]

    tok = lambda w: pl.BlockSpec((TT, w), lambda i: (i, 0))
    d4 = pl.BlockSpec((4, TT // 4, AW), lambda i: (0, i, 0))
    d16 = pl.BlockSpec((16, TT // 16, AW), lambda i: (0, i, 0))
    return pl.pallas_call(
        body, name="dproj_build", grid=(T // TT,),
        in_specs=[tok(AW), d4, d16] * 3 + [tok(AW), tok(1)],
        out_specs=tok(NCOL // 2),
        out_shape=jax.ShapeDtypeStruct((T, NCOL // 2), BF16),
        scratch_shapes=[pltpu.VMEM((4, TT, 128), F32), pltpu.VMEM((4, TT, 128), F32)],
        compiler_params=_cp(("parallel",)),
    )(*dq, *dk, *dv, dag, pos)


def _bwd_x(dproj_a, dproj_h, x, dx2, mixw, w_full, rin, rinb, small4, small6):
    TT = 256
    NT = T // TT

    def body(dpa_r, dph_r, x_r, dx2_r, mw_r, w_r, rin_r, rinb_r, s4_r, s6_r,
             gx_o, pin_o, pinr_o, sall_o, sbuf, send_sems, recv_sems, loc_sems):
        i = pl.program_id(0)
        loc, rem = _chip_copies(_w_in_piece, rin_r, rinb_r, pin_o, pinr_o, send_sems, recv_sems, loc_sems.at[0])

        @pl.when(i == 0)
        def _():
            sbuf[...] = jnp.zeros_like(sbuf)
            for cp in loc + rem:
                cp.start()

        dhn = _mm_nt(dpa_r[...], w_r[:, 0:NCOL // 2]) + _mm_nt(dph_r[...], w_r[:, NCOL // 2:NCOL])
        xv = x_r[...]
        r = lax.rsqrt(jnp.mean(xv * xv, axis=-1, keepdims=True) + EPS)
        dxw = dhn * mw_r[...]
        gx_o[...] = dx2_r[...] + r * dxw - xv * ((r * r * r) * jnp.mean(dxw * xv, axis=-1, keepdims=True))
        sbuf[16:17, :] += jnp.sum(dhn * (xv * r), axis=0, keepdims=True)

        @pl.when(i == NT - 1)
        def _():
            sbuf[0:8, :] = s4_r[...]
            sbuf[8:16, :] = s6_r[...]
            sloc, srem = _small_copies(sbuf, sall_o, send_sems, recv_sems, loc_sems.at[1])
            for cp in sloc + srem:
                cp.start()
            for cp in rem + srem:
                cp.wait_recv()
            for cp in rem + srem:
                cp.wait_send()
            for cp in loc + sloc:
                cp.wait()

    tok = lambda w: pl.BlockSpec((TT, w), lambda i: (i, 0))
    const = lambda shape: pl.BlockSpec(shape, lambda i: (0,) * len(shape))
    hbm = pl.BlockSpec(memory_space=pltpu.HBM)
    return pl.pallas_call(
        body, name="bwd_x", grid=(NT,),
        in_specs=[tok(NCOL // 2), tok(NCOL // 2), tok(D), tok(D), const((1, D)), const((D, NCOL)), hbm, hbm,
                  const((8, D)), const((8, D))],
        out_specs=[tok(D), hbm, hbm, hbm],
        out_shape=[jax.ShapeDtypeStruct((T, D), F32),
                   jax.ShapeDtypeStruct((512, 1024), F32), jax.ShapeDtypeStruct((3, 512, 1024), BF16),
                   jax.ShapeDtypeStruct((8, 24, D), F32)],
        scratch_shapes=[pltpu.VMEM((24, D), F32), pltpu.SemaphoreType.DMA((10,)), pltpu.SemaphoreType.DMA((10,)),
                        pltpu.SemaphoreType.DMA((2,))],
        compiler_params=_cp(("arbitrary",)),
    )(dproj_a, dproj_h, x, dx2, mixw, w_full, rin, rinb, small4, small6)


def _grad_w_in(hn, dproj_a, dproj_h):
    TK = 1024
    NK = T // TK

    def body(hnt_r, dpa_r, dph_r, rin_o, rinb_o, acc, rbuf, obuf, obufb, send_sems, recv_sems, wb_sems):
        j = pl.program_id(0)
        kk = pl.program_id(1)
        x, y, c = lax.axis_index("x"), lax.axis_index("y"), lax.axis_index("c")
        mine = pl.ds(pl.multiple_of(c * 512, 512), 512)
        theirs = pl.ds(pl.multiple_of((1 - c) * 512, 512), 512)

        def send(jj):
            return pltpu.make_async_remote_copy(
                src_ref=acc.at[jj % 2, theirs, :], dst_ref=rbuf.at[jj], send_sem=send_sems.at[jj],
                recv_sem=recv_sems.at[jj], device_id=(x, y, 1 - c), device_id_type=MESH)

        def writeback(jj):
            cols = pl.ds(jj * 1024, 1024)
            return [pltpu.make_async_copy(obuf.at[jj % 2], rin_o.at[:, cols], wb_sems.at[jj % 2]),
                    pltpu.make_async_copy(obufb.at[jj % 2], rinb_o.at[:, cols], wb_sems.at[2 + jj % 2])]

        def wait_writeback(jj):
            for cp in writeback(jj):
                cp.wait()

        def finalize(jj):
            send(jj).wait_recv()
            red = acc[jj % 2, mine, :] + rbuf[jj]
            obuf[jj % 2] = red
            obufb[jj % 2] = red.astype(BF16)
            for cp in writeback(jj):
                cp.start()

        prod = _mm(hnt_r[...], jnp.where(j < 2, dpa_r[...], dph_r[...]))

        @pl.when(kk == 0)
        def _():
            for jj in (2, 3):
                @pl.when(j == jj)
                def _():
                    send(jj - 2).wait_send()
            acc[j % 2] = prod

        @pl.when(kk > 0)
        def _():
            acc[j % 2] += prod

        @pl.when(kk == NK - 1)
        def _():
            for jj in range(4):
                @pl.when(j == jj)
                def _():
                    send(jj).start()
                    if jj in (1, 2):
                        finalize(jj - 1)
                    if jj == 3:
                        wait_writeback(0)
                        finalize(2)
                        wait_writeback(1)
                        finalize(3)
                        wait_writeback(2)
                        wait_writeback(3)
                        send(2).wait_send()
                        send(3).wait_send()

    hbm = pl.BlockSpec(memory_space=pltpu.HBM)
    return pl.pallas_call(
        body, name="grad_w_in", grid=(4, NK),
        in_specs=[pl.BlockSpec((D, TK), lambda j, kk: (0, kk)),
                  pl.BlockSpec((TK, 1024), lambda j, kk: (jnp.where(j < 2, kk, NK - 1), jnp.minimum(j, 1))),
                  pl.BlockSpec((TK, 1024), lambda j, kk: (jnp.where(j < 2, 0, kk), jnp.maximum(j - 2, 0)))],
        out_specs=[hbm, hbm],
        out_shape=[jax.ShapeDtypeStruct((512, NCOL), F32), jax.ShapeDtypeStruct((512, NCOL), BF16)],
        scratch_shapes=[pltpu.VMEM((2, D, 1024), F32), pltpu.VMEM((4, 512, 1024), F32), pltpu.VMEM((2, 512, 1024), F32),
                        pltpu.VMEM((2, 512, 1024), BF16),
                        pltpu.SemaphoreType.DMA((4,)), pltpu.SemaphoreType.DMA((4,)), pltpu.SemaphoreType.DMA((4,))],
        compiler_params=_cp(("arbitrary", "arbitrary")),
    )(hn, dproj_a, dproj_h)


def _w_in_piece(ref, j):
    return ref.at[:, pl.ds(j * 1024, 1024)]


def _w_out_piece(ref, j):
    return ref.at[pl.ds(j * 128, 128), :]


def _chip_copies(piece, src_r, srcb_r, own_o, rem_o, send_sems, recv_sems, loc_sem):
    x, y, c = lax.axis_index("x"), lax.axis_index("y"), lax.axis_index("c")
    chips = [(1 - x, y), (x, 1 - y), (1 - x, 1 - y)]
    loc = [pltpu.make_async_copy(piece(src_r, 2 * x + y), own_o, loc_sem)]
    rem = [pltpu.make_async_remote_copy(
        src_ref=piece(srcb_r, 2 * px + py), dst_ref=rem_o.at[k], send_sem=send_sems.at[k],
        recv_sem=recv_sems.at[k], device_id=(px, py, c), device_id_type=MESH) for k, (px, py) in enumerate(chips)]
    return loc, rem


def _small_copies(small_r, sall_o, send_sems, recv_sems, loc_sem):
    x, y, c = lax.axis_index("x"), lax.axis_index("y"), lax.axis_index("c")
    me = 4 * x + 2 * y + c
    loc = [pltpu.make_async_copy(small_r, sall_o.at[me], loc_sem)]
    rem = []
    k = 3
    for fx in range(2):
        for fy in range(2):
            for fc in range(2):
                if fx or fy or fc:
                    peer = (1 - x if fx else x, 1 - y if fy else y, 1 - c if fc else c)
                    rem.append(pltpu.make_async_remote_copy(
                        src_ref=small_r, dst_ref=sall_o.at[me], send_sem=send_sems.at[k],
                        recv_sem=recv_sems.at[k], device_id=peer, device_id_type=MESH))
                    k += 1
    return loc, rem


def _pair_share(pin_own, pin_rem, pout_own, pout_rem):
    def body(pio_r, pir_r, poo_r, por_r, fin_o, fout_o, sin, sout, send_sems, recv_sems):
        x, y, c = lax.axis_index("x"), lax.axis_index("y"), lax.axis_index("c")
        sibling = (x, y, 1 - c)
        sout[...] = ((poo_r[...] + por_r[0].astype(F32)) + por_r[1].astype(F32)) + por_r[2].astype(F32)
        sin[...] = ((pio_r[...] + pir_r[0].astype(F32)) + pir_r[1].astype(F32)) + pir_r[2].astype(F32)
        rem = [pltpu.make_async_remote_copy(src_ref=sin, dst_ref=fin_o.at[c], send_sem=send_sems.at[0],
                                            recv_sem=recv_sems.at[0], device_id=sibling, device_id_type=MESH),
               pltpu.make_async_remote_copy(src_ref=sout, dst_ref=fout_o.at[c], send_sem=send_sems.at[1],
                                            recv_sem=recv_sems.at[1], device_id=sibling, device_id_type=MESH)]
        for cp in rem:
            cp.start()
        fin_o[c] = sin[...]
        fout_o[c] = sout[...]
        for cp in rem:
            cp.wait_recv()
        for cp in rem:
            cp.wait_send()

    vm = pl.BlockSpec(memory_space=pltpu.VMEM)
    return pl.pallas_call(
        body, name="pair_share",
        out_shape=(jax.ShapeDtypeStruct((2, 512, 1024), F32), jax.ShapeDtypeStruct((2, 128, D), F32)),
        in_specs=[vm, vm, vm, vm], out_specs=(vm, vm),
        scratch_shapes=[pltpu.VMEM((512, 1024), F32), pltpu.VMEM((128, D), F32),
                        pltpu.SemaphoreType.DMA((2,)), pltpu.SemaphoreType.DMA((2,))],
        compiler_params=_cp(),
    )(pin_own, pin_rem, pout_own, pout_rem)


def _adamw_math(w, g, m, v):
    m = B1 * m + (1.0 - B1) * g
    v = B2 * v + (1.0 - B2) * (g * g)
    m_hat = m / (1.0 - B1 ** STEP)
    v_hat = v / (1.0 - B2 ** STEP)
    delta = -LR * (m_hat / (jnp.sqrt(v_hat) + AEPS) + WD * w)
    return delta, m, v


def _adamw(w, g, m, v, name):
    rows, cols = w.shape
    tr = min(rows, 256)

    def body(w_r, g_r, m_r, v_r, d_o, m_o, v_o):
        d, mm, vv = _adamw_math(w_r[...], g_r[...], m_r[...], v_r[...])
        d_o[...] = d
        m_o[...] = mm
        v_o[...] = vv

    blk = pl.BlockSpec((tr, cols), lambda i: (i, 0))
    return pl.pallas_call(
        body, name=name, grid=(rows // tr,),
        in_specs=[blk] * 4, out_specs=[blk] * 3,
        out_shape=[jax.ShapeDtypeStruct((rows, cols), F32)] * 3,
        compiler_params=_cp(("parallel",)),
    )(w, g, m, v)


def _adamw_small(sall, params):
    def body(sall_r, *refs):
        ins, outs = refs[:15], refs[15:]
        tot = sall_r[0]
        for dv in range(1, 8):
            tot = tot + sall_r[dv]
        grads = [tot[16:17, :], tot[1:2, 0:AW], tot[1:2, AW:], tot[8:10, 0:HW], tot[0:1, :]]
        outs[0][...] = tot[2:3, 0:1]
        for p in range(5):
            w_r, m_r, v_r = ins[3 * p:3 * p + 3]
            g = grads[p]
            d, mm, vv = _adamw_math(w_r[...], g, m_r[...], v_r[...])
            outs[1 + 4 * p][...] = g
            outs[2 + 4 * p][...] = d
            outs[3 + 4 * p][...] = mm
            outs[4 + 4 * p][...] = vv

    flat = [a for p in params for a in p]
    shapes = [jax.ShapeDtypeStruct((1, 1), F32)]
    for p in params:
        shapes += [jax.ShapeDtypeStruct(p[0].shape, F32)] * 4
    vm = pl.BlockSpec(memory_space=pltpu.VMEM)
    return pl.pallas_call(
        body, name="adamw_small",
        in_specs=[vm] * 16, out_specs=[vm] * 21, out_shape=shapes,
        compiler_params=_cp(),
    )(sall, *flat)


def kernel(x, positions, w_in, w_out, mix_norm_w, attn_out_norm_w, hgrn_out_norm_w, hgrn_lb_raw, final_norm_w, loss_target, m_w_in, m_w_out, m_mix_norm_w, m_attn_out_norm_w, m_hgrn_out_norm_w, m_hgrn_lb_raw, m_final_norm_w, v_w_in, v_w_out, v_mix_norm_w, v_attn_out_norm_w, v_hgrn_out_norm_w, v_hgrn_lb_raw, v_final_norm_w):
    xs = x.reshape(T, D)
    tgt = loss_target.reshape(T, D)
    pos = positions.reshape(T, 1)
    fnw = final_norm_w.reshape(1, D)

    ti = np.arange(TH)
    tri_np = ((ti[:, None] // CHUNK == ti[None, :] // CHUNK) & (ti[None, :] <= ti[:, None])).astype(np.float32)
    tri = jnp.asarray(tri_np, BF16)
    trit = jnp.asarray(tri_np.T, BF16)
    hi_ = np.arange(AW) // HEAD
    gmat = jnp.asarray((hi_[:256, None] == hi_[None, :256]).astype(np.float32) / HEAD, BF16)
    emat_np = (np.arange(128)[:, None] == hi_[None, :]).astype(np.float32)
    sel_np = np.zeros((2, AW, 128), np.float32)
    sel_np[0, np.arange(8) * HEAD, np.arange(8)] = 1.0
    sel_np[1, np.arange(8) * HEAD, 8 + np.arange(8)] = 1.0
    emat = jnp.asarray(emat_np, BF16)
    selmat = jnp.asarray(sel_np, BF16)

    wb_in, wb_out = _cast_weights(w_in.reshape(D, 1024), w_out.reshape(256, D))
    jm_arr = (2 * lax.axis_index("x") + lax.axis_index("y")).astype(jnp.int32).reshape(1)
    (hn, q1, k1, v1, q4, k4, v4, q16, k16, v16, ag, hq, hf, hi, hg, w_full, wout4) = _fwd_in(
        xs, pos, mix_norm_w, wb_in, wb_out, jm_arr)
    wout_full = wout4.reshape(D, D)
    flat = lambda a: a.reshape(T, AW)
    o1, l1 = _attn_fwd(q1, k1, v1, T // BLK, "attn_fwd_d1")
    o4, l4 = _attn_fwd(flat(q4), flat(k4), flat(v4), T // 4 // BLK, "attn_fwd_d4")
    o16, l16 = _attn_fwd(flat(q16), flat(k16), flat(v16), T // 16 // BLK, "attn_fwd_d16")
    rec, sall = _hgrn_fwd(hq, hf, hi, hgrn_lb_raw, tri)

    (dx2, do1, do4, do16, st1, st4, st16, drec, dag, dhg, rout, routb, small4) = _fwd_out(
        o1, o4.reshape(4, T // 4, AW), o16.reshape(16, T // 16, AW),
        l1, l4.reshape(4, T // 4, 128), l16.reshape(16, T // 16, 128),
        rec, ag, hg, xs, tgt, attn_out_norm_w, hgrn_out_norm_w, fnw, wout_full, gmat, emat, selmat)

    fst = lambda a: a.reshape(T, 128)
    dq1, dk1, dv1 = _attn_bwd(q1, k1, v1, do1, st1, T // BLK, "attn_bwd_d1")
    dq4, dk4, dv4 = _attn_bwd(flat(q4), flat(k4), flat(v4), flat(do4), fst(st4), T // 4 // BLK, "attn_bwd_d4")
    dq16, dk16, dv16 = _attn_bwd(flat(q16), flat(k16), flat(v16), flat(do16), fst(st16), T // 16 // BLK,
                                 "attn_bwd_d16")
    dproj_h, small6, pout_own, pout_rem = _hgrn_bwd(hq, hf, hi, hgrn_lb_raw, tri, trit, drec, sall, dhg,
                                                    rout, routb)

    r4 = lambda a: a.reshape(4, T // 4, AW)
    r16 = lambda a: a.reshape(16, T // 16, AW)
    dproj_a = _dproj_build((dq1, r4(dq4), r16(dq16)), (dk1, r4(dk4), r16(dk16)), (dv1, r4(dv4), r16(dv16)),
                           dag, pos)
    rin, rinb = _grad_w_in(hn, dproj_a, dproj_h)
    gx, pin_own, pin_rem, small_all = _bwd_x(dproj_a, dproj_h, xs, dx2, mix_norm_w, w_full, rin, rinb,
                                             small4, small6)
    fin, fout = _pair_share(pin_own, pin_rem, pout_own, pout_rem)
    g_w_in = fin.reshape(D, 1024)
    g_w_out = fout.reshape(256, D)

    d_in, nm_in, nv_in = _adamw(w_in.reshape(D, 1024), g_w_in, m_w_in.reshape(D, 1024), v_w_in.reshape(D, 1024),
                                "adamw_w_in")
    d_out, nm_out, nv_out = _adamw(w_out.reshape(256, D), g_w_out, m_w_out.reshape(256, D), v_w_out.reshape(256, D),
                                   "adamw_w_out")
    params = [(mix_norm_w, m_mix_norm_w, v_mix_norm_w),
              (attn_out_norm_w, m_attn_out_norm_w, v_attn_out_norm_w),
              (hgrn_out_norm_w, m_hgrn_out_norm_w, v_hgrn_out_norm_w),
              (hgrn_lb_raw, m_hgrn_lb_raw, v_hgrn_lb_raw),
              (fnw, m_final_norm_w.reshape(1, D), v_final_norm_w.reshape(1, D))]
    so = _adamw_small(small_all, params)
    loss = so[0].reshape(())
    g_s = [so[1 + 4 * p] for p in range(5)]
    d_s = [so[2 + 4 * p] for p in range(5)]
    m_s = [so[3 + 4 * p] for p in range(5)]
    v_s = [so[4 + 4 * p] for p in range(5)]
    for lst in (g_s, d_s, m_s, v_s):
        lst[4] = lst[4].reshape(D)

    return (loss, gx.reshape(1, T, D),
            g_w_in.reshape(1, D, 1024), g_w_out.reshape(1, 256, D), *g_s,
            d_in.reshape(1, D, 1024), d_out.reshape(1, 256, D), *d_s,
            nm_in.reshape(1, D, 1024), nm_out.reshape(1, 256, D), *m_s,
            nv_in.reshape(1, D, 1024), nv_out.reshape(1, 256, D), *v_s)
```

```python
import functools

import numpy as np
import jax
import jax.numpy as jnp
from jax import lax
from jax.experimental import pallas as pl
from jax.experimental.pallas import tpu as pltpu

F32 = jnp.float32
BF16 = jnp.bfloat16

T = 4096
D = 1024
AW = 512
HW = 512
NCOL = 4096
HEAD = 64
BLK = 128
CHUNK = 64
EPS = 1e-6
SCALE = HEAD ** -0.5
NEG = -1e30
ROPE_THETA = 500000.0
INV_FREQ = [float(v) for v in
            (np.float32(ROPE_THETA) ** (-(np.arange(8, dtype=np.float32)) * np.float32(0.125)))]
LR, B1, B2, AEPS, WD, STEP = 0.001, 0.9, 0.999, 1e-08, 0.01, 10
VMEM_LIMIT = 56 * 1024 * 1024
MESH = pl.DeviceIdType.MESH


def _cp(sem=None, **kw):
    return pltpu.CompilerParams(dimension_semantics=sem, vmem_limit_bytes=VMEM_LIMIT, **kw)


def _mm(a, b):
    return jnp.dot(a, b, preferred_element_type=F32)


def _mm_nt(a, b):
    return lax.dot_general(a, b, (((1,), (1,)), ((), ())), preferred_element_type=F32)


def _mm_tn(a, b):
    return lax.dot_general(a, b, (((0,), (0,)), ((), ())), preferred_element_type=F32)


def _split3(x):
    h = x.astype(BF16)
    r = x - h.astype(F32)
    m = r.astype(BF16)
    l = (r - m.astype(F32)).astype(BF16)
    return h, m, l


def _mm_exact_l(mat_bf, x):
    h, m, l = _split3(x)
    return _mm(mat_bf, h) + _mm(mat_bf, m) + _mm(mat_bf, l)


def _mm_exact_r(x, mat_bf):
    h = x.astype(BF16)
    l = (x - h.astype(F32)).astype(BF16)
    return _mm(h, mat_bf) + _mm(l, mat_bf)


def _sigmoid(x):
    return 0.5 * jnp.tanh(0.5 * x) + 0.5


def _rope_tables(pos):
    lane = lax.broadcasted_iota(jnp.int32, (1, 128), 1)
    jl = lane & 63
    fi = jl & 7
    inv = jnp.zeros((1, 128), F32)
    for kk in range(8):
        inv = jnp.where(fi == kk, INV_FREQ[kk], inv)
    ang = pos.astype(F32) * inv
    c = jnp.cos(ang)
    s = jnp.sin(ang)
    cosf = jnp.where(jl < 16, c, 1.0)
    s1 = jnp.where(jl < 8, -s, 0.0)
    s2 = jnp.where((jl >= 8) & (jl < 16), s, 0.0)
    return cosf, s1, s2


def _rope(t, cosf, s1, s2):
    parts = []
    for ci in range(t.shape[1] // 128):
        tc = t[:, ci * 128:(ci + 1) * 128]
        parts.append(tc * cosf + pltpu.roll(tc, 120, 1) * s1 + pltpu.roll(tc, 8, 1) * s2)
    return jnp.concatenate(parts, axis=1)


def _rope_bwd(g, cosf, s1, s2):
    parts = []
    for ci in range(g.shape[1] // 128):
        gc = g[:, ci * 128:(ci + 1) * 128]
        parts.append(gc * cosf + pltpu.roll(gc * s1, 8, 1) + pltpu.roll(gc * s2, 120, 1))
    return jnp.concatenate(parts, axis=1)


def _perm_store(val, scr, o1, o4, o16, dt):
    n = val.shape[0]
    o1[...] = val.astype(dt)
    for ci in range(val.shape[1] // 128):
        cs = slice(ci * 128, (ci + 1) * 128)
        scr[ci] = val[:, cs]
        for rr in range(4):
            o4[rr, :, cs] = scr[ci, pl.ds(rr, n // 4, stride=4), :].astype(dt)
        for rr in range(16):
            o16[rr, :, cs] = scr[ci, pl.ds(rr, n // 16, stride=16), :].astype(dt)


def _unperm_load(r4, r16, scr_a, scr_b):
    n = scr_a.shape[1]
    nc = r4.shape[-1] // 128
    for ci in range(nc):
        cs = slice(ci * 128, (ci + 1) * 128)
        for rr in range(4):
            scr_a[ci, pl.ds(rr, n // 4, stride=4), :] = r4[rr, :, cs].astype(F32)
        for rr in range(16):
            scr_b[ci, pl.ds(rr, n // 16, stride=16), :] = r16[rr, :, cs].astype(F32)
    return (jnp.concatenate([scr_a[ci] for ci in range(nc)], axis=1),
            jnp.concatenate([scr_b[ci] for ci in range(nc)], axis=1))


def _cast_weights(w_in, w_out):
    def body(win_ref, wout_ref, bin_ref, bout_ref):
        bin_ref[...] = win_ref[...].astype(BF16)

        @pl.when(pl.program_id(0) == 0)
        def _():
            bout_ref[...] = wout_ref[...].astype(BF16)

    return pl.pallas_call(
        body, name="cast_weights", grid=(4,),
        in_specs=[pl.BlockSpec((256, 1024), lambda i: (i, 0)), pl.BlockSpec((256, D), lambda i: (0, 0))],
        out_specs=[pl.BlockSpec((256, 1024), lambda i: (i, 0)), pl.BlockSpec((256, D), lambda i: (0, 0))],
        out_shape=(jax.ShapeDtypeStruct((D, 1024), BF16), jax.ShapeDtypeStruct((256, D), BF16)),
        compiler_params=_cp(("arbitrary",)),
    )(w_in, w_out)


def _fwd_in(x, pos, mixw, wb_in, wb_out, jm_arr):
    TT = 512
    NT = T // TT

    def body(jm_ref, x_ref, pos_ref, mw_ref, wbin_ref, wbout_ref,
             hnt_ref, q1, k1, v1, q4, k4, v4, q16, k16, v16, ag, hq, hf, hi, hg, wfull_o, woutfull_o,
             wbuf, wobuf, hn_all, scr, send_sems, recv_sems, loc_sems):
        s = pl.program_id(0)
        i = pl.program_id(1)
        mx, my, c = lax.axis_index("x"), lax.axis_index("y"), lax.axis_index("c")
        me, sibling = (mx, my, c), (mx, my, 1 - c)
        chips = [(mx, 1 - my), (1 - mx, my), (1 - mx, 1 - my)]
        jm = 2 * mx + my
        rows_in = [pl.ds(pl.multiple_of(h * 512, 512), 512) for h in (c, 1 - c)]
        rows_out = [pl.ds(pl.multiple_of(h * 128, 128), 128) for h in (c, 1 - c)]

        def blk(k):
            return lax.bitwise_xor(jm, k + 1)

        def rc(n, ref, to):
            return pltpu.make_async_remote_copy(src_ref=ref, dst_ref=ref, send_sem=send_sems.at[n],
                                                recv_sem=recv_sems.at[n], device_id=to, device_id_type=MESH)

        send_in = lambda k: rc(k, wbuf.at[jm, rows_in[0], :], (*chips[k], c))
        send_out = lambda k: rc(3 + k, wobuf.at[jm, rows_out[0], :], (*chips[k], c))
        got_in = lambda k: rc(k, wbuf.at[blk(k), rows_in[0], :], me)
        got_out = lambda k: rc(3 + k, wobuf.at[blk(k), rows_out[0], :], me)
        pass_in = lambda k: rc(6 + k, wbuf.at[blk(k), rows_in[0], :], sibling)
        pass_out = lambda k: rc(9 + k, wobuf.at[blk(k), rows_out[0], :], sibling)
        passed_in = lambda k: rc(6 + k, wbuf.at[blk(k), rows_in[1], :], me)
        passed_out = lambda k: rc(9 + k, wobuf.at[blk(k), rows_out[1], :], me)

        def keep(j, n):
            return pltpu.make_async_copy(wbuf.at[j], wfull_o.at[:, pl.ds(j * 1024, 1024)], loc_sems.at[n])

        @pl.when((s == 0) & (i == 0))
        def _():
            own = [pltpu.make_async_copy(wbin_ref, wbuf.at[jm], loc_sems.at[4]),
                   pltpu.make_async_copy(wbout_ref, wobuf.at[jm], loc_sems.at[5])]
            for cp in own:
                cp.start()
            for cp in own:
                cp.wait()
            send_in(0).start()
            send_in(1).start()
            keep(jm, 0).start()

        def arrive(k):
            if k == 0:
                send_in(0).wait_send()
                send_in(1).wait_send()
                send_in(2).start()
            got_in(k).wait_recv()
            pass_in(k).start()
            passed_in(k).wait_recv()
            keep(blk(k), k + 1).start()
            if k == 2:
                for kk in range(3):
                    send_out(kk).start()

        for k in range(3):
            pl.when((s == k + 1) & (i == 0))(functools.partial(arrive, k))

        tile = pl.ds(pl.multiple_of(i * TT, TT), TT)

        @pl.when(s == 0)
        def _():
            xv = x_ref[...]
            r = lax.rsqrt(jnp.mean(xv * xv, axis=-1, keepdims=True) + EPS)
            hnf = (xv * r) * mw_ref[...]
            hn_all[tile, :] = hnf.astype(BF16)
            hnt_ref[...] = hnf.T.astype(BF16)

        def project(jj):
            hn = hn_all[tile, :]
            lo = _mm(hn, wbuf[jj, :, 0:512])
            hi_cols = _mm(hn, wbuf[jj, :, 512:1024])
            if jj == 0:
                cosf, s1, s2 = _rope_tables(pos_ref[...])
                _perm_store(_rope(lo, cosf, s1, s2), scr, q1, q4, q16, BF16)
                _perm_store(_rope(hi_cols, cosf, s1, s2), scr, k1, k4, k16, BF16)
            elif jj == 1:
                _perm_store(lo, scr, v1, v4, v16, BF16)
                ag[...] = hi_cols
            elif jj == 2:
                hq[...] = lo
                hf[...] = hi_cols
            else:
                hi[...] = lo.astype(BF16)
                hg[...] = hi_cols

        j = lax.bitwise_xor(jm, s)
        for jj in range(4):
            pl.when(j == jj)(functools.partial(project, jj))

        @pl.when((s == 3) & (i == NT - 1))
        def _():
            for k in range(3):
                got_out(k).wait_recv()
                pass_out(k).start()
            for k in range(3):
                passed_out(k).wait_recv()
            out = pltpu.make_async_copy(wobuf, woutfull_o, loc_sems.at[4])
            out.start()
            send_in(2).wait_send()
            for k in range(3):
                send_out(k).wait_send()
                pass_in(k).wait_send()
                pass_out(k).wait_send()
            keep(jm, 0).wait()
            for k in range(3):
                keep(blk(k), k + 1).wait()
            out.wait()

    def at_stage_of(jb):
        def index(s, i, jm_ref):
            sa = lax.bitwise_xor(jm_ref[0], jb)
            return jnp.where(s < sa, 0, jnp.where(s == sa, i, NT - 1))
        return index

    tok = lambda w, jb: pl.BlockSpec((TT, w), lambda s, i, jm_ref: (at_stage_of(jb)(s, i, jm_ref), 0))
    d4 = lambda jb: pl.BlockSpec((4, TT // 4, AW), lambda s, i, jm_ref: (0, at_stage_of(jb)(s, i, jm_ref), 0))
    d16 = lambda jb: pl.BlockSpec((16, TT // 16, AW), lambda s, i, jm_ref: (0, at_stage_of(jb)(s, i, jm_ref), 0))
    hbm = pl.BlockSpec(memory_space=pltpu.HBM)
    sd = lambda shape, dt: jax.ShapeDtypeStruct(shape, dt)
    in_own_stage = lambda s, i: jnp.where(s == 0, i, NT - 1)
    grid_spec = pltpu.PrefetchScalarGridSpec(
        num_scalar_prefetch=1, grid=(4, NT),
        in_specs=[pl.BlockSpec((TT, D), lambda s, i, jm_ref: (in_own_stage(s, i), 0)),
                  pl.BlockSpec((TT, 1), lambda s, i, jm_ref: (i, 0)),
                  pl.BlockSpec((1, D), lambda s, i, jm_ref: (0, 0)), hbm, hbm],
        out_specs=[pl.BlockSpec((D, TT), lambda s, i, jm_ref: (0, in_own_stage(s, i))),
                   tok(AW, 0), tok(AW, 0), tok(AW, 1), d4(0), d4(0), d4(1), d16(0), d16(0), d16(1),
                   tok(AW, 1), tok(AW, 2), tok(AW, 2), tok(AW, 3), tok(AW, 3), hbm, hbm],
        scratch_shapes=[pltpu.VMEM((4, D, 1024), BF16), pltpu.VMEM((4, 256, D), BF16), pltpu.VMEM((T, D), BF16),
                        pltpu.VMEM((4, TT, 128), F32), pltpu.SemaphoreType.DMA((12,)),
                        pltpu.SemaphoreType.DMA((12,)), pltpu.SemaphoreType.DMA((6,))])
    return pl.pallas_call(
        body, name="fwd_in", grid_spec=grid_spec,
        out_shape=[sd((D, T), BF16)] + [sd((T, AW), BF16)] * 3 + [sd((4, T // 4, AW), BF16)] * 3
        + [sd((16, T // 16, AW), BF16)] * 3
        + [sd((T, AW), F32), sd((T, AW), F32), sd((T, AW), F32), sd((T, AW), BF16), sd((T, AW), F32),
           sd((D, NCOL), BF16), sd((4, 256, D), BF16)],
        compiler_params=_cp(("arbitrary", "arbitrary")),
    )(jm_arr, x, pos, mixw, wb_in, wb_out)


def _band_mask(key_axis, nkeys=2 * BLK):
    shape = (nkeys, 2 * BLK) if key_axis == 0 else (2 * BLK, nkeys)
    kj = lax.broadcasted_iota(jnp.int32, shape, key_axis)
    qi = lax.broadcasted_iota(jnp.int32, shape, 1 - key_axis) & (BLK - 1)
    return (kj >= qi) & (kj <= qi + BLK), kj, qi


def _stack_heads(t2, in_a):
    z = jnp.zeros_like(t2)
    return jnp.concatenate([jnp.where(in_a[0], t2, z), jnp.where(in_a[1], t2, z)], axis=0)


def _attn_fwd(q, k, v, nb, name):
    n = min(4, nb)
    CH = n * BLK
    halo = nb > n

    def body(*refs):
        if halo:
            q_ref, k_ref, v_ref, kp_ref, vp_ref, o_ref, lse_ref = refs
        else:
            q_ref, k_ref, v_ref, o_ref, lse_ref = refs
        lane = lax.broadcasted_iota(jnp.int32, (1, 128), 1)
        in_a = [lane < HEAD, lane >= HEAD]
        band, kj, _ = _band_mask(1)
        thr0 = jnp.where((n * pl.program_id(0)) % nb == 0, BLK, 0) if halo else BLK
        mask0 = band & (kj >= thr0)
        for b in range(n):
            rs = slice(b * BLK, (b + 1) * BLK)
            stat = jnp.zeros((BLK, 128), F32)
            for hp in range(4):
                cs = slice(hp * 128, (hp + 1) * 128)
                q2s = _stack_heads(q_ref[rs, cs], in_a)
                if b == 0:
                    kprev = kp_ref[:, cs] if halo else k_ref[rs, cs]
                    vprev = vp_ref[:, cs] if halo else v_ref[rs, cs]
                    kk = jnp.concatenate([kprev, k_ref[rs, cs]], axis=0)
                    vv = jnp.concatenate([vprev, v_ref[rs, cs]], axis=0)
                    mask = mask0
                else:
                    kk = k_ref[(b - 1) * BLK:(b + 1) * BLK, cs]
                    vv = v_ref[(b - 1) * BLK:(b + 1) * BLK, cs]
                    mask = band
                s = jnp.where(mask, _mm_nt(q2s, kk) * SCALE, NEG)
                m = jnp.max(s, axis=-1, keepdims=True)
                p = jnp.exp(s - m)
                l = jnp.sum(p, axis=-1, keepdims=True)
                o = _mm(p.astype(BF16), vv) / l
                lse = m + jnp.log(l)
                o_ref[rs, cs] = jnp.where(in_a[0], o[:BLK], o[BLK:])
                stat = jnp.where(lane == 2 * hp, lse[:BLK], stat)
                stat = jnp.where(lane == 2 * hp + 1, lse[BLK:], stat)
            lse_ref[rs, :] = stat

    cur = pl.BlockSpec((CH, AW), lambda i: (i, 0))
    prev = pl.BlockSpec((BLK, AW), lambda i: (jnp.maximum(n * i - 1, 0), 0))
    return pl.pallas_call(
        body, name=name, grid=(T // CH,),
        in_specs=[cur, cur, cur] + ([prev, prev] if halo else []),
        out_specs=[cur, pl.BlockSpec((CH, 128), lambda i: (i, 0))],
        out_shape=[jax.ShapeDtypeStruct((T, AW), F32), jax.ShapeDtypeStruct((T, 128), F32)],
        compiler_params=_cp(("parallel",)),
    )(*((q, k, v) + ((k, v) if halo else ())))


def _attn_bwd(q, k, v, do, st, nb, name):
    n = min(4, nb)
    CH = n * BLK
    NBLK = T // BLK
    halo = nb > n

    def body(*refs):
        if halo:
            (q_ref, k_ref, v_ref, do_ref, st_ref, kp_ref, vp_ref, qn_ref, don_ref, stn_ref,
             dq_ref, dk_ref, dv_ref) = refs
        else:
            q_ref, k_ref, v_ref, do_ref, st_ref, dq_ref, dk_ref, dv_ref = refs
        i = pl.program_id(0)
        lane = lax.broadcasted_iota(jnp.int32, (1, 128), 1)
        in_a = [lane < HEAD, lane >= HEAD]
        band, kj, _ = _band_mask(0)
        thr0 = jnp.where((n * i) % nb == 0, BLK, 0) if halo else BLK
        mask0 = band & (kj >= thr0)

        def stat_rows(st_t, hp):
            lse_r = jnp.concatenate([st_t[2 * hp:2 * hp + 1, :], st_t[2 * hp + 1:2 * hp + 2, :]], axis=1)
            dl_r = jnp.concatenate([st_t[8 + 2 * hp:9 + 2 * hp, :], st_t[9 + 2 * hp:10 + 2 * hp, :]], axis=1)
            return lse_r, dl_r

        st_t = [st_ref[b * BLK:(b + 1) * BLK, :].T for b in range(n)]
        if halo:
            nxt_thr = jnp.where((n * i + n) % nb == 0, 2 * BLK, 0)
            _, kj1, qi1 = _band_mask(0, BLK)
            mask_next = kj1 >= qi1 + nxt_thr
            stn_t = stn_ref[...].T

        for hp in range(4):
            cs = slice(hp * 128, (hp + 1) * 128)
            kb = [k_ref[b * BLK:(b + 1) * BLK, cs] for b in range(n)]
            vb = [v_ref[b * BLK:(b + 1) * BLK, cs] for b in range(n)]
            dk_acc = [jnp.zeros((BLK, 128), F32) for _ in range(n)]
            dv_acc = [jnp.zeros((BLK, 128), F32) for _ in range(n)]
            for b in range(n):
                rs = slice(b * BLK, (b + 1) * BLK)
                q2s = _stack_heads(q_ref[rs, cs], in_a)
                do2s = _stack_heads(do_ref[rs, cs], in_a)
                if b == 0:
                    kprev = kp_ref[:, cs] if halo else kb[0]
                    vprev = vp_ref[:, cs] if halo else vb[0]
                    mask = mask0
                else:
                    kprev, vprev, mask = kb[b - 1], vb[b - 1], band
                kk = jnp.concatenate([kprev, kb[b]], axis=0)
                vv = jnp.concatenate([vprev, vb[b]], axis=0)
                lse_r, dl_r = stat_rows(st_t[b], hp)
                s_t = jnp.where(mask, _mm_nt(kk, q2s) * SCALE, NEG)
                p_t = jnp.exp(s_t - lse_r)
                ds_t = (p_t * (_mm_nt(vv, do2s) - dl_r)).astype(BF16)
                dkk = _mm(ds_t, q2s) * SCALE
                dvv = _mm(p_t.astype(BF16), do2s)
                dqs = _mm_tn(ds_t, kk) * SCALE
                dq_ref[rs, cs] = jnp.where(in_a[0], dqs[:BLK], dqs[BLK:]).astype(BF16)
                dk_acc[b] += dkk[BLK:]
                dv_acc[b] += dvv[BLK:]
                if b > 0:
                    dk_acc[b - 1] += dkk[:BLK]
                    dv_acc[b - 1] += dvv[:BLK]
            if halo:
                q2s = _stack_heads(qn_ref[:, cs], in_a)
                do2s = _stack_heads(don_ref[:, cs], in_a)
                lse_r, dl_r = stat_rows(stn_t, hp)
                s_t = jnp.where(mask_next, _mm_nt(kb[n - 1], q2s) * SCALE, NEG)
                p_t = jnp.exp(s_t - lse_r)
                ds_t = (p_t * (_mm_nt(vb[n - 1], do2s) - dl_r)).astype(BF16)
                dk_acc[n - 1] += _mm(ds_t, q2s) * SCALE
                dv_acc[n - 1] += _mm(p_t.astype(BF16), do2s)
            for b in range(n):
                dk_ref[b * BLK:(b + 1) * BLK, cs] = dk_acc[b].astype(BF16)
                dv_ref[b * BLK:(b + 1) * BLK, cs] = dv_acc[b].astype(BF16)

    cur = pl.BlockSpec((CH, AW), lambda i: (i, 0))
    cur_st = pl.BlockSpec((CH, 128), lambda i: (i, 0))
    prev = pl.BlockSpec((BLK, AW), lambda i: (jnp.maximum(n * i - 1, 0), 0))
    nxt = pl.BlockSpec((BLK, AW), lambda i: (jnp.minimum(n * i + n, NBLK - 1), 0))
    nxt_st = pl.BlockSpec((BLK, 128), lambda i: (jnp.minimum(n * i + n, NBLK - 1), 0))
    ins = [cur] * 4 + [cur_st] + ([prev, prev, nxt, nxt, nxt_st] if halo else [])
    args = (q, k, v, do, st) + ((k, v, q, do, st) if halo else ())
    return pl.pallas_call(
        body, name=name, grid=(T // CH,),
        in_specs=ins,
        out_specs=[cur] * 3,
        out_shape=[jax.ShapeDtypeStruct((T, AW), BF16)] * 3,
        compiler_params=_cp(("parallel",)),
    )(*args)


TH = 256
NCH = TH // CHUNK


def _hgrn_common(hq_ref, hf_ref, lbr_ref, tri_ref):
    r0 = lbr_ref[0:1, :]
    r1 = lbr_ref[1:2, :]
    mx = jnp.maximum(r0, r1)
    e0 = jnp.exp(r0 - mx)
    e1 = jnp.exp(r1 - mx)
    lb = e0 / (e0 + e1)
    hqv = hq_ref[...]
    sq = _sigmoid(hqv)
    qv = hqv * sq
    sf = _sigmoid(hf_ref[...])
    f = lb + (1.0 - lb) * sf
    kv = 1.0 - f
    g = jnp.log(f)
    cum = _mm_exact_l(tri_ref[...], g)
    lastb = jnp.concatenate(
        [jnp.broadcast_to(cum[c * CHUNK + CHUNK - 1:(c + 1) * CHUNK, :], (CHUNK, HW)) for c in range(NCH)], axis=0)
    ea = jnp.exp(cum)
    ena = jnp.exp(-cum)
    eend = jnp.exp(lastb - cum)
    return dict(lb=lb, hq=hqv, sq=sq, q=qv, sf=sf, f=f, k=kv, cum=cum, lastb=lastb, ea=ea, ena=ena, eend=eend,
                qd=qv * ea, ki=kv * ena, ke=kv * eend, dec=jnp.exp(lastb))


def _tri_mask(transposed=False):
    ti = lax.broadcasted_iota(jnp.int32, (TH, TH), 1 if transposed else 0)
    si = lax.broadcasted_iota(jnp.int32, (TH, TH), 0 if transposed else 1)
    return (si <= ti) & ((si // CHUNK) == (ti // CHUNK))


def _hgrn_fwd(hq, hf, hi, lbr, tri):
    def body(hq_ref, hf_ref, hi_ref, lbr_ref, tri_ref, rec_ref, sall_ref, st_scr):
        @pl.when(pl.program_id(0) == 0)
        def _():
            st_scr[...] = jnp.zeros_like(st_scr)

        w = _hgrn_common(hq_ref, hf_ref, lbr_ref, tri_ref)
        qd, ki, ke = w["qd"].astype(BF16), w["ki"].astype(BF16), w["ke"].astype(BF16)
        dec = w["dec"]
        vb = hi_ref[...]
        causal = _tri_mask()
        for h in range(4):
            cs = slice(h * 128, (h + 1) * 128)
            att = jnp.where(causal, _mm_nt(qd[:, cs], ki[:, cs]), 0.0)
            o_intra = _mm(att.astype(BF16), vb[:, cs])
            for c in range(NCH):
                rs = slice(c * CHUNK, (c + 1) * CHUNK)
                st = st_scr[:, cs]
                sall_ref[c, :, cs] = st
                rec_ref[rs, cs] = o_intra[rs] + _mm_nt(qd[rs, cs], st.astype(BF16))
                st_scr[:, cs] = dec[c * CHUNK:c * CHUNK + 1, cs] * st + _mm_tn(vb[rs, cs], ke[rs, cs])

    tok = pl.BlockSpec((TH, HW), lambda i: (i, 0))
    return pl.pallas_call(
        body, name="hgrn_fwd", grid=(T // TH,),
        in_specs=[tok, tok, tok, pl.BlockSpec((2, HW), lambda i: (0, 0)), pl.BlockSpec((TH, TH), lambda i: (0, 0))],
        out_specs=[tok, pl.BlockSpec((NCH, 128, HW), lambda i: (i, 0, 0))],
        out_shape=[jax.ShapeDtypeStruct((T, HW), F32), jax.ShapeDtypeStruct((T // CHUNK, 128, HW), F32)],
        scratch_shapes=[pltpu.VMEM((128, HW), F32)],
        compiler_params=_cp(("arbitrary",)),
    )(hq, hf, hi, lbr, tri)


def _hgrn_bwd(hq, hf, hi, lbr, tri, trit, drec, sall, dhg, rout, routb, rin_a, rinb_a):
    NT = T // TH

    def body(hq_ref, hf_ref, hi_ref, lbr_ref, tri_ref, trit_ref, do_ref, sall_ref, dhg_ref, rout_r, routb_r,
             rina_r, rinba_r, dph_ref, small_ref, pout_o, poutr_o, pin_o, pinr_o,
             dst_scr, dlb_scr, dqd_scr, dki_scr, dke_scr, dlast_scr, send_sems, recv_sems, loc_sems,
             send2_sems, recv2_sems):
        step = pl.program_id(0)
        loc, rem = _chip_copies(_w_out_piece, rout_r, routb_r, pout_o, poutr_o, send_sems, recv_sems,
                                loc_sems.at[0])
        start_half, wait_half = _half_exchange(0, rina_r, rinba_r, pin_o, pinr_o, send2_sems, recv2_sems,
                                               loc_sems.at[1])

        @pl.when(step == 0)
        def _():
            dst_scr[...] = jnp.zeros_like(dst_scr)
            dlb_scr[...] = jnp.zeros_like(dlb_scr)
            for cp in loc + rem:
                cp.start()
            start_half()

        w = _hgrn_common(hq_ref, hf_ref, lbr_ref, tri_ref)
        qd, ki, ke = w["qd"].astype(BF16), w["ki"].astype(BF16), w["ke"].astype(BF16)
        dec = w["dec"]
        vb = hi_ref[...]
        dob = do_ref[...].astype(BF16)
        causal = _tri_mask()
        causal_t = _tri_mask(transposed=True)
        for h in range(4):
            cs = slice(h * 128, (h + 1) * 128)
            att_t = jnp.where(causal_t, _mm_nt(ki[:, cs], qd[:, cs]), 0.0).astype(BF16)
            datt_t = jnp.where(causal_t, _mm_nt(vb[:, cs], dob[:, cs]), 0.0).astype(BF16)
            datt = jnp.where(causal, _mm_nt(dob[:, cs], vb[:, cs]), 0.0).astype(BF16)
            dv_intra = _mm(att_t, dob[:, cs])
            dqd_intra = _mm(datt, ki[:, cs])
            dki_scr[:, cs] = _mm(datt_t, qd[:, cs])
            for c in reversed(range(NCH)):
                rs = slice(c * CHUNK, (c + 1) * CHUNK)
                dec_c = dec[c * CHUNK:c * CHUNK + 1, :]
                st = sall_ref[c, :, cs]
                dst = dst_scr[:, cs]
                dstb = dst.astype(BF16)
                dph_ref[rs, 2 * HW + h * 128:2 * HW + (h + 1) * 128] = (
                    dv_intra[rs] + _mm_nt(ke[rs, cs], dstb)).astype(BF16)
                dqd_scr[rs, cs] = dqd_intra[rs] + _mm(dob[rs, cs], st.astype(BF16))
                dke_scr[rs, cs] = _mm(vb[rs, cs], dstb)
                ddec = jnp.sum(dst * st, axis=0, keepdims=True)
                dlast_scr[c:c + 1, cs] = ddec * dec_c[:, cs]
                dst_scr[:, cs] = dec_c[:, cs] * dst + _mm_tn(dob[rs, cs], qd[rs, cs])
        dqd, dki, dke = dqd_scr[...], dki_scr[...], dke_scr[...]
        dq = dqd * w["ea"]
        dk = dki * w["ena"] + dke * w["eend"]
        dcum = dqd * w["qd"] - dki * w["ki"] - dke * w["ke"]
        dkeke = dke * w["ke"]
        dlastb = jnp.concatenate(
            [jnp.broadcast_to(dlast_scr[c:c + 1, :] + jnp.sum(dkeke[c * CHUNK:(c + 1) * CHUNK], axis=0, keepdims=True),
                              (CHUNK, HW)) for c in range(NCH)], axis=0)
        dg = _mm_exact_l(trit_ref[...], dcum) + dlastb
        df = dg / w["f"] - dk
        lb, sf, sq = w["lb"], w["sf"], w["sq"]
        dph_ref[:, HW:2 * HW] = (df * (1.0 - lb) * sf * (1.0 - sf)).astype(BF16)
        dph_ref[:, 0:HW] = (dq * (sq * (1.0 + w["hq"] * (1.0 - sq)))).astype(BF16)
        dph_ref[:, 3 * HW:4 * HW] = dhg_ref[...]
        dlb_scr[...] += jnp.sum(df * (1.0 - sf), axis=0, keepdims=True)

        @pl.when(step == NT - 1)
        def _():
            gr = dlb_scr[...] * lb * (1.0 - lb)
            small_ref[...] = jnp.zeros_like(small_ref)
            small_ref[0:1, 0:HW] = gr
            small_ref[1:2, 0:HW] = -gr
            for cp in rem:
                cp.wait_recv()
            for cp in rem:
                cp.wait_send()
            for cp in loc:
                cp.wait()
            wait_half()

    tok = pl.BlockSpec((TH, HW), lambda i: (NT - 1 - i, 0))
    const = lambda shape: pl.BlockSpec(shape, lambda i: (0,) * len(shape))
    hbm = pl.BlockSpec(memory_space=pltpu.HBM)
    return pl.pallas_call(
        body, name="hgrn_bwd", grid=(NT,),
        in_specs=[tok, tok, tok, const((2, HW)), const((TH, TH)), const((TH, TH)), tok,
                  pl.BlockSpec((NCH, 128, HW), lambda i: (NT - 1 - i, 0, 0)), tok, hbm, hbm, hbm, hbm],
        out_specs=[pl.BlockSpec((TH, NCOL // 2), lambda i: (NT - 1 - i, 0)), const((8, D)), hbm, hbm, hbm, hbm],
        out_shape=[jax.ShapeDtypeStruct((T, NCOL // 2), BF16), jax.ShapeDtypeStruct((8, D), F32),
                   jax.ShapeDtypeStruct((128, D), F32), jax.ShapeDtypeStruct((3, 128, D), BF16),
                   jax.ShapeDtypeStruct((512, 1024), F32), jax.ShapeDtypeStruct((3, 512, 1024), BF16)],
        scratch_shapes=[pltpu.VMEM((128, HW), F32), pltpu.VMEM((1, HW), F32), pltpu.VMEM((TH, HW), F32),
                        pltpu.VMEM((TH, HW), F32), pltpu.VMEM((TH, HW), F32), pltpu.VMEM((8, HW), F32),
                        pltpu.SemaphoreType.DMA((3,)), pltpu.SemaphoreType.DMA((3,)), pltpu.SemaphoreType.DMA((2,)),
                        pltpu.SemaphoreType.DMA((2,)), pltpu.SemaphoreType.DMA((3,))],
        compiler_params=_cp(("arbitrary",)),
    )(hq, hf, hi, lbr, tri, trit, drec, sall, dhg, rout, routb, rin_a, rinb_a)


def _fwd_out(o1, o4, o16, l1, l4, l16, rec, ag, hg, x, tgt, anw, hnw, fnw, wout_full, gmat, emat, selmat):
    TT = 256

    def body(o1_r, o4_r, o16_r, l1_r, l4_r, l16_r, rec_r, ag_r, hg_r, x_r, tgt_r, anw_r, hnw_r, fnw_r, wo_r, g_r,
             e_r, sel_r, dx2_o, do1_o, do4_o, do16_o, st1_o, st4_o, st16_o, drec_o, dag_o, dhg_o,
             rout_o, routb_o, small_o, scr_a, scr_b, gwout_o, rbuf, send_sems, recv_sems):
        @pl.when(pl.program_id(0) == 0)
        def _():
            gwout_o[...] = jnp.zeros_like(gwout_o)
            small_o[...] = jnp.zeros_like(small_o)

        def unperm(r4, r16):
            return _unperm_load(r4, r16, scr_a, scr_b)

        def perm_out(val, p1, p4, p16, dt):
            _perm_store(val, scr_a, p1, p4, p16, dt)

        o4u, o16u = unperm(o4_r, o16_r)
        l4c, l16c = unperm(l4_r, l16_r)
        em = e_r[...]
        l1v, l4u, l16u = _mm_exact_r(l1_r[...], em), _mm_exact_r(l4c, em), _mm_exact_r(l16c, em)
        o1v = o1_r[...]
        mx = jnp.maximum(jnp.maximum(l1v, l4u), l16u)
        w1, w4, w16 = jnp.exp(l1v - mx), jnp.exp(l4u - mx), jnp.exp(l16u - mx)
        den = w1 + w4 + w16
        attn = (w1 * o1v + w4 * o4u + w16 * o16u) / den
        lse = mx + jnp.log(den)
        gm = g_r[...]

        def head_mean_a(t):
            return jnp.concatenate([_mm_exact_r(t[:, :256], gm), _mm_exact_r(t[:, 256:], gm)], axis=1)

        def head_mean_h(t):
            return jnp.concatenate(
                [jnp.broadcast_to(jnp.mean(t[:, h * 128:(h + 1) * 128], axis=-1, keepdims=True), (TT, 128))
                 for h in range(4)], axis=1)

        rs_a = lax.rsqrt(head_mean_a(attn * attn) + EPS)
        n_a = attn * rs_a
        agv = ag_r[...]
        sg_a = _sigmoid(agv)
        si_a = agv * sg_a
        anw_v = anw_r[...]
        y_a = (n_a * anw_v) * si_a
        recv = rec_r[...]
        rs_h = lax.rsqrt(head_mean_h(recv * recv) + EPS)
        n_h = recv * rs_h
        hgv = hg_r[...]
        sg_h = _sigmoid(hgv)
        si_h = hgv * sg_h
        hnw_v = hnw_r[...]
        y_h = (n_h * hnw_v) * si_h
        mixed = jnp.concatenate([y_a, y_h], axis=1).astype(BF16)
        xv = x_r[...]
        x2 = xv + _mm(mixed, wo_r[...])
        r2 = lax.rsqrt(jnp.mean(x2 * x2, axis=-1, keepdims=True) + EPS)
        fnw_v = fnw_r[...]
        xn = x2 * r2
        err = xn * fnw_v - tgt_r[...]
        small_o[2:3, :] += 0.5 * jnp.sum(jnp.mean(err * err, axis=-1, keepdims=True), axis=0, keepdims=True)
        dy = err * (1.0 / D)
        small_o[0:1, :] += jnp.sum(dy * xn, axis=0, keepdims=True)
        dyw = dy * fnw_v
        dx2 = r2 * dyw - x2 * ((r2 * r2 * r2) * jnp.mean(dyw * x2, axis=-1, keepdims=True))
        dx2_o[...] = dx2
        dx2b = dx2.astype(BF16)
        gwout_o[...] += _mm_tn(mixed, dx2b)
        dmix = _mm_nt(dx2b, wo_r[...])
        dm_a, dm_h = dmix[:, :AW], dmix[:, AW:]
        dag_o[...] = (dm_a * (n_a * anw_v) * (sg_a * (1.0 + agv * (1.0 - sg_a)))).astype(BF16)
        dn_a = dm_a * anw_v * si_a
        small_o[1:2, 0:AW] += jnp.sum(dm_a * n_a * si_a, axis=0, keepdims=True)
        dattn = rs_a * (dn_a - n_a * head_mean_a(dn_a * n_a))
        delta = head_mean_a(dattn * attn) * float(HEAD)
        perm_out(dattn, do1_o, do4_o, do16_o, BF16)
        stats = _mm_exact_r(lse, sel_r[0]) + _mm_exact_r(delta, sel_r[1])
        perm_out(stats, st1_o, st4_o, st16_o, F32)
        dhg_o[...] = (dm_h * (n_h * hnw_v) * (sg_h * (1.0 + hgv * (1.0 - sg_h)))).astype(BF16)
        dn_h = dm_h * hnw_v * si_h
        small_o[1:2, AW:] += jnp.sum(dm_h * n_h * si_h, axis=0, keepdims=True)
        drec_o[...] = (rs_h * (dn_h - n_h * head_mean_h(dn_h * n_h))).astype(BF16)

        @pl.when(pl.program_id(0) == T // TT - 1)
        def _():
            x, y, c = lax.axis_index("x"), lax.axis_index("y"), lax.axis_index("c")
            cps = [pltpu.make_async_remote_copy(
                src_ref=gwout_o.at[pl.ds(pl.multiple_of(j * 256 + (1 - c) * 128, 128), 128), :], dst_ref=rbuf.at[j],
                send_sem=send_sems.at[j], recv_sem=recv_sems.at[j], device_id=(x, y, 1 - c), device_id_type=MESH)
                for j in range(4)]
            for cp in cps:
                cp.start()
            for j, cp in enumerate(cps):
                cp.wait_recv()
                red = gwout_o[pl.ds(pl.multiple_of(j * 256 + c * 128, 128), 128), :] + rbuf[j]
                rout_o[j * 128:(j + 1) * 128, :] = red
                routb_o[j * 128:(j + 1) * 128, :] = red.astype(BF16)
            for cp in cps:
                cp.wait_send()

    tok = lambda w: pl.BlockSpec((TT, w), lambda i: (i, 0))
    d4 = pl.BlockSpec((4, TT // 4, AW), lambda i: (0, i, 0))
    d16 = pl.BlockSpec((16, TT // 16, AW), lambda i: (0, i, 0))
    const = lambda shape: pl.BlockSpec(shape, lambda i: (0,) * len(shape))
    sd = lambda shape, dt: jax.ShapeDtypeStruct(shape, dt)
    c4 = pl.BlockSpec((4, TT // 4, 128), lambda i: (0, i, 0))
    c16 = pl.BlockSpec((16, TT // 16, 128), lambda i: (0, i, 0))
    p3 = lambda w, dt: [sd((T, w), dt), sd((4, T // 4, w), dt), sd((16, T // 16, w), dt)]
    return pl.pallas_call(
        body, name="fwd_out", grid=(T // TT,),
        in_specs=[tok(AW), d4, d16, tok(128), c4, c16, tok(AW), tok(AW), tok(AW), tok(D), tok(D),
                  const((1, AW)), const((1, HW)), const((1, D)), const((D, D)), const((256, 256)),
                  const((128, AW)), const((2, AW, 128))],
        out_specs=[tok(D)] + [tok(AW), d4, d16] + [tok(128), c4, c16] + [tok(AW)] * 3
        + [const((512, D)), const((512, D)), const((8, D))],
        out_shape=[sd((T, D), F32)] + p3(AW, BF16) + p3(128, F32)
        + [sd((T, AW), BF16), sd((T, AW), BF16), sd((T, AW), BF16), sd((512, D), F32), sd((512, D), BF16),
           sd((8, D), F32)],
        scratch_shapes=[pltpu.VMEM((4, TT, 128), F32), pltpu.VMEM((4, TT, 128), F32), pltpu.VMEM((D, D), F32),
                        pltpu.VMEM((4, 128, D), F32), pltpu.SemaphoreType.DMA((4,)), pltpu.SemaphoreType.DMA((4,))],
        compiler_params=_cp(("arbitrary",)),
    )(o1, o4, o16, l1, l4, l16, rec, ag, hg, x, tgt, anw, hnw, fnw, wout_full, gmat, emat, selmat)


def _dproj_build(dq, dk, dv, dag, pos):
    TT = 256

    def body(dq1, dq4, dq16, dk1, dk4, dk16, dv1, dv4, dv16, dag_r, pos_r, dproj_o, scr_a, scr_b):
        def unperm_sum(r1, r4, r16):
            u4, u16 = _unperm_load(r4, r16, scr_a, scr_b)
            return r1[...] + u4 + u16

        cosf, s1, s2 = _rope_tables(pos_r[...])
        dproj_o[:, 0:512] = _rope_bwd(unperm_sum(dq1, dq4, dq16), cosf, s1, s2).astype(BF16)
        dproj_o[:, 512:1024] = _rope_bwd(unperm_sum(dk1, dk4, dk16), cosf, s1, s2).astype(BF16)
        dproj_o[:, 1024:1536] = unperm_sum(dv1, dv4, dv16).astype(BF16)
        dproj_o[:, 1536:2048] = dag_r[...]

    tok = lambda w: pl.BlockSpec((TT, w), lambda i: (i, 0))
    d4 = pl.BlockSpec((4, TT // 4, AW), lambda i: (0, i, 0))
    d16 = pl.BlockSpec((16, TT // 16, AW), lambda i: (0, i, 0))
    return pl.pallas_call(
        body, name="dproj_build", grid=(T // TT,),
        in_specs=[tok(AW), d4, d16] * 3 + [tok(AW), tok(1)],
        out_specs=tok(NCOL // 2),
        out_shape=jax.ShapeDtypeStruct((T, NCOL // 2), BF16),
        scratch_shapes=[pltpu.VMEM((4, TT, 128), F32), pltpu.VMEM((4, TT, 128), F32)],
        compiler_params=_cp(("parallel",)),
    )(*dq, *dk, *dv, dag, pos)


def _bwd_x(dproj_a, dproj_h, x, dx2, mixw, w_full, rin, rinb, small4, small6, pin_own, pin_rem):
    TT = 256
    NT = T // TT

    def body(dpa_r, dph_r, x_r, dx2_r, mw_r, w_r, rin_r, rinb_r, s4_r, s6_r, pin_in, pinr_in,
             gx_o, pin_o, pinr_o, sall_o, sbuf, send_sems, recv_sems, loc_sems, send2_sems, recv2_sems):
        i = pl.program_id(0)
        start_half, wait_half = _half_exchange(2, rin_r, rinb_r, pin_o, pinr_o, send2_sems, recv2_sems,
                                               loc_sems.at[0])

        @pl.when(i == 0)
        def _():
            sbuf[...] = jnp.zeros_like(sbuf)
            start_half()

        dhn = _mm_nt(dpa_r[...], w_r[:, 0:NCOL // 2]) + _mm_nt(dph_r[...], w_r[:, NCOL // 2:NCOL])
        xv = x_r[...]
        r = lax.rsqrt(jnp.mean(xv * xv, axis=-1, keepdims=True) + EPS)
        dxw = dhn * mw_r[...]
        gx_o[...] = dx2_r[...] + r * dxw - xv * ((r * r * r) * jnp.mean(dxw * xv, axis=-1, keepdims=True))
        sbuf[16:17, :] += jnp.sum(dhn * (xv * r), axis=0, keepdims=True)

        @pl.when(i == NT - 1)
        def _():
            sbuf[0:8, :] = s4_r[...]
            sbuf[8:16, :] = s6_r[...]
            sloc, srem = _small_copies(sbuf, sall_o, send_sems, recv_sems, loc_sems.at[1])
            for cp in sloc + srem:
                cp.start()
            wait_half()
            for cp in srem:
                cp.wait_recv()
            for cp in srem:
                cp.wait_send()
            for cp in sloc:
                cp.wait()

    tok = lambda w: pl.BlockSpec((TT, w), lambda i: (i, 0))
    const = lambda shape: pl.BlockSpec(shape, lambda i: (0,) * len(shape))
    hbm = pl.BlockSpec(memory_space=pltpu.HBM)
    return pl.pallas_call(
        body, name="bwd_x", grid=(NT,),
        in_specs=[tok(NCOL // 2), tok(NCOL // 2), tok(D), tok(D), const((1, D)), const((D, NCOL)), hbm, hbm,
                  const((8, D)), const((8, D)), hbm, hbm],
        out_specs=[tok(D), hbm, hbm, hbm],
        input_output_aliases={10: 1, 11: 2},
        out_shape=[jax.ShapeDtypeStruct((T, D), F32),
                   jax.ShapeDtypeStruct((512, 1024), F32), jax.ShapeDtypeStruct((3, 512, 1024), BF16),
                   jax.ShapeDtypeStruct((8, 24, D), F32)],
        scratch_shapes=[pltpu.VMEM((24, D), F32), pltpu.SemaphoreType.DMA((10,)), pltpu.SemaphoreType.DMA((10,)),
                        pltpu.SemaphoreType.DMA((2,)), pltpu.SemaphoreType.DMA((2,)), pltpu.SemaphoreType.DMA((3,))],
        compiler_params=_cp(("arbitrary",)),
    )(dproj_a, dproj_h, x, dx2, mixw, w_full, rin, rinb, small4, small6, pin_own, pin_rem)


def _grad_w_in(hn, dproj, name):
    TK = 1024
    NK = T // TK

    def body(hnt_r, dp_r, rin_o, rinb_o, acc, rbuf, obuf, obufb, send_sems, recv_sems, wb_sems):
        j = pl.program_id(0)
        kk = pl.program_id(1)
        x, y, c = lax.axis_index("x"), lax.axis_index("y"), lax.axis_index("c")
        mine = pl.ds(pl.multiple_of(c * 512, 512), 512)
        theirs = pl.ds(pl.multiple_of((1 - c) * 512, 512), 512)

        def send(jj):
            return pltpu.make_async_remote_copy(
                src_ref=acc.at[jj, theirs, :], dst_ref=rbuf.at[jj], send_sem=send_sems.at[jj],
                recv_sem=recv_sems.at[jj], device_id=(x, y, 1 - c), device_id_type=MESH)

        def writeback(jj):
            cols = pl.ds(jj * 1024, 1024)
            return [pltpu.make_async_copy(obuf.at[jj], rin_o.at[:, cols], wb_sems.at[jj]),
                    pltpu.make_async_copy(obufb.at[jj], rinb_o.at[:, cols], wb_sems.at[2 + jj])]

        def finalize(jj):
            send(jj).wait_recv()
            red = acc[jj, mine, :] + rbuf[jj]
            obuf[jj] = red
            obufb[jj] = red.astype(BF16)
            for cp in writeback(jj):
                cp.start()

        prod = _mm(hnt_r[...], dp_r[...])

        @pl.when(kk == 0)
        def _():
            acc[j] = prod

        @pl.when(kk > 0)
        def _():
            acc[j] += prod

        @pl.when((kk == NK - 1) & (j == 0))
        def _():
            send(0).start()

        @pl.when((kk == NK - 1) & (j == 1))
        def _():
            send(1).start()
            finalize(0)
            finalize(1)
            for jj in range(2):
                for cp in writeback(jj):
                    cp.wait()
                send(jj).wait_send()

    hbm = pl.BlockSpec(memory_space=pltpu.HBM)
    return pl.pallas_call(
        body, name=name, grid=(2, NK),
        in_specs=[pl.BlockSpec((D, TK), lambda j, kk: (0, kk)), pl.BlockSpec((TK, 1024), lambda j, kk: (kk, j))],
        out_specs=[hbm, hbm],
        out_shape=[jax.ShapeDtypeStruct((512, NCOL // 2), F32), jax.ShapeDtypeStruct((512, NCOL // 2), BF16)],
        scratch_shapes=[pltpu.VMEM((2, D, 1024), F32), pltpu.VMEM((2, 512, 1024), F32), pltpu.VMEM((2, 512, 1024), F32),
                        pltpu.VMEM((2, 512, 1024), BF16),
                        pltpu.SemaphoreType.DMA((2,)), pltpu.SemaphoreType.DMA((2,)), pltpu.SemaphoreType.DMA((4,))],
        compiler_params=_cp(("arbitrary", "arbitrary")),
    )(hn, dproj)


def _half_exchange(base, rin_r, rinb_r, pin_o, pinr_o, send_sems, recv_sems, loc_sem):
    x, y, c = lax.axis_index("x"), lax.axis_index("y"), lax.axis_index("c")
    jm = 2 * x + y

    def piece(ref, j):
        return ref.at[:, pl.ds((j - base) * 1024, 1024)]

    def send(j):
        k = lax.bitwise_xor(jm, j) - 1
        return pltpu.make_async_remote_copy(
            src_ref=piece(rinb_r, j), dst_ref=pinr_o.at[k], send_sem=send_sems.at[j - base], recv_sem=recv_sems.at[k],
            device_id=(jnp.int32(j >> 1), jnp.int32(j & 1), c), device_id_type=MESH)

    def keep(j):
        return pltpu.make_async_copy(piece(rin_r, j), pin_o, loc_sem)

    def landed(k):
        return pltpu.make_async_remote_copy(
            src_ref=pinr_o.at[k], dst_ref=pinr_o.at[k], send_sem=send_sems.at[0], recv_sem=recv_sems.at[k],
            device_id=(x, y, c), device_id_type=MESH)

    def start():
        for j in (base, base + 1):
            pl.when(jm != j)(lambda j=j: send(j).start())
            pl.when(jm == j)(lambda j=j: keep(j).start())

    def wait():
        for j in (base, base + 1):
            pl.when(jm != j)(lambda j=j: send(j).wait_send())

            @pl.when(jm == j)
            def _(j=j):
                keep(j).wait()
                for k in range(3):
                    landed(k).wait_recv()

    return start, wait


def _w_in_piece(ref, j):
    return ref.at[:, pl.ds(j * 1024, 1024)]


def _w_out_piece(ref, j):
    return ref.at[pl.ds(j * 128, 128), :]


def _chip_copies(piece, src_r, srcb_r, own_o, rem_o, send_sems, recv_sems, loc_sem):
    x, y, c = lax.axis_index("x"), lax.axis_index("y"), lax.axis_index("c")
    chips = [(1 - x, y), (x, 1 - y), (1 - x, 1 - y)]
    loc = [pltpu.make_async_copy(piece(src_r, 2 * x + y), own_o, loc_sem)]
    rem = [pltpu.make_async_remote_copy(
        src_ref=piece(srcb_r, 2 * px + py), dst_ref=rem_o.at[k], send_sem=send_sems.at[k],
        recv_sem=recv_sems.at[k], device_id=(px, py, c), device_id_type=MESH) for k, (px, py) in enumerate(chips)]
    return loc, rem


def _small_copies(small_r, sall_o, send_sems, recv_sems, loc_sem):
    x, y, c = lax.axis_index("x"), lax.axis_index("y"), lax.axis_index("c")
    me = 4 * x + 2 * y + c
    loc = [pltpu.make_async_copy(small_r, sall_o.at[me], loc_sem)]
    rem = []
    k = 3
    for fx in range(2):
        for fy in range(2):
            for fc in range(2):
                if fx or fy or fc:
                    peer = (1 - x if fx else x, 1 - y if fy else y, 1 - c if fc else c)
                    rem.append(pltpu.make_async_remote_copy(
                        src_ref=small_r, dst_ref=sall_o.at[me], send_sem=send_sems.at[k],
                        recv_sem=recv_sems.at[k], device_id=peer, device_id_type=MESH))
                    k += 1
    return loc, rem


def _pair_share(pin_own, pin_rem, pout_own, pout_rem):
    def body(pio_r, pir_r, poo_r, por_r, fin_o, fout_o, sin, sout, send_sems, recv_sems):
        x, y, c = lax.axis_index("x"), lax.axis_index("y"), lax.axis_index("c")
        sibling = (x, y, 1 - c)
        sout[...] = ((poo_r[...] + por_r[0].astype(F32)) + por_r[1].astype(F32)) + por_r[2].astype(F32)
        sin[...] = ((pio_r[...] + pir_r[0].astype(F32)) + pir_r[1].astype(F32)) + pir_r[2].astype(F32)
        rem = [pltpu.make_async_remote_copy(src_ref=sin, dst_ref=fin_o.at[c], send_sem=send_sems.at[0],
                                            recv_sem=recv_sems.at[0], device_id=sibling, device_id_type=MESH),
               pltpu.make_async_remote_copy(src_ref=sout, dst_ref=fout_o.at[c], send_sem=send_sems.at[1],
                                            recv_sem=recv_sems.at[1], device_id=sibling, device_id_type=MESH)]
        for cp in rem:
            cp.start()
        fin_o[c] = sin[...]
        fout_o[c] = sout[...]
        for cp in rem:
            cp.wait_recv()
        for cp in rem:
            cp.wait_send()

    vm = pl.BlockSpec(memory_space=pltpu.VMEM)
    return pl.pallas_call(
        body, name="pair_share",
        out_shape=(jax.ShapeDtypeStruct((2, 512, 1024), F32), jax.ShapeDtypeStruct((2, 128, D), F32)),
        in_specs=[vm, vm, vm, vm], out_specs=(vm, vm),
        scratch_shapes=[pltpu.VMEM((512, 1024), F32), pltpu.VMEM((128, D), F32),
                        pltpu.SemaphoreType.DMA((2,)), pltpu.SemaphoreType.DMA((2,))],
        compiler_params=_cp(),
    )(pin_own, pin_rem, pout_own, pout_rem)


def _adamw_math(w, g, m, v):
    m = B1 * m + (1.0 - B1) * g
    v = B2 * v + (1.0 - B2) * (g * g)
    m_hat = m / (1.0 - B1 ** STEP)
    v_hat = v / (1.0 - B2 ** STEP)
    delta = -LR * (m_hat / (jnp.sqrt(v_hat) + AEPS) + WD * w)
    return delta, m, v


def _adamw(w, g, m, v, name):
    rows, cols = w.shape
    tr = min(rows, 256)

    def body(w_r, g_r, m_r, v_r, d_o, m_o, v_o):
        d, mm, vv = _adamw_math(w_r[...], g_r[...], m_r[...], v_r[...])
        d_o[...] = d
        m_o[...] = mm
        v_o[...] = vv

    blk = pl.BlockSpec((tr, cols), lambda i: (i, 0))
    return pl.pallas_call(
        body, name=name, grid=(rows // tr,),
        in_specs=[blk] * 4, out_specs=[blk] * 3,
        out_shape=[jax.ShapeDtypeStruct((rows, cols), F32)] * 3,
        compiler_params=_cp(("parallel",)),
    )(w, g, m, v)


def _adamw_small(sall, params):
    def body(sall_r, *refs):
        ins, outs = refs[:15], refs[15:]
        tot = sall_r[0]
        for dv in range(1, 8):
            tot = tot + sall_r[dv]
        grads = [tot[16:17, :], tot[1:2, 0:AW], tot[1:2, AW:], tot[8:10, 0:HW], tot[0:1, :]]
        outs[0][...] = tot[2:3, 0:1]
        for p in range(5):
            w_r, m_r, v_r = ins[3 * p:3 * p + 3]
            g = grads[p]
            d, mm, vv = _adamw_math(w_r[...], g, m_r[...], v_r[...])
            outs[1 + 4 * p][...] = g
            outs[2 + 4 * p][...] = d
            outs[3 + 4 * p][...] = mm
            outs[4 + 4 * p][...] = vv

    flat = [a for p in params for a in p]
    shapes = [jax.ShapeDtypeStruct((1, 1), F32)]
    for p in params:
        shapes += [jax.ShapeDtypeStruct(p[0].shape, F32)] * 4
    vm = pl.BlockSpec(memory_space=pltpu.VMEM)
    return pl.pallas_call(
        body, name="adamw_small",
        in_specs=[vm] * 16, out_specs=[vm] * 21, out_shape=shapes,
        compiler_params=_cp(),
    )(sall, *flat)


def kernel(x, positions, w_in, w_out, mix_norm_w, attn_out_norm_w, hgrn_out_norm_w, hgrn_lb_raw, final_norm_w, loss_target, m_w_in, m_w_out, m_mix_norm_w, m_attn_out_norm_w, m_hgrn_out_norm_w, m_hgrn_lb_raw, m_final_norm_w, v_w_in, v_w_out, v_mix_norm_w, v_attn_out_norm_w, v_hgrn_out_norm_w, v_hgrn_lb_raw, v_final_norm_w):
    xs = x.reshape(T, D)
    tgt = loss_target.reshape(T, D)
    pos = positions.reshape(T, 1)
    fnw = final_norm_w.reshape(1, D)

    ti = np.arange(TH)
    tri_np = ((ti[:, None] // CHUNK == ti[None, :] // CHUNK) & (ti[None, :] <= ti[:, None])).astype(np.float32)
    tri = jnp.asarray(tri_np, BF16)
    trit = jnp.asarray(tri_np.T, BF16)
    hi_ = np.arange(AW) // HEAD
    gmat = jnp.asarray((hi_[:256, None] == hi_[None, :256]).astype(np.float32) / HEAD, BF16)
    emat_np = (np.arange(128)[:, None] == hi_[None, :]).astype(np.float32)
    sel_np = np.zeros((2, AW, 128), np.float32)
    sel_np[0, np.arange(8) * HEAD, np.arange(8)] = 1.0
    sel_np[1, np.arange(8) * HEAD, 8 + np.arange(8)] = 1.0
    emat = jnp.asarray(emat_np, BF16)
    selmat = jnp.asarray(sel_np, BF16)

    wb_in, wb_out = _cast_weights(w_in.reshape(D, 1024), w_out.reshape(256, D))
    jm_arr = (2 * lax.axis_index("x") + lax.axis_index("y")).astype(jnp.int32).reshape(1)
    (hn, q1, k1, v1, q4, k4, v4, q16, k16, v16, ag, hq, hf, hi, hg, w_full, wout4) = _fwd_in(
        xs, pos, mix_norm_w, wb_in, wb_out, jm_arr)
    wout_full = wout4.reshape(D, D)
    flat = lambda a: a.reshape(T, AW)
    o1, l1 = _attn_fwd(q1, k1, v1, T // BLK, "attn_fwd_d1")
    o4, l4 = _attn_fwd(flat(q4), flat(k4), flat(v4), T // 4 // BLK, "attn_fwd_d4")
    o16, l16 = _attn_fwd(flat(q16), flat(k16), flat(v16), T // 16 // BLK, "attn_fwd_d16")
    rec, sall = _hgrn_fwd(hq, hf, hi, hgrn_lb_raw, tri)

    (dx2, do1, do4, do16, st1, st4, st16, drec, dag, dhg, rout, routb, small4) = _fwd_out(
        o1, o4.reshape(4, T // 4, AW), o16.reshape(16, T // 16, AW),
        l1, l4.reshape(4, T // 4, 128), l16.reshape(16, T // 16, 128),
        rec, ag, hg, xs, tgt, attn_out_norm_w, hgrn_out_norm_w, fnw, wout_full, gmat, emat, selmat)

    fst = lambda a: a.reshape(T, 128)
    dq1, dk1, dv1 = _attn_bwd(q1, k1, v1, do1, st1, T // BLK, "attn_bwd_d1")
    dq4, dk4, dv4 = _attn_bwd(flat(q4), flat(k4), flat(v4), flat(do4), fst(st4), T // 4 // BLK, "attn_bwd_d4")
    dq16, dk16, dv16 = _attn_bwd(flat(q16), flat(k16), flat(v16), flat(do16), fst(st16), T // 16 // BLK,
                                 "attn_bwd_d16")
    r4 = lambda a: a.reshape(4, T // 4, AW)
    r16 = lambda a: a.reshape(16, T // 16, AW)
    dproj_a = _dproj_build((dq1, r4(dq4), r16(dq16)), (dk1, r4(dk4), r16(dk16)), (dv1, r4(dv4), r16(dv16)),
                           dag, pos)
    rin_a, rinb_a = _grad_w_in(hn, dproj_a, "grad_w_in_attn")
    dproj_h, small6, pout_own, pout_rem, pin_own, pin_rem = _hgrn_bwd(
        hq, hf, hi, hgrn_lb_raw, tri, trit, drec, sall, dhg, rout, routb, rin_a, rinb_a)
    rin_h, rinb_h = _grad_w_in(hn, dproj_h, "grad_w_in_hgrn")
    gx, pin_own, pin_rem, small_all = _bwd_x(dproj_a, dproj_h, xs, dx2, mix_norm_w, w_full, rin_h, rinb_h,
                                             small4, small6, pin_own, pin_rem)
    fin, fout = _pair_share(pin_own, pin_rem, pout_own, pout_rem)
    g_w_in = fin.reshape(D, 1024)
    g_w_out = fout.reshape(256, D)

    d_in, nm_in, nv_in = _adamw(w_in.reshape(D, 1024), g_w_in, m_w_in.reshape(D, 1024), v_w_in.reshape(D, 1024),
                                "adamw_w_in")
    d_out, nm_out, nv_out = _adamw(w_out.reshape(256, D), g_w_out, m_w_out.reshape(256, D), v_w_out.reshape(256, D),
                                   "adamw_w_out")
    params = [(mix_norm_w, m_mix_norm_w, v_mix_norm_w),
              (attn_out_norm_w, m_attn_out_norm_w, v_attn_out_norm_w),
              (hgrn_out_norm_w, m_hgrn_out_norm_w, v_hgrn_out_norm_w),
              (hgrn_lb_raw, m_hgrn_lb_raw, v_hgrn_lb_raw),
              (fnw, m_final_norm_w.reshape(1, D), v_final_norm_w.reshape(1, D))]
    so = _adamw_small(small_all, params)
    loss = so[0].reshape(())
    g_s = [so[1 + 4 * p] for p in range(5)]
    d_s = [so[2 + 4 * p] for p in range(5)]
    m_s = [so[3 + 4 * p] for p in range(5)]
    v_s = [so[4 + 4 * p] for p in range(5)]
    for lst in (g_s, d_s, m_s, v_s):
        lst[4] = lst[4].reshape(D)

    return (loss, gx.reshape(1, T, D),
            g_w_in.reshape(1, D, 1024), g_w_out.reshape(1, 256, D), *g_s,
            d_in.reshape(1, D, 1024), d_out.reshape(1, 256, D), *d_s,
            nm_in.reshape(1, D, 1024), nm_out.reshape(1, 256, D), *m_s,
            nv_in.reshape(1, D, 1024), nv_out.reshape(1, 256, D), *v_s)
```

```python
import functools

import numpy as np
import jax
import jax.numpy as jnp
from jax import lax
from jax.experimental import pallas as pl
from jax.experimental.pallas import tpu as pltpu

F32 = jnp.float32
BF16 = jnp.bfloat16

T = 4096
D = 1024
AW = 512
HW = 512
NCOL = 4096
HEAD = 64
BLK = 128
CHUNK = 64
EPS = 1e-6
SCALE = HEAD ** -0.5
NEG = -1e30
ROPE_THETA = 500000.0
INV_FREQ = [float(v) for v in
            (np.float32(ROPE_THETA) ** (-(np.arange(8, dtype=np.float32)) * np.float32(0.125)))]
LR, B1, B2, AEPS, WD, STEP = 0.001, 0.9, 0.999, 1e-08, 0.01, 10
VMEM_LIMIT = 56 * 1024 * 1024
MESH = pl.DeviceIdType.MESH


def _cp(sem=None, **kw):
    return pltpu.CompilerParams(dimension_semantics=sem, vmem_limit_bytes=VMEM_LIMIT, **kw)


def _mm(a, b):
    return jnp.dot(a, b, preferred_element_type=F32)


def _mm_nt(a, b):
    return lax.dot_general(a, b, (((1,), (1,)), ((), ())), preferred_element_type=F32)


def _mm_tn(a, b):
    return lax.dot_general(a, b, (((0,), (0,)), ((), ())), preferred_element_type=F32)


def _split3(x):
    h = x.astype(BF16)
    r = x - h.astype(F32)
    m = r.astype(BF16)
    l = (r - m.astype(F32)).astype(BF16)
    return h, m, l


def _mm_exact_l(mat_bf, x):
    h, m, l = _split3(x)
    return _mm(mat_bf, h) + _mm(mat_bf, m) + _mm(mat_bf, l)


def _mm_exact_r(x, mat_bf):
    h = x.astype(BF16)
    l = (x - h.astype(F32)).astype(BF16)
    return _mm(h, mat_bf) + _mm(l, mat_bf)


def _sigmoid(x):
    return 0.5 * jnp.tanh(0.5 * x) + 0.5


def _rope_tables(pos):
    lane = lax.broadcasted_iota(jnp.int32, (1, 128), 1)
    jl = lane & 63
    fi = jl & 7
    inv = jnp.zeros((1, 128), F32)
    for kk in range(8):
        inv = jnp.where(fi == kk, INV_FREQ[kk], inv)
    ang = pos.astype(F32) * inv
    c = jnp.cos(ang)
    s = jnp.sin(ang)
    cosf = jnp.where(jl < 16, c, 1.0)
    s1 = jnp.where(jl < 8, -s, 0.0)
    s2 = jnp.where((jl >= 8) & (jl < 16), s, 0.0)
    return cosf, s1, s2


def _rope(t, cosf, s1, s2):
    parts = []
    for ci in range(t.shape[1] // 128):
        tc = t[:, ci * 128:(ci + 1) * 128]
        parts.append(tc * cosf + pltpu.roll(tc, 120, 1) * s1 + pltpu.roll(tc, 8, 1) * s2)
    return jnp.concatenate(parts, axis=1)


def _rope_bwd(g, cosf, s1, s2):
    parts = []
    for ci in range(g.shape[1] // 128):
        gc = g[:, ci * 128:(ci + 1) * 128]
        parts.append(gc * cosf + pltpu.roll(gc * s1, 8, 1) + pltpu.roll(gc * s2, 120, 1))
    return jnp.concatenate(parts, axis=1)


def _perm_store(val, scr, o1, o4, o16, dt):
    n = val.shape[0]
    o1[...] = val.astype(dt)
    for ci in range(val.shape[1] // 128):
        cs = slice(ci * 128, (ci + 1) * 128)
        scr[ci] = val[:, cs]
        for rr in range(4):
            o4[rr, :, cs] = scr[ci, pl.ds(rr, n // 4, stride=4), :].astype(dt)
        for rr in range(16):
            o16[rr, :, cs] = scr[ci, pl.ds(rr, n // 16, stride=16), :].astype(dt)


def _unperm_load(r4, r16, scr_a, scr_b):
    n = scr_a.shape[1]
    nc = r4.shape[-1] // 128
    for ci in range(nc):
        cs = slice(ci * 128, (ci + 1) * 128)
        for rr in range(4):
            scr_a[ci, pl.ds(rr, n // 4, stride=4), :] = r4[rr, :, cs].astype(F32)
        for rr in range(16):
            scr_b[ci, pl.ds(rr, n // 16, stride=16), :] = r16[rr, :, cs].astype(F32)
    return (jnp.concatenate([scr_a[ci] for ci in range(nc)], axis=1),
            jnp.concatenate([scr_b[ci] for ci in range(nc)], axis=1))


def _cast_weights(w_in, w_out):
    def body(win_ref, wout_ref, bin_ref, bout_ref):
        bin_ref[...] = win_ref[...].astype(BF16)

        @pl.when(pl.program_id(0) == 0)
        def _():
            bout_ref[...] = wout_ref[...].astype(BF16)

    return pl.pallas_call(
        body, name="cast_weights", grid=(4,),
        in_specs=[pl.BlockSpec((256, 1024), lambda i: (i, 0)), pl.BlockSpec((256, D), lambda i: (0, 0))],
        out_specs=[pl.BlockSpec((256, 1024), lambda i: (i, 0)), pl.BlockSpec((256, D), lambda i: (0, 0))],
        out_shape=(jax.ShapeDtypeStruct((D, 1024), BF16), jax.ShapeDtypeStruct((256, D), BF16)),
        compiler_params=_cp(("arbitrary",)),
    )(w_in, w_out)


def _fwd_in(x, pos, mixw, wb_in, wb_out, jm_arr):
    TT = 512
    NT = T // TT

    def body(jm_ref, x_ref, pos_ref, mw_ref, wbin_ref, wbout_ref,
             hnt_ref, q1, k1, v1, q4, k4, v4, q16, k16, v16, ag, hq, hf, hi, hg, wfull_o, woutfull_o,
             wbuf, wobuf, hn_all, scr, send_sems, recv_sems, loc_sems):
        s = pl.program_id(0)
        i = pl.program_id(1)
        mx, my, c = lax.axis_index("x"), lax.axis_index("y"), lax.axis_index("c")
        me, sibling = (mx, my, c), (mx, my, 1 - c)
        chips = [(mx, 1 - my), (1 - mx, my), (1 - mx, 1 - my)]
        jm = 2 * mx + my
        rows_in = [pl.ds(pl.multiple_of(h * 512, 512), 512) for h in (c, 1 - c)]
        rows_out = [pl.ds(pl.multiple_of(h * 128, 128), 128) for h in (c, 1 - c)]

        def blk(k):
            return lax.bitwise_xor(jm, k + 1)

        def rc(n, ref, to):
            return pltpu.make_async_remote_copy(src_ref=ref, dst_ref=ref, send_sem=send_sems.at[n],
                                                recv_sem=recv_sems.at[n], device_id=to, device_id_type=MESH)

        send_in = lambda k: rc(k, wbuf.at[jm, rows_in[0], :], (*chips[k], c))
        send_out = lambda k: rc(3 + k, wobuf.at[jm, rows_out[0], :], (*chips[k], c))
        got_in = lambda k: rc(k, wbuf.at[blk(k), rows_in[0], :], me)
        got_out = lambda k: rc(3 + k, wobuf.at[blk(k), rows_out[0], :], me)
        pass_in = lambda k: rc(6 + k, wbuf.at[blk(k), rows_in[0], :], sibling)
        pass_out = lambda k: rc(9 + k, wobuf.at[blk(k), rows_out[0], :], sibling)
        passed_in = lambda k: rc(6 + k, wbuf.at[blk(k), rows_in[1], :], me)
        passed_out = lambda k: rc(9 + k, wobuf.at[blk(k), rows_out[1], :], me)

        def keep(j, n):
            return pltpu.make_async_copy(wbuf.at[j], wfull_o.at[:, pl.ds(j * 1024, 1024)], loc_sems.at[n])

        @pl.when((s == 0) & (i == 0))
        def _():
            own = [pltpu.make_async_copy(wbin_ref, wbuf.at[jm], loc_sems.at[4]),
                   pltpu.make_async_copy(wbout_ref, wobuf.at[jm], loc_sems.at[5])]
            for cp in own:
                cp.start()
            for cp in own:
                cp.wait()
            send_in(0).start()
            send_in(1).start()
            keep(jm, 0).start()

        def arrive(k):
            if k == 0:
                send_in(0).wait_send()
                send_in(1).wait_send()
                send_in(2).start()
            got_in(k).wait_recv()
            pass_in(k).start()
            passed_in(k).wait_recv()
            keep(blk(k), k + 1).start()
            if k == 2:
                for kk in range(3):
                    send_out(kk).start()

        for k in range(3):
            pl.when((s == k + 1) & (i == 0))(functools.partial(arrive, k))

        tile = pl.ds(pl.multiple_of(i * TT, TT), TT)

        @pl.when(s == 0)
        def _():
            xv = x_ref[...]
            r = lax.rsqrt(jnp.mean(xv * xv, axis=-1, keepdims=True) + EPS)
            hnf = (xv * r) * mw_ref[...]
            hn_all[tile, :] = hnf.astype(BF16)
            hnt_ref[...] = hnf.T.astype(BF16)

        def project(jj):
            hn = hn_all[tile, :]
            lo = _mm(hn, wbuf[jj, :, 0:512])
            hi_cols = _mm(hn, wbuf[jj, :, 512:1024])
            if jj == 0:
                cosf, s1, s2 = _rope_tables(pos_ref[...])
                _perm_store(_rope(lo, cosf, s1, s2), scr, q1, q4, q16, BF16)
                _perm_store(_rope(hi_cols, cosf, s1, s2), scr, k1, k4, k16, BF16)
            elif jj == 1:
                _perm_store(lo, scr, v1, v4, v16, BF16)
                ag[...] = hi_cols.astype(BF16)
            elif jj == 2:
                hq[...] = lo.astype(BF16)
                hf[...] = hi_cols.astype(BF16)
            else:
                hi[...] = lo.astype(BF16)
                hg[...] = hi_cols.astype(BF16)

        j = lax.bitwise_xor(jm, s)
        for jj in range(4):
            pl.when(j == jj)(functools.partial(project, jj))

        @pl.when((s == 3) & (i == NT - 1))
        def _():
            for k in range(3):
                got_out(k).wait_recv()
                pass_out(k).start()
            for k in range(3):
                passed_out(k).wait_recv()
            out = pltpu.make_async_copy(wobuf, woutfull_o, loc_sems.at[4])
            out.start()
            send_in(2).wait_send()
            for k in range(3):
                send_out(k).wait_send()
                pass_in(k).wait_send()
                pass_out(k).wait_send()
            keep(jm, 0).wait()
            for k in range(3):
                keep(blk(k), k + 1).wait()
            out.wait()

    def at_stage_of(jb):
        def index(s, i, jm_ref):
            sa = lax.bitwise_xor(jm_ref[0], jb)
            return jnp.where(s < sa, 0, jnp.where(s == sa, i, NT - 1))
        return index

    tok = lambda w, jb: pl.BlockSpec((TT, w), lambda s, i, jm_ref: (at_stage_of(jb)(s, i, jm_ref), 0))
    d4 = lambda jb: pl.BlockSpec((4, TT // 4, AW), lambda s, i, jm_ref: (0, at_stage_of(jb)(s, i, jm_ref), 0))
    d16 = lambda jb: pl.BlockSpec((16, TT // 16, AW), lambda s, i, jm_ref: (0, at_stage_of(jb)(s, i, jm_ref), 0))
    hbm = pl.BlockSpec(memory_space=pltpu.HBM)
    sd = lambda shape, dt: jax.ShapeDtypeStruct(shape, dt)
    in_own_stage = lambda s, i: jnp.where(s == 0, i, NT - 1)
    grid_spec = pltpu.PrefetchScalarGridSpec(
        num_scalar_prefetch=1, grid=(4, NT),
        in_specs=[pl.BlockSpec((TT, D), lambda s, i, jm_ref: (in_own_stage(s, i), 0)),
                  pl.BlockSpec((TT, 1), lambda s, i, jm_ref: (i, 0)),
                  pl.BlockSpec((1, D), lambda s, i, jm_ref: (0, 0)), hbm, hbm],
        out_specs=[pl.BlockSpec((D, TT), lambda s, i, jm_ref: (0, in_own_stage(s, i))),
                   tok(AW, 0), tok(AW, 0), tok(AW, 1), d4(0), d4(0), d4(1), d16(0), d16(0), d16(1),
                   tok(AW, 1), tok(AW, 2), tok(AW, 2), tok(AW, 3), tok(AW, 3), hbm, hbm],
        scratch_shapes=[pltpu.VMEM((4, D, 1024), BF16), pltpu.VMEM((4, 256, D), BF16), pltpu.VMEM((T, D), BF16),
                        pltpu.VMEM((4, TT, 128), F32), pltpu.SemaphoreType.DMA((12,)),
                        pltpu.SemaphoreType.DMA((12,)), pltpu.SemaphoreType.DMA((6,))])
    return pl.pallas_call(
        body, name="fwd_in", grid_spec=grid_spec,
        out_shape=[sd((D, T), BF16)] + [sd((T, AW), BF16)] * 3 + [sd((4, T // 4, AW), BF16)] * 3
        + [sd((16, T // 16, AW), BF16)] * 3
        + [sd((T, AW), BF16)] * 5 + [sd((D, NCOL), BF16), sd((4, 256, D), BF16)],
        compiler_params=_cp(("arbitrary", "arbitrary")),
    )(jm_arr, x, pos, mixw, wb_in, wb_out)


def _band_mask(key_axis, nkeys=2 * BLK):
    shape = (nkeys, 2 * BLK) if key_axis == 0 else (2 * BLK, nkeys)
    kj = lax.broadcasted_iota(jnp.int32, shape, key_axis)
    qi = lax.broadcasted_iota(jnp.int32, shape, 1 - key_axis) & (BLK - 1)
    return (kj >= qi) & (kj <= qi + BLK), kj, qi


def _stack_heads(t2, in_a):
    z = jnp.zeros_like(t2)
    return jnp.concatenate([jnp.where(in_a[0], t2, z), jnp.where(in_a[1], t2, z)], axis=0)


def _attn_fwd(q, k, v, nb, name):
    n = min(4, nb)
    CH = n * BLK
    halo = nb > n

    def body(*refs):
        if halo:
            q_ref, k_ref, v_ref, kp_ref, vp_ref, o_ref, lse_ref = refs
        else:
            q_ref, k_ref, v_ref, o_ref, lse_ref = refs
        lane = lax.broadcasted_iota(jnp.int32, (1, 128), 1)
        in_a = [lane < HEAD, lane >= HEAD]
        band, kj, _ = _band_mask(1)
        thr0 = jnp.where((n * pl.program_id(0)) % nb == 0, BLK, 0) if halo else BLK
        mask0 = band & (kj >= thr0)
        for b in range(n):
            rs = slice(b * BLK, (b + 1) * BLK)
            stat = jnp.zeros((BLK, 128), F32)
            for hp in range(4):
                cs = slice(hp * 128, (hp + 1) * 128)
                q2s = _stack_heads(q_ref[rs, cs], in_a)
                if b == 0:
                    kprev = kp_ref[:, cs] if halo else k_ref[rs, cs]
                    vprev = vp_ref[:, cs] if halo else v_ref[rs, cs]
                    kk = jnp.concatenate([kprev, k_ref[rs, cs]], axis=0)
                    vv = jnp.concatenate([vprev, v_ref[rs, cs]], axis=0)
                    mask = mask0
                else:
                    kk = k_ref[(b - 1) * BLK:(b + 1) * BLK, cs]
                    vv = v_ref[(b - 1) * BLK:(b + 1) * BLK, cs]
                    mask = band
                s = jnp.where(mask, _mm_nt(q2s, kk) * SCALE, NEG)
                m = jnp.max(s, axis=-1, keepdims=True)
                p = jnp.exp(s - m)
                l = jnp.sum(p, axis=-1, keepdims=True)
                o = _mm(p.astype(BF16), vv) / l
                lse = m + jnp.log(l)
                o_ref[rs, cs] = jnp.where(in_a[0], o[:BLK], o[BLK:]).astype(BF16)
                stat = jnp.where(lane == 2 * hp, lse[:BLK], stat)
                stat = jnp.where(lane == 2 * hp + 1, lse[BLK:], stat)
            lse_ref[rs, :] = stat

    cur = pl.BlockSpec((CH, AW), lambda i: (i, 0))
    prev = pl.BlockSpec((BLK, AW), lambda i: (jnp.maximum(n * i - 1, 0), 0))
    return pl.pallas_call(
        body, name=name, grid=(T // CH,),
        in_specs=[cur, cur, cur] + ([prev, prev] if halo else []),
        out_specs=[cur, pl.BlockSpec((CH, 128), lambda i: (i, 0))],
        out_shape=[jax.ShapeDtypeStruct((T, AW), BF16), jax.ShapeDtypeStruct((T, 128), F32)],
        compiler_params=_cp(("parallel",)),
    )(*((q, k, v) + ((k, v) if halo else ())))


def _attn_bwd(q, k, v, do, st, nb, name):
    n = min(4, nb)
    CH = n * BLK
    NBLK = T // BLK
    halo = nb > n

    def body(*refs):
        if halo:
            (q_ref, k_ref, v_ref, do_ref, st_ref, kp_ref, vp_ref, qn_ref, don_ref, stn_ref,
             dq_ref, dk_ref, dv_ref) = refs
        else:
            q_ref, k_ref, v_ref, do_ref, st_ref, dq_ref, dk_ref, dv_ref = refs
        i = pl.program_id(0)
        lane = lax.broadcasted_iota(jnp.int32, (1, 128), 1)
        in_a = [lane < HEAD, lane >= HEAD]
        band, kj, _ = _band_mask(0)
        thr0 = jnp.where((n * i) % nb == 0, BLK, 0) if halo else BLK
        mask0 = band & (kj >= thr0)

        def stat_rows(st_t, hp):
            lse_r = jnp.concatenate([st_t[2 * hp:2 * hp + 1, :], st_t[2 * hp + 1:2 * hp + 2, :]], axis=1)
            dl_r = jnp.concatenate([st_t[8 + 2 * hp:9 + 2 * hp, :], st_t[9 + 2 * hp:10 + 2 * hp, :]], axis=1)
            return lse_r, dl_r

        st_t = [st_ref[b * BLK:(b + 1) * BLK, :].T for b in range(n)]
        if halo:
            nxt_thr = jnp.where((n * i + n) % nb == 0, 2 * BLK, 0)
            _, kj1, qi1 = _band_mask(0, BLK)
            mask_next = kj1 >= qi1 + nxt_thr
            stn_t = stn_ref[...].T

        for hp in range(4):
            cs = slice(hp * 128, (hp + 1) * 128)
            kb = [k_ref[b * BLK:(b + 1) * BLK, cs] for b in range(n)]
            vb = [v_ref[b * BLK:(b + 1) * BLK, cs] for b in range(n)]
            dk_acc = [jnp.zeros((BLK, 128), F32) for _ in range(n)]
            dv_acc = [jnp.zeros((BLK, 128), F32) for _ in range(n)]
            for b in range(n):
                rs = slice(b * BLK, (b + 1) * BLK)
                q2s = _stack_heads(q_ref[rs, cs], in_a)
                do2s = _stack_heads(do_ref[rs, cs], in_a)
                if b == 0:
                    kprev = kp_ref[:, cs] if halo else kb[0]
                    vprev = vp_ref[:, cs] if halo else vb[0]
                    mask = mask0
                else:
                    kprev, vprev, mask = kb[b - 1], vb[b - 1], band
                kk = jnp.concatenate([kprev, kb[b]], axis=0)
                vv = jnp.concatenate([vprev, vb[b]], axis=0)
                lse_r, dl_r = stat_rows(st_t[b], hp)
                s_t = jnp.where(mask, _mm_nt(kk, q2s) * SCALE, NEG)
                p_t = jnp.exp(s_t - lse_r)
                ds_t = (p_t * (_mm_nt(vv, do2s) - dl_r)).astype(BF16)
                dkk = _mm(ds_t, q2s) * SCALE
                dvv = _mm(p_t.astype(BF16), do2s)
                dqs = _mm_tn(ds_t, kk) * SCALE
                dq_ref[rs, cs] = jnp.where(in_a[0], dqs[:BLK], dqs[BLK:]).astype(BF16)
                dk_acc[b] += dkk[BLK:]
                dv_acc[b] += dvv[BLK:]
                if b > 0:
                    dk_acc[b - 1] += dkk[:BLK]
                    dv_acc[b - 1] += dvv[:BLK]
            if halo:
                q2s = _stack_heads(qn_ref[:, cs], in_a)
                do2s = _stack_heads(don_ref[:, cs], in_a)
                lse_r, dl_r = stat_rows(stn_t, hp)
                s_t = jnp.where(mask_next, _mm_nt(kb[n - 1], q2s) * SCALE, NEG)
                p_t = jnp.exp(s_t - lse_r)
                ds_t = (p_t * (_mm_nt(vb[n - 1], do2s) - dl_r)).astype(BF16)
                dk_acc[n - 1] += _mm(ds_t, q2s) * SCALE
                dv_acc[n - 1] += _mm(p_t.astype(BF16), do2s)
            for b in range(n):
                dk_ref[b * BLK:(b + 1) * BLK, cs] = dk_acc[b].astype(BF16)
                dv_ref[b * BLK:(b + 1) * BLK, cs] = dv_acc[b].astype(BF16)

    cur = pl.BlockSpec((CH, AW), lambda i: (i, 0))
    cur_st = pl.BlockSpec((CH, 128), lambda i: (i, 0))
    prev = pl.BlockSpec((BLK, AW), lambda i: (jnp.maximum(n * i - 1, 0), 0))
    nxt = pl.BlockSpec((BLK, AW), lambda i: (jnp.minimum(n * i + n, NBLK - 1), 0))
    nxt_st = pl.BlockSpec((BLK, 128), lambda i: (jnp.minimum(n * i + n, NBLK - 1), 0))
    ins = [cur] * 4 + [cur_st] + ([prev, prev, nxt, nxt, nxt_st] if halo else [])
    args = (q, k, v, do, st) + ((k, v, q, do, st) if halo else ())
    return pl.pallas_call(
        body, name=name, grid=(T // CH,),
        in_specs=ins,
        out_specs=[cur] * 3,
        out_shape=[jax.ShapeDtypeStruct((T, AW), BF16)] * 3,
        compiler_params=_cp(("parallel",)),
    )(*args)


TH = 256
NCH = TH // CHUNK


def _hgrn_common(hq_ref, hf_ref, lbr_ref, tri_ref):
    r0 = lbr_ref[0:1, :]
    r1 = lbr_ref[1:2, :]
    mx = jnp.maximum(r0, r1)
    e0 = jnp.exp(r0 - mx)
    e1 = jnp.exp(r1 - mx)
    lb = e0 / (e0 + e1)
    hqv = hq_ref[...].astype(F32)
    sq = _sigmoid(hqv)
    qv = hqv * sq
    sf = _sigmoid(hf_ref[...].astype(F32))
    f = lb + (1.0 - lb) * sf
    kv = 1.0 - f
    g = jnp.log(f)
    cum = _mm_exact_l(tri_ref[...], g)
    lastb = jnp.concatenate(
        [jnp.broadcast_to(cum[c * CHUNK + CHUNK - 1:(c + 1) * CHUNK, :], (CHUNK, HW)) for c in range(NCH)], axis=0)
    ea = jnp.exp(cum)
    ena = jnp.exp(-cum)
    eend = jnp.exp(lastb - cum)
    return dict(lb=lb, hq=hqv, sq=sq, q=qv, sf=sf, f=f, k=kv, cum=cum, lastb=lastb, ea=ea, ena=ena, eend=eend,
                qd=qv * ea, ki=kv * ena, ke=kv * eend, dec=jnp.exp(lastb))


def _tri_mask(transposed=False):
    ti = lax.broadcasted_iota(jnp.int32, (TH, TH), 1 if transposed else 0)
    si = lax.broadcasted_iota(jnp.int32, (TH, TH), 0 if transposed else 1)
    return (si <= ti) & ((si // CHUNK) == (ti // CHUNK))


def _hgrn_fwd(hq, hf, hi, lbr, tri):
    def body(hq_ref, hf_ref, hi_ref, lbr_ref, tri_ref, rec_ref, sall_ref, st_scr):
        @pl.when(pl.program_id(0) == 0)
        def _():
            st_scr[...] = jnp.zeros_like(st_scr)

        w = _hgrn_common(hq_ref, hf_ref, lbr_ref, tri_ref)
        qd, ki, ke = w["qd"].astype(BF16), w["ki"].astype(BF16), w["ke"].astype(BF16)
        dec = w["dec"]
        vb = hi_ref[...]
        causal = _tri_mask()
        for h in range(4):
            cs = slice(h * 128, (h + 1) * 128)
            att = jnp.where(causal, _mm_nt(qd[:, cs], ki[:, cs]), 0.0)
            o_intra = _mm(att.astype(BF16), vb[:, cs])
            for c in range(NCH):
                rs = slice(c * CHUNK, (c + 1) * CHUNK)
                st = st_scr[:, cs]
                sall_ref[c, :, cs] = st
                rec_ref[rs, cs] = (o_intra[rs] + _mm_nt(qd[rs, cs], st.astype(BF16))).astype(BF16)
                st_scr[:, cs] = dec[c * CHUNK:c * CHUNK + 1, cs] * st + _mm_tn(vb[rs, cs], ke[rs, cs])

    tok = pl.BlockSpec((TH, HW), lambda i: (i, 0))
    return pl.pallas_call(
        body, name="hgrn_fwd", grid=(T // TH,),
        in_specs=[tok, tok, tok, pl.BlockSpec((2, HW), lambda i: (0, 0)), pl.BlockSpec((TH, TH), lambda i: (0, 0))],
        out_specs=[tok, pl.BlockSpec((NCH, 128, HW), lambda i: (i, 0, 0))],
        out_shape=[jax.ShapeDtypeStruct((T, HW), BF16), jax.ShapeDtypeStruct((T // CHUNK, 128, HW), F32)],
        scratch_shapes=[pltpu.VMEM((128, HW), F32)],
        compiler_params=_cp(("arbitrary",)),
    )(hq, hf, hi, lbr, tri)


def _hgrn_bwd(hq, hf, hi, lbr, tri, trit, drec, sall, dhg, rout, routb):
    NT = T // TH

    def body(hq_ref, hf_ref, hi_ref, lbr_ref, tri_ref, trit_ref, do_ref, sall_ref, dhg_ref, rout_r, routb_r,
             dph_ref, small_ref, pout_o, poutr_o,
             dst_scr, dlb_scr, dqd_scr, dki_scr, dke_scr, dlast_scr, send_sems, recv_sems, loc_sems):
        step = pl.program_id(0)
        loc, rem = _chip_copies(_w_out_piece, rout_r, routb_r, pout_o, poutr_o, send_sems, recv_sems,
                                loc_sems.at[0])

        @pl.when(step == 0)
        def _():
            dst_scr[...] = jnp.zeros_like(dst_scr)
            dlb_scr[...] = jnp.zeros_like(dlb_scr)
            for cp in loc + rem:
                cp.start()

        w = _hgrn_common(hq_ref, hf_ref, lbr_ref, tri_ref)
        qd, ki, ke = w["qd"].astype(BF16), w["ki"].astype(BF16), w["ke"].astype(BF16)
        dec = w["dec"]
        vb = hi_ref[...]
        dob = do_ref[...].astype(BF16)
        causal = _tri_mask()
        causal_t = _tri_mask(transposed=True)
        for h in range(4):
            cs = slice(h * 128, (h + 1) * 128)
            att_t = jnp.where(causal_t, _mm_nt(ki[:, cs], qd[:, cs]), 0.0).astype(BF16)
            datt_t = jnp.where(causal_t, _mm_nt(vb[:, cs], dob[:, cs]), 0.0).astype(BF16)
            datt = jnp.where(causal, _mm_nt(dob[:, cs], vb[:, cs]), 0.0).astype(BF16)
            dv_intra = _mm(att_t, dob[:, cs])
            dqd_intra = _mm(datt, ki[:, cs])
            dki_scr[:, cs] = _mm(datt_t, qd[:, cs])
            for c in reversed(range(NCH)):
                rs = slice(c * CHUNK, (c + 1) * CHUNK)
                dec_c = dec[c * CHUNK:c * CHUNK + 1, :]
                st = sall_ref[c, :, cs]
                dst = dst_scr[:, cs]
                dstb = dst.astype(BF16)
                dph_ref[rs, 2 * HW + h * 128:2 * HW + (h + 1) * 128] = (
                    dv_intra[rs] + _mm_nt(ke[rs, cs], dstb)).astype(BF16)
                dqd_scr[rs, cs] = dqd_intra[rs] + _mm(dob[rs, cs], st.astype(BF16))
                dke_scr[rs, cs] = _mm(vb[rs, cs], dstb)
                ddec = jnp.sum(dst * st, axis=0, keepdims=True)
                dlast_scr[c:c + 1, cs] = ddec * dec_c[:, cs]
                dst_scr[:, cs] = dec_c[:, cs] * dst + _mm_tn(dob[rs, cs], qd[rs, cs])
        dqd, dki, dke = dqd_scr[...], dki_scr[...], dke_scr[...]
        dq = dqd * w["ea"]
        dk = dki * w["ena"] + dke * w["eend"]
        dcum = dqd * w["qd"] - dki * w["ki"] - dke * w["ke"]
        dkeke = dke * w["ke"]
        dlastb = jnp.concatenate(
            [jnp.broadcast_to(dlast_scr[c:c + 1, :] + jnp.sum(dkeke[c * CHUNK:(c + 1) * CHUNK], axis=0, keepdims=True),
                              (CHUNK, HW)) for c in range(NCH)], axis=0)
        dg = _mm_exact_l(trit_ref[...], dcum) + dlastb
        df = dg / w["f"] - dk
        lb, sf, sq = w["lb"], w["sf"], w["sq"]
        dph_ref[:, HW:2 * HW] = (df * (1.0 - lb) * sf * (1.0 - sf)).astype(BF16)
        dph_ref[:, 0:HW] = (dq * (sq * (1.0 + w["hq"] * (1.0 - sq)))).astype(BF16)
        dph_ref[:, 3 * HW:4 * HW] = dhg_ref[...]
        dlb_scr[...] += jnp.sum(df * (1.0 - sf), axis=0, keepdims=True)

        @pl.when(step == NT - 1)
        def _():
            gr = dlb_scr[...] * lb * (1.0 - lb)
            small_ref[...] = jnp.zeros_like(small_ref)
            small_ref[0:1, 0:HW] = gr
            small_ref[1:2, 0:HW] = -gr
            for cp in rem:
                cp.wait_recv()
            for cp in rem:
                cp.wait_send()
            for cp in loc:
                cp.wait()

    tok = pl.BlockSpec((TH, HW), lambda i: (NT - 1 - i, 0))
    const = lambda shape: pl.BlockSpec(shape, lambda i: (0,) * len(shape))
    hbm = pl.BlockSpec(memory_space=pltpu.HBM)
    return pl.pallas_call(
        body, name="hgrn_bwd", grid=(NT,),
        in_specs=[tok, tok, tok, const((2, HW)), const((TH, TH)), const((TH, TH)), tok,
                  pl.BlockSpec((NCH, 128, HW), lambda i: (NT - 1 - i, 0, 0)), tok, hbm, hbm],
        out_specs=[pl.BlockSpec((TH, NCOL // 2), lambda i: (NT - 1 - i, 0)), const((8, D)), hbm, hbm],
        out_shape=[jax.ShapeDtypeStruct((T, NCOL // 2), BF16), jax.ShapeDtypeStruct((8, D), F32),
                   jax.ShapeDtypeStruct((128, D), F32), jax.ShapeDtypeStruct((3, 128, D), BF16)],
        scratch_shapes=[pltpu.VMEM((128, HW), F32), pltpu.VMEM((1, HW), F32), pltpu.VMEM((TH, HW), F32),
                        pltpu.VMEM((TH, HW), F32), pltpu.VMEM((TH, HW), F32), pltpu.VMEM((8, HW), F32),
                        pltpu.SemaphoreType.DMA((3,)), pltpu.SemaphoreType.DMA((3,)), pltpu.SemaphoreType.DMA((1,))],
        compiler_params=_cp(("arbitrary",)),
    )(hq, hf, hi, lbr, tri, trit, drec, sall, dhg, rout, routb)


def _fwd_out(o1, o4, o16, l1, l4, l16, rec, ag, hg, x, tgt, anw, hnw, fnw, wout_full, gmat, emat, selmat):
    TT = 256

    def body(o1_r, o4_r, o16_r, l1_r, l4_r, l16_r, rec_r, ag_r, hg_r, x_r, tgt_r, anw_r, hnw_r, fnw_r, wo_r, g_r,
             e_r, sel_r, dx2_o, do1_o, do4_o, do16_o, st1_o, st4_o, st16_o, drec_o, dag_o, dhg_o,
             rout_o, routb_o, small_o, scr_a, scr_b, gwout_o, rbuf, send_sems, recv_sems):
        @pl.when(pl.program_id(0) == 0)
        def _():
            gwout_o[...] = jnp.zeros_like(gwout_o)
            small_o[...] = jnp.zeros_like(small_o)

        def unperm(r4, r16):
            return _unperm_load(r4, r16, scr_a, scr_b)

        def perm_out(val, p1, p4, p16, dt):
            _perm_store(val, scr_a, p1, p4, p16, dt)

        o4u, o16u = unperm(o4_r, o16_r)
        l4c, l16c = unperm(l4_r, l16_r)
        em = e_r[...]
        l1v, l4u, l16u = _mm_exact_r(l1_r[...], em), _mm_exact_r(l4c, em), _mm_exact_r(l16c, em)
        o1v = o1_r[...].astype(F32)
        mx = jnp.maximum(jnp.maximum(l1v, l4u), l16u)
        w1, w4, w16 = jnp.exp(l1v - mx), jnp.exp(l4u - mx), jnp.exp(l16u - mx)
        den = w1 + w4 + w16
        attn = (w1 * o1v + w4 * o4u + w16 * o16u) / den
        lse = mx + jnp.log(den)
        gm = g_r[...]

        def head_mean_a(t):
            return jnp.concatenate([_mm_exact_r(t[:, :256], gm), _mm_exact_r(t[:, 256:], gm)], axis=1)

        def head_mean_h(t):
            return jnp.concatenate(
                [jnp.broadcast_to(jnp.mean(t[:, h * 128:(h + 1) * 128], axis=-1, keepdims=True), (TT, 128))
                 for h in range(4)], axis=1)

        rs_a = lax.rsqrt(head_mean_a(attn * attn) + EPS)
        n_a = attn * rs_a
        agv = ag_r[...].astype(F32)
        sg_a = _sigmoid(agv)
        si_a = agv * sg_a
        anw_v = anw_r[...]
        y_a = (n_a * anw_v) * si_a
        recv = rec_r[...].astype(F32)
        rs_h = lax.rsqrt(head_mean_h(recv * recv) + EPS)
        n_h = recv * rs_h
        hgv = hg_r[...].astype(F32)
        sg_h = _sigmoid(hgv)
        si_h = hgv * sg_h
        hnw_v = hnw_r[...]
        y_h = (n_h * hnw_v) * si_h
        mixed = jnp.concatenate([y_a, y_h], axis=1).astype(BF16)
        xv = x_r[...]
        x2 = xv + _mm(mixed, wo_r[...])
        r2 = lax.rsqrt(jnp.mean(x2 * x2, axis=-1, keepdims=True) + EPS)
        fnw_v = fnw_r[...]
        xn = x2 * r2
        err = xn * fnw_v - tgt_r[...]
        small_o[2:3, :] += 0.5 * jnp.sum(jnp.mean(err * err, axis=-1, keepdims=True), axis=0, keepdims=True)
        dy = err * (1.0 / D)
        small_o[0:1, :] += jnp.sum(dy * xn, axis=0, keepdims=True)
        dyw = dy * fnw_v
        dx2 = r2 * dyw - x2 * ((r2 * r2 * r2) * jnp.mean(dyw * x2, axis=-1, keepdims=True))
        dx2_o[...] = dx2
        dx2b = dx2.astype(BF16)
        gwout_o[...] += _mm_tn(mixed, dx2b)
        dmix = _mm_nt(dx2b, wo_r[...])
        dm_a, dm_h = dmix[:, :AW], dmix[:, AW:]
        dag_o[...] = (dm_a * (n_a * anw_v) * (sg_a * (1.0 + agv * (1.0 - sg_a)))).astype(BF16)
        dn_a = dm_a * anw_v * si_a
        small_o[1:2, 0:AW] += jnp.sum(dm_a * n_a * si_a, axis=0, keepdims=True)
        dattn = rs_a * (dn_a - n_a * head_mean_a(dn_a * n_a))
        delta = head_mean_a(dattn * attn) * float(HEAD)
        perm_out(dattn, do1_o, do4_o, do16_o, BF16)
        stats = _mm_exact_r(lse, sel_r[0]) + _mm_exact_r(delta, sel_r[1])
        perm_out(stats, st1_o, st4_o, st16_o, F32)
        dhg_o[...] = (dm_h * (n_h * hnw_v) * (sg_h * (1.0 + hgv * (1.0 - sg_h)))).astype(BF16)
        dn_h = dm_h * hnw_v * si_h
        small_o[1:2, AW:] += jnp.sum(dm_h * n_h * si_h, axis=0, keepdims=True)
        drec_o[...] = (rs_h * (dn_h - n_h * head_mean_h(dn_h * n_h))).astype(BF16)

        @pl.when(pl.program_id(0) == T // TT - 1)
        def _():
            x, y, c = lax.axis_index("x"), lax.axis_index("y"), lax.axis_index("c")
            cps = [pltpu.make_async_remote_copy(
                src_ref=gwout_o.at[pl.ds(pl.multiple_of(j * 256 + (1 - c) * 128, 128), 128), :], dst_ref=rbuf.at[j],
                send_sem=send_sems.at[j], recv_sem=recv_sems.at[j], device_id=(x, y, 1 - c), device_id_type=MESH)
                for j in range(4)]
            for cp in cps:
                cp.start()
            for j, cp in enumerate(cps):
                cp.wait_recv()
                red = gwout_o[pl.ds(pl.multiple_of(j * 256 + c * 128, 128), 128), :] + rbuf[j]
                rout_o[j * 128:(j + 1) * 128, :] = red
                routb_o[j * 128:(j + 1) * 128, :] = red.astype(BF16)
            for cp in cps:
                cp.wait_send()

    tok = lambda w: pl.BlockSpec((TT, w), lambda i: (i, 0))
    d4 = pl.BlockSpec((4, TT // 4, AW), lambda i: (0, i, 0))
    d16 = pl.BlockSpec((16, TT // 16, AW), lambda i: (0, i, 0))
    const = lambda shape: pl.BlockSpec(shape, lambda i: (0,) * len(shape))
    sd = lambda shape, dt: jax.ShapeDtypeStruct(shape, dt)
    c4 = pl.BlockSpec((4, TT // 4, 128), lambda i: (0, i, 0))
    c16 = pl.BlockSpec((16, TT // 16, 128), lambda i: (0, i, 0))
    p3 = lambda w, dt: [sd((T, w), dt), sd((4, T // 4, w), dt), sd((16, T // 16, w), dt)]
    return pl.pallas_call(
        body, name="fwd_out", grid=(T // TT,),
        in_specs=[tok(AW), d4, d16, tok(128), c4, c16, tok(AW), tok(AW), tok(AW), tok(D), tok(D),
                  const((1, AW)), const((1, HW)), const((1, D)), const((D, D)), const((256, 256)),
                  const((128, AW)), const((2, AW, 128))],
        out_specs=[tok(D)] + [tok(AW), d4, d16] + [tok(128), c4, c16] + [tok(AW)] * 3
        + [const((512, D)), const((512, D)), const((8, D))],
        out_shape=[sd((T, D), F32)] + p3(AW, BF16) + p3(128, F32)
        + [sd((T, AW), BF16), sd((T, AW), BF16), sd((T, AW), BF16), sd((512, D), F32), sd((512, D), BF16),
           sd((8, D), F32)],
        scratch_shapes=[pltpu.VMEM((4, TT, 128), F32), pltpu.VMEM((4, TT, 128), F32), pltpu.VMEM((D, D), F32),
                        pltpu.VMEM((4, 128, D), F32), pltpu.SemaphoreType.DMA((4,)), pltpu.SemaphoreType.DMA((4,))],
        compiler_params=_cp(("arbitrary",)),
    )(o1, o4, o16, l1, l4, l16, rec, ag, hg, x, tgt, anw, hnw, fnw, wout_full, gmat, emat, selmat)


def _dproj_build(dq, dk, dv, dag, pos):
    TT = 256

    def body(dq1, dq4, dq16, dk1, dk4, dk16, dv1, dv4, dv16, dag_r, pos_r, dproj_o, scr_a, scr_b):
        def unperm_sum(r1, r4, r16):
            u4, u16 = _unperm_load(r4, r16, scr_a, scr_b)
            return r1[...] + u4 + u16

        cosf, s1, s2 = _rope_tables(pos_r[...])
        dproj_o[:, 0:512] = _rope_bwd(unperm_sum(dq1, dq4, dq16), cosf, s1, s2).astype(BF16)
        dproj_o[:, 512:1024] = _rope_bwd(unperm_sum(dk1, dk4, dk16), cosf, s1, s2).astype(BF16)
        dproj_o[:, 1024:1536] = unperm_sum(dv1, dv4, dv16).astype(BF16)
        dproj_o[:, 1536:2048] = dag_r[...]

    tok = lambda w: pl.BlockSpec((TT, w), lambda i: (i, 0))
    d4 = pl.BlockSpec((4, TT // 4, AW), lambda i: (0, i, 0))
    d16 = pl.BlockSpec((16, TT // 16, AW), lambda i: (0, i, 0))
    return pl.pallas_call(
        body, name="dproj_build", grid=(T // TT,),
        in_specs=[tok(AW), d4, d16] * 3 + [tok(AW), tok(1)],
        out_specs=tok(NCOL // 2),
        out_shape=jax.ShapeDtypeStruct((T, NCOL // 2), BF16),
        scratch_shapes=[pltpu.VMEM((4, TT, 128), F32), pltpu.VMEM((4, TT, 128), F32)],
        compiler_params=_cp(("parallel",)),
    )(*dq, *dk, *dv, dag, pos)


def _bwd_x(dproj_a, dproj_h, x, dx2, mixw, w_full, rin, rinb, small4, small6):
    TT = 256
    NT = T // TT

    def body(dpa_r, dph_r, x_r, dx2_r, mw_r, w_r, rin_r, rinb_r, s4_r, s6_r,
             gx_o, pin_o, pinr_o, sall_o, sbuf, send_sems, recv_sems, loc_sems):
        i = pl.program_id(0)
        loc, rem = _chip_copies(_w_in_piece, rin_r, rinb_r, pin_o, pinr_o, send_sems, recv_sems, loc_sems.at[0])

        @pl.when(i == 0)
        def _():
            sbuf[...] = jnp.zeros_like(sbuf)
            for cp in loc + rem:
                cp.start()

        dhn = _mm_nt(dpa_r[...], w_r[:, 0:NCOL // 2]) + _mm_nt(dph_r[...], w_r[:, NCOL // 2:NCOL])
        xv = x_r[...]
        r = lax.rsqrt(jnp.mean(xv * xv, axis=-1, keepdims=True) + EPS)
        dxw = dhn * mw_r[...]
        gx_o[...] = dx2_r[...] + r * dxw - xv * ((r * r * r) * jnp.mean(dxw * xv, axis=-1, keepdims=True))
        sbuf[16:17, :] += jnp.sum(dhn * (xv * r), axis=0, keepdims=True)

        @pl.when(i == NT - 1)
        def _():
            sbuf[0:8, :] = s4_r[...]
            sbuf[8:16, :] = s6_r[...]
            sloc, srem = _small_copies(sbuf, sall_o, send_sems, recv_sems, loc_sems.at[1])
            for cp in sloc + srem:
                cp.start()
            for cp in rem + srem:
                cp.wait_recv()
            for cp in rem + srem:
                cp.wait_send()
            for cp in loc + sloc:
                cp.wait()

    tok = lambda w: pl.BlockSpec((TT, w), lambda i: (i, 0))
    const = lambda shape: pl.BlockSpec(shape, lambda i: (0,) * len(shape))
    hbm = pl.BlockSpec(memory_space=pltpu.HBM)
    return pl.pallas_call(
        body, name="bwd_x", grid=(NT,),
        in_specs=[tok(NCOL // 2), tok(NCOL // 2), tok(D), tok(D), const((1, D)), const((D, NCOL)), hbm, hbm,
                  const((8, D)), const((8, D))],
        out_specs=[tok(D), hbm, hbm, hbm],
        out_shape=[jax.ShapeDtypeStruct((T, D), F32),
                   jax.ShapeDtypeStruct((512, 1024), F32), jax.ShapeDtypeStruct((3, 512, 1024), BF16),
                   jax.ShapeDtypeStruct((8, 24, D), F32)],
        scratch_shapes=[pltpu.VMEM((24, D), F32), pltpu.SemaphoreType.DMA((10,)), pltpu.SemaphoreType.DMA((10,)),
                        pltpu.SemaphoreType.DMA((2,))],
        compiler_params=_cp(("arbitrary",)),
    )(dproj_a, dproj_h, x, dx2, mixw, w_full, rin, rinb, small4, small6)


def _grad_w_in(hn, dproj_a, dproj_h):
    TK = 1024
    NK = T // TK

    def body(hnt_r, dpa_r, dph_r, rin_o, rinb_o, acc, rbuf, obuf, obufb, send_sems, recv_sems, wb_sems):
        j = pl.program_id(0)
        kk = pl.program_id(1)
        x, y, c = lax.axis_index("x"), lax.axis_index("y"), lax.axis_index("c")
        mine = pl.ds(pl.multiple_of(c * 512, 512), 512)
        theirs = pl.ds(pl.multiple_of((1 - c) * 512, 512), 512)

        def send(jj):
            return pltpu.make_async_remote_copy(
                src_ref=acc.at[jj % 2, theirs, :], dst_ref=rbuf.at[jj], send_sem=send_sems.at[jj],
                recv_sem=recv_sems.at[jj], device_id=(x, y, 1 - c), device_id_type=MESH)

        def writeback(jj):
            cols = pl.ds(jj * 1024, 1024)
            return [pltpu.make_async_copy(obuf.at[jj % 2], rin_o.at[:, cols], wb_sems.at[jj % 2]),
                    pltpu.make_async_copy(obufb.at[jj % 2], rinb_o.at[:, cols], wb_sems.at[2 + jj % 2])]

        def wait_writeback(jj):
            for cp in writeback(jj):
                cp.wait()

        def finalize(jj):
            send(jj).wait_recv()
            red = acc[jj % 2, mine, :] + rbuf[jj]
            obuf[jj % 2] = red
            obufb[jj % 2] = red.astype(BF16)
            for cp in writeback(jj):
                cp.start()

        prod = _mm(hnt_r[...], jnp.where(j < 2, dpa_r[...], dph_r[...]))

        @pl.when(kk == 0)
        def _():
            for jj in (2, 3):
                @pl.when(j == jj)
                def _():
                    send(jj - 2).wait_send()
            acc[j % 2] = prod

        @pl.when(kk > 0)
        def _():
            acc[j % 2] += prod

        @pl.when(kk == NK - 1)
        def _():
            for jj in range(4):
                @pl.when(j == jj)
                def _():
                    send(jj).start()
                    if jj in (1, 2):
                        finalize(jj - 1)
                    if jj == 3:
                        wait_writeback(0)
                        finalize(2)
                        wait_writeback(1)
                        finalize(3)
                        wait_writeback(2)
                        wait_writeback(3)
                        send(2).wait_send()
                        send(3).wait_send()

    hbm = pl.BlockSpec(memory_space=pltpu.HBM)
    return pl.pallas_call(
        body, name="grad_w_in", grid=(4, NK),
        in_specs=[pl.BlockSpec((D, TK), lambda j, kk: (0, kk)),
                  pl.BlockSpec((TK, 1024), lambda j, kk: (jnp.where(j < 2, kk, NK - 1), jnp.minimum(j, 1))),
                  pl.BlockSpec((TK, 1024), lambda j, kk: (jnp.where(j < 2, 0, kk), jnp.maximum(j - 2, 0)))],
        out_specs=[hbm, hbm],
        out_shape=[jax.ShapeDtypeStruct((512, NCOL), F32), jax.ShapeDtypeStruct((512, NCOL), BF16)],
        scratch_shapes=[pltpu.VMEM((2, D, 1024), F32), pltpu.VMEM((4, 512, 1024), F32), pltpu.VMEM((2, 512, 1024), F32),
                        pltpu.VMEM((2, 512, 1024), BF16),
                        pltpu.SemaphoreType.DMA((4,)), pltpu.SemaphoreType.DMA((4,)), pltpu.SemaphoreType.DMA((4,))],
        compiler_params=_cp(("arbitrary", "arbitrary")),
    )(hn, dproj_a, dproj_h)


def _w_in_piece(ref, j):
    return ref.at[:, pl.ds(j * 1024, 1024)]


def _w_out_piece(ref, j):
    return ref.at[pl.ds(j * 128, 128), :]


def _chip_copies(piece, src_r, srcb_r, own_o, rem_o, send_sems, recv_sems, loc_sem):
    x, y, c = lax.axis_index("x"), lax.axis_index("y"), lax.axis_index("c")
    chips = [(1 - x, y), (x, 1 - y), (1 - x, 1 - y)]
    loc = [pltpu.make_async_copy(piece(src_r, 2 * x + y), own_o, loc_sem)]
    rem = [pltpu.make_async_remote_copy(
        src_ref=piece(srcb_r, 2 * px + py), dst_ref=rem_o.at[k], send_sem=send_sems.at[k],
        recv_sem=recv_sems.at[k], device_id=(px, py, c), device_id_type=MESH) for k, (px, py) in enumerate(chips)]
    return loc, rem


def _small_copies(small_r, sall_o, send_sems, recv_sems, loc_sem):
    x, y, c = lax.axis_index("x"), lax.axis_index("y"), lax.axis_index("c")
    me = 4 * x + 2 * y + c
    loc = [pltpu.make_async_copy(small_r, sall_o.at[me], loc_sem)]
    rem = []
    k = 3
    for fx in range(2):
        for fy in range(2):
            for fc in range(2):
                if fx or fy or fc:
                    peer = (1 - x if fx else x, 1 - y if fy else y, 1 - c if fc else c)
                    rem.append(pltpu.make_async_remote_copy(
                        src_ref=small_r, dst_ref=sall_o.at[me], send_sem=send_sems.at[k],
                        recv_sem=recv_sems.at[k], device_id=peer, device_id_type=MESH))
                    k += 1
    return loc, rem


def _pair_share(pin_own, pin_rem, pout_own, pout_rem):
    def body(pio_r, pir_r, poo_r, por_r, fin_o, fout_o, sin, sout, send_sems, recv_sems):
        x, y, c = lax.axis_index("x"), lax.axis_index("y"), lax.axis_index("c")
        sibling = (x, y, 1 - c)
        sout[...] = ((poo_r[...] + por_r[0].astype(F32)) + por_r[1].astype(F32)) + por_r[2].astype(F32)
        sin[...] = ((pio_r[...] + pir_r[0].astype(F32)) + pir_r[1].astype(F32)) + pir_r[2].astype(F32)
        rem = [pltpu.make_async_remote_copy(src_ref=sin, dst_ref=fin_o.at[c], send_sem=send_sems.at[0],
                                            recv_sem=recv_sems.at[0], device_id=sibling, device_id_type=MESH),
               pltpu.make_async_remote_copy(src_ref=sout, dst_ref=fout_o.at[c], send_sem=send_sems.at[1],
                                            recv_sem=recv_sems.at[1], device_id=sibling, device_id_type=MESH)]
        for cp in rem:
            cp.start()
        fin_o[c] = sin[...]
        fout_o[c] = sout[...]
        for cp in rem:
            cp.wait_recv()
        for cp in rem:
            cp.wait_send()

    vm = pl.BlockSpec(memory_space=pltpu.VMEM)
    return pl.pallas_call(
        body, name="pair_share",
        out_shape=(jax.ShapeDtypeStruct((2, 512, 1024), F32), jax.ShapeDtypeStruct((2, 128, D), F32)),
        in_specs=[vm, vm, vm, vm], out_specs=(vm, vm),
        scratch_shapes=[pltpu.VMEM((512, 1024), F32), pltpu.VMEM((128, D), F32),
                        pltpu.SemaphoreType.DMA((2,)), pltpu.SemaphoreType.DMA((2,))],
        compiler_params=_cp(),
    )(pin_own, pin_rem, pout_own, pout_rem)


def _adamw_math(w, g, m, v):
    m = B1 * m + (1.0 - B1) * g
    v = B2 * v + (1.0 - B2) * (g * g)
    m_hat = m / (1.0 - B1 ** STEP)
    v_hat = v / (1.0 - B2 ** STEP)
    delta = -LR * (m_hat / (jnp.sqrt(v_hat) + AEPS) + WD * w)
    return delta, m, v


def _adamw(big_in, big_out, sall, params):
    def body(*refs):
        wi, gi, mi, vi, wo, go, mo, vo, sall_r = refs[:9]
        ins = refs[9:24]
        di_o, mi_o, vi_o, do_o, mo_o, vo_o = refs[24:30]
        outs = refs[30:]
        d, mm, vv = _adamw_math(wi[...], gi[...], mi[...], vi[...])
        di_o[...] = d
        mi_o[...] = mm
        vi_o[...] = vv

        @pl.when(pl.program_id(0) == 0)
        def _():
            d, mm, vv = _adamw_math(wo[...], go[...], mo[...], vo[...])
            do_o[...] = d
            mo_o[...] = mm
            vo_o[...] = vv
            tot = sall_r[0]
            for dv in range(1, 8):
                tot = tot + sall_r[dv]
            grads = [tot[16:17, :], tot[1:2, 0:AW], tot[1:2, AW:], tot[8:10, 0:HW], tot[0:1, :]]
            outs[0][...] = tot[2:3, 0:1]
            for p in range(5):
                w_r, m_r, v_r = ins[3 * p:3 * p + 3]
                g = grads[p]
                d, mm, vv = _adamw_math(w_r[...], g, m_r[...], v_r[...])
                outs[1 + 4 * p][...] = g
                outs[2 + 4 * p][...] = d
                outs[3 + 4 * p][...] = mm
                outs[4 + 4 * p][...] = vv

    flat = [a for p in params for a in p]
    shapes = [jax.ShapeDtypeStruct((D, 1024), F32)] * 3 + [jax.ShapeDtypeStruct((256, D), F32)] * 3
    shapes += [jax.ShapeDtypeStruct((1, 1), F32)]
    for p in params:
        shapes += [jax.ShapeDtypeStruct(p[0].shape, F32)] * 4
    vm = pl.BlockSpec(memory_space=pltpu.VMEM)
    rows = pl.BlockSpec((256, 1024), lambda i: (i, 0))
    whole = pl.BlockSpec((256, D), lambda i: (0, 0))
    return pl.pallas_call(
        body, name="adamw", grid=(4,),
        in_specs=[rows] * 4 + [whole] * 4 + [vm] * 16, out_specs=[rows] * 3 + [whole] * 3 + [vm] * 21,
        out_shape=shapes,
        compiler_params=_cp(("arbitrary",)),
    )(*big_in, *big_out, sall, *flat)


def kernel(x, positions, w_in, w_out, mix_norm_w, attn_out_norm_w, hgrn_out_norm_w, hgrn_lb_raw, final_norm_w, loss_target, m_w_in, m_w_out, m_mix_norm_w, m_attn_out_norm_w, m_hgrn_out_norm_w, m_hgrn_lb_raw, m_final_norm_w, v_w_in, v_w_out, v_mix_norm_w, v_attn_out_norm_w, v_hgrn_out_norm_w, v_hgrn_lb_raw, v_final_norm_w):
    xs = x.reshape(T, D)
    tgt = loss_target.reshape(T, D)
    pos = positions.reshape(T, 1)
    fnw = final_norm_w.reshape(1, D)

    ti = np.arange(TH)
    tri_np = ((ti[:, None] // CHUNK == ti[None, :] // CHUNK) & (ti[None, :] <= ti[:, None])).astype(np.float32)
    tri = jnp.asarray(tri_np, BF16)
    trit = jnp.asarray(tri_np.T, BF16)
    hi_ = np.arange(AW) // HEAD
    gmat = jnp.asarray((hi_[:256, None] == hi_[None, :256]).astype(np.float32) / HEAD, BF16)
    emat_np = (np.arange(128)[:, None] == hi_[None, :]).astype(np.float32)
    sel_np = np.zeros((2, AW, 128), np.float32)
    sel_np[0, np.arange(8) * HEAD, np.arange(8)] = 1.0
    sel_np[1, np.arange(8) * HEAD, 8 + np.arange(8)] = 1.0
    emat = jnp.asarray(emat_np, BF16)
    selmat = jnp.asarray(sel_np, BF16)

    wb_in, wb_out = _cast_weights(w_in.reshape(D, 1024), w_out.reshape(256, D))
    jm_arr = (2 * lax.axis_index("x") + lax.axis_index("y")).astype(jnp.int32).reshape(1)
    (hn, q1, k1, v1, q4, k4, v4, q16, k16, v16, ag, hq, hf, hi, hg, w_full, wout4) = _fwd_in(
        xs, pos, mix_norm_w, wb_in, wb_out, jm_arr)
    wout_full = wout4.reshape(D, D)
    flat = lambda a: a.reshape(T, AW)
    o1, l1 = _attn_fwd(q1, k1, v1, T // BLK, "attn_fwd_d1")
    o4, l4 = _attn_fwd(flat(q4), flat(k4), flat(v4), T // 4 // BLK, "attn_fwd_d4")
    o16, l16 = _attn_fwd(flat(q16), flat(k16), flat(v16), T // 16 // BLK, "attn_fwd_d16")
    rec, sall = _hgrn_fwd(hq, hf, hi, hgrn_lb_raw, tri)

    (dx2, do1, do4, do16, st1, st4, st16, drec, dag, dhg, rout, routb, small4) = _fwd_out(
        o1, o4.reshape(4, T // 4, AW), o16.reshape(16, T // 16, AW),
        l1, l4.reshape(4, T // 4, 128), l16.reshape(16, T // 16, 128),
        rec, ag, hg, xs, tgt, attn_out_norm_w, hgrn_out_norm_w, fnw, wout_full, gmat, emat, selmat)

    fst = lambda a: a.reshape(T, 128)
    dq1, dk1, dv1 = _attn_bwd(q1, k1, v1, do1, st1, T // BLK, "attn_bwd_d1")
    dq4, dk4, dv4 = _attn_bwd(flat(q4), flat(k4), flat(v4), flat(do4), fst(st4), T // 4 // BLK, "attn_bwd_d4")
    dq16, dk16, dv16 = _attn_bwd(flat(q16), flat(k16), flat(v16), flat(do16), fst(st16), T // 16 // BLK,
                                 "attn_bwd_d16")
    dproj_h, small6, pout_own, pout_rem = _hgrn_bwd(hq, hf, hi, hgrn_lb_raw, tri, trit, drec, sall, dhg,
                                                    rout, routb)

    r4 = lambda a: a.reshape(4, T // 4, AW)
    r16 = lambda a: a.reshape(16, T // 16, AW)
    dproj_a = _dproj_build((dq1, r4(dq4), r16(dq16)), (dk1, r4(dk4), r16(dk16)), (dv1, r4(dv4), r16(dv16)),
                           dag, pos)
    rin, rinb = _grad_w_in(hn, dproj_a, dproj_h)
    gx, pin_own, pin_rem, small_all = _bwd_x(dproj_a, dproj_h, xs, dx2, mix_norm_w, w_full, rin, rinb,
                                             small4, small6)
    fin, fout = _pair_share(pin_own, pin_rem, pout_own, pout_rem)
    g_w_in = fin.reshape(D, 1024)
    g_w_out = fout.reshape(256, D)

    params = [(mix_norm_w, m_mix_norm_w, v_mix_norm_w),
              (attn_out_norm_w, m_attn_out_norm_w, v_attn_out_norm_w),
              (hgrn_out_norm_w, m_hgrn_out_norm_w, v_hgrn_out_norm_w),
              (hgrn_lb_raw, m_hgrn_lb_raw, v_hgrn_lb_raw),
              (fnw, m_final_norm_w.reshape(1, D), v_final_norm_w.reshape(1, D))]
    d_in, nm_in, nv_in, d_out, nm_out, nv_out, *so = _adamw(
        (w_in.reshape(D, 1024), g_w_in, m_w_in.reshape(D, 1024), v_w_in.reshape(D, 1024)),
        (w_out.reshape(256, D), g_w_out, m_w_out.reshape(256, D), v_w_out.reshape(256, D)), small_all, params)
    loss = so[0].reshape(())
    g_s = [so[1 + 4 * p] for p in range(5)]
    d_s = [so[2 + 4 * p] for p in range(5)]
    m_s = [so[3 + 4 * p] for p in range(5)]
    v_s = [so[4 + 4 * p] for p in range(5)]
    for lst in (g_s, d_s, m_s, v_s):
        lst[4] = lst[4].reshape(D)

    return (loss, gx.reshape(1, T, D),
            g_w_in.reshape(1, D, 1024), g_w_out.reshape(1, 256, D), *g_s,
            d_in.reshape(1, D, 1024), d_out.reshape(1, 256, D), *d_s,
            nm_in.reshape(1, D, 1024), nm_out.reshape(1, 256, D), *m_s,
            nv_in.reshape(1, D, 1024), nv_out.reshape(1, 256, D), *v_s)
```

```python
import functools

import numpy as np
import jax
import jax.numpy as jnp
from jax import lax
from jax.experimental import pallas as pl
from jax.experimental.pallas import tpu as pltpu

F32 = jnp.float32
BF16 = jnp.bfloat16

T = 4096
D = 1024
AW = 512
HW = 512
NCOL = 4096
HEAD = 64
BLK = 128
CHUNK = 64
EPS = 1e-6
SCALE = HEAD ** -0.5
NEG = -1e30
ROPE_THETA = 500000.0
INV_FREQ = [float(v) for v in
            (np.float32(ROPE_THETA) ** (-(np.arange(8, dtype=np.float32)) * np.float32(0.125)))]
LR, B1, B2, AEPS, WD, STEP = 0.001, 0.9, 0.999, 1e-08, 0.01, 10
VMEM_LIMIT = 56 * 1024 * 1024
MESH = pl.DeviceIdType.MESH


def _cp(sem=None, **kw):
    return pltpu.CompilerParams(dimension_semantics=sem, vmem_limit_bytes=VMEM_LIMIT, **kw)


def _mm(a, b):
    return jnp.dot(a, b, preferred_element_type=F32)


def _mm_nt(a, b):
    return lax.dot_general(a, b, (((1,), (1,)), ((), ())), preferred_element_type=F32)


def _mm_tn(a, b):
    return lax.dot_general(a, b, (((0,), (0,)), ((), ())), preferred_element_type=F32)


def _split3(x):
    h = x.astype(BF16)
    r = x - h.astype(F32)
    m = r.astype(BF16)
    l = (r - m.astype(F32)).astype(BF16)
    return h, m, l


def _mm_exact_l(mat_bf, x):
    h, m, l = _split3(x)
    return _mm(mat_bf, h) + _mm(mat_bf, m) + _mm(mat_bf, l)


def _mm_exact_r(x, mat_bf):
    h = x.astype(BF16)
    l = (x - h.astype(F32)).astype(BF16)
    return _mm(h, mat_bf) + _mm(l, mat_bf)


def _sigmoid(x):
    return 0.5 * jnp.tanh(0.5 * x) + 0.5


def _rope_tables(pos):
    lane = lax.broadcasted_iota(jnp.int32, (1, 128), 1)
    jl = lane & 63
    fi = jl & 7
    inv = jnp.zeros((1, 128), F32)
    for kk in range(8):
        inv = jnp.where(fi == kk, INV_FREQ[kk], inv)
    ang = pos.astype(F32) * inv
    c = jnp.cos(ang)
    s = jnp.sin(ang)
    cosf = jnp.where(jl < 16, c, 1.0)
    s1 = jnp.where(jl < 8, -s, 0.0)
    s2 = jnp.where((jl >= 8) & (jl < 16), s, 0.0)
    return cosf, s1, s2


def _rope(t, cosf, s1, s2):
    parts = []
    for ci in range(t.shape[1] // 128):
        tc = t[:, ci * 128:(ci + 1) * 128]
        parts.append(tc * cosf + pltpu.roll(tc, 120, 1) * s1 + pltpu.roll(tc, 8, 1) * s2)
    return jnp.concatenate(parts, axis=1)


def _rope_bwd(g, cosf, s1, s2):
    parts = []
    for ci in range(g.shape[1] // 128):
        gc = g[:, ci * 128:(ci + 1) * 128]
        parts.append(gc * cosf + pltpu.roll(gc * s1, 8, 1) + pltpu.roll(gc * s2, 120, 1))
    return jnp.concatenate(parts, axis=1)


def _perm_store(val, scr, scr2, o1, o4, o16, dt):
    n = val.shape[0]
    q = n // 4
    o1[...] = val.astype(dt)
    for ci in range(val.shape[1] // 128):
        cs = slice(ci * 128, (ci + 1) * 128)
        scr[ci] = val[:, cs]
        for r4 in range(4):
            part = scr[ci, pl.ds(r4, q, stride=4), :]
            o4[r4, :, cs] = part.astype(dt)
            scr2[ci, r4 * q:(r4 + 1) * q, :] = part
        for r4 in range(4):
            for b in range(4):
                o16[r4 + 4 * b, :, cs] = scr2[ci, pl.ds(r4 * q + b, q // 4, stride=4), :].astype(dt)


def _unperm_load(r4, r16, scr_a, scr_b, scr_c):
    n = scr_a.shape[1]
    q = n // 4
    nc = r4.shape[-1] // 128
    for ci in range(nc):
        cs = slice(ci * 128, (ci + 1) * 128)
        for rr in range(4):
            scr_a[ci, pl.ds(rr, q, stride=4), :] = r4[rr, :, cs].astype(F32)
        for rr in range(4):
            for b in range(4):
                scr_c[ci, pl.ds(rr * q + b, q // 4, stride=4), :] = r16[rr + 4 * b, :, cs].astype(F32)
        for rr in range(4):
            scr_b[ci, pl.ds(rr, q, stride=4), :] = scr_c[ci, rr * q:(rr + 1) * q, :]
    return (jnp.concatenate([scr_a[ci] for ci in range(nc)], axis=1),
            jnp.concatenate([scr_b[ci] for ci in range(nc)], axis=1))


def _cast_weights(w_in, w_out):
    def body(win_ref, wout_ref, bin_ref, bout_ref):
        bin_ref[...] = win_ref[...].astype(BF16)

        @pl.when(pl.program_id(0) == 0)
        def _():
            bout_ref[...] = wout_ref[...].astype(BF16)

    return pl.pallas_call(
        body, name="cast_weights", grid=(4,),
        in_specs=[pl.BlockSpec((256, 1024), lambda i: (i, 0)), pl.BlockSpec((256, D), lambda i: (0, 0))],
        out_specs=[pl.BlockSpec((256, 1024), lambda i: (i, 0)), pl.BlockSpec((256, D), lambda i: (0, 0))],
        out_shape=(jax.ShapeDtypeStruct((D, 1024), BF16), jax.ShapeDtypeStruct((256, D), BF16)),
        compiler_params=_cp(("arbitrary",)),
    )(w_in, w_out)


def _fwd_in(x, pos, mixw, wb_in, wb_out, jm_arr):
    TT = 512
    NT = T // TT

    def body(jm_ref, x_ref, pos_ref, mw_ref, wbin_ref, wbout_ref,
             hnt_ref, q1, k1, v1, q4, k4, v4, q16, k16, v16, ag, hq, hf, hi, hg, wfull_o, woutfull_o,
             wbuf, wobuf, hn_all, scr, scr2, send_sems, recv_sems, loc_sems):
        s = pl.program_id(0)
        i = pl.program_id(1)
        mx, my, c = lax.axis_index("x"), lax.axis_index("y"), lax.axis_index("c")
        me, sibling = (mx, my, c), (mx, my, 1 - c)
        chips = [(mx, 1 - my), (1 - mx, my), (1 - mx, 1 - my)]
        jm = 2 * mx + my
        rows_in = [pl.ds(pl.multiple_of(h * 512, 512), 512) for h in (c, 1 - c)]
        rows_out = [pl.ds(pl.multiple_of(h * 128, 128), 128) for h in (c, 1 - c)]

        def blk(k):
            return lax.bitwise_xor(jm, k + 1)

        def rc(n, ref, to):
            return pltpu.make_async_remote_copy(src_ref=ref, dst_ref=ref, send_sem=send_sems.at[n],
                                                recv_sem=recv_sems.at[n], device_id=to, device_id_type=MESH)

        send_in = lambda k: rc(k, wbuf.at[jm, rows_in[0], :], (*chips[k], c))
        send_out = lambda k: rc(3 + k, wobuf.at[jm, rows_out[0], :], (*chips[k], c))
        got_in = lambda k: rc(k, wbuf.at[blk(k), rows_in[0], :], me)
        got_out = lambda k: rc(3 + k, wobuf.at[blk(k), rows_out[0], :], me)
        pass_in = lambda k: rc(6 + k, wbuf.at[blk(k), rows_in[0], :], sibling)
        pass_out = lambda k: rc(9 + k, wobuf.at[blk(k), rows_out[0], :], sibling)
        passed_in = lambda k: rc(6 + k, wbuf.at[blk(k), rows_in[1], :], me)
        passed_out = lambda k: rc(9 + k, wobuf.at[blk(k), rows_out[1], :], me)

        def keep(j, n):
            return pltpu.make_async_copy(wbuf.at[j], wfull_o.at[:, pl.ds(j * 1024, 1024)], loc_sems.at[n])

        @pl.when((s == 0) & (i == 0))
        def _():
            own = [pltpu.make_async_copy(wbin_ref, wbuf.at[jm], loc_sems.at[4]),
                   pltpu.make_async_copy(wbout_ref, wobuf.at[jm], loc_sems.at[5])]
            for cp in own:
                cp.start()
            for cp in own:
                cp.wait()
            send_in(0).start()
            send_in(1).start()
            keep(jm, 0).start()

        def arrive(k):
            if k == 0:
                send_in(0).wait_send()
                send_in(1).wait_send()
                send_in(2).start()
            got_in(k).wait_recv()
            pass_in(k).start()
            passed_in(k).wait_recv()
            keep(blk(k), k + 1).start()
            if k == 2:
                for kk in range(3):
                    send_out(kk).start()

        for k in range(3):
            pl.when((s == k + 1) & (i == 0))(functools.partial(arrive, k))

        tile = pl.ds(pl.multiple_of(i * TT, TT), TT)

        @pl.when(s == 0)
        def _():
            xv = x_ref[...]
            r = lax.rsqrt(jnp.mean(xv * xv, axis=-1, keepdims=True) + EPS)
            hnf = (xv * r) * mw_ref[...]
            hn_all[tile, :] = hnf.astype(BF16)
            hnt_ref[...] = hnf.T.astype(BF16)

        def project(jj):
            hn = hn_all[tile, :]
            lo = _mm(hn, wbuf[jj, :, 0:512])
            hi_cols = _mm(hn, wbuf[jj, :, 512:1024])
            if jj == 0:
                cosf, s1, s2 = _rope_tables(pos_ref[...])
                _perm_store(_rope(lo, cosf, s1, s2), scr, scr2, q1, q4, q16, BF16)
                _perm_store(_rope(hi_cols, cosf, s1, s2), scr, scr2, k1, k4, k16, BF16)
            elif jj == 1:
                _perm_store(lo, scr, scr2, v1, v4, v16, BF16)
                ag[...] = hi_cols.astype(BF16)
            elif jj == 2:
                hq[...] = lo.astype(BF16)
                hf[...] = hi_cols.astype(BF16)
            else:
                hi[...] = lo.astype(BF16)
                hg[...] = hi_cols.astype(BF16)

        j = lax.bitwise_xor(jm, s)
        for jj in range(4):
            pl.when(j == jj)(functools.partial(project, jj))

        @pl.when((s == 3) & (i == NT - 1))
        def _():
            for k in range(3):
                got_out(k).wait_recv()
                pass_out(k).start()
            for k in range(3):
                passed_out(k).wait_recv()
            out = pltpu.make_async_copy(wobuf, woutfull_o, loc_sems.at[4])
            out.start()
            send_in(2).wait_send()
            for k in range(3):
                send_out(k).wait_send()
                pass_in(k).wait_send()
                pass_out(k).wait_send()
            keep(jm, 0).wait()
            for k in range(3):
                keep(blk(k), k + 1).wait()
            out.wait()

    def at_stage_of(jb):
        def index(s, i, jm_ref):
            sa = lax.bitwise_xor(jm_ref[0], jb)
            return jnp.where(s < sa, 0, jnp.where(s == sa, i, NT - 1))
        return index

    tok = lambda w, jb: pl.BlockSpec((TT, w), lambda s, i, jm_ref: (at_stage_of(jb)(s, i, jm_ref), 0))
    d4 = lambda jb: pl.BlockSpec((4, TT // 4, AW), lambda s, i, jm_ref: (0, at_stage_of(jb)(s, i, jm_ref), 0))
    d16 = lambda jb: pl.BlockSpec((16, TT // 16, AW), lambda s, i, jm_ref: (0, at_stage_of(jb)(s, i, jm_ref), 0))
    hbm = pl.BlockSpec(memory_space=pltpu.HBM)
    sd = lambda shape, dt: jax.ShapeDtypeStruct(shape, dt)
    in_own_stage = lambda s, i: jnp.where(s == 0, i, NT - 1)
    grid_spec = pltpu.PrefetchScalarGridSpec(
        num_scalar_prefetch=1, grid=(4, NT),
        in_specs=[pl.BlockSpec((TT, D), lambda s, i, jm_ref: (in_own_stage(s, i), 0)),
                  pl.BlockSpec((TT, 1), lambda s, i, jm_ref: (i, 0)),
                  pl.BlockSpec((1, D), lambda s, i, jm_ref: (0, 0)), hbm, hbm],
        out_specs=[pl.BlockSpec((D, TT), lambda s, i, jm_ref: (0, in_own_stage(s, i))),
                   tok(AW, 0), tok(AW, 0), tok(AW, 1), d4(0), d4(0), d4(1), d16(0), d16(0), d16(1),
                   tok(AW, 1), tok(AW, 2), tok(AW, 2), tok(AW, 3), tok(AW, 3), hbm, hbm],
        scratch_shapes=[pltpu.VMEM((4, D, 1024), BF16), pltpu.VMEM((4, 256, D), BF16), pltpu.VMEM((T, D), BF16),
                        pltpu.VMEM((4, TT, 128), F32), pltpu.VMEM((4, TT, 128), F32), pltpu.SemaphoreType.DMA((12,)),
                        pltpu.SemaphoreType.DMA((12,)), pltpu.SemaphoreType.DMA((6,))])
    return pl.pallas_call(
        body, name="fwd_in", grid_spec=grid_spec,
        out_shape=[sd((D, T), BF16)] + [sd((T, AW), BF16)] * 3 + [sd((4, T // 4, AW), BF16)] * 3
        + [sd((16, T // 16, AW), BF16)] * 3
        + [sd((T, AW), BF16)] * 5 + [sd((D, NCOL), BF16), sd((4, 256, D), BF16)],
        compiler_params=_cp(("arbitrary", "arbitrary")),
    )(jm_arr, x, pos, mixw, wb_in, wb_out)


def _band_mask(key_axis, nkeys=2 * BLK):
    shape = (nkeys, 2 * BLK) if key_axis == 0 else (2 * BLK, nkeys)
    kj = lax.broadcasted_iota(jnp.int32, shape, key_axis)
    qi = lax.broadcasted_iota(jnp.int32, shape, 1 - key_axis) & (BLK - 1)
    return (kj >= qi) & (kj <= qi + BLK), kj, qi


def _stack_heads(t2, in_a):
    z = jnp.zeros_like(t2)
    return jnp.concatenate([jnp.where(in_a[0], t2, z), jnp.where(in_a[1], t2, z)], axis=0)


def _attn_fwd(q, k, v, nb, name):
    n = min(4, nb)
    CH = n * BLK
    halo = nb > n

    def body(*refs):
        if halo:
            q_ref, k_ref, v_ref, kp_ref, vp_ref, o_ref, lse_ref = refs
        else:
            q_ref, k_ref, v_ref, o_ref, lse_ref = refs
        lane = lax.broadcasted_iota(jnp.int32, (1, 128), 1)
        in_a = [lane < HEAD, lane >= HEAD]
        band, kj, _ = _band_mask(1)
        thr0 = jnp.where((n * pl.program_id(0)) % nb == 0, BLK, 0) if halo else BLK
        mask0 = band & (kj >= thr0)
        for b in range(n):
            rs = slice(b * BLK, (b + 1) * BLK)
            stat = jnp.zeros((BLK, 128), F32)
            for hp in range(4):
                cs = slice(hp * 128, (hp + 1) * 128)
                q2s = _stack_heads(q_ref[rs, cs], in_a)
                if b == 0:
                    kprev = kp_ref[:, cs] if halo else k_ref[rs, cs]
                    vprev = vp_ref[:, cs] if halo else v_ref[rs, cs]
                    kk = jnp.concatenate([kprev, k_ref[rs, cs]], axis=0)
                    vv = jnp.concatenate([vprev, v_ref[rs, cs]], axis=0)
                    mask = mask0
                else:
                    kk = k_ref[(b - 1) * BLK:(b + 1) * BLK, cs]
                    vv = v_ref[(b - 1) * BLK:(b + 1) * BLK, cs]
                    mask = band
                s = jnp.where(mask, _mm_nt(q2s, kk) * SCALE, NEG)
                m = jnp.max(s, axis=-1, keepdims=True)
                p = jnp.exp(s - m)
                l = jnp.sum(p, axis=-1, keepdims=True)
                o = _mm(p.astype(BF16), vv) / l
                lse = m + jnp.log(l)
                o_ref[rs, cs] = jnp.where(in_a[0], o[:BLK], o[BLK:]).astype(BF16)
                stat = jnp.where(lane == 2 * hp, lse[:BLK], stat)
                stat = jnp.where(lane == 2 * hp + 1, lse[BLK:], stat)
            lse_ref[rs, :] = stat

    cur = pl.BlockSpec((CH, AW), lambda i: (i, 0))
    prev = pl.BlockSpec((BLK, AW), lambda i: (jnp.maximum(n * i - 1, 0), 0))
    return pl.pallas_call(
        body, name=name, grid=(T // CH,),
        in_specs=[cur, cur, cur] + ([prev, prev] if halo else []),
        out_specs=[cur, pl.BlockSpec((CH, 128), lambda i: (i, 0))],
        out_shape=[jax.ShapeDtypeStruct((T, AW), BF16), jax.ShapeDtypeStruct((T, 128), F32)],
        compiler_params=_cp(("parallel",)),
    )(*((q, k, v) + ((k, v) if halo else ())))


def _attn_bwd(q, k, v, do, st, nb, name):
    n = min(4, nb)
    CH = n * BLK
    NBLK = T // BLK
    halo = nb > n

    def body(*refs):
        if halo:
            (q_ref, k_ref, v_ref, do_ref, st_ref, kp_ref, vp_ref, qn_ref, don_ref, stn_ref,
             dq_ref, dk_ref, dv_ref) = refs
        else:
            q_ref, k_ref, v_ref, do_ref, st_ref, dq_ref, dk_ref, dv_ref = refs
        i = pl.program_id(0)
        lane = lax.broadcasted_iota(jnp.int32, (1, 128), 1)
        in_a = [lane < HEAD, lane >= HEAD]
        band, kj, _ = _band_mask(0)
        thr0 = jnp.where((n * i) % nb == 0, BLK, 0) if halo else BLK
        mask0 = band & (kj >= thr0)

        def stat_rows(st_t, hp):
            lse_r = jnp.concatenate([st_t[2 * hp:2 * hp + 1, :], st_t[2 * hp + 1:2 * hp + 2, :]], axis=1)
            dl_r = jnp.concatenate([st_t[8 + 2 * hp:9 + 2 * hp, :], st_t[9 + 2 * hp:10 + 2 * hp, :]], axis=1)
            return lse_r, dl_r

        st_t = [st_ref[b * BLK:(b + 1) * BLK, :].T for b in range(n)]
        if halo:
            nxt_thr = jnp.where((n * i + n) % nb == 0, 2 * BLK, 0)
            _, kj1, qi1 = _band_mask(0, BLK)
            mask_next = kj1 >= qi1 + nxt_thr
            stn_t = stn_ref[...].T

        for hp in range(4):
            cs = slice(hp * 128, (hp + 1) * 128)
            kb = [k_ref[b * BLK:(b + 1) * BLK, cs] for b in range(n)]
            vb = [v_ref[b * BLK:(b + 1) * BLK, cs] for b in range(n)]
            dk_acc = [jnp.zeros((BLK, 128), F32) for _ in range(n)]
            dv_acc = [jnp.zeros((BLK, 128), F32) for _ in range(n)]
            for b in range(n):
                rs = slice(b * BLK, (b + 1) * BLK)
                q2s = _stack_heads(q_ref[rs, cs], in_a)
                do2s = _stack_heads(do_ref[rs, cs], in_a)
                if b == 0:
                    kprev = kp_ref[:, cs] if halo else kb[0]
                    vprev = vp_ref[:, cs] if halo else vb[0]
                    mask = mask0
                else:
                    kprev, vprev, mask = kb[b - 1], vb[b - 1], band
                kk = jnp.concatenate([kprev, kb[b]], axis=0)
                vv = jnp.concatenate([vprev, vb[b]], axis=0)
                lse_r, dl_r = stat_rows(st_t[b], hp)
                s_t = jnp.where(mask, _mm_nt(kk, q2s) * SCALE, NEG)
                p_t = jnp.exp(s_t - lse_r)
                ds_t = (p_t * (_mm_nt(vv, do2s) - dl_r)).astype(BF16)
                dkk = _mm(ds_t, q2s) * SCALE
                dvv = _mm(p_t.astype(BF16), do2s)
                dqs = _mm_tn(ds_t, kk) * SCALE
                dq_ref[rs, cs] = jnp.where(in_a[0], dqs[:BLK], dqs[BLK:]).astype(BF16)
                dk_acc[b] += dkk[BLK:]
                dv_acc[b] += dvv[BLK:]
                if b > 0:
                    dk_acc[b - 1] += dkk[:BLK]
                    dv_acc[b - 1] += dvv[:BLK]
            if halo:
                q2s = _stack_heads(qn_ref[:, cs], in_a)
                do2s = _stack_heads(don_ref[:, cs], in_a)
                lse_r, dl_r = stat_rows(stn_t, hp)
                s_t = jnp.where(mask_next, _mm_nt(kb[n - 1], q2s) * SCALE, NEG)
                p_t = jnp.exp(s_t - lse_r)
                ds_t = (p_t * (_mm_nt(vb[n - 1], do2s) - dl_r)).astype(BF16)
                dk_acc[n - 1] += _mm(ds_t, q2s) * SCALE
                dv_acc[n - 1] += _mm(p_t.astype(BF16), do2s)
            for b in range(n):
                dk_ref[b * BLK:(b + 1) * BLK, cs] = dk_acc[b].astype(BF16)
                dv_ref[b * BLK:(b + 1) * BLK, cs] = dv_acc[b].astype(BF16)

    cur = pl.BlockSpec((CH, AW), lambda i: (i, 0))
    cur_st = pl.BlockSpec((CH, 128), lambda i: (i, 0))
    prev = pl.BlockSpec((BLK, AW), lambda i: (jnp.maximum(n * i - 1, 0), 0))
    nxt = pl.BlockSpec((BLK, AW), lambda i: (jnp.minimum(n * i + n, NBLK - 1), 0))
    nxt_st = pl.BlockSpec((BLK, 128), lambda i: (jnp.minimum(n * i + n, NBLK - 1), 0))
    ins = [cur] * 4 + [cur_st] + ([prev, prev, nxt, nxt, nxt_st] if halo else [])
    args = (q, k, v, do, st) + ((k, v, q, do, st) if halo else ())
    return pl.pallas_call(
        body, name=name, grid=(T // CH,),
        in_specs=ins,
        out_specs=[cur] * 3,
        out_shape=[jax.ShapeDtypeStruct((T, AW), BF16)] * 3,
        compiler_params=_cp(("parallel",)),
    )(*args)


TH = 256
NCH = TH // CHUNK


def _hgrn_common(hq_ref, hf_ref, lbr_ref, tri_ref):
    r0 = lbr_ref[0:1, :]
    r1 = lbr_ref[1:2, :]
    mx = jnp.maximum(r0, r1)
    e0 = jnp.exp(r0 - mx)
    e1 = jnp.exp(r1 - mx)
    lb = e0 / (e0 + e1)
    hqv = hq_ref[...].astype(F32)
    sq = _sigmoid(hqv)
    qv = hqv * sq
    sf = _sigmoid(hf_ref[...].astype(F32))
    f = lb + (1.0 - lb) * sf
    kv = 1.0 - f
    g = jnp.log(f)
    cum = _mm_exact_l(tri_ref[...], g)
    lastb = jnp.concatenate(
        [jnp.broadcast_to(cum[c * CHUNK + CHUNK - 1:(c + 1) * CHUNK, :], (CHUNK, HW)) for c in range(NCH)], axis=0)
    ea = jnp.exp(cum)
    ena = jnp.exp(-cum)
    eend = jnp.exp(lastb - cum)
    return dict(lb=lb, hq=hqv, sq=sq, q=qv, sf=sf, f=f, k=kv, cum=cum, lastb=lastb, ea=ea, ena=ena, eend=eend,
                qd=qv * ea, ki=kv * ena, ke=kv * eend, dec=jnp.exp(lastb))


def _tri_mask(transposed=False):
    ti = lax.broadcasted_iota(jnp.int32, (TH, TH), 1 if transposed else 0)
    si = lax.broadcasted_iota(jnp.int32, (TH, TH), 0 if transposed else 1)
    return (si <= ti) & ((si // CHUNK) == (ti // CHUNK))


def _hgrn_fwd(hq, hf, hi, lbr, tri):
    def body(hq_ref, hf_ref, hi_ref, lbr_ref, tri_ref, rec_ref, sall_ref, st_scr):
        @pl.when(pl.program_id(0) == 0)
        def _():
            st_scr[...] = jnp.zeros_like(st_scr)

        w = _hgrn_common(hq_ref, hf_ref, lbr_ref, tri_ref)
        qd, ki, ke = w["qd"].astype(BF16), w["ki"].astype(BF16), w["ke"].astype(BF16)
        dec = w["dec"]
        vb = hi_ref[...]
        causal = _tri_mask()
        for h in range(4):
            cs = slice(h * 128, (h + 1) * 128)
            att = jnp.where(causal, _mm_nt(qd[:, cs], ki[:, cs]), 0.0)
            o_intra = _mm(att.astype(BF16), vb[:, cs])
            for c in range(NCH):
                rs = slice(c * CHUNK, (c + 1) * CHUNK)
                st = st_scr[:, cs]
                sall_ref[c, :, cs] = st
                rec_ref[rs, cs] = (o_intra[rs] + _mm_nt(qd[rs, cs], st.astype(BF16))).astype(BF16)
                st_scr[:, cs] = dec[c * CHUNK:c * CHUNK + 1, cs] * st + _mm_tn(vb[rs, cs], ke[rs, cs])

    tok = pl.BlockSpec((TH, HW), lambda i: (i, 0))
    return pl.pallas_call(
        body, name="hgrn_fwd", grid=(T // TH,),
        in_specs=[tok, tok, tok, pl.BlockSpec((2, HW), lambda i: (0, 0)), pl.BlockSpec((TH, TH), lambda i: (0, 0))],
        out_specs=[tok, pl.BlockSpec((NCH, 128, HW), lambda i: (i, 0, 0))],
        out_shape=[jax.ShapeDtypeStruct((T, HW), BF16), jax.ShapeDtypeStruct((T // CHUNK, 128, HW), F32)],
        scratch_shapes=[pltpu.VMEM((128, HW), F32)],
        compiler_params=_cp(("arbitrary",)),
    )(hq, hf, hi, lbr, tri)


def _hgrn_bwd(hq, hf, hi, lbr, tri, trit, drec, sall, dhg, rout, routb):
    NT = T // TH

    def body(hq_ref, hf_ref, hi_ref, lbr_ref, tri_ref, trit_ref, do_ref, sall_ref, dhg_ref, rout_r, routb_r,
             dph_ref, small_ref, pout_o, poutr_o,
             dst_scr, dlb_scr, dqd_scr, dki_scr, dke_scr, dlast_scr, send_sems, recv_sems, loc_sems):
        step = pl.program_id(0)
        loc, rem = _chip_copies(_w_out_piece, rout_r, routb_r, pout_o, poutr_o, send_sems, recv_sems,
                                loc_sems.at[0])

        @pl.when(step == 0)
        def _():
            dst_scr[...] = jnp.zeros_like(dst_scr)
            dlb_scr[...] = jnp.zeros_like(dlb_scr)
            for cp in loc + rem:
                cp.start()

        w = _hgrn_common(hq_ref, hf_ref, lbr_ref, tri_ref)
        qd, ki, ke = w["qd"].astype(BF16), w["ki"].astype(BF16), w["ke"].astype(BF16)
        dec = w["dec"]
        vb = hi_ref[...]
        dob = do_ref[...].astype(BF16)
        causal = _tri_mask()
        causal_t = _tri_mask(transposed=True)
        for h in range(4):
            cs = slice(h * 128, (h + 1) * 128)
            att_t = jnp.where(causal_t, _mm_nt(ki[:, cs], qd[:, cs]), 0.0).astype(BF16)
            datt_t = jnp.where(causal_t, _mm_nt(vb[:, cs], dob[:, cs]), 0.0).astype(BF16)
            datt = jnp.where(causal, _mm_nt(dob[:, cs], vb[:, cs]), 0.0).astype(BF16)
            dv_intra = _mm(att_t, dob[:, cs])
            dqd_intra = _mm(datt, ki[:, cs])
            dki_scr[:, cs] = _mm(datt_t, qd[:, cs])
            for c in reversed(range(NCH)):
                rs = slice(c * CHUNK, (c + 1) * CHUNK)
                dec_c = dec[c * CHUNK:c * CHUNK + 1, :]
                st = sall_ref[c, :, cs]
                dst = dst_scr[:, cs]
                dstb = dst.astype(BF16)
                dph_ref[rs, 2 * HW + h * 128:2 * HW + (h + 1) * 128] = (
                    dv_intra[rs] + _mm_nt(ke[rs, cs], dstb)).astype(BF16)
                dqd_scr[rs, cs] = dqd_intra[rs] + _mm(dob[rs, cs], st.astype(BF16))
                dke_scr[rs, cs] = _mm(vb[rs, cs], dstb)
                ddec = jnp.sum(dst * st, axis=0, keepdims=True)
                dlast_scr[c:c + 1, cs] = ddec * dec_c[:, cs]
                dst_scr[:, cs] = dec_c[:, cs] * dst + _mm_tn(dob[rs, cs], qd[rs, cs])
        dqd, dki, dke = dqd_scr[...], dki_scr[...], dke_scr[...]
        dq = dqd * w["ea"]
        dk = dki * w["ena"] + dke * w["eend"]
        dcum = dqd * w["qd"] - dki * w["ki"] - dke * w["ke"]
        dkeke = dke * w["ke"]
        dlastb = jnp.concatenate(
            [jnp.broadcast_to(dlast_scr[c:c + 1, :] + jnp.sum(dkeke[c * CHUNK:(c + 1) * CHUNK], axis=0, keepdims=True),
                              (CHUNK, HW)) for c in range(NCH)], axis=0)
        dg = _mm_exact_l(trit_ref[...], dcum) + dlastb
        df = dg / w["f"] - dk
        lb, sf, sq = w["lb"], w["sf"], w["sq"]
        dph_ref[:, HW:2 * HW] = (df * (1.0 - lb) * sf * (1.0 - sf)).astype(BF16)
        dph_ref[:, 0:HW] = (dq * (sq * (1.0 + w["hq"] * (1.0 - sq)))).astype(BF16)
        dph_ref[:, 3 * HW:4 * HW] = dhg_ref[...]
        dlb_scr[...] += jnp.sum(df * (1.0 - sf), axis=0, keepdims=True)

        @pl.when(step == NT - 1)
        def _():
            gr = dlb_scr[...] * lb * (1.0 - lb)
            small_ref[...] = jnp.zeros_like(small_ref)
            small_ref[0:1, 0:HW] = gr
            small_ref[1:2, 0:HW] = -gr
            for cp in rem:
                cp.wait_recv()
            for cp in rem:
                cp.wait_send()
            for cp in loc:
                cp.wait()

    tok = pl.BlockSpec((TH, HW), lambda i: (NT - 1 - i, 0))
    const = lambda shape: pl.BlockSpec(shape, lambda i: (0,) * len(shape))
    hbm = pl.BlockSpec(memory_space=pltpu.HBM)
    return pl.pallas_call(
        body, name="hgrn_bwd", grid=(NT,),
        in_specs=[tok, tok, tok, const((2, HW)), const((TH, TH)), const((TH, TH)), tok,
                  pl.BlockSpec((NCH, 128, HW), lambda i: (NT - 1 - i, 0, 0)), tok, hbm, hbm],
        out_specs=[pl.BlockSpec((TH, NCOL // 2), lambda i: (NT - 1 - i, 0)), const((8, D)), hbm, hbm],
        out_shape=[jax.ShapeDtypeStruct((T, NCOL // 2), BF16), jax.ShapeDtypeStruct((8, D), F32),
                   jax.ShapeDtypeStruct((128, D), F32), jax.ShapeDtypeStruct((3, 128, D), BF16)],
        scratch_shapes=[pltpu.VMEM((128, HW), F32), pltpu.VMEM((1, HW), F32), pltpu.VMEM((TH, HW), F32),
                        pltpu.VMEM((TH, HW), F32), pltpu.VMEM((TH, HW), F32), pltpu.VMEM((8, HW), F32),
                        pltpu.SemaphoreType.DMA((3,)), pltpu.SemaphoreType.DMA((3,)), pltpu.SemaphoreType.DMA((1,))],
        compiler_params=_cp(("arbitrary",)),
    )(hq, hf, hi, lbr, tri, trit, drec, sall, dhg, rout, routb)


def _fwd_out(o1, o4, o16, l1, l4, l16, rec, ag, hg, x, tgt, anw, hnw, fnw, wout_full, gmat, emat, selmat):
    TT = 256

    def body(o1_r, o4_r, o16_r, l1_r, l4_r, l16_r, rec_r, ag_r, hg_r, x_r, tgt_r, anw_r, hnw_r, fnw_r, wo_r, g_r,
             e_r, sel_r, dx2_o, do1_o, do4_o, do16_o, st1_o, st4_o, st16_o, drec_o, dag_o, dhg_o,
             rout_o, routb_o, small_o, scr_a, scr_b, scr_c, gwout_o, rbuf, send_sems, recv_sems):
        @pl.when(pl.program_id(0) == 0)
        def _():
            gwout_o[...] = jnp.zeros_like(gwout_o)
            small_o[...] = jnp.zeros_like(small_o)

        def unperm(r4, r16):
            return _unperm_load(r4, r16, scr_a, scr_b, scr_c)

        def perm_out(val, p1, p4, p16, dt):
            _perm_store(val, scr_a, scr_b, p1, p4, p16, dt)

        o4u, o16u = unperm(o4_r, o16_r)
        l4c, l16c = unperm(l4_r, l16_r)
        l1c = l1_r[...]
        mxc = jnp.maximum(jnp.maximum(l1c, l4c), l16c)
        w1c, w4c, w16c = jnp.exp(l1c - mxc), jnp.exp(l4c - mxc), jnp.exp(l16c - mxc)
        denc = w1c + w4c + w16c
        lane = lax.broadcasted_iota(jnp.int32, (1, 128), 1)
        lse_c = jnp.where(lane < 8, mxc + jnp.log(denc), 0.0)
        em = e_r[...]
        wn1 = _mm_exact_r(w1c / denc, em)
        wn4 = _mm_exact_r(w4c / denc, em)
        o1v = o1_r[...].astype(F32)
        attn = wn1 * o1v + wn4 * o4u + (1.0 - wn1 - wn4) * o16u
        gm = g_r[...]

        def head_mean_a(t):
            return jnp.concatenate([_mm_exact_r(t[:, :256], gm), _mm_exact_r(t[:, 256:], gm)], axis=1)

        def head_mean_h(t):
            return jnp.concatenate(
                [jnp.broadcast_to(jnp.mean(t[:, h * 128:(h + 1) * 128], axis=-1, keepdims=True), (TT, 128))
                 for h in range(4)], axis=1)

        rs_a = lax.rsqrt(head_mean_a(attn * attn) + EPS)
        n_a = attn * rs_a
        agv = ag_r[...].astype(F32)
        sg_a = _sigmoid(agv)
        si_a = agv * sg_a
        anw_v = anw_r[...]
        y_a = (n_a * anw_v) * si_a
        recv = rec_r[...].astype(F32)
        rs_h = lax.rsqrt(head_mean_h(recv * recv) + EPS)
        n_h = recv * rs_h
        hgv = hg_r[...].astype(F32)
        sg_h = _sigmoid(hgv)
        si_h = hgv * sg_h
        hnw_v = hnw_r[...]
        y_h = (n_h * hnw_v) * si_h
        mixed = jnp.concatenate([y_a, y_h], axis=1).astype(BF16)
        xv = x_r[...]
        x2 = xv + _mm(mixed, wo_r[...])
        r2 = lax.rsqrt(jnp.mean(x2 * x2, axis=-1, keepdims=True) + EPS)
        fnw_v = fnw_r[...]
        xn = x2 * r2
        err = xn * fnw_v - tgt_r[...]
        small_o[2:3, :] += 0.5 * jnp.sum(jnp.mean(err * err, axis=-1, keepdims=True), axis=0, keepdims=True)
        dy = err * (1.0 / D)
        small_o[0:1, :] += jnp.sum(dy * xn, axis=0, keepdims=True)
        dyw = dy * fnw_v
        dx2 = r2 * dyw - x2 * ((r2 * r2 * r2) * jnp.mean(dyw * x2, axis=-1, keepdims=True))
        dx2_o[...] = dx2
        dx2b = dx2.astype(BF16)
        gwout_o[...] += _mm_tn(mixed, dx2b)
        dmix = _mm_nt(dx2b, wo_r[...])
        dm_a, dm_h = dmix[:, :AW], dmix[:, AW:]
        dag_o[...] = (dm_a * (n_a * anw_v) * (sg_a * (1.0 + agv * (1.0 - sg_a)))).astype(BF16)
        dn_a = dm_a * anw_v * si_a
        small_o[1:2, 0:AW] += jnp.sum(dm_a * n_a * si_a, axis=0, keepdims=True)
        dattn = rs_a * (dn_a - n_a * head_mean_a(dn_a * n_a))
        perm_out(dattn, do1_o, do4_o, do16_o, BF16)
        stats = lse_c + _mm_exact_r(dattn * attn, sel_r[...])
        perm_out(stats, st1_o, st4_o, st16_o, F32)
        dhg_o[...] = (dm_h * (n_h * hnw_v) * (sg_h * (1.0 + hgv * (1.0 - sg_h)))).astype(BF16)
        dn_h = dm_h * hnw_v * si_h
        small_o[1:2, AW:] += jnp.sum(dm_h * n_h * si_h, axis=0, keepdims=True)
        drec_o[...] = (rs_h * (dn_h - n_h * head_mean_h(dn_h * n_h))).astype(BF16)

        @pl.when(pl.program_id(0) == T // TT - 1)
        def _():
            x, y, c = lax.axis_index("x"), lax.axis_index("y"), lax.axis_index("c")
            cps = [pltpu.make_async_remote_copy(
                src_ref=gwout_o.at[pl.ds(pl.multiple_of(j * 256 + (1 - c) * 128, 128), 128), :], dst_ref=rbuf.at[j],
                send_sem=send_sems.at[j], recv_sem=recv_sems.at[j], device_id=(x, y, 1 - c), device_id_type=MESH)
                for j in range(4)]
            for cp in cps:
                cp.start()
            for j, cp in enumerate(cps):
                cp.wait_recv()
                red = gwout_o[pl.ds(pl.multiple_of(j * 256 + c * 128, 128), 128), :] + rbuf[j]
                rout_o[j * 128:(j + 1) * 128, :] = red
                routb_o[j * 128:(j + 1) * 128, :] = red.astype(BF16)
            for cp in cps:
                cp.wait_send()

    tok = lambda w: pl.BlockSpec((TT, w), lambda i: (i, 0))
    d4 = pl.BlockSpec((4, TT // 4, AW), lambda i: (0, i, 0))
    d16 = pl.BlockSpec((16, TT // 16, AW), lambda i: (0, i, 0))
    const = lambda shape: pl.BlockSpec(shape, lambda i: (0,) * len(shape))
    sd = lambda shape, dt: jax.ShapeDtypeStruct(shape, dt)
    c4 = pl.BlockSpec((4, TT // 4, 128), lambda i: (0, i, 0))
    c16 = pl.BlockSpec((16, TT // 16, 128), lambda i: (0, i, 0))
    p3 = lambda w, dt: [sd((T, w), dt), sd((4, T // 4, w), dt), sd((16, T // 16, w), dt)]
    return pl.pallas_call(
        body, name="fwd_out", grid=(T // TT,),
        in_specs=[tok(AW), d4, d16, tok(128), c4, c16, tok(AW), tok(AW), tok(AW), tok(D), tok(D),
                  const((1, AW)), const((1, HW)), const((1, D)), const((D, D)), const((256, 256)),
                  const((128, AW)), const((AW, 128))],
        out_specs=[tok(D)] + [tok(AW), d4, d16] + [tok(128), c4, c16] + [tok(AW)] * 3
        + [const((512, D)), const((512, D)), const((8, D))],
        out_shape=[sd((T, D), F32)] + p3(AW, BF16) + p3(128, F32)
        + [sd((T, AW), BF16), sd((T, AW), BF16), sd((T, AW), BF16), sd((512, D), F32), sd((512, D), BF16),
           sd((8, D), F32)],
        scratch_shapes=[pltpu.VMEM((4, TT, 128), F32)] * 3 + [pltpu.VMEM((D, D), F32),
                        pltpu.VMEM((4, 128, D), F32), pltpu.SemaphoreType.DMA((4,)), pltpu.SemaphoreType.DMA((4,))],
        compiler_params=_cp(("arbitrary",)),
    )(o1, o4, o16, l1, l4, l16, rec, ag, hg, x, tgt, anw, hnw, fnw, wout_full, gmat, emat, selmat)


def _dproj_build(dq, dk, dv, dag, pos):
    TT = 256

    def body(dq1, dq4, dq16, dk1, dk4, dk16, dv1, dv4, dv16, dag_r, pos_r, dproj_o, scr_a, scr_b, scr_c):
        def unperm_sum(r1, r4, r16):
            u4, u16 = _unperm_load(r4, r16, scr_a, scr_b, scr_c)
            return r1[...] + u4 + u16

        cosf, s1, s2 = _rope_tables(pos_r[...])
        dproj_o[:, 0:512] = _rope_bwd(unperm_sum(dq1, dq4, dq16), cosf, s1, s2).astype(BF16)
        dproj_o[:, 512:1024] = _rope_bwd(unperm_sum(dk1, dk4, dk16), cosf, s1, s2).astype(BF16)
        dproj_o[:, 1024:1536] = unperm_sum(dv1, dv4, dv16).astype(BF16)
        dproj_o[:, 1536:2048] = dag_r[...]

    tok = lambda w: pl.BlockSpec((TT, w), lambda i: (i, 0))
    d4 = pl.BlockSpec((4, TT // 4, AW), lambda i: (0, i, 0))
    d16 = pl.BlockSpec((16, TT // 16, AW), lambda i: (0, i, 0))
    return pl.pallas_call(
        body, name="dproj_build", grid=(T // TT,),
        in_specs=[tok(AW), d4, d16] * 3 + [tok(AW), tok(1)],
        out_specs=tok(NCOL // 2),
        out_shape=jax.ShapeDtypeStruct((T, NCOL // 2), BF16),
        scratch_shapes=[pltpu.VMEM((4, TT, 128), F32)] * 3,
        compiler_params=_cp(("parallel",)),
    )(*dq, *dk, *dv, dag, pos)


def _bwd_x(dproj_a, dproj_h, x, dx2, mixw, w_full, rin, rinb, small4, small6):
    TT = 256
    NT = T // TT

    def body(dpa_r, dph_r, x_r, dx2_r, mw_r, w_r, rin_r, rinb_r, s4_r, s6_r,
             gx_o, pin_o, pinr_o, sall_o, sbuf, send_sems, recv_sems, loc_sems):
        i = pl.program_id(0)
        loc, rem = _chip_copies(_w_in_piece, rin_r, rinb_r, pin_o, pinr_o, send_sems, recv_sems, loc_sems.at[0])

        @pl.when(i == 0)
        def _():
            sbuf[...] = jnp.zeros_like(sbuf)
            for cp in loc + rem:
                cp.start()

        dhn = _mm_nt(dpa_r[...], w_r[:, 0:NCOL // 2]) + _mm_nt(dph_r[...], w_r[:, NCOL // 2:NCOL])
        xv = x_r[...]
        r = lax.rsqrt(jnp.mean(xv * xv, axis=-1, keepdims=True) + EPS)
        dxw = dhn * mw_r[...]
        gx_o[...] = dx2_r[...] + r * dxw - xv * ((r * r * r) * jnp.mean(dxw * xv, axis=-1, keepdims=True))
        sbuf[16:17, :] += jnp.sum(dhn * (xv * r), axis=0, keepdims=True)

        @pl.when(i == NT - 1)
        def _():
            sbuf[0:8, :] = s4_r[...]
            sbuf[8:16, :] = s6_r[...]
            sloc, srem = _small_copies(sbuf, sall_o, send_sems, recv_sems, loc_sems.at[1])
            for cp in sloc + srem:
                cp.start()
            for cp in rem + srem:
                cp.wait_recv()
            for cp in rem + srem:
                cp.wait_send()
            for cp in loc + sloc:
                cp.wait()

    tok = lambda w: pl.BlockSpec((TT, w), lambda i: (i, 0))
    const = lambda shape: pl.BlockSpec(shape, lambda i: (0,) * len(shape))
    hbm = pl.BlockSpec(memory_space=pltpu.HBM)
    return pl.pallas_call(
        body, name="bwd_x", grid=(NT,),
        in_specs=[tok(NCOL // 2), tok(NCOL // 2), tok(D), tok(D), const((1, D)), const((D, NCOL)), hbm, hbm,
                  const((8, D)), const((8, D))],
        out_specs=[tok(D), hbm, hbm, hbm],
        out_shape=[jax.ShapeDtypeStruct((T, D), F32),
                   jax.ShapeDtypeStruct((512, 1024), F32), jax.ShapeDtypeStruct((3, 512, 1024), BF16),
                   jax.ShapeDtypeStruct((8, 24, D), F32)],
        scratch_shapes=[pltpu.VMEM((24, D), F32), pltpu.SemaphoreType.DMA((10,)), pltpu.SemaphoreType.DMA((10,)),
                        pltpu.SemaphoreType.DMA((2,))],
        compiler_params=_cp(("arbitrary",)),
    )(dproj_a, dproj_h, x, dx2, mixw, w_full, rin, rinb, small4, small6)


def _grad_w_in(hn, dproj_a, dproj_h):
    TK = 1024
    NK = T // TK

    def body(hnt_r, dpa_r, dph_r, rin_o, rinb_o, acc, rbuf, obuf, obufb, send_sems, recv_sems, wb_sems):
        j = pl.program_id(0)
        kk = pl.program_id(1)
        x, y, c = lax.axis_index("x"), lax.axis_index("y"), lax.axis_index("c")
        mine = pl.ds(pl.multiple_of(c * 512, 512), 512)
        theirs = pl.ds(pl.multiple_of((1 - c) * 512, 512), 512)

        def send(jj):
            return pltpu.make_async_remote_copy(
                src_ref=acc.at[jj % 2, theirs, :], dst_ref=rbuf.at[jj], send_sem=send_sems.at[jj],
                recv_sem=recv_sems.at[jj], device_id=(x, y, 1 - c), device_id_type=MESH)

        def writeback(jj):
            cols = pl.ds(jj * 1024, 1024)
            return [pltpu.make_async_copy(obuf.at[jj % 2], rin_o.at[:, cols], wb_sems.at[jj % 2]),
                    pltpu.make_async_copy(obufb.at[jj % 2], rinb_o.at[:, cols], wb_sems.at[2 + jj % 2])]

        def wait_writeback(jj):
            for cp in writeback(jj):
                cp.wait()

        def finalize(jj):
            send(jj).wait_recv()
            red = acc[jj % 2, mine, :] + rbuf[jj]
            obuf[jj % 2] = red
            obufb[jj % 2] = red.astype(BF16)
            for cp in writeback(jj):
                cp.start()

        prod = _mm(hnt_r[...], jnp.where(j < 2, dpa_r[...], dph_r[...]))

        @pl.when(kk == 0)
        def _():
            for jj in (2, 3):
                @pl.when(j == jj)
                def _():
                    send(jj - 2).wait_send()
            acc[j % 2] = prod

        @pl.when(kk > 0)
        def _():
            acc[j % 2] += prod

        @pl.when(kk == NK - 1)
        def _():
            for jj in range(4):
                @pl.when(j == jj)
                def _():
                    send(jj).start()
                    if jj in (1, 2):
                        finalize(jj - 1)
                    if jj == 3:
                        wait_writeback(0)
                        finalize(2)
                        wait_writeback(1)
                        finalize(3)
                        wait_writeback(2)
                        wait_writeback(3)
                        send(2).wait_send()
                        send(3).wait_send()

    hbm = pl.BlockSpec(memory_space=pltpu.HBM)
    return pl.pallas_call(
        body, name="grad_w_in", grid=(4, NK),
        in_specs=[pl.BlockSpec((D, TK), lambda j, kk: (0, kk)),
                  pl.BlockSpec((TK, 1024), lambda j, kk: (jnp.where(j < 2, kk, NK - 1), jnp.minimum(j, 1))),
                  pl.BlockSpec((TK, 1024), lambda j, kk: (jnp.where(j < 2, 0, kk), jnp.maximum(j - 2, 0)))],
        out_specs=[hbm, hbm],
        out_shape=[jax.ShapeDtypeStruct((512, NCOL), F32), jax.ShapeDtypeStruct((512, NCOL), BF16)],
        scratch_shapes=[pltpu.VMEM((2, D, 1024), F32), pltpu.VMEM((4, 512, 1024), F32), pltpu.VMEM((2, 512, 1024), F32),
                        pltpu.VMEM((2, 512, 1024), BF16),
                        pltpu.SemaphoreType.DMA((4,)), pltpu.SemaphoreType.DMA((4,)), pltpu.SemaphoreType.DMA((4,))],
        compiler_params=_cp(("arbitrary", "arbitrary")),
    )(hn, dproj_a, dproj_h)


def _w_in_piece(ref, j):
    return ref.at[:, pl.ds(j * 1024, 1024)]


def _w_out_piece(ref, j):
    return ref.at[pl.ds(j * 128, 128), :]


def _chip_copies(piece, src_r, srcb_r, own_o, rem_o, send_sems, recv_sems, loc_sem):
    x, y, c = lax.axis_index("x"), lax.axis_index("y"), lax.axis_index("c")
    chips = [(1 - x, y), (x, 1 - y), (1 - x, 1 - y)]
    loc = [pltpu.make_async_copy(piece(src_r, 2 * x + y), own_o, loc_sem)]
    rem = [pltpu.make_async_remote_copy(
        src_ref=piece(srcb_r, 2 * px + py), dst_ref=rem_o.at[k], send_sem=send_sems.at[k],
        recv_sem=recv_sems.at[k], device_id=(px, py, c), device_id_type=MESH) for k, (px, py) in enumerate(chips)]
    return loc, rem


def _small_copies(small_r, sall_o, send_sems, recv_sems, loc_sem):
    x, y, c = lax.axis_index("x"), lax.axis_index("y"), lax.axis_index("c")
    me = 4 * x + 2 * y + c
    loc = [pltpu.make_async_copy(small_r, sall_o.at[me], loc_sem)]
    rem = []
    k = 3
    for fx in range(2):
        for fy in range(2):
            for fc in range(2):
                if fx or fy or fc:
                    peer = (1 - x if fx else x, 1 - y if fy else y, 1 - c if fc else c)
                    rem.append(pltpu.make_async_remote_copy(
                        src_ref=small_r, dst_ref=sall_o.at[me], send_sem=send_sems.at[k],
                        recv_sem=recv_sems.at[k], device_id=peer, device_id_type=MESH))
                    k += 1
    return loc, rem


def _pair_share(pin_own, pin_rem, pout_own, pout_rem):
    def body(pio_r, pir_r, poo_r, por_r, fin_o, fout_o, sin, sout, send_sems, recv_sems):
        x, y, c = lax.axis_index("x"), lax.axis_index("y"), lax.axis_index("c")
        sibling = (x, y, 1 - c)
        sout[...] = ((poo_r[...] + por_r[0].astype(F32)) + por_r[1].astype(F32)) + por_r[2].astype(F32)
        sin[...] = ((pio_r[...] + pir_r[0].astype(F32)) + pir_r[1].astype(F32)) + pir_r[2].astype(F32)
        rem = [pltpu.make_async_remote_copy(src_ref=sin, dst_ref=fin_o.at[c], send_sem=send_sems.at[0],
                                            recv_sem=recv_sems.at[0], device_id=sibling, device_id_type=MESH),
               pltpu.make_async_remote_copy(src_ref=sout, dst_ref=fout_o.at[c], send_sem=send_sems.at[1],
                                            recv_sem=recv_sems.at[1], device_id=sibling, device_id_type=MESH)]
        for cp in rem:
            cp.start()
        fin_o[c] = sin[...]
        fout_o[c] = sout[...]
        for cp in rem:
            cp.wait_recv()
        for cp in rem:
            cp.wait_send()

    vm = pl.BlockSpec(memory_space=pltpu.VMEM)
    return pl.pallas_call(
        body, name="pair_share",
        out_shape=(jax.ShapeDtypeStruct((2, 512, 1024), F32), jax.ShapeDtypeStruct((2, 128, D), F32)),
        in_specs=[vm, vm, vm, vm], out_specs=(vm, vm),
        scratch_shapes=[pltpu.VMEM((512, 1024), F32), pltpu.VMEM((128, D), F32),
                        pltpu.SemaphoreType.DMA((2,)), pltpu.SemaphoreType.DMA((2,))],
        compiler_params=_cp(),
    )(pin_own, pin_rem, pout_own, pout_rem)


def _adamw_math(w, g, m, v):
    m = B1 * m + (1.0 - B1) * g
    v = B2 * v + (1.0 - B2) * (g * g)
    m_hat = m / (1.0 - B1 ** STEP)
    v_hat = v / (1.0 - B2 ** STEP)
    delta = -LR * (m_hat / (jnp.sqrt(v_hat) + AEPS) + WD * w)
    return delta, m, v


def _adamw(big_in, big_out, sall, params):
    def body(*refs):
        wi, gi, mi, vi, wo, go, mo, vo, sall_r = refs[:9]
        ins = refs[9:24]
        di_o, mi_o, vi_o, do_o, mo_o, vo_o = refs[24:30]
        outs = refs[30:]
        d, mm, vv = _adamw_math(wi[...], gi[...], mi[...], vi[...])
        di_o[...] = d
        mi_o[...] = mm
        vi_o[...] = vv

        @pl.when(pl.program_id(0) == 0)
        def _():
            d, mm, vv = _adamw_math(wo[...], go[...], mo[...], vo[...])
            do_o[...] = d
            mo_o[...] = mm
            vo_o[...] = vv
            tot = sall_r[0]
            for dv in range(1, 8):
                tot = tot + sall_r[dv]
            grads = [tot[16:17, :], tot[1:2, 0:AW], tot[1:2, AW:], tot[8:10, 0:HW], tot[0:1, :]]
            outs[0][...] = tot[2:3, 0:1]
            for p in range(5):
                w_r, m_r, v_r = ins[3 * p:3 * p + 3]
                g = grads[p]
                d, mm, vv = _adamw_math(w_r[...], g, m_r[...], v_r[...])
                outs[1 + 4 * p][...] = g
                outs[2 + 4 * p][...] = d
                outs[3 + 4 * p][...] = mm
                outs[4 + 4 * p][...] = vv

    flat = [a for p in params for a in p]
    shapes = [jax.ShapeDtypeStruct((D, 1024), F32)] * 3 + [jax.ShapeDtypeStruct((256, D), F32)] * 3
    shapes += [jax.ShapeDtypeStruct((1, 1), F32)]
    for p in params:
        shapes += [jax.ShapeDtypeStruct(p[0].shape, F32)] * 4
    vm = pl.BlockSpec(memory_space=pltpu.VMEM)
    rows = pl.BlockSpec((256, 1024), lambda i: (i, 0))
    whole = pl.BlockSpec((256, D), lambda i: (0, 0))
    return pl.pallas_call(
        body, name="adamw", grid=(4,),
        in_specs=[rows] * 4 + [whole] * 4 + [vm] * 16, out_specs=[rows] * 3 + [whole] * 3 + [vm] * 21,
        out_shape=shapes,
        compiler_params=_cp(("arbitrary",)),
    )(*big_in, *big_out, sall, *flat)


def kernel(x, positions, w_in, w_out, mix_norm_w, attn_out_norm_w, hgrn_out_norm_w, hgrn_lb_raw, final_norm_w, loss_target, m_w_in, m_w_out, m_mix_norm_w, m_attn_out_norm_w, m_hgrn_out_norm_w, m_hgrn_lb_raw, m_final_norm_w, v_w_in, v_w_out, v_mix_norm_w, v_attn_out_norm_w, v_hgrn_out_norm_w, v_hgrn_lb_raw, v_final_norm_w):
    xs = x.reshape(T, D)
    tgt = loss_target.reshape(T, D)
    pos = positions.reshape(T, 1)
    fnw = final_norm_w.reshape(1, D)

    ti = np.arange(TH)
    tri_np = ((ti[:, None] // CHUNK == ti[None, :] // CHUNK) & (ti[None, :] <= ti[:, None])).astype(np.float32)
    tri = jnp.asarray(tri_np, BF16)
    trit = jnp.asarray(tri_np.T, BF16)
    hi_ = np.arange(AW) // HEAD
    gmat = jnp.asarray((hi_[:256, None] == hi_[None, :256]).astype(np.float32) / HEAD, BF16)
    emat_np = (np.arange(128)[:, None] == hi_[None, :]).astype(np.float32)
    sel_np = (8 + hi_[:, None] == np.arange(128)[None, :]).astype(np.float32)
    emat = jnp.asarray(emat_np, BF16)
    selmat = jnp.asarray(sel_np, BF16)

    wb_in, wb_out = _cast_weights(w_in.reshape(D, 1024), w_out.reshape(256, D))
    jm_arr = (2 * lax.axis_index("x") + lax.axis_index("y")).astype(jnp.int32).reshape(1)
    (hn, q1, k1, v1, q4, k4, v4, q16, k16, v16, ag, hq, hf, hi, hg, w_full, wout4) = _fwd_in(
        xs, pos, mix_norm_w, wb_in, wb_out, jm_arr)
    wout_full = wout4.reshape(D, D)
    flat = lambda a: a.reshape(T, AW)
    o1, l1 = _attn_fwd(q1, k1, v1, T // BLK, "attn_fwd_d1")
    o4, l4 = _attn_fwd(flat(q4), flat(k4), flat(v4), T // 4 // BLK, "attn_fwd_d4")
    o16, l16 = _attn_fwd(flat(q16), flat(k16), flat(v16), T // 16 // BLK, "attn_fwd_d16")
    rec, sall = _hgrn_fwd(hq, hf, hi, hgrn_lb_raw, tri)

    (dx2, do1, do4, do16, st1, st4, st16, drec, dag, dhg, rout, routb, small4) = _fwd_out(
        o1, o4.reshape(4, T // 4, AW), o16.reshape(16, T // 16, AW),
        l1, l4.reshape(4, T // 4, 128), l16.reshape(16, T // 16, 128),
        rec, ag, hg, xs, tgt, attn_out_norm_w, hgrn_out_norm_w, fnw, wout_full, gmat, emat, selmat)

    fst = lambda a: a.reshape(T, 128)
    dq1, dk1, dv1 = _attn_bwd(q1, k1, v1, do1, st1, T // BLK, "attn_bwd_d1")
    dq4, dk4, dv4 = _attn_bwd(flat(q4), flat(k4), flat(v4), flat(do4), fst(st4), T // 4 // BLK, "attn_bwd_d4")
    dq16, dk16, dv16 = _attn_bwd(flat(q16), flat(k16), flat(v16), flat(do16), fst(st16), T // 16 // BLK,
                                 "attn_bwd_d16")
    dproj_h, small6, pout_own, pout_rem = _hgrn_bwd(hq, hf, hi, hgrn_lb_raw, tri, trit, drec, sall, dhg,
                                                    rout, routb)

    r4 = lambda a: a.reshape(4, T // 4, AW)
    r16 = lambda a: a.reshape(16, T // 16, AW)
    dproj_a = _dproj_build((dq1, r4(dq4), r16(dq16)), (dk1, r4(dk4), r16(dk16)), (dv1, r4(dv4), r16(dv16)),
                           dag, pos)
    rin, rinb = _grad_w_in(hn, dproj_a, dproj_h)
    gx, pin_own, pin_rem, small_all = _bwd_x(dproj_a, dproj_h, xs, dx2, mix_norm_w, w_full, rin, rinb,
                                             small4, small6)
    fin, fout = _pair_share(pin_own, pin_rem, pout_own, pout_rem)
    g_w_in = fin.reshape(D, 1024)
    g_w_out = fout.reshape(256, D)

    params = [(mix_norm_w, m_mix_norm_w, v_mix_norm_w),
              (attn_out_norm_w, m_attn_out_norm_w, v_attn_out_norm_w),
              (hgrn_out_norm_w, m_hgrn_out_norm_w, v_hgrn_out_norm_w),
              (hgrn_lb_raw, m_hgrn_lb_raw, v_hgrn_lb_raw),
              (fnw, m_final_norm_w.reshape(1, D), v_final_norm_w.reshape(1, D))]
    d_in, nm_in, nv_in, d_out, nm_out, nv_out, *so = _adamw(
        (w_in.reshape(D, 1024), g_w_in, m_w_in.reshape(D, 1024), v_w_in.reshape(D, 1024)),
        (w_out.reshape(256, D), g_w_out, m_w_out.reshape(256, D), v_w_out.reshape(256, D)), small_all, params)
    loss = so[0].reshape(())
    g_s = [so[1 + 4 * p] for p in range(5)]
    d_s = [so[2 + 4 * p] for p in range(5)]
    m_s = [so[3 + 4 * p] for p in range(5)]
    v_s = [so[4 + 4 * p] for p in range(5)]
    for lst in (g_s, d_s, m_s, v_s):
        lst[4] = lst[4].reshape(D)

    return (loss, gx.reshape(1, T, D),
            g_w_in.reshape(1, D, 1024), g_w_out.reshape(1, 256, D), *g_s,
            d_in.reshape(1, D, 1024), d_out.reshape(1, 256, D), *d_s,
            nm_in.reshape(1, D, 1024), nm_out.reshape(1, 256, D), *m_s,
            nv_in.reshape(1, D, 1024), nv_out.reshape(1, 256, D), *v_s)
```

```python
import functools

import numpy as np
import jax
import jax.numpy as jnp
from jax import lax
from jax.experimental import pallas as pl
from jax.experimental.pallas import tpu as pltpu

F32 = jnp.float32
BF16 = jnp.bfloat16

T = 4096
D = 1024
AW = 512
HW = 512
NCOL = 4096
HEAD = 64
BLK = 128
CHUNK = 64
EPS = 1e-6
SCALE = HEAD ** -0.5
NEG = -1e30
ROPE_THETA = 500000.0
INV_FREQ = [float(v) for v in
            (np.float32(ROPE_THETA) ** (-(np.arange(8, dtype=np.float32)) * np.float32(0.125)))]
LR, B1, B2, AEPS, WD, STEP = 0.001, 0.9, 0.999, 1e-08, 0.01, 10
VMEM_LIMIT = 56 * 1024 * 1024
MESH = pl.DeviceIdType.MESH


def _cp(sem=None, **kw):
    return pltpu.CompilerParams(dimension_semantics=sem, vmem_limit_bytes=VMEM_LIMIT, **kw)


def _mm(a, b):
    return jnp.dot(a, b, preferred_element_type=F32)


def _mm_nt(a, b):
    return lax.dot_general(a, b, (((1,), (1,)), ((), ())), preferred_element_type=F32)


def _mm_tn(a, b):
    return lax.dot_general(a, b, (((0,), (0,)), ((), ())), preferred_element_type=F32)


def _split3(x):
    h = x.astype(BF16)
    r = x - h.astype(F32)
    m = r.astype(BF16)
    l = (r - m.astype(F32)).astype(BF16)
    return h, m, l


def _mm_exact_l(mat_bf, x):
    h, m, l = _split3(x)
    return _mm(mat_bf, h) + _mm(mat_bf, m) + _mm(mat_bf, l)


def _mm_exact_r(x, mat_bf):
    h = x.astype(BF16)
    l = (x - h.astype(F32)).astype(BF16)
    return _mm(h, mat_bf) + _mm(l, mat_bf)


def _sigmoid(x):
    return 0.5 * jnp.tanh(0.5 * x) + 0.5


def _rope_tables(pos):
    lane = lax.broadcasted_iota(jnp.int32, (1, 128), 1)
    jl = lane & 63
    fi = jl & 7
    inv = jnp.zeros((1, 128), F32)
    for kk in range(8):
        inv = jnp.where(fi == kk, INV_FREQ[kk], inv)
    ang = pos.astype(F32) * inv
    c = jnp.cos(ang)
    s = jnp.sin(ang)
    cosf = jnp.where(jl < 16, c, 1.0)
    s1 = jnp.where(jl < 8, -s, 0.0)
    s2 = jnp.where((jl >= 8) & (jl < 16), s, 0.0)
    return cosf, s1, s2


def _rope(t, cosf, s1, s2):
    parts = []
    for ci in range(t.shape[1] // 128):
        tc = t[:, ci * 128:(ci + 1) * 128]
        parts.append(tc * cosf + pltpu.roll(tc, 120, 1) * s1 + pltpu.roll(tc, 8, 1) * s2)
    return jnp.concatenate(parts, axis=1)


def _rope_bwd(g, cosf, s1, s2):
    parts = []
    for ci in range(g.shape[1] // 128):
        gc = g[:, ci * 128:(ci + 1) * 128]
        parts.append(gc * cosf + pltpu.roll(gc * s1, 8, 1) + pltpu.roll(gc * s2, 120, 1))
    return jnp.concatenate(parts, axis=1)


def _perm_store(val, scr, scr2, o1, o4, o16, dt):
    n = val.shape[0]
    q = n // 4
    o1[...] = val.astype(dt)
    for ci in range(val.shape[1] // 128):
        cs = slice(ci * 128, (ci + 1) * 128)
        scr[ci] = val[:, cs]
        for r4 in range(4):
            part = scr[ci, pl.ds(r4, q, stride=4), :]
            o4[r4, :, cs] = part.astype(dt)
            scr2[ci, r4 * q:(r4 + 1) * q, :] = part
        for r4 in range(4):
            for b in range(4):
                o16[r4 + 4 * b, :, cs] = scr2[ci, pl.ds(r4 * q + b, q // 4, stride=4), :].astype(dt)


def _unperm_load(r4, r16, scr_a, scr_b, scr_c):
    n = scr_a.shape[1]
    q = n // 4
    nc = r4.shape[-1] // 128
    for ci in range(nc):
        cs = slice(ci * 128, (ci + 1) * 128)
        for rr in range(4):
            scr_a[ci, pl.ds(rr, q, stride=4), :] = r4[rr, :, cs].astype(F32)
        for rr in range(4):
            for b in range(4):
                scr_c[ci, pl.ds(rr * q + b, q // 4, stride=4), :] = r16[rr + 4 * b, :, cs].astype(F32)
        for rr in range(4):
            scr_b[ci, pl.ds(rr, q, stride=4), :] = scr_c[ci, rr * q:(rr + 1) * q, :]
    return (jnp.concatenate([scr_a[ci] for ci in range(nc)], axis=1),
            jnp.concatenate([scr_b[ci] for ci in range(nc)], axis=1))


def _fwd_in(x, pos, mixw, w_in, w_out, jm_arr):
    TT = 512
    NT = T // TT

    def body(jm_ref, x_ref, pos_ref, mw_ref, win_ref, wout_ref,
             hnt_ref, q1, k1, v1, q4, k4, v4, q16, k16, v16, ag, hq, hf, hi, hg, wfull_o, woutfull_o,
             wbuf, wobuf, hn_all, scr, scr2, stage, send_sems, recv_sems, loc_sems):
        s = pl.program_id(0)
        i = pl.program_id(1)
        mx, my, c = lax.axis_index("x"), lax.axis_index("y"), lax.axis_index("c")
        me, sibling = (mx, my, c), (mx, my, 1 - c)
        chips = [(mx, 1 - my), (1 - mx, my), (1 - mx, 1 - my)]
        jm = 2 * mx + my
        rows_in = [pl.ds(pl.multiple_of(h * 512, 512), 512) for h in (c, 1 - c)]
        rows_out = [pl.ds(pl.multiple_of(h * 128, 128), 128) for h in (c, 1 - c)]

        def blk(k):
            return lax.bitwise_xor(jm, k + 1)

        def rc(n, ref, to):
            return pltpu.make_async_remote_copy(src_ref=ref, dst_ref=ref, send_sem=send_sems.at[n],
                                                recv_sem=recv_sems.at[n], device_id=to, device_id_type=MESH)

        send_in = lambda k: rc(k, wbuf.at[jm, rows_in[0], :], (*chips[k], c))
        send_out = lambda k: rc(3 + k, wobuf.at[jm, rows_out[0], :], (*chips[k], c))
        got_in = lambda k: rc(k, wbuf.at[blk(k), rows_in[0], :], me)
        got_out = lambda k: rc(3 + k, wobuf.at[blk(k), rows_out[0], :], me)
        pass_in = lambda k: rc(6 + k, wbuf.at[blk(k), rows_in[0], :], sibling)
        pass_out = lambda k: rc(9 + k, wobuf.at[blk(k), rows_out[0], :], sibling)
        passed_in = lambda k: rc(6 + k, wbuf.at[blk(k), rows_in[1], :], me)
        passed_out = lambda k: rc(9 + k, wobuf.at[blk(k), rows_out[1], :], me)

        def keep(j, n):
            return pltpu.make_async_copy(wbuf.at[j], wfull_o.at[:, pl.ds(j * 1024, 1024)], loc_sems.at[n])

        @pl.when((s == 0) & (i == 0))
        def _():
            for p in range(5):
                src = win_ref.at[pl.ds(p * 256, 256), :] if p < 4 else wout_ref
                load = pltpu.make_async_copy(src, stage, loc_sems.at[4])
                load.start()
                load.wait()
                if p < 4:
                    wbuf[jm, p * 256:(p + 1) * 256, :] = stage[...].astype(BF16)
                else:
                    wobuf[jm] = stage[...].astype(BF16)
            send_in(0).start()
            send_in(1).start()
            keep(jm, 0).start()

        def arrive(k):
            if k == 0:
                send_in(0).wait_send()
                send_in(1).wait_send()
                send_in(2).start()
            got_in(k).wait_recv()
            pass_in(k).start()
            passed_in(k).wait_recv()
            keep(blk(k), k + 1).start()
            if k == 2:
                for kk in range(3):
                    send_out(kk).start()

        for k in range(3):
            pl.when((s == k + 1) & (i == 0))(functools.partial(arrive, k))

        tile = pl.ds(pl.multiple_of(i * TT, TT), TT)

        @pl.when(s == 0)
        def _():
            xv = x_ref[...]
            r = lax.rsqrt(jnp.mean(xv * xv, axis=-1, keepdims=True) + EPS)
            hnf = (xv * r) * mw_ref[...]
            hn_all[tile, :] = hnf.astype(BF16)
            hnt_ref[...] = hnf.T.astype(BF16)

        def project(jj):
            hn = hn_all[tile, :]
            lo = _mm(hn, wbuf[jj, :, 0:512])
            hi_cols = _mm(hn, wbuf[jj, :, 512:1024])
            if jj == 0:
                cosf, s1, s2 = _rope_tables(pos_ref[...])
                _perm_store(_rope(lo, cosf, s1, s2), scr, scr2, q1, q4, q16, BF16)
                _perm_store(_rope(hi_cols, cosf, s1, s2), scr, scr2, k1, k4, k16, BF16)
            elif jj == 1:
                _perm_store(lo, scr, scr2, v1, v4, v16, BF16)
                ag[...] = hi_cols.astype(BF16)
            elif jj == 2:
                hq[...] = lo.astype(BF16)
                hf[...] = hi_cols.astype(BF16)
            else:
                hi[...] = lo.astype(BF16)
                hg[...] = hi_cols.astype(BF16)

        j = lax.bitwise_xor(jm, s)
        for jj in range(4):
            pl.when(j == jj)(functools.partial(project, jj))

        @pl.when((s == 3) & (i == NT - 1))
        def _():
            for k in range(3):
                got_out(k).wait_recv()
                pass_out(k).start()
            for k in range(3):
                passed_out(k).wait_recv()
            out = pltpu.make_async_copy(wobuf, woutfull_o, loc_sems.at[4])
            out.start()
            send_in(2).wait_send()
            for k in range(3):
                send_out(k).wait_send()
                pass_in(k).wait_send()
                pass_out(k).wait_send()
            keep(jm, 0).wait()
            for k in range(3):
                keep(blk(k), k + 1).wait()
            out.wait()

    def at_stage_of(jb):
        def index(s, i, jm_ref):
            sa = lax.bitwise_xor(jm_ref[0], jb)
            return jnp.where(s < sa, 0, jnp.where(s == sa, i, NT - 1))
        return index

    tok = lambda w, jb: pl.BlockSpec((TT, w), lambda s, i, jm_ref: (at_stage_of(jb)(s, i, jm_ref), 0))
    d4 = lambda jb: pl.BlockSpec((4, TT // 4, AW), lambda s, i, jm_ref: (0, at_stage_of(jb)(s, i, jm_ref), 0))
    d16 = lambda jb: pl.BlockSpec((16, TT // 16, AW), lambda s, i, jm_ref: (0, at_stage_of(jb)(s, i, jm_ref), 0))
    hbm = pl.BlockSpec(memory_space=pltpu.HBM)
    sd = lambda shape, dt: jax.ShapeDtypeStruct(shape, dt)
    in_own_stage = lambda s, i: jnp.where(s == 0, i, NT - 1)
    grid_spec = pltpu.PrefetchScalarGridSpec(
        num_scalar_prefetch=1, grid=(4, NT),
        in_specs=[pl.BlockSpec((TT, D), lambda s, i, jm_ref: (in_own_stage(s, i), 0)),
                  pl.BlockSpec((TT, 1), lambda s, i, jm_ref: (i, 0)),
                  pl.BlockSpec((1, D), lambda s, i, jm_ref: (0, 0)), hbm, hbm],
        out_specs=[pl.BlockSpec((D, TT), lambda s, i, jm_ref: (0, in_own_stage(s, i))),
                   tok(AW, 0), tok(AW, 0), tok(AW, 1), d4(0), d4(0), d4(1), d16(0), d16(0), d16(1),
                   tok(AW, 1), tok(AW, 2), tok(AW, 2), tok(AW, 3), tok(AW, 3), hbm, hbm],
        scratch_shapes=[pltpu.VMEM((4, D, 1024), BF16), pltpu.VMEM((4, 256, D), BF16), pltpu.VMEM((T, D), BF16),
                        pltpu.VMEM((4, TT, 128), F32), pltpu.VMEM((4, TT, 128), F32), pltpu.VMEM((256, 1024), F32),
                        pltpu.SemaphoreType.DMA((12,)),
                        pltpu.SemaphoreType.DMA((12,)), pltpu.SemaphoreType.DMA((6,))])
    return pl.pallas_call(
        body, name="fwd_in", grid_spec=grid_spec,
        out_shape=[sd((D, T), BF16)] + [sd((T, AW), BF16)] * 3 + [sd((4, T // 4, AW), BF16)] * 3
        + [sd((16, T // 16, AW), BF16)] * 3
        + [sd((T, AW), BF16)] * 5 + [sd((D, NCOL), BF16), sd((4, 256, D), BF16)],
        compiler_params=_cp(("arbitrary", "arbitrary")),
    )(jm_arr, x, pos, mixw, w_in, w_out)


def _band_mask(key_axis, nkeys=2 * BLK):
    shape = (nkeys, 2 * BLK) if key_axis == 0 else (2 * BLK, nkeys)
    kj = lax.broadcasted_iota(jnp.int32, shape, key_axis)
    qi = lax.broadcasted_iota(jnp.int32, shape, 1 - key_axis) & (BLK - 1)
    return (kj >= qi) & (kj <= qi + BLK), kj, qi


def _stack_heads(t2, in_a):
    z = jnp.zeros_like(t2)
    return jnp.concatenate([jnp.where(in_a[0], t2, z), jnp.where(in_a[1], t2, z)], axis=0)


def _attn_fwd(q, k, v, nb, name):
    n = min(4, nb)
    CH = n * BLK
    halo = nb > n

    def body(*refs):
        if halo:
            q_ref, k_ref, v_ref, kp_ref, vp_ref, o_ref, lse_ref = refs
        else:
            q_ref, k_ref, v_ref, o_ref, lse_ref = refs
        lane = lax.broadcasted_iota(jnp.int32, (1, 128), 1)
        in_a = [lane < HEAD, lane >= HEAD]
        band, kj, _ = _band_mask(1)
        thr0 = jnp.where((n * pl.program_id(0)) % nb == 0, BLK, 0) if halo else BLK
        mask0 = band & (kj >= thr0)
        for b in range(n):
            rs = slice(b * BLK, (b + 1) * BLK)
            stat = jnp.zeros((BLK, 128), F32)
            for hp in range(4):
                cs = slice(hp * 128, (hp + 1) * 128)
                q2s = _stack_heads(q_ref[rs, cs], in_a)
                if b == 0:
                    kprev = kp_ref[:, cs] if halo else k_ref[rs, cs]
                    vprev = vp_ref[:, cs] if halo else v_ref[rs, cs]
                    kk = jnp.concatenate([kprev, k_ref[rs, cs]], axis=0)
                    vv = jnp.concatenate([vprev, v_ref[rs, cs]], axis=0)
                    mask = mask0
                else:
                    kk = k_ref[(b - 1) * BLK:(b + 1) * BLK, cs]
                    vv = v_ref[(b - 1) * BLK:(b + 1) * BLK, cs]
                    mask = band
                s = jnp.where(mask, _mm_nt(q2s, kk) * SCALE, NEG)
                m = jnp.max(s, axis=-1, keepdims=True)
                p = jnp.exp(s - m)
                l = jnp.sum(p, axis=-1, keepdims=True)
                o = _mm(p.astype(BF16), vv) / l
                lse = m + jnp.log(l)
                o_ref[rs, cs] = jnp.where(in_a[0], o[:BLK], o[BLK:]).astype(BF16)
                stat = jnp.where(lane == 2 * hp, lse[:BLK], stat)
                stat = jnp.where(lane == 2 * hp + 1, lse[BLK:], stat)
            lse_ref[rs, :] = stat

    cur = pl.BlockSpec((CH, AW), lambda i: (i, 0))
    prev = pl.BlockSpec((BLK, AW), lambda i: (jnp.maximum(n * i - 1, 0), 0))
    return pl.pallas_call(
        body, name=name, grid=(T // CH,),
        in_specs=[cur, cur, cur] + ([prev, prev] if halo else []),
        out_specs=[cur, pl.BlockSpec((CH, 128), lambda i: (i, 0))],
        out_shape=[jax.ShapeDtypeStruct((T, AW), BF16), jax.ShapeDtypeStruct((T, 128), F32)],
        compiler_params=_cp(("parallel",)),
    )(*((q, k, v) + ((k, v) if halo else ())))


def _attn_bwd(q, k, v, do, st, nb, name):
    n = min(4, nb)
    CH = n * BLK
    NBLK = T // BLK
    halo = nb > n

    def body(*refs):
        if halo:
            (q_ref, k_ref, v_ref, do_ref, st_ref, kp_ref, vp_ref, qn_ref, don_ref, stn_ref,
             dq_ref, dk_ref, dv_ref) = refs
        else:
            q_ref, k_ref, v_ref, do_ref, st_ref, dq_ref, dk_ref, dv_ref = refs
        i = pl.program_id(0)
        lane = lax.broadcasted_iota(jnp.int32, (1, 128), 1)
        in_a = [lane < HEAD, lane >= HEAD]
        band, kj, _ = _band_mask(0)
        thr0 = jnp.where((n * i) % nb == 0, BLK, 0) if halo else BLK
        mask0 = band & (kj >= thr0)

        def stat_rows(st_t, hp):
            lse_r = jnp.concatenate([st_t[2 * hp:2 * hp + 1, :], st_t[2 * hp + 1:2 * hp + 2, :]], axis=1)
            dl_r = jnp.concatenate([st_t[8 + 2 * hp:9 + 2 * hp, :], st_t[9 + 2 * hp:10 + 2 * hp, :]], axis=1)
            return lse_r, dl_r

        st_t = [st_ref[b * BLK:(b + 1) * BLK, :].T for b in range(n)]
        if halo:
            nxt_thr = jnp.where((n * i + n) % nb == 0, 2 * BLK, 0)
            _, kj1, qi1 = _band_mask(0, BLK)
            mask_next = kj1 >= qi1 + nxt_thr
            stn_t = stn_ref[...].T

        for hp in range(4):
            cs = slice(hp * 128, (hp + 1) * 128)
            kb = [k_ref[b * BLK:(b + 1) * BLK, cs] for b in range(n)]
            vb = [v_ref[b * BLK:(b + 1) * BLK, cs] for b in range(n)]
            dk_acc = [jnp.zeros((BLK, 128), F32) for _ in range(n)]
            dv_acc = [jnp.zeros((BLK, 128), F32) for _ in range(n)]
            for b in range(n):
                rs = slice(b * BLK, (b + 1) * BLK)
                q2s = _stack_heads(q_ref[rs, cs], in_a)
                do2s = _stack_heads(do_ref[rs, cs], in_a)
                if b == 0:
                    kprev = kp_ref[:, cs] if halo else kb[0]
                    vprev = vp_ref[:, cs] if halo else vb[0]
                    mask = mask0
                else:
                    kprev, vprev, mask = kb[b - 1], vb[b - 1], band
                kk = jnp.concatenate([kprev, kb[b]], axis=0)
                vv = jnp.concatenate([vprev, vb[b]], axis=0)
                lse_r, dl_r = stat_rows(st_t[b], hp)
                s_t = jnp.where(mask, _mm_nt(kk, q2s) * SCALE, NEG)
                p_t = jnp.exp(s_t - lse_r)
                ds_t = (p_t * (_mm_nt(vv, do2s) - dl_r)).astype(BF16)
                dkk = _mm(ds_t, q2s) * SCALE
                dvv = _mm(p_t.astype(BF16), do2s)
                dqs = _mm_tn(ds_t, kk) * SCALE
                dq_ref[rs, cs] = jnp.where(in_a[0], dqs[:BLK], dqs[BLK:]).astype(BF16)
                dk_acc[b] += dkk[BLK:]
                dv_acc[b] += dvv[BLK:]
                if b > 0:
                    dk_acc[b - 1] += dkk[:BLK]
                    dv_acc[b - 1] += dvv[:BLK]
            if halo:
                q2s = _stack_heads(qn_ref[:, cs], in_a)
                do2s = _stack_heads(don_ref[:, cs], in_a)
                lse_r, dl_r = stat_rows(stn_t, hp)
                s_t = jnp.where(mask_next, _mm_nt(kb[n - 1], q2s) * SCALE, NEG)
                p_t = jnp.exp(s_t - lse_r)
                ds_t = (p_t * (_mm_nt(vb[n - 1], do2s) - dl_r)).astype(BF16)
                dk_acc[n - 1] += _mm(ds_t, q2s) * SCALE
                dv_acc[n - 1] += _mm(p_t.astype(BF16), do2s)
            for b in range(n):
                dk_ref[b * BLK:(b + 1) * BLK, cs] = dk_acc[b].astype(BF16)
                dv_ref[b * BLK:(b + 1) * BLK, cs] = dv_acc[b].astype(BF16)

    cur = pl.BlockSpec((CH, AW), lambda i: (i, 0))
    cur_st = pl.BlockSpec((CH, 128), lambda i: (i, 0))
    prev = pl.BlockSpec((BLK, AW), lambda i: (jnp.maximum(n * i - 1, 0), 0))
    nxt = pl.BlockSpec((BLK, AW), lambda i: (jnp.minimum(n * i + n, NBLK - 1), 0))
    nxt_st = pl.BlockSpec((BLK, 128), lambda i: (jnp.minimum(n * i + n, NBLK - 1), 0))
    ins = [cur] * 4 + [cur_st] + ([prev, prev, nxt, nxt, nxt_st] if halo else [])
    args = (q, k, v, do, st) + ((k, v, q, do, st) if halo else ())
    return pl.pallas_call(
        body, name=name, grid=(T // CH,),
        in_specs=ins,
        out_specs=[cur] * 3,
        out_shape=[jax.ShapeDtypeStruct((T, AW), BF16)] * 3,
        compiler_params=_cp(("parallel",)),
    )(*args)


TH = 256
NCH = TH // CHUNK


def _hgrn_common(hq_ref, hf_ref, lbr_ref, tri_ref):
    r0 = lbr_ref[0:1, :]
    r1 = lbr_ref[1:2, :]
    mx = jnp.maximum(r0, r1)
    e0 = jnp.exp(r0 - mx)
    e1 = jnp.exp(r1 - mx)
    lb = e0 / (e0 + e1)
    hqv = hq_ref[...].astype(F32)
    sq = _sigmoid(hqv)
    qv = hqv * sq
    sf = _sigmoid(hf_ref[...].astype(F32))
    f = lb + (1.0 - lb) * sf
    kv = 1.0 - f
    g = jnp.log(f)
    cum = _mm_exact_l(tri_ref[...], g)
    lastb = jnp.concatenate(
        [jnp.broadcast_to(cum[c * CHUNK + CHUNK - 1:(c + 1) * CHUNK, :], (CHUNK, HW)) for c in range(NCH)], axis=0)
    ea = jnp.exp(cum)
    ena = jnp.exp(-cum)
    eend = jnp.exp(lastb - cum)
    return dict(lb=lb, hq=hqv, sq=sq, q=qv, sf=sf, f=f, k=kv, cum=cum, lastb=lastb, ea=ea, ena=ena, eend=eend,
                qd=qv * ea, ki=kv * ena, ke=kv * eend, dec=jnp.exp(lastb))


def _tri_mask(transposed=False):
    ti = lax.broadcasted_iota(jnp.int32, (TH, TH), 1 if transposed else 0)
    si = lax.broadcasted_iota(jnp.int32, (TH, TH), 0 if transposed else 1)
    return (si <= ti) & ((si // CHUNK) == (ti // CHUNK))


def _hgrn_fwd(hq, hf, hi, lbr, tri):
    def body(hq_ref, hf_ref, hi_ref, lbr_ref, tri_ref, rec_ref, sall_ref, st_scr):
        @pl.when(pl.program_id(0) == 0)
        def _():
            st_scr[...] = jnp.zeros_like(st_scr)

        w = _hgrn_common(hq_ref, hf_ref, lbr_ref, tri_ref)
        qd, ki, ke = w["qd"].astype(BF16), w["ki"].astype(BF16), w["ke"].astype(BF16)
        dec = w["dec"]
        vb = hi_ref[...]
        causal = _tri_mask()
        for h in range(4):
            cs = slice(h * 128, (h + 1) * 128)
            att = jnp.where(causal, _mm_nt(qd[:, cs], ki[:, cs]), 0.0)
            o_intra = _mm(att.astype(BF16), vb[:, cs])
            for c in range(NCH):
                rs = slice(c * CHUNK, (c + 1) * CHUNK)
                st = st_scr[:, cs]
                sall_ref[c, :, cs] = st
                rec_ref[rs, cs] = (o_intra[rs] + _mm_nt(qd[rs, cs], st.astype(BF16))).astype(BF16)
                st_scr[:, cs] = dec[c * CHUNK:c * CHUNK + 1, cs] * st + _mm_tn(vb[rs, cs], ke[rs, cs])

    tok = pl.BlockSpec((TH, HW), lambda i: (i, 0))
    return pl.pallas_call(
        body, name="hgrn_fwd", grid=(T // TH,),
        in_specs=[tok, tok, tok, pl.BlockSpec((2, HW), lambda i: (0, 0)), pl.BlockSpec((TH, TH), lambda i: (0, 0))],
        out_specs=[tok, pl.BlockSpec((NCH, 128, HW), lambda i: (i, 0, 0))],
        out_shape=[jax.ShapeDtypeStruct((T, HW), BF16), jax.ShapeDtypeStruct((T // CHUNK, 128, HW), F32)],
        scratch_shapes=[pltpu.VMEM((128, HW), F32)],
        compiler_params=_cp(("arbitrary",)),
    )(hq, hf, hi, lbr, tri)


def _hgrn_bwd(hq, hf, hi, lbr, tri, trit, drec, sall, dhg, rout, routb):
    NT = T // TH

    def body(hq_ref, hf_ref, hi_ref, lbr_ref, tri_ref, trit_ref, do_ref, sall_ref, dhg_ref, rout_r, routb_r,
             dph_ref, small_ref, pout_o, poutr_o,
             dst_scr, dlb_scr, dqd_scr, dki_scr, dke_scr, dlast_scr, send_sems, recv_sems, loc_sems):
        step = pl.program_id(0)
        loc, rem = _chip_copies(_w_out_piece, rout_r, routb_r, pout_o, poutr_o, send_sems, recv_sems,
                                loc_sems.at[0])

        @pl.when(step == 0)
        def _():
            dst_scr[...] = jnp.zeros_like(dst_scr)
            dlb_scr[...] = jnp.zeros_like(dlb_scr)
            for cp in loc + rem:
                cp.start()

        w = _hgrn_common(hq_ref, hf_ref, lbr_ref, tri_ref)
        qd, ki, ke = w["qd"].astype(BF16), w["ki"].astype(BF16), w["ke"].astype(BF16)
        dec = w["dec"]
        vb = hi_ref[...]
        dob = do_ref[...].astype(BF16)
        causal = _tri_mask()
        causal_t = _tri_mask(transposed=True)
        for h in range(4):
            cs = slice(h * 128, (h + 1) * 128)
            att_t = jnp.where(causal_t, _mm_nt(ki[:, cs], qd[:, cs]), 0.0).astype(BF16)
            datt_t = jnp.where(causal_t, _mm_nt(vb[:, cs], dob[:, cs]), 0.0).astype(BF16)
            datt = jnp.where(causal, _mm_nt(dob[:, cs], vb[:, cs]), 0.0).astype(BF16)
            dv_intra = _mm(att_t, dob[:, cs])
            dqd_intra = _mm(datt, ki[:, cs])
            dki_scr[:, cs] = _mm(datt_t, qd[:, cs])
            for c in reversed(range(NCH)):
                rs = slice(c * CHUNK, (c + 1) * CHUNK)
                dec_c = dec[c * CHUNK:c * CHUNK + 1, :]
                st = sall_ref[c, :, cs]
                dst = dst_scr[:, cs]
                dstb = dst.astype(BF16)
                dph_ref[rs, 2 * HW + h * 128:2 * HW + (h + 1) * 128] = (
                    dv_intra[rs] + _mm_nt(ke[rs, cs], dstb)).astype(BF16)
                dqd_scr[rs, cs] = dqd_intra[rs] + _mm(dob[rs, cs], st.astype(BF16))
                dke_scr[rs, cs] = _mm(vb[rs, cs], dstb)
                ddec = jnp.sum(dst * st, axis=0, keepdims=True)
                dlast_scr[c:c + 1, cs] = ddec * dec_c[:, cs]
                dst_scr[:, cs] = dec_c[:, cs] * dst + _mm_tn(dob[rs, cs], qd[rs, cs])
        dqd, dki, dke = dqd_scr[...], dki_scr[...], dke_scr[...]
        dq = dqd * w["ea"]
        dk = dki * w["ena"] + dke * w["eend"]
        dcum = dqd * w["qd"] - dki * w["ki"] - dke * w["ke"]
        dkeke = dke * w["ke"]
        dlastb = jnp.concatenate(
            [jnp.broadcast_to(dlast_scr[c:c + 1, :] + jnp.sum(dkeke[c * CHUNK:(c + 1) * CHUNK], axis=0, keepdims=True),
                              (CHUNK, HW)) for c in range(NCH)], axis=0)
        dg = _mm_exact_l(trit_ref[...], dcum) + dlastb
        df = dg / w["f"] - dk
        lb, sf, sq = w["lb"], w["sf"], w["sq"]
        dph_ref[:, HW:2 * HW] = (df * (1.0 - lb) * sf * (1.0 - sf)).astype(BF16)
        dph_ref[:, 0:HW] = (dq * (sq * (1.0 + w["hq"] * (1.0 - sq)))).astype(BF16)
        dph_ref[:, 3 * HW:4 * HW] = dhg_ref[...]
        dlb_scr[...] += jnp.sum(df * (1.0 - sf), axis=0, keepdims=True)

        @pl.when(step == NT - 1)
        def _():
            gr = dlb_scr[...] * lb * (1.0 - lb)
            small_ref[...] = jnp.zeros_like(small_ref)
            small_ref[0:1, 0:HW] = gr
            small_ref[1:2, 0:HW] = -gr
            for cp in rem:
                cp.wait_recv()
            for cp in rem:
                cp.wait_send()
            for cp in loc:
                cp.wait()

    tok = pl.BlockSpec((TH, HW), lambda i: (NT - 1 - i, 0))
    const = lambda shape: pl.BlockSpec(shape, lambda i: (0,) * len(shape))
    hbm = pl.BlockSpec(memory_space=pltpu.HBM)
    return pl.pallas_call(
        body, name="hgrn_bwd", grid=(NT,),
        in_specs=[tok, tok, tok, const((2, HW)), const((TH, TH)), const((TH, TH)), tok,
                  pl.BlockSpec((NCH, 128, HW), lambda i: (NT - 1 - i, 0, 0)), tok, hbm, hbm],
        out_specs=[pl.BlockSpec((TH, NCOL // 2), lambda i: (NT - 1 - i, 0)), const((8, D)), hbm, hbm],
        out_shape=[jax.ShapeDtypeStruct((T, NCOL // 2), BF16), jax.ShapeDtypeStruct((8, D), F32),
                   jax.ShapeDtypeStruct((128, D), F32), jax.ShapeDtypeStruct((3, 128, D), BF16)],
        scratch_shapes=[pltpu.VMEM((128, HW), F32), pltpu.VMEM((1, HW), F32), pltpu.VMEM((TH, HW), F32),
                        pltpu.VMEM((TH, HW), F32), pltpu.VMEM((TH, HW), F32), pltpu.VMEM((8, HW), F32),
                        pltpu.SemaphoreType.DMA((3,)), pltpu.SemaphoreType.DMA((3,)), pltpu.SemaphoreType.DMA((1,))],
        compiler_params=_cp(("arbitrary",)),
    )(hq, hf, hi, lbr, tri, trit, drec, sall, dhg, rout, routb)


def _fwd_out(o1, o4, o16, l1, l4, l16, rec, ag, hg, x, tgt, anw, hnw, fnw, wout_full, gmat, emat, selmat):
    TT = 256

    def body(o1_r, o4_r, o16_r, l1_r, l4_r, l16_r, rec_r, ag_r, hg_r, x_r, tgt_r, anw_r, hnw_r, fnw_r, wo_r, g_r,
             e_r, sel_r, dx2_o, do1_o, do4_o, do16_o, st1_o, st4_o, st16_o, drec_o, dag_o, dhg_o,
             rout_o, routb_o, small_o, scr_a, scr_b, scr_c, gwout_o, rbuf, send_sems, recv_sems):
        @pl.when(pl.program_id(0) == 0)
        def _():
            gwout_o[...] = jnp.zeros_like(gwout_o)
            small_o[...] = jnp.zeros_like(small_o)

        def unperm(r4, r16):
            return _unperm_load(r4, r16, scr_a, scr_b, scr_c)

        def perm_out(val, p1, p4, p16, dt):
            _perm_store(val, scr_a, scr_b, p1, p4, p16, dt)

        o4u, o16u = unperm(o4_r, o16_r)
        l4c, l16c = unperm(l4_r, l16_r)
        l1c = l1_r[...]
        mxc = jnp.maximum(jnp.maximum(l1c, l4c), l16c)
        w1c, w4c, w16c = jnp.exp(l1c - mxc), jnp.exp(l4c - mxc), jnp.exp(l16c - mxc)
        denc = w1c + w4c + w16c
        lane = lax.broadcasted_iota(jnp.int32, (1, 128), 1)
        lse_c = jnp.where(lane < 8, mxc + jnp.log(denc), 0.0)
        em = e_r[...]
        wn1 = _mm_exact_r(w1c / denc, em)
        wn4 = _mm_exact_r(w4c / denc, em)
        o1v = o1_r[...].astype(F32)
        attn = wn1 * o1v + wn4 * o4u + (1.0 - wn1 - wn4) * o16u
        gm = g_r[...]

        def head_mean_a(t):
            return jnp.concatenate([_mm_exact_r(t[:, :256], gm), _mm_exact_r(t[:, 256:], gm)], axis=1)

        def head_mean_h(t):
            return jnp.concatenate(
                [jnp.broadcast_to(jnp.mean(t[:, h * 128:(h + 1) * 128], axis=-1, keepdims=True), (TT, 128))
                 for h in range(4)], axis=1)

        rs_a = lax.rsqrt(head_mean_a(attn * attn) + EPS)
        n_a = attn * rs_a
        agv = ag_r[...].astype(F32)
        sg_a = _sigmoid(agv)
        si_a = agv * sg_a
        anw_v = anw_r[...]
        y_a = (n_a * anw_v) * si_a
        recv = rec_r[...].astype(F32)
        rs_h = lax.rsqrt(head_mean_h(recv * recv) + EPS)
        n_h = recv * rs_h
        hgv = hg_r[...].astype(F32)
        sg_h = _sigmoid(hgv)
        si_h = hgv * sg_h
        hnw_v = hnw_r[...]
        y_h = (n_h * hnw_v) * si_h
        mixed = jnp.concatenate([y_a, y_h], axis=1).astype(BF16)
        xv = x_r[...]
        x2 = xv + _mm(mixed, wo_r[...])
        r2 = lax.rsqrt(jnp.mean(x2 * x2, axis=-1, keepdims=True) + EPS)
        fnw_v = fnw_r[...]
        xn = x2 * r2
        err = xn * fnw_v - tgt_r[...]
        small_o[2:3, :] += 0.5 * jnp.sum(jnp.mean(err * err, axis=-1, keepdims=True), axis=0, keepdims=True)
        dy = err * (1.0 / D)
        small_o[0:1, :] += jnp.sum(dy * xn, axis=0, keepdims=True)
        dyw = dy * fnw_v
        dx2 = r2 * dyw - x2 * ((r2 * r2 * r2) * jnp.mean(dyw * x2, axis=-1, keepdims=True))
        dx2_o[...] = dx2
        dx2b = dx2.astype(BF16)
        gwout_o[...] += _mm_tn(mixed, dx2b)
        dmix = _mm_nt(dx2b, wo_r[...])
        dm_a, dm_h = dmix[:, :AW], dmix[:, AW:]
        dag_o[...] = (dm_a * (n_a * anw_v) * (sg_a * (1.0 + agv * (1.0 - sg_a)))).astype(BF16)
        dn_a = dm_a * anw_v * si_a
        small_o[1:2, 0:AW] += jnp.sum(dm_a * n_a * si_a, axis=0, keepdims=True)
        dattn = rs_a * (dn_a - n_a * head_mean_a(dn_a * n_a))
        perm_out(dattn, do1_o, do4_o, do16_o, BF16)
        stats = lse_c + _mm_exact_r(dattn * attn, sel_r[...])
        perm_out(stats, st1_o, st4_o, st16_o, F32)
        dhg_o[...] = (dm_h * (n_h * hnw_v) * (sg_h * (1.0 + hgv * (1.0 - sg_h)))).astype(BF16)
        dn_h = dm_h * hnw_v * si_h
        small_o[1:2, AW:] += jnp.sum(dm_h * n_h * si_h, axis=0, keepdims=True)
        drec_o[...] = (rs_h * (dn_h - n_h * head_mean_h(dn_h * n_h))).astype(BF16)

        @pl.when(pl.program_id(0) == T // TT - 1)
        def _():
            x, y, c = lax.axis_index("x"), lax.axis_index("y"), lax.axis_index("c")
            cps = [pltpu.make_async_remote_copy(
                src_ref=gwout_o.at[pl.ds(pl.multiple_of(j * 256 + (1 - c) * 128, 128), 128), :], dst_ref=rbuf.at[j],
                send_sem=send_sems.at[j], recv_sem=recv_sems.at[j], device_id=(x, y, 1 - c), device_id_type=MESH)
                for j in range(4)]
            for cp in cps:
                cp.start()
            for j, cp in enumerate(cps):
                cp.wait_recv()
                red = gwout_o[pl.ds(pl.multiple_of(j * 256 + c * 128, 128), 128), :] + rbuf[j]
                rout_o[j * 128:(j + 1) * 128, :] = red
                routb_o[j * 128:(j + 1) * 128, :] = red.astype(BF16)
            for cp in cps:
                cp.wait_send()

    tok = lambda w: pl.BlockSpec((TT, w), lambda i: (i, 0))
    d4 = pl.BlockSpec((4, TT // 4, AW), lambda i: (0, i, 0))
    d16 = pl.BlockSpec((16, TT // 16, AW), lambda i: (0, i, 0))
    const = lambda shape: pl.BlockSpec(shape, lambda i: (0,) * len(shape))
    sd = lambda shape, dt: jax.ShapeDtypeStruct(shape, dt)
    c4 = pl.BlockSpec((4, TT // 4, 128), lambda i: (0, i, 0))
    c16 = pl.BlockSpec((16, TT // 16, 128), lambda i: (0, i, 0))
    p3 = lambda w, dt: [sd((T, w), dt), sd((4, T // 4, w), dt), sd((16, T // 16, w), dt)]
    return pl.pallas_call(
        body, name="fwd_out", grid=(T // TT,),
        in_specs=[tok(AW), d4, d16, tok(128), c4, c16, tok(AW), tok(AW), tok(AW), tok(D), tok(D),
                  const((1, AW)), const((1, HW)), const((1, D)), const((D, D)), const((256, 256)),
                  const((128, AW)), const((AW, 128))],
        out_specs=[tok(D)] + [tok(AW), d4, d16] + [tok(128), c4, c16] + [tok(AW)] * 3
        + [const((512, D)), const((512, D)), const((8, D))],
        out_shape=[sd((T, D), F32)] + p3(AW, BF16) + p3(128, F32)
        + [sd((T, AW), BF16), sd((T, AW), BF16), sd((T, AW), BF16), sd((512, D), F32), sd((512, D), BF16),
           sd((8, D), F32)],
        scratch_shapes=[pltpu.VMEM((4, TT, 128), F32)] * 3 + [pltpu.VMEM((D, D), F32),
                        pltpu.VMEM((4, 128, D), F32), pltpu.SemaphoreType.DMA((4,)), pltpu.SemaphoreType.DMA((4,))],
        compiler_params=_cp(("arbitrary",)),
    )(o1, o4, o16, l1, l4, l16, rec, ag, hg, x, tgt, anw, hnw, fnw, wout_full, gmat, emat, selmat)


def _dproj_build(dq, dk, dv, dag, pos):
    TT = 256

    def body(dq1, dq4, dq16, dk1, dk4, dk16, dv1, dv4, dv16, dag_r, pos_r, dproj_o, scr_a, scr_b, scr_c):
        def unperm_sum(r1, r4, r16):
            u4, u16 = _unperm_load(r4, r16, scr_a, scr_b, scr_c)
            return r1[...] + u4 + u16

        cosf, s1, s2 = _rope_tables(pos_r[...])
        dproj_o[:, 0:512] = _rope_bwd(unperm_sum(dq1, dq4, dq16), cosf, s1, s2).astype(BF16)
        dproj_o[:, 512:1024] = _rope_bwd(unperm_sum(dk1, dk4, dk16), cosf, s1, s2).astype(BF16)
        dproj_o[:, 1024:1536] = unperm_sum(dv1, dv4, dv16).astype(BF16)
        dproj_o[:, 1536:2048] = dag_r[...]

    tok = lambda w: pl.BlockSpec((TT, w), lambda i: (i, 0))
    d4 = pl.BlockSpec((4, TT // 4, AW), lambda i: (0, i, 0))
    d16 = pl.BlockSpec((16, TT // 16, AW), lambda i: (0, i, 0))
    return pl.pallas_call(
        body, name="dproj_build", grid=(T // TT,),
        in_specs=[tok(AW), d4, d16] * 3 + [tok(AW), tok(1)],
        out_specs=tok(NCOL // 2),
        out_shape=jax.ShapeDtypeStruct((T, NCOL // 2), BF16),
        scratch_shapes=[pltpu.VMEM((4, TT, 128), F32)] * 3,
        compiler_params=_cp(("parallel",)),
    )(*dq, *dk, *dv, dag, pos)


def _bwd_x(dproj_a, dproj_h, x, dx2, mixw, w_full, rin, rinb, small4, small6, pout_own, pout_rem):
    TT = 256
    NT = T // TT

    def body(dpa_r, dph_r, x_r, dx2_r, mw_r, w_r, rin_r, rinb_r, s4_r, s6_r, poo_r, por_r,
             gx_o, pin_o, pinr_o, sall_o, fin_o, fout_o, sbuf, v_own, v_rem, vo_own, vo_rem, sin, sout, got_in,
             got_out, send_sems, recv_sems, loc_sems, share_send, share_recv, fin_sems):
        i = pl.program_id(0)
        loc, rem = _chip_copies(_w_in_piece, rin_r, rinb_r, pin_o, pinr_o, send_sems, recv_sems, loc_sems.at[0])

        @pl.when(i == 0)
        def _():
            sbuf[...] = jnp.zeros_like(sbuf)
            for cp in loc + rem:
                cp.start()

        dhn = _mm_nt(dpa_r[...], w_r[:, 0:NCOL // 2]) + _mm_nt(dph_r[...], w_r[:, NCOL // 2:NCOL])
        xv = x_r[...]
        r = lax.rsqrt(jnp.mean(xv * xv, axis=-1, keepdims=True) + EPS)
        dxw = dhn * mw_r[...]
        gx_o[...] = dx2_r[...] + r * dxw - xv * ((r * r * r) * jnp.mean(dxw * xv, axis=-1, keepdims=True))
        sbuf[16:17, :] += jnp.sum(dhn * (xv * r), axis=0, keepdims=True)

        @pl.when(i == NT - 1)
        def _():
            sbuf[0:8, :] = s4_r[...]
            sbuf[8:16, :] = s6_r[...]
            sloc, srem = _small_copies(sbuf, sall_o, send_sems, recv_sems, loc_sems.at[1])
            for cp in sloc + srem:
                cp.start()
            for cp in rem + srem:
                cp.wait_recv()
            for cp in rem + srem:
                cp.wait_send()
            for cp in loc + sloc:
                cp.wait()
            mx, my, c = lax.axis_index("x"), lax.axis_index("y"), lax.axis_index("c")
            loads = [pltpu.make_async_copy(pin_o, v_own, fin_sems.at[0]),
                     pltpu.make_async_copy(pinr_o, v_rem, fin_sems.at[1]),
                     pltpu.make_async_copy(poo_r, vo_own, fin_sems.at[2]),
                     pltpu.make_async_copy(por_r, vo_rem, fin_sems.at[3])]
            for cp in loads:
                cp.start()
            for cp in loads:
                cp.wait()
            sout[...] = ((vo_own[...] + vo_rem[0].astype(F32)) + vo_rem[1].astype(F32)) + vo_rem[2].astype(F32)
            sin[...] = ((v_own[...] + v_rem[0].astype(F32)) + v_rem[1].astype(F32)) + v_rem[2].astype(F32)
            swap = [pltpu.make_async_remote_copy(src_ref=sin, dst_ref=got_in, send_sem=share_send.at[0],
                                                 recv_sem=share_recv.at[0], device_id=(mx, my, 1 - c),
                                                 device_id_type=MESH),
                    pltpu.make_async_remote_copy(src_ref=sout, dst_ref=got_out, send_sem=share_send.at[1],
                                                 recv_sem=share_recv.at[1], device_id=(mx, my, 1 - c),
                                                 device_id_type=MESH)]
            for cp in swap:
                cp.start()
            mine = [pltpu.make_async_copy(sin, fin_o.at[c], fin_sems.at[0]),
                    pltpu.make_async_copy(sout, fout_o.at[c], fin_sems.at[1])]
            for cp in mine:
                cp.start()
            for cp in swap:
                cp.wait_recv()
            theirs = [pltpu.make_async_copy(got_in, fin_o.at[1 - c], fin_sems.at[2]),
                      pltpu.make_async_copy(got_out, fout_o.at[1 - c], fin_sems.at[3])]
            for cp in theirs:
                cp.start()
            for cp in swap:
                cp.wait_send()
            for cp in mine + theirs:
                cp.wait()

    tok = lambda w: pl.BlockSpec((TT, w), lambda i: (i, 0))
    const = lambda shape: pl.BlockSpec(shape, lambda i: (0,) * len(shape))
    hbm = pl.BlockSpec(memory_space=pltpu.HBM)
    return pl.pallas_call(
        body, name="bwd_x", grid=(NT,),
        in_specs=[tok(NCOL // 2), tok(NCOL // 2), tok(D), tok(D), const((1, D)), const((D, NCOL)), hbm, hbm,
                  const((8, D)), const((8, D)), hbm, hbm],
        out_specs=[tok(D), hbm, hbm, hbm, hbm, hbm],
        out_shape=[jax.ShapeDtypeStruct((T, D), F32),
                   jax.ShapeDtypeStruct((512, 1024), F32), jax.ShapeDtypeStruct((3, 512, 1024), BF16),
                   jax.ShapeDtypeStruct((8, 24, D), F32),
                   jax.ShapeDtypeStruct((2, 512, 1024), F32), jax.ShapeDtypeStruct((2, 128, D), F32)],
        scratch_shapes=[pltpu.VMEM((24, D), F32),
                        pltpu.VMEM((512, 1024), F32), pltpu.VMEM((3, 512, 1024), BF16),
                        pltpu.VMEM((128, D), F32), pltpu.VMEM((3, 128, D), BF16),
                        pltpu.VMEM((512, 1024), F32), pltpu.VMEM((128, D), F32),
                        pltpu.VMEM((512, 1024), F32), pltpu.VMEM((128, D), F32),
                        pltpu.SemaphoreType.DMA((10,)), pltpu.SemaphoreType.DMA((10,)), pltpu.SemaphoreType.DMA((2,)),
                        pltpu.SemaphoreType.DMA((2,)), pltpu.SemaphoreType.DMA((2,)), pltpu.SemaphoreType.DMA((4,))],
        compiler_params=_cp(("arbitrary",)),
    )(dproj_a, dproj_h, x, dx2, mixw, w_full, rin, rinb, small4, small6, pout_own, pout_rem)


def _grad_w_in(hn, dproj_a, dproj_h):
    TK = 1024
    NK = T // TK

    def body(hnt_r, dpa_r, dph_r, rin_o, rinb_o, acc, rbuf, obuf, obufb, send_sems, recv_sems, wb_sems):
        j = pl.program_id(0)
        kk = pl.program_id(1)
        x, y, c = lax.axis_index("x"), lax.axis_index("y"), lax.axis_index("c")
        mine = pl.ds(pl.multiple_of(c * 512, 512), 512)
        theirs = pl.ds(pl.multiple_of((1 - c) * 512, 512), 512)

        def send(jj):
            return pltpu.make_async_remote_copy(
                src_ref=acc.at[jj % 2, theirs, :], dst_ref=rbuf.at[jj], send_sem=send_sems.at[jj],
                recv_sem=recv_sems.at[jj], device_id=(x, y, 1 - c), device_id_type=MESH)

        def writeback(jj):
            cols = pl.ds(jj * 1024, 1024)
            return [pltpu.make_async_copy(obuf.at[jj % 2], rin_o.at[:, cols], wb_sems.at[jj % 2]),
                    pltpu.make_async_copy(obufb.at[jj % 2], rinb_o.at[:, cols], wb_sems.at[2 + jj % 2])]

        def wait_writeback(jj):
            for cp in writeback(jj):
                cp.wait()

        def finalize(jj):
            send(jj).wait_recv()
            red = acc[jj % 2, mine, :] + rbuf[jj]
            obuf[jj % 2] = red
            obufb[jj % 2] = red.astype(BF16)
            for cp in writeback(jj):
                cp.start()

        prod = _mm(hnt_r[...], jnp.where(j < 2, dpa_r[...], dph_r[...]))

        @pl.when(kk == 0)
        def _():
            for jj in (2, 3):
                @pl.when(j == jj)
                def _():
                    send(jj - 2).wait_send()
            acc[j % 2] = prod

        @pl.when(kk > 0)
        def _():
            acc[j % 2] += prod

        @pl.when(kk == NK - 1)
        def _():
            for jj in range(4):
                @pl.when(j == jj)
                def _():
                    send(jj).start()
                    if jj in (1, 2):
                        finalize(jj - 1)
                    if jj == 3:
                        wait_writeback(0)
                        finalize(2)
                        wait_writeback(1)
                        finalize(3)
                        wait_writeback(2)
                        wait_writeback(3)
                        send(2).wait_send()
                        send(3).wait_send()

    hbm = pl.BlockSpec(memory_space=pltpu.HBM)
    return pl.pallas_call(
        body, name="grad_w_in", grid=(4, NK),
        in_specs=[pl.BlockSpec((D, TK), lambda j, kk: (0, kk)),
                  pl.BlockSpec((TK, 1024), lambda j, kk: (jnp.where(j < 2, kk, NK - 1), jnp.minimum(j, 1))),
                  pl.BlockSpec((TK, 1024), lambda j, kk: (jnp.where(j < 2, 0, kk), jnp.maximum(j - 2, 0)))],
        out_specs=[hbm, hbm],
        out_shape=[jax.ShapeDtypeStruct((512, NCOL), F32), jax.ShapeDtypeStruct((512, NCOL), BF16)],
        scratch_shapes=[pltpu.VMEM((2, D, 1024), F32), pltpu.VMEM((4, 512, 1024), F32), pltpu.VMEM((2, 512, 1024), F32),
                        pltpu.VMEM((2, 512, 1024), BF16),
                        pltpu.SemaphoreType.DMA((4,)), pltpu.SemaphoreType.DMA((4,)), pltpu.SemaphoreType.DMA((4,))],
        compiler_params=_cp(("arbitrary", "arbitrary")),
    )(hn, dproj_a, dproj_h)


def _w_in_piece(ref, j):
    return ref.at[:, pl.ds(j * 1024, 1024)]


def _w_out_piece(ref, j):
    return ref.at[pl.ds(j * 128, 128), :]


def _chip_copies(piece, src_r, srcb_r, own_o, rem_o, send_sems, recv_sems, loc_sem):
    x, y, c = lax.axis_index("x"), lax.axis_index("y"), lax.axis_index("c")
    chips = [(1 - x, y), (x, 1 - y), (1 - x, 1 - y)]
    loc = [pltpu.make_async_copy(piece(src_r, 2 * x + y), own_o, loc_sem)]
    rem = [pltpu.make_async_remote_copy(
        src_ref=piece(srcb_r, 2 * px + py), dst_ref=rem_o.at[k], send_sem=send_sems.at[k],
        recv_sem=recv_sems.at[k], device_id=(px, py, c), device_id_type=MESH) for k, (px, py) in enumerate(chips)]
    return loc, rem


def _small_copies(small_r, sall_o, send_sems, recv_sems, loc_sem):
    x, y, c = lax.axis_index("x"), lax.axis_index("y"), lax.axis_index("c")
    me = 4 * x + 2 * y + c
    loc = [pltpu.make_async_copy(small_r, sall_o.at[me], loc_sem)]
    rem = []
    k = 3
    for fx in range(2):
        for fy in range(2):
            for fc in range(2):
                if fx or fy or fc:
                    peer = (1 - x if fx else x, 1 - y if fy else y, 1 - c if fc else c)
                    rem.append(pltpu.make_async_remote_copy(
                        src_ref=small_r, dst_ref=sall_o.at[me], send_sem=send_sems.at[k],
                        recv_sem=recv_sems.at[k], device_id=peer, device_id_type=MESH))
                    k += 1
    return loc, rem


def _adamw_math(w, g, m, v):
    m = B1 * m + (1.0 - B1) * g
    v = B2 * v + (1.0 - B2) * (g * g)
    m_hat = m / (1.0 - B1 ** STEP)
    v_hat = v / (1.0 - B2 ** STEP)
    delta = -LR * (m_hat / (jnp.sqrt(v_hat) + AEPS) + WD * w)
    return delta, m, v


def _adamw(big_in, big_out, sall, params):
    def body(*refs):
        wi, gi, mi, vi, wo, go, mo, vo, sall_r = refs[:9]
        ins = refs[9:24]
        di_o, mi_o, vi_o, do_o, mo_o, vo_o = refs[24:30]
        outs = refs[30:]
        d, mm, vv = _adamw_math(wi[...], gi[...], mi[...], vi[...])
        di_o[...] = d
        mi_o[...] = mm
        vi_o[...] = vv

        @pl.when(pl.program_id(0) == 0)
        def _():
            d, mm, vv = _adamw_math(wo[...], go[...], mo[...], vo[...])
            do_o[...] = d
            mo_o[...] = mm
            vo_o[...] = vv
            tot = sall_r[0]
            for dv in range(1, 8):
                tot = tot + sall_r[dv]
            grads = [tot[16:17, :], tot[1:2, 0:AW], tot[1:2, AW:], tot[8:10, 0:HW], tot[0:1, :]]
            outs[0][...] = tot[2:3, 0:1]
            for p in range(5):
                w_r, m_r, v_r = ins[3 * p:3 * p + 3]
                g = grads[p]
                d, mm, vv = _adamw_math(w_r[...], g, m_r[...], v_r[...])
                outs[1 + 4 * p][...] = g
                outs[2 + 4 * p][...] = d
                outs[3 + 4 * p][...] = mm
                outs[4 + 4 * p][...] = vv

    flat = [a for p in params for a in p]
    shapes = [jax.ShapeDtypeStruct((D, 1024), F32)] * 3 + [jax.ShapeDtypeStruct((256, D), F32)] * 3
    shapes += [jax.ShapeDtypeStruct((1, 1), F32)]
    for p in params:
        shapes += [jax.ShapeDtypeStruct(p[0].shape, F32)] * 4
    vm = pl.BlockSpec(memory_space=pltpu.VMEM)
    rows = pl.BlockSpec((256, 1024), lambda i: (i, 0))
    whole = pl.BlockSpec((256, D), lambda i: (0, 0))
    return pl.pallas_call(
        body, name="adamw", grid=(4,),
        in_specs=[rows] * 4 + [whole] * 4 + [vm] * 16, out_specs=[rows] * 3 + [whole] * 3 + [vm] * 21,
        out_shape=shapes,
        compiler_params=_cp(("arbitrary",)),
    )(*big_in, *big_out, sall, *flat)


def kernel(x, positions, w_in, w_out, mix_norm_w, attn_out_norm_w, hgrn_out_norm_w, hgrn_lb_raw, final_norm_w, loss_target, m_w_in, m_w_out, m_mix_norm_w, m_attn_out_norm_w, m_hgrn_out_norm_w, m_hgrn_lb_raw, m_final_norm_w, v_w_in, v_w_out, v_mix_norm_w, v_attn_out_norm_w, v_hgrn_out_norm_w, v_hgrn_lb_raw, v_final_norm_w):
    xs = x.reshape(T, D)
    tgt = loss_target.reshape(T, D)
    pos = positions.reshape(T, 1)
    fnw = final_norm_w.reshape(1, D)

    ti = np.arange(TH)
    tri_np = ((ti[:, None] // CHUNK == ti[None, :] // CHUNK) & (ti[None, :] <= ti[:, None])).astype(np.float32)
    tri = jnp.asarray(tri_np, BF16)
    trit = jnp.asarray(tri_np.T, BF16)
    hi_ = np.arange(AW) // HEAD
    gmat = jnp.asarray((hi_[:256, None] == hi_[None, :256]).astype(np.float32) / HEAD, BF16)
    emat_np = (np.arange(128)[:, None] == hi_[None, :]).astype(np.float32)
    sel_np = (8 + hi_[:, None] == np.arange(128)[None, :]).astype(np.float32)
    emat = jnp.asarray(emat_np, BF16)
    selmat = jnp.asarray(sel_np, BF16)

    jm_arr = (2 * lax.axis_index("x") + lax.axis_index("y")).astype(jnp.int32).reshape(1)
    (hn, q1, k1, v1, q4, k4, v4, q16, k16, v16, ag, hq, hf, hi, hg, w_full, wout4) = _fwd_in(
        xs, pos, mix_norm_w, w_in.reshape(D, 1024), w_out.reshape(256, D), jm_arr)
    wout_full = wout4.reshape(D, D)
    flat = lambda a: a.reshape(T, AW)
    o1, l1 = _attn_fwd(q1, k1, v1, T // BLK, "attn_fwd_d1")
    o4, l4 = _attn_fwd(flat(q4), flat(k4), flat(v4), T // 4 // BLK, "attn_fwd_d4")
    o16, l16 = _attn_fwd(flat(q16), flat(k16), flat(v16), T // 16 // BLK, "attn_fwd_d16")
    rec, sall = _hgrn_fwd(hq, hf, hi, hgrn_lb_raw, tri)

    (dx2, do1, do4, do16, st1, st4, st16, drec, dag, dhg, rout, routb, small4) = _fwd_out(
        o1, o4.reshape(4, T // 4, AW), o16.reshape(16, T // 16, AW),
        l1, l4.reshape(4, T // 4, 128), l16.reshape(16, T // 16, 128),
        rec, ag, hg, xs, tgt, attn_out_norm_w, hgrn_out_norm_w, fnw, wout_full, gmat, emat, selmat)

    fst = lambda a: a.reshape(T, 128)
    dq1, dk1, dv1 = _attn_bwd(q1, k1, v1, do1, st1, T // BLK, "attn_bwd_d1")
    dq4, dk4, dv4 = _attn_bwd(flat(q4), flat(k4), flat(v4), flat(do4), fst(st4), T // 4 // BLK, "attn_bwd_d4")
    dq16, dk16, dv16 = _attn_bwd(flat(q16), flat(k16), flat(v16), flat(do16), fst(st16), T // 16 // BLK,
                                 "attn_bwd_d16")
    dproj_h, small6, pout_own, pout_rem = _hgrn_bwd(hq, hf, hi, hgrn_lb_raw, tri, trit, drec, sall, dhg,
                                                    rout, routb)

    r4 = lambda a: a.reshape(4, T // 4, AW)
    r16 = lambda a: a.reshape(16, T // 16, AW)
    dproj_a = _dproj_build((dq1, r4(dq4), r16(dq16)), (dk1, r4(dk4), r16(dk16)), (dv1, r4(dv4), r16(dv16)),
                           dag, pos)
    rin, rinb = _grad_w_in(hn, dproj_a, dproj_h)
    gx, _, _, small_all, fin, fout = _bwd_x(dproj_a, dproj_h, xs, dx2, mix_norm_w, w_full, rin, rinb,
                                            small4, small6, pout_own, pout_rem)
    g_w_in = fin.reshape(D, 1024)
    g_w_out = fout.reshape(256, D)

    params = [(mix_norm_w, m_mix_norm_w, v_mix_norm_w),
              (attn_out_norm_w, m_attn_out_norm_w, v_attn_out_norm_w),
              (hgrn_out_norm_w, m_hgrn_out_norm_w, v_hgrn_out_norm_w),
              (hgrn_lb_raw, m_hgrn_lb_raw, v_hgrn_lb_raw),
              (fnw, m_final_norm_w.reshape(1, D), v_final_norm_w.reshape(1, D))]
    d_in, nm_in, nv_in, d_out, nm_out, nv_out, *so = _adamw(
        (w_in.reshape(D, 1024), g_w_in, m_w_in.reshape(D, 1024), v_w_in.reshape(D, 1024)),
        (w_out.reshape(256, D), g_w_out, m_w_out.reshape(256, D), v_w_out.reshape(256, D)), small_all, params)
    loss = so[0].reshape(())
    g_s = [so[1 + 4 * p] for p in range(5)]
    d_s = [so[2 + 4 * p] for p in range(5)]
    m_s = [so[3 + 4 * p] for p in range(5)]
    v_s = [so[4 + 4 * p] for p in range(5)]
    for lst in (g_s, d_s, m_s, v_s):
        lst[4] = lst[4].reshape(D)

    return (loss, gx.reshape(1, T, D),
            g_w_in.reshape(1, D, 1024), g_w_out.reshape(1, 256, D), *g_s,
            d_in.reshape(1, D, 1024), d_out.reshape(1, 256, D), *d_s,
            nm_in.reshape(1, D, 1024), nm_out.reshape(1, 256, D), *m_s,
            nv_in.reshape(1, D, 1024), nv_out.reshape(1, 256, D), *v_s)
```

```python
import functools

import numpy as np
import jax
import jax.numpy as jnp
from jax import lax
from jax.experimental import pallas as pl
from jax.experimental.pallas import tpu as pltpu

F32 = jnp.float32
BF16 = jnp.bfloat16

T = 4096
D = 1024
AW = 512
HW = 512
NCOL = 4096
HEAD = 64
BLK = 128
CHUNK = 64
EPS = 1e-6
SCALE = HEAD ** -0.5
NEG = -1e30
ROPE_THETA = 500000.0
INV_FREQ = [float(v) for v in
            (np.float32(ROPE_THETA) ** (-(np.arange(8, dtype=np.float32)) * np.float32(0.125)))]
LR, B1, B2, AEPS, WD, STEP = 0.001, 0.9, 0.999, 1e-08, 0.01, 10
VMEM_LIMIT = 56 * 1024 * 1024
MESH = pl.DeviceIdType.MESH


def _cp(sem=None, **kw):
    return pltpu.CompilerParams(dimension_semantics=sem, vmem_limit_bytes=VMEM_LIMIT, **kw)


def _mm(a, b):
    return jnp.dot(a, b, preferred_element_type=F32)


def _mm_nt(a, b):
    return lax.dot_general(a, b, (((1,), (1,)), ((), ())), preferred_element_type=F32)


def _mm_tn(a, b):
    return lax.dot_general(a, b, (((0,), (0,)), ((), ())), preferred_element_type=F32)


def _split3(x):
    h = x.astype(BF16)
    r = x - h.astype(F32)
    m = r.astype(BF16)
    l = (r - m.astype(F32)).astype(BF16)
    return h, m, l


def _mm_exact_l(mat_bf, x):
    h, m, l = _split3(x)
    return _mm(mat_bf, h) + _mm(mat_bf, m) + _mm(mat_bf, l)


def _mm_exact_r(x, mat_bf):
    h = x.astype(BF16)
    l = (x - h.astype(F32)).astype(BF16)
    return _mm(h, mat_bf) + _mm(l, mat_bf)


def _sigmoid(x):
    return 0.5 * jnp.tanh(0.5 * x) + 0.5


def _rope_tables(pos):
    lane = lax.broadcasted_iota(jnp.int32, (1, 128), 1)
    jl = lane & 63
    fi = jl & 7
    inv = jnp.zeros((1, 128), F32)
    for kk in range(8):
        inv = jnp.where(fi == kk, INV_FREQ[kk], inv)
    ang = pos.astype(F32) * inv
    c = jnp.cos(ang)
    s = jnp.sin(ang)
    cosf = jnp.where(jl < 16, c, 1.0)
    s1 = jnp.where(jl < 8, -s, 0.0)
    s2 = jnp.where((jl >= 8) & (jl < 16), s, 0.0)
    return cosf, s1, s2


def _rope(t, cosf, s1, s2):
    parts = []
    for ci in range(t.shape[1] // 128):
        tc = t[:, ci * 128:(ci + 1) * 128]
        parts.append(tc * cosf + pltpu.roll(tc, 120, 1) * s1 + pltpu.roll(tc, 8, 1) * s2)
    return jnp.concatenate(parts, axis=1)


def _rope_bwd(g, cosf, s1, s2):
    parts = []
    for ci in range(g.shape[1] // 128):
        gc = g[:, ci * 128:(ci + 1) * 128]
        parts.append(gc * cosf + pltpu.roll(gc * s1, 8, 1) + pltpu.roll(gc * s2, 120, 1))
    return jnp.concatenate(parts, axis=1)


def _perm_store(val, scr, scr2, o1, o4, o16, dt):
    n = val.shape[0]
    q = n // 4
    o1[...] = val.astype(dt)
    for ci in range(val.shape[1] // 128):
        cs = slice(ci * 128, (ci + 1) * 128)
        scr[ci] = val[:, cs]
        for r4 in range(4):
            part = scr[ci, pl.ds(r4, q, stride=4), :]
            o4[r4, :, cs] = part.astype(dt)
            scr2[ci, r4 * q:(r4 + 1) * q, :] = part
        for r4 in range(4):
            for b in range(4):
                o16[r4 + 4 * b, :, cs] = scr2[ci, pl.ds(r4 * q + b, q // 4, stride=4), :].astype(dt)


def _unperm_load(r4, r16, scr_a, scr_b, scr_c):
    n = scr_a.shape[1]
    q = n // 4
    nc = r4.shape[-1] // 128
    for ci in range(nc):
        cs = slice(ci * 128, (ci + 1) * 128)
        for rr in range(4):
            scr_a[ci, pl.ds(rr, q, stride=4), :] = r4[rr, :, cs].astype(F32)
        for rr in range(4):
            for b in range(4):
                scr_c[ci, pl.ds(rr * q + b, q // 4, stride=4), :] = r16[rr + 4 * b, :, cs].astype(F32)
        for rr in range(4):
            scr_b[ci, pl.ds(rr, q, stride=4), :] = scr_c[ci, rr * q:(rr + 1) * q, :]
    return (jnp.concatenate([scr_a[ci] for ci in range(nc)], axis=1),
            jnp.concatenate([scr_b[ci] for ci in range(nc)], axis=1))


def _fwd_in(x, pos, mixw, w_in, w_out, jm_arr):
    TT = 512
    NT = T // TT

    def body(jm_ref, x_ref, pos_ref, mw_ref, win_ref, wout_ref,
             hnt_ref, q1, k1, v1, q4, k4, v4, q16, k16, v16, ag, hq, hf, hi, hg, wfull_o, woutfull_o,
             wbuf, wobuf, hn_all, scr, scr2, stage, send_sems, recv_sems, loc_sems):
        s = pl.program_id(0)
        i = pl.program_id(1)
        mx, my, c = lax.axis_index("x"), lax.axis_index("y"), lax.axis_index("c")
        me, sibling = (mx, my, c), (mx, my, 1 - c)
        chips = [(mx, 1 - my), (1 - mx, my), (1 - mx, 1 - my)]
        jm = 2 * mx + my
        rows_in = [pl.ds(pl.multiple_of(h * 512, 512), 512) for h in (c, 1 - c)]
        rows_out = [pl.ds(pl.multiple_of(h * 128, 128), 128) for h in (c, 1 - c)]

        def blk(k):
            return lax.bitwise_xor(jm, k + 1)

        def rc(n, ref, to):
            return pltpu.make_async_remote_copy(src_ref=ref, dst_ref=ref, send_sem=send_sems.at[n],
                                                recv_sem=recv_sems.at[n], device_id=to, device_id_type=MESH)

        halves = [pl.ds(0, 512), pl.ds(512, 512)]
        send_in = lambda k, h: rc(12 + 2 * k + h, wbuf.at[jm, rows_in[0], halves[h]], (*chips[k], c))
        got_in = lambda k, h: rc(12 + 2 * k + h, wbuf.at[blk(k), rows_in[0], halves[h]], me)
        relay = lambda h: rc(16 + h, wbuf.at[blk(h), rows_in[0], halves[h]], (*chips[1 - h], c))
        got_relay = lambda h: rc(16 + h, wbuf.at[blk(2), rows_in[0], halves[h]], me)
        send_out = lambda k: rc(3 + k, wobuf.at[jm, rows_out[0], :], (*chips[k], c))
        got_out = lambda k: rc(3 + k, wobuf.at[blk(k), rows_out[0], :], me)
        pass_in = lambda k: rc(6 + k, wbuf.at[blk(k), rows_in[0], :], sibling)
        pass_out = lambda k: rc(9 + k, wobuf.at[blk(k), rows_out[0], :], sibling)
        passed_in = lambda k: rc(6 + k, wbuf.at[blk(k), rows_in[1], :], me)
        passed_out = lambda k: rc(9 + k, wobuf.at[blk(k), rows_out[1], :], me)

        def keep(j, n):
            return pltpu.make_async_copy(wbuf.at[j], wfull_o.at[:, pl.ds(j * 1024, 1024)], loc_sems.at[n])

        @pl.when((s == 0) & (i == 0))
        def _():
            for p in range(5):
                src = win_ref.at[pl.ds(p * 256, 256), :] if p < 4 else wout_ref
                load = pltpu.make_async_copy(src, stage, loc_sems.at[4])
                load.start()
                load.wait()
                if p < 4:
                    wbuf[jm, p * 256:(p + 1) * 256, :] = stage[...].astype(BF16)
                else:
                    wobuf[jm] = stage[...].astype(BF16)
            for k in range(2):
                for h in range(2):
                    send_in(k, h).start()
            keep(jm, 0).start()

        def arrive(k):
            if k == 0:
                for kk in range(2):
                    for h in range(2):
                        got_in(kk, h).wait_recv()
                relay(0).start()
                relay(1).start()
            if k == 2:
                got_relay(0).wait_recv()
                got_relay(1).wait_recv()
            pass_in(k).start()
            passed_in(k).wait_recv()
            keep(blk(k), k + 1).start()
            if k == 2:
                for kk in range(3):
                    send_out(kk).start()

        for k in range(3):
            pl.when((s == k + 1) & (i == 0))(functools.partial(arrive, k))

        tile = pl.ds(pl.multiple_of(i * TT, TT), TT)

        @pl.when(s == 0)
        def _():
            xv = x_ref[...]
            r = lax.rsqrt(jnp.mean(xv * xv, axis=-1, keepdims=True) + EPS)
            hnf = (xv * r) * mw_ref[...]
            hn_all[tile, :] = hnf.astype(BF16)
            hnt_ref[...] = hnf.T.astype(BF16)

        def project(jj):
            hn = hn_all[tile, :]
            lo = _mm(hn, wbuf[jj, :, 0:512])
            hi_cols = _mm(hn, wbuf[jj, :, 512:1024])
            if jj == 0:
                cosf, s1, s2 = _rope_tables(pos_ref[...])
                _perm_store(_rope(lo, cosf, s1, s2), scr, scr2, q1, q4, q16, BF16)
                _perm_store(_rope(hi_cols, cosf, s1, s2), scr, scr2, k1, k4, k16, BF16)
            elif jj == 1:
                _perm_store(lo, scr, scr2, v1, v4, v16, BF16)
                ag[...] = hi_cols.astype(BF16)
            elif jj == 2:
                hq[...] = lo.astype(BF16)
                hf[...] = hi_cols.astype(BF16)
            else:
                hi[...] = lo.astype(BF16)
                hg[...] = hi_cols.astype(BF16)

        j = lax.bitwise_xor(jm, s)
        for jj in range(4):
            pl.when(j == jj)(functools.partial(project, jj))

        @pl.when((s == 3) & (i == NT - 1))
        def _():
            for k in range(3):
                got_out(k).wait_recv()
                pass_out(k).start()
            for k in range(3):
                passed_out(k).wait_recv()
            out = pltpu.make_async_copy(wobuf, woutfull_o, loc_sems.at[4])
            out.start()
            for h in range(2):
                relay(h).wait_send()
                for k in range(2):
                    send_in(k, h).wait_send()
            for k in range(3):
                send_out(k).wait_send()
                pass_in(k).wait_send()
                pass_out(k).wait_send()
            keep(jm, 0).wait()
            for k in range(3):
                keep(blk(k), k + 1).wait()
            out.wait()

    def at_stage_of(jb):
        def index(s, i, jm_ref):
            sa = lax.bitwise_xor(jm_ref[0], jb)
            return jnp.where(s < sa, 0, jnp.where(s == sa, i, NT - 1))
        return index

    tok = lambda w, jb: pl.BlockSpec((TT, w), lambda s, i, jm_ref: (at_stage_of(jb)(s, i, jm_ref), 0))
    d4 = lambda jb: pl.BlockSpec((4, TT // 4, AW), lambda s, i, jm_ref: (0, at_stage_of(jb)(s, i, jm_ref), 0))
    d16 = lambda jb: pl.BlockSpec((16, TT // 16, AW), lambda s, i, jm_ref: (0, at_stage_of(jb)(s, i, jm_ref), 0))
    hbm = pl.BlockSpec(memory_space=pltpu.HBM)
    sd = lambda shape, dt: jax.ShapeDtypeStruct(shape, dt)
    in_own_stage = lambda s, i: jnp.where(s == 0, i, NT - 1)
    grid_spec = pltpu.PrefetchScalarGridSpec(
        num_scalar_prefetch=1, grid=(4, NT),
        in_specs=[pl.BlockSpec((TT, D), lambda s, i, jm_ref: (in_own_stage(s, i), 0)),
                  pl.BlockSpec((TT, 1), lambda s, i, jm_ref: (i, 0)),
                  pl.BlockSpec((1, D), lambda s, i, jm_ref: (0, 0)), hbm, hbm],
        out_specs=[pl.BlockSpec((D, TT), lambda s, i, jm_ref: (0, in_own_stage(s, i))),
                   tok(AW, 0), tok(AW, 0), tok(AW, 1), d4(0), d4(0), d4(1), d16(0), d16(0), d16(1),
                   tok(AW, 1), tok(AW, 2), tok(AW, 2), tok(AW, 3), tok(AW, 3), hbm, hbm],
        scratch_shapes=[pltpu.VMEM((4, D, 1024), BF16), pltpu.VMEM((4, 256, D), BF16), pltpu.VMEM((T, D), BF16),
                        pltpu.VMEM((4, TT, 128), F32), pltpu.VMEM((4, TT, 128), F32), pltpu.VMEM((256, 1024), F32),
                        pltpu.SemaphoreType.DMA((18,)),
                        pltpu.SemaphoreType.DMA((18,)), pltpu.SemaphoreType.DMA((6,))])
    return pl.pallas_call(
        body, name="fwd_in", grid_spec=grid_spec,
        out_shape=[sd((D, T), BF16)] + [sd((T, AW), BF16)] * 3 + [sd((4, T // 4, AW), BF16)] * 3
        + [sd((16, T // 16, AW), BF16)] * 3
        + [sd((T, AW), BF16)] * 5 + [sd((D, NCOL), BF16), sd((4, 256, D), BF16)],
        compiler_params=_cp(("arbitrary", "arbitrary")),
    )(jm_arr, x, pos, mixw, w_in, w_out)


def _band_mask(key_axis, nkeys=2 * BLK):
    shape = (nkeys, 2 * BLK) if key_axis == 0 else (2 * BLK, nkeys)
    kj = lax.broadcasted_iota(jnp.int32, shape, key_axis)
    qi = lax.broadcasted_iota(jnp.int32, shape, 1 - key_axis) & (BLK - 1)
    return (kj >= qi) & (kj <= qi + BLK), kj, qi


def _stack_heads(t2, in_a):
    z = jnp.zeros_like(t2)
    return jnp.concatenate([jnp.where(in_a[0], t2, z), jnp.where(in_a[1], t2, z)], axis=0)


def _attn_fwd(q, k, v, nb, name):
    n = min(4, nb)
    CH = n * BLK
    halo = nb > n

    def body(*refs):
        if halo:
            q_ref, k_ref, v_ref, kp_ref, vp_ref, o_ref, lse_ref = refs
        else:
            q_ref, k_ref, v_ref, o_ref, lse_ref = refs
        lane = lax.broadcasted_iota(jnp.int32, (1, 128), 1)
        in_a = [lane < HEAD, lane >= HEAD]
        band, kj, _ = _band_mask(1)
        thr0 = jnp.where((n * pl.program_id(0)) % nb == 0, BLK, 0) if halo else BLK
        mask0 = band & (kj >= thr0)
        for b in range(n):
            rs = slice(b * BLK, (b + 1) * BLK)
            stat = jnp.zeros((BLK, 128), F32)
            for hp in range(4):
                cs = slice(hp * 128, (hp + 1) * 128)
                q2s = _stack_heads(q_ref[rs, cs], in_a)
                if b == 0:
                    kprev = kp_ref[:, cs] if halo else k_ref[rs, cs]
                    vprev = vp_ref[:, cs] if halo else v_ref[rs, cs]
                    kk = jnp.concatenate([kprev, k_ref[rs, cs]], axis=0)
                    vv = jnp.concatenate([vprev, v_ref[rs, cs]], axis=0)
                    mask = mask0
                else:
                    kk = k_ref[(b - 1) * BLK:(b + 1) * BLK, cs]
                    vv = v_ref[(b - 1) * BLK:(b + 1) * BLK, cs]
                    mask = band
                s = jnp.where(mask, _mm_nt(q2s, kk) * SCALE, NEG)
                m = jnp.max(s, axis=-1, keepdims=True)
                p = jnp.exp(s - m)
                l = jnp.sum(p, axis=-1, keepdims=True)
                o = _mm(p.astype(BF16), vv) / l
                lse = m + jnp.log(l)
                o_ref[rs, cs] = jnp.where(in_a[0], o[:BLK], o[BLK:]).astype(BF16)
                stat = jnp.where(lane == 2 * hp, lse[:BLK], stat)
                stat = jnp.where(lane == 2 * hp + 1, lse[BLK:], stat)
            lse_ref[rs, :] = stat

    cur = pl.BlockSpec((CH, AW), lambda i: (i, 0))
    prev = pl.BlockSpec((BLK, AW), lambda i: (jnp.maximum(n * i - 1, 0), 0))
    return pl.pallas_call(
        body, name=name, grid=(T // CH,),
        in_specs=[cur, cur, cur] + ([prev, prev] if halo else []),
        out_specs=[cur, pl.BlockSpec((CH, 128), lambda i: (i, 0))],
        out_shape=[jax.ShapeDtypeStruct((T, AW), BF16), jax.ShapeDtypeStruct((T, 128), F32)],
        compiler_params=_cp(("parallel",)),
    )(*((q, k, v) + ((k, v) if halo else ())))


def _attn_bwd(q, k, v, do, st, nb, name):
    n = min(4, nb)
    CH = n * BLK
    NBLK = T // BLK
    halo = nb > n

    def body(*refs):
        if halo:
            (q_ref, k_ref, v_ref, do_ref, st_ref, kp_ref, vp_ref, qn_ref, don_ref, stn_ref,
             dq_ref, dk_ref, dv_ref) = refs
        else:
            q_ref, k_ref, v_ref, do_ref, st_ref, dq_ref, dk_ref, dv_ref = refs
        i = pl.program_id(0)
        lane = lax.broadcasted_iota(jnp.int32, (1, 128), 1)
        in_a = [lane < HEAD, lane >= HEAD]
        band, kj, _ = _band_mask(0)
        thr0 = jnp.where((n * i) % nb == 0, BLK, 0) if halo else BLK
        mask0 = band & (kj >= thr0)

        def stat_rows(st_t, hp):
            lse_r = jnp.concatenate([st_t[2 * hp:2 * hp + 1, :], st_t[2 * hp + 1:2 * hp + 2, :]], axis=1)
            dl_r = jnp.concatenate([st_t[8 + 2 * hp:9 + 2 * hp, :], st_t[9 + 2 * hp:10 + 2 * hp, :]], axis=1)
            return lse_r, dl_r

        st_t = [st_ref[b * BLK:(b + 1) * BLK, :].T for b in range(n)]
        if halo:
            nxt_thr = jnp.where((n * i + n) % nb == 0, 2 * BLK, 0)
            _, kj1, qi1 = _band_mask(0, BLK)
            mask_next = kj1 >= qi1 + nxt_thr
            stn_t = stn_ref[...].T

        for hp in range(4):
            cs = slice(hp * 128, (hp + 1) * 128)
            kb = [k_ref[b * BLK:(b + 1) * BLK, cs] for b in range(n)]
            vb = [v_ref[b * BLK:(b + 1) * BLK, cs] for b in range(n)]
            dk_acc = [jnp.zeros((BLK, 128), F32) for _ in range(n)]
            dv_acc = [jnp.zeros((BLK, 128), F32) for _ in range(n)]
            for b in range(n):
                rs = slice(b * BLK, (b + 1) * BLK)
                q2s = _stack_heads(q_ref[rs, cs], in_a)
                do2s = _stack_heads(do_ref[rs, cs], in_a)
                if b == 0:
                    kprev = kp_ref[:, cs] if halo else kb[0]
                    vprev = vp_ref[:, cs] if halo else vb[0]
                    mask = mask0
                else:
                    kprev, vprev, mask = kb[b - 1], vb[b - 1], band
                kk = jnp.concatenate([kprev, kb[b]], axis=0)
                vv = jnp.concatenate([vprev, vb[b]], axis=0)
                lse_r, dl_r = stat_rows(st_t[b], hp)
                s_t = jnp.where(mask, _mm_nt(kk, q2s) * SCALE, NEG)
                p_t = jnp.exp(s_t - lse_r)
                ds_t = (p_t * (_mm_nt(vv, do2s) - dl_r)).astype(BF16)
                dkk = _mm(ds_t, q2s) * SCALE
                dvv = _mm(p_t.astype(BF16), do2s)
                dqs = _mm_tn(ds_t, kk) * SCALE
                dq_ref[rs, cs] = jnp.where(in_a[0], dqs[:BLK], dqs[BLK:]).astype(BF16)
                dk_acc[b] += dkk[BLK:]
                dv_acc[b] += dvv[BLK:]
                if b > 0:
                    dk_acc[b - 1] += dkk[:BLK]
                    dv_acc[b - 1] += dvv[:BLK]
            if halo:
                q2s = _stack_heads(qn_ref[:, cs], in_a)
                do2s = _stack_heads(don_ref[:, cs], in_a)
                lse_r, dl_r = stat_rows(stn_t, hp)
                s_t = jnp.where(mask_next, _mm_nt(kb[n - 1], q2s) * SCALE, NEG)
                p_t = jnp.exp(s_t - lse_r)
                ds_t = (p_t * (_mm_nt(vb[n - 1], do2s) - dl_r)).astype(BF16)
                dk_acc[n - 1] += _mm(ds_t, q2s) * SCALE
                dv_acc[n - 1] += _mm(p_t.astype(BF16), do2s)
            for b in range(n):
                dk_ref[b * BLK:(b + 1) * BLK, cs] = dk_acc[b].astype(BF16)
                dv_ref[b * BLK:(b + 1) * BLK, cs] = dv_acc[b].astype(BF16)

    cur = pl.BlockSpec((CH, AW), lambda i: (i, 0))
    cur_st = pl.BlockSpec((CH, 128), lambda i: (i, 0))
    prev = pl.BlockSpec((BLK, AW), lambda i: (jnp.maximum(n * i - 1, 0), 0))
    nxt = pl.BlockSpec((BLK, AW), lambda i: (jnp.minimum(n * i + n, NBLK - 1), 0))
    nxt_st = pl.BlockSpec((BLK, 128), lambda i: (jnp.minimum(n * i + n, NBLK - 1), 0))
    ins = [cur] * 4 + [cur_st] + ([prev, prev, nxt, nxt, nxt_st] if halo else [])
    args = (q, k, v, do, st) + ((k, v, q, do, st) if halo else ())
    return pl.pallas_call(
        body, name=name, grid=(T // CH,),
        in_specs=ins,
        out_specs=[cur] * 3,
        out_shape=[jax.ShapeDtypeStruct((T, AW), BF16)] * 3,
        compiler_params=_cp(("parallel",)),
    )(*args)


TH = 256
NCH = TH // CHUNK


def _hgrn_common(hq_ref, hf_ref, lbr_ref, tri_ref):
    r0 = lbr_ref[0:1, :]
    r1 = lbr_ref[1:2, :]
    mx = jnp.maximum(r0, r1)
    e0 = jnp.exp(r0 - mx)
    e1 = jnp.exp(r1 - mx)
    lb = e0 / (e0 + e1)
    hqv = hq_ref[...].astype(F32)
    sq = _sigmoid(hqv)
    qv = hqv * sq
    sf = _sigmoid(hf_ref[...].astype(F32))
    f = lb + (1.0 - lb) * sf
    kv = 1.0 - f
    g = jnp.log(f)
    cum = _mm_exact_l(tri_ref[...], g)
    lastb = jnp.concatenate(
        [jnp.broadcast_to(cum[c * CHUNK + CHUNK - 1:(c + 1) * CHUNK, :], (CHUNK, HW)) for c in range(NCH)], axis=0)
    ea = jnp.exp(cum)
    ena = jnp.exp(-cum)
    eend = jnp.exp(lastb - cum)
    return dict(lb=lb, hq=hqv, sq=sq, q=qv, sf=sf, f=f, k=kv, cum=cum, lastb=lastb, ea=ea, ena=ena, eend=eend,
                qd=qv * ea, ki=kv * ena, ke=kv * eend, dec=jnp.exp(lastb))


def _tri_mask(transposed=False):
    ti = lax.broadcasted_iota(jnp.int32, (TH, TH), 1 if transposed else 0)
    si = lax.broadcasted_iota(jnp.int32, (TH, TH), 0 if transposed else 1)
    return (si <= ti) & ((si // CHUNK) == (ti // CHUNK))


def _hgrn_fwd(hq, hf, hi, lbr, tri):
    def body(hq_ref, hf_ref, hi_ref, lbr_ref, tri_ref, rec_ref, sall_ref, st_scr):
        @pl.when(pl.program_id(0) == 0)
        def _():
            st_scr[...] = jnp.zeros_like(st_scr)

        w = _hgrn_common(hq_ref, hf_ref, lbr_ref, tri_ref)
        qd, ki, ke = w["qd"].astype(BF16), w["ki"].astype(BF16), w["ke"].astype(BF16)
        dec = w["dec"]
        vb = hi_ref[...]
        causal = _tri_mask()
        for h in range(4):
            cs = slice(h * 128, (h + 1) * 128)
            att = jnp.where(causal, _mm_nt(qd[:, cs], ki[:, cs]), 0.0)
            o_intra = _mm(att.astype(BF16), vb[:, cs])
            for c in range(NCH):
                rs = slice(c * CHUNK, (c + 1) * CHUNK)
                st = st_scr[:, cs]
                sall_ref[c, :, cs] = st
                rec_ref[rs, cs] = (o_intra[rs] + _mm_nt(qd[rs, cs], st.astype(BF16))).astype(BF16)
                st_scr[:, cs] = dec[c * CHUNK:c * CHUNK + 1, cs] * st + _mm_tn(vb[rs, cs], ke[rs, cs])

    tok = pl.BlockSpec((TH, HW), lambda i: (i, 0))
    return pl.pallas_call(
        body, name="hgrn_fwd", grid=(T // TH,),
        in_specs=[tok, tok, tok, pl.BlockSpec((2, HW), lambda i: (0, 0)), pl.BlockSpec((TH, TH), lambda i: (0, 0))],
        out_specs=[tok, pl.BlockSpec((NCH, 128, HW), lambda i: (i, 0, 0))],
        out_shape=[jax.ShapeDtypeStruct((T, HW), BF16), jax.ShapeDtypeStruct((T // CHUNK, 128, HW), F32)],
        scratch_shapes=[pltpu.VMEM((128, HW), F32)],
        compiler_params=_cp(("arbitrary",)),
    )(hq, hf, hi, lbr, tri)


def _hgrn_bwd(hq, hf, hi, lbr, tri, trit, drec, sall, dhg, rout, routb):
    NT = T // TH

    def body(hq_ref, hf_ref, hi_ref, lbr_ref, tri_ref, trit_ref, do_ref, sall_ref, dhg_ref, rout_r, routb_r,
             dph_ref, small_ref, pout_o, poutr_o,
             dst_scr, dlb_scr, dqd_scr, dki_scr, dke_scr, dlast_scr, send_sems, recv_sems, loc_sems):
        step = pl.program_id(0)
        loc, rem = _chip_copies(_w_out_piece, rout_r, routb_r, pout_o, poutr_o, send_sems, recv_sems,
                                loc_sems.at[0])

        @pl.when(step == 0)
        def _():
            dst_scr[...] = jnp.zeros_like(dst_scr)
            dlb_scr[...] = jnp.zeros_like(dlb_scr)
            for cp in loc + rem:
                cp.start()

        w = _hgrn_common(hq_ref, hf_ref, lbr_ref, tri_ref)
        qd, ki, ke = w["qd"].astype(BF16), w["ki"].astype(BF16), w["ke"].astype(BF16)
        dec = w["dec"]
        vb = hi_ref[...]
        dob = do_ref[...].astype(BF16)
        causal = _tri_mask()
        causal_t = _tri_mask(transposed=True)
        for h in range(4):
            cs = slice(h * 128, (h + 1) * 128)
            att_t = jnp.where(causal_t, _mm_nt(ki[:, cs], qd[:, cs]), 0.0).astype(BF16)
            datt_t = jnp.where(causal_t, _mm_nt(vb[:, cs], dob[:, cs]), 0.0).astype(BF16)
            datt = jnp.where(causal, _mm_nt(dob[:, cs], vb[:, cs]), 0.0).astype(BF16)
            dv_intra = _mm(att_t, dob[:, cs])
            dqd_intra = _mm(datt, ki[:, cs])
            dki_scr[:, cs] = _mm(datt_t, qd[:, cs])
            for c in reversed(range(NCH)):
                rs = slice(c * CHUNK, (c + 1) * CHUNK)
                dec_c = dec[c * CHUNK:c * CHUNK + 1, :]
                st = sall_ref[c, :, cs]
                dst = dst_scr[:, cs]
                dstb = dst.astype(BF16)
                dph_ref[rs, 2 * HW + h * 128:2 * HW + (h + 1) * 128] = (
                    dv_intra[rs] + _mm_nt(ke[rs, cs], dstb)).astype(BF16)
                dqd_scr[rs, cs] = dqd_intra[rs] + _mm(dob[rs, cs], st.astype(BF16))
                dke_scr[rs, cs] = _mm(vb[rs, cs], dstb)
                ddec = jnp.sum(dst * st, axis=0, keepdims=True)
                dlast_scr[c:c + 1, cs] = ddec * dec_c[:, cs]
                dst_scr[:, cs] = dec_c[:, cs] * dst + _mm_tn(dob[rs, cs], qd[rs, cs])
        dqd, dki, dke = dqd_scr[...], dki_scr[...], dke_scr[...]
        dq = dqd * w["ea"]
        dk = dki * w["ena"] + dke * w["eend"]
        dcum = dqd * w["qd"] - dki * w["ki"] - dke * w["ke"]
        dkeke = dke * w["ke"]
        dlastb = jnp.concatenate(
            [jnp.broadcast_to(dlast_scr[c:c + 1, :] + jnp.sum(dkeke[c * CHUNK:(c + 1) * CHUNK], axis=0, keepdims=True),
                              (CHUNK, HW)) for c in range(NCH)], axis=0)
        dg = _mm_exact_l(trit_ref[...], dcum) + dlastb
        df = dg / w["f"] - dk
        lb, sf, sq = w["lb"], w["sf"], w["sq"]
        dph_ref[:, HW:2 * HW] = (df * (1.0 - lb) * sf * (1.0 - sf)).astype(BF16)
        dph_ref[:, 0:HW] = (dq * (sq * (1.0 + w["hq"] * (1.0 - sq)))).astype(BF16)
        dph_ref[:, 3 * HW:4 * HW] = dhg_ref[...]
        dlb_scr[...] += jnp.sum(df * (1.0 - sf), axis=0, keepdims=True)

        @pl.when(step == NT - 1)
        def _():
            gr = dlb_scr[...] * lb * (1.0 - lb)
            small_ref[...] = jnp.zeros_like(small_ref)
            small_ref[0:1, 0:HW] = gr
            small_ref[1:2, 0:HW] = -gr
            for cp in rem:
                cp.wait_recv()
            for cp in rem:
                cp.wait_send()
            for cp in loc:
                cp.wait()

    tok = pl.BlockSpec((TH, HW), lambda i: (NT - 1 - i, 0))
    const = lambda shape: pl.BlockSpec(shape, lambda i: (0,) * len(shape))
    hbm = pl.BlockSpec(memory_space=pltpu.HBM)
    return pl.pallas_call(
        body, name="hgrn_bwd", grid=(NT,),
        in_specs=[tok, tok, tok, const((2, HW)), const((TH, TH)), const((TH, TH)), tok,
                  pl.BlockSpec((NCH, 128, HW), lambda i: (NT - 1 - i, 0, 0)), tok, hbm, hbm],
        out_specs=[pl.BlockSpec((TH, NCOL // 2), lambda i: (NT - 1 - i, 0)), const((8, D)), hbm, hbm],
        out_shape=[jax.ShapeDtypeStruct((T, NCOL // 2), BF16), jax.ShapeDtypeStruct((8, D), F32),
                   jax.ShapeDtypeStruct((128, D), F32), jax.ShapeDtypeStruct((3, 128, D), BF16)],
        scratch_shapes=[pltpu.VMEM((128, HW), F32), pltpu.VMEM((1, HW), F32), pltpu.VMEM((TH, HW), F32),
                        pltpu.VMEM((TH, HW), F32), pltpu.VMEM((TH, HW), F32), pltpu.VMEM((8, HW), F32),
                        pltpu.SemaphoreType.DMA((3,)), pltpu.SemaphoreType.DMA((3,)), pltpu.SemaphoreType.DMA((1,))],
        compiler_params=_cp(("arbitrary",)),
    )(hq, hf, hi, lbr, tri, trit, drec, sall, dhg, rout, routb)


def _fwd_out(o1, o4, o16, l1, l4, l16, rec, ag, hg, x, tgt, anw, hnw, fnw, wout_full, gmat, emat, selmat):
    TT = 256

    def body(o1_r, o4_r, o16_r, l1_r, l4_r, l16_r, rec_r, ag_r, hg_r, x_r, tgt_r, anw_r, hnw_r, fnw_r, wo_r, g_r,
             e_r, sel_r, dx2_o, do1_o, do4_o, do16_o, st1_o, st4_o, st16_o, drec_o, dag_o, dhg_o,
             rout_o, routb_o, small_o, scr_a, scr_b, scr_c, gwout_o, rbuf, send_sems, recv_sems):
        @pl.when(pl.program_id(0) == 0)
        def _():
            gwout_o[...] = jnp.zeros_like(gwout_o)
            small_o[...] = jnp.zeros_like(small_o)

        def unperm(r4, r16):
            return _unperm_load(r4, r16, scr_a, scr_b, scr_c)

        def perm_out(val, p1, p4, p16, dt):
            _perm_store(val, scr_a, scr_b, p1, p4, p16, dt)

        o4u, o16u = unperm(o4_r, o16_r)
        l4c, l16c = unperm(l4_r, l16_r)
        l1c = l1_r[...]
        mxc = jnp.maximum(jnp.maximum(l1c, l4c), l16c)
        w1c, w4c, w16c = jnp.exp(l1c - mxc), jnp.exp(l4c - mxc), jnp.exp(l16c - mxc)
        denc = w1c + w4c + w16c
        lane = lax.broadcasted_iota(jnp.int32, (1, 128), 1)
        lse_c = jnp.where(lane < 8, mxc + jnp.log(denc), 0.0)
        em = e_r[...]
        wn1 = _mm_exact_r(w1c / denc, em)
        wn4 = _mm_exact_r(w4c / denc, em)
        o1v = o1_r[...].astype(F32)
        attn = wn1 * o1v + wn4 * o4u + (1.0 - wn1 - wn4) * o16u
        gm = g_r[...]

        def head_mean_a(t):
            return jnp.concatenate([_mm_exact_r(t[:, :256], gm), _mm_exact_r(t[:, 256:], gm)], axis=1)

        def head_mean_h(t):
            return jnp.concatenate(
                [jnp.broadcast_to(jnp.mean(t[:, h * 128:(h + 1) * 128], axis=-1, keepdims=True), (TT, 128))
                 for h in range(4)], axis=1)

        rs_a = lax.rsqrt(head_mean_a(attn * attn) + EPS)
        n_a = attn * rs_a
        agv = ag_r[...].astype(F32)
        sg_a = _sigmoid(agv)
        si_a = agv * sg_a
        anw_v = anw_r[...]
        y_a = (n_a * anw_v) * si_a
        recv = rec_r[...].astype(F32)
        rs_h = lax.rsqrt(head_mean_h(recv * recv) + EPS)
        n_h = recv * rs_h
        hgv = hg_r[...].astype(F32)
        sg_h = _sigmoid(hgv)
        si_h = hgv * sg_h
        hnw_v = hnw_r[...]
        y_h = (n_h * hnw_v) * si_h
        mixed = jnp.concatenate([y_a, y_h], axis=1).astype(BF16)
        xv = x_r[...]
        x2 = xv + _mm(mixed, wo_r[...])
        r2 = lax.rsqrt(jnp.mean(x2 * x2, axis=-1, keepdims=True) + EPS)
        fnw_v = fnw_r[...]
        xn = x2 * r2
        err = xn * fnw_v - tgt_r[...]
        small_o[2:3, :] += 0.5 * jnp.sum(jnp.mean(err * err, axis=-1, keepdims=True), axis=0, keepdims=True)
        dy = err * (1.0 / D)
        small_o[0:1, :] += jnp.sum(dy * xn, axis=0, keepdims=True)
        dyw = dy * fnw_v
        dx2 = r2 * dyw - x2 * ((r2 * r2 * r2) * jnp.mean(dyw * x2, axis=-1, keepdims=True))
        dx2_o[...] = dx2
        dx2b = dx2.astype(BF16)
        gwout_o[...] += _mm_tn(mixed, dx2b)
        dmix = _mm_nt(dx2b, wo_r[...])
        dm_a, dm_h = dmix[:, :AW], dmix[:, AW:]
        dag_o[...] = (dm_a * (n_a * anw_v) * (sg_a * (1.0 + agv * (1.0 - sg_a)))).astype(BF16)
        dn_a = dm_a * anw_v * si_a
        small_o[1:2, 0:AW] += jnp.sum(dm_a * n_a * si_a, axis=0, keepdims=True)
        dattn = rs_a * (dn_a - n_a * head_mean_a(dn_a * n_a))
        perm_out(dattn, do1_o, do4_o, do16_o, BF16)
        stats = lse_c + _mm_exact_r(dattn * attn, sel_r[...])
        perm_out(stats, st1_o, st4_o, st16_o, F32)
        dhg_o[...] = (dm_h * (n_h * hnw_v) * (sg_h * (1.0 + hgv * (1.0 - sg_h)))).astype(BF16)
        dn_h = dm_h * hnw_v * si_h
        small_o[1:2, AW:] += jnp.sum(dm_h * n_h * si_h, axis=0, keepdims=True)
        drec_o[...] = (rs_h * (dn_h - n_h * head_mean_h(dn_h * n_h))).astype(BF16)

        @pl.when(pl.program_id(0) == T // TT - 1)
        def _():
            x, y, c = lax.axis_index("x"), lax.axis_index("y"), lax.axis_index("c")
            cps = [pltpu.make_async_remote_copy(
                src_ref=gwout_o.at[pl.ds(pl.multiple_of(j * 256 + (1 - c) * 128, 128), 128), :], dst_ref=rbuf.at[j],
                send_sem=send_sems.at[j], recv_sem=recv_sems.at[j], device_id=(x, y, 1 - c), device_id_type=MESH)
                for j in range(4)]
            for cp in cps:
                cp.start()
            for j, cp in enumerate(cps):
                cp.wait_recv()
                red = gwout_o[pl.ds(pl.multiple_of(j * 256 + c * 128, 128), 128), :] + rbuf[j]
                rout_o[j * 128:(j + 1) * 128, :] = red
                routb_o[j * 128:(j + 1) * 128, :] = red.astype(BF16)
            for cp in cps:
                cp.wait_send()

    tok = lambda w: pl.BlockSpec((TT, w), lambda i: (i, 0))
    d4 = pl.BlockSpec((4, TT // 4, AW), lambda i: (0, i, 0))
    d16 = pl.BlockSpec((16, TT // 16, AW), lambda i: (0, i, 0))
    const = lambda shape: pl.BlockSpec(shape, lambda i: (0,) * len(shape))
    sd = lambda shape, dt: jax.ShapeDtypeStruct(shape, dt)
    c4 = pl.BlockSpec((4, TT // 4, 128), lambda i: (0, i, 0))
    c16 = pl.BlockSpec((16, TT // 16, 128), lambda i: (0, i, 0))
    p3 = lambda w, dt: [sd((T, w), dt), sd((4, T // 4, w), dt), sd((16, T // 16, w), dt)]
    return pl.pallas_call(
        body, name="fwd_out", grid=(T // TT,),
        in_specs=[tok(AW), d4, d16, tok(128), c4, c16, tok(AW), tok(AW), tok(AW), tok(D), tok(D),
                  const((1, AW)), const((1, HW)), const((1, D)), const((D, D)), const((256, 256)),
                  const((128, AW)), const((AW, 128))],
        out_specs=[tok(D)] + [tok(AW), d4, d16] + [tok(128), c4, c16] + [tok(AW)] * 3
        + [const((512, D)), const((512, D)), const((8, D))],
        out_shape=[sd((T, D), F32)] + p3(AW, BF16) + p3(128, F32)
        + [sd((T, AW), BF16), sd((T, AW), BF16), sd((T, AW), BF16), sd((512, D), F32), sd((512, D), BF16),
           sd((8, D), F32)],
        scratch_shapes=[pltpu.VMEM((4, TT, 128), F32)] * 3 + [pltpu.VMEM((D, D), F32),
                        pltpu.VMEM((4, 128, D), F32), pltpu.SemaphoreType.DMA((4,)), pltpu.SemaphoreType.DMA((4,))],
        compiler_params=_cp(("arbitrary",)),
    )(o1, o4, o16, l1, l4, l16, rec, ag, hg, x, tgt, anw, hnw, fnw, wout_full, gmat, emat, selmat)


def _dproj_build(dq, dk, dv, dag, pos):
    TT = 256

    def body(dq1, dq4, dq16, dk1, dk4, dk16, dv1, dv4, dv16, dag_r, pos_r, dproj_o, scr_a, scr_b, scr_c):
        def unperm_sum(r1, r4, r16):
            u4, u16 = _unperm_load(r4, r16, scr_a, scr_b, scr_c)
            return r1[...] + u4 + u16

        cosf, s1, s2 = _rope_tables(pos_r[...])
        dproj_o[:, 0:512] = _rope_bwd(unperm_sum(dq1, dq4, dq16), cosf, s1, s2).astype(BF16)
        dproj_o[:, 512:1024] = _rope_bwd(unperm_sum(dk1, dk4, dk16), cosf, s1, s2).astype(BF16)
        dproj_o[:, 1024:1536] = unperm_sum(dv1, dv4, dv16).astype(BF16)
        dproj_o[:, 1536:2048] = dag_r[...]

    tok = lambda w: pl.BlockSpec((TT, w), lambda i: (i, 0))
    d4 = pl.BlockSpec((4, TT // 4, AW), lambda i: (0, i, 0))
    d16 = pl.BlockSpec((16, TT // 16, AW), lambda i: (0, i, 0))
    return pl.pallas_call(
        body, name="dproj_build", grid=(T // TT,),
        in_specs=[tok(AW), d4, d16] * 3 + [tok(AW), tok(1)],
        out_specs=tok(NCOL // 2),
        out_shape=jax.ShapeDtypeStruct((T, NCOL // 2), BF16),
        scratch_shapes=[pltpu.VMEM((4, TT, 128), F32)] * 3,
        compiler_params=_cp(("parallel",)),
    )(*dq, *dk, *dv, dag, pos)


def _bwd_x(dproj_a, dproj_h, x, dx2, mixw, w_full, rin, rinb, small4, small6, pout_own, pout_rem):
    TT = 256
    NT = T // TT

    def body(dpa_r, dph_r, x_r, dx2_r, mw_r, w_r, rin_r, rinb_r, s4_r, s6_r, poo_r, por_r,
             gx_o, pin_o, pinr_o, sall_o, fin_o, fout_o, sbuf, v_own, v_rem, vo_own, vo_rem, sin, sout, got_in,
             got_out, send_sems, recv_sems, loc_sems, share_send, share_recv, fin_sems):
        i = pl.program_id(0)
        loc, rem = _chip_copies(_w_in_piece, rin_r, rinb_r, pin_o, pinr_o, send_sems, recv_sems, loc_sems.at[0])

        @pl.when(i == 0)
        def _():
            sbuf[...] = jnp.zeros_like(sbuf)
            for cp in loc + rem:
                cp.start()

        dhn = _mm_nt(dpa_r[...], w_r[:, 0:NCOL // 2]) + _mm_nt(dph_r[...], w_r[:, NCOL // 2:NCOL])
        xv = x_r[...]
        r = lax.rsqrt(jnp.mean(xv * xv, axis=-1, keepdims=True) + EPS)
        dxw = dhn * mw_r[...]
        gx_o[...] = dx2_r[...] + r * dxw - xv * ((r * r * r) * jnp.mean(dxw * xv, axis=-1, keepdims=True))
        sbuf[16:17, :] += jnp.sum(dhn * (xv * r), axis=0, keepdims=True)

        @pl.when(i == NT - 1)
        def _():
            sbuf[0:8, :] = s4_r[...]
            sbuf[8:16, :] = s6_r[...]
            sloc, srem = _small_copies(sbuf, sall_o, send_sems, recv_sems, loc_sems.at[1])
            for cp in sloc + srem:
                cp.start()
            for cp in rem + srem:
                cp.wait_recv()
            for cp in rem + srem:
                cp.wait_send()
            for cp in loc + sloc:
                cp.wait()
            mx, my, c = lax.axis_index("x"), lax.axis_index("y"), lax.axis_index("c")
            loads = [pltpu.make_async_copy(pin_o, v_own, fin_sems.at[0]),
                     pltpu.make_async_copy(pinr_o, v_rem, fin_sems.at[1]),
                     pltpu.make_async_copy(poo_r, vo_own, fin_sems.at[2]),
                     pltpu.make_async_copy(por_r, vo_rem, fin_sems.at[3])]
            for cp in loads:
                cp.start()
            for cp in loads:
                cp.wait()
            sout[...] = ((vo_own[...] + vo_rem[0].astype(F32)) + vo_rem[1].astype(F32)) + vo_rem[2].astype(F32)
            sin[...] = ((v_own[...] + v_rem[0].astype(F32)) + v_rem[1].astype(F32)) + v_rem[2].astype(F32)
            swap = [pltpu.make_async_remote_copy(src_ref=sin, dst_ref=got_in, send_sem=share_send.at[0],
                                                 recv_sem=share_recv.at[0], device_id=(mx, my, 1 - c),
                                                 device_id_type=MESH),
                    pltpu.make_async_remote_copy(src_ref=sout, dst_ref=got_out, send_sem=share_send.at[1],
                                                 recv_sem=share_recv.at[1], device_id=(mx, my, 1 - c),
                                                 device_id_type=MESH)]
            for cp in swap:
                cp.start()
            mine = [pltpu.make_async_copy(sin, fin_o.at[c], fin_sems.at[0]),
                    pltpu.make_async_copy(sout, fout_o.at[c], fin_sems.at[1])]
            for cp in mine:
                cp.start()
            for cp in swap:
                cp.wait_recv()
            theirs = [pltpu.make_async_copy(got_in, fin_o.at[1 - c], fin_sems.at[2]),
                      pltpu.make_async_copy(got_out, fout_o.at[1 - c], fin_sems.at[3])]
            for cp in theirs:
                cp.start()
            for cp in swap:
                cp.wait_send()
            for cp in mine + theirs:
                cp.wait()

    tok = lambda w: pl.BlockSpec((TT, w), lambda i: (i, 0))
    const = lambda shape: pl.BlockSpec(shape, lambda i: (0,) * len(shape))
    hbm = pl.BlockSpec(memory_space=pltpu.HBM)
    return pl.pallas_call(
        body, name="bwd_x", grid=(NT,),
        in_specs=[tok(NCOL // 2), tok(NCOL // 2), tok(D), tok(D), const((1, D)), const((D, NCOL)), hbm, hbm,
                  const((8, D)), const((8, D)), hbm, hbm],
        out_specs=[tok(D), hbm, hbm, hbm, hbm, hbm],
        out_shape=[jax.ShapeDtypeStruct((T, D), F32),
                   jax.ShapeDtypeStruct((512, 1024), F32), jax.ShapeDtypeStruct((3, 512, 1024), BF16),
                   jax.ShapeDtypeStruct((8, 24, D), F32),
                   jax.ShapeDtypeStruct((2, 512, 1024), F32), jax.ShapeDtypeStruct((2, 128, D), F32)],
        scratch_shapes=[pltpu.VMEM((24, D), F32),
                        pltpu.VMEM((512, 1024), F32), pltpu.VMEM((3, 512, 1024), BF16),
                        pltpu.VMEM((128, D), F32), pltpu.VMEM((3, 128, D), BF16),
                        pltpu.VMEM((512, 1024), F32), pltpu.VMEM((128, D), F32),
                        pltpu.VMEM((512, 1024), F32), pltpu.VMEM((128, D), F32),
                        pltpu.SemaphoreType.DMA((10,)), pltpu.SemaphoreType.DMA((10,)), pltpu.SemaphoreType.DMA((2,)),
                        pltpu.SemaphoreType.DMA((2,)), pltpu.SemaphoreType.DMA((2,)), pltpu.SemaphoreType.DMA((4,))],
        compiler_params=_cp(("arbitrary",)),
    )(dproj_a, dproj_h, x, dx2, mixw, w_full, rin, rinb, small4, small6, pout_own, pout_rem)


def _grad_w_in(hn, dproj_a, dproj_h):
    TK = 1024
    NK = T // TK

    def body(hnt_r, dpa_r, dph_r, rin_o, rinb_o, acc, rbuf, obuf, obufb, send_sems, recv_sems, wb_sems):
        j = pl.program_id(0)
        kk = pl.program_id(1)
        x, y, c = lax.axis_index("x"), lax.axis_index("y"), lax.axis_index("c")
        mine = pl.ds(pl.multiple_of(c * 512, 512), 512)
        theirs = pl.ds(pl.multiple_of((1 - c) * 512, 512), 512)

        def send(jj):
            return pltpu.make_async_remote_copy(
                src_ref=acc.at[jj % 2, theirs, :], dst_ref=rbuf.at[jj], send_sem=send_sems.at[jj],
                recv_sem=recv_sems.at[jj], device_id=(x, y, 1 - c), device_id_type=MESH)

        def writeback(jj):
            cols = pl.ds(jj * 1024, 1024)
            return [pltpu.make_async_copy(obuf.at[jj % 2], rin_o.at[:, cols], wb_sems.at[jj % 2]),
                    pltpu.make_async_copy(obufb.at[jj % 2], rinb_o.at[:, cols], wb_sems.at[2 + jj % 2])]

        def wait_writeback(jj):
            for cp in writeback(jj):
                cp.wait()

        def finalize(jj):
            send(jj).wait_recv()
            red = acc[jj % 2, mine, :] + rbuf[jj]
            obuf[jj % 2] = red
            obufb[jj % 2] = red.astype(BF16)
            for cp in writeback(jj):
                cp.start()

        prod = _mm(hnt_r[...], jnp.where(j < 2, dpa_r[...], dph_r[...]))

        @pl.when(kk == 0)
        def _():
            for jj in (2, 3):
                @pl.when(j == jj)
                def _():
                    send(jj - 2).wait_send()
            acc[j % 2] = prod

        @pl.when(kk > 0)
        def _():
            acc[j % 2] += prod

        @pl.when(kk == NK - 1)
        def _():
            for jj in range(4):
                @pl.when(j == jj)
                def _():
                    send(jj).start()
                    if jj in (1, 2):
                        finalize(jj - 1)
                    if jj == 3:
                        wait_writeback(0)
                        finalize(2)
                        wait_writeback(1)
                        finalize(3)
                        wait_writeback(2)
                        wait_writeback(3)
                        send(2).wait_send()
                        send(3).wait_send()

    hbm = pl.BlockSpec(memory_space=pltpu.HBM)
    return pl.pallas_call(
        body, name="grad_w_in", grid=(4, NK),
        in_specs=[pl.BlockSpec((D, TK), lambda j, kk: (0, kk)),
                  pl.BlockSpec((TK, 1024), lambda j, kk: (jnp.where(j < 2, kk, NK - 1), jnp.minimum(j, 1))),
                  pl.BlockSpec((TK, 1024), lambda j, kk: (jnp.where(j < 2, 0, kk), jnp.maximum(j - 2, 0)))],
        out_specs=[hbm, hbm],
        out_shape=[jax.ShapeDtypeStruct((512, NCOL), F32), jax.ShapeDtypeStruct((512, NCOL), BF16)],
        scratch_shapes=[pltpu.VMEM((2, D, 1024), F32), pltpu.VMEM((4, 512, 1024), F32), pltpu.VMEM((2, 512, 1024), F32),
                        pltpu.VMEM((2, 512, 1024), BF16),
                        pltpu.SemaphoreType.DMA((4,)), pltpu.SemaphoreType.DMA((4,)), pltpu.SemaphoreType.DMA((4,))],
        compiler_params=_cp(("arbitrary", "arbitrary")),
    )(hn, dproj_a, dproj_h)


def _w_in_piece(ref, j):
    return ref.at[:, pl.ds(j * 1024, 1024)]


def _w_out_piece(ref, j):
    return ref.at[pl.ds(j * 128, 128), :]


def _chip_copies(piece, src_r, srcb_r, own_o, rem_o, send_sems, recv_sems, loc_sem):
    x, y, c = lax.axis_index("x"), lax.axis_index("y"), lax.axis_index("c")
    chips = [(1 - x, y), (x, 1 - y), (1 - x, 1 - y)]
    loc = [pltpu.make_async_copy(piece(src_r, 2 * x + y), own_o, loc_sem)]
    rem = [pltpu.make_async_remote_copy(
        src_ref=piece(srcb_r, 2 * px + py), dst_ref=rem_o.at[k], send_sem=send_sems.at[k],
        recv_sem=recv_sems.at[k], device_id=(px, py, c), device_id_type=MESH) for k, (px, py) in enumerate(chips)]
    return loc, rem


def _small_copies(small_r, sall_o, send_sems, recv_sems, loc_sem):
    x, y, c = lax.axis_index("x"), lax.axis_index("y"), lax.axis_index("c")
    me = 4 * x + 2 * y + c
    loc = [pltpu.make_async_copy(small_r, sall_o.at[me], loc_sem)]
    rem = []
    k = 3
    for fx in range(2):
        for fy in range(2):
            for fc in range(2):
                if fx or fy or fc:
                    peer = (1 - x if fx else x, 1 - y if fy else y, 1 - c if fc else c)
                    rem.append(pltpu.make_async_remote_copy(
                        src_ref=small_r, dst_ref=sall_o.at[me], send_sem=send_sems.at[k],
                        recv_sem=recv_sems.at[k], device_id=peer, device_id_type=MESH))
                    k += 1
    return loc, rem


def _adamw_math(w, g, m, v):
    m = B1 * m + (1.0 - B1) * g
    v = B2 * v + (1.0 - B2) * (g * g)
    m_hat = m / (1.0 - B1 ** STEP)
    v_hat = v / (1.0 - B2 ** STEP)
    delta = -LR * (m_hat / (jnp.sqrt(v_hat) + AEPS) + WD * w)
    return delta, m, v


def _adamw(big_in, big_out, sall, params):
    def body(*refs):
        wi, gi, mi, vi, wo, go, mo, vo, sall_r = refs[:9]
        ins = refs[9:24]
        di_o, mi_o, vi_o, do_o, mo_o, vo_o = refs[24:30]
        outs = refs[30:]
        d, mm, vv = _adamw_math(wi[...], gi[...], mi[...], vi[...])
        di_o[...] = d
        mi_o[...] = mm
        vi_o[...] = vv

        @pl.when(pl.program_id(0) == 0)
        def _():
            d, mm, vv = _adamw_math(wo[...], go[...], mo[...], vo[...])
            do_o[...] = d
            mo_o[...] = mm
            vo_o[...] = vv
            tot = sall_r[0]
            for dv in range(1, 8):
                tot = tot + sall_r[dv]
            grads = [tot[16:17, :], tot[1:2, 0:AW], tot[1:2, AW:], tot[8:10, 0:HW], tot[0:1, :]]
            outs[0][...] = tot[2:3, 0:1]
            for p in range(5):
                w_r, m_r, v_r = ins[3 * p:3 * p + 3]
                g = grads[p]
                d, mm, vv = _adamw_math(w_r[...], g, m_r[...], v_r[...])
                outs[1 + 4 * p][...] = g
                outs[2 + 4 * p][...] = d
                outs[3 + 4 * p][...] = mm
                outs[4 + 4 * p][...] = vv

    flat = [a for p in params for a in p]
    shapes = [jax.ShapeDtypeStruct((D, 1024), F32)] * 3 + [jax.ShapeDtypeStruct((256, D), F32)] * 3
    shapes += [jax.ShapeDtypeStruct((1, 1), F32)]
    for p in params:
        shapes += [jax.ShapeDtypeStruct(p[0].shape, F32)] * 4
    vm = pl.BlockSpec(memory_space=pltpu.VMEM)
    rows = pl.BlockSpec((256, 1024), lambda i: (i, 0))
    whole = pl.BlockSpec((256, D), lambda i: (0, 0))
    return pl.pallas_call(
        body, name="adamw", grid=(4,),
        in_specs=[rows] * 4 + [whole] * 4 + [vm] * 16, out_specs=[rows] * 3 + [whole] * 3 + [vm] * 21,
        out_shape=shapes,
        compiler_params=_cp(("arbitrary",)),
    )(*big_in, *big_out, sall, *flat)


def kernel(x, positions, w_in, w_out, mix_norm_w, attn_out_norm_w, hgrn_out_norm_w, hgrn_lb_raw, final_norm_w, loss_target, m_w_in, m_w_out, m_mix_norm_w, m_attn_out_norm_w, m_hgrn_out_norm_w, m_hgrn_lb_raw, m_final_norm_w, v_w_in, v_w_out, v_mix_norm_w, v_attn_out_norm_w, v_hgrn_out_norm_w, v_hgrn_lb_raw, v_final_norm_w):
    xs = x.reshape(T, D)
    tgt = loss_target.reshape(T, D)
    pos = positions.reshape(T, 1)
    fnw = final_norm_w.reshape(1, D)

    ti = np.arange(TH)
    tri_np = ((ti[:, None] // CHUNK == ti[None, :] // CHUNK) & (ti[None, :] <= ti[:, None])).astype(np.float32)
    tri = jnp.asarray(tri_np, BF16)
    trit = jnp.asarray(tri_np.T, BF16)
    hi_ = np.arange(AW) // HEAD
    gmat = jnp.asarray((hi_[:256, None] == hi_[None, :256]).astype(np.float32) / HEAD, BF16)
    emat_np = (np.arange(128)[:, None] == hi_[None, :]).astype(np.float32)
    sel_np = (8 + hi_[:, None] == np.arange(128)[None, :]).astype(np.float32)
    emat = jnp.asarray(emat_np, BF16)
    selmat = jnp.asarray(sel_np, BF16)

    jm_arr = (2 * lax.axis_index("x") + lax.axis_index("y")).astype(jnp.int32).reshape(1)
    (hn, q1, k1, v1, q4, k4, v4, q16, k16, v16, ag, hq, hf, hi, hg, w_full, wout4) = _fwd_in(
        xs, pos, mix_norm_w, w_in.reshape(D, 1024), w_out.reshape(256, D), jm_arr)
    wout_full = wout4.reshape(D, D)
    flat = lambda a: a.reshape(T, AW)
    o1, l1 = _attn_fwd(q1, k1, v1, T // BLK, "attn_fwd_d1")
    o4, l4 = _attn_fwd(flat(q4), flat(k4), flat(v4), T // 4 // BLK, "attn_fwd_d4")
    o16, l16 = _attn_fwd(flat(q16), flat(k16), flat(v16), T // 16 // BLK, "attn_fwd_d16")
    rec, sall = _hgrn_fwd(hq, hf, hi, hgrn_lb_raw, tri)

    (dx2, do1, do4, do16, st1, st4, st16, drec, dag, dhg, rout, routb, small4) = _fwd_out(
        o1, o4.reshape(4, T // 4, AW), o16.reshape(16, T // 16, AW),
        l1, l4.reshape(4, T // 4, 128), l16.reshape(16, T // 16, 128),
        rec, ag, hg, xs, tgt, attn_out_norm_w, hgrn_out_norm_w, fnw, wout_full, gmat, emat, selmat)

    fst = lambda a: a.reshape(T, 128)
    dq1, dk1, dv1 = _attn_bwd(q1, k1, v1, do1, st1, T // BLK, "attn_bwd_d1")
    dq4, dk4, dv4 = _attn_bwd(flat(q4), flat(k4), flat(v4), flat(do4), fst(st4), T // 4 // BLK, "attn_bwd_d4")
    dq16, dk16, dv16 = _attn_bwd(flat(q16), flat(k16), flat(v16), flat(do16), fst(st16), T // 16 // BLK,
                                 "attn_bwd_d16")
    dproj_h, small6, pout_own, pout_rem = _hgrn_bwd(hq, hf, hi, hgrn_lb_raw, tri, trit, drec, sall, dhg,
                                                    rout, routb)

    r4 = lambda a: a.reshape(4, T // 4, AW)
    r16 = lambda a: a.reshape(16, T // 16, AW)
    dproj_a = _dproj_build((dq1, r4(dq4), r16(dq16)), (dk1, r4(dk4), r16(dk16)), (dv1, r4(dv4), r16(dv16)),
                           dag, pos)
    rin, rinb = _grad_w_in(hn, dproj_a, dproj_h)
    gx, _, _, small_all, fin, fout = _bwd_x(dproj_a, dproj_h, xs, dx2, mix_norm_w, w_full, rin, rinb,
                                            small4, small6, pout_own, pout_rem)
    g_w_in = fin.reshape(D, 1024)
    g_w_out = fout.reshape(256, D)

    params = [(mix_norm_w, m_mix_norm_w, v_mix_norm_w),
              (attn_out_norm_w, m_attn_out_norm_w, v_attn_out_norm_w),
              (hgrn_out_norm_w, m_hgrn_out_norm_w, v_hgrn_out_norm_w),
              (hgrn_lb_raw, m_hgrn_lb_raw, v_hgrn_lb_raw),
              (fnw, m_final_norm_w.reshape(1, D), v_final_norm_w.reshape(1, D))]
    d_in, nm_in, nv_in, d_out, nm_out, nv_out, *so = _adamw(
        (w_in.reshape(D, 1024), g_w_in, m_w_in.reshape(D, 1024), v_w_in.reshape(D, 1024)),
        (w_out.reshape(256, D), g_w_out, m_w_out.reshape(256, D), v_w_out.reshape(256, D)), small_all, params)
    loss = so[0].reshape(())
    g_s = [so[1 + 4 * p] for p in range(5)]
    d_s = [so[2 + 4 * p] for p in range(5)]
    m_s = [so[3 + 4 * p] for p in range(5)]
    v_s = [so[4 + 4 * p] for p in range(5)]
    for lst in (g_s, d_s, m_s, v_s):
        lst[4] = lst[4].reshape(D)

    return (loss, gx.reshape(1, T, D),
            g_w_in.reshape(1, D, 1024), g_w_out.reshape(1, 256, D), *g_s,
            d_in.reshape(1, D, 1024), d_out.reshape(1, 256, D), *d_s,
            nm_in.reshape(1, D, 1024), nm_out.reshape(1, 256, D), *m_s,
            nv_in.reshape(1, D, 1024), nv_out.reshape(1, 256, D), *v_s)
```

```python
import functools

import numpy as np
import jax
import jax.numpy as jnp
from jax import lax
from jax.experimental import pallas as pl
from jax.experimental.pallas import tpu as pltpu

F32 = jnp.float32
BF16 = jnp.bfloat16

T = 4096
D = 1024
AW = 512
HW = 512
NCOL = 4096
HEAD = 64
BLK = 128
CHUNK = 64
EPS = 1e-6
SCALE = HEAD ** -0.5
NEG = -1e30
ROPE_THETA = 500000.0
INV_FREQ = [float(v) for v in
            (np.float32(ROPE_THETA) ** (-(np.arange(8, dtype=np.float32)) * np.float32(0.125)))]
LR, B1, B2, AEPS, WD, STEP = 0.001, 0.9, 0.999, 1e-08, 0.01, 10
VMEM_LIMIT = 56 * 1024 * 1024
MESH = pl.DeviceIdType.MESH


def _cp(sem=None, **kw):
    return pltpu.CompilerParams(dimension_semantics=sem, vmem_limit_bytes=VMEM_LIMIT, **kw)


def _mm(a, b):
    return jnp.dot(a, b, preferred_element_type=F32)


def _mm_nt(a, b):
    return lax.dot_general(a, b, (((1,), (1,)), ((), ())), preferred_element_type=F32)


def _mm_tn(a, b):
    return lax.dot_general(a, b, (((0,), (0,)), ((), ())), preferred_element_type=F32)


def _split3(x):
    h = x.astype(BF16)
    r = x - h.astype(F32)
    m = r.astype(BF16)
    l = (r - m.astype(F32)).astype(BF16)
    return h, m, l


def _mm_exact_l(mat_bf, x):
    h, m, l = _split3(x)
    return _mm(mat_bf, h) + _mm(mat_bf, m) + _mm(mat_bf, l)


def _mm_exact_r(x, mat_bf):
    h = x.astype(BF16)
    l = (x - h.astype(F32)).astype(BF16)
    return _mm(h, mat_bf) + _mm(l, mat_bf)


def _sigmoid(x):
    return 0.5 * jnp.tanh(0.5 * x) + 0.5


def _rope_tables(pos):
    lane = lax.broadcasted_iota(jnp.int32, (1, 128), 1)
    jl = lane & 63
    fi = jl & 7
    inv = jnp.zeros((1, 128), F32)
    for kk in range(8):
        inv = jnp.where(fi == kk, INV_FREQ[kk], inv)
    ang = pos.astype(F32) * inv
    c = jnp.cos(ang)
    s = jnp.sin(ang)
    cosf = jnp.where(jl < 16, c, 1.0)
    s1 = jnp.where(jl < 8, -s, 0.0)
    s2 = jnp.where((jl >= 8) & (jl < 16), s, 0.0)
    return cosf, s1, s2


def _rope(t, cosf, s1, s2):
    parts = []
    for ci in range(t.shape[1] // 128):
        tc = t[:, ci * 128:(ci + 1) * 128]
        parts.append(tc * cosf + pltpu.roll(tc, 120, 1) * s1 + pltpu.roll(tc, 8, 1) * s2)
    return jnp.concatenate(parts, axis=1)


def _rope_bwd(g, cosf, s1, s2):
    parts = []
    for ci in range(g.shape[1] // 128):
        gc = g[:, ci * 128:(ci + 1) * 128]
        parts.append(gc * cosf + pltpu.roll(gc * s1, 8, 1) + pltpu.roll(gc * s2, 120, 1))
    return jnp.concatenate(parts, axis=1)


def _perm_store(val, scr, scr2, o1, o4, o16, dt):
    n = val.shape[0]
    q = n // 4
    o1[...] = val.astype(dt)
    for ci in range(val.shape[1] // 128):
        cs = slice(ci * 128, (ci + 1) * 128)
        scr[ci] = val[:, cs]
        for r4 in range(4):
            part = scr[ci, pl.ds(r4, q, stride=4), :]
            o4[r4, :, cs] = part.astype(dt)
            scr2[ci, r4 * q:(r4 + 1) * q, :] = part
        for r4 in range(4):
            for b in range(4):
                o16[r4 + 4 * b, :, cs] = scr2[ci, pl.ds(r4 * q + b, q // 4, stride=4), :].astype(dt)


def _unperm_load(r4, r16, scr_a, scr_b, scr_c):
    n = scr_a.shape[1]
    q = n // 4
    nc = r4.shape[-1] // 128
    for ci in range(nc):
        cs = slice(ci * 128, (ci + 1) * 128)
        for rr in range(4):
            scr_a[ci, pl.ds(rr, q, stride=4), :] = r4[rr, :, cs].astype(F32)
        for rr in range(4):
            for b in range(4):
                scr_c[ci, pl.ds(rr * q + b, q // 4, stride=4), :] = r16[rr + 4 * b, :, cs].astype(F32)
        for rr in range(4):
            scr_b[ci, pl.ds(rr, q, stride=4), :] = scr_c[ci, rr * q:(rr + 1) * q, :]
    return (jnp.concatenate([scr_a[ci] for ci in range(nc)], axis=1),
            jnp.concatenate([scr_b[ci] for ci in range(nc)], axis=1))


def _fwd_in(x, pos, mixw, w_in, w_out, jm_arr):
    TT = 512
    NT = T // TT

    def body(jm_ref, x_ref, pos_ref, mw_ref, win_ref, wout_ref,
             hnt_ref, q1, k1, v1, q4, k4, v4, q16, k16, v16, ag, hq, hf, hi, hg, wfull_o, woutfull_o,
             wbuf, wobuf, hn_all, scr, scr2, stage, send_sems, recv_sems, loc_sems):
        s = pl.program_id(0)
        i = pl.program_id(1)
        mx, my, c = lax.axis_index("x"), lax.axis_index("y"), lax.axis_index("c")
        me, sibling = (mx, my, c), (mx, my, 1 - c)
        chips = [(mx, 1 - my), (1 - mx, my), (1 - mx, 1 - my)]
        jm = 2 * mx + my
        rows_in = [pl.ds(pl.multiple_of(h * 512, 512), 512) for h in (c, 1 - c)]
        rows_out = [pl.ds(pl.multiple_of(h * 128, 128), 128) for h in (c, 1 - c)]

        def blk(k):
            return lax.bitwise_xor(jm, k + 1)

        def rc(n, ref, to):
            return pltpu.make_async_remote_copy(src_ref=ref, dst_ref=ref, send_sem=send_sems.at[n],
                                                recv_sem=recv_sems.at[n], device_id=to, device_id_type=MESH)

        halves = [pl.ds(0, 512), pl.ds(512, 512)]
        send_in = lambda k, h: rc(12 + 2 * k + h, wbuf.at[jm, rows_in[0], halves[h]], (*chips[k], c))
        got_in = lambda k, h: rc(12 + 2 * k + h, wbuf.at[blk(k), rows_in[0], halves[h]], me)
        relay = lambda h: rc(16 + h, wbuf.at[blk(h), rows_in[0], halves[h]], (*chips[1 - h], c))
        got_relay = lambda h: rc(16 + h, wbuf.at[blk(2), rows_in[0], halves[h]], me)
        send_out = lambda k: rc(3 + k, wobuf.at[jm, rows_out[0], :], (*chips[k], c))
        got_out = lambda k: rc(3 + k, wobuf.at[blk(k), rows_out[0], :], me)
        pass_in = lambda k: rc(6 + k, wbuf.at[blk(k), rows_in[0], :], sibling)
        pass_out = lambda k: rc(9 + k, wobuf.at[blk(k), rows_out[0], :], sibling)
        passed_in = lambda k: rc(6 + k, wbuf.at[blk(k), rows_in[1], :], me)
        passed_out = lambda k: rc(9 + k, wobuf.at[blk(k), rows_out[1], :], me)

        def keep(j, n):
            return pltpu.make_async_copy(wbuf.at[j], wfull_o.at[:, pl.ds(j * 1024, 1024)], loc_sems.at[n])

        @pl.when((s == 0) & (i == 0))
        def _():
            for p in range(5):
                src = win_ref.at[pl.ds(p * 256, 256), :] if p < 4 else wout_ref
                load = pltpu.make_async_copy(src, stage, loc_sems.at[4])
                load.start()
                load.wait()
                if p < 4:
                    wbuf[jm, p * 256:(p + 1) * 256, :] = stage[...].astype(BF16)
                else:
                    wobuf[jm] = stage[...].astype(BF16)
            for k in range(2):
                for h in range(2):
                    send_in(k, h).start()
            keep(jm, 0).start()

        def arrive(k):
            if k == 0:
                for kk in range(2):
                    for h in range(2):
                        got_in(kk, h).wait_recv()
                relay(0).start()
                relay(1).start()
            if k == 2:
                got_relay(0).wait_recv()
                got_relay(1).wait_recv()
            pass_in(k).start()
            passed_in(k).wait_recv()
            keep(blk(k), k + 1).start()
            if k == 2:
                for kk in range(3):
                    send_out(kk).start()

        for k in range(3):
            pl.when((s == k + 1) & (i == 0))(functools.partial(arrive, k))

        tile = pl.ds(pl.multiple_of(i * TT, TT), TT)

        @pl.when(s == 0)
        def _():
            xv = x_ref[...]
            r = lax.rsqrt(jnp.mean(xv * xv, axis=-1, keepdims=True) + EPS)
            hnf = (xv * r) * mw_ref[...]
            hn_all[tile, :] = hnf.astype(BF16)
            hnt_ref[...] = hnf.T.astype(BF16)

        def project(jj):
            hn = hn_all[tile, :]
            lo = _mm(hn, wbuf[jj, :, 0:512])
            hi_cols = _mm(hn, wbuf[jj, :, 512:1024])
            if jj == 0:
                cosf, s1, s2 = _rope_tables(pos_ref[...])
                _perm_store(_rope(lo, cosf, s1, s2), scr, scr2, q1, q4, q16, BF16)
                _perm_store(_rope(hi_cols, cosf, s1, s2), scr, scr2, k1, k4, k16, BF16)
            elif jj == 1:
                _perm_store(lo, scr, scr2, v1, v4, v16, BF16)
                ag[...] = hi_cols.astype(BF16)
            elif jj == 2:
                hq[...] = lo.astype(BF16)
                hf[...] = hi_cols.astype(BF16)
            else:
                hi[...] = lo.astype(BF16)
                hg[...] = hi_cols.astype(BF16)

        j = lax.bitwise_xor(jm, s)
        for jj in range(4):
            pl.when(j == jj)(functools.partial(project, jj))

        @pl.when((s == 3) & (i == NT - 1))
        def _():
            for k in range(3):
                got_out(k).wait_recv()
                pass_out(k).start()
            for k in range(3):
                passed_out(k).wait_recv()
            out = pltpu.make_async_copy(wobuf, woutfull_o, loc_sems.at[4])
            out.start()
            for h in range(2):
                relay(h).wait_send()
                for k in range(2):
                    send_in(k, h).wait_send()
            for k in range(3):
                send_out(k).wait_send()
                pass_in(k).wait_send()
                pass_out(k).wait_send()
            keep(jm, 0).wait()
            for k in range(3):
                keep(blk(k), k + 1).wait()
            out.wait()

    def at_stage_of(jb):
        def index(s, i, jm_ref):
            sa = lax.bitwise_xor(jm_ref[0], jb)
            return jnp.where(s < sa, 0, jnp.where(s == sa, i, NT - 1))
        return index

    tok = lambda w, jb: pl.BlockSpec((TT, w), lambda s, i, jm_ref: (at_stage_of(jb)(s, i, jm_ref), 0))
    d4 = lambda jb: pl.BlockSpec((4, TT // 4, AW), lambda s, i, jm_ref: (0, at_stage_of(jb)(s, i, jm_ref), 0))
    d16 = lambda jb: pl.BlockSpec((16, TT // 16, AW), lambda s, i, jm_ref: (0, at_stage_of(jb)(s, i, jm_ref), 0))
    hbm = pl.BlockSpec(memory_space=pltpu.HBM)
    sd = lambda shape, dt: jax.ShapeDtypeStruct(shape, dt)
    in_own_stage = lambda s, i: jnp.where(s == 0, i, NT - 1)
    grid_spec = pltpu.PrefetchScalarGridSpec(
        num_scalar_prefetch=1, grid=(4, NT),
        in_specs=[pl.BlockSpec((TT, D), lambda s, i, jm_ref: (in_own_stage(s, i), 0)),
                  pl.BlockSpec((TT, 1), lambda s, i, jm_ref: (i, 0)),
                  pl.BlockSpec((1, D), lambda s, i, jm_ref: (0, 0)), hbm, hbm],
        out_specs=[pl.BlockSpec((D, TT), lambda s, i, jm_ref: (0, in_own_stage(s, i))),
                   tok(AW, 0), tok(AW, 0), tok(AW, 1), d4(0), d4(0), d4(1), d16(0), d16(0), d16(1),
                   tok(AW, 1), tok(AW, 2), tok(AW, 2), tok(AW, 3), tok(AW, 3), hbm, hbm],
        scratch_shapes=[pltpu.VMEM((4, D, 1024), BF16), pltpu.VMEM((4, 256, D), BF16), pltpu.VMEM((T, D), BF16),
                        pltpu.VMEM((4, TT, 128), F32), pltpu.VMEM((4, TT, 128), F32), pltpu.VMEM((256, 1024), F32),
                        pltpu.SemaphoreType.DMA((18,)),
                        pltpu.SemaphoreType.DMA((18,)), pltpu.SemaphoreType.DMA((6,))])
    return pl.pallas_call(
        body, name="fwd_in", grid_spec=grid_spec,
        out_shape=[sd((D, T), BF16)] + [sd((T, AW), BF16)] * 3 + [sd((4, T // 4, AW), BF16)] * 3
        + [sd((16, T // 16, AW), BF16)] * 3
        + [sd((T, AW), BF16)] * 5 + [sd((D, NCOL), BF16), sd((4, 256, D), BF16)],
        compiler_params=_cp(("arbitrary", "arbitrary")),
    )(jm_arr, x, pos, mixw, w_in, w_out)


def _band_mask(key_axis, nkeys=2 * BLK):
    shape = (nkeys, 2 * BLK) if key_axis == 0 else (2 * BLK, nkeys)
    kj = lax.broadcasted_iota(jnp.int32, shape, key_axis)
    qi = lax.broadcasted_iota(jnp.int32, shape, 1 - key_axis) & (BLK - 1)
    return (kj >= qi) & (kj <= qi + BLK), kj, qi


def _stack_heads(t2, in_a):
    z = jnp.zeros_like(t2)
    return jnp.concatenate([jnp.where(in_a[0], t2, z), jnp.where(in_a[1], t2, z)], axis=0)


def _attn_fwd(q, k, v, nb, name):
    n = 8 if nb >= 8 else 4
    CH = n * BLK
    halo = nb > n

    def body(*refs):
        if halo:
            q_ref, k_ref, v_ref, kp_ref, vp_ref, o_ref, lse_ref = refs
        else:
            q_ref, k_ref, v_ref, o_ref, lse_ref = refs
        lane = lax.broadcasted_iota(jnp.int32, (1, 128), 1)
        in_a = [lane < HEAD, lane >= HEAD]
        band, kj, _ = _band_mask(1)
        thr0 = jnp.where((n * pl.program_id(0)) % nb == 0, BLK, 0) if halo else BLK
        mask0 = band & (kj >= thr0)
        mask_first = band & (kj >= BLK)
        for b in range(n):
            rs = slice(b * BLK, (b + 1) * BLK)
            stat = jnp.zeros((BLK, 128), F32)
            for hp in range(4):
                cs = slice(hp * 128, (hp + 1) * 128)
                q2s = _stack_heads(q_ref[rs, cs], in_a)
                if b == 0:
                    kprev = kp_ref[:, cs] if halo else k_ref[rs, cs]
                    vprev = vp_ref[:, cs] if halo else v_ref[rs, cs]
                    kk = jnp.concatenate([kprev, k_ref[rs, cs]], axis=0)
                    vv = jnp.concatenate([vprev, v_ref[rs, cs]], axis=0)
                    mask = mask0
                else:
                    kk = k_ref[(b - 1) * BLK:(b + 1) * BLK, cs]
                    vv = v_ref[(b - 1) * BLK:(b + 1) * BLK, cs]
                    mask = mask_first if b % nb == 0 else band
                s = jnp.where(mask, _mm_nt(q2s, kk) * SCALE, NEG)
                m = jnp.max(s, axis=-1, keepdims=True)
                p = jnp.exp(s - m)
                l = jnp.sum(p, axis=-1, keepdims=True)
                o = _mm(p.astype(BF16), vv) / l
                lse = m + jnp.log(l)
                o_ref[rs, cs] = jnp.where(in_a[0], o[:BLK], o[BLK:]).astype(BF16)
                stat = jnp.where(lane == 2 * hp, lse[:BLK], stat)
                stat = jnp.where(lane == 2 * hp + 1, lse[BLK:], stat)
            lse_ref[rs, :] = stat

    cur = pl.BlockSpec((CH, AW), lambda i: (i, 0))
    prev = pl.BlockSpec((BLK, AW), lambda i: (jnp.maximum(n * i - 1, 0), 0))
    return pl.pallas_call(
        body, name=name, grid=(T // CH,),
        in_specs=[cur, cur, cur] + ([prev, prev] if halo else []),
        out_specs=[cur, pl.BlockSpec((CH, 128), lambda i: (i, 0))],
        out_shape=[jax.ShapeDtypeStruct((T, AW), BF16), jax.ShapeDtypeStruct((T, 128), F32)],
        compiler_params=_cp(("parallel",)),
    )(*((q, k, v) + ((k, v) if halo else ())))


def _attn_bwd(q, k, v, do, st, nb, name):
    n = 8 if nb >= 8 else 4
    CH = n * BLK
    NBLK = T // BLK
    halo = nb > n

    def body(*refs):
        if halo:
            (q_ref, k_ref, v_ref, do_ref, st_ref, kp_ref, vp_ref, qn_ref, don_ref, stn_ref,
             dq_ref, dk_ref, dv_ref) = refs
        else:
            q_ref, k_ref, v_ref, do_ref, st_ref, dq_ref, dk_ref, dv_ref = refs
        i = pl.program_id(0)
        lane = lax.broadcasted_iota(jnp.int32, (1, 128), 1)
        in_a = [lane < HEAD, lane >= HEAD]
        band, kj, _ = _band_mask(0)
        thr0 = jnp.where((n * i) % nb == 0, BLK, 0) if halo else BLK
        mask0 = band & (kj >= thr0)
        mask_first = band & (kj >= BLK)

        def stat_rows(st_t, hp):
            lse_r = jnp.concatenate([st_t[2 * hp:2 * hp + 1, :], st_t[2 * hp + 1:2 * hp + 2, :]], axis=1)
            dl_r = jnp.concatenate([st_t[8 + 2 * hp:9 + 2 * hp, :], st_t[9 + 2 * hp:10 + 2 * hp, :]], axis=1)
            return lse_r, dl_r

        st_t = [st_ref[b * BLK:(b + 1) * BLK, :].T for b in range(n)]
        if halo:
            nxt_thr = jnp.where((n * i + n) % nb == 0, 2 * BLK, 0)
            _, kj1, qi1 = _band_mask(0, BLK)
            mask_next = kj1 >= qi1 + nxt_thr
            stn_t = stn_ref[...].T

        for hp in range(4):
            cs = slice(hp * 128, (hp + 1) * 128)
            kb = [k_ref[b * BLK:(b + 1) * BLK, cs] for b in range(n)]
            vb = [v_ref[b * BLK:(b + 1) * BLK, cs] for b in range(n)]
            dk_acc = [jnp.zeros((BLK, 128), F32) for _ in range(n)]
            dv_acc = [jnp.zeros((BLK, 128), F32) for _ in range(n)]
            for b in range(n):
                rs = slice(b * BLK, (b + 1) * BLK)
                q2s = _stack_heads(q_ref[rs, cs], in_a)
                do2s = _stack_heads(do_ref[rs, cs], in_a)
                if b == 0:
                    kprev = kp_ref[:, cs] if halo else kb[0]
                    vprev = vp_ref[:, cs] if halo else vb[0]
                    mask = mask0
                else:
                    kprev, vprev, mask = kb[b - 1], vb[b - 1], (mask_first if b % nb == 0 else band)
                kk = jnp.concatenate([kprev, kb[b]], axis=0)
                vv = jnp.concatenate([vprev, vb[b]], axis=0)
                lse_r, dl_r = stat_rows(st_t[b], hp)
                s_t = jnp.where(mask, _mm_nt(kk, q2s) * SCALE, NEG)
                p_t = jnp.exp(s_t - lse_r)
                ds_t = (p_t * (_mm_nt(vv, do2s) - dl_r)).astype(BF16)
                dkk = _mm(ds_t, q2s) * SCALE
                dvv = _mm(p_t.astype(BF16), do2s)
                dqs = _mm_tn(ds_t, kk) * SCALE
                dq_ref[rs, cs] = jnp.where(in_a[0], dqs[:BLK], dqs[BLK:]).astype(BF16)
                dk_acc[b] += dkk[BLK:]
                dv_acc[b] += dvv[BLK:]
                if b > 0:
                    dk_acc[b - 1] += dkk[:BLK]
                    dv_acc[b - 1] += dvv[:BLK]
            if halo:
                q2s = _stack_heads(qn_ref[:, cs], in_a)
                do2s = _stack_heads(don_ref[:, cs], in_a)
                lse_r, dl_r = stat_rows(stn_t, hp)
                s_t = jnp.where(mask_next, _mm_nt(kb[n - 1], q2s) * SCALE, NEG)
                p_t = jnp.exp(s_t - lse_r)
                ds_t = (p_t * (_mm_nt(vb[n - 1], do2s) - dl_r)).astype(BF16)
                dk_acc[n - 1] += _mm(ds_t, q2s) * SCALE
                dv_acc[n - 1] += _mm(p_t.astype(BF16), do2s)
            for b in range(n):
                dk_ref[b * BLK:(b + 1) * BLK, cs] = dk_acc[b].astype(BF16)
                dv_ref[b * BLK:(b + 1) * BLK, cs] = dv_acc[b].astype(BF16)

    cur = pl.BlockSpec((CH, AW), lambda i: (i, 0))
    cur_st = pl.BlockSpec((CH, 128), lambda i: (i, 0))
    prev = pl.BlockSpec((BLK, AW), lambda i: (jnp.maximum(n * i - 1, 0), 0))
    nxt = pl.BlockSpec((BLK, AW), lambda i: (jnp.minimum(n * i + n, NBLK - 1), 0))
    nxt_st = pl.BlockSpec((BLK, 128), lambda i: (jnp.minimum(n * i + n, NBLK - 1), 0))
    ins = [cur] * 4 + [cur_st] + ([prev, prev, nxt, nxt, nxt_st] if halo else [])
    args = (q, k, v, do, st) + ((k, v, q, do, st) if halo else ())
    return pl.pallas_call(
        body, name=name, grid=(T // CH,),
        in_specs=ins,
        out_specs=[cur] * 3,
        out_shape=[jax.ShapeDtypeStruct((T, AW), BF16)] * 3,
        compiler_params=_cp(("parallel",)),
    )(*args)


TH = 256
NCH = TH // CHUNK


def _hgrn_common(hq_ref, hf_ref, lbr_ref, tri_ref):
    r0 = lbr_ref[0:1, :]
    r1 = lbr_ref[1:2, :]
    mx = jnp.maximum(r0, r1)
    e0 = jnp.exp(r0 - mx)
    e1 = jnp.exp(r1 - mx)
    lb = e0 / (e0 + e1)
    hqv = hq_ref[...].astype(F32)
    sq = _sigmoid(hqv)
    qv = hqv * sq
    sf = _sigmoid(hf_ref[...].astype(F32))
    f = lb + (1.0 - lb) * sf
    kv = 1.0 - f
    g = jnp.log(f)
    cum = _mm_exact_l(tri_ref[...], g)
    lastb = jnp.concatenate(
        [jnp.broadcast_to(cum[c * CHUNK + CHUNK - 1:(c + 1) * CHUNK, :], (CHUNK, HW)) for c in range(NCH)], axis=0)
    ea = jnp.exp(cum)
    ena = jnp.exp(-cum)
    eend = jnp.exp(lastb - cum)
    return dict(lb=lb, hq=hqv, sq=sq, q=qv, sf=sf, f=f, k=kv, cum=cum, lastb=lastb, ea=ea, ena=ena, eend=eend,
                qd=qv * ea, ki=kv * ena, ke=kv * eend, dec=jnp.exp(lastb))


def _tri_mask(transposed=False):
    ti = lax.broadcasted_iota(jnp.int32, (TH, TH), 1 if transposed else 0)
    si = lax.broadcasted_iota(jnp.int32, (TH, TH), 0 if transposed else 1)
    return (si <= ti) & ((si // CHUNK) == (ti // CHUNK))


def _hgrn_fwd(hq, hf, hi, lbr, tri):
    def body(hq_ref, hf_ref, hi_ref, lbr_ref, tri_ref, rec_ref, sall_ref, st_scr):
        @pl.when(pl.program_id(0) == 0)
        def _():
            st_scr[...] = jnp.zeros_like(st_scr)

        w = _hgrn_common(hq_ref, hf_ref, lbr_ref, tri_ref)
        qd, ki, ke = w["qd"].astype(BF16), w["ki"].astype(BF16), w["ke"].astype(BF16)
        dec = w["dec"]
        vb = hi_ref[...]
        causal = _tri_mask()
        for h in range(4):
            cs = slice(h * 128, (h + 1) * 128)
            att = jnp.where(causal, _mm_nt(qd[:, cs], ki[:, cs]), 0.0)
            o_intra = _mm(att.astype(BF16), vb[:, cs])
            for c in range(NCH):
                rs = slice(c * CHUNK, (c + 1) * CHUNK)
                st = st_scr[:, cs]
                sall_ref[c, :, cs] = st
                rec_ref[rs, cs] = (o_intra[rs] + _mm_nt(qd[rs, cs], st.astype(BF16))).astype(BF16)
                st_scr[:, cs] = dec[c * CHUNK:c * CHUNK + 1, cs] * st + _mm_tn(vb[rs, cs], ke[rs, cs])

    tok = pl.BlockSpec((TH, HW), lambda i: (i, 0))
    return pl.pallas_call(
        body, name="hgrn_fwd", grid=(T // TH,),
        in_specs=[tok, tok, tok, pl.BlockSpec((2, HW), lambda i: (0, 0)), pl.BlockSpec((TH, TH), lambda i: (0, 0))],
        out_specs=[tok, pl.BlockSpec((NCH, 128, HW), lambda i: (i, 0, 0))],
        out_shape=[jax.ShapeDtypeStruct((T, HW), BF16), jax.ShapeDtypeStruct((T // CHUNK, 128, HW), F32)],
        scratch_shapes=[pltpu.VMEM((128, HW), F32)],
        compiler_params=_cp(("arbitrary",)),
    )(hq, hf, hi, lbr, tri)


def _hgrn_bwd(hq, hf, hi, lbr, tri, trit, drec, sall, dhg, rout, routb):
    NT = T // TH

    def body(hq_ref, hf_ref, hi_ref, lbr_ref, tri_ref, trit_ref, do_ref, sall_ref, dhg_ref, rout_r, routb_r,
             dph_ref, small_ref, pout_o, poutr_o,
             dst_scr, dlb_scr, dqd_scr, dki_scr, dke_scr, dlast_scr, send_sems, recv_sems, loc_sems):
        step = pl.program_id(0)
        loc, rem = _chip_copies(_w_out_piece, rout_r, routb_r, pout_o, poutr_o, send_sems, recv_sems,
                                loc_sems.at[0])

        @pl.when(step == 0)
        def _():
            dst_scr[...] = jnp.zeros_like(dst_scr)
            dlb_scr[...] = jnp.zeros_like(dlb_scr)
            for cp in loc + rem:
                cp.start()

        w = _hgrn_common(hq_ref, hf_ref, lbr_ref, tri_ref)
        qd, ki, ke = w["qd"].astype(BF16), w["ki"].astype(BF16), w["ke"].astype(BF16)
        dec = w["dec"]
        vb = hi_ref[...]
        dob = do_ref[...].astype(BF16)
        causal = _tri_mask()
        causal_t = _tri_mask(transposed=True)
        for h in range(4):
            cs = slice(h * 128, (h + 1) * 128)
            att_t = jnp.where(causal_t, _mm_nt(ki[:, cs], qd[:, cs]), 0.0).astype(BF16)
            datt_t = jnp.where(causal_t, _mm_nt(vb[:, cs], dob[:, cs]), 0.0).astype(BF16)
            datt = jnp.where(causal, _mm_nt(dob[:, cs], vb[:, cs]), 0.0).astype(BF16)
            dv_intra = _mm(att_t, dob[:, cs])
            dqd_intra = _mm(datt, ki[:, cs])
            dki_scr[:, cs] = _mm(datt_t, qd[:, cs])
            for c in reversed(range(NCH)):
                rs = slice(c * CHUNK, (c + 1) * CHUNK)
                dec_c = dec[c * CHUNK:c * CHUNK + 1, :]
                st = sall_ref[c, :, cs]
                dst = dst_scr[:, cs]
                dstb = dst.astype(BF16)
                dph_ref[rs, 2 * HW + h * 128:2 * HW + (h + 1) * 128] = (
                    dv_intra[rs] + _mm_nt(ke[rs, cs], dstb)).astype(BF16)
                dqd_scr[rs, cs] = dqd_intra[rs] + _mm(dob[rs, cs], st.astype(BF16))
                dke_scr[rs, cs] = _mm(vb[rs, cs], dstb)
                ddec = jnp.sum(dst * st, axis=0, keepdims=True)
                dlast_scr[c:c + 1, cs] = ddec * dec_c[:, cs]
                dst_scr[:, cs] = dec_c[:, cs] * dst + _mm_tn(dob[rs, cs], qd[rs, cs])
        dqd, dki, dke = dqd_scr[...], dki_scr[...], dke_scr[...]
        dq = dqd * w["ea"]
        dk = dki * w["ena"] + dke * w["eend"]
        dcum = dqd * w["qd"] - dki * w["ki"] - dke * w["ke"]
        dkeke = dke * w["ke"]
        dlastb = jnp.concatenate(
            [jnp.broadcast_to(dlast_scr[c:c + 1, :] + jnp.sum(dkeke[c * CHUNK:(c + 1) * CHUNK], axis=0, keepdims=True),
                              (CHUNK, HW)) for c in range(NCH)], axis=0)
        dg = _mm_exact_l(trit_ref[...], dcum) + dlastb
        df = dg / w["f"] - dk
        lb, sf, sq = w["lb"], w["sf"], w["sq"]
        dph_ref[:, HW:2 * HW] = (df * (1.0 - lb) * sf * (1.0 - sf)).astype(BF16)
        dph_ref[:, 0:HW] = (dq * (sq * (1.0 + w["hq"] * (1.0 - sq)))).astype(BF16)
        dph_ref[:, 3 * HW:4 * HW] = dhg_ref[...]
        dlb_scr[...] += jnp.sum(df * (1.0 - sf), axis=0, keepdims=True)

        @pl.when(step == NT - 1)
        def _():
            gr = dlb_scr[...] * lb * (1.0 - lb)
            small_ref[...] = jnp.zeros_like(small_ref)
            small_ref[0:1, 0:HW] = gr
            small_ref[1:2, 0:HW] = -gr
            for cp in rem:
                cp.wait_recv()
            for cp in rem:
                cp.wait_send()
            for cp in loc:
                cp.wait()

    tok = pl.BlockSpec((TH, HW), lambda i: (NT - 1 - i, 0))
    const = lambda shape: pl.BlockSpec(shape, lambda i: (0,) * len(shape))
    hbm = pl.BlockSpec(memory_space=pltpu.HBM)
    return pl.pallas_call(
        body, name="hgrn_bwd", grid=(NT,),
        in_specs=[tok, tok, tok, const((2, HW)), const((TH, TH)), const((TH, TH)), tok,
                  pl.BlockSpec((NCH, 128, HW), lambda i: (NT - 1 - i, 0, 0)), tok, hbm, hbm],
        out_specs=[pl.BlockSpec((TH, NCOL // 2), lambda i: (NT - 1 - i, 0)), const((8, D)), hbm, hbm],
        out_shape=[jax.ShapeDtypeStruct((T, NCOL // 2), BF16), jax.ShapeDtypeStruct((8, D), F32),
                   jax.ShapeDtypeStruct((128, D), F32), jax.ShapeDtypeStruct((3, 128, D), BF16)],
        scratch_shapes=[pltpu.VMEM((128, HW), F32), pltpu.VMEM((1, HW), F32), pltpu.VMEM((TH, HW), F32),
                        pltpu.VMEM((TH, HW), F32), pltpu.VMEM((TH, HW), F32), pltpu.VMEM((8, HW), F32),
                        pltpu.SemaphoreType.DMA((3,)), pltpu.SemaphoreType.DMA((3,)), pltpu.SemaphoreType.DMA((1,))],
        compiler_params=_cp(("arbitrary",)),
    )(hq, hf, hi, lbr, tri, trit, drec, sall, dhg, rout, routb)


def _fwd_out(o1, o4, o16, l1, l4, l16, rec, ag, hg, x, tgt, anw, hnw, fnw, wout_full, gmat, emat, selmat):
    TT = 256

    def body(o1_r, o4_r, o16_r, l1_r, l4_r, l16_r, rec_r, ag_r, hg_r, x_r, tgt_r, anw_r, hnw_r, fnw_r, wo_r, g_r,
             e_r, sel_r, dx2_o, do1_o, do4_o, do16_o, st1_o, st4_o, st16_o, drec_o, dag_o, dhg_o,
             rout_o, routb_o, small_o, scr_a, scr_b, scr_c, gwout_o, rbuf, send_sems, recv_sems):
        @pl.when(pl.program_id(0) == 0)
        def _():
            gwout_o[...] = jnp.zeros_like(gwout_o)
            small_o[...] = jnp.zeros_like(small_o)

        def unperm(r4, r16):
            return _unperm_load(r4, r16, scr_a, scr_b, scr_c)

        def perm_out(val, p1, p4, p16, dt):
            _perm_store(val, scr_a, scr_b, p1, p4, p16, dt)

        o4u, o16u = unperm(o4_r, o16_r)
        l4c, l16c = unperm(l4_r, l16_r)
        l1c = l1_r[...]
        mxc = jnp.maximum(jnp.maximum(l1c, l4c), l16c)
        w1c, w4c, w16c = jnp.exp(l1c - mxc), jnp.exp(l4c - mxc), jnp.exp(l16c - mxc)
        denc = w1c + w4c + w16c
        lane = lax.broadcasted_iota(jnp.int32, (1, 128), 1)
        lse_c = jnp.where(lane < 8, mxc + jnp.log(denc), 0.0)
        em = e_r[...]
        wn1 = _mm_exact_r(w1c / denc, em)
        wn4 = _mm_exact_r(w4c / denc, em)
        o1v = o1_r[...].astype(F32)
        attn = wn1 * o1v + wn4 * o4u + (1.0 - wn1 - wn4) * o16u
        gm = g_r[...]

        def head_mean_a(t):
            return jnp.concatenate([_mm_exact_r(t[:, :256], gm), _mm_exact_r(t[:, 256:], gm)], axis=1)

        def head_mean_h(t):
            return jnp.concatenate(
                [jnp.broadcast_to(jnp.mean(t[:, h * 128:(h + 1) * 128], axis=-1, keepdims=True), (TT, 128))
                 for h in range(4)], axis=1)

        rs_a = lax.rsqrt(head_mean_a(attn * attn) + EPS)
        n_a = attn * rs_a
        agv = ag_r[...].astype(F32)
        sg_a = _sigmoid(agv)
        si_a = agv * sg_a
        anw_v = anw_r[...]
        y_a = (n_a * anw_v) * si_a
        recv = rec_r[...].astype(F32)
        rs_h = lax.rsqrt(head_mean_h(recv * recv) + EPS)
        n_h = recv * rs_h
        hgv = hg_r[...].astype(F32)
        sg_h = _sigmoid(hgv)
        si_h = hgv * sg_h
        hnw_v = hnw_r[...]
        y_h = (n_h * hnw_v) * si_h
        mixed = jnp.concatenate([y_a, y_h], axis=1).astype(BF16)
        xv = x_r[...]
        x2 = xv + _mm(mixed, wo_r[...])
        r2 = lax.rsqrt(jnp.mean(x2 * x2, axis=-1, keepdims=True) + EPS)
        fnw_v = fnw_r[...]
        xn = x2 * r2
        err = xn * fnw_v - tgt_r[...]
        small_o[2:3, :] += 0.5 * jnp.sum(jnp.mean(err * err, axis=-1, keepdims=True), axis=0, keepdims=True)
        dy = err * (1.0 / D)
        small_o[0:1, :] += jnp.sum(dy * xn, axis=0, keepdims=True)
        dyw = dy * fnw_v
        dx2 = r2 * dyw - x2 * ((r2 * r2 * r2) * jnp.mean(dyw * x2, axis=-1, keepdims=True))
        dx2_o[...] = dx2
        dx2b = dx2.astype(BF16)
        gwout_o[...] += _mm_tn(mixed, dx2b)
        dmix = _mm_nt(dx2b, wo_r[...])
        dm_a, dm_h = dmix[:, :AW], dmix[:, AW:]
        dag_o[...] = (dm_a * (n_a * anw_v) * (sg_a * (1.0 + agv * (1.0 - sg_a)))).astype(BF16)
        dn_a = dm_a * anw_v * si_a
        small_o[1:2, 0:AW] += jnp.sum(dm_a * n_a * si_a, axis=0, keepdims=True)
        dattn = rs_a * (dn_a - n_a * head_mean_a(dn_a * n_a))
        perm_out(dattn, do1_o, do4_o, do16_o, BF16)
        stats = lse_c + _mm_exact_r(dattn * attn, sel_r[...])
        perm_out(stats, st1_o, st4_o, st16_o, F32)
        dhg_o[...] = (dm_h * (n_h * hnw_v) * (sg_h * (1.0 + hgv * (1.0 - sg_h)))).astype(BF16)
        dn_h = dm_h * hnw_v * si_h
        small_o[1:2, AW:] += jnp.sum(dm_h * n_h * si_h, axis=0, keepdims=True)
        drec_o[...] = (rs_h * (dn_h - n_h * head_mean_h(dn_h * n_h))).astype(BF16)

        @pl.when(pl.program_id(0) == T // TT - 1)
        def _():
            x, y, c = lax.axis_index("x"), lax.axis_index("y"), lax.axis_index("c")
            cps = [pltpu.make_async_remote_copy(
                src_ref=gwout_o.at[pl.ds(pl.multiple_of(j * 256 + (1 - c) * 128, 128), 128), :], dst_ref=rbuf.at[j],
                send_sem=send_sems.at[j], recv_sem=recv_sems.at[j], device_id=(x, y, 1 - c), device_id_type=MESH)
                for j in range(4)]
            for cp in cps:
                cp.start()
            for j, cp in enumerate(cps):
                cp.wait_recv()
                red = gwout_o[pl.ds(pl.multiple_of(j * 256 + c * 128, 128), 128), :] + rbuf[j]
                rout_o[j * 128:(j + 1) * 128, :] = red
                routb_o[j * 128:(j + 1) * 128, :] = red.astype(BF16)
            for cp in cps:
                cp.wait_send()

    tok = lambda w: pl.BlockSpec((TT, w), lambda i: (i, 0))
    d4 = pl.BlockSpec((4, TT // 4, AW), lambda i: (0, i, 0))
    d16 = pl.BlockSpec((16, TT // 16, AW), lambda i: (0, i, 0))
    const = lambda shape: pl.BlockSpec(shape, lambda i: (0,) * len(shape))
    sd = lambda shape, dt: jax.ShapeDtypeStruct(shape, dt)
    c4 = pl.BlockSpec((4, TT // 4, 128), lambda i: (0, i, 0))
    c16 = pl.BlockSpec((16, TT // 16, 128), lambda i: (0, i, 0))
    p3 = lambda w, dt: [sd((T, w), dt), sd((4, T // 4, w), dt), sd((16, T // 16, w), dt)]
    return pl.pallas_call(
        body, name="fwd_out", grid=(T // TT,),
        in_specs=[tok(AW), d4, d16, tok(128), c4, c16, tok(AW), tok(AW), tok(AW), tok(D), tok(D),
                  const((1, AW)), const((1, HW)), const((1, D)), const((D, D)), const((256, 256)),
                  const((128, AW)), const((AW, 128))],
        out_specs=[tok(D)] + [tok(AW), d4, d16] + [tok(128), c4, c16] + [tok(AW)] * 3
        + [const((512, D)), const((512, D)), const((8, D))],
        out_shape=[sd((T, D), F32)] + p3(AW, BF16) + p3(128, F32)
        + [sd((T, AW), BF16), sd((T, AW), BF16), sd((T, AW), BF16), sd((512, D), F32), sd((512, D), BF16),
           sd((8, D), F32)],
        scratch_shapes=[pltpu.VMEM((4, TT, 128), F32)] * 3 + [pltpu.VMEM((D, D), F32),
                        pltpu.VMEM((4, 128, D), F32), pltpu.SemaphoreType.DMA((4,)), pltpu.SemaphoreType.DMA((4,))],
        compiler_params=_cp(("arbitrary",)),
    )(o1, o4, o16, l1, l4, l16, rec, ag, hg, x, tgt, anw, hnw, fnw, wout_full, gmat, emat, selmat)


def _dproj_build(dq, dk, dv, dag, pos):
    TT = 512

    def body(dq1, dq4, dq16, dk1, dk4, dk16, dv1, dv4, dv16, dag_r, pos_r, dproj_o, scr_a, scr_b, scr_c):
        def unperm_sum(r1, r4, r16):
            u4, u16 = _unperm_load(r4, r16, scr_a, scr_b, scr_c)
            return r1[...] + u4 + u16

        cosf, s1, s2 = _rope_tables(pos_r[...])
        dproj_o[:, 0:512] = _rope_bwd(unperm_sum(dq1, dq4, dq16), cosf, s1, s2).astype(BF16)
        dproj_o[:, 512:1024] = _rope_bwd(unperm_sum(dk1, dk4, dk16), cosf, s1, s2).astype(BF16)
        dproj_o[:, 1024:1536] = unperm_sum(dv1, dv4, dv16).astype(BF16)
        dproj_o[:, 1536:2048] = dag_r[...]

    tok = lambda w: pl.BlockSpec((TT, w), lambda i: (i, 0))
    d4 = pl.BlockSpec((4, TT // 4, AW), lambda i: (0, i, 0))
    d16 = pl.BlockSpec((16, TT // 16, AW), lambda i: (0, i, 0))
    return pl.pallas_call(
        body, name="dproj_build", grid=(T // TT,),
        in_specs=[tok(AW), d4, d16] * 3 + [tok(AW), tok(1)],
        out_specs=tok(NCOL // 2),
        out_shape=jax.ShapeDtypeStruct((T, NCOL // 2), BF16),
        scratch_shapes=[pltpu.VMEM((4, TT, 128), F32)] * 3,
        compiler_params=_cp(("parallel",)),
    )(*dq, *dk, *dv, dag, pos)


def _bwd_x(dproj_a, dproj_h, x, dx2, mixw, w_full, rin, rinb, small4, small6, pout_own, pout_rem):
    TT = 512
    NT = T // TT

    def body(dpa_r, dph_r, x_r, dx2_r, mw_r, w_r, rin_r, rinb_r, s4_r, s6_r, poo_r, por_r,
             gx_o, pin_o, pinr_o, sall_o, fin_o, fout_o, sbuf, v_own, v_rem, vo_own, vo_rem, sin, sout, got_in,
             got_out, send_sems, recv_sems, loc_sems, share_send, share_recv, fin_sems):
        i = pl.program_id(0)
        loc, rem = _chip_copies(_w_in_piece, rin_r, rinb_r, pin_o, pinr_o, send_sems, recv_sems, loc_sems.at[0])

        @pl.when(i == 0)
        def _():
            sbuf[...] = jnp.zeros_like(sbuf)
            for cp in loc + rem:
                cp.start()

        dhn = _mm_nt(dpa_r[...], w_r[:, 0:NCOL // 2]) + _mm_nt(dph_r[...], w_r[:, NCOL // 2:NCOL])
        xv = x_r[...]
        r = lax.rsqrt(jnp.mean(xv * xv, axis=-1, keepdims=True) + EPS)
        dxw = dhn * mw_r[...]
        gx_o[...] = dx2_r[...] + r * dxw - xv * ((r * r * r) * jnp.mean(dxw * xv, axis=-1, keepdims=True))
        sbuf[16:17, :] += jnp.sum(dhn * (xv * r), axis=0, keepdims=True)

        @pl.when(i == NT - 1)
        def _():
            sbuf[0:8, :] = s4_r[...]
            sbuf[8:16, :] = s6_r[...]
            sloc, srem = _small_copies(sbuf, sall_o, send_sems, recv_sems, loc_sems.at[1])
            for cp in sloc + srem:
                cp.start()
            for cp in rem + srem:
                cp.wait_recv()
            for cp in rem + srem:
                cp.wait_send()
            for cp in loc + sloc:
                cp.wait()
            mx, my, c = lax.axis_index("x"), lax.axis_index("y"), lax.axis_index("c")
            loads = [pltpu.make_async_copy(pin_o, v_own, fin_sems.at[0]),
                     pltpu.make_async_copy(pinr_o, v_rem, fin_sems.at[1]),
                     pltpu.make_async_copy(poo_r, vo_own, fin_sems.at[2]),
                     pltpu.make_async_copy(por_r, vo_rem, fin_sems.at[3])]
            for cp in loads:
                cp.start()
            for cp in loads:
                cp.wait()
            sout[...] = ((vo_own[...] + vo_rem[0].astype(F32)) + vo_rem[1].astype(F32)) + vo_rem[2].astype(F32)
            sin[...] = ((v_own[...] + v_rem[0].astype(F32)) + v_rem[1].astype(F32)) + v_rem[2].astype(F32)
            swap = [pltpu.make_async_remote_copy(src_ref=sin, dst_ref=got_in, send_sem=share_send.at[0],
                                                 recv_sem=share_recv.at[0], device_id=(mx, my, 1 - c),
                                                 device_id_type=MESH),
                    pltpu.make_async_remote_copy(src_ref=sout, dst_ref=got_out, send_sem=share_send.at[1],
                                                 recv_sem=share_recv.at[1], device_id=(mx, my, 1 - c),
                                                 device_id_type=MESH)]
            for cp in swap:
                cp.start()
            mine = [pltpu.make_async_copy(sin, fin_o.at[c], fin_sems.at[0]),
                    pltpu.make_async_copy(sout, fout_o.at[c], fin_sems.at[1])]
            for cp in mine:
                cp.start()
            for cp in swap:
                cp.wait_recv()
            theirs = [pltpu.make_async_copy(got_in, fin_o.at[1 - c], fin_sems.at[2]),
                      pltpu.make_async_copy(got_out, fout_o.at[1 - c], fin_sems.at[3])]
            for cp in theirs:
                cp.start()
            for cp in swap:
                cp.wait_send()
            for cp in mine + theirs:
                cp.wait()

    tok = lambda w: pl.BlockSpec((TT, w), lambda i: (i, 0))
    const = lambda shape: pl.BlockSpec(shape, lambda i: (0,) * len(shape))
    hbm = pl.BlockSpec(memory_space=pltpu.HBM)
    return pl.pallas_call(
        body, name="bwd_x", grid=(NT,),
        in_specs=[tok(NCOL // 2), tok(NCOL // 2), tok(D), tok(D), const((1, D)), const((D, NCOL)), hbm, hbm,
                  const((8, D)), const((8, D)), hbm, hbm],
        out_specs=[tok(D), hbm, hbm, hbm, hbm, hbm],
        out_shape=[jax.ShapeDtypeStruct((T, D), F32),
                   jax.ShapeDtypeStruct((512, 1024), F32), jax.ShapeDtypeStruct((3, 512, 1024), BF16),
                   jax.ShapeDtypeStruct((8, 24, D), F32),
                   jax.ShapeDtypeStruct((2, 512, 1024), F32), jax.ShapeDtypeStruct((2, 128, D), F32)],
        scratch_shapes=[pltpu.VMEM((24, D), F32),
                        pltpu.VMEM((512, 1024), F32), pltpu.VMEM((3, 512, 1024), BF16),
                        pltpu.VMEM((128, D), F32), pltpu.VMEM((3, 128, D), BF16),
                        pltpu.VMEM((512, 1024), F32), pltpu.VMEM((128, D), F32),
                        pltpu.VMEM((512, 1024), F32), pltpu.VMEM((128, D), F32),
                        pltpu.SemaphoreType.DMA((10,)), pltpu.SemaphoreType.DMA((10,)), pltpu.SemaphoreType.DMA((2,)),
                        pltpu.SemaphoreType.DMA((2,)), pltpu.SemaphoreType.DMA((2,)), pltpu.SemaphoreType.DMA((4,))],
        compiler_params=_cp(("arbitrary",)),
    )(dproj_a, dproj_h, x, dx2, mixw, w_full, rin, rinb, small4, small6, pout_own, pout_rem)


def _grad_w_in(hn, dproj_a, dproj_h):
    TK = 1024
    NK = T // TK

    def body(hnt_r, dpa_r, dph_r, rin_o, rinb_o, acc, rbuf, obuf, obufb, send_sems, recv_sems, wb_sems):
        j = pl.program_id(0)
        kk = pl.program_id(1)
        x, y, c = lax.axis_index("x"), lax.axis_index("y"), lax.axis_index("c")
        mine = pl.ds(pl.multiple_of(c * 512, 512), 512)
        theirs = pl.ds(pl.multiple_of((1 - c) * 512, 512), 512)

        def send(jj):
            return pltpu.make_async_remote_copy(
                src_ref=acc.at[jj % 2, theirs, :], dst_ref=rbuf.at[jj], send_sem=send_sems.at[jj],
                recv_sem=recv_sems.at[jj], device_id=(x, y, 1 - c), device_id_type=MESH)

        def writeback(jj):
            cols = pl.ds(jj * 1024, 1024)
            return [pltpu.make_async_copy(obuf.at[jj % 2], rin_o.at[:, cols], wb_sems.at[jj % 2]),
                    pltpu.make_async_copy(obufb.at[jj % 2], rinb_o.at[:, cols], wb_sems.at[2 + jj % 2])]

        def wait_writeback(jj):
            for cp in writeback(jj):
                cp.wait()

        def finalize(jj):
            send(jj).wait_recv()
            red = acc[jj % 2, mine, :] + rbuf[jj]
            obuf[jj % 2] = red
            obufb[jj % 2] = red.astype(BF16)
            for cp in writeback(jj):
                cp.start()

        prod = _mm(hnt_r[...], jnp.where(j < 2, dpa_r[...], dph_r[...]))

        @pl.when(kk == 0)
        def _():
            for jj in (2, 3):
                @pl.when(j == jj)
                def _():
                    send(jj - 2).wait_send()
            acc[j % 2] = prod

        @pl.when(kk > 0)
        def _():
            acc[j % 2] += prod

        @pl.when(kk == NK - 1)
        def _():
            for jj in range(4):
                @pl.when(j == jj)
                def _():
                    send(jj).start()
                    if jj in (1, 2):
                        finalize(jj - 1)
                    if jj == 3:
                        wait_writeback(0)
                        finalize(2)
                        wait_writeback(1)
                        finalize(3)
                        wait_writeback(2)
                        wait_writeback(3)
                        send(2).wait_send()
                        send(3).wait_send()

    hbm = pl.BlockSpec(memory_space=pltpu.HBM)
    return pl.pallas_call(
        body, name="grad_w_in", grid=(4, NK),
        in_specs=[pl.BlockSpec((D, TK), lambda j, kk: (0, kk)),
                  pl.BlockSpec((TK, 1024), lambda j, kk: (jnp.where(j < 2, kk, NK - 1), jnp.minimum(j, 1))),
                  pl.BlockSpec((TK, 1024), lambda j, kk: (jnp.where(j < 2, 0, kk), jnp.maximum(j - 2, 0)))],
        out_specs=[hbm, hbm],
        out_shape=[jax.ShapeDtypeStruct((512, NCOL), F32), jax.ShapeDtypeStruct((512, NCOL), BF16)],
        scratch_shapes=[pltpu.VMEM((2, D, 1024), F32), pltpu.VMEM((4, 512, 1024), F32), pltpu.VMEM((2, 512, 1024), F32),
                        pltpu.VMEM((2, 512, 1024), BF16),
                        pltpu.SemaphoreType.DMA((4,)), pltpu.SemaphoreType.DMA((4,)), pltpu.SemaphoreType.DMA((4,))],
        compiler_params=_cp(("arbitrary", "arbitrary")),
    )(hn, dproj_a, dproj_h)


def _w_in_piece(ref, j):
    return ref.at[:, pl.ds(j * 1024, 1024)]


def _w_out_piece(ref, j):
    return ref.at[pl.ds(j * 128, 128), :]


def _chip_copies(piece, src_r, srcb_r, own_o, rem_o, send_sems, recv_sems, loc_sem):
    x, y, c = lax.axis_index("x"), lax.axis_index("y"), lax.axis_index("c")
    chips = [(1 - x, y), (x, 1 - y), (1 - x, 1 - y)]
    loc = [pltpu.make_async_copy(piece(src_r, 2 * x + y), own_o, loc_sem)]
    rem = [pltpu.make_async_remote_copy(
        src_ref=piece(srcb_r, 2 * px + py), dst_ref=rem_o.at[k], send_sem=send_sems.at[k],
        recv_sem=recv_sems.at[k], device_id=(px, py, c), device_id_type=MESH) for k, (px, py) in enumerate(chips)]
    return loc, rem


def _small_copies(small_r, sall_o, send_sems, recv_sems, loc_sem):
    x, y, c = lax.axis_index("x"), lax.axis_index("y"), lax.axis_index("c")
    me = 4 * x + 2 * y + c
    loc = [pltpu.make_async_copy(small_r, sall_o.at[me], loc_sem)]
    rem = []
    k = 3
    for fx in range(2):
        for fy in range(2):
            for fc in range(2):
                if fx or fy or fc:
                    peer = (1 - x if fx else x, 1 - y if fy else y, 1 - c if fc else c)
                    rem.append(pltpu.make_async_remote_copy(
                        src_ref=small_r, dst_ref=sall_o.at[me], send_sem=send_sems.at[k],
                        recv_sem=recv_sems.at[k], device_id=peer, device_id_type=MESH))
                    k += 1
    return loc, rem


def _adamw_math(w, g, m, v):
    m = B1 * m + (1.0 - B1) * g
    v = B2 * v + (1.0 - B2) * (g * g)
    m_hat = m / (1.0 - B1 ** STEP)
    v_hat = v / (1.0 - B2 ** STEP)
    delta = -LR * (m_hat / (jnp.sqrt(v_hat) + AEPS) + WD * w)
    return delta, m, v


def _adamw(big_in, big_out, sall, params):
    def body(*refs):
        wi, gi, mi, vi, wo, go, mo, vo, sall_r = refs[:9]
        ins = refs[9:24]
        di_o, mi_o, vi_o, do_o, mo_o, vo_o = refs[24:30]
        outs = refs[30:]
        d, mm, vv = _adamw_math(wi[...], gi[...], mi[...], vi[...])
        di_o[...] = d
        mi_o[...] = mm
        vi_o[...] = vv

        @pl.when(pl.program_id(0) == 0)
        def _():
            d, mm, vv = _adamw_math(wo[...], go[...], mo[...], vo[...])
            do_o[...] = d
            mo_o[...] = mm
            vo_o[...] = vv
            tot = sall_r[0]
            for dv in range(1, 8):
                tot = tot + sall_r[dv]
            grads = [tot[16:17, :], tot[1:2, 0:AW], tot[1:2, AW:], tot[8:10, 0:HW], tot[0:1, :]]
            outs[0][...] = tot[2:3, 0:1]
            for p in range(5):
                w_r, m_r, v_r = ins[3 * p:3 * p + 3]
                g = grads[p]
                d, mm, vv = _adamw_math(w_r[...], g, m_r[...], v_r[...])
                outs[1 + 4 * p][...] = g
                outs[2 + 4 * p][...] = d
                outs[3 + 4 * p][...] = mm
                outs[4 + 4 * p][...] = vv

    flat = [a for p in params for a in p]
    shapes = [jax.ShapeDtypeStruct((D, 1024), F32)] * 3 + [jax.ShapeDtypeStruct((256, D), F32)] * 3
    shapes += [jax.ShapeDtypeStruct((1, 1), F32)]
    for p in params:
        shapes += [jax.ShapeDtypeStruct(p[0].shape, F32)] * 4
    vm = pl.BlockSpec(memory_space=pltpu.VMEM)
    rows = pl.BlockSpec((256, 1024), lambda i: (i, 0))
    whole = pl.BlockSpec((256, D), lambda i: (0, 0))
    return pl.pallas_call(
        body, name="adamw", grid=(4,),
        in_specs=[rows] * 4 + [whole] * 4 + [vm] * 16, out_specs=[rows] * 3 + [whole] * 3 + [vm] * 21,
        out_shape=shapes,
        compiler_params=_cp(("arbitrary",)),
    )(*big_in, *big_out, sall, *flat)


def kernel(x, positions, w_in, w_out, mix_norm_w, attn_out_norm_w, hgrn_out_norm_w, hgrn_lb_raw, final_norm_w, loss_target, m_w_in, m_w_out, m_mix_norm_w, m_attn_out_norm_w, m_hgrn_out_norm_w, m_hgrn_lb_raw, m_final_norm_w, v_w_in, v_w_out, v_mix_norm_w, v_attn_out_norm_w, v_hgrn_out_norm_w, v_hgrn_lb_raw, v_final_norm_w):
    xs = x.reshape(T, D)
    tgt = loss_target.reshape(T, D)
    pos = positions.reshape(T, 1)
    fnw = final_norm_w.reshape(1, D)

    ti = np.arange(TH)
    tri_np = ((ti[:, None] // CHUNK == ti[None, :] // CHUNK) & (ti[None, :] <= ti[:, None])).astype(np.float32)
    tri = jnp.asarray(tri_np, BF16)
    trit = jnp.asarray(tri_np.T, BF16)
    hi_ = np.arange(AW) // HEAD
    gmat = jnp.asarray((hi_[:256, None] == hi_[None, :256]).astype(np.float32) / HEAD, BF16)
    emat_np = (np.arange(128)[:, None] == hi_[None, :]).astype(np.float32)
    sel_np = (8 + hi_[:, None] == np.arange(128)[None, :]).astype(np.float32)
    emat = jnp.asarray(emat_np, BF16)
    selmat = jnp.asarray(sel_np, BF16)

    jm_arr = (2 * lax.axis_index("x") + lax.axis_index("y")).astype(jnp.int32).reshape(1)
    (hn, q1, k1, v1, q4, k4, v4, q16, k16, v16, ag, hq, hf, hi, hg, w_full, wout4) = _fwd_in(
        xs, pos, mix_norm_w, w_in.reshape(D, 1024), w_out.reshape(256, D), jm_arr)
    wout_full = wout4.reshape(D, D)
    flat = lambda a: a.reshape(T, AW)
    o1, l1 = _attn_fwd(q1, k1, v1, T // BLK, "attn_fwd_d1")
    o4, l4 = _attn_fwd(flat(q4), flat(k4), flat(v4), T // 4 // BLK, "attn_fwd_d4")
    o16, l16 = _attn_fwd(flat(q16), flat(k16), flat(v16), T // 16 // BLK, "attn_fwd_d16")
    rec, sall = _hgrn_fwd(hq, hf, hi, hgrn_lb_raw, tri)

    (dx2, do1, do4, do16, st1, st4, st16, drec, dag, dhg, rout, routb, small4) = _fwd_out(
        o1, o4.reshape(4, T // 4, AW), o16.reshape(16, T // 16, AW),
        l1, l4.reshape(4, T // 4, 128), l16.reshape(16, T // 16, 128),
        rec, ag, hg, xs, tgt, attn_out_norm_w, hgrn_out_norm_w, fnw, wout_full, gmat, emat, selmat)

    fst = lambda a: a.reshape(T, 128)
    dq1, dk1, dv1 = _attn_bwd(q1, k1, v1, do1, st1, T // BLK, "attn_bwd_d1")
    dq4, dk4, dv4 = _attn_bwd(flat(q4), flat(k4), flat(v4), flat(do4), fst(st4), T // 4 // BLK, "attn_bwd_d4")
    dq16, dk16, dv16 = _attn_bwd(flat(q16), flat(k16), flat(v16), flat(do16), fst(st16), T // 16 // BLK,
                                 "attn_bwd_d16")
    dproj_h, small6, pout_own, pout_rem = _hgrn_bwd(hq, hf, hi, hgrn_lb_raw, tri, trit, drec, sall, dhg,
                                                    rout, routb)

    r4 = lambda a: a.reshape(4, T // 4, AW)
    r16 = lambda a: a.reshape(16, T // 16, AW)
    dproj_a = _dproj_build((dq1, r4(dq4), r16(dq16)), (dk1, r4(dk4), r16(dk16)), (dv1, r4(dv4), r16(dv16)),
                           dag, pos)
    rin, rinb = _grad_w_in(hn, dproj_a, dproj_h)
    gx, _, _, small_all, fin, fout = _bwd_x(dproj_a, dproj_h, xs, dx2, mix_norm_w, w_full, rin, rinb,
                                            small4, small6, pout_own, pout_rem)
    g_w_in = fin.reshape(D, 1024)
    g_w_out = fout.reshape(256, D)

    params = [(mix_norm_w, m_mix_norm_w, v_mix_norm_w),
              (attn_out_norm_w, m_attn_out_norm_w, v_attn_out_norm_w),
              (hgrn_out_norm_w, m_hgrn_out_norm_w, v_hgrn_out_norm_w),
              (hgrn_lb_raw, m_hgrn_lb_raw, v_hgrn_lb_raw),
              (fnw, m_final_norm_w.reshape(1, D), v_final_norm_w.reshape(1, D))]
    d_in, nm_in, nv_in, d_out, nm_out, nv_out, *so = _adamw(
        (w_in.reshape(D, 1024), g_w_in, m_w_in.reshape(D, 1024), v_w_in.reshape(D, 1024)),
        (w_out.reshape(256, D), g_w_out, m_w_out.reshape(256, D), v_w_out.reshape(256, D)), small_all, params)
    loss = so[0].reshape(())
    g_s = [so[1 + 4 * p] for p in range(5)]
    d_s = [so[2 + 4 * p] for p in range(5)]
    m_s = [so[3 + 4 * p] for p in range(5)]
    v_s = [so[4 + 4 * p] for p in range(5)]
    for lst in (g_s, d_s, m_s, v_s):
        lst[4] = lst[4].reshape(D)

    return (loss, gx.reshape(1, T, D),
            g_w_in.reshape(1, D, 1024), g_w_out.reshape(1, 256, D), *g_s,
            d_in.reshape(1, D, 1024), d_out.reshape(1, 256, D), *d_s,
            nm_in.reshape(1, D, 1024), nm_out.reshape(1, 256, D), *m_s,
            nv_in.reshape(1, D, 1024), nv_out.reshape(1, 256, D), *v_s)
```

```python
import functools

import numpy as np
import jax
import jax.numpy as jnp
from jax import lax
from jax.experimental import pallas as pl
from jax.experimental.pallas import tpu as pltpu

F32 = jnp.float32
BF16 = jnp.bfloat16

T = 4096
D = 1024
AW = 512
HW = 512
NCOL = 4096
HEAD = 64
BLK = 128
CHUNK = 64
EPS = 1e-6
SCALE = HEAD ** -0.5
NEG = -1e30
ROPE_THETA = 500000.0
INV_FREQ = [float(v) for v in
            (np.float32(ROPE_THETA) ** (-(np.arange(8, dtype=np.float32)) * np.float32(0.125)))]
LR, B1, B2, AEPS, WD, STEP = 0.001, 0.9, 0.999, 1e-08, 0.01, 10
VMEM_LIMIT = 56 * 1024 * 1024
MESH = pl.DeviceIdType.MESH


def _cp(sem=None, **kw):
    return pltpu.CompilerParams(dimension_semantics=sem, vmem_limit_bytes=VMEM_LIMIT, **kw)


def _mm(a, b):
    return jnp.dot(a, b, preferred_element_type=F32)


def _mm_nt(a, b):
    return lax.dot_general(a, b, (((1,), (1,)), ((), ())), preferred_element_type=F32)


def _mm_tn(a, b):
    return lax.dot_general(a, b, (((0,), (0,)), ((), ())), preferred_element_type=F32)


def _split3(x):
    h = x.astype(BF16)
    r = x - h.astype(F32)
    m = r.astype(BF16)
    l = (r - m.astype(F32)).astype(BF16)
    return h, m, l


def _mm_exact_l(mat_bf, x):
    h, m, l = _split3(x)
    return _mm(mat_bf, h) + _mm(mat_bf, m) + _mm(mat_bf, l)


def _mm_exact_r(x, mat_bf):
    h = x.astype(BF16)
    l = (x - h.astype(F32)).astype(BF16)
    return _mm(h, mat_bf) + _mm(l, mat_bf)


def _sigmoid(x):
    return 0.5 * jnp.tanh(0.5 * x) + 0.5


def _rope_tables(pos):
    lane = lax.broadcasted_iota(jnp.int32, (1, 128), 1)
    jl = lane & 63
    fi = jl & 7
    inv = jnp.zeros((1, 128), F32)
    for kk in range(8):
        inv = jnp.where(fi == kk, INV_FREQ[kk], inv)
    ang = pos.astype(F32) * inv
    c = jnp.cos(ang)
    s = jnp.sin(ang)
    cosf = jnp.where(jl < 16, c, 1.0)
    s1 = jnp.where(jl < 8, -s, 0.0)
    s2 = jnp.where((jl >= 8) & (jl < 16), s, 0.0)
    return cosf, s1, s2


def _rope(t, cosf, s1, s2):
    parts = []
    for ci in range(t.shape[1] // 128):
        tc = t[:, ci * 128:(ci + 1) * 128]
        parts.append(tc * cosf + pltpu.roll(tc, 120, 1) * s1 + pltpu.roll(tc, 8, 1) * s2)
    return jnp.concatenate(parts, axis=1)


def _rope_bwd(g, cosf, s1, s2):
    parts = []
    for ci in range(g.shape[1] // 128):
        gc = g[:, ci * 128:(ci + 1) * 128]
        parts.append(gc * cosf + pltpu.roll(gc * s1, 8, 1) + pltpu.roll(gc * s2, 120, 1))
    return jnp.concatenate(parts, axis=1)


def _perm_store(val, scr, scr2, o1, o4, o16, dt):
    n = val.shape[0]
    q = n // 4
    o1[...] = val.astype(dt)
    for ci in range(val.shape[1] // 128):
        cs = slice(ci * 128, (ci + 1) * 128)
        scr[ci] = val[:, cs]
        for r4 in range(4):
            part = scr[ci, pl.ds(r4, q, stride=4), :]
            o4[r4, :, cs] = part.astype(dt)
            scr2[ci, r4 * q:(r4 + 1) * q, :] = part
        for r4 in range(4):
            for b in range(4):
                o16[r4 + 4 * b, :, cs] = scr2[ci, pl.ds(r4 * q + b, q // 4, stride=4), :].astype(dt)


def _unperm_load(r4, r16, scr_a, scr_b, scr_c):
    n = scr_a.shape[1]
    q = n // 4
    nc = r4.shape[-1] // 128
    for ci in range(nc):
        cs = slice(ci * 128, (ci + 1) * 128)
        for rr in range(4):
            scr_a[ci, pl.ds(rr, q, stride=4), :] = r4[rr, :, cs].astype(F32)
        for rr in range(4):
            for b in range(4):
                scr_c[ci, pl.ds(rr * q + b, q // 4, stride=4), :] = r16[rr + 4 * b, :, cs].astype(F32)
        for rr in range(4):
            scr_b[ci, pl.ds(rr, q, stride=4), :] = scr_c[ci, rr * q:(rr + 1) * q, :]
    return (jnp.concatenate([scr_a[ci] for ci in range(nc)], axis=1),
            jnp.concatenate([scr_b[ci] for ci in range(nc)], axis=1))


def _fwd_in(x, pos, mixw, w_in, w_out, jm_arr):
    TT = 512
    NT = T // TT

    def body(jm_ref, x_ref, pos_ref, mw_ref, win_ref, wout_ref,
             hnt_ref, q1, k1, v1, q4, k4, v4, q16, k16, v16, ag, hq, hf, hi, hg, wfull_o, woutfull_o,
             wbuf, wobuf, hn_all, scr, scr2, stage, send_sems, recv_sems, loc_sems):
        s = pl.program_id(0)
        i = pl.program_id(1)
        mx, my, c = lax.axis_index("x"), lax.axis_index("y"), lax.axis_index("c")
        me, sibling = (mx, my, c), (mx, my, 1 - c)
        chips = [(mx, 1 - my), (1 - mx, my), (1 - mx, 1 - my)]
        jm = 2 * mx + my
        rows_in = [pl.ds(pl.multiple_of(h * 512, 512), 512) for h in (c, 1 - c)]
        rows_out = [pl.ds(pl.multiple_of(h * 128, 128), 128) for h in (c, 1 - c)]

        def blk(k):
            return lax.bitwise_xor(jm, k + 1)

        def rc(n, ref, to):
            return pltpu.make_async_remote_copy(src_ref=ref, dst_ref=ref, send_sem=send_sems.at[n],
                                                recv_sem=recv_sems.at[n], device_id=to, device_id_type=MESH)

        halves = [pl.ds(0, 512), pl.ds(512, 512)]
        send_in = lambda k, h: rc(12 + 2 * k + h, wbuf.at[jm, rows_in[0], halves[h]], (*chips[k], c))
        got_in = lambda k, h: rc(12 + 2 * k + h, wbuf.at[blk(k), rows_in[0], halves[h]], me)
        relay = lambda h: rc(16 + h, wbuf.at[blk(h), rows_in[0], halves[h]], (*chips[1 - h], c))
        got_relay = lambda h: rc(16 + h, wbuf.at[blk(2), rows_in[0], halves[h]], me)
        send_out = lambda k: rc(3 + k, wobuf.at[jm, rows_out[0], :], (*chips[k], c))
        got_out = lambda k: rc(3 + k, wobuf.at[blk(k), rows_out[0], :], me)
        pass_in = lambda k: rc(6 + k, wbuf.at[blk(k), rows_in[0], :], sibling)
        pass_out = lambda k: rc(9 + k, wobuf.at[blk(k), rows_out[0], :], sibling)
        passed_in = lambda k: rc(6 + k, wbuf.at[blk(k), rows_in[1], :], me)
        passed_out = lambda k: rc(9 + k, wobuf.at[blk(k), rows_out[1], :], me)

        def keep(j, n):
            return pltpu.make_async_copy(wbuf.at[j], wfull_o.at[:, pl.ds(j * 1024, 1024)], loc_sems.at[n])

        @pl.when((s == 0) & (i == 0))
        def _():
            for p in range(5):
                src = win_ref.at[pl.ds(p * 256, 256), :] if p < 4 else wout_ref
                load = pltpu.make_async_copy(src, stage, loc_sems.at[4])
                load.start()
                load.wait()
                if p < 4:
                    wbuf[jm, p * 256:(p + 1) * 256, :] = stage[...].astype(BF16)
                else:
                    wobuf[jm] = stage[...].astype(BF16)
            for k in range(2):
                for h in range(2):
                    send_in(k, h).start()
            keep(jm, 0).start()

        def arrive(k):
            if k == 0:
                for kk in range(2):
                    for h in range(2):
                        got_in(kk, h).wait_recv()
                relay(0).start()
                relay(1).start()
            if k == 2:
                got_relay(0).wait_recv()
                got_relay(1).wait_recv()
            pass_in(k).start()
            passed_in(k).wait_recv()
            keep(blk(k), k + 1).start()
            if k == 2:
                for kk in range(3):
                    send_out(kk).start()

        for k in range(3):
            pl.when((s == k + 1) & (i == 0))(functools.partial(arrive, k))

        tile = pl.ds(pl.multiple_of(i * TT, TT), TT)

        @pl.when(s == 0)
        def _():
            xv = x_ref[...]
            r = lax.rsqrt(jnp.mean(xv * xv, axis=-1, keepdims=True) + EPS)
            hnf = (xv * r) * mw_ref[...]
            hn_all[tile, :] = hnf.astype(BF16)
            hnt_ref[...] = hnf.T.astype(BF16)

        def project(jj):
            hn = hn_all[tile, :]
            lo = _mm(hn, wbuf[jj, :, 0:512])
            hi_cols = _mm(hn, wbuf[jj, :, 512:1024])
            if jj == 0:
                cosf, s1, s2 = _rope_tables(pos_ref[...])
                _perm_store(_rope(lo, cosf, s1, s2), scr, scr2, q1, q4, q16, BF16)
                _perm_store(_rope(hi_cols, cosf, s1, s2), scr, scr2, k1, k4, k16, BF16)
            elif jj == 1:
                _perm_store(lo, scr, scr2, v1, v4, v16, BF16)
                ag[...] = hi_cols.astype(BF16)
            elif jj == 2:
                hq[...] = lo.astype(BF16)
                hf[...] = hi_cols.astype(BF16)
            else:
                hi[...] = lo.astype(BF16)
                hg[...] = hi_cols.astype(BF16)

        j = lax.bitwise_xor(jm, s)
        for jj in range(4):
            pl.when(j == jj)(functools.partial(project, jj))

        @pl.when((s == 3) & (i == NT - 1))
        def _():
            for k in range(3):
                got_out(k).wait_recv()
                pass_out(k).start()
            for k in range(3):
                passed_out(k).wait_recv()
            out = pltpu.make_async_copy(wobuf, woutfull_o, loc_sems.at[4])
            out.start()
            for h in range(2):
                relay(h).wait_send()
                for k in range(2):
                    send_in(k, h).wait_send()
            for k in range(3):
                send_out(k).wait_send()
                pass_in(k).wait_send()
                pass_out(k).wait_send()
            keep(jm, 0).wait()
            for k in range(3):
                keep(blk(k), k + 1).wait()
            out.wait()

    def at_stage_of(jb):
        def index(s, i, jm_ref):
            sa = lax.bitwise_xor(jm_ref[0], jb)
            return jnp.where(s < sa, 0, jnp.where(s == sa, i, NT - 1))
        return index

    tok = lambda w, jb: pl.BlockSpec((TT, w), lambda s, i, jm_ref: (at_stage_of(jb)(s, i, jm_ref), 0))
    d4 = lambda jb: pl.BlockSpec((4, TT // 4, AW), lambda s, i, jm_ref: (0, at_stage_of(jb)(s, i, jm_ref), 0))
    d16 = lambda jb: pl.BlockSpec((16, TT // 16, AW), lambda s, i, jm_ref: (0, at_stage_of(jb)(s, i, jm_ref), 0))
    hbm = pl.BlockSpec(memory_space=pltpu.HBM)
    sd = lambda shape, dt: jax.ShapeDtypeStruct(shape, dt)
    in_own_stage = lambda s, i: jnp.where(s == 0, i, NT - 1)
    grid_spec = pltpu.PrefetchScalarGridSpec(
        num_scalar_prefetch=1, grid=(4, NT),
        in_specs=[pl.BlockSpec((TT, D), lambda s, i, jm_ref: (in_own_stage(s, i), 0)),
                  pl.BlockSpec((TT, 1), lambda s, i, jm_ref: (i, 0)),
                  pl.BlockSpec((1, D), lambda s, i, jm_ref: (0, 0)), hbm, hbm],
        out_specs=[pl.BlockSpec((D, TT), lambda s, i, jm_ref: (0, in_own_stage(s, i))),
                   tok(AW, 0), tok(AW, 0), tok(AW, 1), d4(0), d4(0), d4(1), d16(0), d16(0), d16(1),
                   tok(AW, 1), tok(AW, 2), tok(AW, 2), tok(AW, 3), tok(AW, 3), hbm, hbm],
        scratch_shapes=[pltpu.VMEM((4, D, 1024), BF16), pltpu.VMEM((4, 256, D), BF16), pltpu.VMEM((T, D), BF16),
                        pltpu.VMEM((4, TT, 128), F32), pltpu.VMEM((4, TT, 128), F32), pltpu.VMEM((256, 1024), F32),
                        pltpu.SemaphoreType.DMA((18,)),
                        pltpu.SemaphoreType.DMA((18,)), pltpu.SemaphoreType.DMA((6,))])
    return pl.pallas_call(
        body, name="fwd_in", grid_spec=grid_spec,
        out_shape=[sd((D, T), BF16)] + [sd((T, AW), BF16)] * 3 + [sd((4, T // 4, AW), BF16)] * 3
        + [sd((16, T // 16, AW), BF16)] * 3
        + [sd((T, AW), BF16)] * 5 + [sd((D, NCOL), BF16), sd((4, 256, D), BF16)],
        compiler_params=_cp(("arbitrary", "arbitrary")),
    )(jm_arr, x, pos, mixw, w_in, w_out)


def _band_mask(key_axis, nkeys=2 * BLK):
    shape = (nkeys, 2 * BLK) if key_axis == 0 else (2 * BLK, nkeys)
    kj = lax.broadcasted_iota(jnp.int32, shape, key_axis)
    qi = lax.broadcasted_iota(jnp.int32, shape, 1 - key_axis) & (BLK - 1)
    return (kj >= qi) & (kj <= qi + BLK), kj, qi


def _stack_heads(t2, in_a):
    z = jnp.zeros_like(t2)
    return jnp.concatenate([jnp.where(in_a[0], t2, z), jnp.where(in_a[1], t2, z)], axis=0)


def _attn_fwd(q, k, v, nb, name):
    n = 8 if nb >= 8 else 4
    CH = n * BLK
    halo = nb > n

    def body(*refs):
        if halo:
            q_ref, k_ref, v_ref, kp_ref, vp_ref, o_ref, lse_ref = refs
        else:
            q_ref, k_ref, v_ref, o_ref, lse_ref = refs
        lane = lax.broadcasted_iota(jnp.int32, (1, 128), 1)
        in_a = [lane < HEAD, lane >= HEAD]
        band, kj, _ = _band_mask(1)
        thr0 = jnp.where((n * pl.program_id(0)) % nb == 0, BLK, 0) if halo else BLK
        mask0 = band & (kj >= thr0)
        mask_first = band & (kj >= BLK)
        for b in range(n):
            rs = slice(b * BLK, (b + 1) * BLK)
            stat = jnp.zeros((BLK, 128), F32)
            for hp in range(4):
                cs = slice(hp * 128, (hp + 1) * 128)
                q2s = _stack_heads(q_ref[rs, cs], in_a)
                if b == 0:
                    kprev = kp_ref[:, cs] if halo else k_ref[rs, cs]
                    vprev = vp_ref[:, cs] if halo else v_ref[rs, cs]
                    kk = jnp.concatenate([kprev, k_ref[rs, cs]], axis=0)
                    vv = jnp.concatenate([vprev, v_ref[rs, cs]], axis=0)
                    mask = mask0
                else:
                    kk = k_ref[(b - 1) * BLK:(b + 1) * BLK, cs]
                    vv = v_ref[(b - 1) * BLK:(b + 1) * BLK, cs]
                    mask = mask_first if b % nb == 0 else band
                s = jnp.where(mask, _mm_nt(q2s, kk) * SCALE, NEG)
                m = jnp.max(s, axis=-1, keepdims=True)
                p = jnp.exp(s - m)
                l = jnp.sum(p, axis=-1, keepdims=True)
                o = _mm(p.astype(BF16), vv) / l
                lse = m + jnp.log(l)
                o_ref[rs, cs] = jnp.where(in_a[0], o[:BLK], o[BLK:]).astype(BF16)
                stat = jnp.where(lane == 2 * hp, lse[:BLK], stat)
                stat = jnp.where(lane == 2 * hp + 1, lse[BLK:], stat)
            lse_ref[rs, :] = stat

    cur = pl.BlockSpec((CH, AW), lambda i: (i, 0))
    prev = pl.BlockSpec((BLK, AW), lambda i: (jnp.maximum(n * i - 1, 0), 0))
    return pl.pallas_call(
        body, name=name, grid=(T // CH,),
        in_specs=[cur, cur, cur] + ([prev, prev] if halo else []),
        out_specs=[cur, pl.BlockSpec((CH, 128), lambda i: (i, 0))],
        out_shape=[jax.ShapeDtypeStruct((T, AW), BF16), jax.ShapeDtypeStruct((T, 128), F32)],
        compiler_params=_cp(("parallel",)),
    )(*((q, k, v) + ((k, v) if halo else ())))


def _attn_bwd(q, k, v, do, st, nb, name):
    n = 8 if nb >= 8 else 4
    CH = n * BLK
    NBLK = T // BLK
    halo = nb > n

    def body(*refs):
        if halo:
            (q_ref, k_ref, v_ref, do_ref, st_ref, kp_ref, vp_ref, qn_ref, don_ref, stn_ref,
             dq_ref, dk_ref, dv_ref) = refs
        else:
            q_ref, k_ref, v_ref, do_ref, st_ref, dq_ref, dk_ref, dv_ref = refs
        i = pl.program_id(0)
        lane = lax.broadcasted_iota(jnp.int32, (1, 128), 1)
        in_a = [lane < HEAD, lane >= HEAD]
        band, kj, _ = _band_mask(0)
        thr0 = jnp.where((n * i) % nb == 0, BLK, 0) if halo else BLK
        mask0 = band & (kj >= thr0)
        mask_first = band & (kj >= BLK)

        def stat_rows(st_t, hp):
            lse_r = jnp.concatenate([st_t[2 * hp:2 * hp + 1, :], st_t[2 * hp + 1:2 * hp + 2, :]], axis=1)
            dl_r = jnp.concatenate([st_t[8 + 2 * hp:9 + 2 * hp, :], st_t[9 + 2 * hp:10 + 2 * hp, :]], axis=1)
            return lse_r, dl_r

        st_t = [st_ref[b * BLK:(b + 1) * BLK, :].T for b in range(n)]
        if halo:
            nxt_thr = jnp.where((n * i + n) % nb == 0, 2 * BLK, 0)
            _, kj1, qi1 = _band_mask(0, BLK)
            mask_next = kj1 >= qi1 + nxt_thr
            stn_t = stn_ref[...].T

        for hp in range(4):
            cs = slice(hp * 128, (hp + 1) * 128)
            kb = [k_ref[b * BLK:(b + 1) * BLK, cs] for b in range(n)]
            vb = [v_ref[b * BLK:(b + 1) * BLK, cs] for b in range(n)]
            dk_acc = [jnp.zeros((BLK, 128), F32) for _ in range(n)]
            dv_acc = [jnp.zeros((BLK, 128), F32) for _ in range(n)]
            for b in range(n):
                rs = slice(b * BLK, (b + 1) * BLK)
                q2s = _stack_heads(q_ref[rs, cs], in_a)
                do2s = _stack_heads(do_ref[rs, cs], in_a)
                if b == 0:
                    kprev = kp_ref[:, cs] if halo else kb[0]
                    vprev = vp_ref[:, cs] if halo else vb[0]
                    mask = mask0
                else:
                    kprev, vprev, mask = kb[b - 1], vb[b - 1], (mask_first if b % nb == 0 else band)
                kk = jnp.concatenate([kprev, kb[b]], axis=0)
                vv = jnp.concatenate([vprev, vb[b]], axis=0)
                lse_r, dl_r = stat_rows(st_t[b], hp)
                s_t = jnp.where(mask, _mm_nt(kk, q2s) * SCALE, NEG)
                p_t = jnp.exp(s_t - lse_r)
                ds_t = (p_t * (_mm_nt(vv, do2s) - dl_r)).astype(BF16)
                dkk = _mm(ds_t, q2s) * SCALE
                dvv = _mm(p_t.astype(BF16), do2s)
                dqs = _mm_tn(ds_t, kk) * SCALE
                dq_ref[rs, cs] = jnp.where(in_a[0], dqs[:BLK], dqs[BLK:]).astype(BF16)
                dk_acc[b] += dkk[BLK:]
                dv_acc[b] += dvv[BLK:]
                if b > 0:
                    dk_acc[b - 1] += dkk[:BLK]
                    dv_acc[b - 1] += dvv[:BLK]
            if halo:
                q2s = _stack_heads(qn_ref[:, cs], in_a)
                do2s = _stack_heads(don_ref[:, cs], in_a)
                lse_r, dl_r = stat_rows(stn_t, hp)
                s_t = jnp.where(mask_next, _mm_nt(kb[n - 1], q2s) * SCALE, NEG)
                p_t = jnp.exp(s_t - lse_r)
                ds_t = (p_t * (_mm_nt(vb[n - 1], do2s) - dl_r)).astype(BF16)
                dk_acc[n - 1] += _mm(ds_t, q2s) * SCALE
                dv_acc[n - 1] += _mm(p_t.astype(BF16), do2s)
            for b in range(n):
                dk_ref[b * BLK:(b + 1) * BLK, cs] = dk_acc[b].astype(BF16)
                dv_ref[b * BLK:(b + 1) * BLK, cs] = dv_acc[b].astype(BF16)

    cur = pl.BlockSpec((CH, AW), lambda i: (i, 0))
    cur_st = pl.BlockSpec((CH, 128), lambda i: (i, 0))
    prev = pl.BlockSpec((BLK, AW), lambda i: (jnp.maximum(n * i - 1, 0), 0))
    nxt = pl.BlockSpec((BLK, AW), lambda i: (jnp.minimum(n * i + n, NBLK - 1), 0))
    nxt_st = pl.BlockSpec((BLK, 128), lambda i: (jnp.minimum(n * i + n, NBLK - 1), 0))
    ins = [cur] * 4 + [cur_st] + ([prev, prev, nxt, nxt, nxt_st] if halo else [])
    args = (q, k, v, do, st) + ((k, v, q, do, st) if halo else ())
    return pl.pallas_call(
        body, name=name, grid=(T // CH,),
        in_specs=ins,
        out_specs=[cur] * 3,
        out_shape=[jax.ShapeDtypeStruct((T, AW), BF16)] * 3,
        compiler_params=_cp(("parallel",)),
    )(*args)


TH = 256
NCH = TH // CHUNK


def _hgrn_common(hq_ref, hf_ref, lbr_ref, tri_ref):
    r0 = lbr_ref[0:1, :]
    r1 = lbr_ref[1:2, :]
    mx = jnp.maximum(r0, r1)
    e0 = jnp.exp(r0 - mx)
    e1 = jnp.exp(r1 - mx)
    lb = e0 / (e0 + e1)
    hqv = hq_ref[...].astype(F32)
    sq = _sigmoid(hqv)
    qv = hqv * sq
    sf = _sigmoid(hf_ref[...].astype(F32))
    f = lb + (1.0 - lb) * sf
    kv = 1.0 - f
    g = jnp.log(f)
    cum = _mm_exact_l(tri_ref[...], g)
    lastb = jnp.concatenate(
        [jnp.broadcast_to(cum[c * CHUNK + CHUNK - 1:(c + 1) * CHUNK, :], (CHUNK, HW)) for c in range(NCH)], axis=0)
    ea = jnp.exp(cum)
    ena = jnp.exp(-cum)
    eend = jnp.exp(lastb - cum)
    return dict(lb=lb, hq=hqv, sq=sq, q=qv, sf=sf, f=f, k=kv, cum=cum, lastb=lastb, ea=ea, ena=ena, eend=eend,
                qd=qv * ea, ki=kv * ena, ke=kv * eend, dec=jnp.exp(lastb))


def _tri_mask(transposed=False):
    ti = lax.broadcasted_iota(jnp.int32, (TH, TH), 1 if transposed else 0)
    si = lax.broadcasted_iota(jnp.int32, (TH, TH), 0 if transposed else 1)
    return (si <= ti) & ((si // CHUNK) == (ti // CHUNK))


def _hgrn_fwd(hq, hf, hi, lbr, tri):
    def body(hq_ref, hf_ref, hi_ref, lbr_ref, tri_ref, rec_ref, sall_ref, st_scr):
        @pl.when(pl.program_id(0) == 0)
        def _():
            st_scr[...] = jnp.zeros_like(st_scr)

        w = _hgrn_common(hq_ref, hf_ref, lbr_ref, tri_ref)
        qd, ki, ke = w["qd"].astype(BF16), w["ki"].astype(BF16), w["ke"].astype(BF16)
        dec = w["dec"]
        vb = hi_ref[...]
        causal = _tri_mask()
        for h in range(4):
            cs = slice(h * 128, (h + 1) * 128)
            att = jnp.where(causal, _mm_nt(qd[:, cs], ki[:, cs]), 0.0)
            o_intra = _mm(att.astype(BF16), vb[:, cs])
            for c in range(NCH):
                rs = slice(c * CHUNK, (c + 1) * CHUNK)
                st = st_scr[:, cs]
                sall_ref[c, :, cs] = st
                rec_ref[rs, cs] = (o_intra[rs] + _mm_nt(qd[rs, cs], st.astype(BF16))).astype(BF16)
                st_scr[:, cs] = dec[c * CHUNK:c * CHUNK + 1, cs] * st + _mm_tn(vb[rs, cs], ke[rs, cs])

    tok = pl.BlockSpec((TH, HW), lambda i: (i, 0))
    return pl.pallas_call(
        body, name="hgrn_fwd", grid=(T // TH,),
        in_specs=[tok, tok, tok, pl.BlockSpec((2, HW), lambda i: (0, 0)), pl.BlockSpec((TH, TH), lambda i: (0, 0))],
        out_specs=[tok, pl.BlockSpec((NCH, 128, HW), lambda i: (i, 0, 0))],
        out_shape=[jax.ShapeDtypeStruct((T, HW), BF16), jax.ShapeDtypeStruct((T // CHUNK, 128, HW), F32)],
        scratch_shapes=[pltpu.VMEM((128, HW), F32)],
        compiler_params=_cp(("arbitrary",)),
    )(hq, hf, hi, lbr, tri)


def _hgrn_bwd(hq, hf, hi, lbr, tri, trit, drec, sall, dhg, rout, routb):
    NT = T // TH

    def body(hq_ref, hf_ref, hi_ref, lbr_ref, tri_ref, trit_ref, do_ref, sall_ref, dhg_ref, rout_r, routb_r,
             dph_ref, small_ref, pout_o, poutr_o,
             dst_scr, dlb_scr, dqd_scr, dki_scr, dke_scr, dlast_scr, send_sems, recv_sems, loc_sems):
        step = pl.program_id(0)
        loc, rem = _chip_copies(_w_out_piece, rout_r, routb_r, pout_o, poutr_o, send_sems, recv_sems,
                                loc_sems.at[0])

        @pl.when(step == 0)
        def _():
            dst_scr[...] = jnp.zeros_like(dst_scr)
            dlb_scr[...] = jnp.zeros_like(dlb_scr)
            for cp in loc + rem:
                cp.start()

        w = _hgrn_common(hq_ref, hf_ref, lbr_ref, tri_ref)
        qd, ki, ke = w["qd"].astype(BF16), w["ki"].astype(BF16), w["ke"].astype(BF16)
        dec = w["dec"]
        vb = hi_ref[...]
        dob = do_ref[...].astype(BF16)
        causal = _tri_mask()
        causal_t = _tri_mask(transposed=True)
        for h in range(4):
            cs = slice(h * 128, (h + 1) * 128)
            att_t = jnp.where(causal_t, _mm_nt(ki[:, cs], qd[:, cs]), 0.0).astype(BF16)
            datt_t = jnp.where(causal_t, _mm_nt(vb[:, cs], dob[:, cs]), 0.0).astype(BF16)
            datt = jnp.where(causal, _mm_nt(dob[:, cs], vb[:, cs]), 0.0).astype(BF16)
            dv_intra = _mm(att_t, dob[:, cs])
            dqd_intra = _mm(datt, ki[:, cs])
            dki_scr[:, cs] = _mm(datt_t, qd[:, cs])
            for c in reversed(range(NCH)):
                rs = slice(c * CHUNK, (c + 1) * CHUNK)
                dec_c = dec[c * CHUNK:c * CHUNK + 1, :]
                st = sall_ref[c, :, cs]
                dst = dst_scr[:, cs]
                dstb = dst.astype(BF16)
                dph_ref[rs, 2 * HW + h * 128:2 * HW + (h + 1) * 128] = (
                    dv_intra[rs] + _mm_nt(ke[rs, cs], dstb)).astype(BF16)
                dqd_scr[rs, cs] = dqd_intra[rs] + _mm(dob[rs, cs], st.astype(BF16))
                dke_scr[rs, cs] = _mm(vb[rs, cs], dstb)
                ddec = jnp.sum(dst * st, axis=0, keepdims=True)
                dlast_scr[c:c + 1, cs] = ddec * dec_c[:, cs]
                dst_scr[:, cs] = dec_c[:, cs] * dst + _mm_tn(dob[rs, cs], qd[rs, cs])
        dqd, dki, dke = dqd_scr[...], dki_scr[...], dke_scr[...]
        dq = dqd * w["ea"]
        dk = dki * w["ena"] + dke * w["eend"]
        dcum = dqd * w["qd"] - dki * w["ki"] - dke * w["ke"]
        dkeke = dke * w["ke"]
        dlastb = jnp.concatenate(
            [jnp.broadcast_to(dlast_scr[c:c + 1, :] + jnp.sum(dkeke[c * CHUNK:(c + 1) * CHUNK], axis=0, keepdims=True),
                              (CHUNK, HW)) for c in range(NCH)], axis=0)
        dg = _mm_exact_l(trit_ref[...], dcum) + dlastb
        df = dg / w["f"] - dk
        lb, sf, sq = w["lb"], w["sf"], w["sq"]
        dph_ref[:, HW:2 * HW] = (df * (1.0 - lb) * sf * (1.0 - sf)).astype(BF16)
        dph_ref[:, 0:HW] = (dq * (sq * (1.0 + w["hq"] * (1.0 - sq)))).astype(BF16)
        dph_ref[:, 3 * HW:4 * HW] = dhg_ref[...]
        dlb_scr[...] += jnp.sum(df * (1.0 - sf), axis=0, keepdims=True)

        @pl.when(step == NT - 1)
        def _():
            gr = dlb_scr[...] * lb * (1.0 - lb)
            small_ref[...] = jnp.zeros_like(small_ref)
            small_ref[0:1, 0:HW] = gr
            small_ref[1:2, 0:HW] = -gr
            for cp in rem:
                cp.wait_recv()
            for cp in rem:
                cp.wait_send()
            for cp in loc:
                cp.wait()

    tok = pl.BlockSpec((TH, HW), lambda i: (NT - 1 - i, 0))
    const = lambda shape: pl.BlockSpec(shape, lambda i: (0,) * len(shape))
    hbm = pl.BlockSpec(memory_space=pltpu.HBM)
    return pl.pallas_call(
        body, name="hgrn_bwd", grid=(NT,),
        in_specs=[tok, tok, tok, const((2, HW)), const((TH, TH)), const((TH, TH)), tok,
                  pl.BlockSpec((NCH, 128, HW), lambda i: (NT - 1 - i, 0, 0)), tok, hbm, hbm],
        out_specs=[pl.BlockSpec((TH, NCOL // 2), lambda i: (NT - 1 - i, 0)), const((8, D)), hbm, hbm],
        out_shape=[jax.ShapeDtypeStruct((T, NCOL // 2), BF16), jax.ShapeDtypeStruct((8, D), F32),
                   jax.ShapeDtypeStruct((128, D), F32), jax.ShapeDtypeStruct((3, 128, D), BF16)],
        scratch_shapes=[pltpu.VMEM((128, HW), F32), pltpu.VMEM((1, HW), F32), pltpu.VMEM((TH, HW), F32),
                        pltpu.VMEM((TH, HW), F32), pltpu.VMEM((TH, HW), F32), pltpu.VMEM((8, HW), F32),
                        pltpu.SemaphoreType.DMA((3,)), pltpu.SemaphoreType.DMA((3,)), pltpu.SemaphoreType.DMA((1,))],
        compiler_params=_cp(("arbitrary",)),
    )(hq, hf, hi, lbr, tri, trit, drec, sall, dhg, rout, routb)


def _fwd_out(o1, o4, o16, l1, l4, l16, rec, ag, hg, x, tgt, anw, hnw, fnw, wout_full, gmat, emat, selmat):
    TT = 256

    def body(o1_r, o4_r, o16_r, l1_r, l4_r, l16_r, rec_r, ag_r, hg_r, x_r, tgt_r, anw_r, hnw_r, fnw_r, wo_r, g_r,
             e_r, sel_r, dx2_o, do1_o, do4_o, do16_o, st1_o, st4_o, st16_o, drec_o, dag_o, dhg_o,
             rout_o, routb_o, small_o, scr_a, scr_b, scr_c, gwout_o, rbuf, send_sems, recv_sems):
        @pl.when(pl.program_id(0) == 0)
        def _():
            gwout_o[...] = jnp.zeros_like(gwout_o)
            small_o[...] = jnp.zeros_like(small_o)

        def unperm(r4, r16):
            return _unperm_load(r4, r16, scr_a, scr_b, scr_c)

        def perm_out(val, p1, p4, p16, dt):
            _perm_store(val, scr_a, scr_b, p1, p4, p16, dt)

        o4u, o16u = unperm(o4_r, o16_r)
        l4c, l16c = unperm(l4_r, l16_r)
        l1c = l1_r[...]
        mxc = jnp.maximum(jnp.maximum(l1c, l4c), l16c)
        w1c, w4c, w16c = jnp.exp(l1c - mxc), jnp.exp(l4c - mxc), jnp.exp(l16c - mxc)
        denc = w1c + w4c + w16c
        lane = lax.broadcasted_iota(jnp.int32, (1, 128), 1)
        lse_c = jnp.where(lane < 8, mxc + jnp.log(denc), 0.0)
        em = e_r[...]
        wn1 = _mm_exact_r(w1c / denc, em)
        wn4 = _mm_exact_r(w4c / denc, em)
        o1v = o1_r[...].astype(F32)
        attn = wn1 * o1v + wn4 * o4u + (1.0 - wn1 - wn4) * o16u
        gm = g_r[...]

        def head_mean_a(t):
            return jnp.concatenate([_mm_exact_r(t[:, :256], gm), _mm_exact_r(t[:, 256:], gm)], axis=1)

        def head_mean_h(t):
            return jnp.concatenate(
                [jnp.broadcast_to(jnp.mean(t[:, h * 128:(h + 1) * 128], axis=-1, keepdims=True), (TT, 128))
                 for h in range(4)], axis=1)

        rs_a = lax.rsqrt(head_mean_a(attn * attn) + EPS)
        n_a = attn * rs_a
        agv = ag_r[...].astype(F32)
        sg_a = _sigmoid(agv)
        si_a = agv * sg_a
        anw_v = anw_r[...]
        y_a = (n_a * anw_v) * si_a
        recv = rec_r[...].astype(F32)
        rs_h = lax.rsqrt(head_mean_h(recv * recv) + EPS)
        n_h = recv * rs_h
        hgv = hg_r[...].astype(F32)
        sg_h = _sigmoid(hgv)
        si_h = hgv * sg_h
        hnw_v = hnw_r[...]
        y_h = (n_h * hnw_v) * si_h
        mixed = jnp.concatenate([y_a, y_h], axis=1).astype(BF16)
        xv = x_r[...]
        x2 = xv + _mm(mixed, wo_r[...])
        r2 = lax.rsqrt(jnp.mean(x2 * x2, axis=-1, keepdims=True) + EPS)
        fnw_v = fnw_r[...]
        xn = x2 * r2
        err = xn * fnw_v - tgt_r[...]
        small_o[2:3, :] += 0.5 * jnp.sum(jnp.mean(err * err, axis=-1, keepdims=True), axis=0, keepdims=True)
        dy = err * (1.0 / D)
        small_o[0:1, :] += jnp.sum(dy * xn, axis=0, keepdims=True)
        dyw = dy * fnw_v
        dx2 = r2 * dyw - x2 * ((r2 * r2 * r2) * jnp.mean(dyw * x2, axis=-1, keepdims=True))
        dx2_o[...] = dx2
        dx2b = dx2.astype(BF16)
        gwout_o[...] += _mm_tn(mixed, dx2b)
        dmix = _mm_nt(dx2b, wo_r[...])
        dm_a, dm_h = dmix[:, :AW], dmix[:, AW:]
        dag_o[...] = (dm_a * (n_a * anw_v) * (sg_a * (1.0 + agv * (1.0 - sg_a)))).astype(BF16)
        dn_a = dm_a * anw_v * si_a
        small_o[1:2, 0:AW] += jnp.sum(dm_a * n_a * si_a, axis=0, keepdims=True)
        dattn = rs_a * (dn_a - n_a * head_mean_a(dn_a * n_a))
        perm_out(dattn, do1_o, do4_o, do16_o, BF16)
        stats = lse_c + _mm_exact_r(dattn * attn, sel_r[...])
        perm_out(stats, st1_o, st4_o, st16_o, F32)
        dhg_o[...] = (dm_h * (n_h * hnw_v) * (sg_h * (1.0 + hgv * (1.0 - sg_h)))).astype(BF16)
        dn_h = dm_h * hnw_v * si_h
        small_o[1:2, AW:] += jnp.sum(dm_h * n_h * si_h, axis=0, keepdims=True)
        drec_o[...] = (rs_h * (dn_h - n_h * head_mean_h(dn_h * n_h))).astype(BF16)

        @pl.when(pl.program_id(0) == T // TT - 1)
        def _():
            x, y, c = lax.axis_index("x"), lax.axis_index("y"), lax.axis_index("c")
            cps = [pltpu.make_async_remote_copy(
                src_ref=gwout_o.at[pl.ds(pl.multiple_of(j * 256 + (1 - c) * 128, 128), 128), :], dst_ref=rbuf.at[j],
                send_sem=send_sems.at[j], recv_sem=recv_sems.at[j], device_id=(x, y, 1 - c), device_id_type=MESH)
                for j in range(4)]
            for cp in cps:
                cp.start()
            for j, cp in enumerate(cps):
                cp.wait_recv()
                red = gwout_o[pl.ds(pl.multiple_of(j * 256 + c * 128, 128), 128), :] + rbuf[j]
                rout_o[j * 128:(j + 1) * 128, :] = red
                routb_o[j * 128:(j + 1) * 128, :] = red.astype(BF16)
            for cp in cps:
                cp.wait_send()

    tok = lambda w: pl.BlockSpec((TT, w), lambda i: (i, 0))
    d4 = pl.BlockSpec((4, TT // 4, AW), lambda i: (0, i, 0))
    d16 = pl.BlockSpec((16, TT // 16, AW), lambda i: (0, i, 0))
    const = lambda shape: pl.BlockSpec(shape, lambda i: (0,) * len(shape))
    sd = lambda shape, dt: jax.ShapeDtypeStruct(shape, dt)
    c4 = pl.BlockSpec((4, TT // 4, 128), lambda i: (0, i, 0))
    c16 = pl.BlockSpec((16, TT // 16, 128), lambda i: (0, i, 0))
    p3 = lambda w, dt: [sd((T, w), dt), sd((4, T // 4, w), dt), sd((16, T // 16, w), dt)]
    return pl.pallas_call(
        body, name="fwd_out", grid=(T // TT,),
        in_specs=[tok(AW), d4, d16, tok(128), c4, c16, tok(AW), tok(AW), tok(AW), tok(D), tok(D),
                  const((1, AW)), const((1, HW)), const((1, D)), const((D, D)), const((256, 256)),
                  const((128, AW)), const((AW, 128))],
        out_specs=[tok(D)] + [tok(AW), d4, d16] + [tok(128), c4, c16] + [tok(AW)] * 3
        + [const((512, D)), const((512, D)), const((8, D))],
        out_shape=[sd((T, D), F32)] + p3(AW, BF16) + p3(128, F32)
        + [sd((T, AW), BF16), sd((T, AW), BF16), sd((T, AW), BF16), sd((512, D), F32), sd((512, D), BF16),
           sd((8, D), F32)],
        scratch_shapes=[pltpu.VMEM((4, TT, 128), F32)] * 3 + [pltpu.VMEM((D, D), F32),
                        pltpu.VMEM((4, 128, D), F32), pltpu.SemaphoreType.DMA((4,)), pltpu.SemaphoreType.DMA((4,))],
        compiler_params=_cp(("arbitrary",)),
    )(o1, o4, o16, l1, l4, l16, rec, ag, hg, x, tgt, anw, hnw, fnw, wout_full, gmat, emat, selmat)


def _dproj_build(dq, dk, dv, dag, pos):
    TT = 512

    def body(dq1, dq4, dq16, dk1, dk4, dk16, dv1, dv4, dv16, dag_r, pos_r, dproj_o, scr_a, scr_b, scr_c):
        def unperm_sum(r1, r4, r16):
            u4, u16 = _unperm_load(r4, r16, scr_a, scr_b, scr_c)
            return r1[...] + u4 + u16

        cosf, s1, s2 = _rope_tables(pos_r[...])
        dproj_o[:, 0:512] = _rope_bwd(unperm_sum(dq1, dq4, dq16), cosf, s1, s2).astype(BF16)
        dproj_o[:, 512:1024] = _rope_bwd(unperm_sum(dk1, dk4, dk16), cosf, s1, s2).astype(BF16)
        dproj_o[:, 1024:1536] = unperm_sum(dv1, dv4, dv16).astype(BF16)
        dproj_o[:, 1536:2048] = dag_r[...]

    tok = lambda w: pl.BlockSpec((TT, w), lambda i: (i, 0))
    d4 = pl.BlockSpec((4, TT // 4, AW), lambda i: (0, i, 0))
    d16 = pl.BlockSpec((16, TT // 16, AW), lambda i: (0, i, 0))
    return pl.pallas_call(
        body, name="dproj_build", grid=(T // TT,),
        in_specs=[tok(AW), d4, d16] * 3 + [tok(AW), tok(1)],
        out_specs=tok(NCOL // 2),
        out_shape=jax.ShapeDtypeStruct((T, NCOL // 2), BF16),
        scratch_shapes=[pltpu.VMEM((4, TT, 128), F32)] * 3,
        compiler_params=_cp(("parallel",)),
    )(*dq, *dk, *dv, dag, pos)


def _bwd_x(dproj_a, dproj_h, x, dx2, mixw, w_full, rin, rinb, small4, small6, pout_own, pout_rem):
    TT = 256
    NT = T // TT

    def body(dpa_r, dph_r, x_r, dx2_r, mw_r, w_r, rin_r, rinb_r, s4_r, s6_r, poo_r, por_r,
             gx_o, pin_o, pinr_o, sall_o, fin_o, fout_o, sbuf, v_own, v_rem, vo_own, vo_rem, sin, sout, got_in,
             got_out, send_sems, recv_sems, loc_sems, share_send, share_recv, fin_sems):
        i = pl.program_id(0)
        loc, rem = _chip_copies(_w_in_piece, rin_r, rinb_r, pin_o, pinr_o, send_sems, recv_sems, loc_sems.at[0])

        @pl.when(i == 0)
        def _():
            sbuf[...] = jnp.zeros_like(sbuf)
            for cp in loc + rem:
                cp.start()

        dhn = _mm_nt(dpa_r[...], w_r[:, 0:NCOL // 2]) + _mm_nt(dph_r[...], w_r[:, NCOL // 2:NCOL])
        xv = x_r[...]
        r = lax.rsqrt(jnp.mean(xv * xv, axis=-1, keepdims=True) + EPS)
        dxw = dhn * mw_r[...]
        gx_o[...] = dx2_r[...] + r * dxw - xv * ((r * r * r) * jnp.mean(dxw * xv, axis=-1, keepdims=True))
        sbuf[16:17, :] += jnp.sum(dhn * (xv * r), axis=0, keepdims=True)

        @pl.when(i == NT - 1)
        def _():
            sbuf[0:8, :] = s4_r[...]
            sbuf[8:16, :] = s6_r[...]
            sloc, srem = _small_copies(sbuf, sall_o, send_sems, recv_sems, loc_sems.at[1])
            for cp in sloc + srem:
                cp.start()
            for cp in rem + srem:
                cp.wait_recv()
            for cp in rem + srem:
                cp.wait_send()
            for cp in loc + sloc:
                cp.wait()
            mx, my, c = lax.axis_index("x"), lax.axis_index("y"), lax.axis_index("c")
            loads = [pltpu.make_async_copy(pin_o, v_own, fin_sems.at[0]),
                     pltpu.make_async_copy(pinr_o, v_rem, fin_sems.at[1]),
                     pltpu.make_async_copy(poo_r, vo_own, fin_sems.at[2]),
                     pltpu.make_async_copy(por_r, vo_rem, fin_sems.at[3])]
            for cp in loads:
                cp.start()
            for cp in loads:
                cp.wait()
            sout[...] = ((vo_own[...] + vo_rem[0].astype(F32)) + vo_rem[1].astype(F32)) + vo_rem[2].astype(F32)
            sin[...] = ((v_own[...] + v_rem[0].astype(F32)) + v_rem[1].astype(F32)) + v_rem[2].astype(F32)
            swap = [pltpu.make_async_remote_copy(src_ref=sin, dst_ref=got_in, send_sem=share_send.at[0],
                                                 recv_sem=share_recv.at[0], device_id=(mx, my, 1 - c),
                                                 device_id_type=MESH),
                    pltpu.make_async_remote_copy(src_ref=sout, dst_ref=got_out, send_sem=share_send.at[1],
                                                 recv_sem=share_recv.at[1], device_id=(mx, my, 1 - c),
                                                 device_id_type=MESH)]
            for cp in swap:
                cp.start()
            mine = [pltpu.make_async_copy(sin, fin_o.at[c], fin_sems.at[0]),
                    pltpu.make_async_copy(sout, fout_o.at[c], fin_sems.at[1])]
            for cp in mine:
                cp.start()
            for cp in swap:
                cp.wait_recv()
            theirs = [pltpu.make_async_copy(got_in, fin_o.at[1 - c], fin_sems.at[2]),
                      pltpu.make_async_copy(got_out, fout_o.at[1 - c], fin_sems.at[3])]
            for cp in theirs:
                cp.start()
            for cp in swap:
                cp.wait_send()
            for cp in mine + theirs:
                cp.wait()

    tok = lambda w: pl.BlockSpec((TT, w), lambda i: (i, 0))
    const = lambda shape: pl.BlockSpec(shape, lambda i: (0,) * len(shape))
    hbm = pl.BlockSpec(memory_space=pltpu.HBM)
    return pl.pallas_call(
        body, name="bwd_x", grid=(NT,),
        in_specs=[tok(NCOL // 2), tok(NCOL // 2), tok(D), tok(D), const((1, D)), const((D, NCOL)), hbm, hbm,
                  const((8, D)), const((8, D)), hbm, hbm],
        out_specs=[tok(D), hbm, hbm, hbm, hbm, hbm],
        out_shape=[jax.ShapeDtypeStruct((T, D), F32),
                   jax.ShapeDtypeStruct((512, 1024), F32), jax.ShapeDtypeStruct((3, 512, 1024), BF16),
                   jax.ShapeDtypeStruct((8, 24, D), F32),
                   jax.ShapeDtypeStruct((2, 512, 1024), F32), jax.ShapeDtypeStruct((2, 128, D), F32)],
        scratch_shapes=[pltpu.VMEM((24, D), F32),
                        pltpu.VMEM((512, 1024), F32), pltpu.VMEM((3, 512, 1024), BF16),
                        pltpu.VMEM((128, D), F32), pltpu.VMEM((3, 128, D), BF16),
                        pltpu.VMEM((512, 1024), F32), pltpu.VMEM((128, D), F32),
                        pltpu.VMEM((512, 1024), F32), pltpu.VMEM((128, D), F32),
                        pltpu.SemaphoreType.DMA((10,)), pltpu.SemaphoreType.DMA((10,)), pltpu.SemaphoreType.DMA((2,)),
                        pltpu.SemaphoreType.DMA((2,)), pltpu.SemaphoreType.DMA((2,)), pltpu.SemaphoreType.DMA((4,))],
        compiler_params=_cp(("arbitrary",)),
    )(dproj_a, dproj_h, x, dx2, mixw, w_full, rin, rinb, small4, small6, pout_own, pout_rem)


def _grad_w_in(hn, dproj_a, dproj_h):
    TK = 2048
    NK = T // TK

    def body(hnt_r, dpa_r, dph_r, rin_o, rinb_o, acc, rbuf, obuf, obufb, send_sems, recv_sems, wb_sems):
        j = pl.program_id(0)
        kk = pl.program_id(1)
        x, y, c = lax.axis_index("x"), lax.axis_index("y"), lax.axis_index("c")
        mine = pl.ds(pl.multiple_of(c * 512, 512), 512)
        theirs = pl.ds(pl.multiple_of((1 - c) * 512, 512), 512)

        def send(jj):
            return pltpu.make_async_remote_copy(
                src_ref=acc.at[jj % 2, theirs, :], dst_ref=rbuf.at[jj], send_sem=send_sems.at[jj],
                recv_sem=recv_sems.at[jj], device_id=(x, y, 1 - c), device_id_type=MESH)

        def writeback(jj):
            cols = pl.ds(jj * 1024, 1024)
            return [pltpu.make_async_copy(obuf.at[jj % 2], rin_o.at[:, cols], wb_sems.at[jj % 2]),
                    pltpu.make_async_copy(obufb.at[jj % 2], rinb_o.at[:, cols], wb_sems.at[2 + jj % 2])]

        def wait_writeback(jj):
            for cp in writeback(jj):
                cp.wait()

        def finalize(jj):
            send(jj).wait_recv()
            red = acc[jj % 2, mine, :] + rbuf[jj]
            obuf[jj % 2] = red
            obufb[jj % 2] = red.astype(BF16)
            for cp in writeback(jj):
                cp.start()

        prod = _mm(hnt_r[...], jnp.where(j < 2, dpa_r[...], dph_r[...]))

        @pl.when(kk == 0)
        def _():
            for jj in (2, 3):
                @pl.when(j == jj)
                def _():
                    send(jj - 2).wait_send()
            acc[j % 2] = prod

        @pl.when(kk > 0)
        def _():
            acc[j % 2] += prod

        @pl.when(kk == NK - 1)
        def _():
            for jj in range(4):
                @pl.when(j == jj)
                def _():
                    send(jj).start()
                    if jj in (1, 2):
                        finalize(jj - 1)
                    if jj == 3:
                        wait_writeback(0)
                        finalize(2)
                        wait_writeback(1)
                        finalize(3)
                        wait_writeback(2)
                        wait_writeback(3)
                        send(2).wait_send()
                        send(3).wait_send()

    hbm = pl.BlockSpec(memory_space=pltpu.HBM)
    return pl.pallas_call(
        body, name="grad_w_in", grid=(4, NK),
        in_specs=[pl.BlockSpec((D, TK), lambda j, kk: (0, kk)),
                  pl.BlockSpec((TK, 1024), lambda j, kk: (jnp.where(j < 2, kk, NK - 1), jnp.minimum(j, 1))),
                  pl.BlockSpec((TK, 1024), lambda j, kk: (jnp.where(j < 2, 0, kk), jnp.maximum(j - 2, 0)))],
        out_specs=[hbm, hbm],
        out_shape=[jax.ShapeDtypeStruct((512, NCOL), F32), jax.ShapeDtypeStruct((512, NCOL), BF16)],
        scratch_shapes=[pltpu.VMEM((2, D, 1024), F32), pltpu.VMEM((4, 512, 1024), F32), pltpu.VMEM((2, 512, 1024), F32),
                        pltpu.VMEM((2, 512, 1024), BF16),
                        pltpu.SemaphoreType.DMA((4,)), pltpu.SemaphoreType.DMA((4,)), pltpu.SemaphoreType.DMA((4,))],
        compiler_params=_cp(("arbitrary", "arbitrary")),
    )(hn, dproj_a, dproj_h)


def _w_in_piece(ref, j):
    return ref.at[:, pl.ds(j * 1024, 1024)]


def _w_out_piece(ref, j):
    return ref.at[pl.ds(j * 128, 128), :]


def _chip_copies(piece, src_r, srcb_r, own_o, rem_o, send_sems, recv_sems, loc_sem):
    x, y, c = lax.axis_index("x"), lax.axis_index("y"), lax.axis_index("c")
    chips = [(1 - x, y), (x, 1 - y), (1 - x, 1 - y)]
    loc = [pltpu.make_async_copy(piece(src_r, 2 * x + y), own_o, loc_sem)]
    rem = [pltpu.make_async_remote_copy(
        src_ref=piece(srcb_r, 2 * px + py), dst_ref=rem_o.at[k], send_sem=send_sems.at[k],
        recv_sem=recv_sems.at[k], device_id=(px, py, c), device_id_type=MESH) for k, (px, py) in enumerate(chips)]
    return loc, rem


def _small_copies(small_r, sall_o, send_sems, recv_sems, loc_sem):
    x, y, c = lax.axis_index("x"), lax.axis_index("y"), lax.axis_index("c")
    me = 4 * x + 2 * y + c
    loc = [pltpu.make_async_copy(small_r, sall_o.at[me], loc_sem)]
    rem = []
    k = 3
    for fx in range(2):
        for fy in range(2):
            for fc in range(2):
                if fx or fy or fc:
                    peer = (1 - x if fx else x, 1 - y if fy else y, 1 - c if fc else c)
                    rem.append(pltpu.make_async_remote_copy(
                        src_ref=small_r, dst_ref=sall_o.at[me], send_sem=send_sems.at[k],
                        recv_sem=recv_sems.at[k], device_id=peer, device_id_type=MESH))
                    k += 1
    return loc, rem


def _adamw_math(w, g, m, v):
    m = B1 * m + (1.0 - B1) * g
    v = B2 * v + (1.0 - B2) * (g * g)
    m_hat = m / (1.0 - B1 ** STEP)
    v_hat = v / (1.0 - B2 ** STEP)
    delta = -LR * (m_hat / (jnp.sqrt(v_hat) + AEPS) + WD * w)
    return delta, m, v


def _adamw(big_in, big_out, sall, params):
    def body(*refs):
        wi, gi, mi, vi, wo, go, mo, vo, sall_r = refs[:9]
        ins = refs[9:24]
        di_o, mi_o, vi_o, do_o, mo_o, vo_o = refs[24:30]
        outs = refs[30:]
        d, mm, vv = _adamw_math(wi[...], gi[...], mi[...], vi[...])
        di_o[...] = d
        mi_o[...] = mm
        vi_o[...] = vv

        @pl.when(pl.program_id(0) == 0)
        def _():
            d, mm, vv = _adamw_math(wo[...], go[...], mo[...], vo[...])
            do_o[...] = d
            mo_o[...] = mm
            vo_o[...] = vv
            tot = sall_r[0]
            for dv in range(1, 8):
                tot = tot + sall_r[dv]
            grads = [tot[16:17, :], tot[1:2, 0:AW], tot[1:2, AW:], tot[8:10, 0:HW], tot[0:1, :]]
            outs[0][...] = tot[2:3, 0:1]
            for p in range(5):
                w_r, m_r, v_r = ins[3 * p:3 * p + 3]
                g = grads[p]
                d, mm, vv = _adamw_math(w_r[...], g, m_r[...], v_r[...])
                outs[1 + 4 * p][...] = g
                outs[2 + 4 * p][...] = d
                outs[3 + 4 * p][...] = mm
                outs[4 + 4 * p][...] = vv

    flat = [a for p in params for a in p]
    shapes = [jax.ShapeDtypeStruct((D, 1024), F32)] * 3 + [jax.ShapeDtypeStruct((256, D), F32)] * 3
    shapes += [jax.ShapeDtypeStruct((1, 1), F32)]
    for p in params:
        shapes += [jax.ShapeDtypeStruct(p[0].shape, F32)] * 4
    vm = pl.BlockSpec(memory_space=pltpu.VMEM)
    rows = pl.BlockSpec((256, 1024), lambda i: (i, 0))
    whole = pl.BlockSpec((256, D), lambda i: (0, 0))
    return pl.pallas_call(
        body, name="adamw", grid=(4,),
        in_specs=[rows] * 4 + [whole] * 4 + [vm] * 16, out_specs=[rows] * 3 + [whole] * 3 + [vm] * 21,
        out_shape=shapes,
        compiler_params=_cp(("arbitrary",)),
    )(*big_in, *big_out, sall, *flat)


def kernel(x, positions, w_in, w_out, mix_norm_w, attn_out_norm_w, hgrn_out_norm_w, hgrn_lb_raw, final_norm_w, loss_target, m_w_in, m_w_out, m_mix_norm_w, m_attn_out_norm_w, m_hgrn_out_norm_w, m_hgrn_lb_raw, m_final_norm_w, v_w_in, v_w_out, v_mix_norm_w, v_attn_out_norm_w, v_hgrn_out_norm_w, v_hgrn_lb_raw, v_final_norm_w):
    xs = x.reshape(T, D)
    tgt = loss_target.reshape(T, D)
    pos = positions.reshape(T, 1)
    fnw = final_norm_w.reshape(1, D)

    ti = np.arange(TH)
    tri_np = ((ti[:, None] // CHUNK == ti[None, :] // CHUNK) & (ti[None, :] <= ti[:, None])).astype(np.float32)
    tri = jnp.asarray(tri_np, BF16)
    trit = jnp.asarray(tri_np.T, BF16)
    hi_ = np.arange(AW) // HEAD
    gmat = jnp.asarray((hi_[:256, None] == hi_[None, :256]).astype(np.float32) / HEAD, BF16)
    emat_np = (np.arange(128)[:, None] == hi_[None, :]).astype(np.float32)
    sel_np = (8 + hi_[:, None] == np.arange(128)[None, :]).astype(np.float32)
    emat = jnp.asarray(emat_np, BF16)
    selmat = jnp.asarray(sel_np, BF16)

    jm_arr = (2 * lax.axis_index("x") + lax.axis_index("y")).astype(jnp.int32).reshape(1)
    (hn, q1, k1, v1, q4, k4, v4, q16, k16, v16, ag, hq, hf, hi, hg, w_full, wout4) = _fwd_in(
        xs, pos, mix_norm_w, w_in.reshape(D, 1024), w_out.reshape(256, D), jm_arr)
    wout_full = wout4.reshape(D, D)
    flat = lambda a: a.reshape(T, AW)
    o1, l1 = _attn_fwd(q1, k1, v1, T // BLK, "attn_fwd_d1")
    o4, l4 = _attn_fwd(flat(q4), flat(k4), flat(v4), T // 4 // BLK, "attn_fwd_d4")
    o16, l16 = _attn_fwd(flat(q16), flat(k16), flat(v16), T // 16 // BLK, "attn_fwd_d16")
    rec, sall = _hgrn_fwd(hq, hf, hi, hgrn_lb_raw, tri)

    (dx2, do1, do4, do16, st1, st4, st16, drec, dag, dhg, rout, routb, small4) = _fwd_out(
        o1, o4.reshape(4, T // 4, AW), o16.reshape(16, T // 16, AW),
        l1, l4.reshape(4, T // 4, 128), l16.reshape(16, T // 16, 128),
        rec, ag, hg, xs, tgt, attn_out_norm_w, hgrn_out_norm_w, fnw, wout_full, gmat, emat, selmat)

    fst = lambda a: a.reshape(T, 128)
    dq1, dk1, dv1 = _attn_bwd(q1, k1, v1, do1, st1, T // BLK, "attn_bwd_d1")
    dq4, dk4, dv4 = _attn_bwd(flat(q4), flat(k4), flat(v4), flat(do4), fst(st4), T // 4 // BLK, "attn_bwd_d4")
    dq16, dk16, dv16 = _attn_bwd(flat(q16), flat(k16), flat(v16), flat(do16), fst(st16), T // 16 // BLK,
                                 "attn_bwd_d16")
    dproj_h, small6, pout_own, pout_rem = _hgrn_bwd(hq, hf, hi, hgrn_lb_raw, tri, trit, drec, sall, dhg,
                                                    rout, routb)

    r4 = lambda a: a.reshape(4, T // 4, AW)
    r16 = lambda a: a.reshape(16, T // 16, AW)
    dproj_a = _dproj_build((dq1, r4(dq4), r16(dq16)), (dk1, r4(dk4), r16(dk16)), (dv1, r4(dv4), r16(dv16)),
                           dag, pos)
    rin, rinb = _grad_w_in(hn, dproj_a, dproj_h)
    gx, _, _, small_all, fin, fout = _bwd_x(dproj_a, dproj_h, xs, dx2, mix_norm_w, w_full, rin, rinb,
                                            small4, small6, pout_own, pout_rem)
    g_w_in = fin.reshape(D, 1024)
    g_w_out = fout.reshape(256, D)

    params = [(mix_norm_w, m_mix_norm_w, v_mix_norm_w),
              (attn_out_norm_w, m_attn_out_norm_w, v_attn_out_norm_w),
              (hgrn_out_norm_w, m_hgrn_out_norm_w, v_hgrn_out_norm_w),
              (hgrn_lb_raw, m_hgrn_lb_raw, v_hgrn_lb_raw),
              (fnw, m_final_norm_w.reshape(1, D), v_final_norm_w.reshape(1, D))]
    d_in, nm_in, nv_in, d_out, nm_out, nv_out, *so = _adamw(
        (w_in.reshape(D, 1024), g_w_in, m_w_in.reshape(D, 1024), v_w_in.reshape(D, 1024)),
        (w_out.reshape(256, D), g_w_out, m_w_out.reshape(256, D), v_w_out.reshape(256, D)), small_all, params)
    loss = so[0].reshape(())
    g_s = [so[1 + 4 * p] for p in range(5)]
    d_s = [so[2 + 4 * p] for p in range(5)]
    m_s = [so[3 + 4 * p] for p in range(5)]
    v_s = [so[4 + 4 * p] for p in range(5)]
    for lst in (g_s, d_s, m_s, v_s):
        lst[4] = lst[4].reshape(D)

    return (loss, gx.reshape(1, T, D),
            g_w_in.reshape(1, D, 1024), g_w_out.reshape(1, 256, D), *g_s,
            d_in.reshape(1, D, 1024), d_out.reshape(1, 256, D), *d_s,
            nm_in.reshape(1, D, 1024), nm_out.reshape(1, 256, D), *m_s,
            nv_in.reshape(1, D, 1024), nv_out.reshape(1, 256, D), *v_s)
```

```python
import functools

import numpy as np
import jax
import jax.numpy as jnp
from jax import lax
from jax.experimental import pallas as pl
from jax.experimental.pallas import tpu as pltpu

F32 = jnp.float32
BF16 = jnp.bfloat16

T = 4096
D = 1024
AW = 512
HW = 512
NCOL = 4096
HEAD = 64
BLK = 128
ATTN_BLOCKS = 16
CHUNK = 64
EPS = 1e-6
SCALE = HEAD ** -0.5
NEG = -1e30
ROPE_THETA = 500000.0
INV_FREQ = [float(v) for v in
            (np.float32(ROPE_THETA) ** (-(np.arange(8, dtype=np.float32)) * np.float32(0.125)))]
LR, B1, B2, AEPS, WD, STEP = 0.001, 0.9, 0.999, 1e-08, 0.01, 10
VMEM_LIMIT = 56 * 1024 * 1024
MESH = pl.DeviceIdType.MESH


def _cp(sem=None, **kw):
    return pltpu.CompilerParams(dimension_semantics=sem, vmem_limit_bytes=VMEM_LIMIT, **kw)


def _mm(a, b):
    return jnp.dot(a, b, preferred_element_type=F32)


def _mm_nt(a, b):
    return lax.dot_general(a, b, (((1,), (1,)), ((), ())), preferred_element_type=F32)


def _mm_tn(a, b):
    return lax.dot_general(a, b, (((0,), (0,)), ((), ())), preferred_element_type=F32)


def _split3(x):
    h = x.astype(BF16)
    r = x - h.astype(F32)
    m = r.astype(BF16)
    l = (r - m.astype(F32)).astype(BF16)
    return h, m, l


def _mm_exact_l(mat_bf, x):
    h, m, l = _split3(x)
    return _mm(mat_bf, h) + _mm(mat_bf, m) + _mm(mat_bf, l)


def _mm_exact_r(x, mat_bf):
    h = x.astype(BF16)
    l = (x - h.astype(F32)).astype(BF16)
    return _mm(h, mat_bf) + _mm(l, mat_bf)


def _sigmoid(x):
    return 0.5 * jnp.tanh(0.5 * x) + 0.5


def _rope_tables(pos):
    lane = lax.broadcasted_iota(jnp.int32, (1, 128), 1)
    jl = lane & 63
    fi = jl & 7
    inv = jnp.zeros((1, 128), F32)
    for kk in range(8):
        inv = jnp.where(fi == kk, INV_FREQ[kk], inv)
    ang = pos.astype(F32) * inv
    c = jnp.cos(ang)
    s = jnp.sin(ang)
    cosf = jnp.where(jl < 16, c, 1.0)
    s1 = jnp.where(jl < 8, -s, 0.0)
    s2 = jnp.where((jl >= 8) & (jl < 16), s, 0.0)
    return cosf, s1, s2


def _rope(t, cosf, s1, s2):
    parts = []
    for ci in range(t.shape[1] // 128):
        tc = t[:, ci * 128:(ci + 1) * 128]
        parts.append(tc * cosf + pltpu.roll(tc, 120, 1) * s1 + pltpu.roll(tc, 8, 1) * s2)
    return jnp.concatenate(parts, axis=1)


def _rope_bwd(g, cosf, s1, s2):
    parts = []
    for ci in range(g.shape[1] // 128):
        gc = g[:, ci * 128:(ci + 1) * 128]
        parts.append(gc * cosf + pltpu.roll(gc * s1, 8, 1) + pltpu.roll(gc * s2, 120, 1))
    return jnp.concatenate(parts, axis=1)


def _perm_store(val, scr, scr2, o1, o4, o16, dt):
    n = val.shape[0]
    q = n // 4
    o1[...] = val.astype(dt)
    for ci in range(val.shape[1] // 128):
        cs = slice(ci * 128, (ci + 1) * 128)
        scr[ci] = val[:, cs]
        for r4 in range(4):
            part = scr[ci, pl.ds(r4, q, stride=4), :]
            o4[r4, :, cs] = part.astype(dt)
            scr2[ci, r4 * q:(r4 + 1) * q, :] = part
        for r4 in range(4):
            for b in range(4):
                o16[r4 + 4 * b, :, cs] = scr2[ci, pl.ds(r4 * q + b, q // 4, stride=4), :].astype(dt)


def _unperm_load(r4, r16, scr_a, scr_b, scr_c):
    n = scr_a.shape[1]
    q = n // 4
    nc = r4.shape[-1] // 128
    for ci in range(nc):
        cs = slice(ci * 128, (ci + 1) * 128)
        for rr in range(4):
            scr_a[ci, pl.ds(rr, q, stride=4), :] = r4[rr, :, cs].astype(F32)
        for rr in range(4):
            for b in range(4):
                scr_c[ci, pl.ds(rr * q + b, q // 4, stride=4), :] = r16[rr + 4 * b, :, cs].astype(F32)
        for rr in range(4):
            scr_b[ci, pl.ds(rr, q, stride=4), :] = scr_c[ci, rr * q:(rr + 1) * q, :]
    return (jnp.concatenate([scr_a[ci] for ci in range(nc)], axis=1),
            jnp.concatenate([scr_b[ci] for ci in range(nc)], axis=1))


def _fwd_in(x, pos, mixw, w_in, w_out, jm_arr):
    TT = 512
    NT = T // TT

    def body(jm_ref, x_ref, pos_ref, mw_ref, win_ref, wout_ref,
             hnt_ref, q1, k1, v1, q4, k4, v4, q16, k16, v16, ag, hq, hf, hi, hg, wfull_o, woutfull_o,
             wbuf, wobuf, hn_all, scr, scr2, stage, send_sems, recv_sems, loc_sems):
        s = pl.program_id(0)
        i = pl.program_id(1)
        mx, my, c = lax.axis_index("x"), lax.axis_index("y"), lax.axis_index("c")
        me, sibling = (mx, my, c), (mx, my, 1 - c)
        chips = [(mx, 1 - my), (1 - mx, my), (1 - mx, 1 - my)]
        jm = 2 * mx + my
        rows_in = [pl.ds(pl.multiple_of(h * 512, 512), 512) for h in (c, 1 - c)]
        rows_out = [pl.ds(pl.multiple_of(h * 128, 128), 128) for h in (c, 1 - c)]

        def blk(k):
            return lax.bitwise_xor(jm, k + 1)

        def rc(n, ref, to):
            return pltpu.make_async_remote_copy(src_ref=ref, dst_ref=ref, send_sem=send_sems.at[n],
                                                recv_sem=recv_sems.at[n], device_id=to, device_id_type=MESH)

        halves = [pl.ds(0, 512), pl.ds(512, 512)]
        send_in = lambda k, h: rc(12 + 2 * k + h, wbuf.at[jm, rows_in[0], halves[h]], (*chips[k], c))
        got_in = lambda k, h: rc(12 + 2 * k + h, wbuf.at[blk(k), rows_in[0], halves[h]], me)
        relay = lambda h: rc(16 + h, wbuf.at[blk(h), rows_in[0], halves[h]], (*chips[1 - h], c))
        got_relay = lambda h: rc(16 + h, wbuf.at[blk(2), rows_in[0], halves[h]], me)
        send_out = lambda k: rc(3 + k, wobuf.at[jm, rows_out[0], :], (*chips[k], c))
        got_out = lambda k: rc(3 + k, wobuf.at[blk(k), rows_out[0], :], me)
        pass_in = lambda k: rc(6 + k, wbuf.at[blk(k), rows_in[0], :], sibling)
        pass_out = lambda k: rc(9 + k, wobuf.at[blk(k), rows_out[0], :], sibling)
        passed_in = lambda k: rc(6 + k, wbuf.at[blk(k), rows_in[1], :], me)
        passed_out = lambda k: rc(9 + k, wobuf.at[blk(k), rows_out[1], :], me)

        def keep(j, n):
            return pltpu.make_async_copy(wbuf.at[j], wfull_o.at[:, pl.ds(j * 1024, 1024)], loc_sems.at[n])

        @pl.when((s == 0) & (i == 0))
        def _():
            for p in range(5):
                src = win_ref.at[pl.ds(p * 256, 256), :] if p < 4 else wout_ref
                load = pltpu.make_async_copy(src, stage, loc_sems.at[4])
                load.start()
                load.wait()
                if p < 4:
                    wbuf[jm, p * 256:(p + 1) * 256, :] = stage[...].astype(BF16)
                else:
                    wobuf[jm] = stage[...].astype(BF16)
            for k in range(2):
                for h in range(2):
                    send_in(k, h).start()
            keep(jm, 0).start()

        def arrive(k):
            if k == 0:
                for kk in range(2):
                    for h in range(2):
                        got_in(kk, h).wait_recv()
                relay(0).start()
                relay(1).start()
            if k == 2:
                got_relay(0).wait_recv()
                got_relay(1).wait_recv()
            pass_in(k).start()
            passed_in(k).wait_recv()
            keep(blk(k), k + 1).start()
            if k == 2:
                for kk in range(3):
                    send_out(kk).start()

        for k in range(3):
            pl.when((s == k + 1) & (i == 0))(functools.partial(arrive, k))

        tile = pl.ds(pl.multiple_of(i * TT, TT), TT)

        @pl.when(s == 0)
        def _():
            xv = x_ref[...]
            r = lax.rsqrt(jnp.mean(xv * xv, axis=-1, keepdims=True) + EPS)
            hnf = (xv * r) * mw_ref[...]
            hn_all[tile, :] = hnf.astype(BF16)
            hnt_ref[...] = hnf.T.astype(BF16)

        def project(jj):
            hn = hn_all[tile, :]
            lo = _mm(hn, wbuf[jj, :, 0:512])
            hi_cols = _mm(hn, wbuf[jj, :, 512:1024])
            if jj == 0:
                cosf, s1, s2 = _rope_tables(pos_ref[...])
                _perm_store(_rope(lo, cosf, s1, s2), scr, scr2, q1, q4, q16, BF16)
                _perm_store(_rope(hi_cols, cosf, s1, s2), scr, scr2, k1, k4, k16, BF16)
            elif jj == 1:
                _perm_store(lo, scr, scr2, v1, v4, v16, BF16)
                ag[...] = hi_cols.astype(BF16)
            elif jj == 2:
                hq[...] = lo.astype(BF16)
                hf[...] = hi_cols.astype(BF16)
            else:
                hi[...] = lo.astype(BF16)
                hg[...] = hi_cols.astype(BF16)

        j = lax.bitwise_xor(jm, s)
        for jj in range(4):
            pl.when(j == jj)(functools.partial(project, jj))

        @pl.when((s == 3) & (i == NT - 1))
        def _():
            for k in range(3):
                got_out(k).wait_recv()
                pass_out(k).start()
            for k in range(3):
                passed_out(k).wait_recv()
            out = pltpu.make_async_copy(wobuf, woutfull_o, loc_sems.at[4])
            out.start()
            for h in range(2):
                relay(h).wait_send()
                for k in range(2):
                    send_in(k, h).wait_send()
            for k in range(3):
                send_out(k).wait_send()
                pass_in(k).wait_send()
                pass_out(k).wait_send()
            keep(jm, 0).wait()
            for k in range(3):
                keep(blk(k), k + 1).wait()
            out.wait()

    def at_stage_of(jb):
        def index(s, i, jm_ref):
            sa = lax.bitwise_xor(jm_ref[0], jb)
            return jnp.where(s < sa, 0, jnp.where(s == sa, i, NT - 1))
        return index

    tok = lambda w, jb: pl.BlockSpec((TT, w), lambda s, i, jm_ref: (at_stage_of(jb)(s, i, jm_ref), 0))
    d4 = lambda jb: pl.BlockSpec((4, TT // 4, AW), lambda s, i, jm_ref: (0, at_stage_of(jb)(s, i, jm_ref), 0))
    d16 = lambda jb: pl.BlockSpec((16, TT // 16, AW), lambda s, i, jm_ref: (0, at_stage_of(jb)(s, i, jm_ref), 0))
    hbm = pl.BlockSpec(memory_space=pltpu.HBM)
    sd = lambda shape, dt: jax.ShapeDtypeStruct(shape, dt)
    in_own_stage = lambda s, i: jnp.where(s == 0, i, NT - 1)
    grid_spec = pltpu.PrefetchScalarGridSpec(
        num_scalar_prefetch=1, grid=(4, NT),
        in_specs=[pl.BlockSpec((TT, D), lambda s, i, jm_ref: (in_own_stage(s, i), 0)),
                  pl.BlockSpec((TT, 1), lambda s, i, jm_ref: (i, 0)),
                  pl.BlockSpec((1, D), lambda s, i, jm_ref: (0, 0)), hbm, hbm],
        out_specs=[pl.BlockSpec((D, TT), lambda s, i, jm_ref: (0, in_own_stage(s, i))),
                   tok(AW, 0), tok(AW, 0), tok(AW, 1), d4(0), d4(0), d4(1), d16(0), d16(0), d16(1),
                   tok(AW, 1), tok(AW, 2), tok(AW, 2), tok(AW, 3), tok(AW, 3), hbm, hbm],
        scratch_shapes=[pltpu.VMEM((4, D, 1024), BF16), pltpu.VMEM((4, 256, D), BF16), pltpu.VMEM((T, D), BF16),
                        pltpu.VMEM((4, TT, 128), F32), pltpu.VMEM((4, TT, 128), F32), pltpu.VMEM((256, 1024), F32),
                        pltpu.SemaphoreType.DMA((18,)),
                        pltpu.SemaphoreType.DMA((18,)), pltpu.SemaphoreType.DMA((6,))])
    return pl.pallas_call(
        body, name="fwd_in", grid_spec=grid_spec,
        out_shape=[sd((D, T), BF16)] + [sd((T, AW), BF16)] * 3 + [sd((4, T // 4, AW), BF16)] * 3
        + [sd((16, T // 16, AW), BF16)] * 3
        + [sd((T, AW), BF16)] * 5 + [sd((D, NCOL), BF16), sd((4, 256, D), BF16)],
        compiler_params=_cp(("arbitrary", "arbitrary")),
    )(jm_arr, x, pos, mixw, w_in, w_out)


def _band_mask(key_axis, nkeys=2 * BLK):
    shape = (nkeys, 2 * BLK) if key_axis == 0 else (2 * BLK, nkeys)
    kj = lax.broadcasted_iota(jnp.int32, shape, key_axis)
    qi = lax.broadcasted_iota(jnp.int32, shape, 1 - key_axis) & (BLK - 1)
    return (kj >= qi) & (kj <= qi + BLK), kj, qi


def _stack_heads(t2, in_a):
    z = jnp.zeros_like(t2)
    return jnp.concatenate([jnp.where(in_a[0], t2, z), jnp.where(in_a[1], t2, z)], axis=0)


def _attn_fwd(q, k, v, nb, name):
    n = ATTN_BLOCKS
    CH = n * BLK
    halo = nb > n

    def body(*refs):
        if halo:
            q_ref, k_ref, v_ref, kp_ref, vp_ref, o_ref, lse_ref = refs
        else:
            q_ref, k_ref, v_ref, o_ref, lse_ref = refs
        lane = lax.broadcasted_iota(jnp.int32, (1, 128), 1)
        in_a = [lane < HEAD, lane >= HEAD]
        band, kj, _ = _band_mask(1)
        thr0 = jnp.where((n * pl.program_id(0)) % nb == 0, BLK, 0) if halo else BLK
        mask0 = band & (kj >= thr0)
        mask_first = band & (kj >= BLK)
        for b in range(n):
            rs = slice(b * BLK, (b + 1) * BLK)
            stat = jnp.zeros((BLK, 128), F32)
            for hp in range(4):
                cs = slice(hp * 128, (hp + 1) * 128)
                q2s = _stack_heads(q_ref[rs, cs], in_a)
                if b == 0:
                    kprev = kp_ref[:, cs] if halo else k_ref[rs, cs]
                    vprev = vp_ref[:, cs] if halo else v_ref[rs, cs]
                    kk = jnp.concatenate([kprev, k_ref[rs, cs]], axis=0)
                    vv = jnp.concatenate([vprev, v_ref[rs, cs]], axis=0)
                    mask = mask0
                else:
                    kk = k_ref[(b - 1) * BLK:(b + 1) * BLK, cs]
                    vv = v_ref[(b - 1) * BLK:(b + 1) * BLK, cs]
                    mask = mask_first if b % nb == 0 else band
                s = jnp.where(mask, _mm_nt(q2s, kk) * SCALE, NEG)
                m = jnp.max(s, axis=-1, keepdims=True)
                p = jnp.exp(s - m)
                l = jnp.sum(p, axis=-1, keepdims=True)
                o = _mm(p.astype(BF16), vv) / l
                lse = m + jnp.log(l)
                o_ref[rs, cs] = jnp.where(in_a[0], o[:BLK], o[BLK:]).astype(BF16)
                stat = jnp.where(lane == 2 * hp, lse[:BLK], stat)
                stat = jnp.where(lane == 2 * hp + 1, lse[BLK:], stat)
            lse_ref[rs, :] = stat

    cur = pl.BlockSpec((CH, AW), lambda i: (i, 0))
    prev = pl.BlockSpec((BLK, AW), lambda i: (jnp.maximum(n * i - 1, 0), 0))
    return pl.pallas_call(
        body, name=name, grid=(T // CH,),
        in_specs=[cur, cur, cur] + ([prev, prev] if halo else []),
        out_specs=[cur, pl.BlockSpec((CH, 128), lambda i: (i, 0))],
        out_shape=[jax.ShapeDtypeStruct((T, AW), BF16), jax.ShapeDtypeStruct((T, 128), F32)],
        compiler_params=_cp(("parallel",)),
    )(*((q, k, v) + ((k, v) if halo else ())))


def _attn_bwd(q, k, v, do, st, nb, name):
    n = ATTN_BLOCKS
    CH = n * BLK
    NBLK = T // BLK
    halo = nb > n

    def body(*refs):
        if halo:
            (q_ref, k_ref, v_ref, do_ref, st_ref, kp_ref, vp_ref, qn_ref, don_ref, stn_ref,
             dq_ref, dk_ref, dv_ref) = refs
        else:
            q_ref, k_ref, v_ref, do_ref, st_ref, dq_ref, dk_ref, dv_ref = refs
        i = pl.program_id(0)
        lane = lax.broadcasted_iota(jnp.int32, (1, 128), 1)
        in_a = [lane < HEAD, lane >= HEAD]
        band, kj, _ = _band_mask(0)
        thr0 = jnp.where((n * i) % nb == 0, BLK, 0) if halo else BLK
        mask0 = band & (kj >= thr0)
        mask_first = band & (kj >= BLK)

        def stat_rows(st_t, hp):
            lse_r = jnp.concatenate([st_t[2 * hp:2 * hp + 1, :], st_t[2 * hp + 1:2 * hp + 2, :]], axis=1)
            dl_r = jnp.concatenate([st_t[8 + 2 * hp:9 + 2 * hp, :], st_t[9 + 2 * hp:10 + 2 * hp, :]], axis=1)
            return lse_r, dl_r

        st_t = [st_ref[b * BLK:(b + 1) * BLK, :].T for b in range(n)]
        if halo:
            nxt_thr = jnp.where((n * i + n) % nb == 0, 2 * BLK, 0)
            _, kj1, qi1 = _band_mask(0, BLK)
            mask_next = kj1 >= qi1 + nxt_thr
            stn_t = stn_ref[...].T

        for hp in range(4):
            cs = slice(hp * 128, (hp + 1) * 128)
            kb = [k_ref[b * BLK:(b + 1) * BLK, cs] for b in range(n)]
            vb = [v_ref[b * BLK:(b + 1) * BLK, cs] for b in range(n)]
            dk_acc = [jnp.zeros((BLK, 128), F32) for _ in range(n)]
            dv_acc = [jnp.zeros((BLK, 128), F32) for _ in range(n)]
            for b in range(n):
                rs = slice(b * BLK, (b + 1) * BLK)
                q2s = _stack_heads(q_ref[rs, cs], in_a)
                do2s = _stack_heads(do_ref[rs, cs], in_a)
                if b == 0:
                    kprev = kp_ref[:, cs] if halo else kb[0]
                    vprev = vp_ref[:, cs] if halo else vb[0]
                    mask = mask0
                else:
                    kprev, vprev, mask = kb[b - 1], vb[b - 1], (mask_first if b % nb == 0 else band)
                kk = jnp.concatenate([kprev, kb[b]], axis=0)
                vv = jnp.concatenate([vprev, vb[b]], axis=0)
                lse_r, dl_r = stat_rows(st_t[b], hp)
                s_t = jnp.where(mask, _mm_nt(kk, q2s) * SCALE, NEG)
                p_t = jnp.exp(s_t - lse_r)
                ds_t = (p_t * (_mm_nt(vv, do2s) - dl_r)).astype(BF16)
                dkk = _mm(ds_t, q2s) * SCALE
                dvv = _mm(p_t.astype(BF16), do2s)
                dqs = _mm_tn(ds_t, kk) * SCALE
                dq_ref[rs, cs] = jnp.where(in_a[0], dqs[:BLK], dqs[BLK:]).astype(BF16)
                dk_acc[b] += dkk[BLK:]
                dv_acc[b] += dvv[BLK:]
                if b > 0:
                    dk_acc[b - 1] += dkk[:BLK]
                    dv_acc[b - 1] += dvv[:BLK]
            if halo:
                q2s = _stack_heads(qn_ref[:, cs], in_a)
                do2s = _stack_heads(don_ref[:, cs], in_a)
                lse_r, dl_r = stat_rows(stn_t, hp)
                s_t = jnp.where(mask_next, _mm_nt(kb[n - 1], q2s) * SCALE, NEG)
                p_t = jnp.exp(s_t - lse_r)
                ds_t = (p_t * (_mm_nt(vb[n - 1], do2s) - dl_r)).astype(BF16)
                dk_acc[n - 1] += _mm(ds_t, q2s) * SCALE
                dv_acc[n - 1] += _mm(p_t.astype(BF16), do2s)
            for b in range(n):
                dk_ref[b * BLK:(b + 1) * BLK, cs] = dk_acc[b].astype(BF16)
                dv_ref[b * BLK:(b + 1) * BLK, cs] = dv_acc[b].astype(BF16)

    cur = pl.BlockSpec((CH, AW), lambda i: (i, 0))
    cur_st = pl.BlockSpec((CH, 128), lambda i: (i, 0))
    prev = pl.BlockSpec((BLK, AW), lambda i: (jnp.maximum(n * i - 1, 0), 0))
    nxt = pl.BlockSpec((BLK, AW), lambda i: (jnp.minimum(n * i + n, NBLK - 1), 0))
    nxt_st = pl.BlockSpec((BLK, 128), lambda i: (jnp.minimum(n * i + n, NBLK - 1), 0))
    ins = [cur] * 4 + [cur_st] + ([prev, prev, nxt, nxt, nxt_st] if halo else [])
    args = (q, k, v, do, st) + ((k, v, q, do, st) if halo else ())
    return pl.pallas_call(
        body, name=name, grid=(T // CH,),
        in_specs=ins,
        out_specs=[cur] * 3,
        out_shape=[jax.ShapeDtypeStruct((T, AW), BF16)] * 3,
        compiler_params=_cp(("parallel",)),
    )(*args)


TH = 256
NCH = TH // CHUNK


def _hgrn_common(hq_ref, hf_ref, lbr_ref, tri_ref):
    r0 = lbr_ref[0:1, :]
    r1 = lbr_ref[1:2, :]
    mx = jnp.maximum(r0, r1)
    e0 = jnp.exp(r0 - mx)
    e1 = jnp.exp(r1 - mx)
    lb = e0 / (e0 + e1)
    hqv = hq_ref[...].astype(F32)
    sq = _sigmoid(hqv)
    qv = hqv * sq
    sf = _sigmoid(hf_ref[...].astype(F32))
    f = lb + (1.0 - lb) * sf
    kv = 1.0 - f
    g = jnp.log(f)
    cum = _mm_exact_l(tri_ref[...], g)
    lastb = jnp.concatenate(
        [jnp.broadcast_to(cum[c * CHUNK + CHUNK - 1:(c + 1) * CHUNK, :], (CHUNK, HW)) for c in range(NCH)], axis=0)
    ea = jnp.exp(cum)
    ena = jnp.exp(-cum)
    eend = jnp.exp(lastb - cum)
    return dict(lb=lb, hq=hqv, sq=sq, q=qv, sf=sf, f=f, k=kv, cum=cum, lastb=lastb, ea=ea, ena=ena, eend=eend,
                qd=qv * ea, ki=kv * ena, ke=kv * eend, dec=jnp.exp(lastb))


def _tri_mask(transposed=False):
    ti = lax.broadcasted_iota(jnp.int32, (TH, TH), 1 if transposed else 0)
    si = lax.broadcasted_iota(jnp.int32, (TH, TH), 0 if transposed else 1)
    return (si <= ti) & ((si // CHUNK) == (ti // CHUNK))


def _hgrn_fwd(hq, hf, hi, lbr, tri):
    def body(hq_ref, hf_ref, hi_ref, lbr_ref, tri_ref, rec_ref, sall_ref, st_scr):
        @pl.when(pl.program_id(0) == 0)
        def _():
            st_scr[...] = jnp.zeros_like(st_scr)

        w = _hgrn_common(hq_ref, hf_ref, lbr_ref, tri_ref)
        qd, ki, ke = w["qd"].astype(BF16), w["ki"].astype(BF16), w["ke"].astype(BF16)
        dec = w["dec"]
        vb = hi_ref[...]
        causal = _tri_mask()
        for h in range(4):
            cs = slice(h * 128, (h + 1) * 128)
            att = jnp.where(causal, _mm_nt(qd[:, cs], ki[:, cs]), 0.0)
            o_intra = _mm(att.astype(BF16), vb[:, cs])
            for c in range(NCH):
                rs = slice(c * CHUNK, (c + 1) * CHUNK)
                st = st_scr[:, cs]
                sall_ref[c, :, cs] = st
                rec_ref[rs, cs] = (o_intra[rs] + _mm_nt(qd[rs, cs], st.astype(BF16))).astype(BF16)
                st_scr[:, cs] = dec[c * CHUNK:c * CHUNK + 1, cs] * st + _mm_tn(vb[rs, cs], ke[rs, cs])

    tok = pl.BlockSpec((TH, HW), lambda i: (i, 0))
    return pl.pallas_call(
        body, name="hgrn_fwd", grid=(T // TH,),
        in_specs=[tok, tok, tok, pl.BlockSpec((2, HW), lambda i: (0, 0)), pl.BlockSpec((TH, TH), lambda i: (0, 0))],
        out_specs=[tok, pl.BlockSpec((NCH, 128, HW), lambda i: (i, 0, 0))],
        out_shape=[jax.ShapeDtypeStruct((T, HW), BF16), jax.ShapeDtypeStruct((T // CHUNK, 128, HW), F32)],
        scratch_shapes=[pltpu.VMEM((128, HW), F32)],
        compiler_params=_cp(("arbitrary",)),
    )(hq, hf, hi, lbr, tri)


def _hgrn_bwd(hq, hf, hi, lbr, tri, trit, drec, sall, dhg, rout, routb):
    NT = T // TH

    def body(hq_ref, hf_ref, hi_ref, lbr_ref, tri_ref, trit_ref, do_ref, sall_ref, dhg_ref, rout_r, routb_r,
             dph_ref, small_ref, pout_o, poutr_o,
             dst_scr, dlb_scr, dqd_scr, dki_scr, dke_scr, dlast_scr, send_sems, recv_sems, loc_sems):
        step = pl.program_id(0)
        loc, rem = _chip_copies(_w_out_piece, rout_r, routb_r, pout_o, poutr_o, send_sems, recv_sems,
                                loc_sems.at[0])

        @pl.when(step == 0)
        def _():
            dst_scr[...] = jnp.zeros_like(dst_scr)
            dlb_scr[...] = jnp.zeros_like(dlb_scr)
            for cp in loc + rem:
                cp.start()

        w = _hgrn_common(hq_ref, hf_ref, lbr_ref, tri_ref)
        qd, ki, ke = w["qd"].astype(BF16), w["ki"].astype(BF16), w["ke"].astype(BF16)
        dec = w["dec"]
        vb = hi_ref[...]
        dob = do_ref[...].astype(BF16)
        causal = _tri_mask()
        causal_t = _tri_mask(transposed=True)
        for h in range(4):
            cs = slice(h * 128, (h + 1) * 128)
            att_t = jnp.where(causal_t, _mm_nt(ki[:, cs], qd[:, cs]), 0.0).astype(BF16)
            datt_t = jnp.where(causal_t, _mm_nt(vb[:, cs], dob[:, cs]), 0.0).astype(BF16)
            datt = jnp.where(causal, _mm_nt(dob[:, cs], vb[:, cs]), 0.0).astype(BF16)
            dv_intra = _mm(att_t, dob[:, cs])
            dqd_intra = _mm(datt, ki[:, cs])
            dki_scr[:, cs] = _mm(datt_t, qd[:, cs])
            for c in reversed(range(NCH)):
                rs = slice(c * CHUNK, (c + 1) * CHUNK)
                dec_c = dec[c * CHUNK:c * CHUNK + 1, :]
                st = sall_ref[c, :, cs]
                dst = dst_scr[:, cs]
                dstb = dst.astype(BF16)
                dph_ref[rs, 2 * HW + h * 128:2 * HW + (h + 1) * 128] = (
                    dv_intra[rs] + _mm_nt(ke[rs, cs], dstb)).astype(BF16)
                dqd_scr[rs, cs] = dqd_intra[rs] + _mm(dob[rs, cs], st.astype(BF16))
                dke_scr[rs, cs] = _mm(vb[rs, cs], dstb)
                ddec = jnp.sum(dst * st, axis=0, keepdims=True)
                dlast_scr[c:c + 1, cs] = ddec * dec_c[:, cs]
                dst_scr[:, cs] = dec_c[:, cs] * dst + _mm_tn(dob[rs, cs], qd[rs, cs])
        dqd, dki, dke = dqd_scr[...], dki_scr[...], dke_scr[...]
        dq = dqd * w["ea"]
        dk = dki * w["ena"] + dke * w["eend"]
        dcum = dqd * w["qd"] - dki * w["ki"] - dke * w["ke"]
        dkeke = dke * w["ke"]
        dlastb = jnp.concatenate(
            [jnp.broadcast_to(dlast_scr[c:c + 1, :] + jnp.sum(dkeke[c * CHUNK:(c + 1) * CHUNK], axis=0, keepdims=True),
                              (CHUNK, HW)) for c in range(NCH)], axis=0)
        dg = _mm_exact_l(trit_ref[...], dcum) + dlastb
        df = dg / w["f"] - dk
        lb, sf, sq = w["lb"], w["sf"], w["sq"]
        dph_ref[:, HW:2 * HW] = (df * (1.0 - lb) * sf * (1.0 - sf)).astype(BF16)
        dph_ref[:, 0:HW] = (dq * (sq * (1.0 + w["hq"] * (1.0 - sq)))).astype(BF16)
        dph_ref[:, 3 * HW:4 * HW] = dhg_ref[...]
        dlb_scr[...] += jnp.sum(df * (1.0 - sf), axis=0, keepdims=True)

        @pl.when(step == NT - 1)
        def _():
            gr = dlb_scr[...] * lb * (1.0 - lb)
            small_ref[...] = jnp.zeros_like(small_ref)
            small_ref[0:1, 0:HW] = gr
            small_ref[1:2, 0:HW] = -gr
            for cp in rem:
                cp.wait_recv()
            for cp in rem:
                cp.wait_send()
            for cp in loc:
                cp.wait()

    tok = pl.BlockSpec((TH, HW), lambda i: (NT - 1 - i, 0))
    const = lambda shape: pl.BlockSpec(shape, lambda i: (0,) * len(shape))
    hbm = pl.BlockSpec(memory_space=pltpu.HBM)
    return pl.pallas_call(
        body, name="hgrn_bwd", grid=(NT,),
        in_specs=[tok, tok, tok, const((2, HW)), const((TH, TH)), const((TH, TH)), tok,
                  pl.BlockSpec((NCH, 128, HW), lambda i: (NT - 1 - i, 0, 0)), tok, hbm, hbm],
        out_specs=[pl.BlockSpec((TH, NCOL // 2), lambda i: (NT - 1 - i, 0)), const((8, D)), hbm, hbm],
        out_shape=[jax.ShapeDtypeStruct((T, NCOL // 2), BF16), jax.ShapeDtypeStruct((8, D), F32),
                   jax.ShapeDtypeStruct((128, D), F32), jax.ShapeDtypeStruct((3, 128, D), BF16)],
        scratch_shapes=[pltpu.VMEM((128, HW), F32), pltpu.VMEM((1, HW), F32), pltpu.VMEM((TH, HW), F32),
                        pltpu.VMEM((TH, HW), F32), pltpu.VMEM((TH, HW), F32), pltpu.VMEM((8, HW), F32),
                        pltpu.SemaphoreType.DMA((3,)), pltpu.SemaphoreType.DMA((3,)), pltpu.SemaphoreType.DMA((1,))],
        compiler_params=_cp(("arbitrary",)),
    )(hq, hf, hi, lbr, tri, trit, drec, sall, dhg, rout, routb)


def _fwd_out(o1, o4, o16, l1, l4, l16, rec, ag, hg, x, tgt, anw, hnw, fnw, wout_full, gmat, emat, selmat):
    TT = 256

    def body(o1_r, o4_r, o16_r, l1_r, l4_r, l16_r, rec_r, ag_r, hg_r, x_r, tgt_r, anw_r, hnw_r, fnw_r, wo_r, g_r,
             e_r, sel_r, dx2_o, do1_o, do4_o, do16_o, st1_o, st4_o, st16_o, drec_o, dag_o, dhg_o,
             rout_o, routb_o, small_o, scr_a, scr_b, scr_c, gwout_o, rbuf, send_sems, recv_sems):
        @pl.when(pl.program_id(0) == 0)
        def _():
            gwout_o[...] = jnp.zeros_like(gwout_o)
            small_o[...] = jnp.zeros_like(small_o)

        def unperm(r4, r16):
            return _unperm_load(r4, r16, scr_a, scr_b, scr_c)

        def perm_out(val, p1, p4, p16, dt):
            _perm_store(val, scr_a, scr_b, p1, p4, p16, dt)

        o4u, o16u = unperm(o4_r, o16_r)
        l4c, l16c = unperm(l4_r, l16_r)
        l1c = l1_r[...]
        mxc = jnp.maximum(jnp.maximum(l1c, l4c), l16c)
        w1c, w4c, w16c = jnp.exp(l1c - mxc), jnp.exp(l4c - mxc), jnp.exp(l16c - mxc)
        denc = w1c + w4c + w16c
        lane = lax.broadcasted_iota(jnp.int32, (1, 128), 1)
        lse_c = jnp.where(lane < 8, mxc + jnp.log(denc), 0.0)
        em = e_r[...]
        wn1 = _mm_exact_r(w1c / denc, em)
        wn4 = _mm_exact_r(w4c / denc, em)
        o1v = o1_r[...].astype(F32)
        attn = wn1 * o1v + wn4 * o4u + (1.0 - wn1 - wn4) * o16u
        gm = g_r[...]

        def head_mean_a(t):
            return jnp.concatenate([_mm_exact_r(t[:, :256], gm), _mm_exact_r(t[:, 256:], gm)], axis=1)

        def head_mean_h(t):
            return jnp.concatenate(
                [jnp.broadcast_to(jnp.mean(t[:, h * 128:(h + 1) * 128], axis=-1, keepdims=True), (TT, 128))
                 for h in range(4)], axis=1)

        rs_a = lax.rsqrt(head_mean_a(attn * attn) + EPS)
        n_a = attn * rs_a
        agv = ag_r[...].astype(F32)
        sg_a = _sigmoid(agv)
        si_a = agv * sg_a
        anw_v = anw_r[...]
        y_a = (n_a * anw_v) * si_a
        recv = rec_r[...].astype(F32)
        rs_h = lax.rsqrt(head_mean_h(recv * recv) + EPS)
        n_h = recv * rs_h
        hgv = hg_r[...].astype(F32)
        sg_h = _sigmoid(hgv)
        si_h = hgv * sg_h
        hnw_v = hnw_r[...]
        y_h = (n_h * hnw_v) * si_h
        mixed = jnp.concatenate([y_a, y_h], axis=1).astype(BF16)
        xv = x_r[...]
        x2 = xv + _mm(mixed, wo_r[...])
        r2 = lax.rsqrt(jnp.mean(x2 * x2, axis=-1, keepdims=True) + EPS)
        fnw_v = fnw_r[...]
        xn = x2 * r2
        err = xn * fnw_v - tgt_r[...]
        small_o[2:3, :] += 0.5 * jnp.sum(jnp.mean(err * err, axis=-1, keepdims=True), axis=0, keepdims=True)
        dy = err * (1.0 / D)
        small_o[0:1, :] += jnp.sum(dy * xn, axis=0, keepdims=True)
        dyw = dy * fnw_v
        dx2 = r2 * dyw - x2 * ((r2 * r2 * r2) * jnp.mean(dyw * x2, axis=-1, keepdims=True))
        dx2_o[...] = dx2
        dx2b = dx2.astype(BF16)
        gwout_o[...] += _mm_tn(mixed, dx2b)
        dmix = _mm_nt(dx2b, wo_r[...])
        dm_a, dm_h = dmix[:, :AW], dmix[:, AW:]
        dag_o[...] = (dm_a * (n_a * anw_v) * (sg_a * (1.0 + agv * (1.0 - sg_a)))).astype(BF16)
        dn_a = dm_a * anw_v * si_a
        small_o[1:2, 0:AW] += jnp.sum(dm_a * n_a * si_a, axis=0, keepdims=True)
        dattn = rs_a * (dn_a - n_a * head_mean_a(dn_a * n_a))
        perm_out(dattn, do1_o, do4_o, do16_o, BF16)
        stats = lse_c + _mm_exact_r(dattn * attn, sel_r[...])
        perm_out(stats, st1_o, st4_o, st16_o, F32)
        dhg_o[...] = (dm_h * (n_h * hnw_v) * (sg_h * (1.0 + hgv * (1.0 - sg_h)))).astype(BF16)
        dn_h = dm_h * hnw_v * si_h
        small_o[1:2, AW:] += jnp.sum(dm_h * n_h * si_h, axis=0, keepdims=True)
        drec_o[...] = (rs_h * (dn_h - n_h * head_mean_h(dn_h * n_h))).astype(BF16)

        @pl.when(pl.program_id(0) == T // TT - 1)
        def _():
            x, y, c = lax.axis_index("x"), lax.axis_index("y"), lax.axis_index("c")
            cps = [pltpu.make_async_remote_copy(
                src_ref=gwout_o.at[pl.ds(pl.multiple_of(j * 256 + (1 - c) * 128, 128), 128), :], dst_ref=rbuf.at[j],
                send_sem=send_sems.at[j], recv_sem=recv_sems.at[j], device_id=(x, y, 1 - c), device_id_type=MESH)
                for j in range(4)]
            for cp in cps:
                cp.start()
            for j, cp in enumerate(cps):
                cp.wait_recv()
                red = gwout_o[pl.ds(pl.multiple_of(j * 256 + c * 128, 128), 128), :] + rbuf[j]
                rout_o[j * 128:(j + 1) * 128, :] = red
                routb_o[j * 128:(j + 1) * 128, :] = red.astype(BF16)
            for cp in cps:
                cp.wait_send()

    tok = lambda w: pl.BlockSpec((TT, w), lambda i: (i, 0))
    d4 = pl.BlockSpec((4, TT // 4, AW), lambda i: (0, i, 0))
    d16 = pl.BlockSpec((16, TT // 16, AW), lambda i: (0, i, 0))
    const = lambda shape: pl.BlockSpec(shape, lambda i: (0,) * len(shape))
    sd = lambda shape, dt: jax.ShapeDtypeStruct(shape, dt)
    c4 = pl.BlockSpec((4, TT // 4, 128), lambda i: (0, i, 0))
    c16 = pl.BlockSpec((16, TT // 16, 128), lambda i: (0, i, 0))
    p3 = lambda w, dt: [sd((T, w), dt), sd((4, T // 4, w), dt), sd((16, T // 16, w), dt)]
    return pl.pallas_call(
        body, name="fwd_out", grid=(T // TT,),
        in_specs=[tok(AW), d4, d16, tok(128), c4, c16, tok(AW), tok(AW), tok(AW), tok(D), tok(D),
                  const((1, AW)), const((1, HW)), const((1, D)), const((D, D)), const((256, 256)),
                  const((128, AW)), const((AW, 128))],
        out_specs=[tok(D)] + [tok(AW), d4, d16] + [tok(128), c4, c16] + [tok(AW)] * 3
        + [const((512, D)), const((512, D)), const((8, D))],
        out_shape=[sd((T, D), F32)] + p3(AW, BF16) + p3(128, F32)
        + [sd((T, AW), BF16), sd((T, AW), BF16), sd((T, AW), BF16), sd((512, D), F32), sd((512, D), BF16),
           sd((8, D), F32)],
        scratch_shapes=[pltpu.VMEM((4, TT, 128), F32)] * 3 + [pltpu.VMEM((D, D), F32),
                        pltpu.VMEM((4, 128, D), F32), pltpu.SemaphoreType.DMA((4,)), pltpu.SemaphoreType.DMA((4,))],
        compiler_params=_cp(("arbitrary",)),
    )(o1, o4, o16, l1, l4, l16, rec, ag, hg, x, tgt, anw, hnw, fnw, wout_full, gmat, emat, selmat)


def _dproj_build(dq, dk, dv, dag, pos):
    TT = 512

    def body(dq1, dq4, dq16, dk1, dk4, dk16, dv1, dv4, dv16, dag_r, pos_r, dproj_o, scr_a, scr_b, scr_c):
        def unperm_sum(r1, r4, r16):
            u4, u16 = _unperm_load(r4, r16, scr_a, scr_b, scr_c)
            return r1[...] + u4 + u16

        cosf, s1, s2 = _rope_tables(pos_r[...])
        dproj_o[:, 0:512] = _rope_bwd(unperm_sum(dq1, dq4, dq16), cosf, s1, s2).astype(BF16)
        dproj_o[:, 512:1024] = _rope_bwd(unperm_sum(dk1, dk4, dk16), cosf, s1, s2).astype(BF16)
        dproj_o[:, 1024:1536] = unperm_sum(dv1, dv4, dv16).astype(BF16)
        dproj_o[:, 1536:2048] = dag_r[...]

    tok = lambda w: pl.BlockSpec((TT, w), lambda i: (i, 0))
    d4 = pl.BlockSpec((4, TT // 4, AW), lambda i: (0, i, 0))
    d16 = pl.BlockSpec((16, TT // 16, AW), lambda i: (0, i, 0))
    return pl.pallas_call(
        body, name="dproj_build", grid=(T // TT,),
        in_specs=[tok(AW), d4, d16] * 3 + [tok(AW), tok(1)],
        out_specs=tok(NCOL // 2),
        out_shape=jax.ShapeDtypeStruct((T, NCOL // 2), BF16),
        scratch_shapes=[pltpu.VMEM((4, TT, 128), F32)] * 3,
        compiler_params=_cp(("parallel",)),
    )(*dq, *dk, *dv, dag, pos)


def _bwd_x(dproj_a, dproj_h, x, dx2, mixw, w_full, rin, rinb, small4, small6, pout_own, pout_rem):
    TT = 256
    NT = T // TT

    def body(dpa_r, dph_r, x_r, dx2_r, mw_r, w_r, rin_r, rinb_r, s4_r, s6_r, poo_r, por_r,
             gx_o, pin_o, pinr_o, sall_o, fin_o, fout_o, sbuf, v_own, v_rem, vo_own, vo_rem, sin, sout, got_in,
             got_out, send_sems, recv_sems, loc_sems, share_send, share_recv, fin_sems):
        i = pl.program_id(0)
        loc, rem = _chip_copies(_w_in_piece, rin_r, rinb_r, pin_o, pinr_o, send_sems, recv_sems, loc_sems.at[0])

        @pl.when(i == 0)
        def _():
            sbuf[...] = jnp.zeros_like(sbuf)
            for cp in loc + rem:
                cp.start()

        dhn = _mm_nt(dpa_r[...], w_r[:, 0:NCOL // 2]) + _mm_nt(dph_r[...], w_r[:, NCOL // 2:NCOL])
        xv = x_r[...]
        r = lax.rsqrt(jnp.mean(xv * xv, axis=-1, keepdims=True) + EPS)
        dxw = dhn * mw_r[...]
        gx_o[...] = dx2_r[...] + r * dxw - xv * ((r * r * r) * jnp.mean(dxw * xv, axis=-1, keepdims=True))
        sbuf[16:17, :] += jnp.sum(dhn * (xv * r), axis=0, keepdims=True)

        @pl.when(i == NT - 1)
        def _():
            sbuf[0:8, :] = s4_r[...]
            sbuf[8:16, :] = s6_r[...]
            sloc, srem = _small_copies(sbuf, sall_o, send_sems, recv_sems, loc_sems.at[1])
            for cp in sloc + srem:
                cp.start()
            for cp in rem + srem:
                cp.wait_recv()
            for cp in rem + srem:
                cp.wait_send()
            for cp in loc + sloc:
                cp.wait()
            mx, my, c = lax.axis_index("x"), lax.axis_index("y"), lax.axis_index("c")
            loads = [pltpu.make_async_copy(pin_o, v_own, fin_sems.at[0]),
                     pltpu.make_async_copy(pinr_o, v_rem, fin_sems.at[1]),
                     pltpu.make_async_copy(poo_r, vo_own, fin_sems.at[2]),
                     pltpu.make_async_copy(por_r, vo_rem, fin_sems.at[3])]
            for cp in loads:
                cp.start()
            for cp in loads:
                cp.wait()
            sout[...] = ((vo_own[...] + vo_rem[0].astype(F32)) + vo_rem[1].astype(F32)) + vo_rem[2].astype(F32)
            sin[...] = ((v_own[...] + v_rem[0].astype(F32)) + v_rem[1].astype(F32)) + v_rem[2].astype(F32)
            swap = [pltpu.make_async_remote_copy(src_ref=sin, dst_ref=got_in, send_sem=share_send.at[0],
                                                 recv_sem=share_recv.at[0], device_id=(mx, my, 1 - c),
                                                 device_id_type=MESH),
                    pltpu.make_async_remote_copy(src_ref=sout, dst_ref=got_out, send_sem=share_send.at[1],
                                                 recv_sem=share_recv.at[1], device_id=(mx, my, 1 - c),
                                                 device_id_type=MESH)]
            for cp in swap:
                cp.start()
            mine = [pltpu.make_async_copy(sin, fin_o.at[c], fin_sems.at[0]),
                    pltpu.make_async_copy(sout, fout_o.at[c], fin_sems.at[1])]
            for cp in mine:
                cp.start()
            for cp in swap:
                cp.wait_recv()
            theirs = [pltpu.make_async_copy(got_in, fin_o.at[1 - c], fin_sems.at[2]),
                      pltpu.make_async_copy(got_out, fout_o.at[1 - c], fin_sems.at[3])]
            for cp in theirs:
                cp.start()
            for cp in swap:
                cp.wait_send()
            for cp in mine + theirs:
                cp.wait()

    tok = lambda w: pl.BlockSpec((TT, w), lambda i: (i, 0))
    const = lambda shape: pl.BlockSpec(shape, lambda i: (0,) * len(shape))
    hbm = pl.BlockSpec(memory_space=pltpu.HBM)
    return pl.pallas_call(
        body, name="bwd_x", grid=(NT,),
        in_specs=[tok(NCOL // 2), tok(NCOL // 2), tok(D), tok(D), const((1, D)), const((D, NCOL)), hbm, hbm,
                  const((8, D)), const((8, D)), hbm, hbm],
        out_specs=[tok(D), hbm, hbm, hbm, hbm, hbm],
        out_shape=[jax.ShapeDtypeStruct((T, D), F32),
                   jax.ShapeDtypeStruct((512, 1024), F32), jax.ShapeDtypeStruct((3, 512, 1024), BF16),
                   jax.ShapeDtypeStruct((8, 24, D), F32),
                   jax.ShapeDtypeStruct((2, 512, 1024), F32), jax.ShapeDtypeStruct((2, 128, D), F32)],
        scratch_shapes=[pltpu.VMEM((24, D), F32),
                        pltpu.VMEM((512, 1024), F32), pltpu.VMEM((3, 512, 1024), BF16),
                        pltpu.VMEM((128, D), F32), pltpu.VMEM((3, 128, D), BF16),
                        pltpu.VMEM((512, 1024), F32), pltpu.VMEM((128, D), F32),
                        pltpu.VMEM((512, 1024), F32), pltpu.VMEM((128, D), F32),
                        pltpu.SemaphoreType.DMA((10,)), pltpu.SemaphoreType.DMA((10,)), pltpu.SemaphoreType.DMA((2,)),
                        pltpu.SemaphoreType.DMA((2,)), pltpu.SemaphoreType.DMA((2,)), pltpu.SemaphoreType.DMA((4,))],
        compiler_params=_cp(("arbitrary",)),
    )(dproj_a, dproj_h, x, dx2, mixw, w_full, rin, rinb, small4, small6, pout_own, pout_rem)


def _grad_w_in(hn, dproj_a, dproj_h):
    TK = 2048
    NK = T // TK

    def body(hnt_r, dpa_r, dph_r, rin_o, rinb_o, acc, rbuf, obuf, obufb, send_sems, recv_sems, wb_sems):
        j = pl.program_id(0)
        kk = pl.program_id(1)
        x, y, c = lax.axis_index("x"), lax.axis_index("y"), lax.axis_index("c")
        mine = pl.ds(pl.multiple_of(c * 512, 512), 512)
        theirs = pl.ds(pl.multiple_of((1 - c) * 512, 512), 512)

        def send(jj):
            return pltpu.make_async_remote_copy(
                src_ref=acc.at[jj % 2, theirs, :], dst_ref=rbuf.at[jj], send_sem=send_sems.at[jj],
                recv_sem=recv_sems.at[jj], device_id=(x, y, 1 - c), device_id_type=MESH)

        def writeback(jj):
            cols = pl.ds(jj * 1024, 1024)
            return [pltpu.make_async_copy(obuf.at[jj % 2], rin_o.at[:, cols], wb_sems.at[jj % 2]),
                    pltpu.make_async_copy(obufb.at[jj % 2], rinb_o.at[:, cols], wb_sems.at[2 + jj % 2])]

        def wait_writeback(jj):
            for cp in writeback(jj):
                cp.wait()

        def finalize(jj):
            send(jj).wait_recv()
            red = acc[jj % 2, mine, :] + rbuf[jj]
            obuf[jj % 2] = red
            obufb[jj % 2] = red.astype(BF16)
            for cp in writeback(jj):
                cp.start()

        prod = _mm(hnt_r[...], jnp.where(j < 2, dpa_r[...], dph_r[...]))

        @pl.when(kk == 0)
        def _():
            for jj in (2, 3):
                @pl.when(j == jj)
                def _():
                    send(jj - 2).wait_send()
            acc[j % 2] = prod

        @pl.when(kk > 0)
        def _():
            acc[j % 2] += prod

        @pl.when(kk == NK - 1)
        def _():
            for jj in range(4):
                @pl.when(j == jj)
                def _():
                    send(jj).start()
                    if jj in (1, 2):
                        finalize(jj - 1)
                    if jj == 3:
                        wait_writeback(0)
                        finalize(2)
                        wait_writeback(1)
                        finalize(3)
                        wait_writeback(2)
                        wait_writeback(3)
                        send(2).wait_send()
                        send(3).wait_send()

    hbm = pl.BlockSpec(memory_space=pltpu.HBM)
    return pl.pallas_call(
        body, name="grad_w_in", grid=(4, NK),
        in_specs=[pl.BlockSpec((D, TK), lambda j, kk: (0, kk)),
                  pl.BlockSpec((TK, 1024), lambda j, kk: (jnp.where(j < 2, kk, NK - 1), jnp.minimum(j, 1))),
                  pl.BlockSpec((TK, 1024), lambda j, kk: (jnp.where(j < 2, 0, kk), jnp.maximum(j - 2, 0)))],
        out_specs=[hbm, hbm],
        out_shape=[jax.ShapeDtypeStruct((512, NCOL), F32), jax.ShapeDtypeStruct((512, NCOL), BF16)],
        scratch_shapes=[pltpu.VMEM((2, D, 1024), F32), pltpu.VMEM((4, 512, 1024), F32), pltpu.VMEM((2, 512, 1024), F32),
                        pltpu.VMEM((2, 512, 1024), BF16),
                        pltpu.SemaphoreType.DMA((4,)), pltpu.SemaphoreType.DMA((4,)), pltpu.SemaphoreType.DMA((4,))],
        compiler_params=_cp(("arbitrary", "arbitrary")),
    )(hn, dproj_a, dproj_h)


def _w_in_piece(ref, j):
    return ref.at[:, pl.ds(j * 1024, 1024)]


def _w_out_piece(ref, j):
    return ref.at[pl.ds(j * 128, 128), :]


def _chip_copies(piece, src_r, srcb_r, own_o, rem_o, send_sems, recv_sems, loc_sem):
    x, y, c = lax.axis_index("x"), lax.axis_index("y"), lax.axis_index("c")
    chips = [(1 - x, y), (x, 1 - y), (1 - x, 1 - y)]
    loc = [pltpu.make_async_copy(piece(src_r, 2 * x + y), own_o, loc_sem)]
    rem = [pltpu.make_async_remote_copy(
        src_ref=piece(srcb_r, 2 * px + py), dst_ref=rem_o.at[k], send_sem=send_sems.at[k],
        recv_sem=recv_sems.at[k], device_id=(px, py, c), device_id_type=MESH) for k, (px, py) in enumerate(chips)]
    return loc, rem


def _small_copies(small_r, sall_o, send_sems, recv_sems, loc_sem):
    x, y, c = lax.axis_index("x"), lax.axis_index("y"), lax.axis_index("c")
    me = 4 * x + 2 * y + c
    loc = [pltpu.make_async_copy(small_r, sall_o.at[me], loc_sem)]
    rem = []
    k = 3
    for fx in range(2):
        for fy in range(2):
            for fc in range(2):
                if fx or fy or fc:
                    peer = (1 - x if fx else x, 1 - y if fy else y, 1 - c if fc else c)
                    rem.append(pltpu.make_async_remote_copy(
                        src_ref=small_r, dst_ref=sall_o.at[me], send_sem=send_sems.at[k],
                        recv_sem=recv_sems.at[k], device_id=peer, device_id_type=MESH))
                    k += 1
    return loc, rem


def _adamw_math(w, g, m, v):
    m = B1 * m + (1.0 - B1) * g
    v = B2 * v + (1.0 - B2) * (g * g)
    m_hat = m / (1.0 - B1 ** STEP)
    v_hat = v / (1.0 - B2 ** STEP)
    delta = -LR * (m_hat / (jnp.sqrt(v_hat) + AEPS) + WD * w)
    return delta, m, v


def _adamw(big_in, big_out, sall, params):
    def body(*refs):
        wi, gi, mi, vi, wo, go, mo, vo, sall_r = refs[:9]
        ins = refs[9:24]
        di_o, mi_o, vi_o, do_o, mo_o, vo_o = refs[24:30]
        outs = refs[30:]
        d, mm, vv = _adamw_math(wi[...], gi[...], mi[...], vi[...])
        di_o[...] = d
        mi_o[...] = mm
        vi_o[...] = vv

        @pl.when(pl.program_id(0) == 0)
        def _():
            d, mm, vv = _adamw_math(wo[...], go[...], mo[...], vo[...])
            do_o[...] = d
            mo_o[...] = mm
            vo_o[...] = vv
            tot = sall_r[0]
            for dv in range(1, 8):
                tot = tot + sall_r[dv]
            grads = [tot[16:17, :], tot[1:2, 0:AW], tot[1:2, AW:], tot[8:10, 0:HW], tot[0:1, :]]
            outs[0][...] = tot[2:3, 0:1]
            for p in range(5):
                w_r, m_r, v_r = ins[3 * p:3 * p + 3]
                g = grads[p]
                d, mm, vv = _adamw_math(w_r[...], g, m_r[...], v_r[...])
                outs[1 + 4 * p][...] = g
                outs[2 + 4 * p][...] = d
                outs[3 + 4 * p][...] = mm
                outs[4 + 4 * p][...] = vv

    flat = [a for p in params for a in p]
    shapes = [jax.ShapeDtypeStruct((D, 1024), F32)] * 3 + [jax.ShapeDtypeStruct((256, D), F32)] * 3
    shapes += [jax.ShapeDtypeStruct((1, 1), F32)]
    for p in params:
        shapes += [jax.ShapeDtypeStruct(p[0].shape, F32)] * 4
    vm = pl.BlockSpec(memory_space=pltpu.VMEM)
    rows = pl.BlockSpec((256, 1024), lambda i: (i, 0))
    whole = pl.BlockSpec((256, D), lambda i: (0, 0))
    return pl.pallas_call(
        body, name="adamw", grid=(4,),
        in_specs=[rows] * 4 + [whole] * 4 + [vm] * 16, out_specs=[rows] * 3 + [whole] * 3 + [vm] * 21,
        out_shape=shapes,
        compiler_params=_cp(("arbitrary",)),
    )(*big_in, *big_out, sall, *flat)


def kernel(x, positions, w_in, w_out, mix_norm_w, attn_out_norm_w, hgrn_out_norm_w, hgrn_lb_raw, final_norm_w, loss_target, m_w_in, m_w_out, m_mix_norm_w, m_attn_out_norm_w, m_hgrn_out_norm_w, m_hgrn_lb_raw, m_final_norm_w, v_w_in, v_w_out, v_mix_norm_w, v_attn_out_norm_w, v_hgrn_out_norm_w, v_hgrn_lb_raw, v_final_norm_w):
    xs = x.reshape(T, D)
    tgt = loss_target.reshape(T, D)
    pos = positions.reshape(T, 1)
    fnw = final_norm_w.reshape(1, D)

    ti = np.arange(TH)
    tri_np = ((ti[:, None] // CHUNK == ti[None, :] // CHUNK) & (ti[None, :] <= ti[:, None])).astype(np.float32)
    tri = jnp.asarray(tri_np, BF16)
    trit = jnp.asarray(tri_np.T, BF16)
    hi_ = np.arange(AW) // HEAD
    gmat = jnp.asarray((hi_[:256, None] == hi_[None, :256]).astype(np.float32) / HEAD, BF16)
    emat_np = (np.arange(128)[:, None] == hi_[None, :]).astype(np.float32)
    sel_np = (8 + hi_[:, None] == np.arange(128)[None, :]).astype(np.float32)
    emat = jnp.asarray(emat_np, BF16)
    selmat = jnp.asarray(sel_np, BF16)

    jm_arr = (2 * lax.axis_index("x") + lax.axis_index("y")).astype(jnp.int32).reshape(1)
    (hn, q1, k1, v1, q4, k4, v4, q16, k16, v16, ag, hq, hf, hi, hg, w_full, wout4) = _fwd_in(
        xs, pos, mix_norm_w, w_in.reshape(D, 1024), w_out.reshape(256, D), jm_arr)
    wout_full = wout4.reshape(D, D)
    flat = lambda a: a.reshape(T, AW)
    o1, l1 = _attn_fwd(q1, k1, v1, T // BLK, "attn_fwd_d1")
    o4, l4 = _attn_fwd(flat(q4), flat(k4), flat(v4), T // 4 // BLK, "attn_fwd_d4")
    o16, l16 = _attn_fwd(flat(q16), flat(k16), flat(v16), T // 16 // BLK, "attn_fwd_d16")
    rec, sall = _hgrn_fwd(hq, hf, hi, hgrn_lb_raw, tri)

    (dx2, do1, do4, do16, st1, st4, st16, drec, dag, dhg, rout, routb, small4) = _fwd_out(
        o1, o4.reshape(4, T // 4, AW), o16.reshape(16, T // 16, AW),
        l1, l4.reshape(4, T // 4, 128), l16.reshape(16, T // 16, 128),
        rec, ag, hg, xs, tgt, attn_out_norm_w, hgrn_out_norm_w, fnw, wout_full, gmat, emat, selmat)

    fst = lambda a: a.reshape(T, 128)
    dq1, dk1, dv1 = _attn_bwd(q1, k1, v1, do1, st1, T // BLK, "attn_bwd_d1")
    dq4, dk4, dv4 = _attn_bwd(flat(q4), flat(k4), flat(v4), flat(do4), fst(st4), T // 4 // BLK, "attn_bwd_d4")
    dq16, dk16, dv16 = _attn_bwd(flat(q16), flat(k16), flat(v16), flat(do16), fst(st16), T // 16 // BLK,
                                 "attn_bwd_d16")
    dproj_h, small6, pout_own, pout_rem = _hgrn_bwd(hq, hf, hi, hgrn_lb_raw, tri, trit, drec, sall, dhg,
                                                    rout, routb)

    r4 = lambda a: a.reshape(4, T // 4, AW)
    r16 = lambda a: a.reshape(16, T // 16, AW)
    dproj_a = _dproj_build((dq1, r4(dq4), r16(dq16)), (dk1, r4(dk4), r16(dk16)), (dv1, r4(dv4), r16(dv16)),
                           dag, pos)
    rin, rinb = _grad_w_in(hn, dproj_a, dproj_h)
    gx, _, _, small_all, fin, fout = _bwd_x(dproj_a, dproj_h, xs, dx2, mix_norm_w, w_full, rin, rinb,
                                            small4, small6, pout_own, pout_rem)
    g_w_in = fin.reshape(D, 1024)
    g_w_out = fout.reshape(256, D)

    params = [(mix_norm_w, m_mix_norm_w, v_mix_norm_w),
              (attn_out_norm_w, m_attn_out_norm_w, v_attn_out_norm_w),
              (hgrn_out_norm_w, m_hgrn_out_norm_w, v_hgrn_out_norm_w),
              (hgrn_lb_raw, m_hgrn_lb_raw, v_hgrn_lb_raw),
              (fnw, m_final_norm_w.reshape(1, D), v_final_norm_w.reshape(1, D))]
    d_in, nm_in, nv_in, d_out, nm_out, nv_out, *so = _adamw(
        (w_in.reshape(D, 1024), g_w_in, m_w_in.reshape(D, 1024), v_w_in.reshape(D, 1024)),
        (w_out.reshape(256, D), g_w_out, m_w_out.reshape(256, D), v_w_out.reshape(256, D)), small_all, params)
    loss = so[0].reshape(())
    g_s = [so[1 + 4 * p] for p in range(5)]
    d_s = [so[2 + 4 * p] for p in range(5)]
    m_s = [so[3 + 4 * p] for p in range(5)]
    v_s = [so[4 + 4 * p] for p in range(5)]
    for lst in (g_s, d_s, m_s, v_s):
        lst[4] = lst[4].reshape(D)

    return (loss, gx.reshape(1, T, D),
            g_w_in.reshape(1, D, 1024), g_w_out.reshape(1, 256, D), *g_s,
            d_in.reshape(1, D, 1024), d_out.reshape(1, 256, D), *d_s,
            nm_in.reshape(1, D, 1024), nm_out.reshape(1, 256, D), *m_s,
            nv_in.reshape(1, D, 1024), nv_out.reshape(1, 256, D), *v_s)
```

```python
import functools

import numpy as np
import jax
import jax.numpy as jnp
from jax import lax
from jax.experimental import pallas as pl
from jax.experimental.pallas import tpu as pltpu

F32 = jnp.float32
BF16 = jnp.bfloat16

T = 4096
D = 1024
AW = 512
HW = 512
NCOL = 4096
HEAD = 64
BLK = 128
CHUNK = 64
EPS = 1e-6
SCALE = HEAD ** -0.5
NEG = -1e30
ROPE_THETA = 500000.0
INV_FREQ = [float(v) for v in
            (np.float32(ROPE_THETA) ** (-(np.arange(8, dtype=np.float32)) * np.float32(0.125)))]
LR, B1, B2, AEPS, WD, STEP = 0.001, 0.9, 0.999, 1e-08, 0.01, 10
VMEM_LIMIT = 56 * 1024 * 1024
MESH = pl.DeviceIdType.MESH


def _cp(sem=None, **kw):
    return pltpu.CompilerParams(dimension_semantics=sem, vmem_limit_bytes=VMEM_LIMIT, **kw)


def _mm(a, b):
    return jnp.dot(a, b, preferred_element_type=F32)


def _mm_nt(a, b):
    return lax.dot_general(a, b, (((1,), (1,)), ((), ())), preferred_element_type=F32)


def _mm_tn(a, b):
    return lax.dot_general(a, b, (((0,), (0,)), ((), ())), preferred_element_type=F32)


def _split3(x):
    h = x.astype(BF16)
    r = x - h.astype(F32)
    m = r.astype(BF16)
    l = (r - m.astype(F32)).astype(BF16)
    return h, m, l


def _mm_exact_l(mat_bf, x):
    h, m, l = _split3(x)
    return _mm(mat_bf, h) + _mm(mat_bf, m) + _mm(mat_bf, l)


def _mm_exact_r(x, mat_bf):
    h = x.astype(BF16)
    l = (x - h.astype(F32)).astype(BF16)
    return _mm(h, mat_bf) + _mm(l, mat_bf)


def _sigmoid(x):
    return 0.5 * jnp.tanh(0.5 * x) + 0.5


def _rope_tables(pos):
    lane = lax.broadcasted_iota(jnp.int32, (1, 128), 1)
    jl = lane & 63
    fi = jl & 7
    inv = jnp.zeros((1, 128), F32)
    for kk in range(8):
        inv = jnp.where(fi == kk, INV_FREQ[kk], inv)
    ang = pos.astype(F32) * inv
    c = jnp.cos(ang)
    s = jnp.sin(ang)
    cosf = jnp.where(jl < 16, c, 1.0)
    s1 = jnp.where(jl < 8, -s, 0.0)
    s2 = jnp.where((jl >= 8) & (jl < 16), s, 0.0)
    return cosf, s1, s2


def _rope(t, cosf, s1, s2):
    parts = []
    for ci in range(t.shape[1] // 128):
        tc = t[:, ci * 128:(ci + 1) * 128]
        parts.append(tc * cosf + pltpu.roll(tc, 120, 1) * s1 + pltpu.roll(tc, 8, 1) * s2)
    return jnp.concatenate(parts, axis=1)


def _rope_bwd(g, cosf, s1, s2):
    parts = []
    for ci in range(g.shape[1] // 128):
        gc = g[:, ci * 128:(ci + 1) * 128]
        parts.append(gc * cosf + pltpu.roll(gc * s1, 8, 1) + pltpu.roll(gc * s2, 120, 1))
    return jnp.concatenate(parts, axis=1)


def _perm_store(val, scr, scr2, o1, o4, o16, dt):
    n = val.shape[0]
    q = n // 4
    o1[...] = val.astype(dt)
    for ci in range(val.shape[1] // 128):
        cs = slice(ci * 128, (ci + 1) * 128)
        scr[ci] = val[:, cs]
        for r4 in range(4):
            part = scr[ci, pl.ds(r4, q, stride=4), :]
            o4[r4, :, cs] = part.astype(dt)
            scr2[ci, r4 * q:(r4 + 1) * q, :] = part
        for r4 in range(4):
            for b in range(4):
                o16[r4 + 4 * b, :, cs] = scr2[ci, pl.ds(r4 * q + b, q // 4, stride=4), :].astype(dt)


def _unperm_load(r4, r16, scr_a, scr_b, scr_c):
    n = scr_a.shape[1]
    q = n // 4
    nc = r4.shape[-1] // 128
    for ci in range(nc):
        cs = slice(ci * 128, (ci + 1) * 128)
        for rr in range(4):
            scr_a[ci, pl.ds(rr, q, stride=4), :] = r4[rr, :, cs].astype(F32)
        for rr in range(4):
            for b in range(4):
                scr_c[ci, pl.ds(rr * q + b, q // 4, stride=4), :] = r16[rr + 4 * b, :, cs].astype(F32)
        for rr in range(4):
            scr_b[ci, pl.ds(rr, q, stride=4), :] = scr_c[ci, rr * q:(rr + 1) * q, :]
    return (jnp.concatenate([scr_a[ci] for ci in range(nc)], axis=1),
            jnp.concatenate([scr_b[ci] for ci in range(nc)], axis=1))


def _fwd_in(x, pos, mixw, w_in, w_out, jm_arr):
    TT = 512
    NT = T // TT

    def body(jm_ref, x_ref, pos_ref, mw_ref, win_ref, wout_ref,
             hnt_ref, q1, k1, v1, q4, k4, v4, q16, k16, v16, ag, hq, hf, hi, hg, wfull_o, woutfull_o,
             wbuf, wobuf, hn_all, scr, scr2, stage, send_sems, recv_sems, loc_sems):
        s = pl.program_id(0)
        i = pl.program_id(1)
        mx, my, c = lax.axis_index("x"), lax.axis_index("y"), lax.axis_index("c")
        me, sibling = (mx, my, c), (mx, my, 1 - c)
        chips = [(mx, 1 - my), (1 - mx, my), (1 - mx, 1 - my)]
        jm = 2 * mx + my
        rows_in = [pl.ds(pl.multiple_of(h * 512, 512), 512) for h in (c, 1 - c)]
        rows_out = [pl.ds(pl.multiple_of(h * 128, 128), 128) for h in (c, 1 - c)]

        def blk(k):
            return lax.bitwise_xor(jm, k + 1)

        def rc(n, ref, to):
            return pltpu.make_async_remote_copy(src_ref=ref, dst_ref=ref, send_sem=send_sems.at[n],
                                                recv_sem=recv_sems.at[n], device_id=to, device_id_type=MESH)

        halves = [pl.ds(0, 512), pl.ds(512, 512)]
        send_in = lambda k, h: rc(12 + 2 * k + h, wbuf.at[jm, rows_in[0], halves[h]], (*chips[k], c))
        got_in = lambda k, h: rc(12 + 2 * k + h, wbuf.at[blk(k), rows_in[0], halves[h]], me)
        relay = lambda h: rc(16 + h, wbuf.at[blk(h), rows_in[0], halves[h]], (*chips[1 - h], c))
        got_relay = lambda h: rc(16 + h, wbuf.at[blk(2), rows_in[0], halves[h]], me)
        send_out = lambda k: rc(3 + k, wobuf.at[jm, rows_out[0], :], (*chips[k], c))
        got_out = lambda k: rc(3 + k, wobuf.at[blk(k), rows_out[0], :], me)
        pass_in = lambda k: rc(6 + k, wbuf.at[blk(k), rows_in[0], :], sibling)
        pass_out = lambda k: rc(9 + k, wobuf.at[blk(k), rows_out[0], :], sibling)
        passed_in = lambda k: rc(6 + k, wbuf.at[blk(k), rows_in[1], :], me)
        passed_out = lambda k: rc(9 + k, wobuf.at[blk(k), rows_out[1], :], me)

        def keep(j, n):
            return pltpu.make_async_copy(wbuf.at[j], wfull_o.at[:, pl.ds(j * 1024, 1024)], loc_sems.at[n])

        @pl.when((s == 0) & (i == 0))
        def _():
            for p in range(5):
                src = win_ref.at[pl.ds(p * 256, 256), :] if p < 4 else wout_ref
                load = pltpu.make_async_copy(src, stage, loc_sems.at[4])
                load.start()
                load.wait()
                if p < 4:
                    wbuf[jm, p * 256:(p + 1) * 256, :] = stage[...].astype(BF16)
                else:
                    wobuf[jm] = stage[...].astype(BF16)
            for k in range(2):
                for h in range(2):
                    send_in(k, h).start()
            keep(jm, 0).start()

        def arrive(k):
            if k == 0:
                for kk in range(2):
                    for h in range(2):
                        got_in(kk, h).wait_recv()
                relay(0).start()
                relay(1).start()
            if k == 2:
                got_relay(0).wait_recv()
                got_relay(1).wait_recv()
            pass_in(k).start()
            passed_in(k).wait_recv()
            keep(blk(k), k + 1).start()
            if k == 2:
                for kk in range(3):
                    send_out(kk).start()

        pl.when((s == 1) & (i == 0))(functools.partial(arrive, 0))

        @pl.when((s == 2) & (i == 0))
        def _():
            arrive(1)
            arrive(2)

        tile = pl.ds(pl.multiple_of(i * TT, TT), TT)

        @pl.when(s == 0)
        def _():
            xv = x_ref[...]
            r = lax.rsqrt(jnp.mean(xv * xv, axis=-1, keepdims=True) + EPS)
            hnf = (xv * r) * mw_ref[...]
            hn_all[tile, :] = hnf.astype(BF16)
            hnt_ref[...] = hnf.T.astype(BF16)

        def project(jj):
            hn = hn_all[tile, :]
            lo = _mm(hn, wbuf[jj, :, 0:512])
            hi_cols = _mm(hn, wbuf[jj, :, 512:1024])
            if jj == 0:
                cosf, s1, s2 = _rope_tables(pos_ref[...])
                _perm_store(_rope(lo, cosf, s1, s2), scr, scr2, q1, q4, q16, BF16)
                _perm_store(_rope(hi_cols, cosf, s1, s2), scr, scr2, k1, k4, k16, BF16)
            elif jj == 1:
                _perm_store(lo, scr, scr2, v1, v4, v16, BF16)
                ag[...] = hi_cols.astype(BF16)
            elif jj == 2:
                hq[...] = lo.astype(BF16)
                hf[...] = hi_cols.astype(BF16)
            else:
                hi[...] = lo.astype(BF16)
                hg[...] = hi_cols.astype(BF16)

        def project_block(j):
            for jj in range(4):
                pl.when(j == jj)(functools.partial(project, jj))

        @pl.when(s < 2)
        def _():
            project_block(lax.bitwise_xor(jm, s))

        @pl.when(s == 2)
        def _():
            project_block(lax.bitwise_xor(jm, 2))
            project_block(lax.bitwise_xor(jm, 3))

        @pl.when((s == 2) & (i == NT - 1))
        def _():
            for k in range(3):
                got_out(k).wait_recv()
                pass_out(k).start()
            for k in range(3):
                passed_out(k).wait_recv()
            out = pltpu.make_async_copy(wobuf, woutfull_o, loc_sems.at[4])
            out.start()
            for h in range(2):
                relay(h).wait_send()
                for k in range(2):
                    send_in(k, h).wait_send()
            for k in range(3):
                send_out(k).wait_send()
                pass_in(k).wait_send()
                pass_out(k).wait_send()
            keep(jm, 0).wait()
            for k in range(3):
                keep(blk(k), k + 1).wait()
            out.wait()

    def at_stage_of(jb):
        def index(s, i, jm_ref):
            sa = jnp.minimum(lax.bitwise_xor(jm_ref[0], jb), 2)
            return jnp.where(s < sa, 0, jnp.where(s == sa, i, NT - 1))
        return index

    tok = lambda w, jb: pl.BlockSpec((TT, w), lambda s, i, jm_ref: (at_stage_of(jb)(s, i, jm_ref), 0))
    d4 = lambda jb: pl.BlockSpec((4, TT // 4, AW), lambda s, i, jm_ref: (0, at_stage_of(jb)(s, i, jm_ref), 0))
    d16 = lambda jb: pl.BlockSpec((16, TT // 16, AW), lambda s, i, jm_ref: (0, at_stage_of(jb)(s, i, jm_ref), 0))
    hbm = pl.BlockSpec(memory_space=pltpu.HBM)
    sd = lambda shape, dt: jax.ShapeDtypeStruct(shape, dt)
    in_own_stage = lambda s, i: jnp.where(s == 0, i, NT - 1)
    grid_spec = pltpu.PrefetchScalarGridSpec(
        num_scalar_prefetch=1, grid=(3, NT),
        in_specs=[pl.BlockSpec((TT, D), lambda s, i, jm_ref: (in_own_stage(s, i), 0)),
                  pl.BlockSpec((TT, 1), lambda s, i, jm_ref: (i, 0)),
                  pl.BlockSpec((1, D), lambda s, i, jm_ref: (0, 0)), hbm, hbm],
        out_specs=[pl.BlockSpec((D, TT), lambda s, i, jm_ref: (0, in_own_stage(s, i))),
                   tok(AW, 0), tok(AW, 0), tok(AW, 1), d4(0), d4(0), d4(1), d16(0), d16(0), d16(1),
                   tok(AW, 1), tok(AW, 2), tok(AW, 2), tok(AW, 3), tok(AW, 3), hbm, hbm],
        scratch_shapes=[pltpu.VMEM((4, D, 1024), BF16), pltpu.VMEM((4, 256, D), BF16), pltpu.VMEM((T, D), BF16),
                        pltpu.VMEM((4, TT, 128), F32), pltpu.VMEM((4, TT, 128), F32), pltpu.VMEM((256, 1024), F32),
                        pltpu.SemaphoreType.DMA((18,)),
                        pltpu.SemaphoreType.DMA((18,)), pltpu.SemaphoreType.DMA((6,))])
    return pl.pallas_call(
        body, name="fwd_in", grid_spec=grid_spec,
        out_shape=[sd((D, T), BF16)] + [sd((T, AW), BF16)] * 3 + [sd((4, T // 4, AW), BF16)] * 3
        + [sd((16, T // 16, AW), BF16)] * 3
        + [sd((T, AW), BF16)] * 5 + [sd((D, NCOL), BF16), sd((4, 256, D), BF16)],
        compiler_params=_cp(("arbitrary", "arbitrary")),
    )(jm_arr, x, pos, mixw, w_in, w_out)


def _band_mask(key_axis, nkeys=2 * BLK):
    shape = (nkeys, 2 * BLK) if key_axis == 0 else (2 * BLK, nkeys)
    kj = lax.broadcasted_iota(jnp.int32, shape, key_axis)
    qi = lax.broadcasted_iota(jnp.int32, shape, 1 - key_axis) & (BLK - 1)
    return (kj >= qi) & (kj <= qi + BLK), kj, qi


def _stack_heads(t2, in_a):
    z = jnp.zeros_like(t2)
    return jnp.concatenate([jnp.where(in_a[0], t2, z), jnp.where(in_a[1], t2, z)], axis=0)


def _attn_fwd(q, k, v, nb, name):
    n = 8 if nb >= 8 else 4
    CH = n * BLK
    halo = nb > n

    def body(*refs):
        if halo:
            q_ref, k_ref, v_ref, kp_ref, vp_ref, o_ref, lse_ref = refs
        else:
            q_ref, k_ref, v_ref, o_ref, lse_ref = refs
        lane = lax.broadcasted_iota(jnp.int32, (1, 128), 1)
        in_a = [lane < HEAD, lane >= HEAD]
        band, kj, _ = _band_mask(1)
        thr0 = jnp.where((n * pl.program_id(0)) % nb == 0, BLK, 0) if halo else BLK
        mask0 = band & (kj >= thr0)
        mask_first = band & (kj >= BLK)
        for b in range(n):
            rs = slice(b * BLK, (b + 1) * BLK)
            stat = jnp.zeros((BLK, 128), F32)
            for hp in range(4):
                cs = slice(hp * 128, (hp + 1) * 128)
                q2s = _stack_heads(q_ref[rs, cs], in_a)
                if b == 0:
                    kprev = kp_ref[:, cs] if halo else k_ref[rs, cs]
                    vprev = vp_ref[:, cs] if halo else v_ref[rs, cs]
                    kk = jnp.concatenate([kprev, k_ref[rs, cs]], axis=0)
                    vv = jnp.concatenate([vprev, v_ref[rs, cs]], axis=0)
                    mask = mask0
                else:
                    kk = k_ref[(b - 1) * BLK:(b + 1) * BLK, cs]
                    vv = v_ref[(b - 1) * BLK:(b + 1) * BLK, cs]
                    mask = mask_first if b % nb == 0 else band
                s = jnp.where(mask, _mm_nt(q2s, kk) * SCALE, NEG)
                m = jnp.max(s, axis=-1, keepdims=True)
                p = jnp.exp(s - m)
                l = jnp.sum(p, axis=-1, keepdims=True)
                o = _mm(p.astype(BF16), vv) / l
                lse = m + jnp.log(l)
                o_ref[rs, cs] = jnp.where(in_a[0], o[:BLK], o[BLK:]).astype(BF16)
                stat = jnp.where(lane == 2 * hp, lse[:BLK], stat)
                stat = jnp.where(lane == 2 * hp + 1, lse[BLK:], stat)
            lse_ref[rs, :] = stat

    cur = pl.BlockSpec((CH, AW), lambda i: (i, 0))
    prev = pl.BlockSpec((BLK, AW), lambda i: (jnp.maximum(n * i - 1, 0), 0))
    return pl.pallas_call(
        body, name=name, grid=(T // CH,),
        in_specs=[cur, cur, cur] + ([prev, prev] if halo else []),
        out_specs=[cur, pl.BlockSpec((CH, 128), lambda i: (i, 0))],
        out_shape=[jax.ShapeDtypeStruct((T, AW), BF16), jax.ShapeDtypeStruct((T, 128), F32)],
        compiler_params=_cp(("parallel",)),
    )(*((q, k, v) + ((k, v) if halo else ())))


def _attn_bwd(q, k, v, do, st, nb, name):
    n = 8 if nb >= 8 else 4
    CH = n * BLK
    NBLK = T // BLK
    halo = nb > n

    def body(*refs):
        if halo:
            (q_ref, k_ref, v_ref, do_ref, st_ref, kp_ref, vp_ref, qn_ref, don_ref, stn_ref,
             dq_ref, dk_ref, dv_ref) = refs
        else:
            q_ref, k_ref, v_ref, do_ref, st_ref, dq_ref, dk_ref, dv_ref = refs
        i = pl.program_id(0)
        lane = lax.broadcasted_iota(jnp.int32, (1, 128), 1)
        in_a = [lane < HEAD, lane >= HEAD]
        band, kj, _ = _band_mask(0)
        thr0 = jnp.where((n * i) % nb == 0, BLK, 0) if halo else BLK
        mask0 = band & (kj >= thr0)
        mask_first = band & (kj >= BLK)

        def stat_rows(st_t, hp):
            lse_r = jnp.concatenate([st_t[2 * hp:2 * hp + 1, :], st_t[2 * hp + 1:2 * hp + 2, :]], axis=1)
            dl_r = jnp.concatenate([st_t[8 + 2 * hp:9 + 2 * hp, :], st_t[9 + 2 * hp:10 + 2 * hp, :]], axis=1)
            return lse_r, dl_r

        st_t = [st_ref[b * BLK:(b + 1) * BLK, :].T for b in range(n)]
        if halo:
            nxt_thr = jnp.where((n * i + n) % nb == 0, 2 * BLK, 0)
            _, kj1, qi1 = _band_mask(0, BLK)
            mask_next = kj1 >= qi1 + nxt_thr
            stn_t = stn_ref[...].T

        for hp in range(4):
            cs = slice(hp * 128, (hp + 1) * 128)
            kb = [k_ref[b * BLK:(b + 1) * BLK, cs] for b in range(n)]
            vb = [v_ref[b * BLK:(b + 1) * BLK, cs] for b in range(n)]
            dk_acc = [jnp.zeros((BLK, 128), F32) for _ in range(n)]
            dv_acc = [jnp.zeros((BLK, 128), F32) for _ in range(n)]
            for b in range(n):
                rs = slice(b * BLK, (b + 1) * BLK)
                q2s = _stack_heads(q_ref[rs, cs], in_a)
                do2s = _stack_heads(do_ref[rs, cs], in_a)
                if b == 0:
                    kprev = kp_ref[:, cs] if halo else kb[0]
                    vprev = vp_ref[:, cs] if halo else vb[0]
                    mask = mask0
                else:
                    kprev, vprev, mask = kb[b - 1], vb[b - 1], (mask_first if b % nb == 0 else band)
                kk = jnp.concatenate([kprev, kb[b]], axis=0)
                vv = jnp.concatenate([vprev, vb[b]], axis=0)
                lse_r, dl_r = stat_rows(st_t[b], hp)
                s_t = jnp.where(mask, _mm_nt(kk, q2s) * SCALE, NEG)
                p_t = jnp.exp(s_t - lse_r)
                ds_t = (p_t * (_mm_nt(vv, do2s) - dl_r)).astype(BF16)
                dkk = _mm(ds_t, q2s) * SCALE
                dvv = _mm(p_t.astype(BF16), do2s)
                dqs = _mm_tn(ds_t, kk) * SCALE
                dq_ref[rs, cs] = jnp.where(in_a[0], dqs[:BLK], dqs[BLK:]).astype(BF16)
                dk_acc[b] += dkk[BLK:]
                dv_acc[b] += dvv[BLK:]
                if b > 0:
                    dk_acc[b - 1] += dkk[:BLK]
                    dv_acc[b - 1] += dvv[:BLK]
            if halo:
                q2s = _stack_heads(qn_ref[:, cs], in_a)
                do2s = _stack_heads(don_ref[:, cs], in_a)
                lse_r, dl_r = stat_rows(stn_t, hp)
                s_t = jnp.where(mask_next, _mm_nt(kb[n - 1], q2s) * SCALE, NEG)
                p_t = jnp.exp(s_t - lse_r)
                ds_t = (p_t * (_mm_nt(vb[n - 1], do2s) - dl_r)).astype(BF16)
                dk_acc[n - 1] += _mm(ds_t, q2s) * SCALE
                dv_acc[n - 1] += _mm(p_t.astype(BF16), do2s)
            for b in range(n):
                dk_ref[b * BLK:(b + 1) * BLK, cs] = dk_acc[b].astype(BF16)
                dv_ref[b * BLK:(b + 1) * BLK, cs] = dv_acc[b].astype(BF16)

    cur = pl.BlockSpec((CH, AW), lambda i: (i, 0))
    cur_st = pl.BlockSpec((CH, 128), lambda i: (i, 0))
    prev = pl.BlockSpec((BLK, AW), lambda i: (jnp.maximum(n * i - 1, 0), 0))
    nxt = pl.BlockSpec((BLK, AW), lambda i: (jnp.minimum(n * i + n, NBLK - 1), 0))
    nxt_st = pl.BlockSpec((BLK, 128), lambda i: (jnp.minimum(n * i + n, NBLK - 1), 0))
    ins = [cur] * 4 + [cur_st] + ([prev, prev, nxt, nxt, nxt_st] if halo else [])
    args = (q, k, v, do, st) + ((k, v, q, do, st) if halo else ())
    return pl.pallas_call(
        body, name=name, grid=(T // CH,),
        in_specs=ins,
        out_specs=[cur] * 3,
        out_shape=[jax.ShapeDtypeStruct((T, AW), BF16)] * 3,
        compiler_params=_cp(("parallel",)),
    )(*args)


TH = 256
NCH = TH // CHUNK


def _hgrn_common(hq_ref, hf_ref, lbr_ref, tri_ref):
    r0 = lbr_ref[0:1, :]
    r1 = lbr_ref[1:2, :]
    mx = jnp.maximum(r0, r1)
    e0 = jnp.exp(r0 - mx)
    e1 = jnp.exp(r1 - mx)
    lb = e0 / (e0 + e1)
    hqv = hq_ref[...].astype(F32)
    sq = _sigmoid(hqv)
    qv = hqv * sq
    sf = _sigmoid(hf_ref[...].astype(F32))
    f = lb + (1.0 - lb) * sf
    kv = 1.0 - f
    g = jnp.log(f)
    cum = _mm_exact_l(tri_ref[...], g)
    dec = jnp.exp(jnp.concatenate([cum[c * CHUNK + CHUNK - 1:(c + 1) * CHUNK, :] for c in range(NCH)], axis=0))
    decb = jnp.concatenate([jnp.broadcast_to(dec[c:c + 1, :], (CHUNK, HW)) for c in range(NCH)], axis=0)
    ea = jnp.exp(cum)
    ena = jnp.exp(-cum)
    eend = decb * ena
    return dict(lb=lb, hq=hqv, sq=sq, q=qv, sf=sf, f=f, k=kv, cum=cum, ea=ea, ena=ena, eend=eend,
                qd=qv * ea, ki=kv * ena, ke=kv * eend, dec=dec)


def _tri_mask(transposed=False):
    ti = lax.broadcasted_iota(jnp.int32, (TH, TH), 1 if transposed else 0)
    si = lax.broadcasted_iota(jnp.int32, (TH, TH), 0 if transposed else 1)
    return (si <= ti) & ((si // CHUNK) == (ti // CHUNK))


def _hgrn_fwd(hq, hf, hi, lbr, tri):
    def body(hq_ref, hf_ref, hi_ref, lbr_ref, tri_ref, rec_ref, sall_ref, st_scr):
        @pl.when(pl.program_id(0) == 0)
        def _():
            st_scr[...] = jnp.zeros_like(st_scr)

        w = _hgrn_common(hq_ref, hf_ref, lbr_ref, tri_ref)
        qd, ki, ke = w["qd"].astype(BF16), w["ki"].astype(BF16), w["ke"].astype(BF16)
        dec = w["dec"]
        vb = hi_ref[...]
        causal = _tri_mask()
        for h in range(4):
            cs = slice(h * 128, (h + 1) * 128)
            att = jnp.where(causal, _mm_nt(qd[:, cs], ki[:, cs]), 0.0)
            o_intra = _mm(att.astype(BF16), vb[:, cs])
            st = st_scr[:, cs]
            for c in range(NCH):
                rs = slice(c * CHUNK, (c + 1) * CHUNK)
                sall_ref[c, :, cs] = st
                rec_ref[rs, cs] = (o_intra[rs] + _mm_nt(qd[rs, cs], st.astype(BF16))).astype(BF16)
                st = dec[c:c + 1, cs] * st + _mm_tn(vb[rs, cs], ke[rs, cs])
            st_scr[:, cs] = st

    tok = pl.BlockSpec((TH, HW), lambda i: (i, 0))
    return pl.pallas_call(
        body, name="hgrn_fwd", grid=(T // TH,),
        in_specs=[tok, tok, tok, pl.BlockSpec((2, HW), lambda i: (0, 0)), pl.BlockSpec((TH, TH), lambda i: (0, 0))],
        out_specs=[tok, pl.BlockSpec((NCH, 128, HW), lambda i: (i, 0, 0))],
        out_shape=[jax.ShapeDtypeStruct((T, HW), BF16), jax.ShapeDtypeStruct((T // CHUNK, 128, HW), F32)],
        scratch_shapes=[pltpu.VMEM((128, HW), F32)],
        compiler_params=_cp(("arbitrary",)),
    )(hq, hf, hi, lbr, tri)


def _hgrn_bwd(hq, hf, hi, lbr, tri, trit, drec, sall, dhg, rout, routb):
    NT = T // TH

    def body(hq_ref, hf_ref, hi_ref, lbr_ref, tri_ref, trit_ref, do_ref, sall_ref, dhg_ref, rout_r, routb_r,
             dph_ref, small_ref, pout_o, poutr_o,
             dst_scr, dlb_scr, dqd_scr, dki_scr, dke_scr, dlast_scr, send_sems, recv_sems, loc_sems):
        step = pl.program_id(0)
        loc, rem = _chip_copies(_w_out_piece, rout_r, routb_r, pout_o, poutr_o, send_sems, recv_sems,
                                loc_sems.at[0])

        @pl.when(step == 0)
        def _():
            dst_scr[...] = jnp.zeros_like(dst_scr)
            dlb_scr[...] = jnp.zeros_like(dlb_scr)
            for cp in loc + rem:
                cp.start()

        w = _hgrn_common(hq_ref, hf_ref, lbr_ref, tri_ref)
        qd, ki, ke = w["qd"].astype(BF16), w["ki"].astype(BF16), w["ke"].astype(BF16)
        dec = w["dec"]
        vb = hi_ref[...]
        dob = do_ref[...].astype(BF16)
        causal = _tri_mask()
        causal_t = _tri_mask(transposed=True)
        for h in range(4):
            cs = slice(h * 128, (h + 1) * 128)
            att_t = jnp.where(causal_t, _mm_nt(ki[:, cs], qd[:, cs]), 0.0).astype(BF16)
            datt_t = jnp.where(causal_t, _mm_nt(vb[:, cs], dob[:, cs]), 0.0).astype(BF16)
            datt = jnp.where(causal, _mm_nt(dob[:, cs], vb[:, cs]), 0.0).astype(BF16)
            dv_intra = _mm(att_t, dob[:, cs])
            dqd_intra = _mm(datt, ki[:, cs])
            dki_scr[:, cs] = _mm(datt_t, qd[:, cs])
            dst = dst_scr[:, cs]
            for c in reversed(range(NCH)):
                rs = slice(c * CHUNK, (c + 1) * CHUNK)
                dec_c = dec[c:c + 1, :]
                st = sall_ref[c, :, cs]
                dstb = dst.astype(BF16)
                dph_ref[rs, 2 * HW + h * 128:2 * HW + (h + 1) * 128] = (
                    dv_intra[rs] + _mm_nt(ke[rs, cs], dstb)).astype(BF16)
                dqd_scr[rs, cs] = dqd_intra[rs] + _mm(dob[rs, cs], st.astype(BF16))
                dke_scr[rs, cs] = _mm(vb[rs, cs], dstb)
                ddec = jnp.sum(dst * st, axis=0, keepdims=True)
                dlast_scr[c:c + 1, cs] = ddec * dec_c[:, cs]
                dst = dec_c[:, cs] * dst + _mm_tn(dob[rs, cs], qd[rs, cs])
            dst_scr[:, cs] = dst
        dqd, dki, dke = dqd_scr[...], dki_scr[...], dke_scr[...]
        dq = dqd * w["ea"]
        dk = dki * w["ena"] + dke * w["eend"]
        dcum = dqd * w["qd"] - dki * w["ki"] - dke * w["ke"]
        dkeke = dke * w["ke"]
        dlastb = jnp.concatenate(
            [jnp.broadcast_to(dlast_scr[c:c + 1, :] + jnp.sum(dkeke[c * CHUNK:(c + 1) * CHUNK], axis=0, keepdims=True),
                              (CHUNK, HW)) for c in range(NCH)], axis=0)
        dg = _mm_exact_l(trit_ref[...], dcum) + dlastb
        df = dg / w["f"] - dk
        lb, sf, sq = w["lb"], w["sf"], w["sq"]
        dph_ref[:, HW:2 * HW] = (df * (1.0 - lb) * sf * (1.0 - sf)).astype(BF16)
        dph_ref[:, 0:HW] = (dq * (sq * (1.0 + w["hq"] * (1.0 - sq)))).astype(BF16)
        dph_ref[:, 3 * HW:4 * HW] = dhg_ref[...]
        dlb_scr[...] += jnp.sum(df * (1.0 - sf), axis=0, keepdims=True)

        @pl.when(step == NT - 1)
        def _():
            gr = dlb_scr[...] * lb * (1.0 - lb)
            small_ref[...] = jnp.zeros_like(small_ref)
            small_ref[0:1, 0:HW] = gr
            small_ref[1:2, 0:HW] = -gr
            for cp in rem:
                cp.wait_recv()
            for cp in rem:
                cp.wait_send()
            for cp in loc:
                cp.wait()

    tok = pl.BlockSpec((TH, HW), lambda i: (NT - 1 - i, 0))
    const = lambda shape: pl.BlockSpec(shape, lambda i: (0,) * len(shape))
    hbm = pl.BlockSpec(memory_space=pltpu.HBM)
    return pl.pallas_call(
        body, name="hgrn_bwd", grid=(NT,),
        in_specs=[tok, tok, tok, const((2, HW)), const((TH, TH)), const((TH, TH)), tok,
                  pl.BlockSpec((NCH, 128, HW), lambda i: (NT - 1 - i, 0, 0)), tok, hbm, hbm],
        out_specs=[pl.BlockSpec((TH, NCOL // 2), lambda i: (NT - 1 - i, 0)), const((8, D)), hbm, hbm],
        out_shape=[jax.ShapeDtypeStruct((T, NCOL // 2), BF16), jax.ShapeDtypeStruct((8, D), F32),
                   jax.ShapeDtypeStruct((128, D), F32), jax.ShapeDtypeStruct((3, 128, D), BF16)],
        scratch_shapes=[pltpu.VMEM((128, HW), F32), pltpu.VMEM((1, HW), F32), pltpu.VMEM((TH, HW), F32),
                        pltpu.VMEM((TH, HW), F32), pltpu.VMEM((TH, HW), F32), pltpu.VMEM((8, HW), F32),
                        pltpu.SemaphoreType.DMA((3,)), pltpu.SemaphoreType.DMA((3,)), pltpu.SemaphoreType.DMA((1,))],
        compiler_params=_cp(("arbitrary",)),
    )(hq, hf, hi, lbr, tri, trit, drec, sall, dhg, rout, routb)


def _fwd_out(o1, o4, o16, l1, l4, l16, rec, ag, hg, x, tgt, anw, hnw, fnw, wout_full, gmat, emat, selmat):
    TT = 256

    def body(o1_r, o4_r, o16_r, l1_r, l4_r, l16_r, rec_r, ag_r, hg_r, x_r, tgt_r, anw_r, hnw_r, fnw_r, wo_r, g_r,
             e_r, sel_r, dx2_o, do1_o, do4_o, do16_o, st1_o, st4_o, st16_o, drec_o, dag_o, dhg_o,
             rout_o, routb_o, small_o, scr_a, scr_b, scr_c, gwout_o, rbuf, send_sems, recv_sems):
        @pl.when(pl.program_id(0) == 0)
        def _():
            gwout_o[...] = jnp.zeros_like(gwout_o)
            small_o[...] = jnp.zeros_like(small_o)

        def unperm(r4, r16):
            return _unperm_load(r4, r16, scr_a, scr_b, scr_c)

        def perm_out(val, p1, p4, p16, dt):
            _perm_store(val, scr_a, scr_b, p1, p4, p16, dt)

        o4u, o16u = unperm(o4_r, o16_r)
        l4c, l16c = unperm(l4_r, l16_r)
        l1c = l1_r[...]
        mxc = jnp.maximum(jnp.maximum(l1c, l4c), l16c)
        w1c, w4c, w16c = jnp.exp(l1c - mxc), jnp.exp(l4c - mxc), jnp.exp(l16c - mxc)
        denc = w1c + w4c + w16c
        lane = lax.broadcasted_iota(jnp.int32, (1, 128), 1)
        lse_c = jnp.where(lane < 8, mxc + jnp.log(denc), 0.0)
        em = e_r[...]
        wn1 = _mm_exact_r(w1c / denc, em)
        wn4 = _mm_exact_r(w4c / denc, em)
        o1v = o1_r[...].astype(F32)
        attn = wn1 * o1v + wn4 * o4u + (1.0 - wn1 - wn4) * o16u
        gm = g_r[...]

        def head_mean_a(t):
            return jnp.concatenate([_mm_exact_r(t[:, :256], gm), _mm_exact_r(t[:, 256:], gm)], axis=1)

        def head_mean_h(t):
            return jnp.concatenate(
                [jnp.broadcast_to(jnp.mean(t[:, h * 128:(h + 1) * 128], axis=-1, keepdims=True), (TT, 128))
                 for h in range(4)], axis=1)

        rs_a = lax.rsqrt(head_mean_a(attn * attn) + EPS)
        n_a = attn * rs_a
        agv = ag_r[...].astype(F32)
        sg_a = _sigmoid(agv)
        si_a = agv * sg_a
        anw_v = anw_r[...]
        y_a = (n_a * anw_v) * si_a
        recv = rec_r[...].astype(F32)
        rs_h = lax.rsqrt(head_mean_h(recv * recv) + EPS)
        n_h = recv * rs_h
        hgv = hg_r[...].astype(F32)
        sg_h = _sigmoid(hgv)
        si_h = hgv * sg_h
        hnw_v = hnw_r[...]
        y_h = (n_h * hnw_v) * si_h
        mixed = jnp.concatenate([y_a, y_h], axis=1).astype(BF16)
        xv = x_r[...]
        x2 = xv + _mm(mixed, wo_r[...])
        r2 = lax.rsqrt(jnp.mean(x2 * x2, axis=-1, keepdims=True) + EPS)
        fnw_v = fnw_r[...]
        xn = x2 * r2
        err = xn * fnw_v - tgt_r[...]
        small_o[2:3, :] += 0.5 * jnp.sum(jnp.mean(err * err, axis=-1, keepdims=True), axis=0, keepdims=True)
        dy = err * (1.0 / D)
        small_o[0:1, :] += jnp.sum(dy * xn, axis=0, keepdims=True)
        dyw = dy * fnw_v
        dx2 = r2 * dyw - x2 * ((r2 * r2 * r2) * jnp.mean(dyw * x2, axis=-1, keepdims=True))
        dx2_o[...] = dx2
        dx2b = dx2.astype(BF16)
        gwout_o[...] += _mm_tn(mixed, dx2b)
        dmix = _mm_nt(dx2b, wo_r[...])
        dm_a, dm_h = dmix[:, :AW], dmix[:, AW:]
        dag_o[...] = (dm_a * (n_a * anw_v) * (sg_a * (1.0 + agv * (1.0 - sg_a)))).astype(BF16)
        dn_a = dm_a * anw_v * si_a
        small_o[1:2, 0:AW] += jnp.sum(dm_a * n_a * si_a, axis=0, keepdims=True)
        dattn = rs_a * (dn_a - n_a * head_mean_a(dn_a * n_a))
        perm_out(dattn, do1_o, do4_o, do16_o, BF16)
        stats = lse_c + _mm_exact_r(dattn * attn, sel_r[...])
        perm_out(stats, st1_o, st4_o, st16_o, F32)
        dhg_o[...] = (dm_h * (n_h * hnw_v) * (sg_h * (1.0 + hgv * (1.0 - sg_h)))).astype(BF16)
        dn_h = dm_h * hnw_v * si_h
        small_o[1:2, AW:] += jnp.sum(dm_h * n_h * si_h, axis=0, keepdims=True)
        drec_o[...] = (rs_h * (dn_h - n_h * head_mean_h(dn_h * n_h))).astype(BF16)

        @pl.when(pl.program_id(0) == T // TT - 1)
        def _():
            x, y, c = lax.axis_index("x"), lax.axis_index("y"), lax.axis_index("c")
            cps = [pltpu.make_async_remote_copy(
                src_ref=gwout_o.at[pl.ds(pl.multiple_of(j * 256 + (1 - c) * 128, 128), 128), :], dst_ref=rbuf.at[j],
                send_sem=send_sems.at[j], recv_sem=recv_sems.at[j], device_id=(x, y, 1 - c), device_id_type=MESH)
                for j in range(4)]
            for cp in cps:
                cp.start()
            for j, cp in enumerate(cps):
                cp.wait_recv()
                red = gwout_o[pl.ds(pl.multiple_of(j * 256 + c * 128, 128), 128), :] + rbuf[j]
                rout_o[j * 128:(j + 1) * 128, :] = red
                routb_o[j * 128:(j + 1) * 128, :] = red.astype(BF16)
            for cp in cps:
                cp.wait_send()

    tok = lambda w: pl.BlockSpec((TT, w), lambda i: (i, 0))
    d4 = pl.BlockSpec((4, TT // 4, AW), lambda i: (0, i, 0))
    d16 = pl.BlockSpec((16, TT // 16, AW), lambda i: (0, i, 0))
    const = lambda shape: pl.BlockSpec(shape, lambda i: (0,) * len(shape))
    sd = lambda shape, dt: jax.ShapeDtypeStruct(shape, dt)
    c4 = pl.BlockSpec((4, TT // 4, 128), lambda i: (0, i, 0))
    c16 = pl.BlockSpec((16, TT // 16, 128), lambda i: (0, i, 0))
    p3 = lambda w, dt: [sd((T, w), dt), sd((4, T // 4, w), dt), sd((16, T // 16, w), dt)]
    return pl.pallas_call(
        body, name="fwd_out", grid=(T // TT,),
        in_specs=[tok(AW), d4, d16, tok(128), c4, c16, tok(AW), tok(AW), tok(AW), tok(D), tok(D),
                  const((1, AW)), const((1, HW)), const((1, D)), const((D, D)), const((256, 256)),
                  const((128, AW)), const((AW, 128))],
        out_specs=[tok(D)] + [tok(AW), d4, d16] + [tok(128), c4, c16] + [tok(AW)] * 3
        + [const((512, D)), const((512, D)), const((8, D))],
        out_shape=[sd((T, D), F32)] + p3(AW, BF16) + p3(128, F32)
        + [sd((T, AW), BF16), sd((T, AW), BF16), sd((T, AW), BF16), sd((512, D), F32), sd((512, D), BF16),
           sd((8, D), F32)],
        scratch_shapes=[pltpu.VMEM((4, TT, 128), F32)] * 3 + [pltpu.VMEM((D, D), F32),
                        pltpu.VMEM((4, 128, D), F32), pltpu.SemaphoreType.DMA((4,)), pltpu.SemaphoreType.DMA((4,))],
        compiler_params=_cp(("arbitrary",)),
    )(o1, o4, o16, l1, l4, l16, rec, ag, hg, x, tgt, anw, hnw, fnw, wout_full, gmat, emat, selmat)


def _dproj_build(dq, dk, dv, dag, pos):
    TT = 512

    def body(dq1, dq4, dq16, dk1, dk4, dk16, dv1, dv4, dv16, dag_r, pos_r, dproj_o, scr_a, scr_b, scr_c):
        def unperm_sum(r1, r4, r16):
            u4, u16 = _unperm_load(r4, r16, scr_a, scr_b, scr_c)
            return r1[...] + u4 + u16

        cosf, s1, s2 = _rope_tables(pos_r[...])
        dproj_o[:, 0:512] = _rope_bwd(unperm_sum(dq1, dq4, dq16), cosf, s1, s2).astype(BF16)
        dproj_o[:, 512:1024] = _rope_bwd(unperm_sum(dk1, dk4, dk16), cosf, s1, s2).astype(BF16)
        dproj_o[:, 1024:1536] = unperm_sum(dv1, dv4, dv16).astype(BF16)
        dproj_o[:, 1536:2048] = dag_r[...]

    tok = lambda w: pl.BlockSpec((TT, w), lambda i: (i, 0))
    d4 = pl.BlockSpec((4, TT // 4, AW), lambda i: (0, i, 0))
    d16 = pl.BlockSpec((16, TT // 16, AW), lambda i: (0, i, 0))
    return pl.pallas_call(
        body, name="dproj_build", grid=(T // TT,),
        in_specs=[tok(AW), d4, d16] * 3 + [tok(AW), tok(1)],
        out_specs=tok(NCOL // 2),
        out_shape=jax.ShapeDtypeStruct((T, NCOL // 2), BF16),
        scratch_shapes=[pltpu.VMEM((4, TT, 128), F32)] * 3,
        compiler_params=_cp(("parallel",)),
    )(*dq, *dk, *dv, dag, pos)


def _bwd_x(dproj_a, dproj_h, x, dx2, mixw, w_full, rin, rinb, small4, small6, pout_own, pout_rem):
    TT = 256
    NT = T // TT

    def body(dpa_r, dph_r, x_r, dx2_r, mw_r, w_r, rin_r, rinb_r, s4_r, s6_r, poo_r, por_r,
             gx_o, pin_o, pinr_o, sall_o, fin_o, fout_o, sbuf, v_own, v_rem, vo_own, vo_rem, sin, sout, got_in,
             got_out, send_sems, recv_sems, loc_sems, share_send, share_recv, fin_sems):
        i = pl.program_id(0)
        loc, rem = _chip_copies(_w_in_piece, rin_r, rinb_r, pin_o, pinr_o, send_sems, recv_sems, loc_sems.at[0])

        @pl.when(i == 0)
        def _():
            sbuf[...] = jnp.zeros_like(sbuf)
            for cp in loc + rem:
                cp.start()

        dhn = _mm_nt(dpa_r[...], w_r[:, 0:NCOL // 2]) + _mm_nt(dph_r[...], w_r[:, NCOL // 2:NCOL])
        xv = x_r[...]
        r = lax.rsqrt(jnp.mean(xv * xv, axis=-1, keepdims=True) + EPS)
        dxw = dhn * mw_r[...]
        gx_o[...] = dx2_r[...] + r * dxw - xv * ((r * r * r) * jnp.mean(dxw * xv, axis=-1, keepdims=True))
        sbuf[16:17, :] += jnp.sum(dhn * (xv * r), axis=0, keepdims=True)

        @pl.when(i == NT - 1)
        def _():
            sbuf[0:8, :] = s4_r[...]
            sbuf[8:16, :] = s6_r[...]
            sloc, srem = _small_copies(sbuf, sall_o, send_sems, recv_sems, loc_sems.at[1])
            for cp in sloc + srem:
                cp.start()
            for cp in rem + srem:
                cp.wait_recv()
            for cp in rem + srem:
                cp.wait_send()
            for cp in loc + sloc:
                cp.wait()
            mx, my, c = lax.axis_index("x"), lax.axis_index("y"), lax.axis_index("c")
            loads = [pltpu.make_async_copy(pin_o, v_own, fin_sems.at[0]),
                     pltpu.make_async_copy(pinr_o, v_rem, fin_sems.at[1]),
                     pltpu.make_async_copy(poo_r, vo_own, fin_sems.at[2]),
                     pltpu.make_async_copy(por_r, vo_rem, fin_sems.at[3])]
            for cp in loads:
                cp.start()
            for cp in loads:
                cp.wait()
            sout[...] = ((vo_own[...] + vo_rem[0].astype(F32)) + vo_rem[1].astype(F32)) + vo_rem[2].astype(F32)
            sin[...] = ((v_own[...] + v_rem[0].astype(F32)) + v_rem[1].astype(F32)) + v_rem[2].astype(F32)
            swap = [pltpu.make_async_remote_copy(src_ref=sin, dst_ref=got_in, send_sem=share_send.at[0],
                                                 recv_sem=share_recv.at[0], device_id=(mx, my, 1 - c),
                                                 device_id_type=MESH),
                    pltpu.make_async_remote_copy(src_ref=sout, dst_ref=got_out, send_sem=share_send.at[1],
                                                 recv_sem=share_recv.at[1], device_id=(mx, my, 1 - c),
                                                 device_id_type=MESH)]
            for cp in swap:
                cp.start()
            mine = [pltpu.make_async_copy(sin, fin_o.at[c], fin_sems.at[0]),
                    pltpu.make_async_copy(sout, fout_o.at[c], fin_sems.at[1])]
            for cp in mine:
                cp.start()
            for cp in swap:
                cp.wait_recv()
            theirs = [pltpu.make_async_copy(got_in, fin_o.at[1 - c], fin_sems.at[2]),
                      pltpu.make_async_copy(got_out, fout_o.at[1 - c], fin_sems.at[3])]
            for cp in theirs:
                cp.start()
            for cp in swap:
                cp.wait_send()
            for cp in mine + theirs:
                cp.wait()

    tok = lambda w: pl.BlockSpec((TT, w), lambda i: (i, 0))
    const = lambda shape: pl.BlockSpec(shape, lambda i: (0,) * len(shape))
    hbm = pl.BlockSpec(memory_space=pltpu.HBM)
    return pl.pallas_call(
        body, name="bwd_x", grid=(NT,),
        in_specs=[tok(NCOL // 2), tok(NCOL // 2), tok(D), tok(D), const((1, D)), const((D, NCOL)), hbm, hbm,
                  const((8, D)), const((8, D)), hbm, hbm],
        out_specs=[tok(D), hbm, hbm, hbm, hbm, hbm],
        out_shape=[jax.ShapeDtypeStruct((T, D), F32),
                   jax.ShapeDtypeStruct((512, 1024), F32), jax.ShapeDtypeStruct((3, 512, 1024), BF16),
                   jax.ShapeDtypeStruct((8, 24, D), F32),
                   jax.ShapeDtypeStruct((2, 512, 1024), F32), jax.ShapeDtypeStruct((2, 128, D), F32)],
        scratch_shapes=[pltpu.VMEM((24, D), F32),
                        pltpu.VMEM((512, 1024), F32), pltpu.VMEM((3, 512, 1024), BF16),
                        pltpu.VMEM((128, D), F32), pltpu.VMEM((3, 128, D), BF16),
                        pltpu.VMEM((512, 1024), F32), pltpu.VMEM((128, D), F32),
                        pltpu.VMEM((512, 1024), F32), pltpu.VMEM((128, D), F32),
                        pltpu.SemaphoreType.DMA((10,)), pltpu.SemaphoreType.DMA((10,)), pltpu.SemaphoreType.DMA((2,)),
                        pltpu.SemaphoreType.DMA((2,)), pltpu.SemaphoreType.DMA((2,)), pltpu.SemaphoreType.DMA((4,))],
        compiler_params=_cp(("arbitrary",)),
    )(dproj_a, dproj_h, x, dx2, mixw, w_full, rin, rinb, small4, small6, pout_own, pout_rem)


def _grad_w_in(hn, dproj_a, dproj_h):
    TK = 2048
    NK = T // TK

    def body(hnt_r, dpa_r, dph_r, rin_o, rinb_o, acc, rbuf, obuf, obufb, send_sems, recv_sems, wb_sems):
        j = pl.program_id(0)
        kk = pl.program_id(1)
        x, y, c = lax.axis_index("x"), lax.axis_index("y"), lax.axis_index("c")
        mine = pl.ds(pl.multiple_of(c * 512, 512), 512)
        theirs = pl.ds(pl.multiple_of((1 - c) * 512, 512), 512)

        def send(jj):
            return pltpu.make_async_remote_copy(
                src_ref=acc.at[jj % 2, theirs, :], dst_ref=rbuf.at[jj], send_sem=send_sems.at[jj],
                recv_sem=recv_sems.at[jj], device_id=(x, y, 1 - c), device_id_type=MESH)

        def writeback(jj):
            cols = pl.ds(jj * 1024, 1024)
            return [pltpu.make_async_copy(obuf.at[jj % 2], rin_o.at[:, cols], wb_sems.at[jj % 2]),
                    pltpu.make_async_copy(obufb.at[jj % 2], rinb_o.at[:, cols], wb_sems.at[2 + jj % 2])]

        def wait_writeback(jj):
            for cp in writeback(jj):
                cp.wait()

        def finalize(jj):
            send(jj).wait_recv()
            red = acc[jj % 2, mine, :] + rbuf[jj]
            obuf[jj % 2] = red
            obufb[jj % 2] = red.astype(BF16)
            for cp in writeback(jj):
                cp.start()

        prod = _mm(hnt_r[...], jnp.where(j < 2, dpa_r[...], dph_r[...]))

        @pl.when(kk == 0)
        def _():
            for jj in (2, 3):
                @pl.when(j == jj)
                def _():
                    send(jj - 2).wait_send()
            acc[j % 2] = prod

        @pl.when(kk > 0)
        def _():
            acc[j % 2] += prod

        @pl.when(kk == NK - 1)
        def _():
            for jj in range(4):
                @pl.when(j == jj)
                def _():
                    send(jj).start()
                    if jj in (1, 2):
                        finalize(jj - 1)
                    if jj == 3:
                        wait_writeback(0)
                        finalize(2)
                        wait_writeback(1)
                        finalize(3)
                        wait_writeback(2)
                        wait_writeback(3)
                        send(2).wait_send()
                        send(3).wait_send()

    hbm = pl.BlockSpec(memory_space=pltpu.HBM)
    return pl.pallas_call(
        body, name="grad_w_in", grid=(4, NK),
        in_specs=[pl.BlockSpec((D, TK), lambda j, kk: (0, kk)),
                  pl.BlockSpec((TK, 1024), lambda j, kk: (jnp.where(j < 2, kk, NK - 1), jnp.minimum(j, 1))),
                  pl.BlockSpec((TK, 1024), lambda j, kk: (jnp.where(j < 2, 0, kk), jnp.maximum(j - 2, 0)))],
        out_specs=[hbm, hbm],
        out_shape=[jax.ShapeDtypeStruct((512, NCOL), F32), jax.ShapeDtypeStruct((512, NCOL), BF16)],
        scratch_shapes=[pltpu.VMEM((2, D, 1024), F32), pltpu.VMEM((4, 512, 1024), F32), pltpu.VMEM((2, 512, 1024), F32),
                        pltpu.VMEM((2, 512, 1024), BF16),
                        pltpu.SemaphoreType.DMA((4,)), pltpu.SemaphoreType.DMA((4,)), pltpu.SemaphoreType.DMA((4,))],
        compiler_params=_cp(("arbitrary", "arbitrary")),
    )(hn, dproj_a, dproj_h)


def _w_in_piece(ref, j):
    return ref.at[:, pl.ds(j * 1024, 1024)]


def _w_out_piece(ref, j):
    return ref.at[pl.ds(j * 128, 128), :]


def _chip_copies(piece, src_r, srcb_r, own_o, rem_o, send_sems, recv_sems, loc_sem):
    x, y, c = lax.axis_index("x"), lax.axis_index("y"), lax.axis_index("c")
    chips = [(1 - x, y), (x, 1 - y), (1 - x, 1 - y)]
    loc = [pltpu.make_async_copy(piece(src_r, 2 * x + y), own_o, loc_sem)]
    rem = [pltpu.make_async_remote_copy(
        src_ref=piece(srcb_r, 2 * px + py), dst_ref=rem_o.at[k], send_sem=send_sems.at[k],
        recv_sem=recv_sems.at[k], device_id=(px, py, c), device_id_type=MESH) for k, (px, py) in enumerate(chips)]
    return loc, rem


def _small_copies(small_r, sall_o, send_sems, recv_sems, loc_sem):
    x, y, c = lax.axis_index("x"), lax.axis_index("y"), lax.axis_index("c")
    me = 4 * x + 2 * y + c
    loc = [pltpu.make_async_copy(small_r, sall_o.at[me], loc_sem)]
    rem = []
    k = 3
    for fx in range(2):
        for fy in range(2):
            for fc in range(2):
                if fx or fy or fc:
                    peer = (1 - x if fx else x, 1 - y if fy else y, 1 - c if fc else c)
                    rem.append(pltpu.make_async_remote_copy(
                        src_ref=small_r, dst_ref=sall_o.at[me], send_sem=send_sems.at[k],
                        recv_sem=recv_sems.at[k], device_id=peer, device_id_type=MESH))
                    k += 1
    return loc, rem


def _adamw_math(w, g, m, v):
    m = B1 * m + (1.0 - B1) * g
    v = B2 * v + (1.0 - B2) * (g * g)
    m_hat = m / (1.0 - B1 ** STEP)
    v_hat = v / (1.0 - B2 ** STEP)
    delta = -LR * (m_hat / (jnp.sqrt(v_hat) + AEPS) + WD * w)
    return delta, m, v


def _adamw(big_in, big_out, sall, params):
    def body(*refs):
        wi, gi, mi, vi, wo, go, mo, vo, sall_r = refs[:9]
        ins = refs[9:24]
        di_o, mi_o, vi_o, do_o, mo_o, vo_o = refs[24:30]
        outs = refs[30:]
        d, mm, vv = _adamw_math(wi[...], gi[...], mi[...], vi[...])
        di_o[...] = d
        mi_o[...] = mm
        vi_o[...] = vv

        @pl.when(pl.program_id(0) == 0)
        def _():
            d, mm, vv = _adamw_math(wo[...], go[...], mo[...], vo[...])
            do_o[...] = d
            mo_o[...] = mm
            vo_o[...] = vv
            tot = sall_r[0]
            for dv in range(1, 8):
                tot = tot + sall_r[dv]
            grads = [tot[16:17, :], tot[1:2, 0:AW], tot[1:2, AW:], tot[8:10, 0:HW], tot[0:1, :]]
            outs[0][...] = tot[2:3, 0:1]
            for p in range(5):
                w_r, m_r, v_r = ins[3 * p:3 * p + 3]
                g = grads[p]
                d, mm, vv = _adamw_math(w_r[...], g, m_r[...], v_r[...])
                outs[1 + 4 * p][...] = g
                outs[2 + 4 * p][...] = d
                outs[3 + 4 * p][...] = mm
                outs[4 + 4 * p][...] = vv

    flat = [a for p in params for a in p]
    shapes = [jax.ShapeDtypeStruct((D, 1024), F32)] * 3 + [jax.ShapeDtypeStruct((256, D), F32)] * 3
    shapes += [jax.ShapeDtypeStruct((1, 1), F32)]
    for p in params:
        shapes += [jax.ShapeDtypeStruct(p[0].shape, F32)] * 4
    vm = pl.BlockSpec(memory_space=pltpu.VMEM)
    rows = pl.BlockSpec((256, 1024), lambda i: (i, 0))
    whole = pl.BlockSpec((256, D), lambda i: (0, 0))
    return pl.pallas_call(
        body, name="adamw", grid=(4,),
        in_specs=[rows] * 4 + [whole] * 4 + [vm] * 16, out_specs=[rows] * 3 + [whole] * 3 + [vm] * 21,
        out_shape=shapes,
        compiler_params=_cp(("arbitrary",)),
    )(*big_in, *big_out, sall, *flat)


def kernel(x, positions, w_in, w_out, mix_norm_w, attn_out_norm_w, hgrn_out_norm_w, hgrn_lb_raw, final_norm_w, loss_target, m_w_in, m_w_out, m_mix_norm_w, m_attn_out_norm_w, m_hgrn_out_norm_w, m_hgrn_lb_raw, m_final_norm_w, v_w_in, v_w_out, v_mix_norm_w, v_attn_out_norm_w, v_hgrn_out_norm_w, v_hgrn_lb_raw, v_final_norm_w):
    xs = x.reshape(T, D)
    tgt = loss_target.reshape(T, D)
    pos = positions.reshape(T, 1)
    fnw = final_norm_w.reshape(1, D)

    ti = np.arange(TH)
    tri_np = ((ti[:, None] // CHUNK == ti[None, :] // CHUNK) & (ti[None, :] <= ti[:, None])).astype(np.float32)
    tri = jnp.asarray(tri_np, BF16)
    trit = jnp.asarray(tri_np.T, BF16)
    hi_ = np.arange(AW) // HEAD
    gmat = jnp.asarray((hi_[:256, None] == hi_[None, :256]).astype(np.float32) / HEAD, BF16)
    emat_np = (np.arange(128)[:, None] == hi_[None, :]).astype(np.float32)
    sel_np = (8 + hi_[:, None] == np.arange(128)[None, :]).astype(np.float32)
    emat = jnp.asarray(emat_np, BF16)
    selmat = jnp.asarray(sel_np, BF16)

    jm_arr = (2 * lax.axis_index("x") + lax.axis_index("y")).astype(jnp.int32).reshape(1)
    (hn, q1, k1, v1, q4, k4, v4, q16, k16, v16, ag, hq, hf, hi, hg, w_full, wout4) = _fwd_in(
        xs, pos, mix_norm_w, w_in.reshape(D, 1024), w_out.reshape(256, D), jm_arr)
    wout_full = wout4.reshape(D, D)
    flat = lambda a: a.reshape(T, AW)
    o1, l1 = _attn_fwd(q1, k1, v1, T // BLK, "attn_fwd_d1")
    o4, l4 = _attn_fwd(flat(q4), flat(k4), flat(v4), T // 4 // BLK, "attn_fwd_d4")
    o16, l16 = _attn_fwd(flat(q16), flat(k16), flat(v16), T // 16 // BLK, "attn_fwd_d16")
    rec, sall = _hgrn_fwd(hq, hf, hi, hgrn_lb_raw, tri)

    (dx2, do1, do4, do16, st1, st4, st16, drec, dag, dhg, rout, routb, small4) = _fwd_out(
        o1, o4.reshape(4, T // 4, AW), o16.reshape(16, T // 16, AW),
        l1, l4.reshape(4, T // 4, 128), l16.reshape(16, T // 16, 128),
        rec, ag, hg, xs, tgt, attn_out_norm_w, hgrn_out_norm_w, fnw, wout_full, gmat, emat, selmat)

    fst = lambda a: a.reshape(T, 128)
    dq1, dk1, dv1 = _attn_bwd(q1, k1, v1, do1, st1, T // BLK, "attn_bwd_d1")
    dq4, dk4, dv4 = _attn_bwd(flat(q4), flat(k4), flat(v4), flat(do4), fst(st4), T // 4 // BLK, "attn_bwd_d4")
    dq16, dk16, dv16 = _attn_bwd(flat(q16), flat(k16), flat(v16), flat(do16), fst(st16), T // 16 // BLK,
                                 "attn_bwd_d16")
    dproj_h, small6, pout_own, pout_rem = _hgrn_bwd(hq, hf, hi, hgrn_lb_raw, tri, trit, drec, sall, dhg,
                                                    rout, routb)

    r4 = lambda a: a.reshape(4, T // 4, AW)
    r16 = lambda a: a.reshape(16, T // 16, AW)
    dproj_a = _dproj_build((dq1, r4(dq4), r16(dq16)), (dk1, r4(dk4), r16(dk16)), (dv1, r4(dv4), r16(dv16)),
                           dag, pos)
    rin, rinb = _grad_w_in(hn, dproj_a, dproj_h)
    gx, _, _, small_all, fin, fout = _bwd_x(dproj_a, dproj_h, xs, dx2, mix_norm_w, w_full, rin, rinb,
                                            small4, small6, pout_own, pout_rem)
    g_w_in = fin.reshape(D, 1024)
    g_w_out = fout.reshape(256, D)

    params = [(mix_norm_w, m_mix_norm_w, v_mix_norm_w),
              (attn_out_norm_w, m_attn_out_norm_w, v_attn_out_norm_w),
              (hgrn_out_norm_w, m_hgrn_out_norm_w, v_hgrn_out_norm_w),
              (hgrn_lb_raw, m_hgrn_lb_raw, v_hgrn_lb_raw),
              (fnw, m_final_norm_w.reshape(1, D), v_final_norm_w.reshape(1, D))]
    d_in, nm_in, nv_in, d_out, nm_out, nv_out, *so = _adamw(
        (w_in.reshape(D, 1024), g_w_in, m_w_in.reshape(D, 1024), v_w_in.reshape(D, 1024)),
        (w_out.reshape(256, D), g_w_out, m_w_out.reshape(256, D), v_w_out.reshape(256, D)), small_all, params)
    loss = so[0].reshape(())
    g_s = [so[1 + 4 * p] for p in range(5)]
    d_s = [so[2 + 4 * p] for p in range(5)]
    m_s = [so[3 + 4 * p] for p in range(5)]
    v_s = [so[4 + 4 * p] for p in range(5)]
    for lst in (g_s, d_s, m_s, v_s):
        lst[4] = lst[4].reshape(D)

    return (loss, gx.reshape(1, T, D),
            g_w_in.reshape(1, D, 1024), g_w_out.reshape(1, 256, D), *g_s,
            d_in.reshape(1, D, 1024), d_out.reshape(1, 256, D), *d_s,
            nm_in.reshape(1, D, 1024), nm_out.reshape(1, 256, D), *m_s,
            nv_in.reshape(1, D, 1024), nv_out.reshape(1, 256, D), *v_s)
```

```python
import functools

import numpy as np
import jax
import jax.numpy as jnp
from jax import lax
from jax.experimental import pallas as pl
from jax.experimental.pallas import tpu as pltpu

F32 = jnp.float32
BF16 = jnp.bfloat16

T = 4096
D = 1024
AW = 512
HW = 512
NCOL = 4096
HEAD = 64
BLK = 128
CHUNK = 64
EPS = 1e-6
SCALE = HEAD ** -0.5
NEG = -1e30
ROPE_THETA = 500000.0
INV_FREQ = [float(v) for v in
            (np.float32(ROPE_THETA) ** (-(np.arange(8, dtype=np.float32)) * np.float32(0.125)))]
LR, B1, B2, AEPS, WD, STEP = 0.001, 0.9, 0.999, 1e-08, 0.01, 10
VMEM_LIMIT = 56 * 1024 * 1024
MESH = pl.DeviceIdType.MESH


def _cp(sem=None, **kw):
    return pltpu.CompilerParams(dimension_semantics=sem, vmem_limit_bytes=VMEM_LIMIT, **kw)


def _mm(a, b):
    return jnp.dot(a, b, preferred_element_type=F32)


def _mm_nt(a, b):
    return lax.dot_general(a, b, (((1,), (1,)), ((), ())), preferred_element_type=F32)


def _mm_tn(a, b):
    return lax.dot_general(a, b, (((0,), (0,)), ((), ())), preferred_element_type=F32)


def _split3(x):
    h = x.astype(BF16)
    r = x - h.astype(F32)
    m = r.astype(BF16)
    l = (r - m.astype(F32)).astype(BF16)
    return h, m, l


def _mm_exact_l(mat_bf, x):
    h, m, l = _split3(x)
    return _mm(mat_bf, h) + _mm(mat_bf, m) + _mm(mat_bf, l)


def _mm_exact_r(x, mat_bf):
    h = x.astype(BF16)
    l = (x - h.astype(F32)).astype(BF16)
    return _mm(h, mat_bf) + _mm(l, mat_bf)


def _sigmoid(x):
    return 0.5 * jnp.tanh(0.5 * x) + 0.5


def _rope_tables(pos):
    lane = lax.broadcasted_iota(jnp.int32, (1, 128), 1)
    jl = lane & 63
    fi = jl & 7
    inv = jnp.zeros((1, 128), F32)
    for kk in range(8):
        inv = jnp.where(fi == kk, INV_FREQ[kk], inv)
    ang = pos.astype(F32) * inv
    c = jnp.cos(ang)
    s = jnp.sin(ang)
    cosf = jnp.where(jl < 16, c, 1.0)
    s1 = jnp.where(jl < 8, -s, 0.0)
    s2 = jnp.where((jl >= 8) & (jl < 16), s, 0.0)
    return cosf, s1, s2


def _rope(t, cosf, s1, s2):
    parts = []
    for ci in range(t.shape[1] // 128):
        tc = t[:, ci * 128:(ci + 1) * 128]
        parts.append(tc * cosf + pltpu.roll(tc, 120, 1) * s1 + pltpu.roll(tc, 8, 1) * s2)
    return jnp.concatenate(parts, axis=1)


def _rope_bwd(g, cosf, s1, s2):
    parts = []
    for ci in range(g.shape[1] // 128):
        gc = g[:, ci * 128:(ci + 1) * 128]
        parts.append(gc * cosf + pltpu.roll(gc * s1, 8, 1) + pltpu.roll(gc * s2, 120, 1))
    return jnp.concatenate(parts, axis=1)


def _perm_store(val, scr, scr2, o1, o4, o16, dt):
    n = val.shape[0]
    q = n // 4
    o1[...] = val.astype(dt)
    for ci in range(val.shape[1] // 128):
        cs = slice(ci * 128, (ci + 1) * 128)
        scr[ci] = val[:, cs]
        for r4 in range(4):
            part = scr[ci, pl.ds(r4, q, stride=4), :]
            o4[r4, :, cs] = part.astype(dt)
            scr2[ci, r4 * q:(r4 + 1) * q, :] = part
        for r4 in range(4):
            for b in range(4):
                o16[r4 + 4 * b, :, cs] = scr2[ci, pl.ds(r4 * q + b, q // 4, stride=4), :].astype(dt)


def _unperm_load(r4, r16, scr_a, scr_b, scr_c):
    n = scr_a.shape[1]
    q = n // 4
    nc = r4.shape[-1] // 128
    for ci in range(nc):
        cs = slice(ci * 128, (ci + 1) * 128)
        for rr in range(4):
            scr_a[ci, pl.ds(rr, q, stride=4), :] = r4[rr, :, cs].astype(F32)
        for rr in range(4):
            for b in range(4):
                scr_c[ci, pl.ds(rr * q + b, q // 4, stride=4), :] = r16[rr + 4 * b, :, cs].astype(F32)
        for rr in range(4):
            scr_b[ci, pl.ds(rr, q, stride=4), :] = scr_c[ci, rr * q:(rr + 1) * q, :]
    return (jnp.concatenate([scr_a[ci] for ci in range(nc)], axis=1),
            jnp.concatenate([scr_b[ci] for ci in range(nc)], axis=1))


def _fwd_in(x, pos, mixw, w_in, w_out, jm_arr):
    TT = 512
    NT = T // TT

    def body(jm_ref, x_ref, pos_ref, mw_ref, win_ref, wout_ref,
             hnt_ref, q1, k1, v1, q4, k4, v4, q16, k16, v16, ag, hq, hf, hi, hg, wfull_o, woutfull_o,
             wbuf, wobuf, hn_all, scr, scr2, stage, send_sems, recv_sems, loc_sems):
        s = pl.program_id(0)
        i = pl.program_id(1)
        mx, my, c = lax.axis_index("x"), lax.axis_index("y"), lax.axis_index("c")
        me, sibling = (mx, my, c), (mx, my, 1 - c)
        chips = [(mx, 1 - my), (1 - mx, my), (1 - mx, 1 - my)]
        jm = 2 * mx + my
        rows_in = [pl.ds(pl.multiple_of(h * 512, 512), 512) for h in (c, 1 - c)]
        rows_out = [pl.ds(pl.multiple_of(h * 128, 128), 128) for h in (c, 1 - c)]

        def blk(k):
            return lax.bitwise_xor(jm, k + 1)

        def rc(n, ref, to):
            return pltpu.make_async_remote_copy(src_ref=ref, dst_ref=ref, send_sem=send_sems.at[n],
                                                recv_sem=recv_sems.at[n], device_id=to, device_id_type=MESH)

        halves = [pl.ds(0, 512), pl.ds(512, 512)]
        send_in = lambda k, h: rc(12 + 2 * k + h, wbuf.at[jm, rows_in[0], halves[h]], (*chips[k], c))
        got_in = lambda k, h: rc(12 + 2 * k + h, wbuf.at[blk(k), rows_in[0], halves[h]], me)
        relay = lambda h: rc(16 + h, wbuf.at[blk(h), rows_in[0], halves[h]], (*chips[1 - h], c))
        got_relay = lambda h: rc(16 + h, wbuf.at[blk(2), rows_in[0], halves[h]], me)
        send_out = lambda k: rc(3 + k, wobuf.at[jm, rows_out[0], :], (*chips[k], c))
        got_out = lambda k: rc(3 + k, wobuf.at[blk(k), rows_out[0], :], me)
        pass_in = lambda k: rc(6 + k, wbuf.at[blk(k), rows_in[0], :], sibling)
        pass_out = lambda k: rc(9 + k, wobuf.at[blk(k), rows_out[0], :], sibling)
        passed_in = lambda k: rc(6 + k, wbuf.at[blk(k), rows_in[1], :], me)
        passed_out = lambda k: rc(9 + k, wobuf.at[blk(k), rows_out[1], :], me)

        def keep(j, n):
            return pltpu.make_async_copy(wbuf.at[j], wfull_o.at[:, pl.ds(j * 1024, 1024)], loc_sems.at[n])

        @pl.when((s == 0) & (i == 0))
        def _():
            for p in range(5):
                src = win_ref.at[pl.ds(p * 256, 256), :] if p < 4 else wout_ref
                load = pltpu.make_async_copy(src, stage, loc_sems.at[4])
                load.start()
                load.wait()
                if p < 4:
                    wbuf[jm, p * 256:(p + 1) * 256, :] = stage[...].astype(BF16)
                else:
                    wobuf[jm] = stage[...].astype(BF16)
            for k in range(2):
                for h in range(2):
                    send_in(k, h).start()
            keep(jm, 0).start()

        def arrive(k):
            if k == 0:
                for kk in range(2):
                    for h in range(2):
                        got_in(kk, h).wait_recv()
                relay(0).start()
                relay(1).start()
            if k == 2:
                got_relay(0).wait_recv()
                got_relay(1).wait_recv()
            pass_in(k).start()
            passed_in(k).wait_recv()
            keep(blk(k), k + 1).start()
            if k == 2:
                for kk in range(3):
                    send_out(kk).start()

        @pl.when((s == 1) & (i == 0))
        def _():
            arrive(0)
            arrive(1)

        pl.when((s == 2) & (i == 0))(functools.partial(arrive, 2))

        tile = pl.ds(pl.multiple_of(i * TT, TT), TT)

        @pl.when(s == 0)
        def _():
            xv = x_ref[...]
            r = lax.rsqrt(jnp.mean(xv * xv, axis=-1, keepdims=True) + EPS)
            hnf = (xv * r) * mw_ref[...]
            hn_all[tile, :] = hnf.astype(BF16)
            hnt_ref[...] = hnf.T.astype(BF16)

        def project(jj):
            hn = hn_all[tile, :]
            lo = _mm(hn, wbuf[jj, :, 0:512])
            hi_cols = _mm(hn, wbuf[jj, :, 512:1024])
            if jj == 0:
                cosf, s1, s2 = _rope_tables(pos_ref[...])
                _perm_store(_rope(lo, cosf, s1, s2), scr, scr2, q1, q4, q16, BF16)
                _perm_store(_rope(hi_cols, cosf, s1, s2), scr, scr2, k1, k4, k16, BF16)
            elif jj == 1:
                _perm_store(lo, scr, scr2, v1, v4, v16, BF16)
                ag[...] = hi_cols.astype(BF16)
            elif jj == 2:
                hq[...] = lo.astype(BF16)
                hf[...] = hi_cols.astype(BF16)
            else:
                hi[...] = lo.astype(BF16)
                hg[...] = hi_cols.astype(BF16)

        def project_block(j):
            for jj in range(4):
                pl.when(j == jj)(functools.partial(project, jj))

        @pl.when(s != 1)
        def _():
            project_block(lax.bitwise_xor(jm, jnp.where(s == 0, 0, 3)))

        @pl.when(s == 1)
        def _():
            project_block(lax.bitwise_xor(jm, 1))
            project_block(lax.bitwise_xor(jm, 2))

        @pl.when((s == 2) & (i == NT - 1))
        def _():
            for k in range(3):
                got_out(k).wait_recv()
                pass_out(k).start()
            for k in range(3):
                passed_out(k).wait_recv()
            out = pltpu.make_async_copy(wobuf, woutfull_o, loc_sems.at[4])
            out.start()
            for h in range(2):
                relay(h).wait_send()
                for k in range(2):
                    send_in(k, h).wait_send()
            for k in range(3):
                send_out(k).wait_send()
                pass_in(k).wait_send()
                pass_out(k).wait_send()
            keep(jm, 0).wait()
            for k in range(3):
                keep(blk(k), k + 1).wait()
            out.wait()

    def at_stage_of(jb):
        def index(s, i, jm_ref):
            sa = (lax.bitwise_xor(jm_ref[0], jb) + 1) // 2
            return jnp.where(s < sa, 0, jnp.where(s == sa, i, NT - 1))
        return index

    tok = lambda w, jb: pl.BlockSpec((TT, w), lambda s, i, jm_ref: (at_stage_of(jb)(s, i, jm_ref), 0))
    d4 = lambda jb: pl.BlockSpec((4, TT // 4, AW), lambda s, i, jm_ref: (0, at_stage_of(jb)(s, i, jm_ref), 0))
    d16 = lambda jb: pl.BlockSpec((16, TT // 16, AW), lambda s, i, jm_ref: (0, at_stage_of(jb)(s, i, jm_ref), 0))
    hbm = pl.BlockSpec(memory_space=pltpu.HBM)
    sd = lambda shape, dt: jax.ShapeDtypeStruct(shape, dt)
    in_own_stage = lambda s, i: jnp.where(s == 0, i, NT - 1)
    grid_spec = pltpu.PrefetchScalarGridSpec(
        num_scalar_prefetch=1, grid=(3, NT),
        in_specs=[pl.BlockSpec((TT, D), lambda s, i, jm_ref: (in_own_stage(s, i), 0)),
                  pl.BlockSpec((TT, 1), lambda s, i, jm_ref: (i, 0)),
                  pl.BlockSpec((1, D), lambda s, i, jm_ref: (0, 0)), hbm, hbm],
        out_specs=[pl.BlockSpec((D, TT), lambda s, i, jm_ref: (0, in_own_stage(s, i))),
                   tok(AW, 0), tok(AW, 0), tok(AW, 1), d4(0), d4(0), d4(1), d16(0), d16(0), d16(1),
                   tok(AW, 1), tok(AW, 2), tok(AW, 2), tok(AW, 3), tok(AW, 3), hbm, hbm],
        scratch_shapes=[pltpu.VMEM((4, D, 1024), BF16), pltpu.VMEM((4, 256, D), BF16), pltpu.VMEM((T, D), BF16),
                        pltpu.VMEM((4, TT, 128), F32), pltpu.VMEM((4, TT, 128), F32), pltpu.VMEM((256, 1024), F32),
                        pltpu.SemaphoreType.DMA((18,)),
                        pltpu.SemaphoreType.DMA((18,)), pltpu.SemaphoreType.DMA((6,))])
    return pl.pallas_call(
        body, name="fwd_in", grid_spec=grid_spec,
        out_shape=[sd((D, T), BF16)] + [sd((T, AW), BF16)] * 3 + [sd((4, T // 4, AW), BF16)] * 3
        + [sd((16, T // 16, AW), BF16)] * 3
        + [sd((T, AW), BF16)] * 5 + [sd((D, NCOL), BF16), sd((4, 256, D), BF16)],
        compiler_params=_cp(("arbitrary", "arbitrary")),
    )(jm_arr, x, pos, mixw, w_in, w_out)


def _band_mask(key_axis, nkeys=2 * BLK):
    shape = (nkeys, 2 * BLK) if key_axis == 0 else (2 * BLK, nkeys)
    kj = lax.broadcasted_iota(jnp.int32, shape, key_axis)
    qi = lax.broadcasted_iota(jnp.int32, shape, 1 - key_axis) & (BLK - 1)
    return (kj >= qi) & (kj <= qi + BLK), kj, qi


def _stack_heads(t2, in_a):
    z = jnp.zeros_like(t2)
    return jnp.concatenate([jnp.where(in_a[0], t2, z), jnp.where(in_a[1], t2, z)], axis=0)


def _attn_fwd(q, k, v, nb, name):
    n = 8 if nb >= 8 else 4
    CH = n * BLK
    halo = nb > n

    def body(*refs):
        if halo:
            q_ref, k_ref, v_ref, kp_ref, vp_ref, o_ref, lse_ref = refs
        else:
            q_ref, k_ref, v_ref, o_ref, lse_ref = refs
        lane = lax.broadcasted_iota(jnp.int32, (1, 128), 1)
        in_a = [lane < HEAD, lane >= HEAD]
        band, kj, _ = _band_mask(1)
        thr0 = jnp.where((n * pl.program_id(0)) % nb == 0, BLK, 0) if halo else BLK
        mask0 = band & (kj >= thr0)
        mask_first = band & (kj >= BLK)
        for b in range(n):
            rs = slice(b * BLK, (b + 1) * BLK)
            stat = jnp.zeros((BLK, 128), F32)
            for hp in range(4):
                cs = slice(hp * 128, (hp + 1) * 128)
                q2s = _stack_heads(q_ref[rs, cs], in_a)
                if b == 0:
                    kprev = kp_ref[:, cs] if halo else k_ref[rs, cs]
                    vprev = vp_ref[:, cs] if halo else v_ref[rs, cs]
                    kk = jnp.concatenate([kprev, k_ref[rs, cs]], axis=0)
                    vv = jnp.concatenate([vprev, v_ref[rs, cs]], axis=0)
                    mask = mask0
                else:
                    kk = k_ref[(b - 1) * BLK:(b + 1) * BLK, cs]
                    vv = v_ref[(b - 1) * BLK:(b + 1) * BLK, cs]
                    mask = mask_first if b % nb == 0 else band
                s = jnp.where(mask, _mm_nt(q2s, kk) * SCALE, NEG)
                m = jnp.max(s, axis=-1, keepdims=True)
                p = jnp.exp(s - m)
                l = jnp.sum(p, axis=-1, keepdims=True)
                o = _mm(p.astype(BF16), vv) / l
                lse = m + jnp.log(l)
                o_ref[rs, cs] = jnp.where(in_a[0], o[:BLK], o[BLK:]).astype(BF16)
                stat = jnp.where(lane == 2 * hp, lse[:BLK], stat)
                stat = jnp.where(lane == 2 * hp + 1, lse[BLK:], stat)
            lse_ref[rs, :] = stat

    cur = pl.BlockSpec((CH, AW), lambda i: (i, 0))
    prev = pl.BlockSpec((BLK, AW), lambda i: (jnp.maximum(n * i - 1, 0), 0))
    return pl.pallas_call(
        body, name=name, grid=(T // CH,),
        in_specs=[cur, cur, cur] + ([prev, prev] if halo else []),
        out_specs=[cur, pl.BlockSpec((CH, 128), lambda i: (i, 0))],
        out_shape=[jax.ShapeDtypeStruct((T, AW), BF16), jax.ShapeDtypeStruct((T, 128), F32)],
        compiler_params=_cp(("parallel",)),
    )(*((q, k, v) + ((k, v) if halo else ())))


def _attn_bwd(q, k, v, do, st, nb, name):
    n = 8 if nb >= 8 else 4
    CH = n * BLK
    NBLK = T // BLK
    halo = nb > n

    def body(*refs):
        if halo:
            (q_ref, k_ref, v_ref, do_ref, st_ref, kp_ref, vp_ref, qn_ref, don_ref, stn_ref,
             dq_ref, dk_ref, dv_ref) = refs
        else:
            q_ref, k_ref, v_ref, do_ref, st_ref, dq_ref, dk_ref, dv_ref = refs
        i = pl.program_id(0)
        lane = lax.broadcasted_iota(jnp.int32, (1, 128), 1)
        in_a = [lane < HEAD, lane >= HEAD]
        band, kj, _ = _band_mask(0)
        thr0 = jnp.where((n * i) % nb == 0, BLK, 0) if halo else BLK
        mask0 = band & (kj >= thr0)
        mask_first = band & (kj >= BLK)

        def stat_rows(st_t, hp):
            lse_r = jnp.concatenate([st_t[2 * hp:2 * hp + 1, :], st_t[2 * hp + 1:2 * hp + 2, :]], axis=1)
            dl_r = jnp.concatenate([st_t[8 + 2 * hp:9 + 2 * hp, :], st_t[9 + 2 * hp:10 + 2 * hp, :]], axis=1)
            return lse_r, dl_r

        st_t = [st_ref[b * BLK:(b + 1) * BLK, :].T for b in range(n)]
        if halo:
            nxt_thr = jnp.where((n * i + n) % nb == 0, 2 * BLK, 0)
            _, kj1, qi1 = _band_mask(0, BLK)
            mask_next = kj1 >= qi1 + nxt_thr
            stn_t = stn_ref[...].T

        for hp in range(4):
            cs = slice(hp * 128, (hp + 1) * 128)
            kb = [k_ref[b * BLK:(b + 1) * BLK, cs] for b in range(n)]
            vb = [v_ref[b * BLK:(b + 1) * BLK, cs] for b in range(n)]
            dk_acc = [jnp.zeros((BLK, 128), F32) for _ in range(n)]
            dv_acc = [jnp.zeros((BLK, 128), F32) for _ in range(n)]
            for b in range(n):
                rs = slice(b * BLK, (b + 1) * BLK)
                q2s = _stack_heads(q_ref[rs, cs], in_a)
                do2s = _stack_heads(do_ref[rs, cs], in_a)
                if b == 0:
                    kprev = kp_ref[:, cs] if halo else kb[0]
                    vprev = vp_ref[:, cs] if halo else vb[0]
                    mask = mask0
                else:
                    kprev, vprev, mask = kb[b - 1], vb[b - 1], (mask_first if b % nb == 0 else band)
                kk = jnp.concatenate([kprev, kb[b]], axis=0)
                vv = jnp.concatenate([vprev, vb[b]], axis=0)
                lse_r, dl_r = stat_rows(st_t[b], hp)
                s_t = jnp.where(mask, _mm_nt(kk, q2s) * SCALE, NEG)
                p_t = jnp.exp(s_t - lse_r)
                ds_t = (p_t * (_mm_nt(vv, do2s) - dl_r)).astype(BF16)
                dkk = _mm(ds_t, q2s) * SCALE
                dvv = _mm(p_t.astype(BF16), do2s)
                dqs = _mm_tn(ds_t, kk) * SCALE
                dq_ref[rs, cs] = jnp.where(in_a[0], dqs[:BLK], dqs[BLK:]).astype(BF16)
                dk_acc[b] += dkk[BLK:]
                dv_acc[b] += dvv[BLK:]
                if b > 0:
                    dk_acc[b - 1] += dkk[:BLK]
                    dv_acc[b - 1] += dvv[:BLK]
            if halo:
                q2s = _stack_heads(qn_ref[:, cs], in_a)
                do2s = _stack_heads(don_ref[:, cs], in_a)
                lse_r, dl_r = stat_rows(stn_t, hp)
                s_t = jnp.where(mask_next, _mm_nt(kb[n - 1], q2s) * SCALE, NEG)
                p_t = jnp.exp(s_t - lse_r)
                ds_t = (p_t * (_mm_nt(vb[n - 1], do2s) - dl_r)).astype(BF16)
                dk_acc[n - 1] += _mm(ds_t, q2s) * SCALE
                dv_acc[n - 1] += _mm(p_t.astype(BF16), do2s)
            for b in range(n):
                dk_ref[b * BLK:(b + 1) * BLK, cs] = dk_acc[b].astype(BF16)
                dv_ref[b * BLK:(b + 1) * BLK, cs] = dv_acc[b].astype(BF16)

    cur = pl.BlockSpec((CH, AW), lambda i: (i, 0))
    cur_st = pl.BlockSpec((CH, 128), lambda i: (i, 0))
    prev = pl.BlockSpec((BLK, AW), lambda i: (jnp.maximum(n * i - 1, 0), 0))
    nxt = pl.BlockSpec((BLK, AW), lambda i: (jnp.minimum(n * i + n, NBLK - 1), 0))
    nxt_st = pl.BlockSpec((BLK, 128), lambda i: (jnp.minimum(n * i + n, NBLK - 1), 0))
    ins = [cur] * 4 + [cur_st] + ([prev, prev, nxt, nxt, nxt_st] if halo else [])
    args = (q, k, v, do, st) + ((k, v, q, do, st) if halo else ())
    return pl.pallas_call(
        body, name=name, grid=(T // CH,),
        in_specs=ins,
        out_specs=[cur] * 3,
        out_shape=[jax.ShapeDtypeStruct((T, AW), BF16)] * 3,
        compiler_params=_cp(("parallel",)),
    )(*args)


TH = 256
NCH = TH // CHUNK


def _hgrn_common(hq_ref, hf_ref, lbr_ref, tri_ref):
    r0 = lbr_ref[0:1, :]
    r1 = lbr_ref[1:2, :]
    mx = jnp.maximum(r0, r1)
    e0 = jnp.exp(r0 - mx)
    e1 = jnp.exp(r1 - mx)
    lb = e0 / (e0 + e1)
    hqv = hq_ref[...].astype(F32)
    sq = _sigmoid(hqv)
    qv = hqv * sq
    sf = _sigmoid(hf_ref[...].astype(F32))
    f = lb + (1.0 - lb) * sf
    kv = 1.0 - f
    g = jnp.log(f)
    cum = _mm_exact_l(tri_ref[...], g)
    dec = jnp.exp(jnp.concatenate([cum[c * CHUNK + CHUNK - 1:(c + 1) * CHUNK, :] for c in range(NCH)], axis=0))
    decb = jnp.concatenate([jnp.broadcast_to(dec[c:c + 1, :], (CHUNK, HW)) for c in range(NCH)], axis=0)
    ea = jnp.exp(cum)
    ena = jnp.exp(-cum)
    eend = decb * ena
    return dict(lb=lb, hq=hqv, sq=sq, q=qv, sf=sf, f=f, k=kv, cum=cum, ea=ea, ena=ena, eend=eend,
                qd=qv * ea, ki=kv * ena, ke=kv * eend, dec=dec)


def _tri_mask(transposed=False):
    ti = lax.broadcasted_iota(jnp.int32, (TH, TH), 1 if transposed else 0)
    si = lax.broadcasted_iota(jnp.int32, (TH, TH), 0 if transposed else 1)
    return (si <= ti) & ((si // CHUNK) == (ti // CHUNK))


def _hgrn_fwd(hq, hf, hi, lbr, tri):
    def body(hq_ref, hf_ref, hi_ref, lbr_ref, tri_ref, rec_ref, sall_ref, st_scr):
        @pl.when(pl.program_id(0) == 0)
        def _():
            st_scr[...] = jnp.zeros_like(st_scr)

        w = _hgrn_common(hq_ref, hf_ref, lbr_ref, tri_ref)
        qd, ki, ke = w["qd"].astype(BF16), w["ki"].astype(BF16), w["ke"].astype(BF16)
        dec = w["dec"]
        vb = hi_ref[...]
        causal = _tri_mask()
        for h in range(4):
            cs = slice(h * 128, (h + 1) * 128)
            att = jnp.where(causal, _mm_nt(qd[:, cs], ki[:, cs]), 0.0)
            o_intra = _mm(att.astype(BF16), vb[:, cs])
            st = st_scr[:, cs]
            for c in range(NCH):
                rs = slice(c * CHUNK, (c + 1) * CHUNK)
                sall_ref[c, :, cs] = st
                rec_ref[rs, cs] = (o_intra[rs] + _mm_nt(qd[rs, cs], st.astype(BF16))).astype(BF16)
                st = dec[c:c + 1, cs] * st + _mm_tn(vb[rs, cs], ke[rs, cs])
            st_scr[:, cs] = st

    tok = pl.BlockSpec((TH, HW), lambda i: (i, 0))
    return pl.pallas_call(
        body, name="hgrn_fwd", grid=(T // TH,),
        in_specs=[tok, tok, tok, pl.BlockSpec((2, HW), lambda i: (0, 0)), pl.BlockSpec((TH, TH), lambda i: (0, 0))],
        out_specs=[tok, pl.BlockSpec((NCH, 128, HW), lambda i: (i, 0, 0))],
        out_shape=[jax.ShapeDtypeStruct((T, HW), BF16), jax.ShapeDtypeStruct((T // CHUNK, 128, HW), F32)],
        scratch_shapes=[pltpu.VMEM((128, HW), F32)],
        compiler_params=_cp(("arbitrary",)),
    )(hq, hf, hi, lbr, tri)


def _hgrn_bwd(hq, hf, hi, lbr, tri, trit, drec, sall, dhg, rout, routb):
    NT = T // TH

    def body(hq_ref, hf_ref, hi_ref, lbr_ref, tri_ref, trit_ref, do_ref, sall_ref, dhg_ref, rout_r, routb_r,
             dph_ref, small_ref, pout_o, poutr_o,
             dst_scr, dlb_scr, dqd_scr, dki_scr, dke_scr, dlast_scr, send_sems, recv_sems, loc_sems):
        step = pl.program_id(0)
        loc, rem = _chip_copies(_w_out_piece, rout_r, routb_r, pout_o, poutr_o, send_sems, recv_sems,
                                loc_sems.at[0])

        @pl.when(step == 0)
        def _():
            dst_scr[...] = jnp.zeros_like(dst_scr)
            dlb_scr[...] = jnp.zeros_like(dlb_scr)
            for cp in loc + rem:
                cp.start()

        w = _hgrn_common(hq_ref, hf_ref, lbr_ref, tri_ref)
        qd, ki, ke = w["qd"].astype(BF16), w["ki"].astype(BF16), w["ke"].astype(BF16)
        dec = w["dec"]
        vb = hi_ref[...]
        dob = do_ref[...].astype(BF16)
        causal = _tri_mask()
        causal_t = _tri_mask(transposed=True)
        for h in range(4):
            cs = slice(h * 128, (h + 1) * 128)
            att_t = jnp.where(causal_t, _mm_nt(ki[:, cs], qd[:, cs]), 0.0).astype(BF16)
            datt_t = jnp.where(causal_t, _mm_nt(vb[:, cs], dob[:, cs]), 0.0).astype(BF16)
            datt = jnp.where(causal, _mm_nt(dob[:, cs], vb[:, cs]), 0.0).astype(BF16)
            dv_intra = _mm(att_t, dob[:, cs])
            dqd_intra = _mm(datt, ki[:, cs])
            dki_scr[:, cs] = _mm(datt_t, qd[:, cs])
            dst = dst_scr[:, cs]
            for c in reversed(range(NCH)):
                rs = slice(c * CHUNK, (c + 1) * CHUNK)
                dec_c = dec[c:c + 1, :]
                st = sall_ref[c, :, cs]
                dstb = dst.astype(BF16)
                dph_ref[rs, 2 * HW + h * 128:2 * HW + (h + 1) * 128] = (
                    dv_intra[rs] + _mm_nt(ke[rs, cs], dstb)).astype(BF16)
                dqd_scr[rs, cs] = dqd_intra[rs] + _mm(dob[rs, cs], st.astype(BF16))
                dke_scr[rs, cs] = _mm(vb[rs, cs], dstb)
                ddec = jnp.sum(dst * st, axis=0, keepdims=True)
                dlast_scr[c:c + 1, cs] = ddec * dec_c[:, cs]
                dst = dec_c[:, cs] * dst + _mm_tn(dob[rs, cs], qd[rs, cs])
            dst_scr[:, cs] = dst
        dqd, dki, dke = dqd_scr[...], dki_scr[...], dke_scr[...]
        dq = dqd * w["ea"]
        dk = dki * w["ena"] + dke * w["eend"]
        dcum = dqd * w["qd"] - dki * w["ki"] - dke * w["ke"]
        dkeke = dke * w["ke"]
        dlastb = jnp.concatenate(
            [jnp.broadcast_to(dlast_scr[c:c + 1, :] + jnp.sum(dkeke[c * CHUNK:(c + 1) * CHUNK], axis=0, keepdims=True),
                              (CHUNK, HW)) for c in range(NCH)], axis=0)
        dg = _mm_exact_l(trit_ref[...], dcum) + dlastb
        df = dg / w["f"] - dk
        lb, sf, sq = w["lb"], w["sf"], w["sq"]
        dph_ref[:, HW:2 * HW] = (df * (1.0 - lb) * sf * (1.0 - sf)).astype(BF16)
        dph_ref[:, 0:HW] = (dq * (sq * (1.0 + w["hq"] * (1.0 - sq)))).astype(BF16)
        dph_ref[:, 3 * HW:4 * HW] = dhg_ref[...]
        dlb_scr[...] += jnp.sum(df * (1.0 - sf), axis=0, keepdims=True)

        @pl.when(step == NT - 1)
        def _():
            gr = dlb_scr[...] * lb * (1.0 - lb)
            small_ref[...] = jnp.zeros_like(small_ref)
            small_ref[0:1, 0:HW] = gr
            small_ref[1:2, 0:HW] = -gr
            for cp in rem:
                cp.wait_recv()
            for cp in rem:
                cp.wait_send()
            for cp in loc:
                cp.wait()

    tok = pl.BlockSpec((TH, HW), lambda i: (NT - 1 - i, 0))
    const = lambda shape: pl.BlockSpec(shape, lambda i: (0,) * len(shape))
    hbm = pl.BlockSpec(memory_space=pltpu.HBM)
    return pl.pallas_call(
        body, name="hgrn_bwd", grid=(NT,),
        in_specs=[tok, tok, tok, const((2, HW)), const((TH, TH)), const((TH, TH)), tok,
                  pl.BlockSpec((NCH, 128, HW), lambda i: (NT - 1 - i, 0, 0)), tok, hbm, hbm],
        out_specs=[pl.BlockSpec((TH, NCOL // 2), lambda i: (NT - 1 - i, 0)), const((8, D)), hbm, hbm],
        out_shape=[jax.ShapeDtypeStruct((T, NCOL // 2), BF16), jax.ShapeDtypeStruct((8, D), F32),
                   jax.ShapeDtypeStruct((128, D), F32), jax.ShapeDtypeStruct((3, 128, D), BF16)],
        scratch_shapes=[pltpu.VMEM((128, HW), F32), pltpu.VMEM((1, HW), F32), pltpu.VMEM((TH, HW), F32),
                        pltpu.VMEM((TH, HW), F32), pltpu.VMEM((TH, HW), F32), pltpu.VMEM((8, HW), F32),
                        pltpu.SemaphoreType.DMA((3,)), pltpu.SemaphoreType.DMA((3,)), pltpu.SemaphoreType.DMA((1,))],
        compiler_params=_cp(("arbitrary",)),
    )(hq, hf, hi, lbr, tri, trit, drec, sall, dhg, rout, routb)


def _fwd_out(o1, o4, o16, l1, l4, l16, rec, ag, hg, x, tgt, anw, hnw, fnw, wout_full, gmat, emat, selmat):
    TT = 256

    def body(o1_r, o4_r, o16_r, l1_r, l4_r, l16_r, rec_r, ag_r, hg_r, x_r, tgt_r, anw_r, hnw_r, fnw_r, wo_r, g_r,
             e_r, sel_r, dx2_o, do1_o, do4_o, do16_o, st1_o, st4_o, st16_o, drec_o, dag_o, dhg_o,
             rout_o, routb_o, small_o, scr_a, scr_b, scr_c, gwout_o, rbuf, send_sems, recv_sems):
        @pl.when(pl.program_id(0) == 0)
        def _():
            gwout_o[...] = jnp.zeros_like(gwout_o)
            small_o[...] = jnp.zeros_like(small_o)

        def unperm(r4, r16):
            return _unperm_load(r4, r16, scr_a, scr_b, scr_c)

        def perm_out(val, p1, p4, p16, dt):
            _perm_store(val, scr_a, scr_b, p1, p4, p16, dt)

        o4u, o16u = unperm(o4_r, o16_r)
        l4c, l16c = unperm(l4_r, l16_r)
        l1c = l1_r[...]
        mxc = jnp.maximum(jnp.maximum(l1c, l4c), l16c)
        w1c, w4c, w16c = jnp.exp(l1c - mxc), jnp.exp(l4c - mxc), jnp.exp(l16c - mxc)
        denc = w1c + w4c + w16c
        lane = lax.broadcasted_iota(jnp.int32, (1, 128), 1)
        lse_c = jnp.where(lane < 8, mxc + jnp.log(denc), 0.0)
        em = e_r[...]
        wn1 = _mm_exact_r(w1c / denc, em)
        wn4 = _mm_exact_r(w4c / denc, em)
        o1v = o1_r[...].astype(F32)
        attn = wn1 * o1v + wn4 * o4u + (1.0 - wn1 - wn4) * o16u
        gm = g_r[...]

        def head_mean_a(t):
            return jnp.concatenate([_mm_exact_r(t[:, :256], gm), _mm_exact_r(t[:, 256:], gm)], axis=1)

        def head_mean_h(t):
            return jnp.concatenate(
                [jnp.broadcast_to(jnp.mean(t[:, h * 128:(h + 1) * 128], axis=-1, keepdims=True), (TT, 128))
                 for h in range(4)], axis=1)

        rs_a = lax.rsqrt(head_mean_a(attn * attn) + EPS)
        n_a = attn * rs_a
        agv = ag_r[...].astype(F32)
        sg_a = _sigmoid(agv)
        si_a = agv * sg_a
        anw_v = anw_r[...]
        y_a = (n_a * anw_v) * si_a
        recv = rec_r[...].astype(F32)
        rs_h = lax.rsqrt(head_mean_h(recv * recv) + EPS)
        n_h = recv * rs_h
        hgv = hg_r[...].astype(F32)
        sg_h = _sigmoid(hgv)
        si_h = hgv * sg_h
        hnw_v = hnw_r[...]
        y_h = (n_h * hnw_v) * si_h
        mixed = jnp.concatenate([y_a, y_h], axis=1).astype(BF16)
        xv = x_r[...]
        x2 = xv + _mm(mixed, wo_r[...])
        r2 = lax.rsqrt(jnp.mean(x2 * x2, axis=-1, keepdims=True) + EPS)
        fnw_v = fnw_r[...]
        xn = x2 * r2
        err = xn * fnw_v - tgt_r[...]
        small_o[2:3, :] += 0.5 * jnp.sum(jnp.mean(err * err, axis=-1, keepdims=True), axis=0, keepdims=True)
        dy = err * (1.0 / D)
        small_o[0:1, :] += jnp.sum(dy * xn, axis=0, keepdims=True)
        dyw = dy * fnw_v
        dx2 = r2 * dyw - x2 * ((r2 * r2 * r2) * jnp.mean(dyw * x2, axis=-1, keepdims=True))
        dx2_o[...] = dx2
        dx2b = dx2.astype(BF16)
        gwout_o[...] += _mm_tn(mixed, dx2b)
        dmix = _mm_nt(dx2b, wo_r[...])
        dm_a, dm_h = dmix[:, :AW], dmix[:, AW:]
        dag_o[...] = (dm_a * (n_a * anw_v) * (sg_a * (1.0 + agv * (1.0 - sg_a)))).astype(BF16)
        dn_a = dm_a * anw_v * si_a
        small_o[1:2, 0:AW] += jnp.sum(dm_a * n_a * si_a, axis=0, keepdims=True)
        dattn = rs_a * (dn_a - n_a * head_mean_a(dn_a * n_a))
        perm_out(dattn, do1_o, do4_o, do16_o, BF16)
        stats = lse_c + _mm_exact_r(dattn * attn, sel_r[...])
        perm_out(stats, st1_o, st4_o, st16_o, F32)
        dhg_o[...] = (dm_h * (n_h * hnw_v) * (sg_h * (1.0 + hgv * (1.0 - sg_h)))).astype(BF16)
        dn_h = dm_h * hnw_v * si_h
        small_o[1:2, AW:] += jnp.sum(dm_h * n_h * si_h, axis=0, keepdims=True)
        drec_o[...] = (rs_h * (dn_h - n_h * head_mean_h(dn_h * n_h))).astype(BF16)

        @pl.when(pl.program_id(0) == T // TT - 1)
        def _():
            x, y, c = lax.axis_index("x"), lax.axis_index("y"), lax.axis_index("c")
            cps = [pltpu.make_async_remote_copy(
                src_ref=gwout_o.at[pl.ds(pl.multiple_of(j * 256 + (1 - c) * 128, 128), 128), :], dst_ref=rbuf.at[j],
                send_sem=send_sems.at[j], recv_sem=recv_sems.at[j], device_id=(x, y, 1 - c), device_id_type=MESH)
                for j in range(4)]
            for cp in cps:
                cp.start()
            for j, cp in enumerate(cps):
                cp.wait_recv()
                red = gwout_o[pl.ds(pl.multiple_of(j * 256 + c * 128, 128), 128), :] + rbuf[j]
                rout_o[j * 128:(j + 1) * 128, :] = red
                routb_o[j * 128:(j + 1) * 128, :] = red.astype(BF16)
            for cp in cps:
                cp.wait_send()

    tok = lambda w: pl.BlockSpec((TT, w), lambda i: (i, 0))
    d4 = pl.BlockSpec((4, TT // 4, AW), lambda i: (0, i, 0))
    d16 = pl.BlockSpec((16, TT // 16, AW), lambda i: (0, i, 0))
    const = lambda shape: pl.BlockSpec(shape, lambda i: (0,) * len(shape))
    sd = lambda shape, dt: jax.ShapeDtypeStruct(shape, dt)
    c4 = pl.BlockSpec((4, TT // 4, 128), lambda i: (0, i, 0))
    c16 = pl.BlockSpec((16, TT // 16, 128), lambda i: (0, i, 0))
    p3 = lambda w, dt: [sd((T, w), dt), sd((4, T // 4, w), dt), sd((16, T // 16, w), dt)]
    return pl.pallas_call(
        body, name="fwd_out", grid=(T // TT,),
        in_specs=[tok(AW), d4, d16, tok(128), c4, c16, tok(AW), tok(AW), tok(AW), tok(D), tok(D),
                  const((1, AW)), const((1, HW)), const((1, D)), const((D, D)), const((256, 256)),
                  const((128, AW)), const((AW, 128))],
        out_specs=[tok(D)] + [tok(AW), d4, d16] + [tok(128), c4, c16] + [tok(AW)] * 3
        + [const((512, D)), const((512, D)), const((8, D))],
        out_shape=[sd((T, D), F32)] + p3(AW, BF16) + p3(128, F32)
        + [sd((T, AW), BF16), sd((T, AW), BF16), sd((T, AW), BF16), sd((512, D), F32), sd((512, D), BF16),
           sd((8, D), F32)],
        scratch_shapes=[pltpu.VMEM((4, TT, 128), F32)] * 3 + [pltpu.VMEM((D, D), F32),
                        pltpu.VMEM((4, 128, D), F32), pltpu.SemaphoreType.DMA((4,)), pltpu.SemaphoreType.DMA((4,))],
        compiler_params=_cp(("arbitrary",)),
    )(o1, o4, o16, l1, l4, l16, rec, ag, hg, x, tgt, anw, hnw, fnw, wout_full, gmat, emat, selmat)


def _dproj_build(dq, dk, dv, dag, pos):
    TT = 512

    def body(dq1, dq4, dq16, dk1, dk4, dk16, dv1, dv4, dv16, dag_r, pos_r, dproj_o, scr_a, scr_b, scr_c):
        def unperm_sum(r1, r4, r16):
            u4, u16 = _unperm_load(r4, r16, scr_a, scr_b, scr_c)
            return r1[...] + u4 + u16

        cosf, s1, s2 = _rope_tables(pos_r[...])
        dproj_o[:, 0:512] = _rope_bwd(unperm_sum(dq1, dq4, dq16), cosf, s1, s2).astype(BF16)
        dproj_o[:, 512:1024] = _rope_bwd(unperm_sum(dk1, dk4, dk16), cosf, s1, s2).astype(BF16)
        dproj_o[:, 1024:1536] = unperm_sum(dv1, dv4, dv16).astype(BF16)
        dproj_o[:, 1536:2048] = dag_r[...]

    tok = lambda w: pl.BlockSpec((TT, w), lambda i: (i, 0))
    d4 = pl.BlockSpec((4, TT // 4, AW), lambda i: (0, i, 0))
    d16 = pl.BlockSpec((16, TT // 16, AW), lambda i: (0, i, 0))
    return pl.pallas_call(
        body, name="dproj_build", grid=(T // TT,),
        in_specs=[tok(AW), d4, d16] * 3 + [tok(AW), tok(1)],
        out_specs=tok(NCOL // 2),
        out_shape=jax.ShapeDtypeStruct((T, NCOL // 2), BF16),
        scratch_shapes=[pltpu.VMEM((4, TT, 128), F32)] * 3,
        compiler_params=_cp(("parallel",)),
    )(*dq, *dk, *dv, dag, pos)


def _bwd_x(dproj_a, dproj_h, x, dx2, mixw, w_full, rin, rinb, small4, small6, pout_own, pout_rem):
    TT = 256
    NT = T // TT

    def body(dpa_r, dph_r, x_r, dx2_r, mw_r, w_r, rin_r, rinb_r, s4_r, s6_r, poo_r, por_r,
             gx_o, pin_o, pinr_o, sall_o, fin_o, fout_o, sbuf, v_own, v_rem, vo_own, vo_rem, sin, sout, got_in,
             got_out, send_sems, recv_sems, loc_sems, share_send, share_recv, fin_sems):
        i = pl.program_id(0)
        loc, rem = _chip_copies(_w_in_piece, rin_r, rinb_r, pin_o, pinr_o, send_sems, recv_sems, loc_sems.at[0])

        @pl.when(i == 0)
        def _():
            sbuf[...] = jnp.zeros_like(sbuf)
            for cp in loc + rem:
                cp.start()

        dhn = _mm_nt(dpa_r[...], w_r[:, 0:NCOL // 2]) + _mm_nt(dph_r[...], w_r[:, NCOL // 2:NCOL])
        xv = x_r[...]
        r = lax.rsqrt(jnp.mean(xv * xv, axis=-1, keepdims=True) + EPS)
        dxw = dhn * mw_r[...]
        gx_o[...] = dx2_r[...] + r * dxw - xv * ((r * r * r) * jnp.mean(dxw * xv, axis=-1, keepdims=True))
        sbuf[16:17, :] += jnp.sum(dhn * (xv * r), axis=0, keepdims=True)

        @pl.when(i == NT - 1)
        def _():
            sbuf[0:8, :] = s4_r[...]
            sbuf[8:16, :] = s6_r[...]
            sloc, srem = _small_copies(sbuf, sall_o, send_sems, recv_sems, loc_sems.at[1])
            for cp in sloc + srem:
                cp.start()
            for cp in rem + srem:
                cp.wait_recv()
            for cp in rem + srem:
                cp.wait_send()
            for cp in loc + sloc:
                cp.wait()
            mx, my, c = lax.axis_index("x"), lax.axis_index("y"), lax.axis_index("c")
            loads = [pltpu.make_async_copy(pin_o, v_own, fin_sems.at[0]),
                     pltpu.make_async_copy(pinr_o, v_rem, fin_sems.at[1]),
                     pltpu.make_async_copy(poo_r, vo_own, fin_sems.at[2]),
                     pltpu.make_async_copy(por_r, vo_rem, fin_sems.at[3])]
            for cp in loads:
                cp.start()
            for cp in loads:
                cp.wait()
            sout[...] = ((vo_own[...] + vo_rem[0].astype(F32)) + vo_rem[1].astype(F32)) + vo_rem[2].astype(F32)
            sin[...] = ((v_own[...] + v_rem[0].astype(F32)) + v_rem[1].astype(F32)) + v_rem[2].astype(F32)
            swap = [pltpu.make_async_remote_copy(src_ref=sin, dst_ref=got_in, send_sem=share_send.at[0],
                                                 recv_sem=share_recv.at[0], device_id=(mx, my, 1 - c),
                                                 device_id_type=MESH),
                    pltpu.make_async_remote_copy(src_ref=sout, dst_ref=got_out, send_sem=share_send.at[1],
                                                 recv_sem=share_recv.at[1], device_id=(mx, my, 1 - c),
                                                 device_id_type=MESH)]
            for cp in swap:
                cp.start()
            mine = [pltpu.make_async_copy(sin, fin_o.at[c], fin_sems.at[0]),
                    pltpu.make_async_copy(sout, fout_o.at[c], fin_sems.at[1])]
            for cp in mine:
                cp.start()
            for cp in swap:
                cp.wait_recv()
            theirs = [pltpu.make_async_copy(got_in, fin_o.at[1 - c], fin_sems.at[2]),
                      pltpu.make_async_copy(got_out, fout_o.at[1 - c], fin_sems.at[3])]
            for cp in theirs:
                cp.start()
            for cp in swap:
                cp.wait_send()
            for cp in mine + theirs:
                cp.wait()

    tok = lambda w: pl.BlockSpec((TT, w), lambda i: (i, 0))
    const = lambda shape: pl.BlockSpec(shape, lambda i: (0,) * len(shape))
    hbm = pl.BlockSpec(memory_space=pltpu.HBM)
    return pl.pallas_call(
        body, name="bwd_x", grid=(NT,),
        in_specs=[tok(NCOL // 2), tok(NCOL // 2), tok(D), tok(D), const((1, D)), const((D, NCOL)), hbm, hbm,
                  const((8, D)), const((8, D)), hbm, hbm],
        out_specs=[tok(D), hbm, hbm, hbm, hbm, hbm],
        out_shape=[jax.ShapeDtypeStruct((T, D), F32),
                   jax.ShapeDtypeStruct((512, 1024), F32), jax.ShapeDtypeStruct((3, 512, 1024), BF16),
                   jax.ShapeDtypeStruct((8, 24, D), F32),
                   jax.ShapeDtypeStruct((2, 512, 1024), F32), jax.ShapeDtypeStruct((2, 128, D), F32)],
        scratch_shapes=[pltpu.VMEM((24, D), F32),
                        pltpu.VMEM((512, 1024), F32), pltpu.VMEM((3, 512, 1024), BF16),
                        pltpu.VMEM((128, D), F32), pltpu.VMEM((3, 128, D), BF16),
                        pltpu.VMEM((512, 1024), F32), pltpu.VMEM((128, D), F32),
                        pltpu.VMEM((512, 1024), F32), pltpu.VMEM((128, D), F32),
                        pltpu.SemaphoreType.DMA((10,)), pltpu.SemaphoreType.DMA((10,)), pltpu.SemaphoreType.DMA((2,)),
                        pltpu.SemaphoreType.DMA((2,)), pltpu.SemaphoreType.DMA((2,)), pltpu.SemaphoreType.DMA((4,))],
        compiler_params=_cp(("arbitrary",)),
    )(dproj_a, dproj_h, x, dx2, mixw, w_full, rin, rinb, small4, small6, pout_own, pout_rem)


def _grad_w_in(hn, dproj_a, dproj_h):
    TK = 2048
    NK = T // TK

    def body(hnt_r, dpa_r, dph_r, rin_o, rinb_o, acc, rbuf, obuf, obufb, send_sems, recv_sems, wb_sems):
        j = pl.program_id(0)
        kk = pl.program_id(1)
        x, y, c = lax.axis_index("x"), lax.axis_index("y"), lax.axis_index("c")
        mine = pl.ds(pl.multiple_of(c * 512, 512), 512)
        theirs = pl.ds(pl.multiple_of((1 - c) * 512, 512), 512)

        def send(jj):
            return pltpu.make_async_remote_copy(
                src_ref=acc.at[jj % 2, theirs, :], dst_ref=rbuf.at[jj], send_sem=send_sems.at[jj],
                recv_sem=recv_sems.at[jj], device_id=(x, y, 1 - c), device_id_type=MESH)

        def writeback(jj):
            cols = pl.ds(jj * 1024, 1024)
            return [pltpu.make_async_copy(obuf.at[jj % 2], rin_o.at[:, cols], wb_sems.at[jj % 2]),
                    pltpu.make_async_copy(obufb.at[jj % 2], rinb_o.at[:, cols], wb_sems.at[2 + jj % 2])]

        def wait_writeback(jj):
            for cp in writeback(jj):
                cp.wait()

        def finalize(jj):
            send(jj).wait_recv()
            red = acc[jj % 2, mine, :] + rbuf[jj]
            obuf[jj % 2] = red
            obufb[jj % 2] = red.astype(BF16)
            for cp in writeback(jj):
                cp.start()

        prod = _mm(hnt_r[...], jnp.where(j < 2, dpa_r[...], dph_r[...]))

        @pl.when(kk == 0)
        def _():
            for jj in (2, 3):
                @pl.when(j == jj)
                def _():
                    send(jj - 2).wait_send()
            acc[j % 2] = prod

        @pl.when(kk > 0)
        def _():
            acc[j % 2] += prod

        @pl.when(kk == NK - 1)
        def _():
            for jj in range(4):
                @pl.when(j == jj)
                def _():
                    send(jj).start()
                    if jj in (1, 2):
                        finalize(jj - 1)
                    if jj == 3:
                        wait_writeback(0)
                        finalize(2)
                        wait_writeback(1)
                        finalize(3)
                        wait_writeback(2)
                        wait_writeback(3)
                        send(2).wait_send()
                        send(3).wait_send()

    hbm = pl.BlockSpec(memory_space=pltpu.HBM)
    return pl.pallas_call(
        body, name="grad_w_in", grid=(4, NK),
        in_specs=[pl.BlockSpec((D, TK), lambda j, kk: (0, kk)),
                  pl.BlockSpec((TK, 1024), lambda j, kk: (jnp.where(j < 2, kk, NK - 1), jnp.minimum(j, 1))),
                  pl.BlockSpec((TK, 1024), lambda j, kk: (jnp.where(j < 2, 0, kk), jnp.maximum(j - 2, 0)))],
        out_specs=[hbm, hbm],
        out_shape=[jax.ShapeDtypeStruct((512, NCOL), F32), jax.ShapeDtypeStruct((512, NCOL), BF16)],
        scratch_shapes=[pltpu.VMEM((2, D, 1024), F32), pltpu.VMEM((4, 512, 1024), F32), pltpu.VMEM((2, 512, 1024), F32),
                        pltpu.VMEM((2, 512, 1024), BF16),
                        pltpu.SemaphoreType.DMA((4,)), pltpu.SemaphoreType.DMA((4,)), pltpu.SemaphoreType.DMA((4,))],
        compiler_params=_cp(("arbitrary", "arbitrary")),
    )(hn, dproj_a, dproj_h)


def _w_in_piece(ref, j):
    return ref.at[:, pl.ds(j * 1024, 1024)]


def _w_out_piece(ref, j):
    return ref.at[pl.ds(j * 128, 128), :]


def _chip_copies(piece, src_r, srcb_r, own_o, rem_o, send_sems, recv_sems, loc_sem):
    x, y, c = lax.axis_index("x"), lax.axis_index("y"), lax.axis_index("c")
    chips = [(1 - x, y), (x, 1 - y), (1 - x, 1 - y)]
    loc = [pltpu.make_async_copy(piece(src_r, 2 * x + y), own_o, loc_sem)]
    rem = [pltpu.make_async_remote_copy(
        src_ref=piece(srcb_r, 2 * px + py), dst_ref=rem_o.at[k], send_sem=send_sems.at[k],
        recv_sem=recv_sems.at[k], device_id=(px, py, c), device_id_type=MESH) for k, (px, py) in enumerate(chips)]
    return loc, rem


def _small_copies(small_r, sall_o, send_sems, recv_sems, loc_sem):
    x, y, c = lax.axis_index("x"), lax.axis_index("y"), lax.axis_index("c")
    me = 4 * x + 2 * y + c
    loc = [pltpu.make_async_copy(small_r, sall_o.at[me], loc_sem)]
    rem = []
    k = 3
    for fx in range(2):
        for fy in range(2):
            for fc in range(2):
                if fx or fy or fc:
                    peer = (1 - x if fx else x, 1 - y if fy else y, 1 - c if fc else c)
                    rem.append(pltpu.make_async_remote_copy(
                        src_ref=small_r, dst_ref=sall_o.at[me], send_sem=send_sems.at[k],
                        recv_sem=recv_sems.at[k], device_id=peer, device_id_type=MESH))
                    k += 1
    return loc, rem


def _adamw_math(w, g, m, v):
    m = B1 * m + (1.0 - B1) * g
    v = B2 * v + (1.0 - B2) * (g * g)
    m_hat = m / (1.0 - B1 ** STEP)
    v_hat = v / (1.0 - B2 ** STEP)
    delta = -LR * (m_hat / (jnp.sqrt(v_hat) + AEPS) + WD * w)
    return delta, m, v


def _adamw(big_in, big_out, sall, params):
    def body(*refs):
        wi, gi, mi, vi, wo, go, mo, vo, sall_r = refs[:9]
        ins = refs[9:24]
        di_o, mi_o, vi_o, do_o, mo_o, vo_o = refs[24:30]
        outs = refs[30:]
        d, mm, vv = _adamw_math(wi[...], gi[...], mi[...], vi[...])
        di_o[...] = d
        mi_o[...] = mm
        vi_o[...] = vv

        @pl.when(pl.program_id(0) == 0)
        def _():
            d, mm, vv = _adamw_math(wo[...], go[...], mo[...], vo[...])
            do_o[...] = d
            mo_o[...] = mm
            vo_o[...] = vv
            tot = sall_r[0]
            for dv in range(1, 8):
                tot = tot + sall_r[dv]
            grads = [tot[16:17, :], tot[1:2, 0:AW], tot[1:2, AW:], tot[8:10, 0:HW], tot[0:1, :]]
            outs[0][...] = tot[2:3, 0:1]
            for p in range(5):
                w_r, m_r, v_r = ins[3 * p:3 * p + 3]
                g = grads[p]
                d, mm, vv = _adamw_math(w_r[...], g, m_r[...], v_r[...])
                outs[1 + 4 * p][...] = g
                outs[2 + 4 * p][...] = d
                outs[3 + 4 * p][...] = mm
                outs[4 + 4 * p][...] = vv

    flat = [a for p in params for a in p]
    shapes = [jax.ShapeDtypeStruct((D, 1024), F32)] * 3 + [jax.ShapeDtypeStruct((256, D), F32)] * 3
    shapes += [jax.ShapeDtypeStruct((1, 1), F32)]
    for p in params:
        shapes += [jax.ShapeDtypeStruct(p[0].shape, F32)] * 4
    vm = pl.BlockSpec(memory_space=pltpu.VMEM)
    rows = pl.BlockSpec((256, 1024), lambda i: (i, 0))
    whole = pl.BlockSpec((256, D), lambda i: (0, 0))
    return pl.pallas_call(
        body, name="adamw", grid=(4,),
        in_specs=[rows] * 4 + [whole] * 4 + [vm] * 16, out_specs=[rows] * 3 + [whole] * 3 + [vm] * 21,
        out_shape=shapes,
        compiler_params=_cp(("arbitrary",)),
    )(*big_in, *big_out, sall, *flat)


def kernel(x, positions, w_in, w_out, mix_norm_w, attn_out_norm_w, hgrn_out_norm_w, hgrn_lb_raw, final_norm_w, loss_target, m_w_in, m_w_out, m_mix_norm_w, m_attn_out_norm_w, m_hgrn_out_norm_w, m_hgrn_lb_raw, m_final_norm_w, v_w_in, v_w_out, v_mix_norm_w, v_attn_out_norm_w, v_hgrn_out_norm_w, v_hgrn_lb_raw, v_final_norm_w):
    xs = x.reshape(T, D)
    tgt = loss_target.reshape(T, D)
    pos = positions.reshape(T, 1)
    fnw = final_norm_w.reshape(1, D)

    ti = np.arange(TH)
    tri_np = ((ti[:, None] // CHUNK == ti[None, :] // CHUNK) & (ti[None, :] <= ti[:, None])).astype(np.float32)
    tri = jnp.asarray(tri_np, BF16)
    trit = jnp.asarray(tri_np.T, BF16)
    hi_ = np.arange(AW) // HEAD
    gmat = jnp.asarray((hi_[:256, None] == hi_[None, :256]).astype(np.float32) / HEAD, BF16)
    emat_np = (np.arange(128)[:, None] == hi_[None, :]).astype(np.float32)
    sel_np = (8 + hi_[:, None] == np.arange(128)[None, :]).astype(np.float32)
    emat = jnp.asarray(emat_np, BF16)
    selmat = jnp.asarray(sel_np, BF16)

    jm_arr = (2 * lax.axis_index("x") + lax.axis_index("y")).astype(jnp.int32).reshape(1)
    (hn, q1, k1, v1, q4, k4, v4, q16, k16, v16, ag, hq, hf, hi, hg, w_full, wout4) = _fwd_in(
        xs, pos, mix_norm_w, w_in.reshape(D, 1024), w_out.reshape(256, D), jm_arr)
    wout_full = wout4.reshape(D, D)
    flat = lambda a: a.reshape(T, AW)
    o1, l1 = _attn_fwd(q1, k1, v1, T // BLK, "attn_fwd_d1")
    o4, l4 = _attn_fwd(flat(q4), flat(k4), flat(v4), T // 4 // BLK, "attn_fwd_d4")
    o16, l16 = _attn_fwd(flat(q16), flat(k16), flat(v16), T // 16 // BLK, "attn_fwd_d16")
    rec, sall = _hgrn_fwd(hq, hf, hi, hgrn_lb_raw, tri)

    (dx2, do1, do4, do16, st1, st4, st16, drec, dag, dhg, rout, routb, small4) = _fwd_out(
        o1, o4.reshape(4, T // 4, AW), o16.reshape(16, T // 16, AW),
        l1, l4.reshape(4, T // 4, 128), l16.reshape(16, T // 16, 128),
        rec, ag, hg, xs, tgt, attn_out_norm_w, hgrn_out_norm_w, fnw, wout_full, gmat, emat, selmat)

    fst = lambda a: a.reshape(T, 128)
    dq1, dk1, dv1 = _attn_bwd(q1, k1, v1, do1, st1, T // BLK, "attn_bwd_d1")
    dq4, dk4, dv4 = _attn_bwd(flat(q4), flat(k4), flat(v4), flat(do4), fst(st4), T // 4 // BLK, "attn_bwd_d4")
    dq16, dk16, dv16 = _attn_bwd(flat(q16), flat(k16), flat(v16), flat(do16), fst(st16), T // 16 // BLK,
                                 "attn_bwd_d16")
    dproj_h, small6, pout_own, pout_rem = _hgrn_bwd(hq, hf, hi, hgrn_lb_raw, tri, trit, drec, sall, dhg,
                                                    rout, routb)

    r4 = lambda a: a.reshape(4, T // 4, AW)
    r16 = lambda a: a.reshape(16, T // 16, AW)
    dproj_a = _dproj_build((dq1, r4(dq4), r16(dq16)), (dk1, r4(dk4), r16(dk16)), (dv1, r4(dv4), r16(dv16)),
                           dag, pos)
    rin, rinb = _grad_w_in(hn, dproj_a, dproj_h)
    gx, _, _, small_all, fin, fout = _bwd_x(dproj_a, dproj_h, xs, dx2, mix_norm_w, w_full, rin, rinb,
                                            small4, small6, pout_own, pout_rem)
    g_w_in = fin.reshape(D, 1024)
    g_w_out = fout.reshape(256, D)

    params = [(mix_norm_w, m_mix_norm_w, v_mix_norm_w),
              (attn_out_norm_w, m_attn_out_norm_w, v_attn_out_norm_w),
              (hgrn_out_norm_w, m_hgrn_out_norm_w, v_hgrn_out_norm_w),
              (hgrn_lb_raw, m_hgrn_lb_raw, v_hgrn_lb_raw),
              (fnw, m_final_norm_w.reshape(1, D), v_final_norm_w.reshape(1, D))]
    d_in, nm_in, nv_in, d_out, nm_out, nv_out, *so = _adamw(
        (w_in.reshape(D, 1024), g_w_in, m_w_in.reshape(D, 1024), v_w_in.reshape(D, 1024)),
        (w_out.reshape(256, D), g_w_out, m_w_out.reshape(256, D), v_w_out.reshape(256, D)), small_all, params)
    loss = so[0].reshape(())
    g_s = [so[1 + 4 * p] for p in range(5)]
    d_s = [so[2 + 4 * p] for p in range(5)]
    m_s = [so[3 + 4 * p] for p in range(5)]
    v_s = [so[4 + 4 * p] for p in range(5)]
    for lst in (g_s, d_s, m_s, v_s):
        lst[4] = lst[4].reshape(D)

    return (loss, gx.reshape(1, T, D),
            g_w_in.reshape(1, D, 1024), g_w_out.reshape(1, 256, D), *g_s,
            d_in.reshape(1, D, 1024), d_out.reshape(1, 256, D), *d_s,
            nm_in.reshape(1, D, 1024), nm_out.reshape(1, 256, D), *m_s,
            nv_in.reshape(1, D, 1024), nv_out.reshape(1, 256, D), *v_s)
```

```python
import functools

import numpy as np
import jax
import jax.numpy as jnp
from jax import lax
from jax.experimental import pallas as pl
from jax.experimental.pallas import tpu as pltpu

F32 = jnp.float32
BF16 = jnp.bfloat16

T = 4096
D = 1024
AW = 512
HW = 512
NCOL = 4096
HEAD = 64
BLK = 128
CHUNK = 64
EPS = 1e-6
SCALE = HEAD ** -0.5
NEG = -1e30
ROPE_THETA = 500000.0
INV_FREQ = [float(v) for v in
            (np.float32(ROPE_THETA) ** (-(np.arange(8, dtype=np.float32)) * np.float32(0.125)))]
LR, B1, B2, AEPS, WD, STEP = 0.001, 0.9, 0.999, 1e-08, 0.01, 10
VMEM_LIMIT = 56 * 1024 * 1024
MESH = pl.DeviceIdType.MESH


def _cp(sem=None, **kw):
    return pltpu.CompilerParams(dimension_semantics=sem, vmem_limit_bytes=VMEM_LIMIT, **kw)


def _mm(a, b):
    return jnp.dot(a, b, preferred_element_type=F32)


def _mm_nt(a, b):
    return lax.dot_general(a, b, (((1,), (1,)), ((), ())), preferred_element_type=F32)


def _mm_tn(a, b):
    return lax.dot_general(a, b, (((0,), (0,)), ((), ())), preferred_element_type=F32)


def _mm_exact_l(mat_bf, x):
    h = x.astype(BF16)
    l = (x - h.astype(F32)).astype(BF16)
    return _mm(mat_bf, h) + _mm(mat_bf, l)


def _mm_exact_r(x, mat_bf):
    h = x.astype(BF16)
    l = (x - h.astype(F32)).astype(BF16)
    return _mm(h, mat_bf) + _mm(l, mat_bf)


def _sigmoid(x):
    return 0.5 * jnp.tanh(0.5 * x) + 0.5


def _rope_tables(pos):
    lane = lax.broadcasted_iota(jnp.int32, (1, 128), 1)
    jl = lane & 63
    fi = jl & 7
    inv = jnp.zeros((1, 128), F32)
    for kk in range(8):
        inv = jnp.where(fi == kk, INV_FREQ[kk], inv)
    ang = pos.astype(F32) * inv
    c = jnp.cos(ang)
    s = jnp.sin(ang)
    cosf = jnp.where(jl < 16, c, 1.0)
    s1 = jnp.where(jl < 8, -s, 0.0)
    s2 = jnp.where((jl >= 8) & (jl < 16), s, 0.0)
    return cosf, s1, s2


def _rope(t, cosf, s1, s2):
    parts = []
    for ci in range(t.shape[1] // 128):
        tc = t[:, ci * 128:(ci + 1) * 128]
        parts.append(tc * cosf + pltpu.roll(tc, 120, 1) * s1 + pltpu.roll(tc, 8, 1) * s2)
    return jnp.concatenate(parts, axis=1)


def _rope_bwd(g, cosf, s1, s2):
    parts = []
    for ci in range(g.shape[1] // 128):
        gc = g[:, ci * 128:(ci + 1) * 128]
        parts.append(gc * cosf + pltpu.roll(gc * s1, 8, 1) + pltpu.roll(gc * s2, 120, 1))
    return jnp.concatenate(parts, axis=1)


def _perm_store(val, scr, scr2, o1, o4, o16, dt):
    n = val.shape[0]
    q = n // 4
    o1[...] = val.astype(dt)
    for ci in range(val.shape[1] // 128):
        cs = slice(ci * 128, (ci + 1) * 128)
        scr[ci] = val[:, cs]
        for r4 in range(4):
            part = scr[ci, pl.ds(r4, q, stride=4), :]
            o4[r4, :, cs] = part.astype(dt)
            scr2[ci, r4 * q:(r4 + 1) * q, :] = part
        for r4 in range(4):
            for b in range(4):
                o16[r4 + 4 * b, :, cs] = scr2[ci, pl.ds(r4 * q + b, q // 4, stride=4), :].astype(dt)


def _unperm_load(r4, r16, scr_a, scr_b, scr_c):
    n = scr_a.shape[1]
    q = n // 4
    nc = r4.shape[-1] // 128
    for ci in range(nc):
        cs = slice(ci * 128, (ci + 1) * 128)
        for rr in range(4):
            scr_a[ci, pl.ds(rr, q, stride=4), :] = r4[rr, :, cs].astype(F32)
        for rr in range(4):
            for b in range(4):
                scr_c[ci, pl.ds(rr * q + b, q // 4, stride=4), :] = r16[rr + 4 * b, :, cs].astype(F32)
        for rr in range(4):
            scr_b[ci, pl.ds(rr, q, stride=4), :] = scr_c[ci, rr * q:(rr + 1) * q, :]
    return (jnp.concatenate([scr_a[ci] for ci in range(nc)], axis=1),
            jnp.concatenate([scr_b[ci] for ci in range(nc)], axis=1))


def _fwd_in(x, pos, mixw, w_in, w_out, jm_arr):
    TT = 512
    NT = T // TT

    def body(jm_ref, x_ref, pos_ref, mw_ref, win_ref, wout_ref,
             hnt_ref, q1, k1, v1, q4, k4, v4, q16, k16, v16, ag, hq, hf, hi, hg, wfull_o, woutfull_o,
             wbuf, wobuf, hn_all, scr, scr2, stage, send_sems, recv_sems, loc_sems):
        s = pl.program_id(0)
        i = pl.program_id(1)
        mx, my, c = lax.axis_index("x"), lax.axis_index("y"), lax.axis_index("c")
        me, sibling = (mx, my, c), (mx, my, 1 - c)
        chips = [(mx, 1 - my), (1 - mx, my), (1 - mx, 1 - my)]
        jm = 2 * mx + my
        rows_in = [pl.ds(pl.multiple_of(h * 512, 512), 512) for h in (c, 1 - c)]
        rows_out = [pl.ds(pl.multiple_of(h * 128, 128), 128) for h in (c, 1 - c)]

        def blk(k):
            return lax.bitwise_xor(jm, k + 1)

        def rc(n, ref, to):
            return pltpu.make_async_remote_copy(src_ref=ref, dst_ref=ref, send_sem=send_sems.at[n],
                                                recv_sem=recv_sems.at[n], device_id=to, device_id_type=MESH)

        halves = [pl.ds(0, 512), pl.ds(512, 512)]
        send_in = lambda k, h: rc(12 + 2 * k + h, wbuf.at[jm, rows_in[0], halves[h]], (*chips[k], c))
        got_in = lambda k, h: rc(12 + 2 * k + h, wbuf.at[blk(k), rows_in[0], halves[h]], me)
        relay = lambda h: rc(16 + h, wbuf.at[blk(h), rows_in[0], halves[h]], (*chips[1 - h], c))
        got_relay = lambda h: rc(16 + h, wbuf.at[blk(2), rows_in[0], halves[h]], me)
        send_out = lambda k: rc(3 + k, wobuf.at[jm, rows_out[0], :], (*chips[k], c))
        got_out = lambda k: rc(3 + k, wobuf.at[blk(k), rows_out[0], :], me)
        pass_in = lambda k: rc(6 + k, wbuf.at[blk(k), rows_in[0], :], sibling)
        pass_out = lambda k: rc(9 + k, wobuf.at[blk(k), rows_out[0], :], sibling)
        passed_in = lambda k: rc(6 + k, wbuf.at[blk(k), rows_in[1], :], me)
        passed_out = lambda k: rc(9 + k, wobuf.at[blk(k), rows_out[1], :], me)

        def keep(j, n):
            return pltpu.make_async_copy(wbuf.at[j], wfull_o.at[:, pl.ds(j * 1024, 1024)], loc_sems.at[n])

        @pl.when((s == 0) & (i == 0))
        def _():
            for p in range(5):
                src = win_ref.at[pl.ds(p * 256, 256), :] if p < 4 else wout_ref
                load = pltpu.make_async_copy(src, stage, loc_sems.at[4])
                load.start()
                load.wait()
                if p < 4:
                    wbuf[jm, p * 256:(p + 1) * 256, :] = stage[...].astype(BF16)
                else:
                    wobuf[jm] = stage[...].astype(BF16)
            for k in range(2):
                for h in range(2):
                    send_in(k, h).start()
            keep(jm, 0).start()

        def arrive(k):
            if k == 0:
                for kk in range(2):
                    for h in range(2):
                        got_in(kk, h).wait_recv()
                relay(0).start()
                relay(1).start()
            if k == 2:
                got_relay(0).wait_recv()
                got_relay(1).wait_recv()
            pass_in(k).start()
            passed_in(k).wait_recv()
            keep(blk(k), k + 1).start()
            if k == 2:
                for kk in range(3):
                    send_out(kk).start()

        pl.when((s == 1) & (i == 0))(functools.partial(arrive, 0))

        @pl.when((s == 2) & (i == 0))
        def _():
            arrive(1)
            arrive(2)

        tile = pl.ds(pl.multiple_of(i * TT, TT), TT)

        @pl.when(s == 0)
        def _():
            xv = x_ref[...]
            r = lax.rsqrt(jnp.mean(xv * xv, axis=-1, keepdims=True) + EPS)
            hnf = (xv * r) * mw_ref[...]
            hn_all[tile, :] = hnf.astype(BF16)
            hnt_ref[...] = hnf.T.astype(BF16)

        def project(jj):
            hn = hn_all[tile, :]
            lo = _mm(hn, wbuf[jj, :, 0:512])
            hi_cols = _mm(hn, wbuf[jj, :, 512:1024])
            if jj == 0:
                cosf, s1, s2 = _rope_tables(pos_ref[...])
                _perm_store(_rope(lo, cosf, s1, s2), scr, scr2, q1, q4, q16, BF16)
                _perm_store(_rope(hi_cols, cosf, s1, s2), scr, scr2, k1, k4, k16, BF16)
            elif jj == 1:
                _perm_store(lo, scr, scr2, v1, v4, v16, BF16)
                ag[...] = hi_cols.astype(BF16)
            elif jj == 2:
                hq[...] = lo.astype(BF16)
                hf[...] = hi_cols.astype(BF16)
            else:
                hi[...] = lo.astype(BF16)
                hg[...] = hi_cols.astype(BF16)

        def project_block(j):
            for jj in range(4):
                pl.when(j == jj)(functools.partial(project, jj))

        @pl.when(s < 2)
        def _():
            project_block(lax.bitwise_xor(jm, s))

        @pl.when(s == 2)
        def _():
            project_block(lax.bitwise_xor(jm, 2))
            project_block(lax.bitwise_xor(jm, 3))

        @pl.when((s == 2) & (i == NT - 1))
        def _():
            for k in range(3):
                got_out(k).wait_recv()
                pass_out(k).start()
            for k in range(3):
                passed_out(k).wait_recv()
            out = pltpu.make_async_copy(wobuf, woutfull_o, loc_sems.at[4])
            out.start()
            for h in range(2):
                relay(h).wait_send()
                for k in range(2):
                    send_in(k, h).wait_send()
            for k in range(3):
                send_out(k).wait_send()
                pass_in(k).wait_send()
                pass_out(k).wait_send()
            keep(jm, 0).wait()
            for k in range(3):
                keep(blk(k), k + 1).wait()
            out.wait()

    def at_stage_of(jb):
        def index(s, i, jm_ref):
            sa = jnp.minimum(lax.bitwise_xor(jm_ref[0], jb), 2)
            return jnp.where(s < sa, 0, jnp.where(s == sa, i, NT - 1))
        return index

    tok = lambda w, jb: pl.BlockSpec((TT, w), lambda s, i, jm_ref: (at_stage_of(jb)(s, i, jm_ref), 0))
    d4 = lambda jb: pl.BlockSpec((4, TT // 4, AW), lambda s, i, jm_ref: (0, at_stage_of(jb)(s, i, jm_ref), 0))
    d16 = lambda jb: pl.BlockSpec((16, TT // 16, AW), lambda s, i, jm_ref: (0, at_stage_of(jb)(s, i, jm_ref), 0))
    hbm = pl.BlockSpec(memory_space=pltpu.HBM)
    sd = lambda shape, dt: jax.ShapeDtypeStruct(shape, dt)
    in_own_stage = lambda s, i: jnp.where(s == 0, i, NT - 1)
    grid_spec = pltpu.PrefetchScalarGridSpec(
        num_scalar_prefetch=1, grid=(3, NT),
        in_specs=[pl.BlockSpec((TT, D), lambda s, i, jm_ref: (in_own_stage(s, i), 0)),
                  pl.BlockSpec((TT, 1), lambda s, i, jm_ref: (i, 0)),
                  pl.BlockSpec((1, D), lambda s, i, jm_ref: (0, 0)), hbm, hbm],
        out_specs=[pl.BlockSpec((D, TT), lambda s, i, jm_ref: (0, in_own_stage(s, i))),
                   tok(AW, 0), tok(AW, 0), tok(AW, 1), d4(0), d4(0), d4(1), d16(0), d16(0), d16(1),
                   tok(AW, 1), tok(AW, 2), tok(AW, 2), tok(AW, 3), tok(AW, 3), hbm, hbm],
        scratch_shapes=[pltpu.VMEM((4, D, 1024), BF16), pltpu.VMEM((4, 256, D), BF16), pltpu.VMEM((T, D), BF16),
                        pltpu.VMEM((4, TT, 128), F32), pltpu.VMEM((4, TT, 128), F32), pltpu.VMEM((256, 1024), F32),
                        pltpu.SemaphoreType.DMA((18,)),
                        pltpu.SemaphoreType.DMA((18,)), pltpu.SemaphoreType.DMA((6,))])
    return pl.pallas_call(
        body, name="fwd_in", grid_spec=grid_spec,
        out_shape=[sd((D, T), BF16)] + [sd((T, AW), BF16)] * 3 + [sd((4, T // 4, AW), BF16)] * 3
        + [sd((16, T // 16, AW), BF16)] * 3
        + [sd((T, AW), BF16)] * 5 + [sd((D, NCOL), BF16), sd((4, 256, D), BF16)],
        compiler_params=_cp(("arbitrary", "arbitrary")),
    )(jm_arr, x, pos, mixw, w_in, w_out)


def _band_mask(key_axis, nkeys=2 * BLK):
    shape = (nkeys, 2 * BLK) if key_axis == 0 else (2 * BLK, nkeys)
    kj = lax.broadcasted_iota(jnp.int32, shape, key_axis)
    qi = lax.broadcasted_iota(jnp.int32, shape, 1 - key_axis) & (BLK - 1)
    return (kj >= qi) & (kj <= qi + BLK), kj, qi


def _stack_heads(t2, in_a):
    z = jnp.zeros_like(t2)
    return jnp.concatenate([jnp.where(in_a[0], t2, z), jnp.where(in_a[1], t2, z)], axis=0)


def _attn_fwd(q, k, v, nb, name):
    n = 8
    CH = n * BLK
    halo = nb > n

    def body(*refs):
        if halo:
            q_ref, k_ref, v_ref, kp_ref, vp_ref, o_ref, lse_ref = refs
        else:
            q_ref, k_ref, v_ref, o_ref, lse_ref = refs
        lane = lax.broadcasted_iota(jnp.int32, (1, 128), 1)
        in_a = [lane < HEAD, lane >= HEAD]
        band, kj, _ = _band_mask(1)
        thr0 = jnp.where((n * pl.program_id(0)) % nb == 0, BLK, 0) if halo else BLK
        mask0 = band & (kj >= thr0)
        mask_first = band & (kj >= BLK)
        for b in range(n):
            rs = slice(b * BLK, (b + 1) * BLK)
            stat = jnp.zeros((BLK, 128), F32)
            for hp in range(4):
                cs = slice(hp * 128, (hp + 1) * 128)
                q2s = _stack_heads(q_ref[rs, cs], in_a)
                if b == 0:
                    kprev = kp_ref[:, cs] if halo else k_ref[rs, cs]
                    vprev = vp_ref[:, cs] if halo else v_ref[rs, cs]
                    kk = jnp.concatenate([kprev, k_ref[rs, cs]], axis=0)
                    vv = jnp.concatenate([vprev, v_ref[rs, cs]], axis=0)
                    mask = mask0
                else:
                    kk = k_ref[(b - 1) * BLK:(b + 1) * BLK, cs]
                    vv = v_ref[(b - 1) * BLK:(b + 1) * BLK, cs]
                    mask = mask_first if b % nb == 0 else band
                s = jnp.where(mask, _mm_nt(q2s, kk) * SCALE, NEG)
                m = jnp.max(s, axis=-1, keepdims=True)
                p = jnp.exp(s - m)
                l = jnp.sum(p, axis=-1, keepdims=True)
                o = _mm(p.astype(BF16), vv) / l
                lse = m + jnp.log(l)
                o_ref[rs, cs] = jnp.where(in_a[0], o[:BLK], o[BLK:]).astype(BF16)
                stat = jnp.where(lane == 2 * hp, lse[:BLK], stat)
                stat = jnp.where(lane == 2 * hp + 1, lse[BLK:], stat)
            lse_ref[rs, :] = stat

    cur = pl.BlockSpec((CH, AW), lambda i: (i, 0))
    prev = pl.BlockSpec((BLK, AW), lambda i: (jnp.maximum(n * i - 1, 0), 0))
    return pl.pallas_call(
        body, name=name, grid=(T // CH,),
        in_specs=[cur, cur, cur] + ([prev, prev] if halo else []),
        out_specs=[cur, pl.BlockSpec((CH, 128), lambda i: (i, 0))],
        out_shape=[jax.ShapeDtypeStruct((T, AW), BF16), jax.ShapeDtypeStruct((T, 128), F32)],
        compiler_params=_cp(("parallel",)),
    )(*((q, k, v) + ((k, v) if halo else ())))


def _attn_bwd(q, k, v, do, st, nb, name):
    n = 8
    CH = n * BLK
    NBLK = T // BLK
    halo = nb > n

    def body(*refs):
        if halo:
            (q_ref, k_ref, v_ref, do_ref, st_ref, kp_ref, vp_ref, qn_ref, don_ref, stn_ref,
             dq_ref, dk_ref, dv_ref) = refs
        else:
            q_ref, k_ref, v_ref, do_ref, st_ref, dq_ref, dk_ref, dv_ref = refs
        i = pl.program_id(0)
        lane = lax.broadcasted_iota(jnp.int32, (1, 128), 1)
        in_a = [lane < HEAD, lane >= HEAD]
        band, kj, _ = _band_mask(0)
        thr0 = jnp.where((n * i) % nb == 0, BLK, 0) if halo else BLK
        mask0 = band & (kj >= thr0)
        mask_first = band & (kj >= BLK)

        def stat_rows(st_t, hp):
            lse_r = jnp.concatenate([st_t[2 * hp:2 * hp + 1, :], st_t[2 * hp + 1:2 * hp + 2, :]], axis=1)
            dl_r = jnp.concatenate([st_t[8 + 2 * hp:9 + 2 * hp, :], st_t[9 + 2 * hp:10 + 2 * hp, :]], axis=1)
            return lse_r, dl_r

        st_t = [st_ref[b * BLK:(b + 1) * BLK, :].T for b in range(n)]
        if halo:
            nxt_thr = jnp.where((n * i + n) % nb == 0, 2 * BLK, 0)
            _, kj1, qi1 = _band_mask(0, BLK)
            mask_next = kj1 >= qi1 + nxt_thr
            stn_t = stn_ref[...].T

        for hp in range(4):
            cs = slice(hp * 128, (hp + 1) * 128)
            kb = [k_ref[b * BLK:(b + 1) * BLK, cs] for b in range(n)]
            vb = [v_ref[b * BLK:(b + 1) * BLK, cs] for b in range(n)]
            dk_acc = [jnp.zeros((BLK, 128), F32) for _ in range(n)]
            dv_acc = [jnp.zeros((BLK, 128), F32) for _ in range(n)]
            for b in range(n):
                rs = slice(b * BLK, (b + 1) * BLK)
                q2s = _stack_heads(q_ref[rs, cs], in_a)
                do2s = _stack_heads(do_ref[rs, cs], in_a)
                if b == 0:
                    kprev = kp_ref[:, cs] if halo else kb[0]
                    vprev = vp_ref[:, cs] if halo else vb[0]
                    mask = mask0
                else:
                    kprev, vprev, mask = kb[b - 1], vb[b - 1], (mask_first if b % nb == 0 else band)
                kk = jnp.concatenate([kprev, kb[b]], axis=0)
                vv = jnp.concatenate([vprev, vb[b]], axis=0)
                lse_r, dl_r = stat_rows(st_t[b], hp)
                s_t = jnp.where(mask, _mm_nt(kk, q2s) * SCALE, NEG)
                p_t = jnp.exp(s_t - lse_r)
                ds_t = (p_t * (_mm_nt(vv, do2s) - dl_r)).astype(BF16)
                dkk = _mm(ds_t, q2s) * SCALE
                dvv = _mm(p_t.astype(BF16), do2s)
                dqs = _mm_tn(ds_t, kk) * SCALE
                dq_ref[rs, cs] = jnp.where(in_a[0], dqs[:BLK], dqs[BLK:]).astype(BF16)
                dk_acc[b] += dkk[BLK:]
                dv_acc[b] += dvv[BLK:]
                if b > 0:
                    dk_acc[b - 1] += dkk[:BLK]
                    dv_acc[b - 1] += dvv[:BLK]
            if halo:
                q2s = _stack_heads(qn_ref[:, cs], in_a)
                do2s = _stack_heads(don_ref[:, cs], in_a)
                lse_r, dl_r = stat_rows(stn_t, hp)
                s_t = jnp.where(mask_next, _mm_nt(kb[n - 1], q2s) * SCALE, NEG)
                p_t = jnp.exp(s_t - lse_r)
                ds_t = (p_t * (_mm_nt(vb[n - 1], do2s) - dl_r)).astype(BF16)
                dk_acc[n - 1] += _mm(ds_t, q2s) * SCALE
                dv_acc[n - 1] += _mm(p_t.astype(BF16), do2s)
            for b in range(n):
                dk_ref[b * BLK:(b + 1) * BLK, cs] = dk_acc[b].astype(BF16)
                dv_ref[b * BLK:(b + 1) * BLK, cs] = dv_acc[b].astype(BF16)

    cur = pl.BlockSpec((CH, AW), lambda i: (i, 0))
    cur_st = pl.BlockSpec((CH, 128), lambda i: (i, 0))
    prev = pl.BlockSpec((BLK, AW), lambda i: (jnp.maximum(n * i - 1, 0), 0))
    nxt = pl.BlockSpec((BLK, AW), lambda i: (jnp.minimum(n * i + n, NBLK - 1), 0))
    nxt_st = pl.BlockSpec((BLK, 128), lambda i: (jnp.minimum(n * i + n, NBLK - 1), 0))
    ins = [cur] * 4 + [cur_st] + ([prev, prev, nxt, nxt, nxt_st] if halo else [])
    args = (q, k, v, do, st) + ((k, v, q, do, st) if halo else ())
    return pl.pallas_call(
        body, name=name, grid=(T // CH,),
        in_specs=ins,
        out_specs=[cur] * 3,
        out_shape=[jax.ShapeDtypeStruct((T, AW), BF16)] * 3,
        compiler_params=_cp(("parallel",)),
    )(*args)


TH = 256
NCH = TH // CHUNK


def _hgrn_common(hq_ref, hf_ref, lbr_ref, tri_ref):
    r0 = lbr_ref[0:1, :]
    r1 = lbr_ref[1:2, :]
    mx = jnp.maximum(r0, r1)
    e0 = jnp.exp(r0 - mx)
    e1 = jnp.exp(r1 - mx)
    lb = e0 / (e0 + e1)
    hqv = hq_ref[...].astype(F32)
    sq = _sigmoid(hqv)
    qv = hqv * sq
    sf = _sigmoid(hf_ref[...].astype(F32))
    f = lb + (1.0 - lb) * sf
    kv = 1.0 - f
    g = jnp.log(f)
    cum = _mm_exact_l(tri_ref[...], g)
    dec = jnp.exp(jnp.concatenate([cum[c * CHUNK + CHUNK - 1:(c + 1) * CHUNK, :] for c in range(NCH)], axis=0))
    decb = jnp.concatenate([jnp.broadcast_to(dec[c:c + 1, :], (CHUNK, HW)) for c in range(NCH)], axis=0)
    ea = jnp.exp(cum)
    ena = jnp.exp(-cum)
    eend = decb * ena
    return dict(lb=lb, hq=hqv, sq=sq, q=qv, sf=sf, f=f, k=kv, cum=cum, ea=ea, ena=ena, eend=eend,
                qd=qv * ea, ki=kv * ena, ke=kv * eend, dec=dec)


def _tri_mask(transposed=False):
    ti = lax.broadcasted_iota(jnp.int32, (TH, TH), 1 if transposed else 0)
    si = lax.broadcasted_iota(jnp.int32, (TH, TH), 0 if transposed else 1)
    return (si <= ti) & ((si // CHUNK) == (ti // CHUNK))


def _hgrn_fwd(hq, hf, hi, lbr, tri):
    def body(hq_ref, hf_ref, hi_ref, lbr_ref, tri_ref, rec_ref, sall_ref, st_scr):
        @pl.when(pl.program_id(0) == 0)
        def _():
            st_scr[...] = jnp.zeros_like(st_scr)

        w = _hgrn_common(hq_ref, hf_ref, lbr_ref, tri_ref)
        qd, ki, ke = w["qd"].astype(BF16), w["ki"].astype(BF16), w["ke"].astype(BF16)
        dec = w["dec"]
        vb = hi_ref[...]
        causal = _tri_mask()
        for h in range(4):
            cs = slice(h * 128, (h + 1) * 128)
            att = jnp.where(causal, _mm_nt(qd[:, cs], ki[:, cs]), 0.0)
            o_intra = _mm(att.astype(BF16), vb[:, cs])
            st = st_scr[:, cs]
            for c in range(NCH):
                rs = slice(c * CHUNK, (c + 1) * CHUNK)
                sall_ref[c, :, cs] = st
                rec_ref[rs, cs] = (o_intra[rs] + _mm_nt(qd[rs, cs], st.astype(BF16))).astype(BF16)
                st = dec[c:c + 1, cs] * st + _mm_tn(vb[rs, cs], ke[rs, cs])
            st_scr[:, cs] = st

    tok = pl.BlockSpec((TH, HW), lambda i: (i, 0))
    return pl.pallas_call(
        body, name="hgrn_fwd", grid=(T // TH,),
        in_specs=[tok, tok, tok, pl.BlockSpec((2, HW), lambda i: (0, 0)), pl.BlockSpec((TH, TH), lambda i: (0, 0))],
        out_specs=[tok, pl.BlockSpec((NCH, 128, HW), lambda i: (i, 0, 0))],
        out_shape=[jax.ShapeDtypeStruct((T, HW), BF16), jax.ShapeDtypeStruct((T // CHUNK, 128, HW), F32)],
        scratch_shapes=[pltpu.VMEM((128, HW), F32)],
        compiler_params=_cp(("arbitrary",)),
    )(hq, hf, hi, lbr, tri)


def _hgrn_bwd(hq, hf, hi, lbr, tri, trit, drec, sall, dhg, rout, routb):
    NT = T // TH

    def body(hq_ref, hf_ref, hi_ref, lbr_ref, tri_ref, trit_ref, do_ref, sall_ref, dhg_ref, rout_r, routb_r,
             dph_ref, small_ref, pout_o, poutr_o,
             dst_scr, dlb_scr, dqd_scr, dki_scr, dke_scr, dlast_scr, send_sems, recv_sems, loc_sems):
        step = pl.program_id(0)
        loc, rem = _chip_copies(_w_out_piece, rout_r, routb_r, pout_o, poutr_o, send_sems, recv_sems,
                                loc_sems.at[0])

        @pl.when(step == 0)
        def _():
            dst_scr[...] = jnp.zeros_like(dst_scr)
            dlb_scr[...] = jnp.zeros_like(dlb_scr)
            for cp in loc + rem:
                cp.start()

        w = _hgrn_common(hq_ref, hf_ref, lbr_ref, tri_ref)
        qd, ki, ke = w["qd"].astype(BF16), w["ki"].astype(BF16), w["ke"].astype(BF16)
        dec = w["dec"]
        vb = hi_ref[...]
        dob = do_ref[...].astype(BF16)
        causal = _tri_mask()
        causal_t = _tri_mask(transposed=True)
        for h in range(4):
            cs = slice(h * 128, (h + 1) * 128)
            att_t = jnp.where(causal_t, _mm_nt(ki[:, cs], qd[:, cs]), 0.0).astype(BF16)
            datt_t = jnp.where(causal_t, _mm_nt(vb[:, cs], dob[:, cs]), 0.0).astype(BF16)
            datt = jnp.where(causal, _mm_nt(dob[:, cs], vb[:, cs]), 0.0).astype(BF16)
            dv_intra = _mm(att_t, dob[:, cs])
            dqd_intra = _mm(datt, ki[:, cs])
            dki_scr[:, cs] = _mm(datt_t, qd[:, cs])
            dst = dst_scr[:, cs]
            for c in reversed(range(NCH)):
                rs = slice(c * CHUNK, (c + 1) * CHUNK)
                dec_c = dec[c:c + 1, :]
                st = sall_ref[c, :, cs]
                dstb = dst.astype(BF16)
                dph_ref[rs, 2 * HW + h * 128:2 * HW + (h + 1) * 128] = (
                    dv_intra[rs] + _mm_nt(ke[rs, cs], dstb)).astype(BF16)
                dqd_scr[rs, cs] = dqd_intra[rs] + _mm(dob[rs, cs], st.astype(BF16))
                dke_scr[rs, cs] = _mm(vb[rs, cs], dstb)
                ddec = jnp.sum(dst * st, axis=0, keepdims=True)
                dlast_scr[c:c + 1, cs] = ddec * dec_c[:, cs]
                dst = dec_c[:, cs] * dst + _mm_tn(dob[rs, cs], qd[rs, cs])
            dst_scr[:, cs] = dst
        dqd, dki, dke = dqd_scr[...], dki_scr[...], dke_scr[...]
        dq = dqd * w["ea"]
        dk = dki * w["ena"] + dke * w["eend"]
        dcum = dqd * w["qd"] - dki * w["ki"] - dke * w["ke"]
        dkeke = dke * w["ke"]
        dlastb = jnp.concatenate(
            [jnp.broadcast_to(dlast_scr[c:c + 1, :] + jnp.sum(dkeke[c * CHUNK:(c + 1) * CHUNK], axis=0, keepdims=True),
                              (CHUNK, HW)) for c in range(NCH)], axis=0)
        dg = _mm_exact_l(trit_ref[...], dcum) + dlastb
        df = dg / w["f"] - dk
        lb, sf, sq = w["lb"], w["sf"], w["sq"]
        dph_ref[:, HW:2 * HW] = (df * (1.0 - lb) * sf * (1.0 - sf)).astype(BF16)
        dph_ref[:, 0:HW] = (dq * (sq * (1.0 + w["hq"] * (1.0 - sq)))).astype(BF16)
        dph_ref[:, 3 * HW:4 * HW] = dhg_ref[...]
        dlb_scr[...] += jnp.sum(df * (1.0 - sf), axis=0, keepdims=True)

        @pl.when(step == NT - 1)
        def _():
            gr = dlb_scr[...] * lb * (1.0 - lb)
            small_ref[...] = jnp.zeros_like(small_ref)
            small_ref[0:1, 0:HW] = gr
            small_ref[1:2, 0:HW] = -gr
            for cp in rem:
                cp.wait_recv()
            for cp in rem:
                cp.wait_send()
            for cp in loc:
                cp.wait()

    tok = pl.BlockSpec((TH, HW), lambda i: (NT - 1 - i, 0))
    const = lambda shape: pl.BlockSpec(shape, lambda i: (0,) * len(shape))
    hbm = pl.BlockSpec(memory_space=pltpu.HBM)
    return pl.pallas_call(
        body, name="hgrn_bwd", grid=(NT,),
        in_specs=[tok, tok, tok, const((2, HW)), const((TH, TH)), const((TH, TH)), tok,
                  pl.BlockSpec((NCH, 128, HW), lambda i: (NT - 1 - i, 0, 0)), tok, hbm, hbm],
        out_specs=[pl.BlockSpec((TH, NCOL // 2), lambda i: (NT - 1 - i, 0)), const((8, D)), hbm, hbm],
        out_shape=[jax.ShapeDtypeStruct((T, NCOL // 2), BF16), jax.ShapeDtypeStruct((8, D), F32),
                   jax.ShapeDtypeStruct((128, D), F32), jax.ShapeDtypeStruct((3, 128, D), BF16)],
        scratch_shapes=[pltpu.VMEM((128, HW), F32), pltpu.VMEM((1, HW), F32), pltpu.VMEM((TH, HW), F32),
                        pltpu.VMEM((TH, HW), F32), pltpu.VMEM((TH, HW), F32), pltpu.VMEM((8, HW), F32),
                        pltpu.SemaphoreType.DMA((3,)), pltpu.SemaphoreType.DMA((3,)), pltpu.SemaphoreType.DMA((1,))],
        compiler_params=_cp(("arbitrary",)),
    )(hq, hf, hi, lbr, tri, trit, drec, sall, dhg, rout, routb)


def _fwd_out(o1, o4, o16, l1, l4, l16, rec, ag, hg, x, tgt, anw, hnw, fnw, wout_full, gmat, emat, selmat):
    TT = 256

    def body(o1_r, o4_r, o16_r, l1_r, l4_r, l16_r, rec_r, ag_r, hg_r, x_r, tgt_r, anw_r, hnw_r, fnw_r, wo_r, g_r,
             e_r, sel_r, dx2_o, do1_o, do4_o, do16_o, st1_o, st4_o, st16_o, drec_o, dag_o, dhg_o,
             rout_o, routb_o, small_o, scr_a, scr_b, scr_c, gwout_o, rbuf, send_sems, recv_sems):
        @pl.when(pl.program_id(0) == 0)
        def _():
            gwout_o[...] = jnp.zeros_like(gwout_o)
            small_o[...] = jnp.zeros_like(small_o)

        def unperm(r4, r16):
            return _unperm_load(r4, r16, scr_a, scr_b, scr_c)

        def perm_out(val, p1, p4, p16, dt):
            _perm_store(val, scr_a, scr_b, p1, p4, p16, dt)

        o4u, o16u = unperm(o4_r, o16_r)
        l4c, l16c = unperm(l4_r, l16_r)
        l1c = l1_r[...]
        mxc = jnp.maximum(jnp.maximum(l1c, l4c), l16c)
        w1c, w4c, w16c = jnp.exp(l1c - mxc), jnp.exp(l4c - mxc), jnp.exp(l16c - mxc)
        denc = w1c + w4c + w16c
        lane = lax.broadcasted_iota(jnp.int32, (1, 128), 1)
        lse_c = jnp.where(lane < 8, mxc + jnp.log(denc), 0.0)
        em = e_r[...]
        wn1 = _mm_exact_r(w1c / denc, em)
        wn4 = _mm_exact_r(w4c / denc, em)
        o1v = o1_r[...].astype(F32)
        attn = wn1 * o1v + wn4 * o4u + (1.0 - wn1 - wn4) * o16u
        gm = g_r[...]

        def head_mean_a(t):
            return jnp.concatenate([_mm_exact_r(t[:, :256], gm), _mm_exact_r(t[:, 256:], gm)], axis=1)

        def head_mean_h(t):
            return jnp.concatenate(
                [jnp.broadcast_to(jnp.mean(t[:, h * 128:(h + 1) * 128], axis=-1, keepdims=True), (TT, 128))
                 for h in range(4)], axis=1)

        rs_a = lax.rsqrt(head_mean_a(attn * attn) + EPS)
        n_a = attn * rs_a
        agv = ag_r[...].astype(F32)
        sg_a = _sigmoid(agv)
        si_a = agv * sg_a
        anw_v = anw_r[...]
        y_a = (n_a * anw_v) * si_a
        recv = rec_r[...].astype(F32)
        rs_h = lax.rsqrt(head_mean_h(recv * recv) + EPS)
        n_h = recv * rs_h
        hgv = hg_r[...].astype(F32)
        sg_h = _sigmoid(hgv)
        si_h = hgv * sg_h
        hnw_v = hnw_r[...]
        y_h = (n_h * hnw_v) * si_h
        mixed = jnp.concatenate([y_a, y_h], axis=1).astype(BF16)
        xv = x_r[...]
        x2 = xv + _mm(mixed, wo_r[...])
        r2 = lax.rsqrt(jnp.mean(x2 * x2, axis=-1, keepdims=True) + EPS)
        fnw_v = fnw_r[...]
        xn = x2 * r2
        err = xn * fnw_v - tgt_r[...]
        small_o[2:3, :] += 0.5 * jnp.sum(jnp.mean(err * err, axis=-1, keepdims=True), axis=0, keepdims=True)
        dy = err * (1.0 / D)
        small_o[0:1, :] += jnp.sum(dy * xn, axis=0, keepdims=True)
        dyw = dy * fnw_v
        dx2 = r2 * dyw - x2 * ((r2 * r2 * r2) * jnp.mean(dyw * x2, axis=-1, keepdims=True))
        dx2_o[...] = dx2
        dx2b = dx2.astype(BF16)
        gwout_o[...] += _mm_tn(mixed, dx2b)
        dmix = _mm_nt(dx2b, wo_r[...])
        dm_a, dm_h = dmix[:, :AW], dmix[:, AW:]
        dag_o[...] = (dm_a * (n_a * anw_v) * (sg_a * (1.0 + agv * (1.0 - sg_a)))).astype(BF16)
        dn_a = dm_a * anw_v * si_a
        small_o[1:2, 0:AW] += jnp.sum(dm_a * n_a * si_a, axis=0, keepdims=True)
        dattn = rs_a * (dn_a - n_a * head_mean_a(dn_a * n_a))
        perm_out(dattn, do1_o, do4_o, do16_o, BF16)
        stats = lse_c + _mm_exact_r(dattn * attn, sel_r[...])
        perm_out(stats, st1_o, st4_o, st16_o, F32)
        dhg_o[...] = (dm_h * (n_h * hnw_v) * (sg_h * (1.0 + hgv * (1.0 - sg_h)))).astype(BF16)
        dn_h = dm_h * hnw_v * si_h
        small_o[1:2, AW:] += jnp.sum(dm_h * n_h * si_h, axis=0, keepdims=True)
        drec_o[...] = (rs_h * (dn_h - n_h * head_mean_h(dn_h * n_h))).astype(BF16)

        @pl.when(pl.program_id(0) == T // TT - 1)
        def _():
            x, y, c = lax.axis_index("x"), lax.axis_index("y"), lax.axis_index("c")
            cps = [pltpu.make_async_remote_copy(
                src_ref=gwout_o.at[pl.ds(pl.multiple_of(j * 256 + (1 - c) * 128, 128), 128), :], dst_ref=rbuf.at[j],
                send_sem=send_sems.at[j], recv_sem=recv_sems.at[j], device_id=(x, y, 1 - c), device_id_type=MESH)
                for j in range(4)]
            for cp in cps:
                cp.start()
            for j, cp in enumerate(cps):
                cp.wait_recv()
                red = gwout_o[pl.ds(pl.multiple_of(j * 256 + c * 128, 128), 128), :] + rbuf[j]
                rout_o[j * 128:(j + 1) * 128, :] = red
                routb_o[j * 128:(j + 1) * 128, :] = red.astype(BF16)
            for cp in cps:
                cp.wait_send()

    tok = lambda w: pl.BlockSpec((TT, w), lambda i: (i, 0))
    d4 = pl.BlockSpec((4, TT // 4, AW), lambda i: (0, i, 0))
    d16 = pl.BlockSpec((16, TT // 16, AW), lambda i: (0, i, 0))
    const = lambda shape: pl.BlockSpec(shape, lambda i: (0,) * len(shape))
    sd = lambda shape, dt: jax.ShapeDtypeStruct(shape, dt)
    c4 = pl.BlockSpec((4, TT // 4, 128), lambda i: (0, i, 0))
    c16 = pl.BlockSpec((16, TT // 16, 128), lambda i: (0, i, 0))
    p3 = lambda w, dt: [sd((T, w), dt), sd((4, T // 4, w), dt), sd((16, T // 16, w), dt)]
    return pl.pallas_call(
        body, name="fwd_out", grid=(T // TT,),
        in_specs=[tok(AW), d4, d16, tok(128), c4, c16, tok(AW), tok(AW), tok(AW), tok(D), tok(D),
                  const((1, AW)), const((1, HW)), const((1, D)), const((D, D)), const((256, 256)),
                  const((128, AW)), const((AW, 128))],
        out_specs=[tok(D)] + [tok(AW), d4, d16] + [tok(128), c4, c16] + [tok(AW)] * 3
        + [const((512, D)), const((512, D)), const((8, D))],
        out_shape=[sd((T, D), F32)] + p3(AW, BF16) + p3(128, F32)
        + [sd((T, AW), BF16), sd((T, AW), BF16), sd((T, AW), BF16), sd((512, D), F32), sd((512, D), BF16),
           sd((8, D), F32)],
        scratch_shapes=[pltpu.VMEM((4, TT, 128), F32)] * 3 + [pltpu.VMEM((D, D), F32),
                        pltpu.VMEM((4, 128, D), F32), pltpu.SemaphoreType.DMA((4,)), pltpu.SemaphoreType.DMA((4,))],
        compiler_params=_cp(("arbitrary",)),
    )(o1, o4, o16, l1, l4, l16, rec, ag, hg, x, tgt, anw, hnw, fnw, wout_full, gmat, emat, selmat)


def _dproj_build(dq, dk, dv, dag, pos):
    TT = 512

    def body(dq1, dq4, dq16, dk1, dk4, dk16, dv1, dv4, dv16, dag_r, pos_r, dproj_o, scr_a, scr_b, scr_c):
        def unperm_sum(r1, r4, r16):
            u4, u16 = _unperm_load(r4, r16, scr_a, scr_b, scr_c)
            return r1[...] + u4 + u16

        cosf, s1, s2 = _rope_tables(pos_r[...])
        dproj_o[:, 0:512] = _rope_bwd(unperm_sum(dq1, dq4, dq16), cosf, s1, s2).astype(BF16)
        dproj_o[:, 512:1024] = _rope_bwd(unperm_sum(dk1, dk4, dk16), cosf, s1, s2).astype(BF16)
        dproj_o[:, 1024:1536] = unperm_sum(dv1, dv4, dv16).astype(BF16)
        dproj_o[:, 1536:2048] = dag_r[...]

    tok = lambda w: pl.BlockSpec((TT, w), lambda i: (i, 0))
    d4 = pl.BlockSpec((4, TT // 4, AW), lambda i: (0, i, 0))
    d16 = pl.BlockSpec((16, TT // 16, AW), lambda i: (0, i, 0))
    return pl.pallas_call(
        body, name="dproj_build", grid=(T // TT,),
        in_specs=[tok(AW), d4, d16] * 3 + [tok(AW), tok(1)],
        out_specs=tok(NCOL // 2),
        out_shape=jax.ShapeDtypeStruct((T, NCOL // 2), BF16),
        scratch_shapes=[pltpu.VMEM((4, TT, 128), F32)] * 3,
        compiler_params=_cp(("parallel",)),
    )(*dq, *dk, *dv, dag, pos)


def _bwd_x(dproj_a, dproj_h, x, dx2, mixw, w_full, rin, rinb, small4, small6, pout_own, pout_rem):
    TT = 256
    NT = T // TT

    def body(dpa_r, dph_r, x_r, dx2_r, mw_r, w_r, rin_r, rinb_r, s4_r, s6_r, poo_r, por_r,
             gx_o, pin_o, pinr_o, sall_o, fin_o, fout_o, sbuf, v_own, v_rem, vo_own, vo_rem, sin, sout, got_in,
             got_out, send_sems, recv_sems, loc_sems, share_send, share_recv, fin_sems):
        i = pl.program_id(0)
        loc, rem = _chip_copies(_w_in_piece, rin_r, rinb_r, pin_o, pinr_o, send_sems, recv_sems, loc_sems.at[0])

        @pl.when(i == 0)
        def _():
            sbuf[...] = jnp.zeros_like(sbuf)
            for cp in loc + rem:
                cp.start()

        dhn = _mm_nt(dpa_r[...], w_r[:, 0:NCOL // 2]) + _mm_nt(dph_r[...], w_r[:, NCOL // 2:NCOL])
        xv = x_r[...]
        r = lax.rsqrt(jnp.mean(xv * xv, axis=-1, keepdims=True) + EPS)
        dxw = dhn * mw_r[...]
        gx_o[...] = dx2_r[...] + r * dxw - xv * ((r * r * r) * jnp.mean(dxw * xv, axis=-1, keepdims=True))
        sbuf[16:17, :] += jnp.sum(dhn * (xv * r), axis=0, keepdims=True)

        @pl.when(i == NT - 1)
        def _():
            sbuf[0:8, :] = s4_r[...]
            sbuf[8:16, :] = s6_r[...]
            sloc, srem = _small_copies(sbuf, sall_o, send_sems, recv_sems, loc_sems.at[1])
            for cp in sloc + srem:
                cp.start()
            for cp in rem + srem:
                cp.wait_recv()
            for cp in rem + srem:
                cp.wait_send()
            for cp in loc + sloc:
                cp.wait()
            mx, my, c = lax.axis_index("x"), lax.axis_index("y"), lax.axis_index("c")
            loads = [pltpu.make_async_copy(pin_o, v_own, fin_sems.at[0]),
                     pltpu.make_async_copy(pinr_o, v_rem, fin_sems.at[1]),
                     pltpu.make_async_copy(poo_r, vo_own, fin_sems.at[2]),
                     pltpu.make_async_copy(por_r, vo_rem, fin_sems.at[3])]
            for cp in loads:
                cp.start()
            for cp in loads:
                cp.wait()
            sout[...] = ((vo_own[...] + vo_rem[0].astype(F32)) + vo_rem[1].astype(F32)) + vo_rem[2].astype(F32)
            sin[...] = ((v_own[...] + v_rem[0].astype(F32)) + v_rem[1].astype(F32)) + v_rem[2].astype(F32)
            swap = [pltpu.make_async_remote_copy(src_ref=sin, dst_ref=got_in, send_sem=share_send.at[0],
                                                 recv_sem=share_recv.at[0], device_id=(mx, my, 1 - c),
                                                 device_id_type=MESH),
                    pltpu.make_async_remote_copy(src_ref=sout, dst_ref=got_out, send_sem=share_send.at[1],
                                                 recv_sem=share_recv.at[1], device_id=(mx, my, 1 - c),
                                                 device_id_type=MESH)]
            for cp in swap:
                cp.start()
            mine = [pltpu.make_async_copy(sin, fin_o.at[c], fin_sems.at[0]),
                    pltpu.make_async_copy(sout, fout_o.at[c], fin_sems.at[1])]
            for cp in mine:
                cp.start()
            for cp in swap:
                cp.wait_recv()
            theirs = [pltpu.make_async_copy(got_in, fin_o.at[1 - c], fin_sems.at[2]),
                      pltpu.make_async_copy(got_out, fout_o.at[1 - c], fin_sems.at[3])]
            for cp in theirs:
                cp.start()
            for cp in swap:
                cp.wait_send()
            for cp in mine + theirs:
                cp.wait()

    tok = lambda w: pl.BlockSpec((TT, w), lambda i: (i, 0))
    const = lambda shape: pl.BlockSpec(shape, lambda i: (0,) * len(shape))
    hbm = pl.BlockSpec(memory_space=pltpu.HBM)
    return pl.pallas_call(
        body, name="bwd_x", grid=(NT,),
        in_specs=[tok(NCOL // 2), tok(NCOL // 2), tok(D), tok(D), const((1, D)), const((D, NCOL)), hbm, hbm,
                  const((8, D)), const((8, D)), hbm, hbm],
        out_specs=[tok(D), hbm, hbm, hbm, hbm, hbm],
        out_shape=[jax.ShapeDtypeStruct((T, D), F32),
                   jax.ShapeDtypeStruct((512, 1024), F32), jax.ShapeDtypeStruct((3, 512, 1024), BF16),
                   jax.ShapeDtypeStruct((8, 24, D), F32),
                   jax.ShapeDtypeStruct((2, 512, 1024), F32), jax.ShapeDtypeStruct((2, 128, D), F32)],
        scratch_shapes=[pltpu.VMEM((24, D), F32),
                        pltpu.VMEM((512, 1024), F32), pltpu.VMEM((3, 512, 1024), BF16),
                        pltpu.VMEM((128, D), F32), pltpu.VMEM((3, 128, D), BF16),
                        pltpu.VMEM((512, 1024), F32), pltpu.VMEM((128, D), F32),
                        pltpu.VMEM((512, 1024), F32), pltpu.VMEM((128, D), F32),
                        pltpu.SemaphoreType.DMA((10,)), pltpu.SemaphoreType.DMA((10,)), pltpu.SemaphoreType.DMA((2,)),
                        pltpu.SemaphoreType.DMA((2,)), pltpu.SemaphoreType.DMA((2,)), pltpu.SemaphoreType.DMA((4,))],
        compiler_params=_cp(("arbitrary",)),
    )(dproj_a, dproj_h, x, dx2, mixw, w_full, rin, rinb, small4, small6, pout_own, pout_rem)


def _grad_w_in(hn, dproj_a, dproj_h):
    TK = 2048
    NK = T // TK

    def body(hnt_r, dpa_r, dph_r, rin_o, rinb_o, acc, rbuf, obuf, obufb, send_sems, recv_sems, wb_sems):
        j = pl.program_id(0)
        kk = pl.program_id(1)
        x, y, c = lax.axis_index("x"), lax.axis_index("y"), lax.axis_index("c")
        mine = pl.ds(pl.multiple_of(c * 512, 512), 512)
        theirs = pl.ds(pl.multiple_of((1 - c) * 512, 512), 512)

        def send(jj):
            return pltpu.make_async_remote_copy(
                src_ref=acc.at[jj % 2, theirs, :], dst_ref=rbuf.at[jj], send_sem=send_sems.at[jj],
                recv_sem=recv_sems.at[jj], device_id=(x, y, 1 - c), device_id_type=MESH)

        def writeback(jj):
            cols = pl.ds(jj * 1024, 1024)
            return [pltpu.make_async_copy(obuf.at[jj % 2], rin_o.at[:, cols], wb_sems.at[jj % 2]),
                    pltpu.make_async_copy(obufb.at[jj % 2], rinb_o.at[:, cols], wb_sems.at[2 + jj % 2])]

        def wait_writeback(jj):
            for cp in writeback(jj):
                cp.wait()

        def finalize(jj):
            send(jj).wait_recv()
            red = acc[jj % 2, mine, :] + rbuf[jj]
            obuf[jj % 2] = red
            obufb[jj % 2] = red.astype(BF16)
            for cp in writeback(jj):
                cp.start()

        prod = _mm(hnt_r[...], jnp.where(j < 2, dpa_r[...], dph_r[...]))

        @pl.when(kk == 0)
        def _():
            for jj in (2, 3):
                @pl.when(j == jj)
                def _():
                    send(jj - 2).wait_send()
            acc[j % 2] = prod

        @pl.when(kk > 0)
        def _():
            acc[j % 2] += prod

        @pl.when(kk == NK - 1)
        def _():
            for jj in range(4):
                @pl.when(j == jj)
                def _():
                    send(jj).start()
                    if jj in (1, 2):
                        finalize(jj - 1)
                    if jj == 3:
                        wait_writeback(0)
                        finalize(2)
                        wait_writeback(1)
                        finalize(3)
                        wait_writeback(2)
                        wait_writeback(3)
                        send(2).wait_send()
                        send(3).wait_send()

    hbm = pl.BlockSpec(memory_space=pltpu.HBM)
    return pl.pallas_call(
        body, name="grad_w_in", grid=(4, NK),
        in_specs=[pl.BlockSpec((D, TK), lambda j, kk: (0, kk)),
                  pl.BlockSpec((TK, 1024), lambda j, kk: (jnp.where(j < 2, kk, NK - 1), jnp.minimum(j, 1))),
                  pl.BlockSpec((TK, 1024), lambda j, kk: (jnp.where(j < 2, 0, kk), jnp.maximum(j - 2, 0)))],
        out_specs=[hbm, hbm],
        out_shape=[jax.ShapeDtypeStruct((512, NCOL), F32), jax.ShapeDtypeStruct((512, NCOL), BF16)],
        scratch_shapes=[pltpu.VMEM((2, D, 1024), F32), pltpu.VMEM((4, 512, 1024), F32), pltpu.VMEM((2, 512, 1024), F32),
                        pltpu.VMEM((2, 512, 1024), BF16),
                        pltpu.SemaphoreType.DMA((4,)), pltpu.SemaphoreType.DMA((4,)), pltpu.SemaphoreType.DMA((4,))],
        compiler_params=_cp(("arbitrary", "arbitrary")),
    )(hn, dproj_a, dproj_h)


def _w_in_piece(ref, j):
    return ref.at[:, pl.ds(j * 1024, 1024)]


def _w_out_piece(ref, j):
    return ref.at[pl.ds(j * 128, 128), :]


def _chip_copies(piece, src_r, srcb_r, own_o, rem_o, send_sems, recv_sems, loc_sem):
    x, y, c = lax.axis_index("x"), lax.axis_index("y"), lax.axis_index("c")
    chips = [(1 - x, y), (x, 1 - y), (1 - x, 1 - y)]
    loc = [pltpu.make_async_copy(piece(src_r, 2 * x + y), own_o, loc_sem)]
    rem = [pltpu.make_async_remote_copy(
        src_ref=piece(srcb_r, 2 * px + py), dst_ref=rem_o.at[k], send_sem=send_sems.at[k],
        recv_sem=recv_sems.at[k], device_id=(px, py, c), device_id_type=MESH) for k, (px, py) in enumerate(chips)]
    return loc, rem


def _small_copies(small_r, sall_o, send_sems, recv_sems, loc_sem):
    x, y, c = lax.axis_index("x"), lax.axis_index("y"), lax.axis_index("c")
    me = 4 * x + 2 * y + c
    loc = [pltpu.make_async_copy(small_r, sall_o.at[me], loc_sem)]
    rem = []
    k = 3
    for fx in range(2):
        for fy in range(2):
            for fc in range(2):
                if fx or fy or fc:
                    peer = (1 - x if fx else x, 1 - y if fy else y, 1 - c if fc else c)
                    rem.append(pltpu.make_async_remote_copy(
                        src_ref=small_r, dst_ref=sall_o.at[me], send_sem=send_sems.at[k],
                        recv_sem=recv_sems.at[k], device_id=peer, device_id_type=MESH))
                    k += 1
    return loc, rem


def _adamw_math(w, g, m, v):
    m = B1 * m + (1.0 - B1) * g
    v = B2 * v + (1.0 - B2) * (g * g)
    m_hat = m / (1.0 - B1 ** STEP)
    v_hat = v / (1.0 - B2 ** STEP)
    delta = -LR * (m_hat / (jnp.sqrt(v_hat) + AEPS) + WD * w)
    return delta, m, v


def _adamw(big_in, big_out, sall, params):
    def body(*refs):
        wi, gi, mi, vi, wo, go, mo, vo, sall_r = refs[:9]
        ins = refs[9:24]
        di_o, mi_o, vi_o, do_o, mo_o, vo_o = refs[24:30]
        outs = refs[30:]
        d, mm, vv = _adamw_math(wi[...], gi[...], mi[...], vi[...])
        di_o[...] = d
        mi_o[...] = mm
        vi_o[...] = vv

        @pl.when(pl.program_id(0) == 0)
        def _():
            d, mm, vv = _adamw_math(wo[...], go[...], mo[...], vo[...])
            do_o[...] = d
            mo_o[...] = mm
            vo_o[...] = vv
            tot = sall_r[0]
            for dv in range(1, 8):
                tot = tot + sall_r[dv]
            grads = [tot[16:17, :], tot[1:2, 0:AW], tot[1:2, AW:], tot[8:10, 0:HW], tot[0:1, :]]
            outs[0][...] = tot[2:3, 0:1]
            for p in range(5):
                w_r, m_r, v_r = ins[3 * p:3 * p + 3]
                g = grads[p]
                d, mm, vv = _adamw_math(w_r[...], g, m_r[...], v_r[...])
                outs[1 + 4 * p][...] = g
                outs[2 + 4 * p][...] = d
                outs[3 + 4 * p][...] = mm
                outs[4 + 4 * p][...] = vv

    flat = [a for p in params for a in p]
    shapes = [jax.ShapeDtypeStruct((D, 1024), F32)] * 3 + [jax.ShapeDtypeStruct((256, D), F32)] * 3
    shapes += [jax.ShapeDtypeStruct((1, 1), F32)]
    for p in params:
        shapes += [jax.ShapeDtypeStruct(p[0].shape, F32)] * 4
    vm = pl.BlockSpec(memory_space=pltpu.VMEM)
    rows = pl.BlockSpec((256, 1024), lambda i: (i, 0))
    whole = pl.BlockSpec((256, D), lambda i: (0, 0))
    return pl.pallas_call(
        body, name="adamw", grid=(4,),
        in_specs=[rows] * 4 + [whole] * 4 + [vm] * 16, out_specs=[rows] * 3 + [whole] * 3 + [vm] * 21,
        out_shape=shapes,
        compiler_params=_cp(("arbitrary",)),
    )(*big_in, *big_out, sall, *flat)


def kernel(x, positions, w_in, w_out, mix_norm_w, attn_out_norm_w, hgrn_out_norm_w, hgrn_lb_raw, final_norm_w, loss_target, m_w_in, m_w_out, m_mix_norm_w, m_attn_out_norm_w, m_hgrn_out_norm_w, m_hgrn_lb_raw, m_final_norm_w, v_w_in, v_w_out, v_mix_norm_w, v_attn_out_norm_w, v_hgrn_out_norm_w, v_hgrn_lb_raw, v_final_norm_w):
    xs = x.reshape(T, D)
    tgt = loss_target.reshape(T, D)
    pos = positions.reshape(T, 1)
    fnw = final_norm_w.reshape(1, D)

    ti = np.arange(TH)
    tri_np = ((ti[:, None] // CHUNK == ti[None, :] // CHUNK) & (ti[None, :] <= ti[:, None])).astype(np.float32)
    tri = jnp.asarray(tri_np, BF16)
    trit = jnp.asarray(tri_np.T, BF16)
    hi_ = np.arange(AW) // HEAD
    gmat = jnp.asarray((hi_[:256, None] == hi_[None, :256]).astype(np.float32) / HEAD, BF16)
    emat_np = (np.arange(128)[:, None] == hi_[None, :]).astype(np.float32)
    sel_np = (8 + hi_[:, None] == np.arange(128)[None, :]).astype(np.float32)
    emat = jnp.asarray(emat_np, BF16)
    selmat = jnp.asarray(sel_np, BF16)

    jm_arr = (2 * lax.axis_index("x") + lax.axis_index("y")).astype(jnp.int32).reshape(1)
    (hn, q1, k1, v1, q4, k4, v4, q16, k16, v16, ag, hq, hf, hi, hg, w_full, wout4) = _fwd_in(
        xs, pos, mix_norm_w, w_in.reshape(D, 1024), w_out.reshape(256, D), jm_arr)
    wout_full = wout4.reshape(D, D)
    flat = lambda a: a.reshape(T, AW)
    o1, l1 = _attn_fwd(q1, k1, v1, T // BLK, "attn_fwd_d1")
    o4, l4 = _attn_fwd(flat(q4), flat(k4), flat(v4), T // 4 // BLK, "attn_fwd_d4")
    o16, l16 = _attn_fwd(flat(q16), flat(k16), flat(v16), T // 16 // BLK, "attn_fwd_d16")
    rec, sall = _hgrn_fwd(hq, hf, hi, hgrn_lb_raw, tri)

    (dx2, do1, do4, do16, st1, st4, st16, drec, dag, dhg, rout, routb, small4) = _fwd_out(
        o1, o4.reshape(4, T // 4, AW), o16.reshape(16, T // 16, AW),
        l1, l4.reshape(4, T // 4, 128), l16.reshape(16, T // 16, 128),
        rec, ag, hg, xs, tgt, attn_out_norm_w, hgrn_out_norm_w, fnw, wout_full, gmat, emat, selmat)

    fst = lambda a: a.reshape(T, 128)
    dq1, dk1, dv1 = _attn_bwd(q1, k1, v1, do1, st1, T // BLK, "attn_bwd_d1")
    dq4, dk4, dv4 = _attn_bwd(flat(q4), flat(k4), flat(v4), flat(do4), fst(st4), T // 4 // BLK, "attn_bwd_d4")
    dq16, dk16, dv16 = _attn_bwd(flat(q16), flat(k16), flat(v16), flat(do16), fst(st16), T // 16 // BLK,
                                 "attn_bwd_d16")
    dproj_h, small6, pout_own, pout_rem = _hgrn_bwd(hq, hf, hi, hgrn_lb_raw, tri, trit, drec, sall, dhg,
                                                    rout, routb)

    r4 = lambda a: a.reshape(4, T // 4, AW)
    r16 = lambda a: a.reshape(16, T // 16, AW)
    dproj_a = _dproj_build((dq1, r4(dq4), r16(dq16)), (dk1, r4(dk4), r16(dk16)), (dv1, r4(dv4), r16(dv16)),
                           dag, pos)
    rin, rinb = _grad_w_in(hn, dproj_a, dproj_h)
    gx, _, _, small_all, fin, fout = _bwd_x(dproj_a, dproj_h, xs, dx2, mix_norm_w, w_full, rin, rinb,
                                            small4, small6, pout_own, pout_rem)
    g_w_in = fin.reshape(D, 1024)
    g_w_out = fout.reshape(256, D)

    params = [(mix_norm_w, m_mix_norm_w, v_mix_norm_w),
              (attn_out_norm_w, m_attn_out_norm_w, v_attn_out_norm_w),
              (hgrn_out_norm_w, m_hgrn_out_norm_w, v_hgrn_out_norm_w),
              (hgrn_lb_raw, m_hgrn_lb_raw, v_hgrn_lb_raw),
              (fnw, m_final_norm_w.reshape(1, D), v_final_norm_w.reshape(1, D))]
    d_in, nm_in, nv_in, d_out, nm_out, nv_out, *so = _adamw(
        (w_in.reshape(D, 1024), g_w_in, m_w_in.reshape(D, 1024), v_w_in.reshape(D, 1024)),
        (w_out.reshape(256, D), g_w_out, m_w_out.reshape(256, D), v_w_out.reshape(256, D)), small_all, params)
    loss = so[0].reshape(())
    g_s = [so[1 + 4 * p] for p in range(5)]
    d_s = [so[2 + 4 * p] for p in range(5)]
    m_s = [so[3 + 4 * p] for p in range(5)]
    v_s = [so[4 + 4 * p] for p in range(5)]
    for lst in (g_s, d_s, m_s, v_s):
        lst[4] = lst[4].reshape(D)

    return (loss, gx.reshape(1, T, D),
            g_w_in.reshape(1, D, 1024), g_w_out.reshape(1, 256, D), *g_s,
            d_in.reshape(1, D, 1024), d_out.reshape(1, 256, D), *d_s,
            nm_in.reshape(1, D, 1024), nm_out.reshape(1, 256, D), *m_s,
            nv_in.reshape(1, D, 1024), nv_out.reshape(1, 256, D), *v_s)
```

```python
import functools

import numpy as np
import jax
import jax.numpy as jnp
from jax import lax
from jax.experimental import pallas as pl
from jax.experimental.pallas import tpu as pltpu

F32 = jnp.float32
BF16 = jnp.bfloat16

T = 4096
D = 1024
AW = 512
HW = 512
NCOL = 4096
HEAD = 64
BLK = 128
CHUNK = 64
EPS = 1e-6
SCALE = HEAD ** -0.5
NEG = -1e30
ROPE_THETA = 500000.0
INV_FREQ = [float(v) for v in
            (np.float32(ROPE_THETA) ** (-(np.arange(8, dtype=np.float32)) * np.float32(0.125)))]
LR, B1, B2, AEPS, WD, STEP = 0.001, 0.9, 0.999, 1e-08, 0.01, 10
VMEM_LIMIT = 56 * 1024 * 1024
MESH = pl.DeviceIdType.MESH


def _cp(sem=None, **kw):
    return pltpu.CompilerParams(dimension_semantics=sem, vmem_limit_bytes=VMEM_LIMIT, **kw)


def _mm(a, b):
    return jnp.dot(a, b, preferred_element_type=F32)


def _mm_nt(a, b):
    return lax.dot_general(a, b, (((1,), (1,)), ((), ())), preferred_element_type=F32)


def _mm_tn(a, b):
    return lax.dot_general(a, b, (((0,), (0,)), ((), ())), preferred_element_type=F32)


def _mm_exact_l(mat_bf, x):
    h = x.astype(BF16)
    l = (x - h.astype(F32)).astype(BF16)
    return _mm(mat_bf, h) + _mm(mat_bf, l)


def _mm_exact_r(x, mat_bf):
    h = x.astype(BF16)
    l = (x - h.astype(F32)).astype(BF16)
    return _mm(h, mat_bf) + _mm(l, mat_bf)


def _sigmoid(x):
    return 0.5 * jnp.tanh(0.5 * x) + 0.5


def _rope_tables(pos):
    lane = lax.broadcasted_iota(jnp.int32, (1, 128), 1)
    jl = lane & 63
    fi = jl & 7
    inv = jnp.zeros((1, 128), F32)
    for kk in range(8):
        inv = jnp.where(fi == kk, INV_FREQ[kk], inv)
    ang = pos.astype(F32) * inv
    c = jnp.cos(ang)
    s = jnp.sin(ang)
    cosf = jnp.where(jl < 16, c, 1.0)
    s1 = jnp.where(jl < 8, -s, 0.0)
    s2 = jnp.where((jl >= 8) & (jl < 16), s, 0.0)
    return cosf, s1, s2


def _rope(t, cosf, s1, s2):
    parts = []
    for ci in range(t.shape[1] // 128):
        tc = t[:, ci * 128:(ci + 1) * 128]
        parts.append(tc * cosf + pltpu.roll(tc, 120, 1) * s1 + pltpu.roll(tc, 8, 1) * s2)
    return jnp.concatenate(parts, axis=1)


def _rope_bwd(g, cosf, s1, s2):
    parts = []
    for ci in range(g.shape[1] // 128):
        gc = g[:, ci * 128:(ci + 1) * 128]
        parts.append(gc * cosf + pltpu.roll(gc * s1, 8, 1) + pltpu.roll(gc * s2, 120, 1))
    return jnp.concatenate(parts, axis=1)


def _perm_store(val, scr, scr2, o1, o4, o16, dt):
    n = val.shape[0]
    q = n // 4
    o1[...] = val.astype(dt)
    for ci in range(val.shape[1] // 128):
        cs = slice(ci * 128, (ci + 1) * 128)
        scr[ci] = val[:, cs]
        for r4 in range(4):
            part = scr[ci, pl.ds(r4, q, stride=4), :]
            o4[r4, :, cs] = part.astype(dt)
            scr2[ci, r4 * q:(r4 + 1) * q, :] = part
        for r4 in range(4):
            for b in range(4):
                o16[r4 + 4 * b, :, cs] = scr2[ci, pl.ds(r4 * q + b, q // 4, stride=4), :].astype(dt)


def _unperm_load(r4, r16, scr_a, scr_b, scr_c):
    n = scr_a.shape[1]
    q = n // 4
    nc = r4.shape[-1] // 128
    for ci in range(nc):
        cs = slice(ci * 128, (ci + 1) * 128)
        for rr in range(4):
            scr_a[ci, pl.ds(rr, q, stride=4), :] = r4[rr, :, cs].astype(F32)
        for rr in range(4):
            for b in range(4):
                scr_c[ci, pl.ds(rr * q + b, q // 4, stride=4), :] = r16[rr + 4 * b, :, cs].astype(F32)
        for rr in range(4):
            scr_b[ci, pl.ds(rr, q, stride=4), :] = scr_c[ci, rr * q:(rr + 1) * q, :]
    return (jnp.concatenate([scr_a[ci] for ci in range(nc)], axis=1),
            jnp.concatenate([scr_b[ci] for ci in range(nc)], axis=1))


def _fwd_in(x, pos, mixw, w_in, w_out, jm_arr):
    TT = 512
    NT = T // TT

    def body(jm_ref, x_ref, pos_ref, mw_ref, win_ref, wout_ref,
             hnt_ref, q1, k1, v1, q4, k4, v4, q16, k16, v16, ag, hq, hf, hi, hg, wfull_o, woutfull_o,
             wbuf, wobuf, hn_all, scr, scr2, stage, send_sems, recv_sems, loc_sems):
        s = pl.program_id(0)
        i = pl.program_id(1)
        mx, my, c = lax.axis_index("x"), lax.axis_index("y"), lax.axis_index("c")
        me, sibling = (mx, my, c), (mx, my, 1 - c)
        chips = [(mx, 1 - my), (1 - mx, my), (1 - mx, 1 - my)]
        jm = 2 * mx + my
        rows_in = [pl.ds(pl.multiple_of(h * 512, 512), 512) for h in (c, 1 - c)]
        rows_out = [pl.ds(pl.multiple_of(h * 128, 128), 128) for h in (c, 1 - c)]

        def blk(k):
            return lax.bitwise_xor(jm, k + 1)

        def rc(n, ref, to):
            return pltpu.make_async_remote_copy(src_ref=ref, dst_ref=ref, send_sem=send_sems.at[n],
                                                recv_sem=recv_sems.at[n], device_id=to, device_id_type=MESH)

        halves = [pl.ds(0, 512), pl.ds(512, 512)]
        send_in = lambda k, h: rc(12 + 2 * k + h, wbuf.at[jm, rows_in[0], halves[h]], (*chips[k], c))
        got_in = lambda k, h: rc(12 + 2 * k + h, wbuf.at[blk(k), rows_in[0], halves[h]], me)
        relay = lambda h: rc(16 + h, wbuf.at[blk(h), rows_in[0], halves[h]], (*chips[1 - h], c))
        got_relay = lambda h: rc(16 + h, wbuf.at[blk(2), rows_in[0], halves[h]], me)
        send_out = lambda k: rc(3 + k, wobuf.at[jm, rows_out[0], :], (*chips[k], c))
        got_out = lambda k: rc(3 + k, wobuf.at[blk(k), rows_out[0], :], me)
        pass_in = lambda k: rc(6 + k, wbuf.at[blk(k), rows_in[0], :], sibling)
        pass_out = lambda k: rc(9 + k, wobuf.at[blk(k), rows_out[0], :], sibling)
        passed_in = lambda k: rc(6 + k, wbuf.at[blk(k), rows_in[1], :], me)
        passed_out = lambda k: rc(9 + k, wobuf.at[blk(k), rows_out[1], :], me)

        def keep(j, n):
            return pltpu.make_async_copy(wbuf.at[j], wfull_o.at[:, pl.ds(j * 1024, 1024)], loc_sems.at[n])

        @pl.when((s == 0) & (i == 0))
        def _():
            for p in range(5):
                src = win_ref.at[pl.ds(p * 256, 256), :] if p < 4 else wout_ref
                load = pltpu.make_async_copy(src, stage, loc_sems.at[4])
                load.start()
                load.wait()
                if p < 4:
                    wbuf[jm, p * 256:(p + 1) * 256, :] = stage[...].astype(BF16)
                else:
                    wobuf[jm] = stage[...].astype(BF16)
            for k in range(2):
                for h in range(2):
                    send_in(k, h).start()
            keep(jm, 0).start()

        def arrive(k):
            if k == 0:
                for kk in range(2):
                    for h in range(2):
                        got_in(kk, h).wait_recv()
                relay(0).start()
                relay(1).start()
            if k == 2:
                got_relay(0).wait_recv()
                got_relay(1).wait_recv()
            pass_in(k).start()
            passed_in(k).wait_recv()
            keep(blk(k), k + 1).start()
            if k == 2:
                for kk in range(3):
                    send_out(kk).start()

        pl.when((s == 1) & (i == 0))(functools.partial(arrive, 0))

        @pl.when((s == 2) & (i == 0))
        def _():
            arrive(1)
            arrive(2)

        tile = pl.ds(pl.multiple_of(i * TT, TT), TT)

        @pl.when(s == 0)
        def _():
            xv = x_ref[...]
            r = lax.rsqrt(jnp.mean(xv * xv, axis=-1, keepdims=True) + EPS)
            hnf = (xv * r) * mw_ref[...]
            hn_all[tile, :] = hnf.astype(BF16)
            hnt_ref[...] = hnf.T.astype(BF16)

        def project(jj):
            hn = hn_all[tile, :]
            lo = _mm(hn, wbuf[jj, :, 0:512])
            hi_cols = _mm(hn, wbuf[jj, :, 512:1024])
            if jj == 0:
                cosf, s1, s2 = _rope_tables(pos_ref[...])
                _perm_store(_rope(lo, cosf, s1, s2), scr, scr2, q1, q4, q16, BF16)
                _perm_store(_rope(hi_cols, cosf, s1, s2), scr, scr2, k1, k4, k16, BF16)
            elif jj == 1:
                _perm_store(lo, scr, scr2, v1, v4, v16, BF16)
                ag[...] = hi_cols.astype(BF16)
            elif jj == 2:
                hq[...] = lo.astype(BF16)
                hf[...] = hi_cols.astype(BF16)
            else:
                hi[...] = lo.astype(BF16)
                hg[...] = hi_cols.astype(BF16)

        def project_block(j):
            for jj in range(4):
                pl.when(j == jj)(functools.partial(project, jj))

        @pl.when(s < 2)
        def _():
            project_block(lax.bitwise_xor(jm, s))

        @pl.when(s == 2)
        def _():
            project_block(lax.bitwise_xor(jm, 2))
            project_block(lax.bitwise_xor(jm, 3))

        @pl.when((s == 2) & (i == NT - 1))
        def _():
            for k in range(3):
                got_out(k).wait_recv()
                pass_out(k).start()
            for k in range(3):
                passed_out(k).wait_recv()
            out = pltpu.make_async_copy(wobuf, woutfull_o, loc_sems.at[4])
            out.start()
            for h in range(2):
                relay(h).wait_send()
                for k in range(2):
                    send_in(k, h).wait_send()
            for k in range(3):
                send_out(k).wait_send()
                pass_in(k).wait_send()
                pass_out(k).wait_send()
            keep(jm, 0).wait()
            for k in range(3):
                keep(blk(k), k + 1).wait()
            out.wait()

    def at_stage_of(jb):
        def index(s, i, jm_ref):
            sa = jnp.minimum(lax.bitwise_xor(jm_ref[0], jb), 2)
            return jnp.where(s < sa, 0, jnp.where(s == sa, i, NT - 1))
        return index

    tok = lambda w, jb: pl.BlockSpec((TT, w), lambda s, i, jm_ref: (at_stage_of(jb)(s, i, jm_ref), 0))
    d4 = lambda jb: pl.BlockSpec((4, TT // 4, AW), lambda s, i, jm_ref: (0, at_stage_of(jb)(s, i, jm_ref), 0))
    d16 = lambda jb: pl.BlockSpec((16, TT // 16, AW), lambda s, i, jm_ref: (0, at_stage_of(jb)(s, i, jm_ref), 0))
    hbm = pl.BlockSpec(memory_space=pltpu.HBM)
    sd = lambda shape, dt: jax.ShapeDtypeStruct(shape, dt)
    in_own_stage = lambda s, i: jnp.where(s == 0, i, NT - 1)
    grid_spec = pltpu.PrefetchScalarGridSpec(
        num_scalar_prefetch=1, grid=(3, NT),
        in_specs=[pl.BlockSpec((TT, D), lambda s, i, jm_ref: (in_own_stage(s, i), 0)),
                  pl.BlockSpec((TT, 1), lambda s, i, jm_ref: (i, 0)),
                  pl.BlockSpec((1, D), lambda s, i, jm_ref: (0, 0)), hbm, hbm],
        out_specs=[pl.BlockSpec((D, TT), lambda s, i, jm_ref: (0, in_own_stage(s, i))),
                   tok(AW, 0), tok(AW, 0), tok(AW, 1), d4(0), d4(0), d4(1), d16(0), d16(0), d16(1),
                   tok(AW, 1), tok(AW, 2), tok(AW, 2), tok(AW, 3), tok(AW, 3), hbm, hbm],
        scratch_shapes=[pltpu.VMEM((4, D, 1024), BF16), pltpu.VMEM((4, 256, D), BF16), pltpu.VMEM((T, D), BF16),
                        pltpu.VMEM((4, TT, 128), F32), pltpu.VMEM((4, TT, 128), F32), pltpu.VMEM((256, 1024), F32),
                        pltpu.SemaphoreType.DMA((18,)),
                        pltpu.SemaphoreType.DMA((18,)), pltpu.SemaphoreType.DMA((6,))])
    return pl.pallas_call(
        body, name="fwd_in", grid_spec=grid_spec,
        out_shape=[sd((D, T), BF16)] + [sd((T, AW), BF16)] * 3 + [sd((4, T // 4, AW), BF16)] * 3
        + [sd((16, T // 16, AW), BF16)] * 3
        + [sd((T, AW), BF16)] * 5 + [sd((D, NCOL), BF16), sd((4, 256, D), BF16)],
        compiler_params=_cp(("arbitrary", "arbitrary")),
    )(jm_arr, x, pos, mixw, w_in, w_out)


def _band_mask(key_axis, nkeys=2 * BLK):
    shape = (nkeys, 2 * BLK) if key_axis == 0 else (2 * BLK, nkeys)
    kj = lax.broadcasted_iota(jnp.int32, shape, key_axis)
    qi = lax.broadcasted_iota(jnp.int32, shape, 1 - key_axis) & (BLK - 1)
    return (kj >= qi) & (kj <= qi + BLK), kj, qi


def _stack_heads(t2, in_a):
    z = jnp.zeros_like(t2)
    return jnp.concatenate([jnp.where(in_a[0], t2, z), jnp.where(in_a[1], t2, z)], axis=0)


def _attn_fwd(q, k, v, nb, name):
    n = 8
    CH = n * BLK
    halo = nb > n

    def body(*refs):
        if halo:
            q_ref, k_ref, v_ref, kp_ref, vp_ref, o_ref, lse_ref = refs
        else:
            q_ref, k_ref, v_ref, o_ref, lse_ref = refs
        lane = lax.broadcasted_iota(jnp.int32, (1, 128), 1)
        in_a = [lane < HEAD, lane >= HEAD]
        band, kj, _ = _band_mask(1)
        thr0 = jnp.where((n * pl.program_id(0)) % nb == 0, BLK, 0) if halo else BLK
        mask0 = band & (kj >= thr0)
        mask_first = band & (kj >= BLK)
        for b in range(n):
            rs = slice(b * BLK, (b + 1) * BLK)
            stat = jnp.zeros((BLK, 128), F32)
            for hp in range(4):
                cs = slice(hp * 128, (hp + 1) * 128)
                q2s = _stack_heads(q_ref[rs, cs], in_a)
                if b == 0:
                    kprev = kp_ref[:, cs] if halo else k_ref[rs, cs]
                    vprev = vp_ref[:, cs] if halo else v_ref[rs, cs]
                    kk = jnp.concatenate([kprev, k_ref[rs, cs]], axis=0)
                    vv = jnp.concatenate([vprev, v_ref[rs, cs]], axis=0)
                    mask = mask0
                else:
                    kk = k_ref[(b - 1) * BLK:(b + 1) * BLK, cs]
                    vv = v_ref[(b - 1) * BLK:(b + 1) * BLK, cs]
                    mask = mask_first if b % nb == 0 else band
                s = jnp.where(mask, _mm_nt(q2s, kk) * SCALE, NEG)
                m = jnp.max(s, axis=-1, keepdims=True)
                p = jnp.exp(s - m)
                l = jnp.sum(p, axis=-1, keepdims=True)
                o = _mm(p.astype(BF16), vv) / l
                lse = m + jnp.log(l)
                o_ref[rs, cs] = jnp.where(in_a[0], o[:BLK], o[BLK:]).astype(BF16)
                stat = jnp.where(lane == 2 * hp, lse[:BLK], stat)
                stat = jnp.where(lane == 2 * hp + 1, lse[BLK:], stat)
            lse_ref[rs, :] = stat

    cur = pl.BlockSpec((CH, AW), lambda i: (i, 0))
    prev = pl.BlockSpec((BLK, AW), lambda i: (jnp.maximum(n * i - 1, 0), 0))
    return pl.pallas_call(
        body, name=name, grid=(T // CH,),
        in_specs=[cur, cur, cur] + ([prev, prev] if halo else []),
        out_specs=[cur, pl.BlockSpec((CH, 128), lambda i: (i, 0))],
        out_shape=[jax.ShapeDtypeStruct((T, AW), BF16), jax.ShapeDtypeStruct((T, 128), F32)],
        compiler_params=_cp(("parallel",)),
    )(*((q, k, v) + ((k, v) if halo else ())))


def _attn_bwd(q, k, v, do, st, nb, name):
    n = 8
    CH = n * BLK
    NBLK = T // BLK
    halo = nb > n

    def body(*refs):
        if halo:
            (q_ref, k_ref, v_ref, do_ref, st_ref, kp_ref, vp_ref, qn_ref, don_ref, stn_ref,
             dq_ref, dk_ref, dv_ref) = refs
        else:
            q_ref, k_ref, v_ref, do_ref, st_ref, dq_ref, dk_ref, dv_ref = refs
        i = pl.program_id(0)
        lane = lax.broadcasted_iota(jnp.int32, (1, 128), 1)
        in_a = [lane < HEAD, lane >= HEAD]
        band, kj, _ = _band_mask(0)
        thr0 = jnp.where((n * i) % nb == 0, BLK, 0) if halo else BLK
        mask0 = band & (kj >= thr0)
        mask_first = band & (kj >= BLK)

        def stat_rows(st_t, hp):
            lse_r = jnp.concatenate([st_t[2 * hp:2 * hp + 1, :], st_t[2 * hp + 1:2 * hp + 2, :]], axis=1)
            dl_r = jnp.concatenate([st_t[8 + 2 * hp:9 + 2 * hp, :], st_t[9 + 2 * hp:10 + 2 * hp, :]], axis=1)
            return lse_r, dl_r

        st_t = [st_ref[b * BLK:(b + 1) * BLK, :].T for b in range(n)]
        if halo:
            nxt_thr = jnp.where((n * i + n) % nb == 0, 2 * BLK, 0)
            _, kj1, qi1 = _band_mask(0, BLK)
            mask_next = kj1 >= qi1 + nxt_thr
            stn_t = stn_ref[...].T

        for hp in range(4):
            cs = slice(hp * 128, (hp + 1) * 128)
            kb = [k_ref[b * BLK:(b + 1) * BLK, cs] for b in range(n)]
            vb = [v_ref[b * BLK:(b + 1) * BLK, cs] for b in range(n)]
            dk_acc = [jnp.zeros((BLK, 128), F32) for _ in range(n)]
            dv_acc = [jnp.zeros((BLK, 128), F32) for _ in range(n)]
            for b in range(n):
                rs = slice(b * BLK, (b + 1) * BLK)
                q2s = _stack_heads(q_ref[rs, cs], in_a)
                do2s = _stack_heads(do_ref[rs, cs], in_a)
                if b == 0:
                    kprev = kp_ref[:, cs] if halo else kb[0]
                    vprev = vp_ref[:, cs] if halo else vb[0]
                    mask = mask0
                else:
                    kprev, vprev, mask = kb[b - 1], vb[b - 1], (mask_first if b % nb == 0 else band)
                kk = jnp.concatenate([kprev, kb[b]], axis=0)
                vv = jnp.concatenate([vprev, vb[b]], axis=0)
                lse_r, dl_r = stat_rows(st_t[b], hp)
                s_t = jnp.where(mask, _mm_nt(kk, q2s) * SCALE, NEG)
                p_t = jnp.exp(s_t - lse_r)
                ds_t = (p_t * (_mm_nt(vv, do2s) - dl_r)).astype(BF16)
                dkk = _mm(ds_t, q2s) * SCALE
                dvv = _mm(p_t.astype(BF16), do2s)
                dqs = _mm_tn(ds_t, kk) * SCALE
                dq_ref[rs, cs] = jnp.where(in_a[0], dqs[:BLK], dqs[BLK:]).astype(BF16)
                dk_acc[b] += dkk[BLK:]
                dv_acc[b] += dvv[BLK:]
                if b > 0:
                    dk_acc[b - 1] += dkk[:BLK]
                    dv_acc[b - 1] += dvv[:BLK]
            if halo:
                q2s = _stack_heads(qn_ref[:, cs], in_a)
                do2s = _stack_heads(don_ref[:, cs], in_a)
                lse_r, dl_r = stat_rows(stn_t, hp)
                s_t = jnp.where(mask_next, _mm_nt(kb[n - 1], q2s) * SCALE, NEG)
                p_t = jnp.exp(s_t - lse_r)
                ds_t = (p_t * (_mm_nt(vb[n - 1], do2s) - dl_r)).astype(BF16)
                dk_acc[n - 1] += _mm(ds_t, q2s) * SCALE
                dv_acc[n - 1] += _mm(p_t.astype(BF16), do2s)
            for b in range(n):
                dk_ref[b * BLK:(b + 1) * BLK, cs] = dk_acc[b].astype(BF16)
                dv_ref[b * BLK:(b + 1) * BLK, cs] = dv_acc[b].astype(BF16)

    cur = pl.BlockSpec((CH, AW), lambda i: (i, 0))
    cur_st = pl.BlockSpec((CH, 128), lambda i: (i, 0))
    prev = pl.BlockSpec((BLK, AW), lambda i: (jnp.maximum(n * i - 1, 0), 0))
    nxt = pl.BlockSpec((BLK, AW), lambda i: (jnp.minimum(n * i + n, NBLK - 1), 0))
    nxt_st = pl.BlockSpec((BLK, 128), lambda i: (jnp.minimum(n * i + n, NBLK - 1), 0))
    ins = [cur] * 4 + [cur_st] + ([prev, prev, nxt, nxt, nxt_st] if halo else [])
    args = (q, k, v, do, st) + ((k, v, q, do, st) if halo else ())
    return pl.pallas_call(
        body, name=name, grid=(T // CH,),
        in_specs=ins,
        out_specs=[cur] * 3,
        out_shape=[jax.ShapeDtypeStruct((T, AW), BF16)] * 3,
        compiler_params=_cp(("parallel",)),
    )(*args)


TH = 256
NCH = TH // CHUNK


def _hgrn_common(hq_ref, hf_ref, lbr_ref, tri_ref):
    r0 = lbr_ref[0:1, :]
    r1 = lbr_ref[1:2, :]
    mx = jnp.maximum(r0, r1)
    e0 = jnp.exp(r0 - mx)
    e1 = jnp.exp(r1 - mx)
    lb = e0 / (e0 + e1)
    hqv = hq_ref[...].astype(F32)
    sq = _sigmoid(hqv)
    qv = hqv * sq
    sf = _sigmoid(hf_ref[...].astype(F32))
    f = lb + (1.0 - lb) * sf
    kv = 1.0 - f
    g = jnp.log(f)
    cum = _mm_exact_l(tri_ref[...], g)
    dec = jnp.exp(jnp.concatenate([cum[c * CHUNK + CHUNK - 1:(c + 1) * CHUNK, :] for c in range(NCH)], axis=0))
    decb = jnp.concatenate([jnp.broadcast_to(dec[c:c + 1, :], (CHUNK, HW)) for c in range(NCH)], axis=0)
    ea = jnp.exp(cum)
    ena = jnp.exp(-cum)
    eend = decb * ena
    return dict(lb=lb, hq=hqv, sq=sq, q=qv, sf=sf, f=f, k=kv, cum=cum, ea=ea, ena=ena, eend=eend,
                qd=qv * ea, ki=kv * ena, ke=kv * eend, dec=dec)


def _tri_mask(transposed=False):
    ti = lax.broadcasted_iota(jnp.int32, (TH, TH), 1 if transposed else 0)
    si = lax.broadcasted_iota(jnp.int32, (TH, TH), 0 if transposed else 1)
    return (si <= ti) & ((si // CHUNK) == (ti // CHUNK))


def _hgrn_fwd(hq, hf, hi, lbr, tri):
    def body(hq_ref, hf_ref, hi_ref, lbr_ref, tri_ref, rec_ref, sall_ref, st_scr):
        @pl.when(pl.program_id(0) == 0)
        def _():
            st_scr[...] = jnp.zeros_like(st_scr)

        w = _hgrn_common(hq_ref, hf_ref, lbr_ref, tri_ref)
        qd, ki, ke = w["qd"].astype(BF16), w["ki"].astype(BF16), w["ke"].astype(BF16)
        dec = w["dec"]
        vb = hi_ref[...]
        causal = _tri_mask()
        for h in range(4):
            cs = slice(h * 128, (h + 1) * 128)
            att = jnp.where(causal, _mm_nt(qd[:, cs], ki[:, cs]), 0.0)
            o_intra = _mm(att.astype(BF16), vb[:, cs])
            st = st_scr[:, cs]
            for c in range(NCH):
                rs = slice(c * CHUNK, (c + 1) * CHUNK)
                sall_ref[c, :, cs] = st
                rec_ref[rs, cs] = (o_intra[rs] + _mm_nt(qd[rs, cs], st.astype(BF16))).astype(BF16)
                st = dec[c:c + 1, cs] * st + _mm_tn(vb[rs, cs], ke[rs, cs])
            st_scr[:, cs] = st

    tok = pl.BlockSpec((TH, HW), lambda i: (i, 0))
    return pl.pallas_call(
        body, name="hgrn_fwd", grid=(T // TH,),
        in_specs=[tok, tok, tok, pl.BlockSpec((2, HW), lambda i: (0, 0)), pl.BlockSpec((TH, TH), lambda i: (0, 0))],
        out_specs=[tok, pl.BlockSpec((NCH, 128, HW), lambda i: (i, 0, 0))],
        out_shape=[jax.ShapeDtypeStruct((T, HW), BF16), jax.ShapeDtypeStruct((T // CHUNK, 128, HW), F32)],
        scratch_shapes=[pltpu.VMEM((128, HW), F32)],
        compiler_params=_cp(("arbitrary",)),
    )(hq, hf, hi, lbr, tri)


def _hgrn_bwd(hq, hf, hi, lbr, tri, trit, drec, sall, dhg, rout, routb):
    NT = T // TH

    def body(hq_ref, hf_ref, hi_ref, lbr_ref, tri_ref, trit_ref, do_ref, sall_ref, dhg_ref, rout_r, routb_r,
             dph_ref, small_ref, pout_o, poutr_o,
             dst_scr, dlb_scr, dqd_scr, dki_scr, dke_scr, dlast_scr, send_sems, recv_sems, loc_sems):
        step = pl.program_id(0)
        loc, rem = _chip_copies(_w_out_piece, rout_r, routb_r, pout_o, poutr_o, send_sems, recv_sems,
                                loc_sems.at[0])

        @pl.when(step == 0)
        def _():
            dst_scr[...] = jnp.zeros_like(dst_scr)
            dlb_scr[...] = jnp.zeros_like(dlb_scr)
            for cp in loc + rem:
                cp.start()

        w = _hgrn_common(hq_ref, hf_ref, lbr_ref, tri_ref)
        qd, ki, ke = w["qd"].astype(BF16), w["ki"].astype(BF16), w["ke"].astype(BF16)
        dec = w["dec"]
        vb = hi_ref[...]
        dob = do_ref[...].astype(BF16)
        causal = _tri_mask()
        causal_t = _tri_mask(transposed=True)
        for h in range(4):
            cs = slice(h * 128, (h + 1) * 128)
            att_t = jnp.where(causal_t, _mm_nt(ki[:, cs], qd[:, cs]), 0.0).astype(BF16)
            datt_t = jnp.where(causal_t, _mm_nt(vb[:, cs], dob[:, cs]), 0.0).astype(BF16)
            datt = jnp.where(causal, _mm_nt(dob[:, cs], vb[:, cs]), 0.0).astype(BF16)
            dv_intra = _mm(att_t, dob[:, cs])
            dqd_intra = _mm(datt, ki[:, cs])
            dki_scr[:, cs] = _mm(datt_t, qd[:, cs])
            dst = dst_scr[:, cs]
            for c in reversed(range(NCH)):
                rs = slice(c * CHUNK, (c + 1) * CHUNK)
                dec_c = dec[c:c + 1, :]
                st = sall_ref[c, :, cs]
                dstb = dst.astype(BF16)
                dph_ref[rs, 2 * HW + h * 128:2 * HW + (h + 1) * 128] = (
                    dv_intra[rs] + _mm_nt(ke[rs, cs], dstb)).astype(BF16)
                dqd_scr[rs, cs] = dqd_intra[rs] + _mm(dob[rs, cs], st.astype(BF16))
                dke_scr[rs, cs] = _mm(vb[rs, cs], dstb)
                ddec = jnp.sum(dst * st, axis=0, keepdims=True)
                dlast_scr[c:c + 1, cs] = ddec * dec_c[:, cs]
                dst = dec_c[:, cs] * dst + _mm_tn(dob[rs, cs], qd[rs, cs])
            dst_scr[:, cs] = dst
        dqd, dki, dke = dqd_scr[...], dki_scr[...], dke_scr[...]
        dq = dqd * w["ea"]
        dk = dki * w["ena"] + dke * w["eend"]
        dcum = dqd * w["qd"] - dki * w["ki"] - dke * w["ke"]
        dkeke = dke * w["ke"]
        dlastb = jnp.concatenate(
            [jnp.broadcast_to(dlast_scr[c:c + 1, :] + jnp.sum(dkeke[c * CHUNK:(c + 1) * CHUNK], axis=0, keepdims=True),
                              (CHUNK, HW)) for c in range(NCH)], axis=0)
        dg = _mm_exact_l(trit_ref[...], dcum) + dlastb
        df = dg / w["f"] - dk
        lb, sf, sq = w["lb"], w["sf"], w["sq"]
        dph_ref[:, HW:2 * HW] = (df * (1.0 - lb) * sf * (1.0 - sf)).astype(BF16)
        dph_ref[:, 0:HW] = (dq * (sq * (1.0 + w["hq"] * (1.0 - sq)))).astype(BF16)
        dph_ref[:, 3 * HW:4 * HW] = dhg_ref[...]
        dlb_scr[...] += jnp.sum(df * (1.0 - sf), axis=0, keepdims=True)

        @pl.when(step == NT - 1)
        def _():
            gr = dlb_scr[...] * lb * (1.0 - lb)
            small_ref[...] = jnp.zeros_like(small_ref)
            small_ref[0:1, 0:HW] = gr
            small_ref[1:2, 0:HW] = -gr
            for cp in rem:
                cp.wait_recv()
            for cp in rem:
                cp.wait_send()
            for cp in loc:
                cp.wait()

    tok = pl.BlockSpec((TH, HW), lambda i: (NT - 1 - i, 0))
    const = lambda shape: pl.BlockSpec(shape, lambda i: (0,) * len(shape))
    hbm = pl.BlockSpec(memory_space=pltpu.HBM)
    return pl.pallas_call(
        body, name="hgrn_bwd", grid=(NT,),
        in_specs=[tok, tok, tok, const((2, HW)), const((TH, TH)), const((TH, TH)), tok,
                  pl.BlockSpec((NCH, 128, HW), lambda i: (NT - 1 - i, 0, 0)), tok, hbm, hbm],
        out_specs=[pl.BlockSpec((TH, NCOL // 2), lambda i: (NT - 1 - i, 0)), const((8, D)), hbm, hbm],
        out_shape=[jax.ShapeDtypeStruct((T, NCOL // 2), BF16), jax.ShapeDtypeStruct((8, D), F32),
                   jax.ShapeDtypeStruct((128, D), F32), jax.ShapeDtypeStruct((3, 128, D), BF16)],
        scratch_shapes=[pltpu.VMEM((128, HW), F32), pltpu.VMEM((1, HW), F32), pltpu.VMEM((TH, HW), F32),
                        pltpu.VMEM((TH, HW), F32), pltpu.VMEM((TH, HW), F32), pltpu.VMEM((8, HW), F32),
                        pltpu.SemaphoreType.DMA((3,)), pltpu.SemaphoreType.DMA((3,)), pltpu.SemaphoreType.DMA((1,))],
        compiler_params=_cp(("arbitrary",)),
    )(hq, hf, hi, lbr, tri, trit, drec, sall, dhg, rout, routb)


def _fwd_out(o1, o4, o16, l1, l4, l16, rec, ag, hg, x, tgt, anw, hnw, fnw, wout_full, gmat, emat, selmat):
    TT = 256

    def body(o1_r, o4_r, o16_r, l1_r, l4_r, l16_r, rec_r, ag_r, hg_r, x_r, tgt_r, anw_r, hnw_r, fnw_r, wo_r, g_r,
             e_r, sel_r, dx2_o, do1_o, do4_o, do16_o, st1_o, st4_o, st16_o, drec_o, dag_o, dhg_o,
             rout_o, routb_o, small_o, scr_a, scr_b, scr_c, gwout_o, rbuf, send_sems, recv_sems):
        @pl.when(pl.program_id(0) == 0)
        def _():
            gwout_o[...] = jnp.zeros_like(gwout_o)
            small_o[...] = jnp.zeros_like(small_o)

        def unperm(r4, r16):
            return _unperm_load(r4, r16, scr_a, scr_b, scr_c)

        def perm_out(val, p1, p4, p16, dt):
            _perm_store(val, scr_a, scr_b, p1, p4, p16, dt)

        o4u, o16u = unperm(o4_r, o16_r)
        l4c, l16c = unperm(l4_r, l16_r)
        l1c = l1_r[...]
        mxc = jnp.maximum(jnp.maximum(l1c, l4c), l16c)
        w1c, w4c, w16c = jnp.exp(l1c - mxc), jnp.exp(l4c - mxc), jnp.exp(l16c - mxc)
        denc = w1c + w4c + w16c
        lane = lax.broadcasted_iota(jnp.int32, (1, 128), 1)
        lse_c = jnp.where(lane < 8, mxc + jnp.log(denc), 0.0)
        em = e_r[...]
        wn1 = _mm_exact_r(w1c / denc, em)
        wn4 = _mm_exact_r(w4c / denc, em)
        o1v = o1_r[...].astype(F32)
        attn = wn1 * o1v + wn4 * o4u + (1.0 - wn1 - wn4) * o16u
        gm = g_r[...]

        def head_mean_a(t):
            return jnp.concatenate([_mm_exact_r(t[:, :256], gm), _mm_exact_r(t[:, 256:], gm)], axis=1)

        def head_mean_h(t):
            return jnp.concatenate(
                [jnp.broadcast_to(jnp.mean(t[:, h * 128:(h + 1) * 128], axis=-1, keepdims=True), (TT, 128))
                 for h in range(4)], axis=1)

        rs_a = lax.rsqrt(head_mean_a(attn * attn) + EPS)
        n_a = attn * rs_a
        agv = ag_r[...].astype(F32)
        sg_a = _sigmoid(agv)
        si_a = agv * sg_a
        anw_v = anw_r[...]
        y_a = (n_a * anw_v) * si_a
        recv = rec_r[...].astype(F32)
        rs_h = lax.rsqrt(head_mean_h(recv * recv) + EPS)
        n_h = recv * rs_h
        hgv = hg_r[...].astype(F32)
        sg_h = _sigmoid(hgv)
        si_h = hgv * sg_h
        hnw_v = hnw_r[...]
        y_h = (n_h * hnw_v) * si_h
        mixed = jnp.concatenate([y_a, y_h], axis=1).astype(BF16)
        xv = x_r[...]
        x2 = xv + _mm(mixed, wo_r[...])
        r2 = lax.rsqrt(jnp.mean(x2 * x2, axis=-1, keepdims=True) + EPS)
        fnw_v = fnw_r[...]
        xn = x2 * r2
        err = xn * fnw_v - tgt_r[...]
        small_o[2:3, :] += 0.5 * jnp.sum(jnp.mean(err * err, axis=-1, keepdims=True), axis=0, keepdims=True)
        dy = err * (1.0 / D)
        small_o[0:1, :] += jnp.sum(dy * xn, axis=0, keepdims=True)
        dyw = dy * fnw_v
        dx2 = r2 * dyw - x2 * ((r2 * r2 * r2) * jnp.mean(dyw * x2, axis=-1, keepdims=True))
        dx2_o[...] = dx2
        dx2b = dx2.astype(BF16)
        gwout_o[...] += _mm_tn(mixed, dx2b)
        dmix = _mm_nt(dx2b, wo_r[...])
        dm_a, dm_h = dmix[:, :AW], dmix[:, AW:]
        dag_o[...] = (dm_a * (n_a * anw_v) * (sg_a * (1.0 + agv * (1.0 - sg_a)))).astype(BF16)
        dn_a = dm_a * anw_v * si_a
        small_o[1:2, 0:AW] += jnp.sum(dm_a * n_a * si_a, axis=0, keepdims=True)
        dattn = rs_a * (dn_a - n_a * head_mean_a(dn_a * n_a))
        perm_out(dattn, do1_o, do4_o, do16_o, BF16)
        stats = lse_c + _mm_exact_r(dattn * attn, sel_r[...])
        perm_out(stats, st1_o, st4_o, st16_o, F32)
        dhg_o[...] = (dm_h * (n_h * hnw_v) * (sg_h * (1.0 + hgv * (1.0 - sg_h)))).astype(BF16)
        dn_h = dm_h * hnw_v * si_h
        small_o[1:2, AW:] += jnp.sum(dm_h * n_h * si_h, axis=0, keepdims=True)
        drec_o[...] = (rs_h * (dn_h - n_h * head_mean_h(dn_h * n_h))).astype(BF16)

        @pl.when(pl.program_id(0) == T // TT - 1)
        def _():
            x, y, c = lax.axis_index("x"), lax.axis_index("y"), lax.axis_index("c")
            cps = [pltpu.make_async_remote_copy(
                src_ref=gwout_o.at[pl.ds(pl.multiple_of(j * 256 + (1 - c) * 128, 128), 128), :], dst_ref=rbuf.at[j],
                send_sem=send_sems.at[j], recv_sem=recv_sems.at[j], device_id=(x, y, 1 - c), device_id_type=MESH)
                for j in range(4)]
            for cp in cps:
                cp.start()
            for j, cp in enumerate(cps):
                cp.wait_recv()
                red = gwout_o[pl.ds(pl.multiple_of(j * 256 + c * 128, 128), 128), :] + rbuf[j]
                rout_o[j * 128:(j + 1) * 128, :] = red
                routb_o[j * 128:(j + 1) * 128, :] = red.astype(BF16)
            for cp in cps:
                cp.wait_send()

    tok = lambda w: pl.BlockSpec((TT, w), lambda i: (i, 0))
    d4 = pl.BlockSpec((4, TT // 4, AW), lambda i: (0, i, 0))
    d16 = pl.BlockSpec((16, TT // 16, AW), lambda i: (0, i, 0))
    const = lambda shape: pl.BlockSpec(shape, lambda i: (0,) * len(shape))
    sd = lambda shape, dt: jax.ShapeDtypeStruct(shape, dt)
    c4 = pl.BlockSpec((4, TT // 4, 128), lambda i: (0, i, 0))
    c16 = pl.BlockSpec((16, TT // 16, 128), lambda i: (0, i, 0))
    p3 = lambda w, dt: [sd((T, w), dt), sd((4, T // 4, w), dt), sd((16, T // 16, w), dt)]
    return pl.pallas_call(
        body, name="fwd_out", grid=(T // TT,),
        in_specs=[tok(AW), d4, d16, tok(128), c4, c16, tok(AW), tok(AW), tok(AW), tok(D), tok(D),
                  const((1, AW)), const((1, HW)), const((1, D)), const((D, D)), const((256, 256)),
                  const((128, AW)), const((AW, 128))],
        out_specs=[tok(D)] + [tok(AW), d4, d16] + [tok(128), c4, c16] + [tok(AW)] * 3
        + [const((512, D)), const((512, D)), const((8, D))],
        out_shape=[sd((T, D), F32)] + p3(AW, BF16) + p3(128, F32)
        + [sd((T, AW), BF16), sd((T, AW), BF16), sd((T, AW), BF16), sd((512, D), F32), sd((512, D), BF16),
           sd((8, D), F32)],
        scratch_shapes=[pltpu.VMEM((4, TT, 128), F32)] * 3 + [pltpu.VMEM((D, D), F32),
                        pltpu.VMEM((4, 128, D), F32), pltpu.SemaphoreType.DMA((4,)), pltpu.SemaphoreType.DMA((4,))],
        compiler_params=_cp(("arbitrary",)),
    )(o1, o4, o16, l1, l4, l16, rec, ag, hg, x, tgt, anw, hnw, fnw, wout_full, gmat, emat, selmat)


def _dproj_build(dq, dk, dv, dag, pos):
    TT = 512

    def body(dq1, dq4, dq16, dk1, dk4, dk16, dv1, dv4, dv16, dag_r, pos_r, dproj_o, scr_a, scr_b, scr_c):
        def unperm_sum(r1, r4, r16):
            u4, u16 = _unperm_load(r4, r16, scr_a, scr_b, scr_c)
            return r1[...] + u4 + u16

        cosf, s1, s2 = _rope_tables(pos_r[...])
        dproj_o[:, 0:512] = _rope_bwd(unperm_sum(dq1, dq4, dq16), cosf, s1, s2).astype(BF16)
        dproj_o[:, 512:1024] = _rope_bwd(unperm_sum(dk1, dk4, dk16), cosf, s1, s2).astype(BF16)
        dproj_o[:, 1024:1536] = unperm_sum(dv1, dv4, dv16).astype(BF16)
        dproj_o[:, 1536:2048] = dag_r[...]

    tok = lambda w: pl.BlockSpec((TT, w), lambda i: (i, 0))
    d4 = pl.BlockSpec((4, TT // 4, AW), lambda i: (0, i, 0))
    d16 = pl.BlockSpec((16, TT // 16, AW), lambda i: (0, i, 0))
    return pl.pallas_call(
        body, name="dproj_build", grid=(T // TT,),
        in_specs=[tok(AW), d4, d16] * 3 + [tok(AW), tok(1)],
        out_specs=tok(NCOL // 2),
        out_shape=jax.ShapeDtypeStruct((T, NCOL // 2), BF16),
        scratch_shapes=[pltpu.VMEM((4, TT, 128), F32)] * 3,
        compiler_params=_cp(("parallel",)),
    )(*dq, *dk, *dv, dag, pos)


def _bwd_x(dproj_a, dproj_h, x, dx2, mixw, w_full, rin, rinb, small4, small6, pout_own, pout_rem):
    TT = 256
    NT = T // TT

    def body(dpa_r, dph_r, x_r, dx2_r, mw_r, w_r, rin_r, rinb_r, s4_r, s6_r, poo_r, por_r,
             gx_o, pin_o, pinr_o, sall_o, fin_o, fout_o, sbuf, v_own, v_rem, vo_own, vo_rem, sin, sout, got_in,
             got_out, send_sems, recv_sems, loc_sems, share_send, share_recv, fin_sems):
        i = pl.program_id(0)
        loc, rem = _chip_copies(_w_in_piece, rin_r, rinb_r, pin_o, pinr_o, send_sems, recv_sems, loc_sems.at[0])

        @pl.when(i == 0)
        def _():
            sbuf[...] = jnp.zeros_like(sbuf)
            for cp in loc + rem:
                cp.start()

        dhn = _mm_nt(dpa_r[...], w_r[:, 0:NCOL // 2]) + _mm_nt(dph_r[...], w_r[:, NCOL // 2:NCOL])
        xv = x_r[...]
        r = lax.rsqrt(jnp.mean(xv * xv, axis=-1, keepdims=True) + EPS)
        dxw = dhn * mw_r[...]
        gx_o[...] = dx2_r[...] + r * dxw - xv * ((r * r * r) * jnp.mean(dxw * xv, axis=-1, keepdims=True))
        sbuf[16:17, :] += jnp.sum(dhn * (xv * r), axis=0, keepdims=True)

        @pl.when(i == NT - 1)
        def _():
            sbuf[0:8, :] = s4_r[...]
            sbuf[8:16, :] = s6_r[...]
            sloc, srem = _small_copies(sbuf, sall_o, send_sems, recv_sems, loc_sems.at[1])
            for cp in sloc + srem:
                cp.start()
            for cp in rem:
                cp.wait_recv()
            for cp in rem:
                cp.wait_send()
            for cp in loc:
                cp.wait()
            mx, my, c = lax.axis_index("x"), lax.axis_index("y"), lax.axis_index("c")
            loads = [pltpu.make_async_copy(pin_o, v_own, fin_sems.at[0]),
                     pltpu.make_async_copy(pinr_o, v_rem, fin_sems.at[1]),
                     pltpu.make_async_copy(poo_r, vo_own, fin_sems.at[2]),
                     pltpu.make_async_copy(por_r, vo_rem, fin_sems.at[3])]
            for cp in loads:
                cp.start()
            for cp in loads:
                cp.wait()
            sout[...] = ((vo_own[...] + vo_rem[0].astype(F32)) + vo_rem[1].astype(F32)) + vo_rem[2].astype(F32)
            sin[...] = ((v_own[...] + v_rem[0].astype(F32)) + v_rem[1].astype(F32)) + v_rem[2].astype(F32)
            swap = [pltpu.make_async_remote_copy(src_ref=sin, dst_ref=got_in, send_sem=share_send.at[0],
                                                 recv_sem=share_recv.at[0], device_id=(mx, my, 1 - c),
                                                 device_id_type=MESH),
                    pltpu.make_async_remote_copy(src_ref=sout, dst_ref=got_out, send_sem=share_send.at[1],
                                                 recv_sem=share_recv.at[1], device_id=(mx, my, 1 - c),
                                                 device_id_type=MESH)]
            for cp in swap:
                cp.start()
            mine = [pltpu.make_async_copy(sin, fin_o.at[c], fin_sems.at[0]),
                    pltpu.make_async_copy(sout, fout_o.at[c], fin_sems.at[1])]
            for cp in mine:
                cp.start()
            for cp in swap:
                cp.wait_recv()
            theirs = [pltpu.make_async_copy(got_in, fin_o.at[1 - c], fin_sems.at[2]),
                      pltpu.make_async_copy(got_out, fout_o.at[1 - c], fin_sems.at[3])]
            for cp in theirs:
                cp.start()
            for cp in swap:
                cp.wait_send()
            for cp in mine + theirs:
                cp.wait()
            for cp in srem:
                cp.wait_recv()
            for cp in srem:
                cp.wait_send()
            for cp in sloc:
                cp.wait()

    tok = lambda w: pl.BlockSpec((TT, w), lambda i: (i, 0))
    const = lambda shape: pl.BlockSpec(shape, lambda i: (0,) * len(shape))
    hbm = pl.BlockSpec(memory_space=pltpu.HBM)
    return pl.pallas_call(
        body, name="bwd_x", grid=(NT,),
        in_specs=[tok(NCOL // 2), tok(NCOL // 2), tok(D), tok(D), const((1, D)), const((D, NCOL)), hbm, hbm,
                  const((8, D)), const((8, D)), hbm, hbm],
        out_specs=[tok(D), hbm, hbm, hbm, hbm, hbm],
        out_shape=[jax.ShapeDtypeStruct((T, D), F32),
                   jax.ShapeDtypeStruct((512, 1024), F32), jax.ShapeDtypeStruct((3, 512, 1024), BF16),
                   jax.ShapeDtypeStruct((8, 24, D), F32),
                   jax.ShapeDtypeStruct((2, 512, 1024), F32), jax.ShapeDtypeStruct((2, 128, D), F32)],
        scratch_shapes=[pltpu.VMEM((24, D), F32),
                        pltpu.VMEM((512, 1024), F32), pltpu.VMEM((3, 512, 1024), BF16),
                        pltpu.VMEM((128, D), F32), pltpu.VMEM((3, 128, D), BF16),
                        pltpu.VMEM((512, 1024), F32), pltpu.VMEM((128, D), F32),
                        pltpu.VMEM((512, 1024), F32), pltpu.VMEM((128, D), F32),
                        pltpu.SemaphoreType.DMA((10,)), pltpu.SemaphoreType.DMA((10,)), pltpu.SemaphoreType.DMA((2,)),
                        pltpu.SemaphoreType.DMA((2,)), pltpu.SemaphoreType.DMA((2,)), pltpu.SemaphoreType.DMA((4,))],
        compiler_params=_cp(("arbitrary",)),
    )(dproj_a, dproj_h, x, dx2, mixw, w_full, rin, rinb, small4, small6, pout_own, pout_rem)


def _grad_w_in(hn, dproj_a, dproj_h):
    TK = 2048
    NK = T // TK

    def body(hnt_r, dpa_r, dph_r, rin_o, rinb_o, acc, rbuf, obuf, obufb, send_sems, recv_sems, wb_sems):
        j = pl.program_id(0)
        kk = pl.program_id(1)
        x, y, c = lax.axis_index("x"), lax.axis_index("y"), lax.axis_index("c")
        mine = pl.ds(pl.multiple_of(c * 512, 512), 512)
        theirs = pl.ds(pl.multiple_of((1 - c) * 512, 512), 512)

        def send(jj):
            return pltpu.make_async_remote_copy(
                src_ref=acc.at[jj % 2, theirs, :], dst_ref=rbuf.at[jj], send_sem=send_sems.at[jj],
                recv_sem=recv_sems.at[jj], device_id=(x, y, 1 - c), device_id_type=MESH)

        def writeback(jj):
            cols = pl.ds(jj * 1024, 1024)
            return [pltpu.make_async_copy(obuf.at[jj % 2], rin_o.at[:, cols], wb_sems.at[jj % 2]),
                    pltpu.make_async_copy(obufb.at[jj % 2], rinb_o.at[:, cols], wb_sems.at[2 + jj % 2])]

        def wait_writeback(jj):
            for cp in writeback(jj):
                cp.wait()

        def finalize(jj):
            send(jj).wait_recv()
            red = acc[jj % 2, mine, :] + rbuf[jj]
            obuf[jj % 2] = red
            obufb[jj % 2] = red.astype(BF16)
            for cp in writeback(jj):
                cp.start()

        prod = _mm(hnt_r[...], jnp.where(j < 2, dpa_r[...], dph_r[...]))

        @pl.when(kk == 0)
        def _():
            for jj in (2, 3):
                @pl.when(j == jj)
                def _():
                    send(jj - 2).wait_send()
            acc[j % 2] = prod

        @pl.when(kk > 0)
        def _():
            acc[j % 2] += prod

        @pl.when(kk == NK - 1)
        def _():
            for jj in range(4):
                @pl.when(j == jj)
                def _():
                    send(jj).start()
                    if jj in (1, 2):
                        finalize(jj - 1)
                    if jj == 3:
                        wait_writeback(0)
                        finalize(2)
                        wait_writeback(1)
                        finalize(3)
                        wait_writeback(2)
                        wait_writeback(3)
                        send(2).wait_send()
                        send(3).wait_send()

    hbm = pl.BlockSpec(memory_space=pltpu.HBM)
    return pl.pallas_call(
        body, name="grad_w_in", grid=(4, NK),
        in_specs=[pl.BlockSpec((D, TK), lambda j, kk: (0, kk)),
                  pl.BlockSpec((TK, 1024), lambda j, kk: (jnp.where(j < 2, kk, NK - 1), jnp.minimum(j, 1))),
                  pl.BlockSpec((TK, 1024), lambda j, kk: (jnp.where(j < 2, 0, kk), jnp.maximum(j - 2, 0)))],
        out_specs=[hbm, hbm],
        out_shape=[jax.ShapeDtypeStruct((512, NCOL), F32), jax.ShapeDtypeStruct((512, NCOL), BF16)],
        scratch_shapes=[pltpu.VMEM((2, D, 1024), F32), pltpu.VMEM((4, 512, 1024), F32), pltpu.VMEM((2, 512, 1024), F32),
                        pltpu.VMEM((2, 512, 1024), BF16),
                        pltpu.SemaphoreType.DMA((4,)), pltpu.SemaphoreType.DMA((4,)), pltpu.SemaphoreType.DMA((4,))],
        compiler_params=_cp(("arbitrary", "arbitrary")),
    )(hn, dproj_a, dproj_h)


def _w_in_piece(ref, j):
    return ref.at[:, pl.ds(j * 1024, 1024)]


def _w_out_piece(ref, j):
    return ref.at[pl.ds(j * 128, 128), :]


def _chip_copies(piece, src_r, srcb_r, own_o, rem_o, send_sems, recv_sems, loc_sem):
    x, y, c = lax.axis_index("x"), lax.axis_index("y"), lax.axis_index("c")
    chips = [(1 - x, y), (x, 1 - y), (1 - x, 1 - y)]
    loc = [pltpu.make_async_copy(piece(src_r, 2 * x + y), own_o, loc_sem)]
    rem = [pltpu.make_async_remote_copy(
        src_ref=piece(srcb_r, 2 * px + py), dst_ref=rem_o.at[k], send_sem=send_sems.at[k],
        recv_sem=recv_sems.at[k], device_id=(px, py, c), device_id_type=MESH) for k, (px, py) in enumerate(chips)]
    return loc, rem


def _small_copies(small_r, sall_o, send_sems, recv_sems, loc_sem):
    x, y, c = lax.axis_index("x"), lax.axis_index("y"), lax.axis_index("c")
    me = 4 * x + 2 * y + c
    loc = [pltpu.make_async_copy(small_r, sall_o.at[me], loc_sem)]
    rem = []
    k = 3
    for fx in range(2):
        for fy in range(2):
            for fc in range(2):
                if fx or fy or fc:
                    peer = (1 - x if fx else x, 1 - y if fy else y, 1 - c if fc else c)
                    rem.append(pltpu.make_async_remote_copy(
                        src_ref=small_r, dst_ref=sall_o.at[me], send_sem=send_sems.at[k],
                        recv_sem=recv_sems.at[k], device_id=peer, device_id_type=MESH))
                    k += 1
    return loc, rem


def _adamw_math(w, g, m, v):
    m = B1 * m + (1.0 - B1) * g
    v = B2 * v + (1.0 - B2) * (g * g)
    m_hat = m / (1.0 - B1 ** STEP)
    v_hat = v / (1.0 - B2 ** STEP)
    delta = -LR * (m_hat / (jnp.sqrt(v_hat) + AEPS) + WD * w)
    return delta, m, v


def _adamw(big_in, big_out, sall, params):
    def body(*refs):
        wi, gi, mi, vi, wo, go, mo, vo, sall_r = refs[:9]
        ins = refs[9:24]
        di_o, mi_o, vi_o, do_o, mo_o, vo_o = refs[24:30]
        outs = refs[30:]
        d, mm, vv = _adamw_math(wi[...], gi[...], mi[...], vi[...])
        di_o[...] = d
        mi_o[...] = mm
        vi_o[...] = vv

        @pl.when(pl.program_id(0) == 0)
        def _():
            d, mm, vv = _adamw_math(wo[...], go[...], mo[...], vo[...])
            do_o[...] = d
            mo_o[...] = mm
            vo_o[...] = vv
            tot = sall_r[0]
            for dv in range(1, 8):
                tot = tot + sall_r[dv]
            grads = [tot[16:17, :], tot[1:2, 0:AW], tot[1:2, AW:], tot[8:10, 0:HW], tot[0:1, :]]
            outs[0][...] = tot[2:3, 0:1]
            for p in range(5):
                w_r, m_r, v_r = ins[3 * p:3 * p + 3]
                g = grads[p]
                d, mm, vv = _adamw_math(w_r[...], g, m_r[...], v_r[...])
                outs[1 + 4 * p][...] = g
                outs[2 + 4 * p][...] = d
                outs[3 + 4 * p][...] = mm
                outs[4 + 4 * p][...] = vv

    flat = [a for p in params for a in p]
    shapes = [jax.ShapeDtypeStruct((D, 1024), F32)] * 3 + [jax.ShapeDtypeStruct((256, D), F32)] * 3
    shapes += [jax.ShapeDtypeStruct((1, 1), F32)]
    for p in params:
        shapes += [jax.ShapeDtypeStruct(p[0].shape, F32)] * 4
    vm = pl.BlockSpec(memory_space=pltpu.VMEM)
    rows = pl.BlockSpec((256, 1024), lambda i: (i, 0))
    whole = pl.BlockSpec((256, D), lambda i: (0, 0))
    return pl.pallas_call(
        body, name="adamw", grid=(4,),
        in_specs=[rows] * 4 + [whole] * 4 + [vm] * 16, out_specs=[rows] * 3 + [whole] * 3 + [vm] * 21,
        out_shape=shapes,
        compiler_params=_cp(("arbitrary",)),
    )(*big_in, *big_out, sall, *flat)


def kernel(x, positions, w_in, w_out, mix_norm_w, attn_out_norm_w, hgrn_out_norm_w, hgrn_lb_raw, final_norm_w, loss_target, m_w_in, m_w_out, m_mix_norm_w, m_attn_out_norm_w, m_hgrn_out_norm_w, m_hgrn_lb_raw, m_final_norm_w, v_w_in, v_w_out, v_mix_norm_w, v_attn_out_norm_w, v_hgrn_out_norm_w, v_hgrn_lb_raw, v_final_norm_w):
    xs = x.reshape(T, D)
    tgt = loss_target.reshape(T, D)
    pos = positions.reshape(T, 1)
    fnw = final_norm_w.reshape(1, D)

    ti = np.arange(TH)
    tri_np = ((ti[:, None] // CHUNK == ti[None, :] // CHUNK) & (ti[None, :] <= ti[:, None])).astype(np.float32)
    tri = jnp.asarray(tri_np, BF16)
    trit = jnp.asarray(tri_np.T, BF16)
    hi_ = np.arange(AW) // HEAD
    gmat = jnp.asarray((hi_[:256, None] == hi_[None, :256]).astype(np.float32) / HEAD, BF16)
    emat_np = (np.arange(128)[:, None] == hi_[None, :]).astype(np.float32)
    sel_np = (8 + hi_[:, None] == np.arange(128)[None, :]).astype(np.float32)
    emat = jnp.asarray(emat_np, BF16)
    selmat = jnp.asarray(sel_np, BF16)

    jm_arr = (2 * lax.axis_index("x") + lax.axis_index("y")).astype(jnp.int32).reshape(1)
    (hn, q1, k1, v1, q4, k4, v4, q16, k16, v16, ag, hq, hf, hi, hg, w_full, wout4) = _fwd_in(
        xs, pos, mix_norm_w, w_in.reshape(D, 1024), w_out.reshape(256, D), jm_arr)
    wout_full = wout4.reshape(D, D)
    flat = lambda a: a.reshape(T, AW)
    o1, l1 = _attn_fwd(q1, k1, v1, T // BLK, "attn_fwd_d1")
    o4, l4 = _attn_fwd(flat(q4), flat(k4), flat(v4), T // 4 // BLK, "attn_fwd_d4")
    o16, l16 = _attn_fwd(flat(q16), flat(k16), flat(v16), T // 16 // BLK, "attn_fwd_d16")
    rec, sall = _hgrn_fwd(hq, hf, hi, hgrn_lb_raw, tri)

    (dx2, do1, do4, do16, st1, st4, st16, drec, dag, dhg, rout, routb, small4) = _fwd_out(
        o1, o4.reshape(4, T // 4, AW), o16.reshape(16, T // 16, AW),
        l1, l4.reshape(4, T // 4, 128), l16.reshape(16, T // 16, 128),
        rec, ag, hg, xs, tgt, attn_out_norm_w, hgrn_out_norm_w, fnw, wout_full, gmat, emat, selmat)

    fst = lambda a: a.reshape(T, 128)
    dq1, dk1, dv1 = _attn_bwd(q1, k1, v1, do1, st1, T // BLK, "attn_bwd_d1")
    dq4, dk4, dv4 = _attn_bwd(flat(q4), flat(k4), flat(v4), flat(do4), fst(st4), T // 4 // BLK, "attn_bwd_d4")
    dq16, dk16, dv16 = _attn_bwd(flat(q16), flat(k16), flat(v16), flat(do16), fst(st16), T // 16 // BLK,
                                 "attn_bwd_d16")
    dproj_h, small6, pout_own, pout_rem = _hgrn_bwd(hq, hf, hi, hgrn_lb_raw, tri, trit, drec, sall, dhg,
                                                    rout, routb)

    r4 = lambda a: a.reshape(4, T // 4, AW)
    r16 = lambda a: a.reshape(16, T // 16, AW)
    dproj_a = _dproj_build((dq1, r4(dq4), r16(dq16)), (dk1, r4(dk4), r16(dk16)), (dv1, r4(dv4), r16(dv16)),
                           dag, pos)
    rin, rinb = _grad_w_in(hn, dproj_a, dproj_h)
    gx, _, _, small_all, fin, fout = _bwd_x(dproj_a, dproj_h, xs, dx2, mix_norm_w, w_full, rin, rinb,
                                            small4, small6, pout_own, pout_rem)
    g_w_in = fin.reshape(D, 1024)
    g_w_out = fout.reshape(256, D)

    params = [(mix_norm_w, m_mix_norm_w, v_mix_norm_w),
              (attn_out_norm_w, m_attn_out_norm_w, v_attn_out_norm_w),
              (hgrn_out_norm_w, m_hgrn_out_norm_w, v_hgrn_out_norm_w),
              (hgrn_lb_raw, m_hgrn_lb_raw, v_hgrn_lb_raw),
              (fnw, m_final_norm_w.reshape(1, D), v_final_norm_w.reshape(1, D))]
    d_in, nm_in, nv_in, d_out, nm_out, nv_out, *so = _adamw(
        (w_in.reshape(D, 1024), g_w_in, m_w_in.reshape(D, 1024), v_w_in.reshape(D, 1024)),
        (w_out.reshape(256, D), g_w_out, m_w_out.reshape(256, D), v_w_out.reshape(256, D)), small_all, params)
    loss = so[0].reshape(())
    g_s = [so[1 + 4 * p] for p in range(5)]
    d_s = [so[2 + 4 * p] for p in range(5)]
    m_s = [so[3 + 4 * p] for p in range(5)]
    v_s = [so[4 + 4 * p] for p in range(5)]
    for lst in (g_s, d_s, m_s, v_s):
        lst[4] = lst[4].reshape(D)

    return (loss, gx.reshape(1, T, D),
            g_w_in.reshape(1, D, 1024), g_w_out.reshape(1, 256, D), *g_s,
            d_in.reshape(1, D, 1024), d_out.reshape(1, 256, D), *d_s,
            nm_in.reshape(1, D, 1024), nm_out.reshape(1, 256, D), *m_s,
            nv_in.reshape(1, D, 1024), nv_out.reshape(1, 256, D), *v_s)
```

```python
import functools

import numpy as np
import jax
import jax.numpy as jnp
from jax import lax
from jax.experimental import pallas as pl
from jax.experimental.pallas import tpu as pltpu

F32 = jnp.float32
BF16 = jnp.bfloat16

T = 4096
D = 1024
AW = 512
HW = 512
NCOL = 4096
HEAD = 64
BLK = 128
CHUNK = 64
EPS = 1e-6
SCALE = HEAD ** -0.5
NEG = -1e30
ROPE_THETA = 500000.0
INV_FREQ = [float(v) for v in
            (np.float32(ROPE_THETA) ** (-(np.arange(8, dtype=np.float32)) * np.float32(0.125)))]
LR, B1, B2, AEPS, WD, STEP = 0.001, 0.9, 0.999, 1e-08, 0.01, 10
VMEM_LIMIT = 63 * 1024 * 1024
MESH = pl.DeviceIdType.MESH


def _cp(sem=None, **kw):
    return pltpu.CompilerParams(dimension_semantics=sem, vmem_limit_bytes=VMEM_LIMIT, **kw)


def _mm(a, b):
    return jnp.dot(a, b, preferred_element_type=F32)


def _mm_nt(a, b):
    return lax.dot_general(a, b, (((1,), (1,)), ((), ())), preferred_element_type=F32)


def _mm_tn(a, b):
    return lax.dot_general(a, b, (((0,), (0,)), ((), ())), preferred_element_type=F32)


def _mm_exact_l(mat_bf, x):
    h = x.astype(BF16)
    l = (x - h.astype(F32)).astype(BF16)
    return _mm(mat_bf, h) + _mm(mat_bf, l)


def _mm_exact_r(x, mat_bf):
    h = x.astype(BF16)
    l = (x - h.astype(F32)).astype(BF16)
    return _mm(h, mat_bf) + _mm(l, mat_bf)


def _sigmoid(x):
    return 0.5 * jnp.tanh(0.5 * x) + 0.5


def _rope_tables(pos):
    lane = lax.broadcasted_iota(jnp.int32, (1, 128), 1)
    jl = lane & 63
    fi = jl & 7
    inv = jnp.zeros((1, 128), F32)
    for kk in range(8):
        inv = jnp.where(fi == kk, INV_FREQ[kk], inv)
    ang = pos.astype(F32) * inv
    c = jnp.cos(ang)
    s = jnp.sin(ang)
    cosf = jnp.where(jl < 16, c, 1.0)
    s1 = jnp.where(jl < 8, -s, 0.0)
    s2 = jnp.where((jl >= 8) & (jl < 16), s, 0.0)
    return cosf, s1, s2


def _rope(t, cosf, s1, s2):
    parts = []
    for ci in range(t.shape[1] // 128):
        tc = t[:, ci * 128:(ci + 1) * 128]
        parts.append(tc * cosf + pltpu.roll(tc, 120, 1) * s1 + pltpu.roll(tc, 8, 1) * s2)
    return jnp.concatenate(parts, axis=1)


def _rope_bwd(g, cosf, s1, s2):
    parts = []
    for ci in range(g.shape[1] // 128):
        gc = g[:, ci * 128:(ci + 1) * 128]
        parts.append(gc * cosf + pltpu.roll(gc * s1, 8, 1) + pltpu.roll(gc * s2, 120, 1))
    return jnp.concatenate(parts, axis=1)


def _perm_store(val, scr, scr2, o1, o4, o16, dt):
    n = val.shape[0]
    q = n // 4
    o1[...] = val.astype(dt)
    for ci in range(val.shape[1] // 128):
        cs = slice(ci * 128, (ci + 1) * 128)
        scr[ci] = val[:, cs]
        for r4 in range(4):
            part = scr[ci, pl.ds(r4, q, stride=4), :]
            o4[r4, :, cs] = part.astype(dt)
            scr2[ci, r4 * q:(r4 + 1) * q, :] = part
        for r4 in range(4):
            for b in range(4):
                o16[r4 + 4 * b, :, cs] = scr2[ci, pl.ds(r4 * q + b, q // 4, stride=4), :].astype(dt)


def _unperm_load(r4, r16, scr_a, scr_b, scr_c):
    n = scr_a.shape[1]
    q = n // 4
    nc = r4.shape[-1] // 128
    for ci in range(nc):
        cs = slice(ci * 128, (ci + 1) * 128)
        for rr in range(4):
            scr_a[ci, pl.ds(rr, q, stride=4), :] = r4[rr, :, cs].astype(F32)
        for rr in range(4):
            for b in range(4):
                scr_c[ci, pl.ds(rr * q + b, q // 4, stride=4), :] = r16[rr + 4 * b, :, cs].astype(F32)
        for rr in range(4):
            scr_b[ci, pl.ds(rr, q, stride=4), :] = scr_c[ci, rr * q:(rr + 1) * q, :]
    return (jnp.concatenate([scr_a[ci] for ci in range(nc)], axis=1),
            jnp.concatenate([scr_b[ci] for ci in range(nc)], axis=1))


def _fwd_in(x, pos, mixw, w_in, w_out, jm_arr):
    TT = 512
    NT = T // TT

    def body(jm_ref, x_ref, pos_ref, mw_ref, win_ref, wout_ref,
             hnt_ref, q1, k1, v1, q4, k4, v4, q16, k16, v16, ag, hq, hf, hi, hg, wfull_o, woutfull_o,
             wbuf, wobuf, hn_all, scr, scr2, stage, send_sems, recv_sems, loc_sems):
        s = pl.program_id(0)
        i = pl.program_id(1)
        mx, my, c = lax.axis_index("x"), lax.axis_index("y"), lax.axis_index("c")
        me, sibling = (mx, my, c), (mx, my, 1 - c)
        chips = [(mx, 1 - my), (1 - mx, my), (1 - mx, 1 - my)]
        jm = 2 * mx + my
        rows_in = [pl.ds(pl.multiple_of(h * 512, 512), 512) for h in (c, 1 - c)]
        rows_out = [pl.ds(pl.multiple_of(h * 128, 128), 128) for h in (c, 1 - c)]

        def blk(k):
            return lax.bitwise_xor(jm, k + 1)

        def rc(n, ref, to):
            return pltpu.make_async_remote_copy(src_ref=ref, dst_ref=ref, send_sem=send_sems.at[n],
                                                recv_sem=recv_sems.at[n], device_id=to, device_id_type=MESH)

        halves = [pl.ds(0, 512), pl.ds(512, 512)]
        send_in = lambda k, h: rc(12 + 2 * k + h, wbuf.at[jm, rows_in[0], halves[h]], (*chips[k], c))
        got_in = lambda k, h: rc(12 + 2 * k + h, wbuf.at[blk(k), rows_in[0], halves[h]], me)
        relay = lambda h: rc(16 + h, wbuf.at[blk(h), rows_in[0], halves[h]], (*chips[1 - h], c))
        got_relay = lambda h: rc(16 + h, wbuf.at[blk(2), rows_in[0], halves[h]], me)
        send_out = lambda k: rc(3 + k, wobuf.at[jm, rows_out[0], :], (*chips[k], c))
        got_out = lambda k: rc(3 + k, wobuf.at[blk(k), rows_out[0], :], me)
        pass_in = lambda k: rc(6 + k, wbuf.at[blk(k), rows_in[0], :], sibling)
        pass_out = lambda k: rc(9 + k, wobuf.at[blk(k), rows_out[0], :], sibling)
        passed_in = lambda k: rc(6 + k, wbuf.at[blk(k), rows_in[1], :], me)
        passed_out = lambda k: rc(9 + k, wobuf.at[blk(k), rows_out[1], :], me)

        def keep(j, n):
            return pltpu.make_async_copy(wbuf.at[j], wfull_o.at[:, pl.ds(j * 1024, 1024)], loc_sems.at[n])

        @pl.when((s == 0) & (i == 0))
        def _():
            for p in range(5):
                src = win_ref.at[pl.ds(p * 256, 256), :] if p < 4 else wout_ref
                load = pltpu.make_async_copy(src, stage, loc_sems.at[4])
                load.start()
                load.wait()
                if p < 4:
                    wbuf[jm, p * 256:(p + 1) * 256, :] = stage[...].astype(BF16)
                else:
                    wobuf[jm] = stage[...].astype(BF16)
            for k in range(2):
                for h in range(2):
                    send_in(k, h).start()
            keep(jm, 0).start()

        def arrive(k):
            if k == 0:
                for kk in range(2):
                    for h in range(2):
                        got_in(kk, h).wait_recv()
                relay(0).start()
                relay(1).start()
            if k == 2:
                got_relay(0).wait_recv()
                got_relay(1).wait_recv()
            pass_in(k).start()
            passed_in(k).wait_recv()
            keep(blk(k), k + 1).start()
            if k == 2:
                for kk in range(3):
                    send_out(kk).start()

        pl.when((s == 1) & (i == 0))(functools.partial(arrive, 0))

        @pl.when((s == 2) & (i == 0))
        def _():
            arrive(1)
            arrive(2)

        tile = pl.ds(pl.multiple_of(i * TT, TT), TT)

        @pl.when(s == 0)
        def _():
            xv = x_ref[...]
            r = lax.rsqrt(jnp.mean(xv * xv, axis=-1, keepdims=True) + EPS)
            hnf = (xv * r) * mw_ref[...]
            hn_all[tile, :] = hnf.astype(BF16)
            hnt_ref[...] = hnf.T.astype(BF16)

        def project(jj):
            hn = hn_all[tile, :]
            lo = _mm(hn, wbuf[jj, :, 0:512])
            hi_cols = _mm(hn, wbuf[jj, :, 512:1024])
            if jj == 0:
                cosf, s1, s2 = _rope_tables(pos_ref[...])
                _perm_store(_rope(lo, cosf, s1, s2), scr, scr2, q1, q4, q16, BF16)
                _perm_store(_rope(hi_cols, cosf, s1, s2), scr, scr2, k1, k4, k16, BF16)
            elif jj == 1:
                _perm_store(lo, scr, scr2, v1, v4, v16, BF16)
                ag[...] = hi_cols.astype(BF16)
            elif jj == 2:
                hq[...] = lo.astype(BF16)
                hf[...] = hi_cols.astype(BF16)
            else:
                hi[...] = lo.astype(BF16)
                hg[...] = hi_cols.astype(BF16)

        def project_block(j):
            for jj in range(4):
                pl.when(j == jj)(functools.partial(project, jj))

        @pl.when(s < 2)
        def _():
            project_block(lax.bitwise_xor(jm, s))

        @pl.when(s == 2)
        def _():
            project_block(lax.bitwise_xor(jm, 2))
            project_block(lax.bitwise_xor(jm, 3))

        @pl.when((s == 2) & (i == NT - 1))
        def _():
            for k in range(3):
                got_out(k).wait_recv()
                pass_out(k).start()
            for k in range(3):
                passed_out(k).wait_recv()
            out = pltpu.make_async_copy(wobuf, woutfull_o, loc_sems.at[4])
            out.start()
            for h in range(2):
                relay(h).wait_send()
                for k in range(2):
                    send_in(k, h).wait_send()
            for k in range(3):
                send_out(k).wait_send()
                pass_in(k).wait_send()
                pass_out(k).wait_send()
            keep(jm, 0).wait()
            for k in range(3):
                keep(blk(k), k + 1).wait()
            out.wait()

    def at_stage_of(jb):
        def index(s, i, jm_ref):
            sa = jnp.minimum(lax.bitwise_xor(jm_ref[0], jb), 2)
            return jnp.where(s < sa, 0, jnp.where(s == sa, i, NT - 1))
        return index

    tok = lambda w, jb: pl.BlockSpec((TT, w), lambda s, i, jm_ref: (at_stage_of(jb)(s, i, jm_ref), 0))
    d4 = lambda jb: pl.BlockSpec((4, TT // 4, AW), lambda s, i, jm_ref: (0, at_stage_of(jb)(s, i, jm_ref), 0))
    d16 = lambda jb: pl.BlockSpec((16, TT // 16, AW), lambda s, i, jm_ref: (0, at_stage_of(jb)(s, i, jm_ref), 0))
    hbm = pl.BlockSpec(memory_space=pltpu.HBM)
    sd = lambda shape, dt: jax.ShapeDtypeStruct(shape, dt)
    in_own_stage = lambda s, i: jnp.where(s == 0, i, NT - 1)
    grid_spec = pltpu.PrefetchScalarGridSpec(
        num_scalar_prefetch=1, grid=(3, NT),
        in_specs=[pl.BlockSpec((TT, D), lambda s, i, jm_ref: (in_own_stage(s, i), 0)),
                  pl.BlockSpec((TT, 1), lambda s, i, jm_ref: (i, 0)),
                  pl.BlockSpec((1, D), lambda s, i, jm_ref: (0, 0)), hbm, hbm],
        out_specs=[pl.BlockSpec((D, TT), lambda s, i, jm_ref: (0, in_own_stage(s, i))),
                   tok(AW, 0), tok(AW, 0), tok(AW, 1), d4(0), d4(0), d4(1), d16(0), d16(0), d16(1),
                   tok(AW, 1), tok(AW, 2), tok(AW, 2), tok(AW, 3), tok(AW, 3), hbm, hbm],
        scratch_shapes=[pltpu.VMEM((4, D, 1024), BF16), pltpu.VMEM((4, 256, D), BF16), pltpu.VMEM((T, D), BF16),
                        pltpu.VMEM((4, TT, 128), F32), pltpu.VMEM((4, TT, 128), F32), pltpu.VMEM((256, 1024), F32),
                        pltpu.SemaphoreType.DMA((18,)),
                        pltpu.SemaphoreType.DMA((18,)), pltpu.SemaphoreType.DMA((6,))])
    return pl.pallas_call(
        body, name="fwd_in", grid_spec=grid_spec,
        out_shape=[sd((D, T), BF16)] + [sd((T, AW), BF16)] * 3 + [sd((4, T // 4, AW), BF16)] * 3
        + [sd((16, T // 16, AW), BF16)] * 3
        + [sd((T, AW), BF16)] * 5 + [sd((D, NCOL), BF16), sd((4, 256, D), BF16)],
        compiler_params=_cp(("arbitrary", "arbitrary")),
    )(jm_arr, x, pos, mixw, w_in, w_out)


def _band_mask(key_axis, nkeys=2 * BLK):
    shape = (nkeys, 2 * BLK) if key_axis == 0 else (2 * BLK, nkeys)
    kj = lax.broadcasted_iota(jnp.int32, shape, key_axis)
    qi = lax.broadcasted_iota(jnp.int32, shape, 1 - key_axis) & (BLK - 1)
    return (kj >= qi) & (kj <= qi + BLK), kj, qi


def _stack_heads(t2, in_a):
    z = jnp.zeros_like(t2)
    return jnp.concatenate([jnp.where(in_a[0], t2, z), jnp.where(in_a[1], t2, z)], axis=0)


def _attn_fwd(q, k, v, nb, name):
    n = 8
    CH = n * BLK
    halo = nb > n

    def body(*refs):
        if halo:
            q_ref, k_ref, v_ref, kp_ref, vp_ref, o_ref, lse_ref = refs
        else:
            q_ref, k_ref, v_ref, o_ref, lse_ref = refs
        lane = lax.broadcasted_iota(jnp.int32, (1, 128), 1)
        in_a = [lane < HEAD, lane >= HEAD]
        band, kj, _ = _band_mask(1)
        thr0 = jnp.where((n * pl.program_id(0)) % nb == 0, BLK, 0) if halo else BLK
        mask0 = band & (kj >= thr0)
        mask_first = band & (kj >= BLK)
        for b in range(n):
            rs = slice(b * BLK, (b + 1) * BLK)
            stat = jnp.zeros((BLK, 128), F32)
            for hp in range(4):
                cs = slice(hp * 128, (hp + 1) * 128)
                q2s = _stack_heads(q_ref[rs, cs], in_a)
                if b == 0:
                    kprev = kp_ref[:, cs] if halo else k_ref[rs, cs]
                    vprev = vp_ref[:, cs] if halo else v_ref[rs, cs]
                    kk = jnp.concatenate([kprev, k_ref[rs, cs]], axis=0)
                    vv = jnp.concatenate([vprev, v_ref[rs, cs]], axis=0)
                    mask = mask0
                else:
                    kk = k_ref[(b - 1) * BLK:(b + 1) * BLK, cs]
                    vv = v_ref[(b - 1) * BLK:(b + 1) * BLK, cs]
                    mask = mask_first if b % nb == 0 else band
                s = jnp.where(mask, _mm_nt(q2s, kk) * SCALE, NEG)
                m = jnp.max(s, axis=-1, keepdims=True)
                p = jnp.exp(s - m)
                l = jnp.sum(p, axis=-1, keepdims=True)
                o = _mm(p.astype(BF16), vv) / l
                lse = m + jnp.log(l)
                o_ref[rs, cs] = jnp.where(in_a[0], o[:BLK], o[BLK:]).astype(BF16)
                stat = jnp.where(lane == 2 * hp, lse[:BLK], stat)
                stat = jnp.where(lane == 2 * hp + 1, lse[BLK:], stat)
            lse_ref[rs, :] = stat

    cur = pl.BlockSpec((CH, AW), lambda i: (i, 0))
    prev = pl.BlockSpec((BLK, AW), lambda i: (jnp.maximum(n * i - 1, 0), 0))
    return pl.pallas_call(
        body, name=name, grid=(T // CH,),
        in_specs=[cur, cur, cur] + ([prev, prev] if halo else []),
        out_specs=[cur, pl.BlockSpec((CH, 128), lambda i: (i, 0))],
        out_shape=[jax.ShapeDtypeStruct((T, AW), BF16), jax.ShapeDtypeStruct((T, 128), F32)],
        compiler_params=_cp(("parallel",)),
    )(*((q, k, v) + ((k, v) if halo else ())))


def _attn_bwd(q, k, v, do, st, nb, name):
    n = 8
    CH = n * BLK
    NBLK = T // BLK
    halo = nb > n

    def body(*refs):
        if halo:
            (q_ref, k_ref, v_ref, do_ref, st_ref, kp_ref, vp_ref, qn_ref, don_ref, stn_ref,
             dq_ref, dk_ref, dv_ref) = refs
        else:
            q_ref, k_ref, v_ref, do_ref, st_ref, dq_ref, dk_ref, dv_ref = refs
        i = pl.program_id(0)
        lane = lax.broadcasted_iota(jnp.int32, (1, 128), 1)
        in_a = [lane < HEAD, lane >= HEAD]
        band, kj, _ = _band_mask(0)
        thr0 = jnp.where((n * i) % nb == 0, BLK, 0) if halo else BLK
        mask0 = band & (kj >= thr0)
        mask_first = band & (kj >= BLK)

        def stat_rows(st_t, hp):
            lse_r = jnp.concatenate([st_t[2 * hp:2 * hp + 1, :], st_t[2 * hp + 1:2 * hp + 2, :]], axis=1)
            dl_r = jnp.concatenate([st_t[8 + 2 * hp:9 + 2 * hp, :], st_t[9 + 2 * hp:10 + 2 * hp, :]], axis=1)
            return lse_r, dl_r

        st_t = [st_ref[b * BLK:(b + 1) * BLK, :].T for b in range(n)]
        if halo:
            nxt_thr = jnp.where((n * i + n) % nb == 0, 2 * BLK, 0)
            _, kj1, qi1 = _band_mask(0, BLK)
            mask_next = kj1 >= qi1 + nxt_thr
            stn_t = stn_ref[...].T

        for hp in range(4):
            cs = slice(hp * 128, (hp + 1) * 128)
            kb = [k_ref[b * BLK:(b + 1) * BLK, cs] for b in range(n)]
            vb = [v_ref[b * BLK:(b + 1) * BLK, cs] for b in range(n)]
            dk_acc = [jnp.zeros((BLK, 128), F32) for _ in range(n)]
            dv_acc = [jnp.zeros((BLK, 128), F32) for _ in range(n)]
            for b in range(n):
                rs = slice(b * BLK, (b + 1) * BLK)
                q2s = _stack_heads(q_ref[rs, cs], in_a)
                do2s = _stack_heads(do_ref[rs, cs], in_a)
                if b == 0:
                    kprev = kp_ref[:, cs] if halo else kb[0]
                    vprev = vp_ref[:, cs] if halo else vb[0]
                    mask = mask0
                else:
                    kprev, vprev, mask = kb[b - 1], vb[b - 1], (mask_first if b % nb == 0 else band)
                kk = jnp.concatenate([kprev, kb[b]], axis=0)
                vv = jnp.concatenate([vprev, vb[b]], axis=0)
                lse_r, dl_r = stat_rows(st_t[b], hp)
                s_t = jnp.where(mask, _mm_nt(kk, q2s) * SCALE, NEG)
                p_t = jnp.exp(s_t - lse_r)
                ds_t = (p_t * (_mm_nt(vv, do2s) - dl_r)).astype(BF16)
                dkk = _mm(ds_t, q2s) * SCALE
                dvv = _mm(p_t.astype(BF16), do2s)
                dqs = _mm_tn(ds_t, kk) * SCALE
                dq_ref[rs, cs] = jnp.where(in_a[0], dqs[:BLK], dqs[BLK:]).astype(BF16)
                dk_acc[b] += dkk[BLK:]
                dv_acc[b] += dvv[BLK:]
                if b > 0:
                    dk_acc[b - 1] += dkk[:BLK]
                    dv_acc[b - 1] += dvv[:BLK]
            if halo:
                q2s = _stack_heads(qn_ref[:, cs], in_a)
                do2s = _stack_heads(don_ref[:, cs], in_a)
                lse_r, dl_r = stat_rows(stn_t, hp)
                s_t = jnp.where(mask_next, _mm_nt(kb[n - 1], q2s) * SCALE, NEG)
                p_t = jnp.exp(s_t - lse_r)
                ds_t = (p_t * (_mm_nt(vb[n - 1], do2s) - dl_r)).astype(BF16)
                dk_acc[n - 1] += _mm(ds_t, q2s) * SCALE
                dv_acc[n - 1] += _mm(p_t.astype(BF16), do2s)
            for b in range(n):
                dk_ref[b * BLK:(b + 1) * BLK, cs] = dk_acc[b].astype(BF16)
                dv_ref[b * BLK:(b + 1) * BLK, cs] = dv_acc[b].astype(BF16)

    cur = pl.BlockSpec((CH, AW), lambda i: (i, 0))
    cur_st = pl.BlockSpec((CH, 128), lambda i: (i, 0))
    prev = pl.BlockSpec((BLK, AW), lambda i: (jnp.maximum(n * i - 1, 0), 0))
    nxt = pl.BlockSpec((BLK, AW), lambda i: (jnp.minimum(n * i + n, NBLK - 1), 0))
    nxt_st = pl.BlockSpec((BLK, 128), lambda i: (jnp.minimum(n * i + n, NBLK - 1), 0))
    ins = [cur] * 4 + [cur_st] + ([prev, prev, nxt, nxt, nxt_st] if halo else [])
    args = (q, k, v, do, st) + ((k, v, q, do, st) if halo else ())
    return pl.pallas_call(
        body, name=name, grid=(T // CH,),
        in_specs=ins,
        out_specs=[cur] * 3,
        out_shape=[jax.ShapeDtypeStruct((T, AW), BF16)] * 3,
        compiler_params=_cp(("parallel",)),
    )(*args)


TH = 256
NCH = TH // CHUNK


def _hgrn_common(hq_ref, hf_ref, lbr_ref, tri_ref):
    r0 = lbr_ref[0:1, :]
    r1 = lbr_ref[1:2, :]
    mx = jnp.maximum(r0, r1)
    e0 = jnp.exp(r0 - mx)
    e1 = jnp.exp(r1 - mx)
    lb = e0 / (e0 + e1)
    hqv = hq_ref[...].astype(F32)
    sq = _sigmoid(hqv)
    qv = hqv * sq
    sf = _sigmoid(hf_ref[...].astype(F32))
    f = lb + (1.0 - lb) * sf
    kv = 1.0 - f
    g = jnp.log(f)
    cum = _mm_exact_l(tri_ref[...], g)
    dec = jnp.exp(jnp.concatenate([cum[c * CHUNK + CHUNK - 1:(c + 1) * CHUNK, :] for c in range(NCH)], axis=0))
    decb = jnp.concatenate([jnp.broadcast_to(dec[c:c + 1, :], (CHUNK, HW)) for c in range(NCH)], axis=0)
    ea = jnp.exp(cum)
    ena = jnp.exp(-cum)
    eend = decb * ena
    return dict(lb=lb, hq=hqv, sq=sq, q=qv, sf=sf, f=f, k=kv, cum=cum, ea=ea, ena=ena, eend=eend,
                qd=qv * ea, ki=kv * ena, ke=kv * eend, dec=dec)


def _tri_mask(transposed=False):
    ti = lax.broadcasted_iota(jnp.int32, (TH, TH), 1 if transposed else 0)
    si = lax.broadcasted_iota(jnp.int32, (TH, TH), 0 if transposed else 1)
    return (si <= ti) & ((si // CHUNK) == (ti // CHUNK))


def _hgrn_fwd(hq, hf, hi, lbr, tri):
    def body(hq_ref, hf_ref, hi_ref, lbr_ref, tri_ref, rec_ref, sall_ref, st_scr):
        @pl.when(pl.program_id(0) == 0)
        def _():
            st_scr[...] = jnp.zeros_like(st_scr)

        w = _hgrn_common(hq_ref, hf_ref, lbr_ref, tri_ref)
        qd, ki, ke = w["qd"].astype(BF16), w["ki"].astype(BF16), w["ke"].astype(BF16)
        dec = w["dec"]
        vb = hi_ref[...]
        causal = _tri_mask()
        for h in range(4):
            cs = slice(h * 128, (h + 1) * 128)
            att = jnp.where(causal, _mm_nt(qd[:, cs], ki[:, cs]), 0.0)
            o_intra = _mm(att.astype(BF16), vb[:, cs])
            st = st_scr[:, cs]
            for c in range(NCH):
                rs = slice(c * CHUNK, (c + 1) * CHUNK)
                sall_ref[c, :, cs] = st
                rec_ref[rs, cs] = (o_intra[rs] + _mm_nt(qd[rs, cs], st.astype(BF16))).astype(BF16)
                st = dec[c:c + 1, cs] * st + _mm_tn(vb[rs, cs], ke[rs, cs])
            st_scr[:, cs] = st

    tok = pl.BlockSpec((TH, HW), lambda i: (i, 0))
    return pl.pallas_call(
        body, name="hgrn_fwd", grid=(T // TH,),
        in_specs=[tok, tok, tok, pl.BlockSpec((2, HW), lambda i: (0, 0)), pl.BlockSpec((TH, TH), lambda i: (0, 0))],
        out_specs=[tok, pl.BlockSpec((NCH, 128, HW), lambda i: (i, 0, 0))],
        out_shape=[jax.ShapeDtypeStruct((T, HW), BF16), jax.ShapeDtypeStruct((T // CHUNK, 128, HW), F32)],
        scratch_shapes=[pltpu.VMEM((128, HW), F32)],
        compiler_params=_cp(("arbitrary",)),
    )(hq, hf, hi, lbr, tri)


def _hgrn_bwd(hq, hf, hi, lbr, tri, trit, drec, sall, dhg, rout, routb):
    NT = T // TH

    def body(hq_ref, hf_ref, hi_ref, lbr_ref, tri_ref, trit_ref, do_ref, sall_ref, dhg_ref, rout_r, routb_r,
             dph_ref, small_ref, pout_o, poutr_o,
             dst_scr, dlb_scr, dqd_scr, dki_scr, dke_scr, dlast_scr, send_sems, recv_sems, loc_sems):
        step = pl.program_id(0)
        loc, rem = _chip_copies(_w_out_piece, rout_r, routb_r, pout_o, poutr_o, send_sems, recv_sems,
                                loc_sems.at[0])

        @pl.when(step == 0)
        def _():
            dst_scr[...] = jnp.zeros_like(dst_scr)
            dlb_scr[...] = jnp.zeros_like(dlb_scr)
            for cp in loc + rem:
                cp.start()

        w = _hgrn_common(hq_ref, hf_ref, lbr_ref, tri_ref)
        qd, ki, ke = w["qd"].astype(BF16), w["ki"].astype(BF16), w["ke"].astype(BF16)
        dec = w["dec"]
        vb = hi_ref[...]
        dob = do_ref[...].astype(BF16)
        causal = _tri_mask()
        causal_t = _tri_mask(transposed=True)
        for h in range(4):
            cs = slice(h * 128, (h + 1) * 128)
            att_t = jnp.where(causal_t, _mm_nt(ki[:, cs], qd[:, cs]), 0.0).astype(BF16)
            datt_t = jnp.where(causal_t, _mm_nt(vb[:, cs], dob[:, cs]), 0.0).astype(BF16)
            datt = jnp.where(causal, _mm_nt(dob[:, cs], vb[:, cs]), 0.0).astype(BF16)
            dv_intra = _mm(att_t, dob[:, cs])
            dqd_intra = _mm(datt, ki[:, cs])
            dki_scr[:, cs] = _mm(datt_t, qd[:, cs])
            dst = dst_scr[:, cs]
            for c in reversed(range(NCH)):
                rs = slice(c * CHUNK, (c + 1) * CHUNK)
                dec_c = dec[c:c + 1, :]
                st = sall_ref[c, :, cs]
                dstb = dst.astype(BF16)
                dph_ref[rs, 2 * HW + h * 128:2 * HW + (h + 1) * 128] = (
                    dv_intra[rs] + _mm_nt(ke[rs, cs], dstb)).astype(BF16)
                dqd_scr[rs, cs] = dqd_intra[rs] + _mm(dob[rs, cs], st.astype(BF16))
                dke_scr[rs, cs] = _mm(vb[rs, cs], dstb)
                ddec = jnp.sum(dst * st, axis=0, keepdims=True)
                dlast_scr[c:c + 1, cs] = ddec * dec_c[:, cs]
                dst = dec_c[:, cs] * dst + _mm_tn(dob[rs, cs], qd[rs, cs])
            dst_scr[:, cs] = dst
        dqd, dki, dke = dqd_scr[...], dki_scr[...], dke_scr[...]
        dq = dqd * w["ea"]
        dk = dki * w["ena"] + dke * w["eend"]
        dcum = dqd * w["qd"] - dki * w["ki"] - dke * w["ke"]
        dkeke = dke * w["ke"]
        dlastb = jnp.concatenate(
            [jnp.broadcast_to(dlast_scr[c:c + 1, :] + jnp.sum(dkeke[c * CHUNK:(c + 1) * CHUNK], axis=0, keepdims=True),
                              (CHUNK, HW)) for c in range(NCH)], axis=0)
        dg = _mm_exact_l(trit_ref[...], dcum) + dlastb
        df = dg / w["f"] - dk
        lb, sf, sq = w["lb"], w["sf"], w["sq"]
        dph_ref[:, HW:2 * HW] = (df * (1.0 - lb) * sf * (1.0 - sf)).astype(BF16)
        dph_ref[:, 0:HW] = (dq * (sq * (1.0 + w["hq"] * (1.0 - sq)))).astype(BF16)
        dph_ref[:, 3 * HW:4 * HW] = dhg_ref[...]
        dlb_scr[...] += jnp.sum(df * (1.0 - sf), axis=0, keepdims=True)

        @pl.when(step == NT - 1)
        def _():
            gr = dlb_scr[...] * lb * (1.0 - lb)
            small_ref[...] = jnp.zeros_like(small_ref)
            small_ref[0:1, 0:HW] = gr
            small_ref[1:2, 0:HW] = -gr
            for cp in rem:
                cp.wait_recv()
            for cp in rem:
                cp.wait_send()
            for cp in loc:
                cp.wait()

    tok = pl.BlockSpec((TH, HW), lambda i: (NT - 1 - i, 0))
    const = lambda shape: pl.BlockSpec(shape, lambda i: (0,) * len(shape))
    hbm = pl.BlockSpec(memory_space=pltpu.HBM)
    return pl.pallas_call(
        body, name="hgrn_bwd", grid=(NT,),
        in_specs=[tok, tok, tok, const((2, HW)), const((TH, TH)), const((TH, TH)), tok,
                  pl.BlockSpec((NCH, 128, HW), lambda i: (NT - 1 - i, 0, 0)), tok, hbm, hbm],
        out_specs=[pl.BlockSpec((TH, NCOL // 2), lambda i: (NT - 1 - i, 0)), const((8, D)), hbm, hbm],
        out_shape=[jax.ShapeDtypeStruct((T, NCOL // 2), BF16), jax.ShapeDtypeStruct((8, D), F32),
                   jax.ShapeDtypeStruct((128, D), F32), jax.ShapeDtypeStruct((3, 128, D), BF16)],
        scratch_shapes=[pltpu.VMEM((128, HW), F32), pltpu.VMEM((1, HW), F32), pltpu.VMEM((TH, HW), F32),
                        pltpu.VMEM((TH, HW), F32), pltpu.VMEM((TH, HW), F32), pltpu.VMEM((8, HW), F32),
                        pltpu.SemaphoreType.DMA((3,)), pltpu.SemaphoreType.DMA((3,)), pltpu.SemaphoreType.DMA((1,))],
        compiler_params=_cp(("arbitrary",)),
    )(hq, hf, hi, lbr, tri, trit, drec, sall, dhg, rout, routb)


def _fwd_out(o1, o4, o16, l1, l4, l16, rec, ag, hg, x, tgt, anw, hnw, fnw, wout_full, gmat, emat, selmat):
    TT = 512

    def body(o1_r, o4_r, o16_r, l1_r, l4_r, l16_r, rec_r, ag_r, hg_r, x_r, tgt_r, anw_r, hnw_r, fnw_r, wo_r, g_r,
             e_r, sel_r, dx2_o, do1_o, do4_o, do16_o, st1_o, st4_o, st16_o, drec_o, dag_o, dhg_o,
             rout_o, routb_o, small_o, scr_a, scr_b, scr_c, gwout_o, rbuf, send_sems, recv_sems):
        @pl.when(pl.program_id(0) == 0)
        def _():
            gwout_o[...] = jnp.zeros_like(gwout_o)
            small_o[...] = jnp.zeros_like(small_o)

        def unperm(r4, r16):
            return _unperm_load(r4, r16, scr_a, scr_b, scr_c)

        def perm_out(val, p1, p4, p16, dt):
            _perm_store(val, scr_a, scr_b, p1, p4, p16, dt)

        o4u, o16u = unperm(o4_r, o16_r)
        l4c, l16c = unperm(l4_r, l16_r)
        l1c = l1_r[...]
        mxc = jnp.maximum(jnp.maximum(l1c, l4c), l16c)
        w1c, w4c, w16c = jnp.exp(l1c - mxc), jnp.exp(l4c - mxc), jnp.exp(l16c - mxc)
        denc = w1c + w4c + w16c
        lane = lax.broadcasted_iota(jnp.int32, (1, 128), 1)
        lse_c = jnp.where(lane < 8, mxc + jnp.log(denc), 0.0)
        em = e_r[...]
        wn1 = _mm_exact_r(w1c / denc, em)
        wn4 = _mm_exact_r(w4c / denc, em)
        o1v = o1_r[...].astype(F32)
        attn = wn1 * o1v + wn4 * o4u + (1.0 - wn1 - wn4) * o16u
        gm = g_r[...]

        def head_mean_a(t):
            return jnp.concatenate([_mm_exact_r(t[:, :256], gm), _mm_exact_r(t[:, 256:], gm)], axis=1)

        def head_mean_h(t):
            return jnp.concatenate(
                [jnp.broadcast_to(jnp.mean(t[:, h * 128:(h + 1) * 128], axis=-1, keepdims=True), (TT, 128))
                 for h in range(4)], axis=1)

        rs_a = lax.rsqrt(head_mean_a(attn * attn) + EPS)
        n_a = attn * rs_a
        agv = ag_r[...].astype(F32)
        sg_a = _sigmoid(agv)
        si_a = agv * sg_a
        anw_v = anw_r[...]
        y_a = (n_a * anw_v) * si_a
        recv = rec_r[...].astype(F32)
        rs_h = lax.rsqrt(head_mean_h(recv * recv) + EPS)
        n_h = recv * rs_h
        hgv = hg_r[...].astype(F32)
        sg_h = _sigmoid(hgv)
        si_h = hgv * sg_h
        hnw_v = hnw_r[...]
        y_h = (n_h * hnw_v) * si_h
        mixed = jnp.concatenate([y_a, y_h], axis=1).astype(BF16)
        xv = x_r[...]
        x2 = xv + _mm(mixed, wo_r[...])
        r2 = lax.rsqrt(jnp.mean(x2 * x2, axis=-1, keepdims=True) + EPS)
        fnw_v = fnw_r[...]
        xn = x2 * r2
        err = xn * fnw_v - tgt_r[...]
        small_o[2:3, :] += 0.5 * jnp.sum(jnp.mean(err * err, axis=-1, keepdims=True), axis=0, keepdims=True)
        dy = err * (1.0 / D)
        small_o[0:1, :] += jnp.sum(dy * xn, axis=0, keepdims=True)
        dyw = dy * fnw_v
        dx2 = r2 * dyw - x2 * ((r2 * r2 * r2) * jnp.mean(dyw * x2, axis=-1, keepdims=True))
        dx2_o[...] = dx2
        dx2b = dx2.astype(BF16)
        gwout_o[...] += _mm_tn(mixed, dx2b)
        dmix = _mm_nt(dx2b, wo_r[...])
        dm_a, dm_h = dmix[:, :AW], dmix[:, AW:]
        dag_o[...] = (dm_a * (n_a * anw_v) * (sg_a * (1.0 + agv * (1.0 - sg_a)))).astype(BF16)
        dn_a = dm_a * anw_v * si_a
        small_o[1:2, 0:AW] += jnp.sum(dm_a * n_a * si_a, axis=0, keepdims=True)
        dattn = rs_a * (dn_a - n_a * head_mean_a(dn_a * n_a))
        perm_out(dattn, do1_o, do4_o, do16_o, BF16)
        stats = lse_c + _mm_exact_r(dattn * attn, sel_r[...])
        perm_out(stats, st1_o, st4_o, st16_o, F32)
        dhg_o[...] = (dm_h * (n_h * hnw_v) * (sg_h * (1.0 + hgv * (1.0 - sg_h)))).astype(BF16)
        dn_h = dm_h * hnw_v * si_h
        small_o[1:2, AW:] += jnp.sum(dm_h * n_h * si_h, axis=0, keepdims=True)
        drec_o[...] = (rs_h * (dn_h - n_h * head_mean_h(dn_h * n_h))).astype(BF16)

        @pl.when(pl.program_id(0) == T // TT - 1)
        def _():
            x, y, c = lax.axis_index("x"), lax.axis_index("y"), lax.axis_index("c")
            cps = [pltpu.make_async_remote_copy(
                src_ref=gwout_o.at[pl.ds(pl.multiple_of(j * 256 + (1 - c) * 128, 128), 128), :], dst_ref=rbuf.at[j],
                send_sem=send_sems.at[j], recv_sem=recv_sems.at[j], device_id=(x, y, 1 - c), device_id_type=MESH)
                for j in range(4)]
            for cp in cps:
                cp.start()
            for j, cp in enumerate(cps):
                cp.wait_recv()
                red = gwout_o[pl.ds(pl.multiple_of(j * 256 + c * 128, 128), 128), :] + rbuf[j]
                rout_o[j * 128:(j + 1) * 128, :] = red
                routb_o[j * 128:(j + 1) * 128, :] = red.astype(BF16)
            for cp in cps:
                cp.wait_send()

    tok = lambda w: pl.BlockSpec((TT, w), lambda i: (i, 0))
    d4 = pl.BlockSpec((4, TT // 4, AW), lambda i: (0, i, 0))
    d16 = pl.BlockSpec((16, TT // 16, AW), lambda i: (0, i, 0))
    const = lambda shape: pl.BlockSpec(shape, lambda i: (0,) * len(shape))
    sd = lambda shape, dt: jax.ShapeDtypeStruct(shape, dt)
    c4 = pl.BlockSpec((4, TT // 4, 128), lambda i: (0, i, 0))
    c16 = pl.BlockSpec((16, TT // 16, 128), lambda i: (0, i, 0))
    p3 = lambda w, dt: [sd((T, w), dt), sd((4, T // 4, w), dt), sd((16, T // 16, w), dt)]
    return pl.pallas_call(
        body, name="fwd_out", grid=(T // TT,),
        in_specs=[tok(AW), d4, d16, tok(128), c4, c16, tok(AW), tok(AW), tok(AW), tok(D), tok(D),
                  const((1, AW)), const((1, HW)), const((1, D)), const((D, D)), const((256, 256)),
                  const((128, AW)), const((AW, 128))],
        out_specs=[tok(D)] + [tok(AW), d4, d16] + [tok(128), c4, c16] + [tok(AW)] * 3
        + [const((512, D)), const((512, D)), const((8, D))],
        out_shape=[sd((T, D), F32)] + p3(AW, BF16) + p3(128, F32)
        + [sd((T, AW), BF16), sd((T, AW), BF16), sd((T, AW), BF16), sd((512, D), F32), sd((512, D), BF16),
           sd((8, D), F32)],
        scratch_shapes=[pltpu.VMEM((4, TT, 128), F32)] * 3 + [pltpu.VMEM((D, D), F32),
                        pltpu.VMEM((4, 128, D), F32), pltpu.SemaphoreType.DMA((4,)), pltpu.SemaphoreType.DMA((4,))],
        compiler_params=_cp(("arbitrary",)),
    )(o1, o4, o16, l1, l4, l16, rec, ag, hg, x, tgt, anw, hnw, fnw, wout_full, gmat, emat, selmat)


def _dproj_build(dq, dk, dv, dag, pos):
    TT = 512

    def body(dq1, dq4, dq16, dk1, dk4, dk16, dv1, dv4, dv16, dag_r, pos_r, dproj_o, scr_a, scr_b, scr_c):
        def unperm_sum(r1, r4, r16):
            u4, u16 = _unperm_load(r4, r16, scr_a, scr_b, scr_c)
            return r1[...] + u4 + u16

        cosf, s1, s2 = _rope_tables(pos_r[...])
        dproj_o[:, 0:512] = _rope_bwd(unperm_sum(dq1, dq4, dq16), cosf, s1, s2).astype(BF16)
        dproj_o[:, 512:1024] = _rope_bwd(unperm_sum(dk1, dk4, dk16), cosf, s1, s2).astype(BF16)
        dproj_o[:, 1024:1536] = unperm_sum(dv1, dv4, dv16).astype(BF16)
        dproj_o[:, 1536:2048] = dag_r[...]

    tok = lambda w: pl.BlockSpec((TT, w), lambda i: (i, 0))
    d4 = pl.BlockSpec((4, TT // 4, AW), lambda i: (0, i, 0))
    d16 = pl.BlockSpec((16, TT // 16, AW), lambda i: (0, i, 0))
    return pl.pallas_call(
        body, name="dproj_build", grid=(T // TT,),
        in_specs=[tok(AW), d4, d16] * 3 + [tok(AW), tok(1)],
        out_specs=tok(NCOL // 2),
        out_shape=jax.ShapeDtypeStruct((T, NCOL // 2), BF16),
        scratch_shapes=[pltpu.VMEM((4, TT, 128), F32)] * 3,
        compiler_params=_cp(("parallel",)),
    )(*dq, *dk, *dv, dag, pos)


def _bwd_x(dproj_a, dproj_h, x, dx2, mixw, w_full, rin, rinb, small4, small6, pout_own, pout_rem):
    TT = 256
    NT = T // TT

    def body(dpa_r, dph_r, x_r, dx2_r, mw_r, w_r, rin_r, rinb_r, s4_r, s6_r, poo_r, por_r,
             gx_o, pin_o, pinr_o, sall_o, fin_o, fout_o, sbuf, v_own, v_rem, vo_own, vo_rem, sin, sout, got_in,
             got_out, send_sems, recv_sems, loc_sems, share_send, share_recv, fin_sems):
        i = pl.program_id(0)
        loc, rem = _chip_copies(_w_in_piece, rin_r, rinb_r, pin_o, pinr_o, send_sems, recv_sems, loc_sems.at[0])

        @pl.when(i == 0)
        def _():
            sbuf[...] = jnp.zeros_like(sbuf)
            for cp in loc + rem:
                cp.start()

        dhn = _mm_nt(dpa_r[...], w_r[:, 0:NCOL // 2]) + _mm_nt(dph_r[...], w_r[:, NCOL // 2:NCOL])
        xv = x_r[...]
        r = lax.rsqrt(jnp.mean(xv * xv, axis=-1, keepdims=True) + EPS)
        dxw = dhn * mw_r[...]
        gx_o[...] = dx2_r[...] + r * dxw - xv * ((r * r * r) * jnp.mean(dxw * xv, axis=-1, keepdims=True))
        sbuf[16:17, :] += jnp.sum(dhn * (xv * r), axis=0, keepdims=True)

        @pl.when(i == NT - 1)
        def _():
            sbuf[0:8, :] = s4_r[...]
            sbuf[8:16, :] = s6_r[...]
            sloc, srem = _small_copies(sbuf, sall_o, send_sems, recv_sems, loc_sems.at[1])
            for cp in sloc + srem:
                cp.start()
            for cp in rem:
                cp.wait_recv()
            for cp in rem:
                cp.wait_send()
            for cp in loc:
                cp.wait()
            mx, my, c = lax.axis_index("x"), lax.axis_index("y"), lax.axis_index("c")
            loads = [pltpu.make_async_copy(pin_o, v_own, fin_sems.at[0]),
                     pltpu.make_async_copy(pinr_o, v_rem, fin_sems.at[1]),
                     pltpu.make_async_copy(poo_r, vo_own, fin_sems.at[2]),
                     pltpu.make_async_copy(por_r, vo_rem, fin_sems.at[3])]
            for cp in loads:
                cp.start()
            for cp in loads:
                cp.wait()
            sout[...] = ((vo_own[...] + vo_rem[0].astype(F32)) + vo_rem[1].astype(F32)) + vo_rem[2].astype(F32)
            sin[...] = ((v_own[...] + v_rem[0].astype(F32)) + v_rem[1].astype(F32)) + v_rem[2].astype(F32)
            swap = [pltpu.make_async_remote_copy(src_ref=sin, dst_ref=got_in, send_sem=share_send.at[0],
                                                 recv_sem=share_recv.at[0], device_id=(mx, my, 1 - c),
                                                 device_id_type=MESH),
                    pltpu.make_async_remote_copy(src_ref=sout, dst_ref=got_out, send_sem=share_send.at[1],
                                                 recv_sem=share_recv.at[1], device_id=(mx, my, 1 - c),
                                                 device_id_type=MESH)]
            for cp in swap:
                cp.start()
            mine = [pltpu.make_async_copy(sin, fin_o.at[c], fin_sems.at[0]),
                    pltpu.make_async_copy(sout, fout_o.at[c], fin_sems.at[1])]
            for cp in mine:
                cp.start()
            for cp in swap:
                cp.wait_recv()
            theirs = [pltpu.make_async_copy(got_in, fin_o.at[1 - c], fin_sems.at[2]),
                      pltpu.make_async_copy(got_out, fout_o.at[1 - c], fin_sems.at[3])]
            for cp in theirs:
                cp.start()
            for cp in swap:
                cp.wait_send()
            for cp in mine + theirs:
                cp.wait()
            for cp in srem:
                cp.wait_recv()
            for cp in srem:
                cp.wait_send()
            for cp in sloc:
                cp.wait()

    tok = lambda w: pl.BlockSpec((TT, w), lambda i: (i, 0))
    const = lambda shape: pl.BlockSpec(shape, lambda i: (0,) * len(shape))
    hbm = pl.BlockSpec(memory_space=pltpu.HBM)
    return pl.pallas_call(
        body, name="bwd_x", grid=(NT,),
        in_specs=[tok(NCOL // 2), tok(NCOL // 2), tok(D), tok(D), const((1, D)), const((D, NCOL)), hbm, hbm,
                  const((8, D)), const((8, D)), hbm, hbm],
        out_specs=[tok(D), hbm, hbm, hbm, hbm, hbm],
        out_shape=[jax.ShapeDtypeStruct((T, D), F32),
                   jax.ShapeDtypeStruct((512, 1024), F32), jax.ShapeDtypeStruct((3, 512, 1024), BF16),
                   jax.ShapeDtypeStruct((8, 24, D), F32),
                   jax.ShapeDtypeStruct((2, 512, 1024), F32), jax.ShapeDtypeStruct((2, 128, D), F32)],
        scratch_shapes=[pltpu.VMEM((24, D), F32),
                        pltpu.VMEM((512, 1024), F32), pltpu.VMEM((3, 512, 1024), BF16),
                        pltpu.VMEM((128, D), F32), pltpu.VMEM((3, 128, D), BF16),
                        pltpu.VMEM((512, 1024), F32), pltpu.VMEM((128, D), F32),
                        pltpu.VMEM((512, 1024), F32), pltpu.VMEM((128, D), F32),
                        pltpu.SemaphoreType.DMA((10,)), pltpu.SemaphoreType.DMA((10,)), pltpu.SemaphoreType.DMA((2,)),
                        pltpu.SemaphoreType.DMA((2,)), pltpu.SemaphoreType.DMA((2,)), pltpu.SemaphoreType.DMA((4,))],
        compiler_params=_cp(("arbitrary",)),
    )(dproj_a, dproj_h, x, dx2, mixw, w_full, rin, rinb, small4, small6, pout_own, pout_rem)


def _grad_w_in(hn, dproj_a, dproj_h):
    TK = 2048
    NK = T // TK

    def body(hnt_r, dpa_r, dph_r, rin_o, rinb_o, acc, rbuf, obuf, obufb, send_sems, recv_sems, wb_sems):
        j = pl.program_id(0)
        kk = pl.program_id(1)
        x, y, c = lax.axis_index("x"), lax.axis_index("y"), lax.axis_index("c")
        mine = pl.ds(pl.multiple_of(c * 512, 512), 512)
        theirs = pl.ds(pl.multiple_of((1 - c) * 512, 512), 512)

        def send(jj):
            return pltpu.make_async_remote_copy(
                src_ref=acc.at[jj % 2, theirs, :], dst_ref=rbuf.at[jj], send_sem=send_sems.at[jj],
                recv_sem=recv_sems.at[jj], device_id=(x, y, 1 - c), device_id_type=MESH)

        def writeback(jj):
            cols = pl.ds(jj * 1024, 1024)
            return [pltpu.make_async_copy(obuf.at[jj % 2], rin_o.at[:, cols], wb_sems.at[jj % 2]),
                    pltpu.make_async_copy(obufb.at[jj % 2], rinb_o.at[:, cols], wb_sems.at[2 + jj % 2])]

        def wait_writeback(jj):
            for cp in writeback(jj):
                cp.wait()

        def finalize(jj):
            send(jj).wait_recv()
            red = acc[jj % 2, mine, :] + rbuf[jj]
            obuf[jj % 2] = red
            obufb[jj % 2] = red.astype(BF16)
            for cp in writeback(jj):
                cp.start()

        prod = _mm(hnt_r[...], jnp.where(j < 2, dpa_r[...], dph_r[...]))

        @pl.when(kk == 0)
        def _():
            for jj in (2, 3):
                @pl.when(j == jj)
                def _():
                    send(jj - 2).wait_send()
            acc[j % 2] = prod

        @pl.when(kk > 0)
        def _():
            acc[j % 2] += prod

        @pl.when(kk == NK - 1)
        def _():
            for jj in range(4):
                @pl.when(j == jj)
                def _():
                    send(jj).start()
                    if jj in (1, 2):
                        finalize(jj - 1)
                    if jj == 3:
                        wait_writeback(0)
                        finalize(2)
                        wait_writeback(1)
                        finalize(3)
                        wait_writeback(2)
                        wait_writeback(3)
                        send(2).wait_send()
                        send(3).wait_send()

    hbm = pl.BlockSpec(memory_space=pltpu.HBM)
    return pl.pallas_call(
        body, name="grad_w_in", grid=(4, NK),
        in_specs=[pl.BlockSpec((D, TK), lambda j, kk: (0, kk)),
                  pl.BlockSpec((TK, 1024), lambda j, kk: (jnp.where(j < 2, kk, NK - 1), jnp.minimum(j, 1))),
                  pl.BlockSpec((TK, 1024), lambda j, kk: (jnp.where(j < 2, 0, kk), jnp.maximum(j - 2, 0)))],
        out_specs=[hbm, hbm],
        out_shape=[jax.ShapeDtypeStruct((512, NCOL), F32), jax.ShapeDtypeStruct((512, NCOL), BF16)],
        scratch_shapes=[pltpu.VMEM((2, D, 1024), F32), pltpu.VMEM((4, 512, 1024), F32), pltpu.VMEM((2, 512, 1024), F32),
                        pltpu.VMEM((2, 512, 1024), BF16),
                        pltpu.SemaphoreType.DMA((4,)), pltpu.SemaphoreType.DMA((4,)), pltpu.SemaphoreType.DMA((4,))],
        compiler_params=_cp(("arbitrary", "arbitrary")),
    )(hn, dproj_a, dproj_h)


def _w_in_piece(ref, j):
    return ref.at[:, pl.ds(j * 1024, 1024)]


def _w_out_piece(ref, j):
    return ref.at[pl.ds(j * 128, 128), :]


def _chip_copies(piece, src_r, srcb_r, own_o, rem_o, send_sems, recv_sems, loc_sem):
    x, y, c = lax.axis_index("x"), lax.axis_index("y"), lax.axis_index("c")
    chips = [(1 - x, y), (x, 1 - y), (1 - x, 1 - y)]
    loc = [pltpu.make_async_copy(piece(src_r, 2 * x + y), own_o, loc_sem)]
    rem = [pltpu.make_async_remote_copy(
        src_ref=piece(srcb_r, 2 * px + py), dst_ref=rem_o.at[k], send_sem=send_sems.at[k],
        recv_sem=recv_sems.at[k], device_id=(px, py, c), device_id_type=MESH) for k, (px, py) in enumerate(chips)]
    return loc, rem


def _small_copies(small_r, sall_o, send_sems, recv_sems, loc_sem):
    x, y, c = lax.axis_index("x"), lax.axis_index("y"), lax.axis_index("c")
    me = 4 * x + 2 * y + c
    loc = [pltpu.make_async_copy(small_r, sall_o.at[me], loc_sem)]
    rem = []
    k = 3
    for fx in range(2):
        for fy in range(2):
            for fc in range(2):
                if fx or fy or fc:
                    peer = (1 - x if fx else x, 1 - y if fy else y, 1 - c if fc else c)
                    rem.append(pltpu.make_async_remote_copy(
                        src_ref=small_r, dst_ref=sall_o.at[me], send_sem=send_sems.at[k],
                        recv_sem=recv_sems.at[k], device_id=peer, device_id_type=MESH))
                    k += 1
    return loc, rem


def _adamw_math(w, g, m, v):
    m = B1 * m + (1.0 - B1) * g
    v = B2 * v + (1.0 - B2) * (g * g)
    m_hat = m / (1.0 - B1 ** STEP)
    v_hat = v / (1.0 - B2 ** STEP)
    delta = -LR * (m_hat / (jnp.sqrt(v_hat) + AEPS) + WD * w)
    return delta, m, v


def _adamw(big_in, big_out, sall, params):
    def body(*refs):
        wi, gi, mi, vi, wo, go, mo, vo, sall_r = refs[:9]
        ins = refs[9:24]
        di_o, mi_o, vi_o, do_o, mo_o, vo_o = refs[24:30]
        outs = refs[30:]
        d, mm, vv = _adamw_math(wi[...], gi[...], mi[...], vi[...])
        di_o[...] = d
        mi_o[...] = mm
        vi_o[...] = vv

        @pl.when(pl.program_id(0) == 0)
        def _():
            d, mm, vv = _adamw_math(wo[...], go[...], mo[...], vo[...])
            do_o[...] = d
            mo_o[...] = mm
            vo_o[...] = vv
            tot = sall_r[0]
            for dv in range(1, 8):
                tot = tot + sall_r[dv]
            grads = [tot[16:17, :], tot[1:2, 0:AW], tot[1:2, AW:], tot[8:10, 0:HW], tot[0:1, :]]
            outs[0][...] = tot[2:3, 0:1]
            for p in range(5):
                w_r, m_r, v_r = ins[3 * p:3 * p + 3]
                g = grads[p]
                d, mm, vv = _adamw_math(w_r[...], g, m_r[...], v_r[...])
                outs[1 + 4 * p][...] = g
                outs[2 + 4 * p][...] = d
                outs[3 + 4 * p][...] = mm
                outs[4 + 4 * p][...] = vv

    flat = [a for p in params for a in p]
    shapes = [jax.ShapeDtypeStruct((D, 1024), F32)] * 3 + [jax.ShapeDtypeStruct((256, D), F32)] * 3
    shapes += [jax.ShapeDtypeStruct((1, 1), F32)]
    for p in params:
        shapes += [jax.ShapeDtypeStruct(p[0].shape, F32)] * 4
    vm = pl.BlockSpec(memory_space=pltpu.VMEM)
    rows = pl.BlockSpec((256, 1024), lambda i: (i, 0))
    whole = pl.BlockSpec((256, D), lambda i: (0, 0))
    return pl.pallas_call(
        body, name="adamw", grid=(4,),
        in_specs=[rows] * 4 + [whole] * 4 + [vm] * 16, out_specs=[rows] * 3 + [whole] * 3 + [vm] * 21,
        out_shape=shapes,
        compiler_params=_cp(("arbitrary",)),
    )(*big_in, *big_out, sall, *flat)


def kernel(x, positions, w_in, w_out, mix_norm_w, attn_out_norm_w, hgrn_out_norm_w, hgrn_lb_raw, final_norm_w, loss_target, m_w_in, m_w_out, m_mix_norm_w, m_attn_out_norm_w, m_hgrn_out_norm_w, m_hgrn_lb_raw, m_final_norm_w, v_w_in, v_w_out, v_mix_norm_w, v_attn_out_norm_w, v_hgrn_out_norm_w, v_hgrn_lb_raw, v_final_norm_w):
    xs = x.reshape(T, D)
    tgt = loss_target.reshape(T, D)
    pos = positions.reshape(T, 1)
    fnw = final_norm_w.reshape(1, D)

    ti = np.arange(TH)
    tri_np = ((ti[:, None] // CHUNK == ti[None, :] // CHUNK) & (ti[None, :] <= ti[:, None])).astype(np.float32)
    tri = jnp.asarray(tri_np, BF16)
    trit = jnp.asarray(tri_np.T, BF16)
    hi_ = np.arange(AW) // HEAD
    gmat = jnp.asarray((hi_[:256, None] == hi_[None, :256]).astype(np.float32) / HEAD, BF16)
    emat_np = (np.arange(128)[:, None] == hi_[None, :]).astype(np.float32)
    sel_np = (8 + hi_[:, None] == np.arange(128)[None, :]).astype(np.float32)
    emat = jnp.asarray(emat_np, BF16)
    selmat = jnp.asarray(sel_np, BF16)

    jm_arr = (2 * lax.axis_index("x") + lax.axis_index("y")).astype(jnp.int32).reshape(1)
    (hn, q1, k1, v1, q4, k4, v4, q16, k16, v16, ag, hq, hf, hi, hg, w_full, wout4) = _fwd_in(
        xs, pos, mix_norm_w, w_in.reshape(D, 1024), w_out.reshape(256, D), jm_arr)
    wout_full = wout4.reshape(D, D)
    flat = lambda a: a.reshape(T, AW)
    o1, l1 = _attn_fwd(q1, k1, v1, T // BLK, "attn_fwd_d1")
    o4, l4 = _attn_fwd(flat(q4), flat(k4), flat(v4), T // 4 // BLK, "attn_fwd_d4")
    o16, l16 = _attn_fwd(flat(q16), flat(k16), flat(v16), T // 16 // BLK, "attn_fwd_d16")
    rec, sall = _hgrn_fwd(hq, hf, hi, hgrn_lb_raw, tri)

    (dx2, do1, do4, do16, st1, st4, st16, drec, dag, dhg, rout, routb, small4) = _fwd_out(
        o1, o4.reshape(4, T // 4, AW), o16.reshape(16, T // 16, AW),
        l1, l4.reshape(4, T // 4, 128), l16.reshape(16, T // 16, 128),
        rec, ag, hg, xs, tgt, attn_out_norm_w, hgrn_out_norm_w, fnw, wout_full, gmat, emat, selmat)

    fst = lambda a: a.reshape(T, 128)
    dq1, dk1, dv1 = _attn_bwd(q1, k1, v1, do1, st1, T // BLK, "attn_bwd_d1")
    dq4, dk4, dv4 = _attn_bwd(flat(q4), flat(k4), flat(v4), flat(do4), fst(st4), T // 4 // BLK, "attn_bwd_d4")
    dq16, dk16, dv16 = _attn_bwd(flat(q16), flat(k16), flat(v16), flat(do16), fst(st16), T // 16 // BLK,
                                 "attn_bwd_d16")
    dproj_h, small6, pout_own, pout_rem = _hgrn_bwd(hq, hf, hi, hgrn_lb_raw, tri, trit, drec, sall, dhg,
                                                    rout, routb)

    r4 = lambda a: a.reshape(4, T // 4, AW)
    r16 = lambda a: a.reshape(16, T // 16, AW)
    dproj_a = _dproj_build((dq1, r4(dq4), r16(dq16)), (dk1, r4(dk4), r16(dk16)), (dv1, r4(dv4), r16(dv16)),
                           dag, pos)
    rin, rinb = _grad_w_in(hn, dproj_a, dproj_h)
    gx, _, _, small_all, fin, fout = _bwd_x(dproj_a, dproj_h, xs, dx2, mix_norm_w, w_full, rin, rinb,
                                            small4, small6, pout_own, pout_rem)
    g_w_in = fin.reshape(D, 1024)
    g_w_out = fout.reshape(256, D)

    params = [(mix_norm_w, m_mix_norm_w, v_mix_norm_w),
              (attn_out_norm_w, m_attn_out_norm_w, v_attn_out_norm_w),
              (hgrn_out_norm_w, m_hgrn_out_norm_w, v_hgrn_out_norm_w),
              (hgrn_lb_raw, m_hgrn_lb_raw, v_hgrn_lb_raw),
              (fnw, m_final_norm_w.reshape(1, D), v_final_norm_w.reshape(1, D))]
    d_in, nm_in, nv_in, d_out, nm_out, nv_out, *so = _adamw(
        (w_in.reshape(D, 1024), g_w_in, m_w_in.reshape(D, 1024), v_w_in.reshape(D, 1024)),
        (w_out.reshape(256, D), g_w_out, m_w_out.reshape(256, D), v_w_out.reshape(256, D)), small_all, params)
    loss = so[0].reshape(())
    g_s = [so[1 + 4 * p] for p in range(5)]
    d_s = [so[2 + 4 * p] for p in range(5)]
    m_s = [so[3 + 4 * p] for p in range(5)]
    v_s = [so[4 + 4 * p] for p in range(5)]
    for lst in (g_s, d_s, m_s, v_s):
        lst[4] = lst[4].reshape(D)

    return (loss, gx.reshape(1, T, D),
            g_w_in.reshape(1, D, 1024), g_w_out.reshape(1, 256, D), *g_s,
            d_in.reshape(1, D, 1024), d_out.reshape(1, 256, D), *d_s,
            nm_in.reshape(1, D, 1024), nm_out.reshape(1, 256, D), *m_s,
            nv_in.reshape(1, D, 1024), nv_out.reshape(1, 256, D), *v_s)
```

```python
import functools

import numpy as np
import jax
import jax.numpy as jnp
from jax import lax
from jax.experimental import pallas as pl
from jax.experimental.pallas import tpu as pltpu

F32 = jnp.float32
BF16 = jnp.bfloat16

T = 4096
D = 1024
AW = 512
HW = 512
NCOL = 4096
HEAD = 64
BLK = 128
CHUNK = 64
EPS = 1e-6
SCALE = HEAD ** -0.5
NEG = -1e30
ROPE_THETA = 500000.0
INV_FREQ = [float(v) for v in
            (np.float32(ROPE_THETA) ** (-(np.arange(8, dtype=np.float32)) * np.float32(0.125)))]
LR, B1, B2, AEPS, WD, STEP = 0.001, 0.9, 0.999, 1e-08, 0.01, 10
VMEM_LIMIT = 63 * 1024 * 1024
MESH = pl.DeviceIdType.MESH


def _cp(sem=None, **kw):
    return pltpu.CompilerParams(dimension_semantics=sem, vmem_limit_bytes=VMEM_LIMIT, **kw)


def _mm(a, b):
    return jnp.dot(a, b, preferred_element_type=F32)


def _mm_nt(a, b):
    return lax.dot_general(a, b, (((1,), (1,)), ((), ())), preferred_element_type=F32)


def _mm_tn(a, b):
    return lax.dot_general(a, b, (((0,), (0,)), ((), ())), preferred_element_type=F32)


def _mm_exact_l(mat_bf, x):
    h = x.astype(BF16)
    l = (x - h.astype(F32)).astype(BF16)
    return _mm(mat_bf, h) + _mm(mat_bf, l)


def _mm_exact_r(x, mat_bf):
    h = x.astype(BF16)
    l = (x - h.astype(F32)).astype(BF16)
    return _mm(h, mat_bf) + _mm(l, mat_bf)


def _sigmoid(x):
    return 0.5 * jnp.tanh(0.5 * x) + 0.5


def _rope_tables(pos):
    lane = lax.broadcasted_iota(jnp.int32, (1, 128), 1)
    jl = lane & 63
    fi = jl & 7
    inv = jnp.zeros((1, 128), F32)
    for kk in range(8):
        inv = jnp.where(fi == kk, INV_FREQ[kk], inv)
    ang = pos.astype(F32) * inv
    c = jnp.cos(ang)
    s = jnp.sin(ang)
    cosf = jnp.where(jl < 16, c, 1.0)
    s1 = jnp.where(jl < 8, -s, 0.0)
    s2 = jnp.where((jl >= 8) & (jl < 16), s, 0.0)
    return cosf, s1, s2


def _rope(t, cosf, s1, s2):
    parts = []
    for ci in range(t.shape[1] // 128):
        tc = t[:, ci * 128:(ci + 1) * 128]
        parts.append(tc * cosf + pltpu.roll(tc, 120, 1) * s1 + pltpu.roll(tc, 8, 1) * s2)
    return jnp.concatenate(parts, axis=1)


def _rope_bwd(g, cosf, s1, s2):
    parts = []
    for ci in range(g.shape[1] // 128):
        gc = g[:, ci * 128:(ci + 1) * 128]
        parts.append(gc * cosf + pltpu.roll(gc * s1, 8, 1) + pltpu.roll(gc * s2, 120, 1))
    return jnp.concatenate(parts, axis=1)


def _perm_store(val, scr, scr2, o1, o4, o16, dt):
    n = val.shape[0]
    q = n // 4
    o1[...] = val.astype(dt)
    for ci in range(val.shape[1] // 128):
        cs = slice(ci * 128, (ci + 1) * 128)
        scr[ci] = val[:, cs]
        for r4 in range(4):
            part = scr[ci, pl.ds(r4, q, stride=4), :]
            o4[r4, :, cs] = part.astype(dt)
            scr2[ci, r4 * q:(r4 + 1) * q, :] = part
        for r4 in range(4):
            for b in range(4):
                o16[r4 + 4 * b, :, cs] = scr2[ci, pl.ds(r4 * q + b, q // 4, stride=4), :].astype(dt)


def _unperm_load(r4, r16, scr_a, scr_b, scr_c):
    n = scr_a.shape[1]
    q = n // 4
    nc = r4.shape[-1] // 128
    for ci in range(nc):
        cs = slice(ci * 128, (ci + 1) * 128)
        for rr in range(4):
            scr_a[ci, pl.ds(rr, q, stride=4), :] = r4[rr, :, cs].astype(F32)
        for rr in range(4):
            for b in range(4):
                scr_c[ci, pl.ds(rr * q + b, q // 4, stride=4), :] = r16[rr + 4 * b, :, cs].astype(F32)
        for rr in range(4):
            scr_b[ci, pl.ds(rr, q, stride=4), :] = scr_c[ci, rr * q:(rr + 1) * q, :]
    return (jnp.concatenate([scr_a[ci] for ci in range(nc)], axis=1),
            jnp.concatenate([scr_b[ci] for ci in range(nc)], axis=1))


def _fwd_in(x, pos, mixw, w_in, w_out, jm_arr):
    TT = 512
    NT = T // TT

    def body(jm_ref, x_ref, pos_ref, mw_ref, win_ref, wout_ref,
             hnt_ref, q1, k1, v1, q4, k4, v4, q16, k16, v16, ag, hq, hf, hi, hg, wfull_o, woutfull_o,
             wbuf, wobuf, hn_all, scr, scr2, stage, send_sems, recv_sems, loc_sems):
        s = pl.program_id(0)
        i = pl.program_id(1)
        mx, my, c = lax.axis_index("x"), lax.axis_index("y"), lax.axis_index("c")
        me, sibling = (mx, my, c), (mx, my, 1 - c)
        chips = [(mx, 1 - my), (1 - mx, my), (1 - mx, 1 - my)]
        jm = 2 * mx + my
        rows_in = [pl.ds(pl.multiple_of(h * 512, 512), 512) for h in (c, 1 - c)]
        rows_out = [pl.ds(pl.multiple_of(h * 128, 128), 128) for h in (c, 1 - c)]

        def blk(k):
            return lax.bitwise_xor(jm, k + 1)

        def rc(n, ref, to):
            return pltpu.make_async_remote_copy(src_ref=ref, dst_ref=ref, send_sem=send_sems.at[n],
                                                recv_sem=recv_sems.at[n], device_id=to, device_id_type=MESH)

        halves = [pl.ds(0, 512), pl.ds(512, 512)]
        send_in = lambda k, h: rc(12 + 2 * k + h, wbuf.at[jm, rows_in[0], halves[h]], (*chips[k], c))
        got_in = lambda k, h: rc(12 + 2 * k + h, wbuf.at[blk(k), rows_in[0], halves[h]], me)
        relay = lambda h: rc(16 + h, wbuf.at[blk(h), rows_in[0], halves[h]], (*chips[1 - h], c))
        got_relay = lambda h: rc(16 + h, wbuf.at[blk(2), rows_in[0], halves[h]], me)
        send_out = lambda k: rc(3 + k, wobuf.at[jm, rows_out[0], :], (*chips[k], c))
        got_out = lambda k: rc(3 + k, wobuf.at[blk(k), rows_out[0], :], me)
        pass_in = lambda k: rc(6 + k, wbuf.at[blk(k), rows_in[0], :], sibling)
        pass_out = lambda k: rc(9 + k, wobuf.at[blk(k), rows_out[0], :], sibling)
        passed_in = lambda k: rc(6 + k, wbuf.at[blk(k), rows_in[1], :], me)
        passed_out = lambda k: rc(9 + k, wobuf.at[blk(k), rows_out[1], :], me)

        def keep(j, n):
            return pltpu.make_async_copy(wbuf.at[j], wfull_o.at[:, pl.ds(j * 1024, 1024)], loc_sems.at[n])

        @pl.when((s == 0) & (i == 0))
        def _():
            for p in range(5):
                src = win_ref.at[pl.ds(p * 256, 256), :] if p < 4 else wout_ref
                load = pltpu.make_async_copy(src, stage, loc_sems.at[4])
                load.start()
                load.wait()
                if p < 4:
                    wbuf[jm, p * 256:(p + 1) * 256, :] = stage[...].astype(BF16)
                else:
                    wobuf[jm] = stage[...].astype(BF16)
            for k in range(2):
                for h in range(2):
                    send_in(k, h).start()
            keep(jm, 0).start()

        def arrive(k):
            if k == 0:
                for kk in range(2):
                    for h in range(2):
                        got_in(kk, h).wait_recv()
                relay(0).start()
                relay(1).start()
            if k == 2:
                got_relay(0).wait_recv()
                got_relay(1).wait_recv()
            pass_in(k).start()
            passed_in(k).wait_recv()
            keep(blk(k), k + 1).start()
            if k == 2:
                for kk in range(3):
                    send_out(kk).start()

        pl.when((s == 1) & (i == 0))(functools.partial(arrive, 0))

        @pl.when((s == 2) & (i == 0))
        def _():
            arrive(1)
            arrive(2)

        tile = pl.ds(pl.multiple_of(i * TT, TT), TT)

        @pl.when(s == 0)
        def _():
            xv = x_ref[...]
            r = lax.rsqrt(jnp.mean(xv * xv, axis=-1, keepdims=True) + EPS)
            hnf = (xv * r) * mw_ref[...]
            hn_all[tile, :] = hnf.astype(BF16)
            hnt_ref[...] = hnf.T.astype(BF16)

        def project(jj):
            hn = hn_all[tile, :]
            lo = _mm(hn, wbuf[jj, :, 0:512])
            hi_cols = _mm(hn, wbuf[jj, :, 512:1024])
            if jj == 0:
                cosf, s1, s2 = _rope_tables(pos_ref[...])
                _perm_store(_rope(lo, cosf, s1, s2), scr, scr2, q1, q4, q16, BF16)
                _perm_store(_rope(hi_cols, cosf, s1, s2), scr, scr2, k1, k4, k16, BF16)
            elif jj == 1:
                _perm_store(lo, scr, scr2, v1, v4, v16, BF16)
                ag[...] = hi_cols.astype(BF16)
            elif jj == 2:
                hq[...] = lo.astype(BF16)
                hf[...] = hi_cols.astype(BF16)
            else:
                hi[...] = lo.astype(BF16)
                hg[...] = hi_cols.astype(BF16)

        def project_block(j):
            for jj in range(4):
                pl.when(j == jj)(functools.partial(project, jj))

        @pl.when(s < 2)
        def _():
            project_block(lax.bitwise_xor(jm, s))

        @pl.when(s == 2)
        def _():
            project_block(lax.bitwise_xor(jm, 2))
            project_block(lax.bitwise_xor(jm, 3))

        @pl.when((s == 2) & (i == NT - 1))
        def _():
            for k in range(3):
                got_out(k).wait_recv()
                pass_out(k).start()
            for k in range(3):
                passed_out(k).wait_recv()
            out = pltpu.make_async_copy(wobuf, woutfull_o, loc_sems.at[4])
            out.start()
            for h in range(2):
                relay(h).wait_send()
                for k in range(2):
                    send_in(k, h).wait_send()
            for k in range(3):
                send_out(k).wait_send()
                pass_in(k).wait_send()
                pass_out(k).wait_send()
            keep(jm, 0).wait()
            for k in range(3):
                keep(blk(k), k + 1).wait()
            out.wait()

    def at_stage_of(jb):
        def index(s, i, jm_ref):
            sa = jnp.minimum(lax.bitwise_xor(jm_ref[0], jb), 2)
            return jnp.where(s < sa, 0, jnp.where(s == sa, i, NT - 1))
        return index

    tok = lambda w, jb: pl.BlockSpec((TT, w), lambda s, i, jm_ref: (at_stage_of(jb)(s, i, jm_ref), 0))
    d4 = lambda jb: pl.BlockSpec((4, TT // 4, AW), lambda s, i, jm_ref: (0, at_stage_of(jb)(s, i, jm_ref), 0))
    d16 = lambda jb: pl.BlockSpec((16, TT // 16, AW), lambda s, i, jm_ref: (0, at_stage_of(jb)(s, i, jm_ref), 0))
    hbm = pl.BlockSpec(memory_space=pltpu.HBM)
    sd = lambda shape, dt: jax.ShapeDtypeStruct(shape, dt)
    in_own_stage = lambda s, i: jnp.where(s == 0, i, NT - 1)
    grid_spec = pltpu.PrefetchScalarGridSpec(
        num_scalar_prefetch=1, grid=(3, NT),
        in_specs=[pl.BlockSpec((TT, D), lambda s, i, jm_ref: (in_own_stage(s, i), 0)),
                  pl.BlockSpec((TT, 1), lambda s, i, jm_ref: (i, 0)),
                  pl.BlockSpec((1, D), lambda s, i, jm_ref: (0, 0)), hbm, hbm],
        out_specs=[pl.BlockSpec((D, TT), lambda s, i, jm_ref: (0, in_own_stage(s, i))),
                   tok(AW, 0), tok(AW, 0), tok(AW, 1), d4(0), d4(0), d4(1), d16(0), d16(0), d16(1),
                   tok(AW, 1), tok(AW, 2), tok(AW, 2), tok(AW, 3), tok(AW, 3), hbm, hbm],
        scratch_shapes=[pltpu.VMEM((4, D, 1024), BF16), pltpu.VMEM((4, 256, D), BF16), pltpu.VMEM((T, D), BF16),
                        pltpu.VMEM((4, TT, 128), F32), pltpu.VMEM((4, TT, 128), F32), pltpu.VMEM((256, 1024), F32),
                        pltpu.SemaphoreType.DMA((18,)),
                        pltpu.SemaphoreType.DMA((18,)), pltpu.SemaphoreType.DMA((6,))])
    return pl.pallas_call(
        body, name="fwd_in", grid_spec=grid_spec,
        out_shape=[sd((D, T), BF16)] + [sd((T, AW), BF16)] * 3 + [sd((4, T // 4, AW), BF16)] * 3
        + [sd((16, T // 16, AW), BF16)] * 3
        + [sd((T, AW), BF16)] * 5 + [sd((D, NCOL), BF16), sd((4, 256, D), BF16)],
        compiler_params=_cp(("arbitrary", "arbitrary")),
    )(jm_arr, x, pos, mixw, w_in, w_out)


def _band_mask(key_axis, nkeys=2 * BLK):
    shape = (nkeys, 2 * BLK) if key_axis == 0 else (2 * BLK, nkeys)
    kj = lax.broadcasted_iota(jnp.int32, shape, key_axis)
    qi = lax.broadcasted_iota(jnp.int32, shape, 1 - key_axis) & (BLK - 1)
    return (kj >= qi) & (kj <= qi + BLK), kj, qi


def _stack_heads(t2, in_a):
    z = jnp.zeros_like(t2)
    return jnp.concatenate([jnp.where(in_a[0], t2, z), jnp.where(in_a[1], t2, z)], axis=0)


def _attn_fwd(q, k, v, nb, name):
    n = 8
    CH = n * BLK
    halo = nb > n

    def body(*refs):
        if halo:
            q_ref, k_ref, v_ref, kp_ref, vp_ref, o_ref, lse_ref = refs
        else:
            q_ref, k_ref, v_ref, o_ref, lse_ref = refs
        lane = lax.broadcasted_iota(jnp.int32, (1, 128), 1)
        in_a = [lane < HEAD, lane >= HEAD]
        band, kj, _ = _band_mask(1)
        thr0 = jnp.where((n * pl.program_id(0)) % nb == 0, BLK, 0) if halo else BLK
        mask0 = band & (kj >= thr0)
        mask_first = band & (kj >= BLK)
        for b in range(n):
            rs = slice(b * BLK, (b + 1) * BLK)
            stat = jnp.zeros((BLK, 128), F32)
            for hp in range(4):
                cs = slice(hp * 128, (hp + 1) * 128)
                q2s = _stack_heads(q_ref[rs, cs], in_a)
                if b == 0:
                    kprev = kp_ref[:, cs] if halo else k_ref[rs, cs]
                    vprev = vp_ref[:, cs] if halo else v_ref[rs, cs]
                    kk = jnp.concatenate([kprev, k_ref[rs, cs]], axis=0)
                    vv = jnp.concatenate([vprev, v_ref[rs, cs]], axis=0)
                    mask = mask0
                else:
                    kk = k_ref[(b - 1) * BLK:(b + 1) * BLK, cs]
                    vv = v_ref[(b - 1) * BLK:(b + 1) * BLK, cs]
                    mask = mask_first if b % nb == 0 else band
                s = jnp.where(mask, _mm_nt(q2s, kk) * SCALE, NEG)
                m = jnp.max(s, axis=-1, keepdims=True)
                p = jnp.exp(s - m)
                l = jnp.sum(p, axis=-1, keepdims=True)
                o = _mm(p.astype(BF16), vv) / l
                lse = m + jnp.log(l)
                o_ref[rs, cs] = jnp.where(in_a[0], o[:BLK], o[BLK:]).astype(BF16)
                stat = jnp.where(lane == 2 * hp, lse[:BLK], stat)
                stat = jnp.where(lane == 2 * hp + 1, lse[BLK:], stat)
            lse_ref[rs, :] = stat

    cur = pl.BlockSpec((CH, AW), lambda i: (i, 0))
    prev = pl.BlockSpec((BLK, AW), lambda i: (jnp.maximum(n * i - 1, 0), 0))
    return pl.pallas_call(
        body, name=name, grid=(T // CH,),
        in_specs=[cur, cur, cur] + ([prev, prev] if halo else []),
        out_specs=[cur, pl.BlockSpec((CH, 128), lambda i: (i, 0))],
        out_shape=[jax.ShapeDtypeStruct((T, AW), BF16), jax.ShapeDtypeStruct((T, 128), F32)],
        compiler_params=_cp(("parallel",)),
    )(*((q, k, v) + ((k, v) if halo else ())))


def _attn_bwd(q, k, v, do, st, nb, name):
    n = 8
    CH = n * BLK
    NBLK = T // BLK
    halo = nb > n

    def body(*refs):
        if halo:
            (q_ref, k_ref, v_ref, do_ref, st_ref, kp_ref, vp_ref, qn_ref, don_ref, stn_ref,
             dq_ref, dk_ref, dv_ref) = refs
        else:
            q_ref, k_ref, v_ref, do_ref, st_ref, dq_ref, dk_ref, dv_ref = refs
        i = pl.program_id(0)
        lane = lax.broadcasted_iota(jnp.int32, (1, 128), 1)
        in_a = [lane < HEAD, lane >= HEAD]
        band, kj, _ = _band_mask(0)
        thr0 = jnp.where((n * i) % nb == 0, BLK, 0) if halo else BLK
        mask0 = band & (kj >= thr0)
        mask_first = band & (kj >= BLK)

        def stat_rows(st_t, hp):
            lse_r = jnp.concatenate([st_t[2 * hp:2 * hp + 1, :], st_t[2 * hp + 1:2 * hp + 2, :]], axis=1)
            dl_r = jnp.concatenate([st_t[8 + 2 * hp:9 + 2 * hp, :], st_t[9 + 2 * hp:10 + 2 * hp, :]], axis=1)
            return lse_r, dl_r

        st_t = [st_ref[b * BLK:(b + 1) * BLK, :].T for b in range(n)]
        if halo:
            nxt_thr = jnp.where((n * i + n) % nb == 0, 2 * BLK, 0)
            _, kj1, qi1 = _band_mask(0, BLK)
            mask_next = kj1 >= qi1 + nxt_thr
            stn_t = stn_ref[...].T

        for hp in range(4):
            cs = slice(hp * 128, (hp + 1) * 128)
            kb = [k_ref[b * BLK:(b + 1) * BLK, cs] for b in range(n)]
            vb = [v_ref[b * BLK:(b + 1) * BLK, cs] for b in range(n)]
            dk_acc = [jnp.zeros((BLK, 128), F32) for _ in range(n)]
            dv_acc = [jnp.zeros((BLK, 128), F32) for _ in range(n)]
            for b in range(n):
                rs = slice(b * BLK, (b + 1) * BLK)
                q2s = _stack_heads(q_ref[rs, cs], in_a)
                do2s = _stack_heads(do_ref[rs, cs], in_a)
                if b == 0:
                    kprev = kp_ref[:, cs] if halo else kb[0]
                    vprev = vp_ref[:, cs] if halo else vb[0]
                    mask = mask0
                else:
                    kprev, vprev, mask = kb[b - 1], vb[b - 1], (mask_first if b % nb == 0 else band)
                kk = jnp.concatenate([kprev, kb[b]], axis=0)
                vv = jnp.concatenate([vprev, vb[b]], axis=0)
                lse_r, dl_r = stat_rows(st_t[b], hp)
                s_t = jnp.where(mask, _mm_nt(kk, q2s) * SCALE, NEG)
                p_t = jnp.exp(s_t - lse_r)
                ds_t = (p_t * (_mm_nt(vv, do2s) - dl_r)).astype(BF16)
                dkk = _mm(ds_t, q2s) * SCALE
                dvv = _mm(p_t.astype(BF16), do2s)
                dqs = _mm_tn(ds_t, kk) * SCALE
                dq_ref[rs, cs] = jnp.where(in_a[0], dqs[:BLK], dqs[BLK:]).astype(BF16)
                dk_acc[b] += dkk[BLK:]
                dv_acc[b] += dvv[BLK:]
                if b > 0:
                    dk_acc[b - 1] += dkk[:BLK]
                    dv_acc[b - 1] += dvv[:BLK]
            if halo:
                q2s = _stack_heads(qn_ref[:, cs], in_a)
                do2s = _stack_heads(don_ref[:, cs], in_a)
                lse_r, dl_r = stat_rows(stn_t, hp)
                s_t = jnp.where(mask_next, _mm_nt(kb[n - 1], q2s) * SCALE, NEG)
                p_t = jnp.exp(s_t - lse_r)
                ds_t = (p_t * (_mm_nt(vb[n - 1], do2s) - dl_r)).astype(BF16)
                dk_acc[n - 1] += _mm(ds_t, q2s) * SCALE
                dv_acc[n - 1] += _mm(p_t.astype(BF16), do2s)
            for b in range(n):
                dk_ref[b * BLK:(b + 1) * BLK, cs] = dk_acc[b].astype(BF16)
                dv_ref[b * BLK:(b + 1) * BLK, cs] = dv_acc[b].astype(BF16)

    cur = pl.BlockSpec((CH, AW), lambda i: (i, 0))
    cur_st = pl.BlockSpec((CH, 128), lambda i: (i, 0))
    prev = pl.BlockSpec((BLK, AW), lambda i: (jnp.maximum(n * i - 1, 0), 0))
    nxt = pl.BlockSpec((BLK, AW), lambda i: (jnp.minimum(n * i + n, NBLK - 1), 0))
    nxt_st = pl.BlockSpec((BLK, 128), lambda i: (jnp.minimum(n * i + n, NBLK - 1), 0))
    ins = [cur] * 4 + [cur_st] + ([prev, prev, nxt, nxt, nxt_st] if halo else [])
    args = (q, k, v, do, st) + ((k, v, q, do, st) if halo else ())
    return pl.pallas_call(
        body, name=name, grid=(T // CH,),
        in_specs=ins,
        out_specs=[cur] * 3,
        out_shape=[jax.ShapeDtypeStruct((T, AW), BF16)] * 3,
        compiler_params=_cp(("parallel",)),
    )(*args)


TH = 256
NCH = TH // CHUNK


def _hgrn_common(hq_ref, hf_ref, lbr_ref, tri_ref):
    r0 = lbr_ref[0:1, :]
    r1 = lbr_ref[1:2, :]
    mx = jnp.maximum(r0, r1)
    e0 = jnp.exp(r0 - mx)
    e1 = jnp.exp(r1 - mx)
    lb = e0 / (e0 + e1)
    hqv = hq_ref[...].astype(F32)
    sq = _sigmoid(hqv)
    qv = hqv * sq
    sf = _sigmoid(hf_ref[...].astype(F32))
    f = lb + (1.0 - lb) * sf
    kv = 1.0 - f
    g = jnp.log(f)
    cum = _mm_exact_l(tri_ref[...], g)
    dec = jnp.exp(jnp.concatenate([cum[c * CHUNK + CHUNK - 1:(c + 1) * CHUNK, :] for c in range(NCH)], axis=0))
    decb = jnp.concatenate([jnp.broadcast_to(dec[c:c + 1, :], (CHUNK, HW)) for c in range(NCH)], axis=0)
    ea = jnp.exp(cum)
    ena = jnp.exp(-cum)
    eend = decb * ena
    return dict(lb=lb, hq=hqv, sq=sq, q=qv, sf=sf, f=f, k=kv, cum=cum, ea=ea, ena=ena, eend=eend,
                qd=qv * ea, ki=kv * ena, ke=kv * eend, dec=dec)


def _tri_mask(transposed=False):
    ti = lax.broadcasted_iota(jnp.int32, (TH, TH), 1 if transposed else 0)
    si = lax.broadcasted_iota(jnp.int32, (TH, TH), 0 if transposed else 1)
    return (si <= ti) & ((si // CHUNK) == (ti // CHUNK))


def _hgrn_fwd(hq, hf, hi, lbr, tri):
    NSUB = 2

    def body(hq_ref, hf_ref, hi_ref, lbr_ref, tri_ref, rec_ref, sall_ref, st_scr):
        @pl.when(pl.program_id(0) == 0)
        def _():
            st_scr[...] = jnp.zeros_like(st_scr)

        causal = _tri_mask()
        for u in range(NSUB):
            tile = slice(u * TH, (u + 1) * TH)
            w = _hgrn_common(hq_ref.at[tile, :], hf_ref.at[tile, :], lbr_ref, tri_ref)
            qd, ki, ke = w["qd"].astype(BF16), w["ki"].astype(BF16), w["ke"].astype(BF16)
            dec = w["dec"]
            vb = hi_ref[tile, :]
            for h in range(4):
                cs = slice(h * 128, (h + 1) * 128)
                att = jnp.where(causal, _mm_nt(qd[:, cs], ki[:, cs]), 0.0)
                o_intra = _mm(att.astype(BF16), vb[:, cs])
                st = st_scr[:, cs]
                for c in range(NCH):
                    rs = slice(c * CHUNK, (c + 1) * CHUNK)
                    sall_ref[u * NCH + c, :, cs] = st
                    rec_ref[u * TH + c * CHUNK:u * TH + (c + 1) * CHUNK, cs] = (
                        o_intra[rs] + _mm_nt(qd[rs, cs], st.astype(BF16))).astype(BF16)
                    st = dec[c:c + 1, cs] * st + _mm_tn(vb[rs, cs], ke[rs, cs])
                st_scr[:, cs] = st

    tok = pl.BlockSpec((NSUB * TH, HW), lambda i: (i, 0))
    return pl.pallas_call(
        body, name="hgrn_fwd", grid=(T // (NSUB * TH),),
        in_specs=[tok, tok, tok, pl.BlockSpec((2, HW), lambda i: (0, 0)), pl.BlockSpec((TH, TH), lambda i: (0, 0))],
        out_specs=[tok, pl.BlockSpec((NSUB * NCH, 128, HW), lambda i: (i, 0, 0))],
        out_shape=[jax.ShapeDtypeStruct((T, HW), BF16), jax.ShapeDtypeStruct((T // CHUNK, 128, HW), F32)],
        scratch_shapes=[pltpu.VMEM((128, HW), F32)],
        compiler_params=_cp(("arbitrary",)),
    )(hq, hf, hi, lbr, tri)


def _hgrn_bwd(hq, hf, hi, lbr, tri, trit, drec, sall, dhg, rout, routb):
    NSUB = 2
    NT = T // (NSUB * TH)

    def body(hq_ref, hf_ref, hi_ref, lbr_ref, tri_ref, trit_ref, do_ref, sall_ref, dhg_ref, rout_r, routb_r,
             dph_ref, small_ref, pout_o, poutr_o,
             dst_scr, dlb_scr, dqd_scr, dki_scr, dke_scr, dlast_scr, send_sems, recv_sems, loc_sems):
        step = pl.program_id(0)
        loc, rem = _chip_copies(_w_out_piece, rout_r, routb_r, pout_o, poutr_o, send_sems, recv_sems,
                                loc_sems.at[0])

        @pl.when(step == 0)
        def _():
            dst_scr[...] = jnp.zeros_like(dst_scr)
            dlb_scr[...] = jnp.zeros_like(dlb_scr)
            for cp in loc + rem:
                cp.start()

        causal = _tri_mask()
        causal_t = _tri_mask(transposed=True)
        lb = None
        for u in reversed(range(NSUB)):
            tile = slice(u * TH, (u + 1) * TH)
            w = _hgrn_common(hq_ref.at[tile, :], hf_ref.at[tile, :], lbr_ref, tri_ref)
            qd, ki, ke = w["qd"].astype(BF16), w["ki"].astype(BF16), w["ke"].astype(BF16)
            dec = w["dec"]
            vb = hi_ref[tile, :]
            dob = do_ref[tile, :].astype(BF16)
            for h in range(4):
                cs = slice(h * 128, (h + 1) * 128)
                att_t = jnp.where(causal_t, _mm_nt(ki[:, cs], qd[:, cs]), 0.0).astype(BF16)
                datt_t = jnp.where(causal_t, _mm_nt(vb[:, cs], dob[:, cs]), 0.0).astype(BF16)
                datt = jnp.where(causal, _mm_nt(dob[:, cs], vb[:, cs]), 0.0).astype(BF16)
                dv_intra = _mm(att_t, dob[:, cs])
                dqd_intra = _mm(datt, ki[:, cs])
                dki_scr[u, :, cs] = _mm(datt_t, qd[:, cs])
                dst = dst_scr[:, cs]
                for c in reversed(range(NCH)):
                    rs = slice(c * CHUNK, (c + 1) * CHUNK)
                    dec_c = dec[c:c + 1, :]
                    st = sall_ref[u * NCH + c, :, cs]
                    dstb = dst.astype(BF16)
                    dph_ref[u * TH + c * CHUNK:u * TH + (c + 1) * CHUNK, 2 * HW + h * 128:2 * HW + (h + 1) * 128] = (
                        dv_intra[rs] + _mm_nt(ke[rs, cs], dstb)).astype(BF16)
                    dqd_scr[u, rs, cs] = dqd_intra[rs] + _mm(dob[rs, cs], st.astype(BF16))
                    dke_scr[u, rs, cs] = _mm(vb[rs, cs], dstb)
                    ddec = jnp.sum(dst * st, axis=0, keepdims=True)
                    dlast_scr[u, c:c + 1, cs] = ddec * dec_c[:, cs]
                    dst = dec_c[:, cs] * dst + _mm_tn(dob[rs, cs], qd[rs, cs])
                dst_scr[:, cs] = dst
            dqd, dki, dke = dqd_scr[u], dki_scr[u], dke_scr[u]
            dq = dqd * w["ea"]
            dk = dki * w["ena"] + dke * w["eend"]
            dcum = dqd * w["qd"] - dki * w["ki"] - dke * w["ke"]
            dkeke = dke * w["ke"]
            dlastb = jnp.concatenate(
                [jnp.broadcast_to(dlast_scr[u, c:c + 1, :]
                                  + jnp.sum(dkeke[c * CHUNK:(c + 1) * CHUNK], axis=0, keepdims=True), (CHUNK, HW))
                 for c in range(NCH)], axis=0)
            dg = _mm_exact_l(trit_ref[...], dcum) + dlastb
            df = dg / w["f"] - dk
            lb, sf, sq = w["lb"], w["sf"], w["sq"]
            dph_ref[tile, HW:2 * HW] = (df * (1.0 - lb) * sf * (1.0 - sf)).astype(BF16)
            dph_ref[tile, 0:HW] = (dq * (sq * (1.0 + w["hq"] * (1.0 - sq)))).astype(BF16)
            dph_ref[tile, 3 * HW:4 * HW] = dhg_ref[tile, :]
            dlb_scr[...] += jnp.sum(df * (1.0 - sf), axis=0, keepdims=True)

        @pl.when(step == NT - 1)
        def _():
            gr = dlb_scr[...] * lb * (1.0 - lb)
            small_ref[...] = jnp.zeros_like(small_ref)
            small_ref[0:1, 0:HW] = gr
            small_ref[1:2, 0:HW] = -gr
            for cp in rem:
                cp.wait_recv()
            for cp in rem:
                cp.wait_send()
            for cp in loc:
                cp.wait()

    tok = pl.BlockSpec((NSUB * TH, HW), lambda i: (NT - 1 - i, 0))
    const = lambda shape: pl.BlockSpec(shape, lambda i: (0,) * len(shape))
    hbm = pl.BlockSpec(memory_space=pltpu.HBM)
    return pl.pallas_call(
        body, name="hgrn_bwd", grid=(NT,),
        in_specs=[tok, tok, tok, const((2, HW)), const((TH, TH)), const((TH, TH)), tok,
                  pl.BlockSpec((NSUB * NCH, 128, HW), lambda i: (NT - 1 - i, 0, 0)), tok, hbm, hbm],
        out_specs=[pl.BlockSpec((NSUB * TH, NCOL // 2), lambda i: (NT - 1 - i, 0)), const((8, D)), hbm, hbm],
        out_shape=[jax.ShapeDtypeStruct((T, NCOL // 2), BF16), jax.ShapeDtypeStruct((8, D), F32),
                   jax.ShapeDtypeStruct((128, D), F32), jax.ShapeDtypeStruct((3, 128, D), BF16)],
        scratch_shapes=[pltpu.VMEM((128, HW), F32), pltpu.VMEM((1, HW), F32), pltpu.VMEM((NSUB, TH, HW), F32),
                        pltpu.VMEM((NSUB, TH, HW), F32), pltpu.VMEM((NSUB, TH, HW), F32),
                        pltpu.VMEM((NSUB, 8, HW), F32),
                        pltpu.SemaphoreType.DMA((3,)), pltpu.SemaphoreType.DMA((3,)), pltpu.SemaphoreType.DMA((1,))],
        compiler_params=_cp(("arbitrary",)),
    )(hq, hf, hi, lbr, tri, trit, drec, sall, dhg, rout, routb)


def _fwd_out(o1, o4, o16, l1, l4, l16, rec, ag, hg, x, tgt, anw, hnw, fnw, wout_full, gmat, emat, selmat):
    TT = 512

    def body(o1_r, o4_r, o16_r, l1_r, l4_r, l16_r, rec_r, ag_r, hg_r, x_r, tgt_r, anw_r, hnw_r, fnw_r, wo_r, g_r,
             e_r, sel_r, dx2_o, do1_o, do4_o, do16_o, st1_o, st4_o, st16_o, drec_o, dag_o, dhg_o,
             rout_o, routb_o, small_o, scr_a, scr_b, scr_c, gwout_o, rbuf, send_sems, recv_sems):
        @pl.when(pl.program_id(0) == 0)
        def _():
            gwout_o[...] = jnp.zeros_like(gwout_o)
            small_o[...] = jnp.zeros_like(small_o)

        def unperm(r4, r16):
            return _unperm_load(r4, r16, scr_a, scr_b, scr_c)

        def perm_out(val, p1, p4, p16, dt):
            _perm_store(val, scr_a, scr_b, p1, p4, p16, dt)

        o4u, o16u = unperm(o4_r, o16_r)
        l4c, l16c = unperm(l4_r, l16_r)
        l1c = l1_r[...]
        mxc = jnp.maximum(jnp.maximum(l1c, l4c), l16c)
        w1c, w4c, w16c = jnp.exp(l1c - mxc), jnp.exp(l4c - mxc), jnp.exp(l16c - mxc)
        denc = w1c + w4c + w16c
        lane = lax.broadcasted_iota(jnp.int32, (1, 128), 1)
        lse_c = jnp.where(lane < 8, mxc + jnp.log(denc), 0.0)
        em = e_r[...]
        wn1 = _mm_exact_r(w1c / denc, em)
        wn4 = _mm_exact_r(w4c / denc, em)
        o1v = o1_r[...].astype(F32)
        attn = wn1 * o1v + wn4 * o4u + (1.0 - wn1 - wn4) * o16u
        gm = g_r[...]

        def head_mean_a(t):
            return jnp.concatenate([_mm_exact_r(t[:, :256], gm), _mm_exact_r(t[:, 256:], gm)], axis=1)

        def head_mean_h(t):
            return jnp.concatenate(
                [jnp.broadcast_to(jnp.mean(t[:, h * 128:(h + 1) * 128], axis=-1, keepdims=True), (TT, 128))
                 for h in range(4)], axis=1)

        rs_a = lax.rsqrt(head_mean_a(attn * attn) + EPS)
        n_a = attn * rs_a
        agv = ag_r[...].astype(F32)
        sg_a = _sigmoid(agv)
        si_a = agv * sg_a
        anw_v = anw_r[...]
        y_a = (n_a * anw_v) * si_a
        recv = rec_r[...].astype(F32)
        rs_h = lax.rsqrt(head_mean_h(recv * recv) + EPS)
        n_h = recv * rs_h
        hgv = hg_r[...].astype(F32)
        sg_h = _sigmoid(hgv)
        si_h = hgv * sg_h
        hnw_v = hnw_r[...]
        y_h = (n_h * hnw_v) * si_h
        mixed = jnp.concatenate([y_a, y_h], axis=1).astype(BF16)
        xv = x_r[...]
        x2 = xv + _mm(mixed, wo_r[...])
        r2 = lax.rsqrt(jnp.mean(x2 * x2, axis=-1, keepdims=True) + EPS)
        fnw_v = fnw_r[...]
        xn = x2 * r2
        err = xn * fnw_v - tgt_r[...]
        small_o[2:3, :] += 0.5 * jnp.sum(jnp.mean(err * err, axis=-1, keepdims=True), axis=0, keepdims=True)
        dy = err * (1.0 / D)
        small_o[0:1, :] += jnp.sum(dy * xn, axis=0, keepdims=True)
        dyw = dy * fnw_v
        dx2 = r2 * dyw - x2 * ((r2 * r2 * r2) * jnp.mean(dyw * x2, axis=-1, keepdims=True))
        dx2_o[...] = dx2
        dx2b = dx2.astype(BF16)
        gwout_o[...] += _mm_tn(mixed, dx2b)
        dmix = _mm_nt(dx2b, wo_r[...])
        dm_a, dm_h = dmix[:, :AW], dmix[:, AW:]
        dag_o[...] = (dm_a * (n_a * anw_v) * (sg_a * (1.0 + agv * (1.0 - sg_a)))).astype(BF16)
        dn_a = dm_a * anw_v * si_a
        small_o[1:2, 0:AW] += jnp.sum(dm_a * n_a * si_a, axis=0, keepdims=True)
        dattn = rs_a * (dn_a - n_a * head_mean_a(dn_a * n_a))
        perm_out(dattn, do1_o, do4_o, do16_o, BF16)
        stats = lse_c + _mm_exact_r(dattn * attn, sel_r[...])
        perm_out(stats, st1_o, st4_o, st16_o, F32)
        dhg_o[...] = (dm_h * (n_h * hnw_v) * (sg_h * (1.0 + hgv * (1.0 - sg_h)))).astype(BF16)
        dn_h = dm_h * hnw_v * si_h
        small_o[1:2, AW:] += jnp.sum(dm_h * n_h * si_h, axis=0, keepdims=True)
        drec_o[...] = (rs_h * (dn_h - n_h * head_mean_h(dn_h * n_h))).astype(BF16)

        @pl.when(pl.program_id(0) == T // TT - 1)
        def _():
            x, y, c = lax.axis_index("x"), lax.axis_index("y"), lax.axis_index("c")
            cps = [pltpu.make_async_remote_copy(
                src_ref=gwout_o.at[pl.ds(pl.multiple_of(j * 256 + (1 - c) * 128, 128), 128), :], dst_ref=rbuf.at[j],
                send_sem=send_sems.at[j], recv_sem=recv_sems.at[j], device_id=(x, y, 1 - c), device_id_type=MESH)
                for j in range(4)]
            for cp in cps:
                cp.start()
            for j, cp in enumerate(cps):
                cp.wait_recv()
                red = gwout_o[pl.ds(pl.multiple_of(j * 256 + c * 128, 128), 128), :] + rbuf[j]
                rout_o[j * 128:(j + 1) * 128, :] = red
                routb_o[j * 128:(j + 1) * 128, :] = red.astype(BF16)
            for cp in cps:
                cp.wait_send()

    tok = lambda w: pl.BlockSpec((TT, w), lambda i: (i, 0))
    d4 = pl.BlockSpec((4, TT // 4, AW), lambda i: (0, i, 0))
    d16 = pl.BlockSpec((16, TT // 16, AW), lambda i: (0, i, 0))
    const = lambda shape: pl.BlockSpec(shape, lambda i: (0,) * len(shape))
    sd = lambda shape, dt: jax.ShapeDtypeStruct(shape, dt)
    c4 = pl.BlockSpec((4, TT // 4, 128), lambda i: (0, i, 0))
    c16 = pl.BlockSpec((16, TT // 16, 128), lambda i: (0, i, 0))
    p3 = lambda w, dt: [sd((T, w), dt), sd((4, T // 4, w), dt), sd((16, T // 16, w), dt)]
    return pl.pallas_call(
        body, name="fwd_out", grid=(T // TT,),
        in_specs=[tok(AW), d4, d16, tok(128), c4, c16, tok(AW), tok(AW), tok(AW), tok(D), tok(D),
                  const((1, AW)), const((1, HW)), const((1, D)), const((D, D)), const((256, 256)),
                  const((128, AW)), const((AW, 128))],
        out_specs=[tok(D)] + [tok(AW), d4, d16] + [tok(128), c4, c16] + [tok(AW)] * 3
        + [const((512, D)), const((512, D)), const((8, D))],
        out_shape=[sd((T, D), F32)] + p3(AW, BF16) + p3(128, F32)
        + [sd((T, AW), BF16), sd((T, AW), BF16), sd((T, AW), BF16), sd((512, D), F32), sd((512, D), BF16),
           sd((8, D), F32)],
        scratch_shapes=[pltpu.VMEM((4, TT, 128), F32)] * 3 + [pltpu.VMEM((D, D), F32),
                        pltpu.VMEM((4, 128, D), F32), pltpu.SemaphoreType.DMA((4,)), pltpu.SemaphoreType.DMA((4,))],
        compiler_params=_cp(("arbitrary",)),
    )(o1, o4, o16, l1, l4, l16, rec, ag, hg, x, tgt, anw, hnw, fnw, wout_full, gmat, emat, selmat)


def _dproj_build(dq, dk, dv, dag, pos):
    TT = 512

    def body(dq1, dq4, dq16, dk1, dk4, dk16, dv1, dv4, dv16, dag_r, pos_r, dproj_o, scr_a, scr_b, scr_c):
        def unperm_sum(r1, r4, r16):
            u4, u16 = _unperm_load(r4, r16, scr_a, scr_b, scr_c)
            return r1[...] + u4 + u16

        cosf, s1, s2 = _rope_tables(pos_r[...])
        dproj_o[:, 0:512] = _rope_bwd(unperm_sum(dq1, dq4, dq16), cosf, s1, s2).astype(BF16)
        dproj_o[:, 512:1024] = _rope_bwd(unperm_sum(dk1, dk4, dk16), cosf, s1, s2).astype(BF16)
        dproj_o[:, 1024:1536] = unperm_sum(dv1, dv4, dv16).astype(BF16)
        dproj_o[:, 1536:2048] = dag_r[...]

    tok = lambda w: pl.BlockSpec((TT, w), lambda i: (i, 0))
    d4 = pl.BlockSpec((4, TT // 4, AW), lambda i: (0, i, 0))
    d16 = pl.BlockSpec((16, TT // 16, AW), lambda i: (0, i, 0))
    return pl.pallas_call(
        body, name="dproj_build", grid=(T // TT,),
        in_specs=[tok(AW), d4, d16] * 3 + [tok(AW), tok(1)],
        out_specs=tok(NCOL // 2),
        out_shape=jax.ShapeDtypeStruct((T, NCOL // 2), BF16),
        scratch_shapes=[pltpu.VMEM((4, TT, 128), F32)] * 3,
        compiler_params=_cp(("parallel",)),
    )(*dq, *dk, *dv, dag, pos)


def _bwd_x(dproj_a, dproj_h, x, dx2, mixw, w_full, rin, rinb, small4, small6, pout_own, pout_rem):
    TT = 256
    NT = T // TT

    def body(dpa_r, dph_r, x_r, dx2_r, mw_r, w_r, rin_r, rinb_r, s4_r, s6_r, poo_r, por_r,
             gx_o, pin_o, pinr_o, sall_o, fin_o, fout_o, sbuf, v_own, v_rem, vo_own, vo_rem, sin, sout, got_in,
             got_out, send_sems, recv_sems, loc_sems, share_send, share_recv, fin_sems):
        i = pl.program_id(0)
        loc, rem = _chip_copies(_w_in_piece, rin_r, rinb_r, pin_o, pinr_o, send_sems, recv_sems, loc_sems.at[0])

        @pl.when(i == 0)
        def _():
            sbuf[...] = jnp.zeros_like(sbuf)
            for cp in loc + rem:
                cp.start()

        dhn = _mm_nt(dpa_r[...], w_r[:, 0:NCOL // 2]) + _mm_nt(dph_r[...], w_r[:, NCOL // 2:NCOL])
        xv = x_r[...]
        r = lax.rsqrt(jnp.mean(xv * xv, axis=-1, keepdims=True) + EPS)
        dxw = dhn * mw_r[...]
        gx_o[...] = dx2_r[...] + r * dxw - xv * ((r * r * r) * jnp.mean(dxw * xv, axis=-1, keepdims=True))
        sbuf[16:17, :] += jnp.sum(dhn * (xv * r), axis=0, keepdims=True)

        @pl.when(i == NT - 1)
        def _():
            sbuf[0:8, :] = s4_r[...]
            sbuf[8:16, :] = s6_r[...]
            sloc, srem = _small_copies(sbuf, sall_o, send_sems, recv_sems, loc_sems.at[1])
            for cp in sloc + srem:
                cp.start()
            for cp in rem:
                cp.wait_recv()
            for cp in rem:
                cp.wait_send()
            for cp in loc:
                cp.wait()
            mx, my, c = lax.axis_index("x"), lax.axis_index("y"), lax.axis_index("c")
            loads = [pltpu.make_async_copy(pin_o, v_own, fin_sems.at[0]),
                     pltpu.make_async_copy(pinr_o, v_rem, fin_sems.at[1]),
                     pltpu.make_async_copy(poo_r, vo_own, fin_sems.at[2]),
                     pltpu.make_async_copy(por_r, vo_rem, fin_sems.at[3])]
            for cp in loads:
                cp.start()
            for cp in loads:
                cp.wait()
            sout[...] = ((vo_own[...] + vo_rem[0].astype(F32)) + vo_rem[1].astype(F32)) + vo_rem[2].astype(F32)
            sin[...] = ((v_own[...] + v_rem[0].astype(F32)) + v_rem[1].astype(F32)) + v_rem[2].astype(F32)
            swap = [pltpu.make_async_remote_copy(src_ref=sin, dst_ref=got_in, send_sem=share_send.at[0],
                                                 recv_sem=share_recv.at[0], device_id=(mx, my, 1 - c),
                                                 device_id_type=MESH),
                    pltpu.make_async_remote_copy(src_ref=sout, dst_ref=got_out, send_sem=share_send.at[1],
                                                 recv_sem=share_recv.at[1], device_id=(mx, my, 1 - c),
                                                 device_id_type=MESH)]
            for cp in swap:
                cp.start()
            mine = [pltpu.make_async_copy(sin, fin_o.at[c], fin_sems.at[0]),
                    pltpu.make_async_copy(sout, fout_o.at[c], fin_sems.at[1])]
            for cp in mine:
                cp.start()
            for cp in swap:
                cp.wait_recv()
            theirs = [pltpu.make_async_copy(got_in, fin_o.at[1 - c], fin_sems.at[2]),
                      pltpu.make_async_copy(got_out, fout_o.at[1 - c], fin_sems.at[3])]
            for cp in theirs:
                cp.start()
            for cp in swap:
                cp.wait_send()
            for cp in mine + theirs:
                cp.wait()
            for cp in srem:
                cp.wait_recv()
            for cp in srem:
                cp.wait_send()
            for cp in sloc:
                cp.wait()

    tok = lambda w: pl.BlockSpec((TT, w), lambda i: (i, 0))
    const = lambda shape: pl.BlockSpec(shape, lambda i: (0,) * len(shape))
    hbm = pl.BlockSpec(memory_space=pltpu.HBM)
    return pl.pallas_call(
        body, name="bwd_x", grid=(NT,),
        in_specs=[tok(NCOL // 2), tok(NCOL // 2), tok(D), tok(D), const((1, D)), const((D, NCOL)), hbm, hbm,
                  const((8, D)), const((8, D)), hbm, hbm],
        out_specs=[tok(D), hbm, hbm, hbm, hbm, hbm],
        out_shape=[jax.ShapeDtypeStruct((T, D), F32),
                   jax.ShapeDtypeStruct((512, 1024), F32), jax.ShapeDtypeStruct((3, 512, 1024), BF16),
                   jax.ShapeDtypeStruct((8, 24, D), F32),
                   jax.ShapeDtypeStruct((2, 512, 1024), F32), jax.ShapeDtypeStruct((2, 128, D), F32)],
        scratch_shapes=[pltpu.VMEM((24, D), F32),
                        pltpu.VMEM((512, 1024), F32), pltpu.VMEM((3, 512, 1024), BF16),
                        pltpu.VMEM((128, D), F32), pltpu.VMEM((3, 128, D), BF16),
                        pltpu.VMEM((512, 1024), F32), pltpu.VMEM((128, D), F32),
                        pltpu.VMEM((512, 1024), F32), pltpu.VMEM((128, D), F32),
                        pltpu.SemaphoreType.DMA((10,)), pltpu.SemaphoreType.DMA((10,)), pltpu.SemaphoreType.DMA((2,)),
                        pltpu.SemaphoreType.DMA((2,)), pltpu.SemaphoreType.DMA((2,)), pltpu.SemaphoreType.DMA((4,))],
        compiler_params=_cp(("arbitrary",)),
    )(dproj_a, dproj_h, x, dx2, mixw, w_full, rin, rinb, small4, small6, pout_own, pout_rem)


def _grad_w_in(hn, dproj_a, dproj_h):
    TK = 2048
    NK = T // TK

    def body(hnt_r, dpa_r, dph_r, rin_o, rinb_o, acc, rbuf, obuf, obufb, send_sems, recv_sems, wb_sems):
        j = pl.program_id(0)
        kk = pl.program_id(1)
        x, y, c = lax.axis_index("x"), lax.axis_index("y"), lax.axis_index("c")
        mine = pl.ds(pl.multiple_of(c * 512, 512), 512)
        theirs = pl.ds(pl.multiple_of((1 - c) * 512, 512), 512)

        def send(jj):
            return pltpu.make_async_remote_copy(
                src_ref=acc.at[jj % 2, theirs, :], dst_ref=rbuf.at[jj], send_sem=send_sems.at[jj],
                recv_sem=recv_sems.at[jj], device_id=(x, y, 1 - c), device_id_type=MESH)

        def writeback(jj):
            cols = pl.ds(jj * 1024, 1024)
            return [pltpu.make_async_copy(obuf.at[jj % 2], rin_o.at[:, cols], wb_sems.at[jj % 2]),
                    pltpu.make_async_copy(obufb.at[jj % 2], rinb_o.at[:, cols], wb_sems.at[2 + jj % 2])]

        def wait_writeback(jj):
            for cp in writeback(jj):
                cp.wait()

        def finalize(jj):
            send(jj).wait_recv()
            red = acc[jj % 2, mine, :] + rbuf[jj]
            obuf[jj % 2] = red
            obufb[jj % 2] = red.astype(BF16)
            for cp in writeback(jj):
                cp.start()

        prod = _mm(hnt_r[...], jnp.where(j < 2, dpa_r[...], dph_r[...]))

        @pl.when(kk == 0)
        def _():
            for jj in (2, 3):
                @pl.when(j == jj)
                def _():
                    send(jj - 2).wait_send()
            acc[j % 2] = prod

        @pl.when(kk > 0)
        def _():
            acc[j % 2] += prod

        @pl.when(kk == NK - 1)
        def _():
            for jj in range(4):
                @pl.when(j == jj)
                def _():
                    send(jj).start()
                    if jj in (1, 2):
                        finalize(jj - 1)
                    if jj == 3:
                        wait_writeback(0)
                        finalize(2)
                        wait_writeback(1)
                        finalize(3)
                        wait_writeback(2)
                        wait_writeback(3)
                        send(2).wait_send()
                        send(3).wait_send()

    hbm = pl.BlockSpec(memory_space=pltpu.HBM)
    return pl.pallas_call(
        body, name="grad_w_in", grid=(4, NK),
        in_specs=[pl.BlockSpec((D, TK), lambda j, kk: (0, kk)),
                  pl.BlockSpec((TK, 1024), lambda j, kk: (jnp.where(j < 2, kk, NK - 1), jnp.minimum(j, 1))),
                  pl.BlockSpec((TK, 1024), lambda j, kk: (jnp.where(j < 2, 0, kk), jnp.maximum(j - 2, 0)))],
        out_specs=[hbm, hbm],
        out_shape=[jax.ShapeDtypeStruct((512, NCOL), F32), jax.ShapeDtypeStruct((512, NCOL), BF16)],
        scratch_shapes=[pltpu.VMEM((2, D, 1024), F32), pltpu.VMEM((4, 512, 1024), F32), pltpu.VMEM((2, 512, 1024), F32),
                        pltpu.VMEM((2, 512, 1024), BF16),
                        pltpu.SemaphoreType.DMA((4,)), pltpu.SemaphoreType.DMA((4,)), pltpu.SemaphoreType.DMA((4,))],
        compiler_params=_cp(("arbitrary", "arbitrary")),
    )(hn, dproj_a, dproj_h)


def _w_in_piece(ref, j):
    return ref.at[:, pl.ds(j * 1024, 1024)]


def _w_out_piece(ref, j):
    return ref.at[pl.ds(j * 128, 128), :]


def _chip_copies(piece, src_r, srcb_r, own_o, rem_o, send_sems, recv_sems, loc_sem):
    x, y, c = lax.axis_index("x"), lax.axis_index("y"), lax.axis_index("c")
    chips = [(1 - x, y), (x, 1 - y), (1 - x, 1 - y)]
    loc = [pltpu.make_async_copy(piece(src_r, 2 * x + y), own_o, loc_sem)]
    rem = [pltpu.make_async_remote_copy(
        src_ref=piece(srcb_r, 2 * px + py), dst_ref=rem_o.at[k], send_sem=send_sems.at[k],
        recv_sem=recv_sems.at[k], device_id=(px, py, c), device_id_type=MESH) for k, (px, py) in enumerate(chips)]
    return loc, rem


def _small_copies(small_r, sall_o, send_sems, recv_sems, loc_sem):
    x, y, c = lax.axis_index("x"), lax.axis_index("y"), lax.axis_index("c")
    me = 4 * x + 2 * y + c
    loc = [pltpu.make_async_copy(small_r, sall_o.at[me], loc_sem)]
    rem = []
    k = 3
    for fx in range(2):
        for fy in range(2):
            for fc in range(2):
                if fx or fy or fc:
                    peer = (1 - x if fx else x, 1 - y if fy else y, 1 - c if fc else c)
                    rem.append(pltpu.make_async_remote_copy(
                        src_ref=small_r, dst_ref=sall_o.at[me], send_sem=send_sems.at[k],
                        recv_sem=recv_sems.at[k], device_id=peer, device_id_type=MESH))
                    k += 1
    return loc, rem


def _adamw_math(w, g, m, v):
    m = B1 * m + (1.0 - B1) * g
    v = B2 * v + (1.0 - B2) * (g * g)
    m_hat = m / (1.0 - B1 ** STEP)
    v_hat = v / (1.0 - B2 ** STEP)
    delta = -LR * (m_hat / (jnp.sqrt(v_hat) + AEPS) + WD * w)
    return delta, m, v


def _adamw(big_in, big_out, sall, params):
    def body(*refs):
        wi, gi, mi, vi, wo, go, mo, vo, sall_r = refs[:9]
        ins = refs[9:24]
        di_o, mi_o, vi_o, do_o, mo_o, vo_o = refs[24:30]
        outs = refs[30:]
        d, mm, vv = _adamw_math(wi[...], gi[...], mi[...], vi[...])
        di_o[...] = d
        mi_o[...] = mm
        vi_o[...] = vv

        @pl.when(pl.program_id(0) == 0)
        def _():
            d, mm, vv = _adamw_math(wo[...], go[...], mo[...], vo[...])
            do_o[...] = d
            mo_o[...] = mm
            vo_o[...] = vv
            tot = sall_r[0]
            for dv in range(1, 8):
                tot = tot + sall_r[dv]
            grads = [tot[16:17, :], tot[1:2, 0:AW], tot[1:2, AW:], tot[8:10, 0:HW], tot[0:1, :]]
            outs[0][...] = tot[2:3, 0:1]
            for p in range(5):
                w_r, m_r, v_r = ins[3 * p:3 * p + 3]
                g = grads[p]
                d, mm, vv = _adamw_math(w_r[...], g, m_r[...], v_r[...])
                outs[1 + 4 * p][...] = g
                outs[2 + 4 * p][...] = d
                outs[3 + 4 * p][...] = mm
                outs[4 + 4 * p][...] = vv

    flat = [a for p in params for a in p]
    shapes = [jax.ShapeDtypeStruct((D, 1024), F32)] * 3 + [jax.ShapeDtypeStruct((256, D), F32)] * 3
    shapes += [jax.ShapeDtypeStruct((1, 1), F32)]
    for p in params:
        shapes += [jax.ShapeDtypeStruct(p[0].shape, F32)] * 4
    vm = pl.BlockSpec(memory_space=pltpu.VMEM)
    rows = pl.BlockSpec((256, 1024), lambda i: (i, 0))
    whole = pl.BlockSpec((256, D), lambda i: (0, 0))
    return pl.pallas_call(
        body, name="adamw", grid=(4,),
        in_specs=[rows] * 4 + [whole] * 4 + [vm] * 16, out_specs=[rows] * 3 + [whole] * 3 + [vm] * 21,
        out_shape=shapes,
        compiler_params=_cp(("arbitrary",)),
    )(*big_in, *big_out, sall, *flat)


def kernel(x, positions, w_in, w_out, mix_norm_w, attn_out_norm_w, hgrn_out_norm_w, hgrn_lb_raw, final_norm_w, loss_target, m_w_in, m_w_out, m_mix_norm_w, m_attn_out_norm_w, m_hgrn_out_norm_w, m_hgrn_lb_raw, m_final_norm_w, v_w_in, v_w_out, v_mix_norm_w, v_attn_out_norm_w, v_hgrn_out_norm_w, v_hgrn_lb_raw, v_final_norm_w):
    xs = x.reshape(T, D)
    tgt = loss_target.reshape(T, D)
    pos = positions.reshape(T, 1)
    fnw = final_norm_w.reshape(1, D)

    ti = np.arange(TH)
    tri_np = ((ti[:, None] // CHUNK == ti[None, :] // CHUNK) & (ti[None, :] <= ti[:, None])).astype(np.float32)
    tri = jnp.asarray(tri_np, BF16)
    trit = jnp.asarray(tri_np.T, BF16)
    hi_ = np.arange(AW) // HEAD
    gmat = jnp.asarray((hi_[:256, None] == hi_[None, :256]).astype(np.float32) / HEAD, BF16)
    emat_np = (np.arange(128)[:, None] == hi_[None, :]).astype(np.float32)
    sel_np = (8 + hi_[:, None] == np.arange(128)[None, :]).astype(np.float32)
    emat = jnp.asarray(emat_np, BF16)
    selmat = jnp.asarray(sel_np, BF16)

    jm_arr = (2 * lax.axis_index("x") + lax.axis_index("y")).astype(jnp.int32).reshape(1)
    (hn, q1, k1, v1, q4, k4, v4, q16, k16, v16, ag, hq, hf, hi, hg, w_full, wout4) = _fwd_in(
        xs, pos, mix_norm_w, w_in.reshape(D, 1024), w_out.reshape(256, D), jm_arr)
    wout_full = wout4.reshape(D, D)
    flat = lambda a: a.reshape(T, AW)
    o1, l1 = _attn_fwd(q1, k1, v1, T // BLK, "attn_fwd_d1")
    o4, l4 = _attn_fwd(flat(q4), flat(k4), flat(v4), T // 4 // BLK, "attn_fwd_d4")
    o16, l16 = _attn_fwd(flat(q16), flat(k16), flat(v16), T // 16 // BLK, "attn_fwd_d16")
    rec, sall = _hgrn_fwd(hq, hf, hi, hgrn_lb_raw, tri)

    (dx2, do1, do4, do16, st1, st4, st16, drec, dag, dhg, rout, routb, small4) = _fwd_out(
        o1, o4.reshape(4, T // 4, AW), o16.reshape(16, T // 16, AW),
        l1, l4.reshape(4, T // 4, 128), l16.reshape(16, T // 16, 128),
        rec, ag, hg, xs, tgt, attn_out_norm_w, hgrn_out_norm_w, fnw, wout_full, gmat, emat, selmat)

    fst = lambda a: a.reshape(T, 128)
    dq1, dk1, dv1 = _attn_bwd(q1, k1, v1, do1, st1, T // BLK, "attn_bwd_d1")
    dq4, dk4, dv4 = _attn_bwd(flat(q4), flat(k4), flat(v4), flat(do4), fst(st4), T // 4 // BLK, "attn_bwd_d4")
    dq16, dk16, dv16 = _attn_bwd(flat(q16), flat(k16), flat(v16), flat(do16), fst(st16), T // 16 // BLK,
                                 "attn_bwd_d16")
    dproj_h, small6, pout_own, pout_rem = _hgrn_bwd(hq, hf, hi, hgrn_lb_raw, tri, trit, drec, sall, dhg,
                                                    rout, routb)

    r4 = lambda a: a.reshape(4, T // 4, AW)
    r16 = lambda a: a.reshape(16, T // 16, AW)
    dproj_a = _dproj_build((dq1, r4(dq4), r16(dq16)), (dk1, r4(dk4), r16(dk16)), (dv1, r4(dv4), r16(dv16)),
                           dag, pos)
    rin, rinb = _grad_w_in(hn, dproj_a, dproj_h)
    gx, _, _, small_all, fin, fout = _bwd_x(dproj_a, dproj_h, xs, dx2, mix_norm_w, w_full, rin, rinb,
                                            small4, small6, pout_own, pout_rem)
    g_w_in = fin.reshape(D, 1024)
    g_w_out = fout.reshape(256, D)

    params = [(mix_norm_w, m_mix_norm_w, v_mix_norm_w),
              (attn_out_norm_w, m_attn_out_norm_w, v_attn_out_norm_w),
              (hgrn_out_norm_w, m_hgrn_out_norm_w, v_hgrn_out_norm_w),
              (hgrn_lb_raw, m_hgrn_lb_raw, v_hgrn_lb_raw),
              (fnw, m_final_norm_w.reshape(1, D), v_final_norm_w.reshape(1, D))]
    d_in, nm_in, nv_in, d_out, nm_out, nv_out, *so = _adamw(
        (w_in.reshape(D, 1024), g_w_in, m_w_in.reshape(D, 1024), v_w_in.reshape(D, 1024)),
        (w_out.reshape(256, D), g_w_out, m_w_out.reshape(256, D), v_w_out.reshape(256, D)), small_all, params)
    loss = so[0].reshape(())
    g_s = [so[1 + 4 * p] for p in range(5)]
    d_s = [so[2 + 4 * p] for p in range(5)]
    m_s = [so[3 + 4 * p] for p in range(5)]
    v_s = [so[4 + 4 * p] for p in range(5)]
    for lst in (g_s, d_s, m_s, v_s):
        lst[4] = lst[4].reshape(D)

    return (loss, gx.reshape(1, T, D),
            g_w_in.reshape(1, D, 1024), g_w_out.reshape(1, 256, D), *g_s,
            d_in.reshape(1, D, 1024), d_out.reshape(1, 256, D), *d_s,
            nm_in.reshape(1, D, 1024), nm_out.reshape(1, 256, D), *m_s,
            nv_in.reshape(1, D, 1024), nv_out.reshape(1, 256, D), *v_s)
```

```python
import functools

import numpy as np
import jax
import jax.numpy as jnp
from jax import lax
from jax.experimental import pallas as pl
from jax.experimental.pallas import tpu as pltpu

F32 = jnp.float32
BF16 = jnp.bfloat16

T = 4096
D = 1024
AW = 512
HW = 512
NCOL = 4096
HEAD = 64
BLK = 128
CHUNK = 64
EPS = 1e-6
SCALE = HEAD ** -0.5
NEG = -1e30
ROPE_THETA = 500000.0
INV_FREQ = [float(v) for v in
            (np.float32(ROPE_THETA) ** (-(np.arange(8, dtype=np.float32)) * np.float32(0.125)))]
LR, B1, B2, AEPS, WD, STEP = 0.001, 0.9, 0.999, 1e-08, 0.01, 10
VMEM_LIMIT = 63 * 1024 * 1024
MESH = pl.DeviceIdType.MESH


def _cp(sem=None, **kw):
    return pltpu.CompilerParams(dimension_semantics=sem, vmem_limit_bytes=VMEM_LIMIT, **kw)


def _mm(a, b):
    return jnp.dot(a, b, preferred_element_type=F32)


def _mm_nt(a, b):
    return lax.dot_general(a, b, (((1,), (1,)), ((), ())), preferred_element_type=F32)


def _mm_tn(a, b):
    return lax.dot_general(a, b, (((0,), (0,)), ((), ())), preferred_element_type=F32)


def _mm_exact_l(mat_bf, x):
    h = x.astype(BF16)
    l = (x - h.astype(F32)).astype(BF16)
    return _mm(mat_bf, h) + _mm(mat_bf, l)


def _mm_exact_r(x, mat_bf):
    h = x.astype(BF16)
    l = (x - h.astype(F32)).astype(BF16)
    return _mm(h, mat_bf) + _mm(l, mat_bf)


def _sigmoid(x):
    return 0.5 * jnp.tanh(0.5 * x) + 0.5


def _rope_tables(pos):
    lane = lax.broadcasted_iota(jnp.int32, (1, 128), 1)
    jl = lane & 63
    fi = jl & 7
    inv = jnp.zeros((1, 128), F32)
    for kk in range(8):
        inv = jnp.where(fi == kk, INV_FREQ[kk], inv)
    ang = pos.astype(F32) * inv
    c = jnp.cos(ang)
    s = jnp.sin(ang)
    cosf = jnp.where(jl < 16, c, 1.0)
    s1 = jnp.where(jl < 8, -s, 0.0)
    s2 = jnp.where((jl >= 8) & (jl < 16), s, 0.0)
    return cosf, s1, s2


def _rope(t, cosf, s1, s2):
    parts = []
    for ci in range(t.shape[1] // 128):
        tc = t[:, ci * 128:(ci + 1) * 128]
        parts.append(tc * cosf + pltpu.roll(tc, 120, 1) * s1 + pltpu.roll(tc, 8, 1) * s2)
    return jnp.concatenate(parts, axis=1)


def _rope_bwd(g, cosf, s1, s2):
    parts = []
    for ci in range(g.shape[1] // 128):
        gc = g[:, ci * 128:(ci + 1) * 128]
        parts.append(gc * cosf + pltpu.roll(gc * s1, 8, 1) + pltpu.roll(gc * s2, 120, 1))
    return jnp.concatenate(parts, axis=1)


def _perm_store(val, scr, scr2, o1, o4, o16, dt):
    n = val.shape[0]
    q = n // 4
    o1[...] = val.astype(dt)
    for ci in range(val.shape[1] // 128):
        cs = slice(ci * 128, (ci + 1) * 128)
        scr[ci] = val[:, cs]
        for r4 in range(4):
            part = scr[ci, pl.ds(r4, q, stride=4), :]
            o4[r4, :, cs] = part.astype(dt)
            scr2[ci, r4 * q:(r4 + 1) * q, :] = part
        for r4 in range(4):
            for b in range(4):
                o16[r4 + 4 * b, :, cs] = scr2[ci, pl.ds(r4 * q + b, q // 4, stride=4), :].astype(dt)


def _unperm_load(r4, r16, scr_a, scr_b, scr_c):
    n = scr_a.shape[1]
    q = n // 4
    nc = r4.shape[-1] // 128
    for ci in range(nc):
        cs = slice(ci * 128, (ci + 1) * 128)
        for rr in range(4):
            scr_a[ci, pl.ds(rr, q, stride=4), :] = r4[rr, :, cs].astype(F32)
        for rr in range(4):
            for b in range(4):
                scr_c[ci, pl.ds(rr * q + b, q // 4, stride=4), :] = r16[rr + 4 * b, :, cs].astype(F32)
        for rr in range(4):
            scr_b[ci, pl.ds(rr, q, stride=4), :] = scr_c[ci, rr * q:(rr + 1) * q, :]
    return (jnp.concatenate([scr_a[ci] for ci in range(nc)], axis=1),
            jnp.concatenate([scr_b[ci] for ci in range(nc)], axis=1))


def _unperm_sum(r4, r16, scr_b, scr_c):
    n = scr_b.shape[1]
    q = n // 4
    nc = r4.shape[-1] // 128
    for ci in range(nc):
        cs = slice(ci * 128, (ci + 1) * 128)
        for rr in range(4):
            for b in range(4):
                scr_c[ci, pl.ds(rr * q + b, q // 4, stride=4), :] = r16[rr + 4 * b, :, cs].astype(F32)
        for rr in range(4):
            scr_b[ci, pl.ds(rr, q, stride=4), :] = scr_c[ci, rr * q:(rr + 1) * q, :] + r4[rr, :, cs].astype(F32)
    return jnp.concatenate([scr_b[ci] for ci in range(nc)], axis=1)


def _fwd_in(x, pos, mixw, w_in, w_out, jm_arr):
    TT = 512
    NT = T // TT

    def body(jm_ref, x_ref, pos_ref, mw_ref, win_ref, wout_ref,
             hnt_ref, q1, k1, v1, q4, k4, v4, q16, k16, v16, ag, hq, hf, hi, hg, wfull_o, woutfull_o,
             wbuf, wobuf, hn_all, scr, scr2, stage, send_sems, recv_sems, loc_sems):
        s = pl.program_id(0)
        i = pl.program_id(1)
        mx, my, c = lax.axis_index("x"), lax.axis_index("y"), lax.axis_index("c")
        me, sibling = (mx, my, c), (mx, my, 1 - c)
        chips = [(mx, 1 - my), (1 - mx, my), (1 - mx, 1 - my)]
        jm = 2 * mx + my
        rows_in = [pl.ds(pl.multiple_of(h * 512, 512), 512) for h in (c, 1 - c)]
        rows_out = [pl.ds(pl.multiple_of(h * 128, 128), 128) for h in (c, 1 - c)]

        def blk(k):
            return lax.bitwise_xor(jm, k + 1)

        def rc(n, ref, to):
            return pltpu.make_async_remote_copy(src_ref=ref, dst_ref=ref, send_sem=send_sems.at[n],
                                                recv_sem=recv_sems.at[n], device_id=to, device_id_type=MESH)

        halves = [pl.ds(0, 512), pl.ds(512, 512)]
        send_in = lambda k, h: rc(12 + 2 * k + h, wbuf.at[jm, rows_in[0], halves[h]], (*chips[k], c))
        got_in = lambda k, h: rc(12 + 2 * k + h, wbuf.at[blk(k), rows_in[0], halves[h]], me)
        relay = lambda h: rc(16 + h, wbuf.at[blk(h), rows_in[0], halves[h]], (*chips[1 - h], c))
        got_relay = lambda h: rc(16 + h, wbuf.at[blk(2), rows_in[0], halves[h]], me)
        send_out = lambda k: rc(3 + k, wobuf.at[jm, rows_out[0], :], (*chips[k], c))
        got_out = lambda k: rc(3 + k, wobuf.at[blk(k), rows_out[0], :], me)
        pass_in = lambda k: rc(6 + k, wbuf.at[blk(k), rows_in[0], :], sibling)
        pass_out = lambda k: rc(9 + k, wobuf.at[blk(k), rows_out[0], :], sibling)
        passed_in = lambda k: rc(6 + k, wbuf.at[blk(k), rows_in[1], :], me)
        passed_out = lambda k: rc(9 + k, wobuf.at[blk(k), rows_out[1], :], me)

        def keep(j, n):
            return pltpu.make_async_copy(wbuf.at[j], wfull_o.at[:, pl.ds(j * 1024, 1024)], loc_sems.at[n])

        @pl.when((s == 0) & (i == 0))
        def _():
            for p in range(5):
                src = win_ref.at[pl.ds(p * 256, 256), :] if p < 4 else wout_ref
                load = pltpu.make_async_copy(src, stage, loc_sems.at[4])
                load.start()
                load.wait()
                if p < 4:
                    wbuf[jm, p * 256:(p + 1) * 256, :] = stage[...].astype(BF16)
                else:
                    wobuf[jm] = stage[...].astype(BF16)
            for k in range(2):
                for h in range(2):
                    send_in(k, h).start()
            keep(jm, 0).start()

        def arrive(k):
            if k == 0:
                for kk in range(2):
                    for h in range(2):
                        got_in(kk, h).wait_recv()
                relay(0).start()
                relay(1).start()
            if k == 2:
                got_relay(0).wait_recv()
                got_relay(1).wait_recv()
            pass_in(k).start()
            passed_in(k).wait_recv()
            keep(blk(k), k + 1).start()
            if k == 2:
                for kk in range(3):
                    send_out(kk).start()

        pl.when((s == 1) & (i == 0))(functools.partial(arrive, 0))

        @pl.when((s == 2) & (i == 0))
        def _():
            arrive(1)
            arrive(2)

        tile = pl.ds(pl.multiple_of(i * TT, TT), TT)

        @pl.when(s == 0)
        def _():
            xv = x_ref[...]
            r = lax.rsqrt(jnp.mean(xv * xv, axis=-1, keepdims=True) + EPS)
            hnf = (xv * r) * mw_ref[...]
            hn_all[tile, :] = hnf.astype(BF16)
            hnt_ref[...] = hnf.T.astype(BF16)

        def project(jj):
            hn = hn_all[tile, :]
            lo = _mm(hn, wbuf[jj, :, 0:512])
            hi_cols = _mm(hn, wbuf[jj, :, 512:1024])
            if jj == 0:
                cosf, s1, s2 = _rope_tables(pos_ref[...])
                _perm_store(_rope(lo, cosf, s1, s2), scr, scr2, q1, q4, q16, BF16)
                _perm_store(_rope(hi_cols, cosf, s1, s2), scr, scr2, k1, k4, k16, BF16)
            elif jj == 1:
                _perm_store(lo, scr, scr2, v1, v4, v16, BF16)
                ag[...] = hi_cols.astype(BF16)
            elif jj == 2:
                hq[...] = lo.astype(BF16)
                hf[...] = hi_cols.astype(BF16)
            else:
                hi[...] = lo.astype(BF16)
                hg[...] = hi_cols.astype(BF16)

        def project_block(j):
            for jj in range(4):
                pl.when(j == jj)(functools.partial(project, jj))

        @pl.when(s < 2)
        def _():
            project_block(lax.bitwise_xor(jm, s))

        @pl.when(s == 2)
        def _():
            project_block(lax.bitwise_xor(jm, 2))
            project_block(lax.bitwise_xor(jm, 3))

        @pl.when((s == 2) & (i == NT - 1))
        def _():
            for k in range(3):
                got_out(k).wait_recv()
                pass_out(k).start()
            for k in range(3):
                passed_out(k).wait_recv()
            out = pltpu.make_async_copy(wobuf, woutfull_o, loc_sems.at[4])
            out.start()
            for h in range(2):
                relay(h).wait_send()
                for k in range(2):
                    send_in(k, h).wait_send()
            for k in range(3):
                send_out(k).wait_send()
                pass_in(k).wait_send()
                pass_out(k).wait_send()
            keep(jm, 0).wait()
            for k in range(3):
                keep(blk(k), k + 1).wait()
            out.wait()

    def at_stage_of(jb):
        def index(s, i, jm_ref):
            sa = jnp.minimum(lax.bitwise_xor(jm_ref[0], jb), 2)
            return jnp.where(s < sa, 0, jnp.where(s == sa, i, NT - 1))
        return index

    tok = lambda w, jb: pl.BlockSpec((TT, w), lambda s, i, jm_ref: (at_stage_of(jb)(s, i, jm_ref), 0))
    d4 = lambda jb: pl.BlockSpec((4, TT // 4, AW), lambda s, i, jm_ref: (0, at_stage_of(jb)(s, i, jm_ref), 0))
    d16 = lambda jb: pl.BlockSpec((16, TT // 16, AW), lambda s, i, jm_ref: (0, at_stage_of(jb)(s, i, jm_ref), 0))
    hbm = pl.BlockSpec(memory_space=pltpu.HBM)
    sd = lambda shape, dt: jax.ShapeDtypeStruct(shape, dt)
    in_own_stage = lambda s, i: jnp.where(s == 0, i, NT - 1)
    grid_spec = pltpu.PrefetchScalarGridSpec(
        num_scalar_prefetch=1, grid=(3, NT),
        in_specs=[pl.BlockSpec((TT, D), lambda s, i, jm_ref: (in_own_stage(s, i), 0)),
                  pl.BlockSpec((TT, 1), lambda s, i, jm_ref: (i, 0)),
                  pl.BlockSpec((1, D), lambda s, i, jm_ref: (0, 0)), hbm, hbm],
        out_specs=[pl.BlockSpec((D, TT), lambda s, i, jm_ref: (0, in_own_stage(s, i))),
                   tok(AW, 0), tok(AW, 0), tok(AW, 1), d4(0), d4(0), d4(1), d16(0), d16(0), d16(1),
                   tok(AW, 1), tok(AW, 2), tok(AW, 2), tok(AW, 3), tok(AW, 3), hbm, hbm],
        scratch_shapes=[pltpu.VMEM((4, D, 1024), BF16), pltpu.VMEM((4, 256, D), BF16), pltpu.VMEM((T, D), BF16),
                        pltpu.VMEM((4, TT, 128), F32), pltpu.VMEM((4, TT, 128), F32), pltpu.VMEM((256, 1024), F32),
                        pltpu.SemaphoreType.DMA((18,)),
                        pltpu.SemaphoreType.DMA((18,)), pltpu.SemaphoreType.DMA((6,))])
    return pl.pallas_call(
        body, name="fwd_in", grid_spec=grid_spec,
        out_shape=[sd((D, T), BF16)] + [sd((T, AW), BF16)] * 3 + [sd((4, T // 4, AW), BF16)] * 3
        + [sd((16, T // 16, AW), BF16)] * 3
        + [sd((T, AW), BF16)] * 5 + [sd((D, NCOL), BF16), sd((4, 256, D), BF16)],
        compiler_params=_cp(("arbitrary", "arbitrary")),
    )(jm_arr, x, pos, mixw, w_in, w_out)


def _band_mask(key_axis, nkeys=2 * BLK):
    shape = (nkeys, 2 * BLK) if key_axis == 0 else (2 * BLK, nkeys)
    kj = lax.broadcasted_iota(jnp.int32, shape, key_axis)
    qi = lax.broadcasted_iota(jnp.int32, shape, 1 - key_axis) & (BLK - 1)
    return (kj >= qi) & (kj <= qi + BLK), kj, qi


def _stack_heads(t2, in_a):
    z = jnp.zeros_like(t2)
    return jnp.concatenate([jnp.where(in_a[0], t2, z), jnp.where(in_a[1], t2, z)], axis=0)


def _attn_fwd(q, k, v, nb, name):
    n = 8
    CH = n * BLK
    halo = nb > n

    def body(*refs):
        if halo:
            q_ref, k_ref, v_ref, kp_ref, vp_ref, o_ref, lse_ref = refs
        else:
            q_ref, k_ref, v_ref, o_ref, lse_ref = refs
        lane = lax.broadcasted_iota(jnp.int32, (1, 128), 1)
        in_a = [lane < HEAD, lane >= HEAD]
        band, kj, _ = _band_mask(1)
        thr0 = jnp.where((n * pl.program_id(0)) % nb == 0, BLK, 0) if halo else BLK
        mask0 = band & (kj >= thr0)
        mask_first = band & (kj >= BLK)
        for b in range(n):
            rs = slice(b * BLK, (b + 1) * BLK)
            stat = jnp.zeros((BLK, 128), F32)
            for hp in range(4):
                cs = slice(hp * 128, (hp + 1) * 128)
                q2s = _stack_heads(q_ref[rs, cs], in_a)
                if b == 0:
                    kprev = kp_ref[:, cs] if halo else k_ref[rs, cs]
                    vprev = vp_ref[:, cs] if halo else v_ref[rs, cs]
                    kk = jnp.concatenate([kprev, k_ref[rs, cs]], axis=0)
                    vv = jnp.concatenate([vprev, v_ref[rs, cs]], axis=0)
                    mask = mask0
                else:
                    kk = k_ref[(b - 1) * BLK:(b + 1) * BLK, cs]
                    vv = v_ref[(b - 1) * BLK:(b + 1) * BLK, cs]
                    mask = mask_first if b % nb == 0 else band
                s = jnp.where(mask, _mm_nt(q2s, kk) * SCALE, NEG)
                m = jnp.max(s, axis=-1, keepdims=True)
                p = jnp.exp(s - m)
                l = jnp.sum(p, axis=-1, keepdims=True)
                o = _mm(p.astype(BF16), vv) / l
                lse = m + jnp.log(l)
                o_ref[rs, cs] = jnp.where(in_a[0], o[:BLK], o[BLK:]).astype(BF16)
                stat = jnp.where(lane == 2 * hp, lse[:BLK], stat)
                stat = jnp.where(lane == 2 * hp + 1, lse[BLK:], stat)
            lse_ref[rs, :] = stat

    cur = pl.BlockSpec((CH, AW), lambda i: (i, 0))
    prev = pl.BlockSpec((BLK, AW), lambda i: (jnp.maximum(n * i - 1, 0), 0))
    return pl.pallas_call(
        body, name=name, grid=(T // CH,),
        in_specs=[cur, cur, cur] + ([prev, prev] if halo else []),
        out_specs=[cur, pl.BlockSpec((CH, 128), lambda i: (i, 0))],
        out_shape=[jax.ShapeDtypeStruct((T, AW), BF16), jax.ShapeDtypeStruct((T, 128), F32)],
        compiler_params=_cp(("parallel",)),
    )(*((q, k, v) + ((k, v) if halo else ())))


def _attn_bwd(q, k, v, do, st, nb, name):
    n = 8
    CH = n * BLK
    NBLK = T // BLK
    halo = nb > n

    def body(*refs):
        if halo:
            (q_ref, k_ref, v_ref, do_ref, st_ref, kp_ref, vp_ref, qn_ref, don_ref, stn_ref,
             dq_ref, dk_ref, dv_ref) = refs
        else:
            q_ref, k_ref, v_ref, do_ref, st_ref, dq_ref, dk_ref, dv_ref = refs
        i = pl.program_id(0)
        lane = lax.broadcasted_iota(jnp.int32, (1, 128), 1)
        in_a = [lane < HEAD, lane >= HEAD]
        band, kj, _ = _band_mask(0)
        thr0 = jnp.where((n * i) % nb == 0, BLK, 0) if halo else BLK
        mask0 = band & (kj >= thr0)
        mask_first = band & (kj >= BLK)

        def stat_rows(st_t, hp):
            lse_r = jnp.concatenate([st_t[2 * hp:2 * hp + 1, :], st_t[2 * hp + 1:2 * hp + 2, :]], axis=1)
            dl_r = jnp.concatenate([st_t[8 + 2 * hp:9 + 2 * hp, :], st_t[9 + 2 * hp:10 + 2 * hp, :]], axis=1)
            return lse_r, dl_r

        st_t = [st_ref[b * BLK:(b + 1) * BLK, :].T for b in range(n)]
        if halo:
            nxt_thr = jnp.where((n * i + n) % nb == 0, 2 * BLK, 0)
            _, kj1, qi1 = _band_mask(0, BLK)
            mask_next = kj1 >= qi1 + nxt_thr
            stn_t = stn_ref[...].T

        for hp in range(4):
            cs = slice(hp * 128, (hp + 1) * 128)
            kb = [k_ref[b * BLK:(b + 1) * BLK, cs] for b in range(n)]
            vb = [v_ref[b * BLK:(b + 1) * BLK, cs] for b in range(n)]
            dk_acc = [jnp.zeros((BLK, 128), F32) for _ in range(n)]
            dv_acc = [jnp.zeros((BLK, 128), F32) for _ in range(n)]
            for b in range(n):
                rs = slice(b * BLK, (b + 1) * BLK)
                q2s = _stack_heads(q_ref[rs, cs], in_a)
                do2s = _stack_heads(do_ref[rs, cs], in_a)
                if b == 0:
                    kprev = kp_ref[:, cs] if halo else kb[0]
                    vprev = vp_ref[:, cs] if halo else vb[0]
                    mask = mask0
                else:
                    kprev, vprev, mask = kb[b - 1], vb[b - 1], (mask_first if b % nb == 0 else band)
                kk = jnp.concatenate([kprev, kb[b]], axis=0)
                vv = jnp.concatenate([vprev, vb[b]], axis=0)
                lse_r, dl_r = stat_rows(st_t[b], hp)
                s_t = jnp.where(mask, _mm_nt(kk, q2s) * SCALE, NEG)
                p_t = jnp.exp(s_t - lse_r)
                ds_t = (p_t * (_mm_nt(vv, do2s) - dl_r)).astype(BF16)
                dkk = _mm(ds_t, q2s) * SCALE
                dvv = _mm(p_t.astype(BF16), do2s)
                dqs = _mm_tn(ds_t, kk) * SCALE
                dq_ref[rs, cs] = jnp.where(in_a[0], dqs[:BLK], dqs[BLK:]).astype(BF16)
                dk_acc[b] += dkk[BLK:]
                dv_acc[b] += dvv[BLK:]
                if b > 0:
                    dk_acc[b - 1] += dkk[:BLK]
                    dv_acc[b - 1] += dvv[:BLK]
            if halo:
                q2s = _stack_heads(qn_ref[:, cs], in_a)
                do2s = _stack_heads(don_ref[:, cs], in_a)
                lse_r, dl_r = stat_rows(stn_t, hp)
                s_t = jnp.where(mask_next, _mm_nt(kb[n - 1], q2s) * SCALE, NEG)
                p_t = jnp.exp(s_t - lse_r)
                ds_t = (p_t * (_mm_nt(vb[n - 1], do2s) - dl_r)).astype(BF16)
                dk_acc[n - 1] += _mm(ds_t, q2s) * SCALE
                dv_acc[n - 1] += _mm(p_t.astype(BF16), do2s)
            for b in range(n):
                dk_ref[b * BLK:(b + 1) * BLK, cs] = dk_acc[b].astype(BF16)
                dv_ref[b * BLK:(b + 1) * BLK, cs] = dv_acc[b].astype(BF16)

    cur = pl.BlockSpec((CH, AW), lambda i: (i, 0))
    cur_st = pl.BlockSpec((CH, 128), lambda i: (i, 0))
    prev = pl.BlockSpec((BLK, AW), lambda i: (jnp.maximum(n * i - 1, 0), 0))
    nxt = pl.BlockSpec((BLK, AW), lambda i: (jnp.minimum(n * i + n, NBLK - 1), 0))
    nxt_st = pl.BlockSpec((BLK, 128), lambda i: (jnp.minimum(n * i + n, NBLK - 1), 0))
    ins = [cur] * 4 + [cur_st] + ([prev, prev, nxt, nxt, nxt_st] if halo else [])
    args = (q, k, v, do, st) + ((k, v, q, do, st) if halo else ())
    return pl.pallas_call(
        body, name=name, grid=(T // CH,),
        in_specs=ins,
        out_specs=[cur] * 3,
        out_shape=[jax.ShapeDtypeStruct((T, AW), BF16)] * 3,
        compiler_params=_cp(("parallel",)),
    )(*args)


TH = 256
NCH = TH // CHUNK


def _hgrn_common(hq_ref, hf_ref, lbr_ref, tri_ref):
    r0 = lbr_ref[0:1, :]
    r1 = lbr_ref[1:2, :]
    mx = jnp.maximum(r0, r1)
    e0 = jnp.exp(r0 - mx)
    e1 = jnp.exp(r1 - mx)
    lb = e0 / (e0 + e1)
    hqv = hq_ref[...].astype(F32)
    sq = _sigmoid(hqv)
    qv = hqv * sq
    sf = _sigmoid(hf_ref[...].astype(F32))
    f = lb + (1.0 - lb) * sf
    kv = 1.0 - f
    g = jnp.log(f)
    cum = _mm_exact_l(tri_ref[...], g)
    dec = jnp.exp(jnp.concatenate([cum[c * CHUNK + CHUNK - 1:(c + 1) * CHUNK, :] for c in range(NCH)], axis=0))
    decb = jnp.concatenate([jnp.broadcast_to(dec[c:c + 1, :], (CHUNK, HW)) for c in range(NCH)], axis=0)
    ea = jnp.exp(cum)
    ena = jnp.exp(-cum)
    eend = decb * ena
    return dict(lb=lb, hq=hqv, sq=sq, q=qv, sf=sf, f=f, k=kv, cum=cum, ea=ea, ena=ena, eend=eend,
                qd=qv * ea, ki=kv * ena, ke=kv * eend, dec=dec)


def _tri_mask(transposed=False):
    ti = lax.broadcasted_iota(jnp.int32, (TH, TH), 1 if transposed else 0)
    si = lax.broadcasted_iota(jnp.int32, (TH, TH), 0 if transposed else 1)
    return (si <= ti) & ((si // CHUNK) == (ti // CHUNK))


def _hgrn_fwd(hq, hf, hi, lbr, tri):
    NSUB = 2

    def body(hq_ref, hf_ref, hi_ref, lbr_ref, tri_ref, rec_ref, sall_ref, st_scr):
        @pl.when(pl.program_id(0) == 0)
        def _():
            st_scr[...] = jnp.zeros_like(st_scr)

        causal = _tri_mask()
        for u in range(NSUB):
            tile = slice(u * TH, (u + 1) * TH)
            w = _hgrn_common(hq_ref.at[tile, :], hf_ref.at[tile, :], lbr_ref, tri_ref)
            qd, ki, ke = w["qd"].astype(BF16), w["ki"].astype(BF16), w["ke"].astype(BF16)
            dec = w["dec"]
            vb = hi_ref[tile, :]
            for h in range(4):
                cs = slice(h * 128, (h + 1) * 128)
                att = jnp.where(causal, _mm_nt(qd[:, cs], ki[:, cs]), 0.0)
                o_intra = _mm(att.astype(BF16), vb[:, cs])
                st = st_scr[:, cs]
                for c in range(NCH):
                    rs = slice(c * CHUNK, (c + 1) * CHUNK)
                    sall_ref[u * NCH + c, :, cs] = st
                    rec_ref[u * TH + c * CHUNK:u * TH + (c + 1) * CHUNK, cs] = (
                        o_intra[rs] + _mm_nt(qd[rs, cs], st.astype(BF16))).astype(BF16)
                    st = dec[c:c + 1, cs] * st + _mm_tn(vb[rs, cs], ke[rs, cs])
                st_scr[:, cs] = st

    tok = pl.BlockSpec((NSUB * TH, HW), lambda i: (i, 0))
    return pl.pallas_call(
        body, name="hgrn_fwd", grid=(T // (NSUB * TH),),
        in_specs=[tok, tok, tok, pl.BlockSpec((2, HW), lambda i: (0, 0)), pl.BlockSpec((TH, TH), lambda i: (0, 0))],
        out_specs=[tok, pl.BlockSpec((NSUB * NCH, 128, HW), lambda i: (i, 0, 0))],
        out_shape=[jax.ShapeDtypeStruct((T, HW), BF16), jax.ShapeDtypeStruct((T // CHUNK, 128, HW), F32)],
        scratch_shapes=[pltpu.VMEM((128, HW), F32)],
        compiler_params=_cp(("arbitrary",)),
    )(hq, hf, hi, lbr, tri)


def _hgrn_bwd(hq, hf, hi, lbr, tri, trit, drec, sall, dhg, rout, routb):
    NSUB = 2
    NT = T // (NSUB * TH)

    def body(hq_ref, hf_ref, hi_ref, lbr_ref, tri_ref, trit_ref, do_ref, sall_ref, dhg_ref, rout_r, routb_r,
             dph_ref, small_ref, pout_o, poutr_o,
             dst_scr, dlb_scr, dqd_scr, dki_scr, dke_scr, dlast_scr, send_sems, recv_sems, loc_sems):
        step = pl.program_id(0)
        loc, rem = _chip_copies(_w_out_piece, rout_r, routb_r, pout_o, poutr_o, send_sems, recv_sems,
                                loc_sems.at[0])

        @pl.when(step == 0)
        def _():
            dst_scr[...] = jnp.zeros_like(dst_scr)
            dlb_scr[...] = jnp.zeros_like(dlb_scr)
            for cp in loc + rem:
                cp.start()

        causal = _tri_mask()
        causal_t = _tri_mask(transposed=True)
        lb = None
        for u in reversed(range(NSUB)):
            tile = slice(u * TH, (u + 1) * TH)
            w = _hgrn_common(hq_ref.at[tile, :], hf_ref.at[tile, :], lbr_ref, tri_ref)
            qd, ki, ke = w["qd"].astype(BF16), w["ki"].astype(BF16), w["ke"].astype(BF16)
            dec = w["dec"]
            vb = hi_ref[tile, :]
            dob = do_ref[tile, :].astype(BF16)
            for h in range(4):
                cs = slice(h * 128, (h + 1) * 128)
                att_t = jnp.where(causal_t, _mm_nt(ki[:, cs], qd[:, cs]), 0.0).astype(BF16)
                datt_t = jnp.where(causal_t, _mm_nt(vb[:, cs], dob[:, cs]), 0.0).astype(BF16)
                datt = jnp.where(causal, _mm_nt(dob[:, cs], vb[:, cs]), 0.0).astype(BF16)
                dv_intra = _mm(att_t, dob[:, cs])
                dqd_intra = _mm(datt, ki[:, cs])
                dki_scr[u, :, cs] = _mm(datt_t, qd[:, cs])
                dst = dst_scr[:, cs]
                for c in reversed(range(NCH)):
                    rs = slice(c * CHUNK, (c + 1) * CHUNK)
                    dec_c = dec[c:c + 1, :]
                    st = sall_ref[u * NCH + c, :, cs]
                    dstb = dst.astype(BF16)
                    dph_ref[u * TH + c * CHUNK:u * TH + (c + 1) * CHUNK, 2 * HW + h * 128:2 * HW + (h + 1) * 128] = (
                        dv_intra[rs] + _mm_nt(ke[rs, cs], dstb)).astype(BF16)
                    dqd_scr[u, rs, cs] = dqd_intra[rs] + _mm(dob[rs, cs], st.astype(BF16))
                    dke_scr[u, rs, cs] = _mm(vb[rs, cs], dstb)
                    ddec = jnp.sum(dst * st, axis=0, keepdims=True)
                    dlast_scr[u, c:c + 1, cs] = ddec * dec_c[:, cs]
                    dst = dec_c[:, cs] * dst + _mm_tn(dob[rs, cs], qd[rs, cs])
                dst_scr[:, cs] = dst
            dqd, dki, dke = dqd_scr[u], dki_scr[u], dke_scr[u]
            dq = dqd * w["ea"]
            dk = dki * w["ena"] + dke * w["eend"]
            dcum = dqd * w["qd"] - dki * w["ki"] - dke * w["ke"]
            dkeke = dke * w["ke"]
            dlastb = jnp.concatenate(
                [jnp.broadcast_to(dlast_scr[u, c:c + 1, :]
                                  + jnp.sum(dkeke[c * CHUNK:(c + 1) * CHUNK], axis=0, keepdims=True), (CHUNK, HW))
                 for c in range(NCH)], axis=0)
            dg = _mm_exact_l(trit_ref[...], dcum) + dlastb
            df = dg / w["f"] - dk
            lb, sf, sq = w["lb"], w["sf"], w["sq"]
            dph_ref[tile, HW:2 * HW] = (df * (1.0 - lb) * sf * (1.0 - sf)).astype(BF16)
            dph_ref[tile, 0:HW] = (dq * (sq * (1.0 + w["hq"] * (1.0 - sq)))).astype(BF16)
            dph_ref[tile, 3 * HW:4 * HW] = dhg_ref[tile, :]
            dlb_scr[...] += jnp.sum(df * (1.0 - sf), axis=0, keepdims=True)

        @pl.when(step == NT - 1)
        def _():
            gr = dlb_scr[...] * lb * (1.0 - lb)
            small_ref[...] = jnp.zeros_like(small_ref)
            small_ref[0:1, 0:HW] = gr
            small_ref[1:2, 0:HW] = -gr
            for cp in rem:
                cp.wait_recv()
            for cp in rem:
                cp.wait_send()
            for cp in loc:
                cp.wait()

    tok = pl.BlockSpec((NSUB * TH, HW), lambda i: (NT - 1 - i, 0))
    const = lambda shape: pl.BlockSpec(shape, lambda i: (0,) * len(shape))
    hbm = pl.BlockSpec(memory_space=pltpu.HBM)
    return pl.pallas_call(
        body, name="hgrn_bwd", grid=(NT,),
        in_specs=[tok, tok, tok, const((2, HW)), const((TH, TH)), const((TH, TH)), tok,
                  pl.BlockSpec((NSUB * NCH, 128, HW), lambda i: (NT - 1 - i, 0, 0)), tok, hbm, hbm],
        out_specs=[pl.BlockSpec((NSUB * TH, NCOL // 2), lambda i: (NT - 1 - i, 0)), const((8, D)), hbm, hbm],
        out_shape=[jax.ShapeDtypeStruct((T, NCOL // 2), BF16), jax.ShapeDtypeStruct((8, D), F32),
                   jax.ShapeDtypeStruct((128, D), F32), jax.ShapeDtypeStruct((3, 128, D), BF16)],
        scratch_shapes=[pltpu.VMEM((128, HW), F32), pltpu.VMEM((1, HW), F32), pltpu.VMEM((NSUB, TH, HW), F32),
                        pltpu.VMEM((NSUB, TH, HW), F32), pltpu.VMEM((NSUB, TH, HW), F32),
                        pltpu.VMEM((NSUB, 8, HW), F32),
                        pltpu.SemaphoreType.DMA((3,)), pltpu.SemaphoreType.DMA((3,)), pltpu.SemaphoreType.DMA((1,))],
        compiler_params=_cp(("arbitrary",)),
    )(hq, hf, hi, lbr, tri, trit, drec, sall, dhg, rout, routb)


def _fwd_out(o1, o4, o16, l1, l4, l16, rec, ag, hg, x, tgt, anw, hnw, fnw, wout_full, gmat, emat, selmat):
    TT = 512

    def body(o1_r, o4_r, o16_r, l1_r, l4_r, l16_r, rec_r, ag_r, hg_r, x_r, tgt_r, anw_r, hnw_r, fnw_r, wo_r, g_r,
             e_r, sel_r, dx2_o, do1_o, do4_o, do16_o, st1_o, st4_o, st16_o, drec_o, dag_o, dhg_o,
             rout_o, routb_o, small_o, scr_a, scr_b, scr_c, gwout_o, rbuf, send_sems, recv_sems):
        @pl.when(pl.program_id(0) == 0)
        def _():
            gwout_o[...] = jnp.zeros_like(gwout_o)
            small_o[...] = jnp.zeros_like(small_o)

        def unperm(r4, r16):
            return _unperm_load(r4, r16, scr_a, scr_b, scr_c)

        def perm_out(val, p1, p4, p16, dt):
            _perm_store(val, scr_a, scr_b, p1, p4, p16, dt)

        o4u, o16u = unperm(o4_r, o16_r)
        l4c, l16c = unperm(l4_r, l16_r)
        l1c = l1_r[...]
        mxc = jnp.maximum(jnp.maximum(l1c, l4c), l16c)
        w1c, w4c, w16c = jnp.exp(l1c - mxc), jnp.exp(l4c - mxc), jnp.exp(l16c - mxc)
        denc = w1c + w4c + w16c
        lane = lax.broadcasted_iota(jnp.int32, (1, 128), 1)
        lse_c = jnp.where(lane < 8, mxc + jnp.log(denc), 0.0)
        em = e_r[...]
        wn1 = _mm_exact_r(w1c / denc, em)
        wn4 = _mm_exact_r(w4c / denc, em)
        o1v = o1_r[...].astype(F32)
        attn = wn1 * o1v + wn4 * o4u + (1.0 - wn1 - wn4) * o16u
        gm = g_r[...]

        def head_mean_a(t):
            return jnp.concatenate([_mm_exact_r(t[:, :256], gm), _mm_exact_r(t[:, 256:], gm)], axis=1)

        def head_mean_h(t):
            return jnp.concatenate(
                [jnp.broadcast_to(jnp.mean(t[:, h * 128:(h + 1) * 128], axis=-1, keepdims=True), (TT, 128))
                 for h in range(4)], axis=1)

        rs_a = lax.rsqrt(head_mean_a(attn * attn) + EPS)
        n_a = attn * rs_a
        agv = ag_r[...].astype(F32)
        sg_a = _sigmoid(agv)
        si_a = agv * sg_a
        anw_v = anw_r[...]
        y_a = (n_a * anw_v) * si_a
        recv = rec_r[...].astype(F32)
        rs_h = lax.rsqrt(head_mean_h(recv * recv) + EPS)
        n_h = recv * rs_h
        hgv = hg_r[...].astype(F32)
        sg_h = _sigmoid(hgv)
        si_h = hgv * sg_h
        hnw_v = hnw_r[...]
        y_h = (n_h * hnw_v) * si_h
        mixed = jnp.concatenate([y_a, y_h], axis=1).astype(BF16)
        xv = x_r[...]
        x2 = xv + _mm(mixed, wo_r[...])
        r2 = lax.rsqrt(jnp.mean(x2 * x2, axis=-1, keepdims=True) + EPS)
        fnw_v = fnw_r[...]
        xn = x2 * r2
        err = xn * fnw_v - tgt_r[...]
        small_o[2:3, :] += 0.5 * jnp.sum(jnp.mean(err * err, axis=-1, keepdims=True), axis=0, keepdims=True)
        dy = err * (1.0 / D)
        small_o[0:1, :] += jnp.sum(dy * xn, axis=0, keepdims=True)
        dyw = dy * fnw_v
        dx2 = r2 * dyw - x2 * ((r2 * r2 * r2) * jnp.mean(dyw * x2, axis=-1, keepdims=True))
        dx2_o[...] = dx2
        dx2b = dx2.astype(BF16)
        gwout_o[...] += _mm_tn(mixed, dx2b)
        dmix = _mm_nt(dx2b, wo_r[...])
        dm_a, dm_h = dmix[:, :AW], dmix[:, AW:]
        dag_o[...] = (dm_a * (n_a * anw_v) * (sg_a * (1.0 + agv * (1.0 - sg_a)))).astype(BF16)
        dn_a = dm_a * anw_v * si_a
        small_o[1:2, 0:AW] += jnp.sum(dm_a * n_a * si_a, axis=0, keepdims=True)
        dattn = rs_a * (dn_a - n_a * head_mean_a(dn_a * n_a))
        perm_out(dattn, do1_o, do4_o, do16_o, BF16)
        stats = lse_c + _mm_exact_r(dattn * attn, sel_r[...])
        perm_out(stats, st1_o, st4_o, st16_o, F32)
        dhg_o[...] = (dm_h * (n_h * hnw_v) * (sg_h * (1.0 + hgv * (1.0 - sg_h)))).astype(BF16)
        dn_h = dm_h * hnw_v * si_h
        small_o[1:2, AW:] += jnp.sum(dm_h * n_h * si_h, axis=0, keepdims=True)
        drec_o[...] = (rs_h * (dn_h - n_h * head_mean_h(dn_h * n_h))).astype(BF16)

        @pl.when(pl.program_id(0) == T // TT - 1)
        def _():
            x, y, c = lax.axis_index("x"), lax.axis_index("y"), lax.axis_index("c")
            cps = [pltpu.make_async_remote_copy(
                src_ref=gwout_o.at[pl.ds(pl.multiple_of(j * 256 + (1 - c) * 128, 128), 128), :], dst_ref=rbuf.at[j],
                send_sem=send_sems.at[j], recv_sem=recv_sems.at[j], device_id=(x, y, 1 - c), device_id_type=MESH)
                for j in range(4)]
            for cp in cps:
                cp.start()
            for j, cp in enumerate(cps):
                cp.wait_recv()
                red = gwout_o[pl.ds(pl.multiple_of(j * 256 + c * 128, 128), 128), :] + rbuf[j]
                rout_o[j * 128:(j + 1) * 128, :] = red
                routb_o[j * 128:(j + 1) * 128, :] = red.astype(BF16)
            for cp in cps:
                cp.wait_send()

    tok = lambda w: pl.BlockSpec((TT, w), lambda i: (i, 0))
    d4 = pl.BlockSpec((4, TT // 4, AW), lambda i: (0, i, 0))
    d16 = pl.BlockSpec((16, TT // 16, AW), lambda i: (0, i, 0))
    const = lambda shape: pl.BlockSpec(shape, lambda i: (0,) * len(shape))
    sd = lambda shape, dt: jax.ShapeDtypeStruct(shape, dt)
    c4 = pl.BlockSpec((4, TT // 4, 128), lambda i: (0, i, 0))
    c16 = pl.BlockSpec((16, TT // 16, 128), lambda i: (0, i, 0))
    p3 = lambda w, dt: [sd((T, w), dt), sd((4, T // 4, w), dt), sd((16, T // 16, w), dt)]
    return pl.pallas_call(
        body, name="fwd_out", grid=(T // TT,),
        in_specs=[tok(AW), d4, d16, tok(128), c4, c16, tok(AW), tok(AW), tok(AW), tok(D), tok(D),
                  const((1, AW)), const((1, HW)), const((1, D)), const((D, D)), const((256, 256)),
                  const((128, AW)), const((AW, 128))],
        out_specs=[tok(D)] + [tok(AW), d4, d16] + [tok(128), c4, c16] + [tok(AW)] * 3
        + [const((512, D)), const((512, D)), const((8, D))],
        out_shape=[sd((T, D), F32)] + p3(AW, BF16) + p3(128, F32)
        + [sd((T, AW), BF16), sd((T, AW), BF16), sd((T, AW), BF16), sd((512, D), F32), sd((512, D), BF16),
           sd((8, D), F32)],
        scratch_shapes=[pltpu.VMEM((4, TT, 128), F32)] * 3 + [pltpu.VMEM((D, D), F32),
                        pltpu.VMEM((4, 128, D), F32), pltpu.SemaphoreType.DMA((4,)), pltpu.SemaphoreType.DMA((4,))],
        compiler_params=_cp(("arbitrary",)),
    )(o1, o4, o16, l1, l4, l16, rec, ag, hg, x, tgt, anw, hnw, fnw, wout_full, gmat, emat, selmat)


def _dproj_build(dq, dk, dv, dag, pos):
    TT = 512

    def body(dq1, dq4, dq16, dk1, dk4, dk16, dv1, dv4, dv16, dag_r, pos_r, dproj_o, scr_b, scr_c):
        def unperm_sum(r1, r4, r16):
            return r1[...] + _unperm_sum(r4, r16, scr_b, scr_c)

        cosf, s1, s2 = _rope_tables(pos_r[...])
        dproj_o[:, 0:512] = _rope_bwd(unperm_sum(dq1, dq4, dq16), cosf, s1, s2).astype(BF16)
        dproj_o[:, 512:1024] = _rope_bwd(unperm_sum(dk1, dk4, dk16), cosf, s1, s2).astype(BF16)
        dproj_o[:, 1024:1536] = unperm_sum(dv1, dv4, dv16).astype(BF16)
        dproj_o[:, 1536:2048] = dag_r[...]

    tok = lambda w: pl.BlockSpec((TT, w), lambda i: (i, 0))
    d4 = pl.BlockSpec((4, TT // 4, AW), lambda i: (0, i, 0))
    d16 = pl.BlockSpec((16, TT // 16, AW), lambda i: (0, i, 0))
    return pl.pallas_call(
        body, name="dproj_build", grid=(T // TT,),
        in_specs=[tok(AW), d4, d16] * 3 + [tok(AW), tok(1)],
        out_specs=tok(NCOL // 2),
        out_shape=jax.ShapeDtypeStruct((T, NCOL // 2), BF16),
        scratch_shapes=[pltpu.VMEM((4, TT, 128), F32)] * 2,
        compiler_params=_cp(("parallel",)),
    )(*dq, *dk, *dv, dag, pos)


def _bwd_x(dproj_a, dproj_h, x, dx2, mixw, w_full, rin, rinb, small4, small6, pout_own, pout_rem):
    TT = 256
    NT = T // TT

    def body(dpa_r, dph_r, x_r, dx2_r, mw_r, w_r, rin_r, rinb_r, s4_r, s6_r, poo_r, por_r,
             gx_o, pin_o, pinr_o, sall_o, fin_o, fout_o, sbuf, v_own, v_rem, vo_own, vo_rem, sin, sout, got_in,
             got_out, send_sems, recv_sems, loc_sems, share_send, share_recv, fin_sems):
        i = pl.program_id(0)
        loc, rem = _chip_copies(_w_in_piece, rin_r, rinb_r, pin_o, pinr_o, send_sems, recv_sems, loc_sems.at[0])

        @pl.when(i == 0)
        def _():
            sbuf[...] = jnp.zeros_like(sbuf)
            for cp in loc + rem:
                cp.start()

        dhn = _mm_nt(dpa_r[...], w_r[:, 0:NCOL // 2]) + _mm_nt(dph_r[...], w_r[:, NCOL // 2:NCOL])
        xv = x_r[...]
        r = lax.rsqrt(jnp.mean(xv * xv, axis=-1, keepdims=True) + EPS)
        dxw = dhn * mw_r[...]
        gx_o[...] = dx2_r[...] + r * dxw - xv * ((r * r * r) * jnp.mean(dxw * xv, axis=-1, keepdims=True))
        sbuf[16:17, :] += jnp.sum(dhn * (xv * r), axis=0, keepdims=True)

        @pl.when(i == NT - 1)
        def _():
            sbuf[0:8, :] = s4_r[...]
            sbuf[8:16, :] = s6_r[...]
            sloc, srem = _small_copies(sbuf, sall_o, send_sems, recv_sems, loc_sems.at[1])
            for cp in sloc + srem:
                cp.start()
            for cp in rem:
                cp.wait_recv()
            for cp in rem:
                cp.wait_send()
            for cp in loc:
                cp.wait()
            mx, my, c = lax.axis_index("x"), lax.axis_index("y"), lax.axis_index("c")
            loads = [pltpu.make_async_copy(pin_o, v_own, fin_sems.at[0]),
                     pltpu.make_async_copy(pinr_o, v_rem, fin_sems.at[1]),
                     pltpu.make_async_copy(poo_r, vo_own, fin_sems.at[2]),
                     pltpu.make_async_copy(por_r, vo_rem, fin_sems.at[3])]
            for cp in loads:
                cp.start()
            for cp in loads:
                cp.wait()
            sout[...] = ((vo_own[...] + vo_rem[0].astype(F32)) + vo_rem[1].astype(F32)) + vo_rem[2].astype(F32)
            sin[...] = ((v_own[...] + v_rem[0].astype(F32)) + v_rem[1].astype(F32)) + v_rem[2].astype(F32)
            swap = [pltpu.make_async_remote_copy(src_ref=sin, dst_ref=got_in, send_sem=share_send.at[0],
                                                 recv_sem=share_recv.at[0], device_id=(mx, my, 1 - c),
                                                 device_id_type=MESH),
                    pltpu.make_async_remote_copy(src_ref=sout, dst_ref=got_out, send_sem=share_send.at[1],
                                                 recv_sem=share_recv.at[1], device_id=(mx, my, 1 - c),
                                                 device_id_type=MESH)]
            for cp in swap:
                cp.start()
            mine = [pltpu.make_async_copy(sin, fin_o.at[c], fin_sems.at[0]),
                    pltpu.make_async_copy(sout, fout_o.at[c], fin_sems.at[1])]
            for cp in mine:
                cp.start()
            for cp in swap:
                cp.wait_recv()
            theirs = [pltpu.make_async_copy(got_in, fin_o.at[1 - c], fin_sems.at[2]),
                      pltpu.make_async_copy(got_out, fout_o.at[1 - c], fin_sems.at[3])]
            for cp in theirs:
                cp.start()
            for cp in swap:
                cp.wait_send()
            for cp in mine + theirs:
                cp.wait()
            for cp in srem:
                cp.wait_recv()
            for cp in srem:
                cp.wait_send()
            for cp in sloc:
                cp.wait()

    tok = lambda w: pl.BlockSpec((TT, w), lambda i: (i, 0))
    const = lambda shape: pl.BlockSpec(shape, lambda i: (0,) * len(shape))
    hbm = pl.BlockSpec(memory_space=pltpu.HBM)
    return pl.pallas_call(
        body, name="bwd_x", grid=(NT,),
        in_specs=[tok(NCOL // 2), tok(NCOL // 2), tok(D), tok(D), const((1, D)), const((D, NCOL)), hbm, hbm,
                  const((8, D)), const((8, D)), hbm, hbm],
        out_specs=[tok(D), hbm, hbm, hbm, hbm, hbm],
        out_shape=[jax.ShapeDtypeStruct((T, D), F32),
                   jax.ShapeDtypeStruct((512, 1024), F32), jax.ShapeDtypeStruct((3, 512, 1024), BF16),
                   jax.ShapeDtypeStruct((8, 24, D), F32),
                   jax.ShapeDtypeStruct((2, 512, 1024), F32), jax.ShapeDtypeStruct((2, 128, D), F32)],
        scratch_shapes=[pltpu.VMEM((24, D), F32),
                        pltpu.VMEM((512, 1024), F32), pltpu.VMEM((3, 512, 1024), BF16),
                        pltpu.VMEM((128, D), F32), pltpu.VMEM((3, 128, D), BF16),
                        pltpu.VMEM((512, 1024), F32), pltpu.VMEM((128, D), F32),
                        pltpu.VMEM((512, 1024), F32), pltpu.VMEM((128, D), F32),
                        pltpu.SemaphoreType.DMA((10,)), pltpu.SemaphoreType.DMA((10,)), pltpu.SemaphoreType.DMA((2,)),
                        pltpu.SemaphoreType.DMA((2,)), pltpu.SemaphoreType.DMA((2,)), pltpu.SemaphoreType.DMA((4,))],
        compiler_params=_cp(("arbitrary",)),
    )(dproj_a, dproj_h, x, dx2, mixw, w_full, rin, rinb, small4, small6, pout_own, pout_rem)


def _grad_w_in(hn, dproj_a, dproj_h):
    TK = 2048
    NK = T // TK

    def body(hnt_r, dpa_r, dph_r, rin_o, rinb_o, acc, rbuf, obuf, obufb, send_sems, recv_sems, wb_sems):
        j = pl.program_id(0)
        kk = pl.program_id(1)
        x, y, c = lax.axis_index("x"), lax.axis_index("y"), lax.axis_index("c")
        mine = pl.ds(pl.multiple_of(c * 512, 512), 512)
        theirs = pl.ds(pl.multiple_of((1 - c) * 512, 512), 512)

        def send(jj):
            return pltpu.make_async_remote_copy(
                src_ref=acc.at[jj % 2, theirs, :], dst_ref=rbuf.at[jj], send_sem=send_sems.at[jj],
                recv_sem=recv_sems.at[jj], device_id=(x, y, 1 - c), device_id_type=MESH)

        def writeback(jj):
            cols = pl.ds(jj * 1024, 1024)
            return [pltpu.make_async_copy(obuf.at[jj % 2], rin_o.at[:, cols], wb_sems.at[jj % 2]),
                    pltpu.make_async_copy(obufb.at[jj % 2], rinb_o.at[:, cols], wb_sems.at[2 + jj % 2])]

        def wait_writeback(jj):
            for cp in writeback(jj):
                cp.wait()

        def finalize(jj):
            send(jj).wait_recv()
            red = acc[jj % 2, mine, :] + rbuf[jj]
            obuf[jj % 2] = red
            obufb[jj % 2] = red.astype(BF16)
            for cp in writeback(jj):
                cp.start()

        prod = _mm(hnt_r[...], jnp.where(j < 2, dpa_r[...], dph_r[...]))

        @pl.when(kk == 0)
        def _():
            for jj in (2, 3):
                @pl.when(j == jj)
                def _():
                    send(jj - 2).wait_send()
            acc[j % 2] = prod

        @pl.when(kk > 0)
        def _():
            acc[j % 2] += prod

        @pl.when(kk == NK - 1)
        def _():
            for jj in range(4):
                @pl.when(j == jj)
                def _():
                    send(jj).start()
                    if jj in (1, 2):
                        finalize(jj - 1)
                    if jj == 3:
                        wait_writeback(0)
                        finalize(2)
                        wait_writeback(1)
                        finalize(3)
                        wait_writeback(2)
                        wait_writeback(3)
                        send(2).wait_send()
                        send(3).wait_send()

    hbm = pl.BlockSpec(memory_space=pltpu.HBM)
    return pl.pallas_call(
        body, name="grad_w_in", grid=(4, NK),
        in_specs=[pl.BlockSpec((D, TK), lambda j, kk: (0, kk)),
                  pl.BlockSpec((TK, 1024), lambda j, kk: (jnp.where(j < 2, kk, NK - 1), jnp.minimum(j, 1))),
                  pl.BlockSpec((TK, 1024), lambda j, kk: (jnp.where(j < 2, 0, kk), jnp.maximum(j - 2, 0)))],
        out_specs=[hbm, hbm],
        out_shape=[jax.ShapeDtypeStruct((512, NCOL), F32), jax.ShapeDtypeStruct((512, NCOL), BF16)],
        scratch_shapes=[pltpu.VMEM((2, D, 1024), F32), pltpu.VMEM((4, 512, 1024), F32), pltpu.VMEM((2, 512, 1024), F32),
                        pltpu.VMEM((2, 512, 1024), BF16),
                        pltpu.SemaphoreType.DMA((4,)), pltpu.SemaphoreType.DMA((4,)), pltpu.SemaphoreType.DMA((4,))],
        compiler_params=_cp(("arbitrary", "arbitrary")),
    )(hn, dproj_a, dproj_h)


def _w_in_piece(ref, j):
    return ref.at[:, pl.ds(j * 1024, 1024)]


def _w_out_piece(ref, j):
    return ref.at[pl.ds(j * 128, 128), :]


def _chip_copies(piece, src_r, srcb_r, own_o, rem_o, send_sems, recv_sems, loc_sem):
    x, y, c = lax.axis_index("x"), lax.axis_index("y"), lax.axis_index("c")
    chips = [(1 - x, y), (x, 1 - y), (1 - x, 1 - y)]
    loc = [pltpu.make_async_copy(piece(src_r, 2 * x + y), own_o, loc_sem)]
    rem = [pltpu.make_async_remote_copy(
        src_ref=piece(srcb_r, 2 * px + py), dst_ref=rem_o.at[k], send_sem=send_sems.at[k],
        recv_sem=recv_sems.at[k], device_id=(px, py, c), device_id_type=MESH) for k, (px, py) in enumerate(chips)]
    return loc, rem


def _small_copies(small_r, sall_o, send_sems, recv_sems, loc_sem):
    x, y, c = lax.axis_index("x"), lax.axis_index("y"), lax.axis_index("c")
    me = 4 * x + 2 * y + c
    loc = [pltpu.make_async_copy(small_r, sall_o.at[me], loc_sem)]
    rem = []
    k = 3
    for fx in range(2):
        for fy in range(2):
            for fc in range(2):
                if fx or fy or fc:
                    peer = (1 - x if fx else x, 1 - y if fy else y, 1 - c if fc else c)
                    rem.append(pltpu.make_async_remote_copy(
                        src_ref=small_r, dst_ref=sall_o.at[me], send_sem=send_sems.at[k],
                        recv_sem=recv_sems.at[k], device_id=peer, device_id_type=MESH))
                    k += 1
    return loc, rem


def _adamw_math(w, g, m, v):
    m = B1 * m + (1.0 - B1) * g
    v = B2 * v + (1.0 - B2) * (g * g)
    m_hat = m / (1.0 - B1 ** STEP)
    v_hat = v / (1.0 - B2 ** STEP)
    delta = -LR * (m_hat / (jnp.sqrt(v_hat) + AEPS) + WD * w)
    return delta, m, v


def _adamw(big_in, big_out, sall, params):
    def body(*refs):
        wi, gi, mi, vi, wo, go, mo, vo, sall_r = refs[:9]
        ins = refs[9:24]
        di_o, mi_o, vi_o, do_o, mo_o, vo_o = refs[24:30]
        outs = refs[30:]
        d, mm, vv = _adamw_math(wi[...], gi[...], mi[...], vi[...])
        di_o[...] = d
        mi_o[...] = mm
        vi_o[...] = vv

        @pl.when(pl.program_id(0) == 0)
        def _():
            d, mm, vv = _adamw_math(wo[...], go[...], mo[...], vo[...])
            do_o[...] = d
            mo_o[...] = mm
            vo_o[...] = vv
            tot = sall_r[0]
            for dv in range(1, 8):
                tot = tot + sall_r[dv]
            grads = [tot[16:17, :], tot[1:2, 0:AW], tot[1:2, AW:], tot[8:10, 0:HW], tot[0:1, :]]
            outs[0][...] = tot[2:3, 0:1]
            for p in range(5):
                w_r, m_r, v_r = ins[3 * p:3 * p + 3]
                g = grads[p]
                d, mm, vv = _adamw_math(w_r[...], g, m_r[...], v_r[...])
                outs[1 + 4 * p][...] = g
                outs[2 + 4 * p][...] = d
                outs[3 + 4 * p][...] = mm
                outs[4 + 4 * p][...] = vv

    flat = [a for p in params for a in p]
    shapes = [jax.ShapeDtypeStruct((D, 1024), F32)] * 3 + [jax.ShapeDtypeStruct((256, D), F32)] * 3
    shapes += [jax.ShapeDtypeStruct((1, 1), F32)]
    for p in params:
        shapes += [jax.ShapeDtypeStruct(p[0].shape, F32)] * 4
    vm = pl.BlockSpec(memory_space=pltpu.VMEM)
    rows = pl.BlockSpec((512, 1024), lambda i: (i, 0))
    whole = pl.BlockSpec((256, D), lambda i: (0, 0))
    return pl.pallas_call(
        body, name="adamw", grid=(2,),
        in_specs=[rows] * 4 + [whole] * 4 + [vm] * 16, out_specs=[rows] * 3 + [whole] * 3 + [vm] * 21,
        out_shape=shapes,
        compiler_params=_cp(("arbitrary",)),
    )(*big_in, *big_out, sall, *flat)


def kernel(x, positions, w_in, w_out, mix_norm_w, attn_out_norm_w, hgrn_out_norm_w, hgrn_lb_raw, final_norm_w, loss_target, m_w_in, m_w_out, m_mix_norm_w, m_attn_out_norm_w, m_hgrn_out_norm_w, m_hgrn_lb_raw, m_final_norm_w, v_w_in, v_w_out, v_mix_norm_w, v_attn_out_norm_w, v_hgrn_out_norm_w, v_hgrn_lb_raw, v_final_norm_w):
    xs = x.reshape(T, D)
    tgt = loss_target.reshape(T, D)
    pos = positions.reshape(T, 1)
    fnw = final_norm_w.reshape(1, D)

    ti = np.arange(TH)
    tri_np = ((ti[:, None] // CHUNK == ti[None, :] // CHUNK) & (ti[None, :] <= ti[:, None])).astype(np.float32)
    tri = jnp.asarray(tri_np, BF16)
    trit = jnp.asarray(tri_np.T, BF16)
    hi_ = np.arange(AW) // HEAD
    gmat = jnp.asarray((hi_[:256, None] == hi_[None, :256]).astype(np.float32) / HEAD, BF16)
    emat_np = (np.arange(128)[:, None] == hi_[None, :]).astype(np.float32)
    sel_np = (8 + hi_[:, None] == np.arange(128)[None, :]).astype(np.float32)
    emat = jnp.asarray(emat_np, BF16)
    selmat = jnp.asarray(sel_np, BF16)

    jm_arr = (2 * lax.axis_index("x") + lax.axis_index("y")).astype(jnp.int32).reshape(1)
    (hn, q1, k1, v1, q4, k4, v4, q16, k16, v16, ag, hq, hf, hi, hg, w_full, wout4) = _fwd_in(
        xs, pos, mix_norm_w, w_in.reshape(D, 1024), w_out.reshape(256, D), jm_arr)
    wout_full = wout4.reshape(D, D)
    flat = lambda a: a.reshape(T, AW)
    o1, l1 = _attn_fwd(q1, k1, v1, T // BLK, "attn_fwd_d1")
    o4, l4 = _attn_fwd(flat(q4), flat(k4), flat(v4), T // 4 // BLK, "attn_fwd_d4")
    o16, l16 = _attn_fwd(flat(q16), flat(k16), flat(v16), T // 16 // BLK, "attn_fwd_d16")
    rec, sall = _hgrn_fwd(hq, hf, hi, hgrn_lb_raw, tri)

    (dx2, do1, do4, do16, st1, st4, st16, drec, dag, dhg, rout, routb, small4) = _fwd_out(
        o1, o4.reshape(4, T // 4, AW), o16.reshape(16, T // 16, AW),
        l1, l4.reshape(4, T // 4, 128), l16.reshape(16, T // 16, 128),
        rec, ag, hg, xs, tgt, attn_out_norm_w, hgrn_out_norm_w, fnw, wout_full, gmat, emat, selmat)

    fst = lambda a: a.reshape(T, 128)
    dq1, dk1, dv1 = _attn_bwd(q1, k1, v1, do1, st1, T // BLK, "attn_bwd_d1")
    dq4, dk4, dv4 = _attn_bwd(flat(q4), flat(k4), flat(v4), flat(do4), fst(st4), T // 4 // BLK, "attn_bwd_d4")
    dq16, dk16, dv16 = _attn_bwd(flat(q16), flat(k16), flat(v16), flat(do16), fst(st16), T // 16 // BLK,
                                 "attn_bwd_d16")
    dproj_h, small6, pout_own, pout_rem = _hgrn_bwd(hq, hf, hi, hgrn_lb_raw, tri, trit, drec, sall, dhg,
                                                    rout, routb)

    r4 = lambda a: a.reshape(4, T // 4, AW)
    r16 = lambda a: a.reshape(16, T // 16, AW)
    dproj_a = _dproj_build((dq1, r4(dq4), r16(dq16)), (dk1, r4(dk4), r16(dk16)), (dv1, r4(dv4), r16(dv16)),
                           dag, pos)
    rin, rinb = _grad_w_in(hn, dproj_a, dproj_h)
    gx, _, _, small_all, fin, fout = _bwd_x(dproj_a, dproj_h, xs, dx2, mix_norm_w, w_full, rin, rinb,
                                            small4, small6, pout_own, pout_rem)
    g_w_in = fin.reshape(D, 1024)
    g_w_out = fout.reshape(256, D)

    params = [(mix_norm_w, m_mix_norm_w, v_mix_norm_w),
              (attn_out_norm_w, m_attn_out_norm_w, v_attn_out_norm_w),
              (hgrn_out_norm_w, m_hgrn_out_norm_w, v_hgrn_out_norm_w),
              (hgrn_lb_raw, m_hgrn_lb_raw, v_hgrn_lb_raw),
              (fnw, m_final_norm_w.reshape(1, D), v_final_norm_w.reshape(1, D))]
    d_in, nm_in, nv_in, d_out, nm_out, nv_out, *so = _adamw(
        (w_in.reshape(D, 1024), g_w_in, m_w_in.reshape(D, 1024), v_w_in.reshape(D, 1024)),
        (w_out.reshape(256, D), g_w_out, m_w_out.reshape(256, D), v_w_out.reshape(256, D)), small_all, params)
    loss = so[0].reshape(())
    g_s = [so[1 + 4 * p] for p in range(5)]
    d_s = [so[2 + 4 * p] for p in range(5)]
    m_s = [so[3 + 4 * p] for p in range(5)]
    v_s = [so[4 + 4 * p] for p in range(5)]
    for lst in (g_s, d_s, m_s, v_s):
        lst[4] = lst[4].reshape(D)

    return (loss, gx.reshape(1, T, D),
            g_w_in.reshape(1, D, 1024), g_w_out.reshape(1, 256, D), *g_s,
            d_in.reshape(1, D, 1024), d_out.reshape(1, 256, D), *d_s,
            nm_in.reshape(1, D, 1024), nm_out.reshape(1, 256, D), *m_s,
            nv_in.reshape(1, D, 1024), nv_out.reshape(1, 256, D), *v_s)
```

```python
import functools

import numpy as np
import jax
import jax.numpy as jnp
from jax import lax
from jax.experimental import pallas as pl
from jax.experimental.pallas import tpu as pltpu

F32 = jnp.float32
BF16 = jnp.bfloat16

T = 4096
D = 1024
AW = 512
HW = 512
NCOL = 4096
HEAD = 64
BLK = 128
CHUNK = 64
EPS = 1e-6
SCALE = HEAD ** -0.5
NEG = -1e30
ROPE_THETA = 500000.0
INV_FREQ = [float(v) for v in
            (np.float32(ROPE_THETA) ** (-(np.arange(8, dtype=np.float32)) * np.float32(0.125)))]
LR, B1, B2, AEPS, WD, STEP = 0.001, 0.9, 0.999, 1e-08, 0.01, 10
VMEM_LIMIT = 63 * 1024 * 1024
MESH = pl.DeviceIdType.MESH


def _cp(sem=None, **kw):
    return pltpu.CompilerParams(dimension_semantics=sem, vmem_limit_bytes=VMEM_LIMIT, **kw)


def _mm(a, b):
    return jnp.dot(a, b, preferred_element_type=F32)


def _mm_nt(a, b):
    return lax.dot_general(a, b, (((1,), (1,)), ((), ())), preferred_element_type=F32)


def _mm_tn(a, b):
    return lax.dot_general(a, b, (((0,), (0,)), ((), ())), preferred_element_type=F32)


def _mm_exact_l(mat_bf, x):
    h = x.astype(BF16)
    l = (x - h.astype(F32)).astype(BF16)
    return _mm(mat_bf, h) + _mm(mat_bf, l)


def _mm_exact_r(x, mat_bf):
    h = x.astype(BF16)
    l = (x - h.astype(F32)).astype(BF16)
    return _mm(h, mat_bf) + _mm(l, mat_bf)


def _sigmoid(x):
    return 0.5 * jnp.tanh(0.5 * x) + 0.5


def _rope_tables(pos):
    lane = lax.broadcasted_iota(jnp.int32, (1, 128), 1)
    jl = lane & 63
    fi = jl & 7
    inv = jnp.zeros((1, 128), F32)
    for kk in range(8):
        inv = jnp.where(fi == kk, INV_FREQ[kk], inv)
    ang = pos.astype(F32) * inv
    c = jnp.cos(ang)
    s = jnp.sin(ang)
    cosf = jnp.where(jl < 16, c, 1.0)
    s1 = jnp.where(jl < 8, -s, 0.0)
    s2 = jnp.where((jl >= 8) & (jl < 16), s, 0.0)
    return cosf, s1, s2


def _rope(t, cosf, s1, s2):
    parts = []
    for ci in range(t.shape[1] // 128):
        tc = t[:, ci * 128:(ci + 1) * 128]
        parts.append(tc * cosf + pltpu.roll(tc, 120, 1) * s1 + pltpu.roll(tc, 8, 1) * s2)
    return jnp.concatenate(parts, axis=1)


def _rope_bwd(g, cosf, s1, s2):
    parts = []
    for ci in range(g.shape[1] // 128):
        gc = g[:, ci * 128:(ci + 1) * 128]
        parts.append(gc * cosf + pltpu.roll(gc * s1, 8, 1) + pltpu.roll(gc * s2, 120, 1))
    return jnp.concatenate(parts, axis=1)


def _perm_store(val, scr, scr2, o1, o4, o16, dt):
    n = val.shape[0]
    q = n // 4
    o1[...] = val.astype(dt)
    for ci in range(val.shape[1] // 128):
        cs = slice(ci * 128, (ci + 1) * 128)
        scr[ci] = val[:, cs]
        for r4 in range(4):
            part = scr[ci, pl.ds(r4, q, stride=4), :]
            o4[r4, :, cs] = part.astype(dt)
            scr2[ci, r4 * q:(r4 + 1) * q, :] = part
        for r4 in range(4):
            for b in range(4):
                o16[r4 + 4 * b, :, cs] = scr2[ci, pl.ds(r4 * q + b, q // 4, stride=4), :].astype(dt)


def _unperm_load(r4, r16, scr_a, scr_b, scr_c):
    n = scr_a.shape[1]
    q = n // 4
    nc = r4.shape[-1] // 128
    for ci in range(nc):
        cs = slice(ci * 128, (ci + 1) * 128)
        for rr in range(4):
            scr_a[ci, pl.ds(rr, q, stride=4), :] = r4[rr, :, cs].astype(F32)
        for rr in range(4):
            for b in range(4):
                scr_c[ci, pl.ds(rr * q + b, q // 4, stride=4), :] = r16[rr + 4 * b, :, cs].astype(F32)
        for rr in range(4):
            scr_b[ci, pl.ds(rr, q, stride=4), :] = scr_c[ci, rr * q:(rr + 1) * q, :]
    return (jnp.concatenate([scr_a[ci] for ci in range(nc)], axis=1),
            jnp.concatenate([scr_b[ci] for ci in range(nc)], axis=1))


def _unperm_sum(r4, r16, scr_b, scr_c):
    n = scr_b.shape[1]
    q = n // 4
    nc = r4.shape[-1] // 128
    for ci in range(nc):
        cs = slice(ci * 128, (ci + 1) * 128)
        for rr in range(4):
            for b in range(4):
                scr_c[ci, pl.ds(rr * q + b, q // 4, stride=4), :] = r16[rr + 4 * b, :, cs].astype(F32)
        for rr in range(4):
            scr_b[ci, pl.ds(rr, q, stride=4), :] = scr_c[ci, rr * q:(rr + 1) * q, :] + r4[rr, :, cs].astype(F32)
    return jnp.concatenate([scr_b[ci] for ci in range(nc)], axis=1)


def _fwd_in(x, pos, mixw, w_in, w_out, jm_arr):
    TT = 512
    NT = T // TT

    def body(jm_ref, x_ref, pos_ref, mw_ref, win_ref, wout_ref,
             hnt_ref, q1, k1, v1, q4, k4, v4, q16, k16, v16, ag, hq, hf, hi, hg, wfull_o, woutfull_o,
             wbuf, wobuf, hn_all, scr, scr2, stage, send_sems, recv_sems, loc_sems):
        s = pl.program_id(0)
        i = pl.program_id(1)
        mx, my, c = lax.axis_index("x"), lax.axis_index("y"), lax.axis_index("c")
        me, sibling = (mx, my, c), (mx, my, 1 - c)
        chips = [(mx, 1 - my), (1 - mx, my), (1 - mx, 1 - my)]
        jm = 2 * mx + my
        rows_in = [pl.ds(pl.multiple_of(h * 512, 512), 512) for h in (c, 1 - c)]
        rows_out = [pl.ds(pl.multiple_of(h * 128, 128), 128) for h in (c, 1 - c)]

        def blk(k):
            return lax.bitwise_xor(jm, k + 1)

        def rc(n, ref, to):
            return pltpu.make_async_remote_copy(src_ref=ref, dst_ref=ref, send_sem=send_sems.at[n],
                                                recv_sem=recv_sems.at[n], device_id=to, device_id_type=MESH)

        halves = [pl.ds(0, 512), pl.ds(512, 512)]
        send_in = lambda k, h: rc(12 + 2 * k + h, wbuf.at[jm, rows_in[0], halves[h]], (*chips[k], c))
        got_in = lambda k, h: rc(12 + 2 * k + h, wbuf.at[blk(k), rows_in[0], halves[h]], me)
        relay = lambda h: rc(16 + h, wbuf.at[blk(h), rows_in[0], halves[h]], (*chips[1 - h], c))
        got_relay = lambda h: rc(16 + h, wbuf.at[blk(2), rows_in[0], halves[h]], me)
        send_out = lambda k: rc(3 + k, wobuf.at[jm, rows_out[0], :], (*chips[k], c))
        got_out = lambda k: rc(3 + k, wobuf.at[blk(k), rows_out[0], :], me)
        pass_in = lambda k: rc(6 + k, wbuf.at[blk(k), rows_in[0], :], sibling)
        pass_out = lambda k: rc(9 + k, wobuf.at[blk(k), rows_out[0], :], sibling)
        passed_in = lambda k: rc(6 + k, wbuf.at[blk(k), rows_in[1], :], me)
        passed_out = lambda k: rc(9 + k, wobuf.at[blk(k), rows_out[1], :], me)

        def keep(j, n):
            return pltpu.make_async_copy(wbuf.at[j], wfull_o.at[:, pl.ds(j * 1024, 1024)], loc_sems.at[n])

        @pl.when((s == 0) & (i == 0))
        def _():
            for p in range(5):
                src = win_ref.at[pl.ds(p * 256, 256), :] if p < 4 else wout_ref
                load = pltpu.make_async_copy(src, stage, loc_sems.at[4])
                load.start()
                load.wait()
                if p < 4:
                    wbuf[jm, p * 256:(p + 1) * 256, :] = stage[...].astype(BF16)
                else:
                    wobuf[jm] = stage[...].astype(BF16)
            for k in range(2):
                for h in range(2):
                    send_in(k, h).start()
            keep(jm, 0).start()

        def arrive(k):
            if k == 0:
                for kk in range(2):
                    for h in range(2):
                        got_in(kk, h).wait_recv()
                relay(0).start()
                relay(1).start()
            if k == 2:
                got_relay(0).wait_recv()
                got_relay(1).wait_recv()
            pass_in(k).start()
            passed_in(k).wait_recv()
            keep(blk(k), k + 1).start()
            if k == 2:
                for kk in range(3):
                    send_out(kk).start()

        pl.when((s == 1) & (i == 0))(functools.partial(arrive, 0))

        @pl.when((s == 2) & (i == 0))
        def _():
            arrive(1)
            arrive(2)

        tile = pl.ds(pl.multiple_of(i * TT, TT), TT)

        @pl.when(s == 0)
        def _():
            xv = x_ref[...]
            r = lax.rsqrt(jnp.mean(xv * xv, axis=-1, keepdims=True) + EPS)
            hnf = (xv * r) * mw_ref[...]
            hn_all[tile, :] = hnf.astype(BF16)
            hnt_ref[...] = hnf.T.astype(BF16)

        def project(jj):
            hn = hn_all[tile, :]
            lo = _mm(hn, wbuf[jj, :, 0:512])
            hi_cols = _mm(hn, wbuf[jj, :, 512:1024])
            if jj == 0:
                cosf, s1, s2 = _rope_tables(pos_ref[...])
                _perm_store(_rope(lo, cosf, s1, s2) * SCALE, scr, scr2, q1, q4, q16, BF16)
                _perm_store(_rope(hi_cols, cosf, s1, s2), scr, scr2, k1, k4, k16, BF16)
            elif jj == 1:
                _perm_store(lo, scr, scr2, v1, v4, v16, BF16)
                ag[...] = hi_cols.astype(BF16)
            elif jj == 2:
                hq[...] = lo.astype(BF16)
                hf[...] = hi_cols.astype(BF16)
            else:
                hi[...] = lo.astype(BF16)
                hg[...] = hi_cols.astype(BF16)

        def project_block(j):
            for jj in range(4):
                pl.when(j == jj)(functools.partial(project, jj))

        @pl.when(s < 2)
        def _():
            project_block(lax.bitwise_xor(jm, s))

        @pl.when(s == 2)
        def _():
            project_block(lax.bitwise_xor(jm, 2))
            project_block(lax.bitwise_xor(jm, 3))

        @pl.when((s == 2) & (i == NT - 1))
        def _():
            for k in range(3):
                got_out(k).wait_recv()
                pass_out(k).start()
            for k in range(3):
                passed_out(k).wait_recv()
            out = pltpu.make_async_copy(wobuf, woutfull_o, loc_sems.at[4])
            out.start()
            for h in range(2):
                relay(h).wait_send()
                for k in range(2):
                    send_in(k, h).wait_send()
            for k in range(3):
                send_out(k).wait_send()
                pass_in(k).wait_send()
                pass_out(k).wait_send()
            keep(jm, 0).wait()
            for k in range(3):
                keep(blk(k), k + 1).wait()
            out.wait()

    def at_stage_of(jb):
        def index(s, i, jm_ref):
            sa = jnp.minimum(lax.bitwise_xor(jm_ref[0], jb), 2)
            return jnp.where(s < sa, 0, jnp.where(s == sa, i, NT - 1))
        return index

    tok = lambda w, jb: pl.BlockSpec((TT, w), lambda s, i, jm_ref: (at_stage_of(jb)(s, i, jm_ref), 0))
    d4 = lambda jb: pl.BlockSpec((4, TT // 4, AW), lambda s, i, jm_ref: (0, at_stage_of(jb)(s, i, jm_ref), 0))
    d16 = lambda jb: pl.BlockSpec((16, TT // 16, AW), lambda s, i, jm_ref: (0, at_stage_of(jb)(s, i, jm_ref), 0))
    hbm = pl.BlockSpec(memory_space=pltpu.HBM)
    sd = lambda shape, dt: jax.ShapeDtypeStruct(shape, dt)
    in_own_stage = lambda s, i: jnp.where(s == 0, i, NT - 1)
    grid_spec = pltpu.PrefetchScalarGridSpec(
        num_scalar_prefetch=1, grid=(3, NT),
        in_specs=[pl.BlockSpec((TT, D), lambda s, i, jm_ref: (in_own_stage(s, i), 0)),
                  pl.BlockSpec((TT, 1), lambda s, i, jm_ref: (i, 0)),
                  pl.BlockSpec((1, D), lambda s, i, jm_ref: (0, 0)), hbm, hbm],
        out_specs=[pl.BlockSpec((D, TT), lambda s, i, jm_ref: (0, in_own_stage(s, i))),
                   tok(AW, 0), tok(AW, 0), tok(AW, 1), d4(0), d4(0), d4(1), d16(0), d16(0), d16(1),
                   tok(AW, 1), tok(AW, 2), tok(AW, 2), tok(AW, 3), tok(AW, 3), hbm, hbm],
        scratch_shapes=[pltpu.VMEM((4, D, 1024), BF16), pltpu.VMEM((4, 256, D), BF16), pltpu.VMEM((T, D), BF16),
                        pltpu.VMEM((4, TT, 128), F32), pltpu.VMEM((4, TT, 128), F32), pltpu.VMEM((256, 1024), F32),
                        pltpu.SemaphoreType.DMA((18,)),
                        pltpu.SemaphoreType.DMA((18,)), pltpu.SemaphoreType.DMA((6,))])
    return pl.pallas_call(
        body, name="fwd_in", grid_spec=grid_spec,
        out_shape=[sd((D, T), BF16)] + [sd((T, AW), BF16)] * 3 + [sd((4, T // 4, AW), BF16)] * 3
        + [sd((16, T // 16, AW), BF16)] * 3
        + [sd((T, AW), BF16)] * 5 + [sd((D, NCOL), BF16), sd((4, 256, D), BF16)],
        compiler_params=_cp(("arbitrary", "arbitrary")),
    )(jm_arr, x, pos, mixw, w_in, w_out)


def _band_mask(key_axis, nkeys=2 * BLK):
    shape = (nkeys, 2 * BLK) if key_axis == 0 else (2 * BLK, nkeys)
    kj = lax.broadcasted_iota(jnp.int32, shape, key_axis)
    qi = lax.broadcasted_iota(jnp.int32, shape, 1 - key_axis) & (BLK - 1)
    return (kj >= qi) & (kj <= qi + BLK), kj, qi


def _stack_heads(t2, in_a):
    z = jnp.zeros_like(t2)
    return jnp.concatenate([jnp.where(in_a[0], t2, z), jnp.where(in_a[1], t2, z)], axis=0)


def _attn_fwd(q, k, v, nb, name):
    n = 8
    CH = n * BLK
    halo = nb > n

    def body(*refs):
        if halo:
            q_ref, k_ref, v_ref, kp_ref, vp_ref, o_ref, lse_ref = refs
        else:
            q_ref, k_ref, v_ref, o_ref, lse_ref = refs
        lane = lax.broadcasted_iota(jnp.int32, (1, 128), 1)
        in_a = [lane < HEAD, lane >= HEAD]
        band, kj, _ = _band_mask(1)
        thr0 = jnp.where((n * pl.program_id(0)) % nb == 0, BLK, 0) if halo else BLK
        mask0 = band & (kj >= thr0)
        mask_first = band & (kj >= BLK)
        for b in range(n):
            rs = slice(b * BLK, (b + 1) * BLK)
            stat = jnp.zeros((BLK, 128), F32)
            for hp in range(4):
                cs = slice(hp * 128, (hp + 1) * 128)
                q2s = _stack_heads(q_ref[rs, cs], in_a)
                if b == 0:
                    kprev = kp_ref[:, cs] if halo else k_ref[rs, cs]
                    vprev = vp_ref[:, cs] if halo else v_ref[rs, cs]
                    kk = jnp.concatenate([kprev, k_ref[rs, cs]], axis=0)
                    vv = jnp.concatenate([vprev, v_ref[rs, cs]], axis=0)
                    mask = mask0
                else:
                    kk = k_ref[(b - 1) * BLK:(b + 1) * BLK, cs]
                    vv = v_ref[(b - 1) * BLK:(b + 1) * BLK, cs]
                    mask = mask_first if b % nb == 0 else band
                s = jnp.where(mask, _mm_nt(q2s, kk), NEG)
                m = jnp.max(s, axis=-1, keepdims=True)
                p = jnp.exp(s - m)
                l = jnp.sum(p, axis=-1, keepdims=True)
                o = _mm(p.astype(BF16), vv) / l
                lse = m + jnp.log(l)
                o_ref[rs, cs] = jnp.where(in_a[0], o[:BLK], o[BLK:]).astype(BF16)
                stat = jnp.where(lane == 2 * hp, lse[:BLK], stat)
                stat = jnp.where(lane == 2 * hp + 1, lse[BLK:], stat)
            lse_ref[rs, :] = stat

    cur = pl.BlockSpec((CH, AW), lambda i: (i, 0))
    prev = pl.BlockSpec((BLK, AW), lambda i: (jnp.maximum(n * i - 1, 0), 0))
    return pl.pallas_call(
        body, name=name, grid=(T // CH,),
        in_specs=[cur, cur, cur] + ([prev, prev] if halo else []),
        out_specs=[cur, pl.BlockSpec((CH, 128), lambda i: (i, 0))],
        out_shape=[jax.ShapeDtypeStruct((T, AW), BF16), jax.ShapeDtypeStruct((T, 128), F32)],
        compiler_params=_cp(("parallel",)),
    )(*((q, k, v) + ((k, v) if halo else ())))


def _attn_bwd(q, k, v, do, st, nb, name):
    n = 8
    CH = n * BLK
    NBLK = T // BLK
    halo = nb > n

    def body(*refs):
        if halo:
            (q_ref, k_ref, v_ref, do_ref, st_ref, kp_ref, vp_ref, qn_ref, don_ref, stn_ref,
             dq_ref, dk_ref, dv_ref) = refs
        else:
            q_ref, k_ref, v_ref, do_ref, st_ref, dq_ref, dk_ref, dv_ref = refs
        i = pl.program_id(0)
        lane = lax.broadcasted_iota(jnp.int32, (1, 128), 1)
        in_a = [lane < HEAD, lane >= HEAD]
        band, kj, _ = _band_mask(0)
        thr0 = jnp.where((n * i) % nb == 0, BLK, 0) if halo else BLK
        mask0 = band & (kj >= thr0)
        mask_first = band & (kj >= BLK)

        def stat_rows(st_t, hp):
            lse_r = jnp.concatenate([st_t[2 * hp:2 * hp + 1, :], st_t[2 * hp + 1:2 * hp + 2, :]], axis=1)
            dl_r = jnp.concatenate([st_t[8 + 2 * hp:9 + 2 * hp, :], st_t[9 + 2 * hp:10 + 2 * hp, :]], axis=1)
            return lse_r, dl_r

        st_t = [st_ref[b * BLK:(b + 1) * BLK, :].T for b in range(n)]
        if halo:
            nxt_thr = jnp.where((n * i + n) % nb == 0, 2 * BLK, 0)
            _, kj1, qi1 = _band_mask(0, BLK)
            mask_next = kj1 >= qi1 + nxt_thr
            stn_t = stn_ref[...].T

        for hp in range(4):
            cs = slice(hp * 128, (hp + 1) * 128)
            kb = [k_ref[b * BLK:(b + 1) * BLK, cs] for b in range(n)]
            vb = [v_ref[b * BLK:(b + 1) * BLK, cs] for b in range(n)]
            dk_acc = [jnp.zeros((BLK, 128), F32) for _ in range(n)]
            dv_acc = [jnp.zeros((BLK, 128), F32) for _ in range(n)]
            for b in range(n):
                rs = slice(b * BLK, (b + 1) * BLK)
                q2s = _stack_heads(q_ref[rs, cs], in_a)
                do2s = _stack_heads(do_ref[rs, cs], in_a)
                if b == 0:
                    kprev = kp_ref[:, cs] if halo else kb[0]
                    vprev = vp_ref[:, cs] if halo else vb[0]
                    mask = mask0
                else:
                    kprev, vprev, mask = kb[b - 1], vb[b - 1], (mask_first if b % nb == 0 else band)
                kk = jnp.concatenate([kprev, kb[b]], axis=0)
                vv = jnp.concatenate([vprev, vb[b]], axis=0)
                lse_r, dl_r = stat_rows(st_t[b], hp)
                s_t = jnp.where(mask, _mm_nt(kk, q2s), NEG)
                p_t = jnp.exp(s_t - lse_r)
                ds_t = (p_t * (_mm_nt(vv, do2s) - dl_r)).astype(BF16)
                dkk = _mm(ds_t, q2s)
                dvv = _mm(p_t.astype(BF16), do2s)
                dqs = _mm_tn(ds_t, kk) * SCALE
                dq_ref[rs, cs] = jnp.where(in_a[0], dqs[:BLK], dqs[BLK:]).astype(BF16)
                dk_acc[b] += dkk[BLK:]
                dv_acc[b] += dvv[BLK:]
                if b > 0:
                    dk_acc[b - 1] += dkk[:BLK]
                    dv_acc[b - 1] += dvv[:BLK]
            if halo:
                q2s = _stack_heads(qn_ref[:, cs], in_a)
                do2s = _stack_heads(don_ref[:, cs], in_a)
                lse_r, dl_r = stat_rows(stn_t, hp)
                s_t = jnp.where(mask_next, _mm_nt(kb[n - 1], q2s), NEG)
                p_t = jnp.exp(s_t - lse_r)
                ds_t = (p_t * (_mm_nt(vb[n - 1], do2s) - dl_r)).astype(BF16)
                dk_acc[n - 1] += _mm(ds_t, q2s)
                dv_acc[n - 1] += _mm(p_t.astype(BF16), do2s)
            for b in range(n):
                dk_ref[b * BLK:(b + 1) * BLK, cs] = dk_acc[b].astype(BF16)
                dv_ref[b * BLK:(b + 1) * BLK, cs] = dv_acc[b].astype(BF16)

    cur = pl.BlockSpec((CH, AW), lambda i: (i, 0))
    cur_st = pl.BlockSpec((CH, 128), lambda i: (i, 0))
    prev = pl.BlockSpec((BLK, AW), lambda i: (jnp.maximum(n * i - 1, 0), 0))
    nxt = pl.BlockSpec((BLK, AW), lambda i: (jnp.minimum(n * i + n, NBLK - 1), 0))
    nxt_st = pl.BlockSpec((BLK, 128), lambda i: (jnp.minimum(n * i + n, NBLK - 1), 0))
    ins = [cur] * 4 + [cur_st] + ([prev, prev, nxt, nxt, nxt_st] if halo else [])
    args = (q, k, v, do, st) + ((k, v, q, do, st) if halo else ())
    return pl.pallas_call(
        body, name=name, grid=(T // CH,),
        in_specs=ins,
        out_specs=[cur] * 3,
        out_shape=[jax.ShapeDtypeStruct((T, AW), BF16)] * 3,
        compiler_params=_cp(("parallel",)),
    )(*args)


TH = 256
NCH = TH // CHUNK


def _hgrn_common(hq_ref, hf_ref, lbr_ref, tri_ref):
    r0 = lbr_ref[0:1, :]
    r1 = lbr_ref[1:2, :]
    mx = jnp.maximum(r0, r1)
    e0 = jnp.exp(r0 - mx)
    e1 = jnp.exp(r1 - mx)
    lb = e0 / (e0 + e1)
    hqv = hq_ref[...].astype(F32)
    sq = _sigmoid(hqv)
    qv = hqv * sq
    sf = _sigmoid(hf_ref[...].astype(F32))
    f = lb + (1.0 - lb) * sf
    kv = 1.0 - f
    g = jnp.log(f)
    cum = _mm_exact_l(tri_ref[...], g)
    dec = jnp.exp(jnp.concatenate([cum[c * CHUNK + CHUNK - 1:(c + 1) * CHUNK, :] for c in range(NCH)], axis=0))
    decb = jnp.concatenate([jnp.broadcast_to(dec[c:c + 1, :], (CHUNK, HW)) for c in range(NCH)], axis=0)
    ea = jnp.exp(cum)
    ena = jnp.exp(-cum)
    eend = decb * ena
    return dict(lb=lb, hq=hqv, sq=sq, q=qv, sf=sf, f=f, k=kv, cum=cum, ea=ea, ena=ena, eend=eend,
                qd=qv * ea, ki=kv * ena, ke=kv * eend, dec=dec)


def _tri_mask(transposed=False):
    ti = lax.broadcasted_iota(jnp.int32, (TH, TH), 1 if transposed else 0)
    si = lax.broadcasted_iota(jnp.int32, (TH, TH), 0 if transposed else 1)
    return (si <= ti) & ((si // CHUNK) == (ti // CHUNK))


def _hgrn_fwd(hq, hf, hi, lbr, tri):
    NSUB = 2

    def body(hq_ref, hf_ref, hi_ref, lbr_ref, tri_ref, rec_ref, sall_ref, st_scr):
        @pl.when(pl.program_id(0) == 0)
        def _():
            st_scr[...] = jnp.zeros_like(st_scr)

        causal = _tri_mask()
        for u in range(NSUB):
            tile = slice(u * TH, (u + 1) * TH)
            w = _hgrn_common(hq_ref.at[tile, :], hf_ref.at[tile, :], lbr_ref, tri_ref)
            qd, ki, ke = w["qd"].astype(BF16), w["ki"].astype(BF16), w["ke"].astype(BF16)
            dec = w["dec"]
            vb = hi_ref[tile, :]
            for h in range(4):
                cs = slice(h * 128, (h + 1) * 128)
                att = jnp.where(causal, _mm_nt(qd[:, cs], ki[:, cs]), 0.0)
                o_intra = _mm(att.astype(BF16), vb[:, cs])
                st = st_scr[:, cs]
                for c in range(NCH):
                    rs = slice(c * CHUNK, (c + 1) * CHUNK)
                    sall_ref[u * NCH + c, :, cs] = st
                    rec_ref[u * TH + c * CHUNK:u * TH + (c + 1) * CHUNK, cs] = (
                        o_intra[rs] + _mm_nt(qd[rs, cs], st.astype(BF16))).astype(BF16)
                    st = dec[c:c + 1, cs] * st + _mm_tn(vb[rs, cs], ke[rs, cs])
                st_scr[:, cs] = st

    tok = pl.BlockSpec((NSUB * TH, HW), lambda i: (i, 0))
    return pl.pallas_call(
        body, name="hgrn_fwd", grid=(T // (NSUB * TH),),
        in_specs=[tok, tok, tok, pl.BlockSpec((2, HW), lambda i: (0, 0)), pl.BlockSpec((TH, TH), lambda i: (0, 0))],
        out_specs=[tok, pl.BlockSpec((NSUB * NCH, 128, HW), lambda i: (i, 0, 0))],
        out_shape=[jax.ShapeDtypeStruct((T, HW), BF16), jax.ShapeDtypeStruct((T // CHUNK, 128, HW), F32)],
        scratch_shapes=[pltpu.VMEM((128, HW), F32)],
        compiler_params=_cp(("arbitrary",)),
    )(hq, hf, hi, lbr, tri)


def _hgrn_bwd(hq, hf, hi, lbr, tri, trit, drec, sall, dhg, rout, routb):
    NSUB = 2
    NT = T // (NSUB * TH)

    def body(hq_ref, hf_ref, hi_ref, lbr_ref, tri_ref, trit_ref, do_ref, sall_ref, dhg_ref, rout_r, routb_r,
             dph_ref, small_ref, pout_o, poutr_o,
             dst_scr, dlb_scr, dqd_scr, dki_scr, dke_scr, dlast_scr, send_sems, recv_sems, loc_sems):
        step = pl.program_id(0)
        loc, rem = _chip_copies(_w_out_piece, rout_r, routb_r, pout_o, poutr_o, send_sems, recv_sems,
                                loc_sems.at[0])

        @pl.when(step == 0)
        def _():
            dst_scr[...] = jnp.zeros_like(dst_scr)
            dlb_scr[...] = jnp.zeros_like(dlb_scr)
            for cp in loc + rem:
                cp.start()

        causal = _tri_mask()
        causal_t = _tri_mask(transposed=True)
        lb = None
        for u in reversed(range(NSUB)):
            tile = slice(u * TH, (u + 1) * TH)
            w = _hgrn_common(hq_ref.at[tile, :], hf_ref.at[tile, :], lbr_ref, tri_ref)
            qd, ki, ke = w["qd"].astype(BF16), w["ki"].astype(BF16), w["ke"].astype(BF16)
            dec = w["dec"]
            vb = hi_ref[tile, :]
            dob = do_ref[tile, :].astype(BF16)
            for h in range(4):
                cs = slice(h * 128, (h + 1) * 128)
                att_t = jnp.where(causal_t, _mm_nt(ki[:, cs], qd[:, cs]), 0.0).astype(BF16)
                datt_t = jnp.where(causal_t, _mm_nt(vb[:, cs], dob[:, cs]), 0.0).astype(BF16)
                datt = jnp.where(causal, _mm_nt(dob[:, cs], vb[:, cs]), 0.0).astype(BF16)
                dv_intra = _mm(att_t, dob[:, cs])
                dqd_intra = _mm(datt, ki[:, cs])
                dki_scr[u, :, cs] = _mm(datt_t, qd[:, cs])
                dst = dst_scr[:, cs]
                for c in reversed(range(NCH)):
                    rs = slice(c * CHUNK, (c + 1) * CHUNK)
                    dec_c = dec[c:c + 1, :]
                    st = sall_ref[u * NCH + c, :, cs]
                    dstb = dst.astype(BF16)
                    dph_ref[u * TH + c * CHUNK:u * TH + (c + 1) * CHUNK, 2 * HW + h * 128:2 * HW + (h + 1) * 128] = (
                        dv_intra[rs] + _mm_nt(ke[rs, cs], dstb)).astype(BF16)
                    dqd_scr[u, rs, cs] = dqd_intra[rs] + _mm(dob[rs, cs], st.astype(BF16))
                    dke_scr[u, rs, cs] = _mm(vb[rs, cs], dstb)
                    ddec = jnp.sum(dst * st, axis=0, keepdims=True)
                    dlast_scr[u, c:c + 1, cs] = ddec * dec_c[:, cs]
                    dst = dec_c[:, cs] * dst + _mm_tn(dob[rs, cs], qd[rs, cs])
                dst_scr[:, cs] = dst
            dqd, dki, dke = dqd_scr[u], dki_scr[u], dke_scr[u]
            dq = dqd * w["ea"]
            dk = dki * w["ena"] + dke * w["eend"]
            dcum = dqd * w["qd"] - dki * w["ki"] - dke * w["ke"]
            dkeke = dke * w["ke"]
            dlastb = jnp.concatenate(
                [jnp.broadcast_to(dlast_scr[u, c:c + 1, :]
                                  + jnp.sum(dkeke[c * CHUNK:(c + 1) * CHUNK], axis=0, keepdims=True), (CHUNK, HW))
                 for c in range(NCH)], axis=0)
            dg = _mm_exact_l(trit_ref[...], dcum) + dlastb
            df = dg / w["f"] - dk
            lb, sf, sq = w["lb"], w["sf"], w["sq"]
            dph_ref[tile, HW:2 * HW] = (df * (1.0 - lb) * sf * (1.0 - sf)).astype(BF16)
            dph_ref[tile, 0:HW] = (dq * (sq * (1.0 + w["hq"] * (1.0 - sq)))).astype(BF16)
            dph_ref[tile, 3 * HW:4 * HW] = dhg_ref[tile, :]
            dlb_scr[...] += jnp.sum(df * (1.0 - sf), axis=0, keepdims=True)

        @pl.when(step == NT - 1)
        def _():
            gr = dlb_scr[...] * lb * (1.0 - lb)
            small_ref[...] = jnp.zeros_like(small_ref)
            small_ref[0:1, 0:HW] = gr
            small_ref[1:2, 0:HW] = -gr
            for cp in rem:
                cp.wait_recv()
            for cp in rem:
                cp.wait_send()
            for cp in loc:
                cp.wait()

    tok = pl.BlockSpec((NSUB * TH, HW), lambda i: (NT - 1 - i, 0))
    const = lambda shape: pl.BlockSpec(shape, lambda i: (0,) * len(shape))
    hbm = pl.BlockSpec(memory_space=pltpu.HBM)
    return pl.pallas_call(
        body, name="hgrn_bwd", grid=(NT,),
        in_specs=[tok, tok, tok, const((2, HW)), const((TH, TH)), const((TH, TH)), tok,
                  pl.BlockSpec((NSUB * NCH, 128, HW), lambda i: (NT - 1 - i, 0, 0)), tok, hbm, hbm],
        out_specs=[pl.BlockSpec((NSUB * TH, NCOL // 2), lambda i: (NT - 1 - i, 0)), const((8, D)), hbm, hbm],
        out_shape=[jax.ShapeDtypeStruct((T, NCOL // 2), BF16), jax.ShapeDtypeStruct((8, D), F32),
                   jax.ShapeDtypeStruct((128, D), F32), jax.ShapeDtypeStruct((3, 128, D), BF16)],
        scratch_shapes=[pltpu.VMEM((128, HW), F32), pltpu.VMEM((1, HW), F32), pltpu.VMEM((NSUB, TH, HW), F32),
                        pltpu.VMEM((NSUB, TH, HW), F32), pltpu.VMEM((NSUB, TH, HW), F32),
                        pltpu.VMEM((NSUB, 8, HW), F32),
                        pltpu.SemaphoreType.DMA((3,)), pltpu.SemaphoreType.DMA((3,)), pltpu.SemaphoreType.DMA((1,))],
        compiler_params=_cp(("arbitrary",)),
    )(hq, hf, hi, lbr, tri, trit, drec, sall, dhg, rout, routb)


def _fwd_out(o1, o4, o16, l1, l4, l16, rec, ag, hg, x, tgt, anw, hnw, fnw, wout_full, gmat, emat, selmat):
    TT = 512

    def body(o1_r, o4_r, o16_r, l1_r, l4_r, l16_r, rec_r, ag_r, hg_r, x_r, tgt_r, anw_r, hnw_r, fnw_r, wo_r, g_r,
             e_r, sel_r, dx2_o, do1_o, do4_o, do16_o, st1_o, st4_o, st16_o, drec_o, dag_o, dhg_o,
             rout_o, routb_o, small_o, scr_a, scr_b, scr_c, gwout_o, rbuf, send_sems, recv_sems):
        @pl.when(pl.program_id(0) == 0)
        def _():
            gwout_o[...] = jnp.zeros_like(gwout_o)
            small_o[...] = jnp.zeros_like(small_o)

        def unperm(r4, r16):
            return _unperm_load(r4, r16, scr_a, scr_b, scr_c)

        def perm_out(val, p1, p4, p16, dt):
            _perm_store(val, scr_a, scr_b, p1, p4, p16, dt)

        o4u, o16u = unperm(o4_r, o16_r)
        l4c, l16c = unperm(l4_r, l16_r)
        l1c = l1_r[...]
        mxc = jnp.maximum(jnp.maximum(l1c, l4c), l16c)
        w1c, w4c, w16c = jnp.exp(l1c - mxc), jnp.exp(l4c - mxc), jnp.exp(l16c - mxc)
        denc = w1c + w4c + w16c
        lane = lax.broadcasted_iota(jnp.int32, (1, 128), 1)
        lse_c = jnp.where(lane < 8, mxc + jnp.log(denc), 0.0)
        em = e_r[...]
        wn1 = _mm_exact_r(w1c / denc, em)
        wn4 = _mm_exact_r(w4c / denc, em)
        o1v = o1_r[...].astype(F32)
        attn = wn1 * o1v + wn4 * o4u + (1.0 - wn1 - wn4) * o16u
        gm = g_r[...]

        def head_mean_a(t):
            return jnp.concatenate([_mm_exact_r(t[:, :256], gm), _mm_exact_r(t[:, 256:], gm)], axis=1)

        def head_mean_h(t):
            return jnp.concatenate(
                [jnp.broadcast_to(jnp.mean(t[:, h * 128:(h + 1) * 128], axis=-1, keepdims=True), (TT, 128))
                 for h in range(4)], axis=1)

        rs_a = lax.rsqrt(head_mean_a(attn * attn) + EPS)
        n_a = attn * rs_a
        agv = ag_r[...].astype(F32)
        sg_a = _sigmoid(agv)
        si_a = agv * sg_a
        anw_v = anw_r[...]
        y_a = (n_a * anw_v) * si_a
        recv = rec_r[...].astype(F32)
        rs_h = lax.rsqrt(head_mean_h(recv * recv) + EPS)
        n_h = recv * rs_h
        hgv = hg_r[...].astype(F32)
        sg_h = _sigmoid(hgv)
        si_h = hgv * sg_h
        hnw_v = hnw_r[...]
        y_h = (n_h * hnw_v) * si_h
        mixed = jnp.concatenate([y_a, y_h], axis=1).astype(BF16)
        xv = x_r[...]
        x2 = xv + _mm(mixed, wo_r[...])
        r2 = lax.rsqrt(jnp.mean(x2 * x2, axis=-1, keepdims=True) + EPS)
        fnw_v = fnw_r[...]
        xn = x2 * r2
        err = xn * fnw_v - tgt_r[...]
        small_o[2:3, :] += 0.5 * jnp.sum(jnp.mean(err * err, axis=-1, keepdims=True), axis=0, keepdims=True)
        dy = err * (1.0 / D)
        small_o[0:1, :] += jnp.sum(dy * xn, axis=0, keepdims=True)
        dyw = dy * fnw_v
        dx2 = r2 * dyw - x2 * ((r2 * r2 * r2) * jnp.mean(dyw * x2, axis=-1, keepdims=True))
        dx2_o[...] = dx2
        dx2b = dx2.astype(BF16)
        gwout_o[...] += _mm_tn(mixed, dx2b)
        dmix = _mm_nt(dx2b, wo_r[...])
        dm_a, dm_h = dmix[:, :AW], dmix[:, AW:]
        dag_o[...] = (dm_a * (n_a * anw_v) * (sg_a * (1.0 + agv * (1.0 - sg_a)))).astype(BF16)
        dn_a = dm_a * anw_v * si_a
        small_o[1:2, 0:AW] += jnp.sum(dm_a * n_a * si_a, axis=0, keepdims=True)
        dattn = rs_a * (dn_a - n_a * head_mean_a(dn_a * n_a))
        perm_out(dattn, do1_o, do4_o, do16_o, BF16)
        stats = lse_c + _mm_exact_r(dattn * attn, sel_r[...])
        perm_out(stats, st1_o, st4_o, st16_o, F32)
        dhg_o[...] = (dm_h * (n_h * hnw_v) * (sg_h * (1.0 + hgv * (1.0 - sg_h)))).astype(BF16)
        dn_h = dm_h * hnw_v * si_h
        small_o[1:2, AW:] += jnp.sum(dm_h * n_h * si_h, axis=0, keepdims=True)
        drec_o[...] = (rs_h * (dn_h - n_h * head_mean_h(dn_h * n_h))).astype(BF16)

        @pl.when(pl.program_id(0) == T // TT - 1)
        def _():
            x, y, c = lax.axis_index("x"), lax.axis_index("y"), lax.axis_index("c")
            cps = [pltpu.make_async_remote_copy(
                src_ref=gwout_o.at[pl.ds(pl.multiple_of(j * 256 + (1 - c) * 128, 128), 128), :], dst_ref=rbuf.at[j],
                send_sem=send_sems.at[j], recv_sem=recv_sems.at[j], device_id=(x, y, 1 - c), device_id_type=MESH)
                for j in range(4)]
            for cp in cps:
                cp.start()
            for j, cp in enumerate(cps):
                cp.wait_recv()
                red = gwout_o[pl.ds(pl.multiple_of(j * 256 + c * 128, 128), 128), :] + rbuf[j]
                rout_o[j * 128:(j + 1) * 128, :] = red
                routb_o[j * 128:(j + 1) * 128, :] = red.astype(BF16)
            for cp in cps:
                cp.wait_send()

    tok = lambda w: pl.BlockSpec((TT, w), lambda i: (i, 0))
    d4 = pl.BlockSpec((4, TT // 4, AW), lambda i: (0, i, 0))
    d16 = pl.BlockSpec((16, TT // 16, AW), lambda i: (0, i, 0))
    const = lambda shape: pl.BlockSpec(shape, lambda i: (0,) * len(shape))
    sd = lambda shape, dt: jax.ShapeDtypeStruct(shape, dt)
    c4 = pl.BlockSpec((4, TT // 4, 128), lambda i: (0, i, 0))
    c16 = pl.BlockSpec((16, TT // 16, 128), lambda i: (0, i, 0))
    p3 = lambda w, dt: [sd((T, w), dt), sd((4, T // 4, w), dt), sd((16, T // 16, w), dt)]
    return pl.pallas_call(
        body, name="fwd_out", grid=(T // TT,),
        in_specs=[tok(AW), d4, d16, tok(128), c4, c16, tok(AW), tok(AW), tok(AW), tok(D), tok(D),
                  const((1, AW)), const((1, HW)), const((1, D)), const((D, D)), const((256, 256)),
                  const((128, AW)), const((AW, 128))],
        out_specs=[tok(D)] + [tok(AW), d4, d16] + [tok(128), c4, c16] + [tok(AW)] * 3
        + [const((512, D)), const((512, D)), const((8, D))],
        out_shape=[sd((T, D), F32)] + p3(AW, BF16) + p3(128, F32)
        + [sd((T, AW), BF16), sd((T, AW), BF16), sd((T, AW), BF16), sd((512, D), F32), sd((512, D), BF16),
           sd((8, D), F32)],
        scratch_shapes=[pltpu.VMEM((4, TT, 128), F32)] * 3 + [pltpu.VMEM((D, D), F32),
                        pltpu.VMEM((4, 128, D), F32), pltpu.SemaphoreType.DMA((4,)), pltpu.SemaphoreType.DMA((4,))],
        compiler_params=_cp(("arbitrary",)),
    )(o1, o4, o16, l1, l4, l16, rec, ag, hg, x, tgt, anw, hnw, fnw, wout_full, gmat, emat, selmat)


def _dproj_build(dq, dk, dv, dag, pos):
    TT = 512

    def body(dq1, dq4, dq16, dk1, dk4, dk16, dv1, dv4, dv16, dag_r, pos_r, dproj_o, scr_b, scr_c):
        def unperm_sum(r1, r4, r16):
            return r1[...] + _unperm_sum(r4, r16, scr_b, scr_c)

        cosf, s1, s2 = _rope_tables(pos_r[...])
        dproj_o[:, 0:512] = _rope_bwd(unperm_sum(dq1, dq4, dq16), cosf, s1, s2).astype(BF16)
        dproj_o[:, 512:1024] = _rope_bwd(unperm_sum(dk1, dk4, dk16), cosf, s1, s2).astype(BF16)
        dproj_o[:, 1024:1536] = unperm_sum(dv1, dv4, dv16).astype(BF16)
        dproj_o[:, 1536:2048] = dag_r[...]

    tok = lambda w: pl.BlockSpec((TT, w), lambda i: (i, 0))
    d4 = pl.BlockSpec((4, TT // 4, AW), lambda i: (0, i, 0))
    d16 = pl.BlockSpec((16, TT // 16, AW), lambda i: (0, i, 0))
    return pl.pallas_call(
        body, name="dproj_build", grid=(T // TT,),
        in_specs=[tok(AW), d4, d16] * 3 + [tok(AW), tok(1)],
        out_specs=tok(NCOL // 2),
        out_shape=jax.ShapeDtypeStruct((T, NCOL // 2), BF16),
        scratch_shapes=[pltpu.VMEM((4, TT, 128), F32)] * 2,
        compiler_params=_cp(("parallel",)),
    )(*dq, *dk, *dv, dag, pos)


def _bwd_x(dproj_a, dproj_h, x, dx2, mixw, w_full, rin, rinb, small4, small6, pout_own, pout_rem):
    TT = 256
    NT = T // TT

    def body(dpa_r, dph_r, x_r, dx2_r, mw_r, w_r, rin_r, rinb_r, s4_r, s6_r, poo_r, por_r,
             gx_o, pin_o, pinr_o, sall_o, fin_o, fout_o, sbuf, v_own, v_rem, vo_own, vo_rem, sin, sout, got_in,
             got_out, send_sems, recv_sems, loc_sems, share_send, share_recv, fin_sems):
        i = pl.program_id(0)
        loc, rem = _chip_copies(_w_in_piece, rin_r, rinb_r, pin_o, pinr_o, send_sems, recv_sems, loc_sems.at[0])

        @pl.when(i == 0)
        def _():
            sbuf[...] = jnp.zeros_like(sbuf)
            for cp in loc + rem:
                cp.start()

        dhn = _mm_nt(dpa_r[...], w_r[:, 0:NCOL // 2]) + _mm_nt(dph_r[...], w_r[:, NCOL // 2:NCOL])
        xv = x_r[...]
        r = lax.rsqrt(jnp.mean(xv * xv, axis=-1, keepdims=True) + EPS)
        dxw = dhn * mw_r[...]
        gx_o[...] = dx2_r[...] + r * dxw - xv * ((r * r * r) * jnp.mean(dxw * xv, axis=-1, keepdims=True))
        sbuf[16:17, :] += jnp.sum(dhn * (xv * r), axis=0, keepdims=True)

        @pl.when(i == NT - 1)
        def _():
            sbuf[0:8, :] = s4_r[...]
            sbuf[8:16, :] = s6_r[...]
            sloc, srem = _small_copies(sbuf, sall_o, send_sems, recv_sems, loc_sems.at[1])
            for cp in sloc + srem:
                cp.start()
            for cp in rem:
                cp.wait_recv()
            for cp in rem:
                cp.wait_send()
            for cp in loc:
                cp.wait()
            mx, my, c = lax.axis_index("x"), lax.axis_index("y"), lax.axis_index("c")
            loads = [pltpu.make_async_copy(pin_o, v_own, fin_sems.at[0]),
                     pltpu.make_async_copy(pinr_o, v_rem, fin_sems.at[1]),
                     pltpu.make_async_copy(poo_r, vo_own, fin_sems.at[2]),
                     pltpu.make_async_copy(por_r, vo_rem, fin_sems.at[3])]
            for cp in loads:
                cp.start()
            for cp in loads:
                cp.wait()
            sout[...] = ((vo_own[...] + vo_rem[0].astype(F32)) + vo_rem[1].astype(F32)) + vo_rem[2].astype(F32)
            sin[...] = ((v_own[...] + v_rem[0].astype(F32)) + v_rem[1].astype(F32)) + v_rem[2].astype(F32)
            swap = [pltpu.make_async_remote_copy(src_ref=sin, dst_ref=got_in, send_sem=share_send.at[0],
                                                 recv_sem=share_recv.at[0], device_id=(mx, my, 1 - c),
                                                 device_id_type=MESH),
                    pltpu.make_async_remote_copy(src_ref=sout, dst_ref=got_out, send_sem=share_send.at[1],
                                                 recv_sem=share_recv.at[1], device_id=(mx, my, 1 - c),
                                                 device_id_type=MESH)]
            for cp in swap:
                cp.start()
            mine = [pltpu.make_async_copy(sin, fin_o.at[c], fin_sems.at[0]),
                    pltpu.make_async_copy(sout, fout_o.at[c], fin_sems.at[1])]
            for cp in mine:
                cp.start()
            for cp in swap:
                cp.wait_recv()
            theirs = [pltpu.make_async_copy(got_in, fin_o.at[1 - c], fin_sems.at[2]),
                      pltpu.make_async_copy(got_out, fout_o.at[1 - c], fin_sems.at[3])]
            for cp in theirs:
                cp.start()
            for cp in swap:
                cp.wait_send()
            for cp in mine + theirs:
                cp.wait()
            for cp in srem:
                cp.wait_recv()
            for cp in srem:
                cp.wait_send()
            for cp in sloc:
                cp.wait()

    tok = lambda w: pl.BlockSpec((TT, w), lambda i: (i, 0))
    const = lambda shape: pl.BlockSpec(shape, lambda i: (0,) * len(shape))
    hbm = pl.BlockSpec(memory_space=pltpu.HBM)
    return pl.pallas_call(
        body, name="bwd_x", grid=(NT,),
        in_specs=[tok(NCOL // 2), tok(NCOL // 2), tok(D), tok(D), const((1, D)), const((D, NCOL)), hbm, hbm,
                  const((8, D)), const((8, D)), hbm, hbm],
        out_specs=[tok(D), hbm, hbm, hbm, hbm, hbm],
        out_shape=[jax.ShapeDtypeStruct((T, D), F32),
                   jax.ShapeDtypeStruct((512, 1024), F32), jax.ShapeDtypeStruct((3, 512, 1024), BF16),
                   jax.ShapeDtypeStruct((8, 24, D), F32),
                   jax.ShapeDtypeStruct((2, 512, 1024), F32), jax.ShapeDtypeStruct((2, 128, D), F32)],
        scratch_shapes=[pltpu.VMEM((24, D), F32),
                        pltpu.VMEM((512, 1024), F32), pltpu.VMEM((3, 512, 1024), BF16),
                        pltpu.VMEM((128, D), F32), pltpu.VMEM((3, 128, D), BF16),
                        pltpu.VMEM((512, 1024), F32), pltpu.VMEM((128, D), F32),
                        pltpu.VMEM((512, 1024), F32), pltpu.VMEM((128, D), F32),
                        pltpu.SemaphoreType.DMA((10,)), pltpu.SemaphoreType.DMA((10,)), pltpu.SemaphoreType.DMA((2,)),
                        pltpu.SemaphoreType.DMA((2,)), pltpu.SemaphoreType.DMA((2,)), pltpu.SemaphoreType.DMA((4,))],
        compiler_params=_cp(("arbitrary",)),
    )(dproj_a, dproj_h, x, dx2, mixw, w_full, rin, rinb, small4, small6, pout_own, pout_rem)


def _grad_w_in(hn, dproj_a, dproj_h):
    TK = 2048
    NK = T // TK

    def body(hnt_r, dpa_r, dph_r, rin_o, rinb_o, acc, rbuf, obuf, obufb, send_sems, recv_sems, wb_sems):
        j = pl.program_id(0)
        kk = pl.program_id(1)
        x, y, c = lax.axis_index("x"), lax.axis_index("y"), lax.axis_index("c")
        mine = pl.ds(pl.multiple_of(c * 512, 512), 512)
        theirs = pl.ds(pl.multiple_of((1 - c) * 512, 512), 512)

        def send(jj):
            return pltpu.make_async_remote_copy(
                src_ref=acc.at[jj % 2, theirs, :], dst_ref=rbuf.at[jj], send_sem=send_sems.at[jj],
                recv_sem=recv_sems.at[jj], device_id=(x, y, 1 - c), device_id_type=MESH)

        def writeback(jj):
            cols = pl.ds(jj * 1024, 1024)
            return [pltpu.make_async_copy(obuf.at[jj % 2], rin_o.at[:, cols], wb_sems.at[jj % 2]),
                    pltpu.make_async_copy(obufb.at[jj % 2], rinb_o.at[:, cols], wb_sems.at[2 + jj % 2])]

        def wait_writeback(jj):
            for cp in writeback(jj):
                cp.wait()

        def finalize(jj):
            send(jj).wait_recv()
            red = acc[jj % 2, mine, :] + rbuf[jj]
            obuf[jj % 2] = red
            obufb[jj % 2] = red.astype(BF16)
            for cp in writeback(jj):
                cp.start()

        prod = _mm(hnt_r[...], jnp.where(j < 2, dpa_r[...], dph_r[...]))

        @pl.when(kk == 0)
        def _():
            for jj in (2, 3):
                @pl.when(j == jj)
                def _():
                    send(jj - 2).wait_send()
            acc[j % 2] = prod

        @pl.when(kk > 0)
        def _():
            acc[j % 2] += prod

        @pl.when(kk == NK - 1)
        def _():
            for jj in range(4):
                @pl.when(j == jj)
                def _():
                    send(jj).start()
                    if jj in (1, 2):
                        finalize(jj - 1)
                    if jj == 3:
                        wait_writeback(0)
                        finalize(2)
                        wait_writeback(1)
                        finalize(3)
                        wait_writeback(2)
                        wait_writeback(3)
                        send(2).wait_send()
                        send(3).wait_send()

    hbm = pl.BlockSpec(memory_space=pltpu.HBM)
    return pl.pallas_call(
        body, name="grad_w_in", grid=(4, NK),
        in_specs=[pl.BlockSpec((D, TK), lambda j, kk: (0, kk)),
                  pl.BlockSpec((TK, 1024), lambda j, kk: (jnp.where(j < 2, kk, NK - 1), jnp.minimum(j, 1))),
                  pl.BlockSpec((TK, 1024), lambda j, kk: (jnp.where(j < 2, 0, kk), jnp.maximum(j - 2, 0)))],
        out_specs=[hbm, hbm],
        out_shape=[jax.ShapeDtypeStruct((512, NCOL), F32), jax.ShapeDtypeStruct((512, NCOL), BF16)],
        scratch_shapes=[pltpu.VMEM((2, D, 1024), F32), pltpu.VMEM((4, 512, 1024), F32), pltpu.VMEM((2, 512, 1024), F32),
                        pltpu.VMEM((2, 512, 1024), BF16),
                        pltpu.SemaphoreType.DMA((4,)), pltpu.SemaphoreType.DMA((4,)), pltpu.SemaphoreType.DMA((4,))],
        compiler_params=_cp(("arbitrary", "arbitrary")),
    )(hn, dproj_a, dproj_h)


def _w_in_piece(ref, j):
    return ref.at[:, pl.ds(j * 1024, 1024)]


def _w_out_piece(ref, j):
    return ref.at[pl.ds(j * 128, 128), :]


def _chip_copies(piece, src_r, srcb_r, own_o, rem_o, send_sems, recv_sems, loc_sem):
    x, y, c = lax.axis_index("x"), lax.axis_index("y"), lax.axis_index("c")
    chips = [(1 - x, y), (x, 1 - y), (1 - x, 1 - y)]
    loc = [pltpu.make_async_copy(piece(src_r, 2 * x + y), own_o, loc_sem)]
    rem = [pltpu.make_async_remote_copy(
        src_ref=piece(srcb_r, 2 * px + py), dst_ref=rem_o.at[k], send_sem=send_sems.at[k],
        recv_sem=recv_sems.at[k], device_id=(px, py, c), device_id_type=MESH) for k, (px, py) in enumerate(chips)]
    return loc, rem


def _small_copies(small_r, sall_o, send_sems, recv_sems, loc_sem):
    x, y, c = lax.axis_index("x"), lax.axis_index("y"), lax.axis_index("c")
    me = 4 * x + 2 * y + c
    loc = [pltpu.make_async_copy(small_r, sall_o.at[me], loc_sem)]
    rem = []
    k = 3
    for fx in range(2):
        for fy in range(2):
            for fc in range(2):
                if fx or fy or fc:
                    peer = (1 - x if fx else x, 1 - y if fy else y, 1 - c if fc else c)
                    rem.append(pltpu.make_async_remote_copy(
                        src_ref=small_r, dst_ref=sall_o.at[me], send_sem=send_sems.at[k],
                        recv_sem=recv_sems.at[k], device_id=peer, device_id_type=MESH))
                    k += 1
    return loc, rem


def _adamw_math(w, g, m, v):
    m = B1 * m + (1.0 - B1) * g
    v = B2 * v + (1.0 - B2) * (g * g)
    m_hat = m / (1.0 - B1 ** STEP)
    v_hat = v / (1.0 - B2 ** STEP)
    delta = -LR * (m_hat / (jnp.sqrt(v_hat) + AEPS) + WD * w)
    return delta, m, v


def _adamw(big_in, big_out, sall, params):
    def body(*refs):
        wi, gi, mi, vi, wo, go, mo, vo, sall_r = refs[:9]
        ins = refs[9:24]
        di_o, mi_o, vi_o, do_o, mo_o, vo_o = refs[24:30]
        outs = refs[30:]
        d, mm, vv = _adamw_math(wi[...], gi[...], mi[...], vi[...])
        di_o[...] = d
        mi_o[...] = mm
        vi_o[...] = vv

        @pl.when(pl.program_id(0) == 0)
        def _():
            d, mm, vv = _adamw_math(wo[...], go[...], mo[...], vo[...])
            do_o[...] = d
            mo_o[...] = mm
            vo_o[...] = vv
            tot = sall_r[0]
            for dv in range(1, 8):
                tot = tot + sall_r[dv]
            grads = [tot[16:17, :], tot[1:2, 0:AW], tot[1:2, AW:], tot[8:10, 0:HW], tot[0:1, :]]
            outs[0][...] = tot[2:3, 0:1]
            for p in range(5):
                w_r, m_r, v_r = ins[3 * p:3 * p + 3]
                g = grads[p]
                d, mm, vv = _adamw_math(w_r[...], g, m_r[...], v_r[...])
                outs[1 + 4 * p][...] = g
                outs[2 + 4 * p][...] = d
                outs[3 + 4 * p][...] = mm
                outs[4 + 4 * p][...] = vv

    flat = [a for p in params for a in p]
    shapes = [jax.ShapeDtypeStruct((D, 1024), F32)] * 3 + [jax.ShapeDtypeStruct((256, D), F32)] * 3
    shapes += [jax.ShapeDtypeStruct((1, 1), F32)]
    for p in params:
        shapes += [jax.ShapeDtypeStruct(p[0].shape, F32)] * 4
    vm = pl.BlockSpec(memory_space=pltpu.VMEM)
    rows = pl.BlockSpec((512, 1024), lambda i: (i, 0))
    whole = pl.BlockSpec((256, D), lambda i: (0, 0))
    return pl.pallas_call(
        body, name="adamw", grid=(2,),
        in_specs=[rows] * 4 + [whole] * 4 + [vm] * 16, out_specs=[rows] * 3 + [whole] * 3 + [vm] * 21,
        out_shape=shapes,
        compiler_params=_cp(("arbitrary",)),
    )(*big_in, *big_out, sall, *flat)


def kernel(x, positions, w_in, w_out, mix_norm_w, attn_out_norm_w, hgrn_out_norm_w, hgrn_lb_raw, final_norm_w, loss_target, m_w_in, m_w_out, m_mix_norm_w, m_attn_out_norm_w, m_hgrn_out_norm_w, m_hgrn_lb_raw, m_final_norm_w, v_w_in, v_w_out, v_mix_norm_w, v_attn_out_norm_w, v_hgrn_out_norm_w, v_hgrn_lb_raw, v_final_norm_w):
    xs = x.reshape(T, D)
    tgt = loss_target.reshape(T, D)
    pos = positions.reshape(T, 1)
    fnw = final_norm_w.reshape(1, D)

    ti = np.arange(TH)
    tri_np = ((ti[:, None] // CHUNK == ti[None, :] // CHUNK) & (ti[None, :] <= ti[:, None])).astype(np.float32)
    tri = jnp.asarray(tri_np, BF16)
    trit = jnp.asarray(tri_np.T, BF16)
    hi_ = np.arange(AW) // HEAD
    gmat = jnp.asarray((hi_[:256, None] == hi_[None, :256]).astype(np.float32) / HEAD, BF16)
    emat_np = (np.arange(128)[:, None] == hi_[None, :]).astype(np.float32)
    sel_np = (8 + hi_[:, None] == np.arange(128)[None, :]).astype(np.float32)
    emat = jnp.asarray(emat_np, BF16)
    selmat = jnp.asarray(sel_np, BF16)

    jm_arr = (2 * lax.axis_index("x") + lax.axis_index("y")).astype(jnp.int32).reshape(1)
    (hn, q1, k1, v1, q4, k4, v4, q16, k16, v16, ag, hq, hf, hi, hg, w_full, wout4) = _fwd_in(
        xs, pos, mix_norm_w, w_in.reshape(D, 1024), w_out.reshape(256, D), jm_arr)
    wout_full = wout4.reshape(D, D)
    flat = lambda a: a.reshape(T, AW)
    o1, l1 = _attn_fwd(q1, k1, v1, T // BLK, "attn_fwd_d1")
    o4, l4 = _attn_fwd(flat(q4), flat(k4), flat(v4), T // 4 // BLK, "attn_fwd_d4")
    o16, l16 = _attn_fwd(flat(q16), flat(k16), flat(v16), T // 16 // BLK, "attn_fwd_d16")
    rec, sall = _hgrn_fwd(hq, hf, hi, hgrn_lb_raw, tri)

    (dx2, do1, do4, do16, st1, st4, st16, drec, dag, dhg, rout, routb, small4) = _fwd_out(
        o1, o4.reshape(4, T // 4, AW), o16.reshape(16, T // 16, AW),
        l1, l4.reshape(4, T // 4, 128), l16.reshape(16, T // 16, 128),
        rec, ag, hg, xs, tgt, attn_out_norm_w, hgrn_out_norm_w, fnw, wout_full, gmat, emat, selmat)

    fst = lambda a: a.reshape(T, 128)
    dq1, dk1, dv1 = _attn_bwd(q1, k1, v1, do1, st1, T // BLK, "attn_bwd_d1")
    dq4, dk4, dv4 = _attn_bwd(flat(q4), flat(k4), flat(v4), flat(do4), fst(st4), T // 4 // BLK, "attn_bwd_d4")
    dq16, dk16, dv16 = _attn_bwd(flat(q16), flat(k16), flat(v16), flat(do16), fst(st16), T // 16 // BLK,
                                 "attn_bwd_d16")
    dproj_h, small6, pout_own, pout_rem = _hgrn_bwd(hq, hf, hi, hgrn_lb_raw, tri, trit, drec, sall, dhg,
                                                    rout, routb)

    r4 = lambda a: a.reshape(4, T // 4, AW)
    r16 = lambda a: a.reshape(16, T // 16, AW)
    dproj_a = _dproj_build((dq1, r4(dq4), r16(dq16)), (dk1, r4(dk4), r16(dk16)), (dv1, r4(dv4), r16(dv16)),
                           dag, pos)
    rin, rinb = _grad_w_in(hn, dproj_a, dproj_h)
    gx, _, _, small_all, fin, fout = _bwd_x(dproj_a, dproj_h, xs, dx2, mix_norm_w, w_full, rin, rinb,
                                            small4, small6, pout_own, pout_rem)
    g_w_in = fin.reshape(D, 1024)
    g_w_out = fout.reshape(256, D)

    params = [(mix_norm_w, m_mix_norm_w, v_mix_norm_w),
              (attn_out_norm_w, m_attn_out_norm_w, v_attn_out_norm_w),
              (hgrn_out_norm_w, m_hgrn_out_norm_w, v_hgrn_out_norm_w),
              (hgrn_lb_raw, m_hgrn_lb_raw, v_hgrn_lb_raw),
              (fnw, m_final_norm_w.reshape(1, D), v_final_norm_w.reshape(1, D))]
    d_in, nm_in, nv_in, d_out, nm_out, nv_out, *so = _adamw(
        (w_in.reshape(D, 1024), g_w_in, m_w_in.reshape(D, 1024), v_w_in.reshape(D, 1024)),
        (w_out.reshape(256, D), g_w_out, m_w_out.reshape(256, D), v_w_out.reshape(256, D)), small_all, params)
    loss = so[0].reshape(())
    g_s = [so[1 + 4 * p] for p in range(5)]
    d_s = [so[2 + 4 * p] for p in range(5)]
    m_s = [so[3 + 4 * p] for p in range(5)]
    v_s = [so[4 + 4 * p] for p in range(5)]
    for lst in (g_s, d_s, m_s, v_s):
        lst[4] = lst[4].reshape(D)

    return (loss, gx.reshape(1, T, D),
            g_w_in.reshape(1, D, 1024), g_w_out.reshape(1, 256, D), *g_s,
            d_in.reshape(1, D, 1024), d_out.reshape(1, 256, D), *d_s,
            nm_in.reshape(1, D, 1024), nm_out.reshape(1, 256, D), *m_s,
            nv_in.reshape(1, D, 1024), nv_out.reshape(1, 256, D), *v_s)
```

```python
import functools

import numpy as np
import jax
import jax.numpy as jnp
from jax import lax
from jax.experimental import pallas as pl
from jax.experimental.pallas import tpu as pltpu

F32 = jnp.float32
BF16 = jnp.bfloat16

T = 4096
D = 1024
AW = 512
HW = 512
NCOL = 4096
HEAD = 64
BLK = 128
CHUNK = 64
EPS = 1e-6
SCALE = HEAD ** -0.5
NEG = -1e30
ROPE_THETA = 500000.0
INV_FREQ = [float(v) for v in
            (np.float32(ROPE_THETA) ** (-(np.arange(8, dtype=np.float32)) * np.float32(0.125)))]
LR, B1, B2, AEPS, WD, STEP = 0.001, 0.9, 0.999, 1e-08, 0.01, 10
VMEM_LIMIT = 63 * 1024 * 1024
MESH = pl.DeviceIdType.MESH


def _cp(sem=None, **kw):
    return pltpu.CompilerParams(dimension_semantics=sem, vmem_limit_bytes=VMEM_LIMIT, **kw)


def _mm(a, b):
    return jnp.dot(a, b, preferred_element_type=F32)


def _mm_nt(a, b):
    return lax.dot_general(a, b, (((1,), (1,)), ((), ())), preferred_element_type=F32)


def _mm_tn(a, b):
    return lax.dot_general(a, b, (((0,), (0,)), ((), ())), preferred_element_type=F32)


def _mm_exact_l(mat_bf, x):
    h = x.astype(BF16)
    l = (x - h.astype(F32)).astype(BF16)
    return _mm(mat_bf, h) + _mm(mat_bf, l)


def _mm_exact_r(x, mat_bf):
    h = x.astype(BF16)
    l = (x - h.astype(F32)).astype(BF16)
    return _mm(h, mat_bf) + _mm(l, mat_bf)


def _sigmoid(x):
    return 0.5 * jnp.tanh(0.5 * x) + 0.5


def _rope_tables(pos):
    lane = lax.broadcasted_iota(jnp.int32, (1, 128), 1)
    jl = lane & 63
    fi = jl & 7
    inv = jnp.zeros((1, 128), F32)
    for kk in range(8):
        inv = jnp.where(fi == kk, INV_FREQ[kk], inv)
    ang = pos.astype(F32) * inv
    c = jnp.cos(ang)
    s = jnp.sin(ang)
    cosf = jnp.where(jl < 16, c, 1.0)
    s1 = jnp.where(jl < 8, -s, 0.0)
    s2 = jnp.where((jl >= 8) & (jl < 16), s, 0.0)
    return cosf, s1, s2


def _rope(t, cosf, s1, s2):
    parts = []
    for ci in range(t.shape[1] // 128):
        tc = t[:, ci * 128:(ci + 1) * 128]
        parts.append(tc * cosf + pltpu.roll(tc, 120, 1) * s1 + pltpu.roll(tc, 8, 1) * s2)
    return jnp.concatenate(parts, axis=1)


def _rope_bwd(g, cosf, s1, s2):
    parts = []
    for ci in range(g.shape[1] // 128):
        gc = g[:, ci * 128:(ci + 1) * 128]
        parts.append(gc * cosf + pltpu.roll(gc * s1, 8, 1) + pltpu.roll(gc * s2, 120, 1))
    return jnp.concatenate(parts, axis=1)


def _perm_store(val, scr, scr2, o1, o4, o16, dt):
    n = val.shape[0]
    q = n // 4
    o1[...] = val.astype(dt)
    for ci in range(val.shape[1] // 128):
        cs = slice(ci * 128, (ci + 1) * 128)
        scr[ci] = val[:, cs]
        for r4 in range(4):
            part = scr[ci, pl.ds(r4, q, stride=4), :]
            o4[r4, :, cs] = part.astype(dt)
            scr2[ci, r4 * q:(r4 + 1) * q, :] = part
        for r4 in range(4):
            for b in range(4):
                o16[r4 + 4 * b, :, cs] = scr2[ci, pl.ds(r4 * q + b, q // 4, stride=4), :].astype(dt)


def _unperm_load(r4, r16, scr_a, scr_b, scr_c):
    n = scr_a.shape[1]
    q = n // 4
    nc = r4.shape[-1] // 128
    for ci in range(nc):
        cs = slice(ci * 128, (ci + 1) * 128)
        for rr in range(4):
            scr_a[ci, pl.ds(rr, q, stride=4), :] = r4[rr, :, cs].astype(F32)
        for rr in range(4):
            for b in range(4):
                scr_c[ci, pl.ds(rr * q + b, q // 4, stride=4), :] = r16[rr + 4 * b, :, cs].astype(F32)
        for rr in range(4):
            scr_b[ci, pl.ds(rr, q, stride=4), :] = scr_c[ci, rr * q:(rr + 1) * q, :]
    return (jnp.concatenate([scr_a[ci] for ci in range(nc)], axis=1),
            jnp.concatenate([scr_b[ci] for ci in range(nc)], axis=1))


def _unperm_sum(r4, r16, scr_b, scr_c):
    n = scr_b.shape[1]
    q = n // 4
    nc = r4.shape[-1] // 128
    for ci in range(nc):
        cs = slice(ci * 128, (ci + 1) * 128)
        for rr in range(4):
            for b in range(4):
                scr_c[ci, pl.ds(rr * q + b, q // 4, stride=4), :] = r16[rr + 4 * b, :, cs].astype(F32)
        for rr in range(4):
            scr_b[ci, pl.ds(rr, q, stride=4), :] = scr_c[ci, rr * q:(rr + 1) * q, :] + r4[rr, :, cs].astype(F32)
    return jnp.concatenate([scr_b[ci] for ci in range(nc)], axis=1)


def _fwd_in(x, pos, mixw, w_in, w_out, jm_arr):
    TT = 512
    NT = T // TT

    def body(jm_ref, x_ref, pos_ref, mw_ref, win_ref, wout_ref,
             hnt_ref, q1, k1, v1, q4, k4, v4, q16, k16, v16, ag, hq, hf, hi, hg, wfull_o, woutfull_o,
             wbuf, wobuf, hn_all, scr, scr2, stage, send_sems, recv_sems, loc_sems):
        s = pl.program_id(0)
        i = pl.program_id(1)
        mx, my, c = lax.axis_index("x"), lax.axis_index("y"), lax.axis_index("c")
        me, sibling = (mx, my, c), (mx, my, 1 - c)
        chips = [(mx, 1 - my), (1 - mx, my), (1 - mx, 1 - my)]
        jm = 2 * mx + my
        rows_in = [pl.ds(pl.multiple_of(h * 512, 512), 512) for h in (c, 1 - c)]
        rows_out = [pl.ds(pl.multiple_of(h * 128, 128), 128) for h in (c, 1 - c)]

        def blk(k):
            return lax.bitwise_xor(jm, k + 1)

        def rc(n, ref, to):
            return pltpu.make_async_remote_copy(src_ref=ref, dst_ref=ref, send_sem=send_sems.at[n],
                                                recv_sem=recv_sems.at[n], device_id=to, device_id_type=MESH)

        halves = [pl.ds(0, 512), pl.ds(512, 512)]
        send_in = lambda k, h: rc(12 + 2 * k + h, wbuf.at[jm, rows_in[0], halves[h]], (*chips[k], c))
        got_in = lambda k, h: rc(12 + 2 * k + h, wbuf.at[blk(k), rows_in[0], halves[h]], me)
        relay = lambda h: rc(16 + h, wbuf.at[blk(h), rows_in[0], halves[h]], (*chips[1 - h], c))
        got_relay = lambda h: rc(16 + h, wbuf.at[blk(2), rows_in[0], halves[h]], me)
        send_out = lambda k: rc(3 + k, wobuf.at[jm, rows_out[0], :], (*chips[k], c))
        got_out = lambda k: rc(3 + k, wobuf.at[blk(k), rows_out[0], :], me)
        pass_in = lambda k: rc(6 + k, wbuf.at[blk(k), rows_in[0], :], sibling)
        pass_out = lambda k: rc(9 + k, wobuf.at[blk(k), rows_out[0], :], sibling)
        passed_in = lambda k: rc(6 + k, wbuf.at[blk(k), rows_in[1], :], me)
        passed_out = lambda k: rc(9 + k, wobuf.at[blk(k), rows_out[1], :], me)

        def keep(j, n):
            return pltpu.make_async_copy(wbuf.at[j], wfull_o.at[:, pl.ds(j * 1024, 1024)], loc_sems.at[n])

        @pl.when((s == 0) & (i == 0))
        def _():
            for p in range(5):
                src = win_ref.at[pl.ds(p * 256, 256), :] if p < 4 else wout_ref
                load = pltpu.make_async_copy(src, stage, loc_sems.at[4])
                load.start()
                load.wait()
                if p < 4:
                    wbuf[jm, p * 256:(p + 1) * 256, :] = stage[...].astype(BF16)
                else:
                    wobuf[jm] = stage[...].astype(BF16)
            for k in range(2):
                for h in range(2):
                    send_in(k, h).start()
            keep(jm, 0).start()

        def arrive(k):
            if k == 0:
                for kk in range(2):
                    for h in range(2):
                        got_in(kk, h).wait_recv()
                relay(0).start()
                relay(1).start()
            if k == 2:
                got_relay(0).wait_recv()
                got_relay(1).wait_recv()
            pass_in(k).start()
            passed_in(k).wait_recv()
            keep(blk(k), k + 1).start()
            if k == 2:
                for kk in range(3):
                    send_out(kk).start()

        pl.when((s == 1) & (i == 0))(functools.partial(arrive, 0))

        @pl.when((s == 2) & (i == 0))
        def _():
            arrive(1)
            arrive(2)

        tile = pl.ds(pl.multiple_of(i * TT, TT), TT)

        @pl.when(s == 0)
        def _():
            xv = x_ref[...]
            r = lax.rsqrt(jnp.mean(xv * xv, axis=-1, keepdims=True) + EPS)
            hnf = (xv * r) * mw_ref[...]
            hn_all[tile, :] = hnf.astype(BF16)
            hnt_ref[...] = hnf.T.astype(BF16)

        def project(jj):
            hn = hn_all[tile, :]
            lo = _mm(hn, wbuf[jj, :, 0:512])
            hi_cols = _mm(hn, wbuf[jj, :, 512:1024])
            if jj == 0:
                cosf, s1, s2 = _rope_tables(pos_ref[...])
                _perm_store(_rope(lo, cosf, s1, s2) * SCALE, scr, scr2, q1, q4, q16, BF16)
                _perm_store(_rope(hi_cols, cosf, s1, s2), scr, scr2, k1, k4, k16, BF16)
            elif jj == 1:
                _perm_store(lo, scr, scr2, v1, v4, v16, BF16)
                ag[...] = hi_cols.astype(BF16)
            elif jj == 2:
                hq[...] = lo.astype(BF16)
                hf[...] = hi_cols.astype(BF16)
            else:
                hi[...] = lo.astype(BF16)
                hg[...] = hi_cols.astype(BF16)

        def project_block(j):
            for jj in range(4):
                pl.when(j == jj)(functools.partial(project, jj))

        @pl.when(s < 2)
        def _():
            project_block(lax.bitwise_xor(jm, s))

        @pl.when(s == 2)
        def _():
            project_block(lax.bitwise_xor(jm, 2))
            project_block(lax.bitwise_xor(jm, 3))

        @pl.when((s == 2) & (i == NT - 1))
        def _():
            for k in range(3):
                got_out(k).wait_recv()
                pass_out(k).start()
            for k in range(3):
                passed_out(k).wait_recv()
            out = pltpu.make_async_copy(wobuf, woutfull_o, loc_sems.at[4])
            out.start()
            for h in range(2):
                relay(h).wait_send()
                for k in range(2):
                    send_in(k, h).wait_send()
            for k in range(3):
                send_out(k).wait_send()
                pass_in(k).wait_send()
                pass_out(k).wait_send()
            keep(jm, 0).wait()
            for k in range(3):
                keep(blk(k), k + 1).wait()
            out.wait()

    def at_stage_of(jb):
        def index(s, i, jm_ref):
            sa = jnp.minimum(lax.bitwise_xor(jm_ref[0], jb), 2)
            return jnp.where(s < sa, 0, jnp.where(s == sa, i, NT - 1))
        return index

    tok = lambda w, jb: pl.BlockSpec((TT, w), lambda s, i, jm_ref: (at_stage_of(jb)(s, i, jm_ref), 0))
    d4 = lambda jb: pl.BlockSpec((4, TT // 4, AW), lambda s, i, jm_ref: (0, at_stage_of(jb)(s, i, jm_ref), 0))
    d16 = lambda jb: pl.BlockSpec((16, TT // 16, AW), lambda s, i, jm_ref: (0, at_stage_of(jb)(s, i, jm_ref), 0))
    hbm = pl.BlockSpec(memory_space=pltpu.HBM)
    sd = lambda shape, dt: jax.ShapeDtypeStruct(shape, dt)
    in_own_stage = lambda s, i: jnp.where(s == 0, i, NT - 1)
    grid_spec = pltpu.PrefetchScalarGridSpec(
        num_scalar_prefetch=1, grid=(3, NT),
        in_specs=[pl.BlockSpec((TT, D), lambda s, i, jm_ref: (in_own_stage(s, i), 0)),
                  pl.BlockSpec((TT, 1), lambda s, i, jm_ref: (i, 0)),
                  pl.BlockSpec((1, D), lambda s, i, jm_ref: (0, 0)), hbm, hbm],
        out_specs=[pl.BlockSpec((D, TT), lambda s, i, jm_ref: (0, in_own_stage(s, i))),
                   tok(AW, 0), tok(AW, 0), tok(AW, 1), d4(0), d4(0), d4(1), d16(0), d16(0), d16(1),
                   tok(AW, 1), tok(AW, 2), tok(AW, 2), tok(AW, 3), tok(AW, 3), hbm, hbm],
        scratch_shapes=[pltpu.VMEM((4, D, 1024), BF16), pltpu.VMEM((4, 256, D), BF16), pltpu.VMEM((T, D), BF16),
                        pltpu.VMEM((4, TT, 128), F32), pltpu.VMEM((4, TT, 128), F32), pltpu.VMEM((256, 1024), F32),
                        pltpu.SemaphoreType.DMA((18,)),
                        pltpu.SemaphoreType.DMA((18,)), pltpu.SemaphoreType.DMA((6,))])
    return pl.pallas_call(
        body, name="fwd_in", grid_spec=grid_spec,
        out_shape=[sd((D, T), BF16)] + [sd((T, AW), BF16)] * 3 + [sd((4, T // 4, AW), BF16)] * 3
        + [sd((16, T // 16, AW), BF16)] * 3
        + [sd((T, AW), BF16)] * 5 + [sd((D, NCOL), BF16), sd((4, 256, D), BF16)],
        compiler_params=_cp(("arbitrary", "arbitrary")),
    )(jm_arr, x, pos, mixw, w_in, w_out)


def _band_mask(key_axis, nkeys=2 * BLK):
    shape = (nkeys, 2 * BLK) if key_axis == 0 else (2 * BLK, nkeys)
    kj = lax.broadcasted_iota(jnp.int32, shape, key_axis)
    qi = lax.broadcasted_iota(jnp.int32, shape, 1 - key_axis) & (BLK - 1)
    return (kj >= qi) & (kj <= qi + BLK), kj, qi


def _stack_heads(t2, in_a):
    z = jnp.zeros_like(t2)
    return jnp.concatenate([jnp.where(in_a[0], t2, z), jnp.where(in_a[1], t2, z)], axis=0)


def _attn_fwd(patterns):
    n = 8
    CH = n * BLK
    NS = T // CH
    NPAT = len(patterns)
    halos = [nb > n for _, _, _, nb in patterns]

    def one(i, nb, q_ref, k_ref, v_ref, kp_ref, vp_ref, o_ref, lse_ref):
        halo = kp_ref is not None
        lane = lax.broadcasted_iota(jnp.int32, (1, 128), 1)
        in_a = [lane < HEAD, lane >= HEAD]
        band, kj, _ = _band_mask(1)
        thr0 = jnp.where((n * i) % nb == 0, BLK, 0) if halo else BLK
        mask0 = band & (kj >= thr0)
        mask_first = band & (kj >= BLK)
        for b in range(n):
            rs = slice(b * BLK, (b + 1) * BLK)
            stat = jnp.zeros((BLK, 128), F32)
            for hp in range(4):
                cs = slice(hp * 128, (hp + 1) * 128)
                q2s = _stack_heads(q_ref[rs, cs], in_a)
                if b == 0:
                    kprev = kp_ref[:, cs] if halo else k_ref[rs, cs]
                    vprev = vp_ref[:, cs] if halo else v_ref[rs, cs]
                    kk = jnp.concatenate([kprev, k_ref[rs, cs]], axis=0)
                    vv = jnp.concatenate([vprev, v_ref[rs, cs]], axis=0)
                    mask = mask0
                else:
                    kk = k_ref[(b - 1) * BLK:(b + 1) * BLK, cs]
                    vv = v_ref[(b - 1) * BLK:(b + 1) * BLK, cs]
                    mask = mask_first if b % nb == 0 else band
                s = jnp.where(mask, _mm_nt(q2s, kk), NEG)
                m = jnp.max(s, axis=-1, keepdims=True)
                p = jnp.exp(s - m)
                l = jnp.sum(p, axis=-1, keepdims=True)
                o = _mm(p.astype(BF16), vv) / l
                lse = m + jnp.log(l)
                o_ref[rs, cs] = jnp.where(in_a[0], o[:BLK], o[BLK:]).astype(BF16)
                stat = jnp.where(lane == 2 * hp, lse[:BLK], stat)
                stat = jnp.where(lane == 2 * hp + 1, lse[BLK:], stat)
            lse_ref[rs, :] = stat

    def body(*refs):
        ins, outs = list(refs[:-2 * NPAT]), refs[-2 * NPAT:]
        ph, i = pl.program_id(0), pl.program_id(1)
        for pp, (_, _, _, nb) in enumerate(patterns):
            q_ref, k_ref, v_ref = ins[:3]
            ins = ins[3:]
            kp_ref = vp_ref = None
            if halos[pp]:
                kp_ref, vp_ref = ins[:2]
                ins = ins[2:]
            pl.when(ph == pp)(functools.partial(one, i, nb, q_ref, k_ref, v_ref, kp_ref, vp_ref,
                                                outs[2 * pp], outs[2 * pp + 1]))

    def parked(pp, at):
        return lambda ph, i: (jnp.where(ph < pp, at(0), jnp.where(ph == pp, at(i), at(NS - 1))), 0)

    in_specs, args, out_specs, out_shape = [], [], [], []
    for pp, (q, k, v, nb) in enumerate(patterns):
        cur = pl.BlockSpec((CH, AW), parked(pp, lambda i: i))
        in_specs += [cur, cur, cur]
        args += [q, k, v]
        if halos[pp]:
            prev = pl.BlockSpec((BLK, AW), parked(pp, lambda i: jnp.maximum(n * i - 1, 0)))
            in_specs += [prev, prev]
            args += [k, v]
        out_specs += [cur, pl.BlockSpec((CH, 128), parked(pp, lambda i: i))]
        out_shape += [jax.ShapeDtypeStruct((T, AW), BF16), jax.ShapeDtypeStruct((T, 128), F32)]
    return pl.pallas_call(
        body, name="attn_fwd", grid=(NPAT, NS),
        in_specs=in_specs, out_specs=out_specs, out_shape=out_shape,
        compiler_params=_cp(("arbitrary", "arbitrary")),
    )(*args)


def _attn_bwd(patterns):
    n = 8
    CH = n * BLK
    NS = T // CH
    NBLK = T // BLK
    NPAT = len(patterns)
    halos = [p[5] > n for p in patterns]

    def one(i, nb, q_ref, k_ref, v_ref, do_ref, st_ref, halo_refs, dq_ref, dk_ref, dv_ref):
        halo = halo_refs is not None
        if halo:
            kp_ref, vp_ref, qn_ref, don_ref, stn_ref = halo_refs
        lane = lax.broadcasted_iota(jnp.int32, (1, 128), 1)
        in_a = [lane < HEAD, lane >= HEAD]
        band, kj, _ = _band_mask(0)
        thr0 = jnp.where((n * i) % nb == 0, BLK, 0) if halo else BLK
        mask0 = band & (kj >= thr0)
        mask_first = band & (kj >= BLK)

        def stat_rows(st_t, hp):
            lse_r = jnp.concatenate([st_t[2 * hp:2 * hp + 1, :], st_t[2 * hp + 1:2 * hp + 2, :]], axis=1)
            dl_r = jnp.concatenate([st_t[8 + 2 * hp:9 + 2 * hp, :], st_t[9 + 2 * hp:10 + 2 * hp, :]], axis=1)
            return lse_r, dl_r

        st_t = [st_ref[b * BLK:(b + 1) * BLK, :].T for b in range(n)]
        if halo:
            nxt_thr = jnp.where((n * i + n) % nb == 0, 2 * BLK, 0)
            _, kj1, qi1 = _band_mask(0, BLK)
            mask_next = kj1 >= qi1 + nxt_thr
            stn_t = stn_ref[...].T

        for hp in range(4):
            cs = slice(hp * 128, (hp + 1) * 128)
            kb = [k_ref[b * BLK:(b + 1) * BLK, cs] for b in range(n)]
            vb = [v_ref[b * BLK:(b + 1) * BLK, cs] for b in range(n)]
            dk_acc = [jnp.zeros((BLK, 128), F32) for _ in range(n)]
            dv_acc = [jnp.zeros((BLK, 128), F32) for _ in range(n)]
            for b in range(n):
                rs = slice(b * BLK, (b + 1) * BLK)
                q2s = _stack_heads(q_ref[rs, cs], in_a)
                do2s = _stack_heads(do_ref[rs, cs], in_a)
                if b == 0:
                    kprev = kp_ref[:, cs] if halo else kb[0]
                    vprev = vp_ref[:, cs] if halo else vb[0]
                    mask = mask0
                else:
                    kprev, vprev, mask = kb[b - 1], vb[b - 1], (mask_first if b % nb == 0 else band)
                kk = jnp.concatenate([kprev, kb[b]], axis=0)
                vv = jnp.concatenate([vprev, vb[b]], axis=0)
                lse_r, dl_r = stat_rows(st_t[b], hp)
                s_t = jnp.where(mask, _mm_nt(kk, q2s), NEG)
                p_t = jnp.exp(s_t - lse_r)
                ds_t = (p_t * (_mm_nt(vv, do2s) - dl_r)).astype(BF16)
                dkk = _mm(ds_t, q2s)
                dvv = _mm(p_t.astype(BF16), do2s)
                dqs = _mm_tn(ds_t, kk) * SCALE
                dq_ref[rs, cs] = jnp.where(in_a[0], dqs[:BLK], dqs[BLK:]).astype(BF16)
                dk_acc[b] += dkk[BLK:]
                dv_acc[b] += dvv[BLK:]
                if b > 0:
                    dk_acc[b - 1] += dkk[:BLK]
                    dv_acc[b - 1] += dvv[:BLK]
            if halo:
                q2s = _stack_heads(qn_ref[:, cs], in_a)
                do2s = _stack_heads(don_ref[:, cs], in_a)
                lse_r, dl_r = stat_rows(stn_t, hp)
                s_t = jnp.where(mask_next, _mm_nt(kb[n - 1], q2s), NEG)
                p_t = jnp.exp(s_t - lse_r)
                ds_t = (p_t * (_mm_nt(vb[n - 1], do2s) - dl_r)).astype(BF16)
                dk_acc[n - 1] += _mm(ds_t, q2s)
                dv_acc[n - 1] += _mm(p_t.astype(BF16), do2s)
            for b in range(n):
                dk_ref[b * BLK:(b + 1) * BLK, cs] = dk_acc[b].astype(BF16)
                dv_ref[b * BLK:(b + 1) * BLK, cs] = dv_acc[b].astype(BF16)

    def body(*refs):
        ins, outs = list(refs[:-3 * NPAT]), refs[-3 * NPAT:]
        ph, i = pl.program_id(0), pl.program_id(1)
        for pp, pat in enumerate(patterns):
            main, ins = ins[:5], ins[5:]
            halo_refs = None
            if halos[pp]:
                halo_refs, ins = ins[:5], ins[5:]
            pl.when(ph == pp)(functools.partial(one, i, pat[5], *main, halo_refs, *outs[3 * pp:3 * pp + 3]))

    def parked(pp, at):
        return lambda ph, i: (jnp.where(ph < pp, at(0), jnp.where(ph == pp, at(i), at(NS - 1))), 0)

    in_specs, args, out_specs = [], [], []
    for pp, (q, k, v, do, st, nb) in enumerate(patterns):
        cur = pl.BlockSpec((CH, AW), parked(pp, lambda i: i))
        in_specs += [cur] * 4 + [pl.BlockSpec((CH, 128), parked(pp, lambda i: i))]
        args += [q, k, v, do, st]
        if halos[pp]:
            before = lambda i: jnp.maximum(n * i - 1, 0)
            after = lambda i: jnp.minimum(n * i + n, NBLK - 1)
            in_specs += [pl.BlockSpec((BLK, AW), parked(pp, before))] * 2
            in_specs += [pl.BlockSpec((BLK, AW), parked(pp, after))] * 2 + [pl.BlockSpec((BLK, 128), parked(pp, after))]
            args += [k, v, q, do, st]
        out_specs += [cur] * 3
    return pl.pallas_call(
        body, name="attn_bwd", grid=(NPAT, NS),
        in_specs=in_specs, out_specs=out_specs,
        out_shape=[jax.ShapeDtypeStruct((T, AW), BF16)] * (3 * NPAT),
        compiler_params=_cp(("arbitrary", "arbitrary")),
    )(*args)


TH = 256
NCH = TH // CHUNK


def _hgrn_common(hq_ref, hf_ref, lbr_ref, tri_ref):
    r0 = lbr_ref[0:1, :]
    r1 = lbr_ref[1:2, :]
    mx = jnp.maximum(r0, r1)
    e0 = jnp.exp(r0 - mx)
    e1 = jnp.exp(r1 - mx)
    lb = e0 / (e0 + e1)
    hqv = hq_ref[...].astype(F32)
    sq = _sigmoid(hqv)
    qv = hqv * sq
    sf = _sigmoid(hf_ref[...].astype(F32))
    f = lb + (1.0 - lb) * sf
    kv = 1.0 - f
    g = jnp.log(f)
    cum = _mm_exact_l(tri_ref[...], g)
    dec = jnp.exp(jnp.concatenate([cum[c * CHUNK + CHUNK - 1:(c + 1) * CHUNK, :] for c in range(NCH)], axis=0))
    decb = jnp.concatenate([jnp.broadcast_to(dec[c:c + 1, :], (CHUNK, HW)) for c in range(NCH)], axis=0)
    ea = jnp.exp(cum)
    ena = jnp.exp(-cum)
    eend = decb * ena
    return dict(lb=lb, hq=hqv, sq=sq, q=qv, sf=sf, f=f, k=kv, cum=cum, ea=ea, ena=ena, eend=eend,
                qd=qv * ea, ki=kv * ena, ke=kv * eend, dec=dec)


def _tri_mask(transposed=False):
    ti = lax.broadcasted_iota(jnp.int32, (TH, TH), 1 if transposed else 0)
    si = lax.broadcasted_iota(jnp.int32, (TH, TH), 0 if transposed else 1)
    return (si <= ti) & ((si // CHUNK) == (ti // CHUNK))


def _hgrn_fwd(hq, hf, hi, lbr, tri):
    NSUB = 2

    def body(hq_ref, hf_ref, hi_ref, lbr_ref, tri_ref, rec_ref, sall_ref, st_scr):
        @pl.when(pl.program_id(0) == 0)
        def _():
            st_scr[...] = jnp.zeros_like(st_scr)

        causal = _tri_mask()
        for u in range(NSUB):
            tile = slice(u * TH, (u + 1) * TH)
            w = _hgrn_common(hq_ref.at[tile, :], hf_ref.at[tile, :], lbr_ref, tri_ref)
            qd, ki, ke = w["qd"].astype(BF16), w["ki"].astype(BF16), w["ke"].astype(BF16)
            dec = w["dec"]
            vb = hi_ref[tile, :]
            for h in range(4):
                cs = slice(h * 128, (h + 1) * 128)
                att = jnp.where(causal, _mm_nt(qd[:, cs], ki[:, cs]), 0.0)
                o_intra = _mm(att.astype(BF16), vb[:, cs])
                st = st_scr[:, cs]
                for c in range(NCH):
                    rs = slice(c * CHUNK, (c + 1) * CHUNK)
                    sall_ref[u * NCH + c, :, cs] = st
                    rec_ref[u * TH + c * CHUNK:u * TH + (c + 1) * CHUNK, cs] = (
                        o_intra[rs] + _mm_nt(qd[rs, cs], st.astype(BF16))).astype(BF16)
                    st = dec[c:c + 1, cs] * st + _mm_tn(vb[rs, cs], ke[rs, cs])
                st_scr[:, cs] = st

    tok = pl.BlockSpec((NSUB * TH, HW), lambda i: (i, 0))
    return pl.pallas_call(
        body, name="hgrn_fwd", grid=(T // (NSUB * TH),),
        in_specs=[tok, tok, tok, pl.BlockSpec((2, HW), lambda i: (0, 0)), pl.BlockSpec((TH, TH), lambda i: (0, 0))],
        out_specs=[tok, pl.BlockSpec((NSUB * NCH, 128, HW), lambda i: (i, 0, 0))],
        out_shape=[jax.ShapeDtypeStruct((T, HW), BF16), jax.ShapeDtypeStruct((T // CHUNK, 128, HW), F32)],
        scratch_shapes=[pltpu.VMEM((128, HW), F32)],
        compiler_params=_cp(("arbitrary",)),
    )(hq, hf, hi, lbr, tri)


def _hgrn_bwd(hq, hf, hi, lbr, tri, trit, drec, sall, dhg, rout, routb):
    NSUB = 2
    NT = T // (NSUB * TH)

    def body(hq_ref, hf_ref, hi_ref, lbr_ref, tri_ref, trit_ref, do_ref, sall_ref, dhg_ref, rout_r, routb_r,
             dph_ref, small_ref, pout_o, poutr_o,
             dst_scr, dlb_scr, dqd_scr, dki_scr, dke_scr, dlast_scr, send_sems, recv_sems, loc_sems):
        step = pl.program_id(0)
        loc, rem = _chip_copies(_w_out_piece, rout_r, routb_r, pout_o, poutr_o, send_sems, recv_sems,
                                loc_sems.at[0])

        @pl.when(step == 0)
        def _():
            dst_scr[...] = jnp.zeros_like(dst_scr)
            dlb_scr[...] = jnp.zeros_like(dlb_scr)
            for cp in loc + rem:
                cp.start()

        causal = _tri_mask()
        causal_t = _tri_mask(transposed=True)
        lb = None
        for u in reversed(range(NSUB)):
            tile = slice(u * TH, (u + 1) * TH)
            w = _hgrn_common(hq_ref.at[tile, :], hf_ref.at[tile, :], lbr_ref, tri_ref)
            qd, ki, ke = w["qd"].astype(BF16), w["ki"].astype(BF16), w["ke"].astype(BF16)
            dec = w["dec"]
            vb = hi_ref[tile, :]
            dob = do_ref[tile, :].astype(BF16)
            for h in range(4):
                cs = slice(h * 128, (h + 1) * 128)
                att_t = jnp.where(causal_t, _mm_nt(ki[:, cs], qd[:, cs]), 0.0).astype(BF16)
                datt_t = jnp.where(causal_t, _mm_nt(vb[:, cs], dob[:, cs]), 0.0).astype(BF16)
                datt = jnp.where(causal, _mm_nt(dob[:, cs], vb[:, cs]), 0.0).astype(BF16)
                dv_intra = _mm(att_t, dob[:, cs])
                dqd_intra = _mm(datt, ki[:, cs])
                dki_scr[u, :, cs] = _mm(datt_t, qd[:, cs])
                dst = dst_scr[:, cs]
                for c in reversed(range(NCH)):
                    rs = slice(c * CHUNK, (c + 1) * CHUNK)
                    dec_c = dec[c:c + 1, :]
                    st = sall_ref[u * NCH + c, :, cs]
                    dstb = dst.astype(BF16)
                    dph_ref[u * TH + c * CHUNK:u * TH + (c + 1) * CHUNK, 2 * HW + h * 128:2 * HW + (h + 1) * 128] = (
                        dv_intra[rs] + _mm_nt(ke[rs, cs], dstb)).astype(BF16)
                    dqd_scr[u, rs, cs] = dqd_intra[rs] + _mm(dob[rs, cs], st.astype(BF16))
                    dke_scr[u, rs, cs] = _mm(vb[rs, cs], dstb)
                    ddec = jnp.sum(dst * st, axis=0, keepdims=True)
                    dlast_scr[u, c:c + 1, cs] = ddec * dec_c[:, cs]
                    dst = dec_c[:, cs] * dst + _mm_tn(dob[rs, cs], qd[rs, cs])
                dst_scr[:, cs] = dst
            dqd, dki, dke = dqd_scr[u], dki_scr[u], dke_scr[u]
            dq = dqd * w["ea"]
            dk = dki * w["ena"] + dke * w["eend"]
            dcum = dqd * w["qd"] - dki * w["ki"] - dke * w["ke"]
            dkeke = dke * w["ke"]
            dlastb = jnp.concatenate(
                [jnp.broadcast_to(dlast_scr[u, c:c + 1, :]
                                  + jnp.sum(dkeke[c * CHUNK:(c + 1) * CHUNK], axis=0, keepdims=True), (CHUNK, HW))
                 for c in range(NCH)], axis=0)
            dg = _mm_exact_l(trit_ref[...], dcum) + dlastb
            df = dg / w["f"] - dk
            lb, sf, sq = w["lb"], w["sf"], w["sq"]
            dph_ref[tile, HW:2 * HW] = (df * (1.0 - lb) * sf * (1.0 - sf)).astype(BF16)
            dph_ref[tile, 0:HW] = (dq * (sq * (1.0 + w["hq"] * (1.0 - sq)))).astype(BF16)
            dph_ref[tile, 3 * HW:4 * HW] = dhg_ref[tile, :]
            dlb_scr[...] += jnp.sum(df * (1.0 - sf), axis=0, keepdims=True)

        @pl.when(step == NT - 1)
        def _():
            gr = dlb_scr[...] * lb * (1.0 - lb)
            small_ref[...] = jnp.zeros_like(small_ref)
            small_ref[0:1, 0:HW] = gr
            small_ref[1:2, 0:HW] = -gr
            for cp in rem:
                cp.wait_recv()
            for cp in rem:
                cp.wait_send()
            for cp in loc:
                cp.wait()

    tok = pl.BlockSpec((NSUB * TH, HW), lambda i: (NT - 1 - i, 0))
    const = lambda shape: pl.BlockSpec(shape, lambda i: (0,) * len(shape))
    hbm = pl.BlockSpec(memory_space=pltpu.HBM)
    return pl.pallas_call(
        body, name="hgrn_bwd", grid=(NT,),
        in_specs=[tok, tok, tok, const((2, HW)), const((TH, TH)), const((TH, TH)), tok,
                  pl.BlockSpec((NSUB * NCH, 128, HW), lambda i: (NT - 1 - i, 0, 0)), tok, hbm, hbm],
        out_specs=[pl.BlockSpec((NSUB * TH, NCOL // 2), lambda i: (NT - 1 - i, 0)), const((8, D)), hbm, hbm],
        out_shape=[jax.ShapeDtypeStruct((T, NCOL // 2), BF16), jax.ShapeDtypeStruct((8, D), F32),
                   jax.ShapeDtypeStruct((128, D), F32), jax.ShapeDtypeStruct((3, 128, D), BF16)],
        scratch_shapes=[pltpu.VMEM((128, HW), F32), pltpu.VMEM((1, HW), F32), pltpu.VMEM((NSUB, TH, HW), F32),
                        pltpu.VMEM((NSUB, TH, HW), F32), pltpu.VMEM((NSUB, TH, HW), F32),
                        pltpu.VMEM((NSUB, 8, HW), F32),
                        pltpu.SemaphoreType.DMA((3,)), pltpu.SemaphoreType.DMA((3,)), pltpu.SemaphoreType.DMA((1,))],
        compiler_params=_cp(("arbitrary",)),
    )(hq, hf, hi, lbr, tri, trit, drec, sall, dhg, rout, routb)


def _fwd_out(o1, o4, o16, l1, l4, l16, rec, ag, hg, x, tgt, anw, hnw, fnw, wout_full, gmat, emat, selmat):
    TT = 512

    def body(o1_r, o4_r, o16_r, l1_r, l4_r, l16_r, rec_r, ag_r, hg_r, x_r, tgt_r, anw_r, hnw_r, fnw_r, wo_r, g_r,
             e_r, sel_r, dx2_o, do1_o, do4_o, do16_o, st1_o, st4_o, st16_o, drec_o, dag_o, dhg_o,
             rout_o, routb_o, small_o, scr_a, scr_b, scr_c, gwout_o, rbuf, send_sems, recv_sems):
        @pl.when(pl.program_id(0) == 0)
        def _():
            gwout_o[...] = jnp.zeros_like(gwout_o)
            small_o[...] = jnp.zeros_like(small_o)

        def unperm(r4, r16):
            return _unperm_load(r4, r16, scr_a, scr_b, scr_c)

        def perm_out(val, p1, p4, p16, dt):
            _perm_store(val, scr_a, scr_b, p1, p4, p16, dt)

        o4u, o16u = unperm(o4_r, o16_r)
        l4c, l16c = unperm(l4_r, l16_r)
        l1c = l1_r[...]
        mxc = jnp.maximum(jnp.maximum(l1c, l4c), l16c)
        w1c, w4c, w16c = jnp.exp(l1c - mxc), jnp.exp(l4c - mxc), jnp.exp(l16c - mxc)
        denc = w1c + w4c + w16c
        lane = lax.broadcasted_iota(jnp.int32, (1, 128), 1)
        lse_c = jnp.where(lane < 8, mxc + jnp.log(denc), 0.0)
        em = e_r[...]
        wn1 = _mm_exact_r(w1c / denc, em)
        wn4 = _mm_exact_r(w4c / denc, em)
        o1v = o1_r[...].astype(F32)
        attn = wn1 * o1v + wn4 * o4u + (1.0 - wn1 - wn4) * o16u
        gm = g_r[...]

        def head_mean_a(t):
            return jnp.concatenate([_mm_exact_r(t[:, :256], gm), _mm_exact_r(t[:, 256:], gm)], axis=1)

        def head_mean_h(t):
            return jnp.concatenate(
                [jnp.broadcast_to(jnp.mean(t[:, h * 128:(h + 1) * 128], axis=-1, keepdims=True), (TT, 128))
                 for h in range(4)], axis=1)

        rs_a = lax.rsqrt(head_mean_a(attn * attn) + EPS)
        n_a = attn * rs_a
        agv = ag_r[...].astype(F32)
        sg_a = _sigmoid(agv)
        si_a = agv * sg_a
        anw_v = anw_r[...]
        y_a = (n_a * anw_v) * si_a
        recv = rec_r[...].astype(F32)
        rs_h = lax.rsqrt(head_mean_h(recv * recv) + EPS)
        n_h = recv * rs_h
        hgv = hg_r[...].astype(F32)
        sg_h = _sigmoid(hgv)
        si_h = hgv * sg_h
        hnw_v = hnw_r[...]
        y_h = (n_h * hnw_v) * si_h
        mixed = jnp.concatenate([y_a, y_h], axis=1).astype(BF16)
        xv = x_r[...]
        x2 = xv + _mm(mixed, wo_r[...])
        r2 = lax.rsqrt(jnp.mean(x2 * x2, axis=-1, keepdims=True) + EPS)
        fnw_v = fnw_r[...]
        xn = x2 * r2
        err = xn * fnw_v - tgt_r[...]
        small_o[2:3, :] += 0.5 * jnp.sum(jnp.mean(err * err, axis=-1, keepdims=True), axis=0, keepdims=True)
        dy = err * (1.0 / D)
        small_o[0:1, :] += jnp.sum(dy * xn, axis=0, keepdims=True)
        dyw = dy * fnw_v
        dx2 = r2 * dyw - x2 * ((r2 * r2 * r2) * jnp.mean(dyw * x2, axis=-1, keepdims=True))
        dx2_o[...] = dx2
        dx2b = dx2.astype(BF16)
        gwout_o[...] += _mm_tn(mixed, dx2b)
        dmix = _mm_nt(dx2b, wo_r[...])
        dm_a, dm_h = dmix[:, :AW], dmix[:, AW:]
        dag_o[...] = (dm_a * (n_a * anw_v) * (sg_a * (1.0 + agv * (1.0 - sg_a)))).astype(BF16)
        dn_a = dm_a * anw_v * si_a
        small_o[1:2, 0:AW] += jnp.sum(dm_a * n_a * si_a, axis=0, keepdims=True)
        dattn = rs_a * (dn_a - n_a * head_mean_a(dn_a * n_a))
        perm_out(dattn, do1_o, do4_o, do16_o, BF16)
        stats = lse_c + _mm_exact_r(dattn * attn, sel_r[...])
        perm_out(stats, st1_o, st4_o, st16_o, F32)
        dhg_o[...] = (dm_h * (n_h * hnw_v) * (sg_h * (1.0 + hgv * (1.0 - sg_h)))).astype(BF16)
        dn_h = dm_h * hnw_v * si_h
        small_o[1:2, AW:] += jnp.sum(dm_h * n_h * si_h, axis=0, keepdims=True)
        drec_o[...] = (rs_h * (dn_h - n_h * head_mean_h(dn_h * n_h))).astype(BF16)

        @pl.when(pl.program_id(0) == T // TT - 1)
        def _():
            x, y, c = lax.axis_index("x"), lax.axis_index("y"), lax.axis_index("c")
            cps = [pltpu.make_async_remote_copy(
                src_ref=gwout_o.at[pl.ds(pl.multiple_of(j * 256 + (1 - c) * 128, 128), 128), :], dst_ref=rbuf.at[j],
                send_sem=send_sems.at[j], recv_sem=recv_sems.at[j], device_id=(x, y, 1 - c), device_id_type=MESH)
                for j in range(4)]
            for cp in cps:
                cp.start()
            for j, cp in enumerate(cps):
                cp.wait_recv()
                red = gwout_o[pl.ds(pl.multiple_of(j * 256 + c * 128, 128), 128), :] + rbuf[j]
                rout_o[j * 128:(j + 1) * 128, :] = red
                routb_o[j * 128:(j + 1) * 128, :] = red.astype(BF16)
            for cp in cps:
                cp.wait_send()

    tok = lambda w: pl.BlockSpec((TT, w), lambda i: (i, 0))
    d4 = pl.BlockSpec((4, TT // 4, AW), lambda i: (0, i, 0))
    d16 = pl.BlockSpec((16, TT // 16, AW), lambda i: (0, i, 0))
    const = lambda shape: pl.BlockSpec(shape, lambda i: (0,) * len(shape))
    sd = lambda shape, dt: jax.ShapeDtypeStruct(shape, dt)
    c4 = pl.BlockSpec((4, TT // 4, 128), lambda i: (0, i, 0))
    c16 = pl.BlockSpec((16, TT // 16, 128), lambda i: (0, i, 0))
    p3 = lambda w, dt: [sd((T, w), dt), sd((4, T // 4, w), dt), sd((16, T // 16, w), dt)]
    return pl.pallas_call(
        body, name="fwd_out", grid=(T // TT,),
        in_specs=[tok(AW), d4, d16, tok(128), c4, c16, tok(AW), tok(AW), tok(AW), tok(D), tok(D),
                  const((1, AW)), const((1, HW)), const((1, D)), const((D, D)), const((256, 256)),
                  const((128, AW)), const((AW, 128))],
        out_specs=[tok(D)] + [tok(AW), d4, d16] + [tok(128), c4, c16] + [tok(AW)] * 3
        + [const((512, D)), const((512, D)), const((8, D))],
        out_shape=[sd((T, D), F32)] + p3(AW, BF16) + p3(128, F32)
        + [sd((T, AW), BF16), sd((T, AW), BF16), sd((T, AW), BF16), sd((512, D), F32), sd((512, D), BF16),
           sd((8, D), F32)],
        scratch_shapes=[pltpu.VMEM((4, TT, 128), F32)] * 3 + [pltpu.VMEM((D, D), F32),
                        pltpu.VMEM((4, 128, D), F32), pltpu.SemaphoreType.DMA((4,)), pltpu.SemaphoreType.DMA((4,))],
        compiler_params=_cp(("arbitrary",)),
    )(o1, o4, o16, l1, l4, l16, rec, ag, hg, x, tgt, anw, hnw, fnw, wout_full, gmat, emat, selmat)


def _dproj_build(dq, dk, dv, dag, pos):
    TT = 512

    def body(dq1, dq4, dq16, dk1, dk4, dk16, dv1, dv4, dv16, dag_r, pos_r, dproj_o, scr_b, scr_c):
        def unperm_sum(r1, r4, r16):
            return r1[...] + _unperm_sum(r4, r16, scr_b, scr_c)

        cosf, s1, s2 = _rope_tables(pos_r[...])
        dproj_o[:, 0:512] = _rope_bwd(unperm_sum(dq1, dq4, dq16), cosf, s1, s2).astype(BF16)
        dproj_o[:, 512:1024] = _rope_bwd(unperm_sum(dk1, dk4, dk16), cosf, s1, s2).astype(BF16)
        dproj_o[:, 1024:1536] = unperm_sum(dv1, dv4, dv16).astype(BF16)
        dproj_o[:, 1536:2048] = dag_r[...]

    tok = lambda w: pl.BlockSpec((TT, w), lambda i: (i, 0))
    d4 = pl.BlockSpec((4, TT // 4, AW), lambda i: (0, i, 0))
    d16 = pl.BlockSpec((16, TT // 16, AW), lambda i: (0, i, 0))
    return pl.pallas_call(
        body, name="dproj_build", grid=(T // TT,),
        in_specs=[tok(AW), d4, d16] * 3 + [tok(AW), tok(1)],
        out_specs=tok(NCOL // 2),
        out_shape=jax.ShapeDtypeStruct((T, NCOL // 2), BF16),
        scratch_shapes=[pltpu.VMEM((4, TT, 128), F32)] * 2,
        compiler_params=_cp(("parallel",)),
    )(*dq, *dk, *dv, dag, pos)


def _bwd_x(dproj_a, dproj_h, x, dx2, mixw, w_full, rin, rinb, small4, small6, pout_own, pout_rem):
    TT = 256
    NT = T // TT

    def body(dpa_r, dph_r, x_r, dx2_r, mw_r, w_r, rin_r, rinb_r, s4_r, s6_r, poo_r, por_r,
             gx_o, pin_o, pinr_o, sall_o, fin_o, fout_o, sbuf, v_own, v_rem, vo_own, vo_rem, sin, sout, got_in,
             got_out, send_sems, recv_sems, loc_sems, share_send, share_recv, fin_sems):
        i = pl.program_id(0)
        loc, rem = _chip_copies(_w_in_piece, rin_r, rinb_r, pin_o, pinr_o, send_sems, recv_sems, loc_sems.at[0])

        @pl.when(i == 0)
        def _():
            sbuf[...] = jnp.zeros_like(sbuf)
            for cp in loc + rem:
                cp.start()

        dhn = _mm_nt(dpa_r[...], w_r[:, 0:NCOL // 2]) + _mm_nt(dph_r[...], w_r[:, NCOL // 2:NCOL])
        xv = x_r[...]
        r = lax.rsqrt(jnp.mean(xv * xv, axis=-1, keepdims=True) + EPS)
        dxw = dhn * mw_r[...]
        gx_o[...] = dx2_r[...] + r * dxw - xv * ((r * r * r) * jnp.mean(dxw * xv, axis=-1, keepdims=True))
        sbuf[16:17, :] += jnp.sum(dhn * (xv * r), axis=0, keepdims=True)

        @pl.when(i == NT - 1)
        def _():
            sbuf[0:8, :] = s4_r[...]
            sbuf[8:16, :] = s6_r[...]
            sloc, srem = _small_copies(sbuf, sall_o, send_sems, recv_sems, loc_sems.at[1])
            for cp in sloc + srem:
                cp.start()
            for cp in rem:
                cp.wait_recv()
            for cp in rem:
                cp.wait_send()
            for cp in loc:
                cp.wait()
            mx, my, c = lax.axis_index("x"), lax.axis_index("y"), lax.axis_index("c")
            loads = [pltpu.make_async_copy(pin_o, v_own, fin_sems.at[0]),
                     pltpu.make_async_copy(pinr_o, v_rem, fin_sems.at[1]),
                     pltpu.make_async_copy(poo_r, vo_own, fin_sems.at[2]),
                     pltpu.make_async_copy(por_r, vo_rem, fin_sems.at[3])]
            for cp in loads:
                cp.start()
            for cp in loads:
                cp.wait()
            sout[...] = ((vo_own[...] + vo_rem[0].astype(F32)) + vo_rem[1].astype(F32)) + vo_rem[2].astype(F32)
            sin[...] = ((v_own[...] + v_rem[0].astype(F32)) + v_rem[1].astype(F32)) + v_rem[2].astype(F32)
            swap = [pltpu.make_async_remote_copy(src_ref=sin, dst_ref=got_in, send_sem=share_send.at[0],
                                                 recv_sem=share_recv.at[0], device_id=(mx, my, 1 - c),
                                                 device_id_type=MESH),
                    pltpu.make_async_remote_copy(src_ref=sout, dst_ref=got_out, send_sem=share_send.at[1],
                                                 recv_sem=share_recv.at[1], device_id=(mx, my, 1 - c),
                                                 device_id_type=MESH)]
            for cp in swap:
                cp.start()
            mine = [pltpu.make_async_copy(sin, fin_o.at[c], fin_sems.at[0]),
                    pltpu.make_async_copy(sout, fout_o.at[c], fin_sems.at[1])]
            for cp in mine:
                cp.start()
            for cp in swap:
                cp.wait_recv()
            theirs = [pltpu.make_async_copy(got_in, fin_o.at[1 - c], fin_sems.at[2]),
                      pltpu.make_async_copy(got_out, fout_o.at[1 - c], fin_sems.at[3])]
            for cp in theirs:
                cp.start()
            for cp in swap:
                cp.wait_send()
            for cp in mine + theirs:
                cp.wait()
            for cp in srem:
                cp.wait_recv()
            for cp in srem:
                cp.wait_send()
            for cp in sloc:
                cp.wait()

    tok = lambda w: pl.BlockSpec((TT, w), lambda i: (i, 0))
    const = lambda shape: pl.BlockSpec(shape, lambda i: (0,) * len(shape))
    hbm = pl.BlockSpec(memory_space=pltpu.HBM)
    return pl.pallas_call(
        body, name="bwd_x", grid=(NT,),
        in_specs=[tok(NCOL // 2), tok(NCOL // 2), tok(D), tok(D), const((1, D)), const((D, NCOL)), hbm, hbm,
                  const((8, D)), const((8, D)), hbm, hbm],
        out_specs=[tok(D), hbm, hbm, hbm, hbm, hbm],
        out_shape=[jax.ShapeDtypeStruct((T, D), F32),
                   jax.ShapeDtypeStruct((512, 1024), F32), jax.ShapeDtypeStruct((3, 512, 1024), BF16),
                   jax.ShapeDtypeStruct((8, 24, D), F32),
                   jax.ShapeDtypeStruct((2, 512, 1024), F32), jax.ShapeDtypeStruct((2, 128, D), F32)],
        scratch_shapes=[pltpu.VMEM((24, D), F32),
                        pltpu.VMEM((512, 1024), F32), pltpu.VMEM((3, 512, 1024), BF16),
                        pltpu.VMEM((128, D), F32), pltpu.VMEM((3, 128, D), BF16),
                        pltpu.VMEM((512, 1024), F32), pltpu.VMEM((128, D), F32),
                        pltpu.VMEM((512, 1024), F32), pltpu.VMEM((128, D), F32),
                        pltpu.SemaphoreType.DMA((10,)), pltpu.SemaphoreType.DMA((10,)), pltpu.SemaphoreType.DMA((2,)),
                        pltpu.SemaphoreType.DMA((2,)), pltpu.SemaphoreType.DMA((2,)), pltpu.SemaphoreType.DMA((4,))],
        compiler_params=_cp(("arbitrary",)),
    )(dproj_a, dproj_h, x, dx2, mixw, w_full, rin, rinb, small4, small6, pout_own, pout_rem)


def _grad_w_in(hn, dproj_a, dproj_h):
    TK = 2048
    NK = T // TK

    def body(hnt_r, dpa_r, dph_r, rin_o, rinb_o, acc, rbuf, obuf, obufb, send_sems, recv_sems, wb_sems):
        j = pl.program_id(0)
        kk = pl.program_id(1)
        x, y, c = lax.axis_index("x"), lax.axis_index("y"), lax.axis_index("c")
        mine = pl.ds(pl.multiple_of(c * 512, 512), 512)
        theirs = pl.ds(pl.multiple_of((1 - c) * 512, 512), 512)

        def send(jj):
            return pltpu.make_async_remote_copy(
                src_ref=acc.at[jj % 2, theirs, :], dst_ref=rbuf.at[jj], send_sem=send_sems.at[jj],
                recv_sem=recv_sems.at[jj], device_id=(x, y, 1 - c), device_id_type=MESH)

        def writeback(jj):
            cols = pl.ds(jj * 1024, 1024)
            return [pltpu.make_async_copy(obuf.at[jj % 2], rin_o.at[:, cols], wb_sems.at[jj % 2]),
                    pltpu.make_async_copy(obufb.at[jj % 2], rinb_o.at[:, cols], wb_sems.at[2 + jj % 2])]

        def wait_writeback(jj):
            for cp in writeback(jj):
                cp.wait()

        def finalize(jj):
            send(jj).wait_recv()
            red = acc[jj % 2, mine, :] + rbuf[jj]
            obuf[jj % 2] = red
            obufb[jj % 2] = red.astype(BF16)
            for cp in writeback(jj):
                cp.start()

        prod = _mm(hnt_r[...], jnp.where(j < 2, dpa_r[...], dph_r[...]))

        @pl.when(kk == 0)
        def _():
            for jj in (2, 3):
                @pl.when(j == jj)
                def _():
                    send(jj - 2).wait_send()
            acc[j % 2] = prod

        @pl.when(kk > 0)
        def _():
            acc[j % 2] += prod

        @pl.when(kk == NK - 1)
        def _():
            for jj in range(4):
                @pl.when(j == jj)
                def _():
                    send(jj).start()
                    if jj in (1, 2):
                        finalize(jj - 1)
                    if jj == 3:
                        wait_writeback(0)
                        finalize(2)
                        wait_writeback(1)
                        finalize(3)
                        wait_writeback(2)
                        wait_writeback(3)
                        send(2).wait_send()
                        send(3).wait_send()

    hbm = pl.BlockSpec(memory_space=pltpu.HBM)
    return pl.pallas_call(
        body, name="grad_w_in", grid=(4, NK),
        in_specs=[pl.BlockSpec((D, TK), lambda j, kk: (0, kk)),
                  pl.BlockSpec((TK, 1024), lambda j, kk: (jnp.where(j < 2, kk, NK - 1), jnp.minimum(j, 1))),
                  pl.BlockSpec((TK, 1024), lambda j, kk: (jnp.where(j < 2, 0, kk), jnp.maximum(j - 2, 0)))],
        out_specs=[hbm, hbm],
        out_shape=[jax.ShapeDtypeStruct((512, NCOL), F32), jax.ShapeDtypeStruct((512, NCOL), BF16)],
        scratch_shapes=[pltpu.VMEM((2, D, 1024), F32), pltpu.VMEM((4, 512, 1024), F32), pltpu.VMEM((2, 512, 1024), F32),
                        pltpu.VMEM((2, 512, 1024), BF16),
                        pltpu.SemaphoreType.DMA((4,)), pltpu.SemaphoreType.DMA((4,)), pltpu.SemaphoreType.DMA((4,))],
        compiler_params=_cp(("arbitrary", "arbitrary")),
    )(hn, dproj_a, dproj_h)


def _w_in_piece(ref, j):
    return ref.at[:, pl.ds(j * 1024, 1024)]


def _w_out_piece(ref, j):
    return ref.at[pl.ds(j * 128, 128), :]


def _chip_copies(piece, src_r, srcb_r, own_o, rem_o, send_sems, recv_sems, loc_sem):
    x, y, c = lax.axis_index("x"), lax.axis_index("y"), lax.axis_index("c")
    chips = [(1 - x, y), (x, 1 - y), (1 - x, 1 - y)]
    loc = [pltpu.make_async_copy(piece(src_r, 2 * x + y), own_o, loc_sem)]
    rem = [pltpu.make_async_remote_copy(
        src_ref=piece(srcb_r, 2 * px + py), dst_ref=rem_o.at[k], send_sem=send_sems.at[k],
        recv_sem=recv_sems.at[k], device_id=(px, py, c), device_id_type=MESH) for k, (px, py) in enumerate(chips)]
    return loc, rem


def _small_copies(small_r, sall_o, send_sems, recv_sems, loc_sem):
    x, y, c = lax.axis_index("x"), lax.axis_index("y"), lax.axis_index("c")
    me = 4 * x + 2 * y + c
    loc = [pltpu.make_async_copy(small_r, sall_o.at[me], loc_sem)]
    rem = []
    k = 3
    for fx in range(2):
        for fy in range(2):
            for fc in range(2):
                if fx or fy or fc:
                    peer = (1 - x if fx else x, 1 - y if fy else y, 1 - c if fc else c)
                    rem.append(pltpu.make_async_remote_copy(
                        src_ref=small_r, dst_ref=sall_o.at[me], send_sem=send_sems.at[k],
                        recv_sem=recv_sems.at[k], device_id=peer, device_id_type=MESH))
                    k += 1
    return loc, rem


def _adamw_math(w, g, m, v):
    m = B1 * m + (1.0 - B1) * g
    v = B2 * v + (1.0 - B2) * (g * g)
    m_hat = m / (1.0 - B1 ** STEP)
    v_hat = v / (1.0 - B2 ** STEP)
    delta = -LR * (m_hat / (jnp.sqrt(v_hat) + AEPS) + WD * w)
    return delta, m, v


def _adamw(big_in, big_out, sall, params):
    def body(*refs):
        wi, gi, mi, vi, wo, go, mo, vo, sall_r = refs[:9]
        ins = refs[9:24]
        di_o, mi_o, vi_o, do_o, mo_o, vo_o = refs[24:30]
        outs = refs[30:]
        d, mm, vv = _adamw_math(wi[...], gi[...], mi[...], vi[...])
        di_o[...] = d
        mi_o[...] = mm
        vi_o[...] = vv

        @pl.when(pl.program_id(0) == 0)
        def _():
            d, mm, vv = _adamw_math(wo[...], go[...], mo[...], vo[...])
            do_o[...] = d
            mo_o[...] = mm
            vo_o[...] = vv
            tot = sall_r[0]
            for dv in range(1, 8):
                tot = tot + sall_r[dv]
            grads = [tot[16:17, :], tot[1:2, 0:AW], tot[1:2, AW:], tot[8:10, 0:HW], tot[0:1, :]]
            outs[0][...] = tot[2:3, 0:1]
            for p in range(5):
                w_r, m_r, v_r = ins[3 * p:3 * p + 3]
                g = grads[p]
                d, mm, vv = _adamw_math(w_r[...], g, m_r[...], v_r[...])
                outs[1 + 4 * p][...] = g
                outs[2 + 4 * p][...] = d
                outs[3 + 4 * p][...] = mm
                outs[4 + 4 * p][...] = vv

    flat = [a for p in params for a in p]
    shapes = [jax.ShapeDtypeStruct((D, 1024), F32)] * 3 + [jax.ShapeDtypeStruct((256, D), F32)] * 3
    shapes += [jax.ShapeDtypeStruct((1, 1), F32)]
    for p in params:
        shapes += [jax.ShapeDtypeStruct(p[0].shape, F32)] * 4
    vm = pl.BlockSpec(memory_space=pltpu.VMEM)
    rows = pl.BlockSpec((512, 1024), lambda i: (i, 0))
    whole = pl.BlockSpec((256, D), lambda i: (0, 0))
    return pl.pallas_call(
        body, name="adamw", grid=(2,),
        in_specs=[rows] * 4 + [whole] * 4 + [vm] * 16, out_specs=[rows] * 3 + [whole] * 3 + [vm] * 21,
        out_shape=shapes,
        compiler_params=_cp(("arbitrary",)),
    )(*big_in, *big_out, sall, *flat)


def kernel(x, positions, w_in, w_out, mix_norm_w, attn_out_norm_w, hgrn_out_norm_w, hgrn_lb_raw, final_norm_w, loss_target, m_w_in, m_w_out, m_mix_norm_w, m_attn_out_norm_w, m_hgrn_out_norm_w, m_hgrn_lb_raw, m_final_norm_w, v_w_in, v_w_out, v_mix_norm_w, v_attn_out_norm_w, v_hgrn_out_norm_w, v_hgrn_lb_raw, v_final_norm_w):
    xs = x.reshape(T, D)
    tgt = loss_target.reshape(T, D)
    pos = positions.reshape(T, 1)
    fnw = final_norm_w.reshape(1, D)

    ti = np.arange(TH)
    tri_np = ((ti[:, None] // CHUNK == ti[None, :] // CHUNK) & (ti[None, :] <= ti[:, None])).astype(np.float32)
    tri = jnp.asarray(tri_np, BF16)
    trit = jnp.asarray(tri_np.T, BF16)
    hi_ = np.arange(AW) // HEAD
    gmat = jnp.asarray((hi_[:256, None] == hi_[None, :256]).astype(np.float32) / HEAD, BF16)
    emat_np = (np.arange(128)[:, None] == hi_[None, :]).astype(np.float32)
    sel_np = (8 + hi_[:, None] == np.arange(128)[None, :]).astype(np.float32)
    emat = jnp.asarray(emat_np, BF16)
    selmat = jnp.asarray(sel_np, BF16)

    jm_arr = (2 * lax.axis_index("x") + lax.axis_index("y")).astype(jnp.int32).reshape(1)
    (hn, q1, k1, v1, q4, k4, v4, q16, k16, v16, ag, hq, hf, hi, hg, w_full, wout4) = _fwd_in(
        xs, pos, mix_norm_w, w_in.reshape(D, 1024), w_out.reshape(256, D), jm_arr)
    wout_full = wout4.reshape(D, D)
    flat = lambda a: a.reshape(T, AW)
    o1, l1, o4, l4, o16, l16 = _attn_fwd([(q1, k1, v1, T // BLK),
                                          (flat(q4), flat(k4), flat(v4), T // 4 // BLK),
                                          (flat(q16), flat(k16), flat(v16), T // 16 // BLK)])
    rec, sall = _hgrn_fwd(hq, hf, hi, hgrn_lb_raw, tri)

    (dx2, do1, do4, do16, st1, st4, st16, drec, dag, dhg, rout, routb, small4) = _fwd_out(
        o1, o4.reshape(4, T // 4, AW), o16.reshape(16, T // 16, AW),
        l1, l4.reshape(4, T // 4, 128), l16.reshape(16, T // 16, 128),
        rec, ag, hg, xs, tgt, attn_out_norm_w, hgrn_out_norm_w, fnw, wout_full, gmat, emat, selmat)

    fst = lambda a: a.reshape(T, 128)
    dq1, dk1, dv1, dq4, dk4, dv4, dq16, dk16, dv16 = _attn_bwd([
        (q1, k1, v1, do1, st1, T // BLK),
        (flat(q4), flat(k4), flat(v4), flat(do4), fst(st4), T // 4 // BLK),
        (flat(q16), flat(k16), flat(v16), flat(do16), fst(st16), T // 16 // BLK)])
    dproj_h, small6, pout_own, pout_rem = _hgrn_bwd(hq, hf, hi, hgrn_lb_raw, tri, trit, drec, sall, dhg,
                                                    rout, routb)

    r4 = lambda a: a.reshape(4, T // 4, AW)
    r16 = lambda a: a.reshape(16, T // 16, AW)
    dproj_a = _dproj_build((dq1, r4(dq4), r16(dq16)), (dk1, r4(dk4), r16(dk16)), (dv1, r4(dv4), r16(dv16)),
                           dag, pos)
    rin, rinb = _grad_w_in(hn, dproj_a, dproj_h)
    gx, _, _, small_all, fin, fout = _bwd_x(dproj_a, dproj_h, xs, dx2, mix_norm_w, w_full, rin, rinb,
                                            small4, small6, pout_own, pout_rem)
    g_w_in = fin.reshape(D, 1024)
    g_w_out = fout.reshape(256, D)

    params = [(mix_norm_w, m_mix_norm_w, v_mix_norm_w),
              (attn_out_norm_w, m_attn_out_norm_w, v_attn_out_norm_w),
              (hgrn_out_norm_w, m_hgrn_out_norm_w, v_hgrn_out_norm_w),
              (hgrn_lb_raw, m_hgrn_lb_raw, v_hgrn_lb_raw),
              (fnw, m_final_norm_w.reshape(1, D), v_final_norm_w.reshape(1, D))]
    d_in, nm_in, nv_in, d_out, nm_out, nv_out, *so = _adamw(
        (w_in.reshape(D, 1024), g_w_in, m_w_in.reshape(D, 1024), v_w_in.reshape(D, 1024)),
        (w_out.reshape(256, D), g_w_out, m_w_out.reshape(256, D), v_w_out.reshape(256, D)), small_all, params)
    loss = so[0].reshape(())
    g_s = [so[1 + 4 * p] for p in range(5)]
    d_s = [so[2 + 4 * p] for p in range(5)]
    m_s = [so[3 + 4 * p] for p in range(5)]
    v_s = [so[4 + 4 * p] for p in range(5)]
    for lst in (g_s, d_s, m_s, v_s):
        lst[4] = lst[4].reshape(D)

    return (loss, gx.reshape(1, T, D),
            g_w_in.reshape(1, D, 1024), g_w_out.reshape(1, 256, D), *g_s,
            d_in.reshape(1, D, 1024), d_out.reshape(1, 256, D), *d_s,
            nm_in.reshape(1, D, 1024), nm_out.reshape(1, 256, D), *m_s,
            nv_in.reshape(1, D, 1024), nv_out.reshape(1, 256, D), *v_s)
```

```python
import functools

import numpy as np
import jax
import jax.numpy as jnp
from jax import lax
from jax.experimental import pallas as pl
from jax.experimental.pallas import tpu as pltpu

F32 = jnp.float32
BF16 = jnp.bfloat16

T = 4096
D = 1024
AW = 512
HW = 512
NCOL = 4096
HEAD = 64
BLK = 128
CHUNK = 64
EPS = 1e-6
SCALE = HEAD ** -0.5
NEG = -1e30
ROPE_THETA = 500000.0
INV_FREQ = [float(v) for v in
            (np.float32(ROPE_THETA) ** (-(np.arange(8, dtype=np.float32)) * np.float32(0.125)))]
LR, B1, B2, AEPS, WD, STEP = 0.001, 0.9, 0.999, 1e-08, 0.01, 10
VMEM_LIMIT = 63 * 1024 * 1024
MESH = pl.DeviceIdType.MESH


def _cp(sem=None, **kw):
    return pltpu.CompilerParams(dimension_semantics=sem, vmem_limit_bytes=VMEM_LIMIT, **kw)


def _mm(a, b):
    return jnp.dot(a, b, preferred_element_type=F32)


def _mm_nt(a, b):
    return lax.dot_general(a, b, (((1,), (1,)), ((), ())), preferred_element_type=F32)


def _mm_tn(a, b):
    return lax.dot_general(a, b, (((0,), (0,)), ((), ())), preferred_element_type=F32)


def _mm_exact_l(mat_bf, x):
    h = x.astype(BF16)
    l = (x - h.astype(F32)).astype(BF16)
    return _mm(mat_bf, h) + _mm(mat_bf, l)


def _mm_exact_r(x, mat_bf):
    h = x.astype(BF16)
    l = (x - h.astype(F32)).astype(BF16)
    return _mm(h, mat_bf) + _mm(l, mat_bf)


def _sigmoid(x):
    return 0.5 * jnp.tanh(0.5 * x) + 0.5


def _rope_tables(pos):
    lane = lax.broadcasted_iota(jnp.int32, (1, 128), 1)
    jl = lane & 63
    fi = jl & 7
    inv = jnp.zeros((1, 128), F32)
    for kk in range(8):
        inv = jnp.where(fi == kk, INV_FREQ[kk], inv)
    ang = pos.astype(F32) * inv
    c = jnp.cos(ang)
    s = jnp.sin(ang)
    cosf = jnp.where(jl < 16, c, 1.0)
    s1 = jnp.where(jl < 8, -s, 0.0)
    s2 = jnp.where((jl >= 8) & (jl < 16), s, 0.0)
    return cosf, s1, s2


def _rope(t, cosf, s1, s2):
    parts = []
    for ci in range(t.shape[1] // 128):
        tc = t[:, ci * 128:(ci + 1) * 128]
        parts.append(tc * cosf + pltpu.roll(tc, 120, 1) * s1 + pltpu.roll(tc, 8, 1) * s2)
    return jnp.concatenate(parts, axis=1)


def _rope_bwd(g, cosf, s1, s2):
    parts = []
    for ci in range(g.shape[1] // 128):
        gc = g[:, ci * 128:(ci + 1) * 128]
        parts.append(gc * cosf + pltpu.roll(gc * s1, 8, 1) + pltpu.roll(gc * s2, 120, 1))
    return jnp.concatenate(parts, axis=1)


def _perm_store(val, scr, scr2, o1, o4, o16, dt):
    n = val.shape[0]
    q = n // 4
    o1[...] = val.astype(dt)
    for ci in range(val.shape[1] // 128):
        cs = slice(ci * 128, (ci + 1) * 128)
        scr[ci] = val[:, cs]
        for r4 in range(4):
            part = scr[ci, pl.ds(r4, q, stride=4), :]
            o4[r4, :, cs] = part.astype(dt)
            scr2[ci, r4 * q:(r4 + 1) * q, :] = part
        for r4 in range(4):
            for b in range(4):
                o16[r4 + 4 * b, :, cs] = scr2[ci, pl.ds(r4 * q + b, q // 4, stride=4), :].astype(dt)


def _unperm_load(r4, r16, scr_a, scr_b, scr_c):
    n = scr_a.shape[1]
    q = n // 4
    nc = r4.shape[-1] // 128
    for ci in range(nc):
        cs = slice(ci * 128, (ci + 1) * 128)
        for rr in range(4):
            scr_a[ci, pl.ds(rr, q, stride=4), :] = r4[rr, :, cs].astype(F32)
        for rr in range(4):
            for b in range(4):
                scr_c[ci, pl.ds(rr * q + b, q // 4, stride=4), :] = r16[rr + 4 * b, :, cs].astype(F32)
        for rr in range(4):
            scr_b[ci, pl.ds(rr, q, stride=4), :] = scr_c[ci, rr * q:(rr + 1) * q, :]
    return (jnp.concatenate([scr_a[ci] for ci in range(nc)], axis=1),
            jnp.concatenate([scr_b[ci] for ci in range(nc)], axis=1))


def _unperm_sum(r4, r16, scr_b, scr_c):
    n = scr_b.shape[1]
    q = n // 4
    nc = r4.shape[-1] // 128
    for ci in range(nc):
        cs = slice(ci * 128, (ci + 1) * 128)
        for rr in range(4):
            for b in range(4):
                scr_c[ci, pl.ds(rr * q + b, q // 4, stride=4), :] = r16[rr + 4 * b, :, cs].astype(F32)
        for rr in range(4):
            scr_b[ci, pl.ds(rr, q, stride=4), :] = scr_c[ci, rr * q:(rr + 1) * q, :] + r4[rr, :, cs].astype(F32)
    return jnp.concatenate([scr_b[ci] for ci in range(nc)], axis=1)


def _fwd_in(x, pos, mixw, w_in, w_out, jm_arr):
    TT = 512
    NT = T // TT

    def body(jm_ref, x_ref, pos_ref, mw_ref, win_ref, wout_ref,
             hnt_ref, q1, k1, v1, q4, k4, v4, q16, k16, v16, ag, hq, hf, hi, hg, wfull_o, woutfull_o,
             wbuf, wobuf, hn_all, scr, scr2, stage, send_sems, recv_sems, loc_sems):
        s = pl.program_id(0)
        i = pl.program_id(1)
        mx, my, c = lax.axis_index("x"), lax.axis_index("y"), lax.axis_index("c")
        me, sibling = (mx, my, c), (mx, my, 1 - c)
        chips = [(mx, 1 - my), (1 - mx, my), (1 - mx, 1 - my)]
        jm = 2 * mx + my
        rows_in = [pl.ds(pl.multiple_of(h * 512, 512), 512) for h in (c, 1 - c)]
        rows_out = [pl.ds(pl.multiple_of(h * 128, 128), 128) for h in (c, 1 - c)]

        def blk(k):
            return lax.bitwise_xor(jm, k + 1)

        def rc(n, ref, to):
            return pltpu.make_async_remote_copy(src_ref=ref, dst_ref=ref, send_sem=send_sems.at[n],
                                                recv_sem=recv_sems.at[n], device_id=to, device_id_type=MESH)

        halves = [pl.ds(0, 512), pl.ds(512, 512)]
        send_in = lambda k, h: rc(12 + 2 * k + h, wbuf.at[jm, rows_in[0], halves[h]], (*chips[k], c))
        got_in = lambda k, h: rc(12 + 2 * k + h, wbuf.at[blk(k), rows_in[0], halves[h]], me)
        relay = lambda h: rc(16 + h, wbuf.at[blk(h), rows_in[0], halves[h]], (*chips[1 - h], c))
        got_relay = lambda h: rc(16 + h, wbuf.at[blk(2), rows_in[0], halves[h]], me)
        send_out = lambda k: rc(3 + k, wobuf.at[jm, rows_out[0], :], (*chips[k], c))
        got_out = lambda k: rc(3 + k, wobuf.at[blk(k), rows_out[0], :], me)
        pass_in = lambda k: rc(6 + k, wbuf.at[blk(k), rows_in[0], :], sibling)
        pass_out = lambda k: rc(9 + k, wobuf.at[blk(k), rows_out[0], :], sibling)
        passed_in = lambda k: rc(6 + k, wbuf.at[blk(k), rows_in[1], :], me)
        passed_out = lambda k: rc(9 + k, wobuf.at[blk(k), rows_out[1], :], me)

        def keep(j, n):
            return pltpu.make_async_copy(wbuf.at[j], wfull_o.at[:, pl.ds(j * 1024, 1024)], loc_sems.at[n])

        @pl.when((s == 0) & (i == 0))
        def _():
            for p in range(5):
                src = win_ref.at[pl.ds(p * 256, 256), :] if p < 4 else wout_ref
                load = pltpu.make_async_copy(src, stage, loc_sems.at[4])
                load.start()
                load.wait()
                if p < 4:
                    wbuf[jm, p * 256:(p + 1) * 256, :] = stage[...].astype(BF16)
                else:
                    wobuf[jm] = stage[...].astype(BF16)
            for k in range(2):
                for h in range(2):
                    send_in(k, h).start()
            keep(jm, 0).start()

        def arrive(k):
            if k == 0:
                for kk in range(2):
                    for h in range(2):
                        got_in(kk, h).wait_recv()
                relay(0).start()
                relay(1).start()
            if k == 2:
                got_relay(0).wait_recv()
                got_relay(1).wait_recv()
            pass_in(k).start()
            passed_in(k).wait_recv()
            keep(blk(k), k + 1).start()
            if k == 2:
                for kk in range(3):
                    send_out(kk).start()

        pl.when((s == 1) & (i == 0))(functools.partial(arrive, 0))

        @pl.when((s == 2) & (i == 0))
        def _():
            arrive(1)
            arrive(2)

        tile = pl.ds(pl.multiple_of(i * TT, TT), TT)

        @pl.when(s == 0)
        def _():
            xv = x_ref[...]
            r = lax.rsqrt(jnp.mean(xv * xv, axis=-1, keepdims=True) + EPS)
            hnf = (xv * r) * mw_ref[...]
            hn_all[tile, :] = hnf.astype(BF16)
            hnt_ref[...] = hnf.T.astype(BF16)

        def project(jj):
            hn = hn_all[tile, :]
            lo = _mm(hn, wbuf[jj, :, 0:512])
            hi_cols = _mm(hn, wbuf[jj, :, 512:1024])
            if jj == 0:
                cosf, s1, s2 = _rope_tables(pos_ref[...])
                _perm_store(_rope(lo, cosf, s1, s2) * SCALE, scr, scr2, q1, q4, q16, BF16)
                _perm_store(_rope(hi_cols, cosf, s1, s2), scr, scr2, k1, k4, k16, BF16)
            elif jj == 1:
                _perm_store(lo, scr, scr2, v1, v4, v16, BF16)
                ag[...] = hi_cols.astype(BF16)
            elif jj == 2:
                hq[...] = lo.astype(BF16)
                hf[...] = hi_cols.astype(BF16)
            else:
                hi[...] = lo.astype(BF16)
                hg[...] = hi_cols.astype(BF16)

        def project_block(j):
            for jj in range(4):
                pl.when(j == jj)(functools.partial(project, jj))

        @pl.when(s < 2)
        def _():
            project_block(lax.bitwise_xor(jm, s))

        @pl.when(s == 2)
        def _():
            project_block(lax.bitwise_xor(jm, 2))
            project_block(lax.bitwise_xor(jm, 3))

        @pl.when((s == 2) & (i == NT - 1))
        def _():
            for k in range(3):
                got_out(k).wait_recv()
                pass_out(k).start()
            for k in range(3):
                passed_out(k).wait_recv()
            out = pltpu.make_async_copy(wobuf, woutfull_o, loc_sems.at[4])
            out.start()
            for h in range(2):
                relay(h).wait_send()
                for k in range(2):
                    send_in(k, h).wait_send()
            for k in range(3):
                send_out(k).wait_send()
                pass_in(k).wait_send()
                pass_out(k).wait_send()
            keep(jm, 0).wait()
            for k in range(3):
                keep(blk(k), k + 1).wait()
            out.wait()

    def at_stage_of(jb):
        def index(s, i, jm_ref):
            sa = jnp.minimum(lax.bitwise_xor(jm_ref[0], jb), 2)
            return jnp.where(s < sa, 0, jnp.where(s == sa, i, NT - 1))
        return index

    tok = lambda w, jb: pl.BlockSpec((TT, w), lambda s, i, jm_ref: (at_stage_of(jb)(s, i, jm_ref), 0))
    d4 = lambda jb: pl.BlockSpec((4, TT // 4, AW), lambda s, i, jm_ref: (0, at_stage_of(jb)(s, i, jm_ref), 0))
    d16 = lambda jb: pl.BlockSpec((16, TT // 16, AW), lambda s, i, jm_ref: (0, at_stage_of(jb)(s, i, jm_ref), 0))
    hbm = pl.BlockSpec(memory_space=pltpu.HBM)
    sd = lambda shape, dt: jax.ShapeDtypeStruct(shape, dt)
    in_own_stage = lambda s, i: jnp.where(s == 0, i, NT - 1)
    grid_spec = pltpu.PrefetchScalarGridSpec(
        num_scalar_prefetch=1, grid=(3, NT),
        in_specs=[pl.BlockSpec((TT, D), lambda s, i, jm_ref: (in_own_stage(s, i), 0)),
                  pl.BlockSpec((TT, 1), lambda s, i, jm_ref: (i, 0)),
                  pl.BlockSpec((1, D), lambda s, i, jm_ref: (0, 0)), hbm, hbm],
        out_specs=[pl.BlockSpec((D, TT), lambda s, i, jm_ref: (0, in_own_stage(s, i))),
                   tok(AW, 0), tok(AW, 0), tok(AW, 1), d4(0), d4(0), d4(1), d16(0), d16(0), d16(1),
                   tok(AW, 1), tok(AW, 2), tok(AW, 2), tok(AW, 3), tok(AW, 3), hbm, hbm],
        scratch_shapes=[pltpu.VMEM((4, D, 1024), BF16), pltpu.VMEM((4, 256, D), BF16), pltpu.VMEM((T, D), BF16),
                        pltpu.VMEM((4, TT, 128), F32), pltpu.VMEM((4, TT, 128), F32), pltpu.VMEM((256, 1024), F32),
                        pltpu.SemaphoreType.DMA((18,)),
                        pltpu.SemaphoreType.DMA((18,)), pltpu.SemaphoreType.DMA((6,))])
    return pl.pallas_call(
        body, name="fwd_in", grid_spec=grid_spec,
        out_shape=[sd((D, T), BF16)] + [sd((T, AW), BF16)] * 3 + [sd((4, T // 4, AW), BF16)] * 3
        + [sd((16, T // 16, AW), BF16)] * 3
        + [sd((T, AW), BF16)] * 5 + [sd((D, NCOL), BF16), sd((4, 256, D), BF16)],
        compiler_params=_cp(("arbitrary", "arbitrary")),
    )(jm_arr, x, pos, mixw, w_in, w_out)


def _band_mask(key_axis, nkeys=2 * BLK):
    shape = (nkeys, 2 * BLK) if key_axis == 0 else (2 * BLK, nkeys)
    kj = lax.broadcasted_iota(jnp.int32, shape, key_axis)
    qi = lax.broadcasted_iota(jnp.int32, shape, 1 - key_axis) & (BLK - 1)
    return (kj >= qi) & (kj <= qi + BLK), kj, qi


def _stack_heads(t2, in_a):
    z = jnp.zeros_like(t2)
    return jnp.concatenate([jnp.where(in_a[0], t2, z), jnp.where(in_a[1], t2, z)], axis=0)


def _attn_fwd(q, k, v, nb, name):
    n = 8
    CH = n * BLK
    halo = nb > n

    def body(*refs):
        if halo:
            q_ref, k_ref, v_ref, kp_ref, vp_ref, o_ref, lse_ref = refs
        else:
            q_ref, k_ref, v_ref, o_ref, lse_ref = refs
        lane = lax.broadcasted_iota(jnp.int32, (1, 128), 1)
        in_a = [lane < HEAD, lane >= HEAD]
        band, kj, _ = _band_mask(1)
        thr0 = jnp.where((n * pl.program_id(0)) % nb == 0, BLK, 0) if halo else BLK
        mask0 = band & (kj >= thr0)
        mask_first = band & (kj >= BLK)
        for b in range(n):
            rs = slice(b * BLK, (b + 1) * BLK)
            stat = jnp.zeros((BLK, 128), F32)
            for hp in range(4):
                cs = slice(hp * 128, (hp + 1) * 128)
                q2s = _stack_heads(q_ref[rs, cs], in_a)
                if b == 0:
                    kprev = kp_ref[:, cs] if halo else k_ref[rs, cs]
                    vprev = vp_ref[:, cs] if halo else v_ref[rs, cs]
                    kk = jnp.concatenate([kprev, k_ref[rs, cs]], axis=0)
                    vv = jnp.concatenate([vprev, v_ref[rs, cs]], axis=0)
                    mask = mask0
                else:
                    kk = k_ref[(b - 1) * BLK:(b + 1) * BLK, cs]
                    vv = v_ref[(b - 1) * BLK:(b + 1) * BLK, cs]
                    mask = mask_first if b % nb == 0 else band
                s = jnp.where(mask, _mm_nt(q2s, kk), NEG)
                m = jnp.max(s, axis=-1, keepdims=True)
                p = jnp.exp(s - m)
                l = jnp.sum(p, axis=-1, keepdims=True)
                o = _mm(p.astype(BF16), vv) / l
                lse = m + jnp.log(l)
                o_ref[rs, cs] = jnp.where(in_a[0], o[:BLK], o[BLK:]).astype(BF16)
                stat = jnp.where(lane == 2 * hp, lse[:BLK], stat)
                stat = jnp.where(lane == 2 * hp + 1, lse[BLK:], stat)
            lse_ref[rs, :] = stat

    cur = pl.BlockSpec((CH, AW), lambda i: (i, 0))
    prev = pl.BlockSpec((BLK, AW), lambda i: (jnp.maximum(n * i - 1, 0), 0))
    return pl.pallas_call(
        body, name=name, grid=(T // CH,),
        in_specs=[cur, cur, cur] + ([prev, prev] if halo else []),
        out_specs=[cur, pl.BlockSpec((CH, 128), lambda i: (i, 0))],
        out_shape=[jax.ShapeDtypeStruct((T, AW), BF16), jax.ShapeDtypeStruct((T, 128), F32)],
        compiler_params=_cp(("parallel",)),
    )(*((q, k, v) + ((k, v) if halo else ())))


def _attn_bwd(q, k, v, do, st, nb, name):
    n = 8
    CH = n * BLK
    NBLK = T // BLK
    halo = nb > n

    def body(*refs):
        if halo:
            (q_ref, k_ref, v_ref, do_ref, st_ref, kp_ref, vp_ref, qn_ref, don_ref, stn_ref,
             dq_ref, dk_ref, dv_ref) = refs
        else:
            q_ref, k_ref, v_ref, do_ref, st_ref, dq_ref, dk_ref, dv_ref = refs
        i = pl.program_id(0)
        lane = lax.broadcasted_iota(jnp.int32, (1, 128), 1)
        in_a = [lane < HEAD, lane >= HEAD]
        band, kj, _ = _band_mask(0)
        thr0 = jnp.where((n * i) % nb == 0, BLK, 0) if halo else BLK
        mask0 = band & (kj >= thr0)
        mask_first = band & (kj >= BLK)

        def stat_rows(st_t, hp):
            lse_r = jnp.concatenate([st_t[2 * hp:2 * hp + 1, :], st_t[2 * hp + 1:2 * hp + 2, :]], axis=1)
            dl_r = jnp.concatenate([st_t[8 + 2 * hp:9 + 2 * hp, :], st_t[9 + 2 * hp:10 + 2 * hp, :]], axis=1)
            return lse_r, dl_r

        st_t = [st_ref[b * BLK:(b + 1) * BLK, :].T for b in range(n)]
        if halo:
            nxt_thr = jnp.where((n * i + n) % nb == 0, 2 * BLK, 0)
            _, kj1, qi1 = _band_mask(0, BLK)
            mask_next = kj1 >= qi1 + nxt_thr
            stn_t = stn_ref[...].T

        for hp in range(4):
            cs = slice(hp * 128, (hp + 1) * 128)
            kb = [k_ref[b * BLK:(b + 1) * BLK, cs] for b in range(n)]
            vb = [v_ref[b * BLK:(b + 1) * BLK, cs] for b in range(n)]
            dk_acc = [jnp.zeros((BLK, 128), F32) for _ in range(n)]
            dv_acc = [jnp.zeros((BLK, 128), F32) for _ in range(n)]
            for b in range(n):
                rs = slice(b * BLK, (b + 1) * BLK)
                q2s = _stack_heads(q_ref[rs, cs], in_a)
                do2s = _stack_heads(do_ref[rs, cs], in_a)
                if b == 0:
                    kprev = kp_ref[:, cs] if halo else kb[0]
                    vprev = vp_ref[:, cs] if halo else vb[0]
                    mask = mask0
                else:
                    kprev, vprev, mask = kb[b - 1], vb[b - 1], (mask_first if b % nb == 0 else band)
                kk = jnp.concatenate([kprev, kb[b]], axis=0)
                vv = jnp.concatenate([vprev, vb[b]], axis=0)
                lse_r, dl_r = stat_rows(st_t[b], hp)
                s_t = jnp.where(mask, _mm_nt(kk, q2s), NEG)
                p_t = jnp.exp(s_t - lse_r)
                ds_t = (p_t * (_mm_nt(vv, do2s) - dl_r)).astype(BF16)
                dkk = _mm(ds_t, q2s)
                dvv = _mm(p_t.astype(BF16), do2s)
                dqs = _mm_tn(ds_t, kk) * SCALE
                dq_ref[rs, cs] = jnp.where(in_a[0], dqs[:BLK], dqs[BLK:]).astype(BF16)
                dk_acc[b] += dkk[BLK:]
                dv_acc[b] += dvv[BLK:]
                if b > 0:
                    dk_acc[b - 1] += dkk[:BLK]
                    dv_acc[b - 1] += dvv[:BLK]
            if halo:
                q2s = _stack_heads(qn_ref[:, cs], in_a)
                do2s = _stack_heads(don_ref[:, cs], in_a)
                lse_r, dl_r = stat_rows(stn_t, hp)
                s_t = jnp.where(mask_next, _mm_nt(kb[n - 1], q2s), NEG)
                p_t = jnp.exp(s_t - lse_r)
                ds_t = (p_t * (_mm_nt(vb[n - 1], do2s) - dl_r)).astype(BF16)
                dk_acc[n - 1] += _mm(ds_t, q2s)
                dv_acc[n - 1] += _mm(p_t.astype(BF16), do2s)
            for b in range(n):
                dk_ref[b * BLK:(b + 1) * BLK, cs] = dk_acc[b].astype(BF16)
                dv_ref[b * BLK:(b + 1) * BLK, cs] = dv_acc[b].astype(BF16)

    cur = pl.BlockSpec((CH, AW), lambda i: (i, 0))
    cur_st = pl.BlockSpec((CH, 128), lambda i: (i, 0))
    prev = pl.BlockSpec((BLK, AW), lambda i: (jnp.maximum(n * i - 1, 0), 0))
    nxt = pl.BlockSpec((BLK, AW), lambda i: (jnp.minimum(n * i + n, NBLK - 1), 0))
    nxt_st = pl.BlockSpec((BLK, 128), lambda i: (jnp.minimum(n * i + n, NBLK - 1), 0))
    ins = [cur] * 4 + [cur_st] + ([prev, prev, nxt, nxt, nxt_st] if halo else [])
    args = (q, k, v, do, st) + ((k, v, q, do, st) if halo else ())
    return pl.pallas_call(
        body, name=name, grid=(T // CH,),
        in_specs=ins,
        out_specs=[cur] * 3,
        out_shape=[jax.ShapeDtypeStruct((T, AW), BF16)] * 3,
        compiler_params=_cp(("parallel",)),
    )(*args)


TH = 256
NCH = TH // CHUNK


def _hgrn_common(hq_ref, hf_ref, lbr_ref, tri_ref):
    r0 = lbr_ref[0:1, :]
    r1 = lbr_ref[1:2, :]
    mx = jnp.maximum(r0, r1)
    e0 = jnp.exp(r0 - mx)
    e1 = jnp.exp(r1 - mx)
    lb = e0 / (e0 + e1)
    hqv = hq_ref[...].astype(F32)
    sq = _sigmoid(hqv)
    qv = hqv * sq
    sf = _sigmoid(hf_ref[...].astype(F32))
    f = lb + (1.0 - lb) * sf
    kv = 1.0 - f
    g = jnp.log(f)
    cum = _mm_exact_l(tri_ref[...], g)
    dec = jnp.exp(jnp.concatenate([cum[c * CHUNK + CHUNK - 1:(c + 1) * CHUNK, :] for c in range(NCH)], axis=0))
    decb = jnp.concatenate([jnp.broadcast_to(dec[c:c + 1, :], (CHUNK, HW)) for c in range(NCH)], axis=0)
    ea = jnp.exp(cum)
    ena = jnp.exp(-cum)
    eend = decb * ena
    return dict(lb=lb, hq=hqv, sq=sq, q=qv, sf=sf, f=f, k=kv, cum=cum, ea=ea, ena=ena, eend=eend,
                qd=qv * ea, ki=kv * ena, ke=kv * eend, dec=dec)


def _tri_mask(transposed=False):
    ti = lax.broadcasted_iota(jnp.int32, (TH, TH), 1 if transposed else 0)
    si = lax.broadcasted_iota(jnp.int32, (TH, TH), 0 if transposed else 1)
    return (si <= ti) & ((si // CHUNK) == (ti // CHUNK))


def _hgrn_fwd(hq, hf, hi, lbr, tri):
    NSUB = 2

    def body(hq_ref, hf_ref, hi_ref, lbr_ref, tri_ref, rec_ref, sall_ref, st_scr):
        @pl.when(pl.program_id(0) == 0)
        def _():
            st_scr[...] = jnp.zeros_like(st_scr)

        causal = _tri_mask()
        for u in range(NSUB):
            tile = slice(u * TH, (u + 1) * TH)
            w = _hgrn_common(hq_ref.at[tile, :], hf_ref.at[tile, :], lbr_ref, tri_ref)
            qd, ki, ke = w["qd"].astype(BF16), w["ki"].astype(BF16), w["ke"].astype(BF16)
            dec = w["dec"]
            vb = hi_ref[tile, :]
            for h in range(4):
                cs = slice(h * 128, (h + 1) * 128)
                att = jnp.where(causal, _mm_nt(qd[:, cs], ki[:, cs]), 0.0)
                o_intra = _mm(att.astype(BF16), vb[:, cs])
                st = st_scr[:, cs]
                for c in range(NCH):
                    rs = slice(c * CHUNK, (c + 1) * CHUNK)
                    sall_ref[u * NCH + c, :, cs] = st
                    rec_ref[u * TH + c * CHUNK:u * TH + (c + 1) * CHUNK, cs] = (
                        o_intra[rs] + _mm_nt(qd[rs, cs], st.astype(BF16))).astype(BF16)
                    st = dec[c:c + 1, cs] * st + _mm_tn(vb[rs, cs], ke[rs, cs])
                st_scr[:, cs] = st

    tok = pl.BlockSpec((NSUB * TH, HW), lambda i: (i, 0))
    return pl.pallas_call(
        body, name="hgrn_fwd", grid=(T // (NSUB * TH),),
        in_specs=[tok, tok, tok, pl.BlockSpec((2, HW), lambda i: (0, 0)), pl.BlockSpec((TH, TH), lambda i: (0, 0))],
        out_specs=[tok, pl.BlockSpec((NSUB * NCH, 128, HW), lambda i: (i, 0, 0))],
        out_shape=[jax.ShapeDtypeStruct((T, HW), BF16), jax.ShapeDtypeStruct((T // CHUNK, 128, HW), F32)],
        scratch_shapes=[pltpu.VMEM((128, HW), F32)],
        compiler_params=_cp(("arbitrary",)),
    )(hq, hf, hi, lbr, tri)


def _hgrn_bwd(hq, hf, hi, lbr, tri, trit, drec, sall, dhg, rout, routb):
    NSUB = 2
    NT = T // (NSUB * TH)

    def body(hq_ref, hf_ref, hi_ref, lbr_ref, tri_ref, trit_ref, do_ref, sall_ref, dhg_ref, rout_r, routb_r,
             dph_ref, small_ref, pout_o, poutr_o,
             dst_scr, dlb_scr, dqd_scr, dki_scr, dke_scr, dlast_scr, send_sems, recv_sems, loc_sems):
        step = pl.program_id(0)
        loc, rem = _chip_copies(_w_out_piece, rout_r, routb_r, pout_o, poutr_o, send_sems, recv_sems,
                                loc_sems.at[0])

        @pl.when(step == 0)
        def _():
            dst_scr[...] = jnp.zeros_like(dst_scr)
            dlb_scr[...] = jnp.zeros_like(dlb_scr)
            for cp in loc + rem:
                cp.start()

        causal = _tri_mask()
        causal_t = _tri_mask(transposed=True)
        lb = None
        for u in reversed(range(NSUB)):
            tile = slice(u * TH, (u + 1) * TH)
            w = _hgrn_common(hq_ref.at[tile, :], hf_ref.at[tile, :], lbr_ref, tri_ref)
            qd, ki, ke = w["qd"].astype(BF16), w["ki"].astype(BF16), w["ke"].astype(BF16)
            dec = w["dec"]
            vb = hi_ref[tile, :]
            dob = do_ref[tile, :].astype(BF16)
            for h in range(4):
                cs = slice(h * 128, (h + 1) * 128)
                att_t = jnp.where(causal_t, _mm_nt(ki[:, cs], qd[:, cs]), 0.0).astype(BF16)
                datt_t = jnp.where(causal_t, _mm_nt(vb[:, cs], dob[:, cs]), 0.0).astype(BF16)
                datt = jnp.where(causal, _mm_nt(dob[:, cs], vb[:, cs]), 0.0).astype(BF16)
                dv_intra = _mm(att_t, dob[:, cs])
                dqd_intra = _mm(datt, ki[:, cs])
                dki_scr[u, :, cs] = _mm(datt_t, qd[:, cs])
                dst = dst_scr[:, cs]
                for c in reversed(range(NCH)):
                    rs = slice(c * CHUNK, (c + 1) * CHUNK)
                    dec_c = dec[c:c + 1, :]
                    st = sall_ref[u * NCH + c, :, cs]
                    dstb = dst.astype(BF16)
                    dph_ref[u * TH + c * CHUNK:u * TH + (c + 1) * CHUNK, 2 * HW + h * 128:2 * HW + (h + 1) * 128] = (
                        dv_intra[rs] + _mm_nt(ke[rs, cs], dstb)).astype(BF16)
                    dqd_scr[u, rs, cs] = dqd_intra[rs] + _mm(dob[rs, cs], st.astype(BF16))
                    dke_scr[u, rs, cs] = _mm(vb[rs, cs], dstb)
                    ddec = jnp.sum(dst * st, axis=0, keepdims=True)
                    dlast_scr[u, c:c + 1, cs] = ddec * dec_c[:, cs]
                    dst = dec_c[:, cs] * dst + _mm_tn(dob[rs, cs], qd[rs, cs])
                dst_scr[:, cs] = dst
            dqd, dki, dke = dqd_scr[u], dki_scr[u], dke_scr[u]
            dq = dqd * w["ea"]
            dk = dki * w["ena"] + dke * w["eend"]
            dcum = dqd * w["qd"] - dki * w["ki"] - dke * w["ke"]
            dkeke = dke * w["ke"]
            dlastb = jnp.concatenate(
                [jnp.broadcast_to(dlast_scr[u, c:c + 1, :]
                                  + jnp.sum(dkeke[c * CHUNK:(c + 1) * CHUNK], axis=0, keepdims=True), (CHUNK, HW))
                 for c in range(NCH)], axis=0)
            dg = _mm_exact_l(trit_ref[...], dcum) + dlastb
            df = dg / w["f"] - dk
            lb, sf, sq = w["lb"], w["sf"], w["sq"]
            dph_ref[tile, HW:2 * HW] = (df * (1.0 - lb) * sf * (1.0 - sf)).astype(BF16)
            dph_ref[tile, 0:HW] = (dq * (sq * (1.0 + w["hq"] * (1.0 - sq)))).astype(BF16)
            dph_ref[tile, 3 * HW:4 * HW] = dhg_ref[tile, :]
            dlb_scr[...] += jnp.sum(df * (1.0 - sf), axis=0, keepdims=True)

        @pl.when(step == NT - 1)
        def _():
            gr = dlb_scr[...] * lb * (1.0 - lb)
            small_ref[...] = jnp.zeros_like(small_ref)
            small_ref[0:1, 0:HW] = gr
            small_ref[1:2, 0:HW] = -gr
            for cp in rem:
                cp.wait_recv()
            for cp in rem:
                cp.wait_send()
            for cp in loc:
                cp.wait()

    tok = pl.BlockSpec((NSUB * TH, HW), lambda i: (NT - 1 - i, 0))
    const = lambda shape: pl.BlockSpec(shape, lambda i: (0,) * len(shape))
    hbm = pl.BlockSpec(memory_space=pltpu.HBM)
    return pl.pallas_call(
        body, name="hgrn_bwd", grid=(NT,),
        in_specs=[tok, tok, tok, const((2, HW)), const((TH, TH)), const((TH, TH)), tok,
                  pl.BlockSpec((NSUB * NCH, 128, HW), lambda i: (NT - 1 - i, 0, 0)), tok, hbm, hbm],
        out_specs=[pl.BlockSpec((NSUB * TH, NCOL // 2), lambda i: (NT - 1 - i, 0)), const((8, D)), hbm, hbm],
        out_shape=[jax.ShapeDtypeStruct((T, NCOL // 2), BF16), jax.ShapeDtypeStruct((8, D), F32),
                   jax.ShapeDtypeStruct((128, D), F32), jax.ShapeDtypeStruct((3, 128, D), BF16)],
        scratch_shapes=[pltpu.VMEM((128, HW), F32), pltpu.VMEM((1, HW), F32), pltpu.VMEM((NSUB, TH, HW), F32),
                        pltpu.VMEM((NSUB, TH, HW), F32), pltpu.VMEM((NSUB, TH, HW), F32),
                        pltpu.VMEM((NSUB, 8, HW), F32),
                        pltpu.SemaphoreType.DMA((3,)), pltpu.SemaphoreType.DMA((3,)), pltpu.SemaphoreType.DMA((1,))],
        compiler_params=_cp(("arbitrary",)),
    )(hq, hf, hi, lbr, tri, trit, drec, sall, dhg, rout, routb)


def _fwd_out(o1, o4, o16, l1, l4, l16, rec, ag, hg, x, tgt, anw, hnw, fnw, wout_full, gmat, emat, selmat):
    TT = 512

    def body(o1_r, o4_r, o16_r, l1_r, l4_r, l16_r, rec_r, ag_r, hg_r, x_r, tgt_r, anw_r, hnw_r, fnw_r, wo_r, g_r,
             e_r, sel_r, dx2_o, do1_o, do4_o, do16_o, st1_o, st4_o, st16_o, drec_o, dag_o, dhg_o,
             rout_o, routb_o, small_o, scr_a, scr_b, scr_c, gwout_o, rbuf, send_sems, recv_sems):
        @pl.when(pl.program_id(0) == 0)
        def _():
            gwout_o[...] = jnp.zeros_like(gwout_o)
            small_o[...] = jnp.zeros_like(small_o)

        def unperm(r4, r16):
            return _unperm_load(r4, r16, scr_a, scr_b, scr_c)

        def perm_out(val, p1, p4, p16, dt):
            _perm_store(val, scr_a, scr_b, p1, p4, p16, dt)

        o4u, o16u = unperm(o4_r, o16_r)
        l4c, l16c = unperm(l4_r, l16_r)
        l1c = l1_r[...]
        mxc = jnp.maximum(jnp.maximum(l1c, l4c), l16c)
        w1c, w4c, w16c = jnp.exp(l1c - mxc), jnp.exp(l4c - mxc), jnp.exp(l16c - mxc)
        denc = w1c + w4c + w16c
        lane = lax.broadcasted_iota(jnp.int32, (1, 128), 1)
        lse_c = jnp.where(lane < 8, mxc + jnp.log(denc), 0.0)
        em = e_r[...]
        wn1 = _mm_exact_r(w1c / denc, em)
        wn4 = _mm_exact_r(w4c / denc, em)
        o1v = o1_r[...].astype(F32)
        attn = wn1 * o1v + wn4 * o4u + (1.0 - wn1 - wn4) * o16u
        gm = g_r[...]

        def head_mean_a(t):
            return jnp.concatenate([_mm_exact_r(t[:, :256], gm), _mm_exact_r(t[:, 256:], gm)], axis=1)

        def head_mean_h(t):
            return jnp.concatenate(
                [jnp.broadcast_to(jnp.mean(t[:, h * 128:(h + 1) * 128], axis=-1, keepdims=True), (TT, 128))
                 for h in range(4)], axis=1)

        rs_a = lax.rsqrt(head_mean_a(attn * attn) + EPS)
        n_a = attn * rs_a
        agv = ag_r[...].astype(F32)
        sg_a = _sigmoid(agv)
        si_a = agv * sg_a
        anw_v = anw_r[...]
        y_a = (n_a * anw_v) * si_a
        recv = rec_r[...].astype(F32)
        rs_h = lax.rsqrt(head_mean_h(recv * recv) + EPS)
        n_h = recv * rs_h
        hgv = hg_r[...].astype(F32)
        sg_h = _sigmoid(hgv)
        si_h = hgv * sg_h
        hnw_v = hnw_r[...]
        y_h = (n_h * hnw_v) * si_h
        mixed = jnp.concatenate([y_a, y_h], axis=1).astype(BF16)
        xv = x_r[...]
        x2 = xv + _mm(mixed, wo_r[...])
        r2 = lax.rsqrt(jnp.mean(x2 * x2, axis=-1, keepdims=True) + EPS)
        fnw_v = fnw_r[...]
        xn = x2 * r2
        err = xn * fnw_v - tgt_r[...]
        small_o[2:3, :] += 0.5 * jnp.sum(jnp.mean(err * err, axis=-1, keepdims=True), axis=0, keepdims=True)
        dy = err * (1.0 / D)
        small_o[0:1, :] += jnp.sum(dy * xn, axis=0, keepdims=True)
        dyw = dy * fnw_v
        dx2 = r2 * dyw - x2 * ((r2 * r2 * r2) * jnp.mean(dyw * x2, axis=-1, keepdims=True))
        dx2_o[...] = dx2
        dx2b = dx2.astype(BF16)
        gwout_o[...] += _mm_tn(mixed, dx2b)
        dmix = _mm_nt(dx2b, wo_r[...])
        dm_a, dm_h = dmix[:, :AW], dmix[:, AW:]
        dag_o[...] = (dm_a * (n_a * anw_v) * (sg_a * (1.0 + agv * (1.0 - sg_a)))).astype(BF16)
        dn_a = dm_a * anw_v * si_a
        small_o[1:2, 0:AW] += jnp.sum(dm_a * n_a * si_a, axis=0, keepdims=True)
        dattn = rs_a * (dn_a - n_a * head_mean_a(dn_a * n_a))
        perm_out(dattn, do1_o, do4_o, do16_o, BF16)
        stats = lse_c + _mm_exact_r(dattn * attn, sel_r[...])
        perm_out(stats, st1_o, st4_o, st16_o, F32)
        dhg_o[...] = (dm_h * (n_h * hnw_v) * (sg_h * (1.0 + hgv * (1.0 - sg_h)))).astype(BF16)
        dn_h = dm_h * hnw_v * si_h
        small_o[1:2, AW:] += jnp.sum(dm_h * n_h * si_h, axis=0, keepdims=True)
        drec_o[...] = (rs_h * (dn_h - n_h * head_mean_h(dn_h * n_h))).astype(BF16)

        @pl.when(pl.program_id(0) == T // TT - 1)
        def _():
            x, y, c = lax.axis_index("x"), lax.axis_index("y"), lax.axis_index("c")
            cps = [pltpu.make_async_remote_copy(
                src_ref=gwout_o.at[pl.ds(pl.multiple_of(j * 256 + (1 - c) * 128, 128), 128), :], dst_ref=rbuf.at[j],
                send_sem=send_sems.at[j], recv_sem=recv_sems.at[j], device_id=(x, y, 1 - c), device_id_type=MESH)
                for j in range(4)]
            for cp in cps:
                cp.start()
            for j, cp in enumerate(cps):
                cp.wait_recv()
                red = gwout_o[pl.ds(pl.multiple_of(j * 256 + c * 128, 128), 128), :] + rbuf[j]
                rout_o[j * 128:(j + 1) * 128, :] = red
                routb_o[j * 128:(j + 1) * 128, :] = red.astype(BF16)
            for cp in cps:
                cp.wait_send()

    tok = lambda w: pl.BlockSpec((TT, w), lambda i: (i, 0))
    d4 = pl.BlockSpec((4, TT // 4, AW), lambda i: (0, i, 0))
    d16 = pl.BlockSpec((16, TT // 16, AW), lambda i: (0, i, 0))
    const = lambda shape: pl.BlockSpec(shape, lambda i: (0,) * len(shape))
    sd = lambda shape, dt: jax.ShapeDtypeStruct(shape, dt)
    c4 = pl.BlockSpec((4, TT // 4, 128), lambda i: (0, i, 0))
    c16 = pl.BlockSpec((16, TT // 16, 128), lambda i: (0, i, 0))
    p3 = lambda w, dt: [sd((T, w), dt), sd((4, T // 4, w), dt), sd((16, T // 16, w), dt)]
    return pl.pallas_call(
        body, name="fwd_out", grid=(T // TT,),
        in_specs=[tok(AW), d4, d16, tok(128), c4, c16, tok(AW), tok(AW), tok(AW), tok(D), tok(D),
                  const((1, AW)), const((1, HW)), const((1, D)), const((D, D)), const((256, 256)),
                  const((128, AW)), const((AW, 128))],
        out_specs=[tok(D)] + [tok(AW), d4, d16] + [tok(128), c4, c16] + [tok(AW)] * 3
        + [const((512, D)), const((512, D)), const((8, D))],
        out_shape=[sd((T, D), F32)] + p3(AW, BF16) + p3(128, F32)
        + [sd((T, AW), BF16), sd((T, AW), BF16), sd((T, AW), BF16), sd((512, D), F32), sd((512, D), BF16),
           sd((8, D), F32)],
        scratch_shapes=[pltpu.VMEM((4, TT, 128), F32)] * 3 + [pltpu.VMEM((D, D), F32),
                        pltpu.VMEM((4, 128, D), F32), pltpu.SemaphoreType.DMA((4,)), pltpu.SemaphoreType.DMA((4,))],
        compiler_params=_cp(("arbitrary",)),
    )(o1, o4, o16, l1, l4, l16, rec, ag, hg, x, tgt, anw, hnw, fnw, wout_full, gmat, emat, selmat)


def _dproj_build(dq, dk, dv, dag, pos):
    TT = 512

    def body(dq1, dq4, dq16, dk1, dk4, dk16, dv1, dv4, dv16, dag_r, pos_r, dproj_o, scr_b, scr_c):
        def unperm_sum(r1, r4, r16):
            return r1[...] + _unperm_sum(r4, r16, scr_b, scr_c)

        cosf, s1, s2 = _rope_tables(pos_r[...])
        dproj_o[:, 0:512] = _rope_bwd(unperm_sum(dq1, dq4, dq16), cosf, s1, s2).astype(BF16)
        dproj_o[:, 512:1024] = _rope_bwd(unperm_sum(dk1, dk4, dk16), cosf, s1, s2).astype(BF16)
        dproj_o[:, 1024:1536] = unperm_sum(dv1, dv4, dv16).astype(BF16)
        dproj_o[:, 1536:2048] = dag_r[...]

    tok = lambda w: pl.BlockSpec((TT, w), lambda i: (i, 0))
    d4 = pl.BlockSpec((4, TT // 4, AW), lambda i: (0, i, 0))
    d16 = pl.BlockSpec((16, TT // 16, AW), lambda i: (0, i, 0))
    return pl.pallas_call(
        body, name="dproj_build", grid=(T // TT,),
        in_specs=[tok(AW), d4, d16] * 3 + [tok(AW), tok(1)],
        out_specs=tok(NCOL // 2),
        out_shape=jax.ShapeDtypeStruct((T, NCOL // 2), BF16),
        scratch_shapes=[pltpu.VMEM((4, TT, 128), F32)] * 2,
        compiler_params=_cp(("parallel",)),
    )(*dq, *dk, *dv, dag, pos)


def _bwd_x(dproj_a, dproj_h, x, dx2, mixw, w_full, rin, rinb, small4, small6, pout_own, pout_rem):
    TT = 256
    NT = T // TT

    def body(dpa_r, dph_r, x_r, dx2_r, mw_r, w_r, rin_r, rinb_r, s4_r, s6_r, poo_r, por_r,
             gx_o, sall_o, fin_o, fout_o, sbuf, v_own, v_rem, vo_own, vo_rem, sin, sout, got_in,
             got_out, send_sems, recv_sems, loc_sems, share_send, share_recv, fin_sems):
        i = pl.program_id(0)
        loc, rem = _chip_copies(_w_in_piece, rin_r, rinb_r, v_own, v_rem, send_sems, recv_sems, loc_sems.at[0])
        loads = [pltpu.make_async_copy(poo_r, vo_own, fin_sems.at[2]),
                 pltpu.make_async_copy(por_r, vo_rem, fin_sems.at[3])]

        @pl.when(i == 0)
        def _():
            sbuf[...] = jnp.zeros_like(sbuf)
            for cp in loc + rem + loads:
                cp.start()

        dhn = _mm_nt(dpa_r[...], w_r[:, 0:NCOL // 2]) + _mm_nt(dph_r[...], w_r[:, NCOL // 2:NCOL])
        xv = x_r[...]
        r = lax.rsqrt(jnp.mean(xv * xv, axis=-1, keepdims=True) + EPS)
        dxw = dhn * mw_r[...]
        gx_o[...] = dx2_r[...] + r * dxw - xv * ((r * r * r) * jnp.mean(dxw * xv, axis=-1, keepdims=True))
        sbuf[16:17, :] += jnp.sum(dhn * (xv * r), axis=0, keepdims=True)

        @pl.when(i == NT - 1)
        def _():
            sbuf[0:8, :] = s4_r[...]
            sbuf[8:16, :] = s6_r[...]
            sloc, srem = _small_copies(sbuf, sall_o, send_sems, recv_sems, loc_sems.at[1])
            for cp in sloc + srem:
                cp.start()
            for cp in rem:
                cp.wait_recv()
            for cp in rem:
                cp.wait_send()
            for cp in loc:
                cp.wait()
            mx, my, c = lax.axis_index("x"), lax.axis_index("y"), lax.axis_index("c")
            for cp in loads:
                cp.wait()
            sout[...] = ((vo_own[...] + vo_rem[0].astype(F32)) + vo_rem[1].astype(F32)) + vo_rem[2].astype(F32)
            sin[...] = ((v_own[...] + v_rem[0].astype(F32)) + v_rem[1].astype(F32)) + v_rem[2].astype(F32)
            swap = [pltpu.make_async_remote_copy(src_ref=sin, dst_ref=got_in, send_sem=share_send.at[0],
                                                 recv_sem=share_recv.at[0], device_id=(mx, my, 1 - c),
                                                 device_id_type=MESH),
                    pltpu.make_async_remote_copy(src_ref=sout, dst_ref=got_out, send_sem=share_send.at[1],
                                                 recv_sem=share_recv.at[1], device_id=(mx, my, 1 - c),
                                                 device_id_type=MESH)]
            for cp in swap:
                cp.start()
            mine = [pltpu.make_async_copy(sin, fin_o.at[c], fin_sems.at[0]),
                    pltpu.make_async_copy(sout, fout_o.at[c], fin_sems.at[1])]
            for cp in mine:
                cp.start()
            for cp in swap:
                cp.wait_recv()
            theirs = [pltpu.make_async_copy(got_in, fin_o.at[1 - c], fin_sems.at[2]),
                      pltpu.make_async_copy(got_out, fout_o.at[1 - c], fin_sems.at[3])]
            for cp in theirs:
                cp.start()
            for cp in swap:
                cp.wait_send()
            for cp in mine + theirs:
                cp.wait()
            for cp in srem:
                cp.wait_recv()
            for cp in srem:
                cp.wait_send()
            for cp in sloc:
                cp.wait()

    tok = lambda w: pl.BlockSpec((TT, w), lambda i: (i, 0))
    const = lambda shape: pl.BlockSpec(shape, lambda i: (0,) * len(shape))
    hbm = pl.BlockSpec(memory_space=pltpu.HBM)
    return pl.pallas_call(
        body, name="bwd_x", grid=(NT,),
        in_specs=[tok(NCOL // 2), tok(NCOL // 2), tok(D), tok(D), const((1, D)), const((D, NCOL)), hbm, hbm,
                  const((8, D)), const((8, D)), hbm, hbm],
        out_specs=[tok(D), hbm, hbm, hbm],
        out_shape=[jax.ShapeDtypeStruct((T, D), F32),
                   jax.ShapeDtypeStruct((8, 24, D), F32),
                   jax.ShapeDtypeStruct((2, 512, 1024), F32), jax.ShapeDtypeStruct((2, 128, D), F32)],
        scratch_shapes=[pltpu.VMEM((24, D), F32),
                        pltpu.VMEM((512, 1024), F32), pltpu.VMEM((3, 512, 1024), BF16),
                        pltpu.VMEM((128, D), F32), pltpu.VMEM((3, 128, D), BF16),
                        pltpu.VMEM((512, 1024), F32), pltpu.VMEM((128, D), F32),
                        pltpu.VMEM((512, 1024), F32), pltpu.VMEM((128, D), F32),
                        pltpu.SemaphoreType.DMA((10,)), pltpu.SemaphoreType.DMA((10,)), pltpu.SemaphoreType.DMA((2,)),
                        pltpu.SemaphoreType.DMA((2,)), pltpu.SemaphoreType.DMA((2,)), pltpu.SemaphoreType.DMA((4,))],
        compiler_params=_cp(("arbitrary",)),
    )(dproj_a, dproj_h, x, dx2, mixw, w_full, rin, rinb, small4, small6, pout_own, pout_rem)


def _grad_w_in(hn, dproj_a, dproj_h):
    TK = 2048
    NK = T // TK

    def body(hnt_r, dpa_r, dph_r, rin_o, rinb_o, acc, rbuf, obuf, obufb, send_sems, recv_sems, wb_sems):
        j = pl.program_id(0)
        kk = pl.program_id(1)
        x, y, c = lax.axis_index("x"), lax.axis_index("y"), lax.axis_index("c")
        mine = pl.ds(pl.multiple_of(c * 512, 512), 512)
        theirs = pl.ds(pl.multiple_of((1 - c) * 512, 512), 512)

        def send(jj):
            return pltpu.make_async_remote_copy(
                src_ref=acc.at[jj % 2, theirs, :], dst_ref=rbuf.at[jj], send_sem=send_sems.at[jj],
                recv_sem=recv_sems.at[jj], device_id=(x, y, 1 - c), device_id_type=MESH)

        def writeback(jj):
            cols = pl.ds(jj * 1024, 1024)
            return [pltpu.make_async_copy(obuf.at[jj % 2], rin_o.at[:, cols], wb_sems.at[jj % 2]),
                    pltpu.make_async_copy(obufb.at[jj % 2], rinb_o.at[:, cols], wb_sems.at[2 + jj % 2])]

        def wait_writeback(jj):
            for cp in writeback(jj):
                cp.wait()

        def finalize(jj):
            send(jj).wait_recv()
            red = acc[jj % 2, mine, :] + rbuf[jj]
            obuf[jj % 2] = red
            obufb[jj % 2] = red.astype(BF16)
            for cp in writeback(jj):
                cp.start()

        prod = _mm(hnt_r[...], jnp.where(j < 2, dpa_r[...], dph_r[...]))

        @pl.when(kk == 0)
        def _():
            for jj in (2, 3):
                @pl.when(j == jj)
                def _():
                    send(jj - 2).wait_send()
            acc[j % 2] = prod

        @pl.when(kk > 0)
        def _():
            acc[j % 2] += prod

        @pl.when(kk == NK - 1)
        def _():
            for jj in range(4):
                @pl.when(j == jj)
                def _():
                    send(jj).start()
                    if jj in (1, 2):
                        finalize(jj - 1)
                    if jj == 3:
                        wait_writeback(0)
                        finalize(2)
                        wait_writeback(1)
                        finalize(3)
                        wait_writeback(2)
                        wait_writeback(3)
                        send(2).wait_send()
                        send(3).wait_send()

    hbm = pl.BlockSpec(memory_space=pltpu.HBM)
    return pl.pallas_call(
        body, name="grad_w_in", grid=(4, NK),
        in_specs=[pl.BlockSpec((D, TK), lambda j, kk: (0, kk)),
                  pl.BlockSpec((TK, 1024), lambda j, kk: (jnp.where(j < 2, kk, NK - 1), jnp.minimum(j, 1))),
                  pl.BlockSpec((TK, 1024), lambda j, kk: (jnp.where(j < 2, 0, kk), jnp.maximum(j - 2, 0)))],
        out_specs=[hbm, hbm],
        out_shape=[jax.ShapeDtypeStruct((512, NCOL), F32), jax.ShapeDtypeStruct((512, NCOL), BF16)],
        scratch_shapes=[pltpu.VMEM((2, D, 1024), F32), pltpu.VMEM((4, 512, 1024), F32), pltpu.VMEM((2, 512, 1024), F32),
                        pltpu.VMEM((2, 512, 1024), BF16),
                        pltpu.SemaphoreType.DMA((4,)), pltpu.SemaphoreType.DMA((4,)), pltpu.SemaphoreType.DMA((4,))],
        compiler_params=_cp(("arbitrary", "arbitrary")),
    )(hn, dproj_a, dproj_h)


def _w_in_piece(ref, j):
    return ref.at[:, pl.ds(j * 1024, 1024)]


def _w_out_piece(ref, j):
    return ref.at[pl.ds(j * 128, 128), :]


def _chip_copies(piece, src_r, srcb_r, own_o, rem_o, send_sems, recv_sems, loc_sem):
    x, y, c = lax.axis_index("x"), lax.axis_index("y"), lax.axis_index("c")
    chips = [(1 - x, y), (x, 1 - y), (1 - x, 1 - y)]
    loc = [pltpu.make_async_copy(piece(src_r, 2 * x + y), own_o, loc_sem)]
    rem = [pltpu.make_async_remote_copy(
        src_ref=piece(srcb_r, 2 * px + py), dst_ref=rem_o.at[k], send_sem=send_sems.at[k],
        recv_sem=recv_sems.at[k], device_id=(px, py, c), device_id_type=MESH) for k, (px, py) in enumerate(chips)]
    return loc, rem


def _small_copies(small_r, sall_o, send_sems, recv_sems, loc_sem):
    x, y, c = lax.axis_index("x"), lax.axis_index("y"), lax.axis_index("c")
    me = 4 * x + 2 * y + c
    loc = [pltpu.make_async_copy(small_r, sall_o.at[me], loc_sem)]
    rem = []
    k = 3
    for fx in range(2):
        for fy in range(2):
            for fc in range(2):
                if fx or fy or fc:
                    peer = (1 - x if fx else x, 1 - y if fy else y, 1 - c if fc else c)
                    rem.append(pltpu.make_async_remote_copy(
                        src_ref=small_r, dst_ref=sall_o.at[me], send_sem=send_sems.at[k],
                        recv_sem=recv_sems.at[k], device_id=peer, device_id_type=MESH))
                    k += 1
    return loc, rem


def _adamw_math(w, g, m, v):
    m = B1 * m + (1.0 - B1) * g
    v = B2 * v + (1.0 - B2) * (g * g)
    m_hat = m / (1.0 - B1 ** STEP)
    v_hat = v / (1.0 - B2 ** STEP)
    delta = -LR * (m_hat / (jnp.sqrt(v_hat) + AEPS) + WD * w)
    return delta, m, v


def _adamw(big_in, big_out, sall, params):
    def body(*refs):
        wi, gi, mi, vi, wo, go, mo, vo, sall_r = refs[:9]
        ins = refs[9:24]
        di_o, mi_o, vi_o, do_o, mo_o, vo_o = refs[24:30]
        outs = refs[30:]
        d, mm, vv = _adamw_math(wi[...], gi[...], mi[...], vi[...])
        di_o[...] = d
        mi_o[...] = mm
        vi_o[...] = vv

        @pl.when(pl.program_id(0) == 0)
        def _():
            d, mm, vv = _adamw_math(wo[...], go[...], mo[...], vo[...])
            do_o[...] = d
            mo_o[...] = mm
            vo_o[...] = vv
            tot = sall_r[0]
            for dv in range(1, 8):
                tot = tot + sall_r[dv]
            grads = [tot[16:17, :], tot[1:2, 0:AW], tot[1:2, AW:], tot[8:10, 0:HW], tot[0:1, :]]
            outs[0][...] = tot[2:3, 0:1]
            for p in range(5):
                w_r, m_r, v_r = ins[3 * p:3 * p + 3]
                g = grads[p]
                d, mm, vv = _adamw_math(w_r[...], g, m_r[...], v_r[...])
                outs[1 + 4 * p][...] = g
                outs[2 + 4 * p][...] = d
                outs[3 + 4 * p][...] = mm
                outs[4 + 4 * p][...] = vv

    flat = [a for p in params for a in p]
    shapes = [jax.ShapeDtypeStruct((D, 1024), F32)] * 3 + [jax.ShapeDtypeStruct((256, D), F32)] * 3
    shapes += [jax.ShapeDtypeStruct((1, 1), F32)]
    for p in params:
        shapes += [jax.ShapeDtypeStruct(p[0].shape, F32)] * 4
    vm = pl.BlockSpec(memory_space=pltpu.VMEM)
    rows = pl.BlockSpec((512, 1024), lambda i: (i, 0))
    whole = pl.BlockSpec((256, D), lambda i: (0, 0))
    return pl.pallas_call(
        body, name="adamw", grid=(2,),
        in_specs=[rows] * 4 + [whole] * 4 + [vm] * 16, out_specs=[rows] * 3 + [whole] * 3 + [vm] * 21,
        out_shape=shapes,
        compiler_params=_cp(("arbitrary",)),
    )(*big_in, *big_out, sall, *flat)


def kernel(x, positions, w_in, w_out, mix_norm_w, attn_out_norm_w, hgrn_out_norm_w, hgrn_lb_raw, final_norm_w, loss_target, m_w_in, m_w_out, m_mix_norm_w, m_attn_out_norm_w, m_hgrn_out_norm_w, m_hgrn_lb_raw, m_final_norm_w, v_w_in, v_w_out, v_mix_norm_w, v_attn_out_norm_w, v_hgrn_out_norm_w, v_hgrn_lb_raw, v_final_norm_w):
    xs = x.reshape(T, D)
    tgt = loss_target.reshape(T, D)
    pos = positions.reshape(T, 1)
    fnw = final_norm_w.reshape(1, D)

    ti = np.arange(TH)
    tri_np = ((ti[:, None] // CHUNK == ti[None, :] // CHUNK) & (ti[None, :] <= ti[:, None])).astype(np.float32)
    tri = jnp.asarray(tri_np, BF16)
    trit = jnp.asarray(tri_np.T, BF16)
    hi_ = np.arange(AW) // HEAD
    gmat = jnp.asarray((hi_[:256, None] == hi_[None, :256]).astype(np.float32) / HEAD, BF16)
    emat_np = (np.arange(128)[:, None] == hi_[None, :]).astype(np.float32)
    sel_np = (8 + hi_[:, None] == np.arange(128)[None, :]).astype(np.float32)
    emat = jnp.asarray(emat_np, BF16)
    selmat = jnp.asarray(sel_np, BF16)

    jm_arr = (2 * lax.axis_index("x") + lax.axis_index("y")).astype(jnp.int32).reshape(1)
    (hn, q1, k1, v1, q4, k4, v4, q16, k16, v16, ag, hq, hf, hi, hg, w_full, wout4) = _fwd_in(
        xs, pos, mix_norm_w, w_in.reshape(D, 1024), w_out.reshape(256, D), jm_arr)
    wout_full = wout4.reshape(D, D)
    flat = lambda a: a.reshape(T, AW)
    o1, l1 = _attn_fwd(q1, k1, v1, T // BLK, "attn_fwd_d1")
    o4, l4 = _attn_fwd(flat(q4), flat(k4), flat(v4), T // 4 // BLK, "attn_fwd_d4")
    o16, l16 = _attn_fwd(flat(q16), flat(k16), flat(v16), T // 16 // BLK, "attn_fwd_d16")
    rec, sall = _hgrn_fwd(hq, hf, hi, hgrn_lb_raw, tri)

    (dx2, do1, do4, do16, st1, st4, st16, drec, dag, dhg, rout, routb, small4) = _fwd_out(
        o1, o4.reshape(4, T // 4, AW), o16.reshape(16, T // 16, AW),
        l1, l4.reshape(4, T // 4, 128), l16.reshape(16, T // 16, 128),
        rec, ag, hg, xs, tgt, attn_out_norm_w, hgrn_out_norm_w, fnw, wout_full, gmat, emat, selmat)

    fst = lambda a: a.reshape(T, 128)
    dq1, dk1, dv1 = _attn_bwd(q1, k1, v1, do1, st1, T // BLK, "attn_bwd_d1")
    dq4, dk4, dv4 = _attn_bwd(flat(q4), flat(k4), flat(v4), flat(do4), fst(st4), T // 4 // BLK, "attn_bwd_d4")
    dq16, dk16, dv16 = _attn_bwd(flat(q16), flat(k16), flat(v16), flat(do16), fst(st16), T // 16 // BLK,
                                 "attn_bwd_d16")
    dproj_h, small6, pout_own, pout_rem = _hgrn_bwd(hq, hf, hi, hgrn_lb_raw, tri, trit, drec, sall, dhg,
                                                    rout, routb)

    r4 = lambda a: a.reshape(4, T // 4, AW)
    r16 = lambda a: a.reshape(16, T // 16, AW)
    dproj_a = _dproj_build((dq1, r4(dq4), r16(dq16)), (dk1, r4(dk4), r16(dk16)), (dv1, r4(dv4), r16(dv16)),
                           dag, pos)
    rin, rinb = _grad_w_in(hn, dproj_a, dproj_h)
    gx, small_all, fin, fout = _bwd_x(dproj_a, dproj_h, xs, dx2, mix_norm_w, w_full, rin, rinb,
                                            small4, small6, pout_own, pout_rem)
    g_w_in = fin.reshape(D, 1024)
    g_w_out = fout.reshape(256, D)

    params = [(mix_norm_w, m_mix_norm_w, v_mix_norm_w),
              (attn_out_norm_w, m_attn_out_norm_w, v_attn_out_norm_w),
              (hgrn_out_norm_w, m_hgrn_out_norm_w, v_hgrn_out_norm_w),
              (hgrn_lb_raw, m_hgrn_lb_raw, v_hgrn_lb_raw),
              (fnw, m_final_norm_w.reshape(1, D), v_final_norm_w.reshape(1, D))]
    d_in, nm_in, nv_in, d_out, nm_out, nv_out, *so = _adamw(
        (w_in.reshape(D, 1024), g_w_in, m_w_in.reshape(D, 1024), v_w_in.reshape(D, 1024)),
        (w_out.reshape(256, D), g_w_out, m_w_out.reshape(256, D), v_w_out.reshape(256, D)), small_all, params)
    loss = so[0].reshape(())
    g_s = [so[1 + 4 * p] for p in range(5)]
    d_s = [so[2 + 4 * p] for p in range(5)]
    m_s = [so[3 + 4 * p] for p in range(5)]
    v_s = [so[4 + 4 * p] for p in range(5)]
    for lst in (g_s, d_s, m_s, v_s):
        lst[4] = lst[4].reshape(D)

    return (loss, gx.reshape(1, T, D),
            g_w_in.reshape(1, D, 1024), g_w_out.reshape(1, 256, D), *g_s,
            d_in.reshape(1, D, 1024), d_out.reshape(1, 256, D), *d_s,
            nm_in.reshape(1, D, 1024), nm_out.reshape(1, 256, D), *m_s,
            nv_in.reshape(1, D, 1024), nv_out.reshape(1, 256, D), *v_s)
```

```python
import functools

import numpy as np
import jax
import jax.numpy as jnp
from jax import lax
from jax.experimental import pallas as pl
from jax.experimental.pallas import tpu as pltpu

F32 = jnp.float32
BF16 = jnp.bfloat16

T = 4096
D = 1024
AW = 512
HW = 512
NCOL = 4096
HEAD = 64
BLK = 128
CHUNK = 64
EPS = 1e-6
SCALE = HEAD ** -0.5
NEG = -1e30
ROPE_THETA = 500000.0
INV_FREQ = [float(v) for v in
            (np.float32(ROPE_THETA) ** (-(np.arange(8, dtype=np.float32)) * np.float32(0.125)))]
LR, B1, B2, AEPS, WD, STEP = 0.001, 0.9, 0.999, 1e-08, 0.01, 10
VMEM_LIMIT = 63 * 1024 * 1024
MESH = pl.DeviceIdType.MESH


def _cp(sem=None, **kw):
    return pltpu.CompilerParams(dimension_semantics=sem, vmem_limit_bytes=VMEM_LIMIT, **kw)


def _mm(a, b):
    return jnp.dot(a, b, preferred_element_type=F32)


def _mm_nt(a, b):
    return lax.dot_general(a, b, (((1,), (1,)), ((), ())), preferred_element_type=F32)


def _mm_tn(a, b):
    return lax.dot_general(a, b, (((0,), (0,)), ((), ())), preferred_element_type=F32)


def _mm_exact_l(mat_bf, x):
    h = x.astype(BF16)
    l = (x - h.astype(F32)).astype(BF16)
    return _mm(mat_bf, h) + _mm(mat_bf, l)


def _mm_exact_r(x, mat_bf):
    h = x.astype(BF16)
    l = (x - h.astype(F32)).astype(BF16)
    return _mm(h, mat_bf) + _mm(l, mat_bf)


def _sigmoid(x):
    return 0.5 * jnp.tanh(0.5 * x) + 0.5


def _rope_tables(pos):
    lane = lax.broadcasted_iota(jnp.int32, (1, 128), 1)
    jl = lane & 63
    fi = jl & 7
    inv = jnp.zeros((1, 128), F32)
    for kk in range(8):
        inv = jnp.where(fi == kk, INV_FREQ[kk], inv)
    ang = pos.astype(F32) * inv
    c = jnp.cos(ang)
    s = jnp.sin(ang)
    cosf = jnp.where(jl < 16, c, 1.0)
    s1 = jnp.where(jl < 8, -s, 0.0)
    s2 = jnp.where((jl >= 8) & (jl < 16), s, 0.0)
    return cosf, s1, s2


def _rope(t, cosf, s1, s2):
    parts = []
    for ci in range(t.shape[1] // 128):
        tc = t[:, ci * 128:(ci + 1) * 128]
        parts.append(tc * cosf + pltpu.roll(tc, 120, 1) * s1 + pltpu.roll(tc, 8, 1) * s2)
    return jnp.concatenate(parts, axis=1)


def _rope_bwd(g, cosf, s1, s2):
    parts = []
    for ci in range(g.shape[1] // 128):
        gc = g[:, ci * 128:(ci + 1) * 128]
        parts.append(gc * cosf + pltpu.roll(gc * s1, 8, 1) + pltpu.roll(gc * s2, 120, 1))
    return jnp.concatenate(parts, axis=1)


def _perm_store(val, scr, scr2, o1, o4, o16, dt):
    n = val.shape[0]
    q = n // 4
    o1[...] = val.astype(dt)
    for ci in range(val.shape[1] // 128):
        cs = slice(ci * 128, (ci + 1) * 128)
        scr[ci] = val[:, cs]
        for r4 in range(4):
            part = scr[ci, pl.ds(r4, q, stride=4), :]
            o4[r4, :, cs] = part.astype(dt)
            scr2[ci, r4 * q:(r4 + 1) * q, :] = part
        for r4 in range(4):
            for b in range(4):
                o16[r4 + 4 * b, :, cs] = scr2[ci, pl.ds(r4 * q + b, q // 4, stride=4), :].astype(dt)


def _unperm_load(r4, r16, scr_a, scr_b, scr_c):
    n = scr_a.shape[1]
    q = n // 4
    nc = r4.shape[-1] // 128
    for ci in range(nc):
        cs = slice(ci * 128, (ci + 1) * 128)
        for rr in range(4):
            scr_a[ci, pl.ds(rr, q, stride=4), :] = r4[rr, :, cs].astype(F32)
        for rr in range(4):
            for b in range(4):
                scr_c[ci, pl.ds(rr * q + b, q // 4, stride=4), :] = r16[rr + 4 * b, :, cs].astype(F32)
        for rr in range(4):
            scr_b[ci, pl.ds(rr, q, stride=4), :] = scr_c[ci, rr * q:(rr + 1) * q, :]
    return (jnp.concatenate([scr_a[ci] for ci in range(nc)], axis=1),
            jnp.concatenate([scr_b[ci] for ci in range(nc)], axis=1))


def _unperm_sum(r4, r16, scr_b, scr_c):
    n = scr_b.shape[1]
    q = n // 4
    nc = r4.shape[-1] // 128
    for ci in range(nc):
        cs = slice(ci * 128, (ci + 1) * 128)
        for rr in range(4):
            for b in range(4):
                scr_c[ci, pl.ds(rr * q + b, q // 4, stride=4), :] = r16[rr + 4 * b, :, cs].astype(F32)
        for rr in range(4):
            scr_b[ci, pl.ds(rr, q, stride=4), :] = scr_c[ci, rr * q:(rr + 1) * q, :] + r4[rr, :, cs].astype(F32)
    return jnp.concatenate([scr_b[ci] for ci in range(nc)], axis=1)


def _fwd_in(x, pos, mixw, w_in, w_out, jm_arr):
    TT = 512
    NT = T // TT

    def body(jm_ref, x_ref, pos_ref, mw_ref, win_ref, wout_ref,
             hnt_ref, q1, k1, v1, q4, k4, v4, q16, k16, v16, ag, hq, hf, hi, hg, wfull_o, woutfull_o,
             wbuf, wobuf, hn_all, scr, scr2, stage, send_sems, recv_sems, loc_sems):
        s = pl.program_id(0)
        i = pl.program_id(1)
        mx, my, c = lax.axis_index("x"), lax.axis_index("y"), lax.axis_index("c")
        me, sibling = (mx, my, c), (mx, my, 1 - c)
        chips = [(mx, 1 - my), (1 - mx, my), (1 - mx, 1 - my)]
        jm = 2 * mx + my
        rows_in = [pl.ds(pl.multiple_of(h * 512, 512), 512) for h in (c, 1 - c)]
        rows_out = [pl.ds(pl.multiple_of(h * 128, 128), 128) for h in (c, 1 - c)]

        def blk(k):
            return lax.bitwise_xor(jm, k + 1)

        def rc(n, ref, to):
            return pltpu.make_async_remote_copy(src_ref=ref, dst_ref=ref, send_sem=send_sems.at[n],
                                                recv_sem=recv_sems.at[n], device_id=to, device_id_type=MESH)

        halves = [pl.ds(0, 512), pl.ds(512, 512)]
        send_in = lambda k, h: rc(12 + 2 * k + h, wbuf.at[jm, rows_in[0], halves[h]], (*chips[k], c))
        got_in = lambda k, h: rc(12 + 2 * k + h, wbuf.at[blk(k), rows_in[0], halves[h]], me)
        relay = lambda h: rc(16 + h, wbuf.at[blk(h), rows_in[0], halves[h]], (*chips[1 - h], c))
        got_relay = lambda h: rc(16 + h, wbuf.at[blk(2), rows_in[0], halves[h]], me)
        send_out = lambda k: rc(3 + k, wobuf.at[jm, rows_out[0], :], (*chips[k], c))
        got_out = lambda k: rc(3 + k, wobuf.at[blk(k), rows_out[0], :], me)
        pass_in = lambda k: rc(6 + k, wbuf.at[blk(k), rows_in[0], :], sibling)
        pass_out = lambda k: rc(9 + k, wobuf.at[blk(k), rows_out[0], :], sibling)
        passed_in = lambda k: rc(6 + k, wbuf.at[blk(k), rows_in[1], :], me)
        passed_out = lambda k: rc(9 + k, wobuf.at[blk(k), rows_out[1], :], me)

        def keep(j, n):
            return pltpu.make_async_copy(wbuf.at[j], wfull_o.at[:, pl.ds(j * 1024, 1024)], loc_sems.at[n])

        @pl.when((s == 0) & (i == 0))
        def _():
            for p in range(5):
                src = win_ref.at[pl.ds(p * 256, 256), :] if p < 4 else wout_ref
                load = pltpu.make_async_copy(src, stage, loc_sems.at[4])
                load.start()
                load.wait()
                if p < 4:
                    wbuf[jm, p * 256:(p + 1) * 256, :] = stage[...].astype(BF16)
                else:
                    wobuf[jm] = stage[...].astype(BF16)
            for k in range(2):
                for h in range(2):
                    send_in(k, h).start()
            keep(jm, 0).start()

        def arrive(k):
            if k == 0:
                for kk in range(2):
                    for h in range(2):
                        got_in(kk, h).wait_recv()
                relay(0).start()
                relay(1).start()
            if k == 2:
                got_relay(0).wait_recv()
                got_relay(1).wait_recv()
            pass_in(k).start()
            passed_in(k).wait_recv()
            keep(blk(k), k + 1).start()
            if k == 2:
                for kk in range(3):
                    send_out(kk).start()

        pl.when((s == 1) & (i == 0))(functools.partial(arrive, 0))

        @pl.when((s == 2) & (i == 0))
        def _():
            arrive(1)
            arrive(2)

        tile = pl.ds(pl.multiple_of(i * TT, TT), TT)

        @pl.when(s == 0)
        def _():
            xv = x_ref[...]
            r = lax.rsqrt(jnp.mean(xv * xv, axis=-1, keepdims=True) + EPS)
            hnf = (xv * r) * mw_ref[...]
            hn_all[tile, :] = hnf.astype(BF16)
            hnt_ref[...] = hnf.T.astype(BF16)

        def project(jj):
            hn = hn_all[tile, :]
            lo = _mm(hn, wbuf[jj, :, 0:512])
            hi_cols = _mm(hn, wbuf[jj, :, 512:1024])
            if jj == 0:
                cosf, s1, s2 = _rope_tables(pos_ref[...])
                _perm_store(_rope(lo, cosf, s1, s2) * SCALE, scr, scr2, q1, q4, q16, BF16)
                _perm_store(_rope(hi_cols, cosf, s1, s2), scr, scr2, k1, k4, k16, BF16)
            elif jj == 1:
                _perm_store(lo, scr, scr2, v1, v4, v16, BF16)
                ag[...] = hi_cols.astype(BF16)
            elif jj == 2:
                hq[...] = lo.astype(BF16)
                hf[...] = hi_cols.astype(BF16)
            else:
                hi[...] = lo.astype(BF16)
                hg[...] = hi_cols.astype(BF16)

        def project_block(j):
            for jj in range(4):
                pl.when(j == jj)(functools.partial(project, jj))

        @pl.when(s < 2)
        def _():
            project_block(lax.bitwise_xor(jm, s))

        @pl.when(s == 2)
        def _():
            project_block(lax.bitwise_xor(jm, 2))
            project_block(lax.bitwise_xor(jm, 3))

        @pl.when((s == 2) & (i == NT - 1))
        def _():
            for k in range(3):
                got_out(k).wait_recv()
                pass_out(k).start()
            for k in range(3):
                passed_out(k).wait_recv()
            out = pltpu.make_async_copy(wobuf, woutfull_o, loc_sems.at[4])
            out.start()
            for h in range(2):
                relay(h).wait_send()
                for k in range(2):
                    send_in(k, h).wait_send()
            for k in range(3):
                send_out(k).wait_send()
                pass_in(k).wait_send()
                pass_out(k).wait_send()
            keep(jm, 0).wait()
            for k in range(3):
                keep(blk(k), k + 1).wait()
            out.wait()

    def at_stage_of(jb):
        def index(s, i, jm_ref):
            sa = jnp.minimum(lax.bitwise_xor(jm_ref[0], jb), 2)
            return jnp.where(s < sa, 0, jnp.where(s == sa, i, NT - 1))
        return index

    tok = lambda w, jb: pl.BlockSpec((TT, w), lambda s, i, jm_ref: (at_stage_of(jb)(s, i, jm_ref), 0))
    d4 = lambda jb: pl.BlockSpec((4, TT // 4, AW), lambda s, i, jm_ref: (0, at_stage_of(jb)(s, i, jm_ref), 0))
    d16 = lambda jb: pl.BlockSpec((16, TT // 16, AW), lambda s, i, jm_ref: (0, at_stage_of(jb)(s, i, jm_ref), 0))
    hbm = pl.BlockSpec(memory_space=pltpu.HBM)
    sd = lambda shape, dt: jax.ShapeDtypeStruct(shape, dt)
    in_own_stage = lambda s, i: jnp.where(s == 0, i, NT - 1)
    grid_spec = pltpu.PrefetchScalarGridSpec(
        num_scalar_prefetch=1, grid=(3, NT),
        in_specs=[pl.BlockSpec((TT, D), lambda s, i, jm_ref: (in_own_stage(s, i), 0)),
                  pl.BlockSpec((TT, 1), lambda s, i, jm_ref: (i, 0)),
                  pl.BlockSpec((1, D), lambda s, i, jm_ref: (0, 0)), hbm, hbm],
        out_specs=[pl.BlockSpec((D, TT), lambda s, i, jm_ref: (0, in_own_stage(s, i))),
                   tok(AW, 0), tok(AW, 0), tok(AW, 1), d4(0), d4(0), d4(1), d16(0), d16(0), d16(1),
                   tok(AW, 1), tok(AW, 2), tok(AW, 2), tok(AW, 3), tok(AW, 3), hbm, hbm],
        scratch_shapes=[pltpu.VMEM((4, D, 1024), BF16), pltpu.VMEM((4, 256, D), BF16), pltpu.VMEM((T, D), BF16),
                        pltpu.VMEM((4, TT, 128), F32), pltpu.VMEM((4, TT, 128), F32), pltpu.VMEM((256, 1024), F32),
                        pltpu.SemaphoreType.DMA((18,)),
                        pltpu.SemaphoreType.DMA((18,)), pltpu.SemaphoreType.DMA((6,))])
    return pl.pallas_call(
        body, name="fwd_in", grid_spec=grid_spec,
        out_shape=[sd((D, T), BF16)] + [sd((T, AW), BF16)] * 3 + [sd((4, T // 4, AW), BF16)] * 3
        + [sd((16, T // 16, AW), BF16)] * 3
        + [sd((T, AW), BF16)] * 5 + [sd((D, NCOL), BF16), sd((4, 256, D), BF16)],
        compiler_params=_cp(("arbitrary", "arbitrary")),
    )(jm_arr, x, pos, mixw, w_in, w_out)


def _band_mask(key_axis, nkeys=2 * BLK):
    shape = (nkeys, 2 * BLK) if key_axis == 0 else (2 * BLK, nkeys)
    kj = lax.broadcasted_iota(jnp.int32, shape, key_axis)
    qi = lax.broadcasted_iota(jnp.int32, shape, 1 - key_axis) & (BLK - 1)
    return (kj >= qi) & (kj <= qi + BLK), kj, qi


def _stack_heads(t2, in_a):
    z = jnp.zeros_like(t2)
    return jnp.concatenate([jnp.where(in_a[0], t2, z), jnp.where(in_a[1], t2, z)], axis=0)


def _attn_fwd(q, k, v, nb, name):
    n = 8
    CH = n * BLK
    halo = nb > n

    def body(*refs):
        if halo:
            q_ref, k_ref, v_ref, kp_ref, vp_ref, o_ref, lse_ref = refs
        else:
            q_ref, k_ref, v_ref, o_ref, lse_ref = refs
        lane = lax.broadcasted_iota(jnp.int32, (1, 128), 1)
        in_a = [lane < HEAD, lane >= HEAD]
        band, kj, _ = _band_mask(1)
        thr0 = jnp.where((n * pl.program_id(0)) % nb == 0, BLK, 0) if halo else BLK
        mask0 = band & (kj >= thr0)
        mask_first = band & (kj >= BLK)
        for b in range(n):
            rs = slice(b * BLK, (b + 1) * BLK)
            stat = jnp.zeros((BLK, 128), F32)
            for hp in range(4):
                cs = slice(hp * 128, (hp + 1) * 128)
                q2s = _stack_heads(q_ref[rs, cs], in_a)
                if b == 0:
                    kprev = kp_ref[:, cs] if halo else k_ref[rs, cs]
                    vprev = vp_ref[:, cs] if halo else v_ref[rs, cs]
                    kk = jnp.concatenate([kprev, k_ref[rs, cs]], axis=0)
                    vv = jnp.concatenate([vprev, v_ref[rs, cs]], axis=0)
                    mask = mask0
                else:
                    kk = k_ref[(b - 1) * BLK:(b + 1) * BLK, cs]
                    vv = v_ref[(b - 1) * BLK:(b + 1) * BLK, cs]
                    mask = mask_first if b % nb == 0 else band
                s = jnp.where(mask, _mm_nt(q2s, kk), NEG)
                m = jnp.max(s, axis=-1, keepdims=True)
                p = jnp.exp(s - m)
                l = jnp.sum(p, axis=-1, keepdims=True)
                o = _mm(p.astype(BF16), vv) / l
                lse = m + jnp.log(l)
                o_ref[rs, cs] = jnp.where(in_a[0], o[:BLK], o[BLK:]).astype(BF16)
                stat = jnp.where(lane == 2 * hp, lse[:BLK], stat)
                stat = jnp.where(lane == 2 * hp + 1, lse[BLK:], stat)
            lse_ref[rs, :] = stat

    cur = pl.BlockSpec((CH, AW), lambda i: (i, 0))
    prev = pl.BlockSpec((BLK, AW), lambda i: (jnp.maximum(n * i - 1, 0), 0))
    return pl.pallas_call(
        body, name=name, grid=(T // CH,),
        in_specs=[cur, cur, cur] + ([prev, prev] if halo else []),
        out_specs=[cur, pl.BlockSpec((CH, 128), lambda i: (i, 0))],
        out_shape=[jax.ShapeDtypeStruct((T, AW), BF16), jax.ShapeDtypeStruct((T, 128), F32)],
        compiler_params=_cp(("parallel",)),
    )(*((q, k, v) + ((k, v) if halo else ())))


def _attn_bwd(q, k, v, do, st, nb, name):
    n = 8
    CH = n * BLK
    NBLK = T // BLK
    halo = nb > n

    def body(*refs):
        if halo:
            (q_ref, k_ref, v_ref, do_ref, st_ref, kp_ref, vp_ref, qn_ref, don_ref, stn_ref,
             dq_ref, dk_ref, dv_ref) = refs
        else:
            q_ref, k_ref, v_ref, do_ref, st_ref, dq_ref, dk_ref, dv_ref = refs
        i = pl.program_id(0)
        lane = lax.broadcasted_iota(jnp.int32, (1, 128), 1)
        in_a = [lane < HEAD, lane >= HEAD]
        band, kj, _ = _band_mask(0)
        thr0 = jnp.where((n * i) % nb == 0, BLK, 0) if halo else BLK
        mask0 = band & (kj >= thr0)
        mask_first = band & (kj >= BLK)

        def stat_rows(st_t, hp):
            lse_r = jnp.concatenate([st_t[2 * hp:2 * hp + 1, :], st_t[2 * hp + 1:2 * hp + 2, :]], axis=1)
            dl_r = jnp.concatenate([st_t[8 + 2 * hp:9 + 2 * hp, :], st_t[9 + 2 * hp:10 + 2 * hp, :]], axis=1)
            return lse_r, dl_r

        st_t = [st_ref[b * BLK:(b + 1) * BLK, :].T for b in range(n)]
        if halo:
            nxt_thr = jnp.where((n * i + n) % nb == 0, 2 * BLK, 0)
            _, kj1, qi1 = _band_mask(0, BLK)
            mask_next = kj1 >= qi1 + nxt_thr
            stn_t = stn_ref[...].T

        for hp in range(4):
            cs = slice(hp * 128, (hp + 1) * 128)
            kb = [k_ref[b * BLK:(b + 1) * BLK, cs] for b in range(n)]
            vb = [v_ref[b * BLK:(b + 1) * BLK, cs] for b in range(n)]
            dk_acc = [jnp.zeros((BLK, 128), F32) for _ in range(n)]
            dv_acc = [jnp.zeros((BLK, 128), F32) for _ in range(n)]
            for b in range(n):
                rs = slice(b * BLK, (b + 1) * BLK)
                q2s = _stack_heads(q_ref[rs, cs], in_a)
                do2s = _stack_heads(do_ref[rs, cs], in_a)
                if b == 0:
                    kprev = kp_ref[:, cs] if halo else kb[0]
                    vprev = vp_ref[:, cs] if halo else vb[0]
                    mask = mask0
                else:
                    kprev, vprev, mask = kb[b - 1], vb[b - 1], (mask_first if b % nb == 0 else band)
                kk = jnp.concatenate([kprev, kb[b]], axis=0)
                vv = jnp.concatenate([vprev, vb[b]], axis=0)
                lse_r, dl_r = stat_rows(st_t[b], hp)
                s_t = jnp.where(mask, _mm_nt(kk, q2s), NEG)
                p_t = jnp.exp(s_t - lse_r)
                ds_t = (p_t * (_mm_nt(vv, do2s) - dl_r)).astype(BF16)
                dkk = _mm(ds_t, q2s)
                dvv = _mm(p_t.astype(BF16), do2s)
                dqs = _mm_tn(ds_t, kk) * SCALE
                dq_ref[rs, cs] = jnp.where(in_a[0], dqs[:BLK], dqs[BLK:]).astype(BF16)
                dk_acc[b] += dkk[BLK:]
                dv_acc[b] += dvv[BLK:]
                if b > 0:
                    dk_acc[b - 1] += dkk[:BLK]
                    dv_acc[b - 1] += dvv[:BLK]
            if halo:
                q2s = _stack_heads(qn_ref[:, cs], in_a)
                do2s = _stack_heads(don_ref[:, cs], in_a)
                lse_r, dl_r = stat_rows(stn_t, hp)
                s_t = jnp.where(mask_next, _mm_nt(kb[n - 1], q2s), NEG)
                p_t = jnp.exp(s_t - lse_r)
                ds_t = (p_t * (_mm_nt(vb[n - 1], do2s) - dl_r)).astype(BF16)
                dk_acc[n - 1] += _mm(ds_t, q2s)
                dv_acc[n - 1] += _mm(p_t.astype(BF16), do2s)
            for b in range(n):
                dk_ref[b * BLK:(b + 1) * BLK, cs] = dk_acc[b].astype(BF16)
                dv_ref[b * BLK:(b + 1) * BLK, cs] = dv_acc[b].astype(BF16)

    cur = pl.BlockSpec((CH, AW), lambda i: (i, 0))
    cur_st = pl.BlockSpec((CH, 128), lambda i: (i, 0))
    prev = pl.BlockSpec((BLK, AW), lambda i: (jnp.maximum(n * i - 1, 0), 0))
    nxt = pl.BlockSpec((BLK, AW), lambda i: (jnp.minimum(n * i + n, NBLK - 1), 0))
    nxt_st = pl.BlockSpec((BLK, 128), lambda i: (jnp.minimum(n * i + n, NBLK - 1), 0))
    ins = [cur] * 4 + [cur_st] + ([prev, prev, nxt, nxt, nxt_st] if halo else [])
    args = (q, k, v, do, st) + ((k, v, q, do, st) if halo else ())
    return pl.pallas_call(
        body, name=name, grid=(T // CH,),
        in_specs=ins,
        out_specs=[cur] * 3,
        out_shape=[jax.ShapeDtypeStruct((T, AW), BF16)] * 3,
        compiler_params=_cp(("parallel",)),
    )(*args)


TH = 256
NCH = TH // CHUNK


def _hgrn_common(hq_ref, hf_ref, lbr_ref, tri_ref):
    r0 = lbr_ref[0:1, :]
    r1 = lbr_ref[1:2, :]
    mx = jnp.maximum(r0, r1)
    e0 = jnp.exp(r0 - mx)
    e1 = jnp.exp(r1 - mx)
    lb = e0 / (e0 + e1)
    hqv = hq_ref[...].astype(F32)
    sq = _sigmoid(hqv)
    qv = hqv * sq
    sf = _sigmoid(hf_ref[...].astype(F32))
    f = lb + (1.0 - lb) * sf
    kv = 1.0 - f
    g = jnp.log(f)
    cum = _mm_exact_l(tri_ref[...], g)
    dec = jnp.exp(jnp.concatenate([cum[c * CHUNK + CHUNK - 1:(c + 1) * CHUNK, :] for c in range(NCH)], axis=0))
    decb = jnp.concatenate([jnp.broadcast_to(dec[c:c + 1, :], (CHUNK, HW)) for c in range(NCH)], axis=0)
    ea = jnp.exp(cum)
    ena = jnp.exp(-cum)
    eend = decb * ena
    return dict(lb=lb, hq=hqv, sq=sq, q=qv, sf=sf, f=f, k=kv, cum=cum, ea=ea, ena=ena, eend=eend,
                qd=qv * ea, ki=kv * ena, ke=kv * eend, dec=dec)


def _tri_mask(transposed=False):
    ti = lax.broadcasted_iota(jnp.int32, (TH, TH), 1 if transposed else 0)
    si = lax.broadcasted_iota(jnp.int32, (TH, TH), 0 if transposed else 1)
    return (si <= ti) & ((si // CHUNK) == (ti // CHUNK))


def _hgrn_fwd(hq, hf, hi, lbr, tri):
    NSUB = 2

    def body(hq_ref, hf_ref, hi_ref, lbr_ref, tri_ref, rec_ref, sall_ref, st_scr):
        @pl.when(pl.program_id(0) == 0)
        def _():
            st_scr[...] = jnp.zeros_like(st_scr)

        causal = _tri_mask()
        for u in range(NSUB):
            tile = slice(u * TH, (u + 1) * TH)
            w = _hgrn_common(hq_ref.at[tile, :], hf_ref.at[tile, :], lbr_ref, tri_ref)
            qd, ki, ke = w["qd"].astype(BF16), w["ki"].astype(BF16), w["ke"].astype(BF16)
            dec = w["dec"]
            vb = hi_ref[tile, :]
            for h in range(4):
                cs = slice(h * 128, (h + 1) * 128)
                att = jnp.where(causal, _mm_nt(qd[:, cs], ki[:, cs]), 0.0)
                o_intra = _mm(att.astype(BF16), vb[:, cs])
                st = st_scr[:, cs]
                for c in range(NCH):
                    rs = slice(c * CHUNK, (c + 1) * CHUNK)
                    sall_ref[u * NCH + c, :, cs] = st
                    rec_ref[u * TH + c * CHUNK:u * TH + (c + 1) * CHUNK, cs] = (
                        o_intra[rs] + _mm_nt(qd[rs, cs], st.astype(BF16))).astype(BF16)
                    st = dec[c:c + 1, cs] * st + _mm_tn(vb[rs, cs], ke[rs, cs])
                st_scr[:, cs] = st

    tok = pl.BlockSpec((NSUB * TH, HW), lambda i: (i, 0))
    return pl.pallas_call(
        body, name="hgrn_fwd", grid=(T // (NSUB * TH),),
        in_specs=[tok, tok, tok, pl.BlockSpec((2, HW), lambda i: (0, 0)), pl.BlockSpec((TH, TH), lambda i: (0, 0))],
        out_specs=[tok, pl.BlockSpec((NSUB * NCH, 128, HW), lambda i: (i, 0, 0))],
        out_shape=[jax.ShapeDtypeStruct((T, HW), BF16), jax.ShapeDtypeStruct((T // CHUNK, 128, HW), F32)],
        scratch_shapes=[pltpu.VMEM((128, HW), F32)],
        compiler_params=_cp(("arbitrary",)),
    )(hq, hf, hi, lbr, tri)


def _hgrn_bwd(hq, hf, hi, lbr, tri, trit, drec, sall, dhg, rout, routb):
    NSUB = 2
    NT = T // (NSUB * TH)

    def body(hq_ref, hf_ref, hi_ref, lbr_ref, tri_ref, trit_ref, do_ref, sall_ref, dhg_ref, rout_r, routb_r,
             dph_ref, small_ref, pout_o, poutr_o,
             dst_scr, dlb_scr, dqd_scr, dki_scr, dke_scr, dlast_scr, send_sems, recv_sems, loc_sems):
        step = pl.program_id(0)
        loc, rem = _chip_copies(_w_out_piece, rout_r, routb_r, pout_o, poutr_o, send_sems, recv_sems,
                                loc_sems.at[0])

        @pl.when(step == 0)
        def _():
            dst_scr[...] = jnp.zeros_like(dst_scr)
            dlb_scr[...] = jnp.zeros_like(dlb_scr)
            for cp in loc + rem:
                cp.start()

        causal = _tri_mask()
        causal_t = _tri_mask(transposed=True)
        lb = None
        for u in reversed(range(NSUB)):
            tile = slice(u * TH, (u + 1) * TH)
            w = _hgrn_common(hq_ref.at[tile, :], hf_ref.at[tile, :], lbr_ref, tri_ref)
            qd, ki, ke = w["qd"].astype(BF16), w["ki"].astype(BF16), w["ke"].astype(BF16)
            dec = w["dec"]
            vb = hi_ref[tile, :]
            dob = do_ref[tile, :].astype(BF16)
            for h in range(4):
                cs = slice(h * 128, (h + 1) * 128)
                att_t = jnp.where(causal_t, _mm_nt(ki[:, cs], qd[:, cs]), 0.0).astype(BF16)
                datt_t = jnp.where(causal_t, _mm_nt(vb[:, cs], dob[:, cs]), 0.0).astype(BF16)
                datt = jnp.where(causal, _mm_nt(dob[:, cs], vb[:, cs]), 0.0).astype(BF16)
                dv_intra = _mm(att_t, dob[:, cs])
                dqd_intra = _mm(datt, ki[:, cs])
                dki_scr[u, :, cs] = _mm(datt_t, qd[:, cs])
                dst = dst_scr[:, cs]
                for c in reversed(range(NCH)):
                    rs = slice(c * CHUNK, (c + 1) * CHUNK)
                    dec_c = dec[c:c + 1, :]
                    st = sall_ref[u * NCH + c, :, cs]
                    dstb = dst.astype(BF16)
                    dph_ref[u * TH + c * CHUNK:u * TH + (c + 1) * CHUNK, 2 * HW + h * 128:2 * HW + (h + 1) * 128] = (
                        dv_intra[rs] + _mm_nt(ke[rs, cs], dstb)).astype(BF16)
                    dqd_scr[u, rs, cs] = dqd_intra[rs] + _mm(dob[rs, cs], st.astype(BF16))
                    dke_scr[u, rs, cs] = _mm(vb[rs, cs], dstb)
                    ddec = jnp.sum(dst * st, axis=0, keepdims=True)
                    dlast_scr[u, c:c + 1, cs] = ddec * dec_c[:, cs]
                    dst = dec_c[:, cs] * dst + _mm_tn(dob[rs, cs], qd[rs, cs])
                dst_scr[:, cs] = dst
            dqd, dki, dke = dqd_scr[u], dki_scr[u], dke_scr[u]
            dq = dqd * w["ea"]
            dk = dki * w["ena"] + dke * w["eend"]
            dcum = dqd * w["qd"] - dki * w["ki"] - dke * w["ke"]
            dkeke = dke * w["ke"]
            dlastb = jnp.concatenate(
                [jnp.broadcast_to(dlast_scr[u, c:c + 1, :]
                                  + jnp.sum(dkeke[c * CHUNK:(c + 1) * CHUNK], axis=0, keepdims=True), (CHUNK, HW))
                 for c in range(NCH)], axis=0)
            dg = _mm_exact_l(trit_ref[...], dcum) + dlastb
            df = dg / w["f"] - dk
            lb, sf, sq = w["lb"], w["sf"], w["sq"]
            dph_ref[tile, HW:2 * HW] = (df * (1.0 - lb) * sf * (1.0 - sf)).astype(BF16)
            dph_ref[tile, 0:HW] = (dq * (sq * (1.0 + w["hq"] * (1.0 - sq)))).astype(BF16)
            dph_ref[tile, 3 * HW:4 * HW] = dhg_ref[tile, :]
            dlb_scr[...] += jnp.sum(df * (1.0 - sf), axis=0, keepdims=True)

        @pl.when(step == NT - 1)
        def _():
            gr = dlb_scr[...] * lb * (1.0 - lb)
            small_ref[...] = jnp.zeros_like(small_ref)
            small_ref[0:1, 0:HW] = gr
            small_ref[1:2, 0:HW] = -gr
            for cp in rem:
                cp.wait_recv()
            for cp in rem:
                cp.wait_send()
            for cp in loc:
                cp.wait()

    tok = pl.BlockSpec((NSUB * TH, HW), lambda i: (NT - 1 - i, 0))
    const = lambda shape: pl.BlockSpec(shape, lambda i: (0,) * len(shape))
    hbm = pl.BlockSpec(memory_space=pltpu.HBM)
    return pl.pallas_call(
        body, name="hgrn_bwd", grid=(NT,),
        in_specs=[tok, tok, tok, const((2, HW)), const((TH, TH)), const((TH, TH)), tok,
                  pl.BlockSpec((NSUB * NCH, 128, HW), lambda i: (NT - 1 - i, 0, 0)), tok, hbm, hbm],
        out_specs=[pl.BlockSpec((NSUB * TH, NCOL // 2), lambda i: (NT - 1 - i, 0)), const((8, D)), hbm, hbm],
        out_shape=[jax.ShapeDtypeStruct((T, NCOL // 2), BF16), jax.ShapeDtypeStruct((8, D), F32),
                   jax.ShapeDtypeStruct((128, D), F32), jax.ShapeDtypeStruct((3, 128, D), BF16)],
        scratch_shapes=[pltpu.VMEM((128, HW), F32), pltpu.VMEM((1, HW), F32), pltpu.VMEM((NSUB, TH, HW), F32),
                        pltpu.VMEM((NSUB, TH, HW), F32), pltpu.VMEM((NSUB, TH, HW), F32),
                        pltpu.VMEM((NSUB, 8, HW), F32),
                        pltpu.SemaphoreType.DMA((3,)), pltpu.SemaphoreType.DMA((3,)), pltpu.SemaphoreType.DMA((1,))],
        compiler_params=_cp(("arbitrary",)),
    )(hq, hf, hi, lbr, tri, trit, drec, sall, dhg, rout, routb)


def _fwd_out(o1, o4, o16, l1, l4, l16, rec, ag, hg, x, tgt, anw, hnw, fnw, wout_full, gmat, emat, selmat):
    TT = 512

    def body(o1_r, o4_r, o16_r, l1_r, l4_r, l16_r, rec_r, ag_r, hg_r, x_r, tgt_r, anw_r, hnw_r, fnw_r, wo_r, g_r,
             e_r, sel_r, dx2_o, do1_o, do4_o, do16_o, st1_o, st4_o, st16_o, drec_o, dag_o, dhg_o,
             rout_o, routb_o, small_o, scr_a, scr_b, scr_c, gwout_o, rbuf, send_sems, recv_sems):
        @pl.when(pl.program_id(0) == 0)
        def _():
            gwout_o[...] = jnp.zeros_like(gwout_o)
            small_o[...] = jnp.zeros_like(small_o)

        def unperm(r4, r16):
            return _unperm_load(r4, r16, scr_a, scr_b, scr_c)

        def perm_out(val, p1, p4, p16, dt):
            _perm_store(val, scr_a, scr_b, p1, p4, p16, dt)

        o4u, o16u = unperm(o4_r, o16_r)
        l4c, l16c = unperm(l4_r, l16_r)
        l1c = l1_r[...]
        mxc = jnp.maximum(jnp.maximum(l1c, l4c), l16c)
        w1c, w4c, w16c = jnp.exp(l1c - mxc), jnp.exp(l4c - mxc), jnp.exp(l16c - mxc)
        denc = w1c + w4c + w16c
        lane = lax.broadcasted_iota(jnp.int32, (1, 128), 1)
        lse_c = jnp.where(lane < 8, mxc + jnp.log(denc), 0.0)
        em = e_r[...]
        wn1 = _mm_exact_r(w1c / denc, em)
        wn4 = _mm_exact_r(w4c / denc, em)
        o1v = o1_r[...].astype(F32)
        attn = wn1 * o1v + wn4 * o4u + (1.0 - wn1 - wn4) * o16u
        gm = g_r[...]

        def head_mean_a(t):
            return jnp.concatenate([_mm_exact_r(t[:, :256], gm), _mm_exact_r(t[:, 256:], gm)], axis=1)

        def head_mean_h(t):
            return jnp.concatenate(
                [jnp.broadcast_to(jnp.mean(t[:, h * 128:(h + 1) * 128], axis=-1, keepdims=True), (TT, 128))
                 for h in range(4)], axis=1)

        rs_a = lax.rsqrt(head_mean_a(attn * attn) + EPS)
        n_a = attn * rs_a
        agv = ag_r[...].astype(F32)
        sg_a = _sigmoid(agv)
        si_a = agv * sg_a
        anw_v = anw_r[...]
        y_a = (n_a * anw_v) * si_a
        recv = rec_r[...].astype(F32)
        rs_h = lax.rsqrt(head_mean_h(recv * recv) + EPS)
        n_h = recv * rs_h
        hgv = hg_r[...].astype(F32)
        sg_h = _sigmoid(hgv)
        si_h = hgv * sg_h
        hnw_v = hnw_r[...]
        y_h = (n_h * hnw_v) * si_h
        mixed = jnp.concatenate([y_a, y_h], axis=1).astype(BF16)
        xv = x_r[...]
        x2 = xv + _mm(mixed, wo_r[...])
        r2 = lax.rsqrt(jnp.mean(x2 * x2, axis=-1, keepdims=True) + EPS)
        fnw_v = fnw_r[...]
        xn = x2 * r2
        err = xn * fnw_v - tgt_r[...]
        small_o[2:3, :] += 0.5 * jnp.sum(jnp.mean(err * err, axis=-1, keepdims=True), axis=0, keepdims=True)
        dy = err * (1.0 / D)
        small_o[0:1, :] += jnp.sum(dy * xn, axis=0, keepdims=True)
        dyw = dy * fnw_v
        dx2 = r2 * dyw - x2 * ((r2 * r2 * r2) * jnp.mean(dyw * x2, axis=-1, keepdims=True))
        dx2_o[...] = dx2
        dx2b = dx2.astype(BF16)
        gwout_o[...] += _mm_tn(mixed, dx2b)
        dmix = _mm_nt(dx2b, wo_r[...])
        dm_a, dm_h = dmix[:, :AW], dmix[:, AW:]
        dag_o[...] = (dm_a * (n_a * anw_v) * (sg_a * (1.0 + agv * (1.0 - sg_a)))).astype(BF16)
        dn_a = dm_a * anw_v * si_a
        small_o[1:2, 0:AW] += jnp.sum(dm_a * n_a * si_a, axis=0, keepdims=True)
        dattn = rs_a * (dn_a - n_a * head_mean_a(dn_a * n_a))
        perm_out(dattn, do1_o, do4_o, do16_o, BF16)
        stats = lse_c + _mm_exact_r(dattn * attn, sel_r[...])
        perm_out(stats, st1_o, st4_o, st16_o, F32)
        dhg_o[...] = (dm_h * (n_h * hnw_v) * (sg_h * (1.0 + hgv * (1.0 - sg_h)))).astype(BF16)
        dn_h = dm_h * hnw_v * si_h
        small_o[1:2, AW:] += jnp.sum(dm_h * n_h * si_h, axis=0, keepdims=True)
        drec_o[...] = (rs_h * (dn_h - n_h * head_mean_h(dn_h * n_h))).astype(BF16)

        @pl.when(pl.program_id(0) == T // TT - 1)
        def _():
            x, y, c = lax.axis_index("x"), lax.axis_index("y"), lax.axis_index("c")
            cps = [pltpu.make_async_remote_copy(
                src_ref=gwout_o.at[pl.ds(pl.multiple_of(j * 256 + (1 - c) * 128, 128), 128), :], dst_ref=rbuf.at[j],
                send_sem=send_sems.at[j], recv_sem=recv_sems.at[j], device_id=(x, y, 1 - c), device_id_type=MESH)
                for j in range(4)]
            for cp in cps:
                cp.start()
            for j, cp in enumerate(cps):
                cp.wait_recv()
                red = gwout_o[pl.ds(pl.multiple_of(j * 256 + c * 128, 128), 128), :] + rbuf[j]
                rout_o[j * 128:(j + 1) * 128, :] = red
                routb_o[j * 128:(j + 1) * 128, :] = red.astype(BF16)
            for cp in cps:
                cp.wait_send()

    tok = lambda w: pl.BlockSpec((TT, w), lambda i: (i, 0))
    d4 = pl.BlockSpec((4, TT // 4, AW), lambda i: (0, i, 0))
    d16 = pl.BlockSpec((16, TT // 16, AW), lambda i: (0, i, 0))
    const = lambda shape: pl.BlockSpec(shape, lambda i: (0,) * len(shape))
    sd = lambda shape, dt: jax.ShapeDtypeStruct(shape, dt)
    c4 = pl.BlockSpec((4, TT // 4, 128), lambda i: (0, i, 0))
    c16 = pl.BlockSpec((16, TT // 16, 128), lambda i: (0, i, 0))
    p3 = lambda w, dt: [sd((T, w), dt), sd((4, T // 4, w), dt), sd((16, T // 16, w), dt)]
    return pl.pallas_call(
        body, name="fwd_out", grid=(T // TT,),
        in_specs=[tok(AW), d4, d16, tok(128), c4, c16, tok(AW), tok(AW), tok(AW), tok(D), tok(D),
                  const((1, AW)), const((1, HW)), const((1, D)), const((D, D)), const((256, 256)),
                  const((128, AW)), const((AW, 128))],
        out_specs=[tok(D)] + [tok(AW), d4, d16] + [tok(128), c4, c16] + [tok(AW)] * 3
        + [const((512, D)), const((512, D)), const((8, D))],
        out_shape=[sd((T, D), F32)] + p3(AW, BF16) + p3(128, F32)
        + [sd((T, AW), BF16), sd((T, AW), BF16), sd((T, AW), BF16), sd((512, D), F32), sd((512, D), BF16),
           sd((8, D), F32)],
        scratch_shapes=[pltpu.VMEM((4, TT, 128), F32)] * 3 + [pltpu.VMEM((D, D), F32),
                        pltpu.VMEM((4, 128, D), F32), pltpu.SemaphoreType.DMA((4,)), pltpu.SemaphoreType.DMA((4,))],
        compiler_params=_cp(("arbitrary",)),
    )(o1, o4, o16, l1, l4, l16, rec, ag, hg, x, tgt, anw, hnw, fnw, wout_full, gmat, emat, selmat)


def _dproj_build(dq, dk, dv, dag, pos):
    TT = 512

    def body(dq1, dq4, dq16, dk1, dk4, dk16, dv1, dv4, dv16, dag_r, pos_r, dproj_o, scr_b, scr_c):
        def unperm_sum(r1, r4, r16):
            return r1[...] + _unperm_sum(r4, r16, scr_b, scr_c)

        cosf, s1, s2 = _rope_tables(pos_r[...])
        dproj_o[:, 0:512] = _rope_bwd(unperm_sum(dq1, dq4, dq16), cosf, s1, s2).astype(BF16)
        dproj_o[:, 512:1024] = _rope_bwd(unperm_sum(dk1, dk4, dk16), cosf, s1, s2).astype(BF16)
        dproj_o[:, 1024:1536] = unperm_sum(dv1, dv4, dv16).astype(BF16)
        dproj_o[:, 1536:2048] = dag_r[...]

    tok = lambda w: pl.BlockSpec((TT, w), lambda i: (i, 0))
    d4 = pl.BlockSpec((4, TT // 4, AW), lambda i: (0, i, 0))
    d16 = pl.BlockSpec((16, TT // 16, AW), lambda i: (0, i, 0))
    return pl.pallas_call(
        body, name="dproj_build", grid=(T // TT,),
        in_specs=[tok(AW), d4, d16] * 3 + [tok(AW), tok(1)],
        out_specs=tok(NCOL // 2),
        out_shape=jax.ShapeDtypeStruct((T, NCOL // 2), BF16),
        scratch_shapes=[pltpu.VMEM((4, TT, 128), F32)] * 2,
        compiler_params=_cp(("parallel",)),
    )(*dq, *dk, *dv, dag, pos)


def _bwd_x(dproj_a, dproj_h, x, dx2, mixw, w_full, rin, rinb, small4, small6, pout_own, pout_rem):
    TT = 256
    NT = T // TT

    def body(dpa_r, dph_r, x_r, dx2_r, mw_r, w_r, rin_r, rinb_r, s4_r, s6_r, poo_r, por_r,
             gx_o, sall_o, fin_o, fout_o, sbuf, v_own, v_rem, vo_own, vo_rem, sin, sout, got_in,
             got_out, send_sems, recv_sems, loc_sems, share_send, share_recv, fin_sems):
        i = pl.program_id(0)
        loc, rem = _chip_copies(_w_in_piece, rin_r, rinb_r, v_own, v_rem, send_sems, recv_sems, loc_sems.at[0])
        loads = [pltpu.make_async_copy(poo_r, vo_own, fin_sems.at[2]),
                 pltpu.make_async_copy(por_r, vo_rem, fin_sems.at[3])]

        @pl.when(i == 0)
        def _():
            sbuf[...] = jnp.zeros_like(sbuf)
            for cp in loc + rem + loads:
                cp.start()

        dhn = _mm_nt(dpa_r[...], w_r[:, 0:NCOL // 2]) + _mm_nt(dph_r[...], w_r[:, NCOL // 2:NCOL])
        xv = x_r[...]
        r = lax.rsqrt(jnp.mean(xv * xv, axis=-1, keepdims=True) + EPS)
        dxw = dhn * mw_r[...]
        gx_o[...] = dx2_r[...] + r * dxw - xv * ((r * r * r) * jnp.mean(dxw * xv, axis=-1, keepdims=True))
        sbuf[16:17, :] += jnp.sum(dhn * (xv * r), axis=0, keepdims=True)

        mx, my, c = lax.axis_index("x"), lax.axis_index("y"), lax.axis_index("c")
        swap = [pltpu.make_async_remote_copy(src_ref=sin, dst_ref=got_in, send_sem=share_send.at[0],
                                             recv_sem=share_recv.at[0], device_id=(mx, my, 1 - c),
                                             device_id_type=MESH),
                pltpu.make_async_remote_copy(src_ref=sout, dst_ref=got_out, send_sem=share_send.at[1],
                                             recv_sem=share_recv.at[1], device_id=(mx, my, 1 - c),
                                             device_id_type=MESH)]
        mine = [pltpu.make_async_copy(sin, fin_o.at[c], fin_sems.at[0]),
                pltpu.make_async_copy(sout, fout_o.at[c], fin_sems.at[1])]

        @pl.when(i == NT - 3)
        def _():
            for cp in rem:
                cp.wait_recv()
            for cp in rem:
                cp.wait_send()
            for cp in loc + loads:
                cp.wait()
            sout[...] = ((vo_own[...] + vo_rem[0].astype(F32)) + vo_rem[1].astype(F32)) + vo_rem[2].astype(F32)
            sin[...] = ((v_own[...] + v_rem[0].astype(F32)) + v_rem[1].astype(F32)) + v_rem[2].astype(F32)
            for cp in swap + mine:
                cp.start()

        @pl.when(i == NT - 1)
        def _():
            sbuf[0:8, :] = s4_r[...]
            sbuf[8:16, :] = s6_r[...]
            sloc, srem = _small_copies(sbuf, sall_o, send_sems, recv_sems, loc_sems.at[1])
            for cp in sloc + srem:
                cp.start()
            for cp in swap:
                cp.wait_recv()
            theirs = [pltpu.make_async_copy(got_in, fin_o.at[1 - c], fin_sems.at[2]),
                      pltpu.make_async_copy(got_out, fout_o.at[1 - c], fin_sems.at[3])]
            for cp in theirs:
                cp.start()
            for cp in swap:
                cp.wait_send()
            for cp in mine + theirs:
                cp.wait()
            for cp in srem:
                cp.wait_recv()
            for cp in srem:
                cp.wait_send()
            for cp in sloc:
                cp.wait()

    tok = lambda w: pl.BlockSpec((TT, w), lambda i: (i, 0))
    const = lambda shape: pl.BlockSpec(shape, lambda i: (0,) * len(shape))
    hbm = pl.BlockSpec(memory_space=pltpu.HBM)
    return pl.pallas_call(
        body, name="bwd_x", grid=(NT,),
        in_specs=[tok(NCOL // 2), tok(NCOL // 2), tok(D), tok(D), const((1, D)), const((D, NCOL)), hbm, hbm,
                  const((8, D)), const((8, D)), hbm, hbm],
        out_specs=[tok(D), hbm, hbm, hbm],
        out_shape=[jax.ShapeDtypeStruct((T, D), F32),
                   jax.ShapeDtypeStruct((8, 24, D), F32),
                   jax.ShapeDtypeStruct((2, 512, 1024), F32), jax.ShapeDtypeStruct((2, 128, D), F32)],
        scratch_shapes=[pltpu.VMEM((24, D), F32),
                        pltpu.VMEM((512, 1024), F32), pltpu.VMEM((3, 512, 1024), BF16),
                        pltpu.VMEM((128, D), F32), pltpu.VMEM((3, 128, D), BF16),
                        pltpu.VMEM((512, 1024), F32), pltpu.VMEM((128, D), F32),
                        pltpu.VMEM((512, 1024), F32), pltpu.VMEM((128, D), F32),
                        pltpu.SemaphoreType.DMA((10,)), pltpu.SemaphoreType.DMA((10,)), pltpu.SemaphoreType.DMA((2,)),
                        pltpu.SemaphoreType.DMA((2,)), pltpu.SemaphoreType.DMA((2,)), pltpu.SemaphoreType.DMA((4,))],
        compiler_params=_cp(("arbitrary",)),
    )(dproj_a, dproj_h, x, dx2, mixw, w_full, rin, rinb, small4, small6, pout_own, pout_rem)


def _grad_w_in(hn, dproj_a, dproj_h):
    TK = 2048
    NK = T // TK

    def body(hnt_r, dpa_r, dph_r, rin_o, rinb_o, acc, rbuf, obuf, obufb, send_sems, recv_sems, wb_sems):
        j = pl.program_id(0)
        kk = pl.program_id(1)
        x, y, c = lax.axis_index("x"), lax.axis_index("y"), lax.axis_index("c")
        mine = pl.ds(pl.multiple_of(c * 512, 512), 512)
        theirs = pl.ds(pl.multiple_of((1 - c) * 512, 512), 512)

        def send(jj):
            return pltpu.make_async_remote_copy(
                src_ref=acc.at[jj % 2, theirs, :], dst_ref=rbuf.at[jj], send_sem=send_sems.at[jj],
                recv_sem=recv_sems.at[jj], device_id=(x, y, 1 - c), device_id_type=MESH)

        def writeback(jj):
            cols = pl.ds(jj * 1024, 1024)
            return [pltpu.make_async_copy(obuf.at[jj % 2], rin_o.at[:, cols], wb_sems.at[jj % 2]),
                    pltpu.make_async_copy(obufb.at[jj % 2], rinb_o.at[:, cols], wb_sems.at[2 + jj % 2])]

        def wait_writeback(jj):
            for cp in writeback(jj):
                cp.wait()

        def finalize(jj):
            send(jj).wait_recv()
            red = acc[jj % 2, mine, :] + rbuf[jj]
            obuf[jj % 2] = red
            obufb[jj % 2] = red.astype(BF16)
            for cp in writeback(jj):
                cp.start()

        prod = _mm(hnt_r[...], jnp.where(j < 2, dpa_r[...], dph_r[...]))

        @pl.when(kk == 0)
        def _():
            for jj in (2, 3):
                @pl.when(j == jj)
                def _():
                    send(jj - 2).wait_send()
            acc[j % 2] = prod

        @pl.when(kk > 0)
        def _():
            acc[j % 2] += prod

        @pl.when(kk == NK - 1)
        def _():
            for jj in range(4):
                @pl.when(j == jj)
                def _():
                    send(jj).start()
                    if jj in (1, 2):
                        finalize(jj - 1)
                    if jj == 3:
                        wait_writeback(0)
                        finalize(2)
                        wait_writeback(1)
                        finalize(3)
                        wait_writeback(2)
                        wait_writeback(3)
                        send(2).wait_send()
                        send(3).wait_send()

    hbm = pl.BlockSpec(memory_space=pltpu.HBM)
    return pl.pallas_call(
        body, name="grad_w_in", grid=(4, NK),
        in_specs=[pl.BlockSpec((D, TK), lambda j, kk: (0, kk)),
                  pl.BlockSpec((TK, 1024), lambda j, kk: (jnp.where(j < 2, kk, NK - 1), jnp.minimum(j, 1))),
                  pl.BlockSpec((TK, 1024), lambda j, kk: (jnp.where(j < 2, 0, kk), jnp.maximum(j - 2, 0)))],
        out_specs=[hbm, hbm],
        out_shape=[jax.ShapeDtypeStruct((512, NCOL), F32), jax.ShapeDtypeStruct((512, NCOL), BF16)],
        scratch_shapes=[pltpu.VMEM((2, D, 1024), F32), pltpu.VMEM((4, 512, 1024), F32), pltpu.VMEM((2, 512, 1024), F32),
                        pltpu.VMEM((2, 512, 1024), BF16),
                        pltpu.SemaphoreType.DMA((4,)), pltpu.SemaphoreType.DMA((4,)), pltpu.SemaphoreType.DMA((4,))],
        compiler_params=_cp(("arbitrary", "arbitrary")),
    )(hn, dproj_a, dproj_h)


def _w_in_piece(ref, j):
    return ref.at[:, pl.ds(j * 1024, 1024)]


def _w_out_piece(ref, j):
    return ref.at[pl.ds(j * 128, 128), :]


def _chip_copies(piece, src_r, srcb_r, own_o, rem_o, send_sems, recv_sems, loc_sem):
    x, y, c = lax.axis_index("x"), lax.axis_index("y"), lax.axis_index("c")
    chips = [(1 - x, y), (x, 1 - y), (1 - x, 1 - y)]
    loc = [pltpu.make_async_copy(piece(src_r, 2 * x + y), own_o, loc_sem)]
    rem = [pltpu.make_async_remote_copy(
        src_ref=piece(srcb_r, 2 * px + py), dst_ref=rem_o.at[k], send_sem=send_sems.at[k],
        recv_sem=recv_sems.at[k], device_id=(px, py, c), device_id_type=MESH) for k, (px, py) in enumerate(chips)]
    return loc, rem


def _small_copies(small_r, sall_o, send_sems, recv_sems, loc_sem):
    x, y, c = lax.axis_index("x"), lax.axis_index("y"), lax.axis_index("c")
    me = 4 * x + 2 * y + c
    loc = [pltpu.make_async_copy(small_r, sall_o.at[me], loc_sem)]
    rem = []
    k = 3
    for fx in range(2):
        for fy in range(2):
            for fc in range(2):
                if fx or fy or fc:
                    peer = (1 - x if fx else x, 1 - y if fy else y, 1 - c if fc else c)
                    rem.append(pltpu.make_async_remote_copy(
                        src_ref=small_r, dst_ref=sall_o.at[me], send_sem=send_sems.at[k],
                        recv_sem=recv_sems.at[k], device_id=peer, device_id_type=MESH))
                    k += 1
    return loc, rem


def _adamw_math(w, g, m, v):
    m = B1 * m + (1.0 - B1) * g
    v = B2 * v + (1.0 - B2) * (g * g)
    m_hat = m / (1.0 - B1 ** STEP)
    v_hat = v / (1.0 - B2 ** STEP)
    delta = -LR * (m_hat / (jnp.sqrt(v_hat) + AEPS) + WD * w)
    return delta, m, v


def _adamw(big_in, big_out, sall, params):
    def body(*refs):
        wi, gi, mi, vi, wo, go, mo, vo, sall_r = refs[:9]
        ins = refs[9:24]
        di_o, mi_o, vi_o, do_o, mo_o, vo_o = refs[24:30]
        outs = refs[30:]
        d, mm, vv = _adamw_math(wi[...], gi[...], mi[...], vi[...])
        di_o[...] = d
        mi_o[...] = mm
        vi_o[...] = vv

        @pl.when(pl.program_id(0) == 0)
        def _():
            d, mm, vv = _adamw_math(wo[...], go[...], mo[...], vo[...])
            do_o[...] = d
            mo_o[...] = mm
            vo_o[...] = vv
            tot = sall_r[0]
            for dv in range(1, 8):
                tot = tot + sall_r[dv]
            grads = [tot[16:17, :], tot[1:2, 0:AW], tot[1:2, AW:], tot[8:10, 0:HW], tot[0:1, :]]
            outs[0][...] = tot[2:3, 0:1]
            for p in range(5):
                w_r, m_r, v_r = ins[3 * p:3 * p + 3]
                g = grads[p]
                d, mm, vv = _adamw_math(w_r[...], g, m_r[...], v_r[...])
                outs[1 + 4 * p][...] = g
                outs[2 + 4 * p][...] = d
                outs[3 + 4 * p][...] = mm
                outs[4 + 4 * p][...] = vv

    flat = [a for p in params for a in p]
    shapes = [jax.ShapeDtypeStruct((D, 1024), F32)] * 3 + [jax.ShapeDtypeStruct((256, D), F32)] * 3
    shapes += [jax.ShapeDtypeStruct((1, 1), F32)]
    for p in params:
        shapes += [jax.ShapeDtypeStruct(p[0].shape, F32)] * 4
    vm = pl.BlockSpec(memory_space=pltpu.VMEM)
    rows = pl.BlockSpec((512, 1024), lambda i: (i, 0))
    whole = pl.BlockSpec((256, D), lambda i: (0, 0))
    return pl.pallas_call(
        body, name="adamw", grid=(2,),
        in_specs=[rows] * 4 + [whole] * 4 + [vm] * 16, out_specs=[rows] * 3 + [whole] * 3 + [vm] * 21,
        out_shape=shapes,
        compiler_params=_cp(("arbitrary",)),
    )(*big_in, *big_out, sall, *flat)


def kernel(x, positions, w_in, w_out, mix_norm_w, attn_out_norm_w, hgrn_out_norm_w, hgrn_lb_raw, final_norm_w, loss_target, m_w_in, m_w_out, m_mix_norm_w, m_attn_out_norm_w, m_hgrn_out_norm_w, m_hgrn_lb_raw, m_final_norm_w, v_w_in, v_w_out, v_mix_norm_w, v_attn_out_norm_w, v_hgrn_out_norm_w, v_hgrn_lb_raw, v_final_norm_w):
    xs = x.reshape(T, D)
    tgt = loss_target.reshape(T, D)
    pos = positions.reshape(T, 1)
    fnw = final_norm_w.reshape(1, D)

    ti = np.arange(TH)
    tri_np = ((ti[:, None] // CHUNK == ti[None, :] // CHUNK) & (ti[None, :] <= ti[:, None])).astype(np.float32)
    tri = jnp.asarray(tri_np, BF16)
    trit = jnp.asarray(tri_np.T, BF16)
    hi_ = np.arange(AW) // HEAD
    gmat = jnp.asarray((hi_[:256, None] == hi_[None, :256]).astype(np.float32) / HEAD, BF16)
    emat_np = (np.arange(128)[:, None] == hi_[None, :]).astype(np.float32)
    sel_np = (8 + hi_[:, None] == np.arange(128)[None, :]).astype(np.float32)
    emat = jnp.asarray(emat_np, BF16)
    selmat = jnp.asarray(sel_np, BF16)

    jm_arr = (2 * lax.axis_index("x") + lax.axis_index("y")).astype(jnp.int32).reshape(1)
    (hn, q1, k1, v1, q4, k4, v4, q16, k16, v16, ag, hq, hf, hi, hg, w_full, wout4) = _fwd_in(
        xs, pos, mix_norm_w, w_in.reshape(D, 1024), w_out.reshape(256, D), jm_arr)
    wout_full = wout4.reshape(D, D)
    flat = lambda a: a.reshape(T, AW)
    o1, l1 = _attn_fwd(q1, k1, v1, T // BLK, "attn_fwd_d1")
    o4, l4 = _attn_fwd(flat(q4), flat(k4), flat(v4), T // 4 // BLK, "attn_fwd_d4")
    o16, l16 = _attn_fwd(flat(q16), flat(k16), flat(v16), T // 16 // BLK, "attn_fwd_d16")
    rec, sall = _hgrn_fwd(hq, hf, hi, hgrn_lb_raw, tri)

    (dx2, do1, do4, do16, st1, st4, st16, drec, dag, dhg, rout, routb, small4) = _fwd_out(
        o1, o4.reshape(4, T // 4, AW), o16.reshape(16, T // 16, AW),
        l1, l4.reshape(4, T // 4, 128), l16.reshape(16, T // 16, 128),
        rec, ag, hg, xs, tgt, attn_out_norm_w, hgrn_out_norm_w, fnw, wout_full, gmat, emat, selmat)

    fst = lambda a: a.reshape(T, 128)
    dq1, dk1, dv1 = _attn_bwd(q1, k1, v1, do1, st1, T // BLK, "attn_bwd_d1")
    dq4, dk4, dv4 = _attn_bwd(flat(q4), flat(k4), flat(v4), flat(do4), fst(st4), T // 4 // BLK, "attn_bwd_d4")
    dq16, dk16, dv16 = _attn_bwd(flat(q16), flat(k16), flat(v16), flat(do16), fst(st16), T // 16 // BLK,
                                 "attn_bwd_d16")
    dproj_h, small6, pout_own, pout_rem = _hgrn_bwd(hq, hf, hi, hgrn_lb_raw, tri, trit, drec, sall, dhg,
                                                    rout, routb)

    r4 = lambda a: a.reshape(4, T // 4, AW)
    r16 = lambda a: a.reshape(16, T // 16, AW)
    dproj_a = _dproj_build((dq1, r4(dq4), r16(dq16)), (dk1, r4(dk4), r16(dk16)), (dv1, r4(dv4), r16(dv16)),
                           dag, pos)
    rin, rinb = _grad_w_in(hn, dproj_a, dproj_h)
    gx, small_all, fin, fout = _bwd_x(dproj_a, dproj_h, xs, dx2, mix_norm_w, w_full, rin, rinb,
                                            small4, small6, pout_own, pout_rem)
    g_w_in = fin.reshape(D, 1024)
    g_w_out = fout.reshape(256, D)

    params = [(mix_norm_w, m_mix_norm_w, v_mix_norm_w),
              (attn_out_norm_w, m_attn_out_norm_w, v_attn_out_norm_w),
              (hgrn_out_norm_w, m_hgrn_out_norm_w, v_hgrn_out_norm_w),
              (hgrn_lb_raw, m_hgrn_lb_raw, v_hgrn_lb_raw),
              (fnw, m_final_norm_w.reshape(1, D), v_final_norm_w.reshape(1, D))]
    d_in, nm_in, nv_in, d_out, nm_out, nv_out, *so = _adamw(
        (w_in.reshape(D, 1024), g_w_in, m_w_in.reshape(D, 1024), v_w_in.reshape(D, 1024)),
        (w_out.reshape(256, D), g_w_out, m_w_out.reshape(256, D), v_w_out.reshape(256, D)), small_all, params)
    loss = so[0].reshape(())
    g_s = [so[1 + 4 * p] for p in range(5)]
    d_s = [so[2 + 4 * p] for p in range(5)]
    m_s = [so[3 + 4 * p] for p in range(5)]
    v_s = [so[4 + 4 * p] for p in range(5)]
    for lst in (g_s, d_s, m_s, v_s):
        lst[4] = lst[4].reshape(D)

    return (loss, gx.reshape(1, T, D),
            g_w_in.reshape(1, D, 1024), g_w_out.reshape(1, 256, D), *g_s,
            d_in.reshape(1, D, 1024), d_out.reshape(1, 256, D), *d_s,
            nm_in.reshape(1, D, 1024), nm_out.reshape(1, 256, D), *m_s,
            nv_in.reshape(1, D, 1024), nv_out.reshape(1, 256, D), *v_s)
```

```python
import functools

import numpy as np
import jax
import jax.numpy as jnp
from jax import lax
from jax.experimental import pallas as pl
from jax.experimental.pallas import tpu as pltpu

F32 = jnp.float32
BF16 = jnp.bfloat16

T = 4096
D = 1024
AW = 512
HW = 512
NCOL = 4096
HEAD = 64
BLK = 128
CHUNK = 64
EPS = 1e-6
SCALE = HEAD ** -0.5
NEG = -1e30
ROPE_THETA = 500000.0
INV_FREQ = [float(v) for v in
            (np.float32(ROPE_THETA) ** (-(np.arange(8, dtype=np.float32)) * np.float32(0.125)))]
LR, B1, B2, AEPS, WD, STEP = 0.001, 0.9, 0.999, 1e-08, 0.01, 10
VMEM_LIMIT = 63 * 1024 * 1024
MESH = pl.DeviceIdType.MESH


def _cp(sem=None, **kw):
    return pltpu.CompilerParams(dimension_semantics=sem, vmem_limit_bytes=VMEM_LIMIT, **kw)


def _mm(a, b):
    return jnp.dot(a, b, preferred_element_type=F32)


def _mm_nt(a, b):
    return lax.dot_general(a, b, (((1,), (1,)), ((), ())), preferred_element_type=F32)


def _mm_tn(a, b):
    return lax.dot_general(a, b, (((0,), (0,)), ((), ())), preferred_element_type=F32)


def _mm_exact_l(mat_bf, x):
    h = x.astype(BF16)
    l = (x - h.astype(F32)).astype(BF16)
    return _mm(mat_bf, h) + _mm(mat_bf, l)


def _mm_exact_r(x, mat_bf):
    h = x.astype(BF16)
    l = (x - h.astype(F32)).astype(BF16)
    return _mm(h, mat_bf) + _mm(l, mat_bf)


def _sigmoid(x):
    return 0.5 * jnp.tanh(0.5 * x) + 0.5


def _rope_tables(pos):
    lane = lax.broadcasted_iota(jnp.int32, (1, 128), 1)
    jl = lane & 63
    fi = jl & 7
    inv = jnp.zeros((1, 128), F32)
    for kk in range(8):
        inv = jnp.where(fi == kk, INV_FREQ[kk], inv)
    ang = jnp.broadcast_to(pos.astype(F32), (128, pos.shape[1])).T * inv
    c = jnp.cos(ang)
    s = jnp.sin(ang)
    cosf = jnp.where(jl < 16, c, 1.0)
    s1 = jnp.where(jl < 8, -s, 0.0)
    s2 = jnp.where((jl >= 8) & (jl < 16), s, 0.0)
    return cosf, s1, s2


def _rope(t, cosf, s1, s2):
    parts = []
    for ci in range(t.shape[1] // 128):
        tc = t[:, ci * 128:(ci + 1) * 128]
        parts.append(tc * cosf + pltpu.roll(tc, 120, 1) * s1 + pltpu.roll(tc, 8, 1) * s2)
    return jnp.concatenate(parts, axis=1)


def _rope_bwd(g, cosf, s1, s2):
    parts = []
    for ci in range(g.shape[1] // 128):
        gc = g[:, ci * 128:(ci + 1) * 128]
        parts.append(gc * cosf + pltpu.roll(gc * s1, 8, 1) + pltpu.roll(gc * s2, 120, 1))
    return jnp.concatenate(parts, axis=1)


def _perm_store(val, scr, scr2, o1, o4, o16, dt):
    n = val.shape[0]
    q = n // 4
    o1[...] = val.astype(dt)
    for ci in range(val.shape[1] // 128):
        cs = slice(ci * 128, (ci + 1) * 128)
        scr[ci] = val[:, cs]
        for r4 in range(4):
            part = scr[ci, pl.ds(r4, q, stride=4), :]
            o4[r4, :, cs] = part.astype(dt)
            scr2[ci, r4 * q:(r4 + 1) * q, :] = part
        for r4 in range(4):
            for b in range(4):
                o16[r4 + 4 * b, :, cs] = scr2[ci, pl.ds(r4 * q + b, q // 4, stride=4), :].astype(dt)


def _unperm_load(r4, r16, scr_a, scr_b, scr_c):
    n = scr_a.shape[1]
    q = n // 4
    nc = r4.shape[-1] // 128
    for ci in range(nc):
        cs = slice(ci * 128, (ci + 1) * 128)
        for rr in range(4):
            scr_a[ci, pl.ds(rr, q, stride=4), :] = r4[rr, :, cs].astype(F32)
        for rr in range(4):
            for b in range(4):
                scr_c[ci, pl.ds(rr * q + b, q // 4, stride=4), :] = r16[rr + 4 * b, :, cs].astype(F32)
        for rr in range(4):
            scr_b[ci, pl.ds(rr, q, stride=4), :] = scr_c[ci, rr * q:(rr + 1) * q, :]
    return (jnp.concatenate([scr_a[ci] for ci in range(nc)], axis=1),
            jnp.concatenate([scr_b[ci] for ci in range(nc)], axis=1))


def _unperm_sum(r4, r16, scr_b, scr_c):
    n = scr_b.shape[1]
    q = n // 4
    nc = r4.shape[-1] // 128
    for ci in range(nc):
        cs = slice(ci * 128, (ci + 1) * 128)
        for rr in range(4):
            for b in range(4):
                scr_c[ci, pl.ds(rr * q + b, q // 4, stride=4), :] = r16[rr + 4 * b, :, cs].astype(F32)
        for rr in range(4):
            scr_b[ci, pl.ds(rr, q, stride=4), :] = scr_c[ci, rr * q:(rr + 1) * q, :] + r4[rr, :, cs].astype(F32)
    return jnp.concatenate([scr_b[ci] for ci in range(nc)], axis=1)


def _fwd_in(x, pos, mixw, w_in, w_out, jm_arr):
    TT = 512
    NT = T // TT

    def body(jm_ref, x_ref, pos_ref, mw_ref, win_ref, wout_ref,
             hnt_ref, q1, k1, v1, q4, k4, v4, q16, k16, v16, ag, hq, hf, hi, hg, wfull_o, woutfull_o,
             wbuf, wobuf, hn_all, scr, scr2, stage, send_sems, recv_sems, loc_sems):
        s = pl.program_id(0)
        i = pl.program_id(1)
        mx, my, c = lax.axis_index("x"), lax.axis_index("y"), lax.axis_index("c")
        me, sibling = (mx, my, c), (mx, my, 1 - c)
        chips = [(mx, 1 - my), (1 - mx, my), (1 - mx, 1 - my)]
        jm = 2 * mx + my
        rows_in = [pl.ds(pl.multiple_of(h * 512, 512), 512) for h in (c, 1 - c)]
        rows_out = [pl.ds(pl.multiple_of(h * 128, 128), 128) for h in (c, 1 - c)]

        def blk(k):
            return lax.bitwise_xor(jm, k + 1)

        def rc(n, ref, to):
            return pltpu.make_async_remote_copy(src_ref=ref, dst_ref=ref, send_sem=send_sems.at[n],
                                                recv_sem=recv_sems.at[n], device_id=to, device_id_type=MESH)

        halves = [pl.ds(0, 512), pl.ds(512, 512)]
        send_in = lambda k, h: rc(12 + 2 * k + h, wbuf.at[jm, rows_in[0], halves[h]], (*chips[k], c))
        got_in = lambda k, h: rc(12 + 2 * k + h, wbuf.at[blk(k), rows_in[0], halves[h]], me)
        relay = lambda h: rc(16 + h, wbuf.at[blk(h), rows_in[0], halves[h]], (*chips[1 - h], c))
        got_relay = lambda h: rc(16 + h, wbuf.at[blk(2), rows_in[0], halves[h]], me)
        send_out = lambda k: rc(3 + k, wobuf.at[jm, rows_out[0], :], (*chips[k], c))
        got_out = lambda k: rc(3 + k, wobuf.at[blk(k), rows_out[0], :], me)
        pass_in = lambda k: rc(6 + k, wbuf.at[blk(k), rows_in[0], :], sibling)
        pass_out = lambda k: rc(9 + k, wobuf.at[blk(k), rows_out[0], :], sibling)
        passed_in = lambda k: rc(6 + k, wbuf.at[blk(k), rows_in[1], :], me)
        passed_out = lambda k: rc(9 + k, wobuf.at[blk(k), rows_out[1], :], me)

        def keep(j, n):
            return pltpu.make_async_copy(wbuf.at[j], wfull_o.at[:, pl.ds(j * 1024, 1024)], loc_sems.at[n])

        @pl.when((s == 0) & (i == 0))
        def _():
            chunk = [pl.ds(pl.multiple_of(lax.rem(p + 2 * c, 4) * 256, 256), 256) for p in range(4)]
            loads = [pltpu.make_async_copy(win_ref.at[chunk[p], :] if p < 4 else wout_ref, stage.at[p % 2],
                                           loc_sems.at[4 + p % 2]) for p in range(5)]
            loads[0].start()
            for p in range(5):
                if p < 4:
                    loads[p + 1].start()
                loads[p].wait()
                if p < 4:
                    wbuf[jm, chunk[p], :] = stage[p % 2].astype(BF16)
                else:
                    wobuf[jm] = stage[p % 2].astype(BF16)
                if p == 1:
                    for k in range(2):
                        for h in range(2):
                            send_in(k, h).start()
            keep(jm, 0).start()

        def arrive(k):
            if k == 0:
                for kk in range(2):
                    for h in range(2):
                        got_in(kk, h).wait_recv()
                relay(0).start()
                relay(1).start()
            if k == 2:
                got_relay(0).wait_recv()
                got_relay(1).wait_recv()
            pass_in(k).start()
            passed_in(k).wait_recv()
            keep(blk(k), k + 1).start()
            if k == 2:
                for kk in range(3):
                    send_out(kk).start()

        pl.when((s == 1) & (i == 0))(functools.partial(arrive, 0))

        @pl.when((s == 2) & (i == 0))
        def _():
            arrive(1)
            arrive(2)

        tile = pl.ds(pl.multiple_of(i * TT, TT), TT)

        @pl.when(s == 0)
        def _():
            xv = x_ref[...]
            r = lax.rsqrt(jnp.mean(xv * xv, axis=-1, keepdims=True) + EPS)
            hnf = (xv * r) * mw_ref[...]
            hn_all[tile, :] = hnf.astype(BF16)
            hnt_ref[...] = hnf.T.astype(BF16)

        def project(jj):
            hn = hn_all[tile, :]
            lo = _mm(hn, wbuf[jj, :, 0:512])
            hi_cols = _mm(hn, wbuf[jj, :, 512:1024])
            if jj == 0:
                cosf, s1, s2 = _rope_tables(pos_ref[...])
                _perm_store(_rope(lo, cosf, s1, s2) * SCALE, scr, scr2, q1, q4, q16, BF16)
                _perm_store(_rope(hi_cols, cosf, s1, s2), scr, scr2, k1, k4, k16, BF16)
            elif jj == 1:
                _perm_store(lo, scr, scr2, v1, v4, v16, BF16)
                ag[...] = hi_cols.astype(BF16)
            elif jj == 2:
                hq[...] = lo.astype(BF16)
                hf[...] = hi_cols.astype(BF16)
            else:
                hi[...] = lo.astype(BF16)
                hg[...] = hi_cols.astype(BF16)

        def project_block(j):
            for jj in range(4):
                pl.when(j == jj)(functools.partial(project, jj))

        @pl.when(s < 2)
        def _():
            project_block(lax.bitwise_xor(jm, s))

        @pl.when(s == 2)
        def _():
            project_block(lax.bitwise_xor(jm, 2))
            project_block(lax.bitwise_xor(jm, 3))

        @pl.when((s == 2) & (i == NT - 1))
        def _():
            for k in range(3):
                got_out(k).wait_recv()
                pass_out(k).start()
            for k in range(3):
                passed_out(k).wait_recv()
            out = pltpu.make_async_copy(wobuf, woutfull_o, loc_sems.at[4])
            out.start()
            for h in range(2):
                relay(h).wait_send()
                for k in range(2):
                    send_in(k, h).wait_send()
            for k in range(3):
                send_out(k).wait_send()
                pass_in(k).wait_send()
                pass_out(k).wait_send()
            keep(jm, 0).wait()
            for k in range(3):
                keep(blk(k), k + 1).wait()
            out.wait()

    def at_stage_of(jb):
        def index(s, i, jm_ref):
            sa = jnp.minimum(lax.bitwise_xor(jm_ref[0], jb), 2)
            return jnp.where(s < sa, 0, jnp.where(s == sa, i, NT - 1))
        return index

    tok = lambda w, jb: pl.BlockSpec((TT, w), lambda s, i, jm_ref: (at_stage_of(jb)(s, i, jm_ref), 0))
    d4 = lambda jb: pl.BlockSpec((4, TT // 4, AW), lambda s, i, jm_ref: (0, at_stage_of(jb)(s, i, jm_ref), 0))
    d16 = lambda jb: pl.BlockSpec((16, TT // 16, AW), lambda s, i, jm_ref: (0, at_stage_of(jb)(s, i, jm_ref), 0))
    hbm = pl.BlockSpec(memory_space=pltpu.HBM)
    sd = lambda shape, dt: jax.ShapeDtypeStruct(shape, dt)
    in_own_stage = lambda s, i: jnp.where(s == 0, i, NT - 1)
    grid_spec = pltpu.PrefetchScalarGridSpec(
        num_scalar_prefetch=1, grid=(3, NT),
        in_specs=[pl.BlockSpec((TT, D), lambda s, i, jm_ref: (in_own_stage(s, i), 0)),
                  pl.BlockSpec((1, TT), lambda s, i, jm_ref: (0, i)),
                  pl.BlockSpec((1, D), lambda s, i, jm_ref: (0, 0)), hbm, hbm],
        out_specs=[pl.BlockSpec((D, TT), lambda s, i, jm_ref: (0, in_own_stage(s, i))),
                   tok(AW, 0), tok(AW, 0), tok(AW, 1), d4(0), d4(0), d4(1), d16(0), d16(0), d16(1),
                   tok(AW, 1), tok(AW, 2), tok(AW, 2), tok(AW, 3), tok(AW, 3), hbm, hbm],
        scratch_shapes=[pltpu.VMEM((4, D, 1024), BF16), pltpu.VMEM((4, 256, D), BF16), pltpu.VMEM((T, D), BF16),
                        pltpu.VMEM((4, TT, 128), F32), pltpu.VMEM((4, TT, 128), F32), pltpu.VMEM((2, 256, 1024), F32),
                        pltpu.SemaphoreType.DMA((18,)),
                        pltpu.SemaphoreType.DMA((18,)), pltpu.SemaphoreType.DMA((6,))])
    return pl.pallas_call(
        body, name="fwd_in", grid_spec=grid_spec,
        out_shape=[sd((D, T), BF16)] + [sd((T, AW), BF16)] * 3 + [sd((4, T // 4, AW), BF16)] * 3
        + [sd((16, T // 16, AW), BF16)] * 3
        + [sd((T, AW), BF16)] * 5 + [sd((D, NCOL), BF16), sd((4, 256, D), BF16)],
        compiler_params=_cp(("arbitrary", "arbitrary")),
    )(jm_arr, x, pos, mixw, w_in, w_out)


def _band_mask(key_axis, nkeys=2 * BLK):
    shape = (nkeys, 2 * BLK) if key_axis == 0 else (2 * BLK, nkeys)
    kj = lax.broadcasted_iota(jnp.int32, shape, key_axis)
    qi = lax.broadcasted_iota(jnp.int32, shape, 1 - key_axis) & (BLK - 1)
    return (kj >= qi) & (kj <= qi + BLK), kj, qi


def _stack_heads(t2, in_a):
    z = jnp.zeros_like(t2)
    return jnp.concatenate([jnp.where(in_a[0], t2, z), jnp.where(in_a[1], t2, z)], axis=0)


def _attn_fwd(q, k, v, nb, name):
    n = 8
    CH = n * BLK
    halo = nb > n

    def body(*refs):
        if halo:
            q_ref, k_ref, v_ref, kp_ref, vp_ref, o_ref, lse_ref = refs
        else:
            q_ref, k_ref, v_ref, o_ref, lse_ref = refs
        lane = lax.broadcasted_iota(jnp.int32, (1, 128), 1)
        in_a = [lane < HEAD, lane >= HEAD]
        band, kj, _ = _band_mask(1)
        thr0 = jnp.where((n * pl.program_id(0)) % nb == 0, BLK, 0) if halo else BLK
        mask0 = band & (kj >= thr0)
        mask_first = band & (kj >= BLK)
        for b in range(n):
            rs = slice(b * BLK, (b + 1) * BLK)
            stat = jnp.zeros((BLK, 128), F32)
            for hp in range(4):
                cs = slice(hp * 128, (hp + 1) * 128)
                q2s = _stack_heads(q_ref[rs, cs], in_a)
                if b == 0:
                    kprev = kp_ref[:, cs] if halo else k_ref[rs, cs]
                    vprev = vp_ref[:, cs] if halo else v_ref[rs, cs]
                    kk = jnp.concatenate([kprev, k_ref[rs, cs]], axis=0)
                    vv = jnp.concatenate([vprev, v_ref[rs, cs]], axis=0)
                    mask = mask0
                else:
                    kk = k_ref[(b - 1) * BLK:(b + 1) * BLK, cs]
                    vv = v_ref[(b - 1) * BLK:(b + 1) * BLK, cs]
                    mask = mask_first if b % nb == 0 else band
                s = jnp.where(mask, _mm_nt(q2s, kk), NEG)
                m = jnp.max(s, axis=-1, keepdims=True)
                p = jnp.exp(s - m)
                l = jnp.sum(p, axis=-1, keepdims=True)
                o = _mm(p.astype(BF16), vv) / l
                lse = m + jnp.log(l)
                o_ref[rs, cs] = jnp.where(in_a[0], o[:BLK], o[BLK:]).astype(BF16)
                stat = jnp.where(lane == 2 * hp, lse[:BLK], stat)
                stat = jnp.where(lane == 2 * hp + 1, lse[BLK:], stat)
            lse_ref[rs, :] = stat

    cur = pl.BlockSpec((CH, AW), lambda i: (i, 0))
    prev = pl.BlockSpec((BLK, AW), lambda i: (jnp.maximum(n * i - 1, 0), 0))
    return pl.pallas_call(
        body, name=name, grid=(T // CH,),
        in_specs=[cur, cur, cur] + ([prev, prev] if halo else []),
        out_specs=[cur, pl.BlockSpec((CH, 128), lambda i: (i, 0))],
        out_shape=[jax.ShapeDtypeStruct((T, AW), BF16), jax.ShapeDtypeStruct((T, 128), F32)],
        compiler_params=_cp(("parallel",)),
    )(*((q, k, v) + ((k, v) if halo else ())))


def _attn_bwd(q, k, v, do, st, nb, name):
    n = 8
    CH = n * BLK
    NBLK = T // BLK
    halo = nb > n

    def body(*refs):
        if halo:
            (q_ref, k_ref, v_ref, do_ref, st_ref, kp_ref, vp_ref, qn_ref, don_ref, stn_ref,
             dq_ref, dk_ref, dv_ref) = refs
        else:
            q_ref, k_ref, v_ref, do_ref, st_ref, dq_ref, dk_ref, dv_ref = refs
        i = pl.program_id(0)
        lane = lax.broadcasted_iota(jnp.int32, (1, 128), 1)
        in_a = [lane < HEAD, lane >= HEAD]
        band, kj, _ = _band_mask(0)
        thr0 = jnp.where((n * i) % nb == 0, BLK, 0) if halo else BLK
        mask0 = band & (kj >= thr0)
        mask_first = band & (kj >= BLK)

        def stat_rows(st_t, hp):
            lse_r = jnp.concatenate([st_t[2 * hp:2 * hp + 1, :], st_t[2 * hp + 1:2 * hp + 2, :]], axis=1)
            dl_r = jnp.concatenate([st_t[8 + 2 * hp:9 + 2 * hp, :], st_t[9 + 2 * hp:10 + 2 * hp, :]], axis=1)
            return lse_r, dl_r

        st_t = [st_ref[b * BLK:(b + 1) * BLK, :].T for b in range(n)]
        if halo:
            nxt_thr = jnp.where((n * i + n) % nb == 0, 2 * BLK, 0)
            _, kj1, qi1 = _band_mask(0, BLK)
            mask_next = kj1 >= qi1 + nxt_thr
            stn_t = stn_ref[...].T

        for hp in range(4):
            cs = slice(hp * 128, (hp + 1) * 128)
            kb = [k_ref[b * BLK:(b + 1) * BLK, cs] for b in range(n)]
            vb = [v_ref[b * BLK:(b + 1) * BLK, cs] for b in range(n)]
            dk_acc = [jnp.zeros((BLK, 128), F32) for _ in range(n)]
            dv_acc = [jnp.zeros((BLK, 128), F32) for _ in range(n)]
            for b in range(n):
                rs = slice(b * BLK, (b + 1) * BLK)
                q2s = _stack_heads(q_ref[rs, cs], in_a)
                do2s = _stack_heads(do_ref[rs, cs], in_a)
                if b == 0:
                    kprev = kp_ref[:, cs] if halo else kb[0]
                    vprev = vp_ref[:, cs] if halo else vb[0]
                    mask = mask0
                else:
                    kprev, vprev, mask = kb[b - 1], vb[b - 1], (mask_first if b % nb == 0 else band)
                kk = jnp.concatenate([kprev, kb[b]], axis=0)
                vv = jnp.concatenate([vprev, vb[b]], axis=0)
                lse_r, dl_r = stat_rows(st_t[b], hp)
                s_t = jnp.where(mask, _mm_nt(kk, q2s), NEG)
                p_t = jnp.exp(s_t - lse_r)
                ds_t = (p_t * (_mm_nt(vv, do2s) - dl_r)).astype(BF16)
                dkk = _mm(ds_t, q2s)
                dvv = _mm(p_t.astype(BF16), do2s)
                dqs = _mm_tn(ds_t, kk) * SCALE
                dq_ref[rs, cs] = jnp.where(in_a[0], dqs[:BLK], dqs[BLK:]).astype(BF16)
                dk_acc[b] += dkk[BLK:]
                dv_acc[b] += dvv[BLK:]
                if b > 0:
                    dk_acc[b - 1] += dkk[:BLK]
                    dv_acc[b - 1] += dvv[:BLK]
            if halo:
                q2s = _stack_heads(qn_ref[:, cs], in_a)
                do2s = _stack_heads(don_ref[:, cs], in_a)
                lse_r, dl_r = stat_rows(stn_t, hp)
                s_t = jnp.where(mask_next, _mm_nt(kb[n - 1], q2s), NEG)
                p_t = jnp.exp(s_t - lse_r)
                ds_t = (p_t * (_mm_nt(vb[n - 1], do2s) - dl_r)).astype(BF16)
                dk_acc[n - 1] += _mm(ds_t, q2s)
                dv_acc[n - 1] += _mm(p_t.astype(BF16), do2s)
            for b in range(n):
                dk_ref[b * BLK:(b + 1) * BLK, cs] = dk_acc[b].astype(BF16)
                dv_ref[b * BLK:(b + 1) * BLK, cs] = dv_acc[b].astype(BF16)

    cur = pl.BlockSpec((CH, AW), lambda i: (i, 0))
    cur_st = pl.BlockSpec((CH, 128), lambda i: (i, 0))
    prev = pl.BlockSpec((BLK, AW), lambda i: (jnp.maximum(n * i - 1, 0), 0))
    nxt = pl.BlockSpec((BLK, AW), lambda i: (jnp.minimum(n * i + n, NBLK - 1), 0))
    nxt_st = pl.BlockSpec((BLK, 128), lambda i: (jnp.minimum(n * i + n, NBLK - 1), 0))
    ins = [cur] * 4 + [cur_st] + ([prev, prev, nxt, nxt, nxt_st] if halo else [])
    args = (q, k, v, do, st) + ((k, v, q, do, st) if halo else ())
    return pl.pallas_call(
        body, name=name, grid=(T // CH,),
        in_specs=ins,
        out_specs=[cur] * 3,
        out_shape=[jax.ShapeDtypeStruct((T, AW), BF16)] * 3,
        compiler_params=_cp(("parallel",)),
    )(*args)


TH = 256
NCH = TH // CHUNK


def _hgrn_common(hq_ref, hf_ref, lbr_ref, tri_ref):
    r0 = lbr_ref[0:1, :]
    r1 = lbr_ref[1:2, :]
    mx = jnp.maximum(r0, r1)
    e0 = jnp.exp(r0 - mx)
    e1 = jnp.exp(r1 - mx)
    lb = e0 / (e0 + e1)
    hqv = hq_ref[...].astype(F32)
    sq = _sigmoid(hqv)
    qv = hqv * sq
    sf = _sigmoid(hf_ref[...].astype(F32))
    f = lb + (1.0 - lb) * sf
    kv = 1.0 - f
    g = jnp.log(f)
    cum = _mm_exact_l(tri_ref[...], g)
    dec = jnp.exp(jnp.concatenate([cum[c * CHUNK + CHUNK - 1:(c + 1) * CHUNK, :] for c in range(NCH)], axis=0))
    decb = jnp.concatenate([jnp.broadcast_to(dec[c:c + 1, :], (CHUNK, HW)) for c in range(NCH)], axis=0)
    ea = jnp.exp(cum)
    ena = jnp.exp(-cum)
    eend = decb * ena
    return dict(lb=lb, hq=hqv, sq=sq, q=qv, sf=sf, f=f, k=kv, cum=cum, ea=ea, ena=ena, eend=eend,
                qd=qv * ea, ki=kv * ena, ke=kv * eend, dec=dec)


def _tri_mask(transposed=False):
    ti = lax.broadcasted_iota(jnp.int32, (TH, TH), 1 if transposed else 0)
    si = lax.broadcasted_iota(jnp.int32, (TH, TH), 0 if transposed else 1)
    return (si <= ti) & ((si // CHUNK) == (ti // CHUNK))


def _hgrn_fwd(hq, hf, hi, lbr, tri):
    NSUB = 2

    def body(hq_ref, hf_ref, hi_ref, lbr_ref, tri_ref, rec_ref, sall_ref, st_scr):
        @pl.when(pl.program_id(0) == 0)
        def _():
            st_scr[...] = jnp.zeros_like(st_scr)

        causal = _tri_mask()
        for u in range(NSUB):
            tile = slice(u * TH, (u + 1) * TH)
            w = _hgrn_common(hq_ref.at[tile, :], hf_ref.at[tile, :], lbr_ref, tri_ref)
            qd, ki, ke = w["qd"].astype(BF16), w["ki"].astype(BF16), w["ke"].astype(BF16)
            dec = w["dec"]
            vb = hi_ref[tile, :]
            for h in range(4):
                cs = slice(h * 128, (h + 1) * 128)
                att = jnp.where(causal, _mm_nt(qd[:, cs], ki[:, cs]), 0.0)
                o_intra = _mm(att.astype(BF16), vb[:, cs])
                st = st_scr[:, cs]
                for c in range(NCH):
                    rs = slice(c * CHUNK, (c + 1) * CHUNK)
                    sall_ref[u * NCH + c, :, cs] = st
                    rec_ref[u * TH + c * CHUNK:u * TH + (c + 1) * CHUNK, cs] = (
                        o_intra[rs] + _mm_nt(qd[rs, cs], st.astype(BF16))).astype(BF16)
                    st = dec[c:c + 1, cs] * st + _mm_tn(vb[rs, cs], ke[rs, cs])
                st_scr[:, cs] = st

    tok = pl.BlockSpec((NSUB * TH, HW), lambda i: (i, 0))
    return pl.pallas_call(
        body, name="hgrn_fwd", grid=(T // (NSUB * TH),),
        in_specs=[tok, tok, tok, pl.BlockSpec((2, HW), lambda i: (0, 0)), pl.BlockSpec((TH, TH), lambda i: (0, 0))],
        out_specs=[tok, pl.BlockSpec((NSUB * NCH, 128, HW), lambda i: (i, 0, 0))],
        out_shape=[jax.ShapeDtypeStruct((T, HW), BF16), jax.ShapeDtypeStruct((T // CHUNK, 128, HW), F32)],
        scratch_shapes=[pltpu.VMEM((128, HW), F32)],
        compiler_params=_cp(("arbitrary",)),
    )(hq, hf, hi, lbr, tri)


def _hgrn_bwd(hq, hf, hi, lbr, tri, trit, drec, sall, dhg, rout, routb):
    NSUB = 2
    NT = T // (NSUB * TH)

    def body(hq_ref, hf_ref, hi_ref, lbr_ref, tri_ref, trit_ref, do_ref, sall_ref, dhg_ref, rout_r, routb_r,
             dph_ref, small_ref, pout_o, poutr_o,
             dst_scr, dlb_scr, dqd_scr, dki_scr, dke_scr, dlast_scr, send_sems, recv_sems, loc_sems):
        step = pl.program_id(0)
        loc, rem = _chip_copies(_w_out_piece, rout_r, routb_r, pout_o, poutr_o, send_sems, recv_sems,
                                loc_sems.at[0])

        @pl.when(step == 0)
        def _():
            dst_scr[...] = jnp.zeros_like(dst_scr)
            dlb_scr[...] = jnp.zeros_like(dlb_scr)
            for cp in loc + rem:
                cp.start()

        causal = _tri_mask()
        causal_t = _tri_mask(transposed=True)
        lb = None
        for u in reversed(range(NSUB)):
            tile = slice(u * TH, (u + 1) * TH)
            w = _hgrn_common(hq_ref.at[tile, :], hf_ref.at[tile, :], lbr_ref, tri_ref)
            qd, ki, ke = w["qd"].astype(BF16), w["ki"].astype(BF16), w["ke"].astype(BF16)
            dec = w["dec"]
            vb = hi_ref[tile, :]
            dob = do_ref[tile, :].astype(BF16)
            for h in range(4):
                cs = slice(h * 128, (h + 1) * 128)
                att_t = jnp.where(causal_t, _mm_nt(ki[:, cs], qd[:, cs]), 0.0).astype(BF16)
                datt_t = jnp.where(causal_t, _mm_nt(vb[:, cs], dob[:, cs]), 0.0).astype(BF16)
                datt = jnp.where(causal, _mm_nt(dob[:, cs], vb[:, cs]), 0.0).astype(BF16)
                dv_intra = _mm(att_t, dob[:, cs])
                dqd_intra = _mm(datt, ki[:, cs])
                dki_scr[u, :, cs] = _mm(datt_t, qd[:, cs])
                dst = dst_scr[:, cs]
                for c in reversed(range(NCH)):
                    rs = slice(c * CHUNK, (c + 1) * CHUNK)
                    dec_c = dec[c:c + 1, :]
                    st = sall_ref[u * NCH + c, :, cs]
                    dstb = dst.astype(BF16)
                    dph_ref[u * TH + c * CHUNK:u * TH + (c + 1) * CHUNK, 2 * HW + h * 128:2 * HW + (h + 1) * 128] = (
                        dv_intra[rs] + _mm_nt(ke[rs, cs], dstb)).astype(BF16)
                    dqd_scr[u, rs, cs] = dqd_intra[rs] + _mm(dob[rs, cs], st.astype(BF16))
                    dke_scr[u, rs, cs] = _mm(vb[rs, cs], dstb)
                    ddec = jnp.sum(dst * st, axis=0, keepdims=True)
                    dlast_scr[u, c:c + 1, cs] = ddec * dec_c[:, cs]
                    dst = dec_c[:, cs] * dst + _mm_tn(dob[rs, cs], qd[rs, cs])
                dst_scr[:, cs] = dst
            dqd, dki, dke = dqd_scr[u], dki_scr[u], dke_scr[u]
            dq = dqd * w["ea"]
            dk = dki * w["ena"] + dke * w["eend"]
            dcum = dqd * w["qd"] - dki * w["ki"] - dke * w["ke"]
            dkeke = dke * w["ke"]
            dlastb = jnp.concatenate(
                [jnp.broadcast_to(dlast_scr[u, c:c + 1, :]
                                  + jnp.sum(dkeke[c * CHUNK:(c + 1) * CHUNK], axis=0, keepdims=True), (CHUNK, HW))
                 for c in range(NCH)], axis=0)
            dg = _mm_exact_l(trit_ref[...], dcum) + dlastb
            df = dg / w["f"] - dk
            lb, sf, sq = w["lb"], w["sf"], w["sq"]
            dph_ref[tile, HW:2 * HW] = (df * (1.0 - lb) * sf * (1.0 - sf)).astype(BF16)
            dph_ref[tile, 0:HW] = (dq * (sq * (1.0 + w["hq"] * (1.0 - sq)))).astype(BF16)
            dph_ref[tile, 3 * HW:4 * HW] = dhg_ref[tile, :]
            dlb_scr[...] += jnp.sum(df * (1.0 - sf), axis=0, keepdims=True)

        @pl.when(step == NT - 1)
        def _():
            gr = dlb_scr[...] * lb * (1.0 - lb)
            small_ref[...] = jnp.zeros_like(small_ref)
            small_ref[0:1, 0:HW] = gr
            small_ref[1:2, 0:HW] = -gr
            for cp in rem:
                cp.wait_recv()
            for cp in rem:
                cp.wait_send()
            for cp in loc:
                cp.wait()

    tok = pl.BlockSpec((NSUB * TH, HW), lambda i: (NT - 1 - i, 0))
    const = lambda shape: pl.BlockSpec(shape, lambda i: (0,) * len(shape))
    hbm = pl.BlockSpec(memory_space=pltpu.HBM)
    return pl.pallas_call(
        body, name="hgrn_bwd", grid=(NT,),
        in_specs=[tok, tok, tok, const((2, HW)), const((TH, TH)), const((TH, TH)), tok,
                  pl.BlockSpec((NSUB * NCH, 128, HW), lambda i: (NT - 1 - i, 0, 0)), tok, hbm, hbm],
        out_specs=[pl.BlockSpec((NSUB * TH, NCOL // 2), lambda i: (NT - 1 - i, 0)), const((8, D)), hbm, hbm],
        out_shape=[jax.ShapeDtypeStruct((T, NCOL // 2), BF16), jax.ShapeDtypeStruct((8, D), F32),
                   jax.ShapeDtypeStruct((128, D), F32), jax.ShapeDtypeStruct((3, 128, D), BF16)],
        scratch_shapes=[pltpu.VMEM((128, HW), F32), pltpu.VMEM((1, HW), F32), pltpu.VMEM((NSUB, TH, HW), F32),
                        pltpu.VMEM((NSUB, TH, HW), F32), pltpu.VMEM((NSUB, TH, HW), F32),
                        pltpu.VMEM((NSUB, 8, HW), F32),
                        pltpu.SemaphoreType.DMA((3,)), pltpu.SemaphoreType.DMA((3,)), pltpu.SemaphoreType.DMA((1,))],
        compiler_params=_cp(("arbitrary",)),
    )(hq, hf, hi, lbr, tri, trit, drec, sall, dhg, rout, routb)


def _fwd_out(o1, o4, o16, l1, l4, l16, rec, ag, hg, x, tgt, anw, hnw, fnw, wout_full, gmat, emat, selmat):
    TT = 512

    def body(o1_r, o4_r, o16_r, l1_r, l4_r, l16_r, rec_r, ag_r, hg_r, x_r, tgt_r, anw_r, hnw_r, fnw_r, wo_r, g_r,
             e_r, sel_r, dx2_o, do1_o, do4_o, do16_o, st1_o, st4_o, st16_o, drec_o, dag_o, dhg_o,
             rout_o, routb_o, small_o, scr_a, scr_b, scr_c, gwout_o, rbuf, send_sems, recv_sems):
        @pl.when(pl.program_id(0) == 0)
        def _():
            gwout_o[...] = jnp.zeros_like(gwout_o)
            small_o[...] = jnp.zeros_like(small_o)

        def unperm(r4, r16):
            return _unperm_load(r4, r16, scr_a, scr_b, scr_c)

        def perm_out(val, p1, p4, p16, dt):
            _perm_store(val, scr_a, scr_b, p1, p4, p16, dt)

        o4u, o16u = unperm(o4_r, o16_r)
        l4c, l16c = unperm(l4_r, l16_r)
        l1c = l1_r[...]
        mxc = jnp.maximum(jnp.maximum(l1c, l4c), l16c)
        w1c, w4c, w16c = jnp.exp(l1c - mxc), jnp.exp(l4c - mxc), jnp.exp(l16c - mxc)
        denc = w1c + w4c + w16c
        lane = lax.broadcasted_iota(jnp.int32, (1, 128), 1)
        lse_c = jnp.where(lane < 8, mxc + jnp.log(denc), 0.0)
        em = e_r[...]
        wn1 = _mm_exact_r(w1c / denc, em)
        wn4 = _mm_exact_r(w4c / denc, em)
        o1v = o1_r[...].astype(F32)
        attn = wn1 * o1v + wn4 * o4u + (1.0 - wn1 - wn4) * o16u
        gm = g_r[...]

        def head_mean_a(t):
            return jnp.concatenate([_mm_exact_r(t[:, :256], gm), _mm_exact_r(t[:, 256:], gm)], axis=1)

        def head_mean_h(t):
            return jnp.concatenate(
                [jnp.broadcast_to(jnp.mean(t[:, h * 128:(h + 1) * 128], axis=-1, keepdims=True), (TT, 128))
                 for h in range(4)], axis=1)

        rs_a = lax.rsqrt(head_mean_a(attn * attn) + EPS)
        n_a = attn * rs_a
        agv = ag_r[...].astype(F32)
        sg_a = _sigmoid(agv)
        si_a = agv * sg_a
        anw_v = anw_r[...]
        y_a = (n_a * anw_v) * si_a
        recv = rec_r[...].astype(F32)
        rs_h = lax.rsqrt(head_mean_h(recv * recv) + EPS)
        n_h = recv * rs_h
        hgv = hg_r[...].astype(F32)
        sg_h = _sigmoid(hgv)
        si_h = hgv * sg_h
        hnw_v = hnw_r[...]
        y_h = (n_h * hnw_v) * si_h
        mixed = jnp.concatenate([y_a, y_h], axis=1).astype(BF16)
        xv = x_r[...]
        x2 = xv + _mm(mixed, wo_r[...])
        r2 = lax.rsqrt(jnp.mean(x2 * x2, axis=-1, keepdims=True) + EPS)
        fnw_v = fnw_r[...]
        xn = x2 * r2
        err = xn * fnw_v - tgt_r[...]
        small_o[2:3, :] += 0.5 * jnp.sum(jnp.mean(err * err, axis=-1, keepdims=True), axis=0, keepdims=True)
        small_o[0:1, :] += jnp.sum(err * xn, axis=0, keepdims=True) * (1.0 / D)
        dyw = err * (fnw_v * (1.0 / D))
        dx2 = r2 * dyw - x2 * ((r2 * r2 * r2) * jnp.mean(dyw * x2, axis=-1, keepdims=True))
        dx2_o[...] = dx2
        dx2b = dx2.astype(BF16)
        gwout_o[...] += _mm_tn(mixed, dx2b)
        dmix = _mm_nt(dx2b, wo_r[...])
        dm_a, dm_h = dmix[:, :AW], dmix[:, AW:]
        dag_o[...] = (dm_a * (n_a * anw_v) * (sg_a * (1.0 + agv * (1.0 - sg_a)))).astype(BF16)
        dy_a = dm_a * si_a
        dn_a = dy_a * anw_v
        small_o[1:2, 0:AW] += jnp.sum(dy_a * n_a, axis=0, keepdims=True)
        dattn = rs_a * (dn_a - n_a * head_mean_a(dn_a * n_a))
        perm_out(dattn, do1_o, do4_o, do16_o, BF16)
        stats = lse_c + _mm_exact_r(dattn * attn, sel_r[...])
        perm_out(stats, st1_o, st4_o, st16_o, F32)
        dhg_o[...] = (dm_h * (n_h * hnw_v) * (sg_h * (1.0 + hgv * (1.0 - sg_h)))).astype(BF16)
        dy_h = dm_h * si_h
        dn_h = dy_h * hnw_v
        small_o[1:2, AW:] += jnp.sum(dy_h * n_h, axis=0, keepdims=True)
        drec_o[...] = (rs_h * (dn_h - n_h * head_mean_h(dn_h * n_h))).astype(BF16)

        @pl.when(pl.program_id(0) == T // TT - 1)
        def _():
            x, y, c = lax.axis_index("x"), lax.axis_index("y"), lax.axis_index("c")
            cps = [pltpu.make_async_remote_copy(
                src_ref=gwout_o.at[pl.ds(pl.multiple_of(j * 256 + (1 - c) * 128, 128), 128), :], dst_ref=rbuf.at[j],
                send_sem=send_sems.at[j], recv_sem=recv_sems.at[j], device_id=(x, y, 1 - c), device_id_type=MESH)
                for j in range(4)]
            for cp in cps:
                cp.start()
            for j, cp in enumerate(cps):
                cp.wait_recv()
                red = gwout_o[pl.ds(pl.multiple_of(j * 256 + c * 128, 128), 128), :] + rbuf[j]
                rout_o[j * 128:(j + 1) * 128, :] = red
                routb_o[j * 128:(j + 1) * 128, :] = red.astype(BF16)
            for cp in cps:
                cp.wait_send()

    tok = lambda w: pl.BlockSpec((TT, w), lambda i: (i, 0))
    d4 = pl.BlockSpec((4, TT // 4, AW), lambda i: (0, i, 0))
    d16 = pl.BlockSpec((16, TT // 16, AW), lambda i: (0, i, 0))
    const = lambda shape: pl.BlockSpec(shape, lambda i: (0,) * len(shape))
    sd = lambda shape, dt: jax.ShapeDtypeStruct(shape, dt)
    c4 = pl.BlockSpec((4, TT // 4, 128), lambda i: (0, i, 0))
    c16 = pl.BlockSpec((16, TT // 16, 128), lambda i: (0, i, 0))
    p3 = lambda w, dt: [sd((T, w), dt), sd((4, T // 4, w), dt), sd((16, T // 16, w), dt)]
    return pl.pallas_call(
        body, name="fwd_out", grid=(T // TT,),
        in_specs=[tok(AW), d4, d16, tok(128), c4, c16, tok(AW), tok(AW), tok(AW), tok(D), tok(D),
                  const((1, AW)), const((1, HW)), const((1, D)), const((D, D)), const((256, 256)),
                  const((128, AW)), const((AW, 128))],
        out_specs=[tok(D)] + [tok(AW), d4, d16] + [tok(128), c4, c16] + [tok(AW)] * 3
        + [const((512, D)), const((512, D)), const((8, D))],
        out_shape=[sd((T, D), F32)] + p3(AW, BF16) + p3(128, F32)
        + [sd((T, AW), BF16), sd((T, AW), BF16), sd((T, AW), BF16), sd((512, D), F32), sd((512, D), BF16),
           sd((8, D), F32)],
        scratch_shapes=[pltpu.VMEM((4, TT, 128), F32)] * 3 + [pltpu.VMEM((D, D), F32),
                        pltpu.VMEM((4, 128, D), F32), pltpu.SemaphoreType.DMA((4,)), pltpu.SemaphoreType.DMA((4,))],
        compiler_params=_cp(("arbitrary",)),
    )(o1, o4, o16, l1, l4, l16, rec, ag, hg, x, tgt, anw, hnw, fnw, wout_full, gmat, emat, selmat)


def _dproj_build(dq, dk, dv, dag, pos):
    TT = 512

    def body(dq1, dq4, dq16, dk1, dk4, dk16, dv1, dv4, dv16, dag_r, pos_r, dproj_o, scr_b, scr_c):
        def unperm_sum(r1, r4, r16):
            return r1[...] + _unperm_sum(r4, r16, scr_b, scr_c)

        cosf, s1, s2 = _rope_tables(pos_r[...])
        dproj_o[:, 0:512] = _rope_bwd(unperm_sum(dq1, dq4, dq16), cosf, s1, s2).astype(BF16)
        dproj_o[:, 512:1024] = _rope_bwd(unperm_sum(dk1, dk4, dk16), cosf, s1, s2).astype(BF16)
        dproj_o[:, 1024:1536] = unperm_sum(dv1, dv4, dv16).astype(BF16)
        dproj_o[:, 1536:2048] = dag_r[...]

    tok = lambda w: pl.BlockSpec((TT, w), lambda i: (i, 0))
    d4 = pl.BlockSpec((4, TT // 4, AW), lambda i: (0, i, 0))
    d16 = pl.BlockSpec((16, TT // 16, AW), lambda i: (0, i, 0))
    return pl.pallas_call(
        body, name="dproj_build", grid=(T // TT,),
        in_specs=[tok(AW), d4, d16] * 3 + [tok(AW), pl.BlockSpec((1, TT), lambda i: (0, i))],
        out_specs=tok(NCOL // 2),
        out_shape=jax.ShapeDtypeStruct((T, NCOL // 2), BF16),
        scratch_shapes=[pltpu.VMEM((4, TT, 128), F32)] * 2,
        compiler_params=_cp(("parallel",)),
    )(*dq, *dk, *dv, dag, pos)


def _bwd_x(dproj_a, dproj_h, x, dx2, mixw, w_full, rin, rinb, small4, small6, pout_own, pout_rem):
    TT = 256
    NT = T // TT

    def body(dpa_r, dph_r, x_r, dx2_r, mw_r, w_r, rin_r, rinb_r, s4_r, s6_r, poo_r, por_r,
             gx_o, sall_o, fin_o, fout_o, sbuf, v_own, v_rem, vo_own, vo_rem, sin, sout, got_in,
             got_out, send_sems, recv_sems, loc_sems, share_send, share_recv, fin_sems):
        i = pl.program_id(0)
        loc, rem = _chip_copies(_w_in_piece, rin_r, rinb_r, v_own, v_rem, send_sems, recv_sems, loc_sems.at[0])
        loads = [pltpu.make_async_copy(poo_r, vo_own, fin_sems.at[2]),
                 pltpu.make_async_copy(por_r, vo_rem, fin_sems.at[3])]

        @pl.when(i == 0)
        def _():
            sbuf[...] = jnp.zeros_like(sbuf)
            for cp in loc + rem + loads:
                cp.start()

        dhn = _mm_nt(dpa_r[...], w_r[:, 0:NCOL // 2]) + _mm_nt(dph_r[...], w_r[:, NCOL // 2:NCOL])
        xv = x_r[...]
        r = lax.rsqrt(jnp.mean(xv * xv, axis=-1, keepdims=True) + EPS)
        dxw = dhn * mw_r[...]
        gx_o[...] = dx2_r[...] + r * dxw - xv * ((r * r * r) * jnp.mean(dxw * xv, axis=-1, keepdims=True))
        sbuf[16:17, :] += jnp.sum(dhn * (xv * r), axis=0, keepdims=True)

        @pl.when(i == NT - 1)
        def _():
            sbuf[0:8, :] = s4_r[...]
            sbuf[8:16, :] = s6_r[...]
            sloc, srem = _small_copies(sbuf, sall_o, send_sems, recv_sems, loc_sems.at[1])
            for cp in sloc + srem:
                cp.start()
            for cp in rem:
                cp.wait_recv()
            for cp in rem:
                cp.wait_send()
            for cp in loc:
                cp.wait()
            mx, my, c = lax.axis_index("x"), lax.axis_index("y"), lax.axis_index("c")
            for cp in loads:
                cp.wait()
            sout[...] = ((vo_own[...] + vo_rem[0].astype(F32)) + vo_rem[1].astype(F32)) + vo_rem[2].astype(F32)
            sin[...] = ((v_own[...] + v_rem[0].astype(F32)) + v_rem[1].astype(F32)) + v_rem[2].astype(F32)
            swap = [pltpu.make_async_remote_copy(src_ref=sin, dst_ref=got_in, send_sem=share_send.at[0],
                                                 recv_sem=share_recv.at[0], device_id=(mx, my, 1 - c),
                                                 device_id_type=MESH),
                    pltpu.make_async_remote_copy(src_ref=sout, dst_ref=got_out, send_sem=share_send.at[1],
                                                 recv_sem=share_recv.at[1], device_id=(mx, my, 1 - c),
                                                 device_id_type=MESH)]
            for cp in swap:
                cp.start()
            mine = [pltpu.make_async_copy(sin, fin_o.at[c], fin_sems.at[0]),
                    pltpu.make_async_copy(sout, fout_o.at[c], fin_sems.at[1])]
            for cp in mine:
                cp.start()
            for cp in swap:
                cp.wait_recv()
            theirs = [pltpu.make_async_copy(got_in, fin_o.at[1 - c], fin_sems.at[2]),
                      pltpu.make_async_copy(got_out, fout_o.at[1 - c], fin_sems.at[3])]
            for cp in theirs:
                cp.start()
            for cp in swap:
                cp.wait_send()
            for cp in mine + theirs:
                cp.wait()
            for cp in srem:
                cp.wait_recv()
            for cp in srem:
                cp.wait_send()
            for cp in sloc:
                cp.wait()

    tok = lambda w: pl.BlockSpec((TT, w), lambda i: (i, 0))
    const = lambda shape: pl.BlockSpec(shape, lambda i: (0,) * len(shape))
    hbm = pl.BlockSpec(memory_space=pltpu.HBM)
    return pl.pallas_call(
        body, name="bwd_x", grid=(NT,),
        in_specs=[tok(NCOL // 2), tok(NCOL // 2), tok(D), tok(D), const((1, D)), const((D, NCOL)), hbm, hbm,
                  const((8, D)), const((8, D)), hbm, hbm],
        out_specs=[tok(D), hbm, hbm, hbm],
        out_shape=[jax.ShapeDtypeStruct((T, D), F32),
                   jax.ShapeDtypeStruct((8, 24, D), F32),
                   jax.ShapeDtypeStruct((2, 512, 1024), F32), jax.ShapeDtypeStruct((2, 128, D), F32)],
        scratch_shapes=[pltpu.VMEM((24, D), F32),
                        pltpu.VMEM((512, 1024), F32), pltpu.VMEM((3, 512, 1024), BF16),
                        pltpu.VMEM((128, D), F32), pltpu.VMEM((3, 128, D), BF16),
                        pltpu.VMEM((512, 1024), F32), pltpu.VMEM((128, D), F32),
                        pltpu.VMEM((512, 1024), F32), pltpu.VMEM((128, D), F32),
                        pltpu.SemaphoreType.DMA((10,)), pltpu.SemaphoreType.DMA((10,)), pltpu.SemaphoreType.DMA((2,)),
                        pltpu.SemaphoreType.DMA((2,)), pltpu.SemaphoreType.DMA((2,)), pltpu.SemaphoreType.DMA((4,))],
        compiler_params=_cp(("arbitrary",)),
    )(dproj_a, dproj_h, x, dx2, mixw, w_full, rin, rinb, small4, small6, pout_own, pout_rem)


def _grad_w_in(hn, dproj_a, dproj_h):
    TK = 2048
    NK = T // TK

    def body(hnt_r, dpa_r, dph_r, rin_o, rinb_o, acc, rbuf, obuf, obufb, send_sems, recv_sems, wb_sems):
        j = pl.program_id(0)
        kk = pl.program_id(1)
        x, y, c = lax.axis_index("x"), lax.axis_index("y"), lax.axis_index("c")
        mine = pl.ds(pl.multiple_of(c * 512, 512), 512)
        theirs = pl.ds(pl.multiple_of((1 - c) * 512, 512), 512)

        def send(jj):
            return pltpu.make_async_remote_copy(
                src_ref=acc.at[jj % 2, theirs, :], dst_ref=rbuf.at[jj], send_sem=send_sems.at[jj],
                recv_sem=recv_sems.at[jj], device_id=(x, y, 1 - c), device_id_type=MESH)

        def writeback(jj):
            cols = pl.ds(jj * 1024, 1024)
            return [pltpu.make_async_copy(obuf.at[jj % 2], rin_o.at[:, cols], wb_sems.at[jj % 2]),
                    pltpu.make_async_copy(obufb.at[jj % 2], rinb_o.at[:, cols], wb_sems.at[2 + jj % 2])]

        def wait_writeback(jj):
            for cp in writeback(jj):
                cp.wait()

        def finalize(jj):
            send(jj).wait_recv()
            red = acc[jj % 2, mine, :] + rbuf[jj]
            obuf[jj % 2] = red
            obufb[jj % 2] = red.astype(BF16)
            for cp in writeback(jj):
                cp.start()

        prod = _mm(hnt_r[...], jnp.where(j < 2, dpa_r[...], dph_r[...]))

        @pl.when(kk == 0)
        def _():
            for jj in (2, 3):
                @pl.when(j == jj)
                def _():
                    send(jj - 2).wait_send()
            acc[j % 2] = prod

        @pl.when(kk > 0)
        def _():
            acc[j % 2] += prod

        @pl.when(kk == NK - 1)
        def _():
            for jj in range(4):
                @pl.when(j == jj)
                def _():
                    send(jj).start()
                    if jj in (1, 2):
                        finalize(jj - 1)
                    if jj == 3:
                        wait_writeback(0)
                        finalize(2)
                        wait_writeback(1)
                        finalize(3)
                        wait_writeback(2)
                        wait_writeback(3)
                        send(2).wait_send()
                        send(3).wait_send()

    hbm = pl.BlockSpec(memory_space=pltpu.HBM)
    return pl.pallas_call(
        body, name="grad_w_in", grid=(4, NK),
        in_specs=[pl.BlockSpec((D, TK), lambda j, kk: (0, kk)),
                  pl.BlockSpec((TK, 1024), lambda j, kk: (jnp.where(j < 2, kk, NK - 1), jnp.minimum(j, 1))),
                  pl.BlockSpec((TK, 1024), lambda j, kk: (jnp.where(j < 2, 0, kk), jnp.maximum(j - 2, 0)))],
        out_specs=[hbm, hbm],
        out_shape=[jax.ShapeDtypeStruct((512, NCOL), F32), jax.ShapeDtypeStruct((512, NCOL), BF16)],
        scratch_shapes=[pltpu.VMEM((2, D, 1024), F32), pltpu.VMEM((4, 512, 1024), F32), pltpu.VMEM((2, 512, 1024), F32),
                        pltpu.VMEM((2, 512, 1024), BF16),
                        pltpu.SemaphoreType.DMA((4,)), pltpu.SemaphoreType.DMA((4,)), pltpu.SemaphoreType.DMA((4,))],
        compiler_params=_cp(("arbitrary", "arbitrary")),
    )(hn, dproj_a, dproj_h)


def _w_in_piece(ref, j):
    return ref.at[:, pl.ds(j * 1024, 1024)]


def _w_out_piece(ref, j):
    return ref.at[pl.ds(j * 128, 128), :]


def _chip_copies(piece, src_r, srcb_r, own_o, rem_o, send_sems, recv_sems, loc_sem):
    x, y, c = lax.axis_index("x"), lax.axis_index("y"), lax.axis_index("c")
    chips = [(1 - x, y), (x, 1 - y), (1 - x, 1 - y)]
    loc = [pltpu.make_async_copy(piece(src_r, 2 * x + y), own_o, loc_sem)]
    rem = [pltpu.make_async_remote_copy(
        src_ref=piece(srcb_r, 2 * px + py), dst_ref=rem_o.at[k], send_sem=send_sems.at[k],
        recv_sem=recv_sems.at[k], device_id=(px, py, c), device_id_type=MESH) for k, (px, py) in enumerate(chips)]
    return loc, rem


def _small_copies(small_r, sall_o, send_sems, recv_sems, loc_sem):
    x, y, c = lax.axis_index("x"), lax.axis_index("y"), lax.axis_index("c")
    me = 4 * x + 2 * y + c
    loc = [pltpu.make_async_copy(small_r, sall_o.at[me], loc_sem)]
    rem = []
    k = 3
    for fx in range(2):
        for fy in range(2):
            for fc in range(2):
                if fx or fy or fc:
                    peer = (1 - x if fx else x, 1 - y if fy else y, 1 - c if fc else c)
                    rem.append(pltpu.make_async_remote_copy(
                        src_ref=small_r, dst_ref=sall_o.at[me], send_sem=send_sems.at[k],
                        recv_sem=recv_sems.at[k], device_id=peer, device_id_type=MESH))
                    k += 1
    return loc, rem


def _adamw_math(w, g, m, v):
    m = B1 * m + (1.0 - B1) * g
    v = B2 * v + (1.0 - B2) * (g * g)
    m_hat = m / (1.0 - B1 ** STEP)
    v_hat = v / (1.0 - B2 ** STEP)
    delta = -LR * (m_hat / (jnp.sqrt(v_hat) + AEPS) + WD * w)
    return delta, m, v


def _adamw(big_in, big_out, sall, params):
    def body(*refs):
        wi, gi, mi, vi, wo, go, mo, vo, sall_r = refs[:9]
        ins = refs[9:24]
        di_o, mi_o, vi_o, do_o, mo_o, vo_o = refs[24:30]
        outs = refs[30:]
        d, mm, vv = _adamw_math(wi[...], gi[...], mi[...], vi[...])
        di_o[...] = d
        mi_o[...] = mm
        vi_o[...] = vv

        @pl.when(pl.program_id(0) == 0)
        def _():
            d, mm, vv = _adamw_math(wo[...], go[...], mo[...], vo[...])
            do_o[...] = d
            mo_o[...] = mm
            vo_o[...] = vv
            tot = sall_r[0]
            for dv in range(1, 8):
                tot = tot + sall_r[dv]
            grads = [tot[16:17, :], tot[1:2, 0:AW], tot[1:2, AW:], tot[8:10, 0:HW], tot[0:1, :]]
            outs[0][...] = tot[2:3, 0:1]
            for p in range(5):
                w_r, m_r, v_r = ins[3 * p:3 * p + 3]
                g = grads[p]
                d, mm, vv = _adamw_math(w_r[...], g, m_r[...], v_r[...])
                outs[1 + 4 * p][...] = g
                outs[2 + 4 * p][...] = d
                outs[3 + 4 * p][...] = mm
                outs[4 + 4 * p][...] = vv

    flat = [a for p in params for a in p]
    shapes = [jax.ShapeDtypeStruct((D, 1024), F32)] * 3 + [jax.ShapeDtypeStruct((256, D), F32)] * 3
    shapes += [jax.ShapeDtypeStruct((1, 1), F32)]
    for p in params:
        shapes += [jax.ShapeDtypeStruct(p[0].shape, F32)] * 4
    vm = pl.BlockSpec(memory_space=pltpu.VMEM)
    rows = pl.BlockSpec((512, 1024), lambda i: (i, 0))
    whole = pl.BlockSpec((256, D), lambda i: (0, 0))
    return pl.pallas_call(
        body, name="adamw", grid=(2,),
        in_specs=[rows] * 4 + [whole] * 4 + [vm] * 16, out_specs=[rows] * 3 + [whole] * 3 + [vm] * 21,
        out_shape=shapes,
        compiler_params=_cp(("arbitrary",)),
    )(*big_in, *big_out, sall, *flat)


def kernel(x, positions, w_in, w_out, mix_norm_w, attn_out_norm_w, hgrn_out_norm_w, hgrn_lb_raw, final_norm_w, loss_target, m_w_in, m_w_out, m_mix_norm_w, m_attn_out_norm_w, m_hgrn_out_norm_w, m_hgrn_lb_raw, m_final_norm_w, v_w_in, v_w_out, v_mix_norm_w, v_attn_out_norm_w, v_hgrn_out_norm_w, v_hgrn_lb_raw, v_final_norm_w):
    xs = x.reshape(T, D)
    tgt = loss_target.reshape(T, D)
    pos = positions.reshape(1, T)
    fnw = final_norm_w.reshape(1, D)

    ti = np.arange(TH)
    tri_np = ((ti[:, None] // CHUNK == ti[None, :] // CHUNK) & (ti[None, :] <= ti[:, None])).astype(np.float32)
    tri = jnp.asarray(tri_np, BF16)
    trit = jnp.asarray(tri_np.T, BF16)
    hi_ = np.arange(AW) // HEAD
    gmat = jnp.asarray((hi_[:256, None] == hi_[None, :256]).astype(np.float32) / HEAD, BF16)
    emat_np = (np.arange(128)[:, None] == hi_[None, :]).astype(np.float32)
    sel_np = (8 + hi_[:, None] == np.arange(128)[None, :]).astype(np.float32)
    emat = jnp.asarray(emat_np, BF16)
    selmat = jnp.asarray(sel_np, BF16)

    jm_arr = (2 * lax.axis_index("x") + lax.axis_index("y")).astype(jnp.int32).reshape(1)
    (hn, q1, k1, v1, q4, k4, v4, q16, k16, v16, ag, hq, hf, hi, hg, w_full, wout4) = _fwd_in(
        xs, pos, mix_norm_w, w_in.reshape(D, 1024), w_out.reshape(256, D), jm_arr)
    wout_full = wout4.reshape(D, D)
    flat = lambda a: a.reshape(T, AW)
    o1, l1 = _attn_fwd(q1, k1, v1, T // BLK, "attn_fwd_d1")
    o4, l4 = _attn_fwd(flat(q4), flat(k4), flat(v4), T // 4 // BLK, "attn_fwd_d4")
    o16, l16 = _attn_fwd(flat(q16), flat(k16), flat(v16), T // 16 // BLK, "attn_fwd_d16")
    rec, sall = _hgrn_fwd(hq, hf, hi, hgrn_lb_raw, tri)

    (dx2, do1, do4, do16, st1, st4, st16, drec, dag, dhg, rout, routb, small4) = _fwd_out(
        o1, o4.reshape(4, T // 4, AW), o16.reshape(16, T // 16, AW),
        l1, l4.reshape(4, T // 4, 128), l16.reshape(16, T // 16, 128),
        rec, ag, hg, xs, tgt, attn_out_norm_w, hgrn_out_norm_w, fnw, wout_full, gmat, emat, selmat)

    fst = lambda a: a.reshape(T, 128)
    dq1, dk1, dv1 = _attn_bwd(q1, k1, v1, do1, st1, T // BLK, "attn_bwd_d1")
    dq4, dk4, dv4 = _attn_bwd(flat(q4), flat(k4), flat(v4), flat(do4), fst(st4), T // 4 // BLK, "attn_bwd_d4")
    dq16, dk16, dv16 = _attn_bwd(flat(q16), flat(k16), flat(v16), flat(do16), fst(st16), T // 16 // BLK,
                                 "attn_bwd_d16")
    dproj_h, small6, pout_own, pout_rem = _hgrn_bwd(hq, hf, hi, hgrn_lb_raw, tri, trit, drec, sall, dhg,
                                                    rout, routb)

    r4 = lambda a: a.reshape(4, T // 4, AW)
    r16 = lambda a: a.reshape(16, T // 16, AW)
    dproj_a = _dproj_build((dq1, r4(dq4), r16(dq16)), (dk1, r4(dk4), r16(dk16)), (dv1, r4(dv4), r16(dv16)),
                           dag, pos)
    rin, rinb = _grad_w_in(hn, dproj_a, dproj_h)
    gx, small_all, fin, fout = _bwd_x(dproj_a, dproj_h, xs, dx2, mix_norm_w, w_full, rin, rinb,
                                            small4, small6, pout_own, pout_rem)
    g_w_in = fin.reshape(D, 1024)
    g_w_out = fout.reshape(256, D)

    params = [(mix_norm_w, m_mix_norm_w, v_mix_norm_w),
              (attn_out_norm_w, m_attn_out_norm_w, v_attn_out_norm_w),
              (hgrn_out_norm_w, m_hgrn_out_norm_w, v_hgrn_out_norm_w),
              (hgrn_lb_raw, m_hgrn_lb_raw, v_hgrn_lb_raw),
              (fnw, m_final_norm_w.reshape(1, D), v_final_norm_w.reshape(1, D))]
    d_in, nm_in, nv_in, d_out, nm_out, nv_out, *so = _adamw(
        (w_in.reshape(D, 1024), g_w_in, m_w_in.reshape(D, 1024), v_w_in.reshape(D, 1024)),
        (w_out.reshape(256, D), g_w_out, m_w_out.reshape(256, D), v_w_out.reshape(256, D)), small_all, params)
    loss = so[0].reshape(())
    g_s = [so[1 + 4 * p] for p in range(5)]
    d_s = [so[2 + 4 * p] for p in range(5)]
    m_s = [so[3 + 4 * p] for p in range(5)]
    v_s = [so[4 + 4 * p] for p in range(5)]
    for lst in (g_s, d_s, m_s, v_s):
        lst[4] = lst[4].reshape(D)

    return (loss, gx.reshape(1, T, D),
            g_w_in.reshape(1, D, 1024), g_w_out.reshape(1, 256, D), *g_s,
            d_in.reshape(1, D, 1024), d_out.reshape(1, 256, D), *d_s,
            nm_in.reshape(1, D, 1024), nm_out.reshape(1, 256, D), *m_s,
            nv_in.reshape(1, D, 1024), nv_out.reshape(1, 256, D), *v_s)
```

```python
import functools

import numpy as np
import jax
import jax.numpy as jnp
from jax import lax
from jax.experimental import pallas as pl
from jax.experimental.pallas import tpu as pltpu

F32 = jnp.float32
BF16 = jnp.bfloat16

T = 4096
D = 1024
AW = 512
HW = 512
NCOL = 4096
HEAD = 64
BLK = 128
CHUNK = 64
EPS = 1e-6
SCALE = HEAD ** -0.5
NEG = -1e30
ROPE_THETA = 500000.0
INV_FREQ = [float(v) for v in
            (np.float32(ROPE_THETA) ** (-(np.arange(8, dtype=np.float32)) * np.float32(0.125)))]
LR, B1, B2, AEPS, WD, STEP = 0.001, 0.9, 0.999, 1e-08, 0.01, 10
VMEM_LIMIT = 63 * 1024 * 1024
MESH = pl.DeviceIdType.MESH


def _cp(sem=None, **kw):
    return pltpu.CompilerParams(dimension_semantics=sem, vmem_limit_bytes=VMEM_LIMIT, **kw)


def _mm(a, b):
    return jnp.dot(a, b, preferred_element_type=F32)


def _mm_nt(a, b):
    return lax.dot_general(a, b, (((1,), (1,)), ((), ())), preferred_element_type=F32)


def _mm_tn(a, b):
    return lax.dot_general(a, b, (((0,), (0,)), ((), ())), preferred_element_type=F32)


def _mm_exact_l(mat_bf, x):
    h = x.astype(BF16)
    l = (x - h.astype(F32)).astype(BF16)
    return _mm(mat_bf, h) + _mm(mat_bf, l)


def _mm_exact_r(x, mat_bf):
    h = x.astype(BF16)
    l = (x - h.astype(F32)).astype(BF16)
    return _mm(h, mat_bf) + _mm(l, mat_bf)


def _sigmoid(x):
    return 0.5 * jnp.tanh(0.5 * x) + 0.5


def _rope_tables(pos):
    lane = lax.broadcasted_iota(jnp.int32, (1, 128), 1)
    jl = lane & 63
    fi = jl & 7
    inv = jnp.zeros((1, 128), F32)
    for kk in range(8):
        inv = jnp.where(fi == kk, INV_FREQ[kk], inv)
    ang = jnp.broadcast_to(pos.astype(F32), (128, pos.shape[1])).T * inv
    c = jnp.cos(ang)
    s = jnp.sin(ang)
    cosf = jnp.where(jl < 16, c, 1.0)
    s1 = jnp.where(jl < 8, -s, 0.0)
    s2 = jnp.where((jl >= 8) & (jl < 16), s, 0.0)
    return cosf, s1, s2


def _rope(t, cosf, s1, s2):
    parts = []
    for ci in range(t.shape[1] // 128):
        tc = t[:, ci * 128:(ci + 1) * 128]
        parts.append(tc * cosf + pltpu.roll(tc, 120, 1) * s1 + pltpu.roll(tc, 8, 1) * s2)
    return jnp.concatenate(parts, axis=1)


def _rope_bwd(g, cosf, s1, s2):
    parts = []
    for ci in range(g.shape[1] // 128):
        gc = g[:, ci * 128:(ci + 1) * 128]
        parts.append(gc * cosf + pltpu.roll(gc * s1, 8, 1) + pltpu.roll(gc * s2, 120, 1))
    return jnp.concatenate(parts, axis=1)


def _perm_store(val, scr, scr2, o1, o4, o16, dt):
    n = val.shape[0]
    q = n // 4
    o1[...] = val.astype(dt)
    for ci in range(val.shape[1] // 128):
        cs = slice(ci * 128, (ci + 1) * 128)
        scr[ci] = val[:, cs]
        for r4 in range(4):
            part = scr[ci, pl.ds(r4, q, stride=4), :]
            o4[r4, :, cs] = part.astype(dt)
            scr2[ci, r4 * q:(r4 + 1) * q, :] = part
        for r4 in range(4):
            for b in range(4):
                o16[r4 + 4 * b, :, cs] = scr2[ci, pl.ds(r4 * q + b, q // 4, stride=4), :].astype(dt)


def _unperm_load(r4, r16, scr_a, scr_b, scr_c):
    n = scr_a.shape[1]
    q = n // 4
    nc = r4.shape[-1] // 128
    for ci in range(nc):
        cs = slice(ci * 128, (ci + 1) * 128)
        for rr in range(4):
            scr_a[ci, pl.ds(rr, q, stride=4), :] = r4[rr, :, cs].astype(F32)
        for rr in range(4):
            for b in range(4):
                scr_c[ci, pl.ds(rr * q + b, q // 4, stride=4), :] = r16[rr + 4 * b, :, cs].astype(F32)
        for rr in range(4):
            scr_b[ci, pl.ds(rr, q, stride=4), :] = scr_c[ci, rr * q:(rr + 1) * q, :]
    return (jnp.concatenate([scr_a[ci] for ci in range(nc)], axis=1),
            jnp.concatenate([scr_b[ci] for ci in range(nc)], axis=1))


def _unperm_sum(r4, r16, scr_b, scr_c):
    n = scr_b.shape[1]
    q = n // 4
    nc = r4.shape[-1] // 128
    for ci in range(nc):
        cs = slice(ci * 128, (ci + 1) * 128)
        for rr in range(4):
            for b in range(4):
                scr_c[ci, pl.ds(rr * q + b, q // 4, stride=4), :] = r16[rr + 4 * b, :, cs].astype(F32)
        for rr in range(4):
            scr_b[ci, pl.ds(rr, q, stride=4), :] = scr_c[ci, rr * q:(rr + 1) * q, :] + r4[rr, :, cs].astype(F32)
    return jnp.concatenate([scr_b[ci] for ci in range(nc)], axis=1)


def _fwd_in(x, pos, mixw, w_in, w_out, jm_arr):
    TT = 512
    NT = T // TT

    def body(jm_ref, x_ref, pos_ref, mw_ref, win_ref, wout_ref,
             hnt_ref, q1, k1, v1, q4, k4, v4, q16, k16, v16, ag, hq, hf, hi, hg, wfull_o, woutfull_o,
             wbuf, wobuf, hn_all, scr, scr2, stage, send_sems, recv_sems, loc_sems):
        s = pl.program_id(0)
        i = pl.program_id(1)
        mx, my, c = lax.axis_index("x"), lax.axis_index("y"), lax.axis_index("c")
        me, sibling = (mx, my, c), (mx, my, 1 - c)
        chips = [(mx, 1 - my), (1 - mx, my), (1 - mx, 1 - my)]
        jm = 2 * mx + my
        rows_in = [pl.ds(pl.multiple_of(h * 512, 512), 512) for h in (c, 1 - c)]
        rows_out = [pl.ds(pl.multiple_of(h * 128, 128), 128) for h in (c, 1 - c)]

        def blk(k):
            return lax.bitwise_xor(jm, k + 1)

        def rc(n, ref, to):
            return pltpu.make_async_remote_copy(src_ref=ref, dst_ref=ref, send_sem=send_sems.at[n],
                                                recv_sem=recv_sems.at[n], device_id=to, device_id_type=MESH)

        halves = [pl.ds(0, 512), pl.ds(512, 512)]
        send_in = lambda k, h: rc(12 + 2 * k + h, wbuf.at[jm, rows_in[0], halves[h]], (*chips[k], c))
        got_in = lambda k, h: rc(12 + 2 * k + h, wbuf.at[blk(k), rows_in[0], halves[h]], me)
        relay = lambda h: rc(16 + h, wbuf.at[blk(h), rows_in[0], halves[h]], (*chips[1 - h], c))
        got_relay = lambda h: rc(16 + h, wbuf.at[blk(2), rows_in[0], halves[h]], me)
        send_out = lambda k: rc(3 + k, wobuf.at[jm, rows_out[0], :], (*chips[k], c))
        got_out = lambda k: rc(3 + k, wobuf.at[blk(k), rows_out[0], :], me)
        pass_in = lambda k: rc(6 + k, wbuf.at[blk(k), rows_in[0], :], sibling)
        pass_out = lambda k: rc(9 + k, wobuf.at[blk(k), rows_out[0], :], sibling)
        passed_in = lambda k: rc(6 + k, wbuf.at[blk(k), rows_in[1], :], me)
        passed_out = lambda k: rc(9 + k, wobuf.at[blk(k), rows_out[1], :], me)

        def keep(j, n):
            return pltpu.make_async_copy(wbuf.at[j], wfull_o.at[:, pl.ds(j * 1024, 1024)], loc_sems.at[n])

        @pl.when((s == 0) & (i == 0))
        def _():
            chunk = [pl.ds(pl.multiple_of(lax.rem(p + 2 * c, 4) * 256, 256), 256) for p in range(4)]
            loads = [pltpu.make_async_copy(win_ref.at[chunk[p], :] if p < 4 else wout_ref, stage.at[p % 2],
                                           loc_sems.at[4 + p % 2]) for p in range(5)]
            loads[0].start()
            for p in range(5):
                if p < 4:
                    loads[p + 1].start()
                loads[p].wait()
                if p < 4:
                    wbuf[jm, chunk[p], :] = stage[p % 2].astype(BF16)
                else:
                    wobuf[jm] = stage[p % 2].astype(BF16)
                if p == 1:
                    for k in range(2):
                        for h in range(2):
                            send_in(k, h).start()
            keep(jm, 0).start()

        @pl.when((s == 1) & (i == 0))
        def _():
            for kk in range(2):
                for h in range(2):
                    got_in(kk, h).wait_recv()
            relay(0).start()
            relay(1).start()
            pass_in(0).start()
            pass_in(1).start()
            passed_in(0).wait_recv()
            keep(blk(0), 1).start()

        @pl.when((s == 1) & (i == NT - 1))
        def _():
            got_relay(0).wait_recv()
            got_relay(1).wait_recv()
            pass_in(2).start()

        @pl.when((s == 2) & (i == 0))
        def _():
            for k in (1, 2):
                passed_in(k).wait_recv()
                keep(blk(k), k + 1).start()
            for kk in range(3):
                send_out(kk).start()

        @pl.when((s == 2) & (i == NT - 2))
        def _():
            for k in range(3):
                got_out(k).wait_recv()
                pass_out(k).start()

        whole_out = pltpu.make_async_copy(wobuf, woutfull_o, loc_sems.at[4])

        @pl.when((s == 2) & (i == NT - 1))
        def _():
            for k in range(3):
                passed_out(k).wait_recv()
            whole_out.start()

        tile = pl.ds(pl.multiple_of(i * TT, TT), TT)

        @pl.when(s == 0)
        def _():
            xv = x_ref[...]
            r = lax.rsqrt(jnp.mean(xv * xv, axis=-1, keepdims=True) + EPS)
            hnf = (xv * r) * mw_ref[...]
            hn_all[tile, :] = hnf.astype(BF16)
            hnt_ref[...] = hnf.T.astype(BF16)

        def project(jj):
            hn = hn_all[tile, :]
            lo = _mm(hn, wbuf[jj, :, 0:512])
            hi_cols = _mm(hn, wbuf[jj, :, 512:1024])
            if jj == 0:
                cosf, s1, s2 = _rope_tables(pos_ref[...])
                _perm_store(_rope(lo, cosf, s1, s2) * SCALE, scr, scr2, q1, q4, q16, BF16)
                _perm_store(_rope(hi_cols, cosf, s1, s2), scr, scr2, k1, k4, k16, BF16)
            elif jj == 1:
                _perm_store(lo, scr, scr2, v1, v4, v16, BF16)
                ag[...] = hi_cols.astype(BF16)
            elif jj == 2:
                hq[...] = lo.astype(BF16)
                hf[...] = hi_cols.astype(BF16)
            else:
                hi[...] = lo.astype(BF16)
                hg[...] = hi_cols.astype(BF16)

        def project_block(j):
            for jj in range(4):
                pl.when(j == jj)(functools.partial(project, jj))

        @pl.when(s < 2)
        def _():
            project_block(lax.bitwise_xor(jm, s))

        @pl.when(s == 2)
        def _():
            project_block(lax.bitwise_xor(jm, 2))
            project_block(lax.bitwise_xor(jm, 3))

        @pl.when((s == 2) & (i == NT - 1))
        def _():
            for h in range(2):
                relay(h).wait_send()
                for k in range(2):
                    send_in(k, h).wait_send()
            for k in range(3):
                send_out(k).wait_send()
                pass_in(k).wait_send()
                pass_out(k).wait_send()
            keep(jm, 0).wait()
            for k in range(3):
                keep(blk(k), k + 1).wait()
            whole_out.wait()

    def at_stage_of(jb):
        def index(s, i, jm_ref):
            sa = jnp.minimum(lax.bitwise_xor(jm_ref[0], jb), 2)
            return jnp.where(s < sa, 0, jnp.where(s == sa, i, NT - 1))
        return index

    tok = lambda w, jb: pl.BlockSpec((TT, w), lambda s, i, jm_ref: (at_stage_of(jb)(s, i, jm_ref), 0))
    d4 = lambda jb: pl.BlockSpec((4, TT // 4, AW), lambda s, i, jm_ref: (0, at_stage_of(jb)(s, i, jm_ref), 0))
    d16 = lambda jb: pl.BlockSpec((16, TT // 16, AW), lambda s, i, jm_ref: (0, at_stage_of(jb)(s, i, jm_ref), 0))
    hbm = pl.BlockSpec(memory_space=pltpu.HBM)
    sd = lambda shape, dt: jax.ShapeDtypeStruct(shape, dt)
    in_own_stage = lambda s, i: jnp.where(s == 0, i, NT - 1)
    grid_spec = pltpu.PrefetchScalarGridSpec(
        num_scalar_prefetch=1, grid=(3, NT),
        in_specs=[pl.BlockSpec((TT, D), lambda s, i, jm_ref: (in_own_stage(s, i), 0)),
                  pl.BlockSpec((1, TT), lambda s, i, jm_ref: (0, i)),
                  pl.BlockSpec((1, D), lambda s, i, jm_ref: (0, 0)), hbm, hbm],
        out_specs=[pl.BlockSpec((D, TT), lambda s, i, jm_ref: (0, in_own_stage(s, i))),
                   tok(AW, 0), tok(AW, 0), tok(AW, 1), d4(0), d4(0), d4(1), d16(0), d16(0), d16(1),
                   tok(AW, 1), tok(AW, 2), tok(AW, 2), tok(AW, 3), tok(AW, 3), hbm, hbm],
        scratch_shapes=[pltpu.VMEM((4, D, 1024), BF16), pltpu.VMEM((4, 256, D), BF16), pltpu.VMEM((T, D), BF16),
                        pltpu.VMEM((4, TT, 128), F32), pltpu.VMEM((4, TT, 128), F32), pltpu.VMEM((2, 256, 1024), F32),
                        pltpu.SemaphoreType.DMA((18,)),
                        pltpu.SemaphoreType.DMA((18,)), pltpu.SemaphoreType.DMA((6,))])
    return pl.pallas_call(
        body, name="fwd_in", grid_spec=grid_spec,
        out_shape=[sd((D, T), BF16)] + [sd((T, AW), BF16)] * 3 + [sd((4, T // 4, AW), BF16)] * 3
        + [sd((16, T // 16, AW), BF16)] * 3
        + [sd((T, AW), BF16)] * 5 + [sd((D, NCOL), BF16), sd((4, 256, D), BF16)],
        compiler_params=_cp(("arbitrary", "arbitrary")),
    )(jm_arr, x, pos, mixw, w_in, w_out)


def _band_mask(key_axis, nkeys=2 * BLK):
    shape = (nkeys, 2 * BLK) if key_axis == 0 else (2 * BLK, nkeys)
    kj = lax.broadcasted_iota(jnp.int32, shape, key_axis)
    qi = lax.broadcasted_iota(jnp.int32, shape, 1 - key_axis) & (BLK - 1)
    return (kj >= qi) & (kj <= qi + BLK), kj, qi


def _stack_heads(t2, in_a):
    z = jnp.zeros_like(t2)
    return jnp.concatenate([jnp.where(in_a[0], t2, z), jnp.where(in_a[1], t2, z)], axis=0)


def _attn_fwd(q, k, v, nb, name):
    n = 8
    CH = n * BLK
    halo = nb > n

    def body(*refs):
        if halo:
            q_ref, k_ref, v_ref, kp_ref, vp_ref, o_ref, lse_ref = refs
        else:
            q_ref, k_ref, v_ref, o_ref, lse_ref = refs
        lane = lax.broadcasted_iota(jnp.int32, (1, 128), 1)
        in_a = [lane < HEAD, lane >= HEAD]
        band, kj, _ = _band_mask(1)
        thr0 = jnp.where((n * pl.program_id(0)) % nb == 0, BLK, 0) if halo else BLK
        mask0 = band & (kj >= thr0)
        mask_first = band & (kj >= BLK)
        for b in range(n):
            rs = slice(b * BLK, (b + 1) * BLK)
            stat = jnp.zeros((BLK, 128), F32)
            for hp in range(4):
                cs = slice(hp * 128, (hp + 1) * 128)
                q2s = _stack_heads(q_ref[rs, cs], in_a)
                if b == 0:
                    kprev = kp_ref[:, cs] if halo else k_ref[rs, cs]
                    vprev = vp_ref[:, cs] if halo else v_ref[rs, cs]
                    kk = jnp.concatenate([kprev, k_ref[rs, cs]], axis=0)
                    vv = jnp.concatenate([vprev, v_ref[rs, cs]], axis=0)
                    mask = mask0
                else:
                    kk = k_ref[(b - 1) * BLK:(b + 1) * BLK, cs]
                    vv = v_ref[(b - 1) * BLK:(b + 1) * BLK, cs]
                    mask = mask_first if b % nb == 0 else band
                s = jnp.where(mask, _mm_nt(q2s, kk), NEG)
                m = jnp.max(s, axis=-1, keepdims=True)
                p = jnp.exp(s - m)
                l = jnp.sum(p, axis=-1, keepdims=True)
                o = _mm(p.astype(BF16), vv) / l
                lse = m + jnp.log(l)
                o_ref[rs, cs] = jnp.where(in_a[0], o[:BLK], o[BLK:]).astype(BF16)
                stat = jnp.where(lane == 2 * hp, lse[:BLK], stat)
                stat = jnp.where(lane == 2 * hp + 1, lse[BLK:], stat)
            lse_ref[rs, :] = stat

    cur = pl.BlockSpec((CH, AW), lambda i: (i, 0))
    prev = pl.BlockSpec((BLK, AW), lambda i: (jnp.maximum(n * i - 1, 0), 0))
    return pl.pallas_call(
        body, name=name, grid=(T // CH,),
        in_specs=[cur, cur, cur] + ([prev, prev] if halo else []),
        out_specs=[cur, pl.BlockSpec((CH, 128), lambda i: (i, 0))],
        out_shape=[jax.ShapeDtypeStruct((T, AW), BF16), jax.ShapeDtypeStruct((T, 128), F32)],
        compiler_params=_cp(("parallel",)),
    )(*((q, k, v) + ((k, v) if halo else ())))


def _attn_bwd(q, k, v, do, st, nb, name):
    n = 8
    CH = n * BLK
    NBLK = T // BLK
    halo = nb > n

    def body(*refs):
        if halo:
            (q_ref, k_ref, v_ref, do_ref, st_ref, kp_ref, vp_ref, qn_ref, don_ref, stn_ref,
             dq_ref, dk_ref, dv_ref) = refs
        else:
            q_ref, k_ref, v_ref, do_ref, st_ref, dq_ref, dk_ref, dv_ref = refs
        i = pl.program_id(0)
        lane = lax.broadcasted_iota(jnp.int32, (1, 128), 1)
        in_a = [lane < HEAD, lane >= HEAD]
        band, kj, _ = _band_mask(0)
        thr0 = jnp.where((n * i) % nb == 0, BLK, 0) if halo else BLK
        mask0 = band & (kj >= thr0)
        mask_first = band & (kj >= BLK)

        def stat_rows(st_t, hp):
            lse_r = jnp.concatenate([st_t[2 * hp:2 * hp + 1, :], st_t[2 * hp + 1:2 * hp + 2, :]], axis=1)
            dl_r = jnp.concatenate([st_t[8 + 2 * hp:9 + 2 * hp, :], st_t[9 + 2 * hp:10 + 2 * hp, :]], axis=1)
            return lse_r, dl_r

        st_t = [st_ref[b * BLK:(b + 1) * BLK, :].T for b in range(n)]
        if halo:
            nxt_thr = jnp.where((n * i + n) % nb == 0, 2 * BLK, 0)
            _, kj1, qi1 = _band_mask(0, BLK)
            mask_next = kj1 >= qi1 + nxt_thr
            stn_t = stn_ref[...].T

        for hp in range(4):
            cs = slice(hp * 128, (hp + 1) * 128)
            kb = [k_ref[b * BLK:(b + 1) * BLK, cs] for b in range(n)]
            vb = [v_ref[b * BLK:(b + 1) * BLK, cs] for b in range(n)]
            dk_acc = [jnp.zeros((BLK, 128), F32) for _ in range(n)]
            dv_acc = [jnp.zeros((BLK, 128), F32) for _ in range(n)]
            for b in range(n):
                rs = slice(b * BLK, (b + 1) * BLK)
                q2s = _stack_heads(q_ref[rs, cs], in_a)
                do2s = _stack_heads(do_ref[rs, cs], in_a)
                if b == 0:
                    kprev = kp_ref[:, cs] if halo else kb[0]
                    vprev = vp_ref[:, cs] if halo else vb[0]
                    mask = mask0
                else:
                    kprev, vprev, mask = kb[b - 1], vb[b - 1], (mask_first if b % nb == 0 else band)
                kk = jnp.concatenate([kprev, kb[b]], axis=0)
                vv = jnp.concatenate([vprev, vb[b]], axis=0)
                lse_r, dl_r = stat_rows(st_t[b], hp)
                s_t = jnp.where(mask, _mm_nt(kk, q2s), NEG)
                p_t = jnp.exp(s_t - lse_r)
                ds_t = (p_t * (_mm_nt(vv, do2s) - dl_r)).astype(BF16)
                dkk = _mm(ds_t, q2s)
                dvv = _mm(p_t.astype(BF16), do2s)
                dqs = _mm_tn(ds_t, kk) * SCALE
                dq_ref[rs, cs] = jnp.where(in_a[0], dqs[:BLK], dqs[BLK:]).astype(BF16)
                dk_acc[b] += dkk[BLK:]
                dv_acc[b] += dvv[BLK:]
                if b > 0:
                    dk_acc[b - 1] += dkk[:BLK]
                    dv_acc[b - 1] += dvv[:BLK]
            if halo:
                q2s = _stack_heads(qn_ref[:, cs], in_a)
                do2s = _stack_heads(don_ref[:, cs], in_a)
                lse_r, dl_r = stat_rows(stn_t, hp)
                s_t = jnp.where(mask_next, _mm_nt(kb[n - 1], q2s), NEG)
                p_t = jnp.exp(s_t - lse_r)
                ds_t = (p_t * (_mm_nt(vb[n - 1], do2s) - dl_r)).astype(BF16)
                dk_acc[n - 1] += _mm(ds_t, q2s)
                dv_acc[n - 1] += _mm(p_t.astype(BF16), do2s)
            for b in range(n):
                dk_ref[b * BLK:(b + 1) * BLK, cs] = dk_acc[b].astype(BF16)
                dv_ref[b * BLK:(b + 1) * BLK, cs] = dv_acc[b].astype(BF16)

    cur = pl.BlockSpec((CH, AW), lambda i: (i, 0))
    cur_st = pl.BlockSpec((CH, 128), lambda i: (i, 0))
    prev = pl.BlockSpec((BLK, AW), lambda i: (jnp.maximum(n * i - 1, 0), 0))
    nxt = pl.BlockSpec((BLK, AW), lambda i: (jnp.minimum(n * i + n, NBLK - 1), 0))
    nxt_st = pl.BlockSpec((BLK, 128), lambda i: (jnp.minimum(n * i + n, NBLK - 1), 0))
    ins = [cur] * 4 + [cur_st] + ([prev, prev, nxt, nxt, nxt_st] if halo else [])
    args = (q, k, v, do, st) + ((k, v, q, do, st) if halo else ())
    return pl.pallas_call(
        body, name=name, grid=(T // CH,),
        in_specs=ins,
        out_specs=[cur] * 3,
        out_shape=[jax.ShapeDtypeStruct((T, AW), BF16)] * 3,
        compiler_params=_cp(("parallel",)),
    )(*args)


TH = 256
NCH = TH // CHUNK


def _hgrn_common(hq_ref, hf_ref, lbr_ref, tri_ref):
    r0 = lbr_ref[0:1, :]
    r1 = lbr_ref[1:2, :]
    mx = jnp.maximum(r0, r1)
    e0 = jnp.exp(r0 - mx)
    e1 = jnp.exp(r1 - mx)
    lb = e0 / (e0 + e1)
    hqv = hq_ref[...].astype(F32)
    sq = _sigmoid(hqv)
    qv = hqv * sq
    sf = _sigmoid(hf_ref[...].astype(F32))
    f = lb + (1.0 - lb) * sf
    kv = 1.0 - f
    g = jnp.log(f)
    cum = _mm_exact_l(tri_ref[...], g)
    dec = jnp.exp(jnp.concatenate([cum[c * CHUNK + CHUNK - 1:(c + 1) * CHUNK, :] for c in range(NCH)], axis=0))
    decb = jnp.concatenate([jnp.broadcast_to(dec[c:c + 1, :], (CHUNK, HW)) for c in range(NCH)], axis=0)
    ea = jnp.exp(cum)
    ena = jnp.exp(-cum)
    eend = decb * ena
    return dict(lb=lb, hq=hqv, sq=sq, q=qv, sf=sf, f=f, k=kv, cum=cum, ea=ea, ena=ena, eend=eend,
                qd=qv * ea, ki=kv * ena, ke=kv * eend, dec=dec)


def _tri_mask(transposed=False):
    ti = lax.broadcasted_iota(jnp.int32, (TH, TH), 1 if transposed else 0)
    si = lax.broadcasted_iota(jnp.int32, (TH, TH), 0 if transposed else 1)
    return (si <= ti) & ((si // CHUNK) == (ti // CHUNK))


def _hgrn_fwd(hq, hf, hi, lbr, tri):
    NSUB = 2

    def body(hq_ref, hf_ref, hi_ref, lbr_ref, tri_ref, rec_ref, sall_ref, st_scr):
        @pl.when(pl.program_id(0) == 0)
        def _():
            st_scr[...] = jnp.zeros_like(st_scr)

        causal = _tri_mask()
        for u in range(NSUB):
            tile = slice(u * TH, (u + 1) * TH)
            w = _hgrn_common(hq_ref.at[tile, :], hf_ref.at[tile, :], lbr_ref, tri_ref)
            qd, ki, ke = w["qd"].astype(BF16), w["ki"].astype(BF16), w["ke"].astype(BF16)
            dec = w["dec"]
            vb = hi_ref[tile, :]
            for h in range(4):
                cs = slice(h * 128, (h + 1) * 128)
                att = jnp.where(causal, _mm_nt(qd[:, cs], ki[:, cs]), 0.0)
                o_intra = _mm(att.astype(BF16), vb[:, cs])
                st = st_scr[:, cs]
                for c in range(NCH):
                    rs = slice(c * CHUNK, (c + 1) * CHUNK)
                    sall_ref[u * NCH + c, :, cs] = st
                    rec_ref[u * TH + c * CHUNK:u * TH + (c + 1) * CHUNK, cs] = (
                        o_intra[rs] + _mm_nt(qd[rs, cs], st.astype(BF16))).astype(BF16)
                    st = dec[c:c + 1, cs] * st + _mm_tn(vb[rs, cs], ke[rs, cs])
                st_scr[:, cs] = st

    tok = pl.BlockSpec((NSUB * TH, HW), lambda i: (i, 0))
    return pl.pallas_call(
        body, name="hgrn_fwd", grid=(T // (NSUB * TH),),
        in_specs=[tok, tok, tok, pl.BlockSpec((2, HW), lambda i: (0, 0)), pl.BlockSpec((TH, TH), lambda i: (0, 0))],
        out_specs=[tok, pl.BlockSpec((NSUB * NCH, 128, HW), lambda i: (i, 0, 0))],
        out_shape=[jax.ShapeDtypeStruct((T, HW), BF16), jax.ShapeDtypeStruct((T // CHUNK, 128, HW), F32)],
        scratch_shapes=[pltpu.VMEM((128, HW), F32)],
        compiler_params=_cp(("arbitrary",)),
    )(hq, hf, hi, lbr, tri)


def _hgrn_bwd(hq, hf, hi, lbr, tri, trit, drec, sall, dhg, rout, routb):
    NSUB = 2
    NT = T // (NSUB * TH)

    def body(hq_ref, hf_ref, hi_ref, lbr_ref, tri_ref, trit_ref, do_ref, sall_ref, dhg_ref, rout_r, routb_r,
             dph_ref, small_ref, pout_o, poutr_o,
             dst_scr, dlb_scr, dqd_scr, dki_scr, dke_scr, dlast_scr, send_sems, recv_sems, loc_sems):
        step = pl.program_id(0)
        loc, rem = _chip_copies(_w_out_piece, rout_r, routb_r, pout_o, poutr_o, send_sems, recv_sems,
                                loc_sems.at[0])

        @pl.when(step == 0)
        def _():
            dst_scr[...] = jnp.zeros_like(dst_scr)
            dlb_scr[...] = jnp.zeros_like(dlb_scr)
            for cp in loc + rem:
                cp.start()

        causal = _tri_mask()
        causal_t = _tri_mask(transposed=True)
        lb = None
        for u in reversed(range(NSUB)):
            tile = slice(u * TH, (u + 1) * TH)
            w = _hgrn_common(hq_ref.at[tile, :], hf_ref.at[tile, :], lbr_ref, tri_ref)
            qd, ki, ke = w["qd"].astype(BF16), w["ki"].astype(BF16), w["ke"].astype(BF16)
            dec = w["dec"]
            vb = hi_ref[tile, :]
            dob = do_ref[tile, :].astype(BF16)
            for h in range(4):
                cs = slice(h * 128, (h + 1) * 128)
                att_t = jnp.where(causal_t, _mm_nt(ki[:, cs], qd[:, cs]), 0.0).astype(BF16)
                datt_t = jnp.where(causal_t, _mm_nt(vb[:, cs], dob[:, cs]), 0.0).astype(BF16)
                datt = jnp.where(causal, _mm_nt(dob[:, cs], vb[:, cs]), 0.0).astype(BF16)
                dv_intra = _mm(att_t, dob[:, cs])
                dqd_intra = _mm(datt, ki[:, cs])
                dki_scr[u, :, cs] = _mm(datt_t, qd[:, cs])
                dst = dst_scr[:, cs]
                for c in reversed(range(NCH)):
                    rs = slice(c * CHUNK, (c + 1) * CHUNK)
                    dec_c = dec[c:c + 1, :]
                    st = sall_ref[u * NCH + c, :, cs]
                    dstb = dst.astype(BF16)
                    dph_ref[u * TH + c * CHUNK:u * TH + (c + 1) * CHUNK, 2 * HW + h * 128:2 * HW + (h + 1) * 128] = (
                        dv_intra[rs] + _mm_nt(ke[rs, cs], dstb)).astype(BF16)
                    dqd_scr[u, rs, cs] = dqd_intra[rs] + _mm(dob[rs, cs], st.astype(BF16))
                    dke_scr[u, rs, cs] = _mm(vb[rs, cs], dstb)
                    ddec = jnp.sum(dst * st, axis=0, keepdims=True)
                    dlast_scr[u, c:c + 1, cs] = ddec * dec_c[:, cs]
                    dst = dec_c[:, cs] * dst + _mm_tn(dob[rs, cs], qd[rs, cs])
                dst_scr[:, cs] = dst
            dqd, dki, dke = dqd_scr[u], dki_scr[u], dke_scr[u]
            dq = dqd * w["ea"]
            dk = dki * w["ena"] + dke * w["eend"]
            dcum = dqd * w["qd"] - dki * w["ki"] - dke * w["ke"]
            dkeke = dke * w["ke"]
            dlastb = jnp.concatenate(
                [jnp.broadcast_to(dlast_scr[u, c:c + 1, :]
                                  + jnp.sum(dkeke[c * CHUNK:(c + 1) * CHUNK], axis=0, keepdims=True), (CHUNK, HW))
                 for c in range(NCH)], axis=0)
            dg = _mm_exact_l(trit_ref[...], dcum) + dlastb
            df = dg / w["f"] - dk
            lb, sf, sq = w["lb"], w["sf"], w["sq"]
            dph_ref[tile, HW:2 * HW] = (df * (1.0 - lb) * sf * (1.0 - sf)).astype(BF16)
            dph_ref[tile, 0:HW] = (dq * (sq * (1.0 + w["hq"] * (1.0 - sq)))).astype(BF16)
            dph_ref[tile, 3 * HW:4 * HW] = dhg_ref[tile, :]
            dlb_scr[...] += jnp.sum(df * (1.0 - sf), axis=0, keepdims=True)

        @pl.when(step == NT - 1)
        def _():
            gr = dlb_scr[...] * lb * (1.0 - lb)
            small_ref[...] = jnp.zeros_like(small_ref)
            small_ref[0:1, 0:HW] = gr
            small_ref[1:2, 0:HW] = -gr
            for cp in rem:
                cp.wait_recv()
            for cp in rem:
                cp.wait_send()
            for cp in loc:
                cp.wait()

    tok = pl.BlockSpec((NSUB * TH, HW), lambda i: (NT - 1 - i, 0))
    const = lambda shape: pl.BlockSpec(shape, lambda i: (0,) * len(shape))
    hbm = pl.BlockSpec(memory_space=pltpu.HBM)
    return pl.pallas_call(
        body, name="hgrn_bwd", grid=(NT,),
        in_specs=[tok, tok, tok, const((2, HW)), const((TH, TH)), const((TH, TH)), tok,
                  pl.BlockSpec((NSUB * NCH, 128, HW), lambda i: (NT - 1 - i, 0, 0)), tok, hbm, hbm],
        out_specs=[pl.BlockSpec((NSUB * TH, NCOL // 2), lambda i: (NT - 1 - i, 0)), const((8, D)), hbm, hbm],
        out_shape=[jax.ShapeDtypeStruct((T, NCOL // 2), BF16), jax.ShapeDtypeStruct((8, D), F32),
                   jax.ShapeDtypeStruct((128, D), F32), jax.ShapeDtypeStruct((3, 128, D), BF16)],
        scratch_shapes=[pltpu.VMEM((128, HW), F32), pltpu.VMEM((1, HW), F32), pltpu.VMEM((NSUB, TH, HW), F32),
                        pltpu.VMEM((NSUB, TH, HW), F32), pltpu.VMEM((NSUB, TH, HW), F32),
                        pltpu.VMEM((NSUB, 8, HW), F32),
                        pltpu.SemaphoreType.DMA((3,)), pltpu.SemaphoreType.DMA((3,)), pltpu.SemaphoreType.DMA((1,))],
        compiler_params=_cp(("arbitrary",)),
    )(hq, hf, hi, lbr, tri, trit, drec, sall, dhg, rout, routb)


def _fwd_out(o1, o4, o16, l1, l4, l16, rec, ag, hg, x, tgt, anw, hnw, fnw, wout_full, gmat, emat, selmat):
    TT = 512

    def body(o1_r, o4_r, o16_r, l1_r, l4_r, l16_r, rec_r, ag_r, hg_r, x_r, tgt_r, anw_r, hnw_r, fnw_r, wo_r, g_r,
             e_r, sel_r, dx2_o, do1_o, do4_o, do16_o, st1_o, st4_o, st16_o, drec_o, dag_o, dhg_o,
             rout_o, routb_o, small_o, scr_a, scr_b, scr_c, gwout_o, rbuf, send_sems, recv_sems):
        @pl.when(pl.program_id(0) == 0)
        def _():
            gwout_o[...] = jnp.zeros_like(gwout_o)
            small_o[...] = jnp.zeros_like(small_o)

        def unperm(r4, r16):
            return _unperm_load(r4, r16, scr_a, scr_b, scr_c)

        def perm_out(val, p1, p4, p16, dt):
            _perm_store(val, scr_a, scr_b, p1, p4, p16, dt)

        o4u, o16u = unperm(o4_r, o16_r)
        l4c, l16c = unperm(l4_r, l16_r)
        l1c = l1_r[...]
        mxc = jnp.maximum(jnp.maximum(l1c, l4c), l16c)
        w1c, w4c, w16c = jnp.exp(l1c - mxc), jnp.exp(l4c - mxc), jnp.exp(l16c - mxc)
        denc = w1c + w4c + w16c
        lane = lax.broadcasted_iota(jnp.int32, (1, 128), 1)
        lse_c = jnp.where(lane < 8, mxc + jnp.log(denc), 0.0)
        em = e_r[...]
        wn1 = _mm_exact_r(w1c / denc, em)
        wn4 = _mm_exact_r(w4c / denc, em)
        o1v = o1_r[...].astype(F32)
        attn = wn1 * o1v + wn4 * o4u + (1.0 - wn1 - wn4) * o16u
        gm = g_r[...]

        def head_mean_a(t):
            return jnp.concatenate([_mm_exact_r(t[:, :256], gm), _mm_exact_r(t[:, 256:], gm)], axis=1)

        def head_mean_h(t):
            return jnp.concatenate(
                [jnp.broadcast_to(jnp.mean(t[:, h * 128:(h + 1) * 128], axis=-1, keepdims=True), (TT, 128))
                 for h in range(4)], axis=1)

        rs_a = lax.rsqrt(head_mean_a(attn * attn) + EPS)
        n_a = attn * rs_a
        agv = ag_r[...].astype(F32)
        sg_a = _sigmoid(agv)
        si_a = agv * sg_a
        anw_v = anw_r[...]
        y_a = (n_a * anw_v) * si_a
        recv = rec_r[...].astype(F32)
        rs_h = lax.rsqrt(head_mean_h(recv * recv) + EPS)
        n_h = recv * rs_h
        hgv = hg_r[...].astype(F32)
        sg_h = _sigmoid(hgv)
        si_h = hgv * sg_h
        hnw_v = hnw_r[...]
        y_h = (n_h * hnw_v) * si_h
        mixed = jnp.concatenate([y_a, y_h], axis=1).astype(BF16)
        xv = x_r[...]
        x2 = xv + _mm(mixed, wo_r[...])
        r2 = lax.rsqrt(jnp.mean(x2 * x2, axis=-1, keepdims=True) + EPS)
        fnw_v = fnw_r[...]
        xn = x2 * r2
        err = xn * fnw_v - tgt_r[...]
        small_o[2:3, :] += 0.5 * jnp.sum(jnp.mean(err * err, axis=-1, keepdims=True), axis=0, keepdims=True)
        small_o[0:1, :] += jnp.sum(err * xn, axis=0, keepdims=True) * (1.0 / D)
        dyw = err * (fnw_v * (1.0 / D))
        dx2 = r2 * dyw - x2 * ((r2 * r2 * r2) * jnp.mean(dyw * x2, axis=-1, keepdims=True))
        dx2_o[...] = dx2
        dx2b = dx2.astype(BF16)
        gwout_o[...] += _mm_tn(mixed, dx2b)
        dmix = _mm_nt(dx2b, wo_r[...])
        dm_a, dm_h = dmix[:, :AW], dmix[:, AW:]
        dag_o[...] = (dm_a * (n_a * anw_v) * (sg_a * (1.0 + agv * (1.0 - sg_a)))).astype(BF16)
        dy_a = dm_a * si_a
        dn_a = dy_a * anw_v
        small_o[1:2, 0:AW] += jnp.sum(dy_a * n_a, axis=0, keepdims=True)
        dattn = rs_a * (dn_a - n_a * head_mean_a(dn_a * n_a))
        perm_out(dattn, do1_o, do4_o, do16_o, BF16)
        stats = lse_c + _mm_exact_r(dattn * attn, sel_r[...])
        perm_out(stats, st1_o, st4_o, st16_o, F32)
        dhg_o[...] = (dm_h * (n_h * hnw_v) * (sg_h * (1.0 + hgv * (1.0 - sg_h)))).astype(BF16)
        dy_h = dm_h * si_h
        dn_h = dy_h * hnw_v
        small_o[1:2, AW:] += jnp.sum(dy_h * n_h, axis=0, keepdims=True)
        drec_o[...] = (rs_h * (dn_h - n_h * head_mean_h(dn_h * n_h))).astype(BF16)

        @pl.when(pl.program_id(0) == T // TT - 1)
        def _():
            x, y, c = lax.axis_index("x"), lax.axis_index("y"), lax.axis_index("c")
            cps = [pltpu.make_async_remote_copy(
                src_ref=gwout_o.at[pl.ds(pl.multiple_of(j * 256 + (1 - c) * 128, 128), 128), :], dst_ref=rbuf.at[j],
                send_sem=send_sems.at[j], recv_sem=recv_sems.at[j], device_id=(x, y, 1 - c), device_id_type=MESH)
                for j in range(4)]
            for cp in cps:
                cp.start()
            for j, cp in enumerate(cps):
                cp.wait_recv()
                red = gwout_o[pl.ds(pl.multiple_of(j * 256 + c * 128, 128), 128), :] + rbuf[j]
                rout_o[j * 128:(j + 1) * 128, :] = red
                routb_o[j * 128:(j + 1) * 128, :] = red.astype(BF16)
            for cp in cps:
                cp.wait_send()

    tok = lambda w: pl.BlockSpec((TT, w), lambda i: (i, 0))
    d4 = pl.BlockSpec((4, TT // 4, AW), lambda i: (0, i, 0))
    d16 = pl.BlockSpec((16, TT // 16, AW), lambda i: (0, i, 0))
    const = lambda shape: pl.BlockSpec(shape, lambda i: (0,) * len(shape))
    sd = lambda shape, dt: jax.ShapeDtypeStruct(shape, dt)
    c4 = pl.BlockSpec((4, TT // 4, 128), lambda i: (0, i, 0))
    c16 = pl.BlockSpec((16, TT // 16, 128), lambda i: (0, i, 0))
    p3 = lambda w, dt: [sd((T, w), dt), sd((4, T // 4, w), dt), sd((16, T // 16, w), dt)]
    return pl.pallas_call(
        body, name="fwd_out", grid=(T // TT,),
        in_specs=[tok(AW), d4, d16, tok(128), c4, c16, tok(AW), tok(AW), tok(AW), tok(D), tok(D),
                  const((1, AW)), const((1, HW)), const((1, D)), const((D, D)), const((256, 256)),
                  const((128, AW)), const((AW, 128))],
        out_specs=[tok(D)] + [tok(AW), d4, d16] + [tok(128), c4, c16] + [tok(AW)] * 3
        + [const((512, D)), const((512, D)), const((8, D))],
        out_shape=[sd((T, D), F32)] + p3(AW, BF16) + p3(128, F32)
        + [sd((T, AW), BF16), sd((T, AW), BF16), sd((T, AW), BF16), sd((512, D), F32), sd((512, D), BF16),
           sd((8, D), F32)],
        scratch_shapes=[pltpu.VMEM((4, TT, 128), F32)] * 3 + [pltpu.VMEM((D, D), F32),
                        pltpu.VMEM((4, 128, D), F32), pltpu.SemaphoreType.DMA((4,)), pltpu.SemaphoreType.DMA((4,))],
        compiler_params=_cp(("arbitrary",)),
    )(o1, o4, o16, l1, l4, l16, rec, ag, hg, x, tgt, anw, hnw, fnw, wout_full, gmat, emat, selmat)


def _dproj_build(dq, dk, dv, dag, pos):
    TT = 512

    def body(dq1, dq4, dq16, dk1, dk4, dk16, dv1, dv4, dv16, dag_r, pos_r, dproj_o, scr_b, scr_c):
        def unperm_sum(r1, r4, r16):
            return r1[...] + _unperm_sum(r4, r16, scr_b, scr_c)

        cosf, s1, s2 = _rope_tables(pos_r[...])
        dproj_o[:, 0:512] = _rope_bwd(unperm_sum(dq1, dq4, dq16), cosf, s1, s2).astype(BF16)
        dproj_o[:, 512:1024] = _rope_bwd(unperm_sum(dk1, dk4, dk16), cosf, s1, s2).astype(BF16)
        dproj_o[:, 1024:1536] = unperm_sum(dv1, dv4, dv16).astype(BF16)
        dproj_o[:, 1536:2048] = dag_r[...]

    tok = lambda w: pl.BlockSpec((TT, w), lambda i: (i, 0))
    d4 = pl.BlockSpec((4, TT // 4, AW), lambda i: (0, i, 0))
    d16 = pl.BlockSpec((16, TT // 16, AW), lambda i: (0, i, 0))
    return pl.pallas_call(
        body, name="dproj_build", grid=(T // TT,),
        in_specs=[tok(AW), d4, d16] * 3 + [tok(AW), pl.BlockSpec((1, TT), lambda i: (0, i))],
        out_specs=tok(NCOL // 2),
        out_shape=jax.ShapeDtypeStruct((T, NCOL // 2), BF16),
        scratch_shapes=[pltpu.VMEM((4, TT, 128), F32)] * 2,
        compiler_params=_cp(("parallel",)),
    )(*dq, *dk, *dv, dag, pos)


def _bwd_x(dproj_a, dproj_h, x, dx2, mixw, w_full, rin, rinb, small4, small6, pout_own, pout_rem):
    TT = 256
    NT = T // TT

    def body(dpa_r, dph_r, x_r, dx2_r, mw_r, w_r, rin_r, rinb_r, s4_r, s6_r, poo_r, por_r,
             gx_o, sall_o, fin_o, fout_o, sbuf, v_own, v_rem, vo_own, vo_rem, sin, sout, got_in,
             got_out, send_sems, recv_sems, loc_sems, share_send, share_recv, fin_sems):
        i = pl.program_id(0)
        loc, rem = _chip_copies(_w_in_piece, rin_r, rinb_r, v_own, v_rem, send_sems, recv_sems, loc_sems.at[0])
        loads = [pltpu.make_async_copy(poo_r, vo_own, fin_sems.at[2]),
                 pltpu.make_async_copy(por_r, vo_rem, fin_sems.at[3])]

        @pl.when(i == 0)
        def _():
            sbuf[...] = jnp.zeros_like(sbuf)
            for cp in loc + rem + loads:
                cp.start()

        dhn = _mm_nt(dpa_r[...], w_r[:, 0:NCOL // 2]) + _mm_nt(dph_r[...], w_r[:, NCOL // 2:NCOL])
        xv = x_r[...]
        r = lax.rsqrt(jnp.mean(xv * xv, axis=-1, keepdims=True) + EPS)
        dxw = dhn * mw_r[...]
        gx_o[...] = dx2_r[...] + r * dxw - xv * ((r * r * r) * jnp.mean(dxw * xv, axis=-1, keepdims=True))
        sbuf[16:17, :] += jnp.sum(dhn * (xv * r), axis=0, keepdims=True)

        @pl.when(i == NT - 1)
        def _():
            sbuf[0:8, :] = s4_r[...]
            sbuf[8:16, :] = s6_r[...]
            sloc, srem = _small_copies(sbuf, sall_o, send_sems, recv_sems, loc_sems.at[1])
            for cp in sloc + srem:
                cp.start()
            for cp in rem:
                cp.wait_recv()
            for cp in rem:
                cp.wait_send()
            for cp in loc:
                cp.wait()
            mx, my, c = lax.axis_index("x"), lax.axis_index("y"), lax.axis_index("c")
            for cp in loads:
                cp.wait()
            sout[...] = ((vo_own[...] + vo_rem[0].astype(F32)) + vo_rem[1].astype(F32)) + vo_rem[2].astype(F32)
            sin[...] = ((v_own[...] + v_rem[0].astype(F32)) + v_rem[1].astype(F32)) + v_rem[2].astype(F32)
            swap = [pltpu.make_async_remote_copy(src_ref=sin, dst_ref=got_in, send_sem=share_send.at[0],
                                                 recv_sem=share_recv.at[0], device_id=(mx, my, 1 - c),
                                                 device_id_type=MESH),
                    pltpu.make_async_remote_copy(src_ref=sout, dst_ref=got_out, send_sem=share_send.at[1],
                                                 recv_sem=share_recv.at[1], device_id=(mx, my, 1 - c),
                                                 device_id_type=MESH)]
            for cp in swap:
                cp.start()
            mine = [pltpu.make_async_copy(sin, fin_o.at[c], fin_sems.at[0]),
                    pltpu.make_async_copy(sout, fout_o.at[c], fin_sems.at[1])]
            for cp in mine:
                cp.start()
            for cp in swap:
                cp.wait_recv()
            theirs = [pltpu.make_async_copy(got_in, fin_o.at[1 - c], fin_sems.at[2]),
                      pltpu.make_async_copy(got_out, fout_o.at[1 - c], fin_sems.at[3])]
            for cp in theirs:
                cp.start()
            for cp in swap:
                cp.wait_send()
            for cp in mine + theirs:
                cp.wait()
            for cp in srem:
                cp.wait_recv()
            for cp in srem:
                cp.wait_send()
            for cp in sloc:
                cp.wait()

    tok = lambda w: pl.BlockSpec((TT, w), lambda i: (i, 0))
    const = lambda shape: pl.BlockSpec(shape, lambda i: (0,) * len(shape))
    hbm = pl.BlockSpec(memory_space=pltpu.HBM)
    return pl.pallas_call(
        body, name="bwd_x", grid=(NT,),
        in_specs=[tok(NCOL // 2), tok(NCOL // 2), tok(D), tok(D), const((1, D)), const((D, NCOL)), hbm, hbm,
                  const((8, D)), const((8, D)), hbm, hbm],
        out_specs=[tok(D), hbm, hbm, hbm],
        out_shape=[jax.ShapeDtypeStruct((T, D), F32),
                   jax.ShapeDtypeStruct((8, 24, D), F32),
                   jax.ShapeDtypeStruct((2, 512, 1024), F32), jax.ShapeDtypeStruct((2, 128, D), F32)],
        scratch_shapes=[pltpu.VMEM((24, D), F32),
                        pltpu.VMEM((512, 1024), F32), pltpu.VMEM((3, 512, 1024), BF16),
                        pltpu.VMEM((128, D), F32), pltpu.VMEM((3, 128, D), BF16),
                        pltpu.VMEM((512, 1024), F32), pltpu.VMEM((128, D), F32),
                        pltpu.VMEM((512, 1024), F32), pltpu.VMEM((128, D), F32),
                        pltpu.SemaphoreType.DMA((10,)), pltpu.SemaphoreType.DMA((10,)), pltpu.SemaphoreType.DMA((2,)),
                        pltpu.SemaphoreType.DMA((2,)), pltpu.SemaphoreType.DMA((2,)), pltpu.SemaphoreType.DMA((4,))],
        compiler_params=_cp(("arbitrary",)),
    )(dproj_a, dproj_h, x, dx2, mixw, w_full, rin, rinb, small4, small6, pout_own, pout_rem)


def _grad_w_in(hn, dproj_a, dproj_h):
    TK = 2048
    NK = T // TK

    def body(hnt_r, dpa_r, dph_r, rin_o, rinb_o, acc, rbuf, obuf, obufb, send_sems, recv_sems, wb_sems):
        j = pl.program_id(0)
        kk = pl.program_id(1)
        x, y, c = lax.axis_index("x"), lax.axis_index("y"), lax.axis_index("c")
        mine = pl.ds(pl.multiple_of(c * 512, 512), 512)
        theirs = pl.ds(pl.multiple_of((1 - c) * 512, 512), 512)

        def send(jj):
            return pltpu.make_async_remote_copy(
                src_ref=acc.at[jj % 2, theirs, :], dst_ref=rbuf.at[jj], send_sem=send_sems.at[jj],
                recv_sem=recv_sems.at[jj], device_id=(x, y, 1 - c), device_id_type=MESH)

        def writeback(jj):
            cols = pl.ds(jj * 1024, 1024)
            return [pltpu.make_async_copy(obuf.at[jj % 2], rin_o.at[:, cols], wb_sems.at[jj % 2]),
                    pltpu.make_async_copy(obufb.at[jj % 2], rinb_o.at[:, cols], wb_sems.at[2 + jj % 2])]

        def wait_writeback(jj):
            for cp in writeback(jj):
                cp.wait()

        def finalize(jj):
            send(jj).wait_recv()
            red = acc[jj % 2, mine, :] + rbuf[jj]
            obuf[jj % 2] = red
            obufb[jj % 2] = red.astype(BF16)
            for cp in writeback(jj):
                cp.start()

        prod = _mm(hnt_r[...], jnp.where(j < 2, dpa_r[...], dph_r[...]))

        @pl.when(kk == 0)
        def _():
            for jj in (2, 3):
                @pl.when(j == jj)
                def _():
                    send(jj - 2).wait_send()
            acc[j % 2] = prod

        @pl.when(kk > 0)
        def _():
            acc[j % 2] += prod

        @pl.when(kk == NK - 1)
        def _():
            for jj in range(4):
                @pl.when(j == jj)
                def _():
                    send(jj).start()
                    if jj in (1, 2):
                        finalize(jj - 1)
                    if jj == 3:
                        wait_writeback(0)
                        finalize(2)
                        wait_writeback(1)
                        finalize(3)
                        wait_writeback(2)
                        wait_writeback(3)
                        send(2).wait_send()
                        send(3).wait_send()

    hbm = pl.BlockSpec(memory_space=pltpu.HBM)
    return pl.pallas_call(
        body, name="grad_w_in", grid=(4, NK),
        in_specs=[pl.BlockSpec((D, TK), lambda j, kk: (0, kk)),
                  pl.BlockSpec((TK, 1024), lambda j, kk: (jnp.where(j < 2, kk, NK - 1), jnp.minimum(j, 1))),
                  pl.BlockSpec((TK, 1024), lambda j, kk: (jnp.where(j < 2, 0, kk), jnp.maximum(j - 2, 0)))],
        out_specs=[hbm, hbm],
        out_shape=[jax.ShapeDtypeStruct((512, NCOL), F32), jax.ShapeDtypeStruct((512, NCOL), BF16)],
        scratch_shapes=[pltpu.VMEM((2, D, 1024), F32), pltpu.VMEM((4, 512, 1024), F32), pltpu.VMEM((2, 512, 1024), F32),
                        pltpu.VMEM((2, 512, 1024), BF16),
                        pltpu.SemaphoreType.DMA((4,)), pltpu.SemaphoreType.DMA((4,)), pltpu.SemaphoreType.DMA((4,))],
        compiler_params=_cp(("arbitrary", "arbitrary")),
    )(hn, dproj_a, dproj_h)


def _w_in_piece(ref, j):
    return ref.at[:, pl.ds(j * 1024, 1024)]


def _w_out_piece(ref, j):
    return ref.at[pl.ds(j * 128, 128), :]


def _chip_copies(piece, src_r, srcb_r, own_o, rem_o, send_sems, recv_sems, loc_sem):
    x, y, c = lax.axis_index("x"), lax.axis_index("y"), lax.axis_index("c")
    chips = [(1 - x, y), (x, 1 - y), (1 - x, 1 - y)]
    loc = [pltpu.make_async_copy(piece(src_r, 2 * x + y), own_o, loc_sem)]
    rem = [pltpu.make_async_remote_copy(
        src_ref=piece(srcb_r, 2 * px + py), dst_ref=rem_o.at[k], send_sem=send_sems.at[k],
        recv_sem=recv_sems.at[k], device_id=(px, py, c), device_id_type=MESH) for k, (px, py) in enumerate(chips)]
    return loc, rem


def _small_copies(small_r, sall_o, send_sems, recv_sems, loc_sem):
    x, y, c = lax.axis_index("x"), lax.axis_index("y"), lax.axis_index("c")
    me = 4 * x + 2 * y + c
    loc = [pltpu.make_async_copy(small_r, sall_o.at[me], loc_sem)]
    rem = []
    k = 3
    for fx in range(2):
        for fy in range(2):
            for fc in range(2):
                if fx or fy or fc:
                    peer = (1 - x if fx else x, 1 - y if fy else y, 1 - c if fc else c)
                    rem.append(pltpu.make_async_remote_copy(
                        src_ref=small_r, dst_ref=sall_o.at[me], send_sem=send_sems.at[k],
                        recv_sem=recv_sems.at[k], device_id=peer, device_id_type=MESH))
                    k += 1
    return loc, rem


def _adamw_math(w, g, m, v):
    m = B1 * m + (1.0 - B1) * g
    v = B2 * v + (1.0 - B2) * (g * g)
    m_hat = m / (1.0 - B1 ** STEP)
    v_hat = v / (1.0 - B2 ** STEP)
    delta = -LR * (m_hat / (jnp.sqrt(v_hat) + AEPS) + WD * w)
    return delta, m, v


def _adamw(big_in, big_out, sall, params):
    def body(*refs):
        wi, gi, mi, vi, wo, go, mo, vo, sall_r = refs[:9]
        ins = refs[9:24]
        di_o, mi_o, vi_o, do_o, mo_o, vo_o = refs[24:30]
        outs = refs[30:]
        d, mm, vv = _adamw_math(wi[...], gi[...], mi[...], vi[...])
        di_o[...] = d
        mi_o[...] = mm
        vi_o[...] = vv

        @pl.when(pl.program_id(0) == 0)
        def _():
            d, mm, vv = _adamw_math(wo[...], go[...], mo[...], vo[...])
            do_o[...] = d
            mo_o[...] = mm
            vo_o[...] = vv
            tot = sall_r[0]
            for dv in range(1, 8):
                tot = tot + sall_r[dv]
            grads = [tot[16:17, :], tot[1:2, 0:AW], tot[1:2, AW:], tot[8:10, 0:HW], tot[0:1, :]]
            outs[0][...] = tot[2:3, 0:1]
            for p in range(5):
                w_r, m_r, v_r = ins[3 * p:3 * p + 3]
                g = grads[p]
                d, mm, vv = _adamw_math(w_r[...], g, m_r[...], v_r[...])
                outs[1 + 4 * p][...] = g
                outs[2 + 4 * p][...] = d
                outs[3 + 4 * p][...] = mm
                outs[4 + 4 * p][...] = vv

    flat = [a for p in params for a in p]
    shapes = [jax.ShapeDtypeStruct((D, 1024), F32)] * 3 + [jax.ShapeDtypeStruct((256, D), F32)] * 3
    shapes += [jax.ShapeDtypeStruct((1, 1), F32)]
    for p in params:
        shapes += [jax.ShapeDtypeStruct(p[0].shape, F32)] * 4
    vm = pl.BlockSpec(memory_space=pltpu.VMEM)
    rows = pl.BlockSpec((512, 1024), lambda i: (i, 0))
    whole = pl.BlockSpec((256, D), lambda i: (0, 0))
    return pl.pallas_call(
        body, name="adamw", grid=(2,),
        in_specs=[rows] * 4 + [whole] * 4 + [vm] * 16, out_specs=[rows] * 3 + [whole] * 3 + [vm] * 21,
        out_shape=shapes,
        compiler_params=_cp(("arbitrary",)),
    )(*big_in, *big_out, sall, *flat)


def kernel(x, positions, w_in, w_out, mix_norm_w, attn_out_norm_w, hgrn_out_norm_w, hgrn_lb_raw, final_norm_w, loss_target, m_w_in, m_w_out, m_mix_norm_w, m_attn_out_norm_w, m_hgrn_out_norm_w, m_hgrn_lb_raw, m_final_norm_w, v_w_in, v_w_out, v_mix_norm_w, v_attn_out_norm_w, v_hgrn_out_norm_w, v_hgrn_lb_raw, v_final_norm_w):
    xs = x.reshape(T, D)
    tgt = loss_target.reshape(T, D)
    pos = positions.reshape(1, T)
    fnw = final_norm_w.reshape(1, D)

    ti = np.arange(TH)
    tri_np = ((ti[:, None] // CHUNK == ti[None, :] // CHUNK) & (ti[None, :] <= ti[:, None])).astype(np.float32)
    tri = jnp.asarray(tri_np, BF16)
    trit = jnp.asarray(tri_np.T, BF16)
    hi_ = np.arange(AW) // HEAD
    gmat = jnp.asarray((hi_[:256, None] == hi_[None, :256]).astype(np.float32) / HEAD, BF16)
    emat_np = (np.arange(128)[:, None] == hi_[None, :]).astype(np.float32)
    sel_np = (8 + hi_[:, None] == np.arange(128)[None, :]).astype(np.float32)
    emat = jnp.asarray(emat_np, BF16)
    selmat = jnp.asarray(sel_np, BF16)

    jm_arr = (2 * lax.axis_index("x") + lax.axis_index("y")).astype(jnp.int32).reshape(1)
    (hn, q1, k1, v1, q4, k4, v4, q16, k16, v16, ag, hq, hf, hi, hg, w_full, wout4) = _fwd_in(
        xs, pos, mix_norm_w, w_in.reshape(D, 1024), w_out.reshape(256, D), jm_arr)
    wout_full = wout4.reshape(D, D)
    flat = lambda a: a.reshape(T, AW)
    o1, l1 = _attn_fwd(q1, k1, v1, T // BLK, "attn_fwd_d1")
    o4, l4 = _attn_fwd(flat(q4), flat(k4), flat(v4), T // 4 // BLK, "attn_fwd_d4")
    o16, l16 = _attn_fwd(flat(q16), flat(k16), flat(v16), T // 16 // BLK, "attn_fwd_d16")
    rec, sall = _hgrn_fwd(hq, hf, hi, hgrn_lb_raw, tri)

    (dx2, do1, do4, do16, st1, st4, st16, drec, dag, dhg, rout, routb, small4) = _fwd_out(
        o1, o4.reshape(4, T // 4, AW), o16.reshape(16, T // 16, AW),
        l1, l4.reshape(4, T // 4, 128), l16.reshape(16, T // 16, 128),
        rec, ag, hg, xs, tgt, attn_out_norm_w, hgrn_out_norm_w, fnw, wout_full, gmat, emat, selmat)

    fst = lambda a: a.reshape(T, 128)
    dq1, dk1, dv1 = _attn_bwd(q1, k1, v1, do1, st1, T // BLK, "attn_bwd_d1")
    dq4, dk4, dv4 = _attn_bwd(flat(q4), flat(k4), flat(v4), flat(do4), fst(st4), T // 4 // BLK, "attn_bwd_d4")
    dq16, dk16, dv16 = _attn_bwd(flat(q16), flat(k16), flat(v16), flat(do16), fst(st16), T // 16 // BLK,
                                 "attn_bwd_d16")
    dproj_h, small6, pout_own, pout_rem = _hgrn_bwd(hq, hf, hi, hgrn_lb_raw, tri, trit, drec, sall, dhg,
                                                    rout, routb)

    r4 = lambda a: a.reshape(4, T // 4, AW)
    r16 = lambda a: a.reshape(16, T // 16, AW)
    dproj_a = _dproj_build((dq1, r4(dq4), r16(dq16)), (dk1, r4(dk4), r16(dk16)), (dv1, r4(dv4), r16(dv16)),
                           dag, pos)
    rin, rinb = _grad_w_in(hn, dproj_a, dproj_h)
    gx, small_all, fin, fout = _bwd_x(dproj_a, dproj_h, xs, dx2, mix_norm_w, w_full, rin, rinb,
                                            small4, small6, pout_own, pout_rem)
    g_w_in = fin.reshape(D, 1024)
    g_w_out = fout.reshape(256, D)

    params = [(mix_norm_w, m_mix_norm_w, v_mix_norm_w),
              (attn_out_norm_w, m_attn_out_norm_w, v_attn_out_norm_w),
              (hgrn_out_norm_w, m_hgrn_out_norm_w, v_hgrn_out_norm_w),
              (hgrn_lb_raw, m_hgrn_lb_raw, v_hgrn_lb_raw),
              (fnw, m_final_norm_w.reshape(1, D), v_final_norm_w.reshape(1, D))]
    d_in, nm_in, nv_in, d_out, nm_out, nv_out, *so = _adamw(
        (w_in.reshape(D, 1024), g_w_in, m_w_in.reshape(D, 1024), v_w_in.reshape(D, 1024)),
        (w_out.reshape(256, D), g_w_out, m_w_out.reshape(256, D), v_w_out.reshape(256, D)), small_all, params)
    loss = so[0].reshape(())
    g_s = [so[1 + 4 * p] for p in range(5)]
    d_s = [so[2 + 4 * p] for p in range(5)]
    m_s = [so[3 + 4 * p] for p in range(5)]
    v_s = [so[4 + 4 * p] for p in range(5)]
    for lst in (g_s, d_s, m_s, v_s):
        lst[4] = lst[4].reshape(D)

    return (loss, gx.reshape(1, T, D),
            g_w_in.reshape(1, D, 1024), g_w_out.reshape(1, 256, D), *g_s,
            d_in.reshape(1, D, 1024), d_out.reshape(1, 256, D), *d_s,
            nm_in.reshape(1, D, 1024), nm_out.reshape(1, 256, D), *m_s,
            nv_in.reshape(1, D, 1024), nv_out.reshape(1, 256, D), *v_s)
```

```python
import functools

import numpy as np
import jax
import jax.numpy as jnp
from jax import lax
from jax.experimental import pallas as pl
from jax.experimental.pallas import tpu as pltpu

F32 = jnp.float32
BF16 = jnp.bfloat16

T = 4096
D = 1024
AW = 512
HW = 512
NCOL = 4096
HEAD = 64
BLK = 128
CHUNK = 64
EPS = 1e-6
SCALE = HEAD ** -0.5
NEG = -1e30
ROPE_THETA = 500000.0
INV_FREQ = [float(v) for v in
            (np.float32(ROPE_THETA) ** (-(np.arange(8, dtype=np.float32)) * np.float32(0.125)))]
LR, B1, B2, AEPS, WD, STEP = 0.001, 0.9, 0.999, 1e-08, 0.01, 10
VMEM_LIMIT = 63 * 1024 * 1024
MESH = pl.DeviceIdType.MESH


def _cp(sem=None, **kw):
    return pltpu.CompilerParams(dimension_semantics=sem, vmem_limit_bytes=VMEM_LIMIT, **kw)


def _mm(a, b):
    return jnp.dot(a, b, preferred_element_type=F32)


def _mm_nt(a, b):
    return lax.dot_general(a, b, (((1,), (1,)), ((), ())), preferred_element_type=F32)


def _mm_tn(a, b):
    return lax.dot_general(a, b, (((0,), (0,)), ((), ())), preferred_element_type=F32)


def _mm_exact_l(mat_bf, x):
    h = x.astype(BF16)
    l = (x - h.astype(F32)).astype(BF16)
    return _mm(mat_bf, h) + _mm(mat_bf, l)


def _mm_exact_r(x, mat_bf):
    h = x.astype(BF16)
    l = (x - h.astype(F32)).astype(BF16)
    return _mm(h, mat_bf) + _mm(l, mat_bf)


def _sigmoid(x):
    return 0.5 * jnp.tanh(0.5 * x) + 0.5


def _rope_tables(pos):
    lane = lax.broadcasted_iota(jnp.int32, (1, 128), 1)
    jl = lane & 63
    fi = jl & 7
    inv = jnp.zeros((1, 128), F32)
    for kk in range(8):
        inv = jnp.where(fi == kk, INV_FREQ[kk], inv)
    ang = jnp.broadcast_to(pos.astype(F32), (128, pos.shape[1])).T * inv
    c = jnp.cos(ang)
    s = jnp.sin(ang)
    cosf = jnp.where(jl < 16, c, 1.0)
    s1 = jnp.where(jl < 8, -s, 0.0)
    s2 = jnp.where((jl >= 8) & (jl < 16), s, 0.0)
    return cosf, s1, s2


def _rope(t, cosf, s1, s2):
    parts = []
    for ci in range(t.shape[1] // 128):
        tc = t[:, ci * 128:(ci + 1) * 128]
        parts.append(tc * cosf + pltpu.roll(tc, 120, 1) * s1 + pltpu.roll(tc, 8, 1) * s2)
    return jnp.concatenate(parts, axis=1)


def _rope_bwd(g, cosf, s1, s2):
    parts = []
    for ci in range(g.shape[1] // 128):
        gc = g[:, ci * 128:(ci + 1) * 128]
        parts.append(gc * cosf + pltpu.roll(gc * s1, 8, 1) + pltpu.roll(gc * s2, 120, 1))
    return jnp.concatenate(parts, axis=1)


def _perm_store(val, scr, scr2, o1, o4, o16, dt):
    n = val.shape[0]
    q = n // 4
    o1[...] = val.astype(dt)
    for ci in range(val.shape[1] // 128):
        cs = slice(ci * 128, (ci + 1) * 128)
        scr[ci] = val[:, cs]
        for r4 in range(4):
            part = scr[ci, pl.ds(r4, q, stride=4), :]
            o4[r4, :, cs] = part.astype(dt)
            scr2[ci, r4 * q:(r4 + 1) * q, :] = part
        for r4 in range(4):
            for b in range(4):
                o16[r4 + 4 * b, :, cs] = scr2[ci, pl.ds(r4 * q + b, q // 4, stride=4), :].astype(dt)


def _unperm_load(r4, r16, scr_a, scr_b, scr_c):
    n = scr_a.shape[1]
    q = n // 4
    nc = r4.shape[-1] // 128
    for ci in range(nc):
        cs = slice(ci * 128, (ci + 1) * 128)
        for rr in range(4):
            scr_a[ci, pl.ds(rr, q, stride=4), :] = r4[rr, :, cs].astype(F32)
        for rr in range(4):
            for b in range(4):
                scr_c[ci, pl.ds(rr * q + b, q // 4, stride=4), :] = r16[rr + 4 * b, :, cs].astype(F32)
        for rr in range(4):
            scr_b[ci, pl.ds(rr, q, stride=4), :] = scr_c[ci, rr * q:(rr + 1) * q, :]
    return (jnp.concatenate([scr_a[ci] for ci in range(nc)], axis=1),
            jnp.concatenate([scr_b[ci] for ci in range(nc)], axis=1))


def _unperm_sum(r4, r16, scr_b, scr_c):
    n = scr_b.shape[1]
    q = n // 4
    nc = r4.shape[-1] // 128
    for ci in range(nc):
        cs = slice(ci * 128, (ci + 1) * 128)
        for rr in range(4):
            for b in range(4):
                scr_c[ci, pl.ds(rr * q + b, q // 4, stride=4), :] = r16[rr + 4 * b, :, cs].astype(F32)
        for rr in range(4):
            scr_b[ci, pl.ds(rr, q, stride=4), :] = scr_c[ci, rr * q:(rr + 1) * q, :] + r4[rr, :, cs].astype(F32)
    return jnp.concatenate([scr_b[ci] for ci in range(nc)], axis=1)


def _fwd_in(x, pos, mixw, w_in, w_out, jm_arr):
    TT = 512
    NT = T // TT

    def body(jm_ref, x_ref, pos_ref, mw_ref, win_ref, wout_ref,
             hnt_ref, q1, k1, v1, q4, k4, v4, q16, k16, v16, ag, hq, hf, hi, hg, wfull_o, woutfull_o,
             wbuf, wobuf, hn_all, scr, scr2, stage, send_sems, recv_sems, loc_sems):
        s = pl.program_id(0)
        i = pl.program_id(1)
        mx, my, c = lax.axis_index("x"), lax.axis_index("y"), lax.axis_index("c")
        me, sibling = (mx, my, c), (mx, my, 1 - c)
        chips = [(mx, 1 - my), (1 - mx, my), (1 - mx, 1 - my)]
        jm = 2 * mx + my
        rows_in = [pl.ds(pl.multiple_of(h * 512, 512), 512) for h in (c, 1 - c)]
        rows_out = [pl.ds(pl.multiple_of(h * 128, 128), 128) for h in (c, 1 - c)]

        def blk(k):
            return lax.bitwise_xor(jm, k + 1)

        def rc(n, ref, to):
            return pltpu.make_async_remote_copy(src_ref=ref, dst_ref=ref, send_sem=send_sems.at[n],
                                                recv_sem=recv_sems.at[n], device_id=to, device_id_type=MESH)

        halves = [pl.ds(0, 512), pl.ds(512, 512)]
        send_in = lambda k, h: rc(12 + 2 * k + h, wbuf.at[jm, rows_in[0], halves[h]], (*chips[k], c))
        got_in = lambda k, h: rc(12 + 2 * k + h, wbuf.at[blk(k), rows_in[0], halves[h]], me)
        relay = lambda h: rc(16 + h, wbuf.at[blk(h), rows_in[0], halves[h]], (*chips[1 - h], c))
        got_relay = lambda h: rc(16 + h, wbuf.at[blk(2), rows_in[0], halves[h]], me)
        send_out = lambda k: rc(3 + k, wobuf.at[jm, rows_out[0], :], (*chips[k], c))
        got_out = lambda k: rc(3 + k, wobuf.at[blk(k), rows_out[0], :], me)
        pass_in = lambda k: rc(6 + k, wbuf.at[blk(k), rows_in[0], :], sibling)
        pass_out = lambda k: rc(9 + k, wobuf.at[blk(k), rows_out[0], :], sibling)
        passed_in = lambda k: rc(6 + k, wbuf.at[blk(k), rows_in[1], :], me)
        passed_out = lambda k: rc(9 + k, wobuf.at[blk(k), rows_out[1], :], me)

        def keep(j, n):
            return pltpu.make_async_copy(wbuf.at[j], wfull_o.at[:, pl.ds(j * 1024, 1024)], loc_sems.at[n])

        @pl.when((s == 0) & (i == 0))
        def _():
            chunk = [pl.ds(pl.multiple_of(lax.rem(p + 2 * c, 4) * 256, 256), 256) for p in range(4)]
            loads = [pltpu.make_async_copy(win_ref.at[chunk[p], :] if p < 4 else wout_ref, stage.at[p % 2],
                                           loc_sems.at[4 + p % 2]) for p in range(5)]
            loads[0].start()
            for p in range(5):
                if p < 4:
                    loads[p + 1].start()
                loads[p].wait()
                if p < 4:
                    wbuf[jm, chunk[p], :] = stage[p % 2].astype(BF16)
                else:
                    wobuf[jm] = stage[p % 2].astype(BF16)
                if p == 1:
                    for k in range(2):
                        for h in range(2):
                            send_in(k, h).start()
            keep(jm, 0).start()

        @pl.when((s == 0) & (i == NT - 1))
        def _():
            for kk in range(2):
                for h in range(2):
                    got_in(kk, h).wait_recv()
            relay(0).start()
            relay(1).start()
            pass_in(0).start()
            pass_in(1).start()
            passed_in(0).wait_recv()
            keep(blk(0), 1).start()

        @pl.when((s == 1) & (i == NT - 1))
        def _():
            got_relay(0).wait_recv()
            got_relay(1).wait_recv()
            pass_in(2).start()

        @pl.when((s == 2) & (i == 0))
        def _():
            for k in (1, 2):
                passed_in(k).wait_recv()
                keep(blk(k), k + 1).start()
            for kk in range(3):
                send_out(kk).start()

        @pl.when((s == 2) & (i == NT - 2))
        def _():
            for k in range(3):
                got_out(k).wait_recv()
                pass_out(k).start()

        whole_out = pltpu.make_async_copy(wobuf, woutfull_o, loc_sems.at[4])

        @pl.when((s == 2) & (i == NT - 1))
        def _():
            for k in range(3):
                passed_out(k).wait_recv()
            whole_out.start()

        tile = pl.ds(pl.multiple_of(i * TT, TT), TT)

        @pl.when(s == 0)
        def _():
            xv = x_ref[...]
            r = lax.rsqrt(jnp.mean(xv * xv, axis=-1, keepdims=True) + EPS)
            hnf = (xv * r) * mw_ref[...]
            hn_all[tile, :] = hnf.astype(BF16)
            hnt_ref[...] = hnf.T.astype(BF16)

        def project(jj):
            hn = hn_all[tile, :]
            lo = _mm(hn, wbuf[jj, :, 0:512])
            hi_cols = _mm(hn, wbuf[jj, :, 512:1024])
            if jj == 0:
                cosf, s1, s2 = _rope_tables(pos_ref[...])
                _perm_store(_rope(lo, cosf, s1, s2) * SCALE, scr, scr2, q1, q4, q16, BF16)
                _perm_store(_rope(hi_cols, cosf, s1, s2), scr, scr2, k1, k4, k16, BF16)
            elif jj == 1:
                _perm_store(lo, scr, scr2, v1, v4, v16, BF16)
                ag[...] = hi_cols.astype(BF16)
            elif jj == 2:
                hq[...] = lo.astype(BF16)
                hf[...] = hi_cols.astype(BF16)
            else:
                hi[...] = lo.astype(BF16)
                hg[...] = hi_cols.astype(BF16)

        def project_block(j):
            for jj in range(4):
                pl.when(j == jj)(functools.partial(project, jj))

        @pl.when(s < 2)
        def _():
            project_block(lax.bitwise_xor(jm, s))

        @pl.when(s == 2)
        def _():
            project_block(lax.bitwise_xor(jm, 2))
            project_block(lax.bitwise_xor(jm, 3))

        @pl.when((s == 2) & (i == NT - 1))
        def _():
            for h in range(2):
                relay(h).wait_send()
                for k in range(2):
                    send_in(k, h).wait_send()
            for k in range(3):
                send_out(k).wait_send()
                pass_in(k).wait_send()
                pass_out(k).wait_send()
            keep(jm, 0).wait()
            for k in range(3):
                keep(blk(k), k + 1).wait()
            whole_out.wait()

    def at_stage_of(jb):
        def index(s, i, jm_ref):
            sa = jnp.minimum(lax.bitwise_xor(jm_ref[0], jb), 2)
            return jnp.where(s < sa, 0, jnp.where(s == sa, i, NT - 1))
        return index

    tok = lambda w, jb: pl.BlockSpec((TT, w), lambda s, i, jm_ref: (at_stage_of(jb)(s, i, jm_ref), 0))
    d4 = lambda jb: pl.BlockSpec((4, TT // 4, AW), lambda s, i, jm_ref: (0, at_stage_of(jb)(s, i, jm_ref), 0))
    d16 = lambda jb: pl.BlockSpec((16, TT // 16, AW), lambda s, i, jm_ref: (0, at_stage_of(jb)(s, i, jm_ref), 0))
    hbm = pl.BlockSpec(memory_space=pltpu.HBM)
    sd = lambda shape, dt: jax.ShapeDtypeStruct(shape, dt)
    in_own_stage = lambda s, i: jnp.where(s == 0, i, NT - 1)
    grid_spec = pltpu.PrefetchScalarGridSpec(
        num_scalar_prefetch=1, grid=(3, NT),
        in_specs=[pl.BlockSpec((TT, D), lambda s, i, jm_ref: (in_own_stage(s, i), 0)),
                  pl.BlockSpec((1, TT), lambda s, i, jm_ref: (0, i)),
                  pl.BlockSpec((1, D), lambda s, i, jm_ref: (0, 0)), hbm, hbm],
        out_specs=[pl.BlockSpec((D, TT), lambda s, i, jm_ref: (0, in_own_stage(s, i))),
                   tok(AW, 0), tok(AW, 0), tok(AW, 1), d4(0), d4(0), d4(1), d16(0), d16(0), d16(1),
                   tok(AW, 1), tok(AW, 2), tok(AW, 2), tok(AW, 3), tok(AW, 3), hbm, hbm],
        scratch_shapes=[pltpu.VMEM((4, D, 1024), BF16), pltpu.VMEM((4, 256, D), BF16), pltpu.VMEM((T, D), BF16),
                        pltpu.VMEM((4, TT, 128), F32), pltpu.VMEM((4, TT, 128), F32), pltpu.VMEM((2, 256, 1024), F32),
                        pltpu.SemaphoreType.DMA((18,)),
                        pltpu.SemaphoreType.DMA((18,)), pltpu.SemaphoreType.DMA((6,))])
    return pl.pallas_call(
        body, name="fwd_in", grid_spec=grid_spec,
        out_shape=[sd((D, T), BF16)] + [sd((T, AW), BF16)] * 3 + [sd((4, T // 4, AW), BF16)] * 3
        + [sd((16, T // 16, AW), BF16)] * 3
        + [sd((T, AW), BF16)] * 5 + [sd((D, NCOL), BF16), sd((4, 256, D), BF16)],
        compiler_params=_cp(("arbitrary", "arbitrary")),
    )(jm_arr, x, pos, mixw, w_in, w_out)


def _band_mask(key_axis, nkeys=2 * BLK):
    shape = (nkeys, 2 * BLK) if key_axis == 0 else (2 * BLK, nkeys)
    kj = lax.broadcasted_iota(jnp.int32, shape, key_axis)
    qi = lax.broadcasted_iota(jnp.int32, shape, 1 - key_axis) & (BLK - 1)
    return (kj >= qi) & (kj <= qi + BLK), kj, qi


def _stack_heads(t2, in_a):
    z = jnp.zeros_like(t2)
    return jnp.concatenate([jnp.where(in_a[0], t2, z), jnp.where(in_a[1], t2, z)], axis=0)


def _attn_fwd(q, k, v, nb, name):
    n = 8
    CH = n * BLK
    halo = nb > n

    def body(*refs):
        if halo:
            q_ref, k_ref, v_ref, kp_ref, vp_ref, o_ref, lse_ref = refs
        else:
            q_ref, k_ref, v_ref, o_ref, lse_ref = refs
        lane = lax.broadcasted_iota(jnp.int32, (1, 128), 1)
        in_a = [lane < HEAD, lane >= HEAD]
        band, kj, _ = _band_mask(1)
        thr0 = jnp.where((n * pl.program_id(0)) % nb == 0, BLK, 0) if halo else BLK
        mask0 = band & (kj >= thr0)
        mask_first = band & (kj >= BLK)
        for b in range(n):
            rs = slice(b * BLK, (b + 1) * BLK)
            stat = jnp.zeros((BLK, 128), F32)
            for hp in range(4):
                cs = slice(hp * 128, (hp + 1) * 128)
                q2s = _stack_heads(q_ref[rs, cs], in_a)
                if b == 0:
                    kprev = kp_ref[:, cs] if halo else k_ref[rs, cs]
                    vprev = vp_ref[:, cs] if halo else v_ref[rs, cs]
                    kk = jnp.concatenate([kprev, k_ref[rs, cs]], axis=0)
                    vv = jnp.concatenate([vprev, v_ref[rs, cs]], axis=0)
                    mask = mask0
                else:
                    kk = k_ref[(b - 1) * BLK:(b + 1) * BLK, cs]
                    vv = v_ref[(b - 1) * BLK:(b + 1) * BLK, cs]
                    mask = mask_first if b % nb == 0 else band
                s = jnp.where(mask, _mm_nt(q2s, kk), NEG)
                m = jnp.max(s, axis=-1, keepdims=True)
                p = jnp.exp(s - m)
                l = jnp.sum(p, axis=-1, keepdims=True)
                o = _mm(p.astype(BF16), vv) / l
                lse = m + jnp.log(l)
                o_ref[rs, cs] = jnp.where(in_a[0], o[:BLK], o[BLK:]).astype(BF16)
                stat = jnp.where(lane == 2 * hp, lse[:BLK], stat)
                stat = jnp.where(lane == 2 * hp + 1, lse[BLK:], stat)
            lse_ref[rs, :] = stat

    cur = pl.BlockSpec((CH, AW), lambda i: (i, 0))
    prev = pl.BlockSpec((BLK, AW), lambda i: (jnp.maximum(n * i - 1, 0), 0))
    return pl.pallas_call(
        body, name=name, grid=(T // CH,),
        in_specs=[cur, cur, cur] + ([prev, prev] if halo else []),
        out_specs=[cur, pl.BlockSpec((CH, 128), lambda i: (i, 0))],
        out_shape=[jax.ShapeDtypeStruct((T, AW), BF16), jax.ShapeDtypeStruct((T, 128), F32)],
        compiler_params=_cp(("parallel",)),
    )(*((q, k, v) + ((k, v) if halo else ())))


def _attn_bwd(q, k, v, do, st, nb, name):
    n = 8
    CH = n * BLK
    NBLK = T // BLK
    halo = nb > n

    def body(*refs):
        if halo:
            (q_ref, k_ref, v_ref, do_ref, st_ref, kp_ref, vp_ref, qn_ref, don_ref, stn_ref,
             dq_ref, dk_ref, dv_ref) = refs
        else:
            q_ref, k_ref, v_ref, do_ref, st_ref, dq_ref, dk_ref, dv_ref = refs
        i = pl.program_id(0)
        lane = lax.broadcasted_iota(jnp.int32, (1, 128), 1)
        in_a = [lane < HEAD, lane >= HEAD]
        band, kj, _ = _band_mask(0)
        thr0 = jnp.where((n * i) % nb == 0, BLK, 0) if halo else BLK
        mask0 = band & (kj >= thr0)
        mask_first = band & (kj >= BLK)

        def stat_rows(st_t, hp):
            lse_r = jnp.concatenate([st_t[2 * hp:2 * hp + 1, :], st_t[2 * hp + 1:2 * hp + 2, :]], axis=1)
            dl_r = jnp.concatenate([st_t[8 + 2 * hp:9 + 2 * hp, :], st_t[9 + 2 * hp:10 + 2 * hp, :]], axis=1)
            return lse_r, dl_r

        st_t = [st_ref[b * BLK:(b + 1) * BLK, :].T for b in range(n)]
        if halo:
            nxt_thr = jnp.where((n * i + n) % nb == 0, 2 * BLK, 0)
            _, kj1, qi1 = _band_mask(0, BLK)
            mask_next = kj1 >= qi1 + nxt_thr
            stn_t = stn_ref[...].T

        for hp in range(4):
            cs = slice(hp * 128, (hp + 1) * 128)
            kb = [k_ref[b * BLK:(b + 1) * BLK, cs] for b in range(n)]
            vb = [v_ref[b * BLK:(b + 1) * BLK, cs] for b in range(n)]
            dk_acc = [jnp.zeros((BLK, 128), F32) for _ in range(n)]
            dv_acc = [jnp.zeros((BLK, 128), F32) for _ in range(n)]
            for b in range(n):
                rs = slice(b * BLK, (b + 1) * BLK)
                q2s = _stack_heads(q_ref[rs, cs], in_a)
                do2s = _stack_heads(do_ref[rs, cs], in_a)
                if b == 0:
                    kprev = kp_ref[:, cs] if halo else kb[0]
                    vprev = vp_ref[:, cs] if halo else vb[0]
                    mask = mask0
                else:
                    kprev, vprev, mask = kb[b - 1], vb[b - 1], (mask_first if b % nb == 0 else band)
                kk = jnp.concatenate([kprev, kb[b]], axis=0)
                vv = jnp.concatenate([vprev, vb[b]], axis=0)
                lse_r, dl_r = stat_rows(st_t[b], hp)
                s_t = jnp.where(mask, _mm_nt(kk, q2s), NEG)
                p_t = jnp.exp(s_t - lse_r)
                ds_t = (p_t * (_mm_nt(vv, do2s) - dl_r)).astype(BF16)
                dkk = _mm(ds_t, q2s)
                dvv = _mm(p_t.astype(BF16), do2s)
                dqs = _mm_tn(ds_t, kk) * SCALE
                dq_ref[rs, cs] = jnp.where(in_a[0], dqs[:BLK], dqs[BLK:]).astype(BF16)
                dk_acc[b] += dkk[BLK:]
                dv_acc[b] += dvv[BLK:]
                if b > 0:
                    dk_acc[b - 1] += dkk[:BLK]
                    dv_acc[b - 1] += dvv[:BLK]
            if halo:
                q2s = _stack_heads(qn_ref[:, cs], in_a)
                do2s = _stack_heads(don_ref[:, cs], in_a)
                lse_r, dl_r = stat_rows(stn_t, hp)
                s_t = jnp.where(mask_next, _mm_nt(kb[n - 1], q2s), NEG)
                p_t = jnp.exp(s_t - lse_r)
                ds_t = (p_t * (_mm_nt(vb[n - 1], do2s) - dl_r)).astype(BF16)
                dk_acc[n - 1] += _mm(ds_t, q2s)
                dv_acc[n - 1] += _mm(p_t.astype(BF16), do2s)
            for b in range(n):
                dk_ref[b * BLK:(b + 1) * BLK, cs] = dk_acc[b].astype(BF16)
                dv_ref[b * BLK:(b + 1) * BLK, cs] = dv_acc[b].astype(BF16)

    cur = pl.BlockSpec((CH, AW), lambda i: (i, 0))
    cur_st = pl.BlockSpec((CH, 128), lambda i: (i, 0))
    prev = pl.BlockSpec((BLK, AW), lambda i: (jnp.maximum(n * i - 1, 0), 0))
    nxt = pl.BlockSpec((BLK, AW), lambda i: (jnp.minimum(n * i + n, NBLK - 1), 0))
    nxt_st = pl.BlockSpec((BLK, 128), lambda i: (jnp.minimum(n * i + n, NBLK - 1), 0))
    ins = [cur] * 4 + [cur_st] + ([prev, prev, nxt, nxt, nxt_st] if halo else [])
    args = (q, k, v, do, st) + ((k, v, q, do, st) if halo else ())
    return pl.pallas_call(
        body, name=name, grid=(T // CH,),
        in_specs=ins,
        out_specs=[cur] * 3,
        out_shape=[jax.ShapeDtypeStruct((T, AW), BF16)] * 3,
        compiler_params=_cp(("parallel",)),
    )(*args)


TH = 256
NCH = TH // CHUNK


def _hgrn_common(hq_ref, hf_ref, lbr_ref, tri_ref):
    r0 = lbr_ref[0:1, :]
    r1 = lbr_ref[1:2, :]
    mx = jnp.maximum(r0, r1)
    e0 = jnp.exp(r0 - mx)
    e1 = jnp.exp(r1 - mx)
    lb = e0 / (e0 + e1)
    hqv = hq_ref[...].astype(F32)
    sq = _sigmoid(hqv)
    qv = hqv * sq
    sf = _sigmoid(hf_ref[...].astype(F32))
    f = lb + (1.0 - lb) * sf
    kv = 1.0 - f
    g = jnp.log(f)
    cum = _mm_exact_l(tri_ref[...], g)
    dec = jnp.exp(jnp.concatenate([cum[c * CHUNK + CHUNK - 1:(c + 1) * CHUNK, :] for c in range(NCH)], axis=0))
    decb = jnp.concatenate([jnp.broadcast_to(dec[c:c + 1, :], (CHUNK, HW)) for c in range(NCH)], axis=0)
    ea = jnp.exp(cum)
    ena = jnp.exp(-cum)
    eend = decb * ena
    return dict(lb=lb, hq=hqv, sq=sq, q=qv, sf=sf, f=f, k=kv, cum=cum, ea=ea, ena=ena, eend=eend,
                qd=qv * ea, ki=kv * ena, ke=kv * eend, dec=dec)


def _tri_mask(transposed=False):
    ti = lax.broadcasted_iota(jnp.int32, (TH, TH), 1 if transposed else 0)
    si = lax.broadcasted_iota(jnp.int32, (TH, TH), 0 if transposed else 1)
    return (si <= ti) & ((si // CHUNK) == (ti // CHUNK))


def _hgrn_fwd(hq, hf, hi, lbr, tri):
    NSUB = 2

    def body(hq_ref, hf_ref, hi_ref, lbr_ref, tri_ref, rec_ref, sall_ref, st_scr):
        @pl.when(pl.program_id(0) == 0)
        def _():
            st_scr[...] = jnp.zeros_like(st_scr)

        causal = _tri_mask()
        for u in range(NSUB):
            tile = slice(u * TH, (u + 1) * TH)
            w = _hgrn_common(hq_ref.at[tile, :], hf_ref.at[tile, :], lbr_ref, tri_ref)
            qd, ki, ke = w["qd"].astype(BF16), w["ki"].astype(BF16), w["ke"].astype(BF16)
            dec = w["dec"]
            vb = hi_ref[tile, :]
            for h in range(4):
                cs = slice(h * 128, (h + 1) * 128)
                att = jnp.where(causal, _mm_nt(qd[:, cs], ki[:, cs]), 0.0)
                o_intra = _mm(att.astype(BF16), vb[:, cs])
                st = st_scr[:, cs]
                for c in range(NCH):
                    rs = slice(c * CHUNK, (c + 1) * CHUNK)
                    sall_ref[u * NCH + c, :, cs] = st
                    rec_ref[u * TH + c * CHUNK:u * TH + (c + 1) * CHUNK, cs] = (
                        o_intra[rs] + _mm_nt(qd[rs, cs], st.astype(BF16))).astype(BF16)
                    st = dec[c:c + 1, cs] * st + _mm_tn(vb[rs, cs], ke[rs, cs])
                st_scr[:, cs] = st

    tok = pl.BlockSpec((NSUB * TH, HW), lambda i: (i, 0))
    return pl.pallas_call(
        body, name="hgrn_fwd", grid=(T // (NSUB * TH),),
        in_specs=[tok, tok, tok, pl.BlockSpec((2, HW), lambda i: (0, 0)), pl.BlockSpec((TH, TH), lambda i: (0, 0))],
        out_specs=[tok, pl.BlockSpec((NSUB * NCH, 128, HW), lambda i: (i, 0, 0))],
        out_shape=[jax.ShapeDtypeStruct((T, HW), BF16), jax.ShapeDtypeStruct((T // CHUNK, 128, HW), F32)],
        scratch_shapes=[pltpu.VMEM((128, HW), F32)],
        compiler_params=_cp(("arbitrary",)),
    )(hq, hf, hi, lbr, tri)


def _hgrn_bwd(hq, hf, hi, lbr, tri, trit, drec, sall, dhg, rout, routb):
    NSUB = 2
    NT = T // (NSUB * TH)

    def body(hq_ref, hf_ref, hi_ref, lbr_ref, tri_ref, trit_ref, do_ref, sall_ref, dhg_ref, rout_r, routb_r,
             dph_ref, small_ref, pout_o, poutr_o,
             dst_scr, dlb_scr, dqd_scr, dki_scr, dke_scr, dlast_scr, send_sems, recv_sems, loc_sems):
        step = pl.program_id(0)
        loc, rem = _chip_copies(_w_out_piece, rout_r, routb_r, pout_o, poutr_o, send_sems, recv_sems,
                                loc_sems.at[0])

        @pl.when(step == 0)
        def _():
            dst_scr[...] = jnp.zeros_like(dst_scr)
            dlb_scr[...] = jnp.zeros_like(dlb_scr)
            for cp in loc + rem:
                cp.start()

        causal = _tri_mask()
        causal_t = _tri_mask(transposed=True)
        lb = None
        for u in reversed(range(NSUB)):
            tile = slice(u * TH, (u + 1) * TH)
            w = _hgrn_common(hq_ref.at[tile, :], hf_ref.at[tile, :], lbr_ref, tri_ref)
            qd, ki, ke = w["qd"].astype(BF16), w["ki"].astype(BF16), w["ke"].astype(BF16)
            dec = w["dec"]
            vb = hi_ref[tile, :]
            dob = do_ref[tile, :].astype(BF16)
            for h in range(4):
                cs = slice(h * 128, (h + 1) * 128)
                att_t = jnp.where(causal_t, _mm_nt(ki[:, cs], qd[:, cs]), 0.0).astype(BF16)
                datt_t = jnp.where(causal_t, _mm_nt(vb[:, cs], dob[:, cs]), 0.0).astype(BF16)
                datt = jnp.where(causal, _mm_nt(dob[:, cs], vb[:, cs]), 0.0).astype(BF16)
                dv_intra = _mm(att_t, dob[:, cs])
                dqd_intra = _mm(datt, ki[:, cs])
                dki_scr[u, :, cs] = _mm(datt_t, qd[:, cs])
                dst = dst_scr[:, cs]
                for c in reversed(range(NCH)):
                    rs = slice(c * CHUNK, (c + 1) * CHUNK)
                    dec_c = dec[c:c + 1, :]
                    st = sall_ref[u * NCH + c, :, cs]
                    dstb = dst.astype(BF16)
                    dph_ref[u * TH + c * CHUNK:u * TH + (c + 1) * CHUNK, 2 * HW + h * 128:2 * HW + (h + 1) * 128] = (
                        dv_intra[rs] + _mm_nt(ke[rs, cs], dstb)).astype(BF16)
                    dqd_scr[u, rs, cs] = dqd_intra[rs] + _mm(dob[rs, cs], st.astype(BF16))
                    dke_scr[u, rs, cs] = _mm(vb[rs, cs], dstb)
                    ddec = jnp.sum(dst * st, axis=0, keepdims=True)
                    dlast_scr[u, c:c + 1, cs] = ddec * dec_c[:, cs]
                    dst = dec_c[:, cs] * dst + _mm_tn(dob[rs, cs], qd[rs, cs])
                dst_scr[:, cs] = dst
            dqd, dki, dke = dqd_scr[u], dki_scr[u], dke_scr[u]
            dq = dqd * w["ea"]
            dk = dki * w["ena"] + dke * w["eend"]
            dcum = dqd * w["qd"] - dki * w["ki"] - dke * w["ke"]
            dkeke = dke * w["ke"]
            dlastb = jnp.concatenate(
                [jnp.broadcast_to(dlast_scr[u, c:c + 1, :]
                                  + jnp.sum(dkeke[c * CHUNK:(c + 1) * CHUNK], axis=0, keepdims=True), (CHUNK, HW))
                 for c in range(NCH)], axis=0)
            dg = _mm_exact_l(trit_ref[...], dcum) + dlastb
            df = dg / w["f"] - dk
            lb, sf, sq = w["lb"], w["sf"], w["sq"]
            dph_ref[tile, HW:2 * HW] = (df * (1.0 - lb) * sf * (1.0 - sf)).astype(BF16)
            dph_ref[tile, 0:HW] = (dq * (sq * (1.0 + w["hq"] * (1.0 - sq)))).astype(BF16)
            dph_ref[tile, 3 * HW:4 * HW] = dhg_ref[tile, :]
            dlb_scr[...] += jnp.sum(df * (1.0 - sf), axis=0, keepdims=True)

        @pl.when(step == NT - 1)
        def _():
            gr = dlb_scr[...] * lb * (1.0 - lb)
            small_ref[...] = jnp.zeros_like(small_ref)
            small_ref[0:1, 0:HW] = gr
            small_ref[1:2, 0:HW] = -gr
            for cp in rem:
                cp.wait_recv()
            for cp in rem:
                cp.wait_send()
            for cp in loc:
                cp.wait()

    tok = pl.BlockSpec((NSUB * TH, HW), lambda i: (NT - 1 - i, 0))
    const = lambda shape: pl.BlockSpec(shape, lambda i: (0,) * len(shape))
    hbm = pl.BlockSpec(memory_space=pltpu.HBM)
    return pl.pallas_call(
        body, name="hgrn_bwd", grid=(NT,),
        in_specs=[tok, tok, tok, const((2, HW)), const((TH, TH)), const((TH, TH)), tok,
                  pl.BlockSpec((NSUB * NCH, 128, HW), lambda i: (NT - 1 - i, 0, 0)), tok, hbm, hbm],
        out_specs=[pl.BlockSpec((NSUB * TH, NCOL // 2), lambda i: (NT - 1 - i, 0)), const((8, D)), hbm, hbm],
        out_shape=[jax.ShapeDtypeStruct((T, NCOL // 2), BF16), jax.ShapeDtypeStruct((8, D), F32),
                   jax.ShapeDtypeStruct((128, D), F32), jax.ShapeDtypeStruct((3, 128, D), BF16)],
        scratch_shapes=[pltpu.VMEM((128, HW), F32), pltpu.VMEM((1, HW), F32), pltpu.VMEM((NSUB, TH, HW), F32),
                        pltpu.VMEM((NSUB, TH, HW), F32), pltpu.VMEM((NSUB, TH, HW), F32),
                        pltpu.VMEM((NSUB, 8, HW), F32),
                        pltpu.SemaphoreType.DMA((3,)), pltpu.SemaphoreType.DMA((3,)), pltpu.SemaphoreType.DMA((1,))],
        compiler_params=_cp(("arbitrary",)),
    )(hq, hf, hi, lbr, tri, trit, drec, sall, dhg, rout, routb)


def _fwd_out(o1, o4, o16, l1, l4, l16, rec, ag, hg, x, tgt, anw, hnw, fnw, wout_full, gmat, emat, selmat):
    TT = 512

    def body(o1_r, o4_r, o16_r, l1_r, l4_r, l16_r, rec_r, ag_r, hg_r, x_r, tgt_r, anw_r, hnw_r, fnw_r, wo_r, g_r,
             e_r, sel_r, dx2_o, do1_o, do4_o, do16_o, st1_o, st4_o, st16_o, drec_o, dag_o, dhg_o,
             rout_o, routb_o, small_o, scr_a, scr_b, scr_c, gwout_o, rbuf, send_sems, recv_sems):
        @pl.when(pl.program_id(0) == 0)
        def _():
            gwout_o[...] = jnp.zeros_like(gwout_o)
            small_o[...] = jnp.zeros_like(small_o)

        def unperm(r4, r16):
            return _unperm_load(r4, r16, scr_a, scr_b, scr_c)

        def perm_out(val, p1, p4, p16, dt):
            _perm_store(val, scr_a, scr_b, p1, p4, p16, dt)

        o4u, o16u = unperm(o4_r, o16_r)
        l4c, l16c = unperm(l4_r, l16_r)
        l1c = l1_r[...]
        mxc = jnp.maximum(jnp.maximum(l1c, l4c), l16c)
        w1c, w4c, w16c = jnp.exp(l1c - mxc), jnp.exp(l4c - mxc), jnp.exp(l16c - mxc)
        denc = w1c + w4c + w16c
        lane = lax.broadcasted_iota(jnp.int32, (1, 128), 1)
        lse_c = jnp.where(lane < 8, mxc + jnp.log(denc), 0.0)
        em = e_r[...]
        wn1 = _mm_exact_r(w1c / denc, em)
        wn4 = _mm_exact_r(w4c / denc, em)
        o1v = o1_r[...].astype(F32)
        attn = wn1 * o1v + wn4 * o4u + (1.0 - wn1 - wn4) * o16u
        gm = g_r[...]

        def head_mean_a(t):
            return jnp.concatenate([_mm_exact_r(t[:, :256], gm), _mm_exact_r(t[:, 256:], gm)], axis=1)

        def head_mean_h(t):
            return jnp.concatenate(
                [jnp.broadcast_to(jnp.mean(t[:, h * 128:(h + 1) * 128], axis=-1, keepdims=True), (TT, 128))
                 for h in range(4)], axis=1)

        rs_a = lax.rsqrt(head_mean_a(attn * attn) + EPS)
        n_a = attn * rs_a
        agv = ag_r[...].astype(F32)
        sg_a = _sigmoid(agv)
        si_a = agv * sg_a
        anw_v = anw_r[...]
        y_a = (n_a * anw_v) * si_a
        recv = rec_r[...].astype(F32)
        rs_h = lax.rsqrt(head_mean_h(recv * recv) + EPS)
        n_h = recv * rs_h
        hgv = hg_r[...].astype(F32)
        sg_h = _sigmoid(hgv)
        si_h = hgv * sg_h
        hnw_v = hnw_r[...]
        y_h = (n_h * hnw_v) * si_h
        mixed = jnp.concatenate([y_a, y_h], axis=1).astype(BF16)
        xv = x_r[...]
        x2 = xv + _mm(mixed, wo_r[...])
        r2 = lax.rsqrt(jnp.mean(x2 * x2, axis=-1, keepdims=True) + EPS)
        fnw_v = fnw_r[...]
        xn = x2 * r2
        err = xn * fnw_v - tgt_r[...]
        small_o[2:3, :] += 0.5 * jnp.sum(jnp.mean(err * err, axis=-1, keepdims=True), axis=0, keepdims=True)
        small_o[0:1, :] += jnp.sum(err * xn, axis=0, keepdims=True) * (1.0 / D)
        dyw = err * (fnw_v * (1.0 / D))
        dx2 = r2 * dyw - x2 * ((r2 * r2 * r2) * jnp.mean(dyw * x2, axis=-1, keepdims=True))
        dx2_o[...] = dx2
        dx2b = dx2.astype(BF16)
        gwout_o[...] += _mm_tn(mixed, dx2b)
        dmix = _mm_nt(dx2b, wo_r[...])
        dm_a, dm_h = dmix[:, :AW], dmix[:, AW:]
        dag_o[...] = (dm_a * (n_a * anw_v) * (sg_a * (1.0 + agv * (1.0 - sg_a)))).astype(BF16)
        dy_a = dm_a * si_a
        dn_a = dy_a * anw_v
        small_o[1:2, 0:AW] += jnp.sum(dy_a * n_a, axis=0, keepdims=True)
        dattn = rs_a * (dn_a - n_a * head_mean_a(dn_a * n_a))
        perm_out(dattn, do1_o, do4_o, do16_o, BF16)
        stats = lse_c + _mm_exact_r(dattn * attn, sel_r[...])
        perm_out(stats, st1_o, st4_o, st16_o, F32)
        dhg_o[...] = (dm_h * (n_h * hnw_v) * (sg_h * (1.0 + hgv * (1.0 - sg_h)))).astype(BF16)
        dy_h = dm_h * si_h
        dn_h = dy_h * hnw_v
        small_o[1:2, AW:] += jnp.sum(dy_h * n_h, axis=0, keepdims=True)
        drec_o[...] = (rs_h * (dn_h - n_h * head_mean_h(dn_h * n_h))).astype(BF16)

        @pl.when(pl.program_id(0) == T // TT - 1)
        def _():
            x, y, c = lax.axis_index("x"), lax.axis_index("y"), lax.axis_index("c")
            cps = [pltpu.make_async_remote_copy(
                src_ref=gwout_o.at[pl.ds(pl.multiple_of(j * 256 + (1 - c) * 128, 128), 128), :], dst_ref=rbuf.at[j],
                send_sem=send_sems.at[j], recv_sem=recv_sems.at[j], device_id=(x, y, 1 - c), device_id_type=MESH)
                for j in range(4)]
            for cp in cps:
                cp.start()
            for j, cp in enumerate(cps):
                cp.wait_recv()
                red = gwout_o[pl.ds(pl.multiple_of(j * 256 + c * 128, 128), 128), :] + rbuf[j]
                rout_o[j * 128:(j + 1) * 128, :] = red
                routb_o[j * 128:(j + 1) * 128, :] = red.astype(BF16)
            for cp in cps:
                cp.wait_send()

    tok = lambda w: pl.BlockSpec((TT, w), lambda i: (i, 0))
    d4 = pl.BlockSpec((4, TT // 4, AW), lambda i: (0, i, 0))
    d16 = pl.BlockSpec((16, TT // 16, AW), lambda i: (0, i, 0))
    const = lambda shape: pl.BlockSpec(shape, lambda i: (0,) * len(shape))
    sd = lambda shape, dt: jax.ShapeDtypeStruct(shape, dt)
    c4 = pl.BlockSpec((4, TT // 4, 128), lambda i: (0, i, 0))
    c16 = pl.BlockSpec((16, TT // 16, 128), lambda i: (0, i, 0))
    p3 = lambda w, dt: [sd((T, w), dt), sd((4, T // 4, w), dt), sd((16, T // 16, w), dt)]
    return pl.pallas_call(
        body, name="fwd_out", grid=(T // TT,),
        in_specs=[tok(AW), d4, d16, tok(128), c4, c16, tok(AW), tok(AW), tok(AW), tok(D), tok(D),
                  const((1, AW)), const((1, HW)), const((1, D)), const((D, D)), const((256, 256)),
                  const((128, AW)), const((AW, 128))],
        out_specs=[tok(D)] + [tok(AW), d4, d16] + [tok(128), c4, c16] + [tok(AW)] * 3
        + [const((512, D)), const((512, D)), const((8, D))],
        out_shape=[sd((T, D), F32)] + p3(AW, BF16) + p3(128, F32)
        + [sd((T, AW), BF16), sd((T, AW), BF16), sd((T, AW), BF16), sd((512, D), F32), sd((512, D), BF16),
           sd((8, D), F32)],
        scratch_shapes=[pltpu.VMEM((4, TT, 128), F32)] * 3 + [pltpu.VMEM((D, D), F32),
                        pltpu.VMEM((4, 128, D), F32), pltpu.SemaphoreType.DMA((4,)), pltpu.SemaphoreType.DMA((4,))],
        compiler_params=_cp(("arbitrary",)),
    )(o1, o4, o16, l1, l4, l16, rec, ag, hg, x, tgt, anw, hnw, fnw, wout_full, gmat, emat, selmat)


def _dproj_build(dq, dk, dv, dag, pos):
    TT = 512

    def body(dq1, dq4, dq16, dk1, dk4, dk16, dv1, dv4, dv16, dag_r, pos_r, dproj_o, scr_b, scr_c):
        def unperm_sum(r1, r4, r16):
            return r1[...] + _unperm_sum(r4, r16, scr_b, scr_c)

        cosf, s1, s2 = _rope_tables(pos_r[...])
        dproj_o[:, 0:512] = _rope_bwd(unperm_sum(dq1, dq4, dq16), cosf, s1, s2).astype(BF16)
        dproj_o[:, 512:1024] = _rope_bwd(unperm_sum(dk1, dk4, dk16), cosf, s1, s2).astype(BF16)
        dproj_o[:, 1024:1536] = unperm_sum(dv1, dv4, dv16).astype(BF16)
        dproj_o[:, 1536:2048] = dag_r[...]

    tok = lambda w: pl.BlockSpec((TT, w), lambda i: (i, 0))
    d4 = pl.BlockSpec((4, TT // 4, AW), lambda i: (0, i, 0))
    d16 = pl.BlockSpec((16, TT // 16, AW), lambda i: (0, i, 0))
    return pl.pallas_call(
        body, name="dproj_build", grid=(T // TT,),
        in_specs=[tok(AW), d4, d16] * 3 + [tok(AW), pl.BlockSpec((1, TT), lambda i: (0, i))],
        out_specs=tok(NCOL // 2),
        out_shape=jax.ShapeDtypeStruct((T, NCOL // 2), BF16),
        scratch_shapes=[pltpu.VMEM((4, TT, 128), F32)] * 2,
        compiler_params=_cp(("parallel",)),
    )(*dq, *dk, *dv, dag, pos)


def _bwd_x(dproj_a, dproj_h, x, dx2, mixw, w_full, rin, rinb, small4, small6, pout_own, pout_rem):
    TT = 256
    NT = T // TT

    def body(dpa_r, dph_r, x_r, dx2_r, mw_r, w_r, rin_r, rinb_r, s4_r, s6_r, poo_r, por_r,
             gx_o, sall_o, fin_o, fout_o, sbuf, v_own, v_rem, vo_own, vo_rem, sin, sout, got_in,
             got_out, send_sems, recv_sems, loc_sems, share_send, share_recv, fin_sems):
        i = pl.program_id(0)
        loc, rem = _chip_copies(_w_in_piece, rin_r, rinb_r, v_own, v_rem, send_sems, recv_sems, loc_sems.at[0])
        loads = [pltpu.make_async_copy(poo_r, vo_own, fin_sems.at[2]),
                 pltpu.make_async_copy(por_r, vo_rem, fin_sems.at[3])]

        @pl.when(i == 0)
        def _():
            sbuf[...] = jnp.zeros_like(sbuf)
            for cp in loc + rem + loads:
                cp.start()

        dhn = _mm_nt(dpa_r[...], w_r[:, 0:NCOL // 2]) + _mm_nt(dph_r[...], w_r[:, NCOL // 2:NCOL])
        xv = x_r[...]
        r = lax.rsqrt(jnp.mean(xv * xv, axis=-1, keepdims=True) + EPS)
        dxw = dhn * mw_r[...]
        gx_o[...] = dx2_r[...] + r * dxw - xv * ((r * r * r) * jnp.mean(dxw * xv, axis=-1, keepdims=True))
        sbuf[16:17, :] += jnp.sum(dhn * (xv * r), axis=0, keepdims=True)

        @pl.when(i == NT - 1)
        def _():
            sbuf[0:8, :] = s4_r[...]
            sbuf[8:16, :] = s6_r[...]
            sloc, srem = _small_copies(sbuf, sall_o, send_sems, recv_sems, loc_sems.at[1])
            for cp in sloc + srem:
                cp.start()
            for cp in rem:
                cp.wait_recv()
            for cp in rem:
                cp.wait_send()
            for cp in loc:
                cp.wait()
            mx, my, c = lax.axis_index("x"), lax.axis_index("y"), lax.axis_index("c")
            for cp in loads:
                cp.wait()
            sout[...] = ((vo_own[...] + vo_rem[0].astype(F32)) + vo_rem[1].astype(F32)) + vo_rem[2].astype(F32)
            sin[...] = ((v_own[...] + v_rem[0].astype(F32)) + v_rem[1].astype(F32)) + v_rem[2].astype(F32)
            swap = [pltpu.make_async_remote_copy(src_ref=sin, dst_ref=got_in, send_sem=share_send.at[0],
                                                 recv_sem=share_recv.at[0], device_id=(mx, my, 1 - c),
                                                 device_id_type=MESH),
                    pltpu.make_async_remote_copy(src_ref=sout, dst_ref=got_out, send_sem=share_send.at[1],
                                                 recv_sem=share_recv.at[1], device_id=(mx, my, 1 - c),
                                                 device_id_type=MESH)]
            for cp in swap:
                cp.start()
            mine = [pltpu.make_async_copy(sin, fin_o.at[c], fin_sems.at[0]),
                    pltpu.make_async_copy(sout, fout_o.at[c], fin_sems.at[1])]
            for cp in mine:
                cp.start()
            for cp in swap:
                cp.wait_recv()
            theirs = [pltpu.make_async_copy(got_in, fin_o.at[1 - c], fin_sems.at[2]),
                      pltpu.make_async_copy(got_out, fout_o.at[1 - c], fin_sems.at[3])]
            for cp in theirs:
                cp.start()
            for cp in swap:
                cp.wait_send()
            for cp in mine + theirs:
                cp.wait()
            for cp in srem:
                cp.wait_recv()
            for cp in srem:
                cp.wait_send()
            for cp in sloc:
                cp.wait()

    tok = lambda w: pl.BlockSpec((TT, w), lambda i: (i, 0))
    const = lambda shape: pl.BlockSpec(shape, lambda i: (0,) * len(shape))
    hbm = pl.BlockSpec(memory_space=pltpu.HBM)
    return pl.pallas_call(
        body, name="bwd_x", grid=(NT,),
        in_specs=[tok(NCOL // 2), tok(NCOL // 2), tok(D), tok(D), const((1, D)), const((D, NCOL)), hbm, hbm,
                  const((8, D)), const((8, D)), hbm, hbm],
        out_specs=[tok(D), hbm, hbm, hbm],
        out_shape=[jax.ShapeDtypeStruct((T, D), F32),
                   jax.ShapeDtypeStruct((8, 24, D), F32),
                   jax.ShapeDtypeStruct((2, 512, 1024), F32), jax.ShapeDtypeStruct((2, 128, D), F32)],
        scratch_shapes=[pltpu.VMEM((24, D), F32),
                        pltpu.VMEM((512, 1024), F32), pltpu.VMEM((3, 512, 1024), BF16),
                        pltpu.VMEM((128, D), F32), pltpu.VMEM((3, 128, D), BF16),
                        pltpu.VMEM((512, 1024), F32), pltpu.VMEM((128, D), F32),
                        pltpu.VMEM((512, 1024), F32), pltpu.VMEM((128, D), F32),
                        pltpu.SemaphoreType.DMA((10,)), pltpu.SemaphoreType.DMA((10,)), pltpu.SemaphoreType.DMA((2,)),
                        pltpu.SemaphoreType.DMA((2,)), pltpu.SemaphoreType.DMA((2,)), pltpu.SemaphoreType.DMA((4,))],
        compiler_params=_cp(("arbitrary",)),
    )(dproj_a, dproj_h, x, dx2, mixw, w_full, rin, rinb, small4, small6, pout_own, pout_rem)


def _grad_w_in(hn, dproj_a, dproj_h):
    TK = 2048
    NK = T // TK

    def body(hnt_r, dpa_r, dph_r, rin_o, rinb_o, acc, rbuf, obuf, obufb, send_sems, recv_sems, wb_sems):
        j = pl.program_id(0)
        kk = pl.program_id(1)
        x, y, c = lax.axis_index("x"), lax.axis_index("y"), lax.axis_index("c")
        mine = pl.ds(pl.multiple_of(c * 512, 512), 512)
        theirs = pl.ds(pl.multiple_of((1 - c) * 512, 512), 512)

        def send(jj):
            return pltpu.make_async_remote_copy(
                src_ref=acc.at[jj % 2, theirs, :], dst_ref=rbuf.at[jj], send_sem=send_sems.at[jj],
                recv_sem=recv_sems.at[jj], device_id=(x, y, 1 - c), device_id_type=MESH)

        def writeback(jj):
            cols = pl.ds(jj * 1024, 1024)
            return [pltpu.make_async_copy(obuf.at[jj % 2], rin_o.at[:, cols], wb_sems.at[jj % 2]),
                    pltpu.make_async_copy(obufb.at[jj % 2], rinb_o.at[:, cols], wb_sems.at[2 + jj % 2])]

        def wait_writeback(jj):
            for cp in writeback(jj):
                cp.wait()

        def finalize(jj):
            send(jj).wait_recv()
            red = acc[jj % 2, mine, :] + rbuf[jj]
            obuf[jj % 2] = red
            obufb[jj % 2] = red.astype(BF16)
            for cp in writeback(jj):
                cp.start()

        prod = _mm(hnt_r[...], jnp.where(j < 2, dpa_r[...], dph_r[...]))

        @pl.when(kk == 0)
        def _():
            for jj in (2, 3):
                @pl.when(j == jj)
                def _():
                    send(jj - 2).wait_send()
            acc[j % 2] = prod

        @pl.when(kk > 0)
        def _():
            acc[j % 2] += prod

        @pl.when(kk == NK - 1)
        def _():
            for jj in range(4):
                @pl.when(j == jj)
                def _():
                    send(jj).start()
                    if jj in (1, 2):
                        finalize(jj - 1)
                    if jj == 3:
                        wait_writeback(0)
                        finalize(2)
                        wait_writeback(1)
                        finalize(3)
                        wait_writeback(2)
                        wait_writeback(3)
                        send(2).wait_send()
                        send(3).wait_send()

    hbm = pl.BlockSpec(memory_space=pltpu.HBM)
    return pl.pallas_call(
        body, name="grad_w_in", grid=(4, NK),
        in_specs=[pl.BlockSpec((D, TK), lambda j, kk: (0, kk)),
                  pl.BlockSpec((TK, 1024), lambda j, kk: (jnp.where(j < 2, kk, NK - 1), jnp.minimum(j, 1))),
                  pl.BlockSpec((TK, 1024), lambda j, kk: (jnp.where(j < 2, 0, kk), jnp.maximum(j - 2, 0)))],
        out_specs=[hbm, hbm],
        out_shape=[jax.ShapeDtypeStruct((512, NCOL), F32), jax.ShapeDtypeStruct((512, NCOL), BF16)],
        scratch_shapes=[pltpu.VMEM((2, D, 1024), F32), pltpu.VMEM((4, 512, 1024), F32), pltpu.VMEM((2, 512, 1024), F32),
                        pltpu.VMEM((2, 512, 1024), BF16),
                        pltpu.SemaphoreType.DMA((4,)), pltpu.SemaphoreType.DMA((4,)), pltpu.SemaphoreType.DMA((4,))],
        compiler_params=_cp(("arbitrary", "arbitrary")),
    )(hn, dproj_a, dproj_h)


def _w_in_piece(ref, j):
    return ref.at[:, pl.ds(j * 1024, 1024)]


def _w_out_piece(ref, j):
    return ref.at[pl.ds(j * 128, 128), :]


def _chip_copies(piece, src_r, srcb_r, own_o, rem_o, send_sems, recv_sems, loc_sem):
    x, y, c = lax.axis_index("x"), lax.axis_index("y"), lax.axis_index("c")
    chips = [(1 - x, y), (x, 1 - y), (1 - x, 1 - y)]
    loc = [pltpu.make_async_copy(piece(src_r, 2 * x + y), own_o, loc_sem)]
    rem = [pltpu.make_async_remote_copy(
        src_ref=piece(srcb_r, 2 * px + py), dst_ref=rem_o.at[k], send_sem=send_sems.at[k],
        recv_sem=recv_sems.at[k], device_id=(px, py, c), device_id_type=MESH) for k, (px, py) in enumerate(chips)]
    return loc, rem


def _small_copies(small_r, sall_o, send_sems, recv_sems, loc_sem):
    x, y, c = lax.axis_index("x"), lax.axis_index("y"), lax.axis_index("c")
    me = 4 * x + 2 * y + c
    loc = [pltpu.make_async_copy(small_r, sall_o.at[me], loc_sem)]
    rem = []
    k = 3
    for fx in range(2):
        for fy in range(2):
            for fc in range(2):
                if fx or fy or fc:
                    peer = (1 - x if fx else x, 1 - y if fy else y, 1 - c if fc else c)
                    rem.append(pltpu.make_async_remote_copy(
                        src_ref=small_r, dst_ref=sall_o.at[me], send_sem=send_sems.at[k],
                        recv_sem=recv_sems.at[k], device_id=peer, device_id_type=MESH))
                    k += 1
    return loc, rem


def _adamw_math(w, g, m, v):
    m = B1 * m + (1.0 - B1) * g
    v = B2 * v + (1.0 - B2) * (g * g)
    m_hat = m / (1.0 - B1 ** STEP)
    v_hat = v / (1.0 - B2 ** STEP)
    delta = -LR * (m_hat / (jnp.sqrt(v_hat) + AEPS) + WD * w)
    return delta, m, v


def _adamw(big_in, big_out, sall, params):
    def body(*refs):
        wi, gi, mi, vi, wo, go, mo, vo, sall_r = refs[:9]
        ins = refs[9:24]
        di_o, mi_o, vi_o, do_o, mo_o, vo_o = refs[24:30]
        outs = refs[30:]
        d, mm, vv = _adamw_math(wi[...], gi[...], mi[...], vi[...])
        di_o[...] = d
        mi_o[...] = mm
        vi_o[...] = vv

        @pl.when(pl.program_id(0) == 0)
        def _():
            d, mm, vv = _adamw_math(wo[...], go[...], mo[...], vo[...])
            do_o[...] = d
            mo_o[...] = mm
            vo_o[...] = vv
            tot = sall_r[0]
            for dv in range(1, 8):
                tot = tot + sall_r[dv]
            grads = [tot[16:17, :], tot[1:2, 0:AW], tot[1:2, AW:], tot[8:10, 0:HW], tot[0:1, :]]
            outs[0][...] = tot[2:3, 0:1]
            for p in range(5):
                w_r, m_r, v_r = ins[3 * p:3 * p + 3]
                g = grads[p]
                d, mm, vv = _adamw_math(w_r[...], g, m_r[...], v_r[...])
                outs[1 + 4 * p][...] = g
                outs[2 + 4 * p][...] = d
                outs[3 + 4 * p][...] = mm
                outs[4 + 4 * p][...] = vv

    flat = [a for p in params for a in p]
    shapes = [jax.ShapeDtypeStruct((D, 1024), F32)] * 3 + [jax.ShapeDtypeStruct((256, D), F32)] * 3
    shapes += [jax.ShapeDtypeStruct((1, 1), F32)]
    for p in params:
        shapes += [jax.ShapeDtypeStruct(p[0].shape, F32)] * 4
    vm = pl.BlockSpec(memory_space=pltpu.VMEM)
    rows = pl.BlockSpec((512, 1024), lambda i: (i, 0))
    whole = pl.BlockSpec((256, D), lambda i: (0, 0))
    return pl.pallas_call(
        body, name="adamw", grid=(2,),
        in_specs=[rows] * 4 + [whole] * 4 + [vm] * 16, out_specs=[rows] * 3 + [whole] * 3 + [vm] * 21,
        out_shape=shapes,
        compiler_params=_cp(("arbitrary",)),
    )(*big_in, *big_out, sall, *flat)


def kernel(x, positions, w_in, w_out, mix_norm_w, attn_out_norm_w, hgrn_out_norm_w, hgrn_lb_raw, final_norm_w, loss_target, m_w_in, m_w_out, m_mix_norm_w, m_attn_out_norm_w, m_hgrn_out_norm_w, m_hgrn_lb_raw, m_final_norm_w, v_w_in, v_w_out, v_mix_norm_w, v_attn_out_norm_w, v_hgrn_out_norm_w, v_hgrn_lb_raw, v_final_norm_w):
    xs = x.reshape(T, D)
    tgt = loss_target.reshape(T, D)
    pos = positions.reshape(1, T)
    fnw = final_norm_w.reshape(1, D)

    ti = np.arange(TH)
    tri_np = ((ti[:, None] // CHUNK == ti[None, :] // CHUNK) & (ti[None, :] <= ti[:, None])).astype(np.float32)
    tri = jnp.asarray(tri_np, BF16)
    trit = jnp.asarray(tri_np.T, BF16)
    hi_ = np.arange(AW) // HEAD
    gmat = jnp.asarray((hi_[:256, None] == hi_[None, :256]).astype(np.float32) / HEAD, BF16)
    emat_np = (np.arange(128)[:, None] == hi_[None, :]).astype(np.float32)
    sel_np = (8 + hi_[:, None] == np.arange(128)[None, :]).astype(np.float32)
    emat = jnp.asarray(emat_np, BF16)
    selmat = jnp.asarray(sel_np, BF16)

    jm_arr = (2 * lax.axis_index("x") + lax.axis_index("y")).astype(jnp.int32).reshape(1)
    (hn, q1, k1, v1, q4, k4, v4, q16, k16, v16, ag, hq, hf, hi, hg, w_full, wout4) = _fwd_in(
        xs, pos, mix_norm_w, w_in.reshape(D, 1024), w_out.reshape(256, D), jm_arr)
    wout_full = wout4.reshape(D, D)
    flat = lambda a: a.reshape(T, AW)
    o1, l1 = _attn_fwd(q1, k1, v1, T // BLK, "attn_fwd_d1")
    o4, l4 = _attn_fwd(flat(q4), flat(k4), flat(v4), T // 4 // BLK, "attn_fwd_d4")
    o16, l16 = _attn_fwd(flat(q16), flat(k16), flat(v16), T // 16 // BLK, "attn_fwd_d16")
    rec, sall = _hgrn_fwd(hq, hf, hi, hgrn_lb_raw, tri)

    (dx2, do1, do4, do16, st1, st4, st16, drec, dag, dhg, rout, routb, small4) = _fwd_out(
        o1, o4.reshape(4, T // 4, AW), o16.reshape(16, T // 16, AW),
        l1, l4.reshape(4, T // 4, 128), l16.reshape(16, T // 16, 128),
        rec, ag, hg, xs, tgt, attn_out_norm_w, hgrn_out_norm_w, fnw, wout_full, gmat, emat, selmat)

    fst = lambda a: a.reshape(T, 128)
    dq1, dk1, dv1 = _attn_bwd(q1, k1, v1, do1, st1, T // BLK, "attn_bwd_d1")
    dq4, dk4, dv4 = _attn_bwd(flat(q4), flat(k4), flat(v4), flat(do4), fst(st4), T // 4 // BLK, "attn_bwd_d4")
    dq16, dk16, dv16 = _attn_bwd(flat(q16), flat(k16), flat(v16), flat(do16), fst(st16), T // 16 // BLK,
                                 "attn_bwd_d16")
    dproj_h, small6, pout_own, pout_rem = _hgrn_bwd(hq, hf, hi, hgrn_lb_raw, tri, trit, drec, sall, dhg,
                                                    rout, routb)

    r4 = lambda a: a.reshape(4, T // 4, AW)
    r16 = lambda a: a.reshape(16, T // 16, AW)
    dproj_a = _dproj_build((dq1, r4(dq4), r16(dq16)), (dk1, r4(dk4), r16(dk16)), (dv1, r4(dv4), r16(dv16)),
                           dag, pos)
    rin, rinb = _grad_w_in(hn, dproj_a, dproj_h)
    gx, small_all, fin, fout = _bwd_x(dproj_a, dproj_h, xs, dx2, mix_norm_w, w_full, rin, rinb,
                                            small4, small6, pout_own, pout_rem)
    g_w_in = fin.reshape(D, 1024)
    g_w_out = fout.reshape(256, D)

    params = [(mix_norm_w, m_mix_norm_w, v_mix_norm_w),
              (attn_out_norm_w, m_attn_out_norm_w, v_attn_out_norm_w),
              (hgrn_out_norm_w, m_hgrn_out_norm_w, v_hgrn_out_norm_w),
              (hgrn_lb_raw, m_hgrn_lb_raw, v_hgrn_lb_raw),
              (fnw, m_final_norm_w.reshape(1, D), v_final_norm_w.reshape(1, D))]
    d_in, nm_in, nv_in, d_out, nm_out, nv_out, *so = _adamw(
        (w_in.reshape(D, 1024), g_w_in, m_w_in.reshape(D, 1024), v_w_in.reshape(D, 1024)),
        (w_out.reshape(256, D), g_w_out, m_w_out.reshape(256, D), v_w_out.reshape(256, D)), small_all, params)
    loss = so[0].reshape(())
    g_s = [so[1 + 4 * p] for p in range(5)]
    d_s = [so[2 + 4 * p] for p in range(5)]
    m_s = [so[3 + 4 * p] for p in range(5)]
    v_s = [so[4 + 4 * p] for p in range(5)]
    for lst in (g_s, d_s, m_s, v_s):
        lst[4] = lst[4].reshape(D)

    return (loss, gx.reshape(1, T, D),
            g_w_in.reshape(1, D, 1024), g_w_out.reshape(1, 256, D), *g_s,
            d_in.reshape(1, D, 1024), d_out.reshape(1, 256, D), *d_s,
            nm_in.reshape(1, D, 1024), nm_out.reshape(1, 256, D), *m_s,
            nv_in.reshape(1, D, 1024), nv_out.reshape(1, 256, D), *v_s)
```

```python
import functools

import numpy as np
import jax
import jax.numpy as jnp
from jax import lax
from jax.experimental import pallas as pl
from jax.experimental.pallas import tpu as pltpu

F32 = jnp.float32
BF16 = jnp.bfloat16

T = 4096
D = 1024
AW = 512
HW = 512
NCOL = 4096
HEAD = 64
BLK = 128
CHUNK = 64
EPS = 1e-6
SCALE = HEAD ** -0.5
NEG = -1e30
ROPE_THETA = 500000.0
INV_FREQ = [float(v) for v in
            (np.float32(ROPE_THETA) ** (-(np.arange(8, dtype=np.float32)) * np.float32(0.125)))]
LR, B1, B2, AEPS, WD, STEP = 0.001, 0.9, 0.999, 1e-08, 0.01, 10
VMEM_LIMIT = 63 * 1024 * 1024
MESH = pl.DeviceIdType.MESH


def _cp(sem=None, **kw):
    return pltpu.CompilerParams(dimension_semantics=sem, vmem_limit_bytes=VMEM_LIMIT, **kw)


def _mm(a, b):
    return jnp.dot(a, b, preferred_element_type=F32)


def _mm_nt(a, b):
    return lax.dot_general(a, b, (((1,), (1,)), ((), ())), preferred_element_type=F32)


def _mm_tn(a, b):
    return lax.dot_general(a, b, (((0,), (0,)), ((), ())), preferred_element_type=F32)


def _mm_exact_l(mat_bf, x):
    h = x.astype(BF16)
    l = (x - h.astype(F32)).astype(BF16)
    return _mm(mat_bf, h) + _mm(mat_bf, l)


def _mm_exact_r(x, mat_bf):
    h = x.astype(BF16)
    l = (x - h.astype(F32)).astype(BF16)
    return _mm(h, mat_bf) + _mm(l, mat_bf)


def _sigmoid(x):
    return 0.5 * jnp.tanh(0.5 * x) + 0.5


def _rope_tables(pos):
    lane = lax.broadcasted_iota(jnp.int32, (1, 128), 1)
    jl = lane & 63
    fi = jl & 7
    inv = jnp.zeros((1, 128), F32)
    for kk in range(8):
        inv = jnp.where(fi == kk, INV_FREQ[kk], inv)
    ang = jnp.broadcast_to(pos.astype(F32), (128, pos.shape[1])).T * inv
    c = jnp.cos(ang)
    s = jnp.sin(ang)
    cosf = jnp.where(jl < 16, c, 1.0)
    s1 = jnp.where(jl < 8, -s, 0.0)
    s2 = jnp.where((jl >= 8) & (jl < 16), s, 0.0)
    return cosf, s1, s2


def _rope(t, cosf, s1, s2):
    parts = []
    for ci in range(t.shape[1] // 128):
        tc = t[:, ci * 128:(ci + 1) * 128]
        parts.append(tc * cosf + pltpu.roll(tc, 120, 1) * s1 + pltpu.roll(tc, 8, 1) * s2)
    return jnp.concatenate(parts, axis=1)


def _rope_bwd(g, cosf, s1, s2):
    parts = []
    for ci in range(g.shape[1] // 128):
        gc = g[:, ci * 128:(ci + 1) * 128]
        parts.append(gc * cosf + pltpu.roll(gc * s1, 8, 1) + pltpu.roll(gc * s2, 120, 1))
    return jnp.concatenate(parts, axis=1)


def _perm_store(val, scr, scr2, o1, o4, o16, dt):
    n = val.shape[0]
    q = n // 4
    o1[...] = val.astype(dt)
    for ci in range(val.shape[1] // 128):
        cs = slice(ci * 128, (ci + 1) * 128)
        scr[ci] = val[:, cs]
        for r4 in range(4):
            part = scr[ci, pl.ds(r4, q, stride=4), :]
            o4[r4, :, cs] = part.astype(dt)
            scr2[ci, r4 * q:(r4 + 1) * q, :] = part
        for r4 in range(4):
            for b in range(4):
                o16[r4 + 4 * b, :, cs] = scr2[ci, pl.ds(r4 * q + b, q // 4, stride=4), :].astype(dt)


def _unperm_load(r4, r16, scr_a, scr_b, scr_c):
    n = scr_a.shape[1]
    q = n // 4
    nc = r4.shape[-1] // 128
    for ci in range(nc):
        cs = slice(ci * 128, (ci + 1) * 128)
        for rr in range(4):
            scr_a[ci, pl.ds(rr, q, stride=4), :] = r4[rr, :, cs].astype(F32)
        for rr in range(4):
            for b in range(4):
                scr_c[ci, pl.ds(rr * q + b, q // 4, stride=4), :] = r16[rr + 4 * b, :, cs].astype(F32)
        for rr in range(4):
            scr_b[ci, pl.ds(rr, q, stride=4), :] = scr_c[ci, rr * q:(rr + 1) * q, :]
    return (jnp.concatenate([scr_a[ci] for ci in range(nc)], axis=1),
            jnp.concatenate([scr_b[ci] for ci in range(nc)], axis=1))


def _unperm_sum(r4, r16, scr_b, scr_c):
    n = scr_b.shape[1]
    q = n // 4
    nc = r4.shape[-1] // 128
    for ci in range(nc):
        cs = slice(ci * 128, (ci + 1) * 128)
        for rr in range(4):
            for b in range(4):
                scr_c[ci, pl.ds(rr * q + b, q // 4, stride=4), :] = r16[rr + 4 * b, :, cs].astype(F32)
        for rr in range(4):
            scr_b[ci, pl.ds(rr, q, stride=4), :] = scr_c[ci, rr * q:(rr + 1) * q, :] + r4[rr, :, cs].astype(F32)
    return jnp.concatenate([scr_b[ci] for ci in range(nc)], axis=1)


def _fwd_in(x, pos, mixw, w_in, w_out, jm_arr):
    TT = 512
    NT = T // TT

    def body(jm_ref, x_ref, pos_ref, mw_ref, win_ref, wout_ref,
             hnt_ref, q1, k1, v1, q4, k4, v4, q16, k16, v16, ag, hq, hf, hi, hg, wfull_o, woutfull_o,
             wbuf, wobuf, hn_all, scr, scr2, stage, send_sems, recv_sems, loc_sems):
        s = pl.program_id(0)
        i = pl.program_id(1)
        mx, my, c = lax.axis_index("x"), lax.axis_index("y"), lax.axis_index("c")
        me, sibling = (mx, my, c), (mx, my, 1 - c)
        chips = [(mx, 1 - my), (1 - mx, my), (1 - mx, 1 - my)]
        jm = 2 * mx + my
        rows_in = [pl.ds(pl.multiple_of(h * 512, 512), 512) for h in (c, 1 - c)]
        rows_out = [pl.ds(pl.multiple_of(h * 128, 128), 128) for h in (c, 1 - c)]

        def blk(k):
            return lax.bitwise_xor(jm, k + 1)

        def rc(n, ref, to):
            return pltpu.make_async_remote_copy(src_ref=ref, dst_ref=ref, send_sem=send_sems.at[n],
                                                recv_sem=recv_sems.at[n], device_id=to, device_id_type=MESH)

        halves = [pl.ds(0, 512), pl.ds(512, 512)]
        send_in = lambda k, h: rc(12 + 2 * k + h, wbuf.at[jm, rows_in[0], halves[h]], (*chips[k], c))
        got_in = lambda k, h: rc(12 + 2 * k + h, wbuf.at[blk(k), rows_in[0], halves[h]], me)
        relay = lambda h: rc(16 + h, wbuf.at[blk(h), rows_in[0], halves[h]], (*chips[1 - h], c))
        got_relay = lambda h: rc(16 + h, wbuf.at[blk(2), rows_in[0], halves[h]], me)
        send_out = lambda k: rc(3 + k, wobuf.at[jm, rows_out[0], :], (*chips[k], c))
        got_out = lambda k: rc(3 + k, wobuf.at[blk(k), rows_out[0], :], me)
        pass_in = lambda k: rc(6 + k, wbuf.at[blk(k), rows_in[0], :], sibling)
        pass_out = lambda k: rc(9 + k, wobuf.at[blk(k), rows_out[0], :], sibling)
        passed_in = lambda k: rc(6 + k, wbuf.at[blk(k), rows_in[1], :], me)
        passed_out = lambda k: rc(9 + k, wobuf.at[blk(k), rows_out[1], :], me)

        def keep(j, n):
            return pltpu.make_async_copy(wbuf.at[j], wfull_o.at[:, pl.ds(j * 1024, 1024)], loc_sems.at[n])

        @pl.when((s == 0) & (i == 0))
        def _():
            chunk = [pl.ds(pl.multiple_of(lax.rem(p + 2 * c, 4) * 256, 256), 256) for p in range(4)]
            loads = [pltpu.make_async_copy(win_ref.at[chunk[p], :] if p < 4 else wout_ref, stage.at[p % 2],
                                           loc_sems.at[4 + p % 2]) for p in range(5)]
            loads[0].start()
            for p in range(5):
                if p < 4:
                    loads[p + 1].start()
                loads[p].wait()
                if p < 4:
                    wbuf[jm, chunk[p], :] = stage[p % 2].astype(BF16)
                else:
                    wobuf[jm] = stage[p % 2].astype(BF16)
                if p == 1:
                    for k in range(2):
                        for h in range(2):
                            send_in(k, h).start()
            keep(jm, 0).start()

        @pl.when((s == 0) & (i == NT - 2))
        def _():
            for kk in range(2):
                for h in range(2):
                    got_in(kk, h).wait_recv()
            relay(0).start()
            relay(1).start()
            pass_in(0).start()
            pass_in(1).start()
            passed_in(0).wait_recv()
            keep(blk(0), 1).start()

        @pl.when((s == 1) & (i == NT - 1))
        def _():
            got_relay(0).wait_recv()
            got_relay(1).wait_recv()
            pass_in(2).start()

        @pl.when((s == 2) & (i == 0))
        def _():
            for k in (1, 2):
                passed_in(k).wait_recv()
                keep(blk(k), k + 1).start()
            for kk in range(3):
                send_out(kk).start()

        @pl.when((s == 2) & (i == NT - 2))
        def _():
            for k in range(3):
                got_out(k).wait_recv()
                pass_out(k).start()

        whole_out = pltpu.make_async_copy(wobuf, woutfull_o, loc_sems.at[4])

        @pl.when((s == 2) & (i == NT - 1))
        def _():
            for k in range(3):
                passed_out(k).wait_recv()
            whole_out.start()

        tile = pl.ds(pl.multiple_of(i * TT, TT), TT)

        @pl.when(s == 0)
        def _():
            xv = x_ref[...]
            r = lax.rsqrt(jnp.mean(xv * xv, axis=-1, keepdims=True) + EPS)
            hnf = (xv * r) * mw_ref[...]
            hn_all[tile, :] = hnf.astype(BF16)
            hnt_ref[...] = hnf.T.astype(BF16)

        def project(jj):
            hn = hn_all[tile, :]
            lo = _mm(hn, wbuf[jj, :, 0:512])
            hi_cols = _mm(hn, wbuf[jj, :, 512:1024])
            if jj == 0:
                cosf, s1, s2 = _rope_tables(pos_ref[...])
                _perm_store(_rope(lo, cosf, s1, s2) * SCALE, scr, scr2, q1, q4, q16, BF16)
                _perm_store(_rope(hi_cols, cosf, s1, s2), scr, scr2, k1, k4, k16, BF16)
            elif jj == 1:
                _perm_store(lo, scr, scr2, v1, v4, v16, BF16)
                ag[...] = hi_cols.astype(BF16)
            elif jj == 2:
                hq[...] = lo.astype(BF16)
                hf[...] = hi_cols.astype(BF16)
            else:
                hi[...] = lo.astype(BF16)
                hg[...] = hi_cols.astype(BF16)

        def project_block(j):
            for jj in range(4):
                pl.when(j == jj)(functools.partial(project, jj))

        @pl.when(s < 2)
        def _():
            project_block(lax.bitwise_xor(jm, s))

        @pl.when(s == 2)
        def _():
            project_block(lax.bitwise_xor(jm, 2))
            project_block(lax.bitwise_xor(jm, 3))

        @pl.when((s == 2) & (i == NT - 1))
        def _():
            for h in range(2):
                relay(h).wait_send()
                for k in range(2):
                    send_in(k, h).wait_send()
            for k in range(3):
                send_out(k).wait_send()
                pass_in(k).wait_send()
                pass_out(k).wait_send()
            keep(jm, 0).wait()
            for k in range(3):
                keep(blk(k), k + 1).wait()
            whole_out.wait()

    def at_stage_of(jb):
        def index(s, i, jm_ref):
            sa = jnp.minimum(lax.bitwise_xor(jm_ref[0], jb), 2)
            return jnp.where(s < sa, 0, jnp.where(s == sa, i, NT - 1))
        return index

    tok = lambda w, jb: pl.BlockSpec((TT, w), lambda s, i, jm_ref: (at_stage_of(jb)(s, i, jm_ref), 0))
    d4 = lambda jb: pl.BlockSpec((4, TT // 4, AW), lambda s, i, jm_ref: (0, at_stage_of(jb)(s, i, jm_ref), 0))
    d16 = lambda jb: pl.BlockSpec((16, TT // 16, AW), lambda s, i, jm_ref: (0, at_stage_of(jb)(s, i, jm_ref), 0))
    hbm = pl.BlockSpec(memory_space=pltpu.HBM)
    sd = lambda shape, dt: jax.ShapeDtypeStruct(shape, dt)
    in_own_stage = lambda s, i: jnp.where(s == 0, i, NT - 1)
    grid_spec = pltpu.PrefetchScalarGridSpec(
        num_scalar_prefetch=1, grid=(3, NT),
        in_specs=[pl.BlockSpec((TT, D), lambda s, i, jm_ref: (in_own_stage(s, i), 0)),
                  pl.BlockSpec((1, TT), lambda s, i, jm_ref: (0, i)),
                  pl.BlockSpec((1, D), lambda s, i, jm_ref: (0, 0)), hbm, hbm],
        out_specs=[pl.BlockSpec((D, TT), lambda s, i, jm_ref: (0, in_own_stage(s, i))),
                   tok(AW, 0), tok(AW, 0), tok(AW, 1), d4(0), d4(0), d4(1), d16(0), d16(0), d16(1),
                   tok(AW, 1), tok(AW, 2), tok(AW, 2), tok(AW, 3), tok(AW, 3), hbm, hbm],
        scratch_shapes=[pltpu.VMEM((4, D, 1024), BF16), pltpu.VMEM((4, 256, D), BF16), pltpu.VMEM((T, D), BF16),
                        pltpu.VMEM((4, TT, 128), F32), pltpu.VMEM((4, TT, 128), F32), pltpu.VMEM((2, 256, 1024), F32),
                        pltpu.SemaphoreType.DMA((18,)),
                        pltpu.SemaphoreType.DMA((18,)), pltpu.SemaphoreType.DMA((6,))])
    return pl.pallas_call(
        body, name="fwd_in", grid_spec=grid_spec,
        out_shape=[sd((D, T), BF16)] + [sd((T, AW), BF16)] * 3 + [sd((4, T // 4, AW), BF16)] * 3
        + [sd((16, T // 16, AW), BF16)] * 3
        + [sd((T, AW), BF16)] * 5 + [sd((D, NCOL), BF16), sd((4, 256, D), BF16)],
        compiler_params=_cp(("arbitrary", "arbitrary")),
    )(jm_arr, x, pos, mixw, w_in, w_out)


def _band_mask(key_axis, nkeys=2 * BLK):
    shape = (nkeys, 2 * BLK) if key_axis == 0 else (2 * BLK, nkeys)
    kj = lax.broadcasted_iota(jnp.int32, shape, key_axis)
    qi = lax.broadcasted_iota(jnp.int32, shape, 1 - key_axis) & (BLK - 1)
    return (kj >= qi) & (kj <= qi + BLK), kj, qi


def _stack_heads(t2, in_a):
    z = jnp.zeros_like(t2)
    return jnp.concatenate([jnp.where(in_a[0], t2, z), jnp.where(in_a[1], t2, z)], axis=0)


def _attn_fwd(q, k, v, nb, name):
    n = 8
    CH = n * BLK
    halo = nb > n

    def body(*refs):
        if halo:
            q_ref, k_ref, v_ref, kp_ref, vp_ref, o_ref, lse_ref = refs
        else:
            q_ref, k_ref, v_ref, o_ref, lse_ref = refs
        lane = lax.broadcasted_iota(jnp.int32, (1, 128), 1)
        in_a = [lane < HEAD, lane >= HEAD]
        band, kj, _ = _band_mask(1)
        thr0 = jnp.where((n * pl.program_id(0)) % nb == 0, BLK, 0) if halo else BLK
        mask0 = band & (kj >= thr0)
        mask_first = band & (kj >= BLK)
        for b in range(n):
            rs = slice(b * BLK, (b + 1) * BLK)
            stat = jnp.zeros((BLK, 128), F32)
            for hp in range(4):
                cs = slice(hp * 128, (hp + 1) * 128)
                q2s = _stack_heads(q_ref[rs, cs], in_a)
                if b == 0:
                    kprev = kp_ref[:, cs] if halo else k_ref[rs, cs]
                    vprev = vp_ref[:, cs] if halo else v_ref[rs, cs]
                    kk = jnp.concatenate([kprev, k_ref[rs, cs]], axis=0)
                    vv = jnp.concatenate([vprev, v_ref[rs, cs]], axis=0)
                    mask = mask0
                else:
                    kk = k_ref[(b - 1) * BLK:(b + 1) * BLK, cs]
                    vv = v_ref[(b - 1) * BLK:(b + 1) * BLK, cs]
                    mask = mask_first if b % nb == 0 else band
                s = jnp.where(mask, _mm_nt(q2s, kk), NEG)
                m = jnp.max(s, axis=-1, keepdims=True)
                p = jnp.exp(s - m)
                l = jnp.sum(p, axis=-1, keepdims=True)
                o = _mm(p.astype(BF16), vv) / l
                lse = m + jnp.log(l)
                o_ref[rs, cs] = jnp.where(in_a[0], o[:BLK], o[BLK:]).astype(BF16)
                stat = jnp.where(lane == 2 * hp, lse[:BLK], stat)
                stat = jnp.where(lane == 2 * hp + 1, lse[BLK:], stat)
            lse_ref[rs, :] = stat

    cur = pl.BlockSpec((CH, AW), lambda i: (i, 0))
    prev = pl.BlockSpec((BLK, AW), lambda i: (jnp.maximum(n * i - 1, 0), 0))
    return pl.pallas_call(
        body, name=name, grid=(T // CH,),
        in_specs=[cur, cur, cur] + ([prev, prev] if halo else []),
        out_specs=[cur, pl.BlockSpec((CH, 128), lambda i: (i, 0))],
        out_shape=[jax.ShapeDtypeStruct((T, AW), BF16), jax.ShapeDtypeStruct((T, 128), F32)],
        compiler_params=_cp(("parallel",)),
    )(*((q, k, v) + ((k, v) if halo else ())))


def _attn_bwd(q, k, v, do, st, nb, name):
    n = 8
    CH = n * BLK
    NBLK = T // BLK
    halo = nb > n

    def body(*refs):
        if halo:
            (q_ref, k_ref, v_ref, do_ref, st_ref, kp_ref, vp_ref, qn_ref, don_ref, stn_ref,
             dq_ref, dk_ref, dv_ref) = refs
        else:
            q_ref, k_ref, v_ref, do_ref, st_ref, dq_ref, dk_ref, dv_ref = refs
        i = pl.program_id(0)
        lane = lax.broadcasted_iota(jnp.int32, (1, 128), 1)
        in_a = [lane < HEAD, lane >= HEAD]
        band, kj, _ = _band_mask(0)
        thr0 = jnp.where((n * i) % nb == 0, BLK, 0) if halo else BLK
        mask0 = band & (kj >= thr0)
        mask_first = band & (kj >= BLK)

        def stat_rows(st_t, hp):
            lse_r = jnp.concatenate([st_t[2 * hp:2 * hp + 1, :], st_t[2 * hp + 1:2 * hp + 2, :]], axis=1)
            dl_r = jnp.concatenate([st_t[8 + 2 * hp:9 + 2 * hp, :], st_t[9 + 2 * hp:10 + 2 * hp, :]], axis=1)
            return lse_r, dl_r

        st_t = [st_ref[b * BLK:(b + 1) * BLK, :].T for b in range(n)]
        if halo:
            nxt_thr = jnp.where((n * i + n) % nb == 0, 2 * BLK, 0)
            _, kj1, qi1 = _band_mask(0, BLK)
            mask_next = kj1 >= qi1 + nxt_thr
            stn_t = stn_ref[...].T

        for hp in range(4):
            cs = slice(hp * 128, (hp + 1) * 128)
            kb = [k_ref[b * BLK:(b + 1) * BLK, cs] for b in range(n)]
            vb = [v_ref[b * BLK:(b + 1) * BLK, cs] for b in range(n)]
            dk_acc = [jnp.zeros((BLK, 128), F32) for _ in range(n)]
            dv_acc = [jnp.zeros((BLK, 128), F32) for _ in range(n)]
            for b in range(n):
                rs = slice(b * BLK, (b + 1) * BLK)
                q2s = _stack_heads(q_ref[rs, cs], in_a)
                do2s = _stack_heads(do_ref[rs, cs], in_a)
                if b == 0:
                    kprev = kp_ref[:, cs] if halo else kb[0]
                    vprev = vp_ref[:, cs] if halo else vb[0]
                    mask = mask0
                else:
                    kprev, vprev, mask = kb[b - 1], vb[b - 1], (mask_first if b % nb == 0 else band)
                kk = jnp.concatenate([kprev, kb[b]], axis=0)
                vv = jnp.concatenate([vprev, vb[b]], axis=0)
                lse_r, dl_r = stat_rows(st_t[b], hp)
                s_t = jnp.where(mask, _mm_nt(kk, q2s), NEG)
                p_t = jnp.exp(s_t - lse_r)
                ds_t = (p_t * (_mm_nt(vv, do2s) - dl_r)).astype(BF16)
                dkk = _mm(ds_t, q2s)
                dvv = _mm(p_t.astype(BF16), do2s)
                dqs = _mm_tn(ds_t, kk) * SCALE
                dq_ref[rs, cs] = jnp.where(in_a[0], dqs[:BLK], dqs[BLK:]).astype(BF16)
                dk_acc[b] += dkk[BLK:]
                dv_acc[b] += dvv[BLK:]
                if b > 0:
                    dk_acc[b - 1] += dkk[:BLK]
                    dv_acc[b - 1] += dvv[:BLK]
            if halo:
                q2s = _stack_heads(qn_ref[:, cs], in_a)
                do2s = _stack_heads(don_ref[:, cs], in_a)
                lse_r, dl_r = stat_rows(stn_t, hp)
                s_t = jnp.where(mask_next, _mm_nt(kb[n - 1], q2s), NEG)
                p_t = jnp.exp(s_t - lse_r)
                ds_t = (p_t * (_mm_nt(vb[n - 1], do2s) - dl_r)).astype(BF16)
                dk_acc[n - 1] += _mm(ds_t, q2s)
                dv_acc[n - 1] += _mm(p_t.astype(BF16), do2s)
            for b in range(n):
                dk_ref[b * BLK:(b + 1) * BLK, cs] = dk_acc[b].astype(BF16)
                dv_ref[b * BLK:(b + 1) * BLK, cs] = dv_acc[b].astype(BF16)

    cur = pl.BlockSpec((CH, AW), lambda i: (i, 0))
    cur_st = pl.BlockSpec((CH, 128), lambda i: (i, 0))
    prev = pl.BlockSpec((BLK, AW), lambda i: (jnp.maximum(n * i - 1, 0), 0))
    nxt = pl.BlockSpec((BLK, AW), lambda i: (jnp.minimum(n * i + n, NBLK - 1), 0))
    nxt_st = pl.BlockSpec((BLK, 128), lambda i: (jnp.minimum(n * i + n, NBLK - 1), 0))
    ins = [cur] * 4 + [cur_st] + ([prev, prev, nxt, nxt, nxt_st] if halo else [])
    args = (q, k, v, do, st) + ((k, v, q, do, st) if halo else ())
    return pl.pallas_call(
        body, name=name, grid=(T // CH,),
        in_specs=ins,
        out_specs=[cur] * 3,
        out_shape=[jax.ShapeDtypeStruct((T, AW), BF16)] * 3,
        compiler_params=_cp(("parallel",)),
    )(*args)


TH = 256
NCH = TH // CHUNK


def _hgrn_common(hq_ref, hf_ref, lbr_ref, tri_ref):
    r0 = lbr_ref[0:1, :]
    r1 = lbr_ref[1:2, :]
    mx = jnp.maximum(r0, r1)
    e0 = jnp.exp(r0 - mx)
    e1 = jnp.exp(r1 - mx)
    lb = e0 / (e0 + e1)
    hqv = hq_ref[...].astype(F32)
    sq = _sigmoid(hqv)
    qv = hqv * sq
    sf = _sigmoid(hf_ref[...].astype(F32))
    f = lb + (1.0 - lb) * sf
    kv = 1.0 - f
    g = jnp.log(f)
    cum = _mm_exact_l(tri_ref[...], g)
    dec = jnp.exp(jnp.concatenate([cum[c * CHUNK + CHUNK - 1:(c + 1) * CHUNK, :] for c in range(NCH)], axis=0))
    decb = jnp.concatenate([jnp.broadcast_to(dec[c:c + 1, :], (CHUNK, HW)) for c in range(NCH)], axis=0)
    ea = jnp.exp(cum)
    ena = jnp.exp(-cum)
    eend = decb * ena
    return dict(lb=lb, hq=hqv, sq=sq, q=qv, sf=sf, f=f, k=kv, cum=cum, ea=ea, ena=ena, eend=eend,
                qd=qv * ea, ki=kv * ena, ke=kv * eend, dec=dec)


def _tri_mask(transposed=False):
    ti = lax.broadcasted_iota(jnp.int32, (TH, TH), 1 if transposed else 0)
    si = lax.broadcasted_iota(jnp.int32, (TH, TH), 0 if transposed else 1)
    return (si <= ti) & ((si // CHUNK) == (ti // CHUNK))


def _hgrn_fwd(hq, hf, hi, lbr, tri):
    NSUB = 2

    def body(hq_ref, hf_ref, hi_ref, lbr_ref, tri_ref, rec_ref, sall_ref, st_scr):
        @pl.when(pl.program_id(0) == 0)
        def _():
            st_scr[...] = jnp.zeros_like(st_scr)

        causal = _tri_mask()
        for u in range(NSUB):
            tile = slice(u * TH, (u + 1) * TH)
            w = _hgrn_common(hq_ref.at[tile, :], hf_ref.at[tile, :], lbr_ref, tri_ref)
            qd, ki, ke = w["qd"].astype(BF16), w["ki"].astype(BF16), w["ke"].astype(BF16)
            dec = w["dec"]
            vb = hi_ref[tile, :]
            for h in range(4):
                cs = slice(h * 128, (h + 1) * 128)
                att = jnp.where(causal, _mm_nt(qd[:, cs], ki[:, cs]), 0.0)
                o_intra = _mm(att.astype(BF16), vb[:, cs])
                st = st_scr[:, cs]
                for c in range(NCH):
                    rs = slice(c * CHUNK, (c + 1) * CHUNK)
                    sall_ref[u * NCH + c, :, cs] = st
                    rec_ref[u * TH + c * CHUNK:u * TH + (c + 1) * CHUNK, cs] = (
                        o_intra[rs] + _mm_nt(qd[rs, cs], st.astype(BF16))).astype(BF16)
                    st = dec[c:c + 1, cs] * st + _mm_tn(vb[rs, cs], ke[rs, cs])
                st_scr[:, cs] = st

    tok = pl.BlockSpec((NSUB * TH, HW), lambda i: (i, 0))
    return pl.pallas_call(
        body, name="hgrn_fwd", grid=(T // (NSUB * TH),),
        in_specs=[tok, tok, tok, pl.BlockSpec((2, HW), lambda i: (0, 0)), pl.BlockSpec((TH, TH), lambda i: (0, 0))],
        out_specs=[tok, pl.BlockSpec((NSUB * NCH, 128, HW), lambda i: (i, 0, 0))],
        out_shape=[jax.ShapeDtypeStruct((T, HW), BF16), jax.ShapeDtypeStruct((T // CHUNK, 128, HW), F32)],
        scratch_shapes=[pltpu.VMEM((128, HW), F32)],
        compiler_params=_cp(("arbitrary",)),
    )(hq, hf, hi, lbr, tri)


def _hgrn_bwd(hq, hf, hi, lbr, tri, trit, drec, sall, dhg, rout, routb):
    NSUB = 2
    NT = T // (NSUB * TH)

    def body(hq_ref, hf_ref, hi_ref, lbr_ref, tri_ref, trit_ref, do_ref, sall_ref, dhg_ref, rout_r, routb_r,
             dph_ref, small_ref, pout_o, poutr_o,
             dst_scr, dlb_scr, dqd_scr, dki_scr, dke_scr, dlast_scr, send_sems, recv_sems, loc_sems):
        step = pl.program_id(0)
        loc, rem = _chip_copies(_w_out_piece, rout_r, routb_r, pout_o, poutr_o, send_sems, recv_sems,
                                loc_sems.at[0])

        @pl.when(step == 0)
        def _():
            dst_scr[...] = jnp.zeros_like(dst_scr)
            dlb_scr[...] = jnp.zeros_like(dlb_scr)
            for cp in loc + rem:
                cp.start()

        causal = _tri_mask()
        causal_t = _tri_mask(transposed=True)
        lb = None
        for u in reversed(range(NSUB)):
            tile = slice(u * TH, (u + 1) * TH)
            w = _hgrn_common(hq_ref.at[tile, :], hf_ref.at[tile, :], lbr_ref, tri_ref)
            qd, ki, ke = w["qd"].astype(BF16), w["ki"].astype(BF16), w["ke"].astype(BF16)
            dec = w["dec"]
            vb = hi_ref[tile, :]
            dob = do_ref[tile, :].astype(BF16)
            for h in range(4):
                cs = slice(h * 128, (h + 1) * 128)
                att_t = jnp.where(causal_t, _mm_nt(ki[:, cs], qd[:, cs]), 0.0).astype(BF16)
                datt_t = jnp.where(causal_t, _mm_nt(vb[:, cs], dob[:, cs]), 0.0).astype(BF16)
                datt = jnp.where(causal, _mm_nt(dob[:, cs], vb[:, cs]), 0.0).astype(BF16)
                dv_intra = _mm(att_t, dob[:, cs])
                dqd_intra = _mm(datt, ki[:, cs])
                dki_scr[u, :, cs] = _mm(datt_t, qd[:, cs])
                dst = dst_scr[:, cs]
                for c in reversed(range(NCH)):
                    rs = slice(c * CHUNK, (c + 1) * CHUNK)
                    dec_c = dec[c:c + 1, :]
                    st = sall_ref[u * NCH + c, :, cs]
                    dstb = dst.astype(BF16)
                    dph_ref[u * TH + c * CHUNK:u * TH + (c + 1) * CHUNK, 2 * HW + h * 128:2 * HW + (h + 1) * 128] = (
                        dv_intra[rs] + _mm_nt(ke[rs, cs], dstb)).astype(BF16)
                    dqd_scr[u, rs, cs] = dqd_intra[rs] + _mm(dob[rs, cs], st.astype(BF16))
                    dke_scr[u, rs, cs] = _mm(vb[rs, cs], dstb)
                    ddec = jnp.sum(dst * st, axis=0, keepdims=True)
                    dlast_scr[u, c:c + 1, cs] = ddec * dec_c[:, cs]
                    dst = dec_c[:, cs] * dst + _mm_tn(dob[rs, cs], qd[rs, cs])
                dst_scr[:, cs] = dst
            dqd, dki, dke = dqd_scr[u], dki_scr[u], dke_scr[u]
            dq = dqd * w["ea"]
            dk = dki * w["ena"] + dke * w["eend"]
            dcum = dqd * w["qd"] - dki * w["ki"] - dke * w["ke"]
            dkeke = dke * w["ke"]
            dlastb = jnp.concatenate(
                [jnp.broadcast_to(dlast_scr[u, c:c + 1, :]
                                  + jnp.sum(dkeke[c * CHUNK:(c + 1) * CHUNK], axis=0, keepdims=True), (CHUNK, HW))
                 for c in range(NCH)], axis=0)
            dg = _mm_exact_l(trit_ref[...], dcum) + dlastb
            df = dg / w["f"] - dk
            lb, sf, sq = w["lb"], w["sf"], w["sq"]
            dph_ref[tile, HW:2 * HW] = (df * (1.0 - lb) * sf * (1.0 - sf)).astype(BF16)
            dph_ref[tile, 0:HW] = (dq * (sq * (1.0 + w["hq"] * (1.0 - sq)))).astype(BF16)
            dph_ref[tile, 3 * HW:4 * HW] = dhg_ref[tile, :]
            dlb_scr[...] += jnp.sum(df * (1.0 - sf), axis=0, keepdims=True)

        @pl.when(step == NT - 1)
        def _():
            gr = dlb_scr[...] * lb * (1.0 - lb)
            small_ref[...] = jnp.zeros_like(small_ref)
            small_ref[0:1, 0:HW] = gr
            small_ref[1:2, 0:HW] = -gr
            for cp in rem:
                cp.wait_recv()
            for cp in rem:
                cp.wait_send()
            for cp in loc:
                cp.wait()

    tok = pl.BlockSpec((NSUB * TH, HW), lambda i: (NT - 1 - i, 0))
    const = lambda shape: pl.BlockSpec(shape, lambda i: (0,) * len(shape))
    hbm = pl.BlockSpec(memory_space=pltpu.HBM)
    return pl.pallas_call(
        body, name="hgrn_bwd", grid=(NT,),
        in_specs=[tok, tok, tok, const((2, HW)), const((TH, TH)), const((TH, TH)), tok,
                  pl.BlockSpec((NSUB * NCH, 128, HW), lambda i: (NT - 1 - i, 0, 0)), tok, hbm, hbm],
        out_specs=[pl.BlockSpec((NSUB * TH, NCOL // 2), lambda i: (NT - 1 - i, 0)), const((8, D)), hbm, hbm],
        out_shape=[jax.ShapeDtypeStruct((T, NCOL // 2), BF16), jax.ShapeDtypeStruct((8, D), F32),
                   jax.ShapeDtypeStruct((128, D), F32), jax.ShapeDtypeStruct((3, 128, D), BF16)],
        scratch_shapes=[pltpu.VMEM((128, HW), F32), pltpu.VMEM((1, HW), F32), pltpu.VMEM((NSUB, TH, HW), F32),
                        pltpu.VMEM((NSUB, TH, HW), F32), pltpu.VMEM((NSUB, TH, HW), F32),
                        pltpu.VMEM((NSUB, 8, HW), F32),
                        pltpu.SemaphoreType.DMA((3,)), pltpu.SemaphoreType.DMA((3,)), pltpu.SemaphoreType.DMA((1,))],
        compiler_params=_cp(("arbitrary",)),
    )(hq, hf, hi, lbr, tri, trit, drec, sall, dhg, rout, routb)


def _fwd_out(o1, o4, o16, l1, l4, l16, rec, ag, hg, x, tgt, anw, hnw, fnw, wout_full, gmat, emat, selmat):
    TT = 512

    def body(o1_r, o4_r, o16_r, l1_r, l4_r, l16_r, rec_r, ag_r, hg_r, x_r, tgt_r, anw_r, hnw_r, fnw_r, wo_r, g_r,
             e_r, sel_r, dx2_o, do1_o, do4_o, do16_o, st1_o, st4_o, st16_o, drec_o, dag_o, dhg_o,
             rout_o, routb_o, small_o, scr_a, scr_b, scr_c, gwout_o, rbuf, send_sems, recv_sems):
        @pl.when(pl.program_id(0) == 0)
        def _():
            gwout_o[...] = jnp.zeros_like(gwout_o)
            small_o[...] = jnp.zeros_like(small_o)

        def unperm(r4, r16):
            return _unperm_load(r4, r16, scr_a, scr_b, scr_c)

        def perm_out(val, p1, p4, p16, dt):
            _perm_store(val, scr_a, scr_b, p1, p4, p16, dt)

        o4u, o16u = unperm(o4_r, o16_r)
        l4c, l16c = unperm(l4_r, l16_r)
        l1c = l1_r[...]
        mxc = jnp.maximum(jnp.maximum(l1c, l4c), l16c)
        w1c, w4c, w16c = jnp.exp(l1c - mxc), jnp.exp(l4c - mxc), jnp.exp(l16c - mxc)
        denc = w1c + w4c + w16c
        lane = lax.broadcasted_iota(jnp.int32, (1, 128), 1)
        lse_c = jnp.where(lane < 8, mxc + jnp.log(denc), 0.0)
        em = e_r[...]
        wn1 = _mm_exact_r(w1c / denc, em)
        wn4 = _mm_exact_r(w4c / denc, em)
        o1v = o1_r[...].astype(F32)
        attn = wn1 * o1v + wn4 * o4u + (1.0 - wn1 - wn4) * o16u
        gm = g_r[...]

        def head_mean_a(t):
            return jnp.concatenate([_mm_exact_r(t[:, :256], gm), _mm_exact_r(t[:, 256:], gm)], axis=1)

        def head_mean_h(t):
            return jnp.concatenate(
                [jnp.broadcast_to(jnp.mean(t[:, h * 128:(h + 1) * 128], axis=-1, keepdims=True), (TT, 128))
                 for h in range(4)], axis=1)

        rs_a = lax.rsqrt(head_mean_a(attn * attn) + EPS)
        n_a = attn * rs_a
        agv = ag_r[...].astype(F32)
        sg_a = _sigmoid(agv)
        si_a = agv * sg_a
        anw_v = anw_r[...]
        y_a = (n_a * anw_v) * si_a
        recv = rec_r[...].astype(F32)
        rs_h = lax.rsqrt(head_mean_h(recv * recv) + EPS)
        n_h = recv * rs_h
        hgv = hg_r[...].astype(F32)
        sg_h = _sigmoid(hgv)
        si_h = hgv * sg_h
        hnw_v = hnw_r[...]
        y_h = (n_h * hnw_v) * si_h
        mixed = jnp.concatenate([y_a, y_h], axis=1).astype(BF16)
        xv = x_r[...]
        x2 = xv + _mm(mixed, wo_r[...])
        r2 = lax.rsqrt(jnp.mean(x2 * x2, axis=-1, keepdims=True) + EPS)
        fnw_v = fnw_r[...]
        xn = x2 * r2
        err = xn * fnw_v - tgt_r[...]
        small_o[2:3, :] += 0.5 * jnp.sum(jnp.mean(err * err, axis=-1, keepdims=True), axis=0, keepdims=True)
        small_o[0:1, :] += jnp.sum(err * xn, axis=0, keepdims=True) * (1.0 / D)
        dyw = err * (fnw_v * (1.0 / D))
        dx2 = r2 * dyw - x2 * ((r2 * r2 * r2) * jnp.mean(dyw * x2, axis=-1, keepdims=True))
        dx2_o[...] = dx2
        dx2b = dx2.astype(BF16)
        gwout_o[...] += _mm_tn(mixed, dx2b)
        dmix = _mm_nt(dx2b, wo_r[...])
        dm_a, dm_h = dmix[:, :AW], dmix[:, AW:]
        dag_o[...] = (dm_a * (n_a * anw_v) * (sg_a * (1.0 + agv * (1.0 - sg_a)))).astype(BF16)
        dy_a = dm_a * si_a
        dn_a = dy_a * anw_v
        small_o[1:2, 0:AW] += jnp.sum(dy_a * n_a, axis=0, keepdims=True)
        dattn = rs_a * (dn_a - n_a * head_mean_a(dn_a * n_a))
        perm_out(dattn, do1_o, do4_o, do16_o, BF16)
        stats = lse_c + _mm_exact_r(dattn * attn, sel_r[...])
        perm_out(stats, st1_o, st4_o, st16_o, F32)
        dhg_o[...] = (dm_h * (n_h * hnw_v) * (sg_h * (1.0 + hgv * (1.0 - sg_h)))).astype(BF16)
        dy_h = dm_h * si_h
        dn_h = dy_h * hnw_v
        small_o[1:2, AW:] += jnp.sum(dy_h * n_h, axis=0, keepdims=True)
        drec_o[...] = (rs_h * (dn_h - n_h * head_mean_h(dn_h * n_h))).astype(BF16)

        @pl.when(pl.program_id(0) == T // TT - 1)
        def _():
            x, y, c = lax.axis_index("x"), lax.axis_index("y"), lax.axis_index("c")
            cps = [pltpu.make_async_remote_copy(
                src_ref=gwout_o.at[pl.ds(pl.multiple_of(j * 256 + (1 - c) * 128, 128), 128), :], dst_ref=rbuf.at[j],
                send_sem=send_sems.at[j], recv_sem=recv_sems.at[j], device_id=(x, y, 1 - c), device_id_type=MESH)
                for j in range(4)]
            for cp in cps:
                cp.start()
            for j, cp in enumerate(cps):
                cp.wait_recv()
                red = gwout_o[pl.ds(pl.multiple_of(j * 256 + c * 128, 128), 128), :] + rbuf[j]
                rout_o[j * 128:(j + 1) * 128, :] = red
                routb_o[j * 128:(j + 1) * 128, :] = red.astype(BF16)
            for cp in cps:
                cp.wait_send()

    tok = lambda w: pl.BlockSpec((TT, w), lambda i: (i, 0))
    d4 = pl.BlockSpec((4, TT // 4, AW), lambda i: (0, i, 0))
    d16 = pl.BlockSpec((16, TT // 16, AW), lambda i: (0, i, 0))
    const = lambda shape: pl.BlockSpec(shape, lambda i: (0,) * len(shape))
    sd = lambda shape, dt: jax.ShapeDtypeStruct(shape, dt)
    c4 = pl.BlockSpec((4, TT // 4, 128), lambda i: (0, i, 0))
    c16 = pl.BlockSpec((16, TT // 16, 128), lambda i: (0, i, 0))
    p3 = lambda w, dt: [sd((T, w), dt), sd((4, T // 4, w), dt), sd((16, T // 16, w), dt)]
    return pl.pallas_call(
        body, name="fwd_out", grid=(T // TT,),
        in_specs=[tok(AW), d4, d16, tok(128), c4, c16, tok(AW), tok(AW), tok(AW), tok(D), tok(D),
                  const((1, AW)), const((1, HW)), const((1, D)), const((D, D)), const((256, 256)),
                  const((128, AW)), const((AW, 128))],
        out_specs=[tok(D)] + [tok(AW), d4, d16] + [tok(128), c4, c16] + [tok(AW)] * 3
        + [const((512, D)), const((512, D)), const((8, D))],
        out_shape=[sd((T, D), F32)] + p3(AW, BF16) + p3(128, F32)
        + [sd((T, AW), BF16), sd((T, AW), BF16), sd((T, AW), BF16), sd((512, D), F32), sd((512, D), BF16),
           sd((8, D), F32)],
        scratch_shapes=[pltpu.VMEM((4, TT, 128), F32)] * 3 + [pltpu.VMEM((D, D), F32),
                        pltpu.VMEM((4, 128, D), F32), pltpu.SemaphoreType.DMA((4,)), pltpu.SemaphoreType.DMA((4,))],
        compiler_params=_cp(("arbitrary",)),
    )(o1, o4, o16, l1, l4, l16, rec, ag, hg, x, tgt, anw, hnw, fnw, wout_full, gmat, emat, selmat)


def _dproj_build(dq, dk, dv, dag, pos):
    TT = 512

    def body(dq1, dq4, dq16, dk1, dk4, dk16, dv1, dv4, dv16, dag_r, pos_r, dproj_o, scr_b, scr_c):
        def unperm_sum(r1, r4, r16):
            return r1[...] + _unperm_sum(r4, r16, scr_b, scr_c)

        cosf, s1, s2 = _rope_tables(pos_r[...])
        dproj_o[:, 0:512] = _rope_bwd(unperm_sum(dq1, dq4, dq16), cosf, s1, s2).astype(BF16)
        dproj_o[:, 512:1024] = _rope_bwd(unperm_sum(dk1, dk4, dk16), cosf, s1, s2).astype(BF16)
        dproj_o[:, 1024:1536] = unperm_sum(dv1, dv4, dv16).astype(BF16)
        dproj_o[:, 1536:2048] = dag_r[...]

    tok = lambda w: pl.BlockSpec((TT, w), lambda i: (i, 0))
    d4 = pl.BlockSpec((4, TT // 4, AW), lambda i: (0, i, 0))
    d16 = pl.BlockSpec((16, TT // 16, AW), lambda i: (0, i, 0))
    return pl.pallas_call(
        body, name="dproj_build", grid=(T // TT,),
        in_specs=[tok(AW), d4, d16] * 3 + [tok(AW), pl.BlockSpec((1, TT), lambda i: (0, i))],
        out_specs=tok(NCOL // 2),
        out_shape=jax.ShapeDtypeStruct((T, NCOL // 2), BF16),
        scratch_shapes=[pltpu.VMEM((4, TT, 128), F32)] * 2,
        compiler_params=_cp(("parallel",)),
    )(*dq, *dk, *dv, dag, pos)


def _bwd_x(dproj_a, dproj_h, x, dx2, mixw, w_full, rin, rinb, small4, small6, pout_own, pout_rem):
    TT = 256
    NT = T // TT

    def body(dpa_r, dph_r, x_r, dx2_r, mw_r, w_r, rin_r, rinb_r, s4_r, s6_r, poo_r, por_r,
             gx_o, sall_o, fin_o, fout_o, sbuf, v_own, v_rem, vo_own, vo_rem, sin, sout, got_in,
             got_out, send_sems, recv_sems, loc_sems, share_send, share_recv, fin_sems):
        i = pl.program_id(0)
        loc, rem = _chip_copies(_w_in_piece, rin_r, rinb_r, v_own, v_rem, send_sems, recv_sems, loc_sems.at[0])
        loads = [pltpu.make_async_copy(poo_r, vo_own, fin_sems.at[2]),
                 pltpu.make_async_copy(por_r, vo_rem, fin_sems.at[3])]

        @pl.when(i == 0)
        def _():
            sbuf[...] = jnp.zeros_like(sbuf)
            for cp in loc + rem + loads:
                cp.start()

        dhn = _mm_nt(dpa_r[...], w_r[:, 0:NCOL // 2]) + _mm_nt(dph_r[...], w_r[:, NCOL // 2:NCOL])
        xv = x_r[...]
        r = lax.rsqrt(jnp.mean(xv * xv, axis=-1, keepdims=True) + EPS)
        dxw = dhn * mw_r[...]
        gx_o[...] = dx2_r[...] + r * dxw - xv * ((r * r * r) * jnp.mean(dxw * xv, axis=-1, keepdims=True))
        sbuf[16:17, :] += jnp.sum(dhn * (xv * r), axis=0, keepdims=True)

        @pl.when(i == NT - 1)
        def _():
            sbuf[0:8, :] = s4_r[...]
            sbuf[8:16, :] = s6_r[...]
            sloc, srem = _small_copies(sbuf, sall_o, send_sems, recv_sems, loc_sems.at[1])
            for cp in sloc + srem:
                cp.start()
            for cp in rem:
                cp.wait_recv()
            for cp in rem:
                cp.wait_send()
            for cp in loc:
                cp.wait()
            mx, my, c = lax.axis_index("x"), lax.axis_index("y"), lax.axis_index("c")
            for cp in loads:
                cp.wait()
            sout[...] = ((vo_own[...] + vo_rem[0].astype(F32)) + vo_rem[1].astype(F32)) + vo_rem[2].astype(F32)
            sin[...] = ((v_own[...] + v_rem[0].astype(F32)) + v_rem[1].astype(F32)) + v_rem[2].astype(F32)
            swap = [pltpu.make_async_remote_copy(src_ref=sin, dst_ref=got_in, send_sem=share_send.at[0],
                                                 recv_sem=share_recv.at[0], device_id=(mx, my, 1 - c),
                                                 device_id_type=MESH),
                    pltpu.make_async_remote_copy(src_ref=sout, dst_ref=got_out, send_sem=share_send.at[1],
                                                 recv_sem=share_recv.at[1], device_id=(mx, my, 1 - c),
                                                 device_id_type=MESH)]
            for cp in swap:
                cp.start()
            mine = [pltpu.make_async_copy(sin, fin_o.at[c], fin_sems.at[0]),
                    pltpu.make_async_copy(sout, fout_o.at[c], fin_sems.at[1])]
            for cp in mine:
                cp.start()
            for cp in swap:
                cp.wait_recv()
            theirs = [pltpu.make_async_copy(got_in, fin_o.at[1 - c], fin_sems.at[2]),
                      pltpu.make_async_copy(got_out, fout_o.at[1 - c], fin_sems.at[3])]
            for cp in theirs:
                cp.start()
            for cp in swap:
                cp.wait_send()
            for cp in mine + theirs:
                cp.wait()
            for cp in srem:
                cp.wait_recv()
            for cp in srem:
                cp.wait_send()
            for cp in sloc:
                cp.wait()

    tok = lambda w: pl.BlockSpec((TT, w), lambda i: (i, 0))
    const = lambda shape: pl.BlockSpec(shape, lambda i: (0,) * len(shape))
    hbm = pl.BlockSpec(memory_space=pltpu.HBM)
    return pl.pallas_call(
        body, name="bwd_x", grid=(NT,),
        in_specs=[tok(NCOL // 2), tok(NCOL // 2), tok(D), tok(D), const((1, D)), const((D, NCOL)), hbm, hbm,
                  const((8, D)), const((8, D)), hbm, hbm],
        out_specs=[tok(D), hbm, hbm, hbm],
        out_shape=[jax.ShapeDtypeStruct((T, D), F32),
                   jax.ShapeDtypeStruct((8, 24, D), F32),
                   jax.ShapeDtypeStruct((2, 512, 1024), F32), jax.ShapeDtypeStruct((2, 128, D), F32)],
        scratch_shapes=[pltpu.VMEM((24, D), F32),
                        pltpu.VMEM((512, 1024), F32), pltpu.VMEM((3, 512, 1024), BF16),
                        pltpu.VMEM((128, D), F32), pltpu.VMEM((3, 128, D), BF16),
                        pltpu.VMEM((512, 1024), F32), pltpu.VMEM((128, D), F32),
                        pltpu.VMEM((512, 1024), F32), pltpu.VMEM((128, D), F32),
                        pltpu.SemaphoreType.DMA((10,)), pltpu.SemaphoreType.DMA((10,)), pltpu.SemaphoreType.DMA((2,)),
                        pltpu.SemaphoreType.DMA((2,)), pltpu.SemaphoreType.DMA((2,)), pltpu.SemaphoreType.DMA((4,))],
        compiler_params=_cp(("arbitrary",)),
    )(dproj_a, dproj_h, x, dx2, mixw, w_full, rin, rinb, small4, small6, pout_own, pout_rem)


def _grad_w_in(hn, dproj_a, dproj_h):
    TK = 2048
    NK = T // TK

    def body(hnt_r, dpa_r, dph_r, rin_o, rinb_o, acc, rbuf, obuf, obufb, send_sems, recv_sems, wb_sems):
        j = pl.program_id(0)
        kk = pl.program_id(1)
        x, y, c = lax.axis_index("x"), lax.axis_index("y"), lax.axis_index("c")
        mine = pl.ds(pl.multiple_of(c * 512, 512), 512)
        theirs = pl.ds(pl.multiple_of((1 - c) * 512, 512), 512)

        def send(jj, h):
            cols = pl.ds(h * 512, 512)
            return pltpu.make_async_remote_copy(
                src_ref=acc.at[jj % 2, theirs, cols], dst_ref=rbuf.at[jj, :, cols], send_sem=send_sems.at[2 * jj + h],
                recv_sem=recv_sems.at[2 * jj + h], device_id=(x, y, 1 - c), device_id_type=MESH)

        def writeback(jj):
            cols = pl.ds(jj * 1024, 1024)
            return [pltpu.make_async_copy(obuf.at[jj % 2], rin_o.at[:, cols], wb_sems.at[jj % 2]),
                    pltpu.make_async_copy(obufb.at[jj % 2], rinb_o.at[:, cols], wb_sems.at[2 + jj % 2])]

        def wait_writeback(jj):
            for cp in writeback(jj):
                cp.wait()

        def finalize(jj):
            for h in range(2):
                send(jj, h).wait_recv()
            red = acc[jj % 2, mine, :] + rbuf[jj]
            obuf[jj % 2] = red
            obufb[jj % 2] = red.astype(BF16)
            for cp in writeback(jj):
                cp.start()

        @pl.when(kk == 0)
        def _():
            for jj in (2, 3):
                @pl.when(j == jj)
                def _():
                    for h in range(2):
                        send(jj - 2, h).wait_send()

        last = (j == 3) & (kk == NK - 1)

        @pl.when(jnp.logical_not(last))
        def _():
            prod = _mm(hnt_r[...], jnp.where(j < 2, dpa_r[...], dph_r[...]))

            @pl.when(kk == 0)
            def _():
                acc[j % 2] = prod

            @pl.when(kk > 0)
            def _():
                acc[j % 2] += prod

            @pl.when(kk == NK - 1)
            def _():
                for jj in range(3):
                    @pl.when(j == jj)
                    def _():
                        for h in range(2):
                            send(jj, h).start()

        @pl.when(last)
        def _():
            for h in range(2):
                cols = slice(h * 512, (h + 1) * 512)
                acc[1, :, cols] += _mm(hnt_r[...], dph_r[:, cols])
                send(3, h).start()

        @pl.when(kk == NK - 1)
        def _():
            for jj in range(4):
                @pl.when(j == jj)
                def _():
                    if jj in (1, 2):
                        finalize(jj - 1)
                    if jj == 3:
                        wait_writeback(0)
                        finalize(2)
                        wait_writeback(1)
                        finalize(3)
                        wait_writeback(2)
                        wait_writeback(3)
                        for h in range(2):
                            send(2, h).wait_send()
                            send(3, h).wait_send()

    hbm = pl.BlockSpec(memory_space=pltpu.HBM)
    return pl.pallas_call(
        body, name="grad_w_in", grid=(4, NK),
        in_specs=[pl.BlockSpec((D, TK), lambda j, kk: (0, kk)),
                  pl.BlockSpec((TK, 1024), lambda j, kk: (jnp.where(j < 2, kk, NK - 1), jnp.minimum(j, 1))),
                  pl.BlockSpec((TK, 1024), lambda j, kk: (jnp.where(j < 2, 0, kk), jnp.maximum(j - 2, 0)))],
        out_specs=[hbm, hbm],
        out_shape=[jax.ShapeDtypeStruct((512, NCOL), F32), jax.ShapeDtypeStruct((512, NCOL), BF16)],
        scratch_shapes=[pltpu.VMEM((2, D, 1024), F32), pltpu.VMEM((4, 512, 1024), F32), pltpu.VMEM((2, 512, 1024), F32),
                        pltpu.VMEM((2, 512, 1024), BF16),
                        pltpu.SemaphoreType.DMA((8,)), pltpu.SemaphoreType.DMA((8,)), pltpu.SemaphoreType.DMA((4,))],
        compiler_params=_cp(("arbitrary", "arbitrary")),
    )(hn, dproj_a, dproj_h)


def _w_in_piece(ref, j):
    return ref.at[:, pl.ds(j * 1024, 1024)]


def _w_out_piece(ref, j):
    return ref.at[pl.ds(j * 128, 128), :]


def _chip_copies(piece, src_r, srcb_r, own_o, rem_o, send_sems, recv_sems, loc_sem):
    x, y, c = lax.axis_index("x"), lax.axis_index("y"), lax.axis_index("c")
    chips = [(1 - x, y), (x, 1 - y), (1 - x, 1 - y)]
    loc = [pltpu.make_async_copy(piece(src_r, 2 * x + y), own_o, loc_sem)]
    rem = [pltpu.make_async_remote_copy(
        src_ref=piece(srcb_r, 2 * px + py), dst_ref=rem_o.at[k], send_sem=send_sems.at[k],
        recv_sem=recv_sems.at[k], device_id=(px, py, c), device_id_type=MESH) for k, (px, py) in enumerate(chips)]
    return loc, rem


def _small_copies(small_r, sall_o, send_sems, recv_sems, loc_sem):
    x, y, c = lax.axis_index("x"), lax.axis_index("y"), lax.axis_index("c")
    me = 4 * x + 2 * y + c
    loc = [pltpu.make_async_copy(small_r, sall_o.at[me], loc_sem)]
    rem = []
    k = 3
    for fx in range(2):
        for fy in range(2):
            for fc in range(2):
                if fx or fy or fc:
                    peer = (1 - x if fx else x, 1 - y if fy else y, 1 - c if fc else c)
                    rem.append(pltpu.make_async_remote_copy(
                        src_ref=small_r, dst_ref=sall_o.at[me], send_sem=send_sems.at[k],
                        recv_sem=recv_sems.at[k], device_id=peer, device_id_type=MESH))
                    k += 1
    return loc, rem


def _adamw_math(w, g, m, v):
    m = B1 * m + (1.0 - B1) * g
    v = B2 * v + (1.0 - B2) * (g * g)
    m_hat = m / (1.0 - B1 ** STEP)
    v_hat = v / (1.0 - B2 ** STEP)
    delta = -LR * (m_hat / (jnp.sqrt(v_hat) + AEPS) + WD * w)
    return delta, m, v


def _adamw(big_in, big_out, sall, params):
    def body(*refs):
        wi, gi, mi, vi, wo, go, mo, vo, sall_r = refs[:9]
        ins = refs[9:24]
        di_o, mi_o, vi_o, do_o, mo_o, vo_o = refs[24:30]
        outs = refs[30:]
        d, mm, vv = _adamw_math(wi[...], gi[...], mi[...], vi[...])
        di_o[...] = d
        mi_o[...] = mm
        vi_o[...] = vv

        @pl.when(pl.program_id(0) == 0)
        def _():
            d, mm, vv = _adamw_math(wo[...], go[...], mo[...], vo[...])
            do_o[...] = d
            mo_o[...] = mm
            vo_o[...] = vv
            tot = sall_r[0]
            for dv in range(1, 8):
                tot = tot + sall_r[dv]
            grads = [tot[16:17, :], tot[1:2, 0:AW], tot[1:2, AW:], tot[8:10, 0:HW], tot[0:1, :]]
            outs[0][...] = tot[2:3, 0:1]
            for p in range(5):
                w_r, m_r, v_r = ins[3 * p:3 * p + 3]
                g = grads[p]
                d, mm, vv = _adamw_math(w_r[...], g, m_r[...], v_r[...])
                outs[1 + 4 * p][...] = g
                outs[2 + 4 * p][...] = d
                outs[3 + 4 * p][...] = mm
                outs[4 + 4 * p][...] = vv

    flat = [a for p in params for a in p]
    shapes = [jax.ShapeDtypeStruct((D, 1024), F32)] * 3 + [jax.ShapeDtypeStruct((256, D), F32)] * 3
    shapes += [jax.ShapeDtypeStruct((1, 1), F32)]
    for p in params:
        shapes += [jax.ShapeDtypeStruct(p[0].shape, F32)] * 4
    vm = pl.BlockSpec(memory_space=pltpu.VMEM)
    rows = pl.BlockSpec((512, 1024), lambda i: (i, 0))
    whole = pl.BlockSpec((256, D), lambda i: (0, 0))
    return pl.pallas_call(
        body, name="adamw", grid=(2,),
        in_specs=[rows] * 4 + [whole] * 4 + [vm] * 16, out_specs=[rows] * 3 + [whole] * 3 + [vm] * 21,
        out_shape=shapes,
        compiler_params=_cp(("arbitrary",)),
    )(*big_in, *big_out, sall, *flat)


def kernel(x, positions, w_in, w_out, mix_norm_w, attn_out_norm_w, hgrn_out_norm_w, hgrn_lb_raw, final_norm_w, loss_target, m_w_in, m_w_out, m_mix_norm_w, m_attn_out_norm_w, m_hgrn_out_norm_w, m_hgrn_lb_raw, m_final_norm_w, v_w_in, v_w_out, v_mix_norm_w, v_attn_out_norm_w, v_hgrn_out_norm_w, v_hgrn_lb_raw, v_final_norm_w):
    xs = x.reshape(T, D)
    tgt = loss_target.reshape(T, D)
    pos = positions.reshape(1, T)
    fnw = final_norm_w.reshape(1, D)

    ti = np.arange(TH)
    tri_np = ((ti[:, None] // CHUNK == ti[None, :] // CHUNK) & (ti[None, :] <= ti[:, None])).astype(np.float32)
    tri = jnp.asarray(tri_np, BF16)
    trit = jnp.asarray(tri_np.T, BF16)
    hi_ = np.arange(AW) // HEAD
    gmat = jnp.asarray((hi_[:256, None] == hi_[None, :256]).astype(np.float32) / HEAD, BF16)
    emat_np = (np.arange(128)[:, None] == hi_[None, :]).astype(np.float32)
    sel_np = (8 + hi_[:, None] == np.arange(128)[None, :]).astype(np.float32)
    emat = jnp.asarray(emat_np, BF16)
    selmat = jnp.asarray(sel_np, BF16)

    jm_arr = (2 * lax.axis_index("x") + lax.axis_index("y")).astype(jnp.int32).reshape(1)
    (hn, q1, k1, v1, q4, k4, v4, q16, k16, v16, ag, hq, hf, hi, hg, w_full, wout4) = _fwd_in(
        xs, pos, mix_norm_w, w_in.reshape(D, 1024), w_out.reshape(256, D), jm_arr)
    wout_full = wout4.reshape(D, D)
    flat = lambda a: a.reshape(T, AW)
    o1, l1 = _attn_fwd(q1, k1, v1, T // BLK, "attn_fwd_d1")
    o4, l4 = _attn_fwd(flat(q4), flat(k4), flat(v4), T // 4 // BLK, "attn_fwd_d4")
    o16, l16 = _attn_fwd(flat(q16), flat(k16), flat(v16), T // 16 // BLK, "attn_fwd_d16")
    rec, sall = _hgrn_fwd(hq, hf, hi, hgrn_lb_raw, tri)

    (dx2, do1, do4, do16, st1, st4, st16, drec, dag, dhg, rout, routb, small4) = _fwd_out(
        o1, o4.reshape(4, T // 4, AW), o16.reshape(16, T // 16, AW),
        l1, l4.reshape(4, T // 4, 128), l16.reshape(16, T // 16, 128),
        rec, ag, hg, xs, tgt, attn_out_norm_w, hgrn_out_norm_w, fnw, wout_full, gmat, emat, selmat)

    fst = lambda a: a.reshape(T, 128)
    dq1, dk1, dv1 = _attn_bwd(q1, k1, v1, do1, st1, T // BLK, "attn_bwd_d1")
    dq4, dk4, dv4 = _attn_bwd(flat(q4), flat(k4), flat(v4), flat(do4), fst(st4), T // 4 // BLK, "attn_bwd_d4")
    dq16, dk16, dv16 = _attn_bwd(flat(q16), flat(k16), flat(v16), flat(do16), fst(st16), T // 16 // BLK,
                                 "attn_bwd_d16")
    dproj_h, small6, pout_own, pout_rem = _hgrn_bwd(hq, hf, hi, hgrn_lb_raw, tri, trit, drec, sall, dhg,
                                                    rout, routb)

    r4 = lambda a: a.reshape(4, T // 4, AW)
    r16 = lambda a: a.reshape(16, T // 16, AW)
    dproj_a = _dproj_build((dq1, r4(dq4), r16(dq16)), (dk1, r4(dk4), r16(dk16)), (dv1, r4(dv4), r16(dv16)),
                           dag, pos)
    rin, rinb = _grad_w_in(hn, dproj_a, dproj_h)
    gx, small_all, fin, fout = _bwd_x(dproj_a, dproj_h, xs, dx2, mix_norm_w, w_full, rin, rinb,
                                            small4, small6, pout_own, pout_rem)
    g_w_in = fin.reshape(D, 1024)
    g_w_out = fout.reshape(256, D)

    params = [(mix_norm_w, m_mix_norm_w, v_mix_norm_w),
              (attn_out_norm_w, m_attn_out_norm_w, v_attn_out_norm_w),
              (hgrn_out_norm_w, m_hgrn_out_norm_w, v_hgrn_out_norm_w),
              (hgrn_lb_raw, m_hgrn_lb_raw, v_hgrn_lb_raw),
              (fnw, m_final_norm_w.reshape(1, D), v_final_norm_w.reshape(1, D))]
    d_in, nm_in, nv_in, d_out, nm_out, nv_out, *so = _adamw(
        (w_in.reshape(D, 1024), g_w_in, m_w_in.reshape(D, 1024), v_w_in.reshape(D, 1024)),
        (w_out.reshape(256, D), g_w_out, m_w_out.reshape(256, D), v_w_out.reshape(256, D)), small_all, params)
    loss = so[0].reshape(())
    g_s = [so[1 + 4 * p] for p in range(5)]
    d_s = [so[2 + 4 * p] for p in range(5)]
    m_s = [so[3 + 4 * p] for p in range(5)]
    v_s = [so[4 + 4 * p] for p in range(5)]
    for lst in (g_s, d_s, m_s, v_s):
        lst[4] = lst[4].reshape(D)

    return (loss, gx.reshape(1, T, D),
            g_w_in.reshape(1, D, 1024), g_w_out.reshape(1, 256, D), *g_s,
            d_in.reshape(1, D, 1024), d_out.reshape(1, 256, D), *d_s,
            nm_in.reshape(1, D, 1024), nm_out.reshape(1, 256, D), *m_s,
            nv_in.reshape(1, D, 1024), nv_out.reshape(1, 256, D), *v_s)
```

```python
import functools

import numpy as np
import jax
import jax.numpy as jnp
from jax import lax
from jax.experimental import pallas as pl
from jax.experimental.pallas import tpu as pltpu

F32 = jnp.float32
BF16 = jnp.bfloat16

T = 4096
D = 1024
AW = 512
HW = 512
NCOL = 4096
HEAD = 64
BLK = 128
CHUNK = 64
EPS = 1e-6
SCALE = HEAD ** -0.5
NEG = -1e30
ROPE_THETA = 500000.0
INV_FREQ = [float(v) for v in
            (np.float32(ROPE_THETA) ** (-(np.arange(8, dtype=np.float32)) * np.float32(0.125)))]
LR, B1, B2, AEPS, WD, STEP = 0.001, 0.9, 0.999, 1e-08, 0.01, 10
VMEM_LIMIT = 63 * 1024 * 1024
MESH = pl.DeviceIdType.MESH


def _cp(sem=None, **kw):
    return pltpu.CompilerParams(dimension_semantics=sem, vmem_limit_bytes=VMEM_LIMIT, **kw)


def _mm(a, b):
    return jnp.dot(a, b, preferred_element_type=F32)


def _mm_nt(a, b):
    return lax.dot_general(a, b, (((1,), (1,)), ((), ())), preferred_element_type=F32)


def _mm_tn(a, b):
    return lax.dot_general(a, b, (((0,), (0,)), ((), ())), preferred_element_type=F32)


def _mm_exact_l(mat_bf, x):
    h = x.astype(BF16)
    l = (x - h.astype(F32)).astype(BF16)
    return _mm(mat_bf, h) + _mm(mat_bf, l)


def _mm_exact_r(x, mat_bf):
    h = x.astype(BF16)
    l = (x - h.astype(F32)).astype(BF16)
    return _mm(h, mat_bf) + _mm(l, mat_bf)


def _sigmoid(x):
    return 0.5 * jnp.tanh(0.5 * x) + 0.5


def _rope_tables(pos):
    lane = lax.broadcasted_iota(jnp.int32, (1, 128), 1)
    jl = lane & 63
    fi = jl & 7
    inv = jnp.zeros((1, 128), F32)
    for kk in range(8):
        inv = jnp.where(fi == kk, INV_FREQ[kk], inv)
    ang = jnp.broadcast_to(pos.astype(F32), (128, pos.shape[1])).T * inv
    c = jnp.cos(ang)
    s = jnp.sin(ang)
    cosf = jnp.where(jl < 16, c, 1.0)
    s1 = jnp.where(jl < 8, -s, 0.0)
    s2 = jnp.where((jl >= 8) & (jl < 16), s, 0.0)
    return cosf, s1, s2


def _rope(t, cosf, s1, s2):
    parts = []
    for ci in range(t.shape[1] // 128):
        tc = t[:, ci * 128:(ci + 1) * 128]
        parts.append(tc * cosf + pltpu.roll(tc, 120, 1) * s1 + pltpu.roll(tc, 8, 1) * s2)
    return jnp.concatenate(parts, axis=1)


def _rope_bwd(g, cosf, s1, s2):
    parts = []
    for ci in range(g.shape[1] // 128):
        gc = g[:, ci * 128:(ci + 1) * 128]
        parts.append(gc * cosf + pltpu.roll(gc * s1, 8, 1) + pltpu.roll(gc * s2, 120, 1))
    return jnp.concatenate(parts, axis=1)


def _perm_store(val, scr, scr2, o1, o4, o16, dt):
    n = val.shape[0]
    q = n // 4
    o1[...] = val.astype(dt)
    for ci in range(val.shape[1] // 128):
        cs = slice(ci * 128, (ci + 1) * 128)
        scr[ci] = val[:, cs]
        for r4 in range(4):
            part = scr[ci, pl.ds(r4, q, stride=4), :]
            o4[r4, :, cs] = part.astype(dt)
            scr2[ci, r4 * q:(r4 + 1) * q, :] = part
        for r4 in range(4):
            for b in range(4):
                o16[r4 + 4 * b, :, cs] = scr2[ci, pl.ds(r4 * q + b, q // 4, stride=4), :].astype(dt)


def _unperm_load(r4, r16, scr_a, scr_b, scr_c):
    n = scr_a.shape[1]
    q = n // 4
    nc = r4.shape[-1] // 128
    for ci in range(nc):
        cs = slice(ci * 128, (ci + 1) * 128)
        for rr in range(4):
            scr_a[ci, pl.ds(rr, q, stride=4), :] = r4[rr, :, cs].astype(F32)
        for rr in range(4):
            for b in range(4):
                scr_c[ci, pl.ds(rr * q + b, q // 4, stride=4), :] = r16[rr + 4 * b, :, cs].astype(F32)
        for rr in range(4):
            scr_b[ci, pl.ds(rr, q, stride=4), :] = scr_c[ci, rr * q:(rr + 1) * q, :]
    return (jnp.concatenate([scr_a[ci] for ci in range(nc)], axis=1),
            jnp.concatenate([scr_b[ci] for ci in range(nc)], axis=1))


def _unperm_sum(r4, r16, scr_b, scr_c):
    n = scr_b.shape[1]
    q = n // 4
    nc = r4.shape[-1] // 128
    for ci in range(nc):
        cs = slice(ci * 128, (ci + 1) * 128)
        for rr in range(4):
            for b in range(4):
                scr_c[ci, pl.ds(rr * q + b, q // 4, stride=4), :] = r16[rr + 4 * b, :, cs].astype(F32)
        for rr in range(4):
            scr_b[ci, pl.ds(rr, q, stride=4), :] = scr_c[ci, rr * q:(rr + 1) * q, :] + r4[rr, :, cs].astype(F32)
    return jnp.concatenate([scr_b[ci] for ci in range(nc)], axis=1)


def _fwd_in(x, pos, mixw, w_in, w_out, jm_arr):
    TT = 512
    NT = T // TT

    def body(jm_ref, x_ref, pos_ref, mw_ref, win_ref, wout_ref,
             hnt_ref, q1, k1, v1, q4, k4, v4, q16, k16, v16, ag, hq, hf, hi, hg, wfull_o, woutfull_o,
             wbuf, wobuf, hn_all, scr, scr2, stage, send_sems, recv_sems, loc_sems):
        s = pl.program_id(0)
        i = pl.program_id(1)
        mx, my, c = lax.axis_index("x"), lax.axis_index("y"), lax.axis_index("c")
        me, sibling = (mx, my, c), (mx, my, 1 - c)
        chips = [(mx, 1 - my), (1 - mx, my), (1 - mx, 1 - my)]
        jm = 2 * mx + my
        rows_in = [pl.ds(pl.multiple_of(h * 512, 512), 512) for h in (c, 1 - c)]
        rows_out = [pl.ds(pl.multiple_of(h * 128, 128), 128) for h in (c, 1 - c)]

        def blk(k):
            return lax.bitwise_xor(jm, k + 1)

        def rc(n, ref, to):
            return pltpu.make_async_remote_copy(src_ref=ref, dst_ref=ref, send_sem=send_sems.at[n],
                                                recv_sem=recv_sems.at[n], device_id=to, device_id_type=MESH)

        halves = [pl.ds(0, 512), pl.ds(512, 512)]
        quarter = [pl.ds(pl.multiple_of(c * 512 + q * 128, 128), 128) for q in range(4)]
        send_in = lambda k, h, q: rc(18 + 8 * k + 4 * h + q, wbuf.at[jm, quarter[q], halves[h]], (*chips[k], c))
        got_in = lambda k, h, q: rc(18 + 8 * k + 4 * h + q, wbuf.at[blk(k), quarter[q], halves[h]], me)
        relay = lambda h: rc(16 + h, wbuf.at[blk(h), rows_in[0], halves[h]], (*chips[1 - h], c))
        got_relay = lambda h: rc(16 + h, wbuf.at[blk(2), rows_in[0], halves[h]], me)
        send_out = lambda k: rc(3 + k, wobuf.at[jm, rows_out[0], :], (*chips[k], c))
        got_out = lambda k: rc(3 + k, wobuf.at[blk(k), rows_out[0], :], me)
        pass_in = lambda k: rc(6 + k, wbuf.at[blk(k), rows_in[0], :], sibling)
        pass_out = lambda k: rc(9 + k, wobuf.at[blk(k), rows_out[0], :], sibling)
        passed_in = lambda k: rc(6 + k, wbuf.at[blk(k), rows_in[1], :], me)
        passed_out = lambda k: rc(9 + k, wobuf.at[blk(k), rows_out[1], :], me)

        def keep(j, n):
            return pltpu.make_async_copy(wbuf.at[j], wfull_o.at[:, pl.ds(j * 1024, 1024)], loc_sems.at[n])

        @pl.when((s == 0) & (i == 0))
        def _():
            chunk = [pl.ds(pl.multiple_of(lax.rem(p + 2 * c, 4) * 256, 256), 256) for p in range(4)]
            loads = [pltpu.make_async_copy(win_ref.at[chunk[p], :] if p < 4 else wout_ref, stage.at[p % 2],
                                           loc_sems.at[4 + p % 2]) for p in range(5)]
            loads[0].start()
            for p in range(5):
                if p < 4:
                    loads[p + 1].start()
                loads[p].wait()
                if p < 4:
                    wbuf[jm, chunk[p], :] = stage[p % 2].astype(BF16)
                else:
                    wobuf[jm] = stage[p % 2].astype(BF16)
                if p < 2:
                    for k in range(2):
                        for h in range(2):
                            for q in (2 * p, 2 * p + 1):
                                send_in(k, h, q).start()
            keep(jm, 0).start()

        @pl.when((s == 0) & (i == NT - 1))
        def _():
            for kk in range(2):
                for h in range(2):
                    for q in range(4):
                        got_in(kk, h, q).wait_recv()
            relay(0).start()
            relay(1).start()
            pass_in(0).start()
            pass_in(1).start()
            passed_in(0).wait_recv()
            keep(blk(0), 1).start()

        @pl.when((s == 1) & (i == NT - 1))
        def _():
            got_relay(0).wait_recv()
            got_relay(1).wait_recv()
            pass_in(2).start()

        @pl.when((s == 2) & (i == 0))
        def _():
            for k in (1, 2):
                passed_in(k).wait_recv()
                keep(blk(k), k + 1).start()
            for kk in range(3):
                send_out(kk).start()

        @pl.when((s == 2) & (i == NT - 2))
        def _():
            for k in range(3):
                got_out(k).wait_recv()
                pass_out(k).start()

        whole_out = pltpu.make_async_copy(wobuf, woutfull_o, loc_sems.at[4])

        @pl.when((s == 2) & (i == NT - 1))
        def _():
            for k in range(3):
                passed_out(k).wait_recv()
            whole_out.start()

        tile = pl.ds(pl.multiple_of(i * TT, TT), TT)

        @pl.when(s == 0)
        def _():
            xv = x_ref[...]
            r = lax.rsqrt(jnp.mean(xv * xv, axis=-1, keepdims=True) + EPS)
            hnf = (xv * r) * mw_ref[...]
            hn_all[tile, :] = hnf.astype(BF16)
            hnt_ref[...] = hnf.T.astype(BF16)

        def project(jj):
            hn = hn_all[tile, :]
            lo = _mm(hn, wbuf[jj, :, 0:512])
            hi_cols = _mm(hn, wbuf[jj, :, 512:1024])
            if jj == 0:
                cosf, s1, s2 = _rope_tables(pos_ref[...])
                _perm_store(_rope(lo, cosf, s1, s2) * SCALE, scr, scr2, q1, q4, q16, BF16)
                _perm_store(_rope(hi_cols, cosf, s1, s2), scr, scr2, k1, k4, k16, BF16)
            elif jj == 1:
                _perm_store(lo, scr, scr2, v1, v4, v16, BF16)
                ag[...] = hi_cols.astype(BF16)
            elif jj == 2:
                hq[...] = lo.astype(BF16)
                hf[...] = hi_cols.astype(BF16)
            else:
                hi[...] = lo.astype(BF16)
                hg[...] = hi_cols.astype(BF16)

        def project_block(j):
            for jj in range(4):
                pl.when(j == jj)(functools.partial(project, jj))

        @pl.when(s < 2)
        def _():
            project_block(lax.bitwise_xor(jm, s))

        @pl.when(s == 2)
        def _():
            project_block(lax.bitwise_xor(jm, 2))
            project_block(lax.bitwise_xor(jm, 3))

        @pl.when((s == 2) & (i == NT - 1))
        def _():
            for h in range(2):
                relay(h).wait_send()
                for k in range(2):
                    for q in range(4):
                        send_in(k, h, q).wait_send()
            for k in range(3):
                send_out(k).wait_send()
                pass_in(k).wait_send()
                pass_out(k).wait_send()
            keep(jm, 0).wait()
            for k in range(3):
                keep(blk(k), k + 1).wait()
            whole_out.wait()

    def at_stage_of(jb):
        def index(s, i, jm_ref):
            sa = jnp.minimum(lax.bitwise_xor(jm_ref[0], jb), 2)
            return jnp.where(s < sa, 0, jnp.where(s == sa, i, NT - 1))
        return index

    tok = lambda w, jb: pl.BlockSpec((TT, w), lambda s, i, jm_ref: (at_stage_of(jb)(s, i, jm_ref), 0))
    d4 = lambda jb: pl.BlockSpec((4, TT // 4, AW), lambda s, i, jm_ref: (0, at_stage_of(jb)(s, i, jm_ref), 0))
    d16 = lambda jb: pl.BlockSpec((16, TT // 16, AW), lambda s, i, jm_ref: (0, at_stage_of(jb)(s, i, jm_ref), 0))
    hbm = pl.BlockSpec(memory_space=pltpu.HBM)
    sd = lambda shape, dt: jax.ShapeDtypeStruct(shape, dt)
    in_own_stage = lambda s, i: jnp.where(s == 0, i, NT - 1)
    grid_spec = pltpu.PrefetchScalarGridSpec(
        num_scalar_prefetch=1, grid=(3, NT),
        in_specs=[pl.BlockSpec((TT, D), lambda s, i, jm_ref: (in_own_stage(s, i), 0)),
                  pl.BlockSpec((1, TT), lambda s, i, jm_ref: (0, i)),
                  pl.BlockSpec((1, D), lambda s, i, jm_ref: (0, 0)), hbm, hbm],
        out_specs=[pl.BlockSpec((D, TT), lambda s, i, jm_ref: (0, in_own_stage(s, i))),
                   tok(AW, 0), tok(AW, 0), tok(AW, 1), d4(0), d4(0), d4(1), d16(0), d16(0), d16(1),
                   tok(AW, 1), tok(AW, 2), tok(AW, 2), tok(AW, 3), tok(AW, 3), hbm, hbm],
        scratch_shapes=[pltpu.VMEM((4, D, 1024), BF16), pltpu.VMEM((4, 256, D), BF16), pltpu.VMEM((T, D), BF16),
                        pltpu.VMEM((4, TT, 128), F32), pltpu.VMEM((4, TT, 128), F32), pltpu.VMEM((2, 256, 1024), F32),
                        pltpu.SemaphoreType.DMA((34,)),
                        pltpu.SemaphoreType.DMA((34,)), pltpu.SemaphoreType.DMA((6,))])
    return pl.pallas_call(
        body, name="fwd_in", grid_spec=grid_spec,
        out_shape=[sd((D, T), BF16)] + [sd((T, AW), BF16)] * 3 + [sd((4, T // 4, AW), BF16)] * 3
        + [sd((16, T // 16, AW), BF16)] * 3
        + [sd((T, AW), BF16)] * 5 + [sd((D, NCOL), BF16), sd((4, 256, D), BF16)],
        compiler_params=_cp(("arbitrary", "arbitrary")),
    )(jm_arr, x, pos, mixw, w_in, w_out)


def _band_mask(key_axis, nkeys=2 * BLK):
    shape = (nkeys, 2 * BLK) if key_axis == 0 else (2 * BLK, nkeys)
    kj = lax.broadcasted_iota(jnp.int32, shape, key_axis)
    qi = lax.broadcasted_iota(jnp.int32, shape, 1 - key_axis) & (BLK - 1)
    return (kj >= qi) & (kj <= qi + BLK), kj, qi


def _stack_heads(t2, in_a):
    z = jnp.zeros_like(t2)
    return jnp.concatenate([jnp.where(in_a[0], t2, z), jnp.where(in_a[1], t2, z)], axis=0)


def _attn_fwd(q, k, v, nb, name):
    n = 8
    CH = n * BLK
    halo = nb > n

    def body(*refs):
        if halo:
            q_ref, k_ref, v_ref, kp_ref, vp_ref, o_ref, lse_ref = refs
        else:
            q_ref, k_ref, v_ref, o_ref, lse_ref = refs
        lane = lax.broadcasted_iota(jnp.int32, (1, 128), 1)
        in_a = [lane < HEAD, lane >= HEAD]
        band, kj, _ = _band_mask(1)
        thr0 = jnp.where((n * pl.program_id(0)) % nb == 0, BLK, 0) if halo else BLK
        mask0 = band & (kj >= thr0)
        mask_first = band & (kj >= BLK)
        for b in range(n):
            rs = slice(b * BLK, (b + 1) * BLK)
            stat = jnp.zeros((BLK, 128), F32)
            for hp in range(4):
                cs = slice(hp * 128, (hp + 1) * 128)
                q2s = _stack_heads(q_ref[rs, cs], in_a)
                if b == 0:
                    kprev = kp_ref[:, cs] if halo else k_ref[rs, cs]
                    vprev = vp_ref[:, cs] if halo else v_ref[rs, cs]
                    kk = jnp.concatenate([kprev, k_ref[rs, cs]], axis=0)
                    vv = jnp.concatenate([vprev, v_ref[rs, cs]], axis=0)
                    mask = mask0
                else:
                    kk = k_ref[(b - 1) * BLK:(b + 1) * BLK, cs]
                    vv = v_ref[(b - 1) * BLK:(b + 1) * BLK, cs]
                    mask = mask_first if b % nb == 0 else band
                s = jnp.where(mask, _mm_nt(q2s, kk), NEG)
                m = jnp.max(s, axis=-1, keepdims=True)
                p = jnp.exp(s - m)
                l = jnp.sum(p, axis=-1, keepdims=True)
                o = _mm(p.astype(BF16), vv) / l
                lse = m + jnp.log(l)
                o_ref[rs, cs] = jnp.where(in_a[0], o[:BLK], o[BLK:]).astype(BF16)
                stat = jnp.where(lane == 2 * hp, lse[:BLK], stat)
                stat = jnp.where(lane == 2 * hp + 1, lse[BLK:], stat)
            lse_ref[rs, :] = stat

    cur = pl.BlockSpec((CH, AW), lambda i: (i, 0))
    prev = pl.BlockSpec((BLK, AW), lambda i: (jnp.maximum(n * i - 1, 0), 0))
    return pl.pallas_call(
        body, name=name, grid=(T // CH,),
        in_specs=[cur, cur, cur] + ([prev, prev] if halo else []),
        out_specs=[cur, pl.BlockSpec((CH, 128), lambda i: (i, 0))],
        out_shape=[jax.ShapeDtypeStruct((T, AW), BF16), jax.ShapeDtypeStruct((T, 128), F32)],
        compiler_params=_cp(("parallel",)),
    )(*((q, k, v) + ((k, v) if halo else ())))


def _attn_bwd(q, k, v, do, st, nb, name):
    n = 8
    CH = n * BLK
    NBLK = T // BLK
    halo = nb > n

    def body(*refs):
        if halo:
            (q_ref, k_ref, v_ref, do_ref, st_ref, kp_ref, vp_ref, qn_ref, don_ref, stn_ref,
             dq_ref, dk_ref, dv_ref) = refs
        else:
            q_ref, k_ref, v_ref, do_ref, st_ref, dq_ref, dk_ref, dv_ref = refs
        i = pl.program_id(0)
        lane = lax.broadcasted_iota(jnp.int32, (1, 128), 1)
        in_a = [lane < HEAD, lane >= HEAD]
        band, kj, _ = _band_mask(0)
        thr0 = jnp.where((n * i) % nb == 0, BLK, 0) if halo else BLK
        mask0 = band & (kj >= thr0)
        mask_first = band & (kj >= BLK)

        def stat_rows(st_t, hp):
            lse_r = jnp.concatenate([st_t[2 * hp:2 * hp + 1, :], st_t[2 * hp + 1:2 * hp + 2, :]], axis=1)
            dl_r = jnp.concatenate([st_t[8 + 2 * hp:9 + 2 * hp, :], st_t[9 + 2 * hp:10 + 2 * hp, :]], axis=1)
            return lse_r, dl_r

        st_t = [st_ref[b * BLK:(b + 1) * BLK, :].T for b in range(n)]
        if halo:
            nxt_thr = jnp.where((n * i + n) % nb == 0, 2 * BLK, 0)
            _, kj1, qi1 = _band_mask(0, BLK)
            mask_next = kj1 >= qi1 + nxt_thr
            stn_t = stn_ref[...].T

        for hp in range(4):
            cs = slice(hp * 128, (hp + 1) * 128)
            kb = [k_ref[b * BLK:(b + 1) * BLK, cs] for b in range(n)]
            vb = [v_ref[b * BLK:(b + 1) * BLK, cs] for b in range(n)]
            dk_acc = [jnp.zeros((BLK, 128), F32) for _ in range(n)]
            dv_acc = [jnp.zeros((BLK, 128), F32) for _ in range(n)]
            for b in range(n):
                rs = slice(b * BLK, (b + 1) * BLK)
                q2s = _stack_heads(q_ref[rs, cs], in_a)
                do2s = _stack_heads(do_ref[rs, cs], in_a)
                if b == 0:
                    kprev = kp_ref[:, cs] if halo else kb[0]
                    vprev = vp_ref[:, cs] if halo else vb[0]
                    mask = mask0
                else:
                    kprev, vprev, mask = kb[b - 1], vb[b - 1], (mask_first if b % nb == 0 else band)
                kk = jnp.concatenate([kprev, kb[b]], axis=0)
                vv = jnp.concatenate([vprev, vb[b]], axis=0)
                lse_r, dl_r = stat_rows(st_t[b], hp)
                s_t = jnp.where(mask, _mm_nt(kk, q2s), NEG)
                p_t = jnp.exp(s_t - lse_r)
                ds_t = (p_t * (_mm_nt(vv, do2s) - dl_r)).astype(BF16)
                dkk = _mm(ds_t, q2s)
                dvv = _mm(p_t.astype(BF16), do2s)
                dqs = _mm_tn(ds_t, kk) * SCALE
                dq_ref[rs, cs] = jnp.where(in_a[0], dqs[:BLK], dqs[BLK:]).astype(BF16)
                dk_acc[b] += dkk[BLK:]
                dv_acc[b] += dvv[BLK:]
                if b > 0:
                    dk_acc[b - 1] += dkk[:BLK]
                    dv_acc[b - 1] += dvv[:BLK]
            if halo:
                q2s = _stack_heads(qn_ref[:, cs], in_a)
                do2s = _stack_heads(don_ref[:, cs], in_a)
                lse_r, dl_r = stat_rows(stn_t, hp)
                s_t = jnp.where(mask_next, _mm_nt(kb[n - 1], q2s), NEG)
                p_t = jnp.exp(s_t - lse_r)
                ds_t = (p_t * (_mm_nt(vb[n - 1], do2s) - dl_r)).astype(BF16)
                dk_acc[n - 1] += _mm(ds_t, q2s)
                dv_acc[n - 1] += _mm(p_t.astype(BF16), do2s)
            for b in range(n):
                dk_ref[b * BLK:(b + 1) * BLK, cs] = dk_acc[b].astype(BF16)
                dv_ref[b * BLK:(b + 1) * BLK, cs] = dv_acc[b].astype(BF16)

    cur = pl.BlockSpec((CH, AW), lambda i: (i, 0))
    cur_st = pl.BlockSpec((CH, 128), lambda i: (i, 0))
    prev = pl.BlockSpec((BLK, AW), lambda i: (jnp.maximum(n * i - 1, 0), 0))
    nxt = pl.BlockSpec((BLK, AW), lambda i: (jnp.minimum(n * i + n, NBLK - 1), 0))
    nxt_st = pl.BlockSpec((BLK, 128), lambda i: (jnp.minimum(n * i + n, NBLK - 1), 0))
    ins = [cur] * 4 + [cur_st] + ([prev, prev, nxt, nxt, nxt_st] if halo else [])
    args = (q, k, v, do, st) + ((k, v, q, do, st) if halo else ())
    return pl.pallas_call(
        body, name=name, grid=(T // CH,),
        in_specs=ins,
        out_specs=[cur] * 3,
        out_shape=[jax.ShapeDtypeStruct((T, AW), BF16)] * 3,
        compiler_params=_cp(("parallel",)),
    )(*args)


TH = 256
NCH = TH // CHUNK


def _hgrn_common(hq_ref, hf_ref, lbr_ref, tri_ref):
    r0 = lbr_ref[0:1, :]
    r1 = lbr_ref[1:2, :]
    mx = jnp.maximum(r0, r1)
    e0 = jnp.exp(r0 - mx)
    e1 = jnp.exp(r1 - mx)
    lb = e0 / (e0 + e1)
    hqv = hq_ref[...].astype(F32)
    sq = _sigmoid(hqv)
    qv = hqv * sq
    sf = _sigmoid(hf_ref[...].astype(F32))
    f = lb + (1.0 - lb) * sf
    kv = 1.0 - f
    g = jnp.log(f)
    cum = _mm_exact_l(tri_ref[...], g)
    dec = jnp.exp(jnp.concatenate([cum[c * CHUNK + CHUNK - 1:(c + 1) * CHUNK, :] for c in range(NCH)], axis=0))
    decb = jnp.concatenate([jnp.broadcast_to(dec[c:c + 1, :], (CHUNK, HW)) for c in range(NCH)], axis=0)
    ea = jnp.exp(cum)
    ena = jnp.exp(-cum)
    eend = decb * ena
    return dict(lb=lb, hq=hqv, sq=sq, q=qv, sf=sf, f=f, k=kv, cum=cum, ea=ea, ena=ena, eend=eend,
                qd=qv * ea, ki=kv * ena, ke=kv * eend, dec=dec)


def _tri_mask(transposed=False):
    ti = lax.broadcasted_iota(jnp.int32, (TH, TH), 1 if transposed else 0)
    si = lax.broadcasted_iota(jnp.int32, (TH, TH), 0 if transposed else 1)
    return (si <= ti) & ((si // CHUNK) == (ti // CHUNK))


def _hgrn_fwd(hq, hf, hi, lbr, tri):
    NSUB = 2

    def body(hq_ref, hf_ref, hi_ref, lbr_ref, tri_ref, rec_ref, sall_ref, st_scr):
        @pl.when(pl.program_id(0) == 0)
        def _():
            st_scr[...] = jnp.zeros_like(st_scr)

        causal = _tri_mask()
        for u in range(NSUB):
            tile = slice(u * TH, (u + 1) * TH)
            w = _hgrn_common(hq_ref.at[tile, :], hf_ref.at[tile, :], lbr_ref, tri_ref)
            qd, ki, ke = w["qd"].astype(BF16), w["ki"].astype(BF16), w["ke"].astype(BF16)
            dec = w["dec"]
            vb = hi_ref[tile, :]
            for h in range(4):
                cs = slice(h * 128, (h + 1) * 128)
                att = jnp.where(causal, _mm_nt(qd[:, cs], ki[:, cs]), 0.0)
                o_intra = _mm(att.astype(BF16), vb[:, cs])
                st = st_scr[:, cs]
                for c in range(NCH):
                    rs = slice(c * CHUNK, (c + 1) * CHUNK)
                    sall_ref[u * NCH + c, :, cs] = st
                    rec_ref[u * TH + c * CHUNK:u * TH + (c + 1) * CHUNK, cs] = (
                        o_intra[rs] + _mm_nt(qd[rs, cs], st.astype(BF16))).astype(BF16)
                    st = dec[c:c + 1, cs] * st + _mm_tn(vb[rs, cs], ke[rs, cs])
                st_scr[:, cs] = st

    tok = pl.BlockSpec((NSUB * TH, HW), lambda i: (i, 0))
    return pl.pallas_call(
        body, name="hgrn_fwd", grid=(T // (NSUB * TH),),
        in_specs=[tok, tok, tok, pl.BlockSpec((2, HW), lambda i: (0, 0)), pl.BlockSpec((TH, TH), lambda i: (0, 0))],
        out_specs=[tok, pl.BlockSpec((NSUB * NCH, 128, HW), lambda i: (i, 0, 0))],
        out_shape=[jax.ShapeDtypeStruct((T, HW), BF16), jax.ShapeDtypeStruct((T // CHUNK, 128, HW), F32)],
        scratch_shapes=[pltpu.VMEM((128, HW), F32)],
        compiler_params=_cp(("arbitrary",)),
    )(hq, hf, hi, lbr, tri)


def _hgrn_bwd(hq, hf, hi, lbr, tri, trit, drec, sall, dhg, rout, routb):
    NSUB = 2
    NT = T // (NSUB * TH)

    def body(hq_ref, hf_ref, hi_ref, lbr_ref, tri_ref, trit_ref, do_ref, sall_ref, dhg_ref, rout_r, routb_r,
             dph_ref, small_ref, pout_o, poutr_o,
             dst_scr, dlb_scr, dqd_scr, dki_scr, dke_scr, dlast_scr, send_sems, recv_sems, loc_sems):
        step = pl.program_id(0)
        loc, rem = _chip_copies(_w_out_piece, rout_r, routb_r, pout_o, poutr_o, send_sems, recv_sems,
                                loc_sems.at[0])

        @pl.when(step == 0)
        def _():
            dst_scr[...] = jnp.zeros_like(dst_scr)
            dlb_scr[...] = jnp.zeros_like(dlb_scr)
            for cp in loc + rem:
                cp.start()

        causal = _tri_mask()
        causal_t = _tri_mask(transposed=True)
        lb = None
        for u in reversed(range(NSUB)):
            tile = slice(u * TH, (u + 1) * TH)
            w = _hgrn_common(hq_ref.at[tile, :], hf_ref.at[tile, :], lbr_ref, tri_ref)
            qd, ki, ke = w["qd"].astype(BF16), w["ki"].astype(BF16), w["ke"].astype(BF16)
            dec = w["dec"]
            vb = hi_ref[tile, :]
            dob = do_ref[tile, :].astype(BF16)
            for h in range(4):
                cs = slice(h * 128, (h + 1) * 128)
                att_t = jnp.where(causal_t, _mm_nt(ki[:, cs], qd[:, cs]), 0.0).astype(BF16)
                datt_t = jnp.where(causal_t, _mm_nt(vb[:, cs], dob[:, cs]), 0.0).astype(BF16)
                datt = jnp.where(causal, _mm_nt(dob[:, cs], vb[:, cs]), 0.0).astype(BF16)
                dv_intra = _mm(att_t, dob[:, cs])
                dqd_intra = _mm(datt, ki[:, cs])
                dki_scr[u, :, cs] = _mm(datt_t, qd[:, cs])
                dst = dst_scr[:, cs]
                for c in reversed(range(NCH)):
                    rs = slice(c * CHUNK, (c + 1) * CHUNK)
                    dec_c = dec[c:c + 1, :]
                    st = sall_ref[u * NCH + c, :, cs]
                    dstb = dst.astype(BF16)
                    dph_ref[u * TH + c * CHUNK:u * TH + (c + 1) * CHUNK, 2 * HW + h * 128:2 * HW + (h + 1) * 128] = (
                        dv_intra[rs] + _mm_nt(ke[rs, cs], dstb)).astype(BF16)
                    dqd_scr[u, rs, cs] = dqd_intra[rs] + _mm(dob[rs, cs], st.astype(BF16))
                    dke_scr[u, rs, cs] = _mm(vb[rs, cs], dstb)
                    ddec = jnp.sum(dst * st, axis=0, keepdims=True)
                    dlast_scr[u, c:c + 1, cs] = ddec * dec_c[:, cs]
                    dst = dec_c[:, cs] * dst + _mm_tn(dob[rs, cs], qd[rs, cs])
                dst_scr[:, cs] = dst
            dqd, dki, dke = dqd_scr[u], dki_scr[u], dke_scr[u]
            dq = dqd * w["ea"]
            dk = dki * w["ena"] + dke * w["eend"]
            dcum = dqd * w["qd"] - dki * w["ki"] - dke * w["ke"]
            dkeke = dke * w["ke"]
            dlastb = jnp.concatenate(
                [jnp.broadcast_to(dlast_scr[u, c:c + 1, :]
                                  + jnp.sum(dkeke[c * CHUNK:(c + 1) * CHUNK], axis=0, keepdims=True), (CHUNK, HW))
                 for c in range(NCH)], axis=0)
            dg = _mm_exact_l(trit_ref[...], dcum) + dlastb
            df = dg / w["f"] - dk
            lb, sf, sq = w["lb"], w["sf"], w["sq"]
            dph_ref[tile, HW:2 * HW] = (df * (1.0 - lb) * sf * (1.0 - sf)).astype(BF16)
            dph_ref[tile, 0:HW] = (dq * (sq * (1.0 + w["hq"] * (1.0 - sq)))).astype(BF16)
            dph_ref[tile, 3 * HW:4 * HW] = dhg_ref[tile, :]
            dlb_scr[...] += jnp.sum(df * (1.0 - sf), axis=0, keepdims=True)

        @pl.when(step == NT - 1)
        def _():
            gr = dlb_scr[...] * lb * (1.0 - lb)
            small_ref[...] = jnp.zeros_like(small_ref)
            small_ref[0:1, 0:HW] = gr
            small_ref[1:2, 0:HW] = -gr
            for cp in rem:
                cp.wait_recv()
            for cp in rem:
                cp.wait_send()
            for cp in loc:
                cp.wait()

    tok = pl.BlockSpec((NSUB * TH, HW), lambda i: (NT - 1 - i, 0))
    const = lambda shape: pl.BlockSpec(shape, lambda i: (0,) * len(shape))
    hbm = pl.BlockSpec(memory_space=pltpu.HBM)
    return pl.pallas_call(
        body, name="hgrn_bwd", grid=(NT,),
        in_specs=[tok, tok, tok, const((2, HW)), const((TH, TH)), const((TH, TH)), tok,
                  pl.BlockSpec((NSUB * NCH, 128, HW), lambda i: (NT - 1 - i, 0, 0)), tok, hbm, hbm],
        out_specs=[pl.BlockSpec((NSUB * TH, NCOL // 2), lambda i: (NT - 1 - i, 0)), const((8, D)), hbm, hbm],
        out_shape=[jax.ShapeDtypeStruct((T, NCOL // 2), BF16), jax.ShapeDtypeStruct((8, D), F32),
                   jax.ShapeDtypeStruct((128, D), F32), jax.ShapeDtypeStruct((3, 128, D), BF16)],
        scratch_shapes=[pltpu.VMEM((128, HW), F32), pltpu.VMEM((1, HW), F32), pltpu.VMEM((NSUB, TH, HW), F32),
                        pltpu.VMEM((NSUB, TH, HW), F32), pltpu.VMEM((NSUB, TH, HW), F32),
                        pltpu.VMEM((NSUB, 8, HW), F32),
                        pltpu.SemaphoreType.DMA((3,)), pltpu.SemaphoreType.DMA((3,)), pltpu.SemaphoreType.DMA((1,))],
        compiler_params=_cp(("arbitrary",)),
    )(hq, hf, hi, lbr, tri, trit, drec, sall, dhg, rout, routb)


def _fwd_out(o1, o4, o16, l1, l4, l16, rec, ag, hg, x, tgt, anw, hnw, fnw, wout_full, gmat, emat, selmat):
    TT = 512

    def body(o1_r, o4_r, o16_r, l1_r, l4_r, l16_r, rec_r, ag_r, hg_r, x_r, tgt_r, anw_r, hnw_r, fnw_r, wo_r, g_r,
             e_r, sel_r, dx2_o, do1_o, do4_o, do16_o, st1_o, st4_o, st16_o, drec_o, dag_o, dhg_o,
             rout_o, routb_o, small_o, scr_a, scr_b, scr_c, gwout_o, rbuf, send_sems, recv_sems):
        @pl.when(pl.program_id(0) == 0)
        def _():
            gwout_o[...] = jnp.zeros_like(gwout_o)
            small_o[...] = jnp.zeros_like(small_o)

        def unperm(r4, r16):
            return _unperm_load(r4, r16, scr_a, scr_b, scr_c)

        def perm_out(val, p1, p4, p16, dt):
            _perm_store(val, scr_a, scr_b, p1, p4, p16, dt)

        o4u, o16u = unperm(o4_r, o16_r)
        l4c, l16c = unperm(l4_r, l16_r)
        l1c = l1_r[...]
        mxc = jnp.maximum(jnp.maximum(l1c, l4c), l16c)
        w1c, w4c, w16c = jnp.exp(l1c - mxc), jnp.exp(l4c - mxc), jnp.exp(l16c - mxc)
        denc = w1c + w4c + w16c
        lane = lax.broadcasted_iota(jnp.int32, (1, 128), 1)
        lse_c = jnp.where(lane < 8, mxc + jnp.log(denc), 0.0)
        em = e_r[...]
        wn1 = _mm_exact_r(w1c / denc, em)
        wn4 = _mm_exact_r(w4c / denc, em)
        o1v = o1_r[...].astype(F32)
        attn = wn1 * o1v + wn4 * o4u + (1.0 - wn1 - wn4) * o16u
        gm = g_r[...]

        def head_mean_a(t):
            return jnp.concatenate([_mm_exact_r(t[:, :256], gm), _mm_exact_r(t[:, 256:], gm)], axis=1)

        def head_mean_h(t):
            return jnp.concatenate(
                [jnp.broadcast_to(jnp.mean(t[:, h * 128:(h + 1) * 128], axis=-1, keepdims=True), (TT, 128))
                 for h in range(4)], axis=1)

        rs_a = lax.rsqrt(head_mean_a(attn * attn) + EPS)
        n_a = attn * rs_a
        agv = ag_r[...].astype(F32)
        sg_a = _sigmoid(agv)
        si_a = agv * sg_a
        anw_v = anw_r[...]
        y_a = (n_a * anw_v) * si_a
        recv = rec_r[...].astype(F32)
        rs_h = lax.rsqrt(head_mean_h(recv * recv) + EPS)
        n_h = recv * rs_h
        hgv = hg_r[...].astype(F32)
        sg_h = _sigmoid(hgv)
        si_h = hgv * sg_h
        hnw_v = hnw_r[...]
        y_h = (n_h * hnw_v) * si_h
        mixed = jnp.concatenate([y_a, y_h], axis=1).astype(BF16)
        xv = x_r[...]
        x2 = xv + _mm(mixed, wo_r[...])
        r2 = lax.rsqrt(jnp.mean(x2 * x2, axis=-1, keepdims=True) + EPS)
        fnw_v = fnw_r[...]
        xn = x2 * r2
        err = xn * fnw_v - tgt_r[...]
        small_o[2:3, :] += 0.5 * jnp.sum(jnp.mean(err * err, axis=-1, keepdims=True), axis=0, keepdims=True)
        small_o[0:1, :] += jnp.sum(err * xn, axis=0, keepdims=True) * (1.0 / D)
        dyw = err * (fnw_v * (1.0 / D))
        dx2 = r2 * dyw - x2 * ((r2 * r2 * r2) * jnp.mean(dyw * x2, axis=-1, keepdims=True))
        dx2_o[...] = dx2
        dx2b = dx2.astype(BF16)
        gwout_o[...] += _mm_tn(mixed, dx2b)
        dmix = _mm_nt(dx2b, wo_r[...])
        dm_a, dm_h = dmix[:, :AW], dmix[:, AW:]
        dag_o[...] = (dm_a * (n_a * anw_v) * (sg_a * (1.0 + agv * (1.0 - sg_a)))).astype(BF16)
        dy_a = dm_a * si_a
        dn_a = dy_a * anw_v
        small_o[1:2, 0:AW] += jnp.sum(dy_a * n_a, axis=0, keepdims=True)
        dattn = rs_a * (dn_a - n_a * head_mean_a(dn_a * n_a))
        perm_out(dattn, do1_o, do4_o, do16_o, BF16)
        stats = lse_c + _mm_exact_r(dattn * attn, sel_r[...])
        perm_out(stats, st1_o, st4_o, st16_o, F32)
        dhg_o[...] = (dm_h * (n_h * hnw_v) * (sg_h * (1.0 + hgv * (1.0 - sg_h)))).astype(BF16)
        dy_h = dm_h * si_h
        dn_h = dy_h * hnw_v
        small_o[1:2, AW:] += jnp.sum(dy_h * n_h, axis=0, keepdims=True)
        drec_o[...] = (rs_h * (dn_h - n_h * head_mean_h(dn_h * n_h))).astype(BF16)

        @pl.when(pl.program_id(0) == T // TT - 1)
        def _():
            x, y, c = lax.axis_index("x"), lax.axis_index("y"), lax.axis_index("c")
            cps = [pltpu.make_async_remote_copy(
                src_ref=gwout_o.at[pl.ds(pl.multiple_of(j * 256 + (1 - c) * 128, 128), 128), :], dst_ref=rbuf.at[j],
                send_sem=send_sems.at[j], recv_sem=recv_sems.at[j], device_id=(x, y, 1 - c), device_id_type=MESH)
                for j in range(4)]
            for cp in cps:
                cp.start()
            for j, cp in enumerate(cps):
                cp.wait_recv()
                red = gwout_o[pl.ds(pl.multiple_of(j * 256 + c * 128, 128), 128), :] + rbuf[j]
                rout_o[j * 128:(j + 1) * 128, :] = red
                routb_o[j * 128:(j + 1) * 128, :] = red.astype(BF16)
            for cp in cps:
                cp.wait_send()

    tok = lambda w: pl.BlockSpec((TT, w), lambda i: (i, 0))
    d4 = pl.BlockSpec((4, TT // 4, AW), lambda i: (0, i, 0))
    d16 = pl.BlockSpec((16, TT // 16, AW), lambda i: (0, i, 0))
    const = lambda shape: pl.BlockSpec(shape, lambda i: (0,) * len(shape))
    sd = lambda shape, dt: jax.ShapeDtypeStruct(shape, dt)
    c4 = pl.BlockSpec((4, TT // 4, 128), lambda i: (0, i, 0))
    c16 = pl.BlockSpec((16, TT // 16, 128), lambda i: (0, i, 0))
    p3 = lambda w, dt: [sd((T, w), dt), sd((4, T // 4, w), dt), sd((16, T // 16, w), dt)]
    return pl.pallas_call(
        body, name="fwd_out", grid=(T // TT,),
        in_specs=[tok(AW), d4, d16, tok(128), c4, c16, tok(AW), tok(AW), tok(AW), tok(D), tok(D),
                  const((1, AW)), const((1, HW)), const((1, D)), const((D, D)), const((256, 256)),
                  const((128, AW)), const((AW, 128))],
        out_specs=[tok(D)] + [tok(AW), d4, d16] + [tok(128), c4, c16] + [tok(AW)] * 3
        + [const((512, D)), const((512, D)), const((8, D))],
        out_shape=[sd((T, D), F32)] + p3(AW, BF16) + p3(128, F32)
        + [sd((T, AW), BF16), sd((T, AW), BF16), sd((T, AW), BF16), sd((512, D), F32), sd((512, D), BF16),
           sd((8, D), F32)],
        scratch_shapes=[pltpu.VMEM((4, TT, 128), F32)] * 3 + [pltpu.VMEM((D, D), F32),
                        pltpu.VMEM((4, 128, D), F32), pltpu.SemaphoreType.DMA((4,)), pltpu.SemaphoreType.DMA((4,))],
        compiler_params=_cp(("arbitrary",)),
    )(o1, o4, o16, l1, l4, l16, rec, ag, hg, x, tgt, anw, hnw, fnw, wout_full, gmat, emat, selmat)


def _dproj_build(dq, dk, dv, dag, pos):
    TT = 512

    def body(dq1, dq4, dq16, dk1, dk4, dk16, dv1, dv4, dv16, dag_r, pos_r, dproj_o, scr_b, scr_c):
        def unperm_sum(r1, r4, r16):
            return r1[...] + _unperm_sum(r4, r16, scr_b, scr_c)

        cosf, s1, s2 = _rope_tables(pos_r[...])
        dproj_o[:, 0:512] = _rope_bwd(unperm_sum(dq1, dq4, dq16), cosf, s1, s2).astype(BF16)
        dproj_o[:, 512:1024] = _rope_bwd(unperm_sum(dk1, dk4, dk16), cosf, s1, s2).astype(BF16)
        dproj_o[:, 1024:1536] = unperm_sum(dv1, dv4, dv16).astype(BF16)
        dproj_o[:, 1536:2048] = dag_r[...]

    tok = lambda w: pl.BlockSpec((TT, w), lambda i: (i, 0))
    d4 = pl.BlockSpec((4, TT // 4, AW), lambda i: (0, i, 0))
    d16 = pl.BlockSpec((16, TT // 16, AW), lambda i: (0, i, 0))
    return pl.pallas_call(
        body, name="dproj_build", grid=(T // TT,),
        in_specs=[tok(AW), d4, d16] * 3 + [tok(AW), pl.BlockSpec((1, TT), lambda i: (0, i))],
        out_specs=tok(NCOL // 2),
        out_shape=jax.ShapeDtypeStruct((T, NCOL // 2), BF16),
        scratch_shapes=[pltpu.VMEM((4, TT, 128), F32)] * 2,
        compiler_params=_cp(("parallel",)),
    )(*dq, *dk, *dv, dag, pos)


def _bwd_x(dproj_a, dproj_h, x, dx2, mixw, w_full, rin, rinb, small4, small6, pout_own, pout_rem):
    TT = 256
    NT = T // TT

    def body(dpa_r, dph_r, x_r, dx2_r, mw_r, w_r, rin_r, rinb_r, s4_r, s6_r, poo_r, por_r,
             gx_o, sall_o, fin_o, fout_o, sbuf, v_own, v_rem, vo_own, vo_rem, sin, sout, got_in,
             got_out, send_sems, recv_sems, loc_sems, share_send, share_recv, fin_sems):
        i = pl.program_id(0)
        loc, rem = _chip_copies(_w_in_piece, rin_r, rinb_r, v_own, v_rem, send_sems, recv_sems, loc_sems.at[0])
        loads = [pltpu.make_async_copy(poo_r, vo_own, fin_sems.at[2]),
                 pltpu.make_async_copy(por_r, vo_rem, fin_sems.at[3])]

        @pl.when(i == 0)
        def _():
            sbuf[...] = jnp.zeros_like(sbuf)
            for cp in loc + rem + loads:
                cp.start()

        dhn = _mm_nt(dpa_r[...], w_r[:, 0:NCOL // 2]) + _mm_nt(dph_r[...], w_r[:, NCOL // 2:NCOL])
        xv = x_r[...]
        r = lax.rsqrt(jnp.mean(xv * xv, axis=-1, keepdims=True) + EPS)
        dxw = dhn * mw_r[...]
        gx_o[...] = dx2_r[...] + r * dxw - xv * ((r * r * r) * jnp.mean(dxw * xv, axis=-1, keepdims=True))
        sbuf[16:17, :] += jnp.sum(dhn * (xv * r), axis=0, keepdims=True)

        @pl.when(i == NT - 1)
        def _():
            sbuf[0:8, :] = s4_r[...]
            sbuf[8:16, :] = s6_r[...]
            sloc, srem = _small_copies(sbuf, sall_o, send_sems, recv_sems, loc_sems.at[1])
            for cp in sloc + srem:
                cp.start()
            for cp in rem:
                cp.wait_recv()
            for cp in rem:
                cp.wait_send()
            for cp in loc:
                cp.wait()
            mx, my, c = lax.axis_index("x"), lax.axis_index("y"), lax.axis_index("c")
            for cp in loads:
                cp.wait()
            sout[...] = ((vo_own[...] + vo_rem[0].astype(F32)) + vo_rem[1].astype(F32)) + vo_rem[2].astype(F32)
            sin[...] = ((v_own[...] + v_rem[0].astype(F32)) + v_rem[1].astype(F32)) + v_rem[2].astype(F32)
            swap = [pltpu.make_async_remote_copy(src_ref=sin, dst_ref=got_in, send_sem=share_send.at[0],
                                                 recv_sem=share_recv.at[0], device_id=(mx, my, 1 - c),
                                                 device_id_type=MESH),
                    pltpu.make_async_remote_copy(src_ref=sout, dst_ref=got_out, send_sem=share_send.at[1],
                                                 recv_sem=share_recv.at[1], device_id=(mx, my, 1 - c),
                                                 device_id_type=MESH)]
            for cp in swap:
                cp.start()
            mine = [pltpu.make_async_copy(sin, fin_o.at[c], fin_sems.at[0]),
                    pltpu.make_async_copy(sout, fout_o.at[c], fin_sems.at[1])]
            for cp in mine:
                cp.start()
            for cp in swap:
                cp.wait_recv()
            theirs = [pltpu.make_async_copy(got_in, fin_o.at[1 - c], fin_sems.at[2]),
                      pltpu.make_async_copy(got_out, fout_o.at[1 - c], fin_sems.at[3])]
            for cp in theirs:
                cp.start()
            for cp in swap:
                cp.wait_send()
            for cp in mine + theirs:
                cp.wait()
            for cp in srem:
                cp.wait_recv()
            for cp in srem:
                cp.wait_send()
            for cp in sloc:
                cp.wait()

    tok = lambda w: pl.BlockSpec((TT, w), lambda i: (i, 0))
    const = lambda shape: pl.BlockSpec(shape, lambda i: (0,) * len(shape))
    hbm = pl.BlockSpec(memory_space=pltpu.HBM)
    return pl.pallas_call(
        body, name="bwd_x", grid=(NT,),
        in_specs=[tok(NCOL // 2), tok(NCOL // 2), tok(D), tok(D), const((1, D)), const((D, NCOL)), hbm, hbm,
                  const((8, D)), const((8, D)), hbm, hbm],
        out_specs=[tok(D), hbm, hbm, hbm],
        out_shape=[jax.ShapeDtypeStruct((T, D), F32),
                   jax.ShapeDtypeStruct((8, 24, D), F32),
                   jax.ShapeDtypeStruct((2, 512, 1024), F32), jax.ShapeDtypeStruct((2, 128, D), F32)],
        scratch_shapes=[pltpu.VMEM((24, D), F32),
                        pltpu.VMEM((512, 1024), F32), pltpu.VMEM((3, 512, 1024), BF16),
                        pltpu.VMEM((128, D), F32), pltpu.VMEM((3, 128, D), BF16),
                        pltpu.VMEM((512, 1024), F32), pltpu.VMEM((128, D), F32),
                        pltpu.VMEM((512, 1024), F32), pltpu.VMEM((128, D), F32),
                        pltpu.SemaphoreType.DMA((10,)), pltpu.SemaphoreType.DMA((10,)), pltpu.SemaphoreType.DMA((2,)),
                        pltpu.SemaphoreType.DMA((2,)), pltpu.SemaphoreType.DMA((2,)), pltpu.SemaphoreType.DMA((4,))],
        compiler_params=_cp(("arbitrary",)),
    )(dproj_a, dproj_h, x, dx2, mixw, w_full, rin, rinb, small4, small6, pout_own, pout_rem)


def _grad_w_in(hn, dproj_a, dproj_h):
    TK = 2048
    NK = T // TK

    def body(hnt_r, dpa_r, dph_r, rin_o, rinb_o, acc, rbuf, obuf, obufb, send_sems, recv_sems, wb_sems):
        j = pl.program_id(0)
        kk = pl.program_id(1)
        x, y, c = lax.axis_index("x"), lax.axis_index("y"), lax.axis_index("c")
        mine = pl.ds(pl.multiple_of(c * 512, 512), 512)
        theirs = pl.ds(pl.multiple_of((1 - c) * 512, 512), 512)

        def send(jj):
            return pltpu.make_async_remote_copy(
                src_ref=acc.at[jj % 2, theirs, :], dst_ref=rbuf.at[jj], send_sem=send_sems.at[jj],
                recv_sem=recv_sems.at[jj], device_id=(x, y, 1 - c), device_id_type=MESH)

        def writeback(jj):
            cols = pl.ds(jj * 1024, 1024)
            return [pltpu.make_async_copy(obuf.at[jj % 2], rin_o.at[:, cols], wb_sems.at[jj % 2]),
                    pltpu.make_async_copy(obufb.at[jj % 2], rinb_o.at[:, cols], wb_sems.at[2 + jj % 2])]

        def wait_writeback(jj):
            for cp in writeback(jj):
                cp.wait()

        def finalize(jj):
            send(jj).wait_recv()
            red = acc[jj % 2, mine, :] + rbuf[jj]
            obuf[jj % 2] = red
            obufb[jj % 2] = red.astype(BF16)
            for cp in writeback(jj):
                cp.start()

        prod = _mm(hnt_r[...], jnp.where(j < 2, dpa_r[...], dph_r[...]))

        @pl.when(kk == 0)
        def _():
            for jj in (2, 3):
                @pl.when(j == jj)
                def _():
                    send(jj - 2).wait_send()
            acc[j % 2] = prod

        @pl.when(kk > 0)
        def _():
            acc[j % 2] += prod

        @pl.when(kk == NK - 1)
        def _():
            for jj in range(4):
                @pl.when(j == jj)
                def _():
                    send(jj).start()
                    if jj in (1, 2):
                        finalize(jj - 1)
                    if jj == 3:
                        wait_writeback(0)
                        finalize(2)
                        wait_writeback(1)
                        finalize(3)
                        wait_writeback(2)
                        wait_writeback(3)
                        send(2).wait_send()
                        send(3).wait_send()

    hbm = pl.BlockSpec(memory_space=pltpu.HBM)
    return pl.pallas_call(
        body, name="grad_w_in", grid=(4, NK),
        in_specs=[pl.BlockSpec((D, TK), lambda j, kk: (0, kk)),
                  pl.BlockSpec((TK, 1024), lambda j, kk: (jnp.where(j < 2, kk, NK - 1), jnp.minimum(j, 1))),
                  pl.BlockSpec((TK, 1024), lambda j, kk: (jnp.where(j < 2, 0, kk), jnp.maximum(j - 2, 0)))],
        out_specs=[hbm, hbm],
        out_shape=[jax.ShapeDtypeStruct((512, NCOL), F32), jax.ShapeDtypeStruct((512, NCOL), BF16)],
        scratch_shapes=[pltpu.VMEM((2, D, 1024), F32), pltpu.VMEM((4, 512, 1024), F32), pltpu.VMEM((2, 512, 1024), F32),
                        pltpu.VMEM((2, 512, 1024), BF16),
                        pltpu.SemaphoreType.DMA((4,)), pltpu.SemaphoreType.DMA((4,)), pltpu.SemaphoreType.DMA((4,))],
        compiler_params=_cp(("arbitrary", "arbitrary")),
    )(hn, dproj_a, dproj_h)


def _w_in_piece(ref, j):
    return ref.at[:, pl.ds(j * 1024, 1024)]


def _w_out_piece(ref, j):
    return ref.at[pl.ds(j * 128, 128), :]


def _chip_copies(piece, src_r, srcb_r, own_o, rem_o, send_sems, recv_sems, loc_sem):
    x, y, c = lax.axis_index("x"), lax.axis_index("y"), lax.axis_index("c")
    chips = [(1 - x, y), (x, 1 - y), (1 - x, 1 - y)]
    loc = [pltpu.make_async_copy(piece(src_r, 2 * x + y), own_o, loc_sem)]
    rem = [pltpu.make_async_remote_copy(
        src_ref=piece(srcb_r, 2 * px + py), dst_ref=rem_o.at[k], send_sem=send_sems.at[k],
        recv_sem=recv_sems.at[k], device_id=(px, py, c), device_id_type=MESH) for k, (px, py) in enumerate(chips)]
    return loc, rem


def _small_copies(small_r, sall_o, send_sems, recv_sems, loc_sem):
    x, y, c = lax.axis_index("x"), lax.axis_index("y"), lax.axis_index("c")
    me = 4 * x + 2 * y + c
    loc = [pltpu.make_async_copy(small_r, sall_o.at[me], loc_sem)]
    rem = []
    k = 3
    for fx in range(2):
        for fy in range(2):
            for fc in range(2):
                if fx or fy or fc:
                    peer = (1 - x if fx else x, 1 - y if fy else y, 1 - c if fc else c)
                    rem.append(pltpu.make_async_remote_copy(
                        src_ref=small_r, dst_ref=sall_o.at[me], send_sem=send_sems.at[k],
                        recv_sem=recv_sems.at[k], device_id=peer, device_id_type=MESH))
                    k += 1
    return loc, rem


def _adamw_math(w, g, m, v):
    m = B1 * m + (1.0 - B1) * g
    v = B2 * v + (1.0 - B2) * (g * g)
    m_hat = m / (1.0 - B1 ** STEP)
    v_hat = v / (1.0 - B2 ** STEP)
    delta = -LR * (m_hat / (jnp.sqrt(v_hat) + AEPS) + WD * w)
    return delta, m, v


def _adamw(big_in, big_out, sall, params):
    def body(*refs):
        wi, gi, mi, vi, wo, go, mo, vo, sall_r = refs[:9]
        ins = refs[9:24]
        di_o, mi_o, vi_o, do_o, mo_o, vo_o = refs[24:30]
        outs = refs[30:]
        d, mm, vv = _adamw_math(wi[...], gi[...], mi[...], vi[...])
        di_o[...] = d
        mi_o[...] = mm
        vi_o[...] = vv

        @pl.when(pl.program_id(0) == 0)
        def _():
            d, mm, vv = _adamw_math(wo[...], go[...], mo[...], vo[...])
            do_o[...] = d
            mo_o[...] = mm
            vo_o[...] = vv
            tot = sall_r[0]
            for dv in range(1, 8):
                tot = tot + sall_r[dv]
            grads = [tot[16:17, :], tot[1:2, 0:AW], tot[1:2, AW:], tot[8:10, 0:HW], tot[0:1, :]]
            outs[0][...] = tot[2:3, 0:1]
            for p in range(5):
                w_r, m_r, v_r = ins[3 * p:3 * p + 3]
                g = grads[p]
                d, mm, vv = _adamw_math(w_r[...], g, m_r[...], v_r[...])
                outs[1 + 4 * p][...] = g
                outs[2 + 4 * p][...] = d
                outs[3 + 4 * p][...] = mm
                outs[4 + 4 * p][...] = vv

    flat = [a for p in params for a in p]
    shapes = [jax.ShapeDtypeStruct((D, 1024), F32)] * 3 + [jax.ShapeDtypeStruct((256, D), F32)] * 3
    shapes += [jax.ShapeDtypeStruct((1, 1), F32)]
    for p in params:
        shapes += [jax.ShapeDtypeStruct(p[0].shape, F32)] * 4
    vm = pl.BlockSpec(memory_space=pltpu.VMEM)
    rows = pl.BlockSpec((512, 1024), lambda i: (i, 0))
    whole = pl.BlockSpec((256, D), lambda i: (0, 0))
    return pl.pallas_call(
        body, name="adamw", grid=(2,),
        in_specs=[rows] * 4 + [whole] * 4 + [vm] * 16, out_specs=[rows] * 3 + [whole] * 3 + [vm] * 21,
        out_shape=shapes,
        compiler_params=_cp(("arbitrary",)),
    )(*big_in, *big_out, sall, *flat)


def kernel(x, positions, w_in, w_out, mix_norm_w, attn_out_norm_w, hgrn_out_norm_w, hgrn_lb_raw, final_norm_w, loss_target, m_w_in, m_w_out, m_mix_norm_w, m_attn_out_norm_w, m_hgrn_out_norm_w, m_hgrn_lb_raw, m_final_norm_w, v_w_in, v_w_out, v_mix_norm_w, v_attn_out_norm_w, v_hgrn_out_norm_w, v_hgrn_lb_raw, v_final_norm_w):
    xs = x.reshape(T, D)
    tgt = loss_target.reshape(T, D)
    pos = positions.reshape(1, T)
    fnw = final_norm_w.reshape(1, D)

    ti = np.arange(TH)
    tri_np = ((ti[:, None] // CHUNK == ti[None, :] // CHUNK) & (ti[None, :] <= ti[:, None])).astype(np.float32)
    tri = jnp.asarray(tri_np, BF16)
    trit = jnp.asarray(tri_np.T, BF16)
    hi_ = np.arange(AW) // HEAD
    gmat = jnp.asarray((hi_[:256, None] == hi_[None, :256]).astype(np.float32) / HEAD, BF16)
    emat_np = (np.arange(128)[:, None] == hi_[None, :]).astype(np.float32)
    sel_np = (8 + hi_[:, None] == np.arange(128)[None, :]).astype(np.float32)
    emat = jnp.asarray(emat_np, BF16)
    selmat = jnp.asarray(sel_np, BF16)

    jm_arr = (2 * lax.axis_index("x") + lax.axis_index("y")).astype(jnp.int32).reshape(1)
    (hn, q1, k1, v1, q4, k4, v4, q16, k16, v16, ag, hq, hf, hi, hg, w_full, wout4) = _fwd_in(
        xs, pos, mix_norm_w, w_in.reshape(D, 1024), w_out.reshape(256, D), jm_arr)
    wout_full = wout4.reshape(D, D)
    flat = lambda a: a.reshape(T, AW)
    o1, l1 = _attn_fwd(q1, k1, v1, T // BLK, "attn_fwd_d1")
    o4, l4 = _attn_fwd(flat(q4), flat(k4), flat(v4), T // 4 // BLK, "attn_fwd_d4")
    o16, l16 = _attn_fwd(flat(q16), flat(k16), flat(v16), T // 16 // BLK, "attn_fwd_d16")
    rec, sall = _hgrn_fwd(hq, hf, hi, hgrn_lb_raw, tri)

    (dx2, do1, do4, do16, st1, st4, st16, drec, dag, dhg, rout, routb, small4) = _fwd_out(
        o1, o4.reshape(4, T // 4, AW), o16.reshape(16, T // 16, AW),
        l1, l4.reshape(4, T // 4, 128), l16.reshape(16, T // 16, 128),
        rec, ag, hg, xs, tgt, attn_out_norm_w, hgrn_out_norm_w, fnw, wout_full, gmat, emat, selmat)

    fst = lambda a: a.reshape(T, 128)
    dq1, dk1, dv1 = _attn_bwd(q1, k1, v1, do1, st1, T // BLK, "attn_bwd_d1")
    dq4, dk4, dv4 = _attn_bwd(flat(q4), flat(k4), flat(v4), flat(do4), fst(st4), T // 4 // BLK, "attn_bwd_d4")
    dq16, dk16, dv16 = _attn_bwd(flat(q16), flat(k16), flat(v16), flat(do16), fst(st16), T // 16 // BLK,
                                 "attn_bwd_d16")
    dproj_h, small6, pout_own, pout_rem = _hgrn_bwd(hq, hf, hi, hgrn_lb_raw, tri, trit, drec, sall, dhg,
                                                    rout, routb)

    r4 = lambda a: a.reshape(4, T // 4, AW)
    r16 = lambda a: a.reshape(16, T // 16, AW)
    dproj_a = _dproj_build((dq1, r4(dq4), r16(dq16)), (dk1, r4(dk4), r16(dk16)), (dv1, r4(dv4), r16(dv16)),
                           dag, pos)
    rin, rinb = _grad_w_in(hn, dproj_a, dproj_h)
    gx, small_all, fin, fout = _bwd_x(dproj_a, dproj_h, xs, dx2, mix_norm_w, w_full, rin, rinb,
                                            small4, small6, pout_own, pout_rem)
    g_w_in = fin.reshape(D, 1024)
    g_w_out = fout.reshape(256, D)

    params = [(mix_norm_w, m_mix_norm_w, v_mix_norm_w),
              (attn_out_norm_w, m_attn_out_norm_w, v_attn_out_norm_w),
              (hgrn_out_norm_w, m_hgrn_out_norm_w, v_hgrn_out_norm_w),
              (hgrn_lb_raw, m_hgrn_lb_raw, v_hgrn_lb_raw),
              (fnw, m_final_norm_w.reshape(1, D), v_final_norm_w.reshape(1, D))]
    d_in, nm_in, nv_in, d_out, nm_out, nv_out, *so = _adamw(
        (w_in.reshape(D, 1024), g_w_in, m_w_in.reshape(D, 1024), v_w_in.reshape(D, 1024)),
        (w_out.reshape(256, D), g_w_out, m_w_out.reshape(256, D), v_w_out.reshape(256, D)), small_all, params)
    loss = so[0].reshape(())
    g_s = [so[1 + 4 * p] for p in range(5)]
    d_s = [so[2 + 4 * p] for p in range(5)]
    m_s = [so[3 + 4 * p] for p in range(5)]
    v_s = [so[4 + 4 * p] for p in range(5)]
    for lst in (g_s, d_s, m_s, v_s):
        lst[4] = lst[4].reshape(D)

    return (loss, gx.reshape(1, T, D),
            g_w_in.reshape(1, D, 1024), g_w_out.reshape(1, 256, D), *g_s,
            d_in.reshape(1, D, 1024), d_out.reshape(1, 256, D), *d_s,
            nm_in.reshape(1, D, 1024), nm_out.reshape(1, 256, D), *m_s,
            nv_in.reshape(1, D, 1024), nv_out.reshape(1, 256, D), *v_s)
```

```python
import functools

import numpy as np
import jax
import jax.numpy as jnp
from jax import lax
from jax.experimental import pallas as pl
from jax.experimental.pallas import tpu as pltpu

F32 = jnp.float32
BF16 = jnp.bfloat16

T = 4096
D = 1024
AW = 512
HW = 512
NCOL = 4096
HEAD = 64
BLK = 128
CHUNK = 64
EPS = 1e-6
SCALE = HEAD ** -0.5
NEG = -1e30
ROPE_THETA = 500000.0
INV_FREQ = [float(v) for v in
            (np.float32(ROPE_THETA) ** (-(np.arange(8, dtype=np.float32)) * np.float32(0.125)))]
LR, B1, B2, AEPS, WD, STEP = 0.001, 0.9, 0.999, 1e-08, 0.01, 10
VMEM_LIMIT = 63 * 1024 * 1024
MESH = pl.DeviceIdType.MESH


def _cp(sem=None, **kw):
    return pltpu.CompilerParams(dimension_semantics=sem, vmem_limit_bytes=VMEM_LIMIT, **kw)


def _mm(a, b):
    return jnp.dot(a, b, preferred_element_type=F32)


def _mm_nt(a, b):
    return lax.dot_general(a, b, (((1,), (1,)), ((), ())), preferred_element_type=F32)


def _mm_tn(a, b):
    return lax.dot_general(a, b, (((0,), (0,)), ((), ())), preferred_element_type=F32)


def _mm_exact_l(mat_bf, x):
    h = x.astype(BF16)
    l = (x - h.astype(F32)).astype(BF16)
    return _mm(mat_bf, h) + _mm(mat_bf, l)


def _mm_exact_r(x, mat_bf):
    h = x.astype(BF16)
    l = (x - h.astype(F32)).astype(BF16)
    return _mm(h, mat_bf) + _mm(l, mat_bf)


def _sigmoid(x):
    return 0.5 * jnp.tanh(0.5 * x) + 0.5


def _rope_tables(pos):
    lane = lax.broadcasted_iota(jnp.int32, (1, 128), 1)
    jl = lane & 63
    fi = jl & 7
    inv = jnp.zeros((1, 128), F32)
    for kk in range(8):
        inv = jnp.where(fi == kk, INV_FREQ[kk], inv)
    ang = jnp.broadcast_to(pos.astype(F32), (128, pos.shape[1])).T * inv
    c = jnp.cos(ang)
    s = jnp.sin(ang)
    cosf = jnp.where(jl < 16, c, 1.0)
    s1 = jnp.where(jl < 8, -s, 0.0)
    s2 = jnp.where((jl >= 8) & (jl < 16), s, 0.0)
    return cosf, s1, s2


def _rope(t, cosf, s1, s2):
    parts = []
    for ci in range(t.shape[1] // 128):
        tc = t[:, ci * 128:(ci + 1) * 128]
        parts.append(tc * cosf + pltpu.roll(tc, 120, 1) * s1 + pltpu.roll(tc, 8, 1) * s2)
    return jnp.concatenate(parts, axis=1)


def _rope_bwd(g, cosf, s1, s2):
    parts = []
    for ci in range(g.shape[1] // 128):
        gc = g[:, ci * 128:(ci + 1) * 128]
        parts.append(gc * cosf + pltpu.roll(gc * s1, 8, 1) + pltpu.roll(gc * s2, 120, 1))
    return jnp.concatenate(parts, axis=1)


def _perm_store(val, scr, scr2, o1, o4, o16, dt):
    n = val.shape[0]
    q = n // 4
    o1[...] = val.astype(dt)
    for ci in range(val.shape[1] // 128):
        cs = slice(ci * 128, (ci + 1) * 128)
        scr[ci] = val[:, cs]
        for r4 in range(4):
            part = scr[ci, pl.ds(r4, q, stride=4), :]
            o4[r4, :, cs] = part.astype(dt)
            scr2[ci, r4 * q:(r4 + 1) * q, :] = part
        for r4 in range(4):
            for b in range(4):
                o16[r4 + 4 * b, :, cs] = scr2[ci, pl.ds(r4 * q + b, q // 4, stride=4), :].astype(dt)


def _unperm_load(r4, r16, scr_a, scr_b, scr_c):
    n = scr_a.shape[1]
    q = n // 4
    nc = r4.shape[-1] // 128
    for ci in range(nc):
        cs = slice(ci * 128, (ci + 1) * 128)
        for rr in range(4):
            scr_a[ci, pl.ds(rr, q, stride=4), :] = r4[rr, :, cs].astype(F32)
        for rr in range(4):
            for b in range(4):
                scr_c[ci, pl.ds(rr * q + b, q // 4, stride=4), :] = r16[rr + 4 * b, :, cs].astype(F32)
        for rr in range(4):
            scr_b[ci, pl.ds(rr, q, stride=4), :] = scr_c[ci, rr * q:(rr + 1) * q, :]
    return (jnp.concatenate([scr_a[ci] for ci in range(nc)], axis=1),
            jnp.concatenate([scr_b[ci] for ci in range(nc)], axis=1))


def _unperm_sum(r4, r16, scr_b, scr_c):
    n = scr_b.shape[1]
    q = n // 4
    nc = r4.shape[-1] // 128
    for ci in range(nc):
        cs = slice(ci * 128, (ci + 1) * 128)
        for rr in range(4):
            for b in range(4):
                scr_c[ci, pl.ds(rr * q + b, q // 4, stride=4), :] = r16[rr + 4 * b, :, cs].astype(F32)
        for rr in range(4):
            scr_b[ci, pl.ds(rr, q, stride=4), :] = scr_c[ci, rr * q:(rr + 1) * q, :] + r4[rr, :, cs].astype(F32)
    return jnp.concatenate([scr_b[ci] for ci in range(nc)], axis=1)


def _fwd_in(x, pos, mixw, w_in, w_out, jm_arr):
    TT = 512
    NT = T // TT

    def body(jm_ref, x_ref, pos_ref, mw_ref, win_ref, wout_ref,
             hnt_ref, q1, k1, v1, q4, k4, v4, q16, k16, v16, ag, hq, hf, hi, hg, wfull_o, woutfull_o,
             wbuf, wobuf, hn_all, scr, scr2, stage, send_sems, recv_sems, loc_sems):
        s = pl.program_id(0)
        i = pl.program_id(1)
        mx, my, c = lax.axis_index("x"), lax.axis_index("y"), lax.axis_index("c")
        me, sibling = (mx, my, c), (mx, my, 1 - c)
        chips = [(mx, 1 - my), (1 - mx, my), (1 - mx, 1 - my)]
        jm = 2 * mx + my
        rows_in = [pl.ds(pl.multiple_of(h * 512, 512), 512) for h in (c, 1 - c)]
        rows_out = [pl.ds(pl.multiple_of(h * 128, 128), 128) for h in (c, 1 - c)]

        def blk(k):
            return lax.bitwise_xor(jm, k + 1)

        def rc(n, ref, to):
            return pltpu.make_async_remote_copy(src_ref=ref, dst_ref=ref, send_sem=send_sems.at[n],
                                                recv_sem=recv_sems.at[n], device_id=to, device_id_type=MESH)

        halves = [pl.ds(0, 512), pl.ds(512, 512)]
        send_in = lambda k, h: rc(12 + 2 * k + h, wbuf.at[jm, rows_in[0], halves[h]], (*chips[k], c))
        got_in = lambda k, h: rc(12 + 2 * k + h, wbuf.at[blk(k), rows_in[0], halves[h]], me)
        relay = lambda h: rc(16 + h, wbuf.at[blk(h), rows_in[0], halves[h]], (*chips[1 - h], c))
        got_relay = lambda h: rc(16 + h, wbuf.at[blk(2), rows_in[0], halves[h]], me)
        send_out = lambda k: rc(3 + k, wobuf.at[jm, rows_out[0], :], (*chips[k], c))
        got_out = lambda k: rc(3 + k, wobuf.at[blk(k), rows_out[0], :], me)
        pass_in = lambda k: rc(6 + k, wbuf.at[blk(k), rows_in[0], :], sibling)
        pass_out = lambda k: rc(9 + k, wobuf.at[blk(k), rows_out[0], :], sibling)
        passed_in = lambda k: rc(6 + k, wbuf.at[blk(k), rows_in[1], :], me)
        passed_out = lambda k: rc(9 + k, wobuf.at[blk(k), rows_out[1], :], me)

        def keep(j, n):
            return pltpu.make_async_copy(wbuf.at[j], wfull_o.at[:, pl.ds(j * 1024, 1024)], loc_sems.at[n])

        @pl.when((s == 0) & (i == 0))
        def _():
            chunk = [pl.ds(pl.multiple_of(lax.rem(p + 2 * c, 4) * 256, 256), 256) for p in range(4)]
            loads = [pltpu.make_async_copy(win_ref.at[chunk[p], :] if p < 4 else wout_ref, stage.at[p % 2],
                                           loc_sems.at[4 + p % 2]) for p in range(5)]
            loads[0].start()
            for p in range(5):
                if p < 4:
                    loads[p + 1].start()
                loads[p].wait()
                if p < 4:
                    wbuf[jm, chunk[p], :] = stage[p % 2].astype(BF16)
                else:
                    wobuf[jm] = stage[p % 2].astype(BF16)
                if p == 1:
                    for k in range(2):
                        for h in range(2):
                            send_in(k, h).start()
            keep(jm, 0).start()

        @pl.when((s == 0) & (i == jnp.where(jm == 0, NT - 3, NT - 1)))
        def _():
            for kk in range(2):
                for h in range(2):
                    got_in(kk, h).wait_recv()
            relay(0).start()
            relay(1).start()
            pass_in(0).start()
            pass_in(1).start()
            passed_in(0).wait_recv()
            keep(blk(0), 1).start()

        @pl.when((s == 1) & (i == NT - 1))
        def _():
            got_relay(0).wait_recv()
            got_relay(1).wait_recv()
            pass_in(2).start()

        @pl.when((s == 2) & (i == 0))
        def _():
            for k in (1, 2):
                passed_in(k).wait_recv()
                keep(blk(k), k + 1).start()
            for kk in range(3):
                send_out(kk).start()

        @pl.when((s == 2) & (i == NT - 2))
        def _():
            for k in range(3):
                got_out(k).wait_recv()
                pass_out(k).start()

        whole_out = pltpu.make_async_copy(wobuf, woutfull_o, loc_sems.at[4])

        @pl.when((s == 2) & (i == NT - 1))
        def _():
            for k in range(3):
                passed_out(k).wait_recv()
            whole_out.start()

        tile = pl.ds(pl.multiple_of(i * TT, TT), TT)

        @pl.when(s == 0)
        def _():
            xv = x_ref[...]
            r = lax.rsqrt(jnp.mean(xv * xv, axis=-1, keepdims=True) + EPS)
            hnf = (xv * r) * mw_ref[...]
            hn_all[tile, :] = hnf.astype(BF16)
            hnt_ref[...] = hnf.T.astype(BF16)

        def project(jj):
            hn = hn_all[tile, :]
            lo = _mm(hn, wbuf[jj, :, 0:512])
            hi_cols = _mm(hn, wbuf[jj, :, 512:1024])
            if jj == 0:
                cosf, s1, s2 = _rope_tables(pos_ref[...])
                _perm_store(_rope(lo, cosf, s1, s2) * SCALE, scr, scr2, q1, q4, q16, BF16)
                _perm_store(_rope(hi_cols, cosf, s1, s2), scr, scr2, k1, k4, k16, BF16)
            elif jj == 1:
                _perm_store(lo, scr, scr2, v1, v4, v16, BF16)
                ag[...] = hi_cols.astype(BF16)
            elif jj == 2:
                hq[...] = lo.astype(BF16)
                hf[...] = hi_cols.astype(BF16)
            else:
                hi[...] = lo.astype(BF16)
                hg[...] = hi_cols.astype(BF16)

        def project_block(j):
            for jj in range(4):
                pl.when(j == jj)(functools.partial(project, jj))

        @pl.when(s < 2)
        def _():
            project_block(lax.bitwise_xor(jm, s))

        @pl.when(s == 2)
        def _():
            project_block(lax.bitwise_xor(jm, 2))
            project_block(lax.bitwise_xor(jm, 3))

        @pl.when((s == 2) & (i == NT - 1))
        def _():
            for h in range(2):
                relay(h).wait_send()
                for k in range(2):
                    send_in(k, h).wait_send()
            for k in range(3):
                send_out(k).wait_send()
                pass_in(k).wait_send()
                pass_out(k).wait_send()
            keep(jm, 0).wait()
            for k in range(3):
                keep(blk(k), k + 1).wait()
            whole_out.wait()

    def at_stage_of(jb):
        def index(s, i, jm_ref):
            sa = jnp.minimum(lax.bitwise_xor(jm_ref[0], jb), 2)
            return jnp.where(s < sa, 0, jnp.where(s == sa, i, NT - 1))
        return index

    tok = lambda w, jb: pl.BlockSpec((TT, w), lambda s, i, jm_ref: (at_stage_of(jb)(s, i, jm_ref), 0))
    d4 = lambda jb: pl.BlockSpec((4, TT // 4, AW), lambda s, i, jm_ref: (0, at_stage_of(jb)(s, i, jm_ref), 0))
    d16 = lambda jb: pl.BlockSpec((16, TT // 16, AW), lambda s, i, jm_ref: (0, at_stage_of(jb)(s, i, jm_ref), 0))
    hbm = pl.BlockSpec(memory_space=pltpu.HBM)
    sd = lambda shape, dt: jax.ShapeDtypeStruct(shape, dt)
    in_own_stage = lambda s, i: jnp.where(s == 0, i, NT - 1)
    grid_spec = pltpu.PrefetchScalarGridSpec(
        num_scalar_prefetch=1, grid=(3, NT),
        in_specs=[pl.BlockSpec((TT, D), lambda s, i, jm_ref: (in_own_stage(s, i), 0)),
                  pl.BlockSpec((1, TT), lambda s, i, jm_ref: (0, i)),
                  pl.BlockSpec((1, D), lambda s, i, jm_ref: (0, 0)), hbm, hbm],
        out_specs=[pl.BlockSpec((D, TT), lambda s, i, jm_ref: (0, in_own_stage(s, i))),
                   tok(AW, 0), tok(AW, 0), tok(AW, 1), d4(0), d4(0), d4(1), d16(0), d16(0), d16(1),
                   tok(AW, 1), tok(AW, 2), tok(AW, 2), tok(AW, 3), tok(AW, 3), hbm, hbm],
        scratch_shapes=[pltpu.VMEM((4, D, 1024), BF16), pltpu.VMEM((4, 256, D), BF16), pltpu.VMEM((T, D), BF16),
                        pltpu.VMEM((4, TT, 128), F32), pltpu.VMEM((4, TT, 128), F32), pltpu.VMEM((2, 256, 1024), F32),
                        pltpu.SemaphoreType.DMA((18,)),
                        pltpu.SemaphoreType.DMA((18,)), pltpu.SemaphoreType.DMA((6,))])
    return pl.pallas_call(
        body, name="fwd_in", grid_spec=grid_spec,
        out_shape=[sd((D, T), BF16)] + [sd((T, AW), BF16)] * 3 + [sd((4, T // 4, AW), BF16)] * 3
        + [sd((16, T // 16, AW), BF16)] * 3
        + [sd((T, AW), BF16)] * 5 + [sd((D, NCOL), BF16), sd((4, 256, D), BF16)],
        compiler_params=_cp(("arbitrary", "arbitrary")),
    )(jm_arr, x, pos, mixw, w_in, w_out)


def _band_mask(key_axis, nkeys=2 * BLK):
    shape = (nkeys, 2 * BLK) if key_axis == 0 else (2 * BLK, nkeys)
    kj = lax.broadcasted_iota(jnp.int32, shape, key_axis)
    qi = lax.broadcasted_iota(jnp.int32, shape, 1 - key_axis) & (BLK - 1)
    return (kj >= qi) & (kj <= qi + BLK), kj, qi


def _stack_heads(t2, in_a):
    z = jnp.zeros_like(t2)
    return jnp.concatenate([jnp.where(in_a[0], t2, z), jnp.where(in_a[1], t2, z)], axis=0)


def _attn_fwd(q, k, v, nb, name):
    n = 8
    CH = n * BLK
    halo = nb > n

    def body(*refs):
        if halo:
            q_ref, k_ref, v_ref, kp_ref, vp_ref, o_ref, lse_ref = refs
        else:
            q_ref, k_ref, v_ref, o_ref, lse_ref = refs
        lane = lax.broadcasted_iota(jnp.int32, (1, 128), 1)
        in_a = [lane < HEAD, lane >= HEAD]
        band, kj, _ = _band_mask(1)
        thr0 = jnp.where((n * pl.program_id(0)) % nb == 0, BLK, 0) if halo else BLK
        mask0 = band & (kj >= thr0)
        mask_first = band & (kj >= BLK)
        for b in range(n):
            rs = slice(b * BLK, (b + 1) * BLK)
            stat = jnp.zeros((BLK, 128), F32)
            for hp in range(4):
                cs = slice(hp * 128, (hp + 1) * 128)
                q2s = _stack_heads(q_ref[rs, cs], in_a)
                if b == 0:
                    kprev = kp_ref[:, cs] if halo else k_ref[rs, cs]
                    vprev = vp_ref[:, cs] if halo else v_ref[rs, cs]
                    kk = jnp.concatenate([kprev, k_ref[rs, cs]], axis=0)
                    vv = jnp.concatenate([vprev, v_ref[rs, cs]], axis=0)
                    mask = mask0
                else:
                    kk = k_ref[(b - 1) * BLK:(b + 1) * BLK, cs]
                    vv = v_ref[(b - 1) * BLK:(b + 1) * BLK, cs]
                    mask = mask_first if b % nb == 0 else band
                s = jnp.where(mask, _mm_nt(q2s, kk), NEG)
                m = jnp.max(s, axis=-1, keepdims=True)
                p = jnp.exp(s - m)
                l = jnp.sum(p, axis=-1, keepdims=True)
                o = _mm(p.astype(BF16), vv) / l
                lse = m + jnp.log(l)
                o_ref[rs, cs] = jnp.where(in_a[0], o[:BLK], o[BLK:]).astype(BF16)
                stat = jnp.where(lane == 2 * hp, lse[:BLK], stat)
                stat = jnp.where(lane == 2 * hp + 1, lse[BLK:], stat)
            lse_ref[rs, :] = stat

    cur = pl.BlockSpec((CH, AW), lambda i: (i, 0))
    prev = pl.BlockSpec((BLK, AW), lambda i: (jnp.maximum(n * i - 1, 0), 0))
    return pl.pallas_call(
        body, name=name, grid=(T // CH,),
        in_specs=[cur, cur, cur] + ([prev, prev] if halo else []),
        out_specs=[cur, pl.BlockSpec((CH, 128), lambda i: (i, 0))],
        out_shape=[jax.ShapeDtypeStruct((T, AW), BF16), jax.ShapeDtypeStruct((T, 128), F32)],
        compiler_params=_cp(("parallel",)),
    )(*((q, k, v) + ((k, v) if halo else ())))


def _attn_bwd(q, k, v, do, st, nb, name):
    n = 8
    CH = n * BLK
    NBLK = T // BLK
    halo = nb > n

    def body(*refs):
        if halo:
            (q_ref, k_ref, v_ref, do_ref, st_ref, kp_ref, vp_ref, qn_ref, don_ref, stn_ref,
             dq_ref, dk_ref, dv_ref) = refs
        else:
            q_ref, k_ref, v_ref, do_ref, st_ref, dq_ref, dk_ref, dv_ref = refs
        i = pl.program_id(0)
        lane = lax.broadcasted_iota(jnp.int32, (1, 128), 1)
        in_a = [lane < HEAD, lane >= HEAD]
        band, kj, _ = _band_mask(0)
        thr0 = jnp.where((n * i) % nb == 0, BLK, 0) if halo else BLK
        mask0 = band & (kj >= thr0)
        mask_first = band & (kj >= BLK)

        def stat_rows(st_t, hp):
            lse_r = jnp.concatenate([st_t[2 * hp:2 * hp + 1, :], st_t[2 * hp + 1:2 * hp + 2, :]], axis=1)
            dl_r = jnp.concatenate([st_t[8 + 2 * hp:9 + 2 * hp, :], st_t[9 + 2 * hp:10 + 2 * hp, :]], axis=1)
            return lse_r, dl_r

        st_t = [st_ref[b * BLK:(b + 1) * BLK, :].T for b in range(n)]
        if halo:
            nxt_thr = jnp.where((n * i + n) % nb == 0, 2 * BLK, 0)
            _, kj1, qi1 = _band_mask(0, BLK)
            mask_next = kj1 >= qi1 + nxt_thr
            stn_t = stn_ref[...].T

        for hp in range(4):
            cs = slice(hp * 128, (hp + 1) * 128)
            kb = [k_ref[b * BLK:(b + 1) * BLK, cs] for b in range(n)]
            vb = [v_ref[b * BLK:(b + 1) * BLK, cs] for b in range(n)]
            dk_acc = [jnp.zeros((BLK, 128), F32) for _ in range(n)]
            dv_acc = [jnp.zeros((BLK, 128), F32) for _ in range(n)]
            for b in range(n):
                rs = slice(b * BLK, (b + 1) * BLK)
                q2s = _stack_heads(q_ref[rs, cs], in_a)
                do2s = _stack_heads(do_ref[rs, cs], in_a)
                if b == 0:
                    kprev = kp_ref[:, cs] if halo else kb[0]
                    vprev = vp_ref[:, cs] if halo else vb[0]
                    mask = mask0
                else:
                    kprev, vprev, mask = kb[b - 1], vb[b - 1], (mask_first if b % nb == 0 else band)
                kk = jnp.concatenate([kprev, kb[b]], axis=0)
                vv = jnp.concatenate([vprev, vb[b]], axis=0)
                lse_r, dl_r = stat_rows(st_t[b], hp)
                s_t = jnp.where(mask, _mm_nt(kk, q2s), NEG)
                p_t = jnp.exp(s_t - lse_r)
                ds_t = (p_t * (_mm_nt(vv, do2s) - dl_r)).astype(BF16)
                dkk = _mm(ds_t, q2s)
                dvv = _mm(p_t.astype(BF16), do2s)
                dqs = _mm_tn(ds_t, kk) * SCALE
                dq_ref[rs, cs] = jnp.where(in_a[0], dqs[:BLK], dqs[BLK:]).astype(BF16)
                dk_acc[b] += dkk[BLK:]
                dv_acc[b] += dvv[BLK:]
                if b > 0:
                    dk_acc[b - 1] += dkk[:BLK]
                    dv_acc[b - 1] += dvv[:BLK]
            if halo:
                q2s = _stack_heads(qn_ref[:, cs], in_a)
                do2s = _stack_heads(don_ref[:, cs], in_a)
                lse_r, dl_r = stat_rows(stn_t, hp)
                s_t = jnp.where(mask_next, _mm_nt(kb[n - 1], q2s), NEG)
                p_t = jnp.exp(s_t - lse_r)
                ds_t = (p_t * (_mm_nt(vb[n - 1], do2s) - dl_r)).astype(BF16)
                dk_acc[n - 1] += _mm(ds_t, q2s)
                dv_acc[n - 1] += _mm(p_t.astype(BF16), do2s)
            for b in range(n):
                dk_ref[b * BLK:(b + 1) * BLK, cs] = dk_acc[b].astype(BF16)
                dv_ref[b * BLK:(b + 1) * BLK, cs] = dv_acc[b].astype(BF16)

    cur = pl.BlockSpec((CH, AW), lambda i: (i, 0))
    cur_st = pl.BlockSpec((CH, 128), lambda i: (i, 0))
    prev = pl.BlockSpec((BLK, AW), lambda i: (jnp.maximum(n * i - 1, 0), 0))
    nxt = pl.BlockSpec((BLK, AW), lambda i: (jnp.minimum(n * i + n, NBLK - 1), 0))
    nxt_st = pl.BlockSpec((BLK, 128), lambda i: (jnp.minimum(n * i + n, NBLK - 1), 0))
    ins = [cur] * 4 + [cur_st] + ([prev, prev, nxt, nxt, nxt_st] if halo else [])
    args = (q, k, v, do, st) + ((k, v, q, do, st) if halo else ())
    return pl.pallas_call(
        body, name=name, grid=(T // CH,),
        in_specs=ins,
        out_specs=[cur] * 3,
        out_shape=[jax.ShapeDtypeStruct((T, AW), BF16)] * 3,
        compiler_params=_cp(("parallel",)),
    )(*args)


TH = 256
NCH = TH // CHUNK


def _hgrn_common(hq_ref, hf_ref, lbr_ref, tri_ref):
    r0 = lbr_ref[0:1, :]
    r1 = lbr_ref[1:2, :]
    mx = jnp.maximum(r0, r1)
    e0 = jnp.exp(r0 - mx)
    e1 = jnp.exp(r1 - mx)
    lb = e0 / (e0 + e1)
    hqv = hq_ref[...].astype(F32)
    sq = _sigmoid(hqv)
    qv = hqv * sq
    sf = _sigmoid(hf_ref[...].astype(F32))
    f = lb + (1.0 - lb) * sf
    kv = 1.0 - f
    g = jnp.log(f)
    cum = _mm_exact_l(tri_ref[...], g)
    dec = jnp.exp(jnp.concatenate([cum[c * CHUNK + CHUNK - 1:(c + 1) * CHUNK, :] for c in range(NCH)], axis=0))
    decb = jnp.concatenate([jnp.broadcast_to(dec[c:c + 1, :], (CHUNK, HW)) for c in range(NCH)], axis=0)
    ea = jnp.exp(cum)
    ena = jnp.exp(-cum)
    eend = decb * ena
    return dict(lb=lb, hq=hqv, sq=sq, q=qv, sf=sf, f=f, k=kv, cum=cum, ea=ea, ena=ena, eend=eend,
                qd=qv * ea, ki=kv * ena, ke=kv * eend, dec=dec)


def _tri_mask(transposed=False):
    ti = lax.broadcasted_iota(jnp.int32, (TH, TH), 1 if transposed else 0)
    si = lax.broadcasted_iota(jnp.int32, (TH, TH), 0 if transposed else 1)
    return (si <= ti) & ((si // CHUNK) == (ti // CHUNK))


def _hgrn_fwd(hq, hf, hi, lbr, tri):
    NSUB = 2

    def body(hq_ref, hf_ref, hi_ref, lbr_ref, tri_ref, rec_ref, sall_ref, st_scr):
        @pl.when(pl.program_id(0) == 0)
        def _():
            st_scr[...] = jnp.zeros_like(st_scr)

        causal = _tri_mask()
        for u in range(NSUB):
            tile = slice(u * TH, (u + 1) * TH)
            w = _hgrn_common(hq_ref.at[tile, :], hf_ref.at[tile, :], lbr_ref, tri_ref)
            qd, ki, ke = w["qd"].astype(BF16), w["ki"].astype(BF16), w["ke"].astype(BF16)
            dec = w["dec"]
            vb = hi_ref[tile, :]
            for h in range(4):
                cs = slice(h * 128, (h + 1) * 128)
                att = jnp.where(causal, _mm_nt(qd[:, cs], ki[:, cs]), 0.0)
                o_intra = _mm(att.astype(BF16), vb[:, cs])
                st = st_scr[:, cs]
                for c in range(NCH):
                    rs = slice(c * CHUNK, (c + 1) * CHUNK)
                    sall_ref[u * NCH + c, :, cs] = st
                    rec_ref[u * TH + c * CHUNK:u * TH + (c + 1) * CHUNK, cs] = (
                        o_intra[rs] + _mm_nt(qd[rs, cs], st.astype(BF16))).astype(BF16)
                    st = dec[c:c + 1, cs] * st + _mm_tn(vb[rs, cs], ke[rs, cs])
                st_scr[:, cs] = st

    tok = pl.BlockSpec((NSUB * TH, HW), lambda i: (i, 0))
    return pl.pallas_call(
        body, name="hgrn_fwd", grid=(T // (NSUB * TH),),
        in_specs=[tok, tok, tok, pl.BlockSpec((2, HW), lambda i: (0, 0)), pl.BlockSpec((TH, TH), lambda i: (0, 0))],
        out_specs=[tok, pl.BlockSpec((NSUB * NCH, 128, HW), lambda i: (i, 0, 0))],
        out_shape=[jax.ShapeDtypeStruct((T, HW), BF16), jax.ShapeDtypeStruct((T // CHUNK, 128, HW), F32)],
        scratch_shapes=[pltpu.VMEM((128, HW), F32)],
        compiler_params=_cp(("arbitrary",)),
    )(hq, hf, hi, lbr, tri)


def _hgrn_bwd(hq, hf, hi, lbr, tri, trit, drec, sall, dhg, rout, routb):
    NSUB = 2
    NT = T // (NSUB * TH)

    def body(hq_ref, hf_ref, hi_ref, lbr_ref, tri_ref, trit_ref, do_ref, sall_ref, dhg_ref, rout_r, routb_r,
             dph_ref, small_ref, pout_o, poutr_o,
             dst_scr, dlb_scr, dqd_scr, dki_scr, dke_scr, dlast_scr, send_sems, recv_sems, loc_sems):
        step = pl.program_id(0)
        loc, rem = _chip_copies(_w_out_piece, rout_r, routb_r, pout_o, poutr_o, send_sems, recv_sems,
                                loc_sems.at[0])

        @pl.when(step == 0)
        def _():
            dst_scr[...] = jnp.zeros_like(dst_scr)
            dlb_scr[...] = jnp.zeros_like(dlb_scr)
            for cp in loc + rem:
                cp.start()

        causal = _tri_mask()
        causal_t = _tri_mask(transposed=True)
        lb = None
        for u in reversed(range(NSUB)):
            tile = slice(u * TH, (u + 1) * TH)
            w = _hgrn_common(hq_ref.at[tile, :], hf_ref.at[tile, :], lbr_ref, tri_ref)
            qd, ki, ke = w["qd"].astype(BF16), w["ki"].astype(BF16), w["ke"].astype(BF16)
            dec = w["dec"]
            vb = hi_ref[tile, :]
            dob = do_ref[tile, :].astype(BF16)
            for h in range(4):
                cs = slice(h * 128, (h + 1) * 128)
                att_t = jnp.where(causal_t, _mm_nt(ki[:, cs], qd[:, cs]), 0.0).astype(BF16)
                datt_t = jnp.where(causal_t, _mm_nt(vb[:, cs], dob[:, cs]), 0.0).astype(BF16)
                datt = jnp.where(causal, _mm_nt(dob[:, cs], vb[:, cs]), 0.0).astype(BF16)
                dv_intra = _mm(att_t, dob[:, cs])
                dqd_intra = _mm(datt, ki[:, cs])
                dki_scr[u, :, cs] = _mm(datt_t, qd[:, cs])
                dst = dst_scr[:, cs]
                for c in reversed(range(NCH)):
                    rs = slice(c * CHUNK, (c + 1) * CHUNK)
                    dec_c = dec[c:c + 1, :]
                    st = sall_ref[u * NCH + c, :, cs]
                    dstb = dst.astype(BF16)
                    dph_ref[u * TH + c * CHUNK:u * TH + (c + 1) * CHUNK, 2 * HW + h * 128:2 * HW + (h + 1) * 128] = (
                        dv_intra[rs] + _mm_nt(ke[rs, cs], dstb)).astype(BF16)
                    dqd_scr[u, rs, cs] = dqd_intra[rs] + _mm(dob[rs, cs], st.astype(BF16))
                    dke_scr[u, rs, cs] = _mm(vb[rs, cs], dstb)
                    ddec = jnp.sum(dst * st, axis=0, keepdims=True)
                    dlast_scr[u, c:c + 1, cs] = ddec * dec_c[:, cs]
                    dst = dec_c[:, cs] * dst + _mm_tn(dob[rs, cs], qd[rs, cs])
                dst_scr[:, cs] = dst
            dqd, dki, dke = dqd_scr[u], dki_scr[u], dke_scr[u]
            dq = dqd * w["ea"]
            dk = dki * w["ena"] + dke * w["eend"]
            dcum = dqd * w["qd"] - dki * w["ki"] - dke * w["ke"]
            dkeke = dke * w["ke"]
            dlastb = jnp.concatenate(
                [jnp.broadcast_to(dlast_scr[u, c:c + 1, :]
                                  + jnp.sum(dkeke[c * CHUNK:(c + 1) * CHUNK], axis=0, keepdims=True), (CHUNK, HW))
                 for c in range(NCH)], axis=0)
            dg = _mm_exact_l(trit_ref[...], dcum) + dlastb
            df = dg / w["f"] - dk
            lb, sf, sq = w["lb"], w["sf"], w["sq"]
            dph_ref[tile, HW:2 * HW] = (df * (1.0 - lb) * sf * (1.0 - sf)).astype(BF16)
            dph_ref[tile, 0:HW] = (dq * (sq * (1.0 + w["hq"] * (1.0 - sq)))).astype(BF16)
            dph_ref[tile, 3 * HW:4 * HW] = dhg_ref[tile, :]
            dlb_scr[...] += jnp.sum(df * (1.0 - sf), axis=0, keepdims=True)

        @pl.when(step == NT - 1)
        def _():
            gr = dlb_scr[...] * lb * (1.0 - lb)
            small_ref[...] = jnp.zeros_like(small_ref)
            small_ref[0:1, 0:HW] = gr
            small_ref[1:2, 0:HW] = -gr
            for cp in rem:
                cp.wait_recv()
            for cp in rem:
                cp.wait_send()
            for cp in loc:
                cp.wait()

    tok = pl.BlockSpec((NSUB * TH, HW), lambda i: (NT - 1 - i, 0))
    const = lambda shape: pl.BlockSpec(shape, lambda i: (0,) * len(shape))
    hbm = pl.BlockSpec(memory_space=pltpu.HBM)
    return pl.pallas_call(
        body, name="hgrn_bwd", grid=(NT,),
        in_specs=[tok, tok, tok, const((2, HW)), const((TH, TH)), const((TH, TH)), tok,
                  pl.BlockSpec((NSUB * NCH, 128, HW), lambda i: (NT - 1 - i, 0, 0)), tok, hbm, hbm],
        out_specs=[pl.BlockSpec((NSUB * TH, NCOL // 2), lambda i: (NT - 1 - i, 0)), const((8, D)), hbm, hbm],
        out_shape=[jax.ShapeDtypeStruct((T, NCOL // 2), BF16), jax.ShapeDtypeStruct((8, D), F32),
                   jax.ShapeDtypeStruct((128, D), F32), jax.ShapeDtypeStruct((3, 128, D), BF16)],
        scratch_shapes=[pltpu.VMEM((128, HW), F32), pltpu.VMEM((1, HW), F32), pltpu.VMEM((NSUB, TH, HW), F32),
                        pltpu.VMEM((NSUB, TH, HW), F32), pltpu.VMEM((NSUB, TH, HW), F32),
                        pltpu.VMEM((NSUB, 8, HW), F32),
                        pltpu.SemaphoreType.DMA((3,)), pltpu.SemaphoreType.DMA((3,)), pltpu.SemaphoreType.DMA((1,))],
        compiler_params=_cp(("arbitrary",)),
    )(hq, hf, hi, lbr, tri, trit, drec, sall, dhg, rout, routb)


def _fwd_out(o1, o4, o16, l1, l4, l16, rec, ag, hg, x, tgt, anw, hnw, fnw, wout_full, gmat, emat, selmat):
    TT = 512

    def body(o1_r, o4_r, o16_r, l1_r, l4_r, l16_r, rec_r, ag_r, hg_r, x_r, tgt_r, anw_r, hnw_r, fnw_r, wo_r, g_r,
             e_r, sel_r, dx2_o, do1_o, do4_o, do16_o, st1_o, st4_o, st16_o, drec_o, dag_o, dhg_o,
             rout_o, routb_o, small_o, scr_a, scr_b, scr_c, gwout_o, rbuf, send_sems, recv_sems):
        @pl.when(pl.program_id(0) == 0)
        def _():
            gwout_o[...] = jnp.zeros_like(gwout_o)
            small_o[...] = jnp.zeros_like(small_o)

        def unperm(r4, r16):
            return _unperm_load(r4, r16, scr_a, scr_b, scr_c)

        def perm_out(val, p1, p4, p16, dt):
            _perm_store(val, scr_a, scr_b, p1, p4, p16, dt)

        o4u, o16u = unperm(o4_r, o16_r)
        l4c, l16c = unperm(l4_r, l16_r)
        l1c = l1_r[...]
        mxc = jnp.maximum(jnp.maximum(l1c, l4c), l16c)
        w1c, w4c, w16c = jnp.exp(l1c - mxc), jnp.exp(l4c - mxc), jnp.exp(l16c - mxc)
        denc = w1c + w4c + w16c
        lane = lax.broadcasted_iota(jnp.int32, (1, 128), 1)
        lse_c = jnp.where(lane < 8, mxc + jnp.log(denc), 0.0)
        em = e_r[...]
        wn1 = _mm_exact_r(w1c / denc, em)
        wn4 = _mm_exact_r(w4c / denc, em)
        o1v = o1_r[...].astype(F32)
        attn = wn1 * o1v + wn4 * o4u + (1.0 - wn1 - wn4) * o16u
        gm = g_r[...]

        def head_mean_a(t):
            return jnp.concatenate([_mm_exact_r(t[:, :256], gm), _mm_exact_r(t[:, 256:], gm)], axis=1)

        def head_mean_h(t):
            return jnp.concatenate(
                [jnp.broadcast_to(jnp.mean(t[:, h * 128:(h + 1) * 128], axis=-1, keepdims=True), (TT, 128))
                 for h in range(4)], axis=1)

        rs_a = lax.rsqrt(head_mean_a(attn * attn) + EPS)
        n_a = attn * rs_a
        agv = ag_r[...].astype(F32)
        sg_a = _sigmoid(agv)
        si_a = agv * sg_a
        anw_v = anw_r[...]
        y_a = (n_a * anw_v) * si_a
        recv = rec_r[...].astype(F32)
        rs_h = lax.rsqrt(head_mean_h(recv * recv) + EPS)
        n_h = recv * rs_h
        hgv = hg_r[...].astype(F32)
        sg_h = _sigmoid(hgv)
        si_h = hgv * sg_h
        hnw_v = hnw_r[...]
        y_h = (n_h * hnw_v) * si_h
        mixed = jnp.concatenate([y_a, y_h], axis=1).astype(BF16)
        xv = x_r[...]
        x2 = xv + _mm(mixed, wo_r[...])
        r2 = lax.rsqrt(jnp.mean(x2 * x2, axis=-1, keepdims=True) + EPS)
        fnw_v = fnw_r[...]
        xn = x2 * r2
        err = xn * fnw_v - tgt_r[...]
        small_o[2:3, :] += 0.5 * jnp.sum(jnp.mean(err * err, axis=-1, keepdims=True), axis=0, keepdims=True)
        small_o[0:1, :] += jnp.sum(err * xn, axis=0, keepdims=True) * (1.0 / D)
        dyw = err * (fnw_v * (1.0 / D))
        dx2 = r2 * dyw - x2 * ((r2 * r2 * r2) * jnp.mean(dyw * x2, axis=-1, keepdims=True))
        dx2_o[...] = dx2
        dx2b = dx2.astype(BF16)
        gwout_o[...] += _mm_tn(mixed, dx2b)
        dmix = _mm_nt(dx2b, wo_r[...])
        dm_a, dm_h = dmix[:, :AW], dmix[:, AW:]
        dag_o[...] = (dm_a * (n_a * anw_v) * (sg_a * (1.0 + agv * (1.0 - sg_a)))).astype(BF16)
        dy_a = dm_a * si_a
        dn_a = dy_a * anw_v
        small_o[1:2, 0:AW] += jnp.sum(dy_a * n_a, axis=0, keepdims=True)
        dattn = rs_a * (dn_a - n_a * head_mean_a(dn_a * n_a))
        perm_out(dattn, do1_o, do4_o, do16_o, BF16)
        stats = lse_c + _mm_exact_r(dattn * attn, sel_r[...])
        perm_out(stats, st1_o, st4_o, st16_o, F32)
        dhg_o[...] = (dm_h * (n_h * hnw_v) * (sg_h * (1.0 + hgv * (1.0 - sg_h)))).astype(BF16)
        dy_h = dm_h * si_h
        dn_h = dy_h * hnw_v
        small_o[1:2, AW:] += jnp.sum(dy_h * n_h, axis=0, keepdims=True)
        drec_o[...] = (rs_h * (dn_h - n_h * head_mean_h(dn_h * n_h))).astype(BF16)

        @pl.when(pl.program_id(0) == T // TT - 1)
        def _():
            x, y, c = lax.axis_index("x"), lax.axis_index("y"), lax.axis_index("c")
            cps = [pltpu.make_async_remote_copy(
                src_ref=gwout_o.at[pl.ds(pl.multiple_of(j * 256 + (1 - c) * 128, 128), 128), :], dst_ref=rbuf.at[j],
                send_sem=send_sems.at[j], recv_sem=recv_sems.at[j], device_id=(x, y, 1 - c), device_id_type=MESH)
                for j in range(4)]
            for cp in cps:
                cp.start()
            for j, cp in enumerate(cps):
                cp.wait_recv()
                red = gwout_o[pl.ds(pl.multiple_of(j * 256 + c * 128, 128), 128), :] + rbuf[j]
                rout_o[j * 128:(j + 1) * 128, :] = red
                routb_o[j * 128:(j + 1) * 128, :] = red.astype(BF16)
            for cp in cps:
                cp.wait_send()

    tok = lambda w: pl.BlockSpec((TT, w), lambda i: (i, 0))
    d4 = pl.BlockSpec((4, TT // 4, AW), lambda i: (0, i, 0))
    d16 = pl.BlockSpec((16, TT // 16, AW), lambda i: (0, i, 0))
    const = lambda shape: pl.BlockSpec(shape, lambda i: (0,) * len(shape))
    sd = lambda shape, dt: jax.ShapeDtypeStruct(shape, dt)
    c4 = pl.BlockSpec((4, TT // 4, 128), lambda i: (0, i, 0))
    c16 = pl.BlockSpec((16, TT // 16, 128), lambda i: (0, i, 0))
    p3 = lambda w, dt: [sd((T, w), dt), sd((4, T // 4, w), dt), sd((16, T // 16, w), dt)]
    return pl.pallas_call(
        body, name="fwd_out", grid=(T // TT,),
        in_specs=[tok(AW), d4, d16, tok(128), c4, c16, tok(AW), tok(AW), tok(AW), tok(D), tok(D),
                  const((1, AW)), const((1, HW)), const((1, D)), const((D, D)), const((256, 256)),
                  const((128, AW)), const((AW, 128))],
        out_specs=[tok(D)] + [tok(AW), d4, d16] + [tok(128), c4, c16] + [tok(AW)] * 3
        + [const((512, D)), const((512, D)), const((8, D))],
        out_shape=[sd((T, D), F32)] + p3(AW, BF16) + p3(128, F32)
        + [sd((T, AW), BF16), sd((T, AW), BF16), sd((T, AW), BF16), sd((512, D), F32), sd((512, D), BF16),
           sd((8, D), F32)],
        scratch_shapes=[pltpu.VMEM((4, TT, 128), F32)] * 3 + [pltpu.VMEM((D, D), F32),
                        pltpu.VMEM((4, 128, D), F32), pltpu.SemaphoreType.DMA((4,)), pltpu.SemaphoreType.DMA((4,))],
        compiler_params=_cp(("arbitrary",)),
    )(o1, o4, o16, l1, l4, l16, rec, ag, hg, x, tgt, anw, hnw, fnw, wout_full, gmat, emat, selmat)


def _dproj_build(dq, dk, dv, dag, pos):
    TT = 512

    def body(dq1, dq4, dq16, dk1, dk4, dk16, dv1, dv4, dv16, dag_r, pos_r, dproj_o, scr_b, scr_c):
        def unperm_sum(r1, r4, r16):
            return r1[...] + _unperm_sum(r4, r16, scr_b, scr_c)

        cosf, s1, s2 = _rope_tables(pos_r[...])
        dproj_o[:, 0:512] = _rope_bwd(unperm_sum(dq1, dq4, dq16), cosf, s1, s2).astype(BF16)
        dproj_o[:, 512:1024] = _rope_bwd(unperm_sum(dk1, dk4, dk16), cosf, s1, s2).astype(BF16)
        dproj_o[:, 1024:1536] = unperm_sum(dv1, dv4, dv16).astype(BF16)
        dproj_o[:, 1536:2048] = dag_r[...]

    tok = lambda w: pl.BlockSpec((TT, w), lambda i: (i, 0))
    d4 = pl.BlockSpec((4, TT // 4, AW), lambda i: (0, i, 0))
    d16 = pl.BlockSpec((16, TT // 16, AW), lambda i: (0, i, 0))
    return pl.pallas_call(
        body, name="dproj_build", grid=(T // TT,),
        in_specs=[tok(AW), d4, d16] * 3 + [tok(AW), pl.BlockSpec((1, TT), lambda i: (0, i))],
        out_specs=tok(NCOL // 2),
        out_shape=jax.ShapeDtypeStruct((T, NCOL // 2), BF16),
        scratch_shapes=[pltpu.VMEM((4, TT, 128), F32)] * 2,
        compiler_params=_cp(("parallel",)),
    )(*dq, *dk, *dv, dag, pos)


def _bwd_x(dproj_a, dproj_h, x, dx2, mixw, w_full, rin, rinb, small4, small6, pout_own, pout_rem):
    TT = 256
    NT = T // TT

    def body(dpa_r, dph_r, x_r, dx2_r, mw_r, w_r, rin_r, rinb_r, s4_r, s6_r, poo_r, por_r,
             gx_o, sall_o, fin_o, fout_o, sbuf, v_own, v_rem, vo_own, vo_rem, sin, sout, got_in,
             got_out, send_sems, recv_sems, loc_sems, share_send, share_recv, fin_sems):
        i = pl.program_id(0)
        loc, rem = _chip_copies(_w_in_piece, rin_r, rinb_r, v_own, v_rem, send_sems, recv_sems, loc_sems.at[0])
        loads = [pltpu.make_async_copy(poo_r, vo_own, fin_sems.at[2]),
                 pltpu.make_async_copy(por_r, vo_rem, fin_sems.at[3])]

        @pl.when(i == 0)
        def _():
            sbuf[...] = jnp.zeros_like(sbuf)
            for cp in loc + rem + loads:
                cp.start()

        dhn = _mm_nt(dpa_r[...], w_r[:, 0:NCOL // 2]) + _mm_nt(dph_r[...], w_r[:, NCOL // 2:NCOL])
        xv = x_r[...]
        r = lax.rsqrt(jnp.mean(xv * xv, axis=-1, keepdims=True) + EPS)
        dxw = dhn * mw_r[...]
        gx_o[...] = dx2_r[...] + r * dxw - xv * ((r * r * r) * jnp.mean(dxw * xv, axis=-1, keepdims=True))
        sbuf[16:17, :] += jnp.sum(dhn * (xv * r), axis=0, keepdims=True)

        @pl.when(i == NT - 1)
        def _():
            sbuf[0:8, :] = s4_r[...]
            sbuf[8:16, :] = s6_r[...]
            sloc, srem = _small_copies(sbuf, sall_o, send_sems, recv_sems, loc_sems.at[1])
            for cp in sloc + srem:
                cp.start()
            for cp in rem:
                cp.wait_recv()
            for cp in rem:
                cp.wait_send()
            for cp in loc:
                cp.wait()
            mx, my, c = lax.axis_index("x"), lax.axis_index("y"), lax.axis_index("c")
            for cp in loads:
                cp.wait()
            sout[...] = ((vo_own[...] + vo_rem[0].astype(F32)) + vo_rem[1].astype(F32)) + vo_rem[2].astype(F32)
            sin[...] = ((v_own[...] + v_rem[0].astype(F32)) + v_rem[1].astype(F32)) + v_rem[2].astype(F32)
            swap = [pltpu.make_async_remote_copy(src_ref=sin, dst_ref=got_in, send_sem=share_send.at[0],
                                                 recv_sem=share_recv.at[0], device_id=(mx, my, 1 - c),
                                                 device_id_type=MESH),
                    pltpu.make_async_remote_copy(src_ref=sout, dst_ref=got_out, send_sem=share_send.at[1],
                                                 recv_sem=share_recv.at[1], device_id=(mx, my, 1 - c),
                                                 device_id_type=MESH)]
            for cp in swap:
                cp.start()
            mine = [pltpu.make_async_copy(sin, fin_o.at[c], fin_sems.at[0]),
                    pltpu.make_async_copy(sout, fout_o.at[c], fin_sems.at[1])]
            for cp in mine:
                cp.start()
            for cp in swap:
                cp.wait_recv()
            theirs = [pltpu.make_async_copy(got_in, fin_o.at[1 - c], fin_sems.at[2]),
                      pltpu.make_async_copy(got_out, fout_o.at[1 - c], fin_sems.at[3])]
            for cp in theirs:
                cp.start()
            for cp in swap:
                cp.wait_send()
            for cp in mine + theirs:
                cp.wait()
            for cp in srem:
                cp.wait_recv()
            for cp in srem:
                cp.wait_send()
            for cp in sloc:
                cp.wait()

    tok = lambda w: pl.BlockSpec((TT, w), lambda i: (i, 0))
    const = lambda shape: pl.BlockSpec(shape, lambda i: (0,) * len(shape))
    hbm = pl.BlockSpec(memory_space=pltpu.HBM)
    return pl.pallas_call(
        body, name="bwd_x", grid=(NT,),
        in_specs=[tok(NCOL // 2), tok(NCOL // 2), tok(D), tok(D), const((1, D)), const((D, NCOL)), hbm, hbm,
                  const((8, D)), const((8, D)), hbm, hbm],
        out_specs=[tok(D), hbm, hbm, hbm],
        out_shape=[jax.ShapeDtypeStruct((T, D), F32),
                   jax.ShapeDtypeStruct((8, 24, D), F32),
                   jax.ShapeDtypeStruct((2, 512, 1024), F32), jax.ShapeDtypeStruct((2, 128, D), F32)],
        scratch_shapes=[pltpu.VMEM((24, D), F32),
                        pltpu.VMEM((512, 1024), F32), pltpu.VMEM((3, 512, 1024), BF16),
                        pltpu.VMEM((128, D), F32), pltpu.VMEM((3, 128, D), BF16),
                        pltpu.VMEM((512, 1024), F32), pltpu.VMEM((128, D), F32),
                        pltpu.VMEM((512, 1024), F32), pltpu.VMEM((128, D), F32),
                        pltpu.SemaphoreType.DMA((10,)), pltpu.SemaphoreType.DMA((10,)), pltpu.SemaphoreType.DMA((2,)),
                        pltpu.SemaphoreType.DMA((2,)), pltpu.SemaphoreType.DMA((2,)), pltpu.SemaphoreType.DMA((4,))],
        compiler_params=_cp(("arbitrary",)),
    )(dproj_a, dproj_h, x, dx2, mixw, w_full, rin, rinb, small4, small6, pout_own, pout_rem)


def _grad_w_in(hn, dproj_a, dproj_h):
    TK = 2048
    NK = T // TK

    def body(hnt_r, dpa_r, dph_r, rin_o, rinb_o, acc, rbuf, obuf, obufb, send_sems, recv_sems, wb_sems):
        j = pl.program_id(0)
        kk = pl.program_id(1)
        x, y, c = lax.axis_index("x"), lax.axis_index("y"), lax.axis_index("c")
        mine = pl.ds(pl.multiple_of(c * 512, 512), 512)
        theirs = pl.ds(pl.multiple_of((1 - c) * 512, 512), 512)

        def send(jj):
            return pltpu.make_async_remote_copy(
                src_ref=acc.at[jj % 2, theirs, :], dst_ref=rbuf.at[jj], send_sem=send_sems.at[jj],
                recv_sem=recv_sems.at[jj], device_id=(x, y, 1 - c), device_id_type=MESH)

        def writeback(jj):
            cols = pl.ds(jj * 1024, 1024)
            return [pltpu.make_async_copy(obuf.at[jj % 2], rin_o.at[:, cols], wb_sems.at[jj % 2]),
                    pltpu.make_async_copy(obufb.at[jj % 2], rinb_o.at[:, cols], wb_sems.at[2 + jj % 2])]

        def wait_writeback(jj):
            for cp in writeback(jj):
                cp.wait()

        def finalize(jj):
            send(jj).wait_recv()
            red = acc[jj % 2, mine, :] + rbuf[jj]
            obuf[jj % 2] = red
            obufb[jj % 2] = red.astype(BF16)
            for cp in writeback(jj):
                cp.start()

        prod = _mm(hnt_r[...], jnp.where(j < 2, dpa_r[...], dph_r[...]))

        @pl.when(kk == 0)
        def _():
            for jj in (2, 3):
                @pl.when(j == jj)
                def _():
                    send(jj - 2).wait_send()
            acc[j % 2] = prod

        @pl.when(kk > 0)
        def _():
            acc[j % 2] += prod

        @pl.when(kk == NK - 1)
        def _():
            for jj in range(4):
                @pl.when(j == jj)
                def _():
                    send(jj).start()
                    if jj in (1, 2):
                        finalize(jj - 1)
                    if jj == 3:
                        wait_writeback(0)
                        finalize(2)
                        wait_writeback(1)
                        finalize(3)
                        wait_writeback(2)
                        wait_writeback(3)
                        send(2).wait_send()
                        send(3).wait_send()

    hbm = pl.BlockSpec(memory_space=pltpu.HBM)
    return pl.pallas_call(
        body, name="grad_w_in", grid=(4, NK),
        in_specs=[pl.BlockSpec((D, TK), lambda j, kk: (0, kk)),
                  pl.BlockSpec((TK, 1024), lambda j, kk: (jnp.where(j < 2, kk, NK - 1), jnp.minimum(j, 1))),
                  pl.BlockSpec((TK, 1024), lambda j, kk: (jnp.where(j < 2, 0, kk), jnp.maximum(j - 2, 0)))],
        out_specs=[hbm, hbm],
        out_shape=[jax.ShapeDtypeStruct((512, NCOL), F32), jax.ShapeDtypeStruct((512, NCOL), BF16)],
        scratch_shapes=[pltpu.VMEM((2, D, 1024), F32), pltpu.VMEM((4, 512, 1024), F32), pltpu.VMEM((2, 512, 1024), F32),
                        pltpu.VMEM((2, 512, 1024), BF16),
                        pltpu.SemaphoreType.DMA((4,)), pltpu.SemaphoreType.DMA((4,)), pltpu.SemaphoreType.DMA((4,))],
        compiler_params=_cp(("arbitrary", "arbitrary")),
    )(hn, dproj_a, dproj_h)


def _w_in_piece(ref, j):
    return ref.at[:, pl.ds(j * 1024, 1024)]


def _w_out_piece(ref, j):
    return ref.at[pl.ds(j * 128, 128), :]


def _chip_copies(piece, src_r, srcb_r, own_o, rem_o, send_sems, recv_sems, loc_sem):
    x, y, c = lax.axis_index("x"), lax.axis_index("y"), lax.axis_index("c")
    chips = [(1 - x, y), (x, 1 - y), (1 - x, 1 - y)]
    loc = [pltpu.make_async_copy(piece(src_r, 2 * x + y), own_o, loc_sem)]
    rem = [pltpu.make_async_remote_copy(
        src_ref=piece(srcb_r, 2 * px + py), dst_ref=rem_o.at[k], send_sem=send_sems.at[k],
        recv_sem=recv_sems.at[k], device_id=(px, py, c), device_id_type=MESH) for k, (px, py) in enumerate(chips)]
    return loc, rem


def _small_copies(small_r, sall_o, send_sems, recv_sems, loc_sem):
    x, y, c = lax.axis_index("x"), lax.axis_index("y"), lax.axis_index("c")
    me = 4 * x + 2 * y + c
    loc = [pltpu.make_async_copy(small_r, sall_o.at[me], loc_sem)]
    rem = []
    k = 3
    for fx in range(2):
        for fy in range(2):
            for fc in range(2):
                if fx or fy or fc:
                    peer = (1 - x if fx else x, 1 - y if fy else y, 1 - c if fc else c)
                    rem.append(pltpu.make_async_remote_copy(
                        src_ref=small_r, dst_ref=sall_o.at[me], send_sem=send_sems.at[k],
                        recv_sem=recv_sems.at[k], device_id=peer, device_id_type=MESH))
                    k += 1
    return loc, rem


def _adamw_math(w, g, m, v):
    m = B1 * m + (1.0 - B1) * g
    v = B2 * v + (1.0 - B2) * (g * g)
    m_hat = m / (1.0 - B1 ** STEP)
    v_hat = v / (1.0 - B2 ** STEP)
    delta = -LR * (m_hat / (jnp.sqrt(v_hat) + AEPS) + WD * w)
    return delta, m, v


def _adamw(big_in, big_out, sall, params):
    def body(*refs):
        wi, gi, mi, vi, wo, go, mo, vo, sall_r = refs[:9]
        ins = refs[9:24]
        di_o, mi_o, vi_o, do_o, mo_o, vo_o = refs[24:30]
        outs = refs[30:]
        d, mm, vv = _adamw_math(wi[...], gi[...], mi[...], vi[...])
        di_o[...] = d
        mi_o[...] = mm
        vi_o[...] = vv

        @pl.when(pl.program_id(0) == 0)
        def _():
            d, mm, vv = _adamw_math(wo[...], go[...], mo[...], vo[...])
            do_o[...] = d
            mo_o[...] = mm
            vo_o[...] = vv
            tot = sall_r[0]
            for dv in range(1, 8):
                tot = tot + sall_r[dv]
            grads = [tot[16:17, :], tot[1:2, 0:AW], tot[1:2, AW:], tot[8:10, 0:HW], tot[0:1, :]]
            outs[0][...] = tot[2:3, 0:1]
            for p in range(5):
                w_r, m_r, v_r = ins[3 * p:3 * p + 3]
                g = grads[p]
                d, mm, vv = _adamw_math(w_r[...], g, m_r[...], v_r[...])
                outs[1 + 4 * p][...] = g
                outs[2 + 4 * p][...] = d
                outs[3 + 4 * p][...] = mm
                outs[4 + 4 * p][...] = vv

    flat = [a for p in params for a in p]
    shapes = [jax.ShapeDtypeStruct((D, 1024), F32)] * 3 + [jax.ShapeDtypeStruct((256, D), F32)] * 3
    shapes += [jax.ShapeDtypeStruct((1, 1), F32)]
    for p in params:
        shapes += [jax.ShapeDtypeStruct(p[0].shape, F32)] * 4
    vm = pl.BlockSpec(memory_space=pltpu.VMEM)
    rows = pl.BlockSpec((512, 1024), lambda i: (i, 0))
    whole = pl.BlockSpec((256, D), lambda i: (0, 0))
    return pl.pallas_call(
        body, name="adamw", grid=(2,),
        in_specs=[rows] * 4 + [whole] * 4 + [vm] * 16, out_specs=[rows] * 3 + [whole] * 3 + [vm] * 21,
        out_shape=shapes,
        compiler_params=_cp(("arbitrary",)),
    )(*big_in, *big_out, sall, *flat)


def kernel(x, positions, w_in, w_out, mix_norm_w, attn_out_norm_w, hgrn_out_norm_w, hgrn_lb_raw, final_norm_w, loss_target, m_w_in, m_w_out, m_mix_norm_w, m_attn_out_norm_w, m_hgrn_out_norm_w, m_hgrn_lb_raw, m_final_norm_w, v_w_in, v_w_out, v_mix_norm_w, v_attn_out_norm_w, v_hgrn_out_norm_w, v_hgrn_lb_raw, v_final_norm_w):
    xs = x.reshape(T, D)
    tgt = loss_target.reshape(T, D)
    pos = positions.reshape(1, T)
    fnw = final_norm_w.reshape(1, D)

    ti = np.arange(TH)
    tri_np = ((ti[:, None] // CHUNK == ti[None, :] // CHUNK) & (ti[None, :] <= ti[:, None])).astype(np.float32)
    tri = jnp.asarray(tri_np, BF16)
    trit = jnp.asarray(tri_np.T, BF16)
    hi_ = np.arange(AW) // HEAD
    gmat = jnp.asarray((hi_[:256, None] == hi_[None, :256]).astype(np.float32) / HEAD, BF16)
    emat_np = (np.arange(128)[:, None] == hi_[None, :]).astype(np.float32)
    sel_np = (8 + hi_[:, None] == np.arange(128)[None, :]).astype(np.float32)
    emat = jnp.asarray(emat_np, BF16)
    selmat = jnp.asarray(sel_np, BF16)

    jm_arr = (2 * lax.axis_index("x") + lax.axis_index("y")).astype(jnp.int32).reshape(1)
    (hn, q1, k1, v1, q4, k4, v4, q16, k16, v16, ag, hq, hf, hi, hg, w_full, wout4) = _fwd_in(
        xs, pos, mix_norm_w, w_in.reshape(D, 1024), w_out.reshape(256, D), jm_arr)
    wout_full = wout4.reshape(D, D)
    flat = lambda a: a.reshape(T, AW)
    o1, l1 = _attn_fwd(q1, k1, v1, T // BLK, "attn_fwd_d1")
    o4, l4 = _attn_fwd(flat(q4), flat(k4), flat(v4), T // 4 // BLK, "attn_fwd_d4")
    o16, l16 = _attn_fwd(flat(q16), flat(k16), flat(v16), T // 16 // BLK, "attn_fwd_d16")
    rec, sall = _hgrn_fwd(hq, hf, hi, hgrn_lb_raw, tri)

    (dx2, do1, do4, do16, st1, st4, st16, drec, dag, dhg, rout, routb, small4) = _fwd_out(
        o1, o4.reshape(4, T // 4, AW), o16.reshape(16, T // 16, AW),
        l1, l4.reshape(4, T // 4, 128), l16.reshape(16, T // 16, 128),
        rec, ag, hg, xs, tgt, attn_out_norm_w, hgrn_out_norm_w, fnw, wout_full, gmat, emat, selmat)

    fst = lambda a: a.reshape(T, 128)
    dq1, dk1, dv1 = _attn_bwd(q1, k1, v1, do1, st1, T // BLK, "attn_bwd_d1")
    dq4, dk4, dv4 = _attn_bwd(flat(q4), flat(k4), flat(v4), flat(do4), fst(st4), T // 4 // BLK, "attn_bwd_d4")
    dq16, dk16, dv16 = _attn_bwd(flat(q16), flat(k16), flat(v16), flat(do16), fst(st16), T // 16 // BLK,
                                 "attn_bwd_d16")
    dproj_h, small6, pout_own, pout_rem = _hgrn_bwd(hq, hf, hi, hgrn_lb_raw, tri, trit, drec, sall, dhg,
                                                    rout, routb)

    r4 = lambda a: a.reshape(4, T // 4, AW)
    r16 = lambda a: a.reshape(16, T // 16, AW)
    dproj_a = _dproj_build((dq1, r4(dq4), r16(dq16)), (dk1, r4(dk4), r16(dk16)), (dv1, r4(dv4), r16(dv16)),
                           dag, pos)
    rin, rinb = _grad_w_in(hn, dproj_a, dproj_h)
    gx, small_all, fin, fout = _bwd_x(dproj_a, dproj_h, xs, dx2, mix_norm_w, w_full, rin, rinb,
                                            small4, small6, pout_own, pout_rem)
    g_w_in = fin.reshape(D, 1024)
    g_w_out = fout.reshape(256, D)

    params = [(mix_norm_w, m_mix_norm_w, v_mix_norm_w),
              (attn_out_norm_w, m_attn_out_norm_w, v_attn_out_norm_w),
              (hgrn_out_norm_w, m_hgrn_out_norm_w, v_hgrn_out_norm_w),
              (hgrn_lb_raw, m_hgrn_lb_raw, v_hgrn_lb_raw),
              (fnw, m_final_norm_w.reshape(1, D), v_final_norm_w.reshape(1, D))]
    d_in, nm_in, nv_in, d_out, nm_out, nv_out, *so = _adamw(
        (w_in.reshape(D, 1024), g_w_in, m_w_in.reshape(D, 1024), v_w_in.reshape(D, 1024)),
        (w_out.reshape(256, D), g_w_out, m_w_out.reshape(256, D), v_w_out.reshape(256, D)), small_all, params)
    loss = so[0].reshape(())
    g_s = [so[1 + 4 * p] for p in range(5)]
    d_s = [so[2 + 4 * p] for p in range(5)]
    m_s = [so[3 + 4 * p] for p in range(5)]
    v_s = [so[4 + 4 * p] for p in range(5)]
    for lst in (g_s, d_s, m_s, v_s):
        lst[4] = lst[4].reshape(D)

    return (loss, gx.reshape(1, T, D),
            g_w_in.reshape(1, D, 1024), g_w_out.reshape(1, 256, D), *g_s,
            d_in.reshape(1, D, 1024), d_out.reshape(1, 256, D), *d_s,
            nm_in.reshape(1, D, 1024), nm_out.reshape(1, 256, D), *m_s,
            nv_in.reshape(1, D, 1024), nv_out.reshape(1, 256, D), *v_s)
```

```python
import functools

import numpy as np
import jax
import jax.numpy as jnp
from jax import lax
from jax.experimental import pallas as pl
from jax.experimental.pallas import tpu as pltpu

F32 = jnp.float32
BF16 = jnp.bfloat16

T = 4096
D = 1024
AW = 512
HW = 512
NCOL = 4096
HEAD = 64
BLK = 128
CHUNK = 64
EPS = 1e-6
SCALE = HEAD ** -0.5
NEG = -1e30
ROPE_THETA = 500000.0
INV_FREQ = [float(v) for v in
            (np.float32(ROPE_THETA) ** (-(np.arange(8, dtype=np.float32)) * np.float32(0.125)))]
LR, B1, B2, AEPS, WD, STEP = 0.001, 0.9, 0.999, 1e-08, 0.01, 10
VMEM_LIMIT = 63 * 1024 * 1024
MESH = pl.DeviceIdType.MESH


def _cp(sem=None, **kw):
    return pltpu.CompilerParams(dimension_semantics=sem, vmem_limit_bytes=VMEM_LIMIT, **kw)


def _mm(a, b):
    return jnp.dot(a, b, preferred_element_type=F32)


def _mm_nt(a, b):
    return lax.dot_general(a, b, (((1,), (1,)), ((), ())), preferred_element_type=F32)


def _mm_tn(a, b):
    return lax.dot_general(a, b, (((0,), (0,)), ((), ())), preferred_element_type=F32)


def _mm_exact_l(mat_bf, x):
    h = x.astype(BF16)
    l = (x - h.astype(F32)).astype(BF16)
    return _mm(mat_bf, h) + _mm(mat_bf, l)


def _mm_exact_r(x, mat_bf):
    h = x.astype(BF16)
    l = (x - h.astype(F32)).astype(BF16)
    return _mm(h, mat_bf) + _mm(l, mat_bf)


def _sigmoid(x):
    return 0.5 * jnp.tanh(0.5 * x) + 0.5


def _rope_tables(pos):
    lane = lax.broadcasted_iota(jnp.int32, (1, 128), 1)
    jl = lane & 63
    fi = jl & 7
    inv = jnp.zeros((1, 128), F32)
    for kk in range(8):
        inv = jnp.where(fi == kk, INV_FREQ[kk], inv)
    ang = jnp.broadcast_to(pos.astype(F32), (128, pos.shape[1])).T * inv
    c = jnp.cos(ang)
    s = jnp.sin(ang)
    cosf = jnp.where(jl < 16, c, 1.0)
    s1 = jnp.where(jl < 8, -s, 0.0)
    s2 = jnp.where((jl >= 8) & (jl < 16), s, 0.0)
    return cosf, s1, s2


def _rope(t, cosf, s1, s2):
    parts = []
    for ci in range(t.shape[1] // 128):
        tc = t[:, ci * 128:(ci + 1) * 128]
        parts.append(tc * cosf + pltpu.roll(tc, 120, 1) * s1 + pltpu.roll(tc, 8, 1) * s2)
    return jnp.concatenate(parts, axis=1)


def _rope_bwd(g, cosf, s1, s2):
    parts = []
    for ci in range(g.shape[1] // 128):
        gc = g[:, ci * 128:(ci + 1) * 128]
        parts.append(gc * cosf + pltpu.roll(gc * s1, 8, 1) + pltpu.roll(gc * s2, 120, 1))
    return jnp.concatenate(parts, axis=1)


def _perm_store(val, scr, scr2, o1, o4, o16, dt):
    n = val.shape[0]
    q = n // 4
    o1[...] = val.astype(dt)
    for ci in range(val.shape[1] // 128):
        cs = slice(ci * 128, (ci + 1) * 128)
        scr[ci] = val[:, cs]
        for r4 in range(4):
            part = scr[ci, pl.ds(r4, q, stride=4), :]
            o4[r4, :, cs] = part.astype(dt)
            scr2[ci, r4 * q:(r4 + 1) * q, :] = part
        for r4 in range(4):
            for b in range(4):
                o16[r4 + 4 * b, :, cs] = scr2[ci, pl.ds(r4 * q + b, q // 4, stride=4), :].astype(dt)


def _unperm_load(r4, r16, scr_a, scr_b, scr_c):
    n = scr_a.shape[1]
    q = n // 4
    nc = r4.shape[-1] // 128
    for ci in range(nc):
        cs = slice(ci * 128, (ci + 1) * 128)
        for rr in range(4):
            scr_a[ci, pl.ds(rr, q, stride=4), :] = r4[rr, :, cs].astype(F32)
        for rr in range(4):
            for b in range(4):
                scr_c[ci, pl.ds(rr * q + b, q // 4, stride=4), :] = r16[rr + 4 * b, :, cs].astype(F32)
        for rr in range(4):
            scr_b[ci, pl.ds(rr, q, stride=4), :] = scr_c[ci, rr * q:(rr + 1) * q, :]
    return (jnp.concatenate([scr_a[ci] for ci in range(nc)], axis=1),
            jnp.concatenate([scr_b[ci] for ci in range(nc)], axis=1))


def _unperm_sum(r4, r16, scr_b, scr_c):
    n = scr_b.shape[1]
    q = n // 4
    nc = r4.shape[-1] // 128
    for ci in range(nc):
        cs = slice(ci * 128, (ci + 1) * 128)
        for rr in range(4):
            for b in range(4):
                scr_c[ci, pl.ds(rr * q + b, q // 4, stride=4), :] = r16[rr + 4 * b, :, cs].astype(F32)
        for rr in range(4):
            scr_b[ci, pl.ds(rr, q, stride=4), :] = scr_c[ci, rr * q:(rr + 1) * q, :] + r4[rr, :, cs].astype(F32)
    return jnp.concatenate([scr_b[ci] for ci in range(nc)], axis=1)


def _fwd_in(x, pos, mixw, w_in, w_out, jm_arr):
    TT = 512
    NT = T // TT

    def body(jm_ref, x_ref, pos_ref, mw_ref, win_ref, wout_ref,
             hnt_ref, q1, k1, v1, q4, k4, v4, q16, k16, v16, ag, hq, hf, hi, hg, wfull_o, woutfull_o,
             wbuf, wobuf, hn_all, scr, scr2, stage, send_sems, recv_sems, loc_sems):
        s = pl.program_id(0)
        i = pl.program_id(1)
        mx, my, c = lax.axis_index("x"), lax.axis_index("y"), lax.axis_index("c")
        me, sibling = (mx, my, c), (mx, my, 1 - c)
        chips = [(mx, 1 - my), (1 - mx, my), (1 - mx, 1 - my)]
        jm = 2 * mx + my
        rows_in = [pl.ds(pl.multiple_of(h * 512, 512), 512) for h in (c, 1 - c)]
        rows_out = [pl.ds(pl.multiple_of(h * 128, 128), 128) for h in (c, 1 - c)]

        def blk(k):
            return lax.bitwise_xor(jm, k + 1)

        def rc(n, ref, to):
            return pltpu.make_async_remote_copy(src_ref=ref, dst_ref=ref, send_sem=send_sems.at[n],
                                                recv_sem=recv_sems.at[n], device_id=to, device_id_type=MESH)

        halves = [pl.ds(0, 512), pl.ds(512, 512)]
        send_in = lambda k, h: rc(12 + 2 * k + h, wbuf.at[jm, rows_in[0], halves[h]], (*chips[k], c))
        got_in = lambda k, h: rc(12 + 2 * k + h, wbuf.at[blk(k), rows_in[0], halves[h]], me)
        relay = lambda h: rc(16 + h, wbuf.at[blk(h), rows_in[0], halves[h]], (*chips[1 - h], c))
        got_relay = lambda h: rc(16 + h, wbuf.at[blk(2), rows_in[0], halves[h]], me)
        send_out = lambda k: rc(3 + k, wobuf.at[jm, rows_out[0], :], (*chips[k], c))
        got_out = lambda k: rc(3 + k, wobuf.at[blk(k), rows_out[0], :], me)
        pass_in = lambda k: rc(6 + k, wbuf.at[blk(k), rows_in[0], :], sibling)
        pass_out = lambda k: rc(9 + k, wobuf.at[blk(k), rows_out[0], :], sibling)
        passed_in = lambda k: rc(6 + k, wbuf.at[blk(k), rows_in[1], :], me)
        passed_out = lambda k: rc(9 + k, wobuf.at[blk(k), rows_out[1], :], me)

        def keep(j, n):
            return pltpu.make_async_copy(wbuf.at[j], wfull_o.at[:, pl.ds(j * 1024, 1024)], loc_sems.at[n])

        @pl.when((s == 0) & (i == 0))
        def _():
            chunk = [pl.ds(pl.multiple_of(lax.rem(p + 2 * c, 4) * 256, 256), 256) for p in range(4)]
            loads = [pltpu.make_async_copy(win_ref.at[chunk[p], :] if p < 4 else wout_ref, stage.at[p % 2],
                                           loc_sems.at[4 + p % 2]) for p in range(5)]
            loads[0].start()
            for p in range(5):
                if p < 4:
                    loads[p + 1].start()
                loads[p].wait()
                if p < 4:
                    wbuf[jm, chunk[p], :] = stage[p % 2].astype(BF16)
                else:
                    wobuf[jm] = stage[p % 2].astype(BF16)
                if p == 1:
                    for k in range(2):
                        for h in range(2):
                            send_in(k, h).start()
            keep(jm, 0).start()

        @pl.when((s == 0) & (i == NT - 1))
        def _():
            for kk in range(2):
                for h in range(2):
                    got_in(kk, h).wait_recv()
            relay(0).start()
            relay(1).start()
            pass_in(0).start()
            pass_in(1).start()
            passed_in(0).wait_recv()
            keep(blk(0), 1).start()

        @pl.when((s == 1) & (i == NT - 1))
        def _():
            got_relay(0).wait_recv()
            got_relay(1).wait_recv()
            pass_in(2).start()

        @pl.when((s == 2) & (i == 0))
        def _():
            for k in (1, 2):
                passed_in(k).wait_recv()
                keep(blk(k), k + 1).start()
            for kk in range(3):
                send_out(kk).start()

        @pl.when((s == 2) & (i == NT - 2))
        def _():
            for k in range(3):
                got_out(k).wait_recv()
                pass_out(k).start()

        whole_out = pltpu.make_async_copy(wobuf, woutfull_o, loc_sems.at[4])

        @pl.when((s == 2) & (i == NT - 1))
        def _():
            for k in range(3):
                passed_out(k).wait_recv()
            whole_out.start()

        tile = pl.ds(pl.multiple_of(i * TT, TT), TT)

        @pl.when(s == 0)
        def _():
            xv = x_ref[...]
            r = lax.rsqrt(jnp.mean(xv * xv, axis=-1, keepdims=True) + EPS)
            hnf = (xv * r) * mw_ref[...]
            hn_all[tile, :] = hnf.astype(BF16)
            hnt_ref[...] = hnf.T.astype(BF16)

        def project(jj):
            hn = hn_all[tile, :]
            lo = _mm(hn, wbuf[jj, :, 0:512])
            hi_cols = _mm(hn, wbuf[jj, :, 512:1024])
            if jj == 0:
                cosf, s1, s2 = _rope_tables(pos_ref[...])
                _perm_store(_rope(lo, cosf, s1, s2) * SCALE, scr, scr2, q1, q4, q16, BF16)
                _perm_store(_rope(hi_cols, cosf, s1, s2), scr, scr2, k1, k4, k16, BF16)
            elif jj == 1:
                _perm_store(lo, scr, scr2, v1, v4, v16, BF16)
                ag[...] = hi_cols.astype(BF16)
            elif jj == 2:
                hq[...] = lo.astype(BF16)
                hf[...] = hi_cols.astype(BF16)
            else:
                hi[...] = lo.astype(BF16)
                hg[...] = hi_cols.astype(BF16)

        def project_block(j):
            for jj in range(4):
                pl.when(j == jj)(functools.partial(project, jj))

        @pl.when(s < 2)
        def _():
            project_block(lax.bitwise_xor(jm, s))

        @pl.when(s == 2)
        def _():
            project_block(lax.bitwise_xor(jm, 2))
            project_block(lax.bitwise_xor(jm, 3))

        @pl.when((s == 2) & (i == NT - 1))
        def _():
            for h in range(2):
                relay(h).wait_send()
                for k in range(2):
                    send_in(k, h).wait_send()
            for k in range(3):
                send_out(k).wait_send()
                pass_in(k).wait_send()
                pass_out(k).wait_send()
            keep(jm, 0).wait()
            for k in range(3):
                keep(blk(k), k + 1).wait()
            whole_out.wait()

    def at_stage_of(jb):
        def index(s, i, jm_ref):
            sa = jnp.minimum(lax.bitwise_xor(jm_ref[0], jb), 2)
            return jnp.where(s < sa, 0, jnp.where(s == sa, i, NT - 1))
        return index

    tok = lambda w, jb: pl.BlockSpec((TT, w), lambda s, i, jm_ref: (at_stage_of(jb)(s, i, jm_ref), 0))
    d4 = lambda jb: pl.BlockSpec((4, TT // 4, AW), lambda s, i, jm_ref: (0, at_stage_of(jb)(s, i, jm_ref), 0))
    d16 = lambda jb: pl.BlockSpec((16, TT // 16, AW), lambda s, i, jm_ref: (0, at_stage_of(jb)(s, i, jm_ref), 0))
    hbm = pl.BlockSpec(memory_space=pltpu.HBM)
    sd = lambda shape, dt: jax.ShapeDtypeStruct(shape, dt)
    in_own_stage = lambda s, i: jnp.where(s == 0, i, NT - 1)
    grid_spec = pltpu.PrefetchScalarGridSpec(
        num_scalar_prefetch=1, grid=(3, NT),
        in_specs=[pl.BlockSpec((TT, D), lambda s, i, jm_ref: (in_own_stage(s, i), 0)),
                  pl.BlockSpec((1, TT), lambda s, i, jm_ref: (0, i)),
                  pl.BlockSpec((1, D), lambda s, i, jm_ref: (0, 0)), hbm, hbm],
        out_specs=[pl.BlockSpec((D, TT), lambda s, i, jm_ref: (0, in_own_stage(s, i))),
                   tok(AW, 0), tok(AW, 0), tok(AW, 1), d4(0), d4(0), d4(1), d16(0), d16(0), d16(1),
                   tok(AW, 1), tok(AW, 2), tok(AW, 2), tok(AW, 3), tok(AW, 3), hbm, hbm],
        scratch_shapes=[pltpu.VMEM((4, D, 1024), BF16), pltpu.VMEM((4, 256, D), BF16), pltpu.VMEM((T, D), BF16),
                        pltpu.VMEM((4, TT, 128), F32), pltpu.VMEM((4, TT, 128), F32), pltpu.VMEM((2, 256, 1024), F32),
                        pltpu.SemaphoreType.DMA((18,)),
                        pltpu.SemaphoreType.DMA((18,)), pltpu.SemaphoreType.DMA((6,))])
    return pl.pallas_call(
        body, name="fwd_in", grid_spec=grid_spec,
        out_shape=[sd((D, T), BF16)] + [sd((T, AW), BF16)] * 3 + [sd((4, T // 4, AW), BF16)] * 3
        + [sd((16, T // 16, AW), BF16)] * 3
        + [sd((T, AW), BF16)] * 5 + [sd((D, NCOL), BF16), sd((4, 256, D), BF16)],
        compiler_params=_cp(("arbitrary", "arbitrary")),
    )(jm_arr, x, pos, mixw, w_in, w_out)


def _band_mask(key_axis, nkeys=2 * BLK):
    shape = (nkeys, 2 * BLK) if key_axis == 0 else (2 * BLK, nkeys)
    kj = lax.broadcasted_iota(jnp.int32, shape, key_axis)
    qi = lax.broadcasted_iota(jnp.int32, shape, 1 - key_axis) & (BLK - 1)
    return (kj >= qi) & (kj <= qi + BLK), kj, qi


def _stack_heads(t2, in_a):
    z = jnp.zeros_like(t2)
    return jnp.concatenate([jnp.where(in_a[0], t2, z), jnp.where(in_a[1], t2, z)], axis=0)


def _attn_fwd(q, k, v, nb, name):
    n = 8
    CH = n * BLK
    halo = nb > n

    def body(*refs):
        if halo:
            q_ref, k_ref, v_ref, kp_ref, vp_ref, o_ref, lse_ref = refs
        else:
            q_ref, k_ref, v_ref, o_ref, lse_ref = refs
        lane = lax.broadcasted_iota(jnp.int32, (1, 128), 1)
        in_a = [lane < HEAD, lane >= HEAD]
        band, kj, _ = _band_mask(1)
        thr0 = jnp.where((n * pl.program_id(0)) % nb == 0, BLK, 0) if halo else BLK
        mask0 = band & (kj >= thr0)
        mask_first = band & (kj >= BLK)
        for b in range(n):
            rs = slice(b * BLK, (b + 1) * BLK)
            stat = jnp.zeros((BLK, 128), F32)
            for hp in range(4):
                cs = slice(hp * 128, (hp + 1) * 128)
                q2s = _stack_heads(q_ref[rs, cs], in_a)
                if b == 0:
                    kprev = kp_ref[:, cs] if halo else k_ref[rs, cs]
                    vprev = vp_ref[:, cs] if halo else v_ref[rs, cs]
                    kk = jnp.concatenate([kprev, k_ref[rs, cs]], axis=0)
                    vv = jnp.concatenate([vprev, v_ref[rs, cs]], axis=0)
                    mask = mask0
                else:
                    kk = k_ref[(b - 1) * BLK:(b + 1) * BLK, cs]
                    vv = v_ref[(b - 1) * BLK:(b + 1) * BLK, cs]
                    mask = mask_first if b % nb == 0 else band
                s = jnp.where(mask, _mm_nt(q2s, kk), NEG)
                m = jnp.max(s, axis=-1, keepdims=True)
                p = jnp.exp(s - m)
                l = jnp.sum(p, axis=-1, keepdims=True)
                o = _mm(p.astype(BF16), vv) / l
                lse = m + jnp.log(l)
                o_ref[rs, cs] = jnp.where(in_a[0], o[:BLK], o[BLK:]).astype(BF16)
                stat = jnp.where(lane == 2 * hp, lse[:BLK], stat)
                stat = jnp.where(lane == 2 * hp + 1, lse[BLK:], stat)
            lse_ref[rs, :] = stat

    cur = pl.BlockSpec((CH, AW), lambda i: (i, 0))
    prev = pl.BlockSpec((BLK, AW), lambda i: (jnp.maximum(n * i - 1, 0), 0))
    return pl.pallas_call(
        body, name=name, grid=(T // CH,),
        in_specs=[cur, cur, cur] + ([prev, prev] if halo else []),
        out_specs=[cur, pl.BlockSpec((CH, 128), lambda i: (i, 0))],
        out_shape=[jax.ShapeDtypeStruct((T, AW), BF16), jax.ShapeDtypeStruct((T, 128), F32)],
        compiler_params=_cp(("parallel",)),
    )(*((q, k, v) + ((k, v) if halo else ())))


def _attn_bwd(q, k, v, do, st, nb, name):
    n = 8
    CH = n * BLK
    NBLK = T // BLK
    halo = nb > n

    def body(*refs):
        if halo:
            (q_ref, k_ref, v_ref, do_ref, st_ref, kp_ref, vp_ref, qn_ref, don_ref, stn_ref,
             dq_ref, dk_ref, dv_ref) = refs
        else:
            q_ref, k_ref, v_ref, do_ref, st_ref, dq_ref, dk_ref, dv_ref = refs
        i = pl.program_id(0)
        lane = lax.broadcasted_iota(jnp.int32, (1, 128), 1)
        in_a = [lane < HEAD, lane >= HEAD]
        band, kj, _ = _band_mask(0)
        thr0 = jnp.where((n * i) % nb == 0, BLK, 0) if halo else BLK
        mask0 = band & (kj >= thr0)
        mask_first = band & (kj >= BLK)

        def stat_rows(st_t, hp):
            lse_r = jnp.concatenate([st_t[2 * hp:2 * hp + 1, :], st_t[2 * hp + 1:2 * hp + 2, :]], axis=1)
            dl_r = jnp.concatenate([st_t[8 + 2 * hp:9 + 2 * hp, :], st_t[9 + 2 * hp:10 + 2 * hp, :]], axis=1)
            return lse_r, dl_r

        st_t = [st_ref[b * BLK:(b + 1) * BLK, :].T for b in range(n)]
        if halo:
            nxt_thr = jnp.where((n * i + n) % nb == 0, 2 * BLK, 0)
            _, kj1, qi1 = _band_mask(0, BLK)
            mask_next = kj1 >= qi1 + nxt_thr
            stn_t = stn_ref[...].T

        for hp in range(4):
            cs = slice(hp * 128, (hp + 1) * 128)
            kb = [k_ref[b * BLK:(b + 1) * BLK, cs] for b in range(n)]
            vb = [v_ref[b * BLK:(b + 1) * BLK, cs] for b in range(n)]
            dk_acc = [jnp.zeros((BLK, 128), F32) for _ in range(n)]
            dv_acc = [jnp.zeros((BLK, 128), F32) for _ in range(n)]
            for b in range(n):
                rs = slice(b * BLK, (b + 1) * BLK)
                q2s = _stack_heads(q_ref[rs, cs], in_a)
                do2s = _stack_heads(do_ref[rs, cs], in_a)
                if b == 0:
                    kprev = kp_ref[:, cs] if halo else kb[0]
                    vprev = vp_ref[:, cs] if halo else vb[0]
                    mask = mask0
                else:
                    kprev, vprev, mask = kb[b - 1], vb[b - 1], (mask_first if b % nb == 0 else band)
                kk = jnp.concatenate([kprev, kb[b]], axis=0)
                vv = jnp.concatenate([vprev, vb[b]], axis=0)
                lse_r, dl_r = stat_rows(st_t[b], hp)
                s_t = jnp.where(mask, _mm_nt(kk, q2s), NEG)
                p_t = jnp.exp(s_t - lse_r)
                ds_t = (p_t * (_mm_nt(vv, do2s) - dl_r)).astype(BF16)
                dkk = _mm(ds_t, q2s)
                dvv = _mm(p_t.astype(BF16), do2s)
                dqs = _mm_tn(ds_t, kk) * SCALE
                dq_ref[rs, cs] = jnp.where(in_a[0], dqs[:BLK], dqs[BLK:]).astype(BF16)
                dk_acc[b] += dkk[BLK:]
                dv_acc[b] += dvv[BLK:]
                if b > 0:
                    dk_acc[b - 1] += dkk[:BLK]
                    dv_acc[b - 1] += dvv[:BLK]
            if halo:
                q2s = _stack_heads(qn_ref[:, cs], in_a)
                do2s = _stack_heads(don_ref[:, cs], in_a)
                lse_r, dl_r = stat_rows(stn_t, hp)
                s_t = jnp.where(mask_next, _mm_nt(kb[n - 1], q2s), NEG)
                p_t = jnp.exp(s_t - lse_r)
                ds_t = (p_t * (_mm_nt(vb[n - 1], do2s) - dl_r)).astype(BF16)
                dk_acc[n - 1] += _mm(ds_t, q2s)
                dv_acc[n - 1] += _mm(p_t.astype(BF16), do2s)
            for b in range(n):
                dk_ref[b * BLK:(b + 1) * BLK, cs] = dk_acc[b].astype(BF16)
                dv_ref[b * BLK:(b + 1) * BLK, cs] = dv_acc[b].astype(BF16)

    cur = pl.BlockSpec((CH, AW), lambda i: (i, 0))
    cur_st = pl.BlockSpec((CH, 128), lambda i: (i, 0))
    prev = pl.BlockSpec((BLK, AW), lambda i: (jnp.maximum(n * i - 1, 0), 0))
    nxt = pl.BlockSpec((BLK, AW), lambda i: (jnp.minimum(n * i + n, NBLK - 1), 0))
    nxt_st = pl.BlockSpec((BLK, 128), lambda i: (jnp.minimum(n * i + n, NBLK - 1), 0))
    ins = [cur] * 4 + [cur_st] + ([prev, prev, nxt, nxt, nxt_st] if halo else [])
    args = (q, k, v, do, st) + ((k, v, q, do, st) if halo else ())
    return pl.pallas_call(
        body, name=name, grid=(T // CH,),
        in_specs=ins,
        out_specs=[cur] * 3,
        out_shape=[jax.ShapeDtypeStruct((T, AW), BF16)] * 3,
        compiler_params=_cp(("parallel",)),
    )(*args)


TH = 256
NCH = TH // CHUNK


def _hgrn_common(hq_ref, hf_ref, lbr_ref, tri_ref):
    r0 = lbr_ref[0:1, :]
    r1 = lbr_ref[1:2, :]
    mx = jnp.maximum(r0, r1)
    e0 = jnp.exp(r0 - mx)
    e1 = jnp.exp(r1 - mx)
    lb = e0 / (e0 + e1)
    hqv = hq_ref[...].astype(F32)
    sq = _sigmoid(hqv)
    qv = hqv * sq
    sf = _sigmoid(hf_ref[...].astype(F32))
    f = lb + (1.0 - lb) * sf
    kv = 1.0 - f
    g = jnp.log(f)
    cum = _mm_exact_l(tri_ref[...], g)
    dec = jnp.exp(jnp.concatenate([cum[c * CHUNK + CHUNK - 1:(c + 1) * CHUNK, :] for c in range(NCH)], axis=0))
    decb = jnp.concatenate([jnp.broadcast_to(dec[c:c + 1, :], (CHUNK, HW)) for c in range(NCH)], axis=0)
    ea = jnp.exp(cum)
    ena = jnp.exp(-cum)
    eend = decb * ena
    return dict(lb=lb, hq=hqv, sq=sq, q=qv, sf=sf, f=f, k=kv, cum=cum, ea=ea, ena=ena, eend=eend,
                qd=qv * ea, ki=kv * ena, ke=kv * eend, dec=dec)


def _tri_mask(transposed=False):
    ti = lax.broadcasted_iota(jnp.int32, (TH, TH), 1 if transposed else 0)
    si = lax.broadcasted_iota(jnp.int32, (TH, TH), 0 if transposed else 1)
    return (si <= ti) & ((si // CHUNK) == (ti // CHUNK))


def _hgrn_fwd(hq, hf, hi, lbr, tri):
    NSUB = 2

    def body(hq_ref, hf_ref, hi_ref, lbr_ref, tri_ref, rec_ref, sall_ref, st_scr):
        @pl.when(pl.program_id(0) == 0)
        def _():
            st_scr[...] = jnp.zeros_like(st_scr)

        causal = _tri_mask()
        for u in range(NSUB):
            tile = slice(u * TH, (u + 1) * TH)
            w = _hgrn_common(hq_ref.at[tile, :], hf_ref.at[tile, :], lbr_ref, tri_ref)
            qd, ki, ke = w["qd"].astype(BF16), w["ki"].astype(BF16), w["ke"].astype(BF16)
            dec = w["dec"]
            vb = hi_ref[tile, :]
            for h in range(4):
                cs = slice(h * 128, (h + 1) * 128)
                att = jnp.where(causal, _mm_nt(qd[:, cs], ki[:, cs]), 0.0)
                o_intra = _mm(att.astype(BF16), vb[:, cs])
                st = st_scr[:, cs]
                for c in range(NCH):
                    rs = slice(c * CHUNK, (c + 1) * CHUNK)
                    sall_ref[u * NCH + c, :, cs] = st
                    rec_ref[u * TH + c * CHUNK:u * TH + (c + 1) * CHUNK, cs] = (
                        o_intra[rs] + _mm_nt(qd[rs, cs], st.astype(BF16))).astype(BF16)
                    st = dec[c:c + 1, cs] * st + _mm_tn(vb[rs, cs], ke[rs, cs])
                st_scr[:, cs] = st

    tok = pl.BlockSpec((NSUB * TH, HW), lambda i: (i, 0))
    return pl.pallas_call(
        body, name="hgrn_fwd", grid=(T // (NSUB * TH),),
        in_specs=[tok, tok, tok, pl.BlockSpec((2, HW), lambda i: (0, 0)), pl.BlockSpec((TH, TH), lambda i: (0, 0))],
        out_specs=[tok, pl.BlockSpec((NSUB * NCH, 128, HW), lambda i: (i, 0, 0))],
        out_shape=[jax.ShapeDtypeStruct((T, HW), BF16), jax.ShapeDtypeStruct((T // CHUNK, 128, HW), F32)],
        scratch_shapes=[pltpu.VMEM((128, HW), F32)],
        compiler_params=_cp(("arbitrary",)),
    )(hq, hf, hi, lbr, tri)


def _hgrn_bwd(hq, hf, hi, lbr, tri, trit, drec, sall, dhg, gw):
    NSUB = 2
    NT = T // (NSUB * TH)

    def body(hq_ref, hf_ref, hi_ref, lbr_ref, tri_ref, trit_ref, do_ref, sall_ref, dhg_ref, gw_r,
             dph_ref, small_ref, pout_o, poutr_o,
             dst_scr, dlb_scr, dqd_scr, dki_scr, dke_scr, dlast_scr, gfull, rbuf, red, redb,
             send_sems, recv_sems, loc_sems, pair_send, pair_recv):
        step = pl.program_id(0)
        loc, rem = _chip_copies(_w_out_piece, red, redb, pout_o, poutr_o, send_sems, recv_sems, loc_sems.at[0])
        mx, my, c = lax.axis_index("x"), lax.axis_index("y"), lax.axis_index("c")
        load = pltpu.make_async_copy(gw_r, gfull, loc_sems.at[1])
        halves = [pltpu.make_async_remote_copy(
            src_ref=gfull.at[pl.ds(pl.multiple_of(j * 256 + (1 - c) * 128, 128), 128), :], dst_ref=rbuf.at[j],
            send_sem=pair_send.at[j], recv_sem=pair_recv.at[j], device_id=(mx, my, 1 - c), device_id_type=MESH)
            for j in range(4)]

        @pl.when(step == 0)
        def _():
            dst_scr[...] = jnp.zeros_like(dst_scr)
            dlb_scr[...] = jnp.zeros_like(dlb_scr)
            load.start()

        @pl.when(step == 1)
        def _():
            load.wait()
            for cp in halves:
                cp.start()

        @pl.when(step == 2)
        def _():
            for j, cp in enumerate(halves):
                cp.wait_recv()
                part = gfull[pl.ds(pl.multiple_of(j * 256 + c * 128, 128), 128), :] + rbuf[j]
                red[j * 128:(j + 1) * 128, :] = part
                redb[j * 128:(j + 1) * 128, :] = part.astype(BF16)
            for cp in halves:
                cp.wait_send()
            for cp in loc + rem:
                cp.start()

        causal = _tri_mask()
        causal_t = _tri_mask(transposed=True)
        lb = None
        for u in reversed(range(NSUB)):
            tile = slice(u * TH, (u + 1) * TH)
            w = _hgrn_common(hq_ref.at[tile, :], hf_ref.at[tile, :], lbr_ref, tri_ref)
            qd, ki, ke = w["qd"].astype(BF16), w["ki"].astype(BF16), w["ke"].astype(BF16)
            dec = w["dec"]
            vb = hi_ref[tile, :]
            dob = do_ref[tile, :].astype(BF16)
            for h in range(4):
                cs = slice(h * 128, (h + 1) * 128)
                att_t = jnp.where(causal_t, _mm_nt(ki[:, cs], qd[:, cs]), 0.0).astype(BF16)
                datt_t = jnp.where(causal_t, _mm_nt(vb[:, cs], dob[:, cs]), 0.0).astype(BF16)
                datt = jnp.where(causal, _mm_nt(dob[:, cs], vb[:, cs]), 0.0).astype(BF16)
                dv_intra = _mm(att_t, dob[:, cs])
                dqd_intra = _mm(datt, ki[:, cs])
                dki_scr[u, :, cs] = _mm(datt_t, qd[:, cs])
                dst = dst_scr[:, cs]
                for c in reversed(range(NCH)):
                    rs = slice(c * CHUNK, (c + 1) * CHUNK)
                    dec_c = dec[c:c + 1, :]
                    st = sall_ref[u * NCH + c, :, cs]
                    dstb = dst.astype(BF16)
                    dph_ref[u * TH + c * CHUNK:u * TH + (c + 1) * CHUNK, 2 * HW + h * 128:2 * HW + (h + 1) * 128] = (
                        dv_intra[rs] + _mm_nt(ke[rs, cs], dstb)).astype(BF16)
                    dqd_scr[u, rs, cs] = dqd_intra[rs] + _mm(dob[rs, cs], st.astype(BF16))
                    dke_scr[u, rs, cs] = _mm(vb[rs, cs], dstb)
                    ddec = jnp.sum(dst * st, axis=0, keepdims=True)
                    dlast_scr[u, c:c + 1, cs] = ddec * dec_c[:, cs]
                    dst = dec_c[:, cs] * dst + _mm_tn(dob[rs, cs], qd[rs, cs])
                dst_scr[:, cs] = dst
            dqd, dki, dke = dqd_scr[u], dki_scr[u], dke_scr[u]
            dq = dqd * w["ea"]
            dk = dki * w["ena"] + dke * w["eend"]
            dcum = dqd * w["qd"] - dki * w["ki"] - dke * w["ke"]
            dkeke = dke * w["ke"]
            dlastb = jnp.concatenate(
                [jnp.broadcast_to(dlast_scr[u, c:c + 1, :]
                                  + jnp.sum(dkeke[c * CHUNK:(c + 1) * CHUNK], axis=0, keepdims=True), (CHUNK, HW))
                 for c in range(NCH)], axis=0)
            dg = _mm_exact_l(trit_ref[...], dcum) + dlastb
            df = dg / w["f"] - dk
            lb, sf, sq = w["lb"], w["sf"], w["sq"]
            dph_ref[tile, HW:2 * HW] = (df * (1.0 - lb) * sf * (1.0 - sf)).astype(BF16)
            dph_ref[tile, 0:HW] = (dq * (sq * (1.0 + w["hq"] * (1.0 - sq)))).astype(BF16)
            dph_ref[tile, 3 * HW:4 * HW] = dhg_ref[tile, :]
            dlb_scr[...] += jnp.sum(df * (1.0 - sf), axis=0, keepdims=True)

        @pl.when(step == NT - 1)
        def _():
            gr = dlb_scr[...] * lb * (1.0 - lb)
            small_ref[...] = jnp.zeros_like(small_ref)
            small_ref[0:1, 0:HW] = gr
            small_ref[1:2, 0:HW] = -gr
            for cp in rem:
                cp.wait_recv()
            for cp in rem:
                cp.wait_send()
            for cp in loc:
                cp.wait()

    tok = pl.BlockSpec((NSUB * TH, HW), lambda i: (NT - 1 - i, 0))
    const = lambda shape: pl.BlockSpec(shape, lambda i: (0,) * len(shape))
    hbm = pl.BlockSpec(memory_space=pltpu.HBM)
    return pl.pallas_call(
        body, name="hgrn_bwd", grid=(NT,),
        in_specs=[tok, tok, tok, const((2, HW)), const((TH, TH)), const((TH, TH)), tok,
                  pl.BlockSpec((NSUB * NCH, 128, HW), lambda i: (NT - 1 - i, 0, 0)), tok, hbm],
        out_specs=[pl.BlockSpec((NSUB * TH, NCOL // 2), lambda i: (NT - 1 - i, 0)), const((8, D)), hbm, hbm],
        out_shape=[jax.ShapeDtypeStruct((T, NCOL // 2), BF16), jax.ShapeDtypeStruct((8, D), F32),
                   jax.ShapeDtypeStruct((128, D), F32), jax.ShapeDtypeStruct((3, 128, D), BF16)],
        scratch_shapes=[pltpu.VMEM((128, HW), F32), pltpu.VMEM((1, HW), F32), pltpu.VMEM((NSUB, TH, HW), F32),
                        pltpu.VMEM((NSUB, TH, HW), F32), pltpu.VMEM((NSUB, TH, HW), F32),
                        pltpu.VMEM((NSUB, 8, HW), F32),
                        pltpu.VMEM((D, D), F32), pltpu.VMEM((4, 128, D), F32), pltpu.VMEM((512, D), F32),
                        pltpu.VMEM((512, D), BF16),
                        pltpu.SemaphoreType.DMA((3,)), pltpu.SemaphoreType.DMA((3,)), pltpu.SemaphoreType.DMA((2,)),
                        pltpu.SemaphoreType.DMA((4,)), pltpu.SemaphoreType.DMA((4,))],
        compiler_params=_cp(("arbitrary",)),
    )(hq, hf, hi, lbr, tri, trit, drec, sall, dhg, gw)


def _fwd_out(o1, o4, o16, l1, l4, l16, rec, ag, hg, x, tgt, anw, hnw, fnw, wout_full, gmat, emat, selmat):
    TT = 512

    def body(o1_r, o4_r, o16_r, l1_r, l4_r, l16_r, rec_r, ag_r, hg_r, x_r, tgt_r, anw_r, hnw_r, fnw_r, wo_r, g_r,
             e_r, sel_r, dx2_o, do1_o, do4_o, do16_o, st1_o, st4_o, st16_o, drec_o, dag_o, dhg_o,
             gw_o, small_o, scr_a, scr_b, scr_c, gwout_o, out_sem):
        @pl.when(pl.program_id(0) == 0)
        def _():
            gwout_o[...] = jnp.zeros_like(gwout_o)
            small_o[...] = jnp.zeros_like(small_o)

        def unperm(r4, r16):
            return _unperm_load(r4, r16, scr_a, scr_b, scr_c)

        def perm_out(val, p1, p4, p16, dt):
            _perm_store(val, scr_a, scr_b, p1, p4, p16, dt)

        o4u, o16u = unperm(o4_r, o16_r)
        l4c, l16c = unperm(l4_r, l16_r)
        l1c = l1_r[...]
        mxc = jnp.maximum(jnp.maximum(l1c, l4c), l16c)
        w1c, w4c, w16c = jnp.exp(l1c - mxc), jnp.exp(l4c - mxc), jnp.exp(l16c - mxc)
        denc = w1c + w4c + w16c
        lane = lax.broadcasted_iota(jnp.int32, (1, 128), 1)
        lse_c = jnp.where(lane < 8, mxc + jnp.log(denc), 0.0)
        em = e_r[...]
        wn1 = _mm_exact_r(w1c / denc, em)
        wn4 = _mm_exact_r(w4c / denc, em)
        o1v = o1_r[...].astype(F32)
        attn = wn1 * o1v + wn4 * o4u + (1.0 - wn1 - wn4) * o16u
        gm = g_r[...]

        def head_mean_a(t):
            return jnp.concatenate([_mm_exact_r(t[:, :256], gm), _mm_exact_r(t[:, 256:], gm)], axis=1)

        def head_mean_h(t):
            return jnp.concatenate(
                [jnp.broadcast_to(jnp.mean(t[:, h * 128:(h + 1) * 128], axis=-1, keepdims=True), (TT, 128))
                 for h in range(4)], axis=1)

        rs_a = lax.rsqrt(head_mean_a(attn * attn) + EPS)
        n_a = attn * rs_a
        agv = ag_r[...].astype(F32)
        sg_a = _sigmoid(agv)
        si_a = agv * sg_a
        anw_v = anw_r[...]
        y_a = (n_a * anw_v) * si_a
        recv = rec_r[...].astype(F32)
        rs_h = lax.rsqrt(head_mean_h(recv * recv) + EPS)
        n_h = recv * rs_h
        hgv = hg_r[...].astype(F32)
        sg_h = _sigmoid(hgv)
        si_h = hgv * sg_h
        hnw_v = hnw_r[...]
        y_h = (n_h * hnw_v) * si_h
        mixed = jnp.concatenate([y_a, y_h], axis=1).astype(BF16)
        xv = x_r[...]
        x2 = xv + _mm(mixed, wo_r[...])
        r2 = lax.rsqrt(jnp.mean(x2 * x2, axis=-1, keepdims=True) + EPS)
        fnw_v = fnw_r[...]
        xn = x2 * r2
        err = xn * fnw_v - tgt_r[...]
        small_o[2:3, :] += 0.5 * jnp.sum(jnp.mean(err * err, axis=-1, keepdims=True), axis=0, keepdims=True)
        small_o[0:1, :] += jnp.sum(err * xn, axis=0, keepdims=True) * (1.0 / D)
        dyw = err * (fnw_v * (1.0 / D))
        dx2 = r2 * dyw - x2 * ((r2 * r2 * r2) * jnp.mean(dyw * x2, axis=-1, keepdims=True))
        dx2_o[...] = dx2
        dx2b = dx2.astype(BF16)
        gwout_o[...] += _mm_tn(mixed, dx2b)
        dmix = _mm_nt(dx2b, wo_r[...])
        dm_a, dm_h = dmix[:, :AW], dmix[:, AW:]
        dag_o[...] = (dm_a * (n_a * anw_v) * (sg_a * (1.0 + agv * (1.0 - sg_a)))).astype(BF16)
        dy_a = dm_a * si_a
        dn_a = dy_a * anw_v
        small_o[1:2, 0:AW] += jnp.sum(dy_a * n_a, axis=0, keepdims=True)
        dattn = rs_a * (dn_a - n_a * head_mean_a(dn_a * n_a))
        perm_out(dattn, do1_o, do4_o, do16_o, BF16)
        stats = lse_c + _mm_exact_r(dattn * attn, sel_r[...])
        perm_out(stats, st1_o, st4_o, st16_o, F32)
        dhg_o[...] = (dm_h * (n_h * hnw_v) * (sg_h * (1.0 + hgv * (1.0 - sg_h)))).astype(BF16)
        dy_h = dm_h * si_h
        dn_h = dy_h * hnw_v
        small_o[1:2, AW:] += jnp.sum(dy_h * n_h, axis=0, keepdims=True)
        drec_o[...] = (rs_h * (dn_h - n_h * head_mean_h(dn_h * n_h))).astype(BF16)

        @pl.when(pl.program_id(0) == T // TT - 1)
        def _():
            out = pltpu.make_async_copy(gwout_o, gw_o, out_sem.at[0])
            out.start()
            out.wait()

    tok = lambda w: pl.BlockSpec((TT, w), lambda i: (i, 0))
    d4 = pl.BlockSpec((4, TT // 4, AW), lambda i: (0, i, 0))
    d16 = pl.BlockSpec((16, TT // 16, AW), lambda i: (0, i, 0))
    const = lambda shape: pl.BlockSpec(shape, lambda i: (0,) * len(shape))
    sd = lambda shape, dt: jax.ShapeDtypeStruct(shape, dt)
    c4 = pl.BlockSpec((4, TT // 4, 128), lambda i: (0, i, 0))
    c16 = pl.BlockSpec((16, TT // 16, 128), lambda i: (0, i, 0))
    p3 = lambda w, dt: [sd((T, w), dt), sd((4, T // 4, w), dt), sd((16, T // 16, w), dt)]
    return pl.pallas_call(
        body, name="fwd_out", grid=(T // TT,),
        in_specs=[tok(AW), d4, d16, tok(128), c4, c16, tok(AW), tok(AW), tok(AW), tok(D), tok(D),
                  const((1, AW)), const((1, HW)), const((1, D)), const((D, D)), const((256, 256)),
                  const((128, AW)), const((AW, 128))],
        out_specs=[tok(D)] + [tok(AW), d4, d16] + [tok(128), c4, c16] + [tok(AW)] * 3
        + [pl.BlockSpec(memory_space=pltpu.HBM), const((8, D))],
        out_shape=[sd((T, D), F32)] + p3(AW, BF16) + p3(128, F32)
        + [sd((T, AW), BF16), sd((T, AW), BF16), sd((T, AW), BF16), sd((D, D), F32), sd((8, D), F32)],
        scratch_shapes=[pltpu.VMEM((4, TT, 128), F32)] * 3 + [pltpu.VMEM((D, D), F32),
                        pltpu.SemaphoreType.DMA((1,))],
        compiler_params=_cp(("arbitrary",)),
    )(o1, o4, o16, l1, l4, l16, rec, ag, hg, x, tgt, anw, hnw, fnw, wout_full, gmat, emat, selmat)


def _dproj_build(dq, dk, dv, dag, pos):
    TT = 512

    def body(dq1, dq4, dq16, dk1, dk4, dk16, dv1, dv4, dv16, dag_r, pos_r, dproj_o, scr_b, scr_c):
        def unperm_sum(r1, r4, r16):
            return r1[...] + _unperm_sum(r4, r16, scr_b, scr_c)

        cosf, s1, s2 = _rope_tables(pos_r[...])
        dproj_o[:, 0:512] = _rope_bwd(unperm_sum(dq1, dq4, dq16), cosf, s1, s2).astype(BF16)
        dproj_o[:, 512:1024] = _rope_bwd(unperm_sum(dk1, dk4, dk16), cosf, s1, s2).astype(BF16)
        dproj_o[:, 1024:1536] = unperm_sum(dv1, dv4, dv16).astype(BF16)
        dproj_o[:, 1536:2048] = dag_r[...]

    tok = lambda w: pl.BlockSpec((TT, w), lambda i: (i, 0))
    d4 = pl.BlockSpec((4, TT // 4, AW), lambda i: (0, i, 0))
    d16 = pl.BlockSpec((16, TT // 16, AW), lambda i: (0, i, 0))
    return pl.pallas_call(
        body, name="dproj_build", grid=(T // TT,),
        in_specs=[tok(AW), d4, d16] * 3 + [tok(AW), pl.BlockSpec((1, TT), lambda i: (0, i))],
        out_specs=tok(NCOL // 2),
        out_shape=jax.ShapeDtypeStruct((T, NCOL // 2), BF16),
        scratch_shapes=[pltpu.VMEM((4, TT, 128), F32)] * 2,
        compiler_params=_cp(("parallel",)),
    )(*dq, *dk, *dv, dag, pos)


def _bwd_x(dproj_a, dproj_h, x, dx2, mixw, w_full, rin, rinb, small4, small6, pout_own, pout_rem):
    TT = 256
    NT = T // TT

    def body(dpa_r, dph_r, x_r, dx2_r, mw_r, w_r, rin_r, rinb_r, s4_r, s6_r, poo_r, por_r,
             gx_o, sall_o, fin_o, fout_o, sbuf, v_own, v_rem, vo_own, vo_rem, sin, sout, got_in,
             got_out, send_sems, recv_sems, loc_sems, share_send, share_recv, fin_sems):
        i = pl.program_id(0)
        loc, rem = _chip_copies(_w_in_piece, rin_r, rinb_r, v_own, v_rem, send_sems, recv_sems, loc_sems.at[0])
        loads = [pltpu.make_async_copy(poo_r, vo_own, fin_sems.at[2]),
                 pltpu.make_async_copy(por_r, vo_rem, fin_sems.at[3])]

        @pl.when(i == 0)
        def _():
            sbuf[...] = jnp.zeros_like(sbuf)
            for cp in loc + rem + loads:
                cp.start()

        dhn = _mm_nt(dpa_r[...], w_r[:, 0:NCOL // 2]) + _mm_nt(dph_r[...], w_r[:, NCOL // 2:NCOL])
        xv = x_r[...]
        r = lax.rsqrt(jnp.mean(xv * xv, axis=-1, keepdims=True) + EPS)
        dxw = dhn * mw_r[...]
        gx_o[...] = dx2_r[...] + r * dxw - xv * ((r * r * r) * jnp.mean(dxw * xv, axis=-1, keepdims=True))
        sbuf[16:17, :] += jnp.sum(dhn * (xv * r), axis=0, keepdims=True)

        @pl.when(i == NT - 1)
        def _():
            sbuf[0:8, :] = s4_r[...]
            sbuf[8:16, :] = s6_r[...]
            sloc, srem = _small_copies(sbuf, sall_o, send_sems, recv_sems, loc_sems.at[1])
            for cp in sloc + srem:
                cp.start()
            for cp in rem:
                cp.wait_recv()
            for cp in rem:
                cp.wait_send()
            for cp in loc:
                cp.wait()
            mx, my, c = lax.axis_index("x"), lax.axis_index("y"), lax.axis_index("c")
            for cp in loads:
                cp.wait()
            sout[...] = ((vo_own[...] + vo_rem[0].astype(F32)) + vo_rem[1].astype(F32)) + vo_rem[2].astype(F32)
            sin[...] = ((v_own[...] + v_rem[0].astype(F32)) + v_rem[1].astype(F32)) + v_rem[2].astype(F32)
            swap = [pltpu.make_async_remote_copy(src_ref=sin, dst_ref=got_in, send_sem=share_send.at[0],
                                                 recv_sem=share_recv.at[0], device_id=(mx, my, 1 - c),
                                                 device_id_type=MESH),
                    pltpu.make_async_remote_copy(src_ref=sout, dst_ref=got_out, send_sem=share_send.at[1],
                                                 recv_sem=share_recv.at[1], device_id=(mx, my, 1 - c),
                                                 device_id_type=MESH)]
            for cp in swap:
                cp.start()
            mine = [pltpu.make_async_copy(sin, fin_o.at[c], fin_sems.at[0]),
                    pltpu.make_async_copy(sout, fout_o.at[c], fin_sems.at[1])]
            for cp in mine:
                cp.start()
            for cp in swap:
                cp.wait_recv()
            theirs = [pltpu.make_async_copy(got_in, fin_o.at[1 - c], fin_sems.at[2]),
                      pltpu.make_async_copy(got_out, fout_o.at[1 - c], fin_sems.at[3])]
            for cp in theirs:
                cp.start()
            for cp in swap:
                cp.wait_send()
            for cp in mine + theirs:
                cp.wait()
            for cp in srem:
                cp.wait_recv()
            for cp in srem:
                cp.wait_send()
            for cp in sloc:
                cp.wait()

    tok = lambda w: pl.BlockSpec((TT, w), lambda i: (i, 0))
    const = lambda shape: pl.BlockSpec(shape, lambda i: (0,) * len(shape))
    hbm = pl.BlockSpec(memory_space=pltpu.HBM)
    return pl.pallas_call(
        body, name="bwd_x", grid=(NT,),
        in_specs=[tok(NCOL // 2), tok(NCOL // 2), tok(D), tok(D), const((1, D)), const((D, NCOL)), hbm, hbm,
                  const((8, D)), const((8, D)), hbm, hbm],
        out_specs=[tok(D), hbm, hbm, hbm],
        out_shape=[jax.ShapeDtypeStruct((T, D), F32),
                   jax.ShapeDtypeStruct((8, 24, D), F32),
                   jax.ShapeDtypeStruct((2, 512, 1024), F32), jax.ShapeDtypeStruct((2, 128, D), F32)],
        scratch_shapes=[pltpu.VMEM((24, D), F32),
                        pltpu.VMEM((512, 1024), F32), pltpu.VMEM((3, 512, 1024), BF16),
                        pltpu.VMEM((128, D), F32), pltpu.VMEM((3, 128, D), BF16),
                        pltpu.VMEM((512, 1024), F32), pltpu.VMEM((128, D), F32),
                        pltpu.VMEM((512, 1024), F32), pltpu.VMEM((128, D), F32),
                        pltpu.SemaphoreType.DMA((10,)), pltpu.SemaphoreType.DMA((10,)), pltpu.SemaphoreType.DMA((2,)),
                        pltpu.SemaphoreType.DMA((2,)), pltpu.SemaphoreType.DMA((2,)), pltpu.SemaphoreType.DMA((4,))],
        compiler_params=_cp(("arbitrary",)),
    )(dproj_a, dproj_h, x, dx2, mixw, w_full, rin, rinb, small4, small6, pout_own, pout_rem)


def _grad_w_in(hn, dproj_a, dproj_h):
    TK = 2048
    NK = T // TK

    def body(hnt_r, dpa_r, dph_r, rin_o, rinb_o, acc, rbuf, obuf, obufb, send_sems, recv_sems, wb_sems):
        j = pl.program_id(0)
        kk = pl.program_id(1)
        x, y, c = lax.axis_index("x"), lax.axis_index("y"), lax.axis_index("c")
        mine = pl.ds(pl.multiple_of(c * 512, 512), 512)
        theirs = pl.ds(pl.multiple_of((1 - c) * 512, 512), 512)

        def send(jj):
            return pltpu.make_async_remote_copy(
                src_ref=acc.at[jj % 2, theirs, :], dst_ref=rbuf.at[jj], send_sem=send_sems.at[jj],
                recv_sem=recv_sems.at[jj], device_id=(x, y, 1 - c), device_id_type=MESH)

        def writeback(jj):
            cols = pl.ds(jj * 1024, 1024)
            return [pltpu.make_async_copy(obuf.at[jj % 2], rin_o.at[:, cols], wb_sems.at[jj % 2]),
                    pltpu.make_async_copy(obufb.at[jj % 2], rinb_o.at[:, cols], wb_sems.at[2 + jj % 2])]

        def wait_writeback(jj):
            for cp in writeback(jj):
                cp.wait()

        def finalize(jj):
            send(jj).wait_recv()
            red = acc[jj % 2, mine, :] + rbuf[jj]
            obuf[jj % 2] = red
            obufb[jj % 2] = red.astype(BF16)
            for cp in writeback(jj):
                cp.start()

        prod = _mm(hnt_r[...], jnp.where(j < 2, dpa_r[...], dph_r[...]))

        @pl.when(kk == 0)
        def _():
            for jj in (2, 3):
                @pl.when(j == jj)
                def _():
                    send(jj - 2).wait_send()
            acc[j % 2] = prod

        @pl.when(kk > 0)
        def _():
            acc[j % 2] += prod

        @pl.when(kk == NK - 1)
        def _():
            for jj in range(4):
                @pl.when(j == jj)
                def _():
                    send(jj).start()
                    if jj in (1, 2):
                        finalize(jj - 1)
                    if jj == 3:
                        wait_writeback(0)
                        finalize(2)
                        wait_writeback(1)
                        finalize(3)
                        wait_writeback(2)
                        wait_writeback(3)
                        send(2).wait_send()
                        send(3).wait_send()

    hbm = pl.BlockSpec(memory_space=pltpu.HBM)
    return pl.pallas_call(
        body, name="grad_w_in", grid=(4, NK),
        in_specs=[pl.BlockSpec((D, TK), lambda j, kk: (0, kk)),
                  pl.BlockSpec((TK, 1024), lambda j, kk: (jnp.where(j < 2, kk, NK - 1), jnp.minimum(j, 1))),
                  pl.BlockSpec((TK, 1024), lambda j, kk: (jnp.where(j < 2, 0, kk), jnp.maximum(j - 2, 0)))],
        out_specs=[hbm, hbm],
        out_shape=[jax.ShapeDtypeStruct((512, NCOL), F32), jax.ShapeDtypeStruct((512, NCOL), BF16)],
        scratch_shapes=[pltpu.VMEM((2, D, 1024), F32), pltpu.VMEM((4, 512, 1024), F32), pltpu.VMEM((2, 512, 1024), F32),
                        pltpu.VMEM((2, 512, 1024), BF16),
                        pltpu.SemaphoreType.DMA((4,)), pltpu.SemaphoreType.DMA((4,)), pltpu.SemaphoreType.DMA((4,))],
        compiler_params=_cp(("arbitrary", "arbitrary")),
    )(hn, dproj_a, dproj_h)


def _w_in_piece(ref, j):
    return ref.at[:, pl.ds(j * 1024, 1024)]


def _w_out_piece(ref, j):
    return ref.at[pl.ds(j * 128, 128), :]


def _chip_copies(piece, src_r, srcb_r, own_o, rem_o, send_sems, recv_sems, loc_sem):
    x, y, c = lax.axis_index("x"), lax.axis_index("y"), lax.axis_index("c")
    chips = [(1 - x, y), (x, 1 - y), (1 - x, 1 - y)]
    loc = [pltpu.make_async_copy(piece(src_r, 2 * x + y), own_o, loc_sem)]
    rem = [pltpu.make_async_remote_copy(
        src_ref=piece(srcb_r, 2 * px + py), dst_ref=rem_o.at[k], send_sem=send_sems.at[k],
        recv_sem=recv_sems.at[k], device_id=(px, py, c), device_id_type=MESH) for k, (px, py) in enumerate(chips)]
    return loc, rem


def _small_copies(small_r, sall_o, send_sems, recv_sems, loc_sem):
    x, y, c = lax.axis_index("x"), lax.axis_index("y"), lax.axis_index("c")
    me = 4 * x + 2 * y + c
    loc = [pltpu.make_async_copy(small_r, sall_o.at[me], loc_sem)]
    rem = []
    k = 3
    for fx in range(2):
        for fy in range(2):
            for fc in range(2):
                if fx or fy or fc:
                    peer = (1 - x if fx else x, 1 - y if fy else y, 1 - c if fc else c)
                    rem.append(pltpu.make_async_remote_copy(
                        src_ref=small_r, dst_ref=sall_o.at[me], send_sem=send_sems.at[k],
                        recv_sem=recv_sems.at[k], device_id=peer, device_id_type=MESH))
                    k += 1
    return loc, rem


def _adamw_math(w, g, m, v):
    m = B1 * m + (1.0 - B1) * g
    v = B2 * v + (1.0 - B2) * (g * g)
    m_hat = m / (1.0 - B1 ** STEP)
    v_hat = v / (1.0 - B2 ** STEP)
    delta = -LR * (m_hat / (jnp.sqrt(v_hat) + AEPS) + WD * w)
    return delta, m, v


def _adamw(big_in, big_out, sall, params):
    def body(*refs):
        wi, gi, mi, vi, wo, go, mo, vo, sall_r = refs[:9]
        ins = refs[9:24]
        di_o, mi_o, vi_o, do_o, mo_o, vo_o = refs[24:30]
        outs = refs[30:]
        d, mm, vv = _adamw_math(wi[...], gi[...], mi[...], vi[...])
        di_o[...] = d
        mi_o[...] = mm
        vi_o[...] = vv

        @pl.when(pl.program_id(0) == 0)
        def _():
            d, mm, vv = _adamw_math(wo[...], go[...], mo[...], vo[...])
            do_o[...] = d
            mo_o[...] = mm
            vo_o[...] = vv
            tot = sall_r[0]
            for dv in range(1, 8):
                tot = tot + sall_r[dv]
            grads = [tot[16:17, :], tot[1:2, 0:AW], tot[1:2, AW:], tot[8:10, 0:HW], tot[0:1, :]]
            outs[0][...] = tot[2:3, 0:1]
            for p in range(5):
                w_r, m_r, v_r = ins[3 * p:3 * p + 3]
                g = grads[p]
                d, mm, vv = _adamw_math(w_r[...], g, m_r[...], v_r[...])
                outs[1 + 4 * p][...] = g
                outs[2 + 4 * p][...] = d
                outs[3 + 4 * p][...] = mm
                outs[4 + 4 * p][...] = vv

    flat = [a for p in params for a in p]
    shapes = [jax.ShapeDtypeStruct((D, 1024), F32)] * 3 + [jax.ShapeDtypeStruct((256, D), F32)] * 3
    shapes += [jax.ShapeDtypeStruct((1, 1), F32)]
    for p in params:
        shapes += [jax.ShapeDtypeStruct(p[0].shape, F32)] * 4
    vm = pl.BlockSpec(memory_space=pltpu.VMEM)
    rows = pl.BlockSpec((512, 1024), lambda i: (i, 0))
    whole = pl.BlockSpec((256, D), lambda i: (0, 0))
    return pl.pallas_call(
        body, name="adamw", grid=(2,),
        in_specs=[rows] * 4 + [whole] * 4 + [vm] * 16, out_specs=[rows] * 3 + [whole] * 3 + [vm] * 21,
        out_shape=shapes,
        compiler_params=_cp(("arbitrary",)),
    )(*big_in, *big_out, sall, *flat)


def kernel(x, positions, w_in, w_out, mix_norm_w, attn_out_norm_w, hgrn_out_norm_w, hgrn_lb_raw, final_norm_w, loss_target, m_w_in, m_w_out, m_mix_norm_w, m_attn_out_norm_w, m_hgrn_out_norm_w, m_hgrn_lb_raw, m_final_norm_w, v_w_in, v_w_out, v_mix_norm_w, v_attn_out_norm_w, v_hgrn_out_norm_w, v_hgrn_lb_raw, v_final_norm_w):
    xs = x.reshape(T, D)
    tgt = loss_target.reshape(T, D)
    pos = positions.reshape(1, T)
    fnw = final_norm_w.reshape(1, D)

    ti = np.arange(TH)
    tri_np = ((ti[:, None] // CHUNK == ti[None, :] // CHUNK) & (ti[None, :] <= ti[:, None])).astype(np.float32)
    tri = jnp.asarray(tri_np, BF16)
    trit = jnp.asarray(tri_np.T, BF16)
    hi_ = np.arange(AW) // HEAD
    gmat = jnp.asarray((hi_[:256, None] == hi_[None, :256]).astype(np.float32) / HEAD, BF16)
    emat_np = (np.arange(128)[:, None] == hi_[None, :]).astype(np.float32)
    sel_np = (8 + hi_[:, None] == np.arange(128)[None, :]).astype(np.float32)
    emat = jnp.asarray(emat_np, BF16)
    selmat = jnp.asarray(sel_np, BF16)

    jm_arr = (2 * lax.axis_index("x") + lax.axis_index("y")).astype(jnp.int32).reshape(1)
    (hn, q1, k1, v1, q4, k4, v4, q16, k16, v16, ag, hq, hf, hi, hg, w_full, wout4) = _fwd_in(
        xs, pos, mix_norm_w, w_in.reshape(D, 1024), w_out.reshape(256, D), jm_arr)
    wout_full = wout4.reshape(D, D)
    flat = lambda a: a.reshape(T, AW)
    o1, l1 = _attn_fwd(q1, k1, v1, T // BLK, "attn_fwd_d1")
    o4, l4 = _attn_fwd(flat(q4), flat(k4), flat(v4), T // 4 // BLK, "attn_fwd_d4")
    o16, l16 = _attn_fwd(flat(q16), flat(k16), flat(v16), T // 16 // BLK, "attn_fwd_d16")
    rec, sall = _hgrn_fwd(hq, hf, hi, hgrn_lb_raw, tri)

    (dx2, do1, do4, do16, st1, st4, st16, drec, dag, dhg, gw, small4) = _fwd_out(
        o1, o4.reshape(4, T // 4, AW), o16.reshape(16, T // 16, AW),
        l1, l4.reshape(4, T // 4, 128), l16.reshape(16, T // 16, 128),
        rec, ag, hg, xs, tgt, attn_out_norm_w, hgrn_out_norm_w, fnw, wout_full, gmat, emat, selmat)

    fst = lambda a: a.reshape(T, 128)
    dq1, dk1, dv1 = _attn_bwd(q1, k1, v1, do1, st1, T // BLK, "attn_bwd_d1")
    dq4, dk4, dv4 = _attn_bwd(flat(q4), flat(k4), flat(v4), flat(do4), fst(st4), T // 4 // BLK, "attn_bwd_d4")
    dq16, dk16, dv16 = _attn_bwd(flat(q16), flat(k16), flat(v16), flat(do16), fst(st16), T // 16 // BLK,
                                 "attn_bwd_d16")
    dproj_h, small6, pout_own, pout_rem = _hgrn_bwd(hq, hf, hi, hgrn_lb_raw, tri, trit, drec, sall, dhg, gw)

    r4 = lambda a: a.reshape(4, T // 4, AW)
    r16 = lambda a: a.reshape(16, T // 16, AW)
    dproj_a = _dproj_build((dq1, r4(dq4), r16(dq16)), (dk1, r4(dk4), r16(dk16)), (dv1, r4(dv4), r16(dv16)),
                           dag, pos)
    rin, rinb = _grad_w_in(hn, dproj_a, dproj_h)
    gx, small_all, fin, fout = _bwd_x(dproj_a, dproj_h, xs, dx2, mix_norm_w, w_full, rin, rinb,
                                            small4, small6, pout_own, pout_rem)
    g_w_in = fin.reshape(D, 1024)
    g_w_out = fout.reshape(256, D)

    params = [(mix_norm_w, m_mix_norm_w, v_mix_norm_w),
              (attn_out_norm_w, m_attn_out_norm_w, v_attn_out_norm_w),
              (hgrn_out_norm_w, m_hgrn_out_norm_w, v_hgrn_out_norm_w),
              (hgrn_lb_raw, m_hgrn_lb_raw, v_hgrn_lb_raw),
              (fnw, m_final_norm_w.reshape(1, D), v_final_norm_w.reshape(1, D))]
    d_in, nm_in, nv_in, d_out, nm_out, nv_out, *so = _adamw(
        (w_in.reshape(D, 1024), g_w_in, m_w_in.reshape(D, 1024), v_w_in.reshape(D, 1024)),
        (w_out.reshape(256, D), g_w_out, m_w_out.reshape(256, D), v_w_out.reshape(256, D)), small_all, params)
    loss = so[0].reshape(())
    g_s = [so[1 + 4 * p] for p in range(5)]
    d_s = [so[2 + 4 * p] for p in range(5)]
    m_s = [so[3 + 4 * p] for p in range(5)]
    v_s = [so[4 + 4 * p] for p in range(5)]
    for lst in (g_s, d_s, m_s, v_s):
        lst[4] = lst[4].reshape(D)

    return (loss, gx.reshape(1, T, D),
            g_w_in.reshape(1, D, 1024), g_w_out.reshape(1, 256, D), *g_s,
            d_in.reshape(1, D, 1024), d_out.reshape(1, 256, D), *d_s,
            nm_in.reshape(1, D, 1024), nm_out.reshape(1, 256, D), *m_s,
            nv_in.reshape(1, D, 1024), nv_out.reshape(1, 256, D), *v_s)
```

```python
import functools

import numpy as np
import jax
import jax.numpy as jnp
from jax import lax
from jax.experimental import pallas as pl
from jax.experimental.pallas import tpu as pltpu

F32 = jnp.float32
BF16 = jnp.bfloat16

T = 4096
D = 1024
AW = 512
HW = 512
NCOL = 4096
HEAD = 64
BLK = 128
CHUNK = 64
EPS = 1e-6
SCALE = HEAD ** -0.5
NEG = -1e30
ROPE_THETA = 500000.0
INV_FREQ = [float(v) for v in
            (np.float32(ROPE_THETA) ** (-(np.arange(8, dtype=np.float32)) * np.float32(0.125)))]
LR, B1, B2, AEPS, WD, STEP = 0.001, 0.9, 0.999, 1e-08, 0.01, 10
VMEM_LIMIT = 63 * 1024 * 1024
MESH = pl.DeviceIdType.MESH


def _cp(sem=None, **kw):
    return pltpu.CompilerParams(dimension_semantics=sem, vmem_limit_bytes=VMEM_LIMIT, **kw)


def _mm(a, b):
    return jnp.dot(a, b, preferred_element_type=F32)


def _mm_nt(a, b):
    return lax.dot_general(a, b, (((1,), (1,)), ((), ())), preferred_element_type=F32)


def _mm_tn(a, b):
    return lax.dot_general(a, b, (((0,), (0,)), ((), ())), preferred_element_type=F32)


def _mm_exact_l(mat_bf, x):
    h = x.astype(BF16)
    l = (x - h.astype(F32)).astype(BF16)
    return _mm(mat_bf, h) + _mm(mat_bf, l)


def _mm_exact_r(x, mat_bf):
    h = x.astype(BF16)
    l = (x - h.astype(F32)).astype(BF16)
    return _mm(h, mat_bf) + _mm(l, mat_bf)


def _sigmoid(x):
    return 0.5 * jnp.tanh(0.5 * x) + 0.5


def _rope_tables(pos):
    lane = lax.broadcasted_iota(jnp.int32, (1, 128), 1)
    jl = lane & 63
    fi = jl & 7
    inv = jnp.zeros((1, 128), F32)
    for kk in range(8):
        inv = jnp.where(fi == kk, INV_FREQ[kk], inv)
    ang = jnp.broadcast_to(pos.astype(F32), (128, pos.shape[1])).T * inv
    c = jnp.cos(ang)
    s = jnp.sin(ang)
    cosf = jnp.where(jl < 16, c, 1.0)
    s1 = jnp.where(jl < 8, -s, 0.0)
    s2 = jnp.where((jl >= 8) & (jl < 16), s, 0.0)
    return cosf, s1, s2


def _rope(t, cosf, s1, s2):
    parts = []
    for ci in range(t.shape[1] // 128):
        tc = t[:, ci * 128:(ci + 1) * 128]
        parts.append(tc * cosf + pltpu.roll(tc, 120, 1) * s1 + pltpu.roll(tc, 8, 1) * s2)
    return jnp.concatenate(parts, axis=1)


def _rope_bwd(g, cosf, s1, s2):
    parts = []
    for ci in range(g.shape[1] // 128):
        gc = g[:, ci * 128:(ci + 1) * 128]
        parts.append(gc * cosf + pltpu.roll(gc * s1, 8, 1) + pltpu.roll(gc * s2, 120, 1))
    return jnp.concatenate(parts, axis=1)


def _perm_store(val, scr, scr2, o1, o4, o16, dt):
    n = val.shape[0]
    q = n // 4
    o1[...] = val.astype(dt)
    for ci in range(val.shape[1] // 128):
        cs = slice(ci * 128, (ci + 1) * 128)
        scr[ci] = val[:, cs]
        for r4 in range(4):
            part = scr[ci, pl.ds(r4, q, stride=4), :]
            o4[r4, :, cs] = part.astype(dt)
            scr2[ci, r4 * q:(r4 + 1) * q, :] = part
        for r4 in range(4):
            for b in range(4):
                o16[r4 + 4 * b, :, cs] = scr2[ci, pl.ds(r4 * q + b, q // 4, stride=4), :].astype(dt)


def _unperm_load(r4, r16, scr_a, scr_b, scr_c):
    n = scr_a.shape[1]
    q = n // 4
    nc = r4.shape[-1] // 128
    for ci in range(nc):
        cs = slice(ci * 128, (ci + 1) * 128)
        for rr in range(4):
            scr_a[ci, pl.ds(rr, q, stride=4), :] = r4[rr, :, cs].astype(F32)
        for rr in range(4):
            for b in range(4):
                scr_c[ci, pl.ds(rr * q + b, q // 4, stride=4), :] = r16[rr + 4 * b, :, cs].astype(F32)
        for rr in range(4):
            scr_b[ci, pl.ds(rr, q, stride=4), :] = scr_c[ci, rr * q:(rr + 1) * q, :]
    return (jnp.concatenate([scr_a[ci] for ci in range(nc)], axis=1),
            jnp.concatenate([scr_b[ci] for ci in range(nc)], axis=1))


def _unperm_sum(r4, r16, scr_b, scr_c):
    n = scr_b.shape[1]
    q = n // 4
    nc = r4.shape[-1] // 128
    for ci in range(nc):
        cs = slice(ci * 128, (ci + 1) * 128)
        for rr in range(4):
            for b in range(4):
                scr_c[ci, pl.ds(rr * q + b, q // 4, stride=4), :] = r16[rr + 4 * b, :, cs].astype(F32)
        for rr in range(4):
            scr_b[ci, pl.ds(rr, q, stride=4), :] = scr_c[ci, rr * q:(rr + 1) * q, :] + r4[rr, :, cs].astype(F32)
    return jnp.concatenate([scr_b[ci] for ci in range(nc)], axis=1)


def _fwd_in(x, pos, mixw, w_in, w_out, jm_arr):
    TT = 512
    NT = T // TT

    def body(jm_ref, x_ref, pos_ref, mw_ref, win_ref, wout_ref,
             hnt_ref, q1, k1, v1, q4, k4, v4, q16, k16, v16, ag, hq, hf, hi, hg, wfull_o, woutfull_o,
             wbuf, wobuf, hn_all, scr, scr2, stage, send_sems, recv_sems, loc_sems):
        s = pl.program_id(0)
        i = pl.program_id(1)
        mx, my, c = lax.axis_index("x"), lax.axis_index("y"), lax.axis_index("c")
        me, sibling = (mx, my, c), (mx, my, 1 - c)
        chips = [(mx, 1 - my), (1 - mx, my), (1 - mx, 1 - my)]
        jm = 2 * mx + my
        rows_in = [pl.ds(pl.multiple_of(h * 512, 512), 512) for h in (c, 1 - c)]
        rows_out = [pl.ds(pl.multiple_of(h * 128, 128), 128) for h in (c, 1 - c)]

        def blk(k):
            return lax.bitwise_xor(jm, k + 1)

        def rc(n, ref, to):
            return pltpu.make_async_remote_copy(src_ref=ref, dst_ref=ref, send_sem=send_sems.at[n],
                                                recv_sem=recv_sems.at[n], device_id=to, device_id_type=MESH)

        halves = [pl.ds(0, 512), pl.ds(512, 512)]
        send_in = lambda k, h: rc(12 + 2 * k + h, wbuf.at[jm, rows_in[0], halves[h]], (*chips[k], c))
        got_in = lambda k, h: rc(12 + 2 * k + h, wbuf.at[blk(k), rows_in[0], halves[h]], me)
        relay = lambda h: rc(16 + h, wbuf.at[blk(h), rows_in[0], halves[h]], (*chips[1 - h], c))
        got_relay = lambda h: rc(16 + h, wbuf.at[blk(2), rows_in[0], halves[h]], me)
        send_out = lambda k: rc(3 + k, wobuf.at[jm, rows_out[0], :], (*chips[k], c))
        got_out = lambda k: rc(3 + k, wobuf.at[blk(k), rows_out[0], :], me)
        pass_in = lambda k: rc(6 + k, wbuf.at[blk(k), rows_in[0], :], sibling)
        pass_out = lambda k: rc(9 + k, wobuf.at[blk(k), rows_out[0], :], sibling)
        passed_in = lambda k: rc(6 + k, wbuf.at[blk(k), rows_in[1], :], me)
        passed_out = lambda k: rc(9 + k, wobuf.at[blk(k), rows_out[1], :], me)

        def keep(j, n):
            return pltpu.make_async_copy(wbuf.at[j], wfull_o.at[:, pl.ds(j * 1024, 1024)], loc_sems.at[n])

        @pl.when((s == 0) & (i == 0))
        def _():
            chunk = [pl.ds(pl.multiple_of(lax.rem(p + 2 * c, 4) * 256, 256), 256) for p in range(4)]
            loads = [pltpu.make_async_copy(win_ref.at[chunk[p], :] if p < 4 else wout_ref, stage.at[p % 2],
                                           loc_sems.at[4 + p % 2]) for p in range(5)]
            loads[0].start()
            for p in range(5):
                if p < 4:
                    loads[p + 1].start()
                loads[p].wait()
                if p < 4:
                    wbuf[jm, chunk[p], :] = stage[p % 2].astype(BF16)
                else:
                    wobuf[jm] = stage[p % 2].astype(BF16)
                if p == 1:
                    for k in range(2):
                        for h in range(2):
                            send_in(k, h).start()
            keep(jm, 0).start()

        @pl.when((s == 0) & (i == NT - 1))
        def _():
            for kk in range(2):
                for h in range(2):
                    got_in(kk, h).wait_recv()
            relay(0).start()
            relay(1).start()
            pass_in(0).start()
            pass_in(1).start()
            passed_in(0).wait_recv()
            keep(blk(0), 1).start()

        @pl.when((s == 1) & (i == NT - 1))
        def _():
            got_relay(0).wait_recv()
            got_relay(1).wait_recv()
            pass_in(2).start()

        @pl.when((s == 2) & (i == 0))
        def _():
            for k in (1, 2):
                passed_in(k).wait_recv()
                keep(blk(k), k + 1).start()
            for kk in range(3):
                send_out(kk).start()

        @pl.when((s == 2) & (i == NT - 2))
        def _():
            for k in range(3):
                got_out(k).wait_recv()
                pass_out(k).start()

        whole_out = pltpu.make_async_copy(wobuf, woutfull_o, loc_sems.at[4])

        @pl.when((s == 2) & (i == NT - 1))
        def _():
            for k in range(3):
                passed_out(k).wait_recv()
            whole_out.start()

        tile = pl.ds(pl.multiple_of(i * TT, TT), TT)

        @pl.when(s == 0)
        def _():
            xv = x_ref[...]
            r = lax.rsqrt(jnp.mean(xv * xv, axis=-1, keepdims=True) + EPS)
            hnf = (xv * r) * mw_ref[...]
            hn_all[tile, :] = hnf.astype(BF16)
            hnt_ref[...] = hnf.T.astype(BF16)

        def project(jj):
            hn = hn_all[tile, :]
            lo = _mm(hn, wbuf[jj, :, 0:512])
            hi_cols = _mm(hn, wbuf[jj, :, 512:1024])
            if jj == 0:
                cosf, s1, s2 = _rope_tables(pos_ref[...])
                _perm_store(_rope(lo, cosf, s1, s2) * SCALE, scr, scr2, q1, q4, q16, BF16)
                _perm_store(_rope(hi_cols, cosf, s1, s2), scr, scr2, k1, k4, k16, BF16)
            elif jj == 1:
                _perm_store(lo, scr, scr2, v1, v4, v16, BF16)
                ag[...] = hi_cols.astype(BF16)
            elif jj == 2:
                hq[...] = lo.astype(BF16)
                hf[...] = hi_cols.astype(BF16)
            else:
                hi[...] = lo.astype(BF16)
                hg[...] = hi_cols.astype(BF16)

        def project_block(j):
            for jj in range(4):
                pl.when(j == jj)(functools.partial(project, jj))

        @pl.when(s < 2)
        def _():
            project_block(lax.bitwise_xor(jm, s))

        @pl.when(s == 2)
        def _():
            project_block(lax.bitwise_xor(jm, 2))
            project_block(lax.bitwise_xor(jm, 3))

        @pl.when((s == 2) & (i == NT - 1))
        def _():
            for h in range(2):
                relay(h).wait_send()
                for k in range(2):
                    send_in(k, h).wait_send()
            for k in range(3):
                send_out(k).wait_send()
                pass_in(k).wait_send()
                pass_out(k).wait_send()
            keep(jm, 0).wait()
            for k in range(3):
                keep(blk(k), k + 1).wait()
            whole_out.wait()

    def at_stage_of(jb):
        def index(s, i, jm_ref):
            sa = jnp.minimum(lax.bitwise_xor(jm_ref[0], jb), 2)
            return jnp.where(s < sa, 0, jnp.where(s == sa, i, NT - 1))
        return index

    tok = lambda w, jb: pl.BlockSpec((TT, w), lambda s, i, jm_ref: (at_stage_of(jb)(s, i, jm_ref), 0))
    d4 = lambda jb: pl.BlockSpec((4, TT // 4, AW), lambda s, i, jm_ref: (0, at_stage_of(jb)(s, i, jm_ref), 0))
    d16 = lambda jb: pl.BlockSpec((16, TT // 16, AW), lambda s, i, jm_ref: (0, at_stage_of(jb)(s, i, jm_ref), 0))
    hbm = pl.BlockSpec(memory_space=pltpu.HBM)
    sd = lambda shape, dt: jax.ShapeDtypeStruct(shape, dt)
    in_own_stage = lambda s, i: jnp.where(s == 0, i, NT - 1)
    grid_spec = pltpu.PrefetchScalarGridSpec(
        num_scalar_prefetch=1, grid=(3, NT),
        in_specs=[pl.BlockSpec((TT, D), lambda s, i, jm_ref: (in_own_stage(s, i), 0)),
                  pl.BlockSpec((1, TT), lambda s, i, jm_ref: (0, i)),
                  pl.BlockSpec((1, D), lambda s, i, jm_ref: (0, 0)), hbm, hbm],
        out_specs=[pl.BlockSpec((D, TT), lambda s, i, jm_ref: (0, in_own_stage(s, i))),
                   tok(AW, 0), tok(AW, 0), tok(AW, 1), d4(0), d4(0), d4(1), d16(0), d16(0), d16(1),
                   tok(AW, 1), tok(AW, 2), tok(AW, 2), tok(AW, 3), tok(AW, 3), hbm, hbm],
        scratch_shapes=[pltpu.VMEM((4, D, 1024), BF16), pltpu.VMEM((4, 256, D), BF16), pltpu.VMEM((T, D), BF16),
                        pltpu.VMEM((4, TT, 128), F32), pltpu.VMEM((4, TT, 128), F32), pltpu.VMEM((2, 256, 1024), F32),
                        pltpu.SemaphoreType.DMA((18,)),
                        pltpu.SemaphoreType.DMA((18,)), pltpu.SemaphoreType.DMA((6,))])
    return pl.pallas_call(
        body, name="fwd_in", grid_spec=grid_spec,
        out_shape=[sd((D, T), BF16)] + [sd((T, AW), BF16)] * 3 + [sd((4, T // 4, AW), BF16)] * 3
        + [sd((16, T // 16, AW), BF16)] * 3
        + [sd((T, AW), BF16)] * 5 + [sd((D, NCOL), BF16), sd((4, 256, D), BF16)],
        compiler_params=_cp(("arbitrary", "arbitrary")),
    )(jm_arr, x, pos, mixw, w_in, w_out)


def _band_mask(key_axis, nkeys=2 * BLK):
    shape = (nkeys, 2 * BLK) if key_axis == 0 else (2 * BLK, nkeys)
    kj = lax.broadcasted_iota(jnp.int32, shape, key_axis)
    qi = lax.broadcasted_iota(jnp.int32, shape, 1 - key_axis) & (BLK - 1)
    return (kj >= qi) & (kj <= qi + BLK), kj, qi


def _stack_heads(t2, in_a):
    z = jnp.zeros_like(t2)
    return jnp.concatenate([jnp.where(in_a[0], t2, z), jnp.where(in_a[1], t2, z)], axis=0)


def _attn_fwd(q, k, v, nb, name):
    n = 8
    CH = n * BLK
    halo = nb > n

    def body(*refs):
        if halo:
            q_ref, k_ref, v_ref, kp_ref, vp_ref, o_ref, lse_ref = refs
        else:
            q_ref, k_ref, v_ref, o_ref, lse_ref = refs
        lane = lax.broadcasted_iota(jnp.int32, (1, 128), 1)
        in_a = [lane < HEAD, lane >= HEAD]
        band, kj, _ = _band_mask(1)
        thr0 = jnp.where((n * pl.program_id(0)) % nb == 0, BLK, 0) if halo else BLK
        mask0 = band & (kj >= thr0)
        mask_first = band & (kj >= BLK)
        for b in range(n):
            rs = slice(b * BLK, (b + 1) * BLK)
            stat = jnp.zeros((BLK, 128), F32)
            for hp in range(4):
                cs = slice(hp * 128, (hp + 1) * 128)
                q2s = _stack_heads(q_ref[rs, cs], in_a)
                if b == 0:
                    kprev = kp_ref[:, cs] if halo else k_ref[rs, cs]
                    vprev = vp_ref[:, cs] if halo else v_ref[rs, cs]
                    kk = jnp.concatenate([kprev, k_ref[rs, cs]], axis=0)
                    vv = jnp.concatenate([vprev, v_ref[rs, cs]], axis=0)
                    mask = mask0
                else:
                    kk = k_ref[(b - 1) * BLK:(b + 1) * BLK, cs]
                    vv = v_ref[(b - 1) * BLK:(b + 1) * BLK, cs]
                    mask = mask_first if b % nb == 0 else band
                s = jnp.where(mask, _mm_nt(q2s, kk), NEG)
                m = jnp.max(s, axis=-1, keepdims=True)
                p = jnp.exp(s - m)
                l = jnp.sum(p, axis=-1, keepdims=True)
                o = _mm(p.astype(BF16), vv) / l
                lse = m + jnp.log(l)
                o_ref[rs, cs] = jnp.where(in_a[0], o[:BLK], o[BLK:]).astype(BF16)
                stat = jnp.where(lane == 2 * hp, lse[:BLK], stat)
                stat = jnp.where(lane == 2 * hp + 1, lse[BLK:], stat)
            lse_ref[rs, :] = stat

    cur = pl.BlockSpec((CH, AW), lambda i: (i, 0))
    prev = pl.BlockSpec((BLK, AW), lambda i: (jnp.maximum(n * i - 1, 0), 0))
    return pl.pallas_call(
        body, name=name, grid=(T // CH,),
        in_specs=[cur, cur, cur] + ([prev, prev] if halo else []),
        out_specs=[cur, pl.BlockSpec((CH, 128), lambda i: (i, 0))],
        out_shape=[jax.ShapeDtypeStruct((T, AW), BF16), jax.ShapeDtypeStruct((T, 128), F32)],
        compiler_params=_cp(("parallel",)),
    )(*((q, k, v) + ((k, v) if halo else ())))


def _attn_bwd(q, k, v, do, st, nb, name):
    n = 8
    CH = n * BLK
    NBLK = T // BLK
    halo = nb > n

    def body(*refs):
        if halo:
            (q_ref, k_ref, v_ref, do_ref, st_ref, kp_ref, vp_ref, qn_ref, don_ref, stn_ref,
             dq_ref, dk_ref, dv_ref) = refs
        else:
            q_ref, k_ref, v_ref, do_ref, st_ref, dq_ref, dk_ref, dv_ref = refs
        i = pl.program_id(0)
        lane = lax.broadcasted_iota(jnp.int32, (1, 128), 1)
        in_a = [lane < HEAD, lane >= HEAD]
        band, kj, _ = _band_mask(0)
        thr0 = jnp.where((n * i) % nb == 0, BLK, 0) if halo else BLK
        mask0 = band & (kj >= thr0)
        mask_first = band & (kj >= BLK)

        def stat_rows(st_t, hp):
            lse_r = jnp.concatenate([st_t[2 * hp:2 * hp + 1, :], st_t[2 * hp + 1:2 * hp + 2, :]], axis=1)
            dl_r = jnp.concatenate([st_t[8 + 2 * hp:9 + 2 * hp, :], st_t[9 + 2 * hp:10 + 2 * hp, :]], axis=1)
            return lse_r, dl_r

        st_t = [st_ref[b * BLK:(b + 1) * BLK, :].T for b in range(n)]
        if halo:
            nxt_thr = jnp.where((n * i + n) % nb == 0, 2 * BLK, 0)
            _, kj1, qi1 = _band_mask(0, BLK)
            mask_next = kj1 >= qi1 + nxt_thr
            stn_t = stn_ref[...].T

        for hp in range(4):
            cs = slice(hp * 128, (hp + 1) * 128)
            kb = [k_ref[b * BLK:(b + 1) * BLK, cs] for b in range(n)]
            vb = [v_ref[b * BLK:(b + 1) * BLK, cs] for b in range(n)]
            dk_acc = [jnp.zeros((BLK, 128), F32) for _ in range(n)]
            dv_acc = [jnp.zeros((BLK, 128), F32) for _ in range(n)]
            for b in range(n):
                rs = slice(b * BLK, (b + 1) * BLK)
                q2s = _stack_heads(q_ref[rs, cs], in_a)
                do2s = _stack_heads(do_ref[rs, cs], in_a)
                if b == 0:
                    kprev = kp_ref[:, cs] if halo else kb[0]
                    vprev = vp_ref[:, cs] if halo else vb[0]
                    mask = mask0
                else:
                    kprev, vprev, mask = kb[b - 1], vb[b - 1], (mask_first if b % nb == 0 else band)
                kk = jnp.concatenate([kprev, kb[b]], axis=0)
                vv = jnp.concatenate([vprev, vb[b]], axis=0)
                lse_r, dl_r = stat_rows(st_t[b], hp)
                s_t = jnp.where(mask, _mm_nt(kk, q2s), NEG)
                p_t = jnp.exp(s_t - lse_r)
                ds_t = (p_t * (_mm_nt(vv, do2s) - dl_r)).astype(BF16)
                dkk = _mm(ds_t, q2s)
                dvv = _mm(p_t.astype(BF16), do2s)
                dqs = _mm_tn(ds_t, kk) * SCALE
                dq_ref[rs, cs] = jnp.where(in_a[0], dqs[:BLK], dqs[BLK:]).astype(BF16)
                dk_acc[b] += dkk[BLK:]
                dv_acc[b] += dvv[BLK:]
                if b > 0:
                    dk_acc[b - 1] += dkk[:BLK]
                    dv_acc[b - 1] += dvv[:BLK]
            if halo:
                q2s = _stack_heads(qn_ref[:, cs], in_a)
                do2s = _stack_heads(don_ref[:, cs], in_a)
                lse_r, dl_r = stat_rows(stn_t, hp)
                s_t = jnp.where(mask_next, _mm_nt(kb[n - 1], q2s), NEG)
                p_t = jnp.exp(s_t - lse_r)
                ds_t = (p_t * (_mm_nt(vb[n - 1], do2s) - dl_r)).astype(BF16)
                dk_acc[n - 1] += _mm(ds_t, q2s)
                dv_acc[n - 1] += _mm(p_t.astype(BF16), do2s)
            for b in range(n):
                dk_ref[b * BLK:(b + 1) * BLK, cs] = dk_acc[b].astype(BF16)
                dv_ref[b * BLK:(b + 1) * BLK, cs] = dv_acc[b].astype(BF16)

    cur = pl.BlockSpec((CH, AW), lambda i: (i, 0))
    cur_st = pl.BlockSpec((CH, 128), lambda i: (i, 0))
    prev = pl.BlockSpec((BLK, AW), lambda i: (jnp.maximum(n * i - 1, 0), 0))
    nxt = pl.BlockSpec((BLK, AW), lambda i: (jnp.minimum(n * i + n, NBLK - 1), 0))
    nxt_st = pl.BlockSpec((BLK, 128), lambda i: (jnp.minimum(n * i + n, NBLK - 1), 0))
    ins = [cur] * 4 + [cur_st] + ([prev, prev, nxt, nxt, nxt_st] if halo else [])
    args = (q, k, v, do, st) + ((k, v, q, do, st) if halo else ())
    return pl.pallas_call(
        body, name=name, grid=(T // CH,),
        in_specs=ins,
        out_specs=[cur] * 3,
        out_shape=[jax.ShapeDtypeStruct((T, AW), BF16)] * 3,
        compiler_params=_cp(("parallel",)),
    )(*args)


TH = 256
NCH = TH // CHUNK


def _hgrn_common(hq_ref, hf_ref, lbr_ref, tri_ref):
    r0 = lbr_ref[0:1, :]
    r1 = lbr_ref[1:2, :]
    mx = jnp.maximum(r0, r1)
    e0 = jnp.exp(r0 - mx)
    e1 = jnp.exp(r1 - mx)
    lb = e0 / (e0 + e1)
    hqv = hq_ref[...].astype(F32)
    sq = _sigmoid(hqv)
    qv = hqv * sq
    sf = _sigmoid(hf_ref[...].astype(F32))
    f = lb + (1.0 - lb) * sf
    kv = 1.0 - f
    g = jnp.log(f)
    cum = _mm_exact_l(tri_ref[...], g)
    dec = jnp.exp(jnp.concatenate([cum[c * CHUNK + CHUNK - 1:(c + 1) * CHUNK, :] for c in range(NCH)], axis=0))
    decb = jnp.concatenate([jnp.broadcast_to(dec[c:c + 1, :], (CHUNK, HW)) for c in range(NCH)], axis=0)
    ea = jnp.exp(cum)
    ena = jnp.exp(-cum)
    eend = decb * ena
    return dict(lb=lb, hq=hqv, sq=sq, q=qv, sf=sf, f=f, k=kv, cum=cum, ea=ea, ena=ena, eend=eend,
                qd=qv * ea, ki=kv * ena, ke=kv * eend, dec=dec)


def _tri_mask(transposed=False):
    ti = lax.broadcasted_iota(jnp.int32, (TH, TH), 1 if transposed else 0)
    si = lax.broadcasted_iota(jnp.int32, (TH, TH), 0 if transposed else 1)
    return (si <= ti) & ((si // CHUNK) == (ti // CHUNK))


def _hgrn_fwd(hq, hf, hi, lbr, tri):
    NSUB = 2

    def body(hq_ref, hf_ref, hi_ref, lbr_ref, tri_ref, rec_ref, sall_ref, st_scr):
        @pl.when(pl.program_id(0) == 0)
        def _():
            st_scr[...] = jnp.zeros_like(st_scr)

        causal = _tri_mask()
        for u in range(NSUB):
            tile = slice(u * TH, (u + 1) * TH)
            w = _hgrn_common(hq_ref.at[tile, :], hf_ref.at[tile, :], lbr_ref, tri_ref)
            qd, ki, ke = w["qd"].astype(BF16), w["ki"].astype(BF16), w["ke"].astype(BF16)
            dec = w["dec"]
            vb = hi_ref[tile, :]
            for h in range(4):
                cs = slice(h * 128, (h + 1) * 128)
                att = jnp.where(causal, _mm_nt(qd[:, cs], ki[:, cs]), 0.0)
                o_intra = _mm(att.astype(BF16), vb[:, cs])
                st = st_scr[:, cs]
                for c in range(NCH):
                    rs = slice(c * CHUNK, (c + 1) * CHUNK)
                    sall_ref[u * NCH + c, :, cs] = st
                    rec_ref[u * TH + c * CHUNK:u * TH + (c + 1) * CHUNK, cs] = (
                        o_intra[rs] + _mm_nt(qd[rs, cs], st.astype(BF16))).astype(BF16)
                    st = dec[c:c + 1, cs] * st + _mm_tn(vb[rs, cs], ke[rs, cs])
                st_scr[:, cs] = st

    tok = pl.BlockSpec((NSUB * TH, HW), lambda i: (i, 0))
    return pl.pallas_call(
        body, name="hgrn_fwd", grid=(T // (NSUB * TH),),
        in_specs=[tok, tok, tok, pl.BlockSpec((2, HW), lambda i: (0, 0)), pl.BlockSpec((TH, TH), lambda i: (0, 0))],
        out_specs=[tok, pl.BlockSpec((NSUB * NCH, 128, HW), lambda i: (i, 0, 0))],
        out_shape=[jax.ShapeDtypeStruct((T, HW), BF16), jax.ShapeDtypeStruct((T // CHUNK, 128, HW), F32)],
        scratch_shapes=[pltpu.VMEM((128, HW), F32)],
        compiler_params=_cp(("arbitrary",)),
    )(hq, hf, hi, lbr, tri)


def _hgrn_bwd(hq, hf, hi, lbr, tri, trit, drec, sall, dhg, gw):
    NSUB = 2
    NT = T // (NSUB * TH)

    def body(hq_ref, hf_ref, hi_ref, lbr_ref, tri_ref, trit_ref, do_ref, sall_ref, dhg_ref, gw_r,
             dph_ref, small_ref, pout_o, poutr_o,
             dst_scr, dlb_scr, dqd_scr, dki_scr, dke_scr, dlast_scr, gfull, rbuf, red, redb,
             send_sems, recv_sems, loc_sems, pair_send, pair_recv):
        step = pl.program_id(0)
        loc, rem = _chip_copies(_w_out_piece, red, redb, pout_o, poutr_o, send_sems, recv_sems, loc_sems.at[0])
        mx, my, c = lax.axis_index("x"), lax.axis_index("y"), lax.axis_index("c")
        load = pltpu.make_async_copy(gw_r, gfull, loc_sems.at[1])
        halves = [pltpu.make_async_remote_copy(
            src_ref=gfull.at[pl.ds(pl.multiple_of(j * 256 + (1 - c) * 128, 128), 128), :], dst_ref=rbuf.at[j],
            send_sem=pair_send.at[j], recv_sem=pair_recv.at[j], device_id=(mx, my, 1 - c), device_id_type=MESH)
            for j in range(4)]

        @pl.when(step == 0)
        def _():
            dst_scr[...] = jnp.zeros_like(dst_scr)
            dlb_scr[...] = jnp.zeros_like(dlb_scr)
            load.start()

        @pl.when(step == 1)
        def _():
            load.wait()
            for cp in halves:
                cp.start()

        @pl.when(step == 2)
        def _():
            for j, cp in enumerate(halves):
                cp.wait_recv()
                part = gfull[pl.ds(pl.multiple_of(j * 256 + c * 128, 128), 128), :] + rbuf[j]
                red[j * 128:(j + 1) * 128, :] = part
                redb[j * 128:(j + 1) * 128, :] = part.astype(BF16)
            for cp in halves:
                cp.wait_send()
            for cp in loc + rem:
                cp.start()

        causal = _tri_mask()
        causal_t = _tri_mask(transposed=True)
        lb = None
        for u in reversed(range(NSUB)):
            tile = slice(u * TH, (u + 1) * TH)
            w = _hgrn_common(hq_ref.at[tile, :], hf_ref.at[tile, :], lbr_ref, tri_ref)
            qd, ki, ke = w["qd"].astype(BF16), w["ki"].astype(BF16), w["ke"].astype(BF16)
            dec = w["dec"]
            vb = hi_ref[tile, :]
            dob = do_ref[tile, :].astype(BF16)
            for h in range(4):
                cs = slice(h * 128, (h + 1) * 128)
                att_t = jnp.where(causal_t, _mm_nt(ki[:, cs], qd[:, cs]), 0.0).astype(BF16)
                datt_t = jnp.where(causal_t, _mm_nt(vb[:, cs], dob[:, cs]), 0.0).astype(BF16)
                datt = jnp.where(causal, _mm_nt(dob[:, cs], vb[:, cs]), 0.0).astype(BF16)
                dv_intra = _mm(att_t, dob[:, cs])
                dqd_intra = _mm(datt, ki[:, cs])
                dki_scr[u, :, cs] = _mm(datt_t, qd[:, cs])
                dst = dst_scr[:, cs]
                for c in reversed(range(NCH)):
                    rs = slice(c * CHUNK, (c + 1) * CHUNK)
                    dec_c = dec[c:c + 1, :]
                    st = sall_ref[u * NCH + c, :, cs]
                    dstb = dst.astype(BF16)
                    dph_ref[u * TH + c * CHUNK:u * TH + (c + 1) * CHUNK, 2 * HW + h * 128:2 * HW + (h + 1) * 128] = (
                        dv_intra[rs] + _mm_nt(ke[rs, cs], dstb)).astype(BF16)
                    dqd_scr[u, rs, cs] = dqd_intra[rs] + _mm(dob[rs, cs], st.astype(BF16))
                    dke_scr[u, rs, cs] = _mm(vb[rs, cs], dstb)
                    ddec = jnp.sum(dst * st, axis=0, keepdims=True)
                    dlast_scr[u, c:c + 1, cs] = ddec * dec_c[:, cs]
                    dst = dec_c[:, cs] * dst + _mm_tn(dob[rs, cs], qd[rs, cs])
                dst_scr[:, cs] = dst
            dqd, dki, dke = dqd_scr[u], dki_scr[u], dke_scr[u]
            dq = dqd * w["ea"]
            dk = dki * w["ena"] + dke * w["eend"]
            dcum = dqd * w["qd"] - dki * w["ki"] - dke * w["ke"]
            dkeke = dke * w["ke"]
            dlastb = jnp.concatenate(
                [jnp.broadcast_to(dlast_scr[u, c:c + 1, :]
                                  + jnp.sum(dkeke[c * CHUNK:(c + 1) * CHUNK], axis=0, keepdims=True), (CHUNK, HW))
                 for c in range(NCH)], axis=0)
            dg = _mm_exact_l(trit_ref[...], dcum) + dlastb
            df = dg / w["f"] - dk
            lb, sf, sq = w["lb"], w["sf"], w["sq"]
            dph_ref[tile, HW:2 * HW] = (df * (1.0 - lb) * sf * (1.0 - sf)).astype(BF16)
            dph_ref[tile, 0:HW] = (dq * (sq * (1.0 + w["hq"] * (1.0 - sq)))).astype(BF16)
            dph_ref[tile, 3 * HW:4 * HW] = dhg_ref[tile, :]
            dlb_scr[...] += jnp.sum(df * (1.0 - sf), axis=0, keepdims=True)

        @pl.when(step == NT - 1)
        def _():
            gr = dlb_scr[...] * lb * (1.0 - lb)
            small_ref[...] = jnp.zeros_like(small_ref)
            small_ref[0:1, 0:HW] = gr
            small_ref[1:2, 0:HW] = -gr
            for cp in rem:
                cp.wait_recv()
            for cp in rem:
                cp.wait_send()
            for cp in loc:
                cp.wait()

    tok = pl.BlockSpec((NSUB * TH, HW), lambda i: (NT - 1 - i, 0))
    const = lambda shape: pl.BlockSpec(shape, lambda i: (0,) * len(shape))
    hbm = pl.BlockSpec(memory_space=pltpu.HBM)
    return pl.pallas_call(
        body, name="hgrn_bwd", grid=(NT,),
        in_specs=[tok, tok, tok, const((2, HW)), const((TH, TH)), const((TH, TH)), tok,
                  pl.BlockSpec((NSUB * NCH, 128, HW), lambda i: (NT - 1 - i, 0, 0)), tok, hbm],
        out_specs=[pl.BlockSpec((NSUB * TH, NCOL // 2), lambda i: (NT - 1 - i, 0)), const((8, D)), hbm, hbm],
        out_shape=[jax.ShapeDtypeStruct((T, NCOL // 2), BF16), jax.ShapeDtypeStruct((8, D), F32),
                   jax.ShapeDtypeStruct((128, D), F32), jax.ShapeDtypeStruct((3, 128, D), BF16)],
        scratch_shapes=[pltpu.VMEM((128, HW), F32), pltpu.VMEM((1, HW), F32), pltpu.VMEM((NSUB, TH, HW), F32),
                        pltpu.VMEM((NSUB, TH, HW), F32), pltpu.VMEM((NSUB, TH, HW), F32),
                        pltpu.VMEM((NSUB, 8, HW), F32),
                        pltpu.VMEM((D, D), F32), pltpu.VMEM((4, 128, D), F32), pltpu.VMEM((512, D), F32),
                        pltpu.VMEM((512, D), BF16),
                        pltpu.SemaphoreType.DMA((3,)), pltpu.SemaphoreType.DMA((3,)), pltpu.SemaphoreType.DMA((2,)),
                        pltpu.SemaphoreType.DMA((4,)), pltpu.SemaphoreType.DMA((4,))],
        compiler_params=_cp(("arbitrary",)),
    )(hq, hf, hi, lbr, tri, trit, drec, sall, dhg, gw)


def _fwd_out(o1, o4, o16, l1, l4, l16, rec, ag, hg, x, tgt, anw, hnw, fnw, wout_full, gmat, emat, selmat):
    TT = 512

    def body(o1_r, o4_r, o16_r, l1_r, l4_r, l16_r, rec_r, ag_r, hg_r, x_r, tgt_r, anw_r, hnw_r, fnw_r, wo_r, g_r,
             e_r, sel_r, dx2_o, do1_o, do4_o, do16_o, st1_o, st4_o, st16_o, drec_o, dag_o, dhg_o,
             gw_o, small_o, scr_a, scr_b, scr_c, gwout_o, out_sem):
        @pl.when(pl.program_id(0) == 0)
        def _():
            gwout_o[...] = jnp.zeros_like(gwout_o)
            small_o[...] = jnp.zeros_like(small_o)

        def unperm(r4, r16):
            return _unperm_load(r4, r16, scr_a, scr_b, scr_c)

        def perm_out(val, p1, p4, p16, dt):
            _perm_store(val, scr_a, scr_b, p1, p4, p16, dt)

        o4u, o16u = unperm(o4_r, o16_r)
        l4c, l16c = unperm(l4_r, l16_r)
        l1c = l1_r[...]
        mxc = jnp.maximum(jnp.maximum(l1c, l4c), l16c)
        w1c, w4c, w16c = jnp.exp(l1c - mxc), jnp.exp(l4c - mxc), jnp.exp(l16c - mxc)
        denc = w1c + w4c + w16c
        lane = lax.broadcasted_iota(jnp.int32, (1, 128), 1)
        lse_c = jnp.where(lane < 8, mxc + jnp.log(denc), 0.0)
        em = e_r[...]
        wn1 = _mm_exact_r(w1c / denc, em)
        wn4 = _mm_exact_r(w4c / denc, em)
        o1v = o1_r[...].astype(F32)
        attn = wn1 * o1v + wn4 * o4u + (1.0 - wn1 - wn4) * o16u
        gm = g_r[...]

        def head_mean_a(t):
            return jnp.concatenate([_mm_exact_r(t[:, :256], gm), _mm_exact_r(t[:, 256:], gm)], axis=1)

        def head_mean_h(t):
            return jnp.concatenate(
                [jnp.broadcast_to(jnp.mean(t[:, h * 128:(h + 1) * 128], axis=-1, keepdims=True), (TT, 128))
                 for h in range(4)], axis=1)

        rs_a = lax.rsqrt(head_mean_a(attn * attn) + EPS)
        n_a = attn * rs_a
        agv = ag_r[...].astype(F32)
        sg_a = _sigmoid(agv)
        si_a = agv * sg_a
        anw_v = anw_r[...]
        y_a = (n_a * anw_v) * si_a
        recv = rec_r[...].astype(F32)
        rs_h = lax.rsqrt(head_mean_h(recv * recv) + EPS)
        n_h = recv * rs_h
        hgv = hg_r[...].astype(F32)
        sg_h = _sigmoid(hgv)
        si_h = hgv * sg_h
        hnw_v = hnw_r[...]
        y_h = (n_h * hnw_v) * si_h
        mixed = jnp.concatenate([y_a, y_h], axis=1).astype(BF16)
        xv = x_r[...]
        x2 = xv + _mm(mixed, wo_r[...])
        r2 = lax.rsqrt(jnp.mean(x2 * x2, axis=-1, keepdims=True) + EPS)
        fnw_v = fnw_r[...]
        xn = x2 * r2
        err = xn * fnw_v - tgt_r[...]
        small_o[2:3, :] += 0.5 * jnp.sum(jnp.mean(err * err, axis=-1, keepdims=True), axis=0, keepdims=True)
        small_o[0:1, :] += jnp.sum(err * xn, axis=0, keepdims=True) * (1.0 / D)
        dyw = err * (fnw_v * (1.0 / D))
        dx2 = r2 * dyw - x2 * ((r2 * r2 * r2) * jnp.mean(dyw * x2, axis=-1, keepdims=True))
        dx2_o[...] = dx2
        dx2b = dx2.astype(BF16)
        gwout_o[...] += _mm_tn(mixed, dx2b)
        dmix = _mm_nt(dx2b, wo_r[...])
        dm_a, dm_h = dmix[:, :AW], dmix[:, AW:]
        dag_o[...] = (dm_a * (n_a * anw_v) * (sg_a * (1.0 + agv * (1.0 - sg_a)))).astype(BF16)
        dy_a = dm_a * si_a
        dn_a = dy_a * anw_v
        small_o[1:2, 0:AW] += jnp.sum(dy_a * n_a, axis=0, keepdims=True)
        dattn = rs_a * (dn_a - n_a * head_mean_a(dn_a * n_a))
        perm_out(dattn, do1_o, do4_o, do16_o, BF16)
        stats = lse_c + _mm_exact_r(dattn * attn, sel_r[...])
        perm_out(stats, st1_o, st4_o, st16_o, F32)
        dhg_o[...] = (dm_h * (n_h * hnw_v) * (sg_h * (1.0 + hgv * (1.0 - sg_h)))).astype(BF16)
        dy_h = dm_h * si_h
        dn_h = dy_h * hnw_v
        small_o[1:2, AW:] += jnp.sum(dy_h * n_h, axis=0, keepdims=True)
        drec_o[...] = (rs_h * (dn_h - n_h * head_mean_h(dn_h * n_h))).astype(BF16)

        @pl.when(pl.program_id(0) == T // TT - 1)
        def _():
            out = pltpu.make_async_copy(gwout_o, gw_o, out_sem.at[0])
            out.start()
            out.wait()

    tok = lambda w: pl.BlockSpec((TT, w), lambda i: (i, 0))
    d4 = pl.BlockSpec((4, TT // 4, AW), lambda i: (0, i, 0))
    d16 = pl.BlockSpec((16, TT // 16, AW), lambda i: (0, i, 0))
    const = lambda shape: pl.BlockSpec(shape, lambda i: (0,) * len(shape))
    sd = lambda shape, dt: jax.ShapeDtypeStruct(shape, dt)
    c4 = pl.BlockSpec((4, TT // 4, 128), lambda i: (0, i, 0))
    c16 = pl.BlockSpec((16, TT // 16, 128), lambda i: (0, i, 0))
    p3 = lambda w, dt: [sd((T, w), dt), sd((4, T // 4, w), dt), sd((16, T // 16, w), dt)]
    return pl.pallas_call(
        body, name="fwd_out", grid=(T // TT,),
        in_specs=[tok(AW), d4, d16, tok(128), c4, c16, tok(AW), tok(AW), tok(AW), tok(D), tok(D),
                  const((1, AW)), const((1, HW)), const((1, D)), const((D, D)), const((256, 256)),
                  const((128, AW)), const((AW, 128))],
        out_specs=[tok(D)] + [tok(AW), d4, d16] + [tok(128), c4, c16] + [tok(AW)] * 3
        + [pl.BlockSpec(memory_space=pltpu.HBM), const((8, D))],
        out_shape=[sd((T, D), F32)] + p3(AW, BF16) + p3(128, F32)
        + [sd((T, AW), BF16), sd((T, AW), BF16), sd((T, AW), BF16), sd((D, D), F32), sd((8, D), F32)],
        scratch_shapes=[pltpu.VMEM((4, TT, 128), F32)] * 3 + [pltpu.VMEM((D, D), F32),
                        pltpu.SemaphoreType.DMA((1,))],
        compiler_params=_cp(("arbitrary",)),
    )(o1, o4, o16, l1, l4, l16, rec, ag, hg, x, tgt, anw, hnw, fnw, wout_full, gmat, emat, selmat)


def _dproj_build(dq, dk, dv, dag, pos):
    TT = 512

    def body(dq1, dq4, dq16, dk1, dk4, dk16, dv1, dv4, dv16, dag_r, pos_r, dproj_o, scr_b, scr_c):
        def unperm_sum(r1, r4, r16):
            return r1[...] + _unperm_sum(r4, r16, scr_b, scr_c)

        cosf, s1, s2 = _rope_tables(pos_r[...])
        dproj_o[:, 0:512] = _rope_bwd(unperm_sum(dq1, dq4, dq16), cosf, s1, s2).astype(BF16)
        dproj_o[:, 512:1024] = _rope_bwd(unperm_sum(dk1, dk4, dk16), cosf, s1, s2).astype(BF16)
        dproj_o[:, 1024:1536] = unperm_sum(dv1, dv4, dv16).astype(BF16)
        dproj_o[:, 1536:2048] = dag_r[...]

    tok = lambda w: pl.BlockSpec((TT, w), lambda i: (i, 0))
    d4 = pl.BlockSpec((4, TT // 4, AW), lambda i: (0, i, 0))
    d16 = pl.BlockSpec((16, TT // 16, AW), lambda i: (0, i, 0))
    return pl.pallas_call(
        body, name="dproj_build", grid=(T // TT,),
        in_specs=[tok(AW), d4, d16] * 3 + [tok(AW), pl.BlockSpec((1, TT), lambda i: (0, i))],
        out_specs=tok(NCOL // 2),
        out_shape=jax.ShapeDtypeStruct((T, NCOL // 2), BF16),
        scratch_shapes=[pltpu.VMEM((4, TT, 128), F32)] * 2,
        compiler_params=_cp(("parallel",)),
    )(*dq, *dk, *dv, dag, pos)


def _bwd_x(dproj_a, dproj_h, x, dx2, mixw, w_full, rinb, small4, small6, pout_own, pout_rem, raw):
    TT = 256
    NT = T // TT

    def body(dpa_r, dph_r, x_r, dx2_r, mw_r, w_r, rinb_r, s4_r, s6_r, poo_r, por_r, raw_r,
             gx_o, sall_o, fin_o, fout_o, sbuf, v_own, v_rem, vo_own, vo_rem, sin, sout, got_in,
             got_out, v_send, v_got, send_sems, recv_sems, loc_sems, share_send, share_recv, fin_sems, raw_sems,
             hand_send, hand_recv):
        i = pl.program_id(0)
        mx, my, c = lax.axis_index("x"), lax.axis_index("y"), lax.axis_index("c")
        _, rem = _chip_copies(_w_in_piece, rinb_r, rinb_r, v_own, v_rem, send_sems, recv_sems, loc_sems.at[0])
        loc = [pltpu.make_async_copy(raw_r.at[pl.ds(pl.multiple_of(c * 512, 512), 512), :], v_own, loc_sems.at[0])]
        load_theirs = pltpu.make_async_copy(raw_r.at[pl.ds(pl.multiple_of((1 - c) * 512, 512), 512), :], v_send,
                                            raw_sems.at[0])
        hand = pltpu.make_async_remote_copy(src_ref=v_send, dst_ref=v_got, send_sem=hand_send.at[0],
                                            recv_sem=hand_recv.at[0], device_id=(mx, my, 1 - c), device_id_type=MESH)
        loads = [pltpu.make_async_copy(poo_r, vo_own, fin_sems.at[2]),
                 pltpu.make_async_copy(por_r, vo_rem, fin_sems.at[3])]

        @pl.when(i == 0)
        def _():
            sbuf[...] = jnp.zeros_like(sbuf)
            for cp in loc + rem + loads:
                cp.start()
            load_theirs.start()

        @pl.when(i == 1)
        def _():
            load_theirs.wait()
            hand.start()

        dhn = _mm_nt(dpa_r[...], w_r[:, 0:NCOL // 2]) + _mm_nt(dph_r[...], w_r[:, NCOL // 2:NCOL])
        xv = x_r[...]
        r = lax.rsqrt(jnp.mean(xv * xv, axis=-1, keepdims=True) + EPS)
        dxw = dhn * mw_r[...]
        gx_o[...] = dx2_r[...] + r * dxw - xv * ((r * r * r) * jnp.mean(dxw * xv, axis=-1, keepdims=True))
        sbuf[16:17, :] += jnp.sum(dhn * (xv * r), axis=0, keepdims=True)

        @pl.when(i == NT - 1)
        def _():
            sbuf[0:8, :] = s4_r[...]
            sbuf[8:16, :] = s6_r[...]
            sloc, srem = _small_copies(sbuf, sall_o, send_sems, recv_sems, loc_sems.at[1])
            for cp in sloc + srem:
                cp.start()
            for cp in rem:
                cp.wait_recv()
            for cp in rem:
                cp.wait_send()
            for cp in loc:
                cp.wait()
            for cp in loads:
                cp.wait()
            hand.wait_recv()
            hand.wait_send()
            sout[...] = ((vo_own[...] + vo_rem[0].astype(F32)) + vo_rem[1].astype(F32)) + vo_rem[2].astype(F32)
            sin[...] = (((v_own[...] + v_got[...]) + v_rem[0].astype(F32)) + v_rem[1].astype(F32)) + v_rem[2].astype(F32)
            swap = [pltpu.make_async_remote_copy(src_ref=sin, dst_ref=got_in, send_sem=share_send.at[0],
                                                 recv_sem=share_recv.at[0], device_id=(mx, my, 1 - c),
                                                 device_id_type=MESH),
                    pltpu.make_async_remote_copy(src_ref=sout, dst_ref=got_out, send_sem=share_send.at[1],
                                                 recv_sem=share_recv.at[1], device_id=(mx, my, 1 - c),
                                                 device_id_type=MESH)]
            for cp in swap:
                cp.start()
            mine = [pltpu.make_async_copy(sin, fin_o.at[c], fin_sems.at[0]),
                    pltpu.make_async_copy(sout, fout_o.at[c], fin_sems.at[1])]
            for cp in mine:
                cp.start()
            for cp in swap:
                cp.wait_recv()
            theirs = [pltpu.make_async_copy(got_in, fin_o.at[1 - c], fin_sems.at[2]),
                      pltpu.make_async_copy(got_out, fout_o.at[1 - c], fin_sems.at[3])]
            for cp in theirs:
                cp.start()
            for cp in swap:
                cp.wait_send()
            for cp in mine + theirs:
                cp.wait()
            for cp in srem:
                cp.wait_recv()
            for cp in srem:
                cp.wait_send()
            for cp in sloc:
                cp.wait()

    tok = lambda w: pl.BlockSpec((TT, w), lambda i: (i, 0))
    const = lambda shape: pl.BlockSpec(shape, lambda i: (0,) * len(shape))
    hbm = pl.BlockSpec(memory_space=pltpu.HBM)
    return pl.pallas_call(
        body, name="bwd_x", grid=(NT,),
        in_specs=[tok(NCOL // 2), tok(NCOL // 2), tok(D), tok(D), const((1, D)), const((D, NCOL)), hbm,
                  const((8, D)), const((8, D)), hbm, hbm, hbm],
        out_specs=[tok(D), hbm, hbm, hbm],
        out_shape=[jax.ShapeDtypeStruct((T, D), F32),
                   jax.ShapeDtypeStruct((8, 24, D), F32),
                   jax.ShapeDtypeStruct((2, 512, 1024), F32), jax.ShapeDtypeStruct((2, 128, D), F32)],
        scratch_shapes=[pltpu.VMEM((24, D), F32),
                        pltpu.VMEM((512, 1024), F32), pltpu.VMEM((3, 512, 1024), BF16),
                        pltpu.VMEM((128, D), F32), pltpu.VMEM((3, 128, D), BF16),
                        pltpu.VMEM((512, 1024), F32), pltpu.VMEM((128, D), F32),
                        pltpu.VMEM((512, 1024), F32), pltpu.VMEM((128, D), F32),
                        pltpu.VMEM((512, 1024), F32), pltpu.VMEM((512, 1024), F32),
                        pltpu.SemaphoreType.DMA((10,)), pltpu.SemaphoreType.DMA((10,)), pltpu.SemaphoreType.DMA((2,)),
                        pltpu.SemaphoreType.DMA((2,)), pltpu.SemaphoreType.DMA((2,)), pltpu.SemaphoreType.DMA((4,)),
                        pltpu.SemaphoreType.DMA((1,)), pltpu.SemaphoreType.DMA((1,)), pltpu.SemaphoreType.DMA((1,))],
        compiler_params=_cp(("arbitrary",)),
    )(dproj_a, dproj_h, x, dx2, mixw, w_full, rinb, small4, small6, pout_own, pout_rem, raw)


def _grad_w_in(hn, dproj_a, dproj_h, jm_arr):
    TK = 2048
    NK = T // TK

    def block_at(j, jm):
        return lax.rem(jm + 1 + j, 4)

    def body(jm_ref, hnt_r, dpa_r, dph_r, rinb_o, raw_o, acc, rbuf, obufb, send_sems, recv_sems, wb_sems):
        j = pl.program_id(0)
        kk = pl.program_id(1)
        x, y, c = lax.axis_index("x"), lax.axis_index("y"), lax.axis_index("c")
        mine = pl.ds(pl.multiple_of(c * 512, 512), 512)
        theirs = pl.ds(pl.multiple_of((1 - c) * 512, 512), 512)

        def send(jj):
            return pltpu.make_async_remote_copy(
                src_ref=acc.at[jj % 2, theirs, :], dst_ref=rbuf.at[jj], send_sem=send_sems.at[jj],
                recv_sem=recv_sems.at[jj], device_id=(x, y, 1 - c), device_id_type=MESH)

        def writeback(jj):
            cols = pl.ds(pl.multiple_of(block_at(jj, jm_ref[0]) * 1024, 1024), 1024)
            return [pltpu.make_async_copy(obufb.at[jj % 2], rinb_o.at[:, cols], wb_sems.at[2 + jj % 2])]

        def wait_writeback(jj):
            for cp in writeback(jj):
                cp.wait()

        def finalize(jj):
            send(jj).wait_recv()
            obufb[jj % 2] = (acc[jj % 2, mine, :] + rbuf[jj]).astype(BF16)
            for cp in writeback(jj):
                cp.start()

        blk = block_at(j, jm_ref[0])
        prod = _mm(hnt_r[...], jnp.where(blk < 2, dpa_r[...], dph_r[...]))

        @pl.when(kk == 0)
        def _():
            for jj in (2, 3):
                @pl.when(j == jj)
                def _():
                    send(jj - 2).wait_send()
            acc[j % 2] = prod

        @pl.when(kk > 0)
        def _():
            acc[j % 2] += prod

        @pl.when(kk == NK - 1)
        def _():
            for jj in range(4):
                @pl.when(j == jj)
                def _():
                    if jj < 3:
                        send(jj).start()
                    if jj in (1, 2):
                        finalize(jj - 1)
                    if jj == 3:
                        raw = pltpu.make_async_copy(acc.at[1], raw_o, wb_sems.at[4])
                        raw.start()
                        wait_writeback(0)
                        finalize(2)
                        wait_writeback(1)
                        wait_writeback(2)
                        raw.wait()
                        send(2).wait_send()

    def used(is_mine, kk, col):
        return jnp.where(is_mine, kk, 0), jnp.where(is_mine, col, 0)

    hbm = pl.BlockSpec(memory_space=pltpu.HBM)
    grid_spec = pltpu.PrefetchScalarGridSpec(
        num_scalar_prefetch=1, grid=(4, NK),
        in_specs=[pl.BlockSpec((D, TK), lambda j, kk, jm_ref: (0, kk)),
                  pl.BlockSpec((TK, 1024), lambda j, kk, jm_ref: used(
                      block_at(j, jm_ref[0]) < 2, kk, block_at(j, jm_ref[0]))),
                  pl.BlockSpec((TK, 1024), lambda j, kk, jm_ref: used(
                      block_at(j, jm_ref[0]) >= 2, kk, block_at(j, jm_ref[0]) - 2))],
        out_specs=[hbm, hbm],
        scratch_shapes=[pltpu.VMEM((2, D, 1024), F32), pltpu.VMEM((3, 512, 1024), F32),
                        pltpu.VMEM((2, 512, 1024), BF16),
                        pltpu.SemaphoreType.DMA((3,)), pltpu.SemaphoreType.DMA((3,)), pltpu.SemaphoreType.DMA((5,))])
    return pl.pallas_call(
        body, name="grad_w_in", grid_spec=grid_spec,
        out_shape=[jax.ShapeDtypeStruct((512, NCOL), BF16), jax.ShapeDtypeStruct((D, 1024), F32)],
        compiler_params=_cp(("arbitrary", "arbitrary")),
    )(jm_arr, hn, dproj_a, dproj_h)


def _w_in_piece(ref, j):
    return ref.at[:, pl.ds(j * 1024, 1024)]


def _w_out_piece(ref, j):
    return ref.at[pl.ds(j * 128, 128), :]


def _chip_copies(piece, src_r, srcb_r, own_o, rem_o, send_sems, recv_sems, loc_sem):
    x, y, c = lax.axis_index("x"), lax.axis_index("y"), lax.axis_index("c")
    chips = [(1 - x, y), (x, 1 - y), (1 - x, 1 - y)]
    loc = [pltpu.make_async_copy(piece(src_r, 2 * x + y), own_o, loc_sem)]
    rem = [pltpu.make_async_remote_copy(
        src_ref=piece(srcb_r, 2 * px + py), dst_ref=rem_o.at[k], send_sem=send_sems.at[k],
        recv_sem=recv_sems.at[k], device_id=(px, py, c), device_id_type=MESH) for k, (px, py) in enumerate(chips)]
    return loc, rem


def _small_copies(small_r, sall_o, send_sems, recv_sems, loc_sem):
    x, y, c = lax.axis_index("x"), lax.axis_index("y"), lax.axis_index("c")
    me = 4 * x + 2 * y + c
    loc = [pltpu.make_async_copy(small_r, sall_o.at[me], loc_sem)]
    rem = []
    k = 3
    for fx in range(2):
        for fy in range(2):
            for fc in range(2):
                if fx or fy or fc:
                    peer = (1 - x if fx else x, 1 - y if fy else y, 1 - c if fc else c)
                    rem.append(pltpu.make_async_remote_copy(
                        src_ref=small_r, dst_ref=sall_o.at[me], send_sem=send_sems.at[k],
                        recv_sem=recv_sems.at[k], device_id=peer, device_id_type=MESH))
                    k += 1
    return loc, rem


def _adamw_math(w, g, m, v):
    m = B1 * m + (1.0 - B1) * g
    v = B2 * v + (1.0 - B2) * (g * g)
    m_hat = m / (1.0 - B1 ** STEP)
    v_hat = v / (1.0 - B2 ** STEP)
    delta = -LR * (m_hat / (jnp.sqrt(v_hat) + AEPS) + WD * w)
    return delta, m, v


def _adamw(big_in, big_out, sall, params):
    def body(*refs):
        wi, gi, mi, vi, wo, go, mo, vo, sall_r = refs[:9]
        ins = refs[9:24]
        di_o, mi_o, vi_o, do_o, mo_o, vo_o = refs[24:30]
        outs = refs[30:]
        d, mm, vv = _adamw_math(wi[...], gi[...], mi[...], vi[...])
        di_o[...] = d
        mi_o[...] = mm
        vi_o[...] = vv

        @pl.when(pl.program_id(0) == 0)
        def _():
            d, mm, vv = _adamw_math(wo[...], go[...], mo[...], vo[...])
            do_o[...] = d
            mo_o[...] = mm
            vo_o[...] = vv
            tot = sall_r[0]
            for dv in range(1, 8):
                tot = tot + sall_r[dv]
            grads = [tot[16:17, :], tot[1:2, 0:AW], tot[1:2, AW:], tot[8:10, 0:HW], tot[0:1, :]]
            outs[0][...] = tot[2:3, 0:1]
            for p in range(5):
                w_r, m_r, v_r = ins[3 * p:3 * p + 3]
                g = grads[p]
                d, mm, vv = _adamw_math(w_r[...], g, m_r[...], v_r[...])
                outs[1 + 4 * p][...] = g
                outs[2 + 4 * p][...] = d
                outs[3 + 4 * p][...] = mm
                outs[4 + 4 * p][...] = vv

    flat = [a for p in params for a in p]
    shapes = [jax.ShapeDtypeStruct((D, 1024), F32)] * 3 + [jax.ShapeDtypeStruct((256, D), F32)] * 3
    shapes += [jax.ShapeDtypeStruct((1, 1), F32)]
    for p in params:
        shapes += [jax.ShapeDtypeStruct(p[0].shape, F32)] * 4
    vm = pl.BlockSpec(memory_space=pltpu.VMEM)
    rows = pl.BlockSpec((512, 1024), lambda i: (i, 0))
    whole = pl.BlockSpec((256, D), lambda i: (0, 0))
    return pl.pallas_call(
        body, name="adamw", grid=(2,),
        in_specs=[rows] * 4 + [whole] * 4 + [vm] * 16, out_specs=[rows] * 3 + [whole] * 3 + [vm] * 21,
        out_shape=shapes,
        compiler_params=_cp(("arbitrary",)),
    )(*big_in, *big_out, sall, *flat)


def kernel(x, positions, w_in, w_out, mix_norm_w, attn_out_norm_w, hgrn_out_norm_w, hgrn_lb_raw, final_norm_w, loss_target, m_w_in, m_w_out, m_mix_norm_w, m_attn_out_norm_w, m_hgrn_out_norm_w, m_hgrn_lb_raw, m_final_norm_w, v_w_in, v_w_out, v_mix_norm_w, v_attn_out_norm_w, v_hgrn_out_norm_w, v_hgrn_lb_raw, v_final_norm_w):
    xs = x.reshape(T, D)
    tgt = loss_target.reshape(T, D)
    pos = positions.reshape(1, T)
    fnw = final_norm_w.reshape(1, D)

    ti = np.arange(TH)
    tri_np = ((ti[:, None] // CHUNK == ti[None, :] // CHUNK) & (ti[None, :] <= ti[:, None])).astype(np.float32)
    tri = jnp.asarray(tri_np, BF16)
    trit = jnp.asarray(tri_np.T, BF16)
    hi_ = np.arange(AW) // HEAD
    gmat = jnp.asarray((hi_[:256, None] == hi_[None, :256]).astype(np.float32) / HEAD, BF16)
    emat_np = (np.arange(128)[:, None] == hi_[None, :]).astype(np.float32)
    sel_np = (8 + hi_[:, None] == np.arange(128)[None, :]).astype(np.float32)
    emat = jnp.asarray(emat_np, BF16)
    selmat = jnp.asarray(sel_np, BF16)

    jm_arr = (2 * lax.axis_index("x") + lax.axis_index("y")).astype(jnp.int32).reshape(1)
    (hn, q1, k1, v1, q4, k4, v4, q16, k16, v16, ag, hq, hf, hi, hg, w_full, wout4) = _fwd_in(
        xs, pos, mix_norm_w, w_in.reshape(D, 1024), w_out.reshape(256, D), jm_arr)
    wout_full = wout4.reshape(D, D)
    flat = lambda a: a.reshape(T, AW)
    o1, l1 = _attn_fwd(q1, k1, v1, T // BLK, "attn_fwd_d1")
    o4, l4 = _attn_fwd(flat(q4), flat(k4), flat(v4), T // 4 // BLK, "attn_fwd_d4")
    o16, l16 = _attn_fwd(flat(q16), flat(k16), flat(v16), T // 16 // BLK, "attn_fwd_d16")
    rec, sall = _hgrn_fwd(hq, hf, hi, hgrn_lb_raw, tri)

    (dx2, do1, do4, do16, st1, st4, st16, drec, dag, dhg, gw, small4) = _fwd_out(
        o1, o4.reshape(4, T // 4, AW), o16.reshape(16, T // 16, AW),
        l1, l4.reshape(4, T // 4, 128), l16.reshape(16, T // 16, 128),
        rec, ag, hg, xs, tgt, attn_out_norm_w, hgrn_out_norm_w, fnw, wout_full, gmat, emat, selmat)

    fst = lambda a: a.reshape(T, 128)
    dq1, dk1, dv1 = _attn_bwd(q1, k1, v1, do1, st1, T // BLK, "attn_bwd_d1")
    dq4, dk4, dv4 = _attn_bwd(flat(q4), flat(k4), flat(v4), flat(do4), fst(st4), T // 4 // BLK, "attn_bwd_d4")
    dq16, dk16, dv16 = _attn_bwd(flat(q16), flat(k16), flat(v16), flat(do16), fst(st16), T // 16 // BLK,
                                 "attn_bwd_d16")
    dproj_h, small6, pout_own, pout_rem = _hgrn_bwd(hq, hf, hi, hgrn_lb_raw, tri, trit, drec, sall, dhg, gw)

    r4 = lambda a: a.reshape(4, T // 4, AW)
    r16 = lambda a: a.reshape(16, T // 16, AW)
    dproj_a = _dproj_build((dq1, r4(dq4), r16(dq16)), (dk1, r4(dk4), r16(dk16)), (dv1, r4(dv4), r16(dv16)),
                           dag, pos)
    rinb, raw = _grad_w_in(hn, dproj_a, dproj_h, jm_arr)
    gx, small_all, fin, fout = _bwd_x(dproj_a, dproj_h, xs, dx2, mix_norm_w, w_full, rinb,
                                      small4, small6, pout_own, pout_rem, raw)
    g_w_in = fin.reshape(D, 1024)
    g_w_out = fout.reshape(256, D)

    params = [(mix_norm_w, m_mix_norm_w, v_mix_norm_w),
              (attn_out_norm_w, m_attn_out_norm_w, v_attn_out_norm_w),
              (hgrn_out_norm_w, m_hgrn_out_norm_w, v_hgrn_out_norm_w),
              (hgrn_lb_raw, m_hgrn_lb_raw, v_hgrn_lb_raw),
              (fnw, m_final_norm_w.reshape(1, D), v_final_norm_w.reshape(1, D))]
    d_in, nm_in, nv_in, d_out, nm_out, nv_out, *so = _adamw(
        (w_in.reshape(D, 1024), g_w_in, m_w_in.reshape(D, 1024), v_w_in.reshape(D, 1024)),
        (w_out.reshape(256, D), g_w_out, m_w_out.reshape(256, D), v_w_out.reshape(256, D)), small_all, params)
    loss = so[0].reshape(())
    g_s = [so[1 + 4 * p] for p in range(5)]
    d_s = [so[2 + 4 * p] for p in range(5)]
    m_s = [so[3 + 4 * p] for p in range(5)]
    v_s = [so[4 + 4 * p] for p in range(5)]
    for lst in (g_s, d_s, m_s, v_s):
        lst[4] = lst[4].reshape(D)

    return (loss, gx.reshape(1, T, D),
            g_w_in.reshape(1, D, 1024), g_w_out.reshape(1, 256, D), *g_s,
            d_in.reshape(1, D, 1024), d_out.reshape(1, 256, D), *d_s,
            nm_in.reshape(1, D, 1024), nm_out.reshape(1, 256, D), *m_s,
            nv_in.reshape(1, D, 1024), nv_out.reshape(1, 256, D), *v_s)
```

```python
import functools

import numpy as np
import jax
import jax.numpy as jnp
from jax import lax
from jax.experimental import pallas as pl
from jax.experimental.pallas import tpu as pltpu

F32 = jnp.float32
BF16 = jnp.bfloat16

T = 4096
D = 1024
AW = 512
HW = 512
NCOL = 4096
HEAD = 64
BLK = 128
CHUNK = 64
EPS = 1e-6
SCALE = HEAD ** -0.5
NEG = -1e30
ROPE_THETA = 500000.0
INV_FREQ = [float(v) for v in
            (np.float32(ROPE_THETA) ** (-(np.arange(8, dtype=np.float32)) * np.float32(0.125)))]
LR, B1, B2, AEPS, WD, STEP = 0.001, 0.9, 0.999, 1e-08, 0.01, 10
VMEM_LIMIT = 63 * 1024 * 1024
MESH = pl.DeviceIdType.MESH


def _cp(sem=None, **kw):
    return pltpu.CompilerParams(dimension_semantics=sem, vmem_limit_bytes=VMEM_LIMIT, **kw)


def _mm(a, b):
    return jnp.dot(a, b, preferred_element_type=F32)


def _mm_nt(a, b):
    return lax.dot_general(a, b, (((1,), (1,)), ((), ())), preferred_element_type=F32)


def _mm_tn(a, b):
    return lax.dot_general(a, b, (((0,), (0,)), ((), ())), preferred_element_type=F32)


def _mm_exact_l(mat_bf, x):
    h = x.astype(BF16)
    l = (x - h.astype(F32)).astype(BF16)
    return _mm(mat_bf, h) + _mm(mat_bf, l)


def _mm_exact_r(x, mat_bf):
    h = x.astype(BF16)
    l = (x - h.astype(F32)).astype(BF16)
    return _mm(h, mat_bf) + _mm(l, mat_bf)


def _sigmoid(x):
    return 0.5 * jnp.tanh(0.5 * x) + 0.5


def _rope_tables(pos):
    lane = lax.broadcasted_iota(jnp.int32, (1, 128), 1)
    jl = lane & 63
    fi = jl & 7
    inv = jnp.zeros((1, 128), F32)
    for kk in range(8):
        inv = jnp.where(fi == kk, INV_FREQ[kk], inv)
    ang = jnp.broadcast_to(pos.astype(F32), (128, pos.shape[1])).T * inv
    c = jnp.cos(ang)
    s = jnp.sin(ang)
    cosf = jnp.where(jl < 16, c, 1.0)
    s1 = jnp.where(jl < 8, -s, 0.0)
    s2 = jnp.where((jl >= 8) & (jl < 16), s, 0.0)
    return cosf, s1, s2


def _rope(t, cosf, s1, s2):
    parts = []
    for ci in range(t.shape[1] // 128):
        tc = t[:, ci * 128:(ci + 1) * 128]
        parts.append(tc * cosf + pltpu.roll(tc, 120, 1) * s1 + pltpu.roll(tc, 8, 1) * s2)
    return jnp.concatenate(parts, axis=1)


def _rope_bwd(g, cosf, s1, s2):
    parts = []
    for ci in range(g.shape[1] // 128):
        gc = g[:, ci * 128:(ci + 1) * 128]
        parts.append(gc * cosf + pltpu.roll(gc * s1, 8, 1) + pltpu.roll(gc * s2, 120, 1))
    return jnp.concatenate(parts, axis=1)


def _perm_store(val, scr, scr2, o1, o4, o16, dt):
    n = val.shape[0]
    q = n // 4
    o1[...] = val.astype(dt)
    for ci in range(val.shape[1] // 128):
        cs = slice(ci * 128, (ci + 1) * 128)
        scr[ci] = val[:, cs]
        for r4 in range(4):
            part = scr[ci, pl.ds(r4, q, stride=4), :]
            o4[r4, :, cs] = part.astype(dt)
            scr2[ci, r4 * q:(r4 + 1) * q, :] = part
        for r4 in range(4):
            for b in range(4):
                o16[r4 + 4 * b, :, cs] = scr2[ci, pl.ds(r4 * q + b, q // 4, stride=4), :].astype(dt)


def _unperm_load(r4, r16, scr_a, scr_b, scr_c):
    n = scr_a.shape[1]
    q = n // 4
    nc = r4.shape[-1] // 128
    for ci in range(nc):
        cs = slice(ci * 128, (ci + 1) * 128)
        for rr in range(4):
            scr_a[ci, pl.ds(rr, q, stride=4), :] = r4[rr, :, cs].astype(F32)
        for rr in range(4):
            for b in range(4):
                scr_c[ci, pl.ds(rr * q + b, q // 4, stride=4), :] = r16[rr + 4 * b, :, cs].astype(F32)
        for rr in range(4):
            scr_b[ci, pl.ds(rr, q, stride=4), :] = scr_c[ci, rr * q:(rr + 1) * q, :]
    return (jnp.concatenate([scr_a[ci] for ci in range(nc)], axis=1),
            jnp.concatenate([scr_b[ci] for ci in range(nc)], axis=1))


def _unperm_sum(r4, r16, scr_b, scr_c):
    n = scr_b.shape[1]
    q = n // 4
    nc = r4.shape[-1] // 128
    for ci in range(nc):
        cs = slice(ci * 128, (ci + 1) * 128)
        for rr in range(4):
            for b in range(4):
                scr_c[ci, pl.ds(rr * q + b, q // 4, stride=4), :] = r16[rr + 4 * b, :, cs].astype(F32)
        for rr in range(4):
            scr_b[ci, pl.ds(rr, q, stride=4), :] = scr_c[ci, rr * q:(rr + 1) * q, :] + r4[rr, :, cs].astype(F32)
    return jnp.concatenate([scr_b[ci] for ci in range(nc)], axis=1)


def _fwd_in(x, pos, mixw, w_in, w_out, jm_arr):
    TT = 512
    NT = T // TT

    def body(jm_ref, x_ref, pos_ref, mw_ref, win_ref, wout_ref,
             hnt_ref, q1, k1, v1, q4, k4, v4, q16, k16, v16, ag, hq, hf, hi, hg, wfull_o, woutfull_o,
             wbuf, wobuf, hn_all, scr, scr2, stage, send_sems, recv_sems, loc_sems):
        s = pl.program_id(0)
        i = pl.program_id(1)
        mx, my, c = lax.axis_index("x"), lax.axis_index("y"), lax.axis_index("c")
        me, sibling = (mx, my, c), (mx, my, 1 - c)
        chips = [(mx, 1 - my), (1 - mx, my), (1 - mx, 1 - my)]
        jm = 2 * mx + my
        rows_in = [pl.ds(pl.multiple_of(h * 512, 512), 512) for h in (c, 1 - c)]
        rows_out = [pl.ds(pl.multiple_of(h * 128, 128), 128) for h in (c, 1 - c)]

        def blk(k):
            return lax.bitwise_xor(jm, k + 1)

        def rc(n, ref, to):
            return pltpu.make_async_remote_copy(src_ref=ref, dst_ref=ref, send_sem=send_sems.at[n],
                                                recv_sem=recv_sems.at[n], device_id=to, device_id_type=MESH)

        halves = [pl.ds(0, 512), pl.ds(512, 512)]
        send_in = lambda k, h: rc(12 + 2 * k + h, wbuf.at[jm, rows_in[0], halves[h]], (*chips[k], c))
        got_in = lambda k, h: rc(12 + 2 * k + h, wbuf.at[blk(k), rows_in[0], halves[h]], me)
        relay = lambda h: rc(16 + h, wbuf.at[blk(h), rows_in[0], halves[h]], (*chips[1 - h], c))
        got_relay = lambda h: rc(16 + h, wbuf.at[blk(2), rows_in[0], halves[h]], me)
        send_out = lambda k: rc(3 + k, wobuf.at[jm, rows_out[0], :], (*chips[k], c))
        got_out = lambda k: rc(3 + k, wobuf.at[blk(k), rows_out[0], :], me)
        pass_in = lambda k: rc(6 + k, wbuf.at[blk(k), rows_in[0], :], sibling)
        pass_out = lambda k: rc(9 + k, wobuf.at[blk(k), rows_out[0], :], sibling)
        passed_in = lambda k: rc(6 + k, wbuf.at[blk(k), rows_in[1], :], me)
        passed_out = lambda k: rc(9 + k, wobuf.at[blk(k), rows_out[1], :], me)

        def keep(j, n):
            return pltpu.make_async_copy(wbuf.at[j], wfull_o.at[:, pl.ds(j * 1024, 1024)], loc_sems.at[n])

        @pl.when((s == 0) & (i == 0))
        def _():
            chunk = [pl.ds(pl.multiple_of(lax.rem(p + 2 * c, 4) * 256, 256), 256) for p in range(4)]
            loads = [pltpu.make_async_copy(win_ref.at[chunk[p], :] if p < 4 else wout_ref, stage.at[p % 2],
                                           loc_sems.at[4 + p % 2]) for p in range(5)]
            loads[0].start()
            for p in range(5):
                if p < 4:
                    loads[p + 1].start()
                loads[p].wait()
                if p < 4:
                    wbuf[jm, chunk[p], :] = stage[p % 2].astype(BF16)
                else:
                    wobuf[jm] = stage[p % 2].astype(BF16)
                if p == 1:
                    for k in range(2):
                        for h in range(2):
                            send_in(k, h).start()
            keep(jm, 0).start()

        @pl.when((s == 0) & (i == NT - 1))
        def _():
            for kk in range(2):
                for h in range(2):
                    got_in(kk, h).wait_recv()
            relay(0).start()
            relay(1).start()
            pass_in(0).start()
            pass_in(1).start()
            passed_in(0).wait_recv()
            keep(blk(0), 1).start()

        @pl.when((s == 1) & (i == NT - 1))
        def _():
            got_relay(0).wait_recv()
            got_relay(1).wait_recv()
            pass_in(2).start()

        @pl.when((s == 2) & (i == 0))
        def _():
            for k in (1, 2):
                passed_in(k).wait_recv()
                keep(blk(k), k + 1).start()
            for kk in range(3):
                send_out(kk).start()

        @pl.when((s == 2) & (i == NT - 2))
        def _():
            for k in range(3):
                got_out(k).wait_recv()
                pass_out(k).start()

        whole_out = pltpu.make_async_copy(wobuf, woutfull_o, loc_sems.at[4])

        @pl.when((s == 2) & (i == NT - 1))
        def _():
            for k in range(3):
                passed_out(k).wait_recv()
            whole_out.start()

        tile = pl.ds(pl.multiple_of(i * TT, TT), TT)

        @pl.when(s == 0)
        def _():
            xv = x_ref[...]
            r = lax.rsqrt(jnp.mean(xv * xv, axis=-1, keepdims=True) + EPS)
            hnf = (xv * r) * mw_ref[...]
            hn_all[tile, :] = hnf.astype(BF16)
            hnt_ref[...] = hnf.T.astype(BF16)

        def project(jj):
            hn = hn_all[tile, :]
            lo = _mm(hn, wbuf[jj, :, 0:512])
            hi_cols = _mm(hn, wbuf[jj, :, 512:1024])
            if jj == 0:
                cosf, s1, s2 = _rope_tables(pos_ref[...])
                _perm_store(_rope(lo, cosf, s1, s2) * SCALE, scr, scr2, q1, q4, q16, BF16)
                _perm_store(_rope(hi_cols, cosf, s1, s2), scr, scr2, k1, k4, k16, BF16)
            elif jj == 1:
                _perm_store(lo, scr, scr2, v1, v4, v16, BF16)
                ag[...] = hi_cols.astype(BF16)
            elif jj == 2:
                hq[...] = lo.astype(BF16)
                hf[...] = hi_cols.astype(BF16)
            else:
                hi[...] = lo.astype(BF16)
                hg[...] = hi_cols.astype(BF16)

        def project_block(j):
            for jj in range(4):
                pl.when(j == jj)(functools.partial(project, jj))

        @pl.when(s < 2)
        def _():
            project_block(lax.bitwise_xor(jm, s))

        @pl.when(s == 2)
        def _():
            project_block(lax.bitwise_xor(jm, 2))
            project_block(lax.bitwise_xor(jm, 3))

        @pl.when((s == 2) & (i == NT - 1))
        def _():
            for h in range(2):
                relay(h).wait_send()
                for k in range(2):
                    send_in(k, h).wait_send()
            for k in range(3):
                send_out(k).wait_send()
                pass_in(k).wait_send()
                pass_out(k).wait_send()
            keep(jm, 0).wait()
            for k in range(3):
                keep(blk(k), k + 1).wait()
            whole_out.wait()

    def at_stage_of(jb):
        def index(s, i, jm_ref):
            sa = jnp.minimum(lax.bitwise_xor(jm_ref[0], jb), 2)
            return jnp.where(s < sa, 0, jnp.where(s == sa, i, NT - 1))
        return index

    tok = lambda w, jb: pl.BlockSpec((TT, w), lambda s, i, jm_ref: (at_stage_of(jb)(s, i, jm_ref), 0))
    d4 = lambda jb: pl.BlockSpec((4, TT // 4, AW), lambda s, i, jm_ref: (0, at_stage_of(jb)(s, i, jm_ref), 0))
    d16 = lambda jb: pl.BlockSpec((16, TT // 16, AW), lambda s, i, jm_ref: (0, at_stage_of(jb)(s, i, jm_ref), 0))
    hbm = pl.BlockSpec(memory_space=pltpu.HBM)
    sd = lambda shape, dt: jax.ShapeDtypeStruct(shape, dt)
    in_own_stage = lambda s, i: jnp.where(s == 0, i, NT - 1)
    grid_spec = pltpu.PrefetchScalarGridSpec(
        num_scalar_prefetch=1, grid=(3, NT),
        in_specs=[pl.BlockSpec((TT, D), lambda s, i, jm_ref: (in_own_stage(s, i), 0)),
                  pl.BlockSpec((1, TT), lambda s, i, jm_ref: (0, i)),
                  pl.BlockSpec((1, D), lambda s, i, jm_ref: (0, 0)), hbm, hbm],
        out_specs=[pl.BlockSpec((D, TT), lambda s, i, jm_ref: (0, in_own_stage(s, i))),
                   tok(AW, 0), tok(AW, 0), tok(AW, 1), d4(0), d4(0), d4(1), d16(0), d16(0), d16(1),
                   tok(AW, 1), tok(AW, 2), tok(AW, 2), tok(AW, 3), tok(AW, 3), hbm, hbm],
        scratch_shapes=[pltpu.VMEM((4, D, 1024), BF16), pltpu.VMEM((4, 256, D), BF16), pltpu.VMEM((T, D), BF16),
                        pltpu.VMEM((4, TT, 128), F32), pltpu.VMEM((4, TT, 128), F32), pltpu.VMEM((2, 256, 1024), F32),
                        pltpu.SemaphoreType.DMA((18,)),
                        pltpu.SemaphoreType.DMA((18,)), pltpu.SemaphoreType.DMA((6,))])
    return pl.pallas_call(
        body, name="fwd_in", grid_spec=grid_spec,
        out_shape=[sd((D, T), BF16)] + [sd((T, AW), BF16)] * 3 + [sd((4, T // 4, AW), BF16)] * 3
        + [sd((16, T // 16, AW), BF16)] * 3
        + [sd((T, AW), BF16)] * 5 + [sd((D, NCOL), BF16), sd((4, 256, D), BF16)],
        compiler_params=_cp(("arbitrary", "arbitrary")),
    )(jm_arr, x, pos, mixw, w_in, w_out)


def _band_mask(key_axis, nkeys=2 * BLK):
    shape = (nkeys, 2 * BLK) if key_axis == 0 else (2 * BLK, nkeys)
    kj = lax.broadcasted_iota(jnp.int32, shape, key_axis)
    qi = lax.broadcasted_iota(jnp.int32, shape, 1 - key_axis) & (BLK - 1)
    return (kj >= qi) & (kj <= qi + BLK), kj, qi


def _stack_heads(t2, in_a):
    z = jnp.zeros_like(t2)
    return jnp.concatenate([jnp.where(in_a[0], t2, z), jnp.where(in_a[1], t2, z)], axis=0)


def _attn_fwd(q, k, v, nb, name):
    n = 8
    CH = n * BLK
    halo = nb > n

    def body(*refs):
        if halo:
            q_ref, k_ref, v_ref, kp_ref, vp_ref, o_ref, lse_ref = refs
        else:
            q_ref, k_ref, v_ref, o_ref, lse_ref = refs
        lane = lax.broadcasted_iota(jnp.int32, (1, 128), 1)
        in_a = [lane < HEAD, lane >= HEAD]
        band, kj, _ = _band_mask(1)
        thr0 = jnp.where((n * pl.program_id(0)) % nb == 0, BLK, 0) if halo else BLK
        mask0 = band & (kj >= thr0)
        mask_first = band & (kj >= BLK)
        for b in range(n):
            rs = slice(b * BLK, (b + 1) * BLK)
            stat = jnp.zeros((BLK, 128), F32)
            for hp in range(4):
                cs = slice(hp * 128, (hp + 1) * 128)
                q2s = _stack_heads(q_ref[rs, cs], in_a)
                if b == 0:
                    kprev = kp_ref[:, cs] if halo else k_ref[rs, cs]
                    vprev = vp_ref[:, cs] if halo else v_ref[rs, cs]
                    kk = jnp.concatenate([kprev, k_ref[rs, cs]], axis=0)
                    vv = jnp.concatenate([vprev, v_ref[rs, cs]], axis=0)
                    mask = mask0
                else:
                    kk = k_ref[(b - 1) * BLK:(b + 1) * BLK, cs]
                    vv = v_ref[(b - 1) * BLK:(b + 1) * BLK, cs]
                    mask = mask_first if b % nb == 0 else band
                s = jnp.where(mask, _mm_nt(q2s, kk), NEG)
                m = jnp.max(s, axis=-1, keepdims=True)
                p = jnp.exp(s - m)
                l = jnp.sum(p, axis=-1, keepdims=True)
                o = _mm(p.astype(BF16), vv) / l
                lse = m + jnp.log(l)
                o_ref[rs, cs] = jnp.where(in_a[0], o[:BLK], o[BLK:]).astype(BF16)
                stat = jnp.where(lane == 2 * hp, lse[:BLK], stat)
                stat = jnp.where(lane == 2 * hp + 1, lse[BLK:], stat)
            lse_ref[rs, :] = stat

    cur = pl.BlockSpec((CH, AW), lambda i: (i, 0))
    prev = pl.BlockSpec((BLK, AW), lambda i: (jnp.maximum(n * i - 1, 0), 0))
    return pl.pallas_call(
        body, name=name, grid=(T // CH,),
        in_specs=[cur, cur, cur] + ([prev, prev] if halo else []),
        out_specs=[cur, pl.BlockSpec((CH, 128), lambda i: (i, 0))],
        out_shape=[jax.ShapeDtypeStruct((T, AW), BF16), jax.ShapeDtypeStruct((T, 128), F32)],
        compiler_params=_cp(("parallel",)),
    )(*((q, k, v) + ((k, v) if halo else ())))


def _attn_bwd(q, k, v, do, st, nb, name):
    n = 8
    CH = n * BLK
    NBLK = T // BLK
    halo = nb > n

    def body(*refs):
        if halo:
            (q_ref, k_ref, v_ref, do_ref, st_ref, kp_ref, vp_ref, qn_ref, don_ref, stn_ref,
             dq_ref, dk_ref, dv_ref) = refs
        else:
            q_ref, k_ref, v_ref, do_ref, st_ref, dq_ref, dk_ref, dv_ref = refs
        i = pl.program_id(0)
        lane = lax.broadcasted_iota(jnp.int32, (1, 128), 1)
        in_a = [lane < HEAD, lane >= HEAD]
        band, kj, _ = _band_mask(0)
        thr0 = jnp.where((n * i) % nb == 0, BLK, 0) if halo else BLK
        mask0 = band & (kj >= thr0)
        mask_first = band & (kj >= BLK)

        def stat_rows(st_t, hp):
            lse_r = jnp.concatenate([st_t[2 * hp:2 * hp + 1, :], st_t[2 * hp + 1:2 * hp + 2, :]], axis=1)
            dl_r = jnp.concatenate([st_t[8 + 2 * hp:9 + 2 * hp, :], st_t[9 + 2 * hp:10 + 2 * hp, :]], axis=1)
            return lse_r, dl_r

        st_t = [st_ref[b * BLK:(b + 1) * BLK, :].T for b in range(n)]
        if halo:
            nxt_thr = jnp.where((n * i + n) % nb == 0, 2 * BLK, 0)
            _, kj1, qi1 = _band_mask(0, BLK)
            mask_next = kj1 >= qi1 + nxt_thr
            stn_t = stn_ref[...].T

        for hp in range(4):
            cs = slice(hp * 128, (hp + 1) * 128)
            kb = [k_ref[b * BLK:(b + 1) * BLK, cs] for b in range(n)]
            vb = [v_ref[b * BLK:(b + 1) * BLK, cs] for b in range(n)]
            dk_acc = [jnp.zeros((BLK, 128), F32) for _ in range(n)]
            dv_acc = [jnp.zeros((BLK, 128), F32) for _ in range(n)]
            for b in range(n):
                rs = slice(b * BLK, (b + 1) * BLK)
                q2s = _stack_heads(q_ref[rs, cs], in_a)
                do2s = _stack_heads(do_ref[rs, cs], in_a)
                if b == 0:
                    kprev = kp_ref[:, cs] if halo else kb[0]
                    vprev = vp_ref[:, cs] if halo else vb[0]
                    mask = mask0
                else:
                    kprev, vprev, mask = kb[b - 1], vb[b - 1], (mask_first if b % nb == 0 else band)
                kk = jnp.concatenate([kprev, kb[b]], axis=0)
                vv = jnp.concatenate([vprev, vb[b]], axis=0)
                lse_r, dl_r = stat_rows(st_t[b], hp)
                s_t = jnp.where(mask, _mm_nt(kk, q2s), NEG)
                p_t = jnp.exp(s_t - lse_r)
                ds_t = (p_t * (_mm_nt(vv, do2s) - dl_r)).astype(BF16)
                dkk = _mm(ds_t, q2s)
                dvv = _mm(p_t.astype(BF16), do2s)
                dqs = _mm_tn(ds_t, kk) * SCALE
                dq_ref[rs, cs] = jnp.where(in_a[0], dqs[:BLK], dqs[BLK:]).astype(BF16)
                dk_acc[b] += dkk[BLK:]
                dv_acc[b] += dvv[BLK:]
                if b > 0:
                    dk_acc[b - 1] += dkk[:BLK]
                    dv_acc[b - 1] += dvv[:BLK]
            if halo:
                q2s = _stack_heads(qn_ref[:, cs], in_a)
                do2s = _stack_heads(don_ref[:, cs], in_a)
                lse_r, dl_r = stat_rows(stn_t, hp)
                s_t = jnp.where(mask_next, _mm_nt(kb[n - 1], q2s), NEG)
                p_t = jnp.exp(s_t - lse_r)
                ds_t = (p_t * (_mm_nt(vb[n - 1], do2s) - dl_r)).astype(BF16)
                dk_acc[n - 1] += _mm(ds_t, q2s)
                dv_acc[n - 1] += _mm(p_t.astype(BF16), do2s)
            for b in range(n):
                dk_ref[b * BLK:(b + 1) * BLK, cs] = dk_acc[b].astype(BF16)
                dv_ref[b * BLK:(b + 1) * BLK, cs] = dv_acc[b].astype(BF16)

    cur = pl.BlockSpec((CH, AW), lambda i: (i, 0))
    cur_st = pl.BlockSpec((CH, 128), lambda i: (i, 0))
    prev = pl.BlockSpec((BLK, AW), lambda i: (jnp.maximum(n * i - 1, 0), 0))
    nxt = pl.BlockSpec((BLK, AW), lambda i: (jnp.minimum(n * i + n, NBLK - 1), 0))
    nxt_st = pl.BlockSpec((BLK, 128), lambda i: (jnp.minimum(n * i + n, NBLK - 1), 0))
    ins = [cur] * 4 + [cur_st] + ([prev, prev, nxt, nxt, nxt_st] if halo else [])
    args = (q, k, v, do, st) + ((k, v, q, do, st) if halo else ())
    return pl.pallas_call(
        body, name=name, grid=(T // CH,),
        in_specs=ins,
        out_specs=[cur] * 3,
        out_shape=[jax.ShapeDtypeStruct((T, AW), BF16)] * 3,
        compiler_params=_cp(("parallel",)),
    )(*args)


TH = 256
NCH = TH // CHUNK


def _hgrn_common(hq_ref, hf_ref, lbr_ref, tri_ref):
    r0 = lbr_ref[0:1, :]
    r1 = lbr_ref[1:2, :]
    mx = jnp.maximum(r0, r1)
    e0 = jnp.exp(r0 - mx)
    e1 = jnp.exp(r1 - mx)
    lb = e0 / (e0 + e1)
    hqv = hq_ref[...].astype(F32)
    sq = _sigmoid(hqv)
    qv = hqv * sq
    sf = _sigmoid(hf_ref[...].astype(F32))
    f = lb + (1.0 - lb) * sf
    kv = 1.0 - f
    g = jnp.log(f)
    cum = _mm_exact_l(tri_ref[...], g)
    dec = jnp.exp(jnp.concatenate([cum[c * CHUNK + CHUNK - 1:(c + 1) * CHUNK, :] for c in range(NCH)], axis=0))
    decb = jnp.concatenate([jnp.broadcast_to(dec[c:c + 1, :], (CHUNK, HW)) for c in range(NCH)], axis=0)
    ea = jnp.exp(cum)
    ena = jnp.exp(-cum)
    eend = decb * ena
    return dict(lb=lb, hq=hqv, sq=sq, q=qv, sf=sf, f=f, k=kv, cum=cum, ea=ea, ena=ena, eend=eend,
                qd=qv * ea, ki=kv * ena, ke=kv * eend, dec=dec)


def _tri_mask(transposed=False):
    ti = lax.broadcasted_iota(jnp.int32, (TH, TH), 1 if transposed else 0)
    si = lax.broadcasted_iota(jnp.int32, (TH, TH), 0 if transposed else 1)
    return (si <= ti) & ((si // CHUNK) == (ti // CHUNK))


def _hgrn_fwd(hq, hf, hi, lbr, tri):
    NSUB = 2

    def body(hq_ref, hf_ref, hi_ref, lbr_ref, tri_ref, rec_ref, sall_ref, st_scr):
        @pl.when(pl.program_id(0) == 0)
        def _():
            st_scr[...] = jnp.zeros_like(st_scr)

        causal = _tri_mask()
        for u in range(NSUB):
            tile = slice(u * TH, (u + 1) * TH)
            w = _hgrn_common(hq_ref.at[tile, :], hf_ref.at[tile, :], lbr_ref, tri_ref)
            qd, ki, ke = w["qd"].astype(BF16), w["ki"].astype(BF16), w["ke"].astype(BF16)
            dec = w["dec"]
            vb = hi_ref[tile, :]
            for h in range(4):
                cs = slice(h * 128, (h + 1) * 128)
                att = jnp.where(causal, _mm_nt(qd[:, cs], ki[:, cs]), 0.0)
                o_intra = _mm(att.astype(BF16), vb[:, cs])
                st = st_scr[:, cs]
                for c in range(NCH):
                    rs = slice(c * CHUNK, (c + 1) * CHUNK)
                    sall_ref[u * NCH + c, :, cs] = st
                    rec_ref[u * TH + c * CHUNK:u * TH + (c + 1) * CHUNK, cs] = (
                        o_intra[rs] + _mm_nt(qd[rs, cs], st.astype(BF16))).astype(BF16)
                    st = dec[c:c + 1, cs] * st + _mm_tn(vb[rs, cs], ke[rs, cs])
                st_scr[:, cs] = st

    tok = pl.BlockSpec((NSUB * TH, HW), lambda i: (i, 0))
    return pl.pallas_call(
        body, name="hgrn_fwd", grid=(T // (NSUB * TH),),
        in_specs=[tok, tok, tok, pl.BlockSpec((2, HW), lambda i: (0, 0)), pl.BlockSpec((TH, TH), lambda i: (0, 0))],
        out_specs=[tok, pl.BlockSpec((NSUB * NCH, 128, HW), lambda i: (i, 0, 0))],
        out_shape=[jax.ShapeDtypeStruct((T, HW), BF16), jax.ShapeDtypeStruct((T // CHUNK, 128, HW), F32)],
        scratch_shapes=[pltpu.VMEM((128, HW), F32)],
        compiler_params=_cp(("arbitrary",)),
    )(hq, hf, hi, lbr, tri)


def _hgrn_bwd(hq, hf, hi, lbr, tri, trit, drec, sall, dhg, gw):
    NSUB = 2
    NT = T // (NSUB * TH)

    def body(hq_ref, hf_ref, hi_ref, lbr_ref, tri_ref, trit_ref, do_ref, sall_ref, dhg_ref, gw_r,
             dph_ref, small_ref, pout_o, poutr_o,
             dst_scr, dlb_scr, dqd_scr, dki_scr, dke_scr, dlast_scr, gfull, rbuf, red, redb,
             send_sems, recv_sems, loc_sems, pair_send, pair_recv):
        step = pl.program_id(0)
        loc, rem = _chip_copies(_w_out_piece, red, redb, pout_o, poutr_o, send_sems, recv_sems, loc_sems.at[0])
        mx, my, c = lax.axis_index("x"), lax.axis_index("y"), lax.axis_index("c")
        load = pltpu.make_async_copy(gw_r, gfull, loc_sems.at[1])
        halves = [pltpu.make_async_remote_copy(
            src_ref=gfull.at[pl.ds(pl.multiple_of(j * 256 + (1 - c) * 128, 128), 128), :], dst_ref=rbuf.at[j],
            send_sem=pair_send.at[j], recv_sem=pair_recv.at[j], device_id=(mx, my, 1 - c), device_id_type=MESH)
            for j in range(4)]

        @pl.when(step == 0)
        def _():
            dst_scr[...] = jnp.zeros_like(dst_scr)
            dlb_scr[...] = jnp.zeros_like(dlb_scr)
            load.start()

        @pl.when(step == 1)
        def _():
            load.wait()
            for cp in halves:
                cp.start()

        @pl.when(step == 2)
        def _():
            for j, cp in enumerate(halves):
                cp.wait_recv()
                part = gfull[pl.ds(pl.multiple_of(j * 256 + c * 128, 128), 128), :] + rbuf[j]
                red[j * 128:(j + 1) * 128, :] = part
                redb[j * 128:(j + 1) * 128, :] = part.astype(BF16)
            for cp in halves:
                cp.wait_send()
            for cp in loc + rem:
                cp.start()

        causal = _tri_mask()
        causal_t = _tri_mask(transposed=True)
        lb = None
        for u in reversed(range(NSUB)):
            tile = slice(u * TH, (u + 1) * TH)
            w = _hgrn_common(hq_ref.at[tile, :], hf_ref.at[tile, :], lbr_ref, tri_ref)
            qd, ki, ke = w["qd"].astype(BF16), w["ki"].astype(BF16), w["ke"].astype(BF16)
            dec = w["dec"]
            vb = hi_ref[tile, :]
            dob = do_ref[tile, :].astype(BF16)
            for h in range(4):
                cs = slice(h * 128, (h + 1) * 128)
                att_t = jnp.where(causal_t, _mm_nt(ki[:, cs], qd[:, cs]), 0.0).astype(BF16)
                datt_t = jnp.where(causal_t, _mm_nt(vb[:, cs], dob[:, cs]), 0.0).astype(BF16)
                datt = jnp.where(causal, _mm_nt(dob[:, cs], vb[:, cs]), 0.0).astype(BF16)
                dv_intra = _mm(att_t, dob[:, cs])
                dqd_intra = _mm(datt, ki[:, cs])
                dki_scr[u, :, cs] = _mm(datt_t, qd[:, cs])
                dst = dst_scr[:, cs]
                for c in reversed(range(NCH)):
                    rs = slice(c * CHUNK, (c + 1) * CHUNK)
                    dec_c = dec[c:c + 1, :]
                    st = sall_ref[u * NCH + c, :, cs]
                    dstb = dst.astype(BF16)
                    dph_ref[u * TH + c * CHUNK:u * TH + (c + 1) * CHUNK, 2 * HW + h * 128:2 * HW + (h + 1) * 128] = (
                        dv_intra[rs] + _mm_nt(ke[rs, cs], dstb)).astype(BF16)
                    dqd_scr[u, rs, cs] = dqd_intra[rs] + _mm(dob[rs, cs], st.astype(BF16))
                    dke_scr[u, rs, cs] = _mm(vb[rs, cs], dstb)
                    ddec = jnp.sum(dst * st, axis=0, keepdims=True)
                    dlast_scr[u, c:c + 1, cs] = ddec * dec_c[:, cs]
                    dst = dec_c[:, cs] * dst + _mm_tn(dob[rs, cs], qd[rs, cs])
                dst_scr[:, cs] = dst
            dqd, dki, dke = dqd_scr[u], dki_scr[u], dke_scr[u]
            dq = dqd * w["ea"]
            dk = dki * w["ena"] + dke * w["eend"]
            dcum = dqd * w["qd"] - dki * w["ki"] - dke * w["ke"]
            dkeke = dke * w["ke"]
            dlastb = jnp.concatenate(
                [jnp.broadcast_to(dlast_scr[u, c:c + 1, :]
                                  + jnp.sum(dkeke[c * CHUNK:(c + 1) * CHUNK], axis=0, keepdims=True), (CHUNK, HW))
                 for c in range(NCH)], axis=0)
            dg = _mm_exact_l(trit_ref[...], dcum) + dlastb
            df = dg / w["f"] - dk
            lb, sf, sq = w["lb"], w["sf"], w["sq"]
            dph_ref[tile, HW:2 * HW] = (df * (1.0 - lb) * sf * (1.0 - sf)).astype(BF16)
            dph_ref[tile, 0:HW] = (dq * (sq * (1.0 + w["hq"] * (1.0 - sq)))).astype(BF16)
            dph_ref[tile, 3 * HW:4 * HW] = dhg_ref[tile, :]
            dlb_scr[...] += jnp.sum(df * (1.0 - sf), axis=0, keepdims=True)

        @pl.when(step == NT - 1)
        def _():
            gr = dlb_scr[...] * lb * (1.0 - lb)
            small_ref[...] = jnp.zeros_like(small_ref)
            small_ref[0:1, 0:HW] = gr
            small_ref[1:2, 0:HW] = -gr
            for cp in rem:
                cp.wait_recv()
            for cp in rem:
                cp.wait_send()
            for cp in loc:
                cp.wait()

    tok = pl.BlockSpec((NSUB * TH, HW), lambda i: (NT - 1 - i, 0))
    const = lambda shape: pl.BlockSpec(shape, lambda i: (0,) * len(shape))
    hbm = pl.BlockSpec(memory_space=pltpu.HBM)
    return pl.pallas_call(
        body, name="hgrn_bwd", grid=(NT,),
        in_specs=[tok, tok, tok, const((2, HW)), const((TH, TH)), const((TH, TH)), tok,
                  pl.BlockSpec((NSUB * NCH, 128, HW), lambda i: (NT - 1 - i, 0, 0)), tok, hbm],
        out_specs=[pl.BlockSpec((NSUB * TH, NCOL // 2), lambda i: (NT - 1 - i, 0)), const((8, D)), hbm, hbm],
        out_shape=[jax.ShapeDtypeStruct((T, NCOL // 2), BF16), jax.ShapeDtypeStruct((8, D), F32),
                   jax.ShapeDtypeStruct((128, D), F32), jax.ShapeDtypeStruct((3, 128, D), BF16)],
        scratch_shapes=[pltpu.VMEM((128, HW), F32), pltpu.VMEM((1, HW), F32), pltpu.VMEM((NSUB, TH, HW), F32),
                        pltpu.VMEM((NSUB, TH, HW), F32), pltpu.VMEM((NSUB, TH, HW), F32),
                        pltpu.VMEM((NSUB, 8, HW), F32),
                        pltpu.VMEM((D, D), F32), pltpu.VMEM((4, 128, D), F32), pltpu.VMEM((512, D), F32),
                        pltpu.VMEM((512, D), BF16),
                        pltpu.SemaphoreType.DMA((3,)), pltpu.SemaphoreType.DMA((3,)), pltpu.SemaphoreType.DMA((2,)),
                        pltpu.SemaphoreType.DMA((4,)), pltpu.SemaphoreType.DMA((4,))],
        compiler_params=_cp(("arbitrary",)),
    )(hq, hf, hi, lbr, tri, trit, drec, sall, dhg, gw)


def _fwd_out(o1, o4, o16, l1, l4, l16, rec, ag, hg, x, tgt, anw, hnw, fnw, wout_full, gmat, emat, selmat):
    TT = 512

    def body(o1_r, o4_r, o16_r, l1_r, l4_r, l16_r, rec_r, ag_r, hg_r, x_r, tgt_r, anw_r, hnw_r, fnw_r, wo_r, g_r,
             e_r, sel_r, dx2_o, do1_o, do4_o, do16_o, st1_o, st4_o, st16_o, drec_o, dag_o, dhg_o,
             gw_o, small_o, scr_a, scr_b, scr_c, gwout_o, out_sem):
        @pl.when(pl.program_id(0) == 0)
        def _():
            gwout_o[...] = jnp.zeros_like(gwout_o)
            small_o[...] = jnp.zeros_like(small_o)

        def unperm(r4, r16):
            return _unperm_load(r4, r16, scr_a, scr_b, scr_c)

        def perm_out(val, p1, p4, p16, dt):
            _perm_store(val, scr_a, scr_b, p1, p4, p16, dt)

        o4u, o16u = unperm(o4_r, o16_r)
        l4c, l16c = unperm(l4_r, l16_r)
        l1c = l1_r[...]
        mxc = jnp.maximum(jnp.maximum(l1c, l4c), l16c)
        w1c, w4c, w16c = jnp.exp(l1c - mxc), jnp.exp(l4c - mxc), jnp.exp(l16c - mxc)
        denc = w1c + w4c + w16c
        lane = lax.broadcasted_iota(jnp.int32, (1, 128), 1)
        lse_c = jnp.where(lane < 8, mxc + jnp.log(denc), 0.0)
        em = e_r[...]
        wn1 = _mm_exact_r(w1c / denc, em)
        wn4 = _mm_exact_r(w4c / denc, em)
        o1v = o1_r[...].astype(F32)
        attn = wn1 * o1v + wn4 * o4u + (1.0 - wn1 - wn4) * o16u
        gm = g_r[...]

        def head_mean_a(t):
            return jnp.concatenate([_mm_exact_r(t[:, :256], gm), _mm_exact_r(t[:, 256:], gm)], axis=1)

        def head_mean_h(t):
            return jnp.concatenate(
                [jnp.broadcast_to(jnp.mean(t[:, h * 128:(h + 1) * 128], axis=-1, keepdims=True), (TT, 128))
                 for h in range(4)], axis=1)

        rs_a = lax.rsqrt(head_mean_a(attn * attn) + EPS)
        n_a = attn * rs_a
        agv = ag_r[...].astype(F32)
        sg_a = _sigmoid(agv)
        si_a = agv * sg_a
        anw_v = anw_r[...]
        y_a = (n_a * anw_v) * si_a
        recv = rec_r[...].astype(F32)
        rs_h = lax.rsqrt(head_mean_h(recv * recv) + EPS)
        n_h = recv * rs_h
        hgv = hg_r[...].astype(F32)
        sg_h = _sigmoid(hgv)
        si_h = hgv * sg_h
        hnw_v = hnw_r[...]
        y_h = (n_h * hnw_v) * si_h
        mixed = jnp.concatenate([y_a, y_h], axis=1).astype(BF16)
        xv = x_r[...]
        x2 = xv + _mm(mixed, wo_r[...])
        r2 = lax.rsqrt(jnp.mean(x2 * x2, axis=-1, keepdims=True) + EPS)
        fnw_v = fnw_r[...]
        xn = x2 * r2
        err = xn * fnw_v - tgt_r[...]
        small_o[2:3, :] += 0.5 * jnp.sum(jnp.mean(err * err, axis=-1, keepdims=True), axis=0, keepdims=True)
        small_o[0:1, :] += jnp.sum(err * xn, axis=0, keepdims=True) * (1.0 / D)
        dyw = err * (fnw_v * (1.0 / D))
        dx2 = r2 * dyw - x2 * ((r2 * r2 * r2) * jnp.mean(dyw * x2, axis=-1, keepdims=True))
        dx2_o[...] = dx2
        dx2b = dx2.astype(BF16)
        gwout_o[...] += _mm_tn(mixed, dx2b)
        dmix = _mm_nt(dx2b, wo_r[...])
        dm_a, dm_h = dmix[:, :AW], dmix[:, AW:]
        dag_o[...] = (dm_a * (n_a * anw_v) * (sg_a * (1.0 + agv * (1.0 - sg_a)))).astype(BF16)
        dy_a = dm_a * si_a
        dn_a = dy_a * anw_v
        small_o[1:2, 0:AW] += jnp.sum(dy_a * n_a, axis=0, keepdims=True)
        dattn = rs_a * (dn_a - n_a * head_mean_a(dn_a * n_a))
        perm_out(dattn, do1_o, do4_o, do16_o, BF16)
        stats = lse_c + _mm_exact_r(dattn * attn, sel_r[...])
        perm_out(stats, st1_o, st4_o, st16_o, F32)
        dhg_o[...] = (dm_h * (n_h * hnw_v) * (sg_h * (1.0 + hgv * (1.0 - sg_h)))).astype(BF16)
        dy_h = dm_h * si_h
        dn_h = dy_h * hnw_v
        small_o[1:2, AW:] += jnp.sum(dy_h * n_h, axis=0, keepdims=True)
        drec_o[...] = (rs_h * (dn_h - n_h * head_mean_h(dn_h * n_h))).astype(BF16)

        @pl.when(pl.program_id(0) == T // TT - 1)
        def _():
            out = pltpu.make_async_copy(gwout_o, gw_o, out_sem.at[0])
            out.start()
            out.wait()

    tok = lambda w: pl.BlockSpec((TT, w), lambda i: (i, 0))
    d4 = pl.BlockSpec((4, TT // 4, AW), lambda i: (0, i, 0))
    d16 = pl.BlockSpec((16, TT // 16, AW), lambda i: (0, i, 0))
    const = lambda shape: pl.BlockSpec(shape, lambda i: (0,) * len(shape))
    sd = lambda shape, dt: jax.ShapeDtypeStruct(shape, dt)
    c4 = pl.BlockSpec((4, TT // 4, 128), lambda i: (0, i, 0))
    c16 = pl.BlockSpec((16, TT // 16, 128), lambda i: (0, i, 0))
    p3 = lambda w, dt: [sd((T, w), dt), sd((4, T // 4, w), dt), sd((16, T // 16, w), dt)]
    return pl.pallas_call(
        body, name="fwd_out", grid=(T // TT,),
        in_specs=[tok(AW), d4, d16, tok(128), c4, c16, tok(AW), tok(AW), tok(AW), tok(D), tok(D),
                  const((1, AW)), const((1, HW)), const((1, D)), const((D, D)), const((256, 256)),
                  const((128, AW)), const((AW, 128))],
        out_specs=[tok(D)] + [tok(AW), d4, d16] + [tok(128), c4, c16] + [tok(AW)] * 3
        + [pl.BlockSpec(memory_space=pltpu.HBM), const((8, D))],
        out_shape=[sd((T, D), F32)] + p3(AW, BF16) + p3(128, F32)
        + [sd((T, AW), BF16), sd((T, AW), BF16), sd((T, AW), BF16), sd((D, D), F32), sd((8, D), F32)],
        scratch_shapes=[pltpu.VMEM((4, TT, 128), F32)] * 3 + [pltpu.VMEM((D, D), F32),
                        pltpu.SemaphoreType.DMA((1,))],
        compiler_params=_cp(("arbitrary",)),
    )(o1, o4, o16, l1, l4, l16, rec, ag, hg, x, tgt, anw, hnw, fnw, wout_full, gmat, emat, selmat)


def _dproj_build(dq, dk, dv, dag, pos):
    TT = 512

    def body(dq1, dq4, dq16, dk1, dk4, dk16, dv1, dv4, dv16, dag_r, pos_r, dproj_o, scr_b, scr_c):
        def unperm_sum(r1, r4, r16):
            return r1[...] + _unperm_sum(r4, r16, scr_b, scr_c)

        cosf, s1, s2 = _rope_tables(pos_r[...])
        dproj_o[:, 0:512] = _rope_bwd(unperm_sum(dq1, dq4, dq16), cosf, s1, s2).astype(BF16)
        dproj_o[:, 512:1024] = _rope_bwd(unperm_sum(dk1, dk4, dk16), cosf, s1, s2).astype(BF16)
        dproj_o[:, 1024:1536] = unperm_sum(dv1, dv4, dv16).astype(BF16)
        dproj_o[:, 1536:2048] = dag_r[...]

    tok = lambda w: pl.BlockSpec((TT, w), lambda i: (i, 0))
    d4 = pl.BlockSpec((4, TT // 4, AW), lambda i: (0, i, 0))
    d16 = pl.BlockSpec((16, TT // 16, AW), lambda i: (0, i, 0))
    return pl.pallas_call(
        body, name="dproj_build", grid=(T // TT,),
        in_specs=[tok(AW), d4, d16] * 3 + [tok(AW), pl.BlockSpec((1, TT), lambda i: (0, i))],
        out_specs=tok(NCOL // 2),
        out_shape=jax.ShapeDtypeStruct((T, NCOL // 2), BF16),
        scratch_shapes=[pltpu.VMEM((4, TT, 128), F32)] * 2,
        compiler_params=_cp(("parallel",)),
    )(*dq, *dk, *dv, dag, pos)


def _bwd_x(dproj_a, dproj_h, x, dx2, mixw, w_full, rinb, small4, small6, pout_own, pout_rem, raw):
    TT = 256
    NT = T // TT

    def body(dpa_r, dph_r, x_r, dx2_r, mw_r, w_r, rinb_r, s4_r, s6_r, poo_r, por_r, raw_r,
             gx_o, sall_o, fin_o, fout_o, sbuf, v_own, v_rem, vo_own, vo_rem, sin, sout, got_in,
             got_out, v_send, v_got, send_sems, recv_sems, loc_sems, share_send, share_recv, fin_sems, raw_sems,
             hand_send, hand_recv):
        i = pl.program_id(0)
        mx, my, c = lax.axis_index("x"), lax.axis_index("y"), lax.axis_index("c")
        slot_of = lambda ref, b: ref.at[lax.rem(b - (2 * mx + my) + 3, 4)]
        _, rem = _chip_copies(slot_of, rinb_r, rinb_r, v_own, v_rem, send_sems, recv_sems, loc_sems.at[0])
        loc = [pltpu.make_async_copy(raw_r.at[pl.ds(pl.multiple_of(c * 512, 512), 512), :], v_own, loc_sems.at[0])]
        load_theirs = pltpu.make_async_copy(raw_r.at[pl.ds(pl.multiple_of((1 - c) * 512, 512), 512), :], v_send,
                                            raw_sems.at[0])
        hand = pltpu.make_async_remote_copy(src_ref=v_send, dst_ref=v_got, send_sem=hand_send.at[0],
                                            recv_sem=hand_recv.at[0], device_id=(mx, my, 1 - c), device_id_type=MESH)
        loads = [pltpu.make_async_copy(poo_r, vo_own, fin_sems.at[2]),
                 pltpu.make_async_copy(por_r, vo_rem, fin_sems.at[3])]

        @pl.when(i == 0)
        def _():
            sbuf[...] = jnp.zeros_like(sbuf)
            for cp in loc + rem + loads:
                cp.start()
            load_theirs.start()

        @pl.when(i == 1)
        def _():
            load_theirs.wait()
            hand.start()

        dhn = _mm_nt(dpa_r[...], w_r[:, 0:NCOL // 2]) + _mm_nt(dph_r[...], w_r[:, NCOL // 2:NCOL])
        xv = x_r[...]
        r = lax.rsqrt(jnp.mean(xv * xv, axis=-1, keepdims=True) + EPS)
        dxw = dhn * mw_r[...]
        gx_o[...] = dx2_r[...] + r * dxw - xv * ((r * r * r) * jnp.mean(dxw * xv, axis=-1, keepdims=True))
        sbuf[16:17, :] += jnp.sum(dhn * (xv * r), axis=0, keepdims=True)

        @pl.when(i == NT - 1)
        def _():
            sbuf[0:8, :] = s4_r[...]
            sbuf[8:16, :] = s6_r[...]
            sloc, srem = _small_copies(sbuf, sall_o, send_sems, recv_sems, loc_sems.at[1])
            for cp in sloc + srem:
                cp.start()
            for cp in rem:
                cp.wait_recv()
            for cp in rem:
                cp.wait_send()
            for cp in loc:
                cp.wait()
            for cp in loads:
                cp.wait()
            hand.wait_recv()
            hand.wait_send()
            sout[...] = ((vo_own[...] + vo_rem[0].astype(F32)) + vo_rem[1].astype(F32)) + vo_rem[2].astype(F32)
            sin[...] = (((v_own[...] + v_got[...]) + v_rem[0].astype(F32)) + v_rem[1].astype(F32)) + v_rem[2].astype(F32)
            swap = [pltpu.make_async_remote_copy(src_ref=sin, dst_ref=got_in, send_sem=share_send.at[0],
                                                 recv_sem=share_recv.at[0], device_id=(mx, my, 1 - c),
                                                 device_id_type=MESH),
                    pltpu.make_async_remote_copy(src_ref=sout, dst_ref=got_out, send_sem=share_send.at[1],
                                                 recv_sem=share_recv.at[1], device_id=(mx, my, 1 - c),
                                                 device_id_type=MESH)]
            for cp in swap:
                cp.start()
            mine = [pltpu.make_async_copy(sin, fin_o.at[c], fin_sems.at[0]),
                    pltpu.make_async_copy(sout, fout_o.at[c], fin_sems.at[1])]
            for cp in mine:
                cp.start()
            for cp in swap:
                cp.wait_recv()
            theirs = [pltpu.make_async_copy(got_in, fin_o.at[1 - c], fin_sems.at[2]),
                      pltpu.make_async_copy(got_out, fout_o.at[1 - c], fin_sems.at[3])]
            for cp in theirs:
                cp.start()
            for cp in swap:
                cp.wait_send()
            for cp in mine + theirs:
                cp.wait()
            for cp in srem:
                cp.wait_recv()
            for cp in srem:
                cp.wait_send()
            for cp in sloc:
                cp.wait()

    tok = lambda w: pl.BlockSpec((TT, w), lambda i: (i, 0))
    const = lambda shape: pl.BlockSpec(shape, lambda i: (0,) * len(shape))
    hbm = pl.BlockSpec(memory_space=pltpu.HBM)
    return pl.pallas_call(
        body, name="bwd_x", grid=(NT,),
        in_specs=[tok(NCOL // 2), tok(NCOL // 2), tok(D), tok(D), const((1, D)), const((D, NCOL)), hbm,
                  const((8, D)), const((8, D)), hbm, hbm, hbm],
        out_specs=[tok(D), hbm, hbm, hbm],
        out_shape=[jax.ShapeDtypeStruct((T, D), F32),
                   jax.ShapeDtypeStruct((8, 24, D), F32),
                   jax.ShapeDtypeStruct((2, 512, 1024), F32), jax.ShapeDtypeStruct((2, 128, D), F32)],
        scratch_shapes=[pltpu.VMEM((24, D), F32),
                        pltpu.VMEM((512, 1024), F32), pltpu.VMEM((3, 512, 1024), BF16),
                        pltpu.VMEM((128, D), F32), pltpu.VMEM((3, 128, D), BF16),
                        pltpu.VMEM((512, 1024), F32), pltpu.VMEM((128, D), F32),
                        pltpu.VMEM((512, 1024), F32), pltpu.VMEM((128, D), F32),
                        pltpu.VMEM((512, 1024), F32), pltpu.VMEM((512, 1024), F32),
                        pltpu.SemaphoreType.DMA((10,)), pltpu.SemaphoreType.DMA((10,)), pltpu.SemaphoreType.DMA((2,)),
                        pltpu.SemaphoreType.DMA((2,)), pltpu.SemaphoreType.DMA((2,)), pltpu.SemaphoreType.DMA((4,)),
                        pltpu.SemaphoreType.DMA((1,)), pltpu.SemaphoreType.DMA((1,)), pltpu.SemaphoreType.DMA((1,))],
        compiler_params=_cp(("arbitrary",)),
    )(dproj_a, dproj_h, x, dx2, mixw, w_full, rinb, small4, small6, pout_own, pout_rem, raw)


def _grad_w_in(hn, dproj_a, dproj_h, jm_arr):
    TK = 2048
    NK = T // TK

    def block_at(j, jm):
        return lax.rem(jm + 1 + j, 4)

    def body(jm_ref, hnt_r, dpa_r, dph_r, rinb_o, raw_o, acc, rbuf, obufb, send_sems, recv_sems, wb_sems):
        j = pl.program_id(0)
        kk = pl.program_id(1)
        x, y, c = lax.axis_index("x"), lax.axis_index("y"), lax.axis_index("c")
        mine = pl.ds(pl.multiple_of(c * 512, 512), 512)
        theirs = pl.ds(pl.multiple_of((1 - c) * 512, 512), 512)

        def send(jj):
            return pltpu.make_async_remote_copy(
                src_ref=acc.at[jj % 2, theirs, :], dst_ref=rbuf.at[jj], send_sem=send_sems.at[jj],
                recv_sem=recv_sems.at[jj], device_id=(x, y, 1 - c), device_id_type=MESH)

        def writeback(jj):
            return [pltpu.make_async_copy(obufb.at[jj % 2], rinb_o.at[jj], wb_sems.at[2 + jj % 2])]

        def wait_writeback(jj):
            for cp in writeback(jj):
                cp.wait()

        def finalize(jj):
            send(jj).wait_recv()
            obufb[jj % 2] = (acc[jj % 2, mine, :] + rbuf[jj]).astype(BF16)
            for cp in writeback(jj):
                cp.start()

        blk = block_at(j, jm_ref[0])
        prod = _mm(hnt_r[...], jnp.where(blk < 2, dpa_r[...], dph_r[...]))

        @pl.when(kk == 0)
        def _():
            for jj in (2, 3):
                @pl.when(j == jj)
                def _():
                    send(jj - 2).wait_send()
            acc[j % 2] = prod

        @pl.when(kk > 0)
        def _():
            acc[j % 2] += prod

        @pl.when(kk == NK - 1)
        def _():
            for jj in range(4):
                @pl.when(j == jj)
                def _():
                    if jj < 3:
                        send(jj).start()
                    if jj in (1, 2):
                        finalize(jj - 1)
                    if jj == 3:
                        raw = pltpu.make_async_copy(acc.at[1], raw_o, wb_sems.at[4])
                        raw.start()
                        wait_writeback(0)
                        finalize(2)
                        wait_writeback(1)
                        wait_writeback(2)
                        raw.wait()
                        send(2).wait_send()

    def used(is_mine, kk, col):
        return jnp.where(is_mine, kk, 0), jnp.where(is_mine, col, 0)

    hbm = pl.BlockSpec(memory_space=pltpu.HBM)
    grid_spec = pltpu.PrefetchScalarGridSpec(
        num_scalar_prefetch=1, grid=(4, NK),
        in_specs=[pl.BlockSpec((D, TK), lambda j, kk, jm_ref: (0, kk)),
                  pl.BlockSpec((TK, 1024), lambda j, kk, jm_ref: used(
                      block_at(j, jm_ref[0]) < 2, kk, block_at(j, jm_ref[0]))),
                  pl.BlockSpec((TK, 1024), lambda j, kk, jm_ref: used(
                      block_at(j, jm_ref[0]) >= 2, kk, block_at(j, jm_ref[0]) - 2))],
        out_specs=[hbm, hbm],
        scratch_shapes=[pltpu.VMEM((2, D, 1024), F32), pltpu.VMEM((3, 512, 1024), F32),
                        pltpu.VMEM((2, 512, 1024), BF16),
                        pltpu.SemaphoreType.DMA((3,)), pltpu.SemaphoreType.DMA((3,)), pltpu.SemaphoreType.DMA((5,))])
    return pl.pallas_call(
        body, name="grad_w_in", grid_spec=grid_spec,
        out_shape=[jax.ShapeDtypeStruct((3, 512, 1024), BF16), jax.ShapeDtypeStruct((D, 1024), F32)],
        compiler_params=_cp(("arbitrary", "arbitrary")),
    )(jm_arr, hn, dproj_a, dproj_h)


def _w_out_piece(ref, j):
    return ref.at[pl.ds(j * 128, 128), :]


def _chip_copies(piece, src_r, srcb_r, own_o, rem_o, send_sems, recv_sems, loc_sem):
    x, y, c = lax.axis_index("x"), lax.axis_index("y"), lax.axis_index("c")
    chips = [(1 - x, y), (x, 1 - y), (1 - x, 1 - y)]
    loc = [pltpu.make_async_copy(piece(src_r, 2 * x + y), own_o, loc_sem)]
    rem = [pltpu.make_async_remote_copy(
        src_ref=piece(srcb_r, 2 * px + py), dst_ref=rem_o.at[k], send_sem=send_sems.at[k],
        recv_sem=recv_sems.at[k], device_id=(px, py, c), device_id_type=MESH) for k, (px, py) in enumerate(chips)]
    return loc, rem


def _small_copies(small_r, sall_o, send_sems, recv_sems, loc_sem):
    x, y, c = lax.axis_index("x"), lax.axis_index("y"), lax.axis_index("c")
    me = 4 * x + 2 * y + c
    loc = [pltpu.make_async_copy(small_r, sall_o.at[me], loc_sem)]
    rem = []
    k = 3
    for fx in range(2):
        for fy in range(2):
            for fc in range(2):
                if fx or fy or fc:
                    peer = (1 - x if fx else x, 1 - y if fy else y, 1 - c if fc else c)
                    rem.append(pltpu.make_async_remote_copy(
                        src_ref=small_r, dst_ref=sall_o.at[me], send_sem=send_sems.at[k],
                        recv_sem=recv_sems.at[k], device_id=peer, device_id_type=MESH))
                    k += 1
    return loc, rem


def _adamw_math(w, g, m, v):
    m = B1 * m + (1.0 - B1) * g
    v = B2 * v + (1.0 - B2) * (g * g)
    m_hat = m / (1.0 - B1 ** STEP)
    v_hat = v / (1.0 - B2 ** STEP)
    delta = -LR * (m_hat / (jnp.sqrt(v_hat) + AEPS) + WD * w)
    return delta, m, v


def _adamw(big_in, big_out, sall, params):
    def body(*refs):
        wi, gi, mi, vi, wo, go, mo, vo, sall_r = refs[:9]
        ins = refs[9:24]
        di_o, mi_o, vi_o, do_o, mo_o, vo_o = refs[24:30]
        outs = refs[30:]
        d, mm, vv = _adamw_math(wi[...], gi[...], mi[...], vi[...])
        di_o[...] = d
        mi_o[...] = mm
        vi_o[...] = vv

        @pl.when(pl.program_id(0) == 0)
        def _():
            d, mm, vv = _adamw_math(wo[...], go[...], mo[...], vo[...])
            do_o[...] = d
            mo_o[...] = mm
            vo_o[...] = vv
            tot = sall_r[0]
            for dv in range(1, 8):
                tot = tot + sall_r[dv]
            grads = [tot[16:17, :], tot[1:2, 0:AW], tot[1:2, AW:], tot[8:10, 0:HW], tot[0:1, :]]
            outs[0][...] = tot[2:3, 0:1]
            for p in range(5):
                w_r, m_r, v_r = ins[3 * p:3 * p + 3]
                g = grads[p]
                d, mm, vv = _adamw_math(w_r[...], g, m_r[...], v_r[...])
                outs[1 + 4 * p][...] = g
                outs[2 + 4 * p][...] = d
                outs[3 + 4 * p][...] = mm
                outs[4 + 4 * p][...] = vv

    flat = [a for p in params for a in p]
    shapes = [jax.ShapeDtypeStruct((D, 1024), F32)] * 3 + [jax.ShapeDtypeStruct((256, D), F32)] * 3
    shapes += [jax.ShapeDtypeStruct((1, 1), F32)]
    for p in params:
        shapes += [jax.ShapeDtypeStruct(p[0].shape, F32)] * 4
    vm = pl.BlockSpec(memory_space=pltpu.VMEM)
    rows = pl.BlockSpec((512, 1024), lambda i: (i, 0))
    whole = pl.BlockSpec((256, D), lambda i: (0, 0))
    return pl.pallas_call(
        body, name="adamw", grid=(2,),
        in_specs=[rows] * 4 + [whole] * 4 + [vm] * 16, out_specs=[rows] * 3 + [whole] * 3 + [vm] * 21,
        out_shape=shapes,
        compiler_params=_cp(("arbitrary",)),
    )(*big_in, *big_out, sall, *flat)


def kernel(x, positions, w_in, w_out, mix_norm_w, attn_out_norm_w, hgrn_out_norm_w, hgrn_lb_raw, final_norm_w, loss_target, m_w_in, m_w_out, m_mix_norm_w, m_attn_out_norm_w, m_hgrn_out_norm_w, m_hgrn_lb_raw, m_final_norm_w, v_w_in, v_w_out, v_mix_norm_w, v_attn_out_norm_w, v_hgrn_out_norm_w, v_hgrn_lb_raw, v_final_norm_w):
    xs = x.reshape(T, D)
    tgt = loss_target.reshape(T, D)
    pos = positions.reshape(1, T)
    fnw = final_norm_w.reshape(1, D)

    ti = np.arange(TH)
    tri_np = ((ti[:, None] // CHUNK == ti[None, :] // CHUNK) & (ti[None, :] <= ti[:, None])).astype(np.float32)
    tri = jnp.asarray(tri_np, BF16)
    trit = jnp.asarray(tri_np.T, BF16)
    hi_ = np.arange(AW) // HEAD
    gmat = jnp.asarray((hi_[:256, None] == hi_[None, :256]).astype(np.float32) / HEAD, BF16)
    emat_np = (np.arange(128)[:, None] == hi_[None, :]).astype(np.float32)
    sel_np = (8 + hi_[:, None] == np.arange(128)[None, :]).astype(np.float32)
    emat = jnp.asarray(emat_np, BF16)
    selmat = jnp.asarray(sel_np, BF16)

    jm_arr = (2 * lax.axis_index("x") + lax.axis_index("y")).astype(jnp.int32).reshape(1)
    (hn, q1, k1, v1, q4, k4, v4, q16, k16, v16, ag, hq, hf, hi, hg, w_full, wout4) = _fwd_in(
        xs, pos, mix_norm_w, w_in.reshape(D, 1024), w_out.reshape(256, D), jm_arr)
    wout_full = wout4.reshape(D, D)
    flat = lambda a: a.reshape(T, AW)
    o1, l1 = _attn_fwd(q1, k1, v1, T // BLK, "attn_fwd_d1")
    o4, l4 = _attn_fwd(flat(q4), flat(k4), flat(v4), T // 4 // BLK, "attn_fwd_d4")
    o16, l16 = _attn_fwd(flat(q16), flat(k16), flat(v16), T // 16 // BLK, "attn_fwd_d16")
    rec, sall = _hgrn_fwd(hq, hf, hi, hgrn_lb_raw, tri)

    (dx2, do1, do4, do16, st1, st4, st16, drec, dag, dhg, gw, small4) = _fwd_out(
        o1, o4.reshape(4, T // 4, AW), o16.reshape(16, T // 16, AW),
        l1, l4.reshape(4, T // 4, 128), l16.reshape(16, T // 16, 128),
        rec, ag, hg, xs, tgt, attn_out_norm_w, hgrn_out_norm_w, fnw, wout_full, gmat, emat, selmat)

    fst = lambda a: a.reshape(T, 128)
    dq1, dk1, dv1 = _attn_bwd(q1, k1, v1, do1, st1, T // BLK, "attn_bwd_d1")
    dq4, dk4, dv4 = _attn_bwd(flat(q4), flat(k4), flat(v4), flat(do4), fst(st4), T // 4 // BLK, "attn_bwd_d4")
    dq16, dk16, dv16 = _attn_bwd(flat(q16), flat(k16), flat(v16), flat(do16), fst(st16), T // 16 // BLK,
                                 "attn_bwd_d16")
    dproj_h, small6, pout_own, pout_rem = _hgrn_bwd(hq, hf, hi, hgrn_lb_raw, tri, trit, drec, sall, dhg, gw)

    r4 = lambda a: a.reshape(4, T // 4, AW)
    r16 = lambda a: a.reshape(16, T // 16, AW)
    dproj_a = _dproj_build((dq1, r4(dq4), r16(dq16)), (dk1, r4(dk4), r16(dk16)), (dv1, r4(dv4), r16(dv16)),
                           dag, pos)
    rinb, raw = _grad_w_in(hn, dproj_a, dproj_h, jm_arr)
    gx, small_all, fin, fout = _bwd_x(dproj_a, dproj_h, xs, dx2, mix_norm_w, w_full, rinb,
                                      small4, small6, pout_own, pout_rem, raw)
    g_w_in = fin.reshape(D, 1024)
    g_w_out = fout.reshape(256, D)

    params = [(mix_norm_w, m_mix_norm_w, v_mix_norm_w),
              (attn_out_norm_w, m_attn_out_norm_w, v_attn_out_norm_w),
              (hgrn_out_norm_w, m_hgrn_out_norm_w, v_hgrn_out_norm_w),
              (hgrn_lb_raw, m_hgrn_lb_raw, v_hgrn_lb_raw),
              (fnw, m_final_norm_w.reshape(1, D), v_final_norm_w.reshape(1, D))]
    d_in, nm_in, nv_in, d_out, nm_out, nv_out, *so = _adamw(
        (w_in.reshape(D, 1024), g_w_in, m_w_in.reshape(D, 1024), v_w_in.reshape(D, 1024)),
        (w_out.reshape(256, D), g_w_out, m_w_out.reshape(256, D), v_w_out.reshape(256, D)), small_all, params)
    loss = so[0].reshape(())
    g_s = [so[1 + 4 * p] for p in range(5)]
    d_s = [so[2 + 4 * p] for p in range(5)]
    m_s = [so[3 + 4 * p] for p in range(5)]
    v_s = [so[4 + 4 * p] for p in range(5)]
    for lst in (g_s, d_s, m_s, v_s):
        lst[4] = lst[4].reshape(D)

    return (loss, gx.reshape(1, T, D),
            g_w_in.reshape(1, D, 1024), g_w_out.reshape(1, 256, D), *g_s,
            d_in.reshape(1, D, 1024), d_out.reshape(1, 256, D), *d_s,
            nm_in.reshape(1, D, 1024), nm_out.reshape(1, 256, D), *m_s,
            nv_in.reshape(1, D, 1024), nv_out.reshape(1, 256, D), *v_s)
```

```python
import functools

import numpy as np
import jax
import jax.numpy as jnp
from jax import lax
from jax.experimental import pallas as pl
from jax.experimental.pallas import tpu as pltpu

F32 = jnp.float32
BF16 = jnp.bfloat16

T = 4096
D = 1024
AW = 512
HW = 512
NCOL = 4096
HEAD = 64
BLK = 128
CHUNK = 64
EPS = 1e-6
SCALE = HEAD ** -0.5
NEG = -1e30
ROPE_THETA = 500000.0
INV_FREQ = [float(v) for v in
            (np.float32(ROPE_THETA) ** (-(np.arange(8, dtype=np.float32)) * np.float32(0.125)))]
LR, B1, B2, AEPS, WD, STEP = 0.001, 0.9, 0.999, 1e-08, 0.01, 10
VMEM_LIMIT = 63 * 1024 * 1024
MESH = pl.DeviceIdType.MESH


def _cp(sem=None, **kw):
    return pltpu.CompilerParams(dimension_semantics=sem, vmem_limit_bytes=VMEM_LIMIT, **kw)


def _mm(a, b):
    return jnp.dot(a, b, preferred_element_type=F32)


def _mm_nt(a, b):
    return lax.dot_general(a, b, (((1,), (1,)), ((), ())), preferred_element_type=F32)


def _mm_tn(a, b):
    return lax.dot_general(a, b, (((0,), (0,)), ((), ())), preferred_element_type=F32)


def _mm_exact_l(mat_bf, x):
    h = x.astype(BF16)
    l = (x - h.astype(F32)).astype(BF16)
    return _mm(mat_bf, h) + _mm(mat_bf, l)


def _mm_exact_r(x, mat_bf):
    h = x.astype(BF16)
    l = (x - h.astype(F32)).astype(BF16)
    return _mm(h, mat_bf) + _mm(l, mat_bf)


def _sigmoid(x):
    return 0.5 * jnp.tanh(0.5 * x) + 0.5


def _rope_tables(pos):
    lane = lax.broadcasted_iota(jnp.int32, (1, 128), 1)
    jl = lane & 63
    fi = jl & 7
    inv = jnp.zeros((1, 128), F32)
    for kk in range(8):
        inv = jnp.where(fi == kk, INV_FREQ[kk], inv)
    ang = jnp.broadcast_to(pos.astype(F32), (128, pos.shape[1])).T * inv
    c = jnp.cos(ang)
    s = jnp.sin(ang)
    cosf = jnp.where(jl < 16, c, 1.0)
    s1 = jnp.where(jl < 8, -s, 0.0)
    s2 = jnp.where((jl >= 8) & (jl < 16), s, 0.0)
    return cosf, s1, s2


def _rope(t, cosf, s1, s2):
    parts = []
    for ci in range(t.shape[1] // 128):
        tc = t[:, ci * 128:(ci + 1) * 128]
        parts.append(tc * cosf + pltpu.roll(tc, 120, 1) * s1 + pltpu.roll(tc, 8, 1) * s2)
    return jnp.concatenate(parts, axis=1)


def _rope_bwd(g, cosf, s1, s2):
    parts = []
    for ci in range(g.shape[1] // 128):
        gc = g[:, ci * 128:(ci + 1) * 128]
        parts.append(gc * cosf + pltpu.roll(gc * s1, 8, 1) + pltpu.roll(gc * s2, 120, 1))
    return jnp.concatenate(parts, axis=1)


def _perm_store(val, scr, scr2, o1, o4, o16, dt):
    n = val.shape[0]
    q = n // 4
    o1[...] = val.astype(dt)
    for ci in range(val.shape[1] // 128):
        cs = slice(ci * 128, (ci + 1) * 128)
        scr[ci] = val[:, cs]
        for r4 in range(4):
            part = scr[ci, pl.ds(r4, q, stride=4), :]
            o4[r4, :, cs] = part.astype(dt)
            scr2[ci, r4 * q:(r4 + 1) * q, :] = part
        for r4 in range(4):
            for b in range(4):
                o16[r4 + 4 * b, :, cs] = scr2[ci, pl.ds(r4 * q + b, q // 4, stride=4), :].astype(dt)


def _unperm_load(r4, r16, scr_a, scr_b, scr_c):
    n = scr_a.shape[1]
    q = n // 4
    nc = r4.shape[-1] // 128
    for ci in range(nc):
        cs = slice(ci * 128, (ci + 1) * 128)
        for rr in range(4):
            scr_a[ci, pl.ds(rr, q, stride=4), :] = r4[rr, :, cs].astype(F32)
        for rr in range(4):
            for b in range(4):
                scr_c[ci, pl.ds(rr * q + b, q // 4, stride=4), :] = r16[rr + 4 * b, :, cs].astype(F32)
        for rr in range(4):
            scr_b[ci, pl.ds(rr, q, stride=4), :] = scr_c[ci, rr * q:(rr + 1) * q, :]
    return (jnp.concatenate([scr_a[ci] for ci in range(nc)], axis=1),
            jnp.concatenate([scr_b[ci] for ci in range(nc)], axis=1))


def _unperm_sum(r4, r16, scr_b, scr_c):
    n = scr_b.shape[1]
    q = n // 4
    nc = r4.shape[-1] // 128
    for ci in range(nc):
        cs = slice(ci * 128, (ci + 1) * 128)
        for rr in range(4):
            for b in range(4):
                scr_c[ci, pl.ds(rr * q + b, q // 4, stride=4), :] = r16[rr + 4 * b, :, cs].astype(F32)
        for rr in range(4):
            scr_b[ci, pl.ds(rr, q, stride=4), :] = scr_c[ci, rr * q:(rr + 1) * q, :] + r4[rr, :, cs].astype(F32)
    return jnp.concatenate([scr_b[ci] for ci in range(nc)], axis=1)


def _fwd_in(x, pos, mixw, w_in, w_out, jm_arr):
    TT = 512
    NT = T // TT

    def body(jm_ref, x_ref, pos_ref, mw_ref, win_ref, wout_ref,
             hnt_ref, q1, k1, v1, q4, k4, v4, q16, k16, v16, ag, hq, hf, hi, hg, wfull_o, woutfull_o,
             wbuf, wobuf, hn_all, scr, scr2, stage, send_sems, recv_sems, loc_sems):
        s = pl.program_id(0)
        i = pl.program_id(1)
        mx, my, c = lax.axis_index("x"), lax.axis_index("y"), lax.axis_index("c")
        me, sibling = (mx, my, c), (mx, my, 1 - c)
        chips = [(mx, 1 - my), (1 - mx, my), (1 - mx, 1 - my)]
        jm = 2 * mx + my
        rows_in = [pl.ds(pl.multiple_of(h * 512, 512), 512) for h in (c, 1 - c)]
        rows_out = [pl.ds(pl.multiple_of(h * 128, 128), 128) for h in (c, 1 - c)]

        def blk(k):
            return lax.bitwise_xor(jm, k + 1)

        def rc(n, ref, to):
            return pltpu.make_async_remote_copy(src_ref=ref, dst_ref=ref, send_sem=send_sems.at[n],
                                                recv_sem=recv_sems.at[n], device_id=to, device_id_type=MESH)

        halves = [pl.ds(0, 512), pl.ds(512, 512)]
        send_in = lambda k, h: rc(12 + 2 * k + h, wbuf.at[jm, rows_in[0], halves[h]], (*chips[k], c))
        got_in = lambda k, h: rc(12 + 2 * k + h, wbuf.at[blk(k), rows_in[0], halves[h]], me)
        relay = lambda h: rc(16 + h, wbuf.at[blk(h), rows_in[0], halves[h]], (*chips[1 - h], c))
        got_relay = lambda h: rc(16 + h, wbuf.at[blk(2), rows_in[0], halves[h]], me)
        send_out = lambda k: rc(3 + k, wobuf.at[jm, rows_out[0], :], (*chips[k], c))
        got_out = lambda k: rc(3 + k, wobuf.at[blk(k), rows_out[0], :], me)
        pass_in = lambda k: rc(6 + k, wbuf.at[blk(k), rows_in[0], :], sibling)
        pass_out = lambda k: rc(9 + k, wobuf.at[blk(k), rows_out[0], :], sibling)
        passed_in = lambda k: rc(6 + k, wbuf.at[blk(k), rows_in[1], :], me)
        passed_out = lambda k: rc(9 + k, wobuf.at[blk(k), rows_out[1], :], me)

        def keep(j, n):
            return pltpu.make_async_copy(wbuf.at[j], wfull_o.at[:, pl.ds(j * 1024, 1024)], loc_sems.at[n])

        @pl.when((s == 0) & (i == 0))
        def _():
            chunk = [pl.ds(pl.multiple_of(lax.rem(p + 2 * c, 4) * 256, 256), 256) for p in range(4)]
            loads = [pltpu.make_async_copy(win_ref.at[chunk[p], :] if p < 4 else wout_ref, stage.at[p % 2],
                                           loc_sems.at[4 + p % 2]) for p in range(5)]
            loads[0].start()
            for p in range(5):
                if p < 4:
                    loads[p + 1].start()
                loads[p].wait()
                if p < 4:
                    wbuf[jm, chunk[p], :] = stage[p % 2].astype(BF16)
                else:
                    wobuf[jm] = stage[p % 2].astype(BF16)
                if p == 1:
                    for k in range(2):
                        for h in range(2):
                            send_in(k, h).start()
            keep(jm, 0).start()

        @pl.when((s == 0) & (i == NT - 1))
        def _():
            for kk in range(2):
                for h in range(2):
                    got_in(kk, h).wait_recv()
            relay(0).start()
            relay(1).start()
            pass_in(0).start()
            pass_in(1).start()
            passed_in(0).wait_recv()
            keep(blk(0), 1).start()

        @pl.when((s == 1) & (i == NT - 1))
        def _():
            got_relay(0).wait_recv()
            got_relay(1).wait_recv()
            pass_in(2).start()

        @pl.when((s == 2) & (i == 0))
        def _():
            for k in (1, 2):
                passed_in(k).wait_recv()
                keep(blk(k), k + 1).start()
            for kk in range(3):
                send_out(kk).start()

        @pl.when((s == 2) & (i == NT - 2))
        def _():
            for k in range(3):
                got_out(k).wait_recv()
                pass_out(k).start()

        whole_out = pltpu.make_async_copy(wobuf, woutfull_o, loc_sems.at[4])

        @pl.when((s == 2) & (i == NT - 1))
        def _():
            for k in range(3):
                passed_out(k).wait_recv()
            whole_out.start()

        tile = pl.ds(pl.multiple_of(i * TT, TT), TT)

        @pl.when(s == 0)
        def _():
            xv = x_ref[...]
            r = lax.rsqrt(jnp.mean(xv * xv, axis=-1, keepdims=True) + EPS)
            hnf = (xv * r) * mw_ref[...]
            hn_all[tile, :] = hnf.astype(BF16)
            hnt_ref[...] = hnf.T.astype(BF16)

        def project(jj):
            hn = hn_all[tile, :]
            lo = _mm(hn, wbuf[jj, :, 0:512])
            hi_cols = _mm(hn, wbuf[jj, :, 512:1024])
            if jj == 0:
                cosf, s1, s2 = _rope_tables(pos_ref[...])
                _perm_store(_rope(lo, cosf, s1, s2) * SCALE, scr, scr2, q1, q4, q16, BF16)
                _perm_store(_rope(hi_cols, cosf, s1, s2), scr, scr2, k1, k4, k16, BF16)
            elif jj == 1:
                _perm_store(lo, scr, scr2, v1, v4, v16, BF16)
                ag[...] = hi_cols.astype(BF16)
            elif jj == 2:
                hq[...] = lo.astype(BF16)
                hf[...] = hi_cols.astype(BF16)
            else:
                hi[...] = lo.astype(BF16)
                hg[...] = hi_cols.astype(BF16)

        def project_block(j):
            for jj in range(4):
                pl.when(j == jj)(functools.partial(project, jj))

        @pl.when(s < 2)
        def _():
            project_block(lax.bitwise_xor(jm, s))

        @pl.when(s == 2)
        def _():
            project_block(lax.bitwise_xor(jm, 2))
            project_block(lax.bitwise_xor(jm, 3))

        @pl.when((s == 2) & (i == NT - 1))
        def _():
            for h in range(2):
                relay(h).wait_send()
                for k in range(2):
                    send_in(k, h).wait_send()
            for k in range(3):
                send_out(k).wait_send()
                pass_in(k).wait_send()
                pass_out(k).wait_send()
            keep(jm, 0).wait()
            for k in range(3):
                keep(blk(k), k + 1).wait()
            whole_out.wait()

    def at_stage_of(jb):
        def index(s, i, jm_ref):
            sa = jnp.minimum(lax.bitwise_xor(jm_ref[0], jb), 2)
            return jnp.where(s < sa, 0, jnp.where(s == sa, i, NT - 1))
        return index

    tok = lambda w, jb: pl.BlockSpec((TT, w), lambda s, i, jm_ref: (at_stage_of(jb)(s, i, jm_ref), 0))
    d4 = lambda jb: pl.BlockSpec((4, TT // 4, AW), lambda s, i, jm_ref: (0, at_stage_of(jb)(s, i, jm_ref), 0))
    d16 = lambda jb: pl.BlockSpec((16, TT // 16, AW), lambda s, i, jm_ref: (0, at_stage_of(jb)(s, i, jm_ref), 0))
    hbm = pl.BlockSpec(memory_space=pltpu.HBM)
    sd = lambda shape, dt: jax.ShapeDtypeStruct(shape, dt)
    in_own_stage = lambda s, i: jnp.where(s == 0, i, NT - 1)
    grid_spec = pltpu.PrefetchScalarGridSpec(
        num_scalar_prefetch=1, grid=(3, NT),
        in_specs=[pl.BlockSpec((TT, D), lambda s, i, jm_ref: (in_own_stage(s, i), 0)),
                  pl.BlockSpec((1, TT), lambda s, i, jm_ref: (0, i)),
                  pl.BlockSpec((1, D), lambda s, i, jm_ref: (0, 0)), hbm, hbm],
        out_specs=[pl.BlockSpec((D, TT), lambda s, i, jm_ref: (0, in_own_stage(s, i))),
                   tok(AW, 0), tok(AW, 0), tok(AW, 1), d4(0), d4(0), d4(1), d16(0), d16(0), d16(1),
                   tok(AW, 1), tok(AW, 2), tok(AW, 2), tok(AW, 3), tok(AW, 3), hbm, hbm],
        scratch_shapes=[pltpu.VMEM((4, D, 1024), BF16), pltpu.VMEM((4, 256, D), BF16), pltpu.VMEM((T, D), BF16),
                        pltpu.VMEM((4, TT, 128), F32), pltpu.VMEM((4, TT, 128), F32), pltpu.VMEM((2, 256, 1024), F32),
                        pltpu.SemaphoreType.DMA((18,)),
                        pltpu.SemaphoreType.DMA((18,)), pltpu.SemaphoreType.DMA((6,))])
    return pl.pallas_call(
        body, name="fwd_in", grid_spec=grid_spec,
        out_shape=[sd((D, T), BF16)] + [sd((T, AW), BF16)] * 3 + [sd((4, T // 4, AW), BF16)] * 3
        + [sd((16, T // 16, AW), BF16)] * 3
        + [sd((T, AW), BF16)] * 5 + [sd((D, NCOL), BF16), sd((4, 256, D), BF16)],
        compiler_params=_cp(("arbitrary", "arbitrary")),
    )(jm_arr, x, pos, mixw, w_in, w_out)


def _band_mask(key_axis, nkeys=2 * BLK):
    shape = (nkeys, 2 * BLK) if key_axis == 0 else (2 * BLK, nkeys)
    kj = lax.broadcasted_iota(jnp.int32, shape, key_axis)
    qi = lax.broadcasted_iota(jnp.int32, shape, 1 - key_axis) & (BLK - 1)
    return (kj >= qi) & (kj <= qi + BLK), kj, qi


def _stack_heads(t2, in_a):
    z = jnp.zeros_like(t2)
    return jnp.concatenate([jnp.where(in_a[0], t2, z), jnp.where(in_a[1], t2, z)], axis=0)


def _attn_fwd(q, k, v, nb, name):
    n = 8
    CH = n * BLK
    halo = nb > n

    def body(*refs):
        if halo:
            q_ref, k_ref, v_ref, kp_ref, vp_ref, o_ref, lse_ref = refs
        else:
            q_ref, k_ref, v_ref, o_ref, lse_ref = refs
        lane = lax.broadcasted_iota(jnp.int32, (1, 128), 1)
        in_a = [lane < HEAD, lane >= HEAD]
        band, kj, _ = _band_mask(1)
        thr0 = jnp.where((n * pl.program_id(0)) % nb == 0, BLK, 0) if halo else BLK
        mask0 = band & (kj >= thr0)
        mask_first = band & (kj >= BLK)
        for b in range(n):
            rs = slice(b * BLK, (b + 1) * BLK)
            stat = jnp.zeros((BLK, 128), F32)
            for hp in range(4):
                cs = slice(hp * 128, (hp + 1) * 128)
                q2s = _stack_heads(q_ref[rs, cs], in_a)
                if b == 0:
                    kprev = kp_ref[:, cs] if halo else k_ref[rs, cs]
                    vprev = vp_ref[:, cs] if halo else v_ref[rs, cs]
                    kk = jnp.concatenate([kprev, k_ref[rs, cs]], axis=0)
                    vv = jnp.concatenate([vprev, v_ref[rs, cs]], axis=0)
                    mask = mask0
                else:
                    kk = k_ref[(b - 1) * BLK:(b + 1) * BLK, cs]
                    vv = v_ref[(b - 1) * BLK:(b + 1) * BLK, cs]
                    mask = mask_first if b % nb == 0 else band
                s = jnp.where(mask, _mm_nt(q2s, kk), NEG)
                m = jnp.max(s, axis=-1, keepdims=True)
                p = jnp.exp(s - m)
                l = jnp.sum(p, axis=-1, keepdims=True)
                o = _mm(p.astype(BF16), vv) / l
                lse = m + jnp.log(l)
                o_ref[rs, cs] = jnp.where(in_a[0], o[:BLK], o[BLK:]).astype(BF16)
                stat = jnp.where(lane == 2 * hp, lse[:BLK], stat)
                stat = jnp.where(lane == 2 * hp + 1, lse[BLK:], stat)
            lse_ref[rs, :] = stat

    cur = pl.BlockSpec((CH, AW), lambda i: (i, 0))
    prev = pl.BlockSpec((BLK, AW), lambda i: (jnp.maximum(n * i - 1, 0), 0))
    return pl.pallas_call(
        body, name=name, grid=(T // CH,),
        in_specs=[cur, cur, cur] + ([prev, prev] if halo else []),
        out_specs=[cur, pl.BlockSpec((CH, 128), lambda i: (i, 0))],
        out_shape=[jax.ShapeDtypeStruct((T, AW), BF16), jax.ShapeDtypeStruct((T, 128), F32)],
        compiler_params=_cp(("parallel",)),
    )(*((q, k, v) + ((k, v) if halo else ())))


def _attn_bwd(q, k, v, do, st, nb, name):
    n = 8
    CH = n * BLK
    NBLK = T // BLK
    halo = nb > n

    def body(*refs):
        if halo:
            (q_ref, k_ref, v_ref, do_ref, st_ref, kp_ref, vp_ref, qn_ref, don_ref, stn_ref,
             dq_ref, dk_ref, dv_ref) = refs
        else:
            q_ref, k_ref, v_ref, do_ref, st_ref, dq_ref, dk_ref, dv_ref = refs
        i = pl.program_id(0)
        lane = lax.broadcasted_iota(jnp.int32, (1, 128), 1)
        in_a = [lane < HEAD, lane >= HEAD]
        band, kj, _ = _band_mask(0)
        thr0 = jnp.where((n * i) % nb == 0, BLK, 0) if halo else BLK
        mask0 = band & (kj >= thr0)
        mask_first = band & (kj >= BLK)

        def stat_rows(st_t, hp):
            lse_r = jnp.concatenate([st_t[2 * hp:2 * hp + 1, :], st_t[2 * hp + 1:2 * hp + 2, :]], axis=1)
            dl_r = jnp.concatenate([st_t[8 + 2 * hp:9 + 2 * hp, :], st_t[9 + 2 * hp:10 + 2 * hp, :]], axis=1)
            return lse_r, dl_r

        st_t = [st_ref[b * BLK:(b + 1) * BLK, :].T for b in range(n)]
        if halo:
            nxt_thr = jnp.where((n * i + n) % nb == 0, 2 * BLK, 0)
            _, kj1, qi1 = _band_mask(0, BLK)
            mask_next = kj1 >= qi1 + nxt_thr
            stn_t = stn_ref[...].T

        for hp in range(4):
            cs = slice(hp * 128, (hp + 1) * 128)
            kb = [k_ref[b * BLK:(b + 1) * BLK, cs] for b in range(n)]
            vb = [v_ref[b * BLK:(b + 1) * BLK, cs] for b in range(n)]
            dk_acc = [jnp.zeros((BLK, 128), F32) for _ in range(n)]
            dv_acc = [jnp.zeros((BLK, 128), F32) for _ in range(n)]
            for b in range(n):
                rs = slice(b * BLK, (b + 1) * BLK)
                q2s = _stack_heads(q_ref[rs, cs], in_a)
                do2s = _stack_heads(do_ref[rs, cs], in_a)
                if b == 0:
                    kprev = kp_ref[:, cs] if halo else kb[0]
                    vprev = vp_ref[:, cs] if halo else vb[0]
                    mask = mask0
                else:
                    kprev, vprev, mask = kb[b - 1], vb[b - 1], (mask_first if b % nb == 0 else band)
                kk = jnp.concatenate([kprev, kb[b]], axis=0)
                vv = jnp.concatenate([vprev, vb[b]], axis=0)
                lse_r, dl_r = stat_rows(st_t[b], hp)
                s_t = jnp.where(mask, _mm_nt(kk, q2s), NEG)
                p_t = jnp.exp(s_t - lse_r)
                ds_t = (p_t * (_mm_nt(vv, do2s) - dl_r)).astype(BF16)
                dkk = _mm(ds_t, q2s)
                dvv = _mm(p_t.astype(BF16), do2s)
                dqs = _mm_tn(ds_t, kk) * SCALE
                dq_ref[rs, cs] = jnp.where(in_a[0], dqs[:BLK], dqs[BLK:]).astype(BF16)
                dk_acc[b] += dkk[BLK:]
                dv_acc[b] += dvv[BLK:]
                if b > 0:
                    dk_acc[b - 1] += dkk[:BLK]
                    dv_acc[b - 1] += dvv[:BLK]
            if halo:
                q2s = _stack_heads(qn_ref[:, cs], in_a)
                do2s = _stack_heads(don_ref[:, cs], in_a)
                lse_r, dl_r = stat_rows(stn_t, hp)
                s_t = jnp.where(mask_next, _mm_nt(kb[n - 1], q2s), NEG)
                p_t = jnp.exp(s_t - lse_r)
                ds_t = (p_t * (_mm_nt(vb[n - 1], do2s) - dl_r)).astype(BF16)
                dk_acc[n - 1] += _mm(ds_t, q2s)
                dv_acc[n - 1] += _mm(p_t.astype(BF16), do2s)
            for b in range(n):
                dk_ref[b * BLK:(b + 1) * BLK, cs] = dk_acc[b].astype(BF16)
                dv_ref[b * BLK:(b + 1) * BLK, cs] = dv_acc[b].astype(BF16)

    cur = pl.BlockSpec((CH, AW), lambda i: (i, 0))
    cur_st = pl.BlockSpec((CH, 128), lambda i: (i, 0))
    prev = pl.BlockSpec((BLK, AW), lambda i: (jnp.maximum(n * i - 1, 0), 0))
    nxt = pl.BlockSpec((BLK, AW), lambda i: (jnp.minimum(n * i + n, NBLK - 1), 0))
    nxt_st = pl.BlockSpec((BLK, 128), lambda i: (jnp.minimum(n * i + n, NBLK - 1), 0))
    ins = [cur] * 4 + [cur_st] + ([prev, prev, nxt, nxt, nxt_st] if halo else [])
    args = (q, k, v, do, st) + ((k, v, q, do, st) if halo else ())
    return pl.pallas_call(
        body, name=name, grid=(T // CH,),
        in_specs=ins,
        out_specs=[cur] * 3,
        out_shape=[jax.ShapeDtypeStruct((T, AW), BF16)] * 3,
        compiler_params=_cp(("parallel",)),
    )(*args)


TH = 256
NCH = TH // CHUNK


def _hgrn_common(hq_ref, hf_ref, lbr_ref, tri_ref):
    r0 = lbr_ref[0:1, :]
    r1 = lbr_ref[1:2, :]
    mx = jnp.maximum(r0, r1)
    e0 = jnp.exp(r0 - mx)
    e1 = jnp.exp(r1 - mx)
    lb = e0 / (e0 + e1)
    hqv = hq_ref[...].astype(F32)
    sq = _sigmoid(hqv)
    qv = hqv * sq
    sf = _sigmoid(hf_ref[...].astype(F32))
    f = lb + (1.0 - lb) * sf
    kv = 1.0 - f
    g = jnp.log(f)
    cum = _mm_exact_l(tri_ref[...], g)
    dec = jnp.exp(jnp.concatenate([cum[c * CHUNK + CHUNK - 1:(c + 1) * CHUNK, :] for c in range(NCH)], axis=0))
    decb = jnp.concatenate([jnp.broadcast_to(dec[c:c + 1, :], (CHUNK, HW)) for c in range(NCH)], axis=0)
    ea = jnp.exp(cum)
    ena = jnp.exp(-cum)
    eend = decb * ena
    return dict(lb=lb, hq=hqv, sq=sq, q=qv, sf=sf, f=f, k=kv, cum=cum, ea=ea, ena=ena, eend=eend,
                qd=qv * ea, ki=kv * ena, ke=kv * eend, dec=dec)


def _tri_mask(transposed=False):
    ti = lax.broadcasted_iota(jnp.int32, (TH, TH), 1 if transposed else 0)
    si = lax.broadcasted_iota(jnp.int32, (TH, TH), 0 if transposed else 1)
    return (si <= ti) & ((si // CHUNK) == (ti // CHUNK))


def _hgrn_fwd(hq, hf, hi, lbr, tri):
    NSUB = 2

    def body(hq_ref, hf_ref, hi_ref, lbr_ref, tri_ref, rec_ref, sall_ref, st_scr):
        @pl.when(pl.program_id(0) == 0)
        def _():
            st_scr[...] = jnp.zeros_like(st_scr)

        causal = _tri_mask()
        for u in range(NSUB):
            tile = slice(u * TH, (u + 1) * TH)
            w = _hgrn_common(hq_ref.at[tile, :], hf_ref.at[tile, :], lbr_ref, tri_ref)
            qd, ki, ke = w["qd"].astype(BF16), w["ki"].astype(BF16), w["ke"].astype(BF16)
            dec = w["dec"]
            vb = hi_ref[tile, :]
            for h in range(4):
                cs = slice(h * 128, (h + 1) * 128)
                att = jnp.where(causal, _mm_nt(qd[:, cs], ki[:, cs]), 0.0)
                o_intra = _mm(att.astype(BF16), vb[:, cs])
                st = st_scr[:, cs]
                for c in range(NCH):
                    rs = slice(c * CHUNK, (c + 1) * CHUNK)
                    sall_ref[u * NCH + c, :, cs] = st
                    rec_ref[u * TH + c * CHUNK:u * TH + (c + 1) * CHUNK, cs] = (
                        o_intra[rs] + _mm_nt(qd[rs, cs], st.astype(BF16))).astype(BF16)
                    st = dec[c:c + 1, cs] * st + _mm_tn(vb[rs, cs], ke[rs, cs])
                st_scr[:, cs] = st

    tok = pl.BlockSpec((NSUB * TH, HW), lambda i: (i, 0))
    return pl.pallas_call(
        body, name="hgrn_fwd", grid=(T // (NSUB * TH),),
        in_specs=[tok, tok, tok, pl.BlockSpec((2, HW), lambda i: (0, 0)), pl.BlockSpec((TH, TH), lambda i: (0, 0))],
        out_specs=[tok, pl.BlockSpec((NSUB * NCH, 128, HW), lambda i: (i, 0, 0))],
        out_shape=[jax.ShapeDtypeStruct((T, HW), BF16), jax.ShapeDtypeStruct((T // CHUNK, 128, HW), F32)],
        scratch_shapes=[pltpu.VMEM((128, HW), F32)],
        compiler_params=_cp(("arbitrary",)),
    )(hq, hf, hi, lbr, tri)


def _hgrn_bwd(hq, hf, hi, lbr, tri, trit, drec, sall, dhg, gw):
    NSUB = 2
    NT = T // (NSUB * TH)

    def body(hq_ref, hf_ref, hi_ref, lbr_ref, tri_ref, trit_ref, do_ref, sall_ref, dhg_ref, gw_r,
             dph_ref, small_ref, pout_o, poutr_o,
             dst_scr, dlb_scr, dqd_scr, dki_scr, dke_scr, dlast_scr, gfull, rbuf, red, redb,
             send_sems, recv_sems, loc_sems, pair_send, pair_recv):
        step = pl.program_id(0)
        loc, rem = _chip_copies(_w_out_piece, red, redb, pout_o, poutr_o, send_sems, recv_sems, loc_sems.at[0])
        mx, my, c = lax.axis_index("x"), lax.axis_index("y"), lax.axis_index("c")
        load = pltpu.make_async_copy(gw_r, gfull, loc_sems.at[1])
        halves = [pltpu.make_async_remote_copy(
            src_ref=gfull.at[pl.ds(pl.multiple_of(j * 256 + (1 - c) * 128, 128), 128), :], dst_ref=rbuf.at[j],
            send_sem=pair_send.at[j], recv_sem=pair_recv.at[j], device_id=(mx, my, 1 - c), device_id_type=MESH)
            for j in range(4)]

        @pl.when(step == 0)
        def _():
            dst_scr[...] = jnp.zeros_like(dst_scr)
            dlb_scr[...] = jnp.zeros_like(dlb_scr)
            load.start()

        @pl.when(step == 1)
        def _():
            load.wait()
            for cp in halves:
                cp.start()

        @pl.when(step == 2)
        def _():
            for j, cp in enumerate(halves):
                cp.wait_recv()
                part = gfull[pl.ds(pl.multiple_of(j * 256 + c * 128, 128), 128), :] + rbuf[j]
                red[j * 128:(j + 1) * 128, :] = part
                redb[j * 128:(j + 1) * 128, :] = part.astype(BF16)
            for cp in halves:
                cp.wait_send()
            for cp in loc + rem:
                cp.start()

        causal = _tri_mask()
        causal_t = _tri_mask(transposed=True)
        lb = None
        for u in reversed(range(NSUB)):
            tile = slice(u * TH, (u + 1) * TH)
            w = _hgrn_common(hq_ref.at[tile, :], hf_ref.at[tile, :], lbr_ref, tri_ref)
            qd, ki, ke = w["qd"].astype(BF16), w["ki"].astype(BF16), w["ke"].astype(BF16)
            dec = w["dec"]
            vb = hi_ref[tile, :]
            dob = do_ref[tile, :].astype(BF16)
            for h in range(4):
                cs = slice(h * 128, (h + 1) * 128)
                att_t = jnp.where(causal_t, _mm_nt(ki[:, cs], qd[:, cs]), 0.0).astype(BF16)
                datt_t = jnp.where(causal_t, _mm_nt(vb[:, cs], dob[:, cs]), 0.0).astype(BF16)
                datt = jnp.where(causal, _mm_nt(dob[:, cs], vb[:, cs]), 0.0).astype(BF16)
                dv_intra = _mm(att_t, dob[:, cs])
                dqd_intra = _mm(datt, ki[:, cs])
                dki_scr[u, :, cs] = _mm(datt_t, qd[:, cs])
                dst = dst_scr[:, cs]
                for c in reversed(range(NCH)):
                    rs = slice(c * CHUNK, (c + 1) * CHUNK)
                    dec_c = dec[c:c + 1, :]
                    st = sall_ref[u * NCH + c, :, cs]
                    dstb = dst.astype(BF16)
                    dph_ref[u * TH + c * CHUNK:u * TH + (c + 1) * CHUNK, 2 * HW + h * 128:2 * HW + (h + 1) * 128] = (
                        dv_intra[rs] + _mm_nt(ke[rs, cs], dstb)).astype(BF16)
                    dqd_scr[u, rs, cs] = dqd_intra[rs] + _mm(dob[rs, cs], st.astype(BF16))
                    dke_scr[u, rs, cs] = _mm(vb[rs, cs], dstb)
                    ddec = jnp.sum(dst * st, axis=0, keepdims=True)
                    dlast_scr[u, c:c + 1, cs] = ddec * dec_c[:, cs]
                    dst = dec_c[:, cs] * dst + _mm_tn(dob[rs, cs], qd[rs, cs])
                dst_scr[:, cs] = dst
            dqd, dki, dke = dqd_scr[u], dki_scr[u], dke_scr[u]
            dq = dqd * w["ea"]
            dk = dki * w["ena"] + dke * w["eend"]
            dcum = dqd * w["qd"] - dki * w["ki"] - dke * w["ke"]
            dkeke = dke * w["ke"]
            dlastb = jnp.concatenate(
                [jnp.broadcast_to(dlast_scr[u, c:c + 1, :]
                                  + jnp.sum(dkeke[c * CHUNK:(c + 1) * CHUNK], axis=0, keepdims=True), (CHUNK, HW))
                 for c in range(NCH)], axis=0)
            dg = _mm_exact_l(trit_ref[...], dcum) + dlastb
            df = dg / w["f"] - dk
            lb, sf, sq = w["lb"], w["sf"], w["sq"]
            dph_ref[tile, HW:2 * HW] = (df * (1.0 - lb) * sf * (1.0 - sf)).astype(BF16)
            dph_ref[tile, 0:HW] = (dq * (sq * (1.0 + w["hq"] * (1.0 - sq)))).astype(BF16)
            dph_ref[tile, 3 * HW:4 * HW] = dhg_ref[tile, :]
            dlb_scr[...] += jnp.sum(df * (1.0 - sf), axis=0, keepdims=True)

        @pl.when(step == NT - 1)
        def _():
            gr = dlb_scr[...] * lb * (1.0 - lb)
            small_ref[...] = jnp.zeros_like(small_ref)
            small_ref[0:1, 0:HW] = gr
            small_ref[1:2, 0:HW] = -gr
            for cp in rem:
                cp.wait_recv()
            for cp in rem:
                cp.wait_send()
            for cp in loc:
                cp.wait()

    tok = pl.BlockSpec((NSUB * TH, HW), lambda i: (NT - 1 - i, 0))
    const = lambda shape: pl.BlockSpec(shape, lambda i: (0,) * len(shape))
    hbm = pl.BlockSpec(memory_space=pltpu.HBM)
    return pl.pallas_call(
        body, name="hgrn_bwd", grid=(NT,),
        in_specs=[tok, tok, tok, const((2, HW)), const((TH, TH)), const((TH, TH)), tok,
                  pl.BlockSpec((NSUB * NCH, 128, HW), lambda i: (NT - 1 - i, 0, 0)), tok, hbm],
        out_specs=[pl.BlockSpec((NSUB * TH, NCOL // 2), lambda i: (NT - 1 - i, 0)), const((8, D)), hbm, hbm],
        out_shape=[jax.ShapeDtypeStruct((T, NCOL // 2), BF16), jax.ShapeDtypeStruct((8, D), F32),
                   jax.ShapeDtypeStruct((128, D), F32), jax.ShapeDtypeStruct((3, 128, D), BF16)],
        scratch_shapes=[pltpu.VMEM((128, HW), F32), pltpu.VMEM((1, HW), F32), pltpu.VMEM((NSUB, TH, HW), F32),
                        pltpu.VMEM((NSUB, TH, HW), F32), pltpu.VMEM((NSUB, TH, HW), F32),
                        pltpu.VMEM((NSUB, 8, HW), F32),
                        pltpu.VMEM((D, D), F32), pltpu.VMEM((4, 128, D), F32), pltpu.VMEM((512, D), F32),
                        pltpu.VMEM((512, D), BF16),
                        pltpu.SemaphoreType.DMA((3,)), pltpu.SemaphoreType.DMA((3,)), pltpu.SemaphoreType.DMA((2,)),
                        pltpu.SemaphoreType.DMA((4,)), pltpu.SemaphoreType.DMA((4,))],
        compiler_params=_cp(("arbitrary",)),
    )(hq, hf, hi, lbr, tri, trit, drec, sall, dhg, gw)


def _fwd_out(o1, o4, o16, l1, l4, l16, rec, ag, hg, x, tgt, anw, hnw, fnw, wout_full, gmat, emat, selmat):
    TT = 512

    def body(o1_r, o4_r, o16_r, l1_r, l4_r, l16_r, rec_r, ag_r, hg_r, x_r, tgt_r, anw_r, hnw_r, fnw_r, wo_r, g_r,
             e_r, sel_r, dx2_o, do1_o, do4_o, do16_o, st1_o, st4_o, st16_o, drec_o, dag_o, dhg_o,
             gw_o, small_o, scr_a, scr_b, scr_c, gwout_o, out_sem):
        @pl.when(pl.program_id(0) == 0)
        def _():
            gwout_o[...] = jnp.zeros_like(gwout_o)
            small_o[...] = jnp.zeros_like(small_o)

        def unperm(r4, r16):
            return _unperm_load(r4, r16, scr_a, scr_b, scr_c)

        def perm_out(val, p1, p4, p16, dt):
            _perm_store(val, scr_a, scr_b, p1, p4, p16, dt)

        o4u, o16u = unperm(o4_r, o16_r)
        l4c, l16c = unperm(l4_r, l16_r)
        l1c = l1_r[...]
        mxc = jnp.maximum(jnp.maximum(l1c, l4c), l16c)
        w1c, w4c, w16c = jnp.exp(l1c - mxc), jnp.exp(l4c - mxc), jnp.exp(l16c - mxc)
        denc = w1c + w4c + w16c
        lane = lax.broadcasted_iota(jnp.int32, (1, 128), 1)
        lse_c = jnp.where(lane < 8, mxc + jnp.log(denc), 0.0)
        em = e_r[...]
        wn1 = _mm_exact_r(w1c / denc, em)
        wn4 = _mm_exact_r(w4c / denc, em)
        o1v = o1_r[...].astype(F32)
        attn = wn1 * o1v + wn4 * o4u + (1.0 - wn1 - wn4) * o16u
        gm = g_r[...]

        def head_mean_a(t):
            return jnp.concatenate([_mm_exact_r(t[:, :256], gm), _mm_exact_r(t[:, 256:], gm)], axis=1)

        def head_mean_h(t):
            return jnp.concatenate(
                [jnp.broadcast_to(jnp.mean(t[:, h * 128:(h + 1) * 128], axis=-1, keepdims=True), (TT, 128))
                 for h in range(4)], axis=1)

        rs_a = lax.rsqrt(head_mean_a(attn * attn) + EPS)
        n_a = attn * rs_a
        agv = ag_r[...].astype(F32)
        sg_a = _sigmoid(agv)
        si_a = agv * sg_a
        anw_v = anw_r[...]
        y_a = (n_a * anw_v) * si_a
        recv = rec_r[...].astype(F32)
        rs_h = lax.rsqrt(head_mean_h(recv * recv) + EPS)
        n_h = recv * rs_h
        hgv = hg_r[...].astype(F32)
        sg_h = _sigmoid(hgv)
        si_h = hgv * sg_h
        hnw_v = hnw_r[...]
        y_h = (n_h * hnw_v) * si_h
        mixed = jnp.concatenate([y_a, y_h], axis=1).astype(BF16)
        xv = x_r[...]
        x2 = xv + _mm(mixed, wo_r[...])
        r2 = lax.rsqrt(jnp.mean(x2 * x2, axis=-1, keepdims=True) + EPS)
        fnw_v = fnw_r[...]
        xn = x2 * r2
        err = xn * fnw_v - tgt_r[...]
        small_o[2:3, :] += 0.5 * jnp.sum(jnp.mean(err * err, axis=-1, keepdims=True), axis=0, keepdims=True)
        small_o[0:1, :] += jnp.sum(err * xn, axis=0, keepdims=True) * (1.0 / D)
        dyw = err * (fnw_v * (1.0 / D))
        dx2 = r2 * dyw - x2 * ((r2 * r2 * r2) * jnp.mean(dyw * x2, axis=-1, keepdims=True))
        dx2_o[...] = dx2
        dx2b = dx2.astype(BF16)
        gwout_o[...] += _mm_tn(mixed, dx2b)
        dmix = _mm_nt(dx2b, wo_r[...])
        dm_a, dm_h = dmix[:, :AW], dmix[:, AW:]
        dag_o[...] = (dm_a * (n_a * anw_v) * (sg_a * (1.0 + agv * (1.0 - sg_a)))).astype(BF16)
        dy_a = dm_a * si_a
        dn_a = dy_a * anw_v
        small_o[1:2, 0:AW] += jnp.sum(dy_a * n_a, axis=0, keepdims=True)
        dattn = rs_a * (dn_a - n_a * head_mean_a(dn_a * n_a))
        perm_out(dattn, do1_o, do4_o, do16_o, BF16)
        stats = lse_c + _mm_exact_r(dattn * attn, sel_r[...])
        perm_out(stats, st1_o, st4_o, st16_o, F32)
        dhg_o[...] = (dm_h * (n_h * hnw_v) * (sg_h * (1.0 + hgv * (1.0 - sg_h)))).astype(BF16)
        dy_h = dm_h * si_h
        dn_h = dy_h * hnw_v
        small_o[1:2, AW:] += jnp.sum(dy_h * n_h, axis=0, keepdims=True)
        drec_o[...] = (rs_h * (dn_h - n_h * head_mean_h(dn_h * n_h))).astype(BF16)

        @pl.when(pl.program_id(0) == T // TT - 1)
        def _():
            out = pltpu.make_async_copy(gwout_o, gw_o, out_sem.at[0])
            out.start()
            out.wait()

    tok = lambda w: pl.BlockSpec((TT, w), lambda i: (i, 0))
    d4 = pl.BlockSpec((4, TT // 4, AW), lambda i: (0, i, 0))
    d16 = pl.BlockSpec((16, TT // 16, AW), lambda i: (0, i, 0))
    const = lambda shape: pl.BlockSpec(shape, lambda i: (0,) * len(shape))
    sd = lambda shape, dt: jax.ShapeDtypeStruct(shape, dt)
    c4 = pl.BlockSpec((4, TT // 4, 128), lambda i: (0, i, 0))
    c16 = pl.BlockSpec((16, TT // 16, 128), lambda i: (0, i, 0))
    p3 = lambda w, dt: [sd((T, w), dt), sd((4, T // 4, w), dt), sd((16, T // 16, w), dt)]
    return pl.pallas_call(
        body, name="fwd_out", grid=(T // TT,),
        in_specs=[tok(AW), d4, d16, tok(128), c4, c16, tok(AW), tok(AW), tok(AW), tok(D), tok(D),
                  const((1, AW)), const((1, HW)), const((1, D)), const((D, D)), const((256, 256)),
                  const((128, AW)), const((AW, 128))],
        out_specs=[tok(D)] + [tok(AW), d4, d16] + [tok(128), c4, c16] + [tok(AW)] * 3
        + [pl.BlockSpec(memory_space=pltpu.HBM), const((8, D))],
        out_shape=[sd((T, D), F32)] + p3(AW, BF16) + p3(128, F32)
        + [sd((T, AW), BF16), sd((T, AW), BF16), sd((T, AW), BF16), sd((D, D), F32), sd((8, D), F32)],
        scratch_shapes=[pltpu.VMEM((4, TT, 128), F32)] * 3 + [pltpu.VMEM((D, D), F32),
                        pltpu.SemaphoreType.DMA((1,))],
        compiler_params=_cp(("arbitrary",)),
    )(o1, o4, o16, l1, l4, l16, rec, ag, hg, x, tgt, anw, hnw, fnw, wout_full, gmat, emat, selmat)


def _dproj_build(dq, dk, dv, dag, pos):
    TT = 512

    def body(dq1, dq4, dq16, dk1, dk4, dk16, dv1, dv4, dv16, dag_r, pos_r, dproj_o, scr_b, scr_c):
        def unperm_sum(r1, r4, r16):
            return r1[...] + _unperm_sum(r4, r16, scr_b, scr_c)

        cosf, s1, s2 = _rope_tables(pos_r[...])
        dproj_o[:, 0:512] = _rope_bwd(unperm_sum(dq1, dq4, dq16), cosf, s1, s2).astype(BF16)
        dproj_o[:, 512:1024] = _rope_bwd(unperm_sum(dk1, dk4, dk16), cosf, s1, s2).astype(BF16)
        dproj_o[:, 1024:1536] = unperm_sum(dv1, dv4, dv16).astype(BF16)
        dproj_o[:, 1536:2048] = dag_r[...]

    tok = lambda w: pl.BlockSpec((TT, w), lambda i: (i, 0))
    d4 = pl.BlockSpec((4, TT // 4, AW), lambda i: (0, i, 0))
    d16 = pl.BlockSpec((16, TT // 16, AW), lambda i: (0, i, 0))
    return pl.pallas_call(
        body, name="dproj_build", grid=(T // TT,),
        in_specs=[tok(AW), d4, d16] * 3 + [tok(AW), pl.BlockSpec((1, TT), lambda i: (0, i))],
        out_specs=tok(NCOL // 2),
        out_shape=jax.ShapeDtypeStruct((T, NCOL // 2), BF16),
        scratch_shapes=[pltpu.VMEM((4, TT, 128), F32)] * 2,
        compiler_params=_cp(("parallel",)),
    )(*dq, *dk, *dv, dag, pos)


def _bwd_x(dproj_a, dproj_h, x, dx2, mixw, w_full, rinb, small4, small6, pout_own, pout_rem, raw):
    TT = 512
    NT = T // TT

    def body(dpa_r, dph_r, x_r, dx2_r, mw_r, w_r, rinb_r, s4_r, s6_r, poo_r, por_r, raw_r,
             gx_o, sall_o, fin_o, fout_o, sbuf, v_own, v_rem, vo_own, vo_rem, sin, sout, got_in,
             got_out, v_send, v_got, send_sems, recv_sems, loc_sems, share_send, share_recv, fin_sems, raw_sems,
             hand_send, hand_recv):
        i = pl.program_id(0)
        mx, my, c = lax.axis_index("x"), lax.axis_index("y"), lax.axis_index("c")
        slot_of = lambda ref, b: ref.at[lax.rem(b - (2 * mx + my) + 3, 4)]
        _, rem = _chip_copies(slot_of, rinb_r, rinb_r, v_own, v_rem, send_sems, recv_sems, loc_sems.at[0])
        loc = [pltpu.make_async_copy(raw_r.at[pl.ds(pl.multiple_of(c * 512, 512), 512), :], v_own, loc_sems.at[0])]
        load_theirs = pltpu.make_async_copy(raw_r.at[pl.ds(pl.multiple_of((1 - c) * 512, 512), 512), :], v_send,
                                            raw_sems.at[0])
        hand = pltpu.make_async_remote_copy(src_ref=v_send, dst_ref=v_got, send_sem=hand_send.at[0],
                                            recv_sem=hand_recv.at[0], device_id=(mx, my, 1 - c), device_id_type=MESH)
        loads = [pltpu.make_async_copy(poo_r, vo_own, fin_sems.at[2]),
                 pltpu.make_async_copy(por_r, vo_rem, fin_sems.at[3])]

        @pl.when(i == 0)
        def _():
            sbuf[...] = jnp.zeros_like(sbuf)
            for cp in loc + rem + loads:
                cp.start()
            load_theirs.start()

        @pl.when(i == 1)
        def _():
            load_theirs.wait()
            hand.start()

        dhn = _mm_nt(dpa_r[...], w_r[:, 0:NCOL // 2]) + _mm_nt(dph_r[...], w_r[:, NCOL // 2:NCOL])
        xv = x_r[...]
        r = lax.rsqrt(jnp.mean(xv * xv, axis=-1, keepdims=True) + EPS)
        dxw = dhn * mw_r[...]
        gx_o[...] = dx2_r[...] + r * dxw - xv * ((r * r * r) * jnp.mean(dxw * xv, axis=-1, keepdims=True))
        sbuf[16:17, :] += jnp.sum(dhn * (xv * r), axis=0, keepdims=True)

        @pl.when(i == NT - 1)
        def _():
            sbuf[0:8, :] = s4_r[...]
            sbuf[8:16, :] = s6_r[...]
            sloc, srem = _small_copies(sbuf, sall_o, send_sems, recv_sems, loc_sems.at[1])
            for cp in sloc + srem:
                cp.start()
            for cp in rem:
                cp.wait_recv()
            for cp in rem:
                cp.wait_send()
            for cp in loc:
                cp.wait()
            for cp in loads:
                cp.wait()
            hand.wait_recv()
            hand.wait_send()
            sout[...] = ((vo_own[...] + vo_rem[0].astype(F32)) + vo_rem[1].astype(F32)) + vo_rem[2].astype(F32)
            sin[...] = (((v_own[...] + v_got[...]) + v_rem[0].astype(F32)) + v_rem[1].astype(F32)) + v_rem[2].astype(F32)
            swap = [pltpu.make_async_remote_copy(src_ref=sin, dst_ref=got_in, send_sem=share_send.at[0],
                                                 recv_sem=share_recv.at[0], device_id=(mx, my, 1 - c),
                                                 device_id_type=MESH),
                    pltpu.make_async_remote_copy(src_ref=sout, dst_ref=got_out, send_sem=share_send.at[1],
                                                 recv_sem=share_recv.at[1], device_id=(mx, my, 1 - c),
                                                 device_id_type=MESH)]
            for cp in swap:
                cp.start()
            mine = [pltpu.make_async_copy(sin, fin_o.at[c], fin_sems.at[0]),
                    pltpu.make_async_copy(sout, fout_o.at[c], fin_sems.at[1])]
            for cp in mine:
                cp.start()
            for cp in swap:
                cp.wait_recv()
            theirs = [pltpu.make_async_copy(got_in, fin_o.at[1 - c], fin_sems.at[2]),
                      pltpu.make_async_copy(got_out, fout_o.at[1 - c], fin_sems.at[3])]
            for cp in theirs:
                cp.start()
            for cp in swap:
                cp.wait_send()
            for cp in mine + theirs:
                cp.wait()
            for cp in srem:
                cp.wait_recv()
            for cp in srem:
                cp.wait_send()
            for cp in sloc:
                cp.wait()

    tok = lambda w: pl.BlockSpec((TT, w), lambda i: (i, 0))
    const = lambda shape: pl.BlockSpec(shape, lambda i: (0,) * len(shape))
    hbm = pl.BlockSpec(memory_space=pltpu.HBM)
    return pl.pallas_call(
        body, name="bwd_x", grid=(NT,),
        in_specs=[tok(NCOL // 2), tok(NCOL // 2), tok(D), tok(D), const((1, D)), const((D, NCOL)), hbm,
                  const((8, D)), const((8, D)), hbm, hbm, hbm],
        out_specs=[tok(D), hbm, hbm, hbm],
        out_shape=[jax.ShapeDtypeStruct((T, D), F32),
                   jax.ShapeDtypeStruct((8, 24, D), F32),
                   jax.ShapeDtypeStruct((2, 512, 1024), F32), jax.ShapeDtypeStruct((2, 128, D), F32)],
        scratch_shapes=[pltpu.VMEM((24, D), F32),
                        pltpu.VMEM((512, 1024), F32), pltpu.VMEM((3, 512, 1024), BF16),
                        pltpu.VMEM((128, D), F32), pltpu.VMEM((3, 128, D), BF16),
                        pltpu.VMEM((512, 1024), F32), pltpu.VMEM((128, D), F32),
                        pltpu.VMEM((512, 1024), F32), pltpu.VMEM((128, D), F32),
                        pltpu.VMEM((512, 1024), F32), pltpu.VMEM((512, 1024), F32),
                        pltpu.SemaphoreType.DMA((10,)), pltpu.SemaphoreType.DMA((10,)), pltpu.SemaphoreType.DMA((2,)),
                        pltpu.SemaphoreType.DMA((2,)), pltpu.SemaphoreType.DMA((2,)), pltpu.SemaphoreType.DMA((4,)),
                        pltpu.SemaphoreType.DMA((1,)), pltpu.SemaphoreType.DMA((1,)), pltpu.SemaphoreType.DMA((1,))],
        compiler_params=_cp(("arbitrary",)),
    )(dproj_a, dproj_h, x, dx2, mixw, w_full, rinb, small4, small6, pout_own, pout_rem, raw)


def _grad_w_in(hn, dproj_a, dproj_h, jm_arr):
    TK = 2048
    NK = T // TK

    def block_at(j, jm):
        return lax.rem(jm + 1 + j, 4)

    def body(jm_ref, hnt_r, dpa_r, dph_r, rinb_o, raw_o, acc, rbuf, obufb, send_sems, recv_sems, wb_sems):
        j = pl.program_id(0)
        kk = pl.program_id(1)
        x, y, c = lax.axis_index("x"), lax.axis_index("y"), lax.axis_index("c")
        mine = pl.ds(pl.multiple_of(c * 512, 512), 512)
        theirs = pl.ds(pl.multiple_of((1 - c) * 512, 512), 512)

        def send(jj):
            return pltpu.make_async_remote_copy(
                src_ref=acc.at[jj % 2, theirs, :], dst_ref=rbuf.at[jj], send_sem=send_sems.at[jj],
                recv_sem=recv_sems.at[jj], device_id=(x, y, 1 - c), device_id_type=MESH)

        def writeback(jj):
            return [pltpu.make_async_copy(obufb.at[jj % 2], rinb_o.at[jj], wb_sems.at[2 + jj % 2])]

        def wait_writeback(jj):
            for cp in writeback(jj):
                cp.wait()

        def finalize(jj):
            send(jj).wait_recv()
            obufb[jj % 2] = (acc[jj % 2, mine, :] + rbuf[jj]).astype(BF16)
            for cp in writeback(jj):
                cp.start()

        blk = block_at(j, jm_ref[0])
        prod = _mm(hnt_r[...], jnp.where(blk < 2, dpa_r[...], dph_r[...]))

        @pl.when(kk == 0)
        def _():
            for jj in (2, 3):
                @pl.when(j == jj)
                def _():
                    send(jj - 2).wait_send()
            acc[j % 2] = prod

        @pl.when(kk > 0)
        def _():
            acc[j % 2] += prod

        @pl.when(kk == NK - 1)
        def _():
            for jj in range(4):
                @pl.when(j == jj)
                def _():
                    if jj < 3:
                        send(jj).start()
                    if jj in (1, 2):
                        finalize(jj - 1)
                    if jj == 3:
                        raw = pltpu.make_async_copy(acc.at[1], raw_o, wb_sems.at[4])
                        raw.start()
                        wait_writeback(0)
                        finalize(2)
                        wait_writeback(1)
                        wait_writeback(2)
                        raw.wait()
                        send(2).wait_send()

    def used(is_mine, kk, col):
        return jnp.where(is_mine, kk, 0), jnp.where(is_mine, col, 0)

    hbm = pl.BlockSpec(memory_space=pltpu.HBM)
    grid_spec = pltpu.PrefetchScalarGridSpec(
        num_scalar_prefetch=1, grid=(4, NK),
        in_specs=[pl.BlockSpec((D, TK), lambda j, kk, jm_ref: (0, kk)),
                  pl.BlockSpec((TK, 1024), lambda j, kk, jm_ref: used(
                      block_at(j, jm_ref[0]) < 2, kk, block_at(j, jm_ref[0]))),
                  pl.BlockSpec((TK, 1024), lambda j, kk, jm_ref: used(
                      block_at(j, jm_ref[0]) >= 2, kk, block_at(j, jm_ref[0]) - 2))],
        out_specs=[hbm, hbm],
        scratch_shapes=[pltpu.VMEM((2, D, 1024), F32), pltpu.VMEM((3, 512, 1024), F32),
                        pltpu.VMEM((2, 512, 1024), BF16),
                        pltpu.SemaphoreType.DMA((3,)), pltpu.SemaphoreType.DMA((3,)), pltpu.SemaphoreType.DMA((5,))])
    return pl.pallas_call(
        body, name="grad_w_in", grid_spec=grid_spec,
        out_shape=[jax.ShapeDtypeStruct((3, 512, 1024), BF16), jax.ShapeDtypeStruct((D, 1024), F32)],
        compiler_params=_cp(("arbitrary", "arbitrary")),
    )(jm_arr, hn, dproj_a, dproj_h)


def _w_out_piece(ref, j):
    return ref.at[pl.ds(j * 128, 128), :]


def _chip_copies(piece, src_r, srcb_r, own_o, rem_o, send_sems, recv_sems, loc_sem):
    x, y, c = lax.axis_index("x"), lax.axis_index("y"), lax.axis_index("c")
    chips = [(1 - x, y), (x, 1 - y), (1 - x, 1 - y)]
    loc = [pltpu.make_async_copy(piece(src_r, 2 * x + y), own_o, loc_sem)]
    rem = [pltpu.make_async_remote_copy(
        src_ref=piece(srcb_r, 2 * px + py), dst_ref=rem_o.at[k], send_sem=send_sems.at[k],
        recv_sem=recv_sems.at[k], device_id=(px, py, c), device_id_type=MESH) for k, (px, py) in enumerate(chips)]
    return loc, rem


def _small_copies(small_r, sall_o, send_sems, recv_sems, loc_sem):
    x, y, c = lax.axis_index("x"), lax.axis_index("y"), lax.axis_index("c")
    me = 4 * x + 2 * y + c
    loc = [pltpu.make_async_copy(small_r, sall_o.at[me], loc_sem)]
    rem = []
    k = 3
    for fx in range(2):
        for fy in range(2):
            for fc in range(2):
                if fx or fy or fc:
                    peer = (1 - x if fx else x, 1 - y if fy else y, 1 - c if fc else c)
                    rem.append(pltpu.make_async_remote_copy(
                        src_ref=small_r, dst_ref=sall_o.at[me], send_sem=send_sems.at[k],
                        recv_sem=recv_sems.at[k], device_id=peer, device_id_type=MESH))
                    k += 1
    return loc, rem


def _adamw_math(w, g, m, v):
    m = B1 * m + (1.0 - B1) * g
    v = B2 * v + (1.0 - B2) * (g * g)
    m_hat = m / (1.0 - B1 ** STEP)
    v_hat = v / (1.0 - B2 ** STEP)
    delta = -LR * (m_hat / (jnp.sqrt(v_hat) + AEPS) + WD * w)
    return delta, m, v


def _adamw(big_in, big_out, sall, params):
    def body(*refs):
        wi, gi, mi, vi, wo, go, mo, vo, sall_r = refs[:9]
        ins = refs[9:24]
        di_o, mi_o, vi_o, do_o, mo_o, vo_o = refs[24:30]
        outs = refs[30:]
        d, mm, vv = _adamw_math(wi[...], gi[...], mi[...], vi[...])
        di_o[...] = d
        mi_o[...] = mm
        vi_o[...] = vv

        @pl.when(pl.program_id(0) == 0)
        def _():
            d, mm, vv = _adamw_math(wo[...], go[...], mo[...], vo[...])
            do_o[...] = d
            mo_o[...] = mm
            vo_o[...] = vv
            tot = sall_r[0]
            for dv in range(1, 8):
                tot = tot + sall_r[dv]
            grads = [tot[16:17, :], tot[1:2, 0:AW], tot[1:2, AW:], tot[8:10, 0:HW], tot[0:1, :]]
            outs[0][...] = tot[2:3, 0:1]
            for p in range(5):
                w_r, m_r, v_r = ins[3 * p:3 * p + 3]
                g = grads[p]
                d, mm, vv = _adamw_math(w_r[...], g, m_r[...], v_r[...])
                outs[1 + 4 * p][...] = g
                outs[2 + 4 * p][...] = d
                outs[3 + 4 * p][...] = mm
                outs[4 + 4 * p][...] = vv

    flat = [a for p in params for a in p]
    shapes = [jax.ShapeDtypeStruct((D, 1024), F32)] * 3 + [jax.ShapeDtypeStruct((256, D), F32)] * 3
    shapes += [jax.ShapeDtypeStruct((1, 1), F32)]
    for p in params:
        shapes += [jax.ShapeDtypeStruct(p[0].shape, F32)] * 4
    vm = pl.BlockSpec(memory_space=pltpu.VMEM)
    rows = pl.BlockSpec((512, 1024), lambda i: (i, 0))
    whole = pl.BlockSpec((256, D), lambda i: (0, 0))
    return pl.pallas_call(
        body, name="adamw", grid=(2,),
        in_specs=[rows] * 4 + [whole] * 4 + [vm] * 16, out_specs=[rows] * 3 + [whole] * 3 + [vm] * 21,
        out_shape=shapes,
        compiler_params=_cp(("arbitrary",)),
    )(*big_in, *big_out, sall, *flat)


def kernel(x, positions, w_in, w_out, mix_norm_w, attn_out_norm_w, hgrn_out_norm_w, hgrn_lb_raw, final_norm_w, loss_target, m_w_in, m_w_out, m_mix_norm_w, m_attn_out_norm_w, m_hgrn_out_norm_w, m_hgrn_lb_raw, m_final_norm_w, v_w_in, v_w_out, v_mix_norm_w, v_attn_out_norm_w, v_hgrn_out_norm_w, v_hgrn_lb_raw, v_final_norm_w):
    xs = x.reshape(T, D)
    tgt = loss_target.reshape(T, D)
    pos = positions.reshape(1, T)
    fnw = final_norm_w.reshape(1, D)

    ti = np.arange(TH)
    tri_np = ((ti[:, None] // CHUNK == ti[None, :] // CHUNK) & (ti[None, :] <= ti[:, None])).astype(np.float32)
    tri = jnp.asarray(tri_np, BF16)
    trit = jnp.asarray(tri_np.T, BF16)
    hi_ = np.arange(AW) // HEAD
    gmat = jnp.asarray((hi_[:256, None] == hi_[None, :256]).astype(np.float32) / HEAD, BF16)
    emat_np = (np.arange(128)[:, None] == hi_[None, :]).astype(np.float32)
    sel_np = (8 + hi_[:, None] == np.arange(128)[None, :]).astype(np.float32)
    emat = jnp.asarray(emat_np, BF16)
    selmat = jnp.asarray(sel_np, BF16)

    jm_arr = (2 * lax.axis_index("x") + lax.axis_index("y")).astype(jnp.int32).reshape(1)
    (hn, q1, k1, v1, q4, k4, v4, q16, k16, v16, ag, hq, hf, hi, hg, w_full, wout4) = _fwd_in(
        xs, pos, mix_norm_w, w_in.reshape(D, 1024), w_out.reshape(256, D), jm_arr)
    wout_full = wout4.reshape(D, D)
    flat = lambda a: a.reshape(T, AW)
    o1, l1 = _attn_fwd(q1, k1, v1, T // BLK, "attn_fwd_d1")
    o4, l4 = _attn_fwd(flat(q4), flat(k4), flat(v4), T // 4 // BLK, "attn_fwd_d4")
    o16, l16 = _attn_fwd(flat(q16), flat(k16), flat(v16), T // 16 // BLK, "attn_fwd_d16")
    rec, sall = _hgrn_fwd(hq, hf, hi, hgrn_lb_raw, tri)

    (dx2, do1, do4, do16, st1, st4, st16, drec, dag, dhg, gw, small4) = _fwd_out(
        o1, o4.reshape(4, T // 4, AW), o16.reshape(16, T // 16, AW),
        l1, l4.reshape(4, T // 4, 128), l16.reshape(16, T // 16, 128),
        rec, ag, hg, xs, tgt, attn_out_norm_w, hgrn_out_norm_w, fnw, wout_full, gmat, emat, selmat)

    fst = lambda a: a.reshape(T, 128)
    dq1, dk1, dv1 = _attn_bwd(q1, k1, v1, do1, st1, T // BLK, "attn_bwd_d1")
    dq4, dk4, dv4 = _attn_bwd(flat(q4), flat(k4), flat(v4), flat(do4), fst(st4), T // 4 // BLK, "attn_bwd_d4")
    dq16, dk16, dv16 = _attn_bwd(flat(q16), flat(k16), flat(v16), flat(do16), fst(st16), T // 16 // BLK,
                                 "attn_bwd_d16")
    dproj_h, small6, pout_own, pout_rem = _hgrn_bwd(hq, hf, hi, hgrn_lb_raw, tri, trit, drec, sall, dhg, gw)

    r4 = lambda a: a.reshape(4, T // 4, AW)
    r16 = lambda a: a.reshape(16, T // 16, AW)
    dproj_a = _dproj_build((dq1, r4(dq4), r16(dq16)), (dk1, r4(dk4), r16(dk16)), (dv1, r4(dv4), r16(dv16)),
                           dag, pos)
    rinb, raw = _grad_w_in(hn, dproj_a, dproj_h, jm_arr)
    gx, small_all, fin, fout = _bwd_x(dproj_a, dproj_h, xs, dx2, mix_norm_w, w_full, rinb,
                                      small4, small6, pout_own, pout_rem, raw)
    g_w_in = fin.reshape(D, 1024)
    g_w_out = fout.reshape(256, D)

    params = [(mix_norm_w, m_mix_norm_w, v_mix_norm_w),
              (attn_out_norm_w, m_attn_out_norm_w, v_attn_out_norm_w),
              (hgrn_out_norm_w, m_hgrn_out_norm_w, v_hgrn_out_norm_w),
              (hgrn_lb_raw, m_hgrn_lb_raw, v_hgrn_lb_raw),
              (fnw, m_final_norm_w.reshape(1, D), v_final_norm_w.reshape(1, D))]
    d_in, nm_in, nv_in, d_out, nm_out, nv_out, *so = _adamw(
        (w_in.reshape(D, 1024), g_w_in, m_w_in.reshape(D, 1024), v_w_in.reshape(D, 1024)),
        (w_out.reshape(256, D), g_w_out, m_w_out.reshape(256, D), v_w_out.reshape(256, D)), small_all, params)
    loss = so[0].reshape(())
    g_s = [so[1 + 4 * p] for p in range(5)]
    d_s = [so[2 + 4 * p] for p in range(5)]
    m_s = [so[3 + 4 * p] for p in range(5)]
    v_s = [so[4 + 4 * p] for p in range(5)]
    for lst in (g_s, d_s, m_s, v_s):
        lst[4] = lst[4].reshape(D)

    return (loss, gx.reshape(1, T, D),
            g_w_in.reshape(1, D, 1024), g_w_out.reshape(1, 256, D), *g_s,
            d_in.reshape(1, D, 1024), d_out.reshape(1, 256, D), *d_s,
            nm_in.reshape(1, D, 1024), nm_out.reshape(1, 256, D), *m_s,
            nv_in.reshape(1, D, 1024), nv_out.reshape(1, 256, D), *v_s)
```

```python
import functools

import numpy as np
import jax
import jax.numpy as jnp
from jax import lax
from jax.experimental import pallas as pl
from jax.experimental.pallas import tpu as pltpu

F32 = jnp.float32
BF16 = jnp.bfloat16

T = 4096
D = 1024
AW = 512
HW = 512
NCOL = 4096
HEAD = 64
BLK = 128
CHUNK = 64
EPS = 1e-6
SCALE = HEAD ** -0.5
NEG = -1e30
ROPE_THETA = 500000.0
INV_FREQ = [float(v) for v in
            (np.float32(ROPE_THETA) ** (-(np.arange(8, dtype=np.float32)) * np.float32(0.125)))]
LR, B1, B2, AEPS, WD, STEP = 0.001, 0.9, 0.999, 1e-08, 0.01, 10
VMEM_LIMIT = 63 * 1024 * 1024
MESH = pl.DeviceIdType.MESH


def _cp(sem=None, **kw):
    return pltpu.CompilerParams(dimension_semantics=sem, vmem_limit_bytes=VMEM_LIMIT, **kw)


def _mm(a, b):
    return jnp.dot(a, b, preferred_element_type=F32)


def _mm_nt(a, b):
    return lax.dot_general(a, b, (((1,), (1,)), ((), ())), preferred_element_type=F32)


def _mm_tn(a, b):
    return lax.dot_general(a, b, (((0,), (0,)), ((), ())), preferred_element_type=F32)


def _mm_exact_l(mat_bf, x):
    h = x.astype(BF16)
    l = (x - h.astype(F32)).astype(BF16)
    return _mm(mat_bf, h) + _mm(mat_bf, l)


def _mm_exact_r(x, mat_bf):
    h = x.astype(BF16)
    l = (x - h.astype(F32)).astype(BF16)
    return _mm(h, mat_bf) + _mm(l, mat_bf)


def _sigmoid(x):
    return 0.5 * jnp.tanh(0.5 * x) + 0.5


def _rope_tables(pos):
    lane = lax.broadcasted_iota(jnp.int32, (1, 128), 1)
    jl = lane & 63
    fi = jl & 7
    inv = jnp.zeros((1, 128), F32)
    for kk in range(8):
        inv = jnp.where(fi == kk, INV_FREQ[kk], inv)
    ang = jnp.broadcast_to(pos.astype(F32), (128, pos.shape[1])).T * inv
    c = jnp.cos(ang)
    s = jnp.sin(ang)
    cosf = jnp.where(jl < 16, c, 1.0)
    s1 = jnp.where(jl < 8, -s, 0.0)
    s2 = jnp.where((jl >= 8) & (jl < 16), s, 0.0)
    return cosf, s1, s2


def _rope(t, cosf, s1, s2):
    parts = []
    for ci in range(t.shape[1] // 128):
        tc = t[:, ci * 128:(ci + 1) * 128]
        parts.append(tc * cosf + pltpu.roll(tc, 120, 1) * s1 + pltpu.roll(tc, 8, 1) * s2)
    return jnp.concatenate(parts, axis=1)


def _rope_bwd(g, cosf, s1, s2):
    parts = []
    for ci in range(g.shape[1] // 128):
        gc = g[:, ci * 128:(ci + 1) * 128]
        parts.append(gc * cosf + pltpu.roll(gc * s1, 8, 1) + pltpu.roll(gc * s2, 120, 1))
    return jnp.concatenate(parts, axis=1)


def _perm_store(val, scr, scr2, o1, o4, o16, dt):
    n = val.shape[0]
    q = n // 4
    o1[...] = val.astype(dt)
    for ci in range(val.shape[1] // 128):
        cs = slice(ci * 128, (ci + 1) * 128)
        scr[ci] = val[:, cs]
        for r4 in range(4):
            part = scr[ci, pl.ds(r4, q, stride=4), :]
            o4[r4, :, cs] = part.astype(dt)
            scr2[ci, r4 * q:(r4 + 1) * q, :] = part
        for r4 in range(4):
            for b in range(4):
                o16[r4 + 4 * b, :, cs] = scr2[ci, pl.ds(r4 * q + b, q // 4, stride=4), :].astype(dt)


def _unperm_load(r4, r16, scr_a, scr_b, scr_c):
    n = scr_a.shape[1]
    q = n // 4
    nc = r4.shape[-1] // 128
    for ci in range(nc):
        cs = slice(ci * 128, (ci + 1) * 128)
        for rr in range(4):
            scr_a[ci, pl.ds(rr, q, stride=4), :] = r4[rr, :, cs].astype(F32)
        for rr in range(4):
            for b in range(4):
                scr_c[ci, pl.ds(rr * q + b, q // 4, stride=4), :] = r16[rr + 4 * b, :, cs].astype(F32)
        for rr in range(4):
            scr_b[ci, pl.ds(rr, q, stride=4), :] = scr_c[ci, rr * q:(rr + 1) * q, :]
    return (jnp.concatenate([scr_a[ci] for ci in range(nc)], axis=1),
            jnp.concatenate([scr_b[ci] for ci in range(nc)], axis=1))


def _unperm_sum(r4, r16, scr_b, scr_c):
    n = scr_b.shape[1]
    q = n // 4
    nc = r4.shape[-1] // 128
    for ci in range(nc):
        cs = slice(ci * 128, (ci + 1) * 128)
        for rr in range(4):
            for b in range(4):
                scr_c[ci, pl.ds(rr * q + b, q // 4, stride=4), :] = r16[rr + 4 * b, :, cs].astype(F32)
        for rr in range(4):
            scr_b[ci, pl.ds(rr, q, stride=4), :] = scr_c[ci, rr * q:(rr + 1) * q, :] + r4[rr, :, cs].astype(F32)
    return jnp.concatenate([scr_b[ci] for ci in range(nc)], axis=1)


def _fwd_in(x, pos, mixw, w_in, w_out, jm_arr):
    TT = 512
    NT = T // TT

    def body(jm_ref, x_ref, pos_ref, mw_ref, win_ref, wout_ref,
             hnt_ref, q1, k1, v1, q4, k4, v4, q16, k16, v16, ag, hq, hf, hi, hg, wfull_o, woutfull_o,
             wbuf, wobuf, hn_all, scr, scr2, stage, send_sems, recv_sems, loc_sems):
        s = pl.program_id(0)
        i = pl.program_id(1)
        mx, my, c = lax.axis_index("x"), lax.axis_index("y"), lax.axis_index("c")
        me, sibling = (mx, my, c), (mx, my, 1 - c)
        chips = [(mx, 1 - my), (1 - mx, my), (1 - mx, 1 - my)]
        jm = 2 * mx + my
        rows_in = [pl.ds(pl.multiple_of(h * 512, 512), 512) for h in (c, 1 - c)]
        rows_out = [pl.ds(pl.multiple_of(h * 128, 128), 128) for h in (c, 1 - c)]

        def blk(k):
            return lax.bitwise_xor(jm, k + 1)

        def rc(n, ref, to):
            return pltpu.make_async_remote_copy(src_ref=ref, dst_ref=ref, send_sem=send_sems.at[n],
                                                recv_sem=recv_sems.at[n], device_id=to, device_id_type=MESH)

        halves = [pl.ds(0, 512), pl.ds(512, 512)]
        send_in = lambda k, h: rc(12 + 2 * k + h, wbuf.at[jm, rows_in[0], halves[h]], (*chips[k], c))
        got_in = lambda k, h: rc(12 + 2 * k + h, wbuf.at[blk(k), rows_in[0], halves[h]], me)
        relay = lambda h: rc(16 + h, wbuf.at[blk(h), rows_in[0], halves[h]], (*chips[1 - h], c))
        got_relay = lambda h: rc(16 + h, wbuf.at[blk(2), rows_in[0], halves[h]], me)
        send_out = lambda k: rc(3 + k, wobuf.at[jm, rows_out[0], :], (*chips[k], c))
        got_out = lambda k: rc(3 + k, wobuf.at[blk(k), rows_out[0], :], me)
        pass_in = lambda k: rc(6 + k, wbuf.at[blk(k), rows_in[0], :], sibling)
        pass_out = lambda k: rc(9 + k, wobuf.at[blk(k), rows_out[0], :], sibling)
        passed_in = lambda k: rc(6 + k, wbuf.at[blk(k), rows_in[1], :], me)
        passed_out = lambda k: rc(9 + k, wobuf.at[blk(k), rows_out[1], :], me)

        def keep(j, n):
            return pltpu.make_async_copy(wbuf.at[j], wfull_o.at[:, pl.ds(j * 1024, 1024)], loc_sems.at[n])

        @pl.when((s == 0) & (i == 0))
        def _():
            chunk = [pl.ds(pl.multiple_of(lax.rem(p + 2 * c, 4) * 256, 256), 256) for p in range(4)]
            loads = [pltpu.make_async_copy(win_ref.at[chunk[p], :] if p < 4 else wout_ref, stage.at[p % 2],
                                           loc_sems.at[4 + p % 2]) for p in range(5)]
            loads[0].start()
            for p in range(5):
                if p < 4:
                    loads[p + 1].start()
                loads[p].wait()
                if p < 4:
                    wbuf[jm, chunk[p], :] = stage[p % 2].astype(BF16)
                else:
                    wobuf[jm] = stage[p % 2].astype(BF16)
                if p == 1:
                    for k in range(2):
                        for h in range(2):
                            send_in(k, h).start()
            keep(jm, 0).start()

        @pl.when((s == 0) & (i == NT - 1))
        def _():
            for kk in range(2):
                for h in range(2):
                    got_in(kk, h).wait_recv()
            relay(0).start()
            relay(1).start()
            pass_in(0).start()
            pass_in(1).start()
            passed_in(0).wait_recv()
            keep(blk(0), 1).start()

        @pl.when((s == 1) & (i == NT - 1))
        def _():
            got_relay(0).wait_recv()
            got_relay(1).wait_recv()
            pass_in(2).start()

        @pl.when((s == 2) & (i == 0))
        def _():
            for k in (1, 2):
                passed_in(k).wait_recv()
                keep(blk(k), k + 1).start()
            for kk in range(3):
                send_out(kk).start()

        @pl.when((s == 2) & (i == NT - 2))
        def _():
            for k in range(3):
                got_out(k).wait_recv()
                pass_out(k).start()

        whole_out = pltpu.make_async_copy(wobuf, woutfull_o, loc_sems.at[4])

        @pl.when((s == 2) & (i == NT - 1))
        def _():
            for k in range(3):
                passed_out(k).wait_recv()
            whole_out.start()

        tile = pl.ds(pl.multiple_of(i * TT, TT), TT)

        @pl.when(s == 0)
        def _():
            xv = x_ref[...]
            r = lax.rsqrt(jnp.mean(xv * xv, axis=-1, keepdims=True) + EPS)
            hnf = (xv * r) * mw_ref[...]
            hn_all[tile, :] = hnf.astype(BF16)
            hnt_ref[...] = hnf.T.astype(BF16)

        def project(jj):
            hn = hn_all[tile, :]
            lo = _mm(hn, wbuf[jj, :, 0:512])
            hi_cols = _mm(hn, wbuf[jj, :, 512:1024])
            if jj == 0:
                cosf, s1, s2 = _rope_tables(pos_ref[...])
                _perm_store(_rope(lo, cosf, s1, s2) * SCALE, scr, scr2, q1, q4, q16, BF16)
                _perm_store(_rope(hi_cols, cosf, s1, s2), scr, scr2, k1, k4, k16, BF16)
            elif jj == 1:
                _perm_store(lo, scr, scr2, v1, v4, v16, BF16)
                ag[...] = hi_cols.astype(BF16)
            elif jj == 2:
                hq[...] = lo.astype(BF16)
                hf[...] = hi_cols.astype(BF16)
            else:
                hi[...] = lo.astype(BF16)
                hg[...] = hi_cols.astype(BF16)

        def project_block(j):
            for jj in range(4):
                pl.when(j == jj)(functools.partial(project, jj))

        @pl.when(s < 2)
        def _():
            project_block(lax.bitwise_xor(jm, s))

        @pl.when(s == 2)
        def _():
            project_block(lax.bitwise_xor(jm, 2))
            project_block(lax.bitwise_xor(jm, 3))

        @pl.when((s == 2) & (i == NT - 1))
        def _():
            for h in range(2):
                relay(h).wait_send()
                for k in range(2):
                    send_in(k, h).wait_send()
            for k in range(3):
                send_out(k).wait_send()
                pass_in(k).wait_send()
                pass_out(k).wait_send()
            keep(jm, 0).wait()
            for k in range(3):
                keep(blk(k), k + 1).wait()
            whole_out.wait()

    def at_stage_of(jb):
        def index(s, i, jm_ref):
            sa = jnp.minimum(lax.bitwise_xor(jm_ref[0], jb), 2)
            return jnp.where(s < sa, 0, jnp.where(s == sa, i, NT - 1))
        return index

    tok = lambda w, jb: pl.BlockSpec((TT, w), lambda s, i, jm_ref: (at_stage_of(jb)(s, i, jm_ref), 0))
    d4 = lambda jb: pl.BlockSpec((4, TT // 4, AW), lambda s, i, jm_ref: (0, at_stage_of(jb)(s, i, jm_ref), 0))
    d16 = lambda jb: pl.BlockSpec((16, TT // 16, AW), lambda s, i, jm_ref: (0, at_stage_of(jb)(s, i, jm_ref), 0))
    hbm = pl.BlockSpec(memory_space=pltpu.HBM)
    sd = lambda shape, dt: jax.ShapeDtypeStruct(shape, dt)
    in_own_stage = lambda s, i: jnp.where(s == 0, i, NT - 1)
    grid_spec = pltpu.PrefetchScalarGridSpec(
        num_scalar_prefetch=1, grid=(3, NT),
        in_specs=[pl.BlockSpec((TT, D), lambda s, i, jm_ref: (in_own_stage(s, i), 0)),
                  pl.BlockSpec((1, TT), lambda s, i, jm_ref: (0, i)),
                  pl.BlockSpec((1, D), lambda s, i, jm_ref: (0, 0)), hbm, hbm],
        out_specs=[pl.BlockSpec((D, TT), lambda s, i, jm_ref: (0, in_own_stage(s, i))),
                   tok(AW, 0), tok(AW, 0), tok(AW, 1), d4(0), d4(0), d4(1), d16(0), d16(0), d16(1),
                   tok(AW, 1), tok(AW, 2), tok(AW, 2), tok(AW, 3), tok(AW, 3), hbm, hbm],
        scratch_shapes=[pltpu.VMEM((4, D, 1024), BF16), pltpu.VMEM((4, 256, D), BF16), pltpu.VMEM((T, D), BF16),
                        pltpu.VMEM((4, TT, 128), F32), pltpu.VMEM((4, TT, 128), F32), pltpu.VMEM((2, 256, 1024), F32),
                        pltpu.SemaphoreType.DMA((18,)),
                        pltpu.SemaphoreType.DMA((18,)), pltpu.SemaphoreType.DMA((6,))])
    return pl.pallas_call(
        body, name="fwd_in", grid_spec=grid_spec,
        out_shape=[sd((D, T), BF16)] + [sd((T, AW), BF16)] * 3 + [sd((4, T // 4, AW), BF16)] * 3
        + [sd((16, T // 16, AW), BF16)] * 3
        + [sd((T, AW), BF16)] * 5 + [sd((D, NCOL), BF16), sd((4, 256, D), BF16)],
        compiler_params=_cp(("arbitrary", "arbitrary")),
    )(jm_arr, x, pos, mixw, w_in, w_out)


def _band_mask(key_axis, nkeys=2 * BLK):
    shape = (nkeys, 2 * BLK) if key_axis == 0 else (2 * BLK, nkeys)
    kj = lax.broadcasted_iota(jnp.int32, shape, key_axis)
    qi = lax.broadcasted_iota(jnp.int32, shape, 1 - key_axis) & (BLK - 1)
    return (kj >= qi) & (kj <= qi + BLK), kj, qi


def _stack_heads(t2, in_a):
    z = jnp.zeros_like(t2)
    return jnp.concatenate([jnp.where(in_a[0], t2, z), jnp.where(in_a[1], t2, z)], axis=0)


def _attn_fwd(q, k, v, nb, name):
    n = 8
    CH = n * BLK
    halo = nb > n

    def body(*refs):
        if halo:
            q_ref, k_ref, v_ref, kp_ref, vp_ref, o_ref, lse_ref = refs
        else:
            q_ref, k_ref, v_ref, o_ref, lse_ref = refs
        lane = lax.broadcasted_iota(jnp.int32, (1, 128), 1)
        in_a = [lane < HEAD, lane >= HEAD]
        band, kj, _ = _band_mask(1)
        thr0 = jnp.where((n * pl.program_id(0)) % nb == 0, BLK, 0) if halo else BLK
        mask0 = band & (kj >= thr0)
        mask_first = band & (kj >= BLK)
        for b in range(n):
            rs = slice(b * BLK, (b + 1) * BLK)
            stat = jnp.zeros((BLK, 128), F32)
            for hp in range(4):
                cs = slice(hp * 128, (hp + 1) * 128)
                q2s = _stack_heads(q_ref[rs, cs], in_a)
                if b == 0:
                    kprev = kp_ref[:, cs] if halo else k_ref[rs, cs]
                    vprev = vp_ref[:, cs] if halo else v_ref[rs, cs]
                    kk = jnp.concatenate([kprev, k_ref[rs, cs]], axis=0)
                    vv = jnp.concatenate([vprev, v_ref[rs, cs]], axis=0)
                    mask = mask0
                else:
                    kk = k_ref[(b - 1) * BLK:(b + 1) * BLK, cs]
                    vv = v_ref[(b - 1) * BLK:(b + 1) * BLK, cs]
                    mask = mask_first if b % nb == 0 else band
                s = jnp.where(mask, _mm_nt(q2s, kk), NEG)
                m = jnp.max(s, axis=-1, keepdims=True)
                p = jnp.exp(s - m)
                l = jnp.sum(p, axis=-1, keepdims=True)
                o = _mm(p.astype(BF16), vv) / l
                lse = m + jnp.log(l)
                o_ref[rs, cs] = jnp.where(in_a[0], o[:BLK], o[BLK:]).astype(BF16)
                stat = jnp.where(lane == 2 * hp, lse[:BLK], stat)
                stat = jnp.where(lane == 2 * hp + 1, lse[BLK:], stat)
            lse_ref[rs, :] = stat

    cur = pl.BlockSpec((CH, AW), lambda i: (i, 0))
    prev = pl.BlockSpec((BLK, AW), lambda i: (jnp.maximum(n * i - 1, 0), 0))
    return pl.pallas_call(
        body, name=name, grid=(T // CH,),
        in_specs=[cur, cur, cur] + ([prev, prev] if halo else []),
        out_specs=[cur, pl.BlockSpec((CH, 128), lambda i: (i, 0))],
        out_shape=[jax.ShapeDtypeStruct((T, AW), BF16), jax.ShapeDtypeStruct((T, 128), F32)],
        compiler_params=_cp(("parallel",)),
    )(*((q, k, v) + ((k, v) if halo else ())))


def _attn_bwd(q, k, v, do, st, nb, name):
    n = 8
    CH = n * BLK
    NBLK = T // BLK
    halo = nb > n

    def body(*refs):
        if halo:
            (q_ref, k_ref, v_ref, do_ref, st_ref, kp_ref, vp_ref, qn_ref, don_ref, stn_ref,
             dq_ref, dk_ref, dv_ref) = refs
        else:
            q_ref, k_ref, v_ref, do_ref, st_ref, dq_ref, dk_ref, dv_ref = refs
        i = pl.program_id(0)
        lane = lax.broadcasted_iota(jnp.int32, (1, 128), 1)
        in_a = [lane < HEAD, lane >= HEAD]
        band, kj, _ = _band_mask(0)
        thr0 = jnp.where((n * i) % nb == 0, BLK, 0) if halo else BLK
        mask0 = band & (kj >= thr0)
        mask_first = band & (kj >= BLK)

        def stat_rows(st_t, hp):
            lse_r = jnp.concatenate([st_t[2 * hp:2 * hp + 1, :], st_t[2 * hp + 1:2 * hp + 2, :]], axis=1)
            dl_r = jnp.concatenate([st_t[8 + 2 * hp:9 + 2 * hp, :], st_t[9 + 2 * hp:10 + 2 * hp, :]], axis=1)
            return lse_r, dl_r

        st_t = [st_ref[b * BLK:(b + 1) * BLK, :].T for b in range(n)]
        if halo:
            nxt_thr = jnp.where((n * i + n) % nb == 0, 2 * BLK, 0)
            _, kj1, qi1 = _band_mask(0, BLK)
            mask_next = kj1 >= qi1 + nxt_thr
            stn_t = stn_ref[...].T

        for hp in range(4):
            cs = slice(hp * 128, (hp + 1) * 128)
            kb = [k_ref[b * BLK:(b + 1) * BLK, cs] for b in range(n)]
            vb = [v_ref[b * BLK:(b + 1) * BLK, cs] for b in range(n)]
            dk_acc = [jnp.zeros((BLK, 128), F32) for _ in range(n)]
            dv_acc = [jnp.zeros((BLK, 128), F32) for _ in range(n)]
            for b in range(n):
                rs = slice(b * BLK, (b + 1) * BLK)
                q2s = _stack_heads(q_ref[rs, cs], in_a)
                do2s = _stack_heads(do_ref[rs, cs], in_a)
                if b == 0:
                    kprev = kp_ref[:, cs] if halo else kb[0]
                    vprev = vp_ref[:, cs] if halo else vb[0]
                    mask = mask0
                else:
                    kprev, vprev, mask = kb[b - 1], vb[b - 1], (mask_first if b % nb == 0 else band)
                kk = jnp.concatenate([kprev, kb[b]], axis=0)
                vv = jnp.concatenate([vprev, vb[b]], axis=0)
                lse_r, dl_r = stat_rows(st_t[b], hp)
                s_t = jnp.where(mask, _mm_nt(kk, q2s), NEG)
                p_t = jnp.exp(s_t - lse_r)
                ds_t = (p_t * (_mm_nt(vv, do2s) - dl_r)).astype(BF16)
                dkk = _mm(ds_t, q2s)
                dvv = _mm(p_t.astype(BF16), do2s)
                dqs = _mm_tn(ds_t, kk) * SCALE
                dq_ref[rs, cs] = jnp.where(in_a[0], dqs[:BLK], dqs[BLK:]).astype(BF16)
                dk_acc[b] += dkk[BLK:]
                dv_acc[b] += dvv[BLK:]
                if b > 0:
                    dk_acc[b - 1] += dkk[:BLK]
                    dv_acc[b - 1] += dvv[:BLK]
            if halo:
                q2s = _stack_heads(qn_ref[:, cs], in_a)
                do2s = _stack_heads(don_ref[:, cs], in_a)
                lse_r, dl_r = stat_rows(stn_t, hp)
                s_t = jnp.where(mask_next, _mm_nt(kb[n - 1], q2s), NEG)
                p_t = jnp.exp(s_t - lse_r)
                ds_t = (p_t * (_mm_nt(vb[n - 1], do2s) - dl_r)).astype(BF16)
                dk_acc[n - 1] += _mm(ds_t, q2s)
                dv_acc[n - 1] += _mm(p_t.astype(BF16), do2s)
            for b in range(n):
                dk_ref[b * BLK:(b + 1) * BLK, cs] = dk_acc[b].astype(BF16)
                dv_ref[b * BLK:(b + 1) * BLK, cs] = dv_acc[b].astype(BF16)

    cur = pl.BlockSpec((CH, AW), lambda i: (i, 0))
    cur_st = pl.BlockSpec((CH, 128), lambda i: (i, 0))
    prev = pl.BlockSpec((BLK, AW), lambda i: (jnp.maximum(n * i - 1, 0), 0))
    nxt = pl.BlockSpec((BLK, AW), lambda i: (jnp.minimum(n * i + n, NBLK - 1), 0))
    nxt_st = pl.BlockSpec((BLK, 128), lambda i: (jnp.minimum(n * i + n, NBLK - 1), 0))
    ins = [cur] * 4 + [cur_st] + ([prev, prev, nxt, nxt, nxt_st] if halo else [])
    args = (q, k, v, do, st) + ((k, v, q, do, st) if halo else ())
    return pl.pallas_call(
        body, name=name, grid=(T // CH,),
        in_specs=ins,
        out_specs=[cur] * 3,
        out_shape=[jax.ShapeDtypeStruct((T, AW), BF16)] * 3,
        compiler_params=_cp(("parallel",)),
    )(*args)


TH = 256
NCH = TH // CHUNK


def _hgrn_common(hq_ref, hf_ref, lbr_ref, tri_ref):
    r0 = lbr_ref[0:1, :]
    r1 = lbr_ref[1:2, :]
    mx = jnp.maximum(r0, r1)
    e0 = jnp.exp(r0 - mx)
    e1 = jnp.exp(r1 - mx)
    lb = e0 / (e0 + e1)
    hqv = hq_ref[...].astype(F32)
    sq = _sigmoid(hqv)
    qv = hqv * sq
    sf = _sigmoid(hf_ref[...].astype(F32))
    f = lb + (1.0 - lb) * sf
    kv = 1.0 - f
    g = jnp.log(f)
    cum = _mm_exact_l(tri_ref[...], g)
    dec = jnp.exp(jnp.concatenate([cum[c * CHUNK + CHUNK - 1:(c + 1) * CHUNK, :] for c in range(NCH)], axis=0))
    decb = jnp.concatenate([jnp.broadcast_to(dec[c:c + 1, :], (CHUNK, HW)) for c in range(NCH)], axis=0)
    ea = jnp.exp(cum)
    ena = jnp.exp(-cum)
    eend = decb * ena
    return dict(lb=lb, hq=hqv, sq=sq, q=qv, sf=sf, f=f, k=kv, cum=cum, ea=ea, ena=ena, eend=eend,
                qd=qv * ea, ki=kv * ena, ke=kv * eend, dec=dec)


def _tri_mask(transposed=False):
    ti = lax.broadcasted_iota(jnp.int32, (TH, TH), 1 if transposed else 0)
    si = lax.broadcasted_iota(jnp.int32, (TH, TH), 0 if transposed else 1)
    return (si <= ti) & ((si // CHUNK) == (ti // CHUNK))


def _hgrn_fwd(hq, hf, hi, lbr, tri):
    NSUB = 2

    def body(hq_ref, hf_ref, hi_ref, lbr_ref, tri_ref, rec_ref, sall_ref, st_scr):
        @pl.when(pl.program_id(0) == 0)
        def _():
            st_scr[...] = jnp.zeros_like(st_scr)

        causal = _tri_mask()
        for u in range(NSUB):
            tile = slice(u * TH, (u + 1) * TH)
            w = _hgrn_common(hq_ref.at[tile, :], hf_ref.at[tile, :], lbr_ref, tri_ref)
            qd, ki, ke = w["qd"].astype(BF16), w["ki"].astype(BF16), w["ke"].astype(BF16)
            dec = w["dec"]
            vb = hi_ref[tile, :]
            for h in range(4):
                cs = slice(h * 128, (h + 1) * 128)
                att = jnp.where(causal, _mm_nt(qd[:, cs], ki[:, cs]), 0.0)
                o_intra = _mm(att.astype(BF16), vb[:, cs])
                st = st_scr[:, cs]
                for c in range(NCH):
                    rs = slice(c * CHUNK, (c + 1) * CHUNK)
                    sall_ref[u * NCH + c, :, cs] = st
                    rec_ref[u * TH + c * CHUNK:u * TH + (c + 1) * CHUNK, cs] = (
                        o_intra[rs] + _mm_nt(qd[rs, cs], st.astype(BF16))).astype(BF16)
                    st = dec[c:c + 1, cs] * st + _mm_tn(vb[rs, cs], ke[rs, cs])
                st_scr[:, cs] = st

    tok = pl.BlockSpec((NSUB * TH, HW), lambda i: (i, 0))
    return pl.pallas_call(
        body, name="hgrn_fwd", grid=(T // (NSUB * TH),),
        in_specs=[tok, tok, tok, pl.BlockSpec((2, HW), lambda i: (0, 0)), pl.BlockSpec((TH, TH), lambda i: (0, 0))],
        out_specs=[tok, pl.BlockSpec((NSUB * NCH, 128, HW), lambda i: (i, 0, 0))],
        out_shape=[jax.ShapeDtypeStruct((T, HW), BF16), jax.ShapeDtypeStruct((T // CHUNK, 128, HW), F32)],
        scratch_shapes=[pltpu.VMEM((128, HW), F32)],
        compiler_params=_cp(("arbitrary",)),
    )(hq, hf, hi, lbr, tri)


def _hgrn_bwd(hq, hf, hi, lbr, tri, trit, drec, sall, dhg, gw):
    NSUB = 2
    NT = T // (NSUB * TH)

    def body(hq_ref, hf_ref, hi_ref, lbr_ref, tri_ref, trit_ref, do_ref, sall_ref, dhg_ref, gw_r,
             dph_ref, small_ref, pout_o, poutr_o,
             dst_scr, dlb_scr, dqd_scr, dki_scr, dke_scr, dlast_scr, gfull, rbuf, red, redb,
             send_sems, recv_sems, loc_sems, pair_send, pair_recv):
        step = pl.program_id(0)
        loc, rem = _chip_copies(_w_out_piece, red, redb, pout_o, poutr_o, send_sems, recv_sems, loc_sems.at[0])
        mx, my, c = lax.axis_index("x"), lax.axis_index("y"), lax.axis_index("c")
        load = pltpu.make_async_copy(gw_r, gfull, loc_sems.at[1])
        halves = [pltpu.make_async_remote_copy(
            src_ref=gfull.at[pl.ds(pl.multiple_of(j * 256 + (1 - c) * 128, 128), 128), :], dst_ref=rbuf.at[j],
            send_sem=pair_send.at[j], recv_sem=pair_recv.at[j], device_id=(mx, my, 1 - c), device_id_type=MESH)
            for j in range(4)]

        @pl.when(step == 0)
        def _():
            dst_scr[...] = jnp.zeros_like(dst_scr)
            dlb_scr[...] = jnp.zeros_like(dlb_scr)
            load.start()

        @pl.when(step == 1)
        def _():
            load.wait()
            for cp in halves:
                cp.start()

        @pl.when(step == 2)
        def _():
            for j, cp in enumerate(halves):
                cp.wait_recv()
                part = gfull[pl.ds(pl.multiple_of(j * 256 + c * 128, 128), 128), :] + rbuf[j]
                red[j * 128:(j + 1) * 128, :] = part
                redb[j * 128:(j + 1) * 128, :] = part.astype(BF16)
            for cp in halves:
                cp.wait_send()
            for cp in loc + rem:
                cp.start()

        causal = _tri_mask()
        causal_t = _tri_mask(transposed=True)
        lb = None
        for u in reversed(range(NSUB)):
            tile = slice(u * TH, (u + 1) * TH)
            w = _hgrn_common(hq_ref.at[tile, :], hf_ref.at[tile, :], lbr_ref, tri_ref)
            qd, ki, ke = w["qd"].astype(BF16), w["ki"].astype(BF16), w["ke"].astype(BF16)
            dec = w["dec"]
            vb = hi_ref[tile, :]
            dob = do_ref[tile, :].astype(BF16)
            for h in range(4):
                cs = slice(h * 128, (h + 1) * 128)
                att_t = jnp.where(causal_t, _mm_nt(ki[:, cs], qd[:, cs]), 0.0).astype(BF16)
                datt_t = jnp.where(causal_t, _mm_nt(vb[:, cs], dob[:, cs]), 0.0).astype(BF16)
                datt = jnp.where(causal, _mm_nt(dob[:, cs], vb[:, cs]), 0.0).astype(BF16)
                dv_intra = _mm(att_t, dob[:, cs])
                dqd_intra = _mm(datt, ki[:, cs])
                dki_scr[u, :, cs] = _mm(datt_t, qd[:, cs])
                dst = dst_scr[:, cs]
                for c in reversed(range(NCH)):
                    rs = slice(c * CHUNK, (c + 1) * CHUNK)
                    dec_c = dec[c:c + 1, :]
                    st = sall_ref[u * NCH + c, :, cs]
                    dstb = dst.astype(BF16)
                    dph_ref[u * TH + c * CHUNK:u * TH + (c + 1) * CHUNK, 2 * HW + h * 128:2 * HW + (h + 1) * 128] = (
                        dv_intra[rs] + _mm_nt(ke[rs, cs], dstb)).astype(BF16)
                    dqd_scr[u, rs, cs] = dqd_intra[rs] + _mm(dob[rs, cs], st.astype(BF16))
                    dke_scr[u, rs, cs] = _mm(vb[rs, cs], dstb)
                    ddec = jnp.sum(dst * st, axis=0, keepdims=True)
                    dlast_scr[u, c:c + 1, cs] = ddec * dec_c[:, cs]
                    dst = dec_c[:, cs] * dst + _mm_tn(dob[rs, cs], qd[rs, cs])
                dst_scr[:, cs] = dst
            dqd, dki, dke = dqd_scr[u], dki_scr[u], dke_scr[u]
            dq = dqd * w["ea"]
            dk = dki * w["ena"] + dke * w["eend"]
            dcum = dqd * w["qd"] - dki * w["ki"] - dke * w["ke"]
            dkeke = dke * w["ke"]
            dlastb = jnp.concatenate(
                [jnp.broadcast_to(dlast_scr[u, c:c + 1, :]
                                  + jnp.sum(dkeke[c * CHUNK:(c + 1) * CHUNK], axis=0, keepdims=True), (CHUNK, HW))
                 for c in range(NCH)], axis=0)
            dg = _mm_exact_l(trit_ref[...], dcum) + dlastb
            df = dg / w["f"] - dk
            lb, sf, sq = w["lb"], w["sf"], w["sq"]
            dph_ref[tile, HW:2 * HW] = (df * (1.0 - lb) * sf * (1.0 - sf)).astype(BF16)
            dph_ref[tile, 0:HW] = (dq * (sq * (1.0 + w["hq"] * (1.0 - sq)))).astype(BF16)
            dph_ref[tile, 3 * HW:4 * HW] = dhg_ref[tile, :]
            dlb_scr[...] += jnp.sum(df * (1.0 - sf), axis=0, keepdims=True)

        @pl.when(step == NT - 1)
        def _():
            gr = dlb_scr[...] * lb * (1.0 - lb)
            small_ref[...] = jnp.zeros_like(small_ref)
            small_ref[0:1, 0:HW] = gr
            small_ref[1:2, 0:HW] = -gr
            for cp in rem:
                cp.wait_recv()
            for cp in rem:
                cp.wait_send()
            for cp in loc:
                cp.wait()

    tok = pl.BlockSpec((NSUB * TH, HW), lambda i: (NT - 1 - i, 0))
    const = lambda shape: pl.BlockSpec(shape, lambda i: (0,) * len(shape))
    hbm = pl.BlockSpec(memory_space=pltpu.HBM)
    return pl.pallas_call(
        body, name="hgrn_bwd", grid=(NT,),
        in_specs=[tok, tok, tok, const((2, HW)), const((TH, TH)), const((TH, TH)), tok,
                  pl.BlockSpec((NSUB * NCH, 128, HW), lambda i: (NT - 1 - i, 0, 0)), tok, hbm],
        out_specs=[pl.BlockSpec((NSUB * TH, NCOL // 2), lambda i: (NT - 1 - i, 0)), const((8, D)), hbm, hbm],
        out_shape=[jax.ShapeDtypeStruct((T, NCOL // 2), BF16), jax.ShapeDtypeStruct((8, D), F32),
                   jax.ShapeDtypeStruct((128, D), F32), jax.ShapeDtypeStruct((3, 128, D), BF16)],
        scratch_shapes=[pltpu.VMEM((128, HW), F32), pltpu.VMEM((1, HW), F32), pltpu.VMEM((NSUB, TH, HW), F32),
                        pltpu.VMEM((NSUB, TH, HW), F32), pltpu.VMEM((NSUB, TH, HW), F32),
                        pltpu.VMEM((NSUB, 8, HW), F32),
                        pltpu.VMEM((D, D), F32), pltpu.VMEM((4, 128, D), F32), pltpu.VMEM((512, D), F32),
                        pltpu.VMEM((512, D), BF16),
                        pltpu.SemaphoreType.DMA((3,)), pltpu.SemaphoreType.DMA((3,)), pltpu.SemaphoreType.DMA((2,)),
                        pltpu.SemaphoreType.DMA((4,)), pltpu.SemaphoreType.DMA((4,))],
        compiler_params=_cp(("arbitrary",)),
    )(hq, hf, hi, lbr, tri, trit, drec, sall, dhg, gw)


def _fwd_out(o1, o4, o16, l1, l4, l16, rec, ag, hg, x, tgt, anw, hnw, fnw, wout_full, gmat, emat, selmat):
    TT = 512

    def body(o1_r, o4_r, o16_r, l1_r, l4_r, l16_r, rec_r, ag_r, hg_r, x_r, tgt_r, anw_r, hnw_r, fnw_r, wo_r, g_r,
             e_r, sel_r, dx2_o, do1_o, do4_o, do16_o, st1_o, st4_o, st16_o, drec_o, dag_o, dhg_o,
             gw_o, small_o, scr_a, scr_b, scr_c, gwout_o, out_sem):
        @pl.when(pl.program_id(0) == 0)
        def _():
            gwout_o[...] = jnp.zeros_like(gwout_o)
            small_o[...] = jnp.zeros_like(small_o)

        def unperm(r4, r16):
            return _unperm_load(r4, r16, scr_a, scr_b, scr_c)

        def perm_out(val, p1, p4, p16, dt):
            _perm_store(val, scr_a, scr_b, p1, p4, p16, dt)

        o4u, o16u = unperm(o4_r, o16_r)
        l4c, l16c = unperm(l4_r, l16_r)
        l1c = l1_r[...]
        mxc = jnp.maximum(jnp.maximum(l1c, l4c), l16c)
        w1c, w4c, w16c = jnp.exp(l1c - mxc), jnp.exp(l4c - mxc), jnp.exp(l16c - mxc)
        denc = w1c + w4c + w16c
        lane = lax.broadcasted_iota(jnp.int32, (1, 128), 1)
        lse_c = jnp.where(lane < 8, mxc + jnp.log(denc), 0.0)
        em = e_r[...]
        wn1 = _mm_exact_r(w1c / denc, em)
        wn4 = _mm_exact_r(w4c / denc, em)
        o1v = o1_r[...].astype(F32)
        attn = wn1 * o1v + wn4 * o4u + (1.0 - wn1 - wn4) * o16u
        gm = g_r[...]

        def head_mean_a(t):
            return jnp.concatenate([_mm_exact_r(t[:, :256], gm), _mm_exact_r(t[:, 256:], gm)], axis=1)

        def head_mean_h(t):
            return jnp.concatenate(
                [jnp.broadcast_to(jnp.mean(t[:, h * 128:(h + 1) * 128], axis=-1, keepdims=True), (TT, 128))
                 for h in range(4)], axis=1)

        rs_a = lax.rsqrt(head_mean_a(attn * attn) + EPS)
        n_a = attn * rs_a
        agv = ag_r[...].astype(F32)
        sg_a = _sigmoid(agv)
        si_a = agv * sg_a
        anw_v = anw_r[...]
        y_a = (n_a * anw_v) * si_a
        recv = rec_r[...].astype(F32)
        rs_h = lax.rsqrt(head_mean_h(recv * recv) + EPS)
        n_h = recv * rs_h
        hgv = hg_r[...].astype(F32)
        sg_h = _sigmoid(hgv)
        si_h = hgv * sg_h
        hnw_v = hnw_r[...]
        y_h = (n_h * hnw_v) * si_h
        mixed = jnp.concatenate([y_a, y_h], axis=1).astype(BF16)
        xv = x_r[...]
        x2 = xv + _mm(mixed, wo_r[...])
        r2 = lax.rsqrt(jnp.mean(x2 * x2, axis=-1, keepdims=True) + EPS)
        fnw_v = fnw_r[...]
        xn = x2 * r2
        err = xn * fnw_v - tgt_r[...]
        small_o[2:3, :] += 0.5 * jnp.sum(jnp.mean(err * err, axis=-1, keepdims=True), axis=0, keepdims=True)
        small_o[0:1, :] += jnp.sum(err * xn, axis=0, keepdims=True) * (1.0 / D)
        dyw = err * (fnw_v * (1.0 / D))
        dx2 = r2 * dyw - x2 * ((r2 * r2 * r2) * jnp.mean(dyw * x2, axis=-1, keepdims=True))
        dx2_o[...] = dx2
        dx2b = dx2.astype(BF16)
        gwout_o[...] += _mm_tn(mixed, dx2b)
        dmix = _mm_nt(dx2b, wo_r[...])
        dm_a, dm_h = dmix[:, :AW], dmix[:, AW:]
        dag_o[...] = (dm_a * (n_a * anw_v) * (sg_a * (1.0 + agv * (1.0 - sg_a)))).astype(BF16)
        dy_a = dm_a * si_a
        dn_a = dy_a * anw_v
        small_o[1:2, 0:AW] += jnp.sum(dy_a * n_a, axis=0, keepdims=True)
        dattn = rs_a * (dn_a - n_a * head_mean_a(dn_a * n_a))
        perm_out(dattn, do1_o, do4_o, do16_o, BF16)
        stats = lse_c + _mm_exact_r(dattn * attn, sel_r[...])
        perm_out(stats, st1_o, st4_o, st16_o, F32)
        dhg_o[...] = (dm_h * (n_h * hnw_v) * (sg_h * (1.0 + hgv * (1.0 - sg_h)))).astype(BF16)
        dy_h = dm_h * si_h
        dn_h = dy_h * hnw_v
        small_o[1:2, AW:] += jnp.sum(dy_h * n_h, axis=0, keepdims=True)
        drec_o[...] = (rs_h * (dn_h - n_h * head_mean_h(dn_h * n_h))).astype(BF16)

        @pl.when(pl.program_id(0) == T // TT - 1)
        def _():
            out = pltpu.make_async_copy(gwout_o, gw_o, out_sem.at[0])
            out.start()
            out.wait()

    tok = lambda w: pl.BlockSpec((TT, w), lambda i: (i, 0))
    d4 = pl.BlockSpec((4, TT // 4, AW), lambda i: (0, i, 0))
    d16 = pl.BlockSpec((16, TT // 16, AW), lambda i: (0, i, 0))
    const = lambda shape: pl.BlockSpec(shape, lambda i: (0,) * len(shape))
    sd = lambda shape, dt: jax.ShapeDtypeStruct(shape, dt)
    c4 = pl.BlockSpec((4, TT // 4, 128), lambda i: (0, i, 0))
    c16 = pl.BlockSpec((16, TT // 16, 128), lambda i: (0, i, 0))
    p3 = lambda w, dt: [sd((T, w), dt), sd((4, T // 4, w), dt), sd((16, T // 16, w), dt)]
    return pl.pallas_call(
        body, name="fwd_out", grid=(T // TT,),
        in_specs=[tok(AW), d4, d16, tok(128), c4, c16, tok(AW), tok(AW), tok(AW), tok(D), tok(D),
                  const((1, AW)), const((1, HW)), const((1, D)), const((D, D)), const((256, 256)),
                  const((128, AW)), const((AW, 128))],
        out_specs=[tok(D)] + [tok(AW), d4, d16] + [tok(128), c4, c16] + [tok(AW)] * 3
        + [pl.BlockSpec(memory_space=pltpu.HBM), const((8, D))],
        out_shape=[sd((T, D), F32)] + p3(AW, BF16) + p3(128, F32)
        + [sd((T, AW), BF16), sd((T, AW), BF16), sd((T, AW), BF16), sd((D, D), F32), sd((8, D), F32)],
        scratch_shapes=[pltpu.VMEM((4, TT, 128), F32)] * 3 + [pltpu.VMEM((D, D), F32),
                        pltpu.SemaphoreType.DMA((1,))],
        compiler_params=_cp(("arbitrary",)),
    )(o1, o4, o16, l1, l4, l16, rec, ag, hg, x, tgt, anw, hnw, fnw, wout_full, gmat, emat, selmat)


def _dproj_build(dq, dk, dv, dag, pos):
    TT = 512

    def body(dq1, dq4, dq16, dk1, dk4, dk16, dv1, dv4, dv16, dag_r, pos_r, dproj_o, scr_b, scr_c):
        def unperm_sum(r1, r4, r16):
            return r1[...] + _unperm_sum(r4, r16, scr_b, scr_c)

        cosf, s1, s2 = _rope_tables(pos_r[...])
        dproj_o[:, 0:512] = _rope_bwd(unperm_sum(dq1, dq4, dq16), cosf, s1, s2).astype(BF16)
        dproj_o[:, 512:1024] = _rope_bwd(unperm_sum(dk1, dk4, dk16), cosf, s1, s2).astype(BF16)
        dproj_o[:, 1024:1536] = unperm_sum(dv1, dv4, dv16).astype(BF16)
        dproj_o[:, 1536:2048] = dag_r[...]

    tok = lambda w: pl.BlockSpec((TT, w), lambda i: (i, 0))
    d4 = pl.BlockSpec((4, TT // 4, AW), lambda i: (0, i, 0))
    d16 = pl.BlockSpec((16, TT // 16, AW), lambda i: (0, i, 0))
    return pl.pallas_call(
        body, name="dproj_build", grid=(T // TT,),
        in_specs=[tok(AW), d4, d16] * 3 + [tok(AW), pl.BlockSpec((1, TT), lambda i: (0, i))],
        out_specs=tok(NCOL // 2),
        out_shape=jax.ShapeDtypeStruct((T, NCOL // 2), BF16),
        scratch_shapes=[pltpu.VMEM((4, TT, 128), F32)] * 2,
        compiler_params=_cp(("parallel",)),
    )(*dq, *dk, *dv, dag, pos)


def _bwd_x(dproj_a, dproj_h, x, dx2, mixw, w_full, rinb, small4, small6, pout_own, pout_rem, raw):
    TT = 256
    NT = T // TT

    def body(dpa_r, dph_r, x_r, dx2_r, mw_r, w_r, rinb_r, s4_r, s6_r, poo_r, por_r, raw_r,
             gx_o, sall_o, fin_o, fout_o, sbuf, v_own, v_rem, vo_own, vo_rem, sin, sout, got_in,
             got_out, v_send, v_got, send_sems, recv_sems, loc_sems, share_send, share_recv, fin_sems, raw_sems,
             hand_send, hand_recv):
        i = pl.program_id(0)
        mx, my, c = lax.axis_index("x"), lax.axis_index("y"), lax.axis_index("c")
        slot_of = lambda ref, b: ref.at[lax.rem(b - (2 * mx + my) + 3, 4)]
        _, rem = _chip_copies(slot_of, rinb_r, rinb_r, v_own, v_rem, send_sems, recv_sems, loc_sems.at[0])
        loc = [pltpu.make_async_copy(raw_r.at[pl.ds(pl.multiple_of(c * 512, 512), 512), :], v_own, loc_sems.at[0])]
        load_theirs = pltpu.make_async_copy(raw_r.at[pl.ds(pl.multiple_of((1 - c) * 512, 512), 512), :], v_send,
                                            raw_sems.at[0])
        hand = pltpu.make_async_remote_copy(src_ref=v_send, dst_ref=v_got, send_sem=hand_send.at[0],
                                            recv_sem=hand_recv.at[0], device_id=(mx, my, 1 - c), device_id_type=MESH)
        loads = [pltpu.make_async_copy(poo_r, vo_own, fin_sems.at[2]),
                 pltpu.make_async_copy(por_r, vo_rem, fin_sems.at[3])]

        @pl.when(i == 0)
        def _():
            sbuf[...] = jnp.zeros_like(sbuf)
            for cp in loc + rem + loads:
                cp.start()
            load_theirs.start()

        @pl.when(i == 1)
        def _():
            load_theirs.wait()
            hand.start()

        dhn = _mm_nt(dpa_r[...], w_r[:, 0:NCOL // 2]) + _mm_nt(dph_r[...], w_r[:, NCOL // 2:NCOL])
        xv = x_r[...]
        r = lax.rsqrt(jnp.mean(xv * xv, axis=-1, keepdims=True) + EPS)
        dxw = dhn * mw_r[...]
        gx_o[...] = dx2_r[...] + r * dxw - xv * ((r * r * r) * jnp.mean(dxw * xv, axis=-1, keepdims=True))
        sbuf[16:17, :] += jnp.sum(dhn * (xv * r), axis=0, keepdims=True)

        @pl.when(i == NT - 1)
        def _():
            sbuf[0:8, :] = s4_r[...]
            sbuf[8:16, :] = s6_r[...]
            sloc, srem = _small_copies(sbuf, sall_o, send_sems, recv_sems, loc_sems.at[1])
            for cp in sloc + srem:
                cp.start()
            for cp in rem:
                cp.wait_recv()
            for cp in rem:
                cp.wait_send()
            for cp in loc:
                cp.wait()
            for cp in loads:
                cp.wait()
            hand.wait_recv()
            hand.wait_send()
            sout[...] = ((vo_own[...] + vo_rem[0].astype(F32)) + vo_rem[1].astype(F32)) + vo_rem[2].astype(F32)
            sin[...] = (((v_own[...] + v_got[...]) + v_rem[0].astype(F32)) + v_rem[1].astype(F32)) + v_rem[2].astype(F32)
            swap = [pltpu.make_async_remote_copy(src_ref=sin, dst_ref=got_in, send_sem=share_send.at[0],
                                                 recv_sem=share_recv.at[0], device_id=(mx, my, 1 - c),
                                                 device_id_type=MESH),
                    pltpu.make_async_remote_copy(src_ref=sout, dst_ref=got_out, send_sem=share_send.at[1],
                                                 recv_sem=share_recv.at[1], device_id=(mx, my, 1 - c),
                                                 device_id_type=MESH)]
            for cp in swap:
                cp.start()
            mine = [pltpu.make_async_copy(sin, fin_o.at[c], fin_sems.at[0]),
                    pltpu.make_async_copy(sout, fout_o.at[c], fin_sems.at[1])]
            for cp in mine:
                cp.start()
            for cp in swap:
                cp.wait_recv()
            theirs = [pltpu.make_async_copy(got_in, fin_o.at[1 - c], fin_sems.at[2]),
                      pltpu.make_async_copy(got_out, fout_o.at[1 - c], fin_sems.at[3])]
            for cp in theirs:
                cp.start()
            for cp in swap:
                cp.wait_send()
            for cp in mine + theirs:
                cp.wait()
            for cp in srem:
                cp.wait_recv()
            for cp in srem:
                cp.wait_send()
            for cp in sloc:
                cp.wait()

    tok = lambda w: pl.BlockSpec((TT, w), lambda i: (i, 0))
    const = lambda shape: pl.BlockSpec(shape, lambda i: (0,) * len(shape))
    hbm = pl.BlockSpec(memory_space=pltpu.HBM)
    return pl.pallas_call(
        body, name="bwd_x", grid=(NT,),
        in_specs=[tok(NCOL // 2), tok(NCOL // 2), tok(D), tok(D), const((1, D)), const((D, NCOL)), hbm,
                  const((8, D)), const((8, D)), hbm, hbm, hbm],
        out_specs=[tok(D), hbm, hbm, hbm],
        out_shape=[jax.ShapeDtypeStruct((T, D), F32),
                   jax.ShapeDtypeStruct((8, 24, D), F32),
                   jax.ShapeDtypeStruct((2, 512, 1024), F32), jax.ShapeDtypeStruct((2, 128, D), F32)],
        scratch_shapes=[pltpu.VMEM((24, D), F32),
                        pltpu.VMEM((512, 1024), F32), pltpu.VMEM((3, 512, 1024), BF16),
                        pltpu.VMEM((128, D), F32), pltpu.VMEM((3, 128, D), BF16),
                        pltpu.VMEM((512, 1024), F32), pltpu.VMEM((128, D), F32),
                        pltpu.VMEM((512, 1024), F32), pltpu.VMEM((128, D), F32),
                        pltpu.VMEM((512, 1024), F32), pltpu.VMEM((512, 1024), F32),
                        pltpu.SemaphoreType.DMA((10,)), pltpu.SemaphoreType.DMA((10,)), pltpu.SemaphoreType.DMA((2,)),
                        pltpu.SemaphoreType.DMA((2,)), pltpu.SemaphoreType.DMA((2,)), pltpu.SemaphoreType.DMA((4,)),
                        pltpu.SemaphoreType.DMA((1,)), pltpu.SemaphoreType.DMA((1,)), pltpu.SemaphoreType.DMA((1,))],
        compiler_params=_cp(("arbitrary",)),
    )(dproj_a, dproj_h, x, dx2, mixw, w_full, rinb, small4, small6, pout_own, pout_rem, raw)


def _grad_w_in(hn, dproj_a, dproj_h, jm_arr):
    TK = 2048
    NK = T // TK

    def block_at(j, jm):
        return lax.rem(jm + 1 + j, 4)

    def body(jm_ref, hnt_r, dpa_r, dph_r, rinb_o, raw_o, acc, rbuf, obufb, send_sems, recv_sems, wb_sems):
        j = pl.program_id(0)
        kk = pl.program_id(1)
        x, y, c = lax.axis_index("x"), lax.axis_index("y"), lax.axis_index("c")
        mine = pl.ds(pl.multiple_of(c * 512, 512), 512)
        theirs = pl.ds(pl.multiple_of((1 - c) * 512, 512), 512)

        def send(jj):
            return pltpu.make_async_remote_copy(
                src_ref=acc.at[jj % 2, theirs, :], dst_ref=rbuf.at[jj], send_sem=send_sems.at[jj],
                recv_sem=recv_sems.at[jj], device_id=(x, y, 1 - c), device_id_type=MESH)

        def writeback(jj):
            return [pltpu.make_async_copy(obufb.at[jj % 2], rinb_o.at[jj], wb_sems.at[2 + jj % 2])]

        def wait_writeback(jj):
            for cp in writeback(jj):
                cp.wait()

        def finalize(jj):
            send(jj).wait_recv()
            obufb[jj % 2] = (acc[jj % 2, mine, :] + rbuf[jj]).astype(BF16)
            for cp in writeback(jj):
                cp.start()

        blk = block_at(j, jm_ref[0])
        prod = _mm(hnt_r[...], jnp.where(blk < 2, dpa_r[...], dph_r[...]))

        @pl.when(kk == 0)
        def _():
            for jj in (2, 3):
                @pl.when(j == jj)
                def _():
                    send(jj - 2).wait_send()
            acc[j % 2] = prod

        @pl.when(kk > 0)
        def _():
            acc[j % 2] += prod

        @pl.when(kk == NK - 1)
        def _():
            for jj in range(4):
                @pl.when(j == jj)
                def _():
                    if jj < 3:
                        send(jj).start()
                    if jj in (1, 2):
                        finalize(jj - 1)
                    if jj == 3:
                        raw = pltpu.make_async_copy(acc.at[1], raw_o, wb_sems.at[4])
                        raw.start()
                        wait_writeback(0)
                        finalize(2)
                        wait_writeback(1)
                        wait_writeback(2)
                        raw.wait()
                        send(2).wait_send()

    def used(is_mine, kk, col):
        return jnp.where(is_mine, kk, 0), jnp.where(is_mine, col, 0)

    hbm = pl.BlockSpec(memory_space=pltpu.HBM)
    grid_spec = pltpu.PrefetchScalarGridSpec(
        num_scalar_prefetch=1, grid=(4, NK),
        in_specs=[pl.BlockSpec((D, TK), lambda j, kk, jm_ref: (0, kk)),
                  pl.BlockSpec((TK, 1024), lambda j, kk, jm_ref: used(
                      block_at(j, jm_ref[0]) < 2, kk, block_at(j, jm_ref[0]))),
                  pl.BlockSpec((TK, 1024), lambda j, kk, jm_ref: used(
                      block_at(j, jm_ref[0]) >= 2, kk, block_at(j, jm_ref[0]) - 2))],
        out_specs=[hbm, hbm],
        scratch_shapes=[pltpu.VMEM((2, D, 1024), F32), pltpu.VMEM((3, 512, 1024), F32),
                        pltpu.VMEM((2, 512, 1024), BF16),
                        pltpu.SemaphoreType.DMA((3,)), pltpu.SemaphoreType.DMA((3,)), pltpu.SemaphoreType.DMA((5,))])
    return pl.pallas_call(
        body, name="grad_w_in", grid_spec=grid_spec,
        out_shape=[jax.ShapeDtypeStruct((3, 512, 1024), BF16), jax.ShapeDtypeStruct((D, 1024), F32)],
        compiler_params=_cp(("arbitrary", "arbitrary")),
    )(jm_arr, hn, dproj_a, dproj_h)


def _w_out_piece(ref, j):
    return ref.at[pl.ds(j * 128, 128), :]


def _chip_copies(piece, src_r, srcb_r, own_o, rem_o, send_sems, recv_sems, loc_sem):
    x, y, c = lax.axis_index("x"), lax.axis_index("y"), lax.axis_index("c")
    chips = [(1 - x, y), (x, 1 - y), (1 - x, 1 - y)]
    loc = [pltpu.make_async_copy(piece(src_r, 2 * x + y), own_o, loc_sem)]
    rem = [pltpu.make_async_remote_copy(
        src_ref=piece(srcb_r, 2 * px + py), dst_ref=rem_o.at[k], send_sem=send_sems.at[k],
        recv_sem=recv_sems.at[k], device_id=(px, py, c), device_id_type=MESH) for k, (px, py) in enumerate(chips)]
    return loc, rem


def _small_copies(small_r, sall_o, send_sems, recv_sems, loc_sem):
    x, y, c = lax.axis_index("x"), lax.axis_index("y"), lax.axis_index("c")
    me = 4 * x + 2 * y + c
    loc = [pltpu.make_async_copy(small_r, sall_o.at[me], loc_sem)]
    rem = []
    k = 3
    for fx in range(2):
        for fy in range(2):
            for fc in range(2):
                if fx or fy or fc:
                    peer = (1 - x if fx else x, 1 - y if fy else y, 1 - c if fc else c)
                    rem.append(pltpu.make_async_remote_copy(
                        src_ref=small_r, dst_ref=sall_o.at[me], send_sem=send_sems.at[k],
                        recv_sem=recv_sems.at[k], device_id=peer, device_id_type=MESH))
                    k += 1
    return loc, rem


def _adamw_math(w, g, m, v):
    m = B1 * m + (1.0 - B1) * g
    v = B2 * v + (1.0 - B2) * (g * g)
    m_hat = m / (1.0 - B1 ** STEP)
    v_hat = v / (1.0 - B2 ** STEP)
    delta = -LR * (m_hat / (jnp.sqrt(v_hat) + AEPS) + WD * w)
    return delta, m, v


def _adamw(big_in, big_out, sall, params):
    CH = 128
    NB = 3
    chunks = [(0, r) for r in range(D // CH)] + [(1, r) for r in range(256 // CH)]

    def body(*refs):
        big = [refs[0:4], refs[4:8]]
        sall_r = refs[8]
        ins = refs[9:24]
        big_o = [refs[24:27], refs[27:30]]
        outs = refs[30:51]
        ibuf, obuf, in_sems, out_sems = refs[51:]

        def reads(n):
            which, r = chunks[n]
            return [pltpu.make_async_copy(big[which][a].at[pl.ds(r * CH, CH), :], ibuf.at[n % NB, a],
                                          in_sems.at[4 * (n % NB) + a]) for a in range(4)]

        def writes(n):
            which, r = chunks[n]
            return [pltpu.make_async_copy(obuf.at[n % NB, a], big_o[which][a].at[pl.ds(r * CH, CH), :],
                                          out_sems.at[3 * (n % NB) + a]) for a in range(3)]

        for n in range(NB):
            for cp in reads(n):
                cp.start()

        tot = sall_r[0]
        for dv in range(1, 8):
            tot = tot + sall_r[dv]
        grads = [tot[16:17, :], tot[1:2, 0:AW], tot[1:2, AW:], tot[8:10, 0:HW], tot[0:1, :]]
        outs[0][...] = tot[2:3, 0:1]
        for p in range(5):
            w_r, m_r, v_r = ins[3 * p:3 * p + 3]
            g = grads[p]
            d, mm, vv = _adamw_math(w_r[...], g, m_r[...], v_r[...])
            outs[1 + 4 * p][...] = g
            outs[2 + 4 * p][...] = d
            outs[3 + 4 * p][...] = mm
            outs[4 + 4 * p][...] = vv

        for n in range(len(chunks)):
            s = n % NB
            for cp in reads(n):
                cp.wait()
            if n >= NB:
                for cp in writes(n - NB):
                    cp.wait()
            d, mm, vv = _adamw_math(ibuf[s, 0], ibuf[s, 1], ibuf[s, 2], ibuf[s, 3])
            obuf[s, 0] = d
            obuf[s, 1] = mm
            obuf[s, 2] = vv
            for cp in writes(n):
                cp.start()
            if n + NB < len(chunks):
                for cp in reads(n + NB):
                    cp.start()
        for n in range(len(chunks) - NB, len(chunks)):
            for cp in writes(n):
                cp.wait()

    flat = [a for p in params for a in p]
    shapes = [jax.ShapeDtypeStruct((D, 1024), F32)] * 3 + [jax.ShapeDtypeStruct((256, D), F32)] * 3
    shapes += [jax.ShapeDtypeStruct((1, 1), F32)]
    for p in params:
        shapes += [jax.ShapeDtypeStruct(p[0].shape, F32)] * 4
    vm = pl.BlockSpec(memory_space=pltpu.VMEM)
    hbm = pl.BlockSpec(memory_space=pltpu.HBM)
    return pl.pallas_call(
        body, name="adamw",
        in_specs=[hbm] * 8 + [vm] * 16, out_specs=[hbm] * 6 + [vm] * 21,
        out_shape=shapes,
        scratch_shapes=[pltpu.VMEM((NB, 4, CH, 1024), F32), pltpu.VMEM((NB, 3, CH, 1024), F32),
                        pltpu.SemaphoreType.DMA((4 * NB,)), pltpu.SemaphoreType.DMA((3 * NB,))],
        compiler_params=_cp(),
    )(*big_in, *big_out, sall, *flat)


def kernel(x, positions, w_in, w_out, mix_norm_w, attn_out_norm_w, hgrn_out_norm_w, hgrn_lb_raw, final_norm_w, loss_target, m_w_in, m_w_out, m_mix_norm_w, m_attn_out_norm_w, m_hgrn_out_norm_w, m_hgrn_lb_raw, m_final_norm_w, v_w_in, v_w_out, v_mix_norm_w, v_attn_out_norm_w, v_hgrn_out_norm_w, v_hgrn_lb_raw, v_final_norm_w):
    xs = x.reshape(T, D)
    tgt = loss_target.reshape(T, D)
    pos = positions.reshape(1, T)
    fnw = final_norm_w.reshape(1, D)

    ti = np.arange(TH)
    tri_np = ((ti[:, None] // CHUNK == ti[None, :] // CHUNK) & (ti[None, :] <= ti[:, None])).astype(np.float32)
    tri = jnp.asarray(tri_np, BF16)
    trit = jnp.asarray(tri_np.T, BF16)
    hi_ = np.arange(AW) // HEAD
    gmat = jnp.asarray((hi_[:256, None] == hi_[None, :256]).astype(np.float32) / HEAD, BF16)
    emat_np = (np.arange(128)[:, None] == hi_[None, :]).astype(np.float32)
    sel_np = (8 + hi_[:, None] == np.arange(128)[None, :]).astype(np.float32)
    emat = jnp.asarray(emat_np, BF16)
    selmat = jnp.asarray(sel_np, BF16)

    jm_arr = (2 * lax.axis_index("x") + lax.axis_index("y")).astype(jnp.int32).reshape(1)
    (hn, q1, k1, v1, q4, k4, v4, q16, k16, v16, ag, hq, hf, hi, hg, w_full, wout4) = _fwd_in(
        xs, pos, mix_norm_w, w_in.reshape(D, 1024), w_out.reshape(256, D), jm_arr)
    wout_full = wout4.reshape(D, D)
    flat = lambda a: a.reshape(T, AW)
    o1, l1 = _attn_fwd(q1, k1, v1, T // BLK, "attn_fwd_d1")
    o4, l4 = _attn_fwd(flat(q4), flat(k4), flat(v4), T // 4 // BLK, "attn_fwd_d4")
    o16, l16 = _attn_fwd(flat(q16), flat(k16), flat(v16), T // 16 // BLK, "attn_fwd_d16")
    rec, sall = _hgrn_fwd(hq, hf, hi, hgrn_lb_raw, tri)

    (dx2, do1, do4, do16, st1, st4, st16, drec, dag, dhg, gw, small4) = _fwd_out(
        o1, o4.reshape(4, T // 4, AW), o16.reshape(16, T // 16, AW),
        l1, l4.reshape(4, T // 4, 128), l16.reshape(16, T // 16, 128),
        rec, ag, hg, xs, tgt, attn_out_norm_w, hgrn_out_norm_w, fnw, wout_full, gmat, emat, selmat)

    fst = lambda a: a.reshape(T, 128)
    dq1, dk1, dv1 = _attn_bwd(q1, k1, v1, do1, st1, T // BLK, "attn_bwd_d1")
    dq4, dk4, dv4 = _attn_bwd(flat(q4), flat(k4), flat(v4), flat(do4), fst(st4), T // 4 // BLK, "attn_bwd_d4")
    dq16, dk16, dv16 = _attn_bwd(flat(q16), flat(k16), flat(v16), flat(do16), fst(st16), T // 16 // BLK,
                                 "attn_bwd_d16")
    dproj_h, small6, pout_own, pout_rem = _hgrn_bwd(hq, hf, hi, hgrn_lb_raw, tri, trit, drec, sall, dhg, gw)

    r4 = lambda a: a.reshape(4, T // 4, AW)
    r16 = lambda a: a.reshape(16, T // 16, AW)
    dproj_a = _dproj_build((dq1, r4(dq4), r16(dq16)), (dk1, r4(dk4), r16(dk16)), (dv1, r4(dv4), r16(dv16)),
                           dag, pos)
    rinb, raw = _grad_w_in(hn, dproj_a, dproj_h, jm_arr)
    gx, small_all, fin, fout = _bwd_x(dproj_a, dproj_h, xs, dx2, mix_norm_w, w_full, rinb,
                                      small4, small6, pout_own, pout_rem, raw)
    g_w_in = fin.reshape(D, 1024)
    g_w_out = fout.reshape(256, D)

    params = [(mix_norm_w, m_mix_norm_w, v_mix_norm_w),
              (attn_out_norm_w, m_attn_out_norm_w, v_attn_out_norm_w),
              (hgrn_out_norm_w, m_hgrn_out_norm_w, v_hgrn_out_norm_w),
              (hgrn_lb_raw, m_hgrn_lb_raw, v_hgrn_lb_raw),
              (fnw, m_final_norm_w.reshape(1, D), v_final_norm_w.reshape(1, D))]
    d_in, nm_in, nv_in, d_out, nm_out, nv_out, *so = _adamw(
        (w_in.reshape(D, 1024), g_w_in, m_w_in.reshape(D, 1024), v_w_in.reshape(D, 1024)),
        (w_out.reshape(256, D), g_w_out, m_w_out.reshape(256, D), v_w_out.reshape(256, D)), small_all, params)
    loss = so[0].reshape(())
    g_s = [so[1 + 4 * p] for p in range(5)]
    d_s = [so[2 + 4 * p] for p in range(5)]
    m_s = [so[3 + 4 * p] for p in range(5)]
    v_s = [so[4 + 4 * p] for p in range(5)]
    for lst in (g_s, d_s, m_s, v_s):
        lst[4] = lst[4].reshape(D)

    return (loss, gx.reshape(1, T, D),
            g_w_in.reshape(1, D, 1024), g_w_out.reshape(1, 256, D), *g_s,
            d_in.reshape(1, D, 1024), d_out.reshape(1, 256, D), *d_s,
            nm_in.reshape(1, D, 1024), nm_out.reshape(1, 256, D), *m_s,
            nv_in.reshape(1, D, 1024), nv_out.reshape(1, 256, D), *v_s)
```

```python
import functools

import numpy as np
import jax
import jax.numpy as jnp
from jax import lax
from jax.experimental import pallas as pl
from jax.experimental.pallas import tpu as pltpu

F32 = jnp.float32
BF16 = jnp.bfloat16

T = 4096
D = 1024
AW = 512
HW = 512
NCOL = 4096
HEAD = 64
BLK = 128
CHUNK = 64
EPS = 1e-6
SCALE = HEAD ** -0.5
NEG = -1e30
ROPE_THETA = 500000.0
INV_FREQ = [float(v) for v in
            (np.float32(ROPE_THETA) ** (-(np.arange(8, dtype=np.float32)) * np.float32(0.125)))]
LR, B1, B2, AEPS, WD, STEP = 0.001, 0.9, 0.999, 1e-08, 0.01, 10
VMEM_LIMIT = 63 * 1024 * 1024
MESH = pl.DeviceIdType.MESH


def _cp(sem=None, **kw):
    return pltpu.CompilerParams(dimension_semantics=sem, vmem_limit_bytes=VMEM_LIMIT, **kw)


def _mm(a, b):
    return jnp.dot(a, b, preferred_element_type=F32)


def _mm_nt(a, b):
    return lax.dot_general(a, b, (((1,), (1,)), ((), ())), preferred_element_type=F32)


def _mm_tn(a, b):
    return lax.dot_general(a, b, (((0,), (0,)), ((), ())), preferred_element_type=F32)


def _mm_exact_l(mat_bf, x):
    h = x.astype(BF16)
    l = (x - h.astype(F32)).astype(BF16)
    return _mm(mat_bf, h) + _mm(mat_bf, l)


def _mm_exact_r(x, mat_bf):
    h = x.astype(BF16)
    l = (x - h.astype(F32)).astype(BF16)
    return _mm(h, mat_bf) + _mm(l, mat_bf)


def _sigmoid(x):
    return 0.5 * jnp.tanh(0.5 * x) + 0.5


def _rope_tables(pos):
    lane = lax.broadcasted_iota(jnp.int32, (1, 128), 1)
    jl = lane & 63
    fi = jl & 7
    inv = jnp.zeros((1, 128), F32)
    for kk in range(8):
        inv = jnp.where(fi == kk, INV_FREQ[kk], inv)
    ang = jnp.broadcast_to(pos.astype(F32), (128, pos.shape[1])).T * inv
    c = jnp.cos(ang)
    s = jnp.sin(ang)
    cosf = jnp.where(jl < 16, c, 1.0)
    s1 = jnp.where(jl < 8, -s, 0.0)
    s2 = jnp.where((jl >= 8) & (jl < 16), s, 0.0)
    return cosf, s1, s2


def _rope(t, cosf, s1, s2):
    parts = []
    for ci in range(t.shape[1] // 128):
        tc = t[:, ci * 128:(ci + 1) * 128]
        parts.append(tc * cosf + pltpu.roll(tc, 120, 1) * s1 + pltpu.roll(tc, 8, 1) * s2)
    return jnp.concatenate(parts, axis=1)


def _rope_bwd(g, cosf, s1, s2):
    parts = []
    for ci in range(g.shape[1] // 128):
        gc = g[:, ci * 128:(ci + 1) * 128]
        parts.append(gc * cosf + pltpu.roll(gc * s1, 8, 1) + pltpu.roll(gc * s2, 120, 1))
    return jnp.concatenate(parts, axis=1)


def _perm_store(val, scr, scr2, o1, o4, o16, dt):
    n = val.shape[0]
    q = n // 4
    o1[...] = val.astype(dt)
    for ci in range(val.shape[1] // 128):
        cs = slice(ci * 128, (ci + 1) * 128)
        scr[ci] = val[:, cs]
        for r4 in range(4):
            part = scr[ci, pl.ds(r4, q, stride=4), :]
            o4[r4, :, cs] = part.astype(dt)
            scr2[ci, r4 * q:(r4 + 1) * q, :] = part
        for r4 in range(4):
            for b in range(4):
                o16[r4 + 4 * b, :, cs] = scr2[ci, pl.ds(r4 * q + b, q // 4, stride=4), :].astype(dt)


def _unperm_load(r4, r16, scr_a, scr_b, scr_c):
    n = scr_a.shape[1]
    q = n // 4
    nc = r4.shape[-1] // 128
    for ci in range(nc):
        cs = slice(ci * 128, (ci + 1) * 128)
        for rr in range(4):
            scr_a[ci, pl.ds(rr, q, stride=4), :] = r4[rr, :, cs].astype(F32)
        for rr in range(4):
            for b in range(4):
                scr_c[ci, pl.ds(rr * q + b, q // 4, stride=4), :] = r16[rr + 4 * b, :, cs].astype(F32)
        for rr in range(4):
            scr_b[ci, pl.ds(rr, q, stride=4), :] = scr_c[ci, rr * q:(rr + 1) * q, :]
    return (jnp.concatenate([scr_a[ci] for ci in range(nc)], axis=1),
            jnp.concatenate([scr_b[ci] for ci in range(nc)], axis=1))


def _unperm_sum(r4, r16, scr_b, scr_c):
    n = scr_b.shape[1]
    q = n // 4
    nc = r4.shape[-1] // 128
    for ci in range(nc):
        cs = slice(ci * 128, (ci + 1) * 128)
        for rr in range(4):
            for b in range(4):
                scr_c[ci, pl.ds(rr * q + b, q // 4, stride=4), :] = r16[rr + 4 * b, :, cs].astype(F32)
        for rr in range(4):
            scr_b[ci, pl.ds(rr, q, stride=4), :] = scr_c[ci, rr * q:(rr + 1) * q, :] + r4[rr, :, cs].astype(F32)
    return jnp.concatenate([scr_b[ci] for ci in range(nc)], axis=1)


def _fwd_in(x, pos, mixw, w_in, w_out, jm_arr):
    TT = 512
    NT = T // TT

    def body(jm_ref, x_ref, pos_ref, mw_ref, win_ref, wout_ref,
             hnt_ref, q1, k1, v1, q4, k4, v4, q16, k16, v16, ag, hq, hf, hi, hg, wfull_o, woutfull_o,
             wbuf, wobuf, hn_all, scr, scr2, stage, send_sems, recv_sems, loc_sems):
        s = pl.program_id(0)
        i = pl.program_id(1)
        mx, my, c = lax.axis_index("x"), lax.axis_index("y"), lax.axis_index("c")
        me, sibling = (mx, my, c), (mx, my, 1 - c)
        chips = [(mx, 1 - my), (1 - mx, my), (1 - mx, 1 - my)]
        jm = 2 * mx + my
        rows_in = [pl.ds(pl.multiple_of(h * 512, 512), 512) for h in (c, 1 - c)]
        rows_out = [pl.ds(pl.multiple_of(h * 128, 128), 128) for h in (c, 1 - c)]

        def blk(k):
            return lax.bitwise_xor(jm, k + 1)

        def rc(n, ref, to):
            return pltpu.make_async_remote_copy(src_ref=ref, dst_ref=ref, send_sem=send_sems.at[n],
                                                recv_sem=recv_sems.at[n], device_id=to, device_id_type=MESH)

        halves = [pl.ds(0, 512), pl.ds(512, 512)]
        send_in = lambda k, h: rc(12 + 2 * k + h, wbuf.at[jm, rows_in[0], halves[h]], (*chips[k], c))
        got_in = lambda k, h: rc(12 + 2 * k + h, wbuf.at[blk(k), rows_in[0], halves[h]], me)
        relay = lambda h: rc(16 + h, wbuf.at[blk(h), rows_in[0], halves[h]], (*chips[1 - h], c))
        got_relay = lambda h: rc(16 + h, wbuf.at[blk(2), rows_in[0], halves[h]], me)
        send_out = lambda k: rc(3 + k, wobuf.at[jm, rows_out[0], :], (*chips[k], c))
        got_out = lambda k: rc(3 + k, wobuf.at[blk(k), rows_out[0], :], me)
        pass_in = lambda k: rc(6 + k, wbuf.at[blk(k), rows_in[0], :], sibling)
        pass_out = lambda k: rc(9 + k, wobuf.at[blk(k), rows_out[0], :], sibling)
        passed_in = lambda k: rc(6 + k, wbuf.at[blk(k), rows_in[1], :], me)
        passed_out = lambda k: rc(9 + k, wobuf.at[blk(k), rows_out[1], :], me)

        def keep(j, n):
            return pltpu.make_async_copy(wbuf.at[j], wfull_o.at[:, pl.ds(j * 1024, 1024)], loc_sems.at[n])

        @pl.when((s == 0) & (i == 0))
        def _():
            chunk = [pl.ds(pl.multiple_of(lax.rem(p + 2 * c, 4) * 256, 256), 256) for p in range(4)]
            loads = [pltpu.make_async_copy(win_ref.at[chunk[p], :] if p < 4 else wout_ref, stage.at[p % 2],
                                           loc_sems.at[4 + p % 2]) for p in range(5)]
            loads[0].start()
            for p in range(5):
                if p < 4:
                    loads[p + 1].start()
                loads[p].wait()
                if p < 4:
                    wbuf[jm, chunk[p], :] = stage[p % 2].astype(BF16)
                else:
                    wobuf[jm] = stage[p % 2].astype(BF16)
                if p == 1:
                    for k in range(2):
                        for h in range(2):
                            send_in(k, h).start()
            keep(jm, 0).start()

        @pl.when((s == 0) & (i == NT - 1))
        def _():
            for kk in range(2):
                for h in range(2):
                    got_in(kk, h).wait_recv()
            relay(0).start()
            relay(1).start()
            pass_in(0).start()
            pass_in(1).start()
            passed_in(0).wait_recv()
            keep(blk(0), 1).start()

        @pl.when((s == 1) & (i == NT - 1))
        def _():
            got_relay(0).wait_recv()
            got_relay(1).wait_recv()
            pass_in(2).start()

        @pl.when((s == 2) & (i == 0))
        def _():
            for k in (1, 2):
                passed_in(k).wait_recv()
                keep(blk(k), k + 1).start()
            for kk in range(3):
                send_out(kk).start()

        @pl.when((s == 2) & (i == NT - 2))
        def _():
            for k in range(3):
                got_out(k).wait_recv()
                pass_out(k).start()

        whole_out = pltpu.make_async_copy(wobuf, woutfull_o, loc_sems.at[4])

        @pl.when((s == 2) & (i == NT - 1))
        def _():
            for k in range(3):
                passed_out(k).wait_recv()
            whole_out.start()

        tile = pl.ds(pl.multiple_of(i * TT, TT), TT)

        @pl.when(s == 0)
        def _():
            xv = x_ref[...]
            r = lax.rsqrt(jnp.mean(xv * xv, axis=-1, keepdims=True) + EPS)
            hnf = (xv * r) * mw_ref[...]
            hn_all[tile, :] = hnf.astype(BF16)
            hnt_ref[...] = hnf.T.astype(BF16)

        def project(jj):
            hn = hn_all[tile, :]
            lo = _mm(hn, wbuf[jj, :, 0:512])
            hi_cols = _mm(hn, wbuf[jj, :, 512:1024])
            if jj == 0:
                cosf, s1, s2 = _rope_tables(pos_ref[...])
                _perm_store(_rope(lo, cosf, s1, s2) * SCALE, scr, scr2, q1, q4, q16, BF16)
                _perm_store(_rope(hi_cols, cosf, s1, s2), scr, scr2, k1, k4, k16, BF16)
            elif jj == 1:
                _perm_store(lo, scr, scr2, v1, v4, v16, BF16)
                ag[...] = hi_cols.astype(BF16)
            elif jj == 2:
                hq[...] = lo.astype(BF16)
                hf[...] = hi_cols.astype(BF16)
            else:
                hi[...] = lo.astype(BF16)
                hg[...] = hi_cols.astype(BF16)

        def project_block(j):
            for jj in range(4):
                pl.when(j == jj)(functools.partial(project, jj))

        @pl.when(s < 2)
        def _():
            project_block(lax.bitwise_xor(jm, s))

        @pl.when(s == 2)
        def _():
            project_block(lax.bitwise_xor(jm, 2))
            project_block(lax.bitwise_xor(jm, 3))

        @pl.when((s == 2) & (i == NT - 1))
        def _():
            for h in range(2):
                relay(h).wait_send()
                for k in range(2):
                    send_in(k, h).wait_send()
            for k in range(3):
                send_out(k).wait_send()
                pass_in(k).wait_send()
                pass_out(k).wait_send()
            keep(jm, 0).wait()
            for k in range(3):
                keep(blk(k), k + 1).wait()
            whole_out.wait()

    def at_stage_of(jb):
        def index(s, i, jm_ref):
            sa = jnp.minimum(lax.bitwise_xor(jm_ref[0], jb), 2)
            return jnp.where(s < sa, 0, jnp.where(s == sa, i, NT - 1))
        return index

    tok = lambda w, jb: pl.BlockSpec((TT, w), lambda s, i, jm_ref: (at_stage_of(jb)(s, i, jm_ref), 0))
    d4 = lambda jb: pl.BlockSpec((4, TT // 4, AW), lambda s, i, jm_ref: (0, at_stage_of(jb)(s, i, jm_ref), 0))
    d16 = lambda jb: pl.BlockSpec((16, TT // 16, AW), lambda s, i, jm_ref: (0, at_stage_of(jb)(s, i, jm_ref), 0))
    hbm = pl.BlockSpec(memory_space=pltpu.HBM)
    sd = lambda shape, dt: jax.ShapeDtypeStruct(shape, dt)
    in_own_stage = lambda s, i: jnp.where(s == 0, i, NT - 1)
    grid_spec = pltpu.PrefetchScalarGridSpec(
        num_scalar_prefetch=1, grid=(3, NT),
        in_specs=[pl.BlockSpec((TT, D), lambda s, i, jm_ref: (in_own_stage(s, i), 0)),
                  pl.BlockSpec((1, TT), lambda s, i, jm_ref: (0, i)),
                  pl.BlockSpec((1, D), lambda s, i, jm_ref: (0, 0)), hbm, hbm],
        out_specs=[pl.BlockSpec((D, TT), lambda s, i, jm_ref: (0, in_own_stage(s, i))),
                   tok(AW, 0), tok(AW, 0), tok(AW, 1), d4(0), d4(0), d4(1), d16(0), d16(0), d16(1),
                   tok(AW, 1), tok(AW, 2), tok(AW, 2), tok(AW, 3), tok(AW, 3), hbm, hbm],
        scratch_shapes=[pltpu.VMEM((4, D, 1024), BF16), pltpu.VMEM((4, 256, D), BF16), pltpu.VMEM((T, D), BF16),
                        pltpu.VMEM((4, TT, 128), F32), pltpu.VMEM((4, TT, 128), F32), pltpu.VMEM((2, 256, 1024), F32),
                        pltpu.SemaphoreType.DMA((18,)),
                        pltpu.SemaphoreType.DMA((18,)), pltpu.SemaphoreType.DMA((6,))])
    return pl.pallas_call(
        body, name="fwd_in", grid_spec=grid_spec,
        out_shape=[sd((D, T), BF16)] + [sd((T, AW), BF16)] * 3 + [sd((4, T // 4, AW), BF16)] * 3
        + [sd((16, T // 16, AW), BF16)] * 3
        + [sd((T, AW), BF16)] * 5 + [sd((D, NCOL), BF16), sd((4, 256, D), BF16)],
        compiler_params=_cp(("arbitrary", "arbitrary")),
    )(jm_arr, x, pos, mixw, w_in, w_out)


def _band_mask(key_axis, nkeys=2 * BLK):
    shape = (nkeys, 2 * BLK) if key_axis == 0 else (2 * BLK, nkeys)
    kj = lax.broadcasted_iota(jnp.int32, shape, key_axis)
    qi = lax.broadcasted_iota(jnp.int32, shape, 1 - key_axis) & (BLK - 1)
    return (kj >= qi) & (kj <= qi + BLK), kj, qi


def _stack_heads(t2, in_a):
    z = jnp.zeros_like(t2)
    return jnp.concatenate([jnp.where(in_a[0], t2, z), jnp.where(in_a[1], t2, z)], axis=0)


def _attn_fwd(q, k, v, nb, name):
    n = 8
    CH = n * BLK
    halo = nb > n

    def body(*refs):
        if halo:
            q_ref, k_ref, v_ref, kp_ref, vp_ref, o_ref, lse_ref = refs
        else:
            q_ref, k_ref, v_ref, o_ref, lse_ref = refs
        lane = lax.broadcasted_iota(jnp.int32, (1, 128), 1)
        in_a = [lane < HEAD, lane >= HEAD]
        band, kj, _ = _band_mask(1)
        thr0 = jnp.where((n * pl.program_id(0)) % nb == 0, BLK, 0) if halo else BLK
        mask0 = band & (kj >= thr0)
        mask_first = band & (kj >= BLK)
        for b in range(n):
            rs = slice(b * BLK, (b + 1) * BLK)
            stat = jnp.zeros((BLK, 128), F32)
            for hp in range(4):
                cs = slice(hp * 128, (hp + 1) * 128)
                q2s = _stack_heads(q_ref[rs, cs], in_a)
                if b == 0:
                    kprev = kp_ref[:, cs] if halo else k_ref[rs, cs]
                    vprev = vp_ref[:, cs] if halo else v_ref[rs, cs]
                    kk = jnp.concatenate([kprev, k_ref[rs, cs]], axis=0)
                    vv = jnp.concatenate([vprev, v_ref[rs, cs]], axis=0)
                    mask = mask0
                else:
                    kk = k_ref[(b - 1) * BLK:(b + 1) * BLK, cs]
                    vv = v_ref[(b - 1) * BLK:(b + 1) * BLK, cs]
                    mask = mask_first if b % nb == 0 else band
                s = jnp.where(mask, _mm_nt(q2s, kk), NEG)
                m = jnp.max(s, axis=-1, keepdims=True)
                p = jnp.exp(s - m)
                l = jnp.sum(p, axis=-1, keepdims=True)
                o = _mm(p.astype(BF16), vv) / l
                lse = m + jnp.log(l)
                o_ref[rs, cs] = jnp.where(in_a[0], o[:BLK], o[BLK:]).astype(BF16)
                stat = jnp.where(lane == 2 * hp, lse[:BLK], stat)
                stat = jnp.where(lane == 2 * hp + 1, lse[BLK:], stat)
            lse_ref[rs, :] = stat

    cur = pl.BlockSpec((CH, AW), lambda i: (i, 0))
    prev = pl.BlockSpec((BLK, AW), lambda i: (jnp.maximum(n * i - 1, 0), 0))
    return pl.pallas_call(
        body, name=name, grid=(T // CH,),
        in_specs=[cur, cur, cur] + ([prev, prev] if halo else []),
        out_specs=[cur, pl.BlockSpec((CH, 128), lambda i: (i, 0))],
        out_shape=[jax.ShapeDtypeStruct((T, AW), BF16), jax.ShapeDtypeStruct((T, 128), F32)],
        compiler_params=_cp(("parallel",)),
    )(*((q, k, v) + ((k, v) if halo else ())))


def _attn_bwd(q, k, v, do, st, nb, name):
    n = 8
    CH = n * BLK
    NBLK = T // BLK
    halo = nb > n

    def body(*refs):
        if halo:
            (q_ref, k_ref, v_ref, do_ref, st_ref, kp_ref, vp_ref, qn_ref, don_ref, stn_ref,
             dq_ref, dk_ref, dv_ref) = refs
        else:
            q_ref, k_ref, v_ref, do_ref, st_ref, dq_ref, dk_ref, dv_ref = refs
        i = pl.program_id(0)
        lane = lax.broadcasted_iota(jnp.int32, (1, 128), 1)
        in_a = [lane < HEAD, lane >= HEAD]
        band, kj, _ = _band_mask(0)
        thr0 = jnp.where((n * i) % nb == 0, BLK, 0) if halo else BLK
        mask0 = band & (kj >= thr0)
        mask_first = band & (kj >= BLK)

        def stat_rows(st_t, hp):
            lse_r = jnp.concatenate([st_t[2 * hp:2 * hp + 1, :], st_t[2 * hp + 1:2 * hp + 2, :]], axis=1)
            dl_r = jnp.concatenate([st_t[8 + 2 * hp:9 + 2 * hp, :], st_t[9 + 2 * hp:10 + 2 * hp, :]], axis=1)
            return lse_r, dl_r

        st_t = [st_ref[b * BLK:(b + 1) * BLK, :].T for b in range(n)]
        if halo:
            nxt_thr = jnp.where((n * i + n) % nb == 0, 2 * BLK, 0)
            _, kj1, qi1 = _band_mask(0, BLK)
            mask_next = kj1 >= qi1 + nxt_thr
            stn_t = stn_ref[...].T

        for hp in range(4):
            cs = slice(hp * 128, (hp + 1) * 128)
            kb = [k_ref[b * BLK:(b + 1) * BLK, cs] for b in range(n)]
            vb = [v_ref[b * BLK:(b + 1) * BLK, cs] for b in range(n)]
            dk_acc = [jnp.zeros((BLK, 128), F32) for _ in range(n)]
            dv_acc = [jnp.zeros((BLK, 128), F32) for _ in range(n)]
            for b in range(n):
                rs = slice(b * BLK, (b + 1) * BLK)
                q2s = _stack_heads(q_ref[rs, cs], in_a)
                do2s = _stack_heads(do_ref[rs, cs], in_a)
                if b == 0:
                    kprev = kp_ref[:, cs] if halo else kb[0]
                    vprev = vp_ref[:, cs] if halo else vb[0]
                    mask = mask0
                else:
                    kprev, vprev, mask = kb[b - 1], vb[b - 1], (mask_first if b % nb == 0 else band)
                kk = jnp.concatenate([kprev, kb[b]], axis=0)
                vv = jnp.concatenate([vprev, vb[b]], axis=0)
                lse_r, dl_r = stat_rows(st_t[b], hp)
                s_t = jnp.where(mask, _mm_nt(kk, q2s), NEG)
                p_t = jnp.exp(s_t - lse_r)
                ds_t = (p_t * (_mm_nt(vv, do2s) - dl_r)).astype(BF16)
                dkk = _mm(ds_t, q2s)
                dvv = _mm(p_t.astype(BF16), do2s)
                dqs = _mm_tn(ds_t, kk) * SCALE
                dq_ref[rs, cs] = jnp.where(in_a[0], dqs[:BLK], dqs[BLK:]).astype(BF16)
                dk_acc[b] += dkk[BLK:]
                dv_acc[b] += dvv[BLK:]
                if b > 0:
                    dk_acc[b - 1] += dkk[:BLK]
                    dv_acc[b - 1] += dvv[:BLK]
            if halo:
                q2s = _stack_heads(qn_ref[:, cs], in_a)
                do2s = _stack_heads(don_ref[:, cs], in_a)
                lse_r, dl_r = stat_rows(stn_t, hp)
                s_t = jnp.where(mask_next, _mm_nt(kb[n - 1], q2s), NEG)
                p_t = jnp.exp(s_t - lse_r)
                ds_t = (p_t * (_mm_nt(vb[n - 1], do2s) - dl_r)).astype(BF16)
                dk_acc[n - 1] += _mm(ds_t, q2s)
                dv_acc[n - 1] += _mm(p_t.astype(BF16), do2s)
            for b in range(n):
                dk_ref[b * BLK:(b + 1) * BLK, cs] = dk_acc[b].astype(BF16)
                dv_ref[b * BLK:(b + 1) * BLK, cs] = dv_acc[b].astype(BF16)

    cur = pl.BlockSpec((CH, AW), lambda i: (i, 0))
    cur_st = pl.BlockSpec((CH, 128), lambda i: (i, 0))
    prev = pl.BlockSpec((BLK, AW), lambda i: (jnp.maximum(n * i - 1, 0), 0))
    nxt = pl.BlockSpec((BLK, AW), lambda i: (jnp.minimum(n * i + n, NBLK - 1), 0))
    nxt_st = pl.BlockSpec((BLK, 128), lambda i: (jnp.minimum(n * i + n, NBLK - 1), 0))
    ins = [cur] * 4 + [cur_st] + ([prev, prev, nxt, nxt, nxt_st] if halo else [])
    args = (q, k, v, do, st) + ((k, v, q, do, st) if halo else ())
    return pl.pallas_call(
        body, name=name, grid=(T // CH,),
        in_specs=ins,
        out_specs=[cur] * 3,
        out_shape=[jax.ShapeDtypeStruct((T, AW), BF16)] * 3,
        compiler_params=_cp(("parallel",)),
    )(*args)


TH = 256
NCH = TH // CHUNK


def _hgrn_common(hq_ref, hf_ref, lbr_ref, tri_ref):
    r0 = lbr_ref[0:1, :]
    r1 = lbr_ref[1:2, :]
    mx = jnp.maximum(r0, r1)
    e0 = jnp.exp(r0 - mx)
    e1 = jnp.exp(r1 - mx)
    lb = e0 / (e0 + e1)
    hqv = hq_ref[...].astype(F32)
    sq = _sigmoid(hqv)
    qv = hqv * sq
    sf = _sigmoid(hf_ref[...].astype(F32))
    f = lb + (1.0 - lb) * sf
    kv = 1.0 - f
    g = jnp.log(f)
    cum = _mm_exact_l(tri_ref[...], g)
    dec = jnp.exp(jnp.concatenate([cum[c * CHUNK + CHUNK - 1:(c + 1) * CHUNK, :] for c in range(NCH)], axis=0))
    decb = jnp.concatenate([jnp.broadcast_to(dec[c:c + 1, :], (CHUNK, HW)) for c in range(NCH)], axis=0)
    ea = jnp.exp(cum)
    ena = jnp.exp(-cum)
    eend = decb * ena
    return dict(lb=lb, hq=hqv, sq=sq, q=qv, sf=sf, f=f, k=kv, cum=cum, ea=ea, ena=ena, eend=eend,
                qd=qv * ea, ki=kv * ena, ke=kv * eend, dec=dec)


def _tri_mask(transposed=False):
    ti = lax.broadcasted_iota(jnp.int32, (TH, TH), 1 if transposed else 0)
    si = lax.broadcasted_iota(jnp.int32, (TH, TH), 0 if transposed else 1)
    return (si <= ti) & ((si // CHUNK) == (ti // CHUNK))


def _hgrn_fwd(hq, hf, hi, lbr, tri):
    NSUB = 2

    def body(hq_ref, hf_ref, hi_ref, lbr_ref, tri_ref, rec_ref, sall_ref, st_scr):
        @pl.when(pl.program_id(0) == 0)
        def _():
            st_scr[...] = jnp.zeros_like(st_scr)

        causal = _tri_mask()
        for u in range(NSUB):
            tile = slice(u * TH, (u + 1) * TH)
            w = _hgrn_common(hq_ref.at[tile, :], hf_ref.at[tile, :], lbr_ref, tri_ref)
            qd, ki, ke = w["qd"].astype(BF16), w["ki"].astype(BF16), w["ke"].astype(BF16)
            dec = w["dec"]
            vb = hi_ref[tile, :]
            for h in range(4):
                cs = slice(h * 128, (h + 1) * 128)
                att = jnp.where(causal, _mm_nt(qd[:, cs], ki[:, cs]), 0.0)
                o_intra = _mm(att.astype(BF16), vb[:, cs])
                st = st_scr[:, cs]
                for c in range(NCH):
                    rs = slice(c * CHUNK, (c + 1) * CHUNK)
                    sall_ref[u * NCH + c, :, cs] = st
                    rec_ref[u * TH + c * CHUNK:u * TH + (c + 1) * CHUNK, cs] = (
                        o_intra[rs] + _mm_nt(qd[rs, cs], st.astype(BF16))).astype(BF16)
                    st = dec[c:c + 1, cs] * st + _mm_tn(vb[rs, cs], ke[rs, cs])
                st_scr[:, cs] = st

    tok = pl.BlockSpec((NSUB * TH, HW), lambda i: (i, 0))
    return pl.pallas_call(
        body, name="hgrn_fwd", grid=(T // (NSUB * TH),),
        in_specs=[tok, tok, tok, pl.BlockSpec((2, HW), lambda i: (0, 0)), pl.BlockSpec((TH, TH), lambda i: (0, 0))],
        out_specs=[tok, pl.BlockSpec((NSUB * NCH, 128, HW), lambda i: (i, 0, 0))],
        out_shape=[jax.ShapeDtypeStruct((T, HW), BF16), jax.ShapeDtypeStruct((T // CHUNK, 128, HW), F32)],
        scratch_shapes=[pltpu.VMEM((128, HW), F32)],
        compiler_params=_cp(("arbitrary",)),
    )(hq, hf, hi, lbr, tri)


def _hgrn_bwd(hq, hf, hi, lbr, tri, trit, drec, sall, dhg, gw):
    NSUB = 2
    NT = T // (NSUB * TH)

    def body(hq_ref, hf_ref, hi_ref, lbr_ref, tri_ref, trit_ref, do_ref, sall_ref, dhg_ref, gw_r,
             dph_ref, small_ref, pout_o, poutr_o,
             dst_scr, dlb_scr, dqd_scr, dki_scr, dke_scr, dlast_scr, gfull, rbuf, red, redb,
             send_sems, recv_sems, loc_sems, pair_send, pair_recv):
        step = pl.program_id(0)
        loc, rem = _chip_copies(_w_out_piece, red, redb, pout_o, poutr_o, send_sems, recv_sems, loc_sems.at[0])
        mx, my, c = lax.axis_index("x"), lax.axis_index("y"), lax.axis_index("c")
        load = pltpu.make_async_copy(gw_r, gfull, loc_sems.at[1])
        halves = [pltpu.make_async_remote_copy(
            src_ref=gfull.at[pl.ds(pl.multiple_of(j * 256 + (1 - c) * 128, 128), 128), :], dst_ref=rbuf.at[j],
            send_sem=pair_send.at[j], recv_sem=pair_recv.at[j], device_id=(mx, my, 1 - c), device_id_type=MESH)
            for j in range(4)]

        @pl.when(step == 0)
        def _():
            dst_scr[...] = jnp.zeros_like(dst_scr)
            dlb_scr[...] = jnp.zeros_like(dlb_scr)
            load.start()

        @pl.when(step == 1)
        def _():
            load.wait()
            for cp in halves:
                cp.start()

        @pl.when(step == 2)
        def _():
            for j, cp in enumerate(halves):
                cp.wait_recv()
                part = gfull[pl.ds(pl.multiple_of(j * 256 + c * 128, 128), 128), :] + rbuf[j]
                red[j * 128:(j + 1) * 128, :] = part
                redb[j * 128:(j + 1) * 128, :] = part.astype(BF16)
            for cp in halves:
                cp.wait_send()
            for cp in loc + rem:
                cp.start()

        causal = _tri_mask()
        causal_t = _tri_mask(transposed=True)
        lb = None
        for u in reversed(range(NSUB)):
            tile = slice(u * TH, (u + 1) * TH)
            w = _hgrn_common(hq_ref.at[tile, :], hf_ref.at[tile, :], lbr_ref, tri_ref)
            qd, ki, ke = w["qd"].astype(BF16), w["ki"].astype(BF16), w["ke"].astype(BF16)
            dec = w["dec"]
            vb = hi_ref[tile, :]
            dob = do_ref[tile, :].astype(BF16)
            for h in range(4):
                cs = slice(h * 128, (h + 1) * 128)
                att_t = jnp.where(causal_t, _mm_nt(ki[:, cs], qd[:, cs]), 0.0).astype(BF16)
                datt_t = jnp.where(causal_t, _mm_nt(vb[:, cs], dob[:, cs]), 0.0).astype(BF16)
                datt = jnp.where(causal, _mm_nt(dob[:, cs], vb[:, cs]), 0.0).astype(BF16)
                dv_intra = _mm(att_t, dob[:, cs])
                dqd_intra = _mm(datt, ki[:, cs])
                dki_scr[u, :, cs] = _mm(datt_t, qd[:, cs])
                dst = dst_scr[:, cs]
                for c in reversed(range(NCH)):
                    rs = slice(c * CHUNK, (c + 1) * CHUNK)
                    dec_c = dec[c:c + 1, :]
                    st = sall_ref[u * NCH + c, :, cs]
                    dstb = dst.astype(BF16)
                    dph_ref[u * TH + c * CHUNK:u * TH + (c + 1) * CHUNK, 2 * HW + h * 128:2 * HW + (h + 1) * 128] = (
                        dv_intra[rs] + _mm_nt(ke[rs, cs], dstb)).astype(BF16)
                    dqd_scr[u, rs, cs] = dqd_intra[rs] + _mm(dob[rs, cs], st.astype(BF16))
                    dke_scr[u, rs, cs] = _mm(vb[rs, cs], dstb)
                    ddec = jnp.sum(dst * st, axis=0, keepdims=True)
                    dlast_scr[u, c:c + 1, cs] = ddec * dec_c[:, cs]
                    dst = dec_c[:, cs] * dst + _mm_tn(dob[rs, cs], qd[rs, cs])
                dst_scr[:, cs] = dst
            dqd, dki, dke = dqd_scr[u], dki_scr[u], dke_scr[u]
            dq = dqd * w["ea"]
            dk = dki * w["ena"] + dke * w["eend"]
            dcum = dqd * w["qd"] - dki * w["ki"] - dke * w["ke"]
            dkeke = dke * w["ke"]
            dlastb = jnp.concatenate(
                [jnp.broadcast_to(dlast_scr[u, c:c + 1, :]
                                  + jnp.sum(dkeke[c * CHUNK:(c + 1) * CHUNK], axis=0, keepdims=True), (CHUNK, HW))
                 for c in range(NCH)], axis=0)
            dg = _mm_exact_l(trit_ref[...], dcum) + dlastb
            df = dg / w["f"] - dk
            lb, sf, sq = w["lb"], w["sf"], w["sq"]
            dph_ref[tile, HW:2 * HW] = (df * (1.0 - lb) * sf * (1.0 - sf)).astype(BF16)
            dph_ref[tile, 0:HW] = (dq * (sq * (1.0 + w["hq"] * (1.0 - sq)))).astype(BF16)
            dph_ref[tile, 3 * HW:4 * HW] = dhg_ref[tile, :]
            dlb_scr[...] += jnp.sum(df * (1.0 - sf), axis=0, keepdims=True)

        @pl.when(step == NT - 1)
        def _():
            gr = dlb_scr[...] * lb * (1.0 - lb)
            small_ref[...] = jnp.zeros_like(small_ref)
            small_ref[0:1, 0:HW] = gr
            small_ref[1:2, 0:HW] = -gr
            for cp in rem:
                cp.wait_recv()
            for cp in rem:
                cp.wait_send()
            for cp in loc:
                cp.wait()

    tok = pl.BlockSpec((NSUB * TH, HW), lambda i: (NT - 1 - i, 0))
    const = lambda shape: pl.BlockSpec(shape, lambda i: (0,) * len(shape))
    hbm = pl.BlockSpec(memory_space=pltpu.HBM)
    return pl.pallas_call(
        body, name="hgrn_bwd", grid=(NT,),
        in_specs=[tok, tok, tok, const((2, HW)), const((TH, TH)), const((TH, TH)), tok,
                  pl.BlockSpec((NSUB * NCH, 128, HW), lambda i: (NT - 1 - i, 0, 0)), tok, hbm],
        out_specs=[pl.BlockSpec((NSUB * TH, NCOL // 2), lambda i: (NT - 1 - i, 0)), const((8, D)), hbm, hbm],
        out_shape=[jax.ShapeDtypeStruct((T, NCOL // 2), BF16), jax.ShapeDtypeStruct((8, D), F32),
                   jax.ShapeDtypeStruct((128, D), F32), jax.ShapeDtypeStruct((3, 128, D), BF16)],
        scratch_shapes=[pltpu.VMEM((128, HW), F32), pltpu.VMEM((1, HW), F32), pltpu.VMEM((NSUB, TH, HW), F32),
                        pltpu.VMEM((NSUB, TH, HW), F32), pltpu.VMEM((NSUB, TH, HW), F32),
                        pltpu.VMEM((NSUB, 8, HW), F32),
                        pltpu.VMEM((D, D), F32), pltpu.VMEM((4, 128, D), F32), pltpu.VMEM((512, D), F32),
                        pltpu.VMEM((512, D), BF16),
                        pltpu.SemaphoreType.DMA((3,)), pltpu.SemaphoreType.DMA((3,)), pltpu.SemaphoreType.DMA((2,)),
                        pltpu.SemaphoreType.DMA((4,)), pltpu.SemaphoreType.DMA((4,))],
        compiler_params=_cp(("arbitrary",)),
    )(hq, hf, hi, lbr, tri, trit, drec, sall, dhg, gw)


def _fwd_out(o1, o4, o16, l1, l4, l16, rec, ag, hg, x, tgt, anw, hnw, fnw, wout_full, gmat, emat, selmat):
    TT = 512

    def body(o1_r, o4_r, o16_r, l1_r, l4_r, l16_r, rec_r, ag_r, hg_r, x_r, tgt_r, anw_r, hnw_r, fnw_r, wo_r, g_r,
             e_r, sel_r, dx2_o, do1_o, do4_o, do16_o, st1_o, st4_o, st16_o, drec_o, dag_o, dhg_o,
             gw_o, small_o, scr_a, scr_b, scr_c, gwout_o, out_sem):
        @pl.when(pl.program_id(0) == 0)
        def _():
            gwout_o[...] = jnp.zeros_like(gwout_o)
            small_o[...] = jnp.zeros_like(small_o)

        def unperm(r4, r16):
            return _unperm_load(r4, r16, scr_a, scr_b, scr_c)

        def perm_out(val, p1, p4, p16, dt):
            _perm_store(val, scr_a, scr_b, p1, p4, p16, dt)

        o4u, o16u = unperm(o4_r, o16_r)
        l4c, l16c = unperm(l4_r, l16_r)
        l1c = l1_r[...]
        mxc = jnp.maximum(jnp.maximum(l1c, l4c), l16c)
        w1c, w4c, w16c = jnp.exp(l1c - mxc), jnp.exp(l4c - mxc), jnp.exp(l16c - mxc)
        denc = w1c + w4c + w16c
        lane = lax.broadcasted_iota(jnp.int32, (1, 128), 1)
        lse_c = jnp.where(lane < 8, mxc + jnp.log(denc), 0.0)
        em = e_r[...]
        wn1 = _mm_exact_r(w1c / denc, em)
        wn4 = _mm_exact_r(w4c / denc, em)
        o1v = o1_r[...].astype(F32)
        attn = wn1 * o1v + wn4 * o4u + (1.0 - wn1 - wn4) * o16u
        gm = g_r[...]

        def head_mean_a(t):
            return jnp.concatenate([_mm_exact_r(t[:, :256], gm), _mm_exact_r(t[:, 256:], gm)], axis=1)

        def head_mean_h(t):
            return jnp.concatenate(
                [jnp.broadcast_to(jnp.mean(t[:, h * 128:(h + 1) * 128], axis=-1, keepdims=True), (TT, 128))
                 for h in range(4)], axis=1)

        rs_a = lax.rsqrt(head_mean_a(attn * attn) + EPS)
        n_a = attn * rs_a
        agv = ag_r[...].astype(F32)
        sg_a = _sigmoid(agv)
        si_a = agv * sg_a
        anw_v = anw_r[...]
        y_a = (n_a * anw_v) * si_a
        recv = rec_r[...].astype(F32)
        rs_h = lax.rsqrt(head_mean_h(recv * recv) + EPS)
        n_h = recv * rs_h
        hgv = hg_r[...].astype(F32)
        sg_h = _sigmoid(hgv)
        si_h = hgv * sg_h
        hnw_v = hnw_r[...]
        y_h = (n_h * hnw_v) * si_h
        mixed = jnp.concatenate([y_a, y_h], axis=1).astype(BF16)
        xv = x_r[...]
        x2 = xv + _mm(mixed, wo_r[...])
        r2 = lax.rsqrt(jnp.mean(x2 * x2, axis=-1, keepdims=True) + EPS)
        fnw_v = fnw_r[...]
        xn = x2 * r2
        err = xn * fnw_v - tgt_r[...]
        small_o[2:3, :] += 0.5 * jnp.sum(jnp.mean(err * err, axis=-1, keepdims=True), axis=0, keepdims=True)
        small_o[0:1, :] += jnp.sum(err * xn, axis=0, keepdims=True) * (1.0 / D)
        dyw = err * (fnw_v * (1.0 / D))
        dx2 = r2 * dyw - x2 * ((r2 * r2 * r2) * jnp.mean(dyw * x2, axis=-1, keepdims=True))
        dx2_o[...] = dx2
        dx2b = dx2.astype(BF16)
        gwout_o[...] += _mm_tn(mixed, dx2b)
        dmix = _mm_nt(dx2b, wo_r[...])
        dm_a, dm_h = dmix[:, :AW], dmix[:, AW:]
        dag_o[...] = (dm_a * (n_a * anw_v) * (sg_a * (1.0 + agv * (1.0 - sg_a)))).astype(BF16)
        dy_a = dm_a * si_a
        dn_a = dy_a * anw_v
        small_o[1:2, 0:AW] += jnp.sum(dy_a * n_a, axis=0, keepdims=True)
        dattn = rs_a * (dn_a - n_a * head_mean_a(dn_a * n_a))
        perm_out(dattn, do1_o, do4_o, do16_o, BF16)
        stats = lse_c + _mm_exact_r(dattn * attn, sel_r[...])
        perm_out(stats, st1_o, st4_o, st16_o, F32)
        dhg_o[...] = (dm_h * (n_h * hnw_v) * (sg_h * (1.0 + hgv * (1.0 - sg_h)))).astype(BF16)
        dy_h = dm_h * si_h
        dn_h = dy_h * hnw_v
        small_o[1:2, AW:] += jnp.sum(dy_h * n_h, axis=0, keepdims=True)
        drec_o[...] = (rs_h * (dn_h - n_h * head_mean_h(dn_h * n_h))).astype(BF16)

        @pl.when(pl.program_id(0) == T // TT - 1)
        def _():
            out = pltpu.make_async_copy(gwout_o, gw_o, out_sem.at[0])
            out.start()
            out.wait()

    tok = lambda w: pl.BlockSpec((TT, w), lambda i: (i, 0))
    d4 = pl.BlockSpec((4, TT // 4, AW), lambda i: (0, i, 0))
    d16 = pl.BlockSpec((16, TT // 16, AW), lambda i: (0, i, 0))
    const = lambda shape: pl.BlockSpec(shape, lambda i: (0,) * len(shape))
    sd = lambda shape, dt: jax.ShapeDtypeStruct(shape, dt)
    c4 = pl.BlockSpec((4, TT // 4, 128), lambda i: (0, i, 0))
    c16 = pl.BlockSpec((16, TT // 16, 128), lambda i: (0, i, 0))
    p3 = lambda w, dt: [sd((T, w), dt), sd((4, T // 4, w), dt), sd((16, T // 16, w), dt)]
    return pl.pallas_call(
        body, name="fwd_out", grid=(T // TT,),
        in_specs=[tok(AW), d4, d16, tok(128), c4, c16, tok(AW), tok(AW), tok(AW), tok(D), tok(D),
                  const((1, AW)), const((1, HW)), const((1, D)), const((D, D)), const((256, 256)),
                  const((128, AW)), const((AW, 128))],
        out_specs=[tok(D)] + [tok(AW), d4, d16] + [tok(128), c4, c16] + [tok(AW)] * 3
        + [pl.BlockSpec(memory_space=pltpu.HBM), const((8, D))],
        out_shape=[sd((T, D), F32)] + p3(AW, BF16) + p3(128, F32)
        + [sd((T, AW), BF16), sd((T, AW), BF16), sd((T, AW), BF16), sd((D, D), F32), sd((8, D), F32)],
        scratch_shapes=[pltpu.VMEM((4, TT, 128), F32)] * 3 + [pltpu.VMEM((D, D), F32),
                        pltpu.SemaphoreType.DMA((1,))],
        compiler_params=_cp(("arbitrary",)),
    )(o1, o4, o16, l1, l4, l16, rec, ag, hg, x, tgt, anw, hnw, fnw, wout_full, gmat, emat, selmat)


def _dproj_build(dq, dk, dv, dag, pos):
    TT = 512

    def body(dq1, dq4, dq16, dk1, dk4, dk16, dv1, dv4, dv16, dag_r, pos_r, dproj_o, scr_b, scr_c):
        def unperm_sum(r1, r4, r16):
            return r1[...] + _unperm_sum(r4, r16, scr_b, scr_c)

        cosf, s1, s2 = _rope_tables(pos_r[...])
        dproj_o[:, 0:512] = _rope_bwd(unperm_sum(dq1, dq4, dq16), cosf, s1, s2).astype(BF16)
        dproj_o[:, 512:1024] = _rope_bwd(unperm_sum(dk1, dk4, dk16), cosf, s1, s2).astype(BF16)
        dproj_o[:, 1024:1536] = unperm_sum(dv1, dv4, dv16).astype(BF16)
        dproj_o[:, 1536:2048] = dag_r[...]

    tok = lambda w: pl.BlockSpec((TT, w), lambda i: (i, 0))
    d4 = pl.BlockSpec((4, TT // 4, AW), lambda i: (0, i, 0))
    d16 = pl.BlockSpec((16, TT // 16, AW), lambda i: (0, i, 0))
    return pl.pallas_call(
        body, name="dproj_build", grid=(T // TT,),
        in_specs=[tok(AW), d4, d16] * 3 + [tok(AW), pl.BlockSpec((1, TT), lambda i: (0, i))],
        out_specs=tok(NCOL // 2),
        out_shape=jax.ShapeDtypeStruct((T, NCOL // 2), BF16),
        scratch_shapes=[pltpu.VMEM((4, TT, 128), F32)] * 2,
        compiler_params=_cp(("parallel",)),
    )(*dq, *dk, *dv, dag, pos)


def _bwd_x(dproj_a, dproj_h, x, dx2, mixw, w_full, rinb, small4, small6, pout_own, pout_rem, raw):
    TT = 256
    NT = T // TT

    def body(dpa_r, dph_r, x_r, dx2_r, mw_r, w_r, rinb_r, s4_r, s6_r, poo_r, por_r, raw_r,
             gx_o, sall_o, fin_o, fout_o, sbuf, v_own, v_rem, vo_own, vo_rem, sin, sout, got_in,
             got_out, v_send, v_got, send_sems, recv_sems, loc_sems, share_send, share_recv, fin_sems, raw_sems,
             hand_send, hand_recv):
        i = pl.program_id(0)
        mx, my, c = lax.axis_index("x"), lax.axis_index("y"), lax.axis_index("c")
        slot_of = lambda ref, b: ref.at[lax.rem(b - (2 * mx + my) + 3, 4)]
        _, rem = _chip_copies(slot_of, rinb_r, rinb_r, v_own, v_rem, send_sems, recv_sems, loc_sems.at[0])
        loc = [pltpu.make_async_copy(raw_r.at[pl.ds(pl.multiple_of(c * 512, 512), 512), :], v_own, loc_sems.at[0])]
        load_theirs = pltpu.make_async_copy(raw_r.at[pl.ds(pl.multiple_of((1 - c) * 512, 512), 512), :], v_send,
                                            raw_sems.at[0])
        hand = pltpu.make_async_remote_copy(src_ref=v_send, dst_ref=v_got, send_sem=hand_send.at[0],
                                            recv_sem=hand_recv.at[0], device_id=(mx, my, 1 - c), device_id_type=MESH)
        loads = [pltpu.make_async_copy(poo_r, vo_own, fin_sems.at[2]),
                 pltpu.make_async_copy(por_r, vo_rem, fin_sems.at[3])]

        @pl.when(i == 0)
        def _():
            sbuf[...] = jnp.zeros_like(sbuf)
            for cp in loc + rem + loads:
                cp.start()
            load_theirs.start()

        @pl.when(i == 1)
        def _():
            load_theirs.wait()
            hand.start()

        dhn = _mm_nt(dpa_r[...], w_r[:, 0:NCOL // 2]) + _mm_nt(dph_r[...], w_r[:, NCOL // 2:NCOL])
        xv = x_r[...]
        r = lax.rsqrt(jnp.mean(xv * xv, axis=-1, keepdims=True) + EPS)
        dxw = dhn * mw_r[...]
        gx_o[...] = dx2_r[...] + r * dxw - xv * ((r * r * r) * jnp.mean(dxw * xv, axis=-1, keepdims=True))
        sbuf[16:17, :] += jnp.sum(dhn * (xv * r), axis=0, keepdims=True)

        @pl.when(i == NT - 1)
        def _():
            sbuf[0:8, :] = s4_r[...]
            sbuf[8:16, :] = s6_r[...]
            sloc, srem = _small_copies(sbuf, sall_o, send_sems, recv_sems, loc_sems.at[1])
            for cp in sloc + srem:
                cp.start()
            for cp in rem:
                cp.wait_recv()
            for cp in rem:
                cp.wait_send()
            for cp in loc:
                cp.wait()
            for cp in loads:
                cp.wait()
            hand.wait_recv()
            hand.wait_send()
            sout[...] = ((vo_own[...] + vo_rem[0].astype(F32)) + vo_rem[1].astype(F32)) + vo_rem[2].astype(F32)
            sin[...] = (((v_own[...] + v_got[...]) + v_rem[0].astype(F32)) + v_rem[1].astype(F32)) + v_rem[2].astype(F32)
            swap = [pltpu.make_async_remote_copy(src_ref=sin, dst_ref=got_in, send_sem=share_send.at[0],
                                                 recv_sem=share_recv.at[0], device_id=(mx, my, 1 - c),
                                                 device_id_type=MESH),
                    pltpu.make_async_remote_copy(src_ref=sout, dst_ref=got_out, send_sem=share_send.at[1],
                                                 recv_sem=share_recv.at[1], device_id=(mx, my, 1 - c),
                                                 device_id_type=MESH)]
            for cp in swap:
                cp.start()
            mine = [pltpu.make_async_copy(sin, fin_o.at[c], fin_sems.at[0]),
                    pltpu.make_async_copy(sout, fout_o.at[c], fin_sems.at[1])]
            for cp in mine:
                cp.start()
            for cp in swap:
                cp.wait_recv()
            theirs = [pltpu.make_async_copy(got_in, fin_o.at[1 - c], fin_sems.at[2]),
                      pltpu.make_async_copy(got_out, fout_o.at[1 - c], fin_sems.at[3])]
            for cp in theirs:
                cp.start()
            for cp in swap:
                cp.wait_send()
            for cp in mine + theirs:
                cp.wait()
            for cp in srem:
                cp.wait_recv()
            for cp in srem:
                cp.wait_send()
            for cp in sloc:
                cp.wait()

    tok = lambda w: pl.BlockSpec((TT, w), lambda i: (i, 0))
    const = lambda shape: pl.BlockSpec(shape, lambda i: (0,) * len(shape))
    hbm = pl.BlockSpec(memory_space=pltpu.HBM)
    return pl.pallas_call(
        body, name="bwd_x", grid=(NT,),
        in_specs=[tok(NCOL // 2), tok(NCOL // 2), tok(D), tok(D), const((1, D)), const((D, NCOL)), hbm,
                  const((8, D)), const((8, D)), hbm, hbm, hbm],
        out_specs=[tok(D), hbm, hbm, hbm],
        out_shape=[jax.ShapeDtypeStruct((T, D), F32),
                   jax.ShapeDtypeStruct((8, 24, D), F32),
                   jax.ShapeDtypeStruct((2, 512, 1024), F32), jax.ShapeDtypeStruct((2, 128, D), F32)],
        scratch_shapes=[pltpu.VMEM((24, D), F32),
                        pltpu.VMEM((512, 1024), F32), pltpu.VMEM((3, 512, 1024), BF16),
                        pltpu.VMEM((128, D), F32), pltpu.VMEM((3, 128, D), BF16),
                        pltpu.VMEM((512, 1024), F32), pltpu.VMEM((128, D), F32),
                        pltpu.VMEM((512, 1024), F32), pltpu.VMEM((128, D), F32),
                        pltpu.VMEM((512, 1024), F32), pltpu.VMEM((512, 1024), F32),
                        pltpu.SemaphoreType.DMA((10,)), pltpu.SemaphoreType.DMA((10,)), pltpu.SemaphoreType.DMA((2,)),
                        pltpu.SemaphoreType.DMA((2,)), pltpu.SemaphoreType.DMA((2,)), pltpu.SemaphoreType.DMA((4,)),
                        pltpu.SemaphoreType.DMA((1,)), pltpu.SemaphoreType.DMA((1,)), pltpu.SemaphoreType.DMA((1,))],
        compiler_params=_cp(("arbitrary",)),
    )(dproj_a, dproj_h, x, dx2, mixw, w_full, rinb, small4, small6, pout_own, pout_rem, raw)


def _grad_w_in(hn, dproj_a, dproj_h, jm_arr):
    TK = 2048
    NK = T // TK

    def block_at(j, jm):
        return lax.rem(jm + 1 + j, 4)

    def body(jm_ref, hnt_r, dpa_r, dph_r, rinb_o, raw_o, acc, rbuf, obufb, send_sems, recv_sems, wb_sems):
        j = pl.program_id(0)
        kk = pl.program_id(1)
        x, y, c = lax.axis_index("x"), lax.axis_index("y"), lax.axis_index("c")
        mine = pl.ds(pl.multiple_of(c * 512, 512), 512)
        theirs = pl.ds(pl.multiple_of((1 - c) * 512, 512), 512)

        def send(jj):
            return pltpu.make_async_remote_copy(
                src_ref=acc.at[jj % 2, theirs, :], dst_ref=rbuf.at[jj], send_sem=send_sems.at[jj],
                recv_sem=recv_sems.at[jj], device_id=(x, y, 1 - c), device_id_type=MESH)

        def writeback(jj):
            return [pltpu.make_async_copy(obufb.at[jj % 2], rinb_o.at[jj], wb_sems.at[2 + jj % 2])]

        def wait_writeback(jj):
            for cp in writeback(jj):
                cp.wait()

        def finalize(jj):
            send(jj).wait_recv()
            obufb[jj % 2] = (acc[jj % 2, mine, :] + rbuf[jj]).astype(BF16)
            for cp in writeback(jj):
                cp.start()

        blk = block_at(j, jm_ref[0])
        prod = _mm(hnt_r[...], jnp.where(blk < 2, dpa_r[...], dph_r[...]))

        @pl.when(kk == 0)
        def _():
            for jj in (2, 3):
                @pl.when(j == jj)
                def _():
                    send(jj - 2).wait_send()
            acc[j % 2] = prod

        @pl.when(kk > 0)
        def _():
            acc[j % 2] += prod

        @pl.when(kk == NK - 1)
        def _():
            for jj in range(4):
                @pl.when(j == jj)
                def _():
                    if jj < 3:
                        send(jj).start()
                    if jj in (1, 2):
                        finalize(jj - 1)
                    if jj == 3:
                        raw = pltpu.make_async_copy(acc.at[1], raw_o, wb_sems.at[4])
                        raw.start()
                        wait_writeback(0)
                        finalize(2)
                        wait_writeback(1)
                        wait_writeback(2)
                        raw.wait()
                        send(2).wait_send()

    def used(is_mine, kk, col):
        return jnp.where(is_mine, kk, 0), jnp.where(is_mine, col, 0)

    hbm = pl.BlockSpec(memory_space=pltpu.HBM)
    grid_spec = pltpu.PrefetchScalarGridSpec(
        num_scalar_prefetch=1, grid=(4, NK),
        in_specs=[pl.BlockSpec((D, TK), lambda j, kk, jm_ref: (0, kk)),
                  pl.BlockSpec((TK, 1024), lambda j, kk, jm_ref: used(
                      block_at(j, jm_ref[0]) < 2, kk, block_at(j, jm_ref[0]))),
                  pl.BlockSpec((TK, 1024), lambda j, kk, jm_ref: used(
                      block_at(j, jm_ref[0]) >= 2, kk, block_at(j, jm_ref[0]) - 2))],
        out_specs=[hbm, hbm],
        scratch_shapes=[pltpu.VMEM((2, D, 1024), F32), pltpu.VMEM((3, 512, 1024), F32),
                        pltpu.VMEM((2, 512, 1024), BF16),
                        pltpu.SemaphoreType.DMA((3,)), pltpu.SemaphoreType.DMA((3,)), pltpu.SemaphoreType.DMA((5,))])
    return pl.pallas_call(
        body, name="grad_w_in", grid_spec=grid_spec,
        out_shape=[jax.ShapeDtypeStruct((3, 512, 1024), BF16), jax.ShapeDtypeStruct((D, 1024), F32)],
        compiler_params=_cp(("arbitrary", "arbitrary")),
    )(jm_arr, hn, dproj_a, dproj_h)


def _w_out_piece(ref, j):
    return ref.at[pl.ds(j * 128, 128), :]


def _chip_copies(piece, src_r, srcb_r, own_o, rem_o, send_sems, recv_sems, loc_sem):
    x, y, c = lax.axis_index("x"), lax.axis_index("y"), lax.axis_index("c")
    chips = [(1 - x, y), (x, 1 - y), (1 - x, 1 - y)]
    loc = [pltpu.make_async_copy(piece(src_r, 2 * x + y), own_o, loc_sem)]
    rem = [pltpu.make_async_remote_copy(
        src_ref=piece(srcb_r, 2 * px + py), dst_ref=rem_o.at[k], send_sem=send_sems.at[k],
        recv_sem=recv_sems.at[k], device_id=(px, py, c), device_id_type=MESH) for k, (px, py) in enumerate(chips)]
    return loc, rem


def _small_copies(small_r, sall_o, send_sems, recv_sems, loc_sem):
    x, y, c = lax.axis_index("x"), lax.axis_index("y"), lax.axis_index("c")
    me = 4 * x + 2 * y + c
    loc = [pltpu.make_async_copy(small_r, sall_o.at[me], loc_sem)]
    rem = []
    k = 3
    for fx in range(2):
        for fy in range(2):
            for fc in range(2):
                if fx or fy or fc:
                    peer = (1 - x if fx else x, 1 - y if fy else y, 1 - c if fc else c)
                    rem.append(pltpu.make_async_remote_copy(
                        src_ref=small_r, dst_ref=sall_o.at[me], send_sem=send_sems.at[k],
                        recv_sem=recv_sems.at[k], device_id=peer, device_id_type=MESH))
                    k += 1
    return loc, rem


def _adamw_math(w, g, m, v):
    m = B1 * m + (1.0 - B1) * g
    v = B2 * v + (1.0 - B2) * (g * g)
    m_hat = m / (1.0 - B1 ** STEP)
    v_hat = v / (1.0 - B2 ** STEP)
    delta = -LR * (m_hat / (jnp.sqrt(v_hat) + AEPS) + WD * w)
    return delta, m, v


def _adamw(big_in, big_out, sall, params):
    CH = 256
    NB = 3
    chunks = [(0, r) for r in range(D // CH)] + [(1, r) for r in range(256 // CH)]

    def body(*refs):
        big = [refs[0:4], refs[4:8]]
        sall_r = refs[8]
        ins = refs[9:24]
        big_o = [refs[24:27], refs[27:30]]
        outs = refs[30:51]
        ibuf, obuf, in_sems, out_sems = refs[51:]

        def reads(n):
            which, r = chunks[n]
            return [pltpu.make_async_copy(big[which][a].at[pl.ds(r * CH, CH), :], ibuf.at[n % NB, a],
                                          in_sems.at[4 * (n % NB) + a]) for a in range(4)]

        def writes(n):
            which, r = chunks[n]
            return [pltpu.make_async_copy(obuf.at[n % NB, a], big_o[which][a].at[pl.ds(r * CH, CH), :],
                                          out_sems.at[3 * (n % NB) + a]) for a in range(3)]

        for n in range(NB):
            for cp in reads(n):
                cp.start()

        tot = sall_r[0]
        for dv in range(1, 8):
            tot = tot + sall_r[dv]
        grads = [tot[16:17, :], tot[1:2, 0:AW], tot[1:2, AW:], tot[8:10, 0:HW], tot[0:1, :]]
        outs[0][...] = tot[2:3, 0:1]
        for p in range(5):
            w_r, m_r, v_r = ins[3 * p:3 * p + 3]
            g = grads[p]
            d, mm, vv = _adamw_math(w_r[...], g, m_r[...], v_r[...])
            outs[1 + 4 * p][...] = g
            outs[2 + 4 * p][...] = d
            outs[3 + 4 * p][...] = mm
            outs[4 + 4 * p][...] = vv

        for n in range(len(chunks)):
            s = n % NB
            for cp in reads(n):
                cp.wait()
            if n >= NB:
                for cp in writes(n - NB):
                    cp.wait()
            d, mm, vv = _adamw_math(ibuf[s, 0], ibuf[s, 1], ibuf[s, 2], ibuf[s, 3])
            obuf[s, 0] = d
            obuf[s, 1] = mm
            obuf[s, 2] = vv
            for cp in writes(n):
                cp.start()
            if n + NB < len(chunks):
                for cp in reads(n + NB):
                    cp.start()
        for n in range(len(chunks) - NB, len(chunks)):
            for cp in writes(n):
                cp.wait()

    flat = [a for p in params for a in p]
    shapes = [jax.ShapeDtypeStruct((D, 1024), F32)] * 3 + [jax.ShapeDtypeStruct((256, D), F32)] * 3
    shapes += [jax.ShapeDtypeStruct((1, 1), F32)]
    for p in params:
        shapes += [jax.ShapeDtypeStruct(p[0].shape, F32)] * 4
    vm = pl.BlockSpec(memory_space=pltpu.VMEM)
    hbm = pl.BlockSpec(memory_space=pltpu.HBM)
    return pl.pallas_call(
        body, name="adamw",
        in_specs=[hbm] * 8 + [vm] * 16, out_specs=[hbm] * 6 + [vm] * 21,
        out_shape=shapes,
        scratch_shapes=[pltpu.VMEM((NB, 4, CH, 1024), F32), pltpu.VMEM((NB, 3, CH, 1024), F32),
                        pltpu.SemaphoreType.DMA((4 * NB,)), pltpu.SemaphoreType.DMA((3 * NB,))],
        compiler_params=_cp(),
    )(*big_in, *big_out, sall, *flat)


def kernel(x, positions, w_in, w_out, mix_norm_w, attn_out_norm_w, hgrn_out_norm_w, hgrn_lb_raw, final_norm_w, loss_target, m_w_in, m_w_out, m_mix_norm_w, m_attn_out_norm_w, m_hgrn_out_norm_w, m_hgrn_lb_raw, m_final_norm_w, v_w_in, v_w_out, v_mix_norm_w, v_attn_out_norm_w, v_hgrn_out_norm_w, v_hgrn_lb_raw, v_final_norm_w):
    xs = x.reshape(T, D)
    tgt = loss_target.reshape(T, D)
    pos = positions.reshape(1, T)
    fnw = final_norm_w.reshape(1, D)

    ti = np.arange(TH)
    tri_np = ((ti[:, None] // CHUNK == ti[None, :] // CHUNK) & (ti[None, :] <= ti[:, None])).astype(np.float32)
    tri = jnp.asarray(tri_np, BF16)
    trit = jnp.asarray(tri_np.T, BF16)
    hi_ = np.arange(AW) // HEAD
    gmat = jnp.asarray((hi_[:256, None] == hi_[None, :256]).astype(np.float32) / HEAD, BF16)
    emat_np = (np.arange(128)[:, None] == hi_[None, :]).astype(np.float32)
    sel_np = (8 + hi_[:, None] == np.arange(128)[None, :]).astype(np.float32)
    emat = jnp.asarray(emat_np, BF16)
    selmat = jnp.asarray(sel_np, BF16)

    jm_arr = (2 * lax.axis_index("x") + lax.axis_index("y")).astype(jnp.int32).reshape(1)
    (hn, q1, k1, v1, q4, k4, v4, q16, k16, v16, ag, hq, hf, hi, hg, w_full, wout4) = _fwd_in(
        xs, pos, mix_norm_w, w_in.reshape(D, 1024), w_out.reshape(256, D), jm_arr)
    wout_full = wout4.reshape(D, D)
    flat = lambda a: a.reshape(T, AW)
    o1, l1 = _attn_fwd(q1, k1, v1, T // BLK, "attn_fwd_d1")
    o4, l4 = _attn_fwd(flat(q4), flat(k4), flat(v4), T // 4 // BLK, "attn_fwd_d4")
    o16, l16 = _attn_fwd(flat(q16), flat(k16), flat(v16), T // 16 // BLK, "attn_fwd_d16")
    rec, sall = _hgrn_fwd(hq, hf, hi, hgrn_lb_raw, tri)

    (dx2, do1, do4, do16, st1, st4, st16, drec, dag, dhg, gw, small4) = _fwd_out(
        o1, o4.reshape(4, T // 4, AW), o16.reshape(16, T // 16, AW),
        l1, l4.reshape(4, T // 4, 128), l16.reshape(16, T // 16, 128),
        rec, ag, hg, xs, tgt, attn_out_norm_w, hgrn_out_norm_w, fnw, wout_full, gmat, emat, selmat)

    fst = lambda a: a.reshape(T, 128)
    dq1, dk1, dv1 = _attn_bwd(q1, k1, v1, do1, st1, T // BLK, "attn_bwd_d1")
    dq4, dk4, dv4 = _attn_bwd(flat(q4), flat(k4), flat(v4), flat(do4), fst(st4), T // 4 // BLK, "attn_bwd_d4")
    dq16, dk16, dv16 = _attn_bwd(flat(q16), flat(k16), flat(v16), flat(do16), fst(st16), T // 16 // BLK,
                                 "attn_bwd_d16")
    dproj_h, small6, pout_own, pout_rem = _hgrn_bwd(hq, hf, hi, hgrn_lb_raw, tri, trit, drec, sall, dhg, gw)

    r4 = lambda a: a.reshape(4, T // 4, AW)
    r16 = lambda a: a.reshape(16, T // 16, AW)
    dproj_a = _dproj_build((dq1, r4(dq4), r16(dq16)), (dk1, r4(dk4), r16(dk16)), (dv1, r4(dv4), r16(dv16)),
                           dag, pos)
    rinb, raw = _grad_w_in(hn, dproj_a, dproj_h, jm_arr)
    gx, small_all, fin, fout = _bwd_x(dproj_a, dproj_h, xs, dx2, mix_norm_w, w_full, rinb,
                                      small4, small6, pout_own, pout_rem, raw)
    g_w_in = fin.reshape(D, 1024)
    g_w_out = fout.reshape(256, D)

    params = [(mix_norm_w, m_mix_norm_w, v_mix_norm_w),
              (attn_out_norm_w, m_attn_out_norm_w, v_attn_out_norm_w),
              (hgrn_out_norm_w, m_hgrn_out_norm_w, v_hgrn_out_norm_w),
              (hgrn_lb_raw, m_hgrn_lb_raw, v_hgrn_lb_raw),
              (fnw, m_final_norm_w.reshape(1, D), v_final_norm_w.reshape(1, D))]
    d_in, nm_in, nv_in, d_out, nm_out, nv_out, *so = _adamw(
        (w_in.reshape(D, 1024), g_w_in, m_w_in.reshape(D, 1024), v_w_in.reshape(D, 1024)),
        (w_out.reshape(256, D), g_w_out, m_w_out.reshape(256, D), v_w_out.reshape(256, D)), small_all, params)
    loss = so[0].reshape(())
    g_s = [so[1 + 4 * p] for p in range(5)]
    d_s = [so[2 + 4 * p] for p in range(5)]
    m_s = [so[3 + 4 * p] for p in range(5)]
    v_s = [so[4 + 4 * p] for p in range(5)]
    for lst in (g_s, d_s, m_s, v_s):
        lst[4] = lst[4].reshape(D)

    return (loss, gx.reshape(1, T, D),
            g_w_in.reshape(1, D, 1024), g_w_out.reshape(1, 256, D), *g_s,
            d_in.reshape(1, D, 1024), d_out.reshape(1, 256, D), *d_s,
            nm_in.reshape(1, D, 1024), nm_out.reshape(1, 256, D), *m_s,
            nv_in.reshape(1, D, 1024), nv_out.reshape(1, 256, D), *v_s)
```
